```python
import math
import jax, jax.numpy as jnp
from jax import lax
import numpy as np

D_MODEL = 1024
BATCH = 8
SEQ = 4096
DEPTH = 2

MIX_W = D_MODEL
SSD_HEAD_DIM = 64
SSD_INNER = MIX_W // 2
SSD_HEADS = SSD_INNER // SSD_HEAD_DIM
SSD_GROUPS = 2
SSD_HPG = SSD_HEADS // SSD_GROUPS
SSD_STATE = 128
SSD_CONV_K = 4
SSD_CHUNK = 256
SSD_CONV_CH = SSD_INNER + 2 * SSD_GROUPS * SSD_STATE
SSD_PROJ_W = SSD_INNER + SSD_CONV_CH + SSD_HEADS
POOL_W = MIX_W // 4
POOL_WINDOWS = (2, 4, 8, 16)
POOL_GROUPS = len(POOL_WINDOWS)
POOL_CH = POOL_W // POOL_GROUPS
ATT_HEAD_DIM = 64
ATT_W = MIX_W - SSD_INNER - POOL_W
ATT_HEADS = ATT_W // ATT_HEAD_DIM
ATT_PATTERNS = ((128, 1), (512, 4), (2048, 16))
ATT_BLOCK = 128
ROT_DIM = ATT_HEAD_DIM // 4
ROPE_THETA = 500000.0
IN_W = SSD_PROJ_W + POOL_W + 3 * ATT_W
FFN_DIM = 2816
FFN_CONV_K = 3
NORM_EPS = 1e-6

kernel_name = "hybrid_ssd_pool_dilated_attn_trunk"

F32 = jnp.float32


def rmsnorm(x, g):
    xf = x.astype(F32)
    y = xf * lax.rsqrt(jnp.mean(xf * xf, axis=-1, keepdims=True) + NORM_EPS)
    return (y * g.astype(F32)).astype(x.dtype)


def causal_dwconv(x, w, b):
    k, ch = w.shape
    y = lax.conv_general_dilated(x, w.astype(x.dtype)[:, None, :], window_strides=(1,),
                                 padding=[(k - 1, 0)], dimension_numbers=("NWC", "WIO", "NWC"),
                                 feature_group_count=ch)
    return y + b.astype(x.dtype)


def ssd_chunked(xs, da, bm, cm):
    b, s, g, j, p = xs.shape
    n = bm.shape[-1]
    pad = (-s) % SSD_CHUNK
    sp = s + pad
    nc = sp // SSD_CHUNK
    q = SSD_CHUNK
    xs = jnp.pad(xs, ((0, 0), (0, pad), (0, 0), (0, 0), (0, 0))).reshape(b, nc, q, g, j, p)
    da = jnp.pad(da, ((0, 0), (0, pad), (0, 0), (0, 0))).reshape(b, nc, q, g, j)
    bm = jnp.pad(bm, ((0, 0), (0, pad), (0, 0), (0, 0))).reshape(b, nc, q, g, n)
    cm = jnp.pad(cm, ((0, 0), (0, pad), (0, 0), (0, 0))).reshape(b, nc, q, g, n)
    a_cum = jnp.cumsum(da, axis=2)
    acs = jnp.moveaxis(a_cum, 2, -1)
    seg = acs[..., :, None] - acs[..., None, :]
    causal = jnp.tril(jnp.ones((q, q), dtype=bool))
    lmat = jnp.exp(jnp.where(causal, seg, -jnp.inf))
    cb = jnp.einsum("bclgn,bcsgn->bcgls", cm, bm)
    y_diag = jnp.einsum("bcgjls,bcsgjp->bclgjp", cb[:, :, :, None] * lmat, xs)
    decay_states = jnp.exp(a_cum[:, :, -1:] - a_cum)
    states = jnp.einsum("bclgn,bclgj,bclgjp->bcgjpn", bm, decay_states, xs)
    chunk_decay = jnp.exp(a_cum[:, :, -1])

    def step(h, inp):
        st, dec = inp
        return h * dec[..., None, None] + st, h

    h0 = jnp.zeros((b, g, j, p, n), xs.dtype)
    _, h_in = lax.scan(step, h0, (jnp.moveaxis(states, 1, 0), jnp.moveaxis(chunk_decay, 1, 0)))
    h_in = jnp.moveaxis(h_in, 0, 1)
    y_off = jnp.einsum("bclgn,bcgjpn,bclgj->bclgjp", cm, h_in, jnp.exp(a_cum))
    return (y_diag + y_off).reshape(b, sp, g, j, p)[:, :s]


def ssd_mixer(p_in, conv_w, conv_b, dt_bias, a_log, d_skip, norm_g):
    b, s, _ = p_in.shape
    z = p_in[..., :SSD_INNER].astype(F32)
    xbc = p_in[..., SSD_INNER:SSD_INNER + SSD_CONV_CH]
    dt = p_in[..., SSD_INNER + SSD_CONV_CH:].astype(F32)
    xbc = jax.nn.silu(causal_dwconv(xbc, conv_w, conv_b)).astype(F32)
    xs = xbc[..., :SSD_INNER].reshape(b, s, SSD_GROUPS, SSD_HPG, SSD_HEAD_DIM)
    bm = xbc[..., SSD_INNER:SSD_INNER + SSD_GROUPS * SSD_STATE].reshape(b, s, SSD_GROUPS, SSD_STATE)
    cm = xbc[..., SSD_INNER + SSD_GROUPS * SSD_STATE:].reshape(b, s, SSD_GROUPS, SSD_STATE)
    dt = jax.nn.softplus(dt + dt_bias.astype(F32)).reshape(b, s, SSD_GROUPS, SSD_HPG)
    a = -jnp.exp(a_log.astype(F32)).reshape(SSD_GROUPS, SSD_HPG)
    y = ssd_chunked(xs * dt[..., None], dt * a, bm, cm)
    y = y + d_skip.astype(F32).reshape(SSD_GROUPS, SSD_HPG, 1) * xs
    y = y.reshape(b, s, SSD_GROUPS, SSD_HPG * SSD_HEAD_DIM) * jax.nn.silu(z).reshape(b, s, SSD_GROUPS, -1)
    y = y * lax.rsqrt(jnp.mean(y * y, axis=-1, keepdims=True) + NORM_EPS)
    y = y * norm_g.astype(F32).reshape(SSD_GROUPS, -1)
    return y.reshape(b, s, SSD_INNER)


def pool_mixer(u, pool_w, pool_scale):
    b, s, _ = u.shape
    u = u.astype(F32).reshape(b, s, POOL_GROUPS, POOL_CH)
    cs = jnp.cumsum(u, axis=1)
    outs = []
    for gi, w in enumerate(POOL_WINDOWS):
        cg = cs[:, :, gi]
        lag = jnp.pad(cg, ((0, 0), (w, 0), (0, 0)))[:, :s]
        cnt = jnp.minimum(jnp.arange(1, s + 1), w).astype(F32)[None, :, None]
        outs.append((cg - lag) / cnt)
    pooled = jnp.stack(outs, axis=2)
    y = jnp.einsum("bsgc,gcd->bsgd", pooled - u, pool_w.astype(F32))
    return y.reshape(b, s, POOL_W) * pool_scale.astype(F32)


def partial_rope(t, cos, sin):
    half = ROT_DIM // 2
    t1, t2 = t[..., :half], t[..., half:ROT_DIM]
    c, s = cos[:, :, None], sin[:, :, None]
    return jnp.concatenate([t1 * c - t2 * s, t2 * c + t1 * s, t[..., ROT_DIM:]], axis=-1)


def dilated_branch(q, k, v, window, dilation):
    b, s, h, e = q.shape
    L = s // dilation
    steps = window // dilation
    n_prev = -(-steps // ATT_BLOCK)
    nb = -(-L // ATT_BLOCK)
    lp = nb * ATT_BLOCK

    def strided(t):
        t = t.reshape(b, L, dilation, h, e).transpose(0, 2, 3, 1, 4)
        t = jnp.pad(t, ((0, 0), (0, 0), (0, 0), (0, lp - L), (0, 0)))
        return t.reshape(b, dilation, h, nb, ATT_BLOCK, e)

    def band(t):
        tp = jnp.pad(t, ((0, 0), (0, 0), (0, 0), (n_prev, 0), (0, 0), (0, 0)))
        return jnp.concatenate([tp[:, :, :, j:j + nb] for j in range(n_prev + 1)], axis=4)

    qb, kb, vb = strided(q), strided(k), strided(v)
    kband, vband = band(kb), band(vb)
    sc = jnp.einsum("bdhnqe,bdhnke->bdhnqk", qb, kband) * (e ** -0.5)
    qi = jnp.arange(ATT_BLOCK)[:, None] + n_prev * ATT_BLOCK
    kj = jnp.arange((n_prev + 1) * ATT_BLOCK)[None, :]
    rel = qi - kj
    kpos = jnp.arange(nb)[:, None, None] * ATT_BLOCK + kj[None] - n_prev * ATT_BLOCK
    valid = (rel >= 0) & (rel <= steps) & (kpos >= 0)
    sc = jnp.where(valid, sc, -jnp.inf)
    m = jnp.max(sc, axis=-1, keepdims=True)
    p = jnp.exp(sc - m)
    l = jnp.sum(p, axis=-1)
    o = jnp.einsum("bdhnqk,bdhnke->bdhnqe", p, vband) / l[..., None]
    lse = m[..., 0] + jnp.log(l)
    o = o.reshape(b, dilation, h, lp, e)[:, :, :, :L].transpose(0, 3, 1, 2, 4).reshape(b, s, h, e)
    lse = lse.reshape(b, dilation, h, lp)[..., :L].transpose(0, 3, 1, 2).reshape(b, s, h)
    return o, lse


def dilated_attention(qkv, cos, sin):
    b, s, _ = qkv.shape
    qkv = qkv.astype(F32)
    q = partial_rope(qkv[..., :ATT_W].reshape(b, s, ATT_HEADS, ATT_HEAD_DIM), cos, sin)
    k = partial_rope(qkv[..., ATT_W:2 * ATT_W].reshape(b, s, ATT_HEADS, ATT_HEAD_DIM), cos, sin)
    v = qkv[..., 2 * ATT_W:].reshape(b, s, ATT_HEADS, ATT_HEAD_DIM)
    outs, lses = [], []
    for window, dilation in ATT_PATTERNS:
        o, lse = dilated_branch(q, k, v, window, dilation)
        outs.append(o)
        lses.append(lse)
    wts = jax.nn.softmax(jnp.stack(lses, axis=0), axis=0)
    o = jnp.einsum("rbsh,rbshe->bshe", wts, jnp.stack(outs, axis=0))
    return o.reshape(b, s, ATT_W)


def conv_ffn(h, up, conv_w, conv_b, down):
    hid = causal_dwconv(h @ up, conv_w, conv_b)
    g, u = jnp.split(hid, 2, axis=-1)
    return (jax.nn.silu(g) * u) @ down


def _fwd_setup_inputs(seed: int = 0) -> dict:
    key = jax.random.key(seed)
    ks = jax.random.split(key, 24)
    nrm = lambda k, shape, scale: jax.random.normal(k, shape, F32) * scale
    dt = jnp.exp(jax.random.uniform(ks[9], (DEPTH, SSD_HEADS), F32) * (math.log(0.1) - math.log(0.001))
                 + math.log(0.001))
    return {
        "x": nrm(ks[0], (BATCH, SEQ, D_MODEL), 1.0),
        "c": nrm(ks[1], (BATCH, D_MODEL), 1.0),
        "positions": (jax.random.randint(ks[2], (BATCH, 1), 0, 1024, jnp.int32)
                      + jnp.arange(SEQ, dtype=jnp.int32)[None, :]),
        "ada_w": nrm(ks[3], (DEPTH, D_MODEL, 6 * D_MODEL), 0.5 * D_MODEL ** -0.5),
        "ada_b": nrm(ks[4], (DEPTH, 6 * D_MODEL), 0.02),
        "norm1_g": 1.0 + nrm(ks[5], (DEPTH, D_MODEL), 0.02),
        "w_in": nrm(ks[6], (DEPTH, D_MODEL, IN_W), D_MODEL ** -0.5),
        "ssd_conv_w": nrm(ks[7], (DEPTH, SSD_CONV_K, SSD_CONV_CH), SSD_CONV_K ** -0.5),
        "ssd_conv_b": nrm(ks[8], (DEPTH, SSD_CONV_CH), 0.02),
        "ssd_dt_bias": dt + jnp.log(-jnp.expm1(-dt)),
        "ssd_a_log": jnp.log(jax.random.uniform(ks[10], (DEPTH, SSD_HEADS), F32, 1.0, 16.0)),
        "ssd_d": 1.0 + nrm(ks[11], (DEPTH, SSD_HEADS), 0.02),
        "ssd_norm_g": 1.0 + nrm(ks[12], (DEPTH, SSD_INNER), 0.02),
        "pool_w": nrm(ks[13], (DEPTH, POOL_GROUPS, POOL_CH, POOL_CH), POOL_CH ** -0.5),
        "pool_scale": 1.0 + nrm(ks[14], (DEPTH, POOL_W), 0.02),
        "w_out": nrm(ks[15], (DEPTH, MIX_W, D_MODEL), MIX_W ** -0.5),
        "norm2_g": 1.0 + nrm(ks[16], (DEPTH, D_MODEL), 0.02),
        "ffn_up": nrm(ks[17], (DEPTH, D_MODEL, 2 * FFN_DIM), D_MODEL ** -0.5),
        "ffn_conv_w": nrm(ks[18], (DEPTH, FFN_CONV_K, 2 * FFN_DIM), FFN_CONV_K ** -0.5),
        "ffn_conv_b": nrm(ks[19], (DEPTH, 2 * FFN_DIM), 0.02),
        "ffn_down": nrm(ks[20], (DEPTH, FFN_DIM, D_MODEL), FFN_DIM ** -0.5),
        "final_g": 1.0 + nrm(ks[21], (D_MODEL,), 0.02),
    }


def _fwd_reference(x, c, positions, ada_w, ada_b, norm1_g, w_in, ssd_conv_w, ssd_conv_b, ssd_dt_bias,
              ssd_a_log, ssd_d, ssd_norm_g, pool_w, pool_scale, w_out, norm2_g, ffn_up, ffn_conv_w,
              ffn_conv_b, ffn_down, final_g):
    inv_freq = ROPE_THETA ** (-jnp.arange(0, ROT_DIM, 2, dtype=F32) / ROT_DIM)
    ang = positions.astype(F32)[..., None] * inv_freq
    cos, sin = jnp.cos(ang), jnp.sin(ang)
    c_act = jax.nn.silu(c)
    a0 = SSD_PROJ_W
    a1 = SSD_PROJ_W + POOL_W
    for i in range(DEPTH):
        mod = (c_act @ ada_w[i] + ada_b[i])[:, None, :]
        sh1, sc1, g1, sh2, sc2, g2 = jnp.split(mod, 6, axis=-1)
        h = rmsnorm(x, norm1_g[i]) * (1.0 + sc1) + sh1
        proj = h @ w_in[i]
        y_ssd = ssd_mixer(proj[..., :a0], ssd_conv_w[i], ssd_conv_b[i], ssd_dt_bias[i],
                          ssd_a_log[i], ssd_d[i], ssd_norm_g[i])
        y_pool = pool_mixer(proj[..., a0:a1], pool_w[i], pool_scale[i])
        y_att = dilated_attention(proj[..., a1:], cos, sin)
        mix = jnp.concatenate([y_ssd, y_pool, y_att], axis=-1).astype(x.dtype)
        x = x + g2.dtype.type(1) * g1 * (mix @ w_out[i]) if False else x + g1 * (mix @ w_out[i])
        h = rmsnorm(x, norm2_g[i]) * (1.0 + sc2) + sh2
        x = x + g2 * conv_ffn(h, ffn_up[i], ffn_conv_w[i], ffn_conv_b[i], ffn_down[i])
    return rmsnorm(x, final_g)


import jax as _jax
import jax.numpy as _jnp

TWIN_FORMAT = 'train_step'
FWD_PARAMS = ['x', 'c', 'positions', 'ada_w', 'ada_b', 'norm1_g', 'w_in', 'ssd_conv_w', 'ssd_conv_b', 'ssd_dt_bias', 'ssd_a_log', 'ssd_d', 'ssd_norm_g', 'pool_w', 'pool_scale', 'w_out', 'norm2_g', 'ffn_up', 'ffn_conv_w', 'ffn_conv_b', 'ffn_down', 'final_g']
TWIN_WEIGHTS = ['ada_w', 'ada_b', 'norm1_g', 'w_in', 'ssd_conv_w', 'ssd_conv_b', 'ssd_dt_bias', 'ssd_a_log', 'ssd_d', 'ssd_norm_g', 'pool_w', 'pool_scale', 'w_out', 'norm2_g', 'ffn_up', 'ffn_conv_w', 'ffn_conv_b', 'ffn_down', 'final_g']
TWIN_DIFF_INPUT = 'x'
TWIN_INPUTS = ['x', 'c', 'positions', 'ada_w', 'ada_b', 'norm1_g', 'w_in', 'ssd_conv_w', 'ssd_conv_b', 'ssd_dt_bias', 'ssd_a_log', 'ssd_d', 'ssd_norm_g', 'pool_w', 'pool_scale', 'w_out', 'norm2_g', 'ffn_up', 'ffn_conv_w', 'ffn_conv_b', 'ffn_down', 'final_g', 'loss_target', 'm_ada_w', 'm_ada_b', 'm_norm1_g', 'm_w_in', 'm_ssd_conv_w', 'm_ssd_conv_b', 'm_ssd_dt_bias', 'm_ssd_a_log', 'm_ssd_d', 'm_ssd_norm_g', 'm_pool_w', 'm_pool_scale', 'm_w_out', 'm_norm2_g', 'm_ffn_up', 'm_ffn_conv_w', 'm_ffn_conv_b', 'm_ffn_down', 'm_final_g', 'v_ada_w', 'v_ada_b', 'v_norm1_g', 'v_w_in', 'v_ssd_conv_w', 'v_ssd_conv_b', 'v_ssd_dt_bias', 'v_ssd_a_log', 'v_ssd_d', 'v_ssd_norm_g', 'v_pool_w', 'v_pool_scale', 'v_w_out', 'v_norm2_g', 'v_ffn_up', 'v_ffn_conv_w', 'v_ffn_conv_b', 'v_ffn_down', 'v_final_g']
TWIN_OUTPUTS = ['loss', 'grad_x', 'grad_ada_w', 'grad_ada_b', 'grad_norm1_g', 'grad_w_in', 'grad_ssd_conv_w', 'grad_ssd_conv_b', 'grad_ssd_dt_bias', 'grad_ssd_a_log', 'grad_ssd_d', 'grad_ssd_norm_g', 'grad_pool_w', 'grad_pool_scale', 'grad_w_out', 'grad_norm2_g', 'grad_ffn_up', 'grad_ffn_conv_w', 'grad_ffn_conv_b', 'grad_ffn_down', 'grad_final_g', 'delta_ada_w', 'delta_ada_b', 'delta_norm1_g', 'delta_w_in', 'delta_ssd_conv_w', 'delta_ssd_conv_b', 'delta_ssd_dt_bias', 'delta_ssd_a_log', 'delta_ssd_d', 'delta_ssd_norm_g', 'delta_pool_w', 'delta_pool_scale', 'delta_w_out', 'delta_norm2_g', 'delta_ffn_up', 'delta_ffn_conv_w', 'delta_ffn_conv_b', 'delta_ffn_down', 'delta_final_g', 'new_m_ada_w', 'new_m_ada_b', 'new_m_norm1_g', 'new_m_w_in', 'new_m_ssd_conv_w', 'new_m_ssd_conv_b', 'new_m_ssd_dt_bias', 'new_m_ssd_a_log', 'new_m_ssd_d', 'new_m_ssd_norm_g', 'new_m_pool_w', 'new_m_pool_scale', 'new_m_w_out', 'new_m_norm2_g', 'new_m_ffn_up', 'new_m_ffn_conv_w', 'new_m_ffn_conv_b', 'new_m_ffn_down', 'new_m_final_g', 'new_v_ada_w', 'new_v_ada_b', 'new_v_norm1_g', 'new_v_w_in', 'new_v_ssd_conv_w', 'new_v_ssd_conv_b', 'new_v_ssd_dt_bias', 'new_v_ssd_a_log', 'new_v_ssd_d', 'new_v_ssd_norm_g', 'new_v_pool_w', 'new_v_pool_scale', 'new_v_w_out', 'new_v_norm2_g', 'new_v_ffn_up', 'new_v_ffn_conv_w', 'new_v_ffn_conv_b', 'new_v_ffn_down', 'new_v_final_g']
TWIN_LEAF_KINDS = {'loss': 'loss', 'grad_x': 'grad_x', 'grad_ada_w': 'grad_w', 'grad_ada_b': 'grad_w', 'grad_norm1_g': 'grad_w', 'grad_w_in': 'grad_w', 'grad_ssd_conv_w': 'grad_w', 'grad_ssd_conv_b': 'grad_w', 'grad_ssd_dt_bias': 'grad_w', 'grad_ssd_a_log': 'grad_w', 'grad_ssd_d': 'grad_w', 'grad_ssd_norm_g': 'grad_w', 'grad_pool_w': 'grad_w', 'grad_pool_scale': 'grad_w', 'grad_w_out': 'grad_w', 'grad_norm2_g': 'grad_w', 'grad_ffn_up': 'grad_w', 'grad_ffn_conv_w': 'grad_w', 'grad_ffn_conv_b': 'grad_w', 'grad_ffn_down': 'grad_w', 'grad_final_g': 'grad_w', 'delta_ada_w': 'delta_w', 'delta_ada_b': 'delta_w', 'delta_norm1_g': 'delta_w', 'delta_w_in': 'delta_w', 'delta_ssd_conv_w': 'delta_w', 'delta_ssd_conv_b': 'delta_w', 'delta_ssd_dt_bias': 'delta_w', 'delta_ssd_a_log': 'delta_w', 'delta_ssd_d': 'delta_w', 'delta_ssd_norm_g': 'delta_w', 'delta_pool_w': 'delta_w', 'delta_pool_scale': 'delta_w', 'delta_w_out': 'delta_w', 'delta_norm2_g': 'delta_w', 'delta_ffn_up': 'delta_w', 'delta_ffn_conv_w': 'delta_w', 'delta_ffn_conv_b': 'delta_w', 'delta_ffn_down': 'delta_w', 'delta_final_g': 'delta_w', 'new_m_ada_w': 'new_m', 'new_m_ada_b': 'new_m', 'new_m_norm1_g': 'new_m', 'new_m_w_in': 'new_m', 'new_m_ssd_conv_w': 'new_m', 'new_m_ssd_conv_b': 'new_m', 'new_m_ssd_dt_bias': 'new_m', 'new_m_ssd_a_log': 'new_m', 'new_m_ssd_d': 'new_m', 'new_m_ssd_norm_g': 'new_m', 'new_m_pool_w': 'new_m', 'new_m_pool_scale': 'new_m', 'new_m_w_out': 'new_m', 'new_m_norm2_g': 'new_m', 'new_m_ffn_up': 'new_m', 'new_m_ffn_conv_w': 'new_m', 'new_m_ffn_conv_b': 'new_m', 'new_m_ffn_down': 'new_m', 'new_m_final_g': 'new_m', 'new_v_ada_w': 'new_v', 'new_v_ada_b': 'new_v', 'new_v_norm1_g': 'new_v', 'new_v_w_in': 'new_v', 'new_v_ssd_conv_w': 'new_v', 'new_v_ssd_conv_b': 'new_v', 'new_v_ssd_dt_bias': 'new_v', 'new_v_ssd_a_log': 'new_v', 'new_v_ssd_d': 'new_v', 'new_v_ssd_norm_g': 'new_v', 'new_v_pool_w': 'new_v', 'new_v_pool_scale': 'new_v', 'new_v_w_out': 'new_v', 'new_v_norm2_g': 'new_v', 'new_v_ffn_up': 'new_v', 'new_v_ffn_conv_w': 'new_v', 'new_v_ffn_conv_b': 'new_v', 'new_v_ffn_down': 'new_v', 'new_v_final_g': 'new_v'}


def _forward(args):
    return _fwd_reference(*[args[k] for k in FWD_PARAMS])


def _output_shape():
    out = _jax.eval_shape(lambda: _forward(_fwd_setup_inputs(0)))
    return out.shape, out.dtype

N_MICROBATCH = 1
ADAM_LR = 0.001
ADAM_B1 = 0.9
ADAM_B2 = 0.999
ADAM_EPS = 1e-08
ADAM_WD = 0.01
ADAM_STEP = 10
PER_EXAMPLE_BATCH_AXIS = {'x': 0, 'c': 0, 'positions': 0, 'loss_target': 0}
SHARED_INPUTS = []
_WEIGHT_DTYPES = {'ada_w': _jnp.float32, 'ada_b': _jnp.float32, 'norm1_g': _jnp.float32, 'w_in': _jnp.float32, 'ssd_conv_w': _jnp.float32, 'ssd_conv_b': _jnp.float32, 'ssd_dt_bias': _jnp.float32, 'ssd_a_log': _jnp.float32, 'ssd_d': _jnp.float32, 'ssd_norm_g': _jnp.float32, 'pool_w': _jnp.float32, 'pool_scale': _jnp.float32, 'w_out': _jnp.float32, 'norm2_g': _jnp.float32, 'ffn_up': _jnp.float32, 'ffn_conv_w': _jnp.float32, 'ffn_conv_b': _jnp.float32, 'ffn_down': _jnp.float32, 'final_g': _jnp.float32}
MOMENT_SCALE = {'ada_w': 5.872037e-02, 'ada_b': 9.802023e-02, 'norm1_g': 6.100338e-02, 'w_in': 4.096886e-02, 'ssd_conv_w': 4.146015e-02, 'ssd_conv_b': 4.934491e-02, 'ssd_dt_bias': 1.060577e-01, 'ssd_a_log': 2.896337e-01, 'ssd_d': 9.219058e-01, 'ssd_norm_g': 5.225893e-02, 'pool_w': 4.917058e-02, 'pool_scale': 4.809144e-02, 'w_out': 4.658979e-02, 'norm2_g': 5.286345e-02, 'ffn_up': 2.326352e-02, 'ffn_conv_w': 2.364769e-02, 'ffn_conv_b': 2.215115e-02, 'ffn_down': 3.789857e-02, 'final_g': 3.206238e+01}


def _to_microbatches(a, axis):
    t = _jnp.moveaxis(a, axis, 0)
    t = t.reshape((N_MICROBATCH, t.shape[0] // N_MICROBATCH) + t.shape[1:])
    return _jnp.moveaxis(t, 1, axis + 1)


def setup_inputs(seed: int = 0) -> dict:
    inp = _fwd_setup_inputs(seed)
    key = _jax.random.fold_in(_jax.random.key(seed), 7919)
    shape, _ = _output_shape()
    out = dict(inp)
    out["loss_target"] = _jax.random.normal(_jax.random.fold_in(key, 0), shape, _jnp.float32)
    for i, name in enumerate(TWIN_WEIGHTS):
        w = inp[name].astype(_jnp.float32)
        if MOMENT_SCALE is None:
            s = _jnp.sqrt(_jnp.mean(_jnp.square(w)) + 1e-30)
        else:
            s = MOMENT_SCALE[name]
        km, kv = _jax.random.split(_jax.random.fold_in(key, i + 1))
        out[name] = w
        out["m_" + name] = s * _jax.random.normal(km, w.shape, _jnp.float32)
        out["v_" + name] = (s * s) * _jax.random.uniform(kv, w.shape, _jnp.float32, 0.5, 1.5)
    if N_MICROBATCH > 1:
        for name, axis in PER_EXAMPLE_BATCH_AXIS.items():
            out[name] = _to_microbatches(out[name], axis)
    return {'x': out['x'], 'c': out['c'], 'positions': out['positions'], 'ada_w': out['ada_w'], 'ada_b': out['ada_b'], 'norm1_g': out['norm1_g'], 'w_in': out['w_in'], 'ssd_conv_w': out['ssd_conv_w'], 'ssd_conv_b': out['ssd_conv_b'], 'ssd_dt_bias': out['ssd_dt_bias'], 'ssd_a_log': out['ssd_a_log'], 'ssd_d': out['ssd_d'], 'ssd_norm_g': out['ssd_norm_g'], 'pool_w': out['pool_w'], 'pool_scale': out['pool_scale'], 'w_out': out['w_out'], 'norm2_g': out['norm2_g'], 'ffn_up': out['ffn_up'], 'ffn_conv_w': out['ffn_conv_w'], 'ffn_conv_b': out['ffn_conv_b'], 'ffn_down': out['ffn_down'], 'final_g': out['final_g'], 'loss_target': out['loss_target'], 'm_ada_w': out['m_ada_w'], 'm_ada_b': out['m_ada_b'], 'm_norm1_g': out['m_norm1_g'], 'm_w_in': out['m_w_in'], 'm_ssd_conv_w': out['m_ssd_conv_w'], 'm_ssd_conv_b': out['m_ssd_conv_b'], 'm_ssd_dt_bias': out['m_ssd_dt_bias'], 'm_ssd_a_log': out['m_ssd_a_log'], 'm_ssd_d': out['m_ssd_d'], 'm_ssd_norm_g': out['m_ssd_norm_g'], 'm_pool_w': out['m_pool_w'], 'm_pool_scale': out['m_pool_scale'], 'm_w_out': out['m_w_out'], 'm_norm2_g': out['m_norm2_g'], 'm_ffn_up': out['m_ffn_up'], 'm_ffn_conv_w': out['m_ffn_conv_w'], 'm_ffn_conv_b': out['m_ffn_conv_b'], 'm_ffn_down': out['m_ffn_down'], 'm_final_g': out['m_final_g'], 'v_ada_w': out['v_ada_w'], 'v_ada_b': out['v_ada_b'], 'v_norm1_g': out['v_norm1_g'], 'v_w_in': out['v_w_in'], 'v_ssd_conv_w': out['v_ssd_conv_w'], 'v_ssd_conv_b': out['v_ssd_conv_b'], 'v_ssd_dt_bias': out['v_ssd_dt_bias'], 'v_ssd_a_log': out['v_ssd_a_log'], 'v_ssd_d': out['v_ssd_d'], 'v_ssd_norm_g': out['v_ssd_norm_g'], 'v_pool_w': out['v_pool_w'], 'v_pool_scale': out['v_pool_scale'], 'v_w_out': out['v_w_out'], 'v_norm2_g': out['v_norm2_g'], 'v_ffn_up': out['v_ffn_up'], 'v_ffn_conv_w': out['v_ffn_conv_w'], 'v_ffn_conv_b': out['v_ffn_conv_b'], 'v_ffn_down': out['v_ffn_down'], 'v_final_g': out['v_final_g']}


def _loss(weights, diff, rest, loss_target):
    with _jax.named_scope("forward"):
        args = {**rest, TWIN_DIFF_INPUT: diff, **{k: w.astype(_WEIGHT_DTYPES[k]) for k, w in weights.items()}}
        y = _forward(args)
    with _jax.named_scope("loss_head"):
        err = _jnp.square(y.astype(_jnp.float32) - loss_target)
        return 0.5 * _jnp.sum(_jnp.mean(err, axis=-1)) if err.ndim else 0.5 * err


def _adamw(w, g, m, v):
    m = ADAM_B1 * m + (1.0 - ADAM_B1) * g
    v = ADAM_B2 * v + (1.0 - ADAM_B2) * _jnp.square(g)
    m_hat = m / (1.0 - ADAM_B1 ** ADAM_STEP)
    v_hat = v / (1.0 - ADAM_B2 ** ADAM_STEP)
    delta = -ADAM_LR * (m_hat / (_jnp.sqrt(v_hat) + ADAM_EPS) + ADAM_WD * w)
    return delta, m, v


def reference(x, c, positions, ada_w, ada_b, norm1_g, w_in, ssd_conv_w, ssd_conv_b, ssd_dt_bias, ssd_a_log, ssd_d, ssd_norm_g, pool_w, pool_scale, w_out, norm2_g, ffn_up, ffn_conv_w, ffn_conv_b, ffn_down, final_g, loss_target, m_ada_w, m_ada_b, m_norm1_g, m_w_in, m_ssd_conv_w, m_ssd_conv_b, m_ssd_dt_bias, m_ssd_a_log, m_ssd_d, m_ssd_norm_g, m_pool_w, m_pool_scale, m_w_out, m_norm2_g, m_ffn_up, m_ffn_conv_w, m_ffn_conv_b, m_ffn_down, m_final_g, v_ada_w, v_ada_b, v_norm1_g, v_w_in, v_ssd_conv_w, v_ssd_conv_b, v_ssd_dt_bias, v_ssd_a_log, v_ssd_d, v_ssd_norm_g, v_pool_w, v_pool_scale, v_w_out, v_norm2_g, v_ffn_up, v_ffn_conv_w, v_ffn_conv_b, v_ffn_down, v_final_g):
    given = dict(x=x, c=c, positions=positions, ada_w=ada_w, ada_b=ada_b, norm1_g=norm1_g, w_in=w_in, ssd_conv_w=ssd_conv_w, ssd_conv_b=ssd_conv_b, ssd_dt_bias=ssd_dt_bias, ssd_a_log=ssd_a_log, ssd_d=ssd_d, ssd_norm_g=ssd_norm_g, pool_w=pool_w, pool_scale=pool_scale, w_out=w_out, norm2_g=norm2_g, ffn_up=ffn_up, ffn_conv_w=ffn_conv_w, ffn_conv_b=ffn_conv_b, ffn_down=ffn_down, final_g=final_g, loss_target=loss_target, m_ada_w=m_ada_w, m_ada_b=m_ada_b, m_norm1_g=m_norm1_g, m_w_in=m_w_in, m_ssd_conv_w=m_ssd_conv_w, m_ssd_conv_b=m_ssd_conv_b, m_ssd_dt_bias=m_ssd_dt_bias, m_ssd_a_log=m_ssd_a_log, m_ssd_d=m_ssd_d, m_ssd_norm_g=m_ssd_norm_g, m_pool_w=m_pool_w, m_pool_scale=m_pool_scale, m_w_out=m_w_out, m_norm2_g=m_norm2_g, m_ffn_up=m_ffn_up, m_ffn_conv_w=m_ffn_conv_w, m_ffn_conv_b=m_ffn_conv_b, m_ffn_down=m_ffn_down, m_final_g=m_final_g, v_ada_w=v_ada_w, v_ada_b=v_ada_b, v_norm1_g=v_norm1_g, v_w_in=v_w_in, v_ssd_conv_w=v_ssd_conv_w, v_ssd_conv_b=v_ssd_conv_b, v_ssd_dt_bias=v_ssd_dt_bias, v_ssd_a_log=v_ssd_a_log, v_ssd_d=v_ssd_d, v_ssd_norm_g=v_ssd_norm_g, v_pool_w=v_pool_w, v_pool_scale=v_pool_scale, v_w_out=v_w_out, v_norm2_g=v_norm2_g, v_ffn_up=v_ffn_up, v_ffn_conv_w=v_ffn_conv_w, v_ffn_conv_b=v_ffn_conv_b, v_ffn_down=v_ffn_down, v_final_g=v_final_g)
    weights = {n: given[n] for n in TWIN_WEIGHTS}
    shared = {n: given[n] for n in SHARED_INPUTS}
    per_example = {n: given[n] for n in ['x', 'c', 'positions']}
    grad_fn = _jax.value_and_grad(_loss, argnums=(0, 1))

    def one_microbatch(ex, loss_target):
        ex = dict(ex)
        diff = ex.pop(TWIN_DIFF_INPUT)
        return grad_fn(weights, diff, {**shared, **ex}, loss_target)

    if N_MICROBATCH == 1:
        loss, (grad_w, grad_x) = one_microbatch(per_example, given["loss_target"])
    else:
        def body(carry, xs):
            loss_sum, grad_sum = carry
            l_k, (gw_k, gx_k) = one_microbatch(xs[0], xs[1])
            with _jax.named_scope("update"):
                return (loss_sum + l_k, _jax.tree.map(_jnp.add, grad_sum, gw_k)), gx_k

        init = (_jnp.zeros((), _jnp.float32), _jax.tree.map(_jnp.zeros_like, weights))
        (loss, grad_w), grad_x = _jax.lax.scan(body, init, (per_example, given["loss_target"]))
    with _jax.named_scope("update"):
        delta_w, new_m, new_v = {}, {}, {}
        for n in TWIN_WEIGHTS:
            delta_w[n], new_m[n], new_v[n] = _adamw(weights[n], grad_w[n], given["m_" + n], given["v_" + n])
    return (loss, grad_x, *[grad_w[n] for n in TWIN_WEIGHTS], *[delta_w[n] for n in TWIN_WEIGHTS],
            *[new_m[n] for n in TWIN_WEIGHTS], *[new_v[n] for n in TWIN_WEIGHTS])
```

```python
import functools
import math

import numpy as np
import jax
import jax.numpy as jnp
from jax import lax
from jax.experimental import pallas as pl
from jax.experimental.pallas import tpu as pltpu

F32 = jnp.float32
BF16 = jnp.bfloat16
HI = lax.Precision.HIGHEST
MESH = pl.DeviceIdType.MESH

D_MODEL = 1024
SEQ = 4096
DEPTH = 2
SSD_INNER = 512
SSD_HEADS = 8
SSD_STATE = 128
POOL_W = 256
POOL_WINDOWS = (2, 4, 8, 16)
ATT_W = 256
ATT_HEADS = 4
ATT_HEAD_DIM = 64
ATT_PATTERNS = ((128, 1), (512, 4), (2048, 16))
ATT_BLOCK = 128
ROT_DIM = 16
ROPE_THETA = 500000.0
IN_W = 2568
IN_WP = 2688
FFN_DIM = 2816
NORM_EPS = 1e-6
ADAM_LR, ADAM_B1, ADAM_B2, ADAM_EPS, ADAM_WD, ADAM_STEP = 0.001, 0.9, 0.999, 1e-08, 0.01, 10

VMEM_LIMIT_BYTES = 56 * 1024 * 1024
NEG = -1e30


def _mxu(a, b, mode):
    dims = {"nn": ((1,), (0,)), "nt": ((1,), (1,)), "tn": ((0,), (0,))}[mode]
    return lax.dot_general(a.astype(BF16), b.astype(BF16), (dims, ((), ())), preferred_element_type=F32)


@functools.partial(jax.custom_vjp, nondiff_argnums=(2,))
def _bdot(a, b, mode):
    return _mxu(a, b, mode)


def _bdot_fwd(a, b, mode):
    return _mxu(a, b, mode), (a, b)


def _bdot_bwd(mode, res, g):
    a, b = res
    if mode == "nn":
        return _mxu(g, b, "nt"), _mxu(a, g, "tn")
    if mode == "nt":
        return _mxu(g, b, "nn"), _mxu(g, a, "tn")
    return _mxu(b, g, "nt"), _mxu(a, g, "nn")


_bdot.defvjp(_bdot_fwd, _bdot_bwd)


def _fxu(a, b, mode):
    dims = {"nn": ((1,), (0,)), "nt": ((1,), (1,)), "tn": ((0,), (0,))}[mode]
    return lax.dot_general(a, b, (dims, ((), ())), precision=HI, preferred_element_type=F32)


@functools.partial(jax.custom_vjp, nondiff_argnums=(2,))
def _fdot(a, b, mode):
    return _fxu(a, b, mode)


def _fdot_fwd(a, b, mode):
    return _fxu(a, b, mode), (a, b)


def _fdot_bwd(mode, res, g):
    a, b = res
    if mode == "nn":
        return _fxu(g, b, "nt"), _fxu(a, g, "tn")
    if mode == "nt":
        return _fxu(g, b, "nn"), _fxu(g, a, "tn")
    return _fxu(b, g, "nt"), _fxu(a, g, "nn")


_fdot.defvjp(_fdot_fwd, _fdot_bwd)


def _iota(shape, dim):
    return lax.broadcasted_iota(jnp.int32, shape, dim)


def _make_shift(h):
    @functools.partial(jax.custom_vjp, nondiff_argnums=(2,))
    def shift(halo, cur, k):
        if k == 0:
            return cur
        full = jnp.concatenate([halo, cur], axis=0)
        return pltpu.roll(full, k, 0)[h:]

    def fwd(halo, cur, k):
        return shift(halo, cur, k), None

    def bwd(k, _, g):
        t, w = g.shape
        if k == 0:
            return jnp.zeros((h, w), F32), g
        d_cur = jnp.where(_iota((t, w), 0) < t - k, pltpu.roll(g, t - k, 0), 0.0)
        top = g[:h]
        d_halo = jnp.where(_iota((h, w), 0) >= h - k, pltpu.roll(top, h - k, 0) if k < h else top, 0.0)
        return d_halo, d_cur

    shift.defvjp(fwd, bwd)
    return shift


_shift8 = _make_shift(8)
_shift16 = _make_shift(16)


def _make_tail(h):
    @jax.custom_vjp
    def tail(x):
        return x[x.shape[0] - h:]

    def fwd(x):
        return tail(x), x.shape[0]

    def bwd(t, g):
        return (jnp.concatenate([jnp.zeros((t - h, g.shape[1]), F32), g], axis=0),)

    tail.defvjp(fwd, bwd)
    return tail


_tail8 = _make_tail(8)
_tail16 = _make_tail(16)


def _rowk(w, k):
    return jnp.sum(jnp.where(_iota(w.shape, 0) == k, w, 0.0), axis=0, keepdims=True)


def _silu(x):
    return x * (1.0 / (1.0 + jnp.exp(-x)))


def _softplus(x):
    return jnp.maximum(x, 0.0) + jnp.log(1.0 + jnp.exp(-jnp.abs(x)))


def _tile(dim, target, unit=128):
    if dim <= target:
        return dim
    best = None
    for t in range(unit, target + 1, unit):
        if dim % t == 0:
            best = t
    assert best is not None, (dim, target)
    return best


class Row:
    def __init__(self, arr, w=None, fb=None, fc=None, diff=True, slot=False, dcols=None, dfc=None):
        self.arr = arr
        self.w = arr.shape[2] if w is None else w
        self.fb = (lambda b: 0) if fb is None else fb
        self.fc = (lambda b: 0) if fc is None else fc
        self.diff = diff
        self.slot = slot
        self.dcols = dcols
        self.dfc = dfc


class Vec:
    def __init__(self, arr, w=None, fc=None, diff=True):
        self.arr = arr
        self.w = arr.shape[1] if w is None else w
        self.fc = fc
        self.diff = diff


def _row_spec(r, t, nchunk, reverse):
    if reverse:
        return pl.BlockSpec((1, t, r.w), lambda b, i, r=r: (r.fb(b), nchunk - 1 - i, r.fc(b)))
    return pl.BlockSpec((1, t, r.w), lambda b, i, r=r: (r.fb(b), i, r.fc(b)))


def _vec_spec(v):
    if v.fc is None:
        return pl.BlockSpec(v.arr.shape, lambda b, i: (0, 0))
    return pl.BlockSpec((v.arr.shape[0], v.w), lambda b, i, v=v: (0, v.fc(b)))


def _cparams():
    return pltpu.CompilerParams(dimension_semantics=("arbitrary", "arbitrary"), vmem_limit_bytes=VMEM_LIMIT_BYTES)


def scan_fwd(name, fn, *, nb, nchunk, t, rows, vecs, carries, outs, save):
    nr, nv, nc, no = len(rows), len(vecs), len(carries), len(outs)

    def body(*refs):
        row_refs, vec_refs = refs[:nr], refs[nr:nr + nv]
        out_refs = refs[nr + nv:nr + nv + no]
        save_refs = refs[nr + nv + no:nr + nv + no + (nc if save else 0)]
        car = refs[len(refs) - nc:] if nc else ()
        b, i = pl.program_id(0), pl.program_id(1)
        if nc:
            @pl.when(i == 0)
            def _():
                for c_ref in car:
                    c_ref[...] = jnp.zeros(c_ref.shape, F32)
        cin = [c_ref[...] for c_ref in car]
        if save:
            for s_ref, cv in zip(save_refs, cin):
                s_ref[0, 0] = cv
        new_c, o = fn(i, b, cin, [r[0] for r in row_refs], [v[...] for v in vec_refs])
        for c_ref, cv in zip(car, new_c):
            c_ref[...] = cv
        for o_ref, ov in zip(out_refs, o):
            o_ref[0] = ov.astype(o_ref.dtype)

    out_shape = [o.arr for o in outs]
    out_specs = [_row_spec(o, t, nchunk, False) for o in outs]
    if save:
        for cs in carries:
            out_shape.append(jax.ShapeDtypeStruct((nb, nchunk) + tuple(cs), F32))
            out_specs.append(pl.BlockSpec((1, 1) + tuple(cs), lambda b, i: (b, i, 0, 0)))
    res = pl.pallas_call(
        body, name=name, grid=(nb, nchunk),
        in_specs=[_row_spec(r, t, nchunk, False) for r in rows] + [_vec_spec(v) for v in vecs],
        out_specs=out_specs, out_shape=out_shape,
        scratch_shapes=[pltpu.VMEM(tuple(cs), F32) for cs in carries],
        compiler_params=_cparams(),
    )(*[r.arr for r in rows], *[v.arr for v in vecs])
    return list(res[:no]), list(res[no:])


def scan_bwd(name, fn, *, nb, nchunk, t, rows, vecs, carries, saved, douts, adds=None):
    adds = adds or {}
    nr, nv, nc, no = len(rows), len(vecs), len(carries), len(douts)
    dri = [k for k, r in enumerate(rows) if r.diff]
    dvi = [k for k, v in enumerate(vecs) if v.diff]
    add_keys = sorted(adds)
    na = len(add_keys)

    def body(*refs):
        p = 0
        row_refs = refs[p:p + nr]; p += nr
        vec_refs = refs[p:p + nv]; p += nv
        save_refs = refs[p:p + nc]; p += nc
        dout_refs = refs[p:p + no]; p += no
        add_refs = refs[p:p + na]; p += na
        drow_refs = refs[p:p + len(dri)]; p += len(dri)
        dvec_refs = refs[p:p + len(dvi)]; p += len(dvi)
        dcar = refs[p:]
        b, ir = pl.program_id(0), pl.program_id(1)
        ci = nchunk - 1 - ir
        if nc:
            @pl.when(ir == 0)
            def _():
                for c_ref in dcar:
                    c_ref[...] = jnp.zeros(c_ref.shape, F32)
        rows_v = [r[0] for r in row_refs]
        vecs_v = [v[...] for v in vec_refs]
        cin = [s[0, 0] for s in save_refs]
        dc = [c_ref[...] for c_ref in dcar]
        dout_v = [d[0].astype(F32) for d in dout_refs]

        def f(cs, dr, dv):
            rr, vv = list(rows_v), list(vecs_v)
            for k, idx in enumerate(dri):
                rr[idx] = dr[k]
            for k, idx in enumerate(dvi):
                vv[idx] = dv[k]
            return fn(ci, b, cs, rr, vv)

        _, vjp = jax.vjp(f, cin, [rows_v[k].astype(F32) for k in dri], [vecs_v[k].astype(F32) for k in dvi])
        dcin, drows, dvecs = vjp((dc, dout_v))
        for c_ref, cv in zip(dcar, dcin):
            c_ref[...] = cv
        for k, (o_ref, ov) in enumerate(zip(drow_refs, drows)):
            if dri[k] in adds:
                ov = ov + add_refs[add_keys.index(dri[k])][0].astype(F32)
            o_ref[0] = ov.astype(o_ref.dtype)
        for k, (o_ref, ov) in enumerate(zip(dvec_refs, dvecs)):
            first = (ir == 0) if vecs[dvi[k]].fc is not None else jnp.logical_and(ir == 0, b == 0)

            @pl.when(first)
            def _(o_ref=o_ref, ov=ov):
                o_ref[...] = ov

            @pl.when(jnp.logical_not(first))
            def _(o_ref=o_ref, ov=ov):
                o_ref[...] += ov

    in_specs = ([_row_spec(r, t, nchunk, True) for r in rows] + [_vec_spec(v) for v in vecs]
                + [pl.BlockSpec((1, 1) + tuple(cs), lambda b, i: (b, nchunk - 1 - i, 0, 0)) for cs in carries]
                + [_row_spec(d, t, nchunk, True) for d in douts]
                + [_row_spec(adds[k], t, nchunk, True) for k in add_keys])
    out_shape, out_specs = [], []
    for k in dri:
        r = rows[k]
        if r.slot:
            out_shape.append(jax.ShapeDtypeStruct((nb, r.arr.shape[1], r.w), F32))
            out_specs.append(pl.BlockSpec((1, t, r.w), lambda b, i: (b, nchunk - 1 - i, 0)))
        elif r.dcols is not None:
            out_shape.append(jax.ShapeDtypeStruct((r.arr.shape[0], r.arr.shape[1], r.dcols), F32))
            out_specs.append(pl.BlockSpec((1, t, r.w), lambda b, i, r=r: (r.fb(b), nchunk - 1 - i, r.dfc(b))))
        else:
            out_shape.append(jax.ShapeDtypeStruct(r.arr.shape, F32))
            out_specs.append(_row_spec(r, t, nchunk, True))
    for k in dvi:
        out_shape.append(jax.ShapeDtypeStruct(vecs[k].arr.shape, F32))
        out_specs.append(_vec_spec(vecs[k]))
    res = pl.pallas_call(
        body, name=name, grid=(nb, nchunk), in_specs=in_specs, out_specs=out_specs, out_shape=out_shape,
        scratch_shapes=[pltpu.VMEM(tuple(cs), F32) for cs in carries],
        compiler_params=_cparams(),
    )(*[r.arr for r in rows], *[v.arr for v in vecs], *saved, *[d.arr for d in douts], *[adds[k].arr for k in add_keys])
    return list(res[:len(dri)]), list(res[len(dri):])


def out_row(shape, dtype=F32, w=None, fb=None, fc=None):
    return Row(jax.ShapeDtypeStruct(shape, dtype), w, fb, fc)


def _conv(shift, halo, cur, w, bias, taps):
    y = bias
    for k in range(taps):
        y = y + _rowk(w, k) * shift(halo, cur, taps - 1 - k)
    return y


def _ssd_fn(ci, b, carries, rows, vecs):
    cx, cb_, cc, ht = carries
    z, xr, br, cr, dtr = rows
    cwx, cbx, cwb, cbb, cwc, cbc, dtb, alog, dsk, ng, e = vecs
    t = z.shape[0]
    xs = _silu(_conv(_shift8, cx, xr, cwx, cbx, 4))
    bm = _silu(_conv(_shift8, cb_, br, cwb, cbb, 4))
    cm = _silu(_conv(_shift8, cc, cr, cwc, cbc, 4))
    dt = _softplus(dtr + dtb)
    da = dt * (-jnp.exp(alog))
    r, c = _iota((t, t), 0), _iota((t, t), 1)
    causal = r >= c
    acol = _fdot(causal.astype(F32), da, "nn")
    arow = _fdot(da, (r <= c).astype(F32), "tn")
    a = _fdot(acol, e, "nn")
    dtx = _fdot(dt, e, "nn")
    atot = jnp.sum(jnp.where(_iota(a.shape, 0) == t - 1, a, 0.0), axis=0, keepdims=True)
    x = xs * dtx
    cbm = _bdot(cm, bm, "nt")
    lane, sub = _iota(acol.shape, 1), _iota(arow.shape, 0)
    colh = _iota(x.shape, 1) // 64
    ydiag = jnp.zeros(x.shape, F32)
    for j in range(4):
        h = 4 * b + j
        ac = jnp.sum(jnp.where(lane == h, acol, 0.0), axis=1, keepdims=True)
        ar = jnp.sum(jnp.where(sub == h, arow, 0.0), axis=0, keepdims=True)
        lmat = jnp.exp(jnp.where(causal, ac - ar, NEG))
        ydiag = ydiag + _bdot(cbm * lmat, jnp.where(colh == j, x, 0.0), "nn")
    yoff = _bdot(cm, ht, "nn") * jnp.exp(a)
    ht_new = ht * jnp.exp(atot) + _bdot(bm, x * jnp.exp(atot - a), "tn")
    dx = jnp.sum(_fdot(jnp.broadcast_to(dsk, (8, dsk.shape[1])), e, "nn"), axis=0, keepdims=True) * 0.125
    y = ydiag + yoff + dx * xs
    yz = y * _silu(z)
    yn = yz * lax.rsqrt(jnp.mean(yz * yz, axis=-1, keepdims=True) + NORM_EPS) * ng
    return [_tail8(xr), _tail8(br), _tail8(cr), ht_new], [yn]


_SSD_T = 256
_SSD_CARRIES = [(8, 256), (8, 128), (8, 128), (128, 256)]


def _ssd_io(proj3, p):
    own = lambda b: b
    rows = [Row(proj3, 256, fc=own, dcols=512, dfc=own), Row(proj3, 256, fc=lambda b: 2 + b, dcols=512, dfc=own),
            Row(proj3, 128, fc=lambda b: 8 + b, dcols=256, dfc=own), Row(proj3, 128, fc=lambda b: 10 + b, dcols=256, dfc=own),
            Row(proj3, 128, fc=lambda b: 20, slot=True)]
    vecs = [Vec(p["cw"], 256, lambda b: b), Vec(p["cb"], 256, lambda b: b),
            Vec(p["cw"], 128, lambda b: 4 + b), Vec(p["cb"], 128, lambda b: 4 + b),
            Vec(p["cw"], 128, lambda b: 6 + b), Vec(p["cb"], 128, lambda b: 6 + b),
            Vec(p["dtb"]), Vec(p["alog"]), Vec(p["dsk"]), Vec(p["ng"], 256, lambda b: b),
            Vec(p["e"], 256, lambda b: b, diff=False)]
    return rows, vecs


def ssd_forward(name, proj3, p):
    rows, vecs = _ssd_io(proj3, p)
    s = proj3.shape[1]
    (y,), saved = scan_fwd(name, _ssd_fn, nb=2, nchunk=s // _SSD_T, t=_SSD_T, rows=rows, vecs=vecs,
                           carries=_SSD_CARRIES, outs=[out_row((1, s, SSD_INNER), F32, 256, fc=lambda b: b)], save=True)
    return y, saved


def ssd_backward(name, proj3, p, saved, dmix3):
    rows, vecs = _ssd_io(proj3, p)
    s = proj3.shape[1]
    drows, dvecs = scan_bwd(name, _ssd_fn, nb=2, nchunk=s // _SSD_T, t=_SSD_T, rows=rows, vecs=vecs,
                            carries=_SSD_CARRIES, saved=saved, douts=[Row(dmix3, 256, fc=lambda b: b)])
    return drows, dvecs


def _pool_fn(ci, b, carries, rows, vecs):
    (cu,) = carries
    (u,) = rows
    wbd, scale = vecs
    t = u.shape[0]
    pos = ci * t + _iota(u.shape, 0)
    grp = _iota(u.shape, 1) // 64
    acc, pooled, k = u, jnp.zeros(u.shape, F32), 1
    for gi, w in enumerate(POOL_WINDOWS):
        while k < w:
            acc = acc + _shift16(cu, u, k)
            k += 1
        pooled = jnp.where(grp == gi, acc / jnp.minimum(pos + 1, w).astype(F32), pooled)
    y = _bdot(pooled - u, wbd, "nn") * scale
    return [_tail16(u)], [y]


_POOL_T = 256


def _pool_io(proj3, wbd, scale):
    return [Row(proj3, 256, fc=lambda b: 6, dcols=256, dfc=lambda b: 0)], [Vec(wbd), Vec(scale)]


def pool_forward(name, proj3, wbd, scale):
    rows, vecs = _pool_io(proj3, wbd, scale)
    s = proj3.shape[1]
    (y,), saved = scan_fwd(name, _pool_fn, nb=1, nchunk=s // _POOL_T, t=_POOL_T, rows=rows, vecs=vecs,
                           carries=[(16, 256)], outs=[out_row((1, s, POOL_W))], save=True)
    return y, saved


def pool_backward(name, proj3, wbd, scale, saved, dmix3):
    rows, vecs = _pool_io(proj3, wbd, scale)
    s = proj3.shape[1]
    return scan_bwd(name, _pool_fn, nb=1, nchunk=s // _POOL_T, t=_POOL_T, rows=rows, vecs=vecs,
                    carries=[(16, 256)], saved=saved, douts=[Row(dmix3, 256, fc=lambda b: 2)])


def _attn_fn(ci, b, carries, rows, vecs):
    kp, vp = carries
    q, k, v, cs, sn = rows
    (rot,) = vecs
    qr = q * cs + _fdot(q, rot, "nn") * sn
    kr = k * cs + _fdot(k, rot, "nn") * sn
    scale = ATT_HEAD_DIM ** -0.5
    n = q.shape[0]
    r, c = _iota((n, n), 0), _iota((n, n), 1)
    sp = jnp.where(jnp.logical_and(c >= r, ci > 0), _bdot(qr, kp, "nt") * scale, NEG)
    sc = jnp.where(r >= c, _bdot(qr, kr, "nt") * scale, NEG)
    m = lax.stop_gradient(jnp.maximum(jnp.max(sp, axis=1, keepdims=True), jnp.max(sc, axis=1, keepdims=True)))
    pp, pc = jnp.exp(sp - m), jnp.exp(sc - m)
    l = jnp.sum(pp, axis=1, keepdims=True) + jnp.sum(pc, axis=1, keepdims=True)
    o = (_bdot(pp, vp, "nn") + _bdot(pc, v, "nn")) / l
    lse = m + jnp.log(l)
    return [kr, v], [o, jnp.broadcast_to(lse, o.shape)]


def _attn_io(qs, ks, vs, cos, sin, rot):
    per = lambda b: b
    tab = lambda b: b // ATT_HEADS
    rows = [Row(qs, fb=per), Row(ks, fb=per), Row(vs, fb=per), Row(cos, fb=tab, diff=False), Row(sin, fb=tab, diff=False)]
    return rows, [Vec(rot, diff=False)]


_ATT_CARRIES = [(ATT_BLOCK, ATT_HEAD_DIM), (ATT_BLOCK, ATT_HEAD_DIM)]


def attn_forward(name, qs, ks, vs, cos, sin, rot):
    rows, vecs = _attn_io(qs, ks, vs, cos, sin, rot)
    nseq, l, e = qs.shape
    per = lambda b: b
    (o, lse), saved = scan_fwd(name, _attn_fn, nb=nseq, nchunk=l // ATT_BLOCK, t=ATT_BLOCK, rows=rows, vecs=vecs,
                               carries=_ATT_CARRIES, outs=[out_row(qs.shape, fb=per), out_row(qs.shape, fb=per)], save=True)
    return o, lse, saved


def attn_backward(name, qs, ks, vs, cos, sin, rot, saved, do, dlse):
    rows, vecs = _attn_io(qs, ks, vs, cos, sin, rot)
    nseq, l, e = qs.shape
    per = lambda b: b
    drows, _ = scan_bwd(name, _attn_fn, nb=nseq, nchunk=l // ATT_BLOCK, t=ATT_BLOCK, rows=rows, vecs=vecs,
                        carries=_ATT_CARRIES, saved=saved, douts=[Row(do, fb=per), Row(dlse, fb=per)])
    return drows


def _merge_fn(ci, b, carries, rows, vecs):
    o1, o2, o3, l1, l2, l3 = rows
    mx = lax.stop_gradient(jnp.maximum(l1, jnp.maximum(l2, l3)))
    e1, e2, e3 = jnp.exp(l1 - mx), jnp.exp(l2 - mx), jnp.exp(l3 - mx)
    return [], [(e1 * o1 + e2 * o2 + e3 * o3) / (e1 + e2 + e3)]


_ROW_T = 256


def merge_forward(name, os_, ls_):
    s = os_[0].shape[1]
    (y,), _ = scan_fwd(name, _merge_fn, nb=1, nchunk=s // _ROW_T, t=_ROW_T, rows=[Row(a) for a in (*os_, *ls_)], vecs=[],
                       carries=[], outs=[out_row((1, s, ATT_W))], save=False)
    return y


def merge_backward(name, os_, ls_, dmix3):
    s = os_[0].shape[1]
    drows, _ = scan_bwd(name, _merge_fn, nb=1, nchunk=s // _ROW_T, t=_ROW_T, rows=[Row(a) for a in (*os_, *ls_)], vecs=[],
                        carries=[], saved=[], douts=[Row(dmix3, 256, fc=lambda b: 3)])
    return drows


def _norm_mod_fn(ci, b, carries, rows, vecs):
    (x,) = rows
    g, sc, sh = vecs
    xn = x * lax.rsqrt(jnp.mean(x * x, axis=-1, keepdims=True) + NORM_EPS)
    return [], [xn * g * (1.0 + sc) + sh]


def norm_mod_forward(name, x3, g, sc, sh):
    s = x3.shape[1]
    (h,), _ = scan_fwd(name, _norm_mod_fn, nb=1, nchunk=s // _ROW_T, t=_ROW_T, rows=[Row(x3)], vecs=[Vec(g), Vec(sc), Vec(sh)],
                       carries=[], outs=[out_row(x3.shape, BF16)], save=False)
    return h


def norm_mod_backward(name, x3, g, sc, sh, dh3, add3):
    s = x3.shape[1]
    (dx,), dv = scan_bwd(name, _norm_mod_fn, nb=1, nchunk=s // _ROW_T, t=_ROW_T, rows=[Row(x3)], vecs=[Vec(g), Vec(sc), Vec(sh)],
                         carries=[], saved=[], douts=[Row(dh3)], adds={0: Row(add3)})
    return dx, dv


def _gate_fn(ci, b, carries, rows, vecs):
    return [], [rows[0] * vecs[0]]


def gate_backward(name, o3, g, dx3):
    s = o3.shape[1]
    (do,), (dg,) = scan_bwd(name, _gate_fn, nb=1, nchunk=s // _ROW_T, t=_ROW_T, rows=[Row(o3)], vecs=[Vec(g)],
                            carries=[], saved=[], douts=[Row(dx3)])
    return do, dg


def _ffn_fn(ci, b, carries, rows, vecs):
    cg, cu = carries
    ug, uu = rows
    wg, bg, wu, bu = vecs
    hg = _conv(_shift8, cg, ug, wg, bg, 3)
    hu = _conv(_shift8, cu, uu, wu, bu, 3)
    return [_tail8(ug), _tail8(uu)], [_silu(hg) * hu]


_FFN_T = 256
_FFN_CW = FFN_DIM // 2
_FFN_CARRIES = [(8, _FFN_CW), (8, _FFN_CW)]


def _ffn_io(up3, cw, cb):
    lo, hi = (lambda b: b), (lambda b: 2 + b)
    rows = [Row(up3, _FFN_CW, fc=lo, dcols=FFN_DIM, dfc=lo), Row(up3, _FFN_CW, fc=hi, dcols=FFN_DIM, dfc=lo)]
    vecs = [Vec(cw, _FFN_CW, lo), Vec(cb, _FFN_CW, lo), Vec(cw, _FFN_CW, hi), Vec(cb, _FFN_CW, hi)]
    return rows, vecs


def ffn_mid_forward(name, up3, cw, cb):
    rows, vecs = _ffn_io(up3, cw, cb)
    s = up3.shape[1]
    (act,), saved = scan_fwd(name, _ffn_fn, nb=2, nchunk=s // _FFN_T, t=_FFN_T, rows=rows, vecs=vecs, carries=_FFN_CARRIES,
                             outs=[out_row((1, s, FFN_DIM), BF16, _FFN_CW, fc=lambda b: b)], save=True)
    return act, saved


def ffn_mid_backward(name, up3, cw, cb, saved, dact3):
    rows, vecs = _ffn_io(up3, cw, cb)
    s = up3.shape[1]
    return scan_bwd(name, _ffn_fn, nb=2, nchunk=s // _FFN_T, t=_FFN_T, rows=rows, vecs=vecs, carries=_FFN_CARRIES,
                    saved=saved, douts=[Row(dact3, _FFN_CW, fc=lambda b: b)])


def _adam_fn(ci, b, carries, rows, vecs):
    w, g, m, v = rows
    m = ADAM_B1 * m + (1.0 - ADAM_B1) * g
    v = ADAM_B2 * v + (1.0 - ADAM_B2) * (g * g)
    m_hat = m / (1.0 - ADAM_B1 ** ADAM_STEP)
    v_hat = v / (1.0 - ADAM_B2 ** ADAM_STEP)
    delta = -ADAM_LR * (m_hat / (jnp.sqrt(v_hat) + ADAM_EPS) + ADAM_WD * w)
    return [], [delta, m, v]


def adamw(name, w, g, m, v):
    shape = w.shape
    c = shape[-1]
    r = int(np.prod(shape[:-1]))
    t = _tile(r, 256, 8)
    as3 = lambda a: a.reshape(1, r, c)
    outs, _ = scan_fwd(name, _adam_fn, nb=1, nchunk=r // t, t=t, rows=[Row(as3(a)) for a in (w, g, m, v)], vecs=[], carries=[],
                       outs=[out_row((1, r, c)) for _ in range(3)], save=False)
    return [o.reshape(shape) for o in outs]


def _to_strided(t, d):
    l = t.shape[0] // d
    return t.reshape(l, d, ATT_HEADS, ATT_HEAD_DIM).transpose(1, 2, 0, 3).reshape(d * ATT_HEADS, l, ATT_HEAD_DIM)


def _from_strided(t, d):
    l = t.shape[1]
    return t.reshape(d, ATT_HEADS, l, ATT_HEAD_DIM).transpose(2, 0, 1, 3).reshape(l * d, ATT_HEADS * ATT_HEAD_DIM)


def _table_strided(t, d):
    l = t.shape[0] // d
    return t.reshape(l, d, ATT_HEAD_DIM).transpose(1, 0, 2)


def rope_tables(positions):
    half = ROT_DIM // 2
    inv_freq = ROPE_THETA ** (-jnp.arange(0, ROT_DIM, 2, dtype=F32) / ROT_DIM)
    ang = positions.astype(F32)[:, None] * inv_freq
    s = positions.shape[0]
    cs = jnp.concatenate([jnp.cos(ang), jnp.cos(ang), jnp.ones((s, ATT_HEAD_DIM - ROT_DIM), F32)], axis=1)
    sn = jnp.concatenate([jnp.sin(ang), jnp.sin(ang), jnp.zeros((s, ATT_HEAD_DIM - ROT_DIM), F32)], axis=1)
    rot = np.zeros((ATT_HEAD_DIM, ATT_HEAD_DIM), np.float32)
    for e in range(half):
        rot[e + half, e] = -1.0
        rot[e, e + half] = 1.0
    return cs, sn, jnp.asarray(rot)


def attention_forward(lname, proj3, cs, sn, rot):
    q, k, v = proj3[0, :, 1792:2048], proj3[0, :, 2048:2304], proj3[0, :, 2304:2560]
    os_, ls_, keep = [], [], []
    for pi, (_, d) in enumerate(ATT_PATTERNS):
        qs, ks, vs = _to_strided(q, d), _to_strided(k, d), _to_strided(v, d)
        cd, sd = _table_strided(cs, d), _table_strided(sn, d)
        o, lse, saved = attn_forward(f"{lname}_attn{pi}", qs, ks, vs, cd, sd, rot)
        os_.append(_from_strided(o, d)[None])
        ls_.append(_from_strided(lse, d)[None])
        keep.append((qs, ks, vs, cd, sd, saved))
    y = merge_forward(f"{lname}_merge", os_, ls_)
    return y, (os_, ls_, keep)


def attention_backward(lname, res, rot, dmix3):
    os_, ls_, keep = res
    dm = merge_backward(f"{lname}_merge_b", os_, ls_, dmix3)
    tot = None
    for pi, (_, d) in enumerate(ATT_PATTERNS):
        qs, ks, vs, cd, sd, saved = keep[pi]
        do, dl = _to_strided(dm[pi][0], d), _to_strided(dm[3 + pi][0], d)
        dqkv = attn_backward(f"{lname}_attn{pi}_b", qs, ks, vs, cd, sd, rot, saved, do, dl)
        dqkv = [_from_strided(a, d) for a in dqkv]
        tot = dqkv if tot is None else [a + b for a, b in zip(tot, dqkv)]
    return tot


def mm(name, a, b, mode, out_dtype=F32, res=None, gate=None, tm=1024, tn=512, tk=1024):
    if mode == "nn":
        (m, k), n = a.shape, b.shape[1]
    elif mode == "nt":
        (m, k), n = a.shape, b.shape[0]
    else:
        (k, m), n = a.shape, b.shape[1]
    tm, tn, tk = _tile(m, tm), _tile(n, tn), _tile(k, tk)
    nk = k // tk
    a_spec = pl.BlockSpec((tk, tm), lambda i, j, q: (q, i)) if mode == "tn" else pl.BlockSpec((tm, tk), lambda i, j, q: (i, q))
    b_spec = pl.BlockSpec((tn, tk), lambda i, j, q: (j, q)) if mode == "nt" else pl.BlockSpec((tk, tn), lambda i, j, q: (q, j))
    o_spec = pl.BlockSpec((tm, tn), lambda i, j, q: (i, j))
    fused = res is not None

    def body(*refs):
        if fused:
            a_ref, b_ref, r_ref, g_ref, o_ref, o2_ref, acc = refs
        else:
            a_ref, b_ref, o_ref, acc = refs
        q = pl.program_id(2)

        @pl.when(q == 0)
        def _():
            acc[...] = jnp.zeros(acc.shape, F32)

        acc[...] += _mxu(a_ref[...], b_ref[...], mode)

        @pl.when(q == nk - 1)
        def _():
            o_ref[...] = acc[...].astype(o_ref.dtype)
            if fused:
                o2_ref[...] = r_ref[...] + g_ref[...] * acc[...]

    ins, in_specs = [a, b], [a_spec, b_spec]
    out_shape, out_specs = [jax.ShapeDtypeStruct((m, n), out_dtype)], [o_spec]
    if fused:
        ins += [res, gate]
        in_specs += [o_spec, pl.BlockSpec((1, tn), lambda i, j, q: (0, j))]
        out_shape.append(jax.ShapeDtypeStruct((m, n), F32))
        out_specs.append(o_spec)
    out = pl.pallas_call(
        body, name=name, grid=(m // tm, n // tn, nk), in_specs=in_specs, out_specs=out_specs, out_shape=out_shape,
        scratch_shapes=[pltpu.VMEM((tm, tn), F32)],
        compiler_params=pltpu.CompilerParams(dimension_semantics=("parallel", "parallel", "arbitrary"),
                                             vmem_limit_bytes=VMEM_LIMIT_BYTES),
    )(*ins)
    return tuple(out) if fused else out[0]


def final_loss(name, x3, t3, g):
    s, d = x3.shape[1], x3.shape[2]
    t = _ROW_T

    def body(x_ref, t_ref, g_ref, loss_ref, dx_ref, dg_ref):
        i = pl.program_id(0)
        tv = t_ref[0]

        def f(x, gg):
            y = x * lax.rsqrt(jnp.mean(x * x, axis=-1, keepdims=True) + NORM_EPS) * gg
            e = y - tv
            return 0.5 * jnp.sum(jnp.mean(e * e, axis=-1, keepdims=True), axis=0, keepdims=True)

        l, vjp = jax.vjp(f, x_ref[0], g_ref[...])
        dx, dg = vjp(jnp.ones((1, 1), F32))
        dx_ref[0] = dx

        @pl.when(i == 0)
        def _():
            loss_ref[...] = jnp.zeros(loss_ref.shape, F32)
            dg_ref[...] = jnp.zeros(dg_ref.shape, F32)

        loss_ref[...] += jnp.broadcast_to(l, loss_ref.shape)
        dg_ref[...] += dg

    row = pl.BlockSpec((1, t, d), lambda i: (0, i, 0))
    vec = pl.BlockSpec((1, d), lambda i: (0, 0))
    return pl.pallas_call(
        body, name=name, grid=(s // t,), in_specs=[row, row, vec],
        out_specs=[pl.BlockSpec((8, 128), lambda i: (0, 0)), row, vec],
        out_shape=[jax.ShapeDtypeStruct((8, 128), F32), jax.ShapeDtypeStruct(x3.shape, F32), jax.ShapeDtypeStruct((1, d), F32)],
        compiler_params=pltpu.CompilerParams(dimension_semantics=("arbitrary",), vmem_limit_bytes=VMEM_LIMIT_BYTES),
    )(x3, t3, g)


_ADA_TN = 512


def ada_forward(name, c16, ada_w):
    depth, d, cols = ada_w.shape

    def body(c_ref, w_ref, o_ref):
        o_ref[0] = _mxu(_silu(c_ref[...]), w_ref[0], "nn")

    return pl.pallas_call(
        body, name=name, grid=(depth, cols // _ADA_TN),
        in_specs=[pl.BlockSpec((16, d), lambda l, j: (0, 0)), pl.BlockSpec((1, d, _ADA_TN), lambda l, j: (l, 0, j))],
        out_specs=pl.BlockSpec((1, 16, _ADA_TN), lambda l, j: (l, 0, j)),
        out_shape=jax.ShapeDtypeStruct((depth, 16, cols), F32),
        compiler_params=pltpu.CompilerParams(dimension_semantics=("arbitrary", "arbitrary"), vmem_limit_bytes=VMEM_LIMIT_BYTES),
    )(c16, ada_w)


def ada_backward(name, c16, dmod16, w, m, v):
    depth, d, cols = w.shape

    def body(c_ref, dm_ref, w_ref, m_ref, v_ref, g_ref, dl_ref, nm_ref, nv_ref):
        g = _mxu(_silu(c_ref[...]), dm_ref[0], "tn")
        _, (delta, nm, nv) = _adam_fn(None, None, [], [w_ref[0], g, m_ref[0], v_ref[0]], [])
        g_ref[0], dl_ref[0], nm_ref[0], nv_ref[0] = g, delta, nm, nv

    blk = pl.BlockSpec((1, d, _ADA_TN), lambda l, j: (l, 0, j))
    return pl.pallas_call(
        body, name=name, grid=(depth, cols // _ADA_TN),
        in_specs=[pl.BlockSpec((16, d), lambda l, j: (0, 0)), pl.BlockSpec((1, 16, _ADA_TN), lambda l, j: (l, 0, j)), blk, blk, blk],
        out_specs=[blk] * 4, out_shape=[jax.ShapeDtypeStruct(w.shape, F32)] * 4,
        compiler_params=pltpu.CompilerParams(dimension_semantics=("arbitrary", "arbitrary"), vmem_limit_bytes=VMEM_LIMIT_BYTES),
    )(c16, dmod16, w, m, v)


def _sum_fn(ci, b, carries, rows, vecs):
    acc = rows[0]
    for r in rows[1:]:
        acc = acc + r
    return [], [acc]


def sum_slots(name, a, nsum, out_dtype=F32):
    n, r, c = a.shape
    nb = n // nsum
    t = _tile(r, 256, 8)
    rows = [Row(a, fb=(lambda b, k=k: k * nb + b)) for k in range(nsum)]
    (out,), _ = scan_fwd(name, _sum_fn, nb=nb, nchunk=r // t, t=t, rows=rows, vecs=[], carries=[],
                         outs=[out_row((nb, r, c), out_dtype, fb=lambda b: b)], save=False)
    return out


def _flip(mask, pos):
    return tuple((1 - p) if m else p for m, p in zip(mask, pos))


ALL_PEERS = [(a, b, c) for a in (0, 1) for b in (0, 1) for c in (0, 1)][1:]
CHIP_PEERS = [(1, 0, 0), (0, 1, 0), (1, 1, 0)]
SIBLING = [(0, 0, 1)]


def comm_call(name, arrays, out_shapes, masks, src_fn, dst_fn, local_fn=None):
    na, npeer = len(arrays), len(masks)

    def body(*refs):
        ins, outs = refs[:na], refs[na:2 * na]
        send_sems, recv_sems, loc_sems = refs[2 * na:]
        me = (lax.axis_index("x"), lax.axis_index("y"), lax.axis_index("c"))
        local = []
        if local_fn is not None:
            for k in range(na):
                s, d = local_fn(k, ins[k], outs[k], me)
                cp = pltpu.make_async_copy(s, d, loc_sems.at[k])
                cp.start()
                local.append(cp)

        def remote(k, p, sender, to):
            peer = _flip(masks[p], me)
            return pltpu.make_async_remote_copy(
                src_ref=src_fn(k, ins[k], me, peer), dst_ref=dst_fn(k, outs[k], sender),
                send_sem=send_sems.at[k * npeer + p], recv_sem=recv_sems.at[k * npeer + p],
                device_id=to, device_id_type=MESH)

        sends = []
        for k in range(na):
            for p in range(npeer):
                cp = remote(k, p, me, _flip(masks[p], me))
                cp.start()
                sends.append(cp)
        for k in range(na):
            for p in range(npeer):
                peer = _flip(masks[p], me)
                remote(k, p, peer, peer).wait_recv()
        for cp in sends:
            cp.wait_send()
        for cp in local:
            cp.wait()

    hbm = pl.BlockSpec(memory_space=pl.ANY)
    out = pl.pallas_call(
        body, name=name, in_specs=[hbm] * na, out_specs=[hbm] * na,
        out_shape=[jax.ShapeDtypeStruct(s, a.dtype) for s, a in zip(out_shapes, arrays)],
        scratch_shapes=[pltpu.SemaphoreType.DMA((na * npeer,)), pltpu.SemaphoreType.DMA((na * npeer,)),
                        pltpu.SemaphoreType.DMA((na,))],
    )(*arrays)
    return list(out)


def _dev(pos):
    return 4 * pos[0] + 2 * pos[1] + pos[2]


def _chip(pos):
    return 2 * pos[0] + pos[1]


def allgather8(name, a):
    (out,) = comm_call(name, [a], [(8,) + a.shape], ALL_PEERS,
                       src_fn=lambda k, r, me, peer: r, dst_fn=lambda k, o, sender: o.at[_dev(sender)],
                       local_fn=lambda k, r, o, me: (r, o.at[_dev(me)]))
    return out


def gather_layer_from_chips(name, arrays):
    return comm_call(name, arrays, [(4,) + a.shape[1:] for a in arrays], CHIP_PEERS,
                     src_fn=lambda k, r, me, peer: r.at[me[2]], dst_fn=lambda k, o, sender: o.at[_chip(sender)],
                     local_fn=lambda k, r, o, me: (r.at[me[2]], o.at[_chip(me)]))


def swap_layers(name, arrays):
    return comm_call(name, arrays, [(2,) + a.shape for a in arrays], SIBLING,
                     src_fn=lambda k, r, me, peer: r, dst_fn=lambda k, o, sender: o.at[sender[2]],
                     local_fn=lambda k, r, o, me: (r, o.at[me[2]]))


def swap_other_layer(name, arrays):
    return comm_call(name, arrays, [a.shape for a in arrays], SIBLING,
                     src_fn=lambda k, r, me, peer: r.at[peer[2]], dst_fn=lambda k, o, sender: o.at[1],
                     local_fn=lambda k, r, o, me: (r.at[me[2]], o.at[0]))


def scatter_to_chips(name, arrays):
    return comm_call(name, arrays, [a.shape for a in arrays], CHIP_PEERS,
                     src_fn=lambda k, r, me, peer: r.at[_chip(peer)], dst_fn=lambda k, o, sender: o.at[_chip(sender)],
                     local_fn=lambda k, r, o, me: (r.at[_chip(me)], o.at[_chip(me)]))


def _pack(arrs):
    flat = jnp.concatenate([a.reshape(-1).astype(F32) for a in arrs])
    n = flat.shape[0]
    pad = (-n) % (_ROW_T * 128)
    return jnp.pad(flat, (0, pad)).reshape(-1, 128)


def _unpack(buf, shapes):
    flat = buf.reshape(-1)
    out, o = [], 0
    for s in shapes:
        n = int(np.prod(s))
        out.append(flat[o:o + n].reshape(s))
        o += n
    return out


_WEIGHTS = ["ada_w", "ada_b", "norm1_g", "w_in", "ssd_conv_w", "ssd_conv_b", "ssd_dt_bias", "ssd_a_log", "ssd_d", "ssd_norm_g",
            "pool_w", "pool_scale", "w_out", "norm2_g", "ffn_up", "ffn_conv_w", "ffn_conv_b", "ffn_down", "final_g"]
_BIG = ["w_in", "w_out", "ffn_up", "ffn_down"]
_SMALL = [n for n in _WEIGHTS if n not in _BIG and n != "ada_w"]
_COL_SHARDED_SMALL = {"ssd_conv_w": 256, "ffn_conv_w": 1408}


def _pad_lanes(v, n=128):
    return jnp.pad(v.astype(F32), (0, n - v.shape[0]))[None]


def _perm_cols(w):
    pad = jnp.zeros(w.shape[:-1] + (IN_WP - IN_W,), w.dtype)
    return jnp.concatenate([w[..., :1536], w[..., 1544:], w[..., 1536:1544], pad], axis=-1)


def _unperm_cols(g):
    return jnp.concatenate([g[..., :1536], g[..., 2560:2568], g[..., 1536:2560]], axis=-1)


def _cols_by_chip(g, ncol):
    return g.reshape(g.shape[0], 4, ncol).transpose(1, 0, 2)


def _layer_forward(i, x3, modv, wts, sp, cs, sn, rot):
    sh1, sc1, g1, sh2, sc2, g2 = modv
    s = x3.shape[1]
    h1 = norm_mod_forward(f"l{i}_norm1", x3, wts["norm1_g"], sc1, sh1)
    proj3 = mm(f"l{i}_proj", h1[0], wts["w_in"], "nn")[None]
    y_ssd, sv_ssd = ssd_forward(f"l{i}_ssd", proj3, sp)
    y_pool, sv_pool = pool_forward(f"l{i}_pool", proj3, wts["wbd"], wts["pool_scale"])
    y_att, res_att = attention_forward(f"l{i}", proj3, cs, sn, rot)
    mix = jnp.concatenate([y_ssd, y_pool, y_att], axis=-1)
    out, x1 = mm(f"l{i}_wout", mix[0], wts["w_out"], "nn", res=x3[0], gate=g1)
    x1 = x1[None]
    h2 = norm_mod_forward(f"l{i}_norm2", x1, wts["norm2_g"], sc2, sh2)
    up3 = mm(f"l{i}_up", h2[0], wts["ffn_up"], "nn")[None]
    act, sv_ffn = ffn_mid_forward(f"l{i}_ffn", up3, wts["ffn_conv_w"], wts["ffn_conv_b"])
    dn, x2 = mm(f"l{i}_down", act[0], wts["ffn_down"], "nn", res=x1[0], gate=g2)
    keep = dict(x=x3, h1=h1, proj3=proj3, sv_ssd=sv_ssd, sv_pool=sv_pool, res_att=res_att, mix=mix, out=out[None], x1=x1, h2=h2,
                up3=up3, act=act, sv_ffn=sv_ffn, dn=dn[None])
    return x2[None], keep


def _layer_backward(i, dx2, keep, modv, wts, sp, rot):
    sh1, sc1, g1, sh2, sc2, g2 = modv
    k = keep
    d_dn, d_g2 = gate_backward(f"l{i}_gate2_b", k["dn"], g2, dx2)
    d_act = mm(f"l{i}_down_bx", d_dn[0], wts["ffn_down"], "nt")
    g_down = mm(f"l{i}_down_bw", k["act"][0], d_dn[0], "tn")
    (dg_, du_), dv_ffn = ffn_mid_backward(f"l{i}_ffn_b", k["up3"], wts["ffn_conv_w"], wts["ffn_conv_b"], k["sv_ffn"], d_act[None])
    d_up = jnp.concatenate([dg_[0], du_[0]], axis=-1)
    d_h2 = mm(f"l{i}_up_bx", d_up, wts["ffn_up"], "nt")
    g_up = mm(f"l{i}_up_bw", k["h2"][0], d_up, "tn")
    dx1, (d_n2, d_sc2, d_sh2) = norm_mod_backward(f"l{i}_norm2_b", k["x1"], wts["norm2_g"], sc2, sh2, d_h2[None], dx2)
    d_out, d_g1 = gate_backward(f"l{i}_gate1_b", k["out"], g1, dx1)
    d_mix = mm(f"l{i}_wout_bx", d_out[0], wts["w_out"], "nt")[None]
    g_wout = mm(f"l{i}_wout_bw", k["mix"][0], d_out[0], "tn")
    (dz, dxs, dbm, dcm, ddt), dv_ssd = ssd_backward(f"l{i}_ssd_b", k["proj3"], sp, k["sv_ssd"], d_mix)
    (du_pool,), (d_wbd, d_pscale) = pool_backward(f"l{i}_pool_b", k["proj3"], wts["wbd"], wts["pool_scale"], k["sv_pool"], d_mix)
    dq, dk, dv = attention_backward(f"l{i}", k["res_att"], rot, d_mix)
    d_proj = jnp.concatenate([dz[0], dxs[0], dbm[0], dcm[0], du_pool[0], dq, dk, dv, ddt[0] + ddt[1]], axis=-1)
    d_h1 = mm(f"l{i}_proj_bx", d_proj, wts["w_in"], "nt")
    g_win = mm(f"l{i}_proj_bw", k["h1"][0], d_proj, "tn")
    dx, (d_n1, d_sc1, d_sh1) = norm_mod_backward(f"l{i}_norm1_b", k["x"], wts["norm1_g"], sc1, sh1, d_h1[None], dx1)
    dcwx, dcbx, dcwb, dcbb, dcwc, dcbc, ddtb, dalog, ddsk, dng = dv_ssd
    small = dict(
        norm1_g=d_n1[0], norm2_g=d_n2[0],
        ssd_conv_w=jnp.concatenate([dcwx[:, :512], dcwb[:, 512:768], dcwc[:, 768:]], axis=1),
        ssd_conv_b=jnp.concatenate([dcbx[0, :512], dcbb[0, 512:768], dcbc[0, 768:]]),
        ssd_dt_bias=ddtb[0, :8], ssd_a_log=dalog[0, :8], ssd_d=ddsk[0, :8], ssd_norm_g=dng[0],
        pool_w=jnp.stack([d_wbd[64 * g:64 * g + 64, 64 * g:64 * g + 64] for g in range(4)]), pool_scale=d_pscale[0],
        ffn_conv_w=jnp.concatenate([dv_ffn[0][:, :FFN_DIM], dv_ffn[2][:, FFN_DIM:]], axis=1),
        ffn_conv_b=jnp.concatenate([dv_ffn[1][0, :FFN_DIM], dv_ffn[3][0, FFN_DIM:]]),
    )
    dmod = jnp.concatenate([d_sh1[0], d_sc1[0], d_g1[0], d_sh2[0], d_sc2[0], d_g2[0]])
    return dx, dict(w_in=g_win, w_out=g_wout, ffn_up=g_up, ffn_down=g_down), small, dmod


def kernel(x, c, positions, ada_w, ada_b, norm1_g, w_in, ssd_conv_w, ssd_conv_b, ssd_dt_bias, ssd_a_log, ssd_d, ssd_norm_g, pool_w, pool_scale, w_out, norm2_g, ffn_up, ffn_conv_w, ffn_conv_b, ffn_down, final_g, loss_target, m_ada_w, m_ada_b, m_norm1_g, m_w_in, m_ssd_conv_w, m_ssd_conv_b, m_ssd_dt_bias, m_ssd_a_log, m_ssd_d, m_ssd_norm_g, m_pool_w, m_pool_scale, m_w_out, m_norm2_g, m_ffn_up, m_ffn_conv_w, m_ffn_conv_b, m_ffn_down, m_final_g, v_ada_w, v_ada_b, v_norm1_g, v_w_in, v_ssd_conv_w, v_ssd_conv_b, v_ssd_dt_bias, v_ssd_a_log, v_ssd_d, v_ssd_norm_g, v_pool_w, v_pool_scale, v_w_out, v_norm2_g, v_ffn_up, v_ffn_conv_w, v_ffn_conv_b, v_ffn_down, v_final_g):
    args = dict(locals())
    w = {n: args[n] for n in _WEIGHTS}
    m = {n: args["m_" + n] for n in _WEIGHTS}
    v = {n: args["v_" + n] for n in _WEIGHTS}
    d = D_MODEL
    me = (lax.axis_index("x"), lax.axis_index("y"), lax.axis_index("c"))
    chip, dev = _chip(me), _dev(me)

    shapes0 = [c.shape, ssd_conv_w.shape, ffn_conv_w.shape]
    g0 = allgather8("gather_c_conv", _pack([c, ssd_conv_w, ffn_conv_w]))
    c16 = jnp.pad(g0[:, :d // 128, :].reshape(8, d), ((0, 8), (0, 0)))
    by_chip = [_unpack(g0[2 * j], shapes0) for j in range(4)]
    conv_w_full = jnp.concatenate([p[1] for p in by_chip], axis=-1)
    fconv_w_full = jnp.concatenate([p[2] for p in by_chip], axis=-1)

    modp = ada_forward("ada_fwd", c16, ada_w)[:, :8]
    g1 = allgather8("gather_mod", _pack([modp]))
    modfull = jnp.concatenate([_unpack(g1[2 * j], [modp.shape])[0] for j in range(4)], axis=-1)
    mod = lax.dynamic_index_in_dim(modfull, dev, axis=1, keepdims=False) + ada_b
    modv = [[mod[i, q * d:(q + 1) * d][None] for q in range(6)] for i in range(DEPTH)]

    got = gather_layer_from_chips("gather_w", [w[n].astype(BF16) for n in _BIG])
    both = swap_layers("swap_w", got)
    full = dict(
        w_in=_perm_cols(both[0].transpose(0, 2, 1, 3).reshape(DEPTH, d, IN_W)),
        w_out=both[1].reshape(DEPTH, d, d),
        ffn_up=both[2].transpose(0, 2, 1, 3).reshape(DEPTH, d, 2 * FFN_DIM),
        ffn_down=both[3].reshape(DEPTH, FFN_DIM, d),
    )

    cs, sn, rot = rope_tables(positions[0])
    e_mat = jnp.asarray((np.arange(128)[:, None] == (np.arange(SSD_INNER)[None, :] // 64)).astype(np.float32))
    eye4 = jnp.eye(4, dtype=F32)
    wts, sps = [], []
    for i in range(DEPTH):
        wts.append(dict(
            w_in=full["w_in"][i], w_out=full["w_out"][i], ffn_up=full["ffn_up"][i], ffn_down=full["ffn_down"][i],
            norm1_g=norm1_g[i][None], norm2_g=norm2_g[i][None], pool_scale=pool_scale[i][None],
            wbd=(eye4[:, None, :, None] * pool_w[i][:, :, None, :]).reshape(POOL_W, POOL_W),
            ffn_conv_w=fconv_w_full[i], ffn_conv_b=ffn_conv_b[i][None]))
        sps.append(dict(cw=conv_w_full[i], cb=ssd_conv_b[i][None], dtb=_pad_lanes(ssd_dt_bias[i]), alog=_pad_lanes(ssd_a_log[i]),
                        dsk=_pad_lanes(ssd_d[i]), ng=ssd_norm_g[i][None], e=e_mat))

    xc, keeps = x, []
    for i in range(DEPTH):
        xc, keep = _layer_forward(i, xc, modv[i], wts[i], sps[i], cs, sn, rot)
        keeps.append(keep)
    lossblk, dx, d_final = final_loss("final_loss", xc, loss_target, final_g[None])
    loss = lax.psum(lossblk[0, 0], ("x", "y", "c"))

    big_g, small_g, dmods = [None] * DEPTH, [None] * DEPTH, [None] * DEPTH
    for i in reversed(range(DEPTH)):
        dx, big_g[i], small_g[i], dmods[i] = _layer_backward(i, dx, keeps[i], modv[i], wts[i], sps[i], rot)

    by_dest = [
        jnp.stack([_cols_by_chip(_unperm_cols(big_g[i]["w_in"]), IN_W // 4) for i in range(DEPTH)]),
        jnp.stack([big_g[i]["w_out"].reshape(4, d // 4, d) for i in range(DEPTH)]),
        jnp.stack([_cols_by_chip(big_g[i]["ffn_up"], 2 * FFN_DIM // 4) for i in range(DEPTH)]),
        jnp.stack([big_g[i]["ffn_down"].reshape(4, FFN_DIM // 4, d) for i in range(DEPTH)]),
    ]
    pair = swap_other_layer("swap_g", by_dest)
    core_sum = [sum_slots(f"sum_cores_{n}", p.reshape((8,) + p.shape[2:]), 2) for n, p in zip(_BIG, pair)]
    from_chips = scatter_to_chips("scatter_g", core_sum)
    chip_sum = [sum_slots(f"sum_chips_{n}", q, 4)[0] for n, q in zip(_BIG, from_chips)]
    reduced = swap_layers("swap_r", chip_sum)
    grads = dict(zip(_BIG, reduced))

    part = dict(ada_b=jnp.stack(dmods), final_g=d_final[0])
    for n in _SMALL:
        if n not in part:
            part[n] = jnp.stack([small_g[i][n] for i in range(DEPTH)])
    full_shapes = [part[n].shape for n in _SMALL]
    gs = allgather8("gather_small", _pack([part[n] for n in _SMALL]))
    tot = _unpack(sum_slots("sum_small", gs, 8)[0], full_shapes)
    small_tot = dict(zip(_SMALL, tot))
    dmod_all = gs[:, :DEPTH * 6 * d // 128, :].reshape(8, DEPTH, 6 * d)
    for n, ncol in _COL_SHARDED_SMALL.items():
        small_tot[n] = lax.dynamic_slice_in_dim(small_tot[n], chip * ncol, ncol, axis=2)
    grads.update(small_tot)

    ncol = ada_w.shape[2]
    dm = lax.dynamic_slice_in_dim(dmod_all, chip * ncol, ncol, axis=2).transpose(1, 0, 2)
    upd = {}
    g_ada, *upd["ada_w"] = ada_backward("ada_bwd", c16, jnp.pad(dm, ((0, 0), (0, 8), (0, 0))), ada_w, m["ada_w"], v["ada_w"])
    grads["ada_w"] = g_ada

    for n in _BIG:
        upd[n] = adamw(f"adam_{n}", w[n], grads[n], m[n], v[n])
    shapes_s = [w[n].shape for n in _SMALL]
    packed = [_pack([src[n] for n in _SMALL]) for src in (w, grads, m, v)]
    outs_s = [_unpack(o, shapes_s) for o in adamw("adam_small", *packed)]
    for q, n in enumerate(_SMALL):
        upd[n] = [outs_s[0][q], outs_s[1][q], outs_s[2][q]]

    return (loss, dx, *[grads[n] for n in _WEIGHTS], *[upd[n][0] for n in _WEIGHTS], *[upd[n][1] for n in _WEIGHTS],
            *[upd[n][2] for n in _WEIGHTS])
```

```python
import functools
import math

import numpy as np
import jax
import jax.numpy as jnp
from jax import lax
from jax.experimental import pallas as pl
from jax.experimental.pallas import tpu as pltpu

F32 = jnp.float32
BF16 = jnp.bfloat16
HI = lax.Precision.HIGHEST
MESH = pl.DeviceIdType.MESH

D_MODEL = 1024
SEQ = 4096
DEPTH = 2
SSD_INNER = 512
SSD_HEADS = 8
SSD_STATE = 128
POOL_W = 256
POOL_WINDOWS = (2, 4, 8, 16)
ATT_W = 256
ATT_HEADS = 4
ATT_HEAD_DIM = 64
ATT_PATTERNS = ((128, 1), (512, 4), (2048, 16))
ATT_BLOCK = 128
ROT_DIM = 16
ROPE_THETA = 500000.0
IN_W = 2568
IN_WP = 2816
IN_USED = 2688
FFN_DIM = 2816
NORM_EPS = 1e-6
ADAM_LR, ADAM_B1, ADAM_B2, ADAM_EPS, ADAM_WD, ADAM_STEP = 0.001, 0.9, 0.999, 1e-08, 0.01, 10

VMEM_LIMIT_BYTES = 56 * 1024 * 1024
NEG = -1e30


def _mxu(a, b, mode):
    dims = {"nn": ((1,), (0,)), "nt": ((1,), (1,)), "tn": ((0,), (0,))}[mode]
    return lax.dot_general(a.astype(BF16), b.astype(BF16), (dims, ((), ())), preferred_element_type=F32)


@functools.partial(jax.custom_vjp, nondiff_argnums=(2,))
def _bdot(a, b, mode):
    return _mxu(a, b, mode)


def _bdot_fwd(a, b, mode):
    return _mxu(a, b, mode), (a, b)


def _bdot_bwd(mode, res, g):
    a, b = res
    if mode == "nn":
        return _mxu(g, b, "nt"), _mxu(a, g, "tn")
    if mode == "nt":
        return _mxu(g, b, "nn"), _mxu(g, a, "tn")
    return _mxu(b, g, "nt"), _mxu(a, g, "nn")


_bdot.defvjp(_bdot_fwd, _bdot_bwd)


def _fxu(a, b, mode):
    dims = {"nn": ((1,), (0,)), "nt": ((1,), (1,)), "tn": ((0,), (0,))}[mode]
    return lax.dot_general(a, b, (dims, ((), ())), precision=HI, preferred_element_type=F32)


@functools.partial(jax.custom_vjp, nondiff_argnums=(2,))
def _fdot(a, b, mode):
    return _fxu(a, b, mode)


def _fdot_fwd(a, b, mode):
    return _fxu(a, b, mode), (a, b)


def _fdot_bwd(mode, res, g):
    a, b = res
    if mode == "nn":
        return _fxu(g, b, "nt"), _fxu(a, g, "tn")
    if mode == "nt":
        return _fxu(g, b, "nn"), _fxu(g, a, "tn")
    return _fxu(b, g, "nt"), _fxu(a, g, "nn")


_fdot.defvjp(_fdot_fwd, _fdot_bwd)


def _iota(shape, dim):
    return lax.broadcasted_iota(jnp.int32, shape, dim)


def _make_shift(h):
    @functools.partial(jax.custom_vjp, nondiff_argnums=(2,))
    def shift(halo, cur, k):
        if k == 0:
            return cur
        full = jnp.concatenate([halo, cur], axis=0)
        return pltpu.roll(full, k, 0)[h:]

    def fwd(halo, cur, k):
        return shift(halo, cur, k), None

    def bwd(k, _, g):
        t, w = g.shape
        if k == 0:
            return jnp.zeros((h, w), F32), g
        d_cur = jnp.where(_iota((t, w), 0) < t - k, pltpu.roll(g, t - k, 0), 0.0)
        top = g[:h]
        d_halo = jnp.where(_iota((h, w), 0) >= h - k, pltpu.roll(top, h - k, 0) if k < h else top, 0.0)
        return d_halo, d_cur

    shift.defvjp(fwd, bwd)
    return shift


_shift8 = _make_shift(8)
_shift16 = _make_shift(16)


def _make_tail(h):
    @jax.custom_vjp
    def tail(x):
        return x[x.shape[0] - h:]

    def fwd(x):
        return tail(x), x.shape[0]

    def bwd(t, g):
        return (jnp.concatenate([jnp.zeros((t - h, g.shape[1]), F32), g], axis=0),)

    tail.defvjp(fwd, bwd)
    return tail


_tail8 = _make_tail(8)
_tail16 = _make_tail(16)


def _rowk(w, k):
    return jnp.sum(jnp.where(_iota(w.shape, 0) == k, w, 0.0), axis=0, keepdims=True)


def _silu(x):
    return x * (1.0 / (1.0 + jnp.exp(-x)))


def _softplus(x):
    return jnp.maximum(x, 0.0) + jnp.log(1.0 + jnp.exp(-jnp.abs(x)))


def _tile(dim, target, unit=128):
    if dim <= target:
        return dim
    best = None
    for t in range(unit, target + 1, unit):
        if dim % t == 0:
            best = t
    assert best is not None, (dim, target)
    return best


class Row:
    def __init__(self, arr, w=None, fb=None, fc=None, diff=True, slot=False, dcols=None, dfc=None):
        self.arr = arr
        self.w = arr.shape[2] if w is None else w
        self.fb = (lambda b: 0) if fb is None else fb
        self.fc = (lambda b: 0) if fc is None else fc
        self.diff = diff
        self.slot = slot
        self.dcols = dcols
        self.dfc = dfc


class Vec:
    def __init__(self, arr, w=None, fc=None, diff=True):
        self.arr = arr
        self.w = arr.shape[1] if w is None else w
        self.fc = fc
        self.diff = diff


def _row_spec(r, t, nchunk, reverse):
    if reverse:
        return pl.BlockSpec((1, t, r.w), lambda b, i, r=r: (r.fb(b), nchunk - 1 - i, r.fc(b)))
    return pl.BlockSpec((1, t, r.w), lambda b, i, r=r: (r.fb(b), i, r.fc(b)))


def _vec_spec(v):
    if v.fc is None:
        return pl.BlockSpec(v.arr.shape, lambda b, i: (0, 0))
    return pl.BlockSpec((v.arr.shape[0], v.w), lambda b, i, v=v: (0, v.fc(b)))


def _cparams():
    return pltpu.CompilerParams(dimension_semantics=("arbitrary", "arbitrary"), vmem_limit_bytes=VMEM_LIMIT_BYTES)


def scan_fwd(name, fn, *, nb, nchunk, t, rows, vecs, carries, outs, save):
    nr, nv, nc, no = len(rows), len(vecs), len(carries), len(outs)

    def body(*refs):
        row_refs, vec_refs = refs[:nr], refs[nr:nr + nv]
        out_refs = refs[nr + nv:nr + nv + no]
        save_refs = refs[nr + nv + no:nr + nv + no + (nc if save else 0)]
        car = refs[len(refs) - nc:] if nc else ()
        b, i = pl.program_id(0), pl.program_id(1)
        if nc:
            @pl.when(i == 0)
            def _():
                for c_ref in car:
                    c_ref[...] = jnp.zeros(c_ref.shape, F32)
        cin = [c_ref[...] for c_ref in car]
        if save:
            for s_ref, cv in zip(save_refs, cin):
                s_ref[0, 0] = cv
        new_c, o = fn(i, b, cin, [r[0] for r in row_refs], [v[...] for v in vec_refs])
        for c_ref, cv in zip(car, new_c):
            c_ref[...] = cv
        for o_ref, ov in zip(out_refs, o):
            o_ref[0] = ov.astype(o_ref.dtype)

    out_shape = [o.arr for o in outs]
    out_specs = [_row_spec(o, t, nchunk, False) for o in outs]
    if save:
        for cs in carries:
            out_shape.append(jax.ShapeDtypeStruct((nb, nchunk) + tuple(cs), F32))
            out_specs.append(pl.BlockSpec((1, 1) + tuple(cs), lambda b, i: (b, i, 0, 0)))
    res = pl.pallas_call(
        body, name=name, grid=(nb, nchunk),
        in_specs=[_row_spec(r, t, nchunk, False) for r in rows] + [_vec_spec(v) for v in vecs],
        out_specs=out_specs, out_shape=out_shape,
        scratch_shapes=[pltpu.VMEM(tuple(cs), F32) for cs in carries],
        compiler_params=_cparams(),
    )(*[r.arr for r in rows], *[v.arr for v in vecs])
    return list(res[:no]), list(res[no:])


def scan_bwd(name, fn, *, nb, nchunk, t, rows, vecs, carries, saved, douts, adds=None):
    adds = adds or {}
    nr, nv, nc, no = len(rows), len(vecs), len(carries), len(douts)
    dri = [k for k, r in enumerate(rows) if r.diff]
    dvi = [k for k, v in enumerate(vecs) if v.diff]
    add_keys = sorted(adds)
    na = len(add_keys)

    def body(*refs):
        p = 0
        row_refs = refs[p:p + nr]; p += nr
        vec_refs = refs[p:p + nv]; p += nv
        save_refs = refs[p:p + nc]; p += nc
        dout_refs = refs[p:p + no]; p += no
        add_refs = refs[p:p + na]; p += na
        drow_refs = refs[p:p + len(dri)]; p += len(dri)
        dvec_refs = refs[p:p + len(dvi)]; p += len(dvi)
        dcar = refs[p:]
        b, ir = pl.program_id(0), pl.program_id(1)
        ci = nchunk - 1 - ir
        if nc:
            @pl.when(ir == 0)
            def _():
                for c_ref in dcar:
                    c_ref[...] = jnp.zeros(c_ref.shape, F32)
        rows_v = [r[0] for r in row_refs]
        vecs_v = [v[...] for v in vec_refs]
        cin = [s[0, 0] for s in save_refs]
        dc = [c_ref[...] for c_ref in dcar]
        dout_v = [d[0].astype(F32) for d in dout_refs]

        def f(cs, dr, dv):
            rr, vv = list(rows_v), list(vecs_v)
            for k, idx in enumerate(dri):
                rr[idx] = dr[k]
            for k, idx in enumerate(dvi):
                vv[idx] = dv[k]
            return fn(ci, b, cs, rr, vv)

        _, vjp = jax.vjp(f, cin, [rows_v[k].astype(F32) for k in dri], [vecs_v[k].astype(F32) for k in dvi])
        dcin, drows, dvecs = vjp((dc, dout_v))
        for c_ref, cv in zip(dcar, dcin):
            c_ref[...] = cv
        for k, (o_ref, ov) in enumerate(zip(drow_refs, drows)):
            if dri[k] in adds:
                ov = ov + add_refs[add_keys.index(dri[k])][0].astype(F32)
            o_ref[0] = ov.astype(o_ref.dtype)
        for k, (o_ref, ov) in enumerate(zip(dvec_refs, dvecs)):
            first = (ir == 0) if vecs[dvi[k]].fc is not None else jnp.logical_and(ir == 0, b == 0)

            @pl.when(first)
            def _(o_ref=o_ref, ov=ov):
                o_ref[...] = ov

            @pl.when(jnp.logical_not(first))
            def _(o_ref=o_ref, ov=ov):
                o_ref[...] += ov

    in_specs = ([_row_spec(r, t, nchunk, True) for r in rows] + [_vec_spec(v) for v in vecs]
                + [pl.BlockSpec((1, 1) + tuple(cs), lambda b, i: (b, nchunk - 1 - i, 0, 0)) for cs in carries]
                + [_row_spec(d, t, nchunk, True) for d in douts]
                + [_row_spec(adds[k], t, nchunk, True) for k in add_keys])
    out_shape, out_specs = [], []
    for k in dri:
        r = rows[k]
        if r.slot:
            out_shape.append(jax.ShapeDtypeStruct((nb, r.arr.shape[1], r.w), F32))
            out_specs.append(pl.BlockSpec((1, t, r.w), lambda b, i: (b, nchunk - 1 - i, 0)))
        elif r.dcols is not None:
            out_shape.append(jax.ShapeDtypeStruct((r.arr.shape[0], r.arr.shape[1], r.dcols), F32))
            out_specs.append(pl.BlockSpec((1, t, r.w), lambda b, i, r=r: (r.fb(b), nchunk - 1 - i, r.dfc(b))))
        else:
            out_shape.append(jax.ShapeDtypeStruct(r.arr.shape, F32))
            out_specs.append(_row_spec(r, t, nchunk, True))
    for k in dvi:
        out_shape.append(jax.ShapeDtypeStruct(vecs[k].arr.shape, F32))
        out_specs.append(_vec_spec(vecs[k]))
    res = pl.pallas_call(
        body, name=name, grid=(nb, nchunk), in_specs=in_specs, out_specs=out_specs, out_shape=out_shape,
        scratch_shapes=[pltpu.VMEM(tuple(cs), F32) for cs in carries],
        compiler_params=_cparams(),
    )(*[r.arr for r in rows], *[v.arr for v in vecs], *saved, *[d.arr for d in douts], *[adds[k].arr for k in add_keys])
    return list(res[:len(dri)]), list(res[len(dri):])


def out_row(shape, dtype=F32, w=None, fb=None, fc=None):
    return Row(jax.ShapeDtypeStruct(shape, dtype), w, fb, fc)


def _conv(shift, halo, cur, w, bias, taps):
    y = bias
    for k in range(taps):
        y = y + _rowk(w, k) * shift(halo, cur, taps - 1 - k)
    return y


def _ssd_fn(ci, b, carries, rows, vecs):
    cx, cb_, cc, ht = carries
    z, xr, br, cr, dtr = rows
    cwx, cbx, cwb, cbb, cwc, cbc, dtb, alog, dsk, ng, e = vecs
    t = z.shape[0]
    xs = _silu(_conv(_shift8, cx, xr, cwx, cbx, 4))
    bm = _silu(_conv(_shift8, cb_, br, cwb, cbb, 4))
    cm = _silu(_conv(_shift8, cc, cr, cwc, cbc, 4))
    dt = _softplus(dtr + dtb)
    da = dt * (-jnp.exp(alog))
    r, c = _iota((t, t), 0), _iota((t, t), 1)
    causal = r >= c
    acol = _fdot(causal.astype(F32), da, "nn")
    arow = _fdot(da, (r <= c).astype(F32), "tn")
    a = _fdot(acol, e, "nn")
    dtx = _fdot(dt, e, "nn")
    atot = jnp.sum(jnp.where(_iota(a.shape, 0) == t - 1, a, 0.0), axis=0, keepdims=True)
    x = xs * dtx
    cbm = _bdot(cm, bm, "nt")
    lane, sub = _iota(acol.shape, 1), _iota(arow.shape, 0)
    colh = _iota(x.shape, 1) // 64
    ydiag = jnp.zeros(x.shape, F32)
    for j in range(4):
        h = 4 * b + j
        ac = jnp.sum(jnp.where(lane == h, acol, 0.0), axis=1, keepdims=True)
        ar = jnp.sum(jnp.where(sub == h, arow, 0.0), axis=0, keepdims=True)
        lmat = jnp.exp(jnp.where(causal, ac - ar, NEG))
        ydiag = ydiag + _bdot(cbm * lmat, jnp.where(colh == j, x, 0.0), "nn")
    yoff = _bdot(cm, ht, "nn") * jnp.exp(a)
    ht_new = ht * jnp.exp(atot) + _bdot(bm, x * jnp.exp(atot - a), "tn")
    dx = jnp.sum(_fdot(jnp.broadcast_to(dsk, (8, dsk.shape[1])), e, "nn"), axis=0, keepdims=True) * 0.125
    y = ydiag + yoff + dx * xs
    yz = y * _silu(z)
    yn = yz * lax.rsqrt(jnp.mean(yz * yz, axis=-1, keepdims=True) + NORM_EPS) * ng
    return [_tail8(xr), _tail8(br), _tail8(cr), ht_new], [yn]


_SSD_T = 256
_SSD_CARRIES = [(8, 256), (8, 128), (8, 128), (128, 256)]


def _ssd_io(proj3, p):
    own = lambda b: b
    rows = [Row(proj3, 256, fc=own, dcols=512, dfc=own), Row(proj3, 256, fc=lambda b: 2 + b, dcols=512, dfc=own),
            Row(proj3, 128, fc=lambda b: 8 + b, dcols=256, dfc=own), Row(proj3, 128, fc=lambda b: 10 + b, dcols=256, dfc=own),
            Row(proj3, 128, fc=lambda b: 20, slot=True)]
    vecs = [Vec(p["cw"], 256, lambda b: b), Vec(p["cb"], 256, lambda b: b),
            Vec(p["cw"], 128, lambda b: 4 + b), Vec(p["cb"], 128, lambda b: 4 + b),
            Vec(p["cw"], 128, lambda b: 6 + b), Vec(p["cb"], 128, lambda b: 6 + b),
            Vec(p["dtb"]), Vec(p["alog"]), Vec(p["dsk"]), Vec(p["ng"], 256, lambda b: b),
            Vec(p["e"], 256, lambda b: b, diff=False)]
    return rows, vecs


def ssd_forward(name, proj3, p):
    rows, vecs = _ssd_io(proj3, p)
    s = proj3.shape[1]
    (y,), saved = scan_fwd(name, _ssd_fn, nb=2, nchunk=s // _SSD_T, t=_SSD_T, rows=rows, vecs=vecs,
                           carries=_SSD_CARRIES, outs=[out_row((1, s, SSD_INNER), F32, 256, fc=lambda b: b)], save=True)
    return y, saved


def ssd_backward(name, proj3, p, saved, dmix3):
    rows, vecs = _ssd_io(proj3, p)
    s = proj3.shape[1]
    drows, dvecs = scan_bwd(name, _ssd_fn, nb=2, nchunk=s // _SSD_T, t=_SSD_T, rows=rows, vecs=vecs,
                            carries=_SSD_CARRIES, saved=saved, douts=[Row(dmix3, 256, fc=lambda b: b)])
    return drows, dvecs


def _pool_fn(ci, b, carries, rows, vecs):
    (cu,) = carries
    (u,) = rows
    wbd, scale = vecs
    t = u.shape[0]
    pos = ci * t + _iota(u.shape, 0)
    grp = _iota(u.shape, 1) // 64
    acc, pooled, k = u, jnp.zeros(u.shape, F32), 1
    for gi, w in enumerate(POOL_WINDOWS):
        while k < w:
            acc = acc + _shift16(cu, u, k)
            k += 1
        pooled = jnp.where(grp == gi, acc / jnp.minimum(pos + 1, w).astype(F32), pooled)
    y = _bdot(pooled - u, wbd, "nn") * scale
    return [_tail16(u)], [y]


_POOL_T = 256


def _pool_io(proj3, wbd, scale):
    return [Row(proj3, 256, fc=lambda b: 6, dcols=256, dfc=lambda b: 0)], [Vec(wbd), Vec(scale)]


def pool_forward(name, proj3, wbd, scale):
    rows, vecs = _pool_io(proj3, wbd, scale)
    s = proj3.shape[1]
    (y,), saved = scan_fwd(name, _pool_fn, nb=1, nchunk=s // _POOL_T, t=_POOL_T, rows=rows, vecs=vecs,
                           carries=[(16, 256)], outs=[out_row((1, s, POOL_W))], save=True)
    return y, saved


def pool_backward(name, proj3, wbd, scale, saved, dmix3):
    rows, vecs = _pool_io(proj3, wbd, scale)
    s = proj3.shape[1]
    return scan_bwd(name, _pool_fn, nb=1, nchunk=s // _POOL_T, t=_POOL_T, rows=rows, vecs=vecs,
                    carries=[(16, 256)], saved=saved, douts=[Row(dmix3, 256, fc=lambda b: 2)])


def _attn_fn(ci, b, carries, rows, vecs):
    kp, vp = carries
    q, k, v, cs, sn = rows
    (rot,) = vecs
    qr = q * cs + _fdot(q, rot, "nn") * sn
    kr = k * cs + _fdot(k, rot, "nn") * sn
    scale = ATT_HEAD_DIM ** -0.5
    n = q.shape[0]
    r, c = _iota((n, n), 0), _iota((n, n), 1)
    prev_ok, cur_ok = jnp.logical_and(c >= r, ci > 0), r >= c
    head = _iota(q.shape, 1) // ATT_HEAD_DIM
    o, lse = jnp.zeros(q.shape, F32), jnp.zeros(q.shape, F32)
    for h in range(ATT_HEADS):
        mine = head == h
        qh = jnp.where(mine, qr, 0.0)
        sp = jnp.where(prev_ok, _bdot(qh, kp, "nt") * scale, NEG)
        sc = jnp.where(cur_ok, _bdot(qh, kr, "nt") * scale, NEG)
        m = lax.stop_gradient(jnp.maximum(jnp.max(sp, axis=1, keepdims=True), jnp.max(sc, axis=1, keepdims=True)))
        pp, pc = jnp.exp(sp - m), jnp.exp(sc - m)
        l = jnp.sum(pp, axis=1, keepdims=True) + jnp.sum(pc, axis=1, keepdims=True)
        o = jnp.where(mine, (_bdot(pp, vp, "nn") + _bdot(pc, v, "nn")) / l, o)
        lse = jnp.where(mine, m + jnp.log(l), lse)
    return [kr, v], [o, lse]


_ATT_CARRIES = [(ATT_BLOCK, ATT_W), (ATT_BLOCK, ATT_W)]
_Q_BLOCK = 7


def _attn_io(pv, cv, sv, rot, d):
    per_row = IN_WP // ATT_W
    own = lambda b: b
    rows = [Row(pv, ATT_W, fc=(lambda b, j=j: b * per_row + _Q_BLOCK + j), dcols=d * ATT_W, dfc=own) for j in range(3)]
    rows += [Row(cv, ATT_W, fc=own, diff=False), Row(sv, ATT_W, fc=own, diff=False)]
    return rows, [Vec(rot, diff=False)]


def attn_forward(name, pv, cv, sv, rot, d):
    rows, vecs = _attn_io(pv, cv, sv, rot, d)
    l = pv.shape[1]
    own = lambda b: b
    outs = [out_row((1, l, d * ATT_W), F32, ATT_W, fc=own) for _ in range(2)]
    (o, lse), saved = scan_fwd(name, _attn_fn, nb=d, nchunk=l // ATT_BLOCK, t=ATT_BLOCK, rows=rows, vecs=vecs,
                               carries=_ATT_CARRIES, outs=outs, save=True)
    return o, lse, saved


def attn_backward(name, pv, cv, sv, rot, d, saved, do, dlse):
    rows, vecs = _attn_io(pv, cv, sv, rot, d)
    l = pv.shape[1]
    own = lambda b: b
    drows, _ = scan_bwd(name, _attn_fn, nb=d, nchunk=l // ATT_BLOCK, t=ATT_BLOCK, rows=rows, vecs=vecs,
                        carries=_ATT_CARRIES, saved=saved, douts=[Row(do, ATT_W, fc=own), Row(dlse, ATT_W, fc=own)])
    return drows


def _merge_fn(ci, b, carries, rows, vecs):
    o1, o2, o3, l1, l2, l3 = rows
    mx = lax.stop_gradient(jnp.maximum(l1, jnp.maximum(l2, l3)))
    e1, e2, e3 = jnp.exp(l1 - mx), jnp.exp(l2 - mx), jnp.exp(l3 - mx)
    return [], [(e1 * o1 + e2 * o2 + e3 * o3) / (e1 + e2 + e3)]


_ROW_T = 256


def merge_forward(name, os_, ls_):
    s = os_[0].shape[1]
    (y,), _ = scan_fwd(name, _merge_fn, nb=1, nchunk=s // _ROW_T, t=_ROW_T, rows=[Row(a) for a in (*os_, *ls_)], vecs=[],
                       carries=[], outs=[out_row((1, s, ATT_W))], save=False)
    return y


def merge_backward(name, os_, ls_, dmix3):
    s = os_[0].shape[1]
    drows, _ = scan_bwd(name, _merge_fn, nb=1, nchunk=s // _ROW_T, t=_ROW_T, rows=[Row(a) for a in (*os_, *ls_)], vecs=[],
                        carries=[], saved=[], douts=[Row(dmix3, 256, fc=lambda b: 3)])
    return drows


def _norm_mod_fn(ci, b, carries, rows, vecs):
    (x,) = rows
    g, sc, sh = vecs
    xn = x * lax.rsqrt(jnp.mean(x * x, axis=-1, keepdims=True) + NORM_EPS)
    return [], [xn * g * (1.0 + sc) + sh]


def norm_mod_forward(name, x3, g, sc, sh):
    s = x3.shape[1]
    (h,), _ = scan_fwd(name, _norm_mod_fn, nb=1, nchunk=s // _ROW_T, t=_ROW_T, rows=[Row(x3)], vecs=[Vec(g), Vec(sc), Vec(sh)],
                       carries=[], outs=[out_row(x3.shape, BF16)], save=False)
    return h


def norm_mod_backward(name, x3, g, sc, sh, dh3, add3):
    s = x3.shape[1]
    (dx,), dv = scan_bwd(name, _norm_mod_fn, nb=1, nchunk=s // _ROW_T, t=_ROW_T, rows=[Row(x3)], vecs=[Vec(g), Vec(sc), Vec(sh)],
                         carries=[], saved=[], douts=[Row(dh3)], adds={0: Row(add3)})
    return dx, dv


def _gate_fn(ci, b, carries, rows, vecs):
    return [], [rows[0] * vecs[0]]


def gate_backward(name, o3, g, dx3):
    s = o3.shape[1]
    (do,), (dg,) = scan_bwd(name, _gate_fn, nb=1, nchunk=s // _ROW_T, t=_ROW_T, rows=[Row(o3)], vecs=[Vec(g)],
                            carries=[], saved=[], douts=[Row(dx3)])
    return do, dg


def _ffn_fn(ci, b, carries, rows, vecs):
    cg, cu = carries
    ug, uu = rows
    wg, bg, wu, bu = vecs
    hg = _conv(_shift8, cg, ug, wg, bg, 3)
    hu = _conv(_shift8, cu, uu, wu, bu, 3)
    return [_tail8(ug), _tail8(uu)], [_silu(hg) * hu]


_FFN_T = 256
_FFN_CW = FFN_DIM // 2
_FFN_CARRIES = [(8, _FFN_CW), (8, _FFN_CW)]


def _ffn_io(up3, cw, cb):
    lo, hi = (lambda b: b), (lambda b: 2 + b)
    rows = [Row(up3, _FFN_CW, fc=lo, dcols=FFN_DIM, dfc=lo), Row(up3, _FFN_CW, fc=hi, dcols=FFN_DIM, dfc=lo)]
    vecs = [Vec(cw, _FFN_CW, lo), Vec(cb, _FFN_CW, lo), Vec(cw, _FFN_CW, hi), Vec(cb, _FFN_CW, hi)]
    return rows, vecs


def ffn_mid_forward(name, up3, cw, cb):
    rows, vecs = _ffn_io(up3, cw, cb)
    s = up3.shape[1]
    (act,), saved = scan_fwd(name, _ffn_fn, nb=2, nchunk=s // _FFN_T, t=_FFN_T, rows=rows, vecs=vecs, carries=_FFN_CARRIES,
                             outs=[out_row((1, s, FFN_DIM), BF16, _FFN_CW, fc=lambda b: b)], save=True)
    return act, saved


def ffn_mid_backward(name, up3, cw, cb, saved, dact3):
    rows, vecs = _ffn_io(up3, cw, cb)
    s = up3.shape[1]
    return scan_bwd(name, _ffn_fn, nb=2, nchunk=s // _FFN_T, t=_FFN_T, rows=rows, vecs=vecs, carries=_FFN_CARRIES,
                    saved=saved, douts=[Row(dact3, _FFN_CW, fc=lambda b: b)])


def _adam_fn(ci, b, carries, rows, vecs):
    w, g, m, v = rows
    m = ADAM_B1 * m + (1.0 - ADAM_B1) * g
    v = ADAM_B2 * v + (1.0 - ADAM_B2) * (g * g)
    m_hat = m / (1.0 - ADAM_B1 ** ADAM_STEP)
    v_hat = v / (1.0 - ADAM_B2 ** ADAM_STEP)
    delta = -ADAM_LR * (m_hat / (jnp.sqrt(v_hat) + ADAM_EPS) + ADAM_WD * w)
    return [], [delta, m, v]


def adamw(name, w, g, m, v):
    shape = w.shape
    c = shape[-1]
    r = int(np.prod(shape[:-1]))
    t = _tile(r, 256, 8)
    as3 = lambda a: a.reshape(1, r, c)
    outs, _ = scan_fwd(name, _adam_fn, nb=1, nchunk=r // t, t=t, rows=[Row(as3(a)) for a in (w, g, m, v)], vecs=[], carries=[],
                       outs=[out_row((1, r, c)) for _ in range(3)], save=False)
    return [o.reshape(shape) for o in outs]


def rope_tables(positions):
    half = ROT_DIM // 2
    inv_freq = ROPE_THETA ** (-jnp.arange(0, ROT_DIM, 2, dtype=F32) / ROT_DIM)
    ang = positions.astype(F32)[:, None] * inv_freq
    s = positions.shape[0]
    cs = jnp.concatenate([jnp.cos(ang), jnp.cos(ang), jnp.ones((s, ATT_HEAD_DIM - ROT_DIM), F32)], axis=1)
    sn = jnp.concatenate([jnp.sin(ang), jnp.sin(ang), jnp.zeros((s, ATT_HEAD_DIM - ROT_DIM), F32)], axis=1)
    rot = np.zeros((ATT_W, ATT_W), np.float32)
    for h in range(ATT_HEADS):
        for e in range(half):
            rot[h * ATT_HEAD_DIM + e + half, h * ATT_HEAD_DIM + e] = -1.0
            rot[h * ATT_HEAD_DIM + e, h * ATT_HEAD_DIM + e + half] = 1.0
    return jnp.tile(cs, (1, ATT_HEADS)), jnp.tile(sn, (1, ATT_HEADS)), jnp.asarray(rot)


def attention_forward(lname, proj3, cs, sn, rot):
    s = proj3.shape[1]
    os_, ls_, keep = [], [], []
    for pi, (_, d) in enumerate(ATT_PATTERNS):
        view = lambda a: a.reshape(1, s // d, d * a.shape[-1])
        o, lse, saved = attn_forward(f"{lname}_attn{pi}", view(proj3), view(cs), view(sn), rot, d)
        os_.append(o.reshape(1, s, ATT_W))
        ls_.append(lse.reshape(1, s, ATT_W))
        keep.append(saved)
    y = merge_forward(f"{lname}_merge", os_, ls_)
    return y, (os_, ls_, keep)


def attention_backward(lname, proj3, cs, sn, rot, res, dmix3):
    os_, ls_, keep = res
    s = proj3.shape[1]
    dm = merge_backward(f"{lname}_merge_b", os_, ls_, dmix3)
    tot = None
    for pi, (_, d) in enumerate(ATT_PATTERNS):
        view = lambda a: a.reshape(1, s // d, d * a.shape[-1])
        dqkv = attn_backward(f"{lname}_attn{pi}_b", view(proj3), view(cs), view(sn), rot, d, keep[pi], view(dm[pi]), view(dm[3 + pi]))
        dqkv = [a.reshape(s, ATT_W) for a in dqkv]
        tot = dqkv if tot is None else [a + b for a, b in zip(tot, dqkv)]
    return tot


def mm(name, a, b, mode, out_dtype=F32, res=None, gate=None, tm=1024, tn=1536, tk=1024):
    if mode == "nn":
        (m, k), n = a.shape, b.shape[1]
    elif mode == "nt":
        (m, k), n = a.shape, b.shape[0]
    else:
        (k, m), n = a.shape, b.shape[1]
    tm, tn, tk = _tile(m, tm), _tile(n, tn), _tile(k, tk)
    nk = k // tk
    a_spec = pl.BlockSpec((tk, tm), lambda i, j, q: (q, i)) if mode == "tn" else pl.BlockSpec((tm, tk), lambda i, j, q: (i, q))
    b_spec = pl.BlockSpec((tn, tk), lambda i, j, q: (j, q)) if mode == "nt" else pl.BlockSpec((tk, tn), lambda i, j, q: (q, j))
    o_spec = pl.BlockSpec((tm, tn), lambda i, j, q: (i, j))
    fused = res is not None

    def body(*refs):
        if fused:
            a_ref, b_ref, r_ref, g_ref, o_ref, o2_ref, acc = refs
        else:
            a_ref, b_ref, o_ref, acc = refs
        q = pl.program_id(2)

        @pl.when(q == 0)
        def _():
            acc[...] = jnp.zeros(acc.shape, F32)

        acc[...] += _mxu(a_ref[...], b_ref[...], mode)

        @pl.when(q == nk - 1)
        def _():
            o_ref[...] = acc[...].astype(o_ref.dtype)
            if fused:
                o2_ref[...] = r_ref[...] + g_ref[...] * acc[...]

    ins, in_specs = [a, b], [a_spec, b_spec]
    out_shape, out_specs = [jax.ShapeDtypeStruct((m, n), out_dtype)], [o_spec]
    if fused:
        ins += [res, gate]
        in_specs += [o_spec, pl.BlockSpec((1, tn), lambda i, j, q: (0, j))]
        out_shape.append(jax.ShapeDtypeStruct((m, n), F32))
        out_specs.append(o_spec)
    out = pl.pallas_call(
        body, name=name, grid=(m // tm, n // tn, nk), in_specs=in_specs, out_specs=out_specs, out_shape=out_shape,
        scratch_shapes=[pltpu.VMEM((tm, tn), F32)],
        compiler_params=pltpu.CompilerParams(dimension_semantics=("parallel", "parallel", "arbitrary"),
                                             vmem_limit_bytes=VMEM_LIMIT_BYTES),
    )(*ins)
    return tuple(out) if fused else out[0]


def final_loss(name, x3, t3, g):
    s, d = x3.shape[1], x3.shape[2]
    t = _ROW_T

    def body(x_ref, t_ref, g_ref, loss_ref, dx_ref, dg_ref):
        i = pl.program_id(0)
        tv = t_ref[0]

        def f(x, gg):
            y = x * lax.rsqrt(jnp.mean(x * x, axis=-1, keepdims=True) + NORM_EPS) * gg
            e = y - tv
            return 0.5 * jnp.sum(jnp.mean(e * e, axis=-1, keepdims=True), axis=0, keepdims=True)

        l, vjp = jax.vjp(f, x_ref[0], g_ref[...])
        dx, dg = vjp(jnp.ones((1, 1), F32))
        dx_ref[0] = dx

        @pl.when(i == 0)
        def _():
            loss_ref[...] = jnp.zeros(loss_ref.shape, F32)
            dg_ref[...] = jnp.zeros(dg_ref.shape, F32)

        loss_ref[...] += jnp.broadcast_to(l, loss_ref.shape)
        dg_ref[...] += dg

    row = pl.BlockSpec((1, t, d), lambda i: (0, i, 0))
    vec = pl.BlockSpec((1, d), lambda i: (0, 0))
    return pl.pallas_call(
        body, name=name, grid=(s // t,), in_specs=[row, row, vec],
        out_specs=[pl.BlockSpec((8, 128), lambda i: (0, 0)), row, vec],
        out_shape=[jax.ShapeDtypeStruct((8, 128), F32), jax.ShapeDtypeStruct(x3.shape, F32), jax.ShapeDtypeStruct((1, d), F32)],
        compiler_params=pltpu.CompilerParams(dimension_semantics=("arbitrary",), vmem_limit_bytes=VMEM_LIMIT_BYTES),
    )(x3, t3, g)


_ADA_TN = 512


def ada_forward(name, c16, ada_w):
    depth, d, cols = ada_w.shape

    def body(c_ref, w_ref, o_ref):
        o_ref[0] = _mxu(_silu(c_ref[...]), w_ref[0], "nn")

    return pl.pallas_call(
        body, name=name, grid=(depth, cols // _ADA_TN),
        in_specs=[pl.BlockSpec((16, d), lambda l, j: (0, 0)), pl.BlockSpec((1, d, _ADA_TN), lambda l, j: (l, 0, j))],
        out_specs=pl.BlockSpec((1, 16, _ADA_TN), lambda l, j: (l, 0, j)),
        out_shape=jax.ShapeDtypeStruct((depth, 16, cols), F32),
        compiler_params=pltpu.CompilerParams(dimension_semantics=("arbitrary", "arbitrary"), vmem_limit_bytes=VMEM_LIMIT_BYTES),
    )(c16, ada_w)


def ada_backward(name, c16, dmod16, w, m, v):
    depth, d, cols = w.shape

    def body(c_ref, dm_ref, w_ref, m_ref, v_ref, g_ref, dl_ref, nm_ref, nv_ref):
        g = _mxu(_silu(c_ref[...]), dm_ref[0], "tn")
        _, (delta, nm, nv) = _adam_fn(None, None, [], [w_ref[0], g, m_ref[0], v_ref[0]], [])
        g_ref[0], dl_ref[0], nm_ref[0], nv_ref[0] = g, delta, nm, nv

    blk = pl.BlockSpec((1, d, _ADA_TN), lambda l, j: (l, 0, j))
    return pl.pallas_call(
        body, name=name, grid=(depth, cols // _ADA_TN),
        in_specs=[pl.BlockSpec((16, d), lambda l, j: (0, 0)), pl.BlockSpec((1, 16, _ADA_TN), lambda l, j: (l, 0, j)), blk, blk, blk],
        out_specs=[blk] * 4, out_shape=[jax.ShapeDtypeStruct(w.shape, F32)] * 4,
        compiler_params=pltpu.CompilerParams(dimension_semantics=("arbitrary", "arbitrary"), vmem_limit_bytes=VMEM_LIMIT_BYTES),
    )(c16, dmod16, w, m, v)


def _sum_fn(ci, b, carries, rows, vecs):
    acc = rows[0]
    for r in rows[1:]:
        acc = acc + r
    return [], [acc]


def sum_slots(name, a, nsum, out_dtype=F32):
    n, r, c = a.shape
    nb = n // nsum
    t = _tile(r, 256, 8)
    rows = [Row(a, fb=(lambda b, k=k: k * nb + b)) for k in range(nsum)]
    (out,), _ = scan_fwd(name, _sum_fn, nb=nb, nchunk=r // t, t=t, rows=rows, vecs=[], carries=[],
                         outs=[out_row((nb, r, c), out_dtype, fb=lambda b: b)], save=False)
    return out


def _flip(mask, pos):
    return tuple((1 - p) if m else p for m, p in zip(mask, pos))


ALL_PEERS = [(a, b, c) for a in (0, 1) for b in (0, 1) for c in (0, 1)][1:]
CHIP_PEERS = [(1, 0, 0), (0, 1, 0), (1, 1, 0)]
SIBLING = [(0, 0, 1)]


def _divisor(size, target, unit):
    best = 1
    for n in range(1, target + 1):
        if size % n == 0 and (size // n) % unit == 0:
            best = n
    return best


def _pieces(src, dst, pieces):
    shape = src.shape
    unit = 16 if src.dtype == BF16 else 8
    if pieces <= 1:
        return [(src, dst)]
    if len(shape) == 2:
        n = _divisor(shape[0], pieces, unit)
        s = shape[0] // n
        return [(src.at[pl.ds(i * s, s)], dst.at[pl.ds(i * s, s)]) for i in range(n)]
    assert len(shape) == 3, shape
    n = _divisor(shape[1], max(pieces // shape[0], 1), unit)
    s = shape[1] // n
    return [(src.at[j, pl.ds(i * s, s)], dst.at[j, pl.ds(i * s, s)]) for j in range(shape[0]) for i in range(n)]


def comm_call(name, arrays, out_shapes, masks, src_fn, dst_fn, local_fn=None, pieces=1):
    na, npeer = len(arrays), len(masks)

    def body(*refs):
        ins, outs = refs[:na], refs[na:2 * na]
        send_sems, recv_sems, loc_sems = refs[2 * na:]
        me = (lax.axis_index("x"), lax.axis_index("y"), lax.axis_index("c"))
        local = []
        if local_fn is not None:
            for k in range(na):
                s, d = local_fn(k, ins[k], outs[k], me)
                for ps, pd in _pieces(s, d, pieces):
                    pltpu.make_async_copy(ps, pd, loc_sems.at[k]).start()
                local.append(pltpu.make_async_copy(s, d, loc_sems.at[k]))

        def remote(k, p, src, dst, to):
            return pltpu.make_async_remote_copy(
                src_ref=src, dst_ref=dst, send_sem=send_sems.at[k * npeer + p], recv_sem=recv_sems.at[k * npeer + p],
                device_id=to, device_id_type=MESH)

        for k in range(na):
            for p in range(npeer):
                peer = _flip(masks[p], me)
                for ps, pd in _pieces(src_fn(k, ins[k], me, peer), dst_fn(k, outs[k], me), pieces):
                    remote(k, p, ps, pd, peer).start()
        for k in range(na):
            for p in range(npeer):
                peer = _flip(masks[p], me)
                remote(k, p, src_fn(k, ins[k], me, peer), dst_fn(k, outs[k], peer), peer).wait_recv()
        for k in range(na):
            for p in range(npeer):
                peer = _flip(masks[p], me)
                remote(k, p, src_fn(k, ins[k], me, peer), dst_fn(k, outs[k], me), peer).wait_send()
        for cp in local:
            cp.wait()

    hbm = pl.BlockSpec(memory_space=pl.ANY)
    out = pl.pallas_call(
        body, name=name, in_specs=[hbm] * na, out_specs=[hbm] * na,
        out_shape=[jax.ShapeDtypeStruct(s, a.dtype) for s, a in zip(out_shapes, arrays)],
        scratch_shapes=[pltpu.SemaphoreType.DMA((na * npeer,)), pltpu.SemaphoreType.DMA((na * npeer,)),
                        pltpu.SemaphoreType.DMA((na,))],
    )(*arrays)
    return list(out)


def _dev(pos):
    return 4 * pos[0] + 2 * pos[1] + pos[2]


def _chip(pos):
    return 2 * pos[0] + pos[1]


def allgather8(name, a):
    (out,) = comm_call(name, [a], [(8,) + a.shape], ALL_PEERS,
                       src_fn=lambda k, r, me, peer: r, dst_fn=lambda k, o, sender: o.at[_dev(sender)],
                       local_fn=lambda k, r, o, me: (r, o.at[_dev(me)]))
    return out


def gather_layer_from_chips(name, arrays):
    return comm_call(name, arrays, [(4,) + a.shape[1:] for a in arrays], CHIP_PEERS,
                     src_fn=lambda k, r, me, peer: r.at[me[2]], dst_fn=lambda k, o, sender: o.at[_chip(sender)],
                     local_fn=lambda k, r, o, me: (r.at[me[2]], o.at[_chip(me)]), pieces=8)


def swap_layers(name, arrays):
    return comm_call(name, arrays, [(2,) + a.shape for a in arrays], SIBLING,
                     src_fn=lambda k, r, me, peer: r, dst_fn=lambda k, o, sender: o.at[sender[2]],
                     local_fn=lambda k, r, o, me: (r, o.at[me[2]]), pieces=32)


def swap_other_layer(name, arrays):
    return comm_call(name, arrays, [a.shape for a in arrays], SIBLING,
                     src_fn=lambda k, r, me, peer: r.at[peer[2]], dst_fn=lambda k, o, sender: o.at[1],
                     local_fn=lambda k, r, o, me: (r.at[me[2]], o.at[0]), pieces=32)


def scatter_to_chips(name, arrays):
    return comm_call(name, arrays, [a.shape for a in arrays], CHIP_PEERS,
                     src_fn=lambda k, r, me, peer: r.at[_chip(peer)], dst_fn=lambda k, o, sender: o.at[_chip(sender)],
                     local_fn=lambda k, r, o, me: (r.at[_chip(me)], o.at[_chip(me)]), pieces=8)


def _pack(arrs):
    flat = jnp.concatenate([a.reshape(-1).astype(F32) for a in arrs])
    n = flat.shape[0]
    pad = (-n) % (_ROW_T * 128)
    return jnp.pad(flat, (0, pad)).reshape(-1, 128)


def _unpack(buf, shapes):
    flat = buf.reshape(-1)
    out, o = [], 0
    for s in shapes:
        n = int(np.prod(s))
        out.append(flat[o:o + n].reshape(s))
        o += n
    return out


_WEIGHTS = ["ada_w", "ada_b", "norm1_g", "w_in", "ssd_conv_w", "ssd_conv_b", "ssd_dt_bias", "ssd_a_log", "ssd_d", "ssd_norm_g",
            "pool_w", "pool_scale", "w_out", "norm2_g", "ffn_up", "ffn_conv_w", "ffn_conv_b", "ffn_down", "final_g"]
_BIG = ["w_in", "w_out", "ffn_up", "ffn_down"]
_SMALL = [n for n in _WEIGHTS if n not in _BIG and n != "ada_w"]
_COL_SHARDED_SMALL = {"ssd_conv_w": 256, "ffn_conv_w": 1408}


def _pad_lanes(v, n=128):
    return jnp.pad(v.astype(F32), (0, n - v.shape[0]))[None]


def _perm_cols(w):
    pad = jnp.zeros(w.shape[:-1] + (IN_WP - IN_W,), w.dtype)
    return jnp.concatenate([w[..., :1536], w[..., 1544:], w[..., 1536:1544], pad], axis=-1)


def _unperm_cols(g):
    return jnp.concatenate([g[..., :1536], g[..., 2560:2568], g[..., 1536:2560]], axis=-1)


def _cols_by_chip(g, ncol):
    return g.reshape(g.shape[0], 4, ncol).transpose(1, 0, 2)


def _layer_forward(i, x3, modv, wts, sp, cs, sn, rot):
    sh1, sc1, g1, sh2, sc2, g2 = modv
    s = x3.shape[1]
    h1 = norm_mod_forward(f"l{i}_norm1", x3, wts["norm1_g"], sc1, sh1)
    proj3 = mm(f"l{i}_proj", h1[0], wts["w_in"], "nn")[None]
    y_ssd, sv_ssd = ssd_forward(f"l{i}_ssd", proj3, sp)
    y_pool, sv_pool = pool_forward(f"l{i}_pool", proj3, wts["wbd"], wts["pool_scale"])
    y_att, res_att = attention_forward(f"l{i}", proj3, cs, sn, rot)
    mix = jnp.concatenate([y_ssd, y_pool, y_att], axis=-1)
    out, x1 = mm(f"l{i}_wout", mix[0], wts["w_out"], "nn", res=x3[0], gate=g1)
    x1 = x1[None]
    h2 = norm_mod_forward(f"l{i}_norm2", x1, wts["norm2_g"], sc2, sh2)
    up3 = mm(f"l{i}_up", h2[0], wts["ffn_up"], "nn")[None]
    act, sv_ffn = ffn_mid_forward(f"l{i}_ffn", up3, wts["ffn_conv_w"], wts["ffn_conv_b"])
    dn, x2 = mm(f"l{i}_down", act[0], wts["ffn_down"], "nn", res=x1[0], gate=g2)
    keep = dict(x=x3, h1=h1, proj3=proj3, sv_ssd=sv_ssd, sv_pool=sv_pool, res_att=res_att, mix=mix, out=out[None], x1=x1, h2=h2,
                up3=up3, act=act, sv_ffn=sv_ffn, dn=dn[None])
    return x2[None], keep


def _layer_backward(i, dx2, keep, modv, wts, sp, cs, sn, rot):
    sh1, sc1, g1, sh2, sc2, g2 = modv
    k = keep
    d_dn, d_g2 = gate_backward(f"l{i}_gate2_b", k["dn"], g2, dx2)
    d_act = mm(f"l{i}_down_bx", d_dn[0], wts["ffn_down"], "nt")
    g_down = mm(f"l{i}_down_bw", k["act"][0], d_dn[0], "tn")
    (dg_, du_), dv_ffn = ffn_mid_backward(f"l{i}_ffn_b", k["up3"], wts["ffn_conv_w"], wts["ffn_conv_b"], k["sv_ffn"], d_act[None])
    d_up = jnp.concatenate([dg_[0], du_[0]], axis=-1)
    d_h2 = mm(f"l{i}_up_bx", d_up, wts["ffn_up"], "nt")
    g_up = mm(f"l{i}_up_bw", k["h2"][0], d_up, "tn")
    dx1, (d_n2, d_sc2, d_sh2) = norm_mod_backward(f"l{i}_norm2_b", k["x1"], wts["norm2_g"], sc2, sh2, d_h2[None], dx2)
    d_out, d_g1 = gate_backward(f"l{i}_gate1_b", k["out"], g1, dx1)
    d_mix = mm(f"l{i}_wout_bx", d_out[0], wts["w_out"], "nt")[None]
    g_wout = mm(f"l{i}_wout_bw", k["mix"][0], d_out[0], "tn")
    (dz, dxs, dbm, dcm, ddt), dv_ssd = ssd_backward(f"l{i}_ssd_b", k["proj3"], sp, k["sv_ssd"], d_mix)
    (du_pool,), (d_wbd, d_pscale) = pool_backward(f"l{i}_pool_b", k["proj3"], wts["wbd"], wts["pool_scale"], k["sv_pool"], d_mix)
    dq, dk, dv = attention_backward(f"l{i}", k["proj3"], cs, sn, rot, k["res_att"], d_mix)
    d_proj = jnp.concatenate([dz[0], dxs[0], dbm[0], dcm[0], du_pool[0], dq, dk, dv, ddt[0] + ddt[1],
                              jnp.zeros((dq.shape[0], IN_WP - IN_USED), F32)], axis=-1)
    d_h1 = mm(f"l{i}_proj_bx", d_proj, wts["w_in"], "nt")
    g_win = mm(f"l{i}_proj_bw", k["h1"][0], d_proj, "tn")
    dx, (d_n1, d_sc1, d_sh1) = norm_mod_backward(f"l{i}_norm1_b", k["x"], wts["norm1_g"], sc1, sh1, d_h1[None], dx1)
    dcwx, dcbx, dcwb, dcbb, dcwc, dcbc, ddtb, dalog, ddsk, dng = dv_ssd
    small = dict(
        norm1_g=d_n1[0], norm2_g=d_n2[0],
        ssd_conv_w=jnp.concatenate([dcwx[:, :512], dcwb[:, 512:768], dcwc[:, 768:]], axis=1),
        ssd_conv_b=jnp.concatenate([dcbx[0, :512], dcbb[0, 512:768], dcbc[0, 768:]]),
        ssd_dt_bias=ddtb[0, :8], ssd_a_log=dalog[0, :8], ssd_d=ddsk[0, :8], ssd_norm_g=dng[0],
        pool_w=jnp.stack([d_wbd[64 * g:64 * g + 64, 64 * g:64 * g + 64] for g in range(4)]), pool_scale=d_pscale[0],
        ffn_conv_w=jnp.concatenate([dv_ffn[0][:, :FFN_DIM], dv_ffn[2][:, FFN_DIM:]], axis=1),
        ffn_conv_b=jnp.concatenate([dv_ffn[1][0, :FFN_DIM], dv_ffn[3][0, FFN_DIM:]]),
    )
    dmod = jnp.concatenate([d_sh1[0], d_sc1[0], d_g1[0], d_sh2[0], d_sc2[0], d_g2[0]])
    return dx, dict(w_in=g_win, w_out=g_wout, ffn_up=g_up, ffn_down=g_down), small, dmod


def kernel(x, c, positions, ada_w, ada_b, norm1_g, w_in, ssd_conv_w, ssd_conv_b, ssd_dt_bias, ssd_a_log, ssd_d, ssd_norm_g, pool_w, pool_scale, w_out, norm2_g, ffn_up, ffn_conv_w, ffn_conv_b, ffn_down, final_g, loss_target, m_ada_w, m_ada_b, m_norm1_g, m_w_in, m_ssd_conv_w, m_ssd_conv_b, m_ssd_dt_bias, m_ssd_a_log, m_ssd_d, m_ssd_norm_g, m_pool_w, m_pool_scale, m_w_out, m_norm2_g, m_ffn_up, m_ffn_conv_w, m_ffn_conv_b, m_ffn_down, m_final_g, v_ada_w, v_ada_b, v_norm1_g, v_w_in, v_ssd_conv_w, v_ssd_conv_b, v_ssd_dt_bias, v_ssd_a_log, v_ssd_d, v_ssd_norm_g, v_pool_w, v_pool_scale, v_w_out, v_norm2_g, v_ffn_up, v_ffn_conv_w, v_ffn_conv_b, v_ffn_down, v_final_g):
    args = dict(locals())
    w = {n: args[n] for n in _WEIGHTS}
    m = {n: args["m_" + n] for n in _WEIGHTS}
    v = {n: args["v_" + n] for n in _WEIGHTS}
    d = D_MODEL
    me = (lax.axis_index("x"), lax.axis_index("y"), lax.axis_index("c"))
    chip, dev = _chip(me), _dev(me)

    shapes0 = [c.shape, ssd_conv_w.shape, ffn_conv_w.shape]
    g0 = allgather8("gather_c_conv", _pack([c, ssd_conv_w, ffn_conv_w]))
    c16 = jnp.pad(g0[:, :d // 128, :].reshape(8, d), ((0, 8), (0, 0)))
    by_chip = [_unpack(g0[2 * j], shapes0) for j in range(4)]
    conv_w_full = jnp.concatenate([p[1] for p in by_chip], axis=-1)
    fconv_w_full = jnp.concatenate([p[2] for p in by_chip], axis=-1)

    modp = ada_forward("ada_fwd", c16, ada_w)[:, :8]
    g1 = allgather8("gather_mod", _pack([modp]))
    modfull = jnp.concatenate([_unpack(g1[2 * j], [modp.shape])[0] for j in range(4)], axis=-1)
    mod = lax.dynamic_index_in_dim(modfull, dev, axis=1, keepdims=False) + ada_b
    modv = [[mod[i, q * d:(q + 1) * d][None] for q in range(6)] for i in range(DEPTH)]

    got = gather_layer_from_chips("gather_w", [w[n].astype(BF16) for n in _BIG])
    both = swap_layers("swap_w", got)
    full = dict(
        w_in=_perm_cols(both[0].transpose(0, 2, 1, 3).reshape(DEPTH, d, IN_W)),
        w_out=both[1].reshape(DEPTH, d, d),
        ffn_up=both[2].transpose(0, 2, 1, 3).reshape(DEPTH, d, 2 * FFN_DIM),
        ffn_down=both[3].reshape(DEPTH, FFN_DIM, d),
    )

    cs, sn, rot = rope_tables(positions[0])
    e_mat = jnp.asarray((np.arange(128)[:, None] == (np.arange(SSD_INNER)[None, :] // 64)).astype(np.float32))
    eye4 = jnp.eye(4, dtype=F32)
    wts, sps = [], []
    for i in range(DEPTH):
        wts.append(dict(
            w_in=full["w_in"][i], w_out=full["w_out"][i], ffn_up=full["ffn_up"][i], ffn_down=full["ffn_down"][i],
            norm1_g=norm1_g[i][None], norm2_g=norm2_g[i][None], pool_scale=pool_scale[i][None],
            wbd=(eye4[:, None, :, None] * pool_w[i][:, :, None, :]).reshape(POOL_W, POOL_W),
            ffn_conv_w=fconv_w_full[i], ffn_conv_b=ffn_conv_b[i][None]))
        sps.append(dict(cw=conv_w_full[i], cb=ssd_conv_b[i][None], dtb=_pad_lanes(ssd_dt_bias[i]), alog=_pad_lanes(ssd_a_log[i]),
                        dsk=_pad_lanes(ssd_d[i]), ng=ssd_norm_g[i][None], e=e_mat))

    xc, keeps = x, []
    for i in range(DEPTH):
        xc, keep = _layer_forward(i, xc, modv[i], wts[i], sps[i], cs, sn, rot)
        keeps.append(keep)
    lossblk, dx, d_final = final_loss("final_loss", xc, loss_target, final_g[None])
    loss = lax.psum(lossblk[0, 0], ("x", "y", "c"))

    big_g, small_g, dmods = [None] * DEPTH, [None] * DEPTH, [None] * DEPTH
    for i in reversed(range(DEPTH)):
        dx, big_g[i], small_g[i], dmods[i] = _layer_backward(i, dx, keeps[i], modv[i], wts[i], sps[i], cs, sn, rot)

    by_dest = [
        jnp.stack([_cols_by_chip(_unperm_cols(big_g[i]["w_in"]), IN_W // 4) for i in range(DEPTH)]),
        jnp.stack([big_g[i]["w_out"].reshape(4, d // 4, d) for i in range(DEPTH)]),
        jnp.stack([_cols_by_chip(big_g[i]["ffn_up"], 2 * FFN_DIM // 4) for i in range(DEPTH)]),
        jnp.stack([big_g[i]["ffn_down"].reshape(4, FFN_DIM // 4, d) for i in range(DEPTH)]),
    ]
    pair = swap_other_layer("swap_g", by_dest)
    core_sum = [sum_slots(f"sum_cores_{n}", p.reshape((8,) + p.shape[2:]), 2, BF16) for n, p in zip(_BIG, pair)]
    from_chips = scatter_to_chips("scatter_g", core_sum)
    chip_sum = [sum_slots(f"sum_chips_{n}", q, 4)[0] for n, q in zip(_BIG, from_chips)]
    reduced = swap_layers("swap_r", chip_sum)
    grads = dict(zip(_BIG, reduced))

    part = dict(ada_b=jnp.stack(dmods), final_g=d_final[0])
    for n in _SMALL:
        if n not in part:
            part[n] = jnp.stack([small_g[i][n] for i in range(DEPTH)])
    full_shapes = [part[n].shape for n in _SMALL]
    gs = allgather8("gather_small", _pack([part[n] for n in _SMALL]))
    tot = _unpack(sum_slots("sum_small", gs, 8)[0], full_shapes)
    small_tot = dict(zip(_SMALL, tot))
    dmod_all = gs[:, :DEPTH * 6 * d // 128, :].reshape(8, DEPTH, 6 * d)
    for n, ncol in _COL_SHARDED_SMALL.items():
        small_tot[n] = lax.dynamic_slice_in_dim(small_tot[n], chip * ncol, ncol, axis=2)
    grads.update(small_tot)

    ncol = ada_w.shape[2]
    dm = lax.dynamic_slice_in_dim(dmod_all, chip * ncol, ncol, axis=2).transpose(1, 0, 2)
    upd = {}
    g_ada, *upd["ada_w"] = ada_backward("ada_bwd", c16, jnp.pad(dm, ((0, 0), (0, 8), (0, 0))), ada_w, m["ada_w"], v["ada_w"])
    grads["ada_w"] = g_ada

    for n in _BIG:
        upd[n] = adamw(f"adam_{n}", w[n], grads[n], m[n], v[n])
    shapes_s = [w[n].shape for n in _SMALL]
    packed = [_pack([src[n] for n in _SMALL]) for src in (w, grads, m, v)]
    outs_s = [_unpack(o, shapes_s) for o in adamw("adam_small", *packed)]
    for q, n in enumerate(_SMALL):
        upd[n] = [outs_s[0][q], outs_s[1][q], outs_s[2][q]]

    return (loss, dx, *[grads[n] for n in _WEIGHTS], *[upd[n][0] for n in _WEIGHTS], *[upd[n][1] for n in _WEIGHTS],
            *[upd[n][2] for n in _WEIGHTS])
```

```python
import functools
import math

import numpy as np
import jax
import jax.numpy as jnp
from jax import lax
from jax.experimental import pallas as pl
from jax.experimental.pallas import tpu as pltpu

F32 = jnp.float32
BF16 = jnp.bfloat16
HI = lax.Precision.HIGHEST
MESH = pl.DeviceIdType.MESH

D_MODEL = 1024
SEQ = 4096
DEPTH = 2
SSD_INNER = 512
SSD_HEADS = 8
SSD_STATE = 128
POOL_W = 256
POOL_WINDOWS = (2, 4, 8, 16)
ATT_W = 256
ATT_HEADS = 4
ATT_HEAD_DIM = 64
ATT_PATTERNS = ((128, 1), (512, 4), (2048, 16))
ATT_BLOCK = 128
ROT_DIM = 16
ROPE_THETA = 500000.0
IN_W = 2568
IN_WP = 2816
IN_USED = 2688
FFN_DIM = 2816
NORM_EPS = 1e-6
ADAM_LR, ADAM_B1, ADAM_B2, ADAM_EPS, ADAM_WD, ADAM_STEP = 0.001, 0.9, 0.999, 1e-08, 0.01, 10

VMEM_LIMIT_BYTES = 56 * 1024 * 1024
NEG = -1e30


def _mxu(a, b, mode):
    dims = {"nn": ((1,), (0,)), "nt": ((1,), (1,)), "tn": ((0,), (0,))}[mode]
    return lax.dot_general(a.astype(BF16), b.astype(BF16), (dims, ((), ())), preferred_element_type=F32)


@functools.partial(jax.custom_vjp, nondiff_argnums=(2,))
def _bdot(a, b, mode):
    return _mxu(a, b, mode)


def _bdot_fwd(a, b, mode):
    return _mxu(a, b, mode), (a, b)


def _bdot_bwd(mode, res, g):
    a, b = res
    if mode == "nn":
        return _mxu(g, b, "nt"), _mxu(a, g, "tn")
    if mode == "nt":
        return _mxu(g, b, "nn"), _mxu(g, a, "tn")
    return _mxu(b, g, "nt"), _mxu(a, g, "nn")


_bdot.defvjp(_bdot_fwd, _bdot_bwd)


def _fxu(a, b, mode):
    dims = {"nn": ((1,), (0,)), "nt": ((1,), (1,)), "tn": ((0,), (0,))}[mode]
    return lax.dot_general(a, b, (dims, ((), ())), precision=HI, preferred_element_type=F32)


@functools.partial(jax.custom_vjp, nondiff_argnums=(2,))
def _fdot(a, b, mode):
    return _fxu(a, b, mode)


def _fdot_fwd(a, b, mode):
    return _fxu(a, b, mode), (a, b)


def _fdot_bwd(mode, res, g):
    a, b = res
    if mode == "nn":
        return _fxu(g, b, "nt"), _fxu(a, g, "tn")
    if mode == "nt":
        return _fxu(g, b, "nn"), _fxu(g, a, "tn")
    return _fxu(b, g, "nt"), _fxu(a, g, "nn")


_fdot.defvjp(_fdot_fwd, _fdot_bwd)


def _iota(shape, dim):
    return lax.broadcasted_iota(jnp.int32, shape, dim)


def _make_shift(h):
    @functools.partial(jax.custom_vjp, nondiff_argnums=(2,))
    def shift(halo, cur, k):
        if k == 0:
            return cur
        full = jnp.concatenate([halo, cur], axis=0)
        return pltpu.roll(full, k, 0)[h:]

    def fwd(halo, cur, k):
        return shift(halo, cur, k), None

    def bwd(k, _, g):
        t, w = g.shape
        if k == 0:
            return jnp.zeros((h, w), F32), g
        d_cur = jnp.where(_iota((t, w), 0) < t - k, pltpu.roll(g, t - k, 0), 0.0)
        top = g[:h]
        d_halo = jnp.where(_iota((h, w), 0) >= h - k, pltpu.roll(top, h - k, 0) if k < h else top, 0.0)
        return d_halo, d_cur

    shift.defvjp(fwd, bwd)
    return shift


_shift8 = _make_shift(8)
_shift16 = _make_shift(16)


def _make_tail(h):
    @jax.custom_vjp
    def tail(x):
        return x[x.shape[0] - h:]

    def fwd(x):
        return tail(x), x.shape[0]

    def bwd(t, g):
        return (jnp.concatenate([jnp.zeros((t - h, g.shape[1]), F32), g], axis=0),)

    tail.defvjp(fwd, bwd)
    return tail


_tail8 = _make_tail(8)
_tail16 = _make_tail(16)


def _rowk(w, k):
    return jnp.sum(jnp.where(_iota(w.shape, 0) == k, w, 0.0), axis=0, keepdims=True)


def _silu(x):
    return x * (1.0 / (1.0 + jnp.exp(-x)))


def _softplus(x):
    return jnp.maximum(x, 0.0) + jnp.log(1.0 + jnp.exp(-jnp.abs(x)))


def _tile(dim, target, unit=128):
    if dim <= target:
        return dim
    best = None
    for t in range(unit, target + 1, unit):
        if dim % t == 0:
            best = t
    assert best is not None, (dim, target)
    return best


class Row:
    def __init__(self, arr, w=None, fb=None, fc=None, diff=True, slot=False, dcols=None, dfc=None):
        self.arr = arr
        self.w = arr.shape[2] if w is None else w
        self.fb = (lambda b: 0) if fb is None else fb
        self.fc = (lambda b: 0) if fc is None else fc
        self.diff = diff
        self.slot = slot
        self.dcols = dcols
        self.dfc = dfc


class Vec:
    def __init__(self, arr, w=None, fc=None, diff=True):
        self.arr = arr
        self.w = arr.shape[1] if w is None else w
        self.fc = fc
        self.diff = diff


def _row_spec(r, t, nchunk, reverse):
    if reverse:
        return pl.BlockSpec((1, t, r.w), lambda b, i, r=r: (r.fb(b), nchunk - 1 - i, r.fc(b)))
    return pl.BlockSpec((1, t, r.w), lambda b, i, r=r: (r.fb(b), i, r.fc(b)))


def _vec_spec(v):
    if v.fc is None:
        return pl.BlockSpec(v.arr.shape, lambda b, i: (0, 0))
    return pl.BlockSpec((v.arr.shape[0], v.w), lambda b, i, v=v: (0, v.fc(b)))


def _cparams():
    return pltpu.CompilerParams(dimension_semantics=("arbitrary", "arbitrary"), vmem_limit_bytes=VMEM_LIMIT_BYTES)


def scan_fwd(name, fn, *, nb, nchunk, t, rows, vecs, carries, outs, save):
    nr, nv, nc, no = len(rows), len(vecs), len(carries), len(outs)

    def body(*refs):
        row_refs, vec_refs = refs[:nr], refs[nr:nr + nv]
        out_refs = refs[nr + nv:nr + nv + no]
        save_refs = refs[nr + nv + no:nr + nv + no + (nc if save else 0)]
        car = refs[len(refs) - nc:] if nc else ()
        b, i = pl.program_id(0), pl.program_id(1)
        if nc:
            @pl.when(i == 0)
            def _():
                for c_ref in car:
                    c_ref[...] = jnp.zeros(c_ref.shape, F32)
        cin = [c_ref[...] for c_ref in car]
        if save:
            for s_ref, cv in zip(save_refs, cin):
                s_ref[0, 0] = cv
        new_c, o = fn(i, b, cin, [r[0] for r in row_refs], [v[...] for v in vec_refs])
        for c_ref, cv in zip(car, new_c):
            c_ref[...] = cv
        for o_ref, ov in zip(out_refs, o):
            o_ref[0] = ov.astype(o_ref.dtype)

    out_shape = [o.arr for o in outs]
    out_specs = [_row_spec(o, t, nchunk, False) for o in outs]
    if save:
        for cs in carries:
            out_shape.append(jax.ShapeDtypeStruct((nb, nchunk) + tuple(cs), F32))
            out_specs.append(pl.BlockSpec((1, 1) + tuple(cs), lambda b, i: (b, i, 0, 0)))
    res = pl.pallas_call(
        body, name=name, grid=(nb, nchunk),
        in_specs=[_row_spec(r, t, nchunk, False) for r in rows] + [_vec_spec(v) for v in vecs],
        out_specs=out_specs, out_shape=out_shape,
        scratch_shapes=[pltpu.VMEM(tuple(cs), F32) for cs in carries],
        compiler_params=_cparams(),
    )(*[r.arr for r in rows], *[v.arr for v in vecs])
    return list(res[:no]), list(res[no:])


def scan_bwd(name, fn, *, nb, nchunk, t, rows, vecs, carries, saved, douts, adds=None):
    adds = adds or {}
    nr, nv, nc, no = len(rows), len(vecs), len(carries), len(douts)
    dri = [k for k, r in enumerate(rows) if r.diff]
    dvi = [k for k, v in enumerate(vecs) if v.diff]
    add_keys = sorted(adds)
    na = len(add_keys)

    def body(*refs):
        p = 0
        row_refs = refs[p:p + nr]; p += nr
        vec_refs = refs[p:p + nv]; p += nv
        save_refs = refs[p:p + nc]; p += nc
        dout_refs = refs[p:p + no]; p += no
        add_refs = refs[p:p + na]; p += na
        drow_refs = refs[p:p + len(dri)]; p += len(dri)
        dvec_refs = refs[p:p + len(dvi)]; p += len(dvi)
        dcar = refs[p:]
        b, ir = pl.program_id(0), pl.program_id(1)
        ci = nchunk - 1 - ir
        if nc:
            @pl.when(ir == 0)
            def _():
                for c_ref in dcar:
                    c_ref[...] = jnp.zeros(c_ref.shape, F32)
        rows_v = [r[0] for r in row_refs]
        vecs_v = [v[...] for v in vec_refs]
        cin = [s[0, 0] for s in save_refs]
        dc = [c_ref[...] for c_ref in dcar]
        dout_v = [d[0].astype(F32) for d in dout_refs]

        def f(cs, dr, dv):
            rr, vv = list(rows_v), list(vecs_v)
            for k, idx in enumerate(dri):
                rr[idx] = dr[k]
            for k, idx in enumerate(dvi):
                vv[idx] = dv[k]
            return fn(ci, b, cs, rr, vv)

        _, vjp = jax.vjp(f, cin, [rows_v[k].astype(F32) for k in dri], [vecs_v[k].astype(F32) for k in dvi])
        dcin, drows, dvecs = vjp((dc, dout_v))
        for c_ref, cv in zip(dcar, dcin):
            c_ref[...] = cv
        for k, (o_ref, ov) in enumerate(zip(drow_refs, drows)):
            if dri[k] in adds:
                ov = ov + add_refs[add_keys.index(dri[k])][0].astype(F32)
            o_ref[0] = ov.astype(o_ref.dtype)
        for k, (o_ref, ov) in enumerate(zip(dvec_refs, dvecs)):
            first = (ir == 0) if vecs[dvi[k]].fc is not None else jnp.logical_and(ir == 0, b == 0)

            @pl.when(first)
            def _(o_ref=o_ref, ov=ov):
                o_ref[...] = ov

            @pl.when(jnp.logical_not(first))
            def _(o_ref=o_ref, ov=ov):
                o_ref[...] += ov

    in_specs = ([_row_spec(r, t, nchunk, True) for r in rows] + [_vec_spec(v) for v in vecs]
                + [pl.BlockSpec((1, 1) + tuple(cs), lambda b, i: (b, nchunk - 1 - i, 0, 0)) for cs in carries]
                + [_row_spec(d, t, nchunk, True) for d in douts]
                + [_row_spec(adds[k], t, nchunk, True) for k in add_keys])
    out_shape, out_specs = [], []
    for k in dri:
        r = rows[k]
        if r.slot:
            out_shape.append(jax.ShapeDtypeStruct((nb, r.arr.shape[1], r.w), F32))
            out_specs.append(pl.BlockSpec((1, t, r.w), lambda b, i: (b, nchunk - 1 - i, 0)))
        elif r.dcols is not None:
            out_shape.append(jax.ShapeDtypeStruct((r.arr.shape[0], r.arr.shape[1], r.dcols), F32))
            out_specs.append(pl.BlockSpec((1, t, r.w), lambda b, i, r=r: (r.fb(b), nchunk - 1 - i, r.dfc(b))))
        else:
            out_shape.append(jax.ShapeDtypeStruct(r.arr.shape, F32))
            out_specs.append(_row_spec(r, t, nchunk, True))
    for k in dvi:
        out_shape.append(jax.ShapeDtypeStruct(vecs[k].arr.shape, F32))
        out_specs.append(_vec_spec(vecs[k]))
    res = pl.pallas_call(
        body, name=name, grid=(nb, nchunk), in_specs=in_specs, out_specs=out_specs, out_shape=out_shape,
        scratch_shapes=[pltpu.VMEM(tuple(cs), F32) for cs in carries],
        compiler_params=_cparams(),
    )(*[r.arr for r in rows], *[v.arr for v in vecs], *saved, *[d.arr for d in douts], *[adds[k].arr for k in add_keys])
    return list(res[:len(dri)]), list(res[len(dri):])


def out_row(shape, dtype=F32, w=None, fb=None, fc=None):
    return Row(jax.ShapeDtypeStruct(shape, dtype), w, fb, fc)


def _conv(shift, halo, cur, w, bias, taps):
    y = bias
    for k in range(taps):
        y = y + _rowk(w, k) * shift(halo, cur, taps - 1 - k)
    return y


def _ssd_fn(ci, b, carries, rows, vecs):
    cx, cb_, cc, ht = carries
    z, xr, br, cr, dtr = rows
    cwx, cbx, cwb, cbb, cwc, cbc, dtb, alog, dsk, ng, e = vecs
    t = z.shape[0]
    xs = _silu(_conv(_shift8, cx, xr, cwx, cbx, 4))
    bm = _silu(_conv(_shift8, cb_, br, cwb, cbb, 4))
    cm = _silu(_conv(_shift8, cc, cr, cwc, cbc, 4))
    dt = _softplus(dtr + dtb)
    da = dt * (-jnp.exp(alog))
    r, c = _iota((t, t), 0), _iota((t, t), 1)
    causal = r >= c
    acol = _fdot(causal.astype(F32), da, "nn")
    arow = _fdot(da, (r <= c).astype(F32), "tn")
    a = _fdot(acol, e, "nn")
    dtx = _fdot(dt, e, "nn")
    atot = jnp.sum(jnp.where(_iota(a.shape, 0) == t - 1, a, 0.0), axis=0, keepdims=True)
    x = xs * dtx
    cbm = _bdot(cm, bm, "nt")
    lane, sub = _iota(acol.shape, 1), _iota(arow.shape, 0)
    colh = _iota(x.shape, 1) // 64
    ydiag = jnp.zeros(x.shape, F32)
    for j in range(4):
        h = 4 * b + j
        ac = jnp.sum(jnp.where(lane == h, acol, 0.0), axis=1, keepdims=True)
        ar = jnp.sum(jnp.where(sub == h, arow, 0.0), axis=0, keepdims=True)
        lmat = jnp.exp(jnp.where(causal, ac - ar, NEG))
        ydiag = ydiag + _bdot(cbm * lmat, jnp.where(colh == j, x, 0.0), "nn")
    yoff = _bdot(cm, ht, "nn") * jnp.exp(a)
    ht_new = ht * jnp.exp(atot) + _bdot(bm, x * jnp.exp(atot - a), "tn")
    dx = jnp.sum(_fdot(jnp.broadcast_to(dsk, (8, dsk.shape[1])), e, "nn"), axis=0, keepdims=True) * 0.125
    y = ydiag + yoff + dx * xs
    yz = y * _silu(z)
    yn = yz * lax.rsqrt(jnp.mean(yz * yz, axis=-1, keepdims=True) + NORM_EPS) * ng
    return [_tail8(xr), _tail8(br), _tail8(cr), ht_new], [yn]


_SSD_T = 256
_SSD_CARRIES = [(8, 256), (8, 128), (8, 128), (128, 256)]


def _ssd_io(proj3, p):
    own = lambda b: b
    rows = [Row(proj3, 256, fc=own, dcols=512, dfc=own), Row(proj3, 256, fc=lambda b: 2 + b, dcols=512, dfc=own),
            Row(proj3, 128, fc=lambda b: 8 + b, dcols=256, dfc=own), Row(proj3, 128, fc=lambda b: 10 + b, dcols=256, dfc=own),
            Row(proj3, 128, fc=lambda b: 20, slot=True)]
    vecs = [Vec(p["cw"], 256, lambda b: b), Vec(p["cb"], 256, lambda b: b),
            Vec(p["cw"], 128, lambda b: 4 + b), Vec(p["cb"], 128, lambda b: 4 + b),
            Vec(p["cw"], 128, lambda b: 6 + b), Vec(p["cb"], 128, lambda b: 6 + b),
            Vec(p["dtb"]), Vec(p["alog"]), Vec(p["dsk"]), Vec(p["ng"], 256, lambda b: b),
            Vec(p["e"], 256, lambda b: b, diff=False)]
    return rows, vecs


def ssd_forward(name, proj3, p):
    rows, vecs = _ssd_io(proj3, p)
    s = proj3.shape[1]
    (y,), saved = scan_fwd(name, _ssd_fn, nb=2, nchunk=s // _SSD_T, t=_SSD_T, rows=rows, vecs=vecs,
                           carries=_SSD_CARRIES, outs=[out_row((1, s, SSD_INNER), F32, 256, fc=lambda b: b)], save=True)
    return y, saved


def ssd_backward(name, proj3, p, saved, dmix3):
    rows, vecs = _ssd_io(proj3, p)
    s = proj3.shape[1]
    drows, dvecs = scan_bwd(name, _ssd_fn, nb=2, nchunk=s // _SSD_T, t=_SSD_T, rows=rows, vecs=vecs,
                            carries=_SSD_CARRIES, saved=saved, douts=[Row(dmix3, 256, fc=lambda b: b)])
    return drows, dvecs


def _pool_fn(ci, b, carries, rows, vecs):
    (cu,) = carries
    (u,) = rows
    wbd, scale = vecs
    t = u.shape[0]
    pos = ci * t + _iota(u.shape, 0)
    grp = _iota(u.shape, 1) // 64
    acc, pooled, k = u, jnp.zeros(u.shape, F32), 1
    for gi, w in enumerate(POOL_WINDOWS):
        while k < w:
            acc = acc + _shift16(cu, u, k)
            k += 1
        pooled = jnp.where(grp == gi, acc / jnp.minimum(pos + 1, w).astype(F32), pooled)
    y = _bdot(pooled - u, wbd, "nn") * scale
    return [_tail16(u)], [y]


_POOL_T = 256


def _pool_io(proj3, wbd, scale):
    return [Row(proj3, 256, fc=lambda b: 6, dcols=256, dfc=lambda b: 0)], [Vec(wbd), Vec(scale)]


def pool_forward(name, proj3, wbd, scale):
    rows, vecs = _pool_io(proj3, wbd, scale)
    s = proj3.shape[1]
    (y,), saved = scan_fwd(name, _pool_fn, nb=1, nchunk=s // _POOL_T, t=_POOL_T, rows=rows, vecs=vecs,
                           carries=[(16, 256)], outs=[out_row((1, s, POOL_W))], save=True)
    return y, saved


def pool_backward(name, proj3, wbd, scale, saved, dmix3):
    rows, vecs = _pool_io(proj3, wbd, scale)
    s = proj3.shape[1]
    return scan_bwd(name, _pool_fn, nb=1, nchunk=s // _POOL_T, t=_POOL_T, rows=rows, vecs=vecs,
                    carries=[(16, 256)], saved=saved, douts=[Row(dmix3, 256, fc=lambda b: 2)])


def _attn_fn(ci, b, carries, rows, vecs):
    kp, vp = carries
    q, k, v, cs, sn = rows
    (rot,) = vecs
    qr = q * cs + _fdot(q, rot, "nn") * sn
    kr = k * cs + _fdot(k, rot, "nn") * sn
    scale = ATT_HEAD_DIM ** -0.5
    n = q.shape[0]
    r, c = _iota((n, n), 0), _iota((n, n), 1)
    prev_ok, cur_ok = jnp.logical_and(c >= r, ci > 0), r >= c
    head = _iota(q.shape, 1) // ATT_HEAD_DIM
    o, lse = jnp.zeros(q.shape, F32), jnp.zeros(q.shape, F32)
    for h in range(ATT_HEADS):
        mine = head == h
        qh = jnp.where(mine, qr, 0.0)
        sp = jnp.where(prev_ok, _bdot(qh, kp, "nt") * scale, NEG)
        sc = jnp.where(cur_ok, _bdot(qh, kr, "nt") * scale, NEG)
        m = lax.stop_gradient(jnp.maximum(jnp.max(sp, axis=1, keepdims=True), jnp.max(sc, axis=1, keepdims=True)))
        pp, pc = jnp.exp(sp - m), jnp.exp(sc - m)
        l = jnp.sum(pp, axis=1, keepdims=True) + jnp.sum(pc, axis=1, keepdims=True)
        o = jnp.where(mine, (_bdot(pp, vp, "nn") + _bdot(pc, v, "nn")) / l, o)
        lse = jnp.where(mine, m + jnp.log(l), lse)
    return [kr, v], [o, lse]


_ATT_CARRIES = [(ATT_BLOCK, ATT_W), (ATT_BLOCK, ATT_W)]
_Q_BLOCK = 7


def _attn_io(pv, cv, sv, rot, d):
    per_row = IN_WP // ATT_W
    own = lambda b: b
    rows = [Row(pv, ATT_W, fc=(lambda b, j=j: b * per_row + _Q_BLOCK + j), dcols=d * ATT_W, dfc=own) for j in range(3)]
    rows += [Row(cv, ATT_W, fc=own, diff=False), Row(sv, ATT_W, fc=own, diff=False)]
    return rows, [Vec(rot, diff=False)]


def attn_forward(name, pv, cv, sv, rot, d):
    rows, vecs = _attn_io(pv, cv, sv, rot, d)
    l = pv.shape[1]
    own = lambda b: b
    outs = [out_row((1, l, d * ATT_W), F32, ATT_W, fc=own) for _ in range(2)]
    (o, lse), saved = scan_fwd(name, _attn_fn, nb=d, nchunk=l // ATT_BLOCK, t=ATT_BLOCK, rows=rows, vecs=vecs,
                               carries=_ATT_CARRIES, outs=outs, save=True)
    return o, lse, saved


def attn_backward(name, pv, cv, sv, rot, d, saved, do, dlse):
    rows, vecs = _attn_io(pv, cv, sv, rot, d)
    l = pv.shape[1]
    own = lambda b: b
    drows, _ = scan_bwd(name, _attn_fn, nb=d, nchunk=l // ATT_BLOCK, t=ATT_BLOCK, rows=rows, vecs=vecs,
                        carries=_ATT_CARRIES, saved=saved, douts=[Row(do, ATT_W, fc=own), Row(dlse, ATT_W, fc=own)])
    return drows


def _merge_fn(ci, b, carries, rows, vecs):
    o1, o2, o3, l1, l2, l3 = rows
    mx = lax.stop_gradient(jnp.maximum(l1, jnp.maximum(l2, l3)))
    e1, e2, e3 = jnp.exp(l1 - mx), jnp.exp(l2 - mx), jnp.exp(l3 - mx)
    return [], [(e1 * o1 + e2 * o2 + e3 * o3) / (e1 + e2 + e3)]


_ROW_T = 256


def merge_forward(name, os_, ls_):
    s = os_[0].shape[1]
    (y,), _ = scan_fwd(name, _merge_fn, nb=1, nchunk=s // _ROW_T, t=_ROW_T, rows=[Row(a) for a in (*os_, *ls_)], vecs=[],
                       carries=[], outs=[out_row((1, s, ATT_W))], save=False)
    return y


def merge_backward(name, os_, ls_, dmix3):
    s = os_[0].shape[1]
    drows, _ = scan_bwd(name, _merge_fn, nb=1, nchunk=s // _ROW_T, t=_ROW_T, rows=[Row(a) for a in (*os_, *ls_)], vecs=[],
                        carries=[], saved=[], douts=[Row(dmix3, 256, fc=lambda b: 3)])
    return drows


def _norm_mod_fn(ci, b, carries, rows, vecs):
    (x,) = rows
    g, sc, sh = vecs
    xn = x * lax.rsqrt(jnp.mean(x * x, axis=-1, keepdims=True) + NORM_EPS)
    return [], [xn * g * (1.0 + sc) + sh]


def norm_mod_forward(name, x3, g, sc, sh):
    s = x3.shape[1]
    (h,), _ = scan_fwd(name, _norm_mod_fn, nb=1, nchunk=s // _ROW_T, t=_ROW_T, rows=[Row(x3)], vecs=[Vec(g), Vec(sc), Vec(sh)],
                       carries=[], outs=[out_row(x3.shape, BF16)], save=False)
    return h


def norm_mod_backward(name, x3, g, sc, sh, dh3, add3):
    s = x3.shape[1]
    (dx,), dv = scan_bwd(name, _norm_mod_fn, nb=1, nchunk=s // _ROW_T, t=_ROW_T, rows=[Row(x3)], vecs=[Vec(g), Vec(sc), Vec(sh)],
                         carries=[], saved=[], douts=[Row(dh3)], adds={0: Row(add3)})
    return dx, dv


def _gate_fn(ci, b, carries, rows, vecs):
    return [], [rows[0] * vecs[0]]


def gate_backward(name, o3, g, dx3):
    s = o3.shape[1]
    (do,), (dg,) = scan_bwd(name, _gate_fn, nb=1, nchunk=s // _ROW_T, t=_ROW_T, rows=[Row(o3)], vecs=[Vec(g)],
                            carries=[], saved=[], douts=[Row(dx3)])
    return do, dg


def _ffn_fn(ci, b, carries, rows, vecs):
    cg, cu = carries
    ug, uu = rows
    wg, bg, wu, bu = vecs
    hg = _conv(_shift8, cg, ug, wg, bg, 3)
    hu = _conv(_shift8, cu, uu, wu, bu, 3)
    return [_tail8(ug), _tail8(uu)], [_silu(hg) * hu]


_FFN_T = 256
_FFN_CW = FFN_DIM // 2
_FFN_CARRIES = [(8, _FFN_CW), (8, _FFN_CW)]


def _ffn_io(up3, cw, cb):
    lo, hi = (lambda b: b), (lambda b: 2 + b)
    rows = [Row(up3, _FFN_CW, fc=lo, dcols=FFN_DIM, dfc=lo), Row(up3, _FFN_CW, fc=hi, dcols=FFN_DIM, dfc=lo)]
    vecs = [Vec(cw, _FFN_CW, lo), Vec(cb, _FFN_CW, lo), Vec(cw, _FFN_CW, hi), Vec(cb, _FFN_CW, hi)]
    return rows, vecs


def ffn_mid_forward(name, up3, cw, cb):
    rows, vecs = _ffn_io(up3, cw, cb)
    s = up3.shape[1]
    (act,), saved = scan_fwd(name, _ffn_fn, nb=2, nchunk=s // _FFN_T, t=_FFN_T, rows=rows, vecs=vecs, carries=_FFN_CARRIES,
                             outs=[out_row((1, s, FFN_DIM), BF16, _FFN_CW, fc=lambda b: b)], save=True)
    return act, saved


def ffn_mid_backward(name, up3, cw, cb, saved, dact3):
    rows, vecs = _ffn_io(up3, cw, cb)
    s = up3.shape[1]
    return scan_bwd(name, _ffn_fn, nb=2, nchunk=s // _FFN_T, t=_FFN_T, rows=rows, vecs=vecs, carries=_FFN_CARRIES,
                    saved=saved, douts=[Row(dact3, _FFN_CW, fc=lambda b: b)])


def _adam_fn(ci, b, carries, rows, vecs):
    w, g, m, v = rows
    m = ADAM_B1 * m + (1.0 - ADAM_B1) * g
    v = ADAM_B2 * v + (1.0 - ADAM_B2) * (g * g)
    m_hat = m / (1.0 - ADAM_B1 ** ADAM_STEP)
    v_hat = v / (1.0 - ADAM_B2 ** ADAM_STEP)
    delta = -ADAM_LR * (m_hat / (jnp.sqrt(v_hat) + ADAM_EPS) + ADAM_WD * w)
    return [], [delta, m, v]


def adamw(name, w, g, m, v):
    shape = w.shape
    c = shape[-1]
    r = int(np.prod(shape[:-1]))
    t = _tile(r, 256, 8)
    as3 = lambda a: a.reshape(1, r, c)
    outs, _ = scan_fwd(name, _adam_fn, nb=1, nchunk=r // t, t=t, rows=[Row(as3(a)) for a in (w, g, m, v)], vecs=[], carries=[],
                       outs=[out_row((1, r, c)) for _ in range(3)], save=False)
    return [o.reshape(shape) for o in outs]


def rope_tables(positions):
    half = ROT_DIM // 2
    inv_freq = ROPE_THETA ** (-jnp.arange(0, ROT_DIM, 2, dtype=F32) / ROT_DIM)
    ang = positions.astype(F32)[:, None] * inv_freq
    s = positions.shape[0]
    cs = jnp.concatenate([jnp.cos(ang), jnp.cos(ang), jnp.ones((s, ATT_HEAD_DIM - ROT_DIM), F32)], axis=1)
    sn = jnp.concatenate([jnp.sin(ang), jnp.sin(ang), jnp.zeros((s, ATT_HEAD_DIM - ROT_DIM), F32)], axis=1)
    rot = np.zeros((ATT_W, ATT_W), np.float32)
    for h in range(ATT_HEADS):
        for e in range(half):
            rot[h * ATT_HEAD_DIM + e + half, h * ATT_HEAD_DIM + e] = -1.0
            rot[h * ATT_HEAD_DIM + e, h * ATT_HEAD_DIM + e + half] = 1.0
    return jnp.tile(cs, (1, ATT_HEADS)), jnp.tile(sn, (1, ATT_HEADS)), jnp.asarray(rot)


def attention_forward(lname, proj3, cs, sn, rot):
    s = proj3.shape[1]
    os_, ls_, keep = [], [], []
    for pi, (_, d) in enumerate(ATT_PATTERNS):
        view = lambda a: a.reshape(1, s // d, d * a.shape[-1])
        o, lse, saved = attn_forward(f"{lname}_attn{pi}", view(proj3), view(cs), view(sn), rot, d)
        os_.append(o.reshape(1, s, ATT_W))
        ls_.append(lse.reshape(1, s, ATT_W))
        keep.append(saved)
    y = merge_forward(f"{lname}_merge", os_, ls_)
    return y, (os_, ls_, keep)


def attention_backward(lname, proj3, cs, sn, rot, res, dmix3):
    os_, ls_, keep = res
    s = proj3.shape[1]
    dm = merge_backward(f"{lname}_merge_b", os_, ls_, dmix3)
    tot = None
    for pi, (_, d) in enumerate(ATT_PATTERNS):
        view = lambda a: a.reshape(1, s // d, d * a.shape[-1])
        dqkv = attn_backward(f"{lname}_attn{pi}_b", view(proj3), view(cs), view(sn), rot, d, keep[pi], view(dm[pi]), view(dm[3 + pi]))
        dqkv = [a.reshape(s, ATT_W) for a in dqkv]
        tot = dqkv if tot is None else [a + b for a, b in zip(tot, dqkv)]
    return tot


def mm(name, a, b, mode, out_dtype=F32, res=None, gate=None, tm=1024, tn=1536, tk=1024):
    if mode == "nn":
        (m, k), n = a.shape, b.shape[1]
    elif mode == "nt":
        (m, k), n = a.shape, b.shape[0]
    else:
        (k, m), n = a.shape, b.shape[1]
    tm, tn, tk = _tile(m, tm), _tile(n, tn), _tile(k, tk)
    nk = k // tk
    a_spec = pl.BlockSpec((tk, tm), lambda i, j, q: (q, i)) if mode == "tn" else pl.BlockSpec((tm, tk), lambda i, j, q: (i, q))
    b_spec = pl.BlockSpec((tn, tk), lambda i, j, q: (j, q)) if mode == "nt" else pl.BlockSpec((tk, tn), lambda i, j, q: (q, j))
    o_spec = pl.BlockSpec((tm, tn), lambda i, j, q: (i, j))
    fused = res is not None

    def body(*refs):
        if fused:
            a_ref, b_ref, r_ref, g_ref, o_ref, o2_ref, acc = refs
        else:
            a_ref, b_ref, o_ref, acc = refs
        q = pl.program_id(2)

        @pl.when(q == 0)
        def _():
            acc[...] = jnp.zeros(acc.shape, F32)

        acc[...] += _mxu(a_ref[...], b_ref[...], mode)

        @pl.when(q == nk - 1)
        def _():
            o_ref[...] = acc[...].astype(o_ref.dtype)
            if fused:
                o2_ref[...] = r_ref[...] + g_ref[...] * acc[...]

    ins, in_specs = [a, b], [a_spec, b_spec]
    out_shape, out_specs = [jax.ShapeDtypeStruct((m, n), out_dtype)], [o_spec]
    if fused:
        ins += [res, gate]
        in_specs += [o_spec, pl.BlockSpec((1, tn), lambda i, j, q: (0, j))]
        out_shape.append(jax.ShapeDtypeStruct((m, n), F32))
        out_specs.append(o_spec)
    out = pl.pallas_call(
        body, name=name, grid=(m // tm, n // tn, nk), in_specs=in_specs, out_specs=out_specs, out_shape=out_shape,
        scratch_shapes=[pltpu.VMEM((tm, tn), F32)],
        compiler_params=pltpu.CompilerParams(dimension_semantics=("parallel", "parallel", "arbitrary"),
                                             vmem_limit_bytes=VMEM_LIMIT_BYTES),
    )(*ins)
    return tuple(out) if fused else out[0]


def final_loss(name, x3, t3, g):
    s, d = x3.shape[1], x3.shape[2]
    t = _ROW_T

    def body(x_ref, t_ref, g_ref, loss_ref, dx_ref, dg_ref):
        i = pl.program_id(0)
        tv = t_ref[0]

        def f(x, gg):
            y = x * lax.rsqrt(jnp.mean(x * x, axis=-1, keepdims=True) + NORM_EPS) * gg
            e = y - tv
            return 0.5 * jnp.sum(jnp.mean(e * e, axis=-1, keepdims=True), axis=0, keepdims=True)

        l, vjp = jax.vjp(f, x_ref[0], g_ref[...])
        dx, dg = vjp(jnp.ones((1, 1), F32))
        dx_ref[0] = dx

        @pl.when(i == 0)
        def _():
            loss_ref[...] = jnp.zeros(loss_ref.shape, F32)
            dg_ref[...] = jnp.zeros(dg_ref.shape, F32)

        loss_ref[...] += jnp.broadcast_to(l, loss_ref.shape)
        dg_ref[...] += dg

    row = pl.BlockSpec((1, t, d), lambda i: (0, i, 0))
    vec = pl.BlockSpec((1, d), lambda i: (0, 0))
    return pl.pallas_call(
        body, name=name, grid=(s // t,), in_specs=[row, row, vec],
        out_specs=[pl.BlockSpec((8, 128), lambda i: (0, 0)), row, vec],
        out_shape=[jax.ShapeDtypeStruct((8, 128), F32), jax.ShapeDtypeStruct(x3.shape, F32), jax.ShapeDtypeStruct((1, d), F32)],
        compiler_params=pltpu.CompilerParams(dimension_semantics=("arbitrary",), vmem_limit_bytes=VMEM_LIMIT_BYTES),
    )(x3, t3, g)


_ADA_TN = 512


def ada_forward(name, c16, ada_w):
    depth, d, cols = ada_w.shape

    def body(c_ref, w_ref, o_ref):
        o_ref[0] = _mxu(_silu(c_ref[...]), w_ref[0], "nn")

    return pl.pallas_call(
        body, name=name, grid=(depth, cols // _ADA_TN),
        in_specs=[pl.BlockSpec((16, d), lambda l, j: (0, 0)), pl.BlockSpec((1, d, _ADA_TN), lambda l, j: (l, 0, j))],
        out_specs=pl.BlockSpec((1, 16, _ADA_TN), lambda l, j: (l, 0, j)),
        out_shape=jax.ShapeDtypeStruct((depth, 16, cols), F32),
        compiler_params=pltpu.CompilerParams(dimension_semantics=("arbitrary", "arbitrary"), vmem_limit_bytes=VMEM_LIMIT_BYTES),
    )(c16, ada_w)


def ada_backward(name, c16, dmod16, w, m, v):
    depth, d, cols = w.shape

    def body(c_ref, dm_ref, w_ref, m_ref, v_ref, g_ref, dl_ref, nm_ref, nv_ref):
        g = _mxu(_silu(c_ref[...]), dm_ref[0], "tn")
        _, (delta, nm, nv) = _adam_fn(None, None, [], [w_ref[0], g, m_ref[0], v_ref[0]], [])
        g_ref[0], dl_ref[0], nm_ref[0], nv_ref[0] = g, delta, nm, nv

    blk = pl.BlockSpec((1, d, _ADA_TN), lambda l, j: (l, 0, j))
    return pl.pallas_call(
        body, name=name, grid=(depth, cols // _ADA_TN),
        in_specs=[pl.BlockSpec((16, d), lambda l, j: (0, 0)), pl.BlockSpec((1, 16, _ADA_TN), lambda l, j: (l, 0, j)), blk, blk, blk],
        out_specs=[blk] * 4, out_shape=[jax.ShapeDtypeStruct(w.shape, F32)] * 4,
        compiler_params=pltpu.CompilerParams(dimension_semantics=("arbitrary", "arbitrary"), vmem_limit_bytes=VMEM_LIMIT_BYTES),
    )(c16, dmod16, w, m, v)


def _sum_fn(ci, b, carries, rows, vecs):
    acc = rows[0]
    for r in rows[1:]:
        acc = acc + r
    return [], [acc]


def sum_slots(name, a, nsum, out_dtype=F32):
    n, r, c = a.shape
    nb = n // nsum
    t = _tile(r, 256, 8)
    rows = [Row(a, fb=(lambda b, k=k: k * nb + b)) for k in range(nsum)]
    (out,), _ = scan_fwd(name, _sum_fn, nb=nb, nchunk=r // t, t=t, rows=rows, vecs=[], carries=[],
                         outs=[out_row((nb, r, c), out_dtype, fb=lambda b: b)], save=False)
    return out


def _sum_my_layer_fn(ci, b, carries, rows, vecs):
    layer0, layer1, theirs = rows
    return [], [jnp.where(lax.axis_index("c") == 0, layer0, layer1) + theirs]


def sum_cores(name, g, theirs, out_dtype):
    _, nb, r, c = g.shape
    g8 = g.reshape(2 * nb, r, c)
    t = _tile(r, 256, 8)
    rows = [Row(g8, fb=lambda b: b), Row(g8, fb=lambda b: nb + b), Row(theirs, fb=lambda b: b)]
    (out,), _ = scan_fwd(name, _sum_my_layer_fn, nb=nb, nchunk=r // t, t=t, rows=rows, vecs=[], carries=[],
                         outs=[out_row((nb, r, c), out_dtype, fb=lambda b: b)], save=False)
    return out


def _flip(mask, pos):
    return tuple((1 - p) if m else p for m, p in zip(mask, pos))


ALL_PEERS = [(a, b, c) for a in (0, 1) for b in (0, 1) for c in (0, 1)][1:]
CHIP_PEERS = [(1, 0, 0), (0, 1, 0), (1, 1, 0)]
SIBLING = [(0, 0, 1)]


def _divisor(size, target, unit):
    best = 1
    for n in range(1, target + 1):
        if size % n == 0 and (size // n) % unit == 0:
            best = n
    return best


def _pieces(src, dst, pieces):
    shape = src.shape
    unit = 16 if src.dtype == BF16 else 8
    if pieces <= 1:
        return [(src, dst)]
    if len(shape) == 2:
        n = _divisor(shape[0], pieces, unit)
        s = shape[0] // n
        return [(src.at[pl.ds(i * s, s)], dst.at[pl.ds(i * s, s)]) for i in range(n)]
    assert len(shape) == 3, shape
    n = _divisor(shape[1], max(pieces // shape[0], 1), unit)
    s = shape[1] // n
    return [(src.at[j, pl.ds(i * s, s)], dst.at[j, pl.ds(i * s, s)]) for j in range(shape[0]) for i in range(n)]


def comm_call(name, arrays, out_shapes, masks, src_fn, dst_fn, local_fn=None, pieces=1):
    na, npeer = len(arrays), len(masks)

    def body(*refs):
        ins, outs = refs[:na], refs[na:2 * na]
        send_sems, recv_sems, loc_sems = refs[2 * na:]
        me = (lax.axis_index("x"), lax.axis_index("y"), lax.axis_index("c"))
        local = []
        if local_fn is not None:
            for k in range(na):
                s, d = local_fn(k, ins[k], outs[k], me)
                for ps, pd in _pieces(s, d, pieces):
                    pltpu.make_async_copy(ps, pd, loc_sems.at[k]).start()
                local.append(pltpu.make_async_copy(s, d, loc_sems.at[k]))

        def remote(k, p, src, dst, to):
            return pltpu.make_async_remote_copy(
                src_ref=src, dst_ref=dst, send_sem=send_sems.at[k * npeer + p], recv_sem=recv_sems.at[k * npeer + p],
                device_id=to, device_id_type=MESH)

        for k in range(na):
            for p in range(npeer):
                peer = _flip(masks[p], me)
                for ps, pd in _pieces(src_fn(k, ins[k], me, peer), dst_fn(k, outs[k], me), pieces):
                    remote(k, p, ps, pd, peer).start()
        for k in range(na):
            for p in range(npeer):
                peer = _flip(masks[p], me)
                remote(k, p, src_fn(k, ins[k], me, peer), dst_fn(k, outs[k], peer), peer).wait_recv()
        for k in range(na):
            for p in range(npeer):
                peer = _flip(masks[p], me)
                remote(k, p, src_fn(k, ins[k], me, peer), dst_fn(k, outs[k], me), peer).wait_send()
        for cp in local:
            cp.wait()

    hbm = pl.BlockSpec(memory_space=pl.ANY)
    out = pl.pallas_call(
        body, name=name, in_specs=[hbm] * na, out_specs=[hbm] * na,
        out_shape=[jax.ShapeDtypeStruct(s, a.dtype) for s, a in zip(out_shapes, arrays)],
        scratch_shapes=[pltpu.SemaphoreType.DMA((na * npeer,)), pltpu.SemaphoreType.DMA((na * npeer,)),
                        pltpu.SemaphoreType.DMA((na,))],
    )(*arrays)
    return list(out)


def _dev(pos):
    return 4 * pos[0] + 2 * pos[1] + pos[2]


def _chip(pos):
    return 2 * pos[0] + pos[1]


def allgather8(name, a):
    (out,) = comm_call(name, [a], [(8,) + a.shape], ALL_PEERS,
                       src_fn=lambda k, r, me, peer: r, dst_fn=lambda k, o, sender: o.at[_dev(sender)],
                       local_fn=lambda k, r, o, me: (r, o.at[_dev(me)]))
    return out


def gather_layer_from_chips(name, arrays):
    return comm_call(name, arrays, [(4,) + a.shape[1:] for a in arrays], CHIP_PEERS,
                     src_fn=lambda k, r, me, peer: r.at[me[2]], dst_fn=lambda k, o, sender: o.at[_chip(sender)],
                     local_fn=lambda k, r, o, me: (r.at[me[2]], o.at[_chip(me)]), pieces=8)


def swap_layers(name, arrays, c):
    got = comm_call(name, arrays, [a.shape for a in arrays], SIBLING,
                    src_fn=lambda k, r, me, peer: r, dst_fn=lambda k, o, sender: o, pieces=32)
    return [[jnp.where(c == 0, a, g), jnp.where(c == 0, g, a)] for a, g in zip(arrays, got)]


def swap_other_layer(name, arrays):
    return comm_call(name, arrays, [a.shape[1:] for a in arrays], SIBLING,
                     src_fn=lambda k, r, me, peer: r.at[peer[2]], dst_fn=lambda k, o, sender: o, pieces=32)


def scatter_to_chips(name, arrays):
    return comm_call(name, arrays, [a.shape for a in arrays], CHIP_PEERS,
                     src_fn=lambda k, r, me, peer: r.at[_chip(peer)], dst_fn=lambda k, o, sender: o.at[_chip(sender)],
                     local_fn=lambda k, r, o, me: (r.at[_chip(me)], o.at[_chip(me)]), pieces=8)


def _pack(arrs):
    flat = jnp.concatenate([a.reshape(-1).astype(F32) for a in arrs])
    n = flat.shape[0]
    pad = (-n) % (_ROW_T * 128)
    return jnp.pad(flat, (0, pad)).reshape(-1, 128)


def _unpack(buf, shapes):
    flat = buf.reshape(-1)
    out, o = [], 0
    for s in shapes:
        n = int(np.prod(s))
        out.append(flat[o:o + n].reshape(s))
        o += n
    return out


_WEIGHTS = ["ada_w", "ada_b", "norm1_g", "w_in", "ssd_conv_w", "ssd_conv_b", "ssd_dt_bias", "ssd_a_log", "ssd_d", "ssd_norm_g",
            "pool_w", "pool_scale", "w_out", "norm2_g", "ffn_up", "ffn_conv_w", "ffn_conv_b", "ffn_down", "final_g"]
_BIG = ["w_in", "w_out", "ffn_up", "ffn_down"]
_SMALL = [n for n in _WEIGHTS if n not in _BIG and n != "ada_w"]
_COL_SHARDED_SMALL = {"ssd_conv_w": 256, "ffn_conv_w": 1408}


def _pad_lanes(v, n=128):
    return jnp.pad(v.astype(F32), (0, n - v.shape[0]))[None]


def _perm_cols(w):
    pad = jnp.zeros(w.shape[:-1] + (IN_WP - IN_W,), w.dtype)
    return jnp.concatenate([w[..., :1536], w[..., 1544:], w[..., 1536:1544], pad], axis=-1)


def _unperm_cols(g):
    return jnp.concatenate([g[..., :1536], g[..., 2560:2568], g[..., 1536:2560]], axis=-1)


def _cols_by_chip(g, ncol):
    return g.reshape(g.shape[0], 4, ncol).transpose(1, 0, 2)


def _layer_forward(i, x3, modv, wts, sp, cs, sn, rot):
    sh1, sc1, g1, sh2, sc2, g2 = modv
    s = x3.shape[1]
    h1 = norm_mod_forward(f"l{i}_norm1", x3, wts["norm1_g"], sc1, sh1)
    proj3 = mm(f"l{i}_proj", h1[0], wts["w_in"], "nn")[None]
    y_ssd, sv_ssd = ssd_forward(f"l{i}_ssd", proj3, sp)
    y_pool, sv_pool = pool_forward(f"l{i}_pool", proj3, wts["wbd"], wts["pool_scale"])
    y_att, res_att = attention_forward(f"l{i}", proj3, cs, sn, rot)
    mix = jnp.concatenate([y_ssd, y_pool, y_att], axis=-1)
    out, x1 = mm(f"l{i}_wout", mix[0], wts["w_out"], "nn", res=x3[0], gate=g1)
    x1 = x1[None]
    h2 = norm_mod_forward(f"l{i}_norm2", x1, wts["norm2_g"], sc2, sh2)
    up3 = mm(f"l{i}_up", h2[0], wts["ffn_up"], "nn")[None]
    act, sv_ffn = ffn_mid_forward(f"l{i}_ffn", up3, wts["ffn_conv_w"], wts["ffn_conv_b"])
    dn, x2 = mm(f"l{i}_down", act[0], wts["ffn_down"], "nn", res=x1[0], gate=g2)
    keep = dict(x=x3, h1=h1, proj3=proj3, sv_ssd=sv_ssd, sv_pool=sv_pool, res_att=res_att, mix=mix, out=out[None], x1=x1, h2=h2,
                up3=up3, act=act, sv_ffn=sv_ffn, dn=dn[None])
    return x2[None], keep


def _layer_backward(i, dx2, keep, modv, wts, sp, cs, sn, rot):
    sh1, sc1, g1, sh2, sc2, g2 = modv
    k = keep
    d_dn, d_g2 = gate_backward(f"l{i}_gate2_b", k["dn"], g2, dx2)
    d_act = mm(f"l{i}_down_bx", d_dn[0], wts["ffn_down"], "nt")
    g_down = mm(f"l{i}_down_bw", k["act"][0], d_dn[0], "tn")
    (dg_, du_), dv_ffn = ffn_mid_backward(f"l{i}_ffn_b", k["up3"], wts["ffn_conv_w"], wts["ffn_conv_b"], k["sv_ffn"], d_act[None])
    d_up = jnp.concatenate([dg_[0], du_[0]], axis=-1)
    d_h2 = mm(f"l{i}_up_bx", d_up, wts["ffn_up"], "nt")
    g_up = mm(f"l{i}_up_bw", k["h2"][0], d_up, "tn")
    dx1, (d_n2, d_sc2, d_sh2) = norm_mod_backward(f"l{i}_norm2_b", k["x1"], wts["norm2_g"], sc2, sh2, d_h2[None], dx2)
    d_out, d_g1 = gate_backward(f"l{i}_gate1_b", k["out"], g1, dx1)
    d_mix = mm(f"l{i}_wout_bx", d_out[0], wts["w_out"], "nt")[None]
    g_wout = mm(f"l{i}_wout_bw", k["mix"][0], d_out[0], "tn")
    (dz, dxs, dbm, dcm, ddt), dv_ssd = ssd_backward(f"l{i}_ssd_b", k["proj3"], sp, k["sv_ssd"], d_mix)
    (du_pool,), (d_wbd, d_pscale) = pool_backward(f"l{i}_pool_b", k["proj3"], wts["wbd"], wts["pool_scale"], k["sv_pool"], d_mix)
    dq, dk, dv = attention_backward(f"l{i}", k["proj3"], cs, sn, rot, k["res_att"], d_mix)
    d_proj = jnp.concatenate([dz[0], dxs[0], dbm[0], dcm[0], du_pool[0], dq, dk, dv, ddt[0] + ddt[1],
                              jnp.zeros((dq.shape[0], IN_WP - IN_USED), F32)], axis=-1)
    d_h1 = mm(f"l{i}_proj_bx", d_proj, wts["w_in"], "nt")
    g_win = mm(f"l{i}_proj_bw", k["h1"][0], d_proj, "tn")
    dx, (d_n1, d_sc1, d_sh1) = norm_mod_backward(f"l{i}_norm1_b", k["x"], wts["norm1_g"], sc1, sh1, d_h1[None], dx1)
    dcwx, dcbx, dcwb, dcbb, dcwc, dcbc, ddtb, dalog, ddsk, dng = dv_ssd
    small = dict(
        norm1_g=d_n1[0], norm2_g=d_n2[0],
        ssd_conv_w=jnp.concatenate([dcwx[:, :512], dcwb[:, 512:768], dcwc[:, 768:]], axis=1),
        ssd_conv_b=jnp.concatenate([dcbx[0, :512], dcbb[0, 512:768], dcbc[0, 768:]]),
        ssd_dt_bias=ddtb[0, :8], ssd_a_log=dalog[0, :8], ssd_d=ddsk[0, :8], ssd_norm_g=dng[0],
        pool_w=jnp.stack([d_wbd[64 * g:64 * g + 64, 64 * g:64 * g + 64] for g in range(4)]), pool_scale=d_pscale[0],
        ffn_conv_w=jnp.concatenate([dv_ffn[0][:, :FFN_DIM], dv_ffn[2][:, FFN_DIM:]], axis=1),
        ffn_conv_b=jnp.concatenate([dv_ffn[1][0, :FFN_DIM], dv_ffn[3][0, FFN_DIM:]]),
    )
    dmod = jnp.concatenate([d_sh1[0], d_sc1[0], d_g1[0], d_sh2[0], d_sc2[0], d_g2[0]])
    return dx, dict(w_in=g_win, w_out=g_wout, ffn_up=g_up, ffn_down=g_down), small, dmod


def kernel(x, c, positions, ada_w, ada_b, norm1_g, w_in, ssd_conv_w, ssd_conv_b, ssd_dt_bias, ssd_a_log, ssd_d, ssd_norm_g, pool_w, pool_scale, w_out, norm2_g, ffn_up, ffn_conv_w, ffn_conv_b, ffn_down, final_g, loss_target, m_ada_w, m_ada_b, m_norm1_g, m_w_in, m_ssd_conv_w, m_ssd_conv_b, m_ssd_dt_bias, m_ssd_a_log, m_ssd_d, m_ssd_norm_g, m_pool_w, m_pool_scale, m_w_out, m_norm2_g, m_ffn_up, m_ffn_conv_w, m_ffn_conv_b, m_ffn_down, m_final_g, v_ada_w, v_ada_b, v_norm1_g, v_w_in, v_ssd_conv_w, v_ssd_conv_b, v_ssd_dt_bias, v_ssd_a_log, v_ssd_d, v_ssd_norm_g, v_pool_w, v_pool_scale, v_w_out, v_norm2_g, v_ffn_up, v_ffn_conv_w, v_ffn_conv_b, v_ffn_down, v_final_g):
    args = dict(locals())
    w = {n: args[n] for n in _WEIGHTS}
    m = {n: args["m_" + n] for n in _WEIGHTS}
    v = {n: args["v_" + n] for n in _WEIGHTS}
    d = D_MODEL
    me = (lax.axis_index("x"), lax.axis_index("y"), lax.axis_index("c"))
    chip, dev = _chip(me), _dev(me)

    shapes0 = [c.shape, ssd_conv_w.shape, ffn_conv_w.shape]
    g0 = allgather8("gather_c_conv", _pack([c, ssd_conv_w, ffn_conv_w]))
    c16 = jnp.pad(g0[:, :d // 128, :].reshape(8, d), ((0, 8), (0, 0)))
    by_chip = [_unpack(g0[2 * j], shapes0) for j in range(4)]
    conv_w_full = jnp.concatenate([p[1] for p in by_chip], axis=-1)
    fconv_w_full = jnp.concatenate([p[2] for p in by_chip], axis=-1)

    modp = ada_forward("ada_fwd", c16, ada_w)[:, :8]
    g1 = allgather8("gather_mod", _pack([modp]))
    modfull = jnp.concatenate([_unpack(g1[2 * j], [modp.shape])[0] for j in range(4)], axis=-1)
    mod = lax.dynamic_index_in_dim(modfull, dev, axis=1, keepdims=False) + ada_b
    modv = [[mod[i, q * d:(q + 1) * d][None] for q in range(6)] for i in range(DEPTH)]

    got = gather_layer_from_chips("gather_w", [w[n].astype(BF16) for n in _BIG])
    both = swap_layers("swap_w", got, me[2])
    full = dict(
        w_in=[_perm_cols(a.transpose(1, 0, 2).reshape(d, IN_W)) for a in both[0]],
        w_out=[a.reshape(d, d) for a in both[1]],
        ffn_up=[a.transpose(1, 0, 2).reshape(d, 2 * FFN_DIM) for a in both[2]],
        ffn_down=[a.reshape(FFN_DIM, d) for a in both[3]],
    )

    cs, sn, rot = rope_tables(positions[0])
    e_mat = jnp.asarray((np.arange(128)[:, None] == (np.arange(SSD_INNER)[None, :] // 64)).astype(np.float32))
    eye4 = jnp.eye(4, dtype=F32)
    wts, sps = [], []
    for i in range(DEPTH):
        wts.append(dict(
            w_in=full["w_in"][i], w_out=full["w_out"][i], ffn_up=full["ffn_up"][i], ffn_down=full["ffn_down"][i],
            norm1_g=norm1_g[i][None], norm2_g=norm2_g[i][None], pool_scale=pool_scale[i][None],
            wbd=(eye4[:, None, :, None] * pool_w[i][:, :, None, :]).reshape(POOL_W, POOL_W),
            ffn_conv_w=fconv_w_full[i], ffn_conv_b=ffn_conv_b[i][None]))
        sps.append(dict(cw=conv_w_full[i], cb=ssd_conv_b[i][None], dtb=_pad_lanes(ssd_dt_bias[i]), alog=_pad_lanes(ssd_a_log[i]),
                        dsk=_pad_lanes(ssd_d[i]), ng=ssd_norm_g[i][None], e=e_mat))

    xc, keeps = x, []
    for i in range(DEPTH):
        xc, keep = _layer_forward(i, xc, modv[i], wts[i], sps[i], cs, sn, rot)
        keeps.append(keep)
    lossblk, dx, d_final = final_loss("final_loss", xc, loss_target, final_g[None])
    loss = lax.psum(lossblk[0, 0], ("x", "y", "c"))

    big_g, small_g, dmods = [None] * DEPTH, [None] * DEPTH, [None] * DEPTH
    for i in reversed(range(DEPTH)):
        dx, big_g[i], small_g[i], dmods[i] = _layer_backward(i, dx, keeps[i], modv[i], wts[i], sps[i], cs, sn, rot)

    by_dest = [
        jnp.stack([_cols_by_chip(_unperm_cols(big_g[i]["w_in"]), IN_W // 4) for i in range(DEPTH)]),
        jnp.stack([big_g[i]["w_out"].reshape(4, d // 4, d) for i in range(DEPTH)]),
        jnp.stack([_cols_by_chip(big_g[i]["ffn_up"], 2 * FFN_DIM // 4) for i in range(DEPTH)]),
        jnp.stack([big_g[i]["ffn_down"].reshape(4, FFN_DIM // 4, d) for i in range(DEPTH)]),
    ]
    theirs = swap_other_layer("swap_g", by_dest)
    core_sum = [sum_cores(f"sum_cores_{n}", g, t, BF16) for n, g, t in zip(_BIG, by_dest, theirs)]
    from_chips = scatter_to_chips("scatter_g", core_sum)
    chip_sum = [sum_slots(f"sum_chips_{n}", q, 4)[0] for n, q in zip(_BIG, from_chips)]
    reduced = swap_layers("swap_r", chip_sum, me[2])
    grads = {n: jnp.stack(r) for n, r in zip(_BIG, reduced)}

    part = dict(ada_b=jnp.stack(dmods), final_g=d_final[0])
    for n in _SMALL:
        if n not in part:
            part[n] = jnp.stack([small_g[i][n] for i in range(DEPTH)])
    full_shapes = [part[n].shape for n in _SMALL]
    gs = allgather8("gather_small", _pack([part[n] for n in _SMALL]))
    tot = _unpack(sum_slots("sum_small", gs, 8)[0], full_shapes)
    small_tot = dict(zip(_SMALL, tot))
    dmod_all = gs[:, :DEPTH * 6 * d // 128, :].reshape(8, DEPTH, 6 * d)
    for n, ncol in _COL_SHARDED_SMALL.items():
        small_tot[n] = lax.dynamic_slice_in_dim(small_tot[n], chip * ncol, ncol, axis=2)
    grads.update(small_tot)

    ncol = ada_w.shape[2]
    dm = lax.dynamic_slice_in_dim(dmod_all, chip * ncol, ncol, axis=2).transpose(1, 0, 2)
    upd = {}
    g_ada, *upd["ada_w"] = ada_backward("ada_bwd", c16, jnp.pad(dm, ((0, 0), (0, 8), (0, 0))), ada_w, m["ada_w"], v["ada_w"])
    grads["ada_w"] = g_ada

    for n in _BIG:
        upd[n] = adamw(f"adam_{n}", w[n], grads[n], m[n], v[n])
    shapes_s = [w[n].shape for n in _SMALL]
    packed = [_pack([src[n] for n in _SMALL]) for src in (w, grads, m, v)]
    outs_s = [_unpack(o, shapes_s) for o in adamw("adam_small", *packed)]
    for q, n in enumerate(_SMALL):
        upd[n] = [outs_s[0][q], outs_s[1][q], outs_s[2][q]]

    return (loss, dx, *[grads[n] for n in _WEIGHTS], *[upd[n][0] for n in _WEIGHTS], *[upd[n][1] for n in _WEIGHTS],
            *[upd[n][2] for n in _WEIGHTS])
```

```python
import functools
import math

import numpy as np
import jax
import jax.numpy as jnp
from jax import lax
from jax.experimental import pallas as pl
from jax.experimental.pallas import tpu as pltpu

F32 = jnp.float32
BF16 = jnp.bfloat16
HI = lax.Precision.HIGHEST
MESH = pl.DeviceIdType.MESH

D_MODEL = 1024
SEQ = 4096
DEPTH = 2
SSD_INNER = 512
SSD_HEADS = 8
SSD_STATE = 128
POOL_W = 256
POOL_WINDOWS = (2, 4, 8, 16)
ATT_W = 256
ATT_HEADS = 4
ATT_HEAD_DIM = 64
ATT_PATTERNS = ((128, 1), (512, 4), (2048, 16))
ATT_BLOCK = 128
ROT_DIM = 16
ROPE_THETA = 500000.0
IN_W = 2568
IN_WP = 2688
IN_MAIN = 1920
FFN_DIM = 2816
NORM_EPS = 1e-6
ADAM_LR, ADAM_B1, ADAM_B2, ADAM_EPS, ADAM_WD, ADAM_STEP = 0.001, 0.9, 0.999, 1e-08, 0.01, 10

VMEM_LIMIT_BYTES = 56 * 1024 * 1024
NEG = -1e30


def _mxu(a, b, mode):
    dims = {"nn": ((1,), (0,)), "nt": ((1,), (1,)), "tn": ((0,), (0,))}[mode]
    return lax.dot_general(a.astype(BF16), b.astype(BF16), (dims, ((), ())), preferred_element_type=F32)


@functools.partial(jax.custom_vjp, nondiff_argnums=(2,))
def _bdot(a, b, mode):
    return _mxu(a, b, mode)


def _bdot_fwd(a, b, mode):
    return _mxu(a, b, mode), (a, b)


def _bdot_bwd(mode, res, g):
    a, b = res
    if mode == "nn":
        return _mxu(g, b, "nt"), _mxu(a, g, "tn")
    if mode == "nt":
        return _mxu(g, b, "nn"), _mxu(g, a, "tn")
    return _mxu(b, g, "nt"), _mxu(a, g, "nn")


_bdot.defvjp(_bdot_fwd, _bdot_bwd)


def _fxu(a, b, mode):
    dims = {"nn": ((1,), (0,)), "nt": ((1,), (1,)), "tn": ((0,), (0,))}[mode]
    return lax.dot_general(a, b, (dims, ((), ())), precision=HI, preferred_element_type=F32)


@functools.partial(jax.custom_vjp, nondiff_argnums=(2,))
def _fdot(a, b, mode):
    return _fxu(a, b, mode)


def _fdot_fwd(a, b, mode):
    return _fxu(a, b, mode), (a, b)


def _fdot_bwd(mode, res, g):
    a, b = res
    if mode == "nn":
        return _fxu(g, b, "nt"), _fxu(a, g, "tn")
    if mode == "nt":
        return _fxu(g, b, "nn"), _fxu(g, a, "tn")
    return _fxu(b, g, "nt"), _fxu(a, g, "nn")


_fdot.defvjp(_fdot_fwd, _fdot_bwd)


def _iota(shape, dim):
    return lax.broadcasted_iota(jnp.int32, shape, dim)


def _make_shift(h):
    @functools.partial(jax.custom_vjp, nondiff_argnums=(2,))
    def shift(halo, cur, k):
        if k == 0:
            return cur
        full = jnp.concatenate([halo, cur], axis=0)
        return pltpu.roll(full, k, 0)[h:]

    def fwd(halo, cur, k):
        return shift(halo, cur, k), None

    def bwd(k, _, g):
        t, w = g.shape
        if k == 0:
            return jnp.zeros((h, w), F32), g
        d_cur = jnp.where(_iota((t, w), 0) < t - k, pltpu.roll(g, t - k, 0), 0.0)
        top = g[:h]
        d_halo = jnp.where(_iota((h, w), 0) >= h - k, pltpu.roll(top, h - k, 0) if k < h else top, 0.0)
        return d_halo, d_cur

    shift.defvjp(fwd, bwd)
    return shift


_shift8 = _make_shift(8)
_shift16 = _make_shift(16)


def _make_tail(h):
    @jax.custom_vjp
    def tail(x):
        return x[x.shape[0] - h:]

    def fwd(x):
        return tail(x), x.shape[0]

    def bwd(t, g):
        return (jnp.concatenate([jnp.zeros((t - h, g.shape[1]), F32), g], axis=0),)

    tail.defvjp(fwd, bwd)
    return tail


_tail8 = _make_tail(8)
_tail16 = _make_tail(16)


def _rowk(w, k):
    return jnp.sum(jnp.where(_iota(w.shape, 0) == k, w, 0.0), axis=0, keepdims=True)


def _silu(x):
    return x * (0.5 * jnp.tanh(0.5 * x) + 0.5)


def _softplus(x):
    return jnp.maximum(x, 0.0) + jnp.log(1.0 + jnp.exp(-jnp.abs(x)))


def _tile(dim, target, unit=128):
    if dim <= target:
        return dim
    best = None
    for t in range(unit, target + 1, unit):
        if dim % t == 0:
            best = t
    assert best is not None, (dim, target)
    return best


class Row:
    def __init__(self, arr, w=None, fb=None, fc=None, diff=True, slot=False, dcols=None, dfc=None, ddtype=F32):
        self.ddtype = ddtype
        self.arr = arr
        self.w = arr.shape[2] if w is None else w
        self.fb = (lambda b: 0) if fb is None else fb
        self.fc = (lambda b: 0) if fc is None else fc
        self.diff = diff
        self.slot = slot
        self.dcols = dcols
        self.dfc = dfc


class Vec:
    def __init__(self, arr, w=None, fc=None, diff=True):
        self.arr = arr
        self.w = arr.shape[1] if w is None else w
        self.fc = fc
        self.diff = diff


def _row_spec(r, t, nchunk, reverse):
    if reverse:
        return pl.BlockSpec((1, t, r.w), lambda b, i, r=r: (r.fb(b), nchunk - 1 - i, r.fc(b)))
    return pl.BlockSpec((1, t, r.w), lambda b, i, r=r: (r.fb(b), i, r.fc(b)))


def _vec_spec(v):
    if v.fc is None:
        return pl.BlockSpec(v.arr.shape, lambda b, i: (0, 0))
    return pl.BlockSpec((v.arr.shape[0], v.w), lambda b, i, v=v: (0, v.fc(b)))


def _cparams():
    return pltpu.CompilerParams(dimension_semantics=("arbitrary", "arbitrary"), vmem_limit_bytes=VMEM_LIMIT_BYTES)


def scan_fwd(name, fn, *, nb, nchunk, t, rows, vecs, carries, outs, save):
    nr, nv, nc, no = len(rows), len(vecs), len(carries), len(outs)

    def body(*refs):
        row_refs, vec_refs = refs[:nr], refs[nr:nr + nv]
        out_refs = refs[nr + nv:nr + nv + no]
        save_refs = refs[nr + nv + no:nr + nv + no + (nc if save else 0)]
        car = refs[len(refs) - nc:] if nc else ()
        b, i = pl.program_id(0), pl.program_id(1)
        if nc:
            @pl.when(i == 0)
            def _():
                for c_ref in car:
                    c_ref[...] = jnp.zeros(c_ref.shape, F32)
        cin = [c_ref[...] for c_ref in car]
        if save:
            for s_ref, cv in zip(save_refs, cin):
                s_ref[0, 0] = cv
        new_c, o = fn(i, b, cin, [r[0] for r in row_refs], [v[...] for v in vec_refs])
        for c_ref, cv in zip(car, new_c):
            c_ref[...] = cv
        for o_ref, ov in zip(out_refs, o):
            o_ref[0] = ov.astype(o_ref.dtype)

    out_shape = [o.arr for o in outs]
    out_specs = [_row_spec(o, t, nchunk, False) for o in outs]
    if save:
        for cs in carries:
            out_shape.append(jax.ShapeDtypeStruct((nb, nchunk) + tuple(cs), F32))
            out_specs.append(pl.BlockSpec((1, 1) + tuple(cs), lambda b, i: (b, i, 0, 0)))
    res = pl.pallas_call(
        body, name=name, grid=(nb, nchunk),
        in_specs=[_row_spec(r, t, nchunk, False) for r in rows] + [_vec_spec(v) for v in vecs],
        out_specs=out_specs, out_shape=out_shape,
        scratch_shapes=[pltpu.VMEM(tuple(cs), F32) for cs in carries],
        compiler_params=_cparams(),
    )(*[r.arr for r in rows], *[v.arr for v in vecs])
    return list(res[:no]), list(res[no:])


def scan_bwd(name, fn, *, nb, nchunk, t, rows, vecs, carries, saved, douts, adds=None):
    adds = adds or {}
    nr, nv, nc, no = len(rows), len(vecs), len(carries), len(douts)
    dri = [k for k, r in enumerate(rows) if r.diff]
    dvi = [k for k, v in enumerate(vecs) if v.diff]
    add_keys = sorted(adds)
    na = len(add_keys)

    def body(*refs):
        p = 0
        row_refs = refs[p:p + nr]; p += nr
        vec_refs = refs[p:p + nv]; p += nv
        save_refs = refs[p:p + nc]; p += nc
        dout_refs = refs[p:p + no]; p += no
        add_refs = refs[p:p + na]; p += na
        drow_refs = refs[p:p + len(dri)]; p += len(dri)
        dvec_refs = refs[p:p + len(dvi)]; p += len(dvi)
        dcar = refs[p:]
        b, ir = pl.program_id(0), pl.program_id(1)
        ci = nchunk - 1 - ir
        if nc:
            @pl.when(ir == 0)
            def _():
                for c_ref in dcar:
                    c_ref[...] = jnp.zeros(c_ref.shape, F32)
        rows_v = [r[0] for r in row_refs]
        vecs_v = [v[...] for v in vec_refs]
        cin = [s[0, 0] for s in save_refs]
        dc = [c_ref[...] for c_ref in dcar]
        dout_v = [d[0].astype(F32) for d in dout_refs]

        def f(cs, dr, dv):
            rr, vv = list(rows_v), list(vecs_v)
            for k, idx in enumerate(dri):
                rr[idx] = dr[k]
            for k, idx in enumerate(dvi):
                vv[idx] = dv[k]
            return fn(ci, b, cs, rr, vv)

        _, vjp = jax.vjp(f, cin, [rows_v[k].astype(F32) for k in dri], [vecs_v[k].astype(F32) for k in dvi])
        dcin, drows, dvecs = vjp((dc, dout_v))
        for c_ref, cv in zip(dcar, dcin):
            c_ref[...] = cv
        for k, (o_ref, ov) in enumerate(zip(drow_refs, drows)):
            if dri[k] in adds:
                ov = ov + add_refs[add_keys.index(dri[k])][0].astype(F32)
            o_ref[0] = ov.astype(o_ref.dtype)
        for k, (o_ref, ov) in enumerate(zip(dvec_refs, dvecs)):
            first = (ir == 0) if vecs[dvi[k]].fc is not None else jnp.logical_and(ir == 0, b == 0)

            @pl.when(first)
            def _(o_ref=o_ref, ov=ov):
                o_ref[...] = ov

            @pl.when(jnp.logical_not(first))
            def _(o_ref=o_ref, ov=ov):
                o_ref[...] += ov

    in_specs = ([_row_spec(r, t, nchunk, True) for r in rows] + [_vec_spec(v) for v in vecs]
                + [pl.BlockSpec((1, 1) + tuple(cs), lambda b, i: (b, nchunk - 1 - i, 0, 0)) for cs in carries]
                + [_row_spec(d, t, nchunk, True) for d in douts]
                + [_row_spec(adds[k], t, nchunk, True) for k in add_keys])
    out_shape, out_specs = [], []
    for k in dri:
        r = rows[k]
        if r.slot:
            out_shape.append(jax.ShapeDtypeStruct((nb, r.arr.shape[1], r.w), r.ddtype))
            out_specs.append(pl.BlockSpec((1, t, r.w), lambda b, i: (b, nchunk - 1 - i, 0)))
        elif r.dcols is not None:
            out_shape.append(jax.ShapeDtypeStruct((r.arr.shape[0], r.arr.shape[1], r.dcols), r.ddtype))
            out_specs.append(pl.BlockSpec((1, t, r.w), lambda b, i, r=r: (r.fb(b), nchunk - 1 - i, r.dfc(b))))
        else:
            out_shape.append(jax.ShapeDtypeStruct(r.arr.shape, r.ddtype))
            out_specs.append(_row_spec(r, t, nchunk, True))
    for k in dvi:
        out_shape.append(jax.ShapeDtypeStruct(vecs[k].arr.shape, F32))
        out_specs.append(_vec_spec(vecs[k]))
    res = pl.pallas_call(
        body, name=name, grid=(nb, nchunk), in_specs=in_specs, out_specs=out_specs, out_shape=out_shape,
        scratch_shapes=[pltpu.VMEM(tuple(cs), F32) for cs in carries],
        compiler_params=_cparams(),
    )(*[r.arr for r in rows], *[v.arr for v in vecs], *saved, *[d.arr for d in douts], *[adds[k].arr for k in add_keys])
    return list(res[:len(dri)]), list(res[len(dri):])


def out_row(shape, dtype=F32, w=None, fb=None, fc=None):
    return Row(jax.ShapeDtypeStruct(shape, dtype), w, fb, fc)


def _conv(shift, halo, cur, w, bias, taps):
    y = bias
    for k in range(taps):
        y = y + _rowk(w, k) * shift(halo, cur, taps - 1 - k)
    return y


def _ssd_fn(ci, b, carries, rows, vecs):
    cx, cb_, cc, ht = carries
    z, xr, br, cr, dtr = rows
    cwx, cbx, cwb, cbb, cwc, cbc, dtb, alog, dsk, ng, e = vecs
    t = z.shape[0]
    xs = _silu(_conv(_shift8, cx, xr, cwx, cbx, 4))
    bm = _silu(_conv(_shift8, cb_, br, cwb, cbb, 4))
    cm = _silu(_conv(_shift8, cc, cr, cwc, cbc, 4))
    dt = _softplus(dtr + dtb)
    da = dt * (-jnp.exp(alog))
    r, c = _iota((t, t), 0), _iota((t, t), 1)
    causal = r >= c
    acol = _fdot(causal.astype(F32), da, "nn")
    arow = _fdot(da, (r <= c).astype(F32), "tn")
    a = _fdot(acol, e, "nn")
    dtx = _fdot(dt, e, "nn")
    atot = jnp.sum(jnp.where(_iota(a.shape, 0) == t - 1, a, 0.0), axis=0, keepdims=True)
    x = xs * dtx
    cbm = _bdot(cm, bm, "nt")
    lane, sub = _iota(acol.shape, 1), _iota(arow.shape, 0)
    colh = _iota(x.shape, 1) // 64
    ydiag = jnp.zeros(x.shape, F32)
    for j in range(4):
        h = 4 * b + j
        ac = jnp.sum(jnp.where(lane == h, acol, 0.0), axis=1, keepdims=True)
        ar = jnp.sum(jnp.where(sub == h, arow, 0.0), axis=0, keepdims=True)
        lmat = jnp.exp(jnp.where(causal, ac - ar, NEG))
        ydiag = ydiag + _bdot(cbm * lmat, jnp.where(colh == j, x, 0.0), "nn")
    yoff = _bdot(cm, ht, "nn") * jnp.exp(a)
    ht_new = ht * jnp.exp(atot) + _bdot(bm, x * jnp.exp(atot - a), "tn")
    dx = jnp.sum(_fdot(jnp.broadcast_to(dsk, (8, dsk.shape[1])), e, "nn"), axis=0, keepdims=True) * 0.125
    y = ydiag + yoff + dx * xs
    yz = y * _silu(z)
    yn = yz * lax.rsqrt(jnp.mean(yz * yz, axis=-1, keepdims=True) + NORM_EPS) * ng
    return [_tail8(xr), _tail8(br), _tail8(cr), ht_new], [yn]


_SSD_T = 256
_SSD_CARRIES = [(8, 256), (8, 128), (8, 128), (128, 256)]


def _ssd_io(proj3, p):
    own = lambda b: b
    rows = [Row(proj3, 256, fc=own, dcols=512, dfc=own, ddtype=BF16),
            Row(proj3, 256, fc=lambda b: 2 + b, dcols=512, dfc=own, ddtype=BF16),
            Row(proj3, 128, fc=lambda b: 8 + b, dcols=256, dfc=own, ddtype=BF16),
            Row(proj3, 128, fc=lambda b: 10 + b, dcols=256, dfc=own, ddtype=BF16),
            Row(proj3, 128, fc=lambda b: 14, slot=True)]
    vecs = [Vec(p["cw"], 256, lambda b: b), Vec(p["cb"], 256, lambda b: b),
            Vec(p["cw"], 128, lambda b: 4 + b), Vec(p["cb"], 128, lambda b: 4 + b),
            Vec(p["cw"], 128, lambda b: 6 + b), Vec(p["cb"], 128, lambda b: 6 + b),
            Vec(p["dtb"]), Vec(p["alog"]), Vec(p["dsk"]), Vec(p["ng"], 256, lambda b: b),
            Vec(p["e"], 256, lambda b: b, diff=False)]
    return rows, vecs


def ssd_forward(name, proj3, p):
    rows, vecs = _ssd_io(proj3, p)
    s = proj3.shape[1]
    (y,), saved = scan_fwd(name, _ssd_fn, nb=2, nchunk=s // _SSD_T, t=_SSD_T, rows=rows, vecs=vecs,
                           carries=_SSD_CARRIES, outs=[out_row((1, s, SSD_INNER), BF16, 256, fc=lambda b: b)], save=True)
    return y, saved


def ssd_backward(name, proj3, p, saved, dmix3):
    rows, vecs = _ssd_io(proj3, p)
    s = proj3.shape[1]
    drows, dvecs = scan_bwd(name, _ssd_fn, nb=2, nchunk=s // _SSD_T, t=_SSD_T, rows=rows, vecs=vecs,
                            carries=_SSD_CARRIES, saved=saved, douts=[Row(dmix3, 256, fc=lambda b: b)])
    return drows, dvecs


def _pool_fn(ci, b, carries, rows, vecs):
    (cu,) = carries
    (u,) = rows
    wbd, scale = vecs
    t = u.shape[0]
    pos = ci * t + _iota(u.shape, 0)
    grp = _iota(u.shape, 1) // 64
    acc, pooled, k = u, jnp.zeros(u.shape, F32), 1
    for gi, w in enumerate(POOL_WINDOWS):
        while k < w:
            acc = acc + _shift16(cu, u, k)
            k += 1
        pooled = jnp.where(grp == gi, acc / jnp.minimum(pos + 1, w).astype(F32), pooled)
    y = _bdot(pooled - u, wbd, "nn") * scale
    return [_tail16(u)], [y]


_POOL_T = 256


def _pool_io(proj3, wbd, scale):
    return [Row(proj3, 256, fc=lambda b: 6, dcols=256, dfc=lambda b: 0, ddtype=BF16)], [Vec(wbd), Vec(scale)]


def pool_forward(name, proj3, wbd, scale):
    rows, vecs = _pool_io(proj3, wbd, scale)
    s = proj3.shape[1]
    (y,), saved = scan_fwd(name, _pool_fn, nb=1, nchunk=s // _POOL_T, t=_POOL_T, rows=rows, vecs=vecs,
                           carries=[(16, 256)], outs=[out_row((1, s, POOL_W), BF16)], save=True)
    return y, saved


def pool_backward(name, proj3, wbd, scale, saved, dmix3):
    rows, vecs = _pool_io(proj3, wbd, scale)
    s = proj3.shape[1]
    return scan_bwd(name, _pool_fn, nb=1, nchunk=s // _POOL_T, t=_POOL_T, rows=rows, vecs=vecs,
                    carries=[(16, 256)], saved=saved, douts=[Row(dmix3, 256, fc=lambda b: 2)])


def _attn_fn(ci, b, carries, rows, vecs):
    kp, vp = carries
    q, k, v, cs, sn = rows
    (rot,) = vecs
    qr = q * cs + _fdot(q, rot, "nn") * sn
    kr = k * cs + _fdot(k, rot, "nn") * sn
    scale = ATT_HEAD_DIM ** -0.5
    n = q.shape[0]
    r, c = _iota((n, n), 0), _iota((n, n), 1)
    prev_ok, cur_ok = jnp.logical_and(c >= r, ci > 0), r >= c
    head = _iota(q.shape, 1) // ATT_HEAD_DIM
    o, lse = jnp.zeros(q.shape, F32), jnp.zeros(q.shape, F32)
    for h in range(ATT_HEADS):
        mine = head == h
        qh = jnp.where(mine, qr, 0.0)
        sp = jnp.where(prev_ok, _bdot(qh, kp, "nt") * scale, NEG)
        sc = jnp.where(cur_ok, _bdot(qh, kr, "nt") * scale, NEG)
        m = lax.stop_gradient(jnp.maximum(jnp.max(sp, axis=1, keepdims=True), jnp.max(sc, axis=1, keepdims=True)))
        pp, pc = jnp.exp(sp - m), jnp.exp(sc - m)
        l = jnp.sum(pp, axis=1, keepdims=True) + jnp.sum(pc, axis=1, keepdims=True)
        o = jnp.where(mine, (_bdot(pp, vp, "nn") + _bdot(pc, v, "nn")) / l, o)
        lse = jnp.where(mine, m + jnp.log(l), lse)
    return [kr, v], [o, lse]


_ATT_CARRIES = [(ATT_BLOCK, ATT_W), (ATT_BLOCK, ATT_W)]
def _attn_io(pv, cv, sv, rot, d):
    own = lambda b: b
    rows = [Row(pv, ATT_W, fc=(lambda b, j=j: b * 3 + j), dcols=d * ATT_W, dfc=own) for j in range(3)]
    rows += [Row(cv, ATT_W, fc=own, diff=False), Row(sv, ATT_W, fc=own, diff=False)]
    return rows, [Vec(rot, diff=False)]


def attn_forward(name, pv, cv, sv, rot, d):
    rows, vecs = _attn_io(pv, cv, sv, rot, d)
    l = pv.shape[1]
    own = lambda b: b
    outs = [out_row((1, l, d * ATT_W), F32, ATT_W, fc=own) for _ in range(2)]
    (o, lse), saved = scan_fwd(name, _attn_fn, nb=d, nchunk=l // ATT_BLOCK, t=ATT_BLOCK, rows=rows, vecs=vecs,
                               carries=_ATT_CARRIES, outs=outs, save=True)
    return o, lse, saved


def attn_backward(name, pv, cv, sv, rot, d, saved, do, dlse):
    rows, vecs = _attn_io(pv, cv, sv, rot, d)
    l = pv.shape[1]
    own = lambda b: b
    drows, _ = scan_bwd(name, _attn_fn, nb=d, nchunk=l // ATT_BLOCK, t=ATT_BLOCK, rows=rows, vecs=vecs,
                        carries=_ATT_CARRIES, saved=saved, douts=[Row(do, ATT_W, fc=own), Row(dlse, ATT_W, fc=own)])
    return drows


def _merge_fn(ci, b, carries, rows, vecs):
    o1, o2, o3, l1, l2, l3 = rows
    mx = lax.stop_gradient(jnp.maximum(l1, jnp.maximum(l2, l3)))
    e1, e2, e3 = jnp.exp(l1 - mx), jnp.exp(l2 - mx), jnp.exp(l3 - mx)
    return [], [(e1 * o1 + e2 * o2 + e3 * o3) / (e1 + e2 + e3)]


_ROW_T = 256


def merge_forward(name, os_, ls_):
    s = os_[0].shape[1]
    (y,), _ = scan_fwd(name, _merge_fn, nb=1, nchunk=s // _ROW_T, t=_ROW_T, rows=[Row(a) for a in (*os_, *ls_)], vecs=[],
                       carries=[], outs=[out_row((1, s, ATT_W), BF16)], save=False)
    return y


def merge_backward(name, os_, ls_, dmix3):
    s = os_[0].shape[1]
    drows, _ = scan_bwd(name, _merge_fn, nb=1, nchunk=s // _ROW_T, t=_ROW_T, rows=[Row(a) for a in (*os_, *ls_)], vecs=[],
                        carries=[], saved=[], douts=[Row(dmix3, 256, fc=lambda b: 3)])
    return drows


def _norm_mod_fn(ci, b, carries, rows, vecs):
    (x,) = rows
    g, sc, sh = vecs
    xn = x * lax.rsqrt(jnp.mean(x * x, axis=-1, keepdims=True) + NORM_EPS)
    return [], [xn * g * (1.0 + sc) + sh]


def norm_mod_forward(name, x3, g, sc, sh):
    s = x3.shape[1]
    (h,), _ = scan_fwd(name, _norm_mod_fn, nb=1, nchunk=s // _ROW_T, t=_ROW_T, rows=[Row(x3)], vecs=[Vec(g), Vec(sc), Vec(sh)],
                       carries=[], outs=[out_row(x3.shape, BF16)], save=False)
    return h


def norm_mod_backward(name, x3, g, sc, sh, dh3, add3):
    s = x3.shape[1]
    (dx,), dv = scan_bwd(name, _norm_mod_fn, nb=1, nchunk=s // _ROW_T, t=_ROW_T, rows=[Row(x3)], vecs=[Vec(g), Vec(sc), Vec(sh)],
                         carries=[], saved=[], douts=[Row(dh3)], adds={0: Row(add3)})
    return dx, dv


def _gate_fn(ci, b, carries, rows, vecs):
    return [], [rows[0] * vecs[0]]


def gate_backward(name, o3, g, dx3):
    s = o3.shape[1]
    (do,), (dg,) = scan_bwd(name, _gate_fn, nb=1, nchunk=s // _ROW_T, t=_ROW_T, rows=[Row(o3, ddtype=BF16)], vecs=[Vec(g)],
                            carries=[], saved=[], douts=[Row(dx3)])
    return do, dg


def _make_halves():
    @jax.custom_vjp
    def halves(x):
        h = x.shape[1] // 2
        return x[:, :h], x[:, h:]

    def fwd(x):
        return halves(x), None

    def bwd(_, g):
        return (jnp.concatenate(g, axis=1),)

    halves.defvjp(fwd, bwd)
    return halves


_halves = _make_halves()


def _ffn_fn(ci, b, carries, rows, vecs):
    (cu,) = carries
    (u,) = rows
    w, bias = vecs
    hg, hu = _halves(_conv(_shift8, cu, u, w, bias, 3))
    return [_tail8(u)], [_silu(hg) * hu]


_FFN_T = 256
_FFN_CW = FFN_DIM // 2
_FFN_CARRIES = [(8, 2 * _FFN_CW)]
FFN_BLOCK_ORDER = [0, 2, 1, 3]


def _ffn_io(up3, cw, cb):
    own = lambda b: b
    return [Row(up3, 2 * _FFN_CW, fc=own, ddtype=BF16)], [Vec(cw, 2 * _FFN_CW, own), Vec(cb, 2 * _FFN_CW, own)]


def ffn_mid_forward(name, up3, cw, cb):
    rows, vecs = _ffn_io(up3, cw, cb)
    s = up3.shape[1]
    (act,), saved = scan_fwd(name, _ffn_fn, nb=2, nchunk=s // _FFN_T, t=_FFN_T, rows=rows, vecs=vecs, carries=_FFN_CARRIES,
                             outs=[out_row((1, s, FFN_DIM), BF16, _FFN_CW, fc=lambda b: b)], save=True)
    return act, saved


def ffn_mid_backward(name, up3, cw, cb, saved, dact3):
    rows, vecs = _ffn_io(up3, cw, cb)
    s = up3.shape[1]
    return scan_bwd(name, _ffn_fn, nb=2, nchunk=s // _FFN_T, t=_FFN_T, rows=rows, vecs=vecs, carries=_FFN_CARRIES,
                    saved=saved, douts=[Row(dact3, _FFN_CW, fc=lambda b: b)])


def _adam_fn(ci, b, carries, rows, vecs):
    w, g, m, v = rows
    m = ADAM_B1 * m + (1.0 - ADAM_B1) * g
    v = ADAM_B2 * v + (1.0 - ADAM_B2) * (g * g)
    m_hat = m / (1.0 - ADAM_B1 ** ADAM_STEP)
    v_hat = v / (1.0 - ADAM_B2 ** ADAM_STEP)
    delta = -ADAM_LR * (m_hat / (jnp.sqrt(v_hat) + ADAM_EPS) + ADAM_WD * w)
    return [], [delta, m, v]


def adamw(name, w, g, m, v):
    shape = w.shape
    c = shape[-1]
    r = int(np.prod(shape[:-1]))
    t = _tile(r, 256, 8)
    as3 = lambda a: a.reshape(1, r, c)
    outs, _ = scan_fwd(name, _adam_fn, nb=1, nchunk=r // t, t=t, rows=[Row(as3(a)) for a in (w, g, m, v)], vecs=[], carries=[],
                       outs=[out_row((1, r, c)) for _ in range(3)], save=False)
    return [o.reshape(shape) for o in outs]


def rope_tables(positions):
    half = ROT_DIM // 2
    inv_freq = ROPE_THETA ** (-jnp.arange(0, ROT_DIM, 2, dtype=F32) / ROT_DIM)
    ang = positions.astype(F32)[:, None] * inv_freq
    s = positions.shape[0]
    cs = jnp.concatenate([jnp.cos(ang), jnp.cos(ang), jnp.ones((s, ATT_HEAD_DIM - ROT_DIM), F32)], axis=1)
    sn = jnp.concatenate([jnp.sin(ang), jnp.sin(ang), jnp.zeros((s, ATT_HEAD_DIM - ROT_DIM), F32)], axis=1)
    rot = np.zeros((ATT_W, ATT_W), np.float32)
    for h in range(ATT_HEADS):
        for e in range(half):
            rot[h * ATT_HEAD_DIM + e + half, h * ATT_HEAD_DIM + e] = -1.0
            rot[h * ATT_HEAD_DIM + e, h * ATT_HEAD_DIM + e + half] = 1.0
    return jnp.tile(cs, (1, ATT_HEADS)), jnp.tile(sn, (1, ATT_HEADS)), jnp.asarray(rot)


def attention_forward(lname, proj3, cs, sn, rot):
    s = proj3.shape[1]
    os_, ls_, keep = [], [], []
    for pi, (_, d) in enumerate(ATT_PATTERNS):
        view = lambda a: a.reshape(1, s // d, d * a.shape[-1])
        o, lse, saved = attn_forward(f"{lname}_attn{pi}", view(proj3), view(cs), view(sn), rot, d)
        os_.append(o.reshape(1, s, ATT_W))
        ls_.append(lse.reshape(1, s, ATT_W))
        keep.append(saved)
    y = merge_forward(f"{lname}_merge", os_, ls_)
    return y, (os_, ls_, keep)


def attention_backward(lname, proj3, cs, sn, rot, res, dmix3):
    os_, ls_, keep = res
    s = proj3.shape[1]
    dm = merge_backward(f"{lname}_merge_b", os_, ls_, dmix3)
    tot = None
    for pi, (_, d) in enumerate(ATT_PATTERNS):
        view = lambda a: a.reshape(1, s // d, d * a.shape[-1])
        dqkv = attn_backward(f"{lname}_attn{pi}_b", view(proj3), view(cs), view(sn), rot, d, keep[pi], view(dm[pi]), view(dm[3 + pi]))
        dqkv = [a.reshape(s, ATT_W) for a in dqkv]
        tot = dqkv if tot is None else [a + b for a, b in zip(tot, dqkv)]
    return tot


def mm(name, a, b, mode, out_dtype=F32, res=None, gate=None, tm=1024, tn=1536, tk=1024):
    if mode == "nn":
        (m, k), n = a.shape, b.shape[1]
    elif mode == "nt":
        (m, k), n = a.shape, b.shape[0]
    else:
        (k, m), n = a.shape, b.shape[1]
    tm, tn, tk = _tile(m, tm), _tile(n, tn), _tile(k, tk)
    nk = k // tk
    a_spec = pl.BlockSpec((tk, tm), lambda i, j, q: (q, i)) if mode == "tn" else pl.BlockSpec((tm, tk), lambda i, j, q: (i, q))
    b_spec = pl.BlockSpec((tn, tk), lambda i, j, q: (j, q)) if mode == "nt" else pl.BlockSpec((tk, tn), lambda i, j, q: (q, j))
    o_spec = pl.BlockSpec((tm, tn), lambda i, j, q: (i, j))
    fused = res is not None

    def body(*refs):
        if fused:
            a_ref, b_ref, r_ref, g_ref, o_ref, o2_ref, acc = refs
        else:
            a_ref, b_ref, o_ref, acc = refs
        q = pl.program_id(2)

        @pl.when(q == 0)
        def _():
            acc[...] = jnp.zeros(acc.shape, F32)

        acc[...] += _mxu(a_ref[...], b_ref[...], mode)

        @pl.when(q == nk - 1)
        def _():
            o_ref[...] = acc[...].astype(o_ref.dtype)
            if fused:
                o2_ref[...] = r_ref[...] + g_ref[...] * acc[...]

    ins, in_specs = [a, b], [a_spec, b_spec]
    out_shape, out_specs = [jax.ShapeDtypeStruct((m, n), out_dtype)], [o_spec]
    if fused:
        ins += [res, gate]
        in_specs += [o_spec, pl.BlockSpec((1, tn), lambda i, j, q: (0, j))]
        out_shape.append(jax.ShapeDtypeStruct((m, n), F32))
        out_specs.append(o_spec)
    out = pl.pallas_call(
        body, name=name, grid=(m // tm, n // tn, nk), in_specs=in_specs, out_specs=out_specs, out_shape=out_shape,
        scratch_shapes=[pltpu.VMEM((tm, tn), F32)],
        compiler_params=pltpu.CompilerParams(dimension_semantics=("parallel", "parallel", "arbitrary"),
                                             vmem_limit_bytes=VMEM_LIMIT_BYTES),
    )(*ins)
    return tuple(out) if fused else out[0]


def final_loss(name, x3, t3, g):
    s, d = x3.shape[1], x3.shape[2]
    t = _ROW_T

    def body(x_ref, t_ref, g_ref, loss_ref, dx_ref, dg_ref):
        i = pl.program_id(0)
        tv = t_ref[0]

        def f(x, gg):
            y = x * lax.rsqrt(jnp.mean(x * x, axis=-1, keepdims=True) + NORM_EPS) * gg
            e = y - tv
            return 0.5 * jnp.sum(jnp.mean(e * e, axis=-1, keepdims=True), axis=0, keepdims=True)

        l, vjp = jax.vjp(f, x_ref[0], g_ref[...])
        dx, dg = vjp(jnp.ones((1, 1), F32))
        dx_ref[0] = dx

        @pl.when(i == 0)
        def _():
            loss_ref[...] = jnp.zeros(loss_ref.shape, F32)
            dg_ref[...] = jnp.zeros(dg_ref.shape, F32)

        loss_ref[...] += jnp.broadcast_to(l, loss_ref.shape)
        dg_ref[...] += dg

    row = pl.BlockSpec((1, t, d), lambda i: (0, i, 0))
    vec = pl.BlockSpec((1, d), lambda i: (0, 0))
    return pl.pallas_call(
        body, name=name, grid=(s // t,), in_specs=[row, row, vec],
        out_specs=[pl.BlockSpec((8, 128), lambda i: (0, 0)), row, vec],
        out_shape=[jax.ShapeDtypeStruct((8, 128), F32), jax.ShapeDtypeStruct(x3.shape, F32), jax.ShapeDtypeStruct((1, d), F32)],
        compiler_params=pltpu.CompilerParams(dimension_semantics=("arbitrary",), vmem_limit_bytes=VMEM_LIMIT_BYTES),
    )(x3, t3, g)


_ADA_TN = 512


def ada_forward(name, c16, ada_w):
    depth, d, cols = ada_w.shape

    def body(c_ref, w_ref, o_ref):
        o_ref[0] = _mxu(_silu(c_ref[...]), w_ref[0], "nn")

    return pl.pallas_call(
        body, name=name, grid=(depth, cols // _ADA_TN),
        in_specs=[pl.BlockSpec((16, d), lambda l, j: (0, 0)), pl.BlockSpec((1, d, _ADA_TN), lambda l, j: (l, 0, j))],
        out_specs=pl.BlockSpec((1, 16, _ADA_TN), lambda l, j: (l, 0, j)),
        out_shape=jax.ShapeDtypeStruct((depth, 16, cols), F32),
        compiler_params=pltpu.CompilerParams(dimension_semantics=("arbitrary", "arbitrary"), vmem_limit_bytes=VMEM_LIMIT_BYTES),
    )(c16, ada_w)


def ada_backward(name, c16, dmod16, w, m, v):
    depth, d, cols = w.shape

    def body(c_ref, dm_ref, w_ref, m_ref, v_ref, g_ref, dl_ref, nm_ref, nv_ref):
        g = _mxu(_silu(c_ref[...]), dm_ref[0], "tn")
        _, (delta, nm, nv) = _adam_fn(None, None, [], [w_ref[0], g, m_ref[0], v_ref[0]], [])
        g_ref[0], dl_ref[0], nm_ref[0], nv_ref[0] = g, delta, nm, nv

    blk = pl.BlockSpec((1, d, _ADA_TN), lambda l, j: (l, 0, j))
    return pl.pallas_call(
        body, name=name, grid=(depth, cols // _ADA_TN),
        in_specs=[pl.BlockSpec((16, d), lambda l, j: (0, 0)), pl.BlockSpec((1, 16, _ADA_TN), lambda l, j: (l, 0, j)), blk, blk, blk],
        out_specs=[blk] * 4, out_shape=[jax.ShapeDtypeStruct(w.shape, F32)] * 4,
        compiler_params=pltpu.CompilerParams(dimension_semantics=("arbitrary", "arbitrary"), vmem_limit_bytes=VMEM_LIMIT_BYTES),
    )(c16, dmod16, w, m, v)


def _sum_fn(ci, b, carries, rows, vecs):
    acc = rows[0]
    for r in rows[1:]:
        acc = acc + r
    return [], [acc]


def sum_slots(name, a, nsum, out_dtype=F32):
    n, r, c = a.shape
    nb = n // nsum
    t = _tile(r, 256, 8)
    rows = [Row(a, fb=(lambda b, k=k: k * nb + b)) for k in range(nsum)]
    (out,), _ = scan_fwd(name, _sum_fn, nb=nb, nchunk=r // t, t=t, rows=rows, vecs=[], carries=[],
                         outs=[out_row((nb, r, c), out_dtype, fb=lambda b: b)], save=False)
    return out


def _sum_my_layer_fn(ci, b, carries, rows, vecs):
    layer0, layer1, theirs = rows
    return [], [jnp.where(lax.axis_index("c") == 0, layer0, layer1) + theirs]


def sum_cores(name, g, theirs, out_dtype):
    _, nb, r, c = g.shape
    g8 = g.reshape(2 * nb, r, c)
    t = _tile(r, 256, 8)
    rows = [Row(g8, fb=lambda b: b), Row(g8, fb=lambda b: nb + b), Row(theirs, fb=lambda b: b)]
    (out,), _ = scan_fwd(name, _sum_my_layer_fn, nb=nb, nchunk=r // t, t=t, rows=rows, vecs=[], carries=[],
                         outs=[out_row((nb, r, c), out_dtype, fb=lambda b: b)], save=False)
    return out


def _flip(mask, pos):
    return tuple((1 - p) if m else p for m, p in zip(mask, pos))


ALL_PEERS = [(a, b, c) for a in (0, 1) for b in (0, 1) for c in (0, 1)][1:]
CHIP_PEERS = [(1, 0, 0), (0, 1, 0), (1, 1, 0)]
SIBLING = [(0, 0, 1)]


def _divisor(size, target, unit):
    best = 1
    for n in range(1, target + 1):
        if size % n == 0 and (size // n) % unit == 0:
            best = n
    return best


def _pieces(src, dst, pieces):
    shape = src.shape
    unit = 16 if src.dtype == BF16 else 8
    if pieces <= 1:
        return [(src, dst)]
    if len(shape) == 2:
        n = _divisor(shape[0], pieces, unit)
        s = shape[0] // n
        return [(src.at[pl.ds(i * s, s)], dst.at[pl.ds(i * s, s)]) for i in range(n)]
    assert len(shape) == 3, shape
    n = _divisor(shape[1], max(pieces // shape[0], 1), unit)
    s = shape[1] // n
    return [(src.at[j, pl.ds(i * s, s)], dst.at[j, pl.ds(i * s, s)]) for j in range(shape[0]) for i in range(n)]


def comm_call(name, arrays, out_shapes, masks, src_fn, dst_fn, local_fn=None, pieces=1):
    na, npeer = len(arrays), len(masks)

    def body(*refs):
        ins, outs = refs[:na], refs[na:2 * na]
        send_sems, recv_sems, loc_sems = refs[2 * na:]
        me = (lax.axis_index("x"), lax.axis_index("y"), lax.axis_index("c"))
        local = []
        if local_fn is not None:
            for k in range(na):
                s, d = local_fn(k, ins[k], outs[k], me)
                for ps, pd in _pieces(s, d, pieces):
                    pltpu.make_async_copy(ps, pd, loc_sems.at[k]).start()
                local.append(pltpu.make_async_copy(s, d, loc_sems.at[k]))

        def remote(k, p, src, dst, to):
            return pltpu.make_async_remote_copy(
                src_ref=src, dst_ref=dst, send_sem=send_sems.at[k * npeer + p], recv_sem=recv_sems.at[k * npeer + p],
                device_id=to, device_id_type=MESH)

        for k in range(na):
            for p in range(npeer):
                peer = _flip(masks[p], me)
                for ps, pd in _pieces(src_fn(k, ins[k], me, peer), dst_fn(k, outs[k], me), pieces):
                    remote(k, p, ps, pd, peer).start()
        for k in range(na):
            for p in range(npeer):
                peer = _flip(masks[p], me)
                remote(k, p, src_fn(k, ins[k], me, peer), dst_fn(k, outs[k], peer), peer).wait_recv()
        for k in range(na):
            for p in range(npeer):
                peer = _flip(masks[p], me)
                remote(k, p, src_fn(k, ins[k], me, peer), dst_fn(k, outs[k], me), peer).wait_send()
        for cp in local:
            cp.wait()

    hbm = pl.BlockSpec(memory_space=pl.ANY)
    out = pl.pallas_call(
        body, name=name, in_specs=[hbm] * na, out_specs=[hbm] * na,
        out_shape=[jax.ShapeDtypeStruct(s, a.dtype) for s, a in zip(out_shapes, arrays)],
        scratch_shapes=[pltpu.SemaphoreType.DMA((na * npeer,)), pltpu.SemaphoreType.DMA((na * npeer,)),
                        pltpu.SemaphoreType.DMA((na,))],
    )(*arrays)
    return list(out)


def _dev(pos):
    return 4 * pos[0] + 2 * pos[1] + pos[2]


def _chip(pos):
    return 2 * pos[0] + pos[1]


def allgather8(name, a):
    (out,) = comm_call(name, [a], [(8,) + a.shape], ALL_PEERS,
                       src_fn=lambda k, r, me, peer: r, dst_fn=lambda k, o, sender: o.at[_dev(sender)],
                       local_fn=lambda k, r, o, me: (r, o.at[_dev(me)]))
    return out


def gather_layer_from_chips(name, arrays):
    return comm_call(name, arrays, [(4,) + a.shape[1:] for a in arrays], CHIP_PEERS,
                     src_fn=lambda k, r, me, peer: r.at[me[2]], dst_fn=lambda k, o, sender: o.at[_chip(sender)],
                     local_fn=lambda k, r, o, me: (r.at[me[2]], o.at[_chip(me)]), pieces=8)


def swap_layers(name, arrays, c):
    got = comm_call(name, arrays, [a.shape for a in arrays], SIBLING,
                    src_fn=lambda k, r, me, peer: r, dst_fn=lambda k, o, sender: o, pieces=32)
    return [[jnp.where(c == 0, a, g), jnp.where(c == 0, g, a)] for a, g in zip(arrays, got)]


def swap_other_layer(name, arrays):
    return comm_call(name, arrays, [a.shape[1:] for a in arrays], SIBLING,
                     src_fn=lambda k, r, me, peer: r.at[peer[2]], dst_fn=lambda k, o, sender: o, pieces=32)


def scatter_to_chips(name, arrays):
    return comm_call(name, arrays, [a.shape for a in arrays], CHIP_PEERS,
                     src_fn=lambda k, r, me, peer: r.at[_chip(peer)], dst_fn=lambda k, o, sender: o.at[_chip(sender)],
                     local_fn=lambda k, r, o, me: (r.at[_chip(me)], o.at[_chip(me)]), pieces=8)


def _pack(arrs):
    flat = jnp.concatenate([a.reshape(-1).astype(F32) for a in arrs])
    n = flat.shape[0]
    pad = (-n) % (_ROW_T * 128)
    return jnp.pad(flat, (0, pad)).reshape(-1, 128)


def _unpack(buf, shapes):
    flat = buf.reshape(-1)
    out, o = [], 0
    for s in shapes:
        n = int(np.prod(s))
        out.append(flat[o:o + n].reshape(s))
        o += n
    return out


_WEIGHTS = ["ada_w", "ada_b", "norm1_g", "w_in", "ssd_conv_w", "ssd_conv_b", "ssd_dt_bias", "ssd_a_log", "ssd_d", "ssd_norm_g",
            "pool_w", "pool_scale", "w_out", "norm2_g", "ffn_up", "ffn_conv_w", "ffn_conv_b", "ffn_down", "final_g"]
_BIG = ["w_in", "w_out", "ffn_up", "ffn_down"]
_SMALL = [n for n in _WEIGHTS if n not in _BIG and n != "ada_w"]
_COL_SHARDED_SMALL = {"ssd_conv_w": 256, "ffn_conv_w": 1408}


def _pad_lanes(v, n=128):
    return jnp.pad(v.astype(F32), (0, n - v.shape[0]))[None]


def _perm_cols(w):
    pad = jnp.zeros(w.shape[:-1] + (IN_WP - IN_W,), w.dtype)
    return jnp.concatenate([w[..., :1536], w[..., 1544:1800], w[..., 1536:1544], pad, w[..., 1800:]], axis=-1)


def _unperm_cols(g):
    return jnp.concatenate([g[..., :1536], g[..., 1792:1800], g[..., 1536:1792], g[..., IN_MAIN:]], axis=-1)


def _cols_by_chip(g, ncol, order=None):
    g = g.reshape(g.shape[0], 4, ncol)
    if order is not None:
        g = g[:, np.asarray(order)]
    return g.transpose(1, 0, 2)


def _ffn_block_perm(a):
    lead = a.shape[:-1]
    return a.reshape(lead + (4, a.shape[-1] // 4))[..., np.asarray(FFN_BLOCK_ORDER), :].reshape(a.shape)


def _layer_forward(i, x3, modv, wts, sp, cs, sn, rot):
    sh1, sc1, g1, sh2, sc2, g2 = modv
    h1 = norm_mod_forward(f"l{i}_norm1", x3, wts["norm1_g"], sc1, sh1)
    proj3 = mm(f"l{i}_proj", h1[0], wts["w_in"][:, :IN_MAIN], "nn")[None]
    qkv3 = mm(f"l{i}_qkv", h1[0], wts["w_in"][:, IN_MAIN:], "nn")[None]
    y_ssd, sv_ssd = ssd_forward(f"l{i}_ssd", proj3, sp)
    y_pool, sv_pool = pool_forward(f"l{i}_pool", proj3, wts["wbd"], wts["pool_scale"])
    y_att, res_att = attention_forward(f"l{i}", qkv3, cs, sn, rot)
    mix = jnp.concatenate([y_ssd, y_pool, y_att], axis=-1)
    out, x1 = mm(f"l{i}_wout", mix[0], wts["w_out"], "nn", res=x3[0], gate=g1)
    x1 = x1[None]
    h2 = norm_mod_forward(f"l{i}_norm2", x1, wts["norm2_g"], sc2, sh2)
    up3 = mm(f"l{i}_up", h2[0], wts["ffn_up"], "nn")[None]
    act, sv_ffn = ffn_mid_forward(f"l{i}_ffn", up3, wts["ffn_conv_w"], wts["ffn_conv_b"])
    dn, x2 = mm(f"l{i}_down", act[0], wts["ffn_down"], "nn", res=x1[0], gate=g2)
    keep = dict(x=x3, h1=h1, proj3=proj3, qkv3=qkv3, sv_ssd=sv_ssd, sv_pool=sv_pool, res_att=res_att, mix=mix, out=out[None],
                x1=x1, h2=h2, up3=up3, act=act, sv_ffn=sv_ffn, dn=dn[None])
    return x2[None], keep


def _layer_backward(i, dx2, keep, modv, wts, sp, cs, sn, rot):
    sh1, sc1, g1, sh2, sc2, g2 = modv
    k = keep
    d_dn, d_g2 = gate_backward(f"l{i}_gate2_b", k["dn"], g2, dx2)
    d_act = mm(f"l{i}_down_bx", d_dn[0], wts["ffn_down"], "nt")
    g_down = mm(f"l{i}_down_bw", k["act"][0], d_dn[0], "tn")
    (d_up,), dv_ffn = ffn_mid_backward(f"l{i}_ffn_b", k["up3"], wts["ffn_conv_w"], wts["ffn_conv_b"], k["sv_ffn"], d_act[None])
    d_h2 = mm(f"l{i}_up_bx", d_up[0], wts["ffn_up"], "nt")
    g_up = mm(f"l{i}_up_bw", k["h2"][0], d_up[0], "tn")
    dx1, (d_n2, d_sc2, d_sh2) = norm_mod_backward(f"l{i}_norm2_b", k["x1"], wts["norm2_g"], sc2, sh2, d_h2[None], dx2)
    d_out, d_g1 = gate_backward(f"l{i}_gate1_b", k["out"], g1, dx1)
    d_mix = mm(f"l{i}_wout_bx", d_out[0], wts["w_out"], "nt")[None]
    g_wout = mm(f"l{i}_wout_bw", k["mix"][0], d_out[0], "tn")
    (dz, dxs, dbm, dcm, ddt), dv_ssd = ssd_backward(f"l{i}_ssd_b", k["proj3"], sp, k["sv_ssd"], d_mix)
    (du_pool,), (d_wbd, d_pscale) = pool_backward(f"l{i}_pool_b", k["proj3"], wts["wbd"], wts["pool_scale"], k["sv_pool"], d_mix)
    dq, dk, dv = attention_backward(f"l{i}", k["qkv3"], cs, sn, rot, k["res_att"], d_mix)
    d_proj = jnp.concatenate([dz[0], dxs[0], dbm[0], dcm[0], du_pool[0], (ddt[0] + ddt[1]).astype(BF16),
                              dq.astype(BF16), dk.astype(BF16), dv.astype(BF16)], axis=-1)
    d_h1 = mm(f"l{i}_proj_bx", d_proj, wts["w_in"], "nt")
    g_win = mm(f"l{i}_proj_bw", k["h1"][0], d_proj, "tn")
    dx, (d_n1, d_sc1, d_sh1) = norm_mod_backward(f"l{i}_norm1_b", k["x"], wts["norm1_g"], sc1, sh1, d_h1[None], dx1)
    dcwx, dcbx, dcwb, dcbb, dcwc, dcbc, ddtb, dalog, ddsk, dng = dv_ssd
    small = dict(
        norm1_g=d_n1[0], norm2_g=d_n2[0],
        ssd_conv_w=jnp.concatenate([dcwx[:, :512], dcwb[:, 512:768], dcwc[:, 768:]], axis=1),
        ssd_conv_b=jnp.concatenate([dcbx[0, :512], dcbb[0, 512:768], dcbc[0, 768:]]),
        ssd_dt_bias=ddtb[0, :8], ssd_a_log=dalog[0, :8], ssd_d=ddsk[0, :8], ssd_norm_g=dng[0],
        pool_w=jnp.stack([d_wbd[64 * g:64 * g + 64, 64 * g:64 * g + 64] for g in range(4)]), pool_scale=d_pscale[0],
        ffn_conv_w=_ffn_block_perm(dv_ffn[0]), ffn_conv_b=_ffn_block_perm(dv_ffn[1][0]),
    )
    dmod = jnp.concatenate([d_sh1[0], d_sc1[0], d_g1[0], d_sh2[0], d_sc2[0], d_g2[0]])
    return dx, dict(w_in=g_win, w_out=g_wout, ffn_up=g_up, ffn_down=g_down), small, dmod


def kernel(x, c, positions, ada_w, ada_b, norm1_g, w_in, ssd_conv_w, ssd_conv_b, ssd_dt_bias, ssd_a_log, ssd_d, ssd_norm_g, pool_w, pool_scale, w_out, norm2_g, ffn_up, ffn_conv_w, ffn_conv_b, ffn_down, final_g, loss_target, m_ada_w, m_ada_b, m_norm1_g, m_w_in, m_ssd_conv_w, m_ssd_conv_b, m_ssd_dt_bias, m_ssd_a_log, m_ssd_d, m_ssd_norm_g, m_pool_w, m_pool_scale, m_w_out, m_norm2_g, m_ffn_up, m_ffn_conv_w, m_ffn_conv_b, m_ffn_down, m_final_g, v_ada_w, v_ada_b, v_norm1_g, v_w_in, v_ssd_conv_w, v_ssd_conv_b, v_ssd_dt_bias, v_ssd_a_log, v_ssd_d, v_ssd_norm_g, v_pool_w, v_pool_scale, v_w_out, v_norm2_g, v_ffn_up, v_ffn_conv_w, v_ffn_conv_b, v_ffn_down, v_final_g):
    args = dict(locals())
    w = {n: args[n] for n in _WEIGHTS}
    m = {n: args["m_" + n] for n in _WEIGHTS}
    v = {n: args["v_" + n] for n in _WEIGHTS}
    d = D_MODEL
    me = (lax.axis_index("x"), lax.axis_index("y"), lax.axis_index("c"))
    chip, dev = _chip(me), _dev(me)

    shapes0 = [c.shape, ssd_conv_w.shape, ffn_conv_w.shape]
    g0 = allgather8("gather_c_conv", _pack([c, ssd_conv_w, ffn_conv_w]))
    c16 = jnp.pad(g0[:, :d // 128, :].reshape(8, d), ((0, 8), (0, 0)))
    by_chip = [_unpack(g0[2 * j], shapes0) for j in range(4)]
    conv_w_full = jnp.concatenate([p[1] for p in by_chip], axis=-1)
    fconv_w_full = jnp.concatenate([p[2] for p in by_chip], axis=-1)

    modp = ada_forward("ada_fwd", c16, ada_w)[:, :8]
    g1 = allgather8("gather_mod", _pack([modp]))
    modfull = jnp.concatenate([_unpack(g1[2 * j], [modp.shape])[0] for j in range(4)], axis=-1)
    mod = lax.dynamic_index_in_dim(modfull, dev, axis=1, keepdims=False) + ada_b
    modv = [[mod[i, q * d:(q + 1) * d][None] for q in range(6)] for i in range(DEPTH)]

    got = gather_layer_from_chips("gather_w", [w[n].astype(BF16) for n in _BIG])
    both = swap_layers("swap_w", got, me[2])
    full = dict(
        w_in=[_perm_cols(a.transpose(1, 0, 2).reshape(d, IN_W)) for a in both[0]],
        w_out=[a.reshape(d, d) for a in both[1]],
        ffn_up=[a[np.asarray(FFN_BLOCK_ORDER)].transpose(1, 0, 2).reshape(d, 2 * FFN_DIM) for a in both[2]],
        ffn_down=[a.reshape(FFN_DIM, d) for a in both[3]],
    )

    cs, sn, rot = rope_tables(positions[0])
    e_mat = jnp.asarray((np.arange(128)[:, None] == (np.arange(SSD_INNER)[None, :] // 64)).astype(np.float32))
    eye4 = jnp.eye(4, dtype=F32)
    wts, sps = [], []
    for i in range(DEPTH):
        wts.append(dict(
            w_in=full["w_in"][i], w_out=full["w_out"][i], ffn_up=full["ffn_up"][i], ffn_down=full["ffn_down"][i],
            norm1_g=norm1_g[i][None], norm2_g=norm2_g[i][None], pool_scale=pool_scale[i][None],
            wbd=(eye4[:, None, :, None] * pool_w[i][:, :, None, :]).reshape(POOL_W, POOL_W),
            ffn_conv_w=_ffn_block_perm(fconv_w_full[i]), ffn_conv_b=_ffn_block_perm(ffn_conv_b[i])[None]))
        sps.append(dict(cw=conv_w_full[i], cb=ssd_conv_b[i][None], dtb=_pad_lanes(ssd_dt_bias[i]), alog=_pad_lanes(ssd_a_log[i]),
                        dsk=_pad_lanes(ssd_d[i]), ng=ssd_norm_g[i][None], e=e_mat))

    xc, keeps = x, []
    for i in range(DEPTH):
        xc, keep = _layer_forward(i, xc, modv[i], wts[i], sps[i], cs, sn, rot)
        keeps.append(keep)
    lossblk, dx, d_final = final_loss("final_loss", xc, loss_target, final_g[None])
    loss = lax.psum(lossblk[0, 0], ("x", "y", "c"))

    big_g, small_g, dmods = [None] * DEPTH, [None] * DEPTH, [None] * DEPTH
    for i in reversed(range(DEPTH)):
        dx, big_g[i], small_g[i], dmods[i] = _layer_backward(i, dx, keeps[i], modv[i], wts[i], sps[i], cs, sn, rot)

    by_dest = [
        jnp.stack([_cols_by_chip(_unperm_cols(big_g[i]["w_in"]), IN_W // 4) for i in range(DEPTH)]),
        jnp.stack([big_g[i]["w_out"].reshape(4, d // 4, d) for i in range(DEPTH)]),
        jnp.stack([_cols_by_chip(big_g[i]["ffn_up"], 2 * FFN_DIM // 4, FFN_BLOCK_ORDER) for i in range(DEPTH)]),
        jnp.stack([big_g[i]["ffn_down"].reshape(4, FFN_DIM // 4, d) for i in range(DEPTH)]),
    ]
    theirs = swap_other_layer("swap_g", by_dest)
    core_sum = [sum_cores(f"sum_cores_{n}", g, t, BF16) for n, g, t in zip(_BIG, by_dest, theirs)]
    from_chips = scatter_to_chips("scatter_g", core_sum)
    chip_sum = [sum_slots(f"sum_chips_{n}", q, 4)[0] for n, q in zip(_BIG, from_chips)]
    reduced = swap_layers("swap_r", chip_sum, me[2])
    grads = {n: jnp.stack(r) for n, r in zip(_BIG, reduced)}

    part = dict(ada_b=jnp.stack(dmods), final_g=d_final[0])
    for n in _SMALL:
        if n not in part:
            part[n] = jnp.stack([small_g[i][n] for i in range(DEPTH)])
    full_shapes = [part[n].shape for n in _SMALL]
    gs = allgather8("gather_small", _pack([part[n] for n in _SMALL]))
    tot = _unpack(sum_slots("sum_small", gs, 8)[0], full_shapes)
    small_tot = dict(zip(_SMALL, tot))
    dmod_all = gs[:, :DEPTH * 6 * d // 128, :].reshape(8, DEPTH, 6 * d)
    for n, ncol in _COL_SHARDED_SMALL.items():
        small_tot[n] = lax.dynamic_slice_in_dim(small_tot[n], chip * ncol, ncol, axis=2)
    grads.update(small_tot)

    ncol = ada_w.shape[2]
    dm = lax.dynamic_slice_in_dim(dmod_all, chip * ncol, ncol, axis=2).transpose(1, 0, 2)
    upd = {}
    g_ada, *upd["ada_w"] = ada_backward("ada_bwd", c16, jnp.pad(dm, ((0, 0), (0, 8), (0, 0))), ada_w, m["ada_w"], v["ada_w"])
    grads["ada_w"] = g_ada

    for n in _BIG:
        upd[n] = adamw(f"adam_{n}", w[n], grads[n], m[n], v[n])
    shapes_s = [w[n].shape for n in _SMALL]
    packed = [_pack([src[n] for n in _SMALL]) for src in (w, grads, m, v)]
    outs_s = [_unpack(o, shapes_s) for o in adamw("adam_small", *packed)]
    for q, n in enumerate(_SMALL):
        upd[n] = [outs_s[0][q], outs_s[1][q], outs_s[2][q]]

    return (loss, dx, *[grads[n] for n in _WEIGHTS], *[upd[n][0] for n in _WEIGHTS], *[upd[n][1] for n in _WEIGHTS],
            *[upd[n][2] for n in _WEIGHTS])
```

```python
import functools
import math

import numpy as np
import jax
import jax.numpy as jnp
from jax import lax
from jax.experimental import pallas as pl
from jax.experimental.pallas import tpu as pltpu

F32 = jnp.float32
BF16 = jnp.bfloat16
HI = lax.Precision.HIGHEST
MESH = pl.DeviceIdType.MESH

D_MODEL = 1024
SEQ = 4096
DEPTH = 2
SSD_INNER = 512
SSD_HEADS = 8
SSD_STATE = 128
POOL_W = 256
POOL_WINDOWS = (2, 4, 8, 16)
ATT_W = 256
ATT_HEADS = 4
ATT_HEAD_DIM = 64
ATT_PATTERNS = ((128, 1), (512, 4), (2048, 16))
ATT_BLOCK = 128
ROT_DIM = 16
ROPE_THETA = 500000.0
IN_W = 2568
IN_WP = 2688
IN_MAIN = 1920
FFN_DIM = 2816
NORM_EPS = 1e-6
ADAM_LR, ADAM_B1, ADAM_B2, ADAM_EPS, ADAM_WD, ADAM_STEP = 0.001, 0.9, 0.999, 1e-08, 0.01, 10

VMEM_LIMIT_BYTES = 56 * 1024 * 1024
NEG = -1e30


def _mxu(a, b, mode):
    dims = {"nn": ((1,), (0,)), "nt": ((1,), (1,)), "tn": ((0,), (0,))}[mode]
    return lax.dot_general(a.astype(BF16), b.astype(BF16), (dims, ((), ())), preferred_element_type=F32)


@functools.partial(jax.custom_vjp, nondiff_argnums=(2,))
def _bdot(a, b, mode):
    return _mxu(a, b, mode)


def _bdot_fwd(a, b, mode):
    return _mxu(a, b, mode), (a, b)


def _bdot_bwd(mode, res, g):
    a, b = res
    if mode == "nn":
        return _mxu(g, b, "nt"), _mxu(a, g, "tn")
    if mode == "nt":
        return _mxu(g, b, "nn"), _mxu(g, a, "tn")
    return _mxu(b, g, "nt"), _mxu(a, g, "nn")


_bdot.defvjp(_bdot_fwd, _bdot_bwd)


def _fxu(a, b, mode):
    dims = {"nn": ((1,), (0,)), "nt": ((1,), (1,)), "tn": ((0,), (0,))}[mode]
    return lax.dot_general(a, b, (dims, ((), ())), precision=HI, preferred_element_type=F32)


@functools.partial(jax.custom_vjp, nondiff_argnums=(2,))
def _fdot(a, b, mode):
    return _fxu(a, b, mode)


def _fdot_fwd(a, b, mode):
    return _fxu(a, b, mode), (a, b)


def _fdot_bwd(mode, res, g):
    a, b = res
    if mode == "nn":
        return _fxu(g, b, "nt"), _fxu(a, g, "tn")
    if mode == "nt":
        return _fxu(g, b, "nn"), _fxu(g, a, "tn")
    return _fxu(b, g, "nt"), _fxu(a, g, "nn")


_fdot.defvjp(_fdot_fwd, _fdot_bwd)


def _iota(shape, dim):
    return lax.broadcasted_iota(jnp.int32, shape, dim)


def _make_shift(h):
    @functools.partial(jax.custom_vjp, nondiff_argnums=(2,))
    def shift(halo, cur, k):
        if k == 0:
            return cur
        full = jnp.concatenate([halo, cur], axis=0)
        return pltpu.roll(full, k, 0)[h:]

    def fwd(halo, cur, k):
        return shift(halo, cur, k), None

    def bwd(k, _, g):
        t, w = g.shape
        if k == 0:
            return jnp.zeros((h, w), F32), g
        d_cur = jnp.where(_iota((t, w), 0) < t - k, pltpu.roll(g, t - k, 0), 0.0)
        top = g[:h]
        d_halo = jnp.where(_iota((h, w), 0) >= h - k, pltpu.roll(top, h - k, 0) if k < h else top, 0.0)
        return d_halo, d_cur

    shift.defvjp(fwd, bwd)
    return shift


_shift8 = _make_shift(8)
_shift16 = _make_shift(16)


def _make_tail(h):
    @jax.custom_vjp
    def tail(x):
        return x[x.shape[0] - h:]

    def fwd(x):
        return tail(x), x.shape[0]

    def bwd(t, g):
        return (jnp.concatenate([jnp.zeros((t - h, g.shape[1]), F32), g], axis=0),)

    tail.defvjp(fwd, bwd)
    return tail


_tail8 = _make_tail(8)
_tail16 = _make_tail(16)


@jax.custom_vjp
def _cumsum_rows(x):
    t = x.shape[0]
    row, s = _iota(x.shape, 0), 1
    while s < t:
        x = x + jnp.where(row >= s, pltpu.roll(x, s, 0), 0.0)
        s *= 2
    return x


def _cumsum_rows_fwd(x):
    return _cumsum_rows(x), None


def _cumsum_rows_bwd(_, g):
    t = g.shape[0]
    row, s = _iota(g.shape, 0), 1
    while s < t:
        g = g + jnp.where(row < t - s, pltpu.roll(g, t - s, 0), 0.0)
        s *= 2
    return (g,)


_cumsum_rows.defvjp(_cumsum_rows_fwd, _cumsum_rows_bwd)


@jax.custom_vjp
def _rot_pairs(t):
    e = _iota(t.shape, 1) % ATT_HEAD_DIM
    n = t.shape[1]
    return jnp.where(e < 8, -pltpu.roll(t, n - 8, 1), jnp.where(e < 16, pltpu.roll(t, 8, 1), 0.0))


def _rot_pairs_fwd(t):
    return _rot_pairs(t), None


def _rot_pairs_bwd(_, g):
    e = _iota(g.shape, 1) % ATT_HEAD_DIM
    n = g.shape[1]
    return (pltpu.roll(jnp.where(e < 8, -g, 0.0), 8, 1) + pltpu.roll(jnp.where(jnp.logical_and(e >= 8, e < 16), g, 0.0), n - 8, 1),)


_rot_pairs.defvjp(_rot_pairs_fwd, _rot_pairs_bwd)


def _make_thirds():
    @jax.custom_vjp
    def thirds(x):
        w = x.shape[1] // 3
        return x[:, :w], x[:, w:2 * w], x[:, 2 * w:]

    def fwd(x):
        return thirds(x), None

    def bwd(_, g):
        return (jnp.concatenate(g, axis=1),)

    thirds.defvjp(fwd, bwd)
    return thirds


_thirds = _make_thirds()


def _rowk(w, k):
    return jnp.sum(jnp.where(_iota(w.shape, 0) == k, w, 0.0), axis=0, keepdims=True)


def _silu(x):
    return x * (0.5 * jnp.tanh(0.5 * x) + 0.5)


def _softplus(x):
    return jnp.maximum(x, 0.0) + jnp.log(1.0 + jnp.exp(-jnp.abs(x)))


def _tile(dim, target, unit=128):
    if dim <= target:
        return dim
    best = None
    for t in range(unit, target + 1, unit):
        if dim % t == 0:
            best = t
    assert best is not None, (dim, target)
    return best


class Row:
    def __init__(self, arr, w=None, fb=None, fc=None, diff=True, slot=False, dcols=None, dfc=None, ddtype=F32):
        self.ddtype = ddtype
        self.arr = arr
        self.w = arr.shape[2] if w is None else w
        self.fb = (lambda b: 0) if fb is None else fb
        self.fc = (lambda b: 0) if fc is None else fc
        self.diff = diff
        self.slot = slot
        self.dcols = dcols
        self.dfc = dfc


class Vec:
    def __init__(self, arr, w=None, fc=None, diff=True):
        self.arr = arr
        self.w = arr.shape[1] if w is None else w
        self.fc = fc
        self.diff = diff


def _row_spec(r, t, nchunk, reverse):
    if reverse:
        return pl.BlockSpec((1, t, r.w), lambda b, i, r=r: (r.fb(b), nchunk - 1 - i, r.fc(b)))
    return pl.BlockSpec((1, t, r.w), lambda b, i, r=r: (r.fb(b), i, r.fc(b)))


def _vec_spec(v):
    if v.fc is None:
        return pl.BlockSpec(v.arr.shape, lambda b, i: (0, 0))
    return pl.BlockSpec((v.arr.shape[0], v.w), lambda b, i, v=v: (0, v.fc(b)))


def _cparams():
    return pltpu.CompilerParams(dimension_semantics=("arbitrary", "arbitrary"), vmem_limit_bytes=VMEM_LIMIT_BYTES)


def scan_fwd(name, fn, *, nb, nchunk, t, rows, vecs, carries, outs, save):
    nr, nv, nc, no = len(rows), len(vecs), len(carries), len(outs)

    def body(*refs):
        row_refs, vec_refs = refs[:nr], refs[nr:nr + nv]
        out_refs = refs[nr + nv:nr + nv + no]
        save_refs = refs[nr + nv + no:nr + nv + no + (nc if save else 0)]
        car = refs[len(refs) - nc:] if nc else ()
        b, i = pl.program_id(0), pl.program_id(1)
        if nc:
            @pl.when(i == 0)
            def _():
                for c_ref in car:
                    c_ref[...] = jnp.zeros(c_ref.shape, F32)
        cin = [c_ref[...] for c_ref in car]
        if save:
            for s_ref, cv in zip(save_refs, cin):
                s_ref[0, 0] = cv
        new_c, o = fn(i, b, cin, [r[0] for r in row_refs], [v[...] for v in vec_refs])
        for c_ref, cv in zip(car, new_c):
            c_ref[...] = cv
        for o_ref, ov in zip(out_refs, o):
            o_ref[0] = ov.astype(o_ref.dtype)

    out_shape = [o.arr for o in outs]
    out_specs = [_row_spec(o, t, nchunk, False) for o in outs]
    if save:
        for cs in carries:
            out_shape.append(jax.ShapeDtypeStruct((nb, nchunk) + tuple(cs), F32))
            out_specs.append(pl.BlockSpec((1, 1) + tuple(cs), lambda b, i: (b, i, 0, 0)))
    res = pl.pallas_call(
        body, name=name, grid=(nb, nchunk),
        in_specs=[_row_spec(r, t, nchunk, False) for r in rows] + [_vec_spec(v) for v in vecs],
        out_specs=out_specs, out_shape=out_shape,
        scratch_shapes=[pltpu.VMEM(tuple(cs), F32) for cs in carries],
        compiler_params=_cparams(),
    )(*[r.arr for r in rows], *[v.arr for v in vecs])
    return list(res[:no]), list(res[no:])


def scan_bwd(name, fn, *, nb, nchunk, t, rows, vecs, carries, saved, douts, adds=None):
    adds = adds or {}
    nr, nv, nc, no = len(rows), len(vecs), len(carries), len(douts)
    dri = [k for k, r in enumerate(rows) if r.diff]
    dvi = [k for k, v in enumerate(vecs) if v.diff]
    add_keys = sorted(adds)
    na = len(add_keys)

    def body(*refs):
        p = 0
        row_refs = refs[p:p + nr]; p += nr
        vec_refs = refs[p:p + nv]; p += nv
        save_refs = refs[p:p + nc]; p += nc
        dout_refs = refs[p:p + no]; p += no
        add_refs = refs[p:p + na]; p += na
        drow_refs = refs[p:p + len(dri)]; p += len(dri)
        dvec_refs = refs[p:p + len(dvi)]; p += len(dvi)
        dcar = refs[p:]
        b, ir = pl.program_id(0), pl.program_id(1)
        ci = nchunk - 1 - ir
        if nc:
            @pl.when(ir == 0)
            def _():
                for c_ref in dcar:
                    c_ref[...] = jnp.zeros(c_ref.shape, F32)
        rows_v = [r[0] for r in row_refs]
        vecs_v = [v[...] for v in vec_refs]
        cin = [s[0, 0] for s in save_refs]
        dc = [c_ref[...] for c_ref in dcar]
        dout_v = [d[0].astype(F32) for d in dout_refs]

        def f(cs, dr, dv):
            rr, vv = list(rows_v), list(vecs_v)
            for k, idx in enumerate(dri):
                rr[idx] = dr[k]
            for k, idx in enumerate(dvi):
                vv[idx] = dv[k]
            return fn(ci, b, cs, rr, vv)

        _, vjp = jax.vjp(f, cin, [rows_v[k].astype(F32) for k in dri], [vecs_v[k].astype(F32) for k in dvi])
        dcin, drows, dvecs = vjp((dc, dout_v))
        for c_ref, cv in zip(dcar, dcin):
            c_ref[...] = cv
        for k, (o_ref, ov) in enumerate(zip(drow_refs, drows)):
            if dri[k] in adds:
                ov = ov + add_refs[add_keys.index(dri[k])][0].astype(F32)
            o_ref[0] = ov.astype(o_ref.dtype)
        for k, (o_ref, ov) in enumerate(zip(dvec_refs, dvecs)):
            first = (ir == 0) if vecs[dvi[k]].fc is not None else jnp.logical_and(ir == 0, b == 0)

            @pl.when(first)
            def _(o_ref=o_ref, ov=ov):
                o_ref[...] = ov

            @pl.when(jnp.logical_not(first))
            def _(o_ref=o_ref, ov=ov):
                o_ref[...] += ov

    in_specs = ([_row_spec(r, t, nchunk, True) for r in rows] + [_vec_spec(v) for v in vecs]
                + [pl.BlockSpec((1, 1) + tuple(cs), lambda b, i: (b, nchunk - 1 - i, 0, 0)) for cs in carries]
                + [_row_spec(d, t, nchunk, True) for d in douts]
                + [_row_spec(adds[k], t, nchunk, True) for k in add_keys])
    out_shape, out_specs = [], []
    for k in dri:
        r = rows[k]
        if r.slot:
            out_shape.append(jax.ShapeDtypeStruct((nb, r.arr.shape[1], r.w), r.ddtype))
            out_specs.append(pl.BlockSpec((1, t, r.w), lambda b, i: (b, nchunk - 1 - i, 0)))
        elif r.dcols is not None:
            out_shape.append(jax.ShapeDtypeStruct((r.arr.shape[0], r.arr.shape[1], r.dcols), r.ddtype))
            out_specs.append(pl.BlockSpec((1, t, r.w), lambda b, i, r=r: (r.fb(b), nchunk - 1 - i, r.dfc(b))))
        else:
            out_shape.append(jax.ShapeDtypeStruct(r.arr.shape, r.ddtype))
            out_specs.append(_row_spec(r, t, nchunk, True))
    for k in dvi:
        out_shape.append(jax.ShapeDtypeStruct(vecs[k].arr.shape, F32))
        out_specs.append(_vec_spec(vecs[k]))
    res = pl.pallas_call(
        body, name=name, grid=(nb, nchunk), in_specs=in_specs, out_specs=out_specs, out_shape=out_shape,
        scratch_shapes=[pltpu.VMEM(tuple(cs), F32) for cs in carries],
        compiler_params=_cparams(),
    )(*[r.arr for r in rows], *[v.arr for v in vecs], *saved, *[d.arr for d in douts], *[adds[k].arr for k in add_keys])
    return list(res[:len(dri)]), list(res[len(dri):])


def out_row(shape, dtype=F32, w=None, fb=None, fc=None):
    return Row(jax.ShapeDtypeStruct(shape, dtype), w, fb, fc)


def _conv(shift, halo, cur, w, bias, taps):
    y = bias
    for k in range(taps):
        y = y + _rowk(w, k) * shift(halo, cur, taps - 1 - k)
    return y


def _ssd_fn(ci, b, carries, rows, vecs):
    cx, cb_, cc, ht = carries
    z, xr, br, cr, dtr = rows
    cwx, cbx, cwb, cbb, cwc, cbc, dtb, alog, dsk, ng = vecs
    t = z.shape[0]
    xs = _silu(_conv(_shift8, cx, xr, cwx, cbx, 4))
    bm = _silu(_conv(_shift8, cb_, br, cwb, cbb, 4))
    cm = _silu(_conv(_shift8, cc, cr, cwc, cbc, 4))
    dt = _softplus(dtr + dtb)
    acol = _cumsum_rows(dt * (-jnp.exp(alog)))
    arow = acol.T
    r, c = _iota((t, t), 0), _iota((t, t), 1)
    causal = r >= c
    cbm = _bdot(cm, bm, "nt")
    lane, sub = _iota(acol.shape, 1), _iota(arow.shape, 0)
    colh = _iota(xs.shape, 1) // 64
    a, dtx, dx, acs = jnp.zeros(xs.shape, F32), jnp.zeros(xs.shape, F32), jnp.zeros((1, xs.shape[1]), F32), []
    for j in range(4):
        h = 4 * b + j
        ac = jnp.sum(jnp.where(lane == h, acol, 0.0), axis=1, keepdims=True)
        acs.append(ac)
        a = jnp.where(colh == j, ac, a)
        dtx = jnp.where(colh == j, jnp.sum(jnp.where(lane == h, dt, 0.0), axis=1, keepdims=True), dtx)
        dx = jnp.where(_iota(dx.shape, 1) // 64 == j, jnp.sum(jnp.where(_iota(dsk.shape, 1) == h, dsk, 0.0), axis=1, keepdims=True), dx)
    atot = jnp.sum(jnp.where(_iota(a.shape, 0) == t - 1, a, 0.0), axis=0, keepdims=True)
    x = xs * dtx
    ydiag = jnp.zeros(x.shape, F32)
    for j in range(4):
        ar = jnp.sum(jnp.where(sub == 4 * b + j, arow, 0.0), axis=0, keepdims=True)
        lmat = jnp.exp(jnp.where(causal, acs[j] - ar, NEG))
        ydiag = ydiag + _bdot(cbm * lmat, jnp.where(colh == j, x, 0.0), "nn")
    yoff = _bdot(cm, ht, "nn") * jnp.exp(a)
    ht_new = ht * jnp.exp(atot) + _bdot(bm, x * jnp.exp(atot - a), "tn")
    y = ydiag + yoff + dx * xs
    yz = y * _silu(z)
    yn = yz * lax.rsqrt(jnp.mean(yz * yz, axis=-1, keepdims=True) + NORM_EPS) * ng
    return [_tail8(xr), _tail8(br), _tail8(cr), ht_new], [yn]


_SSD_T = 256
_SSD_CARRIES = [(8, 256), (8, 128), (8, 128), (128, 256)]


def _ssd_io(proj3, p):
    own = lambda b: b
    rows = [Row(proj3, 256, fc=own, dcols=512, dfc=own, ddtype=BF16),
            Row(proj3, 256, fc=lambda b: 2 + b, dcols=512, dfc=own, ddtype=BF16),
            Row(proj3, 128, fc=lambda b: 8 + b, dcols=256, dfc=own, ddtype=BF16),
            Row(proj3, 128, fc=lambda b: 10 + b, dcols=256, dfc=own, ddtype=BF16),
            Row(proj3, 128, fc=lambda b: 14, slot=True)]
    vecs = [Vec(p["cw"], 256, lambda b: b), Vec(p["cb"], 256, lambda b: b),
            Vec(p["cw"], 128, lambda b: 4 + b), Vec(p["cb"], 128, lambda b: 4 + b),
            Vec(p["cw"], 128, lambda b: 6 + b), Vec(p["cb"], 128, lambda b: 6 + b),
            Vec(p["dtb"]), Vec(p["alog"]), Vec(p["dsk"]), Vec(p["ng"], 256, lambda b: b)]
    return rows, vecs


def ssd_forward(name, proj3, p):
    rows, vecs = _ssd_io(proj3, p)
    s = proj3.shape[1]
    (y,), saved = scan_fwd(name, _ssd_fn, nb=2, nchunk=s // _SSD_T, t=_SSD_T, rows=rows, vecs=vecs,
                           carries=_SSD_CARRIES, outs=[out_row((1, s, SSD_INNER), BF16, 256, fc=lambda b: b)], save=True)
    return y, saved


def ssd_backward(name, proj3, p, saved, dmix3):
    rows, vecs = _ssd_io(proj3, p)
    s = proj3.shape[1]
    drows, dvecs = scan_bwd(name, _ssd_fn, nb=2, nchunk=s // _SSD_T, t=_SSD_T, rows=rows, vecs=vecs,
                            carries=_SSD_CARRIES, saved=saved, douts=[Row(dmix3, 256, fc=lambda b: b)])
    return drows, dvecs


def _pool_fn(ci, b, carries, rows, vecs):
    (cu,) = carries
    (u,) = rows
    wbd, scale = vecs
    t = u.shape[0]
    pos = ci * t + _iota(u.shape, 0)
    grp = _iota(u.shape, 1) // 64
    acc, pooled, k = u, jnp.zeros(u.shape, F32), 1
    for gi, w in enumerate(POOL_WINDOWS):
        while k < w:
            acc = acc + _shift16(cu, u, k)
            k += 1
        pooled = jnp.where(grp == gi, acc / jnp.minimum(pos + 1, w).astype(F32), pooled)
    y = _bdot(pooled - u, wbd, "nn") * scale
    return [_tail16(u)], [y]


_POOL_T = 256


def _pool_io(proj3, wbd, scale):
    return [Row(proj3, 256, fc=lambda b: 6, dcols=256, dfc=lambda b: 0, ddtype=BF16)], [Vec(wbd), Vec(scale)]


def pool_forward(name, proj3, wbd, scale):
    rows, vecs = _pool_io(proj3, wbd, scale)
    s = proj3.shape[1]
    (y,), saved = scan_fwd(name, _pool_fn, nb=1, nchunk=s // _POOL_T, t=_POOL_T, rows=rows, vecs=vecs,
                           carries=[(16, 256)], outs=[out_row((1, s, POOL_W), BF16)], save=True)
    return y, saved


def pool_backward(name, proj3, wbd, scale, saved, dmix3):
    rows, vecs = _pool_io(proj3, wbd, scale)
    s = proj3.shape[1]
    return scan_bwd(name, _pool_fn, nb=1, nchunk=s // _POOL_T, t=_POOL_T, rows=rows, vecs=vecs,
                    carries=[(16, 256)], saved=saved, douts=[Row(dmix3, 256, fc=lambda b: 2)])


def _attn_fn(ci, b, carries, rows, vecs):
    kp, vp = carries
    qr, kr, v = _thirds(rows[0])
    scale = ATT_HEAD_DIM ** -0.5
    q = qr
    n = q.shape[0]
    r, c = _iota((n, n), 0), _iota((n, n), 1)
    prev_ok, cur_ok = jnp.logical_and(c >= r, ci > 0), r >= c
    head = _iota(q.shape, 1) // ATT_HEAD_DIM
    o, lse = jnp.zeros(q.shape, F32), jnp.zeros(q.shape, F32)
    for h in range(ATT_HEADS):
        mine = head == h
        qh = jnp.where(mine, qr, 0.0)
        sp = jnp.where(prev_ok, _bdot(qh, kp, "nt") * scale, NEG)
        sc = jnp.where(cur_ok, _bdot(qh, kr, "nt") * scale, NEG)
        m = lax.stop_gradient(jnp.maximum(jnp.max(sp, axis=1, keepdims=True), jnp.max(sc, axis=1, keepdims=True)))
        pp, pc = jnp.exp(sp - m), jnp.exp(sc - m)
        l = jnp.sum(pp, axis=1, keepdims=True) + jnp.sum(pc, axis=1, keepdims=True)
        o = jnp.where(mine, (_bdot(pp, vp, "nn") + _bdot(pc, v, "nn")) / l, o)
        lse = jnp.where(mine, m + jnp.log(l), lse)
    return [kr, v], [o, lse]


_ATT_CARRIES = [(ATT_BLOCK, ATT_W), (ATT_BLOCK, ATT_W)]


def attn_forward(name, pv, d):
    l = pv.shape[1]
    own = lambda b: b
    outs = [out_row((1, l, d * ATT_W), F32, ATT_W, fc=own) for _ in range(2)]
    (o, lse), saved = scan_fwd(name, _attn_fn, nb=d, nchunk=l // ATT_BLOCK, t=ATT_BLOCK, rows=[Row(pv, 3 * ATT_W, fc=own)],
                               vecs=[], carries=_ATT_CARRIES, outs=outs, save=True)
    return o, lse, saved


def attn_backward(name, pv, d, saved, do, dlse):
    l = pv.shape[1]
    own = lambda b: b
    (dpv,), _ = scan_bwd(name, _attn_fn, nb=d, nchunk=l // ATT_BLOCK, t=ATT_BLOCK, rows=[Row(pv, 3 * ATT_W, fc=own)], vecs=[],
                         carries=_ATT_CARRIES, saved=saved, douts=[Row(do, ATT_W, fc=own), Row(dlse, ATT_W, fc=own)])
    return dpv


def _rope_fn(ci, b, carries, rows, vecs):
    x, cs, sn = rows
    return [], [x * cs + _rot_pairs(x) * sn]


def _rope3_fn(ci, b, carries, rows, vecs):
    _, (y,) = _rope_fn(ci, b, carries, rows, vecs)
    return [], [y, y, y]


def rope_forward(name, qkv3, cs3, sn3):
    s = qkv3.shape[1]
    (y,), _ = scan_fwd(name, _rope_fn, nb=1, nchunk=s // _ROW_T, t=_ROW_T, vecs=[], carries=[], save=False,
                       rows=[Row(qkv3), Row(cs3, diff=False), Row(sn3, diff=False)], outs=[out_row(qkv3.shape)])
    return y


def rope_backward(name, qkv3, cs3, sn3, dys):
    s = qkv3.shape[1]
    (dx,), _ = scan_bwd(name, _rope3_fn, nb=1, nchunk=s // _ROW_T, t=_ROW_T, vecs=[], carries=[], saved=[],
                        rows=[Row(qkv3, ddtype=BF16), Row(cs3, diff=False), Row(sn3, diff=False)], douts=[Row(a) for a in dys])
    return dx


def _merge_fn(ci, b, carries, rows, vecs):
    o1, o2, o3, l1, l2, l3 = rows
    mx = lax.stop_gradient(jnp.maximum(l1, jnp.maximum(l2, l3)))
    e1, e2, e3 = jnp.exp(l1 - mx), jnp.exp(l2 - mx), jnp.exp(l3 - mx)
    return [], [(e1 * o1 + e2 * o2 + e3 * o3) / (e1 + e2 + e3)]


_ROW_T = 256


def merge_forward(name, os_, ls_):
    s = os_[0].shape[1]
    (y,), _ = scan_fwd(name, _merge_fn, nb=1, nchunk=s // _ROW_T, t=_ROW_T, rows=[Row(a) for a in (*os_, *ls_)], vecs=[],
                       carries=[], outs=[out_row((1, s, ATT_W), BF16)], save=False)
    return y


def merge_backward(name, os_, ls_, dmix3):
    s = os_[0].shape[1]
    drows, _ = scan_bwd(name, _merge_fn, nb=1, nchunk=s // _ROW_T, t=_ROW_T, rows=[Row(a) for a in (*os_, *ls_)], vecs=[],
                        carries=[], saved=[], douts=[Row(dmix3, 256, fc=lambda b: 3)])
    return drows


def _norm_mod_fn(ci, b, carries, rows, vecs):
    (x,) = rows
    g, sc, sh = vecs
    xn = x * lax.rsqrt(jnp.mean(x * x, axis=-1, keepdims=True) + NORM_EPS)
    return [], [xn * g * (1.0 + sc) + sh]


def norm_mod_forward(name, x3, g, sc, sh):
    s = x3.shape[1]
    (h,), _ = scan_fwd(name, _norm_mod_fn, nb=1, nchunk=s // _ROW_T, t=_ROW_T, rows=[Row(x3)], vecs=[Vec(g), Vec(sc), Vec(sh)],
                       carries=[], outs=[out_row(x3.shape, BF16)], save=False)
    return h


def norm_mod_backward(name, x3, g, sc, sh, dh3, add3):
    s = x3.shape[1]
    (dx,), dv = scan_bwd(name, _norm_mod_fn, nb=1, nchunk=s // _ROW_T, t=_ROW_T, rows=[Row(x3)], vecs=[Vec(g), Vec(sc), Vec(sh)],
                         carries=[], saved=[], douts=[Row(dh3)], adds={0: Row(add3)})
    return dx, dv


def _gate_fn(ci, b, carries, rows, vecs):
    return [], [rows[0] * vecs[0]]


def gate_backward(name, o3, g, dx3):
    s = o3.shape[1]
    (do,), (dg,) = scan_bwd(name, _gate_fn, nb=1, nchunk=s // _ROW_T, t=_ROW_T, rows=[Row(o3, ddtype=BF16)], vecs=[Vec(g)],
                            carries=[], saved=[], douts=[Row(dx3)])
    return do, dg


def _make_halves():
    @jax.custom_vjp
    def halves(x):
        h = x.shape[1] // 2
        return x[:, :h], x[:, h:]

    def fwd(x):
        return halves(x), None

    def bwd(_, g):
        return (jnp.concatenate(g, axis=1),)

    halves.defvjp(fwd, bwd)
    return halves


_halves = _make_halves()


def _ffn_fn(ci, b, carries, rows, vecs):
    (cu,) = carries
    (u,) = rows
    w, bias = vecs
    hg, hu = _halves(_conv(_shift8, cu, u, w, bias, 3))
    return [_tail8(u)], [_silu(hg) * hu]


_FFN_T = 256
_FFN_CW = FFN_DIM // 2
_FFN_CARRIES = [(8, 2 * _FFN_CW)]
FFN_BLOCK_ORDER = [0, 2, 1, 3]


def _ffn_io(up3, cw, cb):
    own = lambda b: b
    return [Row(up3, 2 * _FFN_CW, fc=own, ddtype=BF16)], [Vec(cw, 2 * _FFN_CW, own), Vec(cb, 2 * _FFN_CW, own)]


def ffn_mid_forward(name, up3, cw, cb):
    rows, vecs = _ffn_io(up3, cw, cb)
    s = up3.shape[1]
    (act,), saved = scan_fwd(name, _ffn_fn, nb=2, nchunk=s // _FFN_T, t=_FFN_T, rows=rows, vecs=vecs, carries=_FFN_CARRIES,
                             outs=[out_row((1, s, FFN_DIM), BF16, _FFN_CW, fc=lambda b: b)], save=True)
    return act, saved


def ffn_mid_backward(name, up3, cw, cb, saved, dact3):
    rows, vecs = _ffn_io(up3, cw, cb)
    s = up3.shape[1]
    return scan_bwd(name, _ffn_fn, nb=2, nchunk=s // _FFN_T, t=_FFN_T, rows=rows, vecs=vecs, carries=_FFN_CARRIES,
                    saved=saved, douts=[Row(dact3, _FFN_CW, fc=lambda b: b)])


def _adam_fn(ci, b, carries, rows, vecs):
    w, g, m, v = rows
    m = ADAM_B1 * m + (1.0 - ADAM_B1) * g
    v = ADAM_B2 * v + (1.0 - ADAM_B2) * (g * g)
    m_hat = m / (1.0 - ADAM_B1 ** ADAM_STEP)
    v_hat = v / (1.0 - ADAM_B2 ** ADAM_STEP)
    delta = -ADAM_LR * (m_hat / (jnp.sqrt(v_hat) + ADAM_EPS) + ADAM_WD * w)
    return [], [delta, m, v]


def adamw(name, w, g, m, v):
    shape = w.shape
    c = shape[-1]
    r = int(np.prod(shape[:-1]))
    t = _tile(r, 256, 8)
    as3 = lambda a: a.reshape(1, r, c)
    outs, _ = scan_fwd(name, _adam_fn, nb=1, nchunk=r // t, t=t, rows=[Row(as3(a)) for a in (w, g, m, v)], vecs=[], carries=[],
                       outs=[out_row((1, r, c)) for _ in range(3)], save=False)
    return [o.reshape(shape) for o in outs]


def rope_tables(positions):
    inv_freq = ROPE_THETA ** (-jnp.arange(0, ROT_DIM, 2, dtype=F32) / ROT_DIM)
    ang = positions.astype(F32)[:, None] * inv_freq
    s = positions.shape[0]
    cs = jnp.concatenate([jnp.cos(ang), jnp.cos(ang), jnp.ones((s, ATT_HEAD_DIM - ROT_DIM), F32)], axis=1)
    sn = jnp.concatenate([jnp.sin(ang), jnp.sin(ang), jnp.zeros((s, ATT_HEAD_DIM - ROT_DIM), F32)], axis=1)
    cs3 = jnp.concatenate([jnp.tile(cs, (1, 2 * ATT_HEADS)), jnp.ones((s, ATT_W), F32)], axis=1)
    sn3 = jnp.concatenate([jnp.tile(sn, (1, 2 * ATT_HEADS)), jnp.zeros((s, ATT_W), F32)], axis=1)
    return cs3[None], sn3[None]


def attention_forward(lname, qkv3, cs3, sn3):
    s = qkv3.shape[1]
    rotated = rope_forward(f"{lname}_rope", qkv3, cs3, sn3)
    os_, ls_, keep = [], [], []
    for pi, (_, d) in enumerate(ATT_PATTERNS):
        o, lse, saved = attn_forward(f"{lname}_attn{pi}", rotated.reshape(1, s // d, d * 3 * ATT_W), d)
        os_.append(o.reshape(1, s, ATT_W))
        ls_.append(lse.reshape(1, s, ATT_W))
        keep.append(saved)
    y = merge_forward(f"{lname}_merge", os_, ls_)
    return y, (rotated, os_, ls_, keep)


def attention_backward(lname, qkv3, cs3, sn3, res, dmix3):
    rotated, os_, ls_, keep = res
    s = qkv3.shape[1]
    dm = merge_backward(f"{lname}_merge_b", os_, ls_, dmix3)
    dys = []
    for pi, (_, d) in enumerate(ATT_PATTERNS):
        view = lambda a: a.reshape(1, s // d, d * a.shape[-1])
        dys.append(attn_backward(f"{lname}_attn{pi}_b", view(rotated), d, keep[pi], view(dm[pi]), view(dm[3 + pi])).reshape(qkv3.shape))
    return rope_backward(f"{lname}_rope_b", qkv3, cs3, sn3, dys)


def mm(name, a, b, mode, out_dtype=F32, res=None, gate=None, tm=1024, tn=1536, tk=1024):
    if mode == "nn":
        (m, k), n = a.shape, b.shape[1]
    elif mode == "nt":
        (m, k), n = a.shape, b.shape[0]
    else:
        (k, m), n = a.shape, b.shape[1]
    tm, tn, tk = _tile(m, tm), _tile(n, tn), _tile(k, tk)
    nk = k // tk
    a_spec = pl.BlockSpec((tk, tm), lambda i, j, q: (q, i)) if mode == "tn" else pl.BlockSpec((tm, tk), lambda i, j, q: (i, q))
    b_spec = pl.BlockSpec((tn, tk), lambda i, j, q: (j, q)) if mode == "nt" else pl.BlockSpec((tk, tn), lambda i, j, q: (q, j))
    o_spec = pl.BlockSpec((tm, tn), lambda i, j, q: (i, j))
    fused = res is not None

    def body(*refs):
        if fused:
            a_ref, b_ref, r_ref, g_ref, o_ref, o2_ref, acc = refs
        else:
            a_ref, b_ref, o_ref, acc = refs
        q = pl.program_id(2)

        @pl.when(q == 0)
        def _():
            acc[...] = jnp.zeros(acc.shape, F32)

        acc[...] += _mxu(a_ref[...], b_ref[...], mode)

        @pl.when(q == nk - 1)
        def _():
            o_ref[...] = acc[...].astype(o_ref.dtype)
            if fused:
                o2_ref[...] = r_ref[...] + g_ref[...] * acc[...]

    ins, in_specs = [a, b], [a_spec, b_spec]
    out_shape, out_specs = [jax.ShapeDtypeStruct((m, n), out_dtype)], [o_spec]
    if fused:
        ins += [res, gate]
        in_specs += [o_spec, pl.BlockSpec((1, tn), lambda i, j, q: (0, j))]
        out_shape.append(jax.ShapeDtypeStruct((m, n), F32))
        out_specs.append(o_spec)
    out = pl.pallas_call(
        body, name=name, grid=(m // tm, n // tn, nk), in_specs=in_specs, out_specs=out_specs, out_shape=out_shape,
        scratch_shapes=[pltpu.VMEM((tm, tn), F32)],
        compiler_params=pltpu.CompilerParams(dimension_semantics=("parallel", "parallel", "arbitrary"),
                                             vmem_limit_bytes=VMEM_LIMIT_BYTES),
    )(*ins)
    return tuple(out) if fused else out[0]


def final_loss(name, x3, t3, g):
    s, d = x3.shape[1], x3.shape[2]
    t = _ROW_T

    def body(x_ref, t_ref, g_ref, loss_ref, dx_ref, dg_ref):
        i = pl.program_id(0)
        tv = t_ref[0]

        def f(x, gg):
            y = x * lax.rsqrt(jnp.mean(x * x, axis=-1, keepdims=True) + NORM_EPS) * gg
            e = y - tv
            return 0.5 * jnp.sum(jnp.mean(e * e, axis=-1, keepdims=True), axis=0, keepdims=True)

        l, vjp = jax.vjp(f, x_ref[0], g_ref[...])
        dx, dg = vjp(jnp.ones((1, 1), F32))
        dx_ref[0] = dx

        @pl.when(i == 0)
        def _():
            loss_ref[...] = jnp.zeros(loss_ref.shape, F32)
            dg_ref[...] = jnp.zeros(dg_ref.shape, F32)

        loss_ref[...] += jnp.broadcast_to(l, loss_ref.shape)
        dg_ref[...] += dg

    row = pl.BlockSpec((1, t, d), lambda i: (0, i, 0))
    vec = pl.BlockSpec((1, d), lambda i: (0, 0))
    return pl.pallas_call(
        body, name=name, grid=(s // t,), in_specs=[row, row, vec],
        out_specs=[pl.BlockSpec((8, 128), lambda i: (0, 0)), row, vec],
        out_shape=[jax.ShapeDtypeStruct((8, 128), F32), jax.ShapeDtypeStruct(x3.shape, F32), jax.ShapeDtypeStruct((1, d), F32)],
        compiler_params=pltpu.CompilerParams(dimension_semantics=("arbitrary",), vmem_limit_bytes=VMEM_LIMIT_BYTES),
    )(x3, t3, g)


_ADA_TN = 512


def ada_forward(name, c16, ada_w):
    depth, d, cols = ada_w.shape

    def body(c_ref, w_ref, o_ref):
        o_ref[0] = _mxu(_silu(c_ref[...]), w_ref[0], "nn")

    return pl.pallas_call(
        body, name=name, grid=(depth, cols // _ADA_TN),
        in_specs=[pl.BlockSpec((16, d), lambda l, j: (0, 0)), pl.BlockSpec((1, d, _ADA_TN), lambda l, j: (l, 0, j))],
        out_specs=pl.BlockSpec((1, 16, _ADA_TN), lambda l, j: (l, 0, j)),
        out_shape=jax.ShapeDtypeStruct((depth, 16, cols), F32),
        compiler_params=pltpu.CompilerParams(dimension_semantics=("arbitrary", "arbitrary"), vmem_limit_bytes=VMEM_LIMIT_BYTES),
    )(c16, ada_w)


def ada_backward(name, c16, dmod16, w, m, v):
    depth, d, cols = w.shape

    def body(c_ref, dm_ref, w_ref, m_ref, v_ref, g_ref, dl_ref, nm_ref, nv_ref):
        g = _mxu(_silu(c_ref[...]), dm_ref[0], "tn")
        _, (delta, nm, nv) = _adam_fn(None, None, [], [w_ref[0], g, m_ref[0], v_ref[0]], [])
        g_ref[0], dl_ref[0], nm_ref[0], nv_ref[0] = g, delta, nm, nv

    blk = pl.BlockSpec((1, d, _ADA_TN), lambda l, j: (l, 0, j))
    return pl.pallas_call(
        body, name=name, grid=(depth, cols // _ADA_TN),
        in_specs=[pl.BlockSpec((16, d), lambda l, j: (0, 0)), pl.BlockSpec((1, 16, _ADA_TN), lambda l, j: (l, 0, j)), blk, blk, blk],
        out_specs=[blk] * 4, out_shape=[jax.ShapeDtypeStruct(w.shape, F32)] * 4,
        compiler_params=pltpu.CompilerParams(dimension_semantics=("arbitrary", "arbitrary"), vmem_limit_bytes=VMEM_LIMIT_BYTES),
    )(c16, dmod16, w, m, v)


def _sum_fn(ci, b, carries, rows, vecs):
    acc = rows[0]
    for r in rows[1:]:
        acc = acc + r
    return [], [acc]


def sum_slots(name, a, nsum, out_dtype=F32):
    n, r, c = a.shape
    nb = n // nsum
    t = _tile(r, 256, 8)
    rows = [Row(a, fb=(lambda b, k=k: k * nb + b)) for k in range(nsum)]
    (out,), _ = scan_fwd(name, _sum_fn, nb=nb, nchunk=r // t, t=t, rows=rows, vecs=[], carries=[],
                         outs=[out_row((nb, r, c), out_dtype, fb=lambda b: b)], save=False)
    return out


def _sum_my_layer_fn(ci, b, carries, rows, vecs):
    layer0, layer1, theirs = rows
    return [], [jnp.where(lax.axis_index("c") == 0, layer0, layer1) + theirs]


def sum_cores(name, g, theirs, out_dtype):
    _, nb, r, c = g.shape
    g8 = g.reshape(2 * nb, r, c)
    t = _tile(r, 256, 8)
    rows = [Row(g8, fb=lambda b: b), Row(g8, fb=lambda b: nb + b), Row(theirs, fb=lambda b: b)]
    (out,), _ = scan_fwd(name, _sum_my_layer_fn, nb=nb, nchunk=r // t, t=t, rows=rows, vecs=[], carries=[],
                         outs=[out_row((nb, r, c), out_dtype, fb=lambda b: b)], save=False)
    return out


def _flip(mask, pos):
    return tuple((1 - p) if m else p for m, p in zip(mask, pos))


ALL_PEERS = [(a, b, c) for a in (0, 1) for b in (0, 1) for c in (0, 1)][1:]
CHIP_PEERS = [(1, 0, 0), (0, 1, 0), (1, 1, 0)]
SIBLING = [(0, 0, 1)]


def _divisor(size, target, unit):
    best = 1
    for n in range(1, target + 1):
        if size % n == 0 and (size // n) % unit == 0:
            best = n
    return best


def _pieces(src, dst, pieces):
    shape = src.shape
    unit = 16 if src.dtype == BF16 else 8
    if pieces <= 1:
        return [(src, dst)]
    if len(shape) == 2:
        n = _divisor(shape[0], pieces, unit)
        s = shape[0] // n
        return [(src.at[pl.ds(i * s, s)], dst.at[pl.ds(i * s, s)]) for i in range(n)]
    assert len(shape) == 3, shape
    n = _divisor(shape[1], max(pieces // shape[0], 1), unit)
    s = shape[1] // n
    return [(src.at[j, pl.ds(i * s, s)], dst.at[j, pl.ds(i * s, s)]) for j in range(shape[0]) for i in range(n)]


def comm_call(name, arrays, out_shapes, masks, src_fn, dst_fn, local_fn=None, pieces=1):
    na, npeer = len(arrays), len(masks)

    def body(*refs):
        ins, outs = refs[:na], refs[na:2 * na]
        send_sems, recv_sems, loc_sems = refs[2 * na:]
        me = (lax.axis_index("x"), lax.axis_index("y"), lax.axis_index("c"))
        local = []
        if local_fn is not None:
            for k in range(na):
                s, d = local_fn(k, ins[k], outs[k], me)
                for ps, pd in _pieces(s, d, pieces):
                    pltpu.make_async_copy(ps, pd, loc_sems.at[k]).start()
                local.append(pltpu.make_async_copy(s, d, loc_sems.at[k]))

        def remote(k, p, src, dst, to):
            return pltpu.make_async_remote_copy(
                src_ref=src, dst_ref=dst, send_sem=send_sems.at[k * npeer + p], recv_sem=recv_sems.at[k * npeer + p],
                device_id=to, device_id_type=MESH)

        for k in range(na):
            for p in range(npeer):
                peer = _flip(masks[p], me)
                for ps, pd in _pieces(src_fn(k, ins[k], me, peer), dst_fn(k, outs[k], me), pieces):
                    remote(k, p, ps, pd, peer).start()
        for k in range(na):
            for p in range(npeer):
                peer = _flip(masks[p], me)
                remote(k, p, src_fn(k, ins[k], me, peer), dst_fn(k, outs[k], peer), peer).wait_recv()
        for k in range(na):
            for p in range(npeer):
                peer = _flip(masks[p], me)
                remote(k, p, src_fn(k, ins[k], me, peer), dst_fn(k, outs[k], me), peer).wait_send()
        for cp in local:
            cp.wait()

    hbm = pl.BlockSpec(memory_space=pl.ANY)
    out = pl.pallas_call(
        body, name=name, in_specs=[hbm] * na, out_specs=[hbm] * na,
        out_shape=[jax.ShapeDtypeStruct(s, a.dtype) for s, a in zip(out_shapes, arrays)],
        scratch_shapes=[pltpu.SemaphoreType.DMA((na * npeer,)), pltpu.SemaphoreType.DMA((na * npeer,)),
                        pltpu.SemaphoreType.DMA((na,))],
    )(*arrays)
    return list(out)


def _dev(pos):
    return 4 * pos[0] + 2 * pos[1] + pos[2]


def _chip(pos):
    return 2 * pos[0] + pos[1]


def allgather8(name, a):
    (out,) = comm_call(name, [a], [(8,) + a.shape], ALL_PEERS,
                       src_fn=lambda k, r, me, peer: r, dst_fn=lambda k, o, sender: o.at[_dev(sender)],
                       local_fn=lambda k, r, o, me: (r, o.at[_dev(me)]))
    return out


def gather_layer_from_chips(name, arrays):
    return comm_call(name, arrays, [(4,) + a.shape[1:] for a in arrays], CHIP_PEERS,
                     src_fn=lambda k, r, me, peer: r.at[me[2]], dst_fn=lambda k, o, sender: o.at[_chip(sender)],
                     local_fn=lambda k, r, o, me: (r.at[me[2]], o.at[_chip(me)]), pieces=8)


def swap_layers(name, arrays, c):
    got = comm_call(name, arrays, [a.shape for a in arrays], SIBLING,
                    src_fn=lambda k, r, me, peer: r, dst_fn=lambda k, o, sender: o, pieces=32)
    return [[jnp.where(c == 0, a, g), jnp.where(c == 0, g, a)] for a, g in zip(arrays, got)]


def swap_other_layer(name, arrays):
    return comm_call(name, arrays, [a.shape[1:] for a in arrays], SIBLING,
                     src_fn=lambda k, r, me, peer: r.at[peer[2]], dst_fn=lambda k, o, sender: o, pieces=32)


def scatter_to_chips(name, arrays):
    return comm_call(name, arrays, [a.shape for a in arrays], CHIP_PEERS,
                     src_fn=lambda k, r, me, peer: r.at[_chip(peer)], dst_fn=lambda k, o, sender: o.at[_chip(sender)],
                     local_fn=lambda k, r, o, me: (r.at[_chip(me)], o.at[_chip(me)]), pieces=8)


def _pack(arrs):
    flat = jnp.concatenate([a.reshape(-1).astype(F32) for a in arrs])
    n = flat.shape[0]
    pad = (-n) % (_ROW_T * 128)
    return jnp.pad(flat, (0, pad)).reshape(-1, 128)


def _unpack(buf, shapes):
    flat = buf.reshape(-1)
    out, o = [], 0
    for s in shapes:
        n = int(np.prod(s))
        out.append(flat[o:o + n].reshape(s))
        o += n
    return out


_WEIGHTS = ["ada_w", "ada_b", "norm1_g", "w_in", "ssd_conv_w", "ssd_conv_b", "ssd_dt_bias", "ssd_a_log", "ssd_d", "ssd_norm_g",
            "pool_w", "pool_scale", "w_out", "norm2_g", "ffn_up", "ffn_conv_w", "ffn_conv_b", "ffn_down", "final_g"]
_BIG = ["w_in", "w_out", "ffn_up", "ffn_down"]
_SMALL = [n for n in _WEIGHTS if n not in _BIG and n != "ada_w"]
_COL_SHARDED_SMALL = {"ssd_conv_w": 256, "ffn_conv_w": 1408}


def _pad_lanes(v, n=128):
    return jnp.pad(v.astype(F32), (0, n - v.shape[0]))[None]


def _perm_cols(w):
    pad = jnp.zeros(w.shape[:-1] + (IN_WP - IN_W,), w.dtype)
    return jnp.concatenate([w[..., :1536], w[..., 1544:1800], w[..., 1536:1544], pad, w[..., 1800:]], axis=-1)


def _unperm_cols(g):
    return jnp.concatenate([g[..., :1536], g[..., 1792:1800], g[..., 1536:1792], g[..., IN_MAIN:]], axis=-1)


def _cols_by_chip(g, ncol, order=None):
    g = g.reshape(g.shape[0], 4, ncol)
    if order is not None:
        g = jnp.stack([g[:, j] for j in order], axis=1)
    return g.transpose(1, 0, 2)


def _ffn_block_perm(a):
    n = a.shape[-1] // 4
    return jnp.concatenate([a[..., j * n:(j + 1) * n] for j in FFN_BLOCK_ORDER], axis=-1)


def _layer_forward(i, x3, modv, wts, sp, cs3, sn3):
    sh1, sc1, g1, sh2, sc2, g2 = modv
    h1 = norm_mod_forward(f"l{i}_norm1", x3, wts["norm1_g"], sc1, sh1)
    proj3 = mm(f"l{i}_proj", h1[0], wts["w_in"][:, :IN_MAIN], "nn")[None]
    qkv3 = mm(f"l{i}_qkv", h1[0], wts["w_in"][:, IN_MAIN:], "nn")[None]
    y_ssd, sv_ssd = ssd_forward(f"l{i}_ssd", proj3, sp)
    y_pool, sv_pool = pool_forward(f"l{i}_pool", proj3, wts["wbd"], wts["pool_scale"])
    y_att, res_att = attention_forward(f"l{i}", qkv3, cs3, sn3)
    mix = jnp.concatenate([y_ssd, y_pool, y_att], axis=-1)
    out, x1 = mm(f"l{i}_wout", mix[0], wts["w_out"], "nn", res=x3[0], gate=g1)
    x1 = x1[None]
    h2 = norm_mod_forward(f"l{i}_norm2", x1, wts["norm2_g"], sc2, sh2)
    up3 = mm(f"l{i}_up", h2[0], wts["ffn_up"], "nn")[None]
    act, sv_ffn = ffn_mid_forward(f"l{i}_ffn", up3, wts["ffn_conv_w"], wts["ffn_conv_b"])
    dn, x2 = mm(f"l{i}_down", act[0], wts["ffn_down"], "nn", res=x1[0], gate=g2)
    keep = dict(x=x3, h1=h1, proj3=proj3, qkv3=qkv3, sv_ssd=sv_ssd, sv_pool=sv_pool, res_att=res_att, mix=mix, out=out[None],
                x1=x1, h2=h2, up3=up3, act=act, sv_ffn=sv_ffn, dn=dn[None])
    return x2[None], keep


def _layer_backward(i, dx2, keep, modv, wts, sp, cs3, sn3):
    sh1, sc1, g1, sh2, sc2, g2 = modv
    k = keep
    d_dn, d_g2 = gate_backward(f"l{i}_gate2_b", k["dn"], g2, dx2)
    d_act = mm(f"l{i}_down_bx", d_dn[0], wts["ffn_down"], "nt")
    g_down = mm(f"l{i}_down_bw", k["act"][0], d_dn[0], "tn")
    (d_up,), dv_ffn = ffn_mid_backward(f"l{i}_ffn_b", k["up3"], wts["ffn_conv_w"], wts["ffn_conv_b"], k["sv_ffn"], d_act[None])
    d_h2 = mm(f"l{i}_up_bx", d_up[0], wts["ffn_up"], "nt")
    g_up = mm(f"l{i}_up_bw", k["h2"][0], d_up[0], "tn")
    dx1, (d_n2, d_sc2, d_sh2) = norm_mod_backward(f"l{i}_norm2_b", k["x1"], wts["norm2_g"], sc2, sh2, d_h2[None], dx2)
    d_out, d_g1 = gate_backward(f"l{i}_gate1_b", k["out"], g1, dx1)
    d_mix = mm(f"l{i}_wout_bx", d_out[0], wts["w_out"], "nt")[None]
    g_wout = mm(f"l{i}_wout_bw", k["mix"][0], d_out[0], "tn")
    (dz, dxs, dbm, dcm, ddt), dv_ssd = ssd_backward(f"l{i}_ssd_b", k["proj3"], sp, k["sv_ssd"], d_mix)
    (du_pool,), (d_wbd, d_pscale) = pool_backward(f"l{i}_pool_b", k["proj3"], wts["wbd"], wts["pool_scale"], k["sv_pool"], d_mix)
    d_qkv = attention_backward(f"l{i}", k["qkv3"], cs3, sn3, k["res_att"], d_mix)
    d_proj = jnp.concatenate([dz[0], dxs[0], dbm[0], dcm[0], du_pool[0], (ddt[0] + ddt[1]).astype(BF16), d_qkv[0]], axis=-1)
    d_h1 = mm(f"l{i}_proj_bx", d_proj, wts["w_in"], "nt")
    g_win = mm(f"l{i}_proj_bw", k["h1"][0], d_proj, "tn")
    dx, (d_n1, d_sc1, d_sh1) = norm_mod_backward(f"l{i}_norm1_b", k["x"], wts["norm1_g"], sc1, sh1, d_h1[None], dx1)
    dcwx, dcbx, dcwb, dcbb, dcwc, dcbc, ddtb, dalog, ddsk, dng = dv_ssd
    small = dict(
        norm1_g=d_n1[0], norm2_g=d_n2[0],
        ssd_conv_w=jnp.concatenate([dcwx[:, :512], dcwb[:, 512:768], dcwc[:, 768:]], axis=1),
        ssd_conv_b=jnp.concatenate([dcbx[0, :512], dcbb[0, 512:768], dcbc[0, 768:]]),
        ssd_dt_bias=ddtb[0, :8], ssd_a_log=dalog[0, :8], ssd_d=ddsk[0, :8], ssd_norm_g=dng[0],
        pool_w=jnp.stack([d_wbd[64 * g:64 * g + 64, 64 * g:64 * g + 64] for g in range(4)]), pool_scale=d_pscale[0],
        ffn_conv_w=_ffn_block_perm(dv_ffn[0]), ffn_conv_b=_ffn_block_perm(dv_ffn[1][0]),
    )
    dmod = jnp.concatenate([d_sh1[0], d_sc1[0], d_g1[0], d_sh2[0], d_sc2[0], d_g2[0]])
    return dx, dict(w_in=g_win, w_out=g_wout, ffn_up=g_up, ffn_down=g_down), small, dmod


def kernel(x, c, positions, ada_w, ada_b, norm1_g, w_in, ssd_conv_w, ssd_conv_b, ssd_dt_bias, ssd_a_log, ssd_d, ssd_norm_g, pool_w, pool_scale, w_out, norm2_g, ffn_up, ffn_conv_w, ffn_conv_b, ffn_down, final_g, loss_target, m_ada_w, m_ada_b, m_norm1_g, m_w_in, m_ssd_conv_w, m_ssd_conv_b, m_ssd_dt_bias, m_ssd_a_log, m_ssd_d, m_ssd_norm_g, m_pool_w, m_pool_scale, m_w_out, m_norm2_g, m_ffn_up, m_ffn_conv_w, m_ffn_conv_b, m_ffn_down, m_final_g, v_ada_w, v_ada_b, v_norm1_g, v_w_in, v_ssd_conv_w, v_ssd_conv_b, v_ssd_dt_bias, v_ssd_a_log, v_ssd_d, v_ssd_norm_g, v_pool_w, v_pool_scale, v_w_out, v_norm2_g, v_ffn_up, v_ffn_conv_w, v_ffn_conv_b, v_ffn_down, v_final_g):
    args = dict(locals())
    w = {n: args[n] for n in _WEIGHTS}
    m = {n: args["m_" + n] for n in _WEIGHTS}
    v = {n: args["v_" + n] for n in _WEIGHTS}
    d = D_MODEL
    me = (lax.axis_index("x"), lax.axis_index("y"), lax.axis_index("c"))
    chip, dev = _chip(me), _dev(me)

    shapes0 = [c.shape, ssd_conv_w.shape, ffn_conv_w.shape]
    g0 = allgather8("gather_c_conv", _pack([c, ssd_conv_w, ffn_conv_w]))
    c16 = jnp.pad(g0[:, :d // 128, :].reshape(8, d), ((0, 8), (0, 0)))
    by_chip = [_unpack(g0[2 * j], shapes0) for j in range(4)]
    conv_w_full = jnp.concatenate([p[1] for p in by_chip], axis=-1)
    fconv_w_full = jnp.concatenate([p[2] for p in by_chip], axis=-1)

    modp = ada_forward("ada_fwd", c16, ada_w)[:, :8]
    g1 = allgather8("gather_mod", _pack([modp]))
    modfull = jnp.concatenate([_unpack(g1[2 * j], [modp.shape])[0] for j in range(4)], axis=-1)
    mod = lax.dynamic_index_in_dim(modfull, dev, axis=1, keepdims=False) + ada_b
    modv = [[mod[i, q * d:(q + 1) * d][None] for q in range(6)] for i in range(DEPTH)]

    got = gather_layer_from_chips("gather_w", [w[n].astype(BF16) for n in _BIG])
    both = swap_layers("swap_w", got, me[2])
    full = dict(
        w_in=[_perm_cols(jnp.concatenate([a[j] for j in range(4)], axis=1)) for a in both[0]],
        w_out=[a.reshape(d, d) for a in both[1]],
        ffn_up=[jnp.concatenate([a[j] for j in FFN_BLOCK_ORDER], axis=1) for a in both[2]],
        ffn_down=[a.reshape(FFN_DIM, d) for a in both[3]],
    )

    cs3, sn3 = rope_tables(positions[0])
    eye4 = jnp.eye(4, dtype=F32)
    wts, sps = [], []
    for i in range(DEPTH):
        wts.append(dict(
            w_in=full["w_in"][i], w_out=full["w_out"][i], ffn_up=full["ffn_up"][i], ffn_down=full["ffn_down"][i],
            norm1_g=norm1_g[i][None], norm2_g=norm2_g[i][None], pool_scale=pool_scale[i][None],
            wbd=(eye4[:, None, :, None] * pool_w[i][:, :, None, :]).reshape(POOL_W, POOL_W),
            ffn_conv_w=_ffn_block_perm(fconv_w_full[i]), ffn_conv_b=_ffn_block_perm(ffn_conv_b[i])[None]))
        sps.append(dict(cw=conv_w_full[i], cb=ssd_conv_b[i][None], dtb=_pad_lanes(ssd_dt_bias[i]), alog=_pad_lanes(ssd_a_log[i]),
                        dsk=_pad_lanes(ssd_d[i]), ng=ssd_norm_g[i][None]))

    xc, keeps = x, []
    for i in range(DEPTH):
        xc, keep = _layer_forward(i, xc, modv[i], wts[i], sps[i], cs3, sn3)
        keeps.append(keep)
    lossblk, dx, d_final = final_loss("final_loss", xc, loss_target, final_g[None])
    loss = lax.psum(lossblk[0, 0], ("x", "y", "c"))

    big_g, small_g, dmods = [None] * DEPTH, [None] * DEPTH, [None] * DEPTH
    for i in reversed(range(DEPTH)):
        dx, big_g[i], small_g[i], dmods[i] = _layer_backward(i, dx, keeps[i], modv[i], wts[i], sps[i], cs3, sn3)

    by_dest = [
        jnp.stack([_cols_by_chip(_unperm_cols(big_g[i]["w_in"]), IN_W // 4) for i in range(DEPTH)]),
        jnp.stack([big_g[i]["w_out"].reshape(4, d // 4, d) for i in range(DEPTH)]),
        jnp.stack([_cols_by_chip(big_g[i]["ffn_up"], 2 * FFN_DIM // 4, FFN_BLOCK_ORDER) for i in range(DEPTH)]),
        jnp.stack([big_g[i]["ffn_down"].reshape(4, FFN_DIM // 4, d) for i in range(DEPTH)]),
    ]
    theirs = swap_other_layer("swap_g", by_dest)
    core_sum = [sum_cores(f"sum_cores_{n}", g, t, BF16) for n, g, t in zip(_BIG, by_dest, theirs)]
    from_chips = scatter_to_chips("scatter_g", core_sum)
    chip_sum = [sum_slots(f"sum_chips_{n}", q, 4)[0] for n, q in zip(_BIG, from_chips)]
    reduced = swap_layers("swap_r", chip_sum, me[2])
    grads = {n: jnp.stack(r) for n, r in zip(_BIG, reduced)}

    part = dict(ada_b=jnp.stack(dmods), final_g=d_final[0])
    for n in _SMALL:
        if n not in part:
            part[n] = jnp.stack([small_g[i][n] for i in range(DEPTH)])
    full_shapes = [part[n].shape for n in _SMALL]
    gs = allgather8("gather_small", _pack([part[n] for n in _SMALL]))
    tot = _unpack(sum_slots("sum_small", gs, 8)[0], full_shapes)
    small_tot = dict(zip(_SMALL, tot))
    dmod_all = gs[:, :DEPTH * 6 * d // 128, :].reshape(8, DEPTH, 6 * d)
    for n, ncol in _COL_SHARDED_SMALL.items():
        small_tot[n] = lax.dynamic_slice_in_dim(small_tot[n], chip * ncol, ncol, axis=2)
    grads.update(small_tot)

    ncol = ada_w.shape[2]
    dm = lax.dynamic_slice_in_dim(dmod_all, chip * ncol, ncol, axis=2).transpose(1, 0, 2)
    upd = {}
    g_ada, *upd["ada_w"] = ada_backward("ada_bwd", c16, jnp.pad(dm, ((0, 0), (0, 8), (0, 0))), ada_w, m["ada_w"], v["ada_w"])
    grads["ada_w"] = g_ada

    for n in _BIG:
        upd[n] = adamw(f"adam_{n}", w[n], grads[n], m[n], v[n])
    shapes_s = [w[n].shape for n in _SMALL]
    packed = [_pack([src[n] for n in _SMALL]) for src in (w, grads, m, v)]
    outs_s = [_unpack(o, shapes_s) for o in adamw("adam_small", *packed)]
    for q, n in enumerate(_SMALL):
        upd[n] = [outs_s[0][q], outs_s[1][q], outs_s[2][q]]

    return (loss, dx, *[grads[n] for n in _WEIGHTS], *[upd[n][0] for n in _WEIGHTS], *[upd[n][1] for n in _WEIGHTS],
            *[upd[n][2] for n in _WEIGHTS])
```

```python
import functools
import math

import numpy as np
import jax
import jax.numpy as jnp
from jax import lax
from jax.experimental import pallas as pl
from jax.experimental.pallas import tpu as pltpu

F32 = jnp.float32
BF16 = jnp.bfloat16
HI = lax.Precision.HIGHEST
MESH = pl.DeviceIdType.MESH

D_MODEL = 1024
SEQ = 4096
DEPTH = 2
SSD_INNER = 512
SSD_HEADS = 8
SSD_STATE = 128
POOL_W = 256
POOL_WINDOWS = (2, 4, 8, 16)
ATT_W = 256
ATT_HEADS = 4
ATT_HEAD_DIM = 64
ATT_PATTERNS = ((128, 1), (512, 4), (2048, 16))
ATT_BLOCK = 128
ROT_DIM = 16
ROPE_THETA = 500000.0
IN_W = 2568
IN_WP = 2688
IN_MAIN = 1920
FFN_DIM = 2816
NORM_EPS = 1e-6
ADAM_LR, ADAM_B1, ADAM_B2, ADAM_EPS, ADAM_WD, ADAM_STEP = 0.001, 0.9, 0.999, 1e-08, 0.01, 10

VMEM_LIMIT_BYTES = 56 * 1024 * 1024
NEG = -1e30


def _mxu(a, b, mode):
    dims = {"nn": ((1,), (0,)), "nt": ((1,), (1,)), "tn": ((0,), (0,))}[mode]
    return lax.dot_general(a.astype(BF16), b.astype(BF16), (dims, ((), ())), preferred_element_type=F32)


@functools.partial(jax.custom_vjp, nondiff_argnums=(2,))
def _bdot(a, b, mode):
    return _mxu(a, b, mode)


def _bdot_fwd(a, b, mode):
    return _mxu(a, b, mode), (a, b)


def _bdot_bwd(mode, res, g):
    a, b = res
    if mode == "nn":
        return _mxu(g, b, "nt"), _mxu(a, g, "tn")
    if mode == "nt":
        return _mxu(g, b, "nn"), _mxu(g, a, "tn")
    return _mxu(b, g, "nt"), _mxu(a, g, "nn")


_bdot.defvjp(_bdot_fwd, _bdot_bwd)


def _fxu(a, b, mode):
    dims = {"nn": ((1,), (0,)), "nt": ((1,), (1,)), "tn": ((0,), (0,))}[mode]
    return lax.dot_general(a, b, (dims, ((), ())), precision=HI, preferred_element_type=F32)


@functools.partial(jax.custom_vjp, nondiff_argnums=(2,))
def _fdot(a, b, mode):
    return _fxu(a, b, mode)


def _fdot_fwd(a, b, mode):
    return _fxu(a, b, mode), (a, b)


def _fdot_bwd(mode, res, g):
    a, b = res
    if mode == "nn":
        return _fxu(g, b, "nt"), _fxu(a, g, "tn")
    if mode == "nt":
        return _fxu(g, b, "nn"), _fxu(g, a, "tn")
    return _fxu(b, g, "nt"), _fxu(a, g, "nn")


_fdot.defvjp(_fdot_fwd, _fdot_bwd)


def _iota(shape, dim):
    return lax.broadcasted_iota(jnp.int32, shape, dim)


def _make_shift(h):
    @functools.partial(jax.custom_vjp, nondiff_argnums=(2,))
    def shift(halo, cur, k):
        if k == 0:
            return cur
        full = jnp.concatenate([halo, cur], axis=0)
        return pltpu.roll(full, k, 0)[h:]

    def fwd(halo, cur, k):
        return shift(halo, cur, k), None

    def bwd(k, _, g):
        t, w = g.shape
        if k == 0:
            return jnp.zeros((h, w), F32), g
        d_cur = jnp.where(_iota((t, w), 0) < t - k, pltpu.roll(g, t - k, 0), 0.0)
        top = g[:h]
        d_halo = jnp.where(_iota((h, w), 0) >= h - k, pltpu.roll(top, h - k, 0) if k < h else top, 0.0)
        return d_halo, d_cur

    shift.defvjp(fwd, bwd)
    return shift


_shift8 = _make_shift(8)
_shift16 = _make_shift(16)


def _make_tail(h):
    @jax.custom_vjp
    def tail(x):
        return x[x.shape[0] - h:]

    def fwd(x):
        return tail(x), x.shape[0]

    def bwd(t, g):
        return (jnp.concatenate([jnp.zeros((t - h, g.shape[1]), F32), g], axis=0),)

    tail.defvjp(fwd, bwd)
    return tail


_tail8 = _make_tail(8)
_tail16 = _make_tail(16)


@jax.custom_vjp
def _cumsum_rows(x):
    t = x.shape[0]
    row, s = _iota(x.shape, 0), 1
    while s < t:
        x = x + jnp.where(row >= s, pltpu.roll(x, s, 0), 0.0)
        s *= 2
    return x


def _cumsum_rows_fwd(x):
    return _cumsum_rows(x), None


def _cumsum_rows_bwd(_, g):
    t = g.shape[0]
    row, s = _iota(g.shape, 0), 1
    while s < t:
        g = g + jnp.where(row < t - s, pltpu.roll(g, t - s, 0), 0.0)
        s *= 2
    return (g,)


_cumsum_rows.defvjp(_cumsum_rows_fwd, _cumsum_rows_bwd)


@jax.custom_vjp
def _rot_pairs(t):
    e = _iota(t.shape, 1) % ATT_HEAD_DIM
    n = t.shape[1]
    return jnp.where(e < 8, -pltpu.roll(t, n - 8, 1), jnp.where(e < 16, pltpu.roll(t, 8, 1), 0.0))


def _rot_pairs_fwd(t):
    return _rot_pairs(t), None


def _rot_pairs_bwd(_, g):
    e = _iota(g.shape, 1) % ATT_HEAD_DIM
    n = g.shape[1]
    return (pltpu.roll(jnp.where(e < 8, -g, 0.0), 8, 1) + pltpu.roll(jnp.where(jnp.logical_and(e >= 8, e < 16), g, 0.0), n - 8, 1),)


_rot_pairs.defvjp(_rot_pairs_fwd, _rot_pairs_bwd)


def _make_thirds():
    @jax.custom_vjp
    def thirds(x):
        w = x.shape[1] // 3
        return x[:, :w], x[:, w:2 * w], x[:, 2 * w:]

    def fwd(x):
        return thirds(x), None

    def bwd(_, g):
        return (jnp.concatenate(g, axis=1),)

    thirds.defvjp(fwd, bwd)
    return thirds


_thirds = _make_thirds()


def _rowk(w, k):
    return jnp.sum(jnp.where(_iota(w.shape, 0) == k, w, 0.0), axis=0, keepdims=True)


def _silu(x):
    return x * (0.5 * jnp.tanh(0.5 * x) + 0.5)


def _softplus(x):
    return jnp.maximum(x, 0.0) + jnp.log(1.0 + jnp.exp(-jnp.abs(x)))


def _tile(dim, target, unit=128):
    if dim <= target:
        return dim
    best = None
    for t in range(unit, target + 1, unit):
        if dim % t == 0:
            best = t
    assert best is not None, (dim, target)
    return best


class Row:
    def __init__(self, arr, w=None, fb=None, fc=None, diff=True, slot=False, dcols=None, dfc=None, ddtype=F32, view=None):
        self.ddtype = ddtype
        self.view = view
        self.arr = arr
        self.w = arr.shape[2] if w is None else w
        self.fb = (lambda b: 0) if fb is None else fb
        self.fc = (lambda b: 0) if fc is None else fc
        self.diff = diff
        self.slot = slot
        self.dcols = dcols
        self.dfc = dfc


class Vec:
    def __init__(self, arr, w=None, fc=None, diff=True):
        self.arr = arr
        self.w = arr.shape[1] if w is None else w
        self.fc = fc
        self.diff = diff


def _row_spec(r, t, nchunk, reverse):
    shape = (1, t, r.w) if r.view is None else (1, t // r.view, r.view * r.w)
    if reverse:
        return pl.BlockSpec(shape, lambda b, i, r=r: (r.fb(b), nchunk - 1 - i, r.fc(b)))
    return pl.BlockSpec(shape, lambda b, i, r=r: (r.fb(b), i, r.fc(b)))


def _load_row(ref, r, t, scr):
    if r.view is None:
        return ref[0]
    d, w = r.view, r.w
    for q in range(d):
        for j in range(w // 128):
            scr[j, pl.ds(q, t // d, stride=d), :] = ref[0, :, q * w + 128 * j:q * w + 128 * (j + 1)].astype(F32)
    return jnp.concatenate([scr[j] for j in range(w // 128)], axis=1)


def _store_row(ref, r, t, scr, val):
    if r.view is None:
        ref[0] = val.astype(ref.dtype)
        return
    d, w = r.view, r.w
    for j in range(w // 128):
        scr[j] = val[:, 128 * j:128 * (j + 1)]
    for q in range(d):
        for j in range(w // 128):
            ref[0, :, q * w + 128 * j:q * w + 128 * (j + 1)] = scr[j, pl.ds(q, t // d, stride=d), :].astype(ref.dtype)


def _view_scratch(specs, t):
    ws = [r.w for r in specs if r.view is not None]
    return [pltpu.VMEM((max(ws) // 128, t, 128), F32)] if ws else []


def _vec_spec(v):
    if v.fc is None:
        return pl.BlockSpec(v.arr.shape, lambda b, i: (0, 0))
    return pl.BlockSpec((v.arr.shape[0], v.w), lambda b, i, v=v: (0, v.fc(b)))


def _cparams():
    return pltpu.CompilerParams(dimension_semantics=("arbitrary", "arbitrary"), vmem_limit_bytes=VMEM_LIMIT_BYTES)


def scan_fwd(name, fn, *, nb, nchunk, t, rows, vecs, carries, outs, save):
    nr, nv, nc, no = len(rows), len(vecs), len(carries), len(outs)

    def body(*refs):
        row_refs, vec_refs = refs[:nr], refs[nr:nr + nv]
        out_refs = refs[nr + nv:nr + nv + no]
        save_refs = refs[nr + nv + no:nr + nv + no + (nc if save else 0)]
        scr = refs[len(refs) - 1] if stage else None
        car = refs[len(refs) - nc - len(stage):len(refs) - len(stage)] if nc else ()
        b, i = pl.program_id(0), pl.program_id(1)
        if nc:
            @pl.when(i == 0)
            def _():
                for c_ref in car:
                    c_ref[...] = jnp.zeros(c_ref.shape, F32)
        cin = [c_ref[...] for c_ref in car]
        if save:
            for s_ref, cv in zip(save_refs, cin):
                s_ref[0, 0] = cv
        new_c, o = fn(i, b, cin, [_load_row(ref, r, t, scr) for ref, r in zip(row_refs, rows)], [v[...] for v in vec_refs])
        for c_ref, cv in zip(car, new_c):
            c_ref[...] = cv
        for o_ref, spec, ov in zip(out_refs, outs, o):
            _store_row(o_ref, spec, t, scr, ov)

    stage = _view_scratch(list(rows) + list(outs), t)
    out_shape = [o.arr for o in outs]
    out_specs = [_row_spec(o, t, nchunk, False) for o in outs]
    if save:
        for cs in carries:
            out_shape.append(jax.ShapeDtypeStruct((nb, nchunk) + tuple(cs), F32))
            out_specs.append(pl.BlockSpec((1, 1) + tuple(cs), lambda b, i: (b, i, 0, 0)))
    res = pl.pallas_call(
        body, name=name, grid=(nb, nchunk),
        in_specs=[_row_spec(r, t, nchunk, False) for r in rows] + [_vec_spec(v) for v in vecs],
        out_specs=out_specs, out_shape=out_shape,
        scratch_shapes=[pltpu.VMEM(tuple(cs), F32) for cs in carries] + stage,
        compiler_params=_cparams(),
    )(*[r.arr for r in rows], *[v.arr for v in vecs])
    return list(res[:no]), list(res[no:])


def scan_bwd(name, fn, *, nb, nchunk, t, rows, vecs, carries, saved, douts, adds=None):
    adds = adds or {}
    nr, nv, nc, no = len(rows), len(vecs), len(carries), len(douts)
    dri = [k for k, r in enumerate(rows) if r.diff]
    dvi = [k for k, v in enumerate(vecs) if v.diff]
    add_keys = sorted(adds)
    na = len(add_keys)

    def body(*refs):
        p = 0
        row_refs = refs[p:p + nr]; p += nr
        vec_refs = refs[p:p + nv]; p += nv
        save_refs = refs[p:p + nc]; p += nc
        dout_refs = refs[p:p + no]; p += no
        add_refs = refs[p:p + na]; p += na
        drow_refs = refs[p:p + len(dri)]; p += len(dri)
        dvec_refs = refs[p:p + len(dvi)]; p += len(dvi)
        dcar = refs[p:p + nc]
        scr = refs[len(refs) - 1] if stage else None
        b, ir = pl.program_id(0), pl.program_id(1)
        ci = nchunk - 1 - ir
        if nc:
            @pl.when(ir == 0)
            def _():
                for c_ref in dcar:
                    c_ref[...] = jnp.zeros(c_ref.shape, F32)
        rows_v = [_load_row(ref, r, t, scr) for ref, r in zip(row_refs, rows)]
        vecs_v = [v[...] for v in vec_refs]
        cin = [s[0, 0] for s in save_refs]
        dc = [c_ref[...] for c_ref in dcar]
        dout_v = [_load_row(ref, r, t, scr).astype(F32) for ref, r in zip(dout_refs, douts)]

        def f(cs, dr, dv):
            rr, vv = list(rows_v), list(vecs_v)
            for k, idx in enumerate(dri):
                rr[idx] = dr[k]
            for k, idx in enumerate(dvi):
                vv[idx] = dv[k]
            return fn(ci, b, cs, rr, vv)

        _, vjp = jax.vjp(f, cin, [rows_v[k].astype(F32) for k in dri], [vecs_v[k].astype(F32) for k in dvi])
        dcin, drows, dvecs = vjp((dc, dout_v))
        for c_ref, cv in zip(dcar, dcin):
            c_ref[...] = cv
        for k, (o_ref, ov) in enumerate(zip(drow_refs, drows)):
            if dri[k] in adds:
                ov = ov + add_refs[add_keys.index(dri[k])][0].astype(F32)
            _store_row(o_ref, rows[dri[k]], t, scr, ov)
        for k, (o_ref, ov) in enumerate(zip(dvec_refs, dvecs)):
            first = (ir == 0) if vecs[dvi[k]].fc is not None else jnp.logical_and(ir == 0, b == 0)

            @pl.when(first)
            def _(o_ref=o_ref, ov=ov):
                o_ref[...] = ov

            @pl.when(jnp.logical_not(first))
            def _(o_ref=o_ref, ov=ov):
                o_ref[...] += ov

    stage = _view_scratch(list(rows) + list(douts), t)
    in_specs = ([_row_spec(r, t, nchunk, True) for r in rows] + [_vec_spec(v) for v in vecs]
                + [pl.BlockSpec((1, 1) + tuple(cs), lambda b, i: (b, nchunk - 1 - i, 0, 0)) for cs in carries]
                + [_row_spec(d, t, nchunk, True) for d in douts]
                + [_row_spec(adds[k], t, nchunk, True) for k in add_keys])
    out_shape, out_specs = [], []
    for k in dri:
        r = rows[k]
        if r.slot:
            out_shape.append(jax.ShapeDtypeStruct((nb, r.arr.shape[1], r.w), r.ddtype))
            out_specs.append(pl.BlockSpec((1, t, r.w), lambda b, i: (b, nchunk - 1 - i, 0)))
        elif r.dcols is not None:
            out_shape.append(jax.ShapeDtypeStruct((r.arr.shape[0], r.arr.shape[1], r.dcols), r.ddtype))
            out_specs.append(pl.BlockSpec((1, t, r.w), lambda b, i, r=r: (r.fb(b), nchunk - 1 - i, r.dfc(b))))
        else:
            out_shape.append(jax.ShapeDtypeStruct(r.arr.shape, r.ddtype))
            out_specs.append(_row_spec(r, t, nchunk, True))
    for k in dvi:
        out_shape.append(jax.ShapeDtypeStruct(vecs[k].arr.shape, F32))
        out_specs.append(_vec_spec(vecs[k]))
    res = pl.pallas_call(
        body, name=name, grid=(nb, nchunk), in_specs=in_specs, out_specs=out_specs, out_shape=out_shape,
        scratch_shapes=[pltpu.VMEM(tuple(cs), F32) for cs in carries] + stage,
        compiler_params=_cparams(),
    )(*[r.arr for r in rows], *[v.arr for v in vecs], *saved, *[d.arr for d in douts], *[adds[k].arr for k in add_keys])
    return list(res[:len(dri)]), list(res[len(dri):])


def out_row(shape, dtype=F32, w=None, fb=None, fc=None):
    return Row(jax.ShapeDtypeStruct(shape, dtype), w, fb, fc)


def _conv(shift, halo, cur, w, bias, taps):
    y = bias
    for k in range(taps):
        y = y + _rowk(w, k) * shift(halo, cur, taps - 1 - k)
    return y


def _ssd_fn(ci, b, carries, rows, vecs):
    cx, cb_, cc, ht = carries
    z, xr, br, cr, dtr = rows
    cwx, cbx, cwb, cbb, cwc, cbc, dtb, alog, dsk, ng = vecs
    t = z.shape[0]
    xs = _silu(_conv(_shift8, cx, xr, cwx, cbx, 4))
    bm = _silu(_conv(_shift8, cb_, br, cwb, cbb, 4))
    cm = _silu(_conv(_shift8, cc, cr, cwc, cbc, 4))
    dt = _softplus(dtr + dtb)
    acol = _cumsum_rows(dt * (-jnp.exp(alog)))
    arow = acol.T
    r, c = _iota((t, t), 0), _iota((t, t), 1)
    causal = r >= c
    cbm = _bdot(cm, bm, "nt")
    lane, sub = _iota(acol.shape, 1), _iota(arow.shape, 0)
    colh = _iota(xs.shape, 1) // 64
    a, dtx, dx, acs = jnp.zeros(xs.shape, F32), jnp.zeros(xs.shape, F32), jnp.zeros((1, xs.shape[1]), F32), []
    for j in range(4):
        h = 4 * b + j
        ac = jnp.sum(jnp.where(lane == h, acol, 0.0), axis=1, keepdims=True)
        acs.append(ac)
        a = jnp.where(colh == j, ac, a)
        dtx = jnp.where(colh == j, jnp.sum(jnp.where(lane == h, dt, 0.0), axis=1, keepdims=True), dtx)
        dx = jnp.where(_iota(dx.shape, 1) // 64 == j, jnp.sum(jnp.where(_iota(dsk.shape, 1) == h, dsk, 0.0), axis=1, keepdims=True), dx)
    atot = jnp.sum(jnp.where(_iota(a.shape, 0) == t - 1, a, 0.0), axis=0, keepdims=True)
    x = xs * dtx
    ydiag = jnp.zeros(x.shape, F32)
    for j in range(4):
        ar = jnp.sum(jnp.where(sub == 4 * b + j, arow, 0.0), axis=0, keepdims=True)
        lmat = jnp.exp(jnp.where(causal, acs[j] - ar, NEG))
        ydiag = ydiag + _bdot(cbm * lmat, jnp.where(colh == j, x, 0.0), "nn")
    yoff = _bdot(cm, ht, "nn") * jnp.exp(a)
    ht_new = ht * jnp.exp(atot) + _bdot(bm, x * jnp.exp(atot - a), "tn")
    y = ydiag + yoff + dx * xs
    yz = y * _silu(z)
    yn = yz * lax.rsqrt(jnp.mean(yz * yz, axis=-1, keepdims=True) + NORM_EPS) * ng
    return [_tail8(xr), _tail8(br), _tail8(cr), ht_new], [yn]


_SSD_T = 256
_SSD_CARRIES = [(8, 256), (8, 128), (8, 128), (128, 256)]


def _ssd_io(proj3, p):
    own = lambda b: b
    rows = [Row(proj3, 256, fc=own, dcols=512, dfc=own, ddtype=BF16),
            Row(proj3, 256, fc=lambda b: 2 + b, dcols=512, dfc=own, ddtype=BF16),
            Row(proj3, 128, fc=lambda b: 8 + b, dcols=256, dfc=own, ddtype=BF16),
            Row(proj3, 128, fc=lambda b: 10 + b, dcols=256, dfc=own, ddtype=BF16),
            Row(proj3, 128, fc=lambda b: 14, slot=True)]
    vecs = [Vec(p["cw"], 256, lambda b: b), Vec(p["cb"], 256, lambda b: b),
            Vec(p["cw"], 128, lambda b: 4 + b), Vec(p["cb"], 128, lambda b: 4 + b),
            Vec(p["cw"], 128, lambda b: 6 + b), Vec(p["cb"], 128, lambda b: 6 + b),
            Vec(p["dtb"]), Vec(p["alog"]), Vec(p["dsk"]), Vec(p["ng"], 256, lambda b: b)]
    return rows, vecs


def ssd_forward(name, proj3, p):
    rows, vecs = _ssd_io(proj3, p)
    s = proj3.shape[1]
    (y,), saved = scan_fwd(name, _ssd_fn, nb=2, nchunk=s // _SSD_T, t=_SSD_T, rows=rows, vecs=vecs,
                           carries=_SSD_CARRIES, outs=[out_row((1, s, SSD_INNER), BF16, 256, fc=lambda b: b)], save=True)
    return y, saved


def ssd_backward(name, proj3, p, saved, dmix3):
    rows, vecs = _ssd_io(proj3, p)
    s = proj3.shape[1]
    drows, dvecs = scan_bwd(name, _ssd_fn, nb=2, nchunk=s // _SSD_T, t=_SSD_T, rows=rows, vecs=vecs,
                            carries=_SSD_CARRIES, saved=saved, douts=[Row(dmix3, 256, fc=lambda b: b)])
    return drows, dvecs


def _pool_fn(ci, b, carries, rows, vecs):
    (cu,) = carries
    (u,) = rows
    wbd, scale = vecs
    t = u.shape[0]
    pos = ci * t + _iota(u.shape, 0)
    grp = _iota(u.shape, 1) // 64
    acc, pooled, k = u, jnp.zeros(u.shape, F32), 1
    for gi, w in enumerate(POOL_WINDOWS):
        while k < w:
            acc = acc + _shift16(cu, u, k)
            k += 1
        pooled = jnp.where(grp == gi, acc / jnp.minimum(pos + 1, w).astype(F32), pooled)
    y = _bdot(pooled - u, wbd, "nn") * scale
    return [_tail16(u)], [y]


_POOL_T = 256


def _pool_io(proj3, wbd, scale):
    return [Row(proj3, 256, fc=lambda b: 6, dcols=256, dfc=lambda b: 0, ddtype=BF16)], [Vec(wbd), Vec(scale)]


def pool_forward(name, proj3, wbd, scale):
    rows, vecs = _pool_io(proj3, wbd, scale)
    s = proj3.shape[1]
    (y,), saved = scan_fwd(name, _pool_fn, nb=1, nchunk=s // _POOL_T, t=_POOL_T, rows=rows, vecs=vecs,
                           carries=[(16, 256)], outs=[out_row((1, s, POOL_W), BF16)], save=True)
    return y, saved


def pool_backward(name, proj3, wbd, scale, saved, dmix3):
    rows, vecs = _pool_io(proj3, wbd, scale)
    s = proj3.shape[1]
    return scan_bwd(name, _pool_fn, nb=1, nchunk=s // _POOL_T, t=_POOL_T, rows=rows, vecs=vecs,
                    carries=[(16, 256)], saved=saved, douts=[Row(dmix3, 256, fc=lambda b: 2)])


def _attn_fn(ci, b, carries, rows, vecs):
    kp, vp = carries
    qr, kr, v = _thirds(rows[0])
    scale = ATT_HEAD_DIM ** -0.5
    q = qr
    n = q.shape[0]
    r, c = _iota((n, n), 0), _iota((n, n), 1)
    prev_ok, cur_ok = jnp.logical_and(c >= r, ci > 0), r >= c
    head = _iota(q.shape, 1) // ATT_HEAD_DIM
    o, lse = jnp.zeros(q.shape, F32), jnp.zeros(q.shape, F32)
    for h in range(ATT_HEADS):
        mine = head == h
        qh = jnp.where(mine, qr, 0.0)
        sp = jnp.where(prev_ok, _bdot(qh, kp, "nt") * scale, NEG)
        sc = jnp.where(cur_ok, _bdot(qh, kr, "nt") * scale, NEG)
        m = lax.stop_gradient(jnp.maximum(jnp.max(sp, axis=1, keepdims=True), jnp.max(sc, axis=1, keepdims=True)))
        pp, pc = jnp.exp(sp - m), jnp.exp(sc - m)
        l = jnp.sum(pp, axis=1, keepdims=True) + jnp.sum(pc, axis=1, keepdims=True)
        o = jnp.where(mine, (_bdot(pp, vp, "nn") + _bdot(pc, v, "nn")) / l, o)
        lse = jnp.where(mine, m + jnp.log(l), lse)
    return [kr, v], [o, lse]


_ATT_CARRIES = [(ATT_BLOCK, ATT_W), (ATT_BLOCK, ATT_W)]


def attn_forward(name, pv, d):
    l = pv.shape[1]
    own = lambda b: b
    outs = [out_row((1, l, d * ATT_W), F32, ATT_W, fc=own) for _ in range(2)]
    (o, lse), saved = scan_fwd(name, _attn_fn, nb=d, nchunk=l // ATT_BLOCK, t=ATT_BLOCK, rows=[Row(pv, 3 * ATT_W, fc=own)],
                               vecs=[], carries=_ATT_CARRIES, outs=outs, save=True)
    return o, lse, saved


def attn_backward(name, pv, d, saved, do, dlse):
    l = pv.shape[1]
    own = lambda b: b
    (dpv,), _ = scan_bwd(name, _attn_fn, nb=d, nchunk=l // ATT_BLOCK, t=ATT_BLOCK, rows=[Row(pv, 3 * ATT_W, fc=own)], vecs=[],
                         carries=_ATT_CARRIES, saved=saved, douts=[Row(do, ATT_W, fc=own), Row(dlse, ATT_W, fc=own)])
    return dpv


def _rope_fn(ci, b, carries, rows, vecs):
    x, cs, sn = rows
    return [], [x * cs + _rot_pairs(x) * sn]


def _rope3_fn(ci, b, carries, rows, vecs):
    _, (y,) = _rope_fn(ci, b, carries, rows, vecs)
    return [], [y, y, y]


def _by_residue(a_or_shape, w, d):
    if isinstance(a_or_shape, tuple):
        _, s, _ = a_or_shape
        return Row(jax.ShapeDtypeStruct((1, s // d, d * w), F32), w, view=None if d == 1 else d)
    return Row(a_or_shape, w, view=None if d == 1 else d)


def rope_forward(name, qkv3, cs3, sn3):
    s, w = qkv3.shape[1], qkv3.shape[2]
    ys, _ = scan_fwd(name, _rope3_fn, nb=1, nchunk=s // _ROW_T, t=_ROW_T, vecs=[], carries=[], save=False,
                     rows=[Row(qkv3), Row(cs3, diff=False), Row(sn3, diff=False)],
                     outs=[_by_residue(qkv3.shape, w, d) for _, d in ATT_PATTERNS])
    return ys


def rope_backward(name, qkv3, cs3, sn3, dys):
    s, w = qkv3.shape[1], qkv3.shape[2]
    (dx,), _ = scan_bwd(name, _rope3_fn, nb=1, nchunk=s // _ROW_T, t=_ROW_T, vecs=[], carries=[], saved=[],
                        rows=[Row(qkv3, ddtype=BF16), Row(cs3, diff=False), Row(sn3, diff=False)],
                        douts=[_by_residue(a, w, d) for a, (_, d) in zip(dys, ATT_PATTERNS)])
    return dx


def _merge_fn(ci, b, carries, rows, vecs):
    o1, o2, o3, l1, l2, l3 = rows
    mx = lax.stop_gradient(jnp.maximum(l1, jnp.maximum(l2, l3)))
    e1, e2, e3 = jnp.exp(l1 - mx), jnp.exp(l2 - mx), jnp.exp(l3 - mx)
    return [], [(e1 * o1 + e2 * o2 + e3 * o3) / (e1 + e2 + e3)]


_ROW_T = 256


def _merge_rows(os_, ls_):
    ds = [d for _, d in ATT_PATTERNS]
    return [_by_residue(a, ATT_W, d) for a, d in zip(os_, ds)] + [_by_residue(a, ATT_W, d) for a, d in zip(ls_, ds)]


def merge_forward(name, os_, ls_, s):
    (y,), _ = scan_fwd(name, _merge_fn, nb=1, nchunk=s // _ROW_T, t=_ROW_T, rows=_merge_rows(os_, ls_), vecs=[],
                       carries=[], outs=[out_row((1, s, ATT_W), BF16)], save=False)
    return y


def merge_backward(name, os_, ls_, dmix3):
    s = dmix3.shape[1]
    drows, _ = scan_bwd(name, _merge_fn, nb=1, nchunk=s // _ROW_T, t=_ROW_T, rows=_merge_rows(os_, ls_), vecs=[],
                        carries=[], saved=[], douts=[Row(dmix3, 256, fc=lambda b: 3)])
    return drows


def _norm_mod_fn(ci, b, carries, rows, vecs):
    (x,) = rows
    g, sc, sh = vecs
    xn = x * lax.rsqrt(jnp.mean(x * x, axis=-1, keepdims=True) + NORM_EPS)
    return [], [xn * g * (1.0 + sc) + sh]


def norm_mod_forward(name, x3, g, sc, sh):
    s = x3.shape[1]
    (h,), _ = scan_fwd(name, _norm_mod_fn, nb=1, nchunk=s // _ROW_T, t=_ROW_T, rows=[Row(x3)], vecs=[Vec(g), Vec(sc), Vec(sh)],
                       carries=[], outs=[out_row(x3.shape, BF16)], save=False)
    return h


def norm_mod_backward(name, x3, g, sc, sh, dh3, add3):
    s = x3.shape[1]
    (dx,), dv = scan_bwd(name, _norm_mod_fn, nb=1, nchunk=s // _ROW_T, t=_ROW_T, rows=[Row(x3)], vecs=[Vec(g), Vec(sc), Vec(sh)],
                         carries=[], saved=[], douts=[Row(dh3)], adds={0: Row(add3)})
    return dx, dv


def _gate_fn(ci, b, carries, rows, vecs):
    return [], [rows[0] * vecs[0]]


def gate_backward(name, o3, g, dx3):
    s = o3.shape[1]
    (do,), (dg,) = scan_bwd(name, _gate_fn, nb=1, nchunk=s // _ROW_T, t=_ROW_T, rows=[Row(o3, ddtype=BF16)], vecs=[Vec(g)],
                            carries=[], saved=[], douts=[Row(dx3)])
    return do, dg


def _make_halves():
    @jax.custom_vjp
    def halves(x):
        h = x.shape[1] // 2
        return x[:, :h], x[:, h:]

    def fwd(x):
        return halves(x), None

    def bwd(_, g):
        return (jnp.concatenate(g, axis=1),)

    halves.defvjp(fwd, bwd)
    return halves


_halves = _make_halves()


def _ffn_fn(ci, b, carries, rows, vecs):
    (cu,) = carries
    (u,) = rows
    w, bias = vecs
    hg, hu = _halves(_conv(_shift8, cu, u, w, bias, 3))
    return [_tail8(u)], [_silu(hg) * hu]


_FFN_T = 256
_FFN_CW = FFN_DIM // 2
_FFN_CARRIES = [(8, 2 * _FFN_CW)]
FFN_BLOCK_ORDER = [0, 2, 1, 3]


def _ffn_io(up3, cw, cb):
    own = lambda b: b
    return [Row(up3, 2 * _FFN_CW, fc=own, ddtype=BF16)], [Vec(cw, 2 * _FFN_CW, own), Vec(cb, 2 * _FFN_CW, own)]


def ffn_mid_forward(name, up3, cw, cb):
    rows, vecs = _ffn_io(up3, cw, cb)
    s = up3.shape[1]
    (act,), saved = scan_fwd(name, _ffn_fn, nb=2, nchunk=s // _FFN_T, t=_FFN_T, rows=rows, vecs=vecs, carries=_FFN_CARRIES,
                             outs=[out_row((1, s, FFN_DIM), BF16, _FFN_CW, fc=lambda b: b)], save=True)
    return act, saved


def ffn_mid_backward(name, up3, cw, cb, saved, dact3):
    rows, vecs = _ffn_io(up3, cw, cb)
    s = up3.shape[1]
    return scan_bwd(name, _ffn_fn, nb=2, nchunk=s // _FFN_T, t=_FFN_T, rows=rows, vecs=vecs, carries=_FFN_CARRIES,
                    saved=saved, douts=[Row(dact3, _FFN_CW, fc=lambda b: b)])


def _adam_fn(ci, b, carries, rows, vecs):
    w, g, m, v = rows
    m = ADAM_B1 * m + (1.0 - ADAM_B1) * g
    v = ADAM_B2 * v + (1.0 - ADAM_B2) * (g * g)
    m_hat = m / (1.0 - ADAM_B1 ** ADAM_STEP)
    v_hat = v / (1.0 - ADAM_B2 ** ADAM_STEP)
    delta = -ADAM_LR * (m_hat / (jnp.sqrt(v_hat) + ADAM_EPS) + ADAM_WD * w)
    return [], [delta, m, v]


def adamw(name, w, g, m, v):
    shape = w.shape
    c = shape[-1]
    r = int(np.prod(shape[:-1]))
    t = _tile(r, 256, 8)
    as3 = lambda a: a.reshape(1, r, c)
    outs, _ = scan_fwd(name, _adam_fn, nb=1, nchunk=r // t, t=t, rows=[Row(as3(a)) for a in (w, g, m, v)], vecs=[], carries=[],
                       outs=[out_row((1, r, c)) for _ in range(3)], save=False)
    return [o.reshape(shape) for o in outs]


def rope_tables(positions):
    inv_freq = ROPE_THETA ** (-jnp.arange(0, ROT_DIM, 2, dtype=F32) / ROT_DIM)
    ang = positions.astype(F32)[:, None] * inv_freq
    s = positions.shape[0]
    cs = jnp.concatenate([jnp.cos(ang), jnp.cos(ang), jnp.ones((s, ATT_HEAD_DIM - ROT_DIM), F32)], axis=1)
    sn = jnp.concatenate([jnp.sin(ang), jnp.sin(ang), jnp.zeros((s, ATT_HEAD_DIM - ROT_DIM), F32)], axis=1)
    cs3 = jnp.concatenate([jnp.tile(cs, (1, 2 * ATT_HEADS)), jnp.ones((s, ATT_W), F32)], axis=1)
    sn3 = jnp.concatenate([jnp.tile(sn, (1, 2 * ATT_HEADS)), jnp.zeros((s, ATT_W), F32)], axis=1)
    return cs3[None], sn3[None]


def attention_forward(lname, qkv3, cs3, sn3):
    s = qkv3.shape[1]
    rotated = rope_forward(f"{lname}_rope", qkv3, cs3, sn3)
    os_, ls_, keep = [], [], []
    for pi, (_, d) in enumerate(ATT_PATTERNS):
        o, lse, saved = attn_forward(f"{lname}_attn{pi}", rotated[pi], d)
        os_.append(o)
        ls_.append(lse)
        keep.append(saved)
    y = merge_forward(f"{lname}_merge", os_, ls_, s)
    return y, (rotated, os_, ls_, keep)


def attention_backward(lname, qkv3, cs3, sn3, res, dmix3):
    rotated, os_, ls_, keep = res
    dm = merge_backward(f"{lname}_merge_b", os_, ls_, dmix3)
    dys = [attn_backward(f"{lname}_attn{pi}_b", rotated[pi], d, keep[pi], dm[pi], dm[3 + pi]) for pi, (_, d) in enumerate(ATT_PATTERNS)]
    return rope_backward(f"{lname}_rope_b", qkv3, cs3, sn3, dys)


def mm(name, a, b, mode, out_dtype=F32, res=None, gate=None, tm=1408, tn=1536, tk=1408):
    if mode == "nn":
        (m, k), n = a.shape, b.shape[1]
    elif mode == "nt":
        (m, k), n = a.shape, b.shape[0]
    else:
        (k, m), n = a.shape, b.shape[1]
    tm, tn, tk = _tile(m, tm), _tile(n, tn), _tile(k, tk)
    nk = k // tk
    a_spec = pl.BlockSpec((tk, tm), lambda i, j, q: (q, i)) if mode == "tn" else pl.BlockSpec((tm, tk), lambda i, j, q: (i, q))
    b_spec = pl.BlockSpec((tn, tk), lambda i, j, q: (j, q)) if mode == "nt" else pl.BlockSpec((tk, tn), lambda i, j, q: (q, j))
    o_spec = pl.BlockSpec((tm, tn), lambda i, j, q: (i, j))
    fused = res is not None

    def body(*refs):
        if fused:
            a_ref, b_ref, r_ref, g_ref, o_ref, o2_ref, acc = refs
        else:
            a_ref, b_ref, o_ref, acc = refs
        q = pl.program_id(2)

        @pl.when(q == 0)
        def _():
            acc[...] = jnp.zeros(acc.shape, F32)

        acc[...] += _mxu(a_ref[...], b_ref[...], mode)

        @pl.when(q == nk - 1)
        def _():
            o_ref[...] = acc[...].astype(o_ref.dtype)
            if fused:
                o2_ref[...] = r_ref[...] + g_ref[...] * acc[...]

    ins, in_specs = [a, b], [a_spec, b_spec]
    out_shape, out_specs = [jax.ShapeDtypeStruct((m, n), out_dtype)], [o_spec]
    if fused:
        ins += [res, gate]
        in_specs += [o_spec, pl.BlockSpec((1, tn), lambda i, j, q: (0, j))]
        out_shape.append(jax.ShapeDtypeStruct((m, n), F32))
        out_specs.append(o_spec)
    out = pl.pallas_call(
        body, name=name, grid=(m // tm, n // tn, nk), in_specs=in_specs, out_specs=out_specs, out_shape=out_shape,
        scratch_shapes=[pltpu.VMEM((tm, tn), F32)],
        compiler_params=pltpu.CompilerParams(dimension_semantics=("parallel", "parallel", "arbitrary"),
                                             vmem_limit_bytes=VMEM_LIMIT_BYTES),
    )(*ins)
    return tuple(out) if fused else out[0]


def final_loss(name, x3, t3, g):
    s, d = x3.shape[1], x3.shape[2]
    t = _ROW_T

    def body(x_ref, t_ref, g_ref, loss_ref, dx_ref, dg_ref):
        i = pl.program_id(0)
        tv = t_ref[0]

        def f(x, gg):
            y = x * lax.rsqrt(jnp.mean(x * x, axis=-1, keepdims=True) + NORM_EPS) * gg
            e = y - tv
            return 0.5 * jnp.sum(jnp.mean(e * e, axis=-1, keepdims=True), axis=0, keepdims=True)

        l, vjp = jax.vjp(f, x_ref[0], g_ref[...])
        dx, dg = vjp(jnp.ones((1, 1), F32))
        dx_ref[0] = dx

        @pl.when(i == 0)
        def _():
            loss_ref[...] = jnp.zeros(loss_ref.shape, F32)
            dg_ref[...] = jnp.zeros(dg_ref.shape, F32)

        loss_ref[...] += jnp.broadcast_to(l, loss_ref.shape)
        dg_ref[...] += dg

    row = pl.BlockSpec((1, t, d), lambda i: (0, i, 0))
    vec = pl.BlockSpec((1, d), lambda i: (0, 0))
    return pl.pallas_call(
        body, name=name, grid=(s // t,), in_specs=[row, row, vec],
        out_specs=[pl.BlockSpec((8, 128), lambda i: (0, 0)), row, vec],
        out_shape=[jax.ShapeDtypeStruct((8, 128), F32), jax.ShapeDtypeStruct(x3.shape, F32), jax.ShapeDtypeStruct((1, d), F32)],
        compiler_params=pltpu.CompilerParams(dimension_semantics=("arbitrary",), vmem_limit_bytes=VMEM_LIMIT_BYTES),
    )(x3, t3, g)


_ADA_TN = 512


def ada_forward(name, c16, ada_w):
    depth, d, cols = ada_w.shape

    def body(c_ref, w_ref, o_ref):
        o_ref[0] = _mxu(_silu(c_ref[...]), w_ref[0], "nn")

    return pl.pallas_call(
        body, name=name, grid=(depth, cols // _ADA_TN),
        in_specs=[pl.BlockSpec((16, d), lambda l, j: (0, 0)), pl.BlockSpec((1, d, _ADA_TN), lambda l, j: (l, 0, j))],
        out_specs=pl.BlockSpec((1, 16, _ADA_TN), lambda l, j: (l, 0, j)),
        out_shape=jax.ShapeDtypeStruct((depth, 16, cols), F32),
        compiler_params=pltpu.CompilerParams(dimension_semantics=("arbitrary", "arbitrary"), vmem_limit_bytes=VMEM_LIMIT_BYTES),
    )(c16, ada_w)


def ada_backward(name, c16, dmod16, w, m, v):
    depth, d, cols = w.shape

    def body(c_ref, dm_ref, w_ref, m_ref, v_ref, g_ref, dl_ref, nm_ref, nv_ref):
        g = _mxu(_silu(c_ref[...]), dm_ref[0], "tn")
        _, (delta, nm, nv) = _adam_fn(None, None, [], [w_ref[0], g, m_ref[0], v_ref[0]], [])
        g_ref[0], dl_ref[0], nm_ref[0], nv_ref[0] = g, delta, nm, nv

    blk = pl.BlockSpec((1, d, _ADA_TN), lambda l, j: (l, 0, j))
    return pl.pallas_call(
        body, name=name, grid=(depth, cols // _ADA_TN),
        in_specs=[pl.BlockSpec((16, d), lambda l, j: (0, 0)), pl.BlockSpec((1, 16, _ADA_TN), lambda l, j: (l, 0, j)), blk, blk, blk],
        out_specs=[blk] * 4, out_shape=[jax.ShapeDtypeStruct(w.shape, F32)] * 4,
        compiler_params=pltpu.CompilerParams(dimension_semantics=("arbitrary", "arbitrary"), vmem_limit_bytes=VMEM_LIMIT_BYTES),
    )(c16, dmod16, w, m, v)


def _sum_fn(ci, b, carries, rows, vecs):
    acc = rows[0]
    for r in rows[1:]:
        acc = acc + r
    return [], [acc]


def sum_slots(name, a, nsum, out_dtype=F32):
    n, r, c = a.shape
    nb = n // nsum
    t = _tile(r, 256, 8)
    rows = [Row(a, fb=(lambda b, k=k: k * nb + b)) for k in range(nsum)]
    (out,), _ = scan_fwd(name, _sum_fn, nb=nb, nchunk=r // t, t=t, rows=rows, vecs=[], carries=[],
                         outs=[out_row((nb, r, c), out_dtype, fb=lambda b: b)], save=False)
    return out


def _sum_my_layer_fn(ci, b, carries, rows, vecs):
    layer0, layer1, theirs = rows
    return [], [jnp.where(lax.axis_index("c") == 0, layer0, layer1) + theirs]


def sum_cores(name, g, theirs, out_dtype):
    _, nb, r, c = g.shape
    g8 = g.reshape(2 * nb, r, c)
    t = _tile(r, 256, 8)
    rows = [Row(g8, fb=lambda b: b), Row(g8, fb=lambda b: nb + b), Row(theirs, fb=lambda b: b)]
    (out,), _ = scan_fwd(name, _sum_my_layer_fn, nb=nb, nchunk=r // t, t=t, rows=rows, vecs=[], carries=[],
                         outs=[out_row((nb, r, c), out_dtype, fb=lambda b: b)], save=False)
    return out


def _flip(mask, pos):
    return tuple((1 - p) if m else p for m, p in zip(mask, pos))


ALL_PEERS = [(a, b, c) for a in (0, 1) for b in (0, 1) for c in (0, 1)][1:]
CHIP_PEERS = [(1, 0, 0), (0, 1, 0), (1, 1, 0)]
SIBLING = [(0, 0, 1)]


def _divisor(size, target, unit):
    best = 1
    for n in range(1, target + 1):
        if size % n == 0 and (size // n) % unit == 0:
            best = n
    return best


def _pieces(src, dst, pieces):
    shape = src.shape
    unit = 16 if src.dtype == BF16 else 8
    if pieces <= 1:
        return [(src, dst)]
    if len(shape) == 2:
        n = _divisor(shape[0], pieces, unit)
        s = shape[0] // n
        return [(src.at[pl.ds(i * s, s)], dst.at[pl.ds(i * s, s)]) for i in range(n)]
    assert len(shape) == 3, shape
    n = _divisor(shape[1], max(pieces // shape[0], 1), unit)
    s = shape[1] // n
    return [(src.at[j, pl.ds(i * s, s)], dst.at[j, pl.ds(i * s, s)]) for j in range(shape[0]) for i in range(n)]


def comm_call(name, arrays, out_shapes, masks, src_fn, dst_fn, local_fn=None, pieces=1):
    na, npeer = len(arrays), len(masks)

    def body(*refs):
        ins, outs = refs[:na], refs[na:2 * na]
        send_sems, recv_sems, loc_sems = refs[2 * na:]
        me = (lax.axis_index("x"), lax.axis_index("y"), lax.axis_index("c"))
        local = []
        if local_fn is not None:
            for k in range(na):
                s, d = local_fn(k, ins[k], outs[k], me)
                for ps, pd in _pieces(s, d, pieces):
                    pltpu.make_async_copy(ps, pd, loc_sems.at[k]).start()
                local.append(pltpu.make_async_copy(s, d, loc_sems.at[k]))

        def remote(k, p, src, dst, to):
            return pltpu.make_async_remote_copy(
                src_ref=src, dst_ref=dst, send_sem=send_sems.at[k * npeer + p], recv_sem=recv_sems.at[k * npeer + p],
                device_id=to, device_id_type=MESH)

        for k in range(na):
            for p in range(npeer):
                peer = _flip(masks[p], me)
                for ps, pd in _pieces(src_fn(k, ins[k], me, peer), dst_fn(k, outs[k], me), pieces):
                    remote(k, p, ps, pd, peer).start()
        for k in range(na):
            for p in range(npeer):
                peer = _flip(masks[p], me)
                remote(k, p, src_fn(k, ins[k], me, peer), dst_fn(k, outs[k], peer), peer).wait_recv()
        for k in range(na):
            for p in range(npeer):
                peer = _flip(masks[p], me)
                remote(k, p, src_fn(k, ins[k], me, peer), dst_fn(k, outs[k], me), peer).wait_send()
        for cp in local:
            cp.wait()

    hbm = pl.BlockSpec(memory_space=pl.ANY)
    out = pl.pallas_call(
        body, name=name, in_specs=[hbm] * na, out_specs=[hbm] * na,
        out_shape=[jax.ShapeDtypeStruct(s, a.dtype) for s, a in zip(out_shapes, arrays)],
        scratch_shapes=[pltpu.SemaphoreType.DMA((na * npeer,)), pltpu.SemaphoreType.DMA((na * npeer,)),
                        pltpu.SemaphoreType.DMA((na,))],
    )(*arrays)
    return list(out)


def _dev(pos):
    return 4 * pos[0] + 2 * pos[1] + pos[2]


def _chip(pos):
    return 2 * pos[0] + pos[1]


def allgather8(name, a):
    (out,) = comm_call(name, [a], [(8,) + a.shape], ALL_PEERS,
                       src_fn=lambda k, r, me, peer: r, dst_fn=lambda k, o, sender: o.at[_dev(sender)],
                       local_fn=lambda k, r, o, me: (r, o.at[_dev(me)]))
    return out


def gather_layer_from_chips(name, arrays):
    return comm_call(name, arrays, [(4,) + a.shape[1:] for a in arrays], CHIP_PEERS,
                     src_fn=lambda k, r, me, peer: r.at[me[2]], dst_fn=lambda k, o, sender: o.at[_chip(sender)],
                     local_fn=lambda k, r, o, me: (r.at[me[2]], o.at[_chip(me)]), pieces=8)


def swap_layers(name, arrays, c):
    got = comm_call(name, arrays, [a.shape for a in arrays], SIBLING,
                    src_fn=lambda k, r, me, peer: r, dst_fn=lambda k, o, sender: o, pieces=32)
    return [[jnp.where(c == 0, a, g), jnp.where(c == 0, g, a)] for a, g in zip(arrays, got)]


def swap_other_layer(name, arrays):
    return comm_call(name, arrays, [a.shape[1:] for a in arrays], SIBLING,
                     src_fn=lambda k, r, me, peer: r.at[peer[2]], dst_fn=lambda k, o, sender: o, pieces=32)


def scatter_to_chips(name, arrays):
    return comm_call(name, arrays, [a.shape for a in arrays], CHIP_PEERS,
                     src_fn=lambda k, r, me, peer: r.at[_chip(peer)], dst_fn=lambda k, o, sender: o.at[_chip(sender)],
                     local_fn=lambda k, r, o, me: (r.at[_chip(me)], o.at[_chip(me)]), pieces=8)


def _rows_of(shape):
    return -(-int(np.prod(shape)) // 128)


def _pack(arrs):
    parts = []
    for a in arrs:
        flat = a.reshape(-1).astype(F32)
        parts.append(jnp.pad(flat, (0, _rows_of(a.shape) * 128 - flat.shape[0])).reshape(-1, 128))
    rows = sum(p.shape[0] for p in parts)
    parts.append(jnp.zeros(((-rows) % _ROW_T, 128), F32))
    return jnp.concatenate(parts, axis=0)


def _unpack(buf, shapes):
    out, o = [], 0
    for s in shapes:
        r, n = _rows_of(s), int(np.prod(s))
        out.append(buf[o:o + r].reshape(-1)[:n].reshape(s))
        o += r
    return out


_WEIGHTS = ["ada_w", "ada_b", "norm1_g", "w_in", "ssd_conv_w", "ssd_conv_b", "ssd_dt_bias", "ssd_a_log", "ssd_d", "ssd_norm_g",
            "pool_w", "pool_scale", "w_out", "norm2_g", "ffn_up", "ffn_conv_w", "ffn_conv_b", "ffn_down", "final_g"]
_BIG = ["w_in", "w_out", "ffn_up", "ffn_down"]
_SMALL = [n for n in _WEIGHTS if n not in _BIG and n != "ada_w"]
_COL_SHARDED_SMALL = {"ssd_conv_w": 256, "ffn_conv_w": 1408}


def _pad_lanes(v, n=128):
    return jnp.pad(v.astype(F32), (0, n - v.shape[0]))[None]


def _perm_cols(w):
    pad = jnp.zeros(w.shape[:-1] + (IN_WP - IN_W,), w.dtype)
    return jnp.concatenate([w[..., :1536], w[..., 1544:1800], w[..., 1536:1544], pad, w[..., 1800:]], axis=-1)


def _unperm_cols(g):
    return jnp.concatenate([g[..., :1536], g[..., 1792:1800], g[..., 1536:1792], g[..., IN_MAIN:]], axis=-1)


def _cols_by_chip(g, ncol, order=None):
    g = g.reshape(g.shape[0], 4, ncol)
    if order is not None:
        g = jnp.stack([g[:, j] for j in order], axis=1)
    return g.transpose(1, 0, 2)


def _ffn_block_perm(a):
    n = a.shape[-1] // 4
    return jnp.concatenate([a[..., j * n:(j + 1) * n] for j in FFN_BLOCK_ORDER], axis=-1)


def _layer_forward(i, x3, modv, wts, sp, cs3, sn3):
    sh1, sc1, g1, sh2, sc2, g2 = modv
    h1 = norm_mod_forward(f"l{i}_norm1", x3, wts["norm1_g"], sc1, sh1)
    proj3 = mm(f"l{i}_proj", h1[0], wts["w_in"][:, :IN_MAIN], "nn")[None]
    qkv3 = mm(f"l{i}_qkv", h1[0], wts["w_in"][:, IN_MAIN:], "nn")[None]
    y_ssd, sv_ssd = ssd_forward(f"l{i}_ssd", proj3, sp)
    y_pool, sv_pool = pool_forward(f"l{i}_pool", proj3, wts["wbd"], wts["pool_scale"])
    y_att, res_att = attention_forward(f"l{i}", qkv3, cs3, sn3)
    mix = jnp.concatenate([y_ssd, y_pool, y_att], axis=-1)
    out, x1 = mm(f"l{i}_wout", mix[0], wts["w_out"], "nn", res=x3[0], gate=g1)
    x1 = x1[None]
    h2 = norm_mod_forward(f"l{i}_norm2", x1, wts["norm2_g"], sc2, sh2)
    up3 = mm(f"l{i}_up", h2[0], wts["ffn_up"], "nn")[None]
    act, sv_ffn = ffn_mid_forward(f"l{i}_ffn", up3, wts["ffn_conv_w"], wts["ffn_conv_b"])
    dn, x2 = mm(f"l{i}_down", act[0], wts["ffn_down"], "nn", res=x1[0], gate=g2)
    keep = dict(x=x3, h1=h1, proj3=proj3, qkv3=qkv3, sv_ssd=sv_ssd, sv_pool=sv_pool, res_att=res_att, mix=mix, out=out[None],
                x1=x1, h2=h2, up3=up3, act=act, sv_ffn=sv_ffn, dn=dn[None])
    return x2[None], keep


def _layer_backward(i, dx2, keep, modv, wts, sp, cs3, sn3):
    sh1, sc1, g1, sh2, sc2, g2 = modv
    k = keep
    d_dn, d_g2 = gate_backward(f"l{i}_gate2_b", k["dn"], g2, dx2)
    d_act = mm(f"l{i}_down_bx", d_dn[0], wts["ffn_down"], "nt")
    g_down = mm(f"l{i}_down_bw", k["act"][0], d_dn[0], "tn")
    (d_up,), dv_ffn = ffn_mid_backward(f"l{i}_ffn_b", k["up3"], wts["ffn_conv_w"], wts["ffn_conv_b"], k["sv_ffn"], d_act[None])
    d_h2 = mm(f"l{i}_up_bx", d_up[0], wts["ffn_up"], "nt")
    g_up = mm(f"l{i}_up_bw", k["h2"][0], d_up[0], "tn")
    dx1, (d_n2, d_sc2, d_sh2) = norm_mod_backward(f"l{i}_norm2_b", k["x1"], wts["norm2_g"], sc2, sh2, d_h2[None], dx2)
    d_out, d_g1 = gate_backward(f"l{i}_gate1_b", k["out"], g1, dx1)
    d_mix = mm(f"l{i}_wout_bx", d_out[0], wts["w_out"], "nt")[None]
    g_wout = mm(f"l{i}_wout_bw", k["mix"][0], d_out[0], "tn")
    (dz, dxs, dbm, dcm, ddt), dv_ssd = ssd_backward(f"l{i}_ssd_b", k["proj3"], sp, k["sv_ssd"], d_mix)
    (du_pool,), (d_wbd, d_pscale) = pool_backward(f"l{i}_pool_b", k["proj3"], wts["wbd"], wts["pool_scale"], k["sv_pool"], d_mix)
    d_qkv = attention_backward(f"l{i}", k["qkv3"], cs3, sn3, k["res_att"], d_mix)
    d_proj = jnp.concatenate([dz[0], dxs[0], dbm[0], dcm[0], du_pool[0], (ddt[0] + ddt[1]).astype(BF16), d_qkv[0]], axis=-1)
    d_h1 = mm(f"l{i}_proj_bx", d_proj, wts["w_in"], "nt")
    g_win = mm(f"l{i}_proj_bw", k["h1"][0], d_proj, "tn")
    dx, (d_n1, d_sc1, d_sh1) = norm_mod_backward(f"l{i}_norm1_b", k["x"], wts["norm1_g"], sc1, sh1, d_h1[None], dx1)
    dcwx, dcbx, dcwb, dcbb, dcwc, dcbc, ddtb, dalog, ddsk, dng = dv_ssd
    small = dict(
        norm1_g=d_n1[0], norm2_g=d_n2[0],
        ssd_conv_w=jnp.concatenate([dcwx[:, :512], dcwb[:, 512:768], dcwc[:, 768:]], axis=1),
        ssd_conv_b=jnp.concatenate([dcbx[0, :512], dcbb[0, 512:768], dcbc[0, 768:]]),
        ssd_dt_bias=ddtb[0, :8], ssd_a_log=dalog[0, :8], ssd_d=ddsk[0, :8], ssd_norm_g=dng[0],
        pool_w=jnp.stack([d_wbd[64 * g:64 * g + 64, 64 * g:64 * g + 64] for g in range(4)]), pool_scale=d_pscale[0],
        ffn_conv_w=_ffn_block_perm(dv_ffn[0]), ffn_conv_b=_ffn_block_perm(dv_ffn[1][0]),
    )
    dmod = jnp.concatenate([d_sh1[0], d_sc1[0], d_g1[0], d_sh2[0], d_sc2[0], d_g2[0]])
    return dx, dict(w_in=g_win, w_out=g_wout, ffn_up=g_up, ffn_down=g_down), small, dmod


def kernel(x, c, positions, ada_w, ada_b, norm1_g, w_in, ssd_conv_w, ssd_conv_b, ssd_dt_bias, ssd_a_log, ssd_d, ssd_norm_g, pool_w, pool_scale, w_out, norm2_g, ffn_up, ffn_conv_w, ffn_conv_b, ffn_down, final_g, loss_target, m_ada_w, m_ada_b, m_norm1_g, m_w_in, m_ssd_conv_w, m_ssd_conv_b, m_ssd_dt_bias, m_ssd_a_log, m_ssd_d, m_ssd_norm_g, m_pool_w, m_pool_scale, m_w_out, m_norm2_g, m_ffn_up, m_ffn_conv_w, m_ffn_conv_b, m_ffn_down, m_final_g, v_ada_w, v_ada_b, v_norm1_g, v_w_in, v_ssd_conv_w, v_ssd_conv_b, v_ssd_dt_bias, v_ssd_a_log, v_ssd_d, v_ssd_norm_g, v_pool_w, v_pool_scale, v_w_out, v_norm2_g, v_ffn_up, v_ffn_conv_w, v_ffn_conv_b, v_ffn_down, v_final_g):
    args = dict(locals())
    w = {n: args[n] for n in _WEIGHTS}
    m = {n: args["m_" + n] for n in _WEIGHTS}
    v = {n: args["v_" + n] for n in _WEIGHTS}
    d = D_MODEL
    me = (lax.axis_index("x"), lax.axis_index("y"), lax.axis_index("c"))
    chip, dev = _chip(me), _dev(me)

    shapes0 = [c.shape, ssd_conv_w.shape, ffn_conv_w.shape]
    g0 = allgather8("gather_c_conv", _pack([c, ssd_conv_w, ffn_conv_w]))
    c16 = jnp.pad(g0[:, :d // 128, :].reshape(8, d), ((0, 8), (0, 0)))
    by_chip = [_unpack(g0[2 * j], shapes0) for j in range(4)]
    conv_w_full = jnp.concatenate([p[1] for p in by_chip], axis=-1)
    fconv_w_full = jnp.concatenate([p[2] for p in by_chip], axis=-1)

    modp = ada_forward("ada_fwd", c16, ada_w)[:, :8]
    g1 = allgather8("gather_mod", _pack([modp]))
    modfull = jnp.concatenate([_unpack(g1[2 * j], [modp.shape])[0] for j in range(4)], axis=-1)
    mod = lax.dynamic_index_in_dim(modfull, dev, axis=1, keepdims=False) + ada_b
    modv = [[mod[i, q * d:(q + 1) * d][None] for q in range(6)] for i in range(DEPTH)]

    got = gather_layer_from_chips("gather_w", [w[n].astype(BF16) for n in _BIG])
    both = swap_layers("swap_w", got, me[2])
    full = dict(
        w_in=[_perm_cols(jnp.concatenate([a[j] for j in range(4)], axis=1)) for a in both[0]],
        w_out=[a.reshape(d, d) for a in both[1]],
        ffn_up=[jnp.concatenate([a[j] for j in FFN_BLOCK_ORDER], axis=1) for a in both[2]],
        ffn_down=[a.reshape(FFN_DIM, d) for a in both[3]],
    )

    cs3, sn3 = rope_tables(positions[0])
    eye4 = jnp.eye(4, dtype=F32)
    wts, sps = [], []
    for i in range(DEPTH):
        wts.append(dict(
            w_in=full["w_in"][i], w_out=full["w_out"][i], ffn_up=full["ffn_up"][i], ffn_down=full["ffn_down"][i],
            norm1_g=norm1_g[i][None], norm2_g=norm2_g[i][None], pool_scale=pool_scale[i][None],
            wbd=(eye4[:, None, :, None] * pool_w[i][:, :, None, :]).reshape(POOL_W, POOL_W),
            ffn_conv_w=_ffn_block_perm(fconv_w_full[i]), ffn_conv_b=_ffn_block_perm(ffn_conv_b[i])[None]))
        sps.append(dict(cw=conv_w_full[i], cb=ssd_conv_b[i][None], dtb=_pad_lanes(ssd_dt_bias[i]), alog=_pad_lanes(ssd_a_log[i]),
                        dsk=_pad_lanes(ssd_d[i]), ng=ssd_norm_g[i][None]))

    xc, keeps = x, []
    for i in range(DEPTH):
        xc, keep = _layer_forward(i, xc, modv[i], wts[i], sps[i], cs3, sn3)
        keeps.append(keep)
    lossblk, dx, d_final = final_loss("final_loss", xc, loss_target, final_g[None])
    loss = lax.psum(lossblk[0, 0], ("x", "y", "c"))

    big_g, small_g, dmods = [None] * DEPTH, [None] * DEPTH, [None] * DEPTH
    for i in reversed(range(DEPTH)):
        dx, big_g[i], small_g[i], dmods[i] = _layer_backward(i, dx, keeps[i], modv[i], wts[i], sps[i], cs3, sn3)

    by_dest = [
        jnp.stack([_cols_by_chip(_unperm_cols(big_g[i]["w_in"]), IN_W // 4) for i in range(DEPTH)]),
        jnp.stack([big_g[i]["w_out"].reshape(4, d // 4, d) for i in range(DEPTH)]),
        jnp.stack([_cols_by_chip(big_g[i]["ffn_up"], 2 * FFN_DIM // 4, FFN_BLOCK_ORDER) for i in range(DEPTH)]),
        jnp.stack([big_g[i]["ffn_down"].reshape(4, FFN_DIM // 4, d) for i in range(DEPTH)]),
    ]
    theirs = swap_other_layer("swap_g", by_dest)
    core_sum = [sum_cores(f"sum_cores_{n}", g, t, BF16) for n, g, t in zip(_BIG, by_dest, theirs)]
    from_chips = scatter_to_chips("scatter_g", core_sum)
    chip_sum = [sum_slots(f"sum_chips_{n}", q, 4)[0] for n, q in zip(_BIG, from_chips)]
    reduced = swap_layers("swap_r", chip_sum, me[2])
    grads = {n: jnp.stack(r) for n, r in zip(_BIG, reduced)}

    part = dict(ada_b=jnp.stack(dmods), final_g=d_final[0])
    for n in _SMALL:
        if n not in part:
            part[n] = jnp.stack([small_g[i][n] for i in range(DEPTH)])
    full_shapes = [part[n].shape for n in _SMALL]
    gs = allgather8("gather_small", _pack([part[n] for n in _SMALL]))
    tot = _unpack(sum_slots("sum_small", gs, 8)[0], full_shapes)
    small_tot = dict(zip(_SMALL, tot))
    dmod_all = gs[:, :DEPTH * 6 * d // 128, :].reshape(8, DEPTH, 6 * d)
    for n, ncol in _COL_SHARDED_SMALL.items():
        small_tot[n] = lax.dynamic_slice_in_dim(small_tot[n], chip * ncol, ncol, axis=2)
    grads.update(small_tot)

    ncol = ada_w.shape[2]
    dm = lax.dynamic_slice_in_dim(dmod_all, chip * ncol, ncol, axis=2).transpose(1, 0, 2)
    upd = {}
    g_ada, *upd["ada_w"] = ada_backward("ada_bwd", c16, jnp.pad(dm, ((0, 0), (0, 8), (0, 0))), ada_w, m["ada_w"], v["ada_w"])
    grads["ada_w"] = g_ada

    for n in _BIG:
        upd[n] = adamw(f"adam_{n}", w[n], grads[n], m[n], v[n])
    shapes_s = [w[n].shape for n in _SMALL]
    packed = [_pack([src[n] for n in _SMALL]) for src in (w, grads, m, v)]
    outs_s = [_unpack(o, shapes_s) for o in adamw("adam_small", *packed)]
    for q, n in enumerate(_SMALL):
        upd[n] = [outs_s[0][q], outs_s[1][q], outs_s[2][q]]

    return (loss, dx, *[grads[n] for n in _WEIGHTS], *[upd[n][0] for n in _WEIGHTS], *[upd[n][1] for n in _WEIGHTS],
            *[upd[n][2] for n in _WEIGHTS])
```

```python
import functools
import math

import numpy as np
import jax
import jax.numpy as jnp
from jax import lax
from jax.experimental import pallas as pl
from jax.experimental.pallas import tpu as pltpu

F32 = jnp.float32
BF16 = jnp.bfloat16
HI = lax.Precision.HIGHEST
MESH = pl.DeviceIdType.MESH

D_MODEL = 1024
SEQ = 4096
DEPTH = 2
SSD_INNER = 512
SSD_HEADS = 8
SSD_STATE = 128
POOL_W = 256
POOL_WINDOWS = (2, 4, 8, 16)
ATT_W = 256
ATT_HEADS = 4
ATT_HEAD_DIM = 64
ATT_PATTERNS = ((128, 1), (512, 4), (2048, 16))
ATT_BLOCK = 128
ROT_DIM = 16
ROPE_THETA = 500000.0
IN_W = 2568
IN_WP = 2688
IN_MAIN = 1920
FFN_DIM = 2816
NORM_EPS = 1e-6
ADAM_LR, ADAM_B1, ADAM_B2, ADAM_EPS, ADAM_WD, ADAM_STEP = 0.001, 0.9, 0.999, 1e-08, 0.01, 10

VMEM_LIMIT_BYTES = 56 * 1024 * 1024
NEG = -1e30


def _mxu(a, b, mode):
    dims = {"nn": ((1,), (0,)), "nt": ((1,), (1,)), "tn": ((0,), (0,))}[mode]
    return lax.dot_general(a.astype(BF16), b.astype(BF16), (dims, ((), ())), preferred_element_type=F32)


@functools.partial(jax.custom_vjp, nondiff_argnums=(2,))
def _bdot(a, b, mode):
    return _mxu(a, b, mode)


def _bdot_fwd(a, b, mode):
    return _mxu(a, b, mode), (a, b)


def _bdot_bwd(mode, res, g):
    a, b = res
    if mode == "nn":
        return _mxu(g, b, "nt"), _mxu(a, g, "tn")
    if mode == "nt":
        return _mxu(g, b, "nn"), _mxu(g, a, "tn")
    return _mxu(b, g, "nt"), _mxu(a, g, "nn")


_bdot.defvjp(_bdot_fwd, _bdot_bwd)


def _fxu(a, b, mode):
    dims = {"nn": ((1,), (0,)), "nt": ((1,), (1,)), "tn": ((0,), (0,))}[mode]
    return lax.dot_general(a, b, (dims, ((), ())), precision=HI, preferred_element_type=F32)


@functools.partial(jax.custom_vjp, nondiff_argnums=(2,))
def _fdot(a, b, mode):
    return _fxu(a, b, mode)


def _fdot_fwd(a, b, mode):
    return _fxu(a, b, mode), (a, b)


def _fdot_bwd(mode, res, g):
    a, b = res
    if mode == "nn":
        return _fxu(g, b, "nt"), _fxu(a, g, "tn")
    if mode == "nt":
        return _fxu(g, b, "nn"), _fxu(g, a, "tn")
    return _fxu(b, g, "nt"), _fxu(a, g, "nn")


_fdot.defvjp(_fdot_fwd, _fdot_bwd)


def _iota(shape, dim):
    return lax.broadcasted_iota(jnp.int32, shape, dim)


def _make_shift(h):
    @functools.partial(jax.custom_vjp, nondiff_argnums=(2,))
    def shift(halo, cur, k):
        if k == 0:
            return cur
        full = jnp.concatenate([halo, cur], axis=0)
        return pltpu.roll(full, k, 0)[h:]

    def fwd(halo, cur, k):
        return shift(halo, cur, k), None

    def bwd(k, _, g):
        t, w = g.shape
        if k == 0:
            return jnp.zeros((h, w), F32), g
        d_cur = jnp.where(_iota((t, w), 0) < t - k, pltpu.roll(g, t - k, 0), 0.0)
        top = g[:h]
        d_halo = jnp.where(_iota((h, w), 0) >= h - k, pltpu.roll(top, h - k, 0) if k < h else top, 0.0)
        return d_halo, d_cur

    shift.defvjp(fwd, bwd)
    return shift


_shift8 = _make_shift(8)
_shift16 = _make_shift(16)


def _make_tail(h):
    @jax.custom_vjp
    def tail(x):
        return x[x.shape[0] - h:]

    def fwd(x):
        return tail(x), x.shape[0]

    def bwd(t, g):
        return (jnp.concatenate([jnp.zeros((t - h, g.shape[1]), F32), g], axis=0),)

    tail.defvjp(fwd, bwd)
    return tail


_tail8 = _make_tail(8)
_tail16 = _make_tail(16)


@jax.custom_vjp
def _cumsum_rows(x):
    t = x.shape[0]
    row, s = _iota(x.shape, 0), 1
    while s < t:
        x = x + jnp.where(row >= s, pltpu.roll(x, s, 0), 0.0)
        s *= 2
    return x


def _cumsum_rows_fwd(x):
    return _cumsum_rows(x), None


def _cumsum_rows_bwd(_, g):
    t = g.shape[0]
    row, s = _iota(g.shape, 0), 1
    while s < t:
        g = g + jnp.where(row < t - s, pltpu.roll(g, t - s, 0), 0.0)
        s *= 2
    return (g,)


_cumsum_rows.defvjp(_cumsum_rows_fwd, _cumsum_rows_bwd)


@jax.custom_vjp
def _rot_pairs(t):
    e = _iota(t.shape, 1) % ATT_HEAD_DIM
    n = t.shape[1]
    return jnp.where(e < 8, -pltpu.roll(t, n - 8, 1), jnp.where(e < 16, pltpu.roll(t, 8, 1), 0.0))


def _rot_pairs_fwd(t):
    return _rot_pairs(t), None


def _rot_pairs_bwd(_, g):
    e = _iota(g.shape, 1) % ATT_HEAD_DIM
    n = g.shape[1]
    return (pltpu.roll(jnp.where(e < 8, -g, 0.0), 8, 1) + pltpu.roll(jnp.where(jnp.logical_and(e >= 8, e < 16), g, 0.0), n - 8, 1),)


_rot_pairs.defvjp(_rot_pairs_fwd, _rot_pairs_bwd)


def _make_thirds():
    @jax.custom_vjp
    def thirds(x):
        w = x.shape[1] // 3
        return x[:, :w], x[:, w:2 * w], x[:, 2 * w:]

    def fwd(x):
        return thirds(x), None

    def bwd(_, g):
        return (jnp.concatenate(g, axis=1),)

    thirds.defvjp(fwd, bwd)
    return thirds


_thirds = _make_thirds()


def _rowk(w, k):
    return jnp.sum(jnp.where(_iota(w.shape, 0) == k, w, 0.0), axis=0, keepdims=True)


def _silu(x):
    return x * (0.5 * jnp.tanh(0.5 * x) + 0.5)


def _softplus(x):
    return jnp.maximum(x, 0.0) + jnp.log(1.0 + jnp.exp(-jnp.abs(x)))


def _tile(dim, target, unit=128):
    if dim <= target:
        return dim
    best = None
    for t in range(unit, target + 1, unit):
        if dim % t == 0:
            best = t
    assert best is not None, (dim, target)
    return best


class Row:
    def __init__(self, arr, w=None, fb=None, fc=None, diff=True, slot=False, dcols=None, dfc=None, ddtype=F32, view=None):
        self.ddtype = ddtype
        self.view = view
        self.arr = arr
        self.w = arr.shape[2] if w is None else w
        self.fb = (lambda b: 0) if fb is None else fb
        self.fc = (lambda b: 0) if fc is None else fc
        self.diff = diff
        self.slot = slot
        self.dcols = dcols
        self.dfc = dfc


class Vec:
    def __init__(self, arr, w=None, fc=None, diff=True):
        self.arr = arr
        self.w = arr.shape[1] if w is None else w
        self.fc = fc
        self.diff = diff


def _row_spec(r, t, nchunk, reverse):
    shape = (1, t, r.w) if r.view is None else (1, t // r.view, r.view * r.w)
    if reverse:
        return pl.BlockSpec(shape, lambda b, i, r=r: (r.fb(b), nchunk - 1 - i, r.fc(b)))
    return pl.BlockSpec(shape, lambda b, i, r=r: (r.fb(b), i, r.fc(b)))


def _load_row(ref, r, t, scr):
    if r.view is None:
        return ref[0]
    d, w = r.view, r.w
    for q in range(d):
        for j in range(w // 128):
            scr[j, pl.ds(q, t // d, stride=d), :] = ref[0, :, q * w + 128 * j:q * w + 128 * (j + 1)].astype(F32)
    return jnp.concatenate([scr[j] for j in range(w // 128)], axis=1)


def _store_row(ref, r, t, scr, val):
    if r.view is None:
        ref[0] = val.astype(ref.dtype)
        return
    d, w = r.view, r.w
    for j in range(w // 128):
        scr[j] = val[:, 128 * j:128 * (j + 1)]
    for q in range(d):
        for j in range(w // 128):
            ref[0, :, q * w + 128 * j:q * w + 128 * (j + 1)] = scr[j, pl.ds(q, t // d, stride=d), :].astype(ref.dtype)


def _view_scratch(specs, t):
    ws = [r.w for r in specs if r.view is not None]
    return [pltpu.VMEM((max(ws) // 128, t, 128), F32)] if ws else []


def _vec_spec(v):
    if v.fc is None:
        return pl.BlockSpec(v.arr.shape, lambda b, i: (0, 0))
    return pl.BlockSpec((v.arr.shape[0], v.w), lambda b, i, v=v: (0, v.fc(b)))


def _cparams():
    return pltpu.CompilerParams(dimension_semantics=("arbitrary", "arbitrary"), vmem_limit_bytes=VMEM_LIMIT_BYTES)


def scan_fwd(name, fn, *, nb, nchunk, t, rows, vecs, carries, outs, save):
    nr, nv, nc, no = len(rows), len(vecs), len(carries), len(outs)

    def body(*refs):
        row_refs, vec_refs = refs[:nr], refs[nr:nr + nv]
        out_refs = refs[nr + nv:nr + nv + no]
        save_refs = refs[nr + nv + no:nr + nv + no + (nc if save else 0)]
        scr = refs[len(refs) - 1] if stage else None
        car = refs[len(refs) - nc - len(stage):len(refs) - len(stage)] if nc else ()
        b, i = pl.program_id(0), pl.program_id(1)
        if nc:
            @pl.when(i == 0)
            def _():
                for c_ref in car:
                    c_ref[...] = jnp.zeros(c_ref.shape, F32)
        cin = [c_ref[...] for c_ref in car]
        if save:
            for s_ref, cv in zip(save_refs, cin):
                s_ref[0, 0] = cv
        new_c, o = fn(i, b, cin, [_load_row(ref, r, t, scr) for ref, r in zip(row_refs, rows)], [v[...] for v in vec_refs])
        for c_ref, cv in zip(car, new_c):
            c_ref[...] = cv
        for o_ref, spec, ov in zip(out_refs, outs, o):
            _store_row(o_ref, spec, t, scr, ov)

    stage = _view_scratch(list(rows) + list(outs), t)
    out_shape = [o.arr for o in outs]
    out_specs = [_row_spec(o, t, nchunk, False) for o in outs]
    if save:
        for cs in carries:
            out_shape.append(jax.ShapeDtypeStruct((nb, nchunk) + tuple(cs), F32))
            out_specs.append(pl.BlockSpec((1, 1) + tuple(cs), lambda b, i: (b, i, 0, 0)))
    res = pl.pallas_call(
        body, name=name, grid=(nb, nchunk),
        in_specs=[_row_spec(r, t, nchunk, False) for r in rows] + [_vec_spec(v) for v in vecs],
        out_specs=out_specs, out_shape=out_shape,
        scratch_shapes=[pltpu.VMEM(tuple(cs), F32) for cs in carries] + stage,
        compiler_params=_cparams(),
    )(*[r.arr for r in rows], *[v.arr for v in vecs])
    return list(res[:no]), list(res[no:])


def scan_bwd(name, fn, *, nb, nchunk, t, rows, vecs, carries, saved, douts, adds=None):
    adds = adds or {}
    nr, nv, nc, no = len(rows), len(vecs), len(carries), len(douts)
    dri = [k for k, r in enumerate(rows) if r.diff]
    dvi = [k for k, v in enumerate(vecs) if v.diff]
    add_keys = sorted(adds)
    na = len(add_keys)

    def body(*refs):
        p = 0
        row_refs = refs[p:p + nr]; p += nr
        vec_refs = refs[p:p + nv]; p += nv
        save_refs = refs[p:p + nc]; p += nc
        dout_refs = refs[p:p + no]; p += no
        add_refs = refs[p:p + na]; p += na
        drow_refs = refs[p:p + len(dri)]; p += len(dri)
        dvec_refs = refs[p:p + len(dvi)]; p += len(dvi)
        dcar = refs[p:p + nc]
        scr = refs[len(refs) - 1] if stage else None
        b, ir = pl.program_id(0), pl.program_id(1)
        ci = nchunk - 1 - ir
        if nc:
            @pl.when(ir == 0)
            def _():
                for c_ref in dcar:
                    c_ref[...] = jnp.zeros(c_ref.shape, F32)
        rows_v = [_load_row(ref, r, t, scr) for ref, r in zip(row_refs, rows)]
        vecs_v = [v[...] for v in vec_refs]
        cin = [s[0, 0] for s in save_refs]
        dc = [c_ref[...] for c_ref in dcar]
        dout_v = [_load_row(ref, r, t, scr).astype(F32) for ref, r in zip(dout_refs, douts)]

        def f(cs, dr, dv):
            rr, vv = list(rows_v), list(vecs_v)
            for k, idx in enumerate(dri):
                rr[idx] = dr[k]
            for k, idx in enumerate(dvi):
                vv[idx] = dv[k]
            return fn(ci, b, cs, rr, vv)

        _, vjp = jax.vjp(f, cin, [rows_v[k].astype(F32) for k in dri], [vecs_v[k].astype(F32) for k in dvi])
        dcin, drows, dvecs = vjp((dc, dout_v))
        for c_ref, cv in zip(dcar, dcin):
            c_ref[...] = cv
        for k, (o_ref, ov) in enumerate(zip(drow_refs, drows)):
            if dri[k] in adds:
                ov = ov + add_refs[add_keys.index(dri[k])][0].astype(F32)
            _store_row(o_ref, rows[dri[k]], t, scr, ov)
        for k, (o_ref, ov) in enumerate(zip(dvec_refs, dvecs)):
            first = (ir == 0) if vecs[dvi[k]].fc is not None else jnp.logical_and(ir == 0, b == 0)

            @pl.when(first)
            def _(o_ref=o_ref, ov=ov):
                o_ref[...] = ov

            @pl.when(jnp.logical_not(first))
            def _(o_ref=o_ref, ov=ov):
                o_ref[...] += ov

    stage = _view_scratch(list(rows) + list(douts), t)
    in_specs = ([_row_spec(r, t, nchunk, True) for r in rows] + [_vec_spec(v) for v in vecs]
                + [pl.BlockSpec((1, 1) + tuple(cs), lambda b, i: (b, nchunk - 1 - i, 0, 0)) for cs in carries]
                + [_row_spec(d, t, nchunk, True) for d in douts]
                + [_row_spec(adds[k], t, nchunk, True) for k in add_keys])
    out_shape, out_specs = [], []
    for k in dri:
        r = rows[k]
        if r.slot:
            out_shape.append(jax.ShapeDtypeStruct((nb, r.arr.shape[1], r.w), r.ddtype))
            out_specs.append(pl.BlockSpec((1, t, r.w), lambda b, i: (b, nchunk - 1 - i, 0)))
        elif r.dcols is not None:
            out_shape.append(jax.ShapeDtypeStruct((r.arr.shape[0], r.arr.shape[1], r.dcols), r.ddtype))
            out_specs.append(pl.BlockSpec((1, t, r.w), lambda b, i, r=r: (r.fb(b), nchunk - 1 - i, r.dfc(b))))
        else:
            out_shape.append(jax.ShapeDtypeStruct(r.arr.shape, r.ddtype))
            out_specs.append(_row_spec(r, t, nchunk, True))
    for k in dvi:
        out_shape.append(jax.ShapeDtypeStruct(vecs[k].arr.shape, F32))
        out_specs.append(_vec_spec(vecs[k]))
    res = pl.pallas_call(
        body, name=name, grid=(nb, nchunk), in_specs=in_specs, out_specs=out_specs, out_shape=out_shape,
        scratch_shapes=[pltpu.VMEM(tuple(cs), F32) for cs in carries] + stage,
        compiler_params=_cparams(),
    )(*[r.arr for r in rows], *[v.arr for v in vecs], *saved, *[d.arr for d in douts], *[adds[k].arr for k in add_keys])
    return list(res[:len(dri)]), list(res[len(dri):])


def out_row(shape, dtype=F32, w=None, fb=None, fc=None):
    return Row(jax.ShapeDtypeStruct(shape, dtype), w, fb, fc)


def _conv(shift, halo, cur, w, bias, taps):
    y = bias
    for k in range(taps):
        y = y + _rowk(w, k) * shift(halo, cur, taps - 1 - k)
    return y


def _ssd_fn(ci, b, carries, rows, vecs):
    cx, cb_, cc, ht = carries
    z, xr, br, cr, dtr = rows
    cwx, cbx, cwb, cbb, cwc, cbc, dtb, alog, dsk, ng = vecs
    t = z.shape[0]
    xs = _silu(_conv(_shift8, cx, xr, cwx, cbx, 4))
    bm = _silu(_conv(_shift8, cb_, br, cwb, cbb, 4))
    cm = _silu(_conv(_shift8, cc, cr, cwc, cbc, 4))
    dt = _softplus(dtr + dtb)
    acol = _cumsum_rows(dt * (-jnp.exp(alog)))
    arow = acol.T
    r, c = _iota((t, t), 0), _iota((t, t), 1)
    causal = r >= c
    cbm = _bdot(cm, bm, "nt")
    lane, sub = _iota(acol.shape, 1), _iota(arow.shape, 0)
    colh = _iota(xs.shape, 1) // 64
    a, dtx, dx, acs = jnp.zeros(xs.shape, F32), jnp.zeros(xs.shape, F32), jnp.zeros((1, xs.shape[1]), F32), []
    for j in range(4):
        h = 4 * b + j
        ac = jnp.sum(jnp.where(lane == h, acol, 0.0), axis=1, keepdims=True)
        acs.append(ac)
        a = jnp.where(colh == j, ac, a)
        dtx = jnp.where(colh == j, jnp.sum(jnp.where(lane == h, dt, 0.0), axis=1, keepdims=True), dtx)
        dx = jnp.where(_iota(dx.shape, 1) // 64 == j, jnp.sum(jnp.where(_iota(dsk.shape, 1) == h, dsk, 0.0), axis=1, keepdims=True), dx)
    atot = jnp.sum(jnp.where(_iota(a.shape, 0) == t - 1, a, 0.0), axis=0, keepdims=True)
    x = xs * dtx
    ydiag = jnp.zeros(x.shape, F32)
    for j in range(4):
        ar = jnp.sum(jnp.where(sub == 4 * b + j, arow, 0.0), axis=0, keepdims=True)
        lmat = jnp.exp(jnp.where(causal, acs[j] - ar, NEG))
        ydiag = ydiag + _bdot(cbm * lmat, jnp.where(colh == j, x, 0.0), "nn")
    yoff = _bdot(cm, ht, "nn") * jnp.exp(a)
    ht_new = ht * jnp.exp(atot) + _bdot(bm, x * jnp.exp(atot - a), "tn")
    y = ydiag + yoff + dx * xs
    yz = y * _silu(z)
    yn = yz * lax.rsqrt(jnp.mean(yz * yz, axis=-1, keepdims=True) + NORM_EPS) * ng
    return [_tail8(xr), _tail8(br), _tail8(cr), ht_new], [yn]


_SSD_T = 256
_SSD_CARRIES = [(8, 256), (8, 128), (8, 128), (128, 256)]


def _ssd_io(proj3, p):
    own = lambda b: b
    rows = [Row(proj3, 256, fc=own, dcols=512, dfc=own, ddtype=BF16),
            Row(proj3, 256, fc=lambda b: 2 + b, dcols=512, dfc=own, ddtype=BF16),
            Row(proj3, 128, fc=lambda b: 8 + b, dcols=256, dfc=own, ddtype=BF16),
            Row(proj3, 128, fc=lambda b: 10 + b, dcols=256, dfc=own, ddtype=BF16),
            Row(proj3, 128, fc=lambda b: 14, slot=True)]
    vecs = [Vec(p["cw"], 256, lambda b: b), Vec(p["cb"], 256, lambda b: b),
            Vec(p["cw"], 128, lambda b: 4 + b), Vec(p["cb"], 128, lambda b: 4 + b),
            Vec(p["cw"], 128, lambda b: 6 + b), Vec(p["cb"], 128, lambda b: 6 + b),
            Vec(p["dtb"]), Vec(p["alog"]), Vec(p["dsk"]), Vec(p["ng"], 256, lambda b: b)]
    return rows, vecs


def ssd_forward(name, proj3, p):
    rows, vecs = _ssd_io(proj3, p)
    s = proj3.shape[1]
    (y,), saved = scan_fwd(name, _ssd_fn, nb=2, nchunk=s // _SSD_T, t=_SSD_T, rows=rows, vecs=vecs,
                           carries=_SSD_CARRIES, outs=[out_row((1, s, SSD_INNER), BF16, 256, fc=lambda b: b)], save=True)
    return y, saved


def ssd_backward(name, proj3, p, saved, dmix3):
    rows, vecs = _ssd_io(proj3, p)
    s = proj3.shape[1]
    drows, dvecs = scan_bwd(name, _ssd_fn, nb=2, nchunk=s // _SSD_T, t=_SSD_T, rows=rows, vecs=vecs,
                            carries=_SSD_CARRIES, saved=saved, douts=[Row(dmix3, 256, fc=lambda b: b)])
    return drows, dvecs


def _pool_fn(ci, b, carries, rows, vecs):
    (cu,) = carries
    (u,) = rows
    wbd, scale = vecs
    t = u.shape[0]
    pos = ci * t + _iota(u.shape, 0)
    grp = _iota(u.shape, 1) // 64
    acc, pooled, k = u, jnp.zeros(u.shape, F32), 1
    for gi, w in enumerate(POOL_WINDOWS):
        while k < w:
            acc = acc + _shift16(cu, u, k)
            k += 1
        pooled = jnp.where(grp == gi, acc / jnp.minimum(pos + 1, w).astype(F32), pooled)
    y = _bdot(pooled - u, wbd, "nn") * scale
    return [_tail16(u)], [y]


_POOL_T = 256


def _pool_io(proj3, wbd, scale):
    return [Row(proj3, 256, fc=lambda b: 6, dcols=256, dfc=lambda b: 0, ddtype=BF16)], [Vec(wbd), Vec(scale)]


def pool_forward(name, proj3, wbd, scale):
    rows, vecs = _pool_io(proj3, wbd, scale)
    s = proj3.shape[1]
    (y,), saved = scan_fwd(name, _pool_fn, nb=1, nchunk=s // _POOL_T, t=_POOL_T, rows=rows, vecs=vecs,
                           carries=[(16, 256)], outs=[out_row((1, s, POOL_W), BF16)], save=True)
    return y, saved


def pool_backward(name, proj3, wbd, scale, saved, dmix3):
    rows, vecs = _pool_io(proj3, wbd, scale)
    s = proj3.shape[1]
    return scan_bwd(name, _pool_fn, nb=1, nchunk=s // _POOL_T, t=_POOL_T, rows=rows, vecs=vecs,
                    carries=[(16, 256)], saved=saved, douts=[Row(dmix3, 256, fc=lambda b: 2)])


def _attn_fn(ci, b, carries, rows, vecs):
    kp, vp = carries
    qr, kr, v = _thirds(rows[0])
    scale = ATT_HEAD_DIM ** -0.5
    q = qr
    n = q.shape[0]
    r, c = _iota((n, n), 0), _iota((n, n), 1)
    prev_ok, cur_ok = jnp.logical_and(c >= r, ci > 0), r >= c
    head = _iota(q.shape, 1) // ATT_HEAD_DIM
    o, lse = jnp.zeros(q.shape, F32), jnp.zeros(q.shape, F32)
    for h in range(ATT_HEADS):
        mine = head == h
        qh = jnp.where(mine, qr, 0.0)
        sp = jnp.where(prev_ok, _bdot(qh, kp, "nt") * scale, NEG)
        sc = jnp.where(cur_ok, _bdot(qh, kr, "nt") * scale, NEG)
        m = lax.stop_gradient(jnp.maximum(jnp.max(sp, axis=1, keepdims=True), jnp.max(sc, axis=1, keepdims=True)))
        pp, pc = jnp.exp(sp - m), jnp.exp(sc - m)
        l = jnp.sum(pp, axis=1, keepdims=True) + jnp.sum(pc, axis=1, keepdims=True)
        o = jnp.where(mine, (_bdot(pp, vp, "nn") + _bdot(pc, v, "nn")) / l, o)
        lse = jnp.where(mine, m + jnp.log(l), lse)
    return [kr, v], [o, lse]


_ATT_CARRIES = [(ATT_BLOCK, ATT_W), (ATT_BLOCK, ATT_W)]


def attn_forward(name, pv, d):
    l = pv.shape[1]
    own = lambda b: b
    outs = [out_row((1, l, d * ATT_W), F32, ATT_W, fc=own) for _ in range(2)]
    (o, lse), saved = scan_fwd(name, _attn_fn, nb=d, nchunk=l // ATT_BLOCK, t=ATT_BLOCK, rows=[Row(pv, 3 * ATT_W, fc=own)],
                               vecs=[], carries=_ATT_CARRIES, outs=outs, save=True)
    return o, lse, saved


def attn_backward(name, pv, d, saved, do, dlse):
    l = pv.shape[1]
    own = lambda b: b
    (dpv,), _ = scan_bwd(name, _attn_fn, nb=d, nchunk=l // ATT_BLOCK, t=ATT_BLOCK, rows=[Row(pv, 3 * ATT_W, fc=own)], vecs=[],
                         carries=_ATT_CARRIES, saved=saved, douts=[Row(do, ATT_W, fc=own), Row(dlse, ATT_W, fc=own)])
    return dpv


def _rope_fn(ci, b, carries, rows, vecs):
    x, cs, sn = rows
    return [], [x * cs + _rot_pairs(x) * sn]


def _rope3_fn(ci, b, carries, rows, vecs):
    _, (y,) = _rope_fn(ci, b, carries, rows, vecs)
    return [], [y, y, y]


def _by_residue(a_or_shape, w, d):
    if isinstance(a_or_shape, tuple):
        _, s, _ = a_or_shape
        return Row(jax.ShapeDtypeStruct((1, s // d, d * w), F32), w, view=None if d == 1 else d)
    return Row(a_or_shape, w, view=None if d == 1 else d)


def rope_forward(name, qkv3, cs3, sn3):
    s, w = qkv3.shape[1], qkv3.shape[2]
    ys, _ = scan_fwd(name, _rope3_fn, nb=1, nchunk=s // _ROW_T, t=_ROW_T, vecs=[], carries=[], save=False,
                     rows=[Row(qkv3), Row(cs3, diff=False), Row(sn3, diff=False)],
                     outs=[_by_residue(qkv3.shape, w, d) for _, d in ATT_PATTERNS])
    return ys


def rope_backward(name, qkv3, cs3, sn3, dys):
    s, w = qkv3.shape[1], qkv3.shape[2]
    (dx,), _ = scan_bwd(name, _rope3_fn, nb=1, nchunk=s // _ROW_T, t=_ROW_T, vecs=[], carries=[], saved=[],
                        rows=[Row(qkv3, ddtype=BF16), Row(cs3, diff=False), Row(sn3, diff=False)],
                        douts=[_by_residue(a, w, d) for a, (_, d) in zip(dys, ATT_PATTERNS)])
    return dx


def _merge_fn(ci, b, carries, rows, vecs):
    o1, o2, o3, l1, l2, l3 = rows
    mx = lax.stop_gradient(jnp.maximum(l1, jnp.maximum(l2, l3)))
    e1, e2, e3 = jnp.exp(l1 - mx), jnp.exp(l2 - mx), jnp.exp(l3 - mx)
    return [], [(e1 * o1 + e2 * o2 + e3 * o3) / (e1 + e2 + e3)]


_ROW_T = 256


def _merge_rows(os_, ls_):
    ds = [d for _, d in ATT_PATTERNS]
    return [_by_residue(a, ATT_W, d) for a, d in zip(os_, ds)] + [_by_residue(a, ATT_W, d) for a, d in zip(ls_, ds)]


def merge_forward(name, os_, ls_, s):
    (y,), _ = scan_fwd(name, _merge_fn, nb=1, nchunk=s // _ROW_T, t=_ROW_T, rows=_merge_rows(os_, ls_), vecs=[],
                       carries=[], outs=[out_row((1, s, ATT_W), BF16)], save=False)
    return y


def merge_backward(name, os_, ls_, dmix3):
    s = dmix3.shape[1]
    drows, _ = scan_bwd(name, _merge_fn, nb=1, nchunk=s // _ROW_T, t=_ROW_T, rows=_merge_rows(os_, ls_), vecs=[],
                        carries=[], saved=[], douts=[Row(dmix3, 256, fc=lambda b: 3)])
    return drows


def _norm_mod_fn(ci, b, carries, rows, vecs):
    (x,) = rows
    g, sc, sh = vecs
    xn = x * lax.rsqrt(jnp.mean(x * x, axis=-1, keepdims=True) + NORM_EPS)
    return [], [xn * g * (1.0 + sc) + sh]


def norm_mod_forward(name, x3, g, sc, sh):
    s = x3.shape[1]
    (h,), _ = scan_fwd(name, _norm_mod_fn, nb=1, nchunk=s // _ROW_T, t=_ROW_T, rows=[Row(x3)], vecs=[Vec(g), Vec(sc), Vec(sh)],
                       carries=[], outs=[out_row(x3.shape, BF16)], save=False)
    return h


def norm_mod_backward(name, x3, g, sc, sh, dh3, add3):
    s = x3.shape[1]
    (dx,), dv = scan_bwd(name, _norm_mod_fn, nb=1, nchunk=s // _ROW_T, t=_ROW_T, rows=[Row(x3)], vecs=[Vec(g), Vec(sc), Vec(sh)],
                         carries=[], saved=[], douts=[Row(dh3)], adds={0: Row(add3)})
    return dx, dv


def _gate_fn(ci, b, carries, rows, vecs):
    return [], [rows[0] * vecs[0]]


def gate_backward(name, o3, g, dx3):
    s = o3.shape[1]
    (do,), (dg,) = scan_bwd(name, _gate_fn, nb=1, nchunk=s // _ROW_T, t=_ROW_T, rows=[Row(o3, ddtype=BF16)], vecs=[Vec(g)],
                            carries=[], saved=[], douts=[Row(dx3)])
    return do, dg


def _make_halves():
    @jax.custom_vjp
    def halves(x):
        h = x.shape[1] // 2
        return x[:, :h], x[:, h:]

    def fwd(x):
        return halves(x), None

    def bwd(_, g):
        return (jnp.concatenate(g, axis=1),)

    halves.defvjp(fwd, bwd)
    return halves


_halves = _make_halves()


def _ffn_fn(ci, b, carries, rows, vecs):
    (cu,) = carries
    (u,) = rows
    w, bias = vecs
    hg, hu = _halves(_conv(_shift8, cu, u, w, bias, 3))
    return [_tail8(u)], [_silu(hg) * hu]


_FFN_T = 256
_FFN_CW = FFN_DIM // 2
_FFN_CARRIES = [(8, 2 * _FFN_CW)]
FFN_BLOCK_ORDER = [0, 2, 1, 3]


def _ffn_io(up3, cw, cb):
    own = lambda b: b
    return [Row(up3, 2 * _FFN_CW, fc=own, ddtype=BF16)], [Vec(cw, 2 * _FFN_CW, own), Vec(cb, 2 * _FFN_CW, own)]


def ffn_mid_forward(name, up3, cw, cb):
    rows, vecs = _ffn_io(up3, cw, cb)
    s = up3.shape[1]
    (act,), saved = scan_fwd(name, _ffn_fn, nb=2, nchunk=s // _FFN_T, t=_FFN_T, rows=rows, vecs=vecs, carries=_FFN_CARRIES,
                             outs=[out_row((1, s, FFN_DIM), BF16, _FFN_CW, fc=lambda b: b)], save=True)
    return act, saved


def ffn_mid_backward(name, up3, cw, cb, saved, dact3):
    rows, vecs = _ffn_io(up3, cw, cb)
    s = up3.shape[1]
    return scan_bwd(name, _ffn_fn, nb=2, nchunk=s // _FFN_T, t=_FFN_T, rows=rows, vecs=vecs, carries=_FFN_CARRIES,
                    saved=saved, douts=[Row(dact3, _FFN_CW, fc=lambda b: b)])


def _adam_fn(ci, b, carries, rows, vecs):
    w, g, m, v = rows
    m = ADAM_B1 * m + (1.0 - ADAM_B1) * g
    v = ADAM_B2 * v + (1.0 - ADAM_B2) * (g * g)
    m_hat = m / (1.0 - ADAM_B1 ** ADAM_STEP)
    v_hat = v / (1.0 - ADAM_B2 ** ADAM_STEP)
    delta = -ADAM_LR * (m_hat / (jnp.sqrt(v_hat) + ADAM_EPS) + ADAM_WD * w)
    return [], [delta, m, v]


def adamw(name, w, g, m, v):
    shape = w.shape
    c = shape[-1]
    r = int(np.prod(shape[:-1]))
    t = _tile(r, 256, 8)
    as3 = lambda a: a.reshape(1, r, c)
    outs, _ = scan_fwd(name, _adam_fn, nb=1, nchunk=r // t, t=t, rows=[Row(as3(a)) for a in (w, g, m, v)], vecs=[], carries=[],
                       outs=[out_row((1, r, c)) for _ in range(3)], save=False)
    return [o.reshape(shape) for o in outs]


def rope_tables(positions):
    inv_freq = ROPE_THETA ** (-jnp.arange(0, ROT_DIM, 2, dtype=F32) / ROT_DIM)
    ang = positions.astype(F32)[:, None] * inv_freq
    s = positions.shape[0]
    cs = jnp.concatenate([jnp.cos(ang), jnp.cos(ang), jnp.ones((s, ATT_HEAD_DIM - ROT_DIM), F32)], axis=1)
    sn = jnp.concatenate([jnp.sin(ang), jnp.sin(ang), jnp.zeros((s, ATT_HEAD_DIM - ROT_DIM), F32)], axis=1)
    cs3 = jnp.concatenate([jnp.tile(cs, (1, 2 * ATT_HEADS)), jnp.ones((s, ATT_W), F32)], axis=1)
    sn3 = jnp.concatenate([jnp.tile(sn, (1, 2 * ATT_HEADS)), jnp.zeros((s, ATT_W), F32)], axis=1)
    return cs3[None], sn3[None]


def attention_forward(lname, qkv3, cs3, sn3):
    s = qkv3.shape[1]
    rotated = rope_forward(f"{lname}_rope", qkv3, cs3, sn3)
    os_, ls_, keep = [], [], []
    for pi, (_, d) in enumerate(ATT_PATTERNS):
        o, lse, saved = attn_forward(f"{lname}_attn{pi}", rotated[pi], d)
        os_.append(o)
        ls_.append(lse)
        keep.append(saved)
    y = merge_forward(f"{lname}_merge", os_, ls_, s)
    return y, (rotated, os_, ls_, keep)


def attention_backward(lname, qkv3, cs3, sn3, res, dmix3):
    rotated, os_, ls_, keep = res
    dm = merge_backward(f"{lname}_merge_b", os_, ls_, dmix3)
    dys = [attn_backward(f"{lname}_attn{pi}_b", rotated[pi], d, keep[pi], dm[pi], dm[3 + pi]) for pi, (_, d) in enumerate(ATT_PATTERNS)]
    return rope_backward(f"{lname}_rope_b", qkv3, cs3, sn3, dys)


def mm(name, a, b, mode, out_dtype=F32, res=None, gate=None, tm=1408, tn=1536, tk=1408, into=None):
    if mode == "nn":
        (m, k), n = a.shape, b.shape[1]
    elif mode == "nt":
        (m, k), n = a.shape, b.shape[0]
    else:
        (k, m), n = a.shape, b.shape[1]
    tm, tn, tk = _tile(m, tm), _tile(n, tn), _tile(k, tk)
    nk = k // tk
    a_spec = pl.BlockSpec((tk, tm), lambda i, j, q: (q, i)) if mode == "tn" else pl.BlockSpec((tm, tk), lambda i, j, q: (i, q))
    b_spec = pl.BlockSpec((tn, tk), lambda i, j, q: (j, q)) if mode == "nt" else pl.BlockSpec((tk, tn), lambda i, j, q: (q, j))
    o_spec = pl.BlockSpec((tm, tn), lambda i, j, q: (i, j))
    fused = res is not None
    lead = 0 if into is None else into[0].ndim - 2
    first = (0,) * lead + (slice(None), slice(None))

    def body(*refs):
        if fused:
            a_ref, b_ref, r_ref, g_ref, o_ref, o2_ref, acc = refs
        elif into is not None:
            a_ref, b_ref, _, o_ref, acc = refs
        else:
            a_ref, b_ref, o_ref, acc = refs
        q = pl.program_id(2)

        @pl.when(q == 0)
        def _():
            acc[...] = jnp.zeros(acc.shape, F32)

        acc[...] += _mxu(a_ref[...], b_ref[...], mode)

        @pl.when(q == nk - 1)
        def _():
            o_ref[first] = acc[...].astype(o_ref.dtype)
            if fused:
                o2_ref[...] = r_ref[...] + g_ref[...] * acc[...]

    ins, in_specs = [a, b], [a_spec, b_spec]
    out_shape, out_specs = [jax.ShapeDtypeStruct((m, n), out_dtype)], [o_spec]
    if fused:
        ins += [res, gate]
        in_specs += [o_spec, pl.BlockSpec((1, tn), lambda i, j, q: (0, j))]
        out_shape.append(jax.ShapeDtypeStruct((m, n), F32))
        out_specs.append(o_spec)
    aliases = {}
    if into is not None:
        buf, omap = into
        ins.append(buf)
        in_specs.append(pl.BlockSpec(memory_space=pl.ANY))
        out_shape = [jax.ShapeDtypeStruct(buf.shape, buf.dtype)]
        out_specs = [pl.BlockSpec((1,) * lead + (tm, tn), lambda i, j, q: omap(i, j))]
        aliases = {2: 0}
    out = pl.pallas_call(
        body, name=name, grid=(m // tm, n // tn, nk), in_specs=in_specs, out_specs=out_specs, out_shape=out_shape,
        scratch_shapes=[pltpu.VMEM((tm, tn), F32)], input_output_aliases=aliases,
        compiler_params=pltpu.CompilerParams(dimension_semantics=("parallel", "parallel", "arbitrary"),
                                             vmem_limit_bytes=VMEM_LIMIT_BYTES),
    )(*ins)
    return tuple(out) if fused else out[0]


def final_loss(name, x3, t3, g):
    s, d = x3.shape[1], x3.shape[2]
    t = _ROW_T

    def body(x_ref, t_ref, g_ref, loss_ref, dx_ref, dg_ref):
        i = pl.program_id(0)
        tv = t_ref[0]

        def f(x, gg):
            y = x * lax.rsqrt(jnp.mean(x * x, axis=-1, keepdims=True) + NORM_EPS) * gg
            e = y - tv
            return 0.5 * jnp.sum(jnp.mean(e * e, axis=-1, keepdims=True), axis=0, keepdims=True)

        l, vjp = jax.vjp(f, x_ref[0], g_ref[...])
        dx, dg = vjp(jnp.ones((1, 1), F32))
        dx_ref[0] = dx

        @pl.when(i == 0)
        def _():
            loss_ref[...] = jnp.zeros(loss_ref.shape, F32)
            dg_ref[...] = jnp.zeros(dg_ref.shape, F32)

        loss_ref[...] += jnp.broadcast_to(l, loss_ref.shape)
        dg_ref[...] += dg

    row = pl.BlockSpec((1, t, d), lambda i: (0, i, 0))
    vec = pl.BlockSpec((1, d), lambda i: (0, 0))
    return pl.pallas_call(
        body, name=name, grid=(s // t,), in_specs=[row, row, vec],
        out_specs=[pl.BlockSpec((8, 128), lambda i: (0, 0)), row, vec],
        out_shape=[jax.ShapeDtypeStruct((8, 128), F32), jax.ShapeDtypeStruct(x3.shape, F32), jax.ShapeDtypeStruct((1, d), F32)],
        compiler_params=pltpu.CompilerParams(dimension_semantics=("arbitrary",), vmem_limit_bytes=VMEM_LIMIT_BYTES),
    )(x3, t3, g)


_ADA_TN = 512


def ada_forward(name, c16, ada_w):
    depth, d, cols = ada_w.shape

    def body(c_ref, w_ref, o_ref):
        o_ref[0] = _mxu(_silu(c_ref[...]), w_ref[0], "nn")

    return pl.pallas_call(
        body, name=name, grid=(depth, cols // _ADA_TN),
        in_specs=[pl.BlockSpec((16, d), lambda l, j: (0, 0)), pl.BlockSpec((1, d, _ADA_TN), lambda l, j: (l, 0, j))],
        out_specs=pl.BlockSpec((1, 16, _ADA_TN), lambda l, j: (l, 0, j)),
        out_shape=jax.ShapeDtypeStruct((depth, 16, cols), F32),
        compiler_params=pltpu.CompilerParams(dimension_semantics=("arbitrary", "arbitrary"), vmem_limit_bytes=VMEM_LIMIT_BYTES),
    )(c16, ada_w)


def ada_backward(name, c16, dmod16, w, m, v):
    depth, d, cols = w.shape

    def body(c_ref, dm_ref, w_ref, m_ref, v_ref, g_ref, dl_ref, nm_ref, nv_ref):
        g = _mxu(_silu(c_ref[...]), dm_ref[0], "tn")
        _, (delta, nm, nv) = _adam_fn(None, None, [], [w_ref[0], g, m_ref[0], v_ref[0]], [])
        g_ref[0], dl_ref[0], nm_ref[0], nv_ref[0] = g, delta, nm, nv

    blk = pl.BlockSpec((1, d, _ADA_TN), lambda l, j: (l, 0, j))
    return pl.pallas_call(
        body, name=name, grid=(depth, cols // _ADA_TN),
        in_specs=[pl.BlockSpec((16, d), lambda l, j: (0, 0)), pl.BlockSpec((1, 16, _ADA_TN), lambda l, j: (l, 0, j)), blk, blk, blk],
        out_specs=[blk] * 4, out_shape=[jax.ShapeDtypeStruct(w.shape, F32)] * 4,
        compiler_params=pltpu.CompilerParams(dimension_semantics=("arbitrary", "arbitrary"), vmem_limit_bytes=VMEM_LIMIT_BYTES),
    )(c16, dmod16, w, m, v)


def _sum_fn(ci, b, carries, rows, vecs):
    acc = rows[0]
    for r in rows[1:]:
        acc = acc + r
    return [], [acc]


def sum_slots(name, a, nsum, out_dtype=F32):
    n, r, c = a.shape
    nb = n // nsum
    t = _tile(r, 256, 8)
    rows = [Row(a, fb=(lambda b, k=k: k * nb + b)) for k in range(nsum)]
    (out,), _ = scan_fwd(name, _sum_fn, nb=nb, nchunk=r // t, t=t, rows=rows, vecs=[], carries=[],
                         outs=[out_row((nb, r, c), out_dtype, fb=lambda b: b)], save=False)
    return out


def _sum_my_layer_fn(ci, b, carries, rows, vecs):
    layer0, layer1, theirs = rows
    return [], [jnp.where(lax.axis_index("c") == 0, layer0, layer1) + theirs]


def sum_cores(name, g, theirs, out_dtype):
    _, nb, r, c = g.shape
    g8 = g.reshape(2 * nb, r, c)
    t = _tile(r, 256, 8)
    rows = [Row(g8, fb=lambda b: b), Row(g8, fb=lambda b: nb + b), Row(theirs, fb=lambda b: b)]
    (out,), _ = scan_fwd(name, _sum_my_layer_fn, nb=nb, nchunk=r // t, t=t, rows=rows, vecs=[], carries=[],
                         outs=[out_row((nb, r, c), out_dtype, fb=lambda b: b)], save=False)
    return out


def _flip(mask, pos):
    return tuple((1 - p) if m else p for m, p in zip(mask, pos))


ALL_PEERS = [(a, b, c) for a in (0, 1) for b in (0, 1) for c in (0, 1)][1:]
CHIP_PEERS = [(1, 0, 0), (0, 1, 0), (1, 1, 0)]
SIBLING = [(0, 0, 1)]


def _divisor(size, target, unit):
    best = 1
    for n in range(1, target + 1):
        if size % n == 0 and (size // n) % unit == 0:
            best = n
    return best


def _pieces(src, dst, pieces):
    shape = src.shape
    unit = 16 if src.dtype == BF16 else 8
    if pieces <= 1:
        return [(src, dst)]
    if len(shape) == 2:
        n = _divisor(shape[0], pieces, unit)
        s = shape[0] // n
        return [(src.at[pl.ds(i * s, s)], dst.at[pl.ds(i * s, s)]) for i in range(n)]
    assert len(shape) == 3, shape
    n = _divisor(shape[1], max(pieces // shape[0], 1), unit)
    s = shape[1] // n
    return [(src.at[j, pl.ds(i * s, s)], dst.at[j, pl.ds(i * s, s)]) for j in range(shape[0]) for i in range(n)]


def comm_call(name, arrays, out_shapes, masks, src_fn, dst_fn, local_fn=None, pieces=1):
    na, npeer = len(arrays), len(masks)

    def body(*refs):
        ins, outs = refs[:na], refs[na:2 * na]
        send_sems, recv_sems, loc_sems = refs[2 * na:]
        me = (lax.axis_index("x"), lax.axis_index("y"), lax.axis_index("c"))
        local = []
        if local_fn is not None:
            for k in range(na):
                s, d = local_fn(k, ins[k], outs[k], me)
                for ps, pd in _pieces(s, d, pieces):
                    pltpu.make_async_copy(ps, pd, loc_sems.at[k]).start()
                local.append(pltpu.make_async_copy(s, d, loc_sems.at[k]))

        def remote(k, p, src, dst, to):
            return pltpu.make_async_remote_copy(
                src_ref=src, dst_ref=dst, send_sem=send_sems.at[k * npeer + p], recv_sem=recv_sems.at[k * npeer + p],
                device_id=to, device_id_type=MESH)

        for k in range(na):
            for p in range(npeer):
                peer = _flip(masks[p], me)
                for ps, pd in _pieces(src_fn(k, ins[k], me, peer), dst_fn(k, outs[k], me), pieces):
                    remote(k, p, ps, pd, peer).start()
        for k in range(na):
            for p in range(npeer):
                peer = _flip(masks[p], me)
                remote(k, p, src_fn(k, ins[k], me, peer), dst_fn(k, outs[k], peer), peer).wait_recv()
        for k in range(na):
            for p in range(npeer):
                peer = _flip(masks[p], me)
                remote(k, p, src_fn(k, ins[k], me, peer), dst_fn(k, outs[k], me), peer).wait_send()
        for cp in local:
            cp.wait()

    hbm = pl.BlockSpec(memory_space=pl.ANY)
    out = pl.pallas_call(
        body, name=name, in_specs=[hbm] * na, out_specs=[hbm] * na,
        out_shape=[jax.ShapeDtypeStruct(s, a.dtype) for s, a in zip(out_shapes, arrays)],
        scratch_shapes=[pltpu.SemaphoreType.DMA((na * npeer,)), pltpu.SemaphoreType.DMA((na * npeer,)),
                        pltpu.SemaphoreType.DMA((na,))],
    )(*arrays)
    return list(out)


def _dev(pos):
    return 4 * pos[0] + 2 * pos[1] + pos[2]


def _chip(pos):
    return 2 * pos[0] + pos[1]


def allgather8(name, a):
    (out,) = comm_call(name, [a], [(8,) + a.shape], ALL_PEERS,
                       src_fn=lambda k, r, me, peer: r, dst_fn=lambda k, o, sender: o.at[_dev(sender)],
                       local_fn=lambda k, r, o, me: (r, o.at[_dev(me)]))
    return out


def gather_layer_from_chips(name, arrays):
    return comm_call(name, arrays, [(4,) + a.shape[1:] for a in arrays], CHIP_PEERS,
                     src_fn=lambda k, r, me, peer: r.at[me[2]], dst_fn=lambda k, o, sender: o.at[_chip(sender)],
                     local_fn=lambda k, r, o, me: (r.at[me[2]], o.at[_chip(me)]), pieces=8)


def swap_layers(name, arrays, c):
    got = comm_call(name, arrays, [a.shape for a in arrays], SIBLING,
                    src_fn=lambda k, r, me, peer: r, dst_fn=lambda k, o, sender: o, pieces=32)
    return [[jnp.where(c == 0, a, g), jnp.where(c == 0, g, a)] for a, g in zip(arrays, got)]


def swap_other_layer(name, arrays):
    return comm_call(name, arrays, [a.shape[1:] for a in arrays], SIBLING,
                     src_fn=lambda k, r, me, peer: r.at[peer[2]], dst_fn=lambda k, o, sender: o, pieces=32)


def scatter_to_chips(name, arrays):
    return comm_call(name, arrays, [a.shape for a in arrays], CHIP_PEERS,
                     src_fn=lambda k, r, me, peer: r.at[_chip(peer)], dst_fn=lambda k, o, sender: o.at[_chip(sender)],
                     local_fn=lambda k, r, o, me: (r.at[_chip(me)], o.at[_chip(me)]), pieces=8)


def _rows_of(shape):
    return -(-int(np.prod(shape)) // 128)


def _pack(arrs):
    parts = []
    for a in arrs:
        flat = a.reshape(-1).astype(F32)
        parts.append(jnp.pad(flat, (0, _rows_of(a.shape) * 128 - flat.shape[0])).reshape(-1, 128))
    rows = sum(p.shape[0] for p in parts)
    parts.append(jnp.zeros(((-rows) % _ROW_T, 128), F32))
    return jnp.concatenate(parts, axis=0)


def _unpack(buf, shapes):
    out, o = [], 0
    for s in shapes:
        r, n = _rows_of(s), int(np.prod(s))
        out.append(buf[o:o + r].reshape(-1)[:n].reshape(s))
        o += r
    return out


_WEIGHTS = ["ada_w", "ada_b", "norm1_g", "w_in", "ssd_conv_w", "ssd_conv_b", "ssd_dt_bias", "ssd_a_log", "ssd_d", "ssd_norm_g",
            "pool_w", "pool_scale", "w_out", "norm2_g", "ffn_up", "ffn_conv_w", "ffn_conv_b", "ffn_down", "final_g"]
_BIG = ["w_in", "w_out", "ffn_up", "ffn_down"]
_SMALL = [n for n in _WEIGHTS if n not in _BIG and n != "ada_w"]
_COL_SHARDED_SMALL = {"ssd_conv_w": 256, "ffn_conv_w": 1408}


def _pad_lanes(v, n=128):
    return jnp.pad(v.astype(F32), (0, n - v.shape[0]))[None]


def _perm_cols(w):
    pad = jnp.zeros(w.shape[:-1] + (IN_WP - IN_W,), w.dtype)
    return jnp.concatenate([w[..., :1536], w[..., 1544:1800], w[..., 1536:1544], pad, w[..., 1800:]], axis=-1)


def _unperm_cols(g):
    return jnp.concatenate([g[..., :1536], g[..., 1792:1800], g[..., 1536:1792], g[..., IN_MAIN:]], axis=-1)


_CHIP2_PARTS = [(1284, 1536), (1792, 1800), (1536, 1792), (IN_MAIN, IN_MAIN + 126)]


def _w_in_chip_cols(gp):
    q = IN_W // 4
    return [gp[:, :q], gp[:, q:2 * q], jnp.concatenate([gp[:, a:b] for a, b in _CHIP2_PARTS], axis=1), gp[:, IN_WP - q:]]


def _w_in_from_chips(a):
    c2 = a[2]
    pad = jnp.zeros((a.shape[1], IN_WP - IN_W), a.dtype)
    return jnp.concatenate([a[0], a[1], c2[:, :252], c2[:, 260:516], c2[:, 252:260], pad, c2[:, 516:], a[3]], axis=1)


def _ffn_block_perm(a):
    n = a.shape[-1] // 4
    return jnp.concatenate([a[..., j * n:(j + 1) * n] for j in FFN_BLOCK_ORDER], axis=-1)


def _layer_forward(i, x3, modv, wts, sp, cs3, sn3):
    sh1, sc1, g1, sh2, sc2, g2 = modv
    h1 = norm_mod_forward(f"l{i}_norm1", x3, wts["norm1_g"], sc1, sh1)
    proj3 = mm(f"l{i}_proj", h1[0], wts["w_in"][:, :IN_MAIN], "nn")[None]
    qkv3 = mm(f"l{i}_qkv", h1[0], wts["w_in"][:, IN_MAIN:], "nn")[None]
    y_ssd, sv_ssd = ssd_forward(f"l{i}_ssd", proj3, sp)
    y_pool, sv_pool = pool_forward(f"l{i}_pool", proj3, wts["wbd"], wts["pool_scale"])
    y_att, res_att = attention_forward(f"l{i}", qkv3, cs3, sn3)
    mix = jnp.concatenate([y_ssd, y_pool, y_att], axis=-1)
    out, x1 = mm(f"l{i}_wout", mix[0], wts["w_out"], "nn", res=x3[0], gate=g1)
    x1 = x1[None]
    h2 = norm_mod_forward(f"l{i}_norm2", x1, wts["norm2_g"], sc2, sh2)
    up3 = mm(f"l{i}_up", h2[0], wts["ffn_up"], "nn")[None]
    act, sv_ffn = ffn_mid_forward(f"l{i}_ffn", up3, wts["ffn_conv_w"], wts["ffn_conv_b"])
    dn, x2 = mm(f"l{i}_down", act[0], wts["ffn_down"], "nn", res=x1[0], gate=g2)
    keep = dict(x=x3, h1=h1, proj3=proj3, qkv3=qkv3, sv_ssd=sv_ssd, sv_pool=sv_pool, res_att=res_att, mix=mix, out=out[None],
                x1=x1, h2=h2, up3=up3, act=act, sv_ffn=sv_ffn, dn=dn[None])
    return x2[None], keep


def _layer_backward(i, dx2, keep, modv, wts, sp, cs3, sn3, gbuf):
    sh1, sc1, g1, sh2, sc2, g2 = modv
    k = keep
    d_dn, d_g2 = gate_backward(f"l{i}_gate2_b", k["dn"], g2, dx2)
    d_act = mm(f"l{i}_down_bx", d_dn[0], wts["ffn_down"], "nt")
    g_down = mm(f"l{i}_down_bw", k["act"][0], d_dn[0], "tn", into=(gbuf["ffn_down"], lambda r, c: (i, r, c)))
    (d_up,), dv_ffn = ffn_mid_backward(f"l{i}_ffn_b", k["up3"], wts["ffn_conv_w"], wts["ffn_conv_b"], k["sv_ffn"], d_act[None])
    d_h2 = mm(f"l{i}_up_bx", d_up[0], wts["ffn_up"], "nt")
    g_up = mm(f"l{i}_up_bw", k["h2"][0], d_up[0], "tn", tn=_FFN_CW,
              into=(gbuf["ffn_up"], lambda r, c: (i, (c % 2) * 2 + c // 2, r, 0)))
    dx1, (d_n2, d_sc2, d_sh2) = norm_mod_backward(f"l{i}_norm2_b", k["x1"], wts["norm2_g"], sc2, sh2, d_h2[None], dx2)
    d_out, d_g1 = gate_backward(f"l{i}_gate1_b", k["out"], g1, dx1)
    d_mix = mm(f"l{i}_wout_bx", d_out[0], wts["w_out"], "nt")[None]
    g_wout = mm(f"l{i}_wout_bw", k["mix"][0], d_out[0], "tn", into=(gbuf["w_out"], lambda r, c: (i, r, c)))
    (dz, dxs, dbm, dcm, ddt), dv_ssd = ssd_backward(f"l{i}_ssd_b", k["proj3"], sp, k["sv_ssd"], d_mix)
    (du_pool,), (d_wbd, d_pscale) = pool_backward(f"l{i}_pool_b", k["proj3"], wts["wbd"], wts["pool_scale"], k["sv_pool"], d_mix)
    d_qkv = attention_backward(f"l{i}", k["qkv3"], cs3, sn3, k["res_att"], d_mix)
    d_proj = jnp.concatenate([dz[0], dxs[0], dbm[0], dcm[0], du_pool[0], (ddt[0] + ddt[1]).astype(BF16), d_qkv[0]], axis=-1)
    d_h1 = mm(f"l{i}_proj_bx", d_proj, wts["w_in"], "nt")
    g_win = mm(f"l{i}_proj_bw", k["h1"][0], d_proj, "tn")
    dx, (d_n1, d_sc1, d_sh1) = norm_mod_backward(f"l{i}_norm1_b", k["x"], wts["norm1_g"], sc1, sh1, d_h1[None], dx1)
    dcwx, dcbx, dcwb, dcbb, dcwc, dcbc, ddtb, dalog, ddsk, dng = dv_ssd
    small = dict(
        norm1_g=d_n1[0], norm2_g=d_n2[0],
        ssd_conv_w=jnp.concatenate([dcwx[:, :512], dcwb[:, 512:768], dcwc[:, 768:]], axis=1),
        ssd_conv_b=jnp.concatenate([dcbx[0, :512], dcbb[0, 512:768], dcbc[0, 768:]]),
        ssd_dt_bias=ddtb[0, :8], ssd_a_log=dalog[0, :8], ssd_d=ddsk[0, :8], ssd_norm_g=dng[0],
        pool_w=jnp.stack([d_wbd[64 * g:64 * g + 64, 64 * g:64 * g + 64] for g in range(4)]), pool_scale=d_pscale[0],
        ffn_conv_w=_ffn_block_perm(dv_ffn[0]), ffn_conv_b=_ffn_block_perm(dv_ffn[1][0]),
    )
    dmod = jnp.concatenate([d_sh1[0], d_sc1[0], d_g1[0], d_sh2[0], d_sc2[0], d_g2[0]])
    return dx, g_win, dict(w_out=g_wout, ffn_up=g_up, ffn_down=g_down), small, dmod


def kernel(x, c, positions, ada_w, ada_b, norm1_g, w_in, ssd_conv_w, ssd_conv_b, ssd_dt_bias, ssd_a_log, ssd_d, ssd_norm_g, pool_w, pool_scale, w_out, norm2_g, ffn_up, ffn_conv_w, ffn_conv_b, ffn_down, final_g, loss_target, m_ada_w, m_ada_b, m_norm1_g, m_w_in, m_ssd_conv_w, m_ssd_conv_b, m_ssd_dt_bias, m_ssd_a_log, m_ssd_d, m_ssd_norm_g, m_pool_w, m_pool_scale, m_w_out, m_norm2_g, m_ffn_up, m_ffn_conv_w, m_ffn_conv_b, m_ffn_down, m_final_g, v_ada_w, v_ada_b, v_norm1_g, v_w_in, v_ssd_conv_w, v_ssd_conv_b, v_ssd_dt_bias, v_ssd_a_log, v_ssd_d, v_ssd_norm_g, v_pool_w, v_pool_scale, v_w_out, v_norm2_g, v_ffn_up, v_ffn_conv_w, v_ffn_conv_b, v_ffn_down, v_final_g):
    args = dict(locals())
    w = {n: args[n] for n in _WEIGHTS}
    m = {n: args["m_" + n] for n in _WEIGHTS}
    v = {n: args["v_" + n] for n in _WEIGHTS}
    d = D_MODEL
    me = (lax.axis_index("x"), lax.axis_index("y"), lax.axis_index("c"))
    chip, dev = _chip(me), _dev(me)

    shapes0 = [c.shape, ssd_conv_w.shape, ffn_conv_w.shape]
    g0 = allgather8("gather_c_conv", _pack([c, ssd_conv_w, ffn_conv_w]))
    c16 = jnp.pad(g0[:, :d // 128, :].reshape(8, d), ((0, 8), (0, 0)))
    by_chip = [_unpack(g0[2 * j], shapes0) for j in range(4)]
    conv_w_full = jnp.concatenate([p[1] for p in by_chip], axis=-1)
    fconv_w_full = jnp.concatenate([p[2] for p in by_chip], axis=-1)

    modp = ada_forward("ada_fwd", c16, ada_w)[:, :8]
    g1 = allgather8("gather_mod", _pack([modp]))
    modfull = jnp.concatenate([_unpack(g1[2 * j], [modp.shape])[0] for j in range(4)], axis=-1)
    mod = lax.dynamic_index_in_dim(modfull, dev, axis=1, keepdims=False) + ada_b
    modv = [[mod[i, q * d:(q + 1) * d][None] for q in range(6)] for i in range(DEPTH)]

    got = gather_layer_from_chips("gather_w", [w[n].astype(BF16) for n in _BIG])
    both = swap_layers("swap_w", got, me[2])
    full = dict(
        w_in=[_w_in_from_chips(a) for a in both[0]],
        w_out=[a.reshape(d, d) for a in both[1]],
        ffn_up=[jnp.concatenate([a[j] for j in FFN_BLOCK_ORDER], axis=1) for a in both[2]],
        ffn_down=[a.reshape(FFN_DIM, d) for a in both[3]],
    )

    cs3, sn3 = rope_tables(positions[0])
    eye4 = jnp.eye(4, dtype=F32)
    wts, sps = [], []
    for i in range(DEPTH):
        wts.append(dict(
            w_in=full["w_in"][i], w_out=full["w_out"][i], ffn_up=full["ffn_up"][i], ffn_down=full["ffn_down"][i],
            norm1_g=norm1_g[i][None], norm2_g=norm2_g[i][None], pool_scale=pool_scale[i][None],
            wbd=(eye4[:, None, :, None] * pool_w[i][:, :, None, :]).reshape(POOL_W, POOL_W),
            ffn_conv_w=_ffn_block_perm(fconv_w_full[i]), ffn_conv_b=_ffn_block_perm(ffn_conv_b[i])[None]))
        sps.append(dict(cw=conv_w_full[i], cb=ssd_conv_b[i][None], dtb=_pad_lanes(ssd_dt_bias[i]), alog=_pad_lanes(ssd_a_log[i]),
                        dsk=_pad_lanes(ssd_d[i]), ng=ssd_norm_g[i][None]))

    xc, keeps = x, []
    for i in range(DEPTH):
        xc, keep = _layer_forward(i, xc, modv[i], wts[i], sps[i], cs3, sn3)
        keeps.append(keep)
    lossblk, dx, d_final = final_loss("final_loss", xc, loss_target, final_g[None])
    loss = lax.psum(lossblk[0, 0], ("x", "y", "c"))

    g_win, small_g, dmods = [None] * DEPTH, [None] * DEPTH, [None] * DEPTH
    gbuf = dict(w_out=lax.empty((DEPTH, d, d), F32), ffn_up=lax.empty((DEPTH, 4, d, 2 * FFN_DIM // 4), F32),
                ffn_down=lax.empty((DEPTH, FFN_DIM, d), F32))
    for i in reversed(range(DEPTH)):
        dx, g_win[i], gbuf, small_g[i], dmods[i] = _layer_backward(i, dx, keeps[i], modv[i], wts[i], sps[i], cs3, sn3, gbuf)

    by_dest = [
        jnp.stack([jnp.stack(_w_in_chip_cols(g_win[i])) for i in range(DEPTH)]),
        gbuf["w_out"].reshape(DEPTH, 4, d // 4, d),
        gbuf["ffn_up"],
        gbuf["ffn_down"].reshape(DEPTH, 4, FFN_DIM // 4, d),
    ]
    theirs = swap_other_layer("swap_g", by_dest)
    core_sum = [sum_cores(f"sum_cores_{n}", g, t, BF16) for n, g, t in zip(_BIG, by_dest, theirs)]
    from_chips = scatter_to_chips("scatter_g", core_sum)
    chip_sum = [sum_slots(f"sum_chips_{n}", q, 4)[0] for n, q in zip(_BIG, from_chips)]
    reduced = swap_layers("swap_r", chip_sum, me[2])
    grads = {n: jnp.stack(r) for n, r in zip(_BIG, reduced)}

    part = dict(ada_b=jnp.stack(dmods), final_g=d_final[0])
    for n in _SMALL:
        if n not in part:
            part[n] = jnp.stack([small_g[i][n] for i in range(DEPTH)])
    full_shapes = [part[n].shape for n in _SMALL]
    gs = allgather8("gather_small", _pack([part[n] for n in _SMALL]))
    tot = _unpack(sum_slots("sum_small", gs, 8)[0], full_shapes)
    small_tot = dict(zip(_SMALL, tot))
    dmod_all = gs[:, :DEPTH * 6 * d // 128, :].reshape(8, DEPTH, 6 * d)
    for n, ncol in _COL_SHARDED_SMALL.items():
        small_tot[n] = lax.dynamic_slice_in_dim(small_tot[n], chip * ncol, ncol, axis=2)
    grads.update(small_tot)

    ncol = ada_w.shape[2]
    dm = lax.dynamic_slice_in_dim(dmod_all, chip * ncol, ncol, axis=2).transpose(1, 0, 2)
    upd = {}
    g_ada, *upd["ada_w"] = ada_backward("ada_bwd", c16, jnp.pad(dm, ((0, 0), (0, 8), (0, 0))), ada_w, m["ada_w"], v["ada_w"])
    grads["ada_w"] = g_ada

    for n in _BIG:
        upd[n] = adamw(f"adam_{n}", w[n], grads[n], m[n], v[n])
    shapes_s = [w[n].shape for n in _SMALL]
    packed = [_pack([src[n] for n in _SMALL]) for src in (w, grads, m, v)]
    outs_s = [_unpack(o, shapes_s) for o in adamw("adam_small", *packed)]
    for q, n in enumerate(_SMALL):
        upd[n] = [outs_s[0][q], outs_s[1][q], outs_s[2][q]]

    return (loss, dx, *[grads[n] for n in _WEIGHTS], *[upd[n][0] for n in _WEIGHTS], *[upd[n][1] for n in _WEIGHTS],
            *[upd[n][2] for n in _WEIGHTS])
```

```python
import functools
import math

import numpy as np
import jax
import jax.numpy as jnp
from jax import lax
from jax.experimental import pallas as pl
from jax.experimental.pallas import tpu as pltpu

F32 = jnp.float32
BF16 = jnp.bfloat16
HI = lax.Precision.HIGHEST
MESH = pl.DeviceIdType.MESH

D_MODEL = 1024
SEQ = 4096
DEPTH = 2
SSD_INNER = 512
SSD_HEADS = 8
SSD_STATE = 128
POOL_W = 256
POOL_WINDOWS = (2, 4, 8, 16)
ATT_W = 256
ATT_HEADS = 4
ATT_HEAD_DIM = 64
ATT_PATTERNS = ((128, 1), (512, 4), (2048, 16))
ATT_BLOCK = 128
ROT_DIM = 16
ROPE_THETA = 500000.0
IN_W = 2568
IN_WP = 2688
IN_MAIN = 1920
FFN_DIM = 2816
NORM_EPS = 1e-6
ADAM_LR, ADAM_B1, ADAM_B2, ADAM_EPS, ADAM_WD, ADAM_STEP = 0.001, 0.9, 0.999, 1e-08, 0.01, 10

VMEM_LIMIT_BYTES = 56 * 1024 * 1024
NEG = -1e30


def _mxu(a, b, mode):
    dims = {"nn": ((1,), (0,)), "nt": ((1,), (1,)), "tn": ((0,), (0,))}[mode]
    return lax.dot_general(a.astype(BF16), b.astype(BF16), (dims, ((), ())), preferred_element_type=F32)


@functools.partial(jax.custom_vjp, nondiff_argnums=(2,))
def _bdot(a, b, mode):
    return _mxu(a, b, mode)


def _bdot_fwd(a, b, mode):
    return _mxu(a, b, mode), (a, b)


def _bdot_bwd(mode, res, g):
    a, b = res
    if mode == "nn":
        return _mxu(g, b, "nt"), _mxu(a, g, "tn")
    if mode == "nt":
        return _mxu(g, b, "nn"), _mxu(g, a, "tn")
    return _mxu(b, g, "nt"), _mxu(a, g, "nn")


_bdot.defvjp(_bdot_fwd, _bdot_bwd)


def _fxu(a, b, mode):
    dims = {"nn": ((1,), (0,)), "nt": ((1,), (1,)), "tn": ((0,), (0,))}[mode]
    return lax.dot_general(a, b, (dims, ((), ())), precision=HI, preferred_element_type=F32)


@functools.partial(jax.custom_vjp, nondiff_argnums=(2,))
def _fdot(a, b, mode):
    return _fxu(a, b, mode)


def _fdot_fwd(a, b, mode):
    return _fxu(a, b, mode), (a, b)


def _fdot_bwd(mode, res, g):
    a, b = res
    if mode == "nn":
        return _fxu(g, b, "nt"), _fxu(a, g, "tn")
    if mode == "nt":
        return _fxu(g, b, "nn"), _fxu(g, a, "tn")
    return _fxu(b, g, "nt"), _fxu(a, g, "nn")


_fdot.defvjp(_fdot_fwd, _fdot_bwd)


def _iota(shape, dim):
    return lax.broadcasted_iota(jnp.int32, shape, dim)


def _make_shift(h):
    @functools.partial(jax.custom_vjp, nondiff_argnums=(2,))
    def shift(halo, cur, k):
        if k == 0:
            return cur
        full = jnp.concatenate([halo, cur], axis=0)
        return pltpu.roll(full, k, 0)[h:]

    def fwd(halo, cur, k):
        return shift(halo, cur, k), None

    def bwd(k, _, g):
        t, w = g.shape
        if k == 0:
            return jnp.zeros((h, w), F32), g
        d_cur = jnp.where(_iota((t, w), 0) < t - k, pltpu.roll(g, t - k, 0), 0.0)
        top = g[:h]
        d_halo = jnp.where(_iota((h, w), 0) >= h - k, pltpu.roll(top, h - k, 0) if k < h else top, 0.0)
        return d_halo, d_cur

    shift.defvjp(fwd, bwd)
    return shift


_shift8 = _make_shift(8)
_shift16 = _make_shift(16)


def _make_tail(h):
    @jax.custom_vjp
    def tail(x):
        return x[x.shape[0] - h:]

    def fwd(x):
        return tail(x), x.shape[0]

    def bwd(t, g):
        return (jnp.concatenate([jnp.zeros((t - h, g.shape[1]), F32), g], axis=0),)

    tail.defvjp(fwd, bwd)
    return tail


_tail8 = _make_tail(8)
_tail16 = _make_tail(16)


@jax.custom_vjp
def _cumsum_rows(x):
    t = x.shape[0]
    row, s = _iota(x.shape, 0), 1
    while s < t:
        x = x + jnp.where(row >= s, pltpu.roll(x, s, 0), 0.0)
        s *= 2
    return x


def _cumsum_rows_fwd(x):
    return _cumsum_rows(x), None


def _cumsum_rows_bwd(_, g):
    t = g.shape[0]
    row, s = _iota(g.shape, 0), 1
    while s < t:
        g = g + jnp.where(row < t - s, pltpu.roll(g, t - s, 0), 0.0)
        s *= 2
    return (g,)


_cumsum_rows.defvjp(_cumsum_rows_fwd, _cumsum_rows_bwd)


@jax.custom_vjp
def _rot_pairs(t):
    e = _iota(t.shape, 1) % ATT_HEAD_DIM
    n = t.shape[1]
    return jnp.where(e < 8, -pltpu.roll(t, n - 8, 1), jnp.where(e < 16, pltpu.roll(t, 8, 1), 0.0))


def _rot_pairs_fwd(t):
    return _rot_pairs(t), None


def _rot_pairs_bwd(_, g):
    e = _iota(g.shape, 1) % ATT_HEAD_DIM
    n = g.shape[1]
    return (pltpu.roll(jnp.where(e < 8, -g, 0.0), 8, 1) + pltpu.roll(jnp.where(jnp.logical_and(e >= 8, e < 16), g, 0.0), n - 8, 1),)


_rot_pairs.defvjp(_rot_pairs_fwd, _rot_pairs_bwd)


def _make_thirds():
    @jax.custom_vjp
    def thirds(x):
        w = x.shape[1] // 3
        return x[:, :w], x[:, w:2 * w], x[:, 2 * w:]

    def fwd(x):
        return thirds(x), None

    def bwd(_, g):
        return (jnp.concatenate(g, axis=1),)

    thirds.defvjp(fwd, bwd)
    return thirds


_thirds = _make_thirds()


def _rowk(w, k):
    return jnp.sum(jnp.where(_iota(w.shape, 0) == k, w, 0.0), axis=0, keepdims=True)


def _silu(x):
    return x * (0.5 * jnp.tanh(0.5 * x) + 0.5)


def _softplus(x):
    return jnp.maximum(x, 0.0) + jnp.log(1.0 + jnp.exp(-jnp.abs(x)))


def _tile(dim, target, unit=128):
    if dim <= target:
        return dim
    best = None
    for t in range(unit, target + 1, unit):
        if dim % t == 0:
            best = t
    assert best is not None, (dim, target)
    return best


class Row:
    def __init__(self, arr, w=None, fb=None, fc=None, diff=True, slot=False, dcols=None, dfc=None, ddtype=F32, view=None):
        self.ddtype = ddtype
        self.view = view
        self.arr = arr
        self.w = arr.shape[2] if w is None else w
        self.fb = (lambda b: 0) if fb is None else fb
        self.fc = (lambda b: 0) if fc is None else fc
        self.diff = diff
        self.slot = slot
        self.dcols = dcols
        self.dfc = dfc


class Vec:
    def __init__(self, arr, w=None, fc=None, diff=True):
        self.arr = arr
        self.w = arr.shape[1] if w is None else w
        self.fc = fc
        self.diff = diff


def _row_spec(r, t, nchunk, reverse):
    shape = (1, t, r.w) if r.view is None else (1, t // r.view, r.view * r.w)
    if reverse:
        return pl.BlockSpec(shape, lambda b, i, r=r: (r.fb(b), nchunk - 1 - i, r.fc(b)))
    return pl.BlockSpec(shape, lambda b, i, r=r: (r.fb(b), i, r.fc(b)))


def _load_row(ref, r, t, scr):
    if r.view is None:
        return ref[0]
    d, w = r.view, r.w
    for q in range(d):
        for j in range(w // 128):
            scr[j, pl.ds(q, t // d, stride=d), :] = ref[0, :, q * w + 128 * j:q * w + 128 * (j + 1)].astype(F32)
    return jnp.concatenate([scr[j] for j in range(w // 128)], axis=1)


def _store_row(ref, r, t, scr, val):
    if r.view is None:
        ref[0] = val.astype(ref.dtype)
        return
    d, w = r.view, r.w
    for j in range(w // 128):
        scr[j] = val[:, 128 * j:128 * (j + 1)]
    for q in range(d):
        for j in range(w // 128):
            ref[0, :, q * w + 128 * j:q * w + 128 * (j + 1)] = scr[j, pl.ds(q, t // d, stride=d), :].astype(ref.dtype)


def _view_scratch(specs, t):
    ws = [r.w for r in specs if r.view is not None]
    return [pltpu.VMEM((max(ws) // 128, t, 128), F32)] if ws else []


def _vec_spec(v):
    if v.fc is None:
        return pl.BlockSpec(v.arr.shape, lambda b, i: (0, 0))
    return pl.BlockSpec((v.arr.shape[0], v.w), lambda b, i, v=v: (0, v.fc(b)))


def _cparams():
    return pltpu.CompilerParams(dimension_semantics=("arbitrary", "arbitrary"), vmem_limit_bytes=VMEM_LIMIT_BYTES)


def scan_fwd(name, fn, *, nb, nchunk, t, rows, vecs, carries, outs, save):
    nr, nv, nc, no = len(rows), len(vecs), len(carries), len(outs)

    def body(*refs):
        row_refs, vec_refs = refs[:nr], refs[nr:nr + nv]
        out_refs = refs[nr + nv:nr + nv + no]
        save_refs = refs[nr + nv + no:nr + nv + no + (nc if save else 0)]
        scr = refs[len(refs) - 1] if stage else None
        car = refs[len(refs) - nc - len(stage):len(refs) - len(stage)] if nc else ()
        b, i = pl.program_id(0), pl.program_id(1)
        if nc:
            @pl.when(i == 0)
            def _():
                for c_ref in car:
                    c_ref[...] = jnp.zeros(c_ref.shape, F32)
        cin = [c_ref[...] for c_ref in car]
        if save:
            for s_ref, cv in zip(save_refs, cin):
                s_ref[0, 0] = cv
        new_c, o = fn(i, b, cin, [_load_row(ref, r, t, scr) for ref, r in zip(row_refs, rows)], [v[...] for v in vec_refs])
        for c_ref, cv in zip(car, new_c):
            c_ref[...] = cv
        for o_ref, spec, ov in zip(out_refs, outs, o):
            _store_row(o_ref, spec, t, scr, ov)

    stage = _view_scratch(list(rows) + list(outs), t)
    out_shape = [o.arr for o in outs]
    out_specs = [_row_spec(o, t, nchunk, False) for o in outs]
    if save:
        for cs in carries:
            out_shape.append(jax.ShapeDtypeStruct((nb, nchunk) + tuple(cs), F32))
            out_specs.append(pl.BlockSpec((1, 1) + tuple(cs), lambda b, i: (b, i, 0, 0)))
    res = pl.pallas_call(
        body, name=name, grid=(nb, nchunk),
        in_specs=[_row_spec(r, t, nchunk, False) for r in rows] + [_vec_spec(v) for v in vecs],
        out_specs=out_specs, out_shape=out_shape,
        scratch_shapes=[pltpu.VMEM(tuple(cs), F32) for cs in carries] + stage,
        compiler_params=_cparams(),
    )(*[r.arr for r in rows], *[v.arr for v in vecs])
    return list(res[:no]), list(res[no:])


def scan_bwd(name, fn, *, nb, nchunk, t, rows, vecs, carries, saved, douts, adds=None):
    adds = adds or {}
    nr, nv, nc, no = len(rows), len(vecs), len(carries), len(douts)
    dri = [k for k, r in enumerate(rows) if r.diff]
    dvi = [k for k, v in enumerate(vecs) if v.diff]
    add_keys = sorted(adds)
    na = len(add_keys)

    def body(*refs):
        p = 0
        row_refs = refs[p:p + nr]; p += nr
        vec_refs = refs[p:p + nv]; p += nv
        save_refs = refs[p:p + nc]; p += nc
        dout_refs = refs[p:p + no]; p += no
        add_refs = refs[p:p + na]; p += na
        drow_refs = refs[p:p + len(dri)]; p += len(dri)
        dvec_refs = refs[p:p + len(dvi)]; p += len(dvi)
        dcar = refs[p:p + nc]
        scr = refs[len(refs) - 1] if stage else None
        b, ir = pl.program_id(0), pl.program_id(1)
        ci = nchunk - 1 - ir
        if nc:
            @pl.when(ir == 0)
            def _():
                for c_ref in dcar:
                    c_ref[...] = jnp.zeros(c_ref.shape, F32)
        rows_v = [_load_row(ref, r, t, scr) for ref, r in zip(row_refs, rows)]
        vecs_v = [v[...] for v in vec_refs]
        cin = [s[0, 0] for s in save_refs]
        dc = [c_ref[...] for c_ref in dcar]
        dout_v = [_load_row(ref, r, t, scr).astype(F32) for ref, r in zip(dout_refs, douts)]

        def f(cs, dr, dv):
            rr, vv = list(rows_v), list(vecs_v)
            for k, idx in enumerate(dri):
                rr[idx] = dr[k]
            for k, idx in enumerate(dvi):
                vv[idx] = dv[k]
            return fn(ci, b, cs, rr, vv)

        _, vjp = jax.vjp(f, cin, [rows_v[k].astype(F32) for k in dri], [vecs_v[k].astype(F32) for k in dvi])
        dcin, drows, dvecs = vjp((dc, dout_v))
        for c_ref, cv in zip(dcar, dcin):
            c_ref[...] = cv
        for k, (o_ref, ov) in enumerate(zip(drow_refs, drows)):
            if dri[k] in adds:
                ov = ov + add_refs[add_keys.index(dri[k])][0].astype(F32)
            _store_row(o_ref, rows[dri[k]], t, scr, ov)
        for k, (o_ref, ov) in enumerate(zip(dvec_refs, dvecs)):
            first = (ir == 0) if vecs[dvi[k]].fc is not None else jnp.logical_and(ir == 0, b == 0)

            @pl.when(first)
            def _(o_ref=o_ref, ov=ov):
                o_ref[...] = ov

            @pl.when(jnp.logical_not(first))
            def _(o_ref=o_ref, ov=ov):
                o_ref[...] += ov

    stage = _view_scratch(list(rows) + list(douts), t)
    in_specs = ([_row_spec(r, t, nchunk, True) for r in rows] + [_vec_spec(v) for v in vecs]
                + [pl.BlockSpec((1, 1) + tuple(cs), lambda b, i: (b, nchunk - 1 - i, 0, 0)) for cs in carries]
                + [_row_spec(d, t, nchunk, True) for d in douts]
                + [_row_spec(adds[k], t, nchunk, True) for k in add_keys])
    out_shape, out_specs = [], []
    for k in dri:
        r = rows[k]
        if r.slot:
            out_shape.append(jax.ShapeDtypeStruct((nb, r.arr.shape[1], r.w), r.ddtype))
            out_specs.append(pl.BlockSpec((1, t, r.w), lambda b, i: (b, nchunk - 1 - i, 0)))
        elif r.dcols is not None:
            out_shape.append(jax.ShapeDtypeStruct((r.arr.shape[0], r.arr.shape[1], r.dcols), r.ddtype))
            out_specs.append(pl.BlockSpec((1, t, r.w), lambda b, i, r=r: (r.fb(b), nchunk - 1 - i, r.dfc(b))))
        else:
            out_shape.append(jax.ShapeDtypeStruct(r.arr.shape, r.ddtype))
            out_specs.append(_row_spec(r, t, nchunk, True))
    for k in dvi:
        out_shape.append(jax.ShapeDtypeStruct(vecs[k].arr.shape, F32))
        out_specs.append(_vec_spec(vecs[k]))
    res = pl.pallas_call(
        body, name=name, grid=(nb, nchunk), in_specs=in_specs, out_specs=out_specs, out_shape=out_shape,
        scratch_shapes=[pltpu.VMEM(tuple(cs), F32) for cs in carries] + stage,
        compiler_params=_cparams(),
    )(*[r.arr for r in rows], *[v.arr for v in vecs], *saved, *[d.arr for d in douts], *[adds[k].arr for k in add_keys])
    return list(res[:len(dri)]), list(res[len(dri):])


def out_row(shape, dtype=F32, w=None, fb=None, fc=None):
    return Row(jax.ShapeDtypeStruct(shape, dtype), w, fb, fc)


def _conv(shift, halo, cur, w, bias, taps):
    y = bias
    for k in range(taps):
        y = y + _rowk(w, k) * shift(halo, cur, taps - 1 - k)
    return y


def _ssd_fn(ci, b, carries, rows, vecs):
    cx, cb_, cc, ht = carries
    z, xr, br, cr, dtr = rows
    cwx, cbx, cwb, cbb, cwc, cbc, dtb, alog, dsk, ng = vecs
    t = z.shape[0]
    xs = _silu(_conv(_shift8, cx, xr, cwx, cbx, 4))
    bm = _silu(_conv(_shift8, cb_, br, cwb, cbb, 4))
    cm = _silu(_conv(_shift8, cc, cr, cwc, cbc, 4))
    dt = _softplus(dtr + dtb)
    acol = _cumsum_rows(dt * (-jnp.exp(alog)))
    arow = acol.T
    r, c = _iota((t, t), 0), _iota((t, t), 1)
    causal = r >= c
    cbm = _bdot(cm, bm, "nt")
    lane, sub = _iota(acol.shape, 1), _iota(arow.shape, 0)
    colh = _iota(xs.shape, 1) // 64
    a, dtx, dx, acs = jnp.zeros(xs.shape, F32), jnp.zeros(xs.shape, F32), jnp.zeros((1, xs.shape[1]), F32), []
    for j in range(4):
        h = 4 * b + j
        ac = jnp.sum(jnp.where(lane == h, acol, 0.0), axis=1, keepdims=True)
        acs.append(ac)
        a = jnp.where(colh == j, ac, a)
        dtx = jnp.where(colh == j, jnp.sum(jnp.where(lane == h, dt, 0.0), axis=1, keepdims=True), dtx)
        dx = jnp.where(_iota(dx.shape, 1) // 64 == j, jnp.sum(jnp.where(_iota(dsk.shape, 1) == h, dsk, 0.0), axis=1, keepdims=True), dx)
    atot = jnp.sum(jnp.where(_iota(a.shape, 0) == t - 1, a, 0.0), axis=0, keepdims=True)
    x = xs * dtx
    ydiag = jnp.zeros(x.shape, F32)
    for j in range(4):
        ar = jnp.sum(jnp.where(sub == 4 * b + j, arow, 0.0), axis=0, keepdims=True)
        lmat = jnp.exp(jnp.where(causal, acs[j] - ar, NEG))
        ydiag = ydiag + _bdot(cbm * lmat, jnp.where(colh == j, x, 0.0), "nn")
    yoff = _bdot(cm, ht, "nn") * jnp.exp(a)
    ht_new = ht * jnp.exp(atot) + _bdot(bm, x * jnp.exp(atot - a), "tn")
    y = ydiag + yoff + dx * xs
    yz = y * _silu(z)
    yn = yz * lax.rsqrt(jnp.mean(yz * yz, axis=-1, keepdims=True) + NORM_EPS) * ng
    return [_tail8(xr), _tail8(br), _tail8(cr), ht_new], [yn]


_SSD_T = 256
_SSD_CARRIES = [(8, 256), (8, 128), (8, 128), (128, 256)]


def _ssd_io(proj3, p):
    own = lambda b: b
    rows = [Row(proj3, 256, fc=own, dcols=512, dfc=own, ddtype=BF16),
            Row(proj3, 256, fc=lambda b: 2 + b, dcols=512, dfc=own, ddtype=BF16),
            Row(proj3, 128, fc=lambda b: 8 + b, dcols=256, dfc=own, ddtype=BF16),
            Row(proj3, 128, fc=lambda b: 10 + b, dcols=256, dfc=own, ddtype=BF16),
            Row(proj3, 128, fc=lambda b: 14, slot=True)]
    vecs = [Vec(p["cw"], 256, lambda b: b), Vec(p["cb"], 256, lambda b: b),
            Vec(p["cw"], 128, lambda b: 4 + b), Vec(p["cb"], 128, lambda b: 4 + b),
            Vec(p["cw"], 128, lambda b: 6 + b), Vec(p["cb"], 128, lambda b: 6 + b),
            Vec(p["dtb"]), Vec(p["alog"]), Vec(p["dsk"]), Vec(p["ng"], 256, lambda b: b)]
    return rows, vecs


def ssd_forward(name, proj3, p):
    rows, vecs = _ssd_io(proj3, p)
    s = proj3.shape[1]
    (y,), saved = scan_fwd(name, _ssd_fn, nb=2, nchunk=s // _SSD_T, t=_SSD_T, rows=rows, vecs=vecs,
                           carries=_SSD_CARRIES, outs=[out_row((1, s, SSD_INNER), BF16, 256, fc=lambda b: b)], save=True)
    return y, saved


def ssd_backward(name, proj3, p, saved, dmix3):
    rows, vecs = _ssd_io(proj3, p)
    s = proj3.shape[1]
    drows, dvecs = scan_bwd(name, _ssd_fn, nb=2, nchunk=s // _SSD_T, t=_SSD_T, rows=rows, vecs=vecs,
                            carries=_SSD_CARRIES, saved=saved, douts=[Row(dmix3, 256, fc=lambda b: b)])
    return drows, dvecs


def _pool_fn(ci, b, carries, rows, vecs):
    (cu,) = carries
    (u,) = rows
    wbd, scale = vecs
    t = u.shape[0]
    pos = ci * t + _iota(u.shape, 0)
    grp = _iota(u.shape, 1) // 64
    acc, pooled, k = u, jnp.zeros(u.shape, F32), 1
    for gi, w in enumerate(POOL_WINDOWS):
        while k < w:
            acc = acc + _shift16(cu, u, k)
            k += 1
        pooled = jnp.where(grp == gi, acc / jnp.minimum(pos + 1, w).astype(F32), pooled)
    y = _bdot(pooled - u, wbd, "nn") * scale
    return [_tail16(u)], [y]


_POOL_T = 256


def _pool_io(proj3, wbd, scale):
    return [Row(proj3, 256, fc=lambda b: 6, dcols=256, dfc=lambda b: 0, ddtype=BF16)], [Vec(wbd), Vec(scale)]


def pool_forward(name, proj3, wbd, scale):
    rows, vecs = _pool_io(proj3, wbd, scale)
    s = proj3.shape[1]
    (y,), saved = scan_fwd(name, _pool_fn, nb=1, nchunk=s // _POOL_T, t=_POOL_T, rows=rows, vecs=vecs,
                           carries=[(16, 256)], outs=[out_row((1, s, POOL_W), BF16)], save=True)
    return y, saved


def pool_backward(name, proj3, wbd, scale, saved, dmix3):
    rows, vecs = _pool_io(proj3, wbd, scale)
    s = proj3.shape[1]
    return scan_bwd(name, _pool_fn, nb=1, nchunk=s // _POOL_T, t=_POOL_T, rows=rows, vecs=vecs,
                    carries=[(16, 256)], saved=saved, douts=[Row(dmix3, 256, fc=lambda b: 2)])


def _attn_fn(ci, b, carries, rows, vecs):
    kp, vp = carries
    qr, kr, v = _thirds(rows[0])
    scale = ATT_HEAD_DIM ** -0.5
    q = qr
    n = q.shape[0]
    r, c = _iota((n, n), 0), _iota((n, n), 1)
    prev_ok, cur_ok = jnp.logical_and(c >= r, ci > 0), r >= c
    head = _iota(q.shape, 1) // ATT_HEAD_DIM
    o, lse = jnp.zeros(q.shape, F32), jnp.zeros(q.shape, F32)
    for h in range(ATT_HEADS):
        mine = head == h
        qh = jnp.where(mine, qr, 0.0)
        sp = jnp.where(prev_ok, _bdot(qh, kp, "nt") * scale, NEG)
        sc = jnp.where(cur_ok, _bdot(qh, kr, "nt") * scale, NEG)
        m = lax.stop_gradient(jnp.maximum(jnp.max(sp, axis=1, keepdims=True), jnp.max(sc, axis=1, keepdims=True)))
        pp, pc = jnp.exp(sp - m), jnp.exp(sc - m)
        l = jnp.sum(pp, axis=1, keepdims=True) + jnp.sum(pc, axis=1, keepdims=True)
        o = jnp.where(mine, (_bdot(pp, vp, "nn") + _bdot(pc, v, "nn")) / l, o)
        lse = jnp.where(mine, m + jnp.log(l), lse)
    return [kr, v], [o, lse]


_ATT_CARRIES = [(ATT_BLOCK, ATT_W), (ATT_BLOCK, ATT_W)]


def attn_forward(name, pv, d):
    l = pv.shape[1]
    own = lambda b: b
    outs = [out_row((1, l, d * ATT_W), F32, ATT_W, fc=own) for _ in range(2)]
    (o, lse), saved = scan_fwd(name, _attn_fn, nb=d, nchunk=l // ATT_BLOCK, t=ATT_BLOCK, rows=[Row(pv, 3 * ATT_W, fc=own)],
                               vecs=[], carries=_ATT_CARRIES, outs=outs, save=True)
    return o, lse, saved


def attn_backward(name, pv, d, saved, do, dlse):
    l = pv.shape[1]
    own = lambda b: b
    (dpv,), _ = scan_bwd(name, _attn_fn, nb=d, nchunk=l // ATT_BLOCK, t=ATT_BLOCK, rows=[Row(pv, 3 * ATT_W, fc=own)], vecs=[],
                         carries=_ATT_CARRIES, saved=saved, douts=[Row(do, ATT_W, fc=own), Row(dlse, ATT_W, fc=own)])
    return dpv


def _rope_fn(ci, b, carries, rows, vecs):
    x, cs, sn = rows
    return [], [x * cs + _rot_pairs(x) * sn]


def _rope3_fn(ci, b, carries, rows, vecs):
    _, (y,) = _rope_fn(ci, b, carries, rows, vecs)
    return [], [y, y, y]


def _by_residue(a_or_shape, w, d):
    if isinstance(a_or_shape, tuple):
        _, s, _ = a_or_shape
        return Row(jax.ShapeDtypeStruct((1, s // d, d * w), F32), w, view=None if d == 1 else d)
    return Row(a_or_shape, w, view=None if d == 1 else d)


def rope_forward(name, qkv3, cs3, sn3):
    s, w = qkv3.shape[1], qkv3.shape[2]
    ys, _ = scan_fwd(name, _rope3_fn, nb=1, nchunk=s // _ROW_T, t=_ROW_T, vecs=[], carries=[], save=False,
                     rows=[Row(qkv3), Row(cs3, diff=False), Row(sn3, diff=False)],
                     outs=[_by_residue(qkv3.shape, w, d) for _, d in ATT_PATTERNS])
    return ys


def rope_backward(name, qkv3, cs3, sn3, dys):
    s, w = qkv3.shape[1], qkv3.shape[2]
    (dx,), _ = scan_bwd(name, _rope3_fn, nb=1, nchunk=s // _ROW_T, t=_ROW_T, vecs=[], carries=[], saved=[],
                        rows=[Row(qkv3, ddtype=BF16), Row(cs3, diff=False), Row(sn3, diff=False)],
                        douts=[_by_residue(a, w, d) for a, (_, d) in zip(dys, ATT_PATTERNS)])
    return dx


def _merge_fn(ci, b, carries, rows, vecs):
    o1, o2, o3, l1, l2, l3 = rows
    mx = lax.stop_gradient(jnp.maximum(l1, jnp.maximum(l2, l3)))
    e1, e2, e3 = jnp.exp(l1 - mx), jnp.exp(l2 - mx), jnp.exp(l3 - mx)
    return [], [(e1 * o1 + e2 * o2 + e3 * o3) / (e1 + e2 + e3)]


_ROW_T = 256


def _merge_rows(os_, ls_):
    ds = [d for _, d in ATT_PATTERNS]
    return [_by_residue(a, ATT_W, d) for a, d in zip(os_, ds)] + [_by_residue(a, ATT_W, d) for a, d in zip(ls_, ds)]


def merge_forward(name, os_, ls_, s):
    (y,), _ = scan_fwd(name, _merge_fn, nb=1, nchunk=s // _ROW_T, t=_ROW_T, rows=_merge_rows(os_, ls_), vecs=[],
                       carries=[], outs=[out_row((1, s, ATT_W), BF16)], save=False)
    return y


def merge_backward(name, os_, ls_, dmix3):
    s = dmix3.shape[1]
    drows, _ = scan_bwd(name, _merge_fn, nb=1, nchunk=s // _ROW_T, t=_ROW_T, rows=_merge_rows(os_, ls_), vecs=[],
                        carries=[], saved=[], douts=[Row(dmix3, 256, fc=lambda b: 3)])
    return drows


def _norm_mod_fn(ci, b, carries, rows, vecs):
    (x,) = rows
    g, sc, sh = vecs
    xn = x * lax.rsqrt(jnp.mean(x * x, axis=-1, keepdims=True) + NORM_EPS)
    return [], [xn * g * (1.0 + sc) + sh]


def norm_mod_forward(name, x3, g, sc, sh):
    s = x3.shape[1]
    (h,), _ = scan_fwd(name, _norm_mod_fn, nb=1, nchunk=s // _ROW_T, t=_ROW_T, rows=[Row(x3)], vecs=[Vec(g), Vec(sc), Vec(sh)],
                       carries=[], outs=[out_row(x3.shape, BF16)], save=False)
    return h


def norm_mod_backward(name, x3, g, sc, sh, dh3, add3):
    s = x3.shape[1]
    (dx,), dv = scan_bwd(name, _norm_mod_fn, nb=1, nchunk=s // _ROW_T, t=_ROW_T, rows=[Row(x3)], vecs=[Vec(g), Vec(sc), Vec(sh)],
                         carries=[], saved=[], douts=[Row(dh3)], adds={0: Row(add3)})
    return dx, dv


def _gate_fn(ci, b, carries, rows, vecs):
    return [], [rows[0] * vecs[0]]


def gate_backward(name, o3, g, dx3):
    s = o3.shape[1]
    (do,), (dg,) = scan_bwd(name, _gate_fn, nb=1, nchunk=s // _ROW_T, t=_ROW_T, rows=[Row(o3, ddtype=BF16)], vecs=[Vec(g)],
                            carries=[], saved=[], douts=[Row(dx3)])
    return do, dg


def _make_halves():
    @jax.custom_vjp
    def halves(x):
        h = x.shape[1] // 2
        return x[:, :h], x[:, h:]

    def fwd(x):
        return halves(x), None

    def bwd(_, g):
        return (jnp.concatenate(g, axis=1),)

    halves.defvjp(fwd, bwd)
    return halves


_halves = _make_halves()


def _ffn_fn(ci, b, carries, rows, vecs):
    (cu,) = carries
    (u,) = rows
    w, bias = vecs
    hg, hu = _halves(_conv(_shift8, cu, u, w, bias, 3))
    return [_tail8(u)], [_silu(hg) * hu]


_FFN_T = 256
_FFN_CW = FFN_DIM // 2
_FFN_CARRIES = [(8, 2 * _FFN_CW)]
FFN_BLOCK_ORDER = [0, 2, 1, 3]


def _ffn_io(up3, cw, cb):
    own = lambda b: b
    return [Row(up3, 2 * _FFN_CW, fc=own, ddtype=BF16)], [Vec(cw, 2 * _FFN_CW, own), Vec(cb, 2 * _FFN_CW, own)]


def ffn_mid_forward(name, up3, cw, cb):
    rows, vecs = _ffn_io(up3, cw, cb)
    s = up3.shape[1]
    (act,), saved = scan_fwd(name, _ffn_fn, nb=2, nchunk=s // _FFN_T, t=_FFN_T, rows=rows, vecs=vecs, carries=_FFN_CARRIES,
                             outs=[out_row((1, s, FFN_DIM), BF16, _FFN_CW, fc=lambda b: b)], save=True)
    return act, saved


def ffn_mid_backward(name, up3, cw, cb, saved, dact3):
    rows, vecs = _ffn_io(up3, cw, cb)
    s = up3.shape[1]
    return scan_bwd(name, _ffn_fn, nb=2, nchunk=s // _FFN_T, t=_FFN_T, rows=rows, vecs=vecs, carries=_FFN_CARRIES,
                    saved=saved, douts=[Row(dact3, _FFN_CW, fc=lambda b: b)])


def _adam_fn(ci, b, carries, rows, vecs):
    w, g, m, v = rows
    m = ADAM_B1 * m + (1.0 - ADAM_B1) * g
    v = ADAM_B2 * v + (1.0 - ADAM_B2) * (g * g)
    m_hat = m / (1.0 - ADAM_B1 ** ADAM_STEP)
    v_hat = v / (1.0 - ADAM_B2 ** ADAM_STEP)
    delta = -ADAM_LR * (m_hat / (jnp.sqrt(v_hat) + ADAM_EPS) + ADAM_WD * w)
    return [], [delta, m, v]


def adamw(name, w, g, m, v):
    shape = w.shape
    c = shape[-1]
    r = int(np.prod(shape[:-1]))
    t = _tile(r, 256, 8)
    as3 = lambda a: a.reshape(1, r, c)
    outs, _ = scan_fwd(name, _adam_fn, nb=1, nchunk=r // t, t=t, rows=[Row(as3(a)) for a in (w, g, m, v)], vecs=[], carries=[],
                       outs=[out_row((1, r, c)) for _ in range(3)], save=False)
    return [o.reshape(shape) for o in outs]


def rope_tables(positions):
    inv_freq = ROPE_THETA ** (-jnp.arange(0, ROT_DIM, 2, dtype=F32) / ROT_DIM)
    ang = positions.astype(F32)[:, None] * inv_freq
    s = positions.shape[0]
    cs = jnp.concatenate([jnp.cos(ang), jnp.cos(ang), jnp.ones((s, ATT_HEAD_DIM - ROT_DIM), F32)], axis=1)
    sn = jnp.concatenate([jnp.sin(ang), jnp.sin(ang), jnp.zeros((s, ATT_HEAD_DIM - ROT_DIM), F32)], axis=1)
    cs3 = jnp.concatenate([jnp.tile(cs, (1, 2 * ATT_HEADS)), jnp.ones((s, ATT_W), F32)], axis=1)
    sn3 = jnp.concatenate([jnp.tile(sn, (1, 2 * ATT_HEADS)), jnp.zeros((s, ATT_W), F32)], axis=1)
    return cs3[None], sn3[None]


def attention_forward(lname, qkv3, cs3, sn3):
    s = qkv3.shape[1]
    rotated = rope_forward(f"{lname}_rope", qkv3, cs3, sn3)
    os_, ls_, keep = [], [], []
    for pi, (_, d) in enumerate(ATT_PATTERNS):
        o, lse, saved = attn_forward(f"{lname}_attn{pi}", rotated[pi], d)
        os_.append(o)
        ls_.append(lse)
        keep.append(saved)
    y = merge_forward(f"{lname}_merge", os_, ls_, s)
    return y, (rotated, os_, ls_, keep)


def attention_backward(lname, qkv3, cs3, sn3, res, dmix3):
    rotated, os_, ls_, keep = res
    dm = merge_backward(f"{lname}_merge_b", os_, ls_, dmix3)
    dys = [attn_backward(f"{lname}_attn{pi}_b", rotated[pi], d, keep[pi], dm[pi], dm[3 + pi]) for pi, (_, d) in enumerate(ATT_PATTERNS)]
    return rope_backward(f"{lname}_rope_b", qkv3, cs3, sn3, dys)


def mm(name, a, b, mode, out_dtype=F32, res=None, gate=None, tm=1408, tn=1536, tk=1408, into=None):
    if mode == "nn":
        (m, k), n = a.shape, b.shape[1]
    elif mode == "nt":
        (m, k), n = a.shape, b.shape[0]
    else:
        (k, m), n = a.shape, b.shape[1]
    tm, tn, tk = _tile(m, tm), _tile(n, tn), _tile(k, tk)
    nk = k // tk
    a_spec = pl.BlockSpec((tk, tm), lambda i, j, q: (q, i)) if mode == "tn" else pl.BlockSpec((tm, tk), lambda i, j, q: (i, q))
    b_spec = pl.BlockSpec((tn, tk), lambda i, j, q: (j, q)) if mode == "nt" else pl.BlockSpec((tk, tn), lambda i, j, q: (q, j))
    o_spec = pl.BlockSpec((tm, tn), lambda i, j, q: (i, j))
    fused = res is not None
    lead = 0 if into is None else into[0].ndim - 2
    first = (0,) * lead + (slice(None), slice(None))

    def body(*refs):
        if fused:
            a_ref, b_ref, r_ref, g_ref, o_ref, o2_ref, acc = refs
        elif into is not None:
            a_ref, b_ref, _, o_ref, acc = refs
        else:
            a_ref, b_ref, o_ref, acc = refs
        q = pl.program_id(2)

        @pl.when(q == 0)
        def _():
            acc[...] = jnp.zeros(acc.shape, F32)

        acc[...] += _mxu(a_ref[...], b_ref[...], mode)

        @pl.when(q == nk - 1)
        def _():
            o_ref[first] = acc[...].astype(o_ref.dtype)
            if fused:
                o2_ref[...] = r_ref[...] + g_ref[...] * acc[...]

    ins, in_specs = [a, b], [a_spec, b_spec]
    out_shape, out_specs = [jax.ShapeDtypeStruct((m, n), out_dtype)], [o_spec]
    if fused:
        ins += [res, gate]
        in_specs += [o_spec, pl.BlockSpec((1, tn), lambda i, j, q: (0, j))]
        out_shape.append(jax.ShapeDtypeStruct((m, n), F32))
        out_specs.append(o_spec)
    aliases = {}
    if into is not None:
        buf, omap = into
        ins.append(buf)
        in_specs.append(pl.BlockSpec(memory_space=pl.ANY))
        out_shape = [jax.ShapeDtypeStruct(buf.shape, buf.dtype)]
        out_specs = [pl.BlockSpec((1,) * lead + (tm, tn), lambda i, j, q: omap(i, j))]
        aliases = {2: 0}
    out = pl.pallas_call(
        body, name=name, grid=(m // tm, n // tn, nk), in_specs=in_specs, out_specs=out_specs, out_shape=out_shape,
        scratch_shapes=[pltpu.VMEM((tm, tn), F32)], input_output_aliases=aliases,
        compiler_params=pltpu.CompilerParams(dimension_semantics=("parallel", "parallel", "arbitrary"),
                                             vmem_limit_bytes=VMEM_LIMIT_BYTES),
    )(*ins)
    return tuple(out) if fused else out[0]


def final_loss(name, x3, t3, g):
    s, d = x3.shape[1], x3.shape[2]
    t = _ROW_T

    def body(x_ref, t_ref, g_ref, loss_ref, dx_ref, dg_ref):
        i = pl.program_id(0)
        tv = t_ref[0]

        def f(x, gg):
            y = x * lax.rsqrt(jnp.mean(x * x, axis=-1, keepdims=True) + NORM_EPS) * gg
            e = y - tv
            return 0.5 * jnp.sum(jnp.mean(e * e, axis=-1, keepdims=True), axis=0, keepdims=True)

        l, vjp = jax.vjp(f, x_ref[0], g_ref[...])
        dx, dg = vjp(jnp.ones((1, 1), F32))
        dx_ref[0] = dx

        @pl.when(i == 0)
        def _():
            loss_ref[...] = jnp.zeros(loss_ref.shape, F32)
            dg_ref[...] = jnp.zeros(dg_ref.shape, F32)

        loss_ref[...] += jnp.broadcast_to(l, loss_ref.shape)
        dg_ref[...] += dg

    row = pl.BlockSpec((1, t, d), lambda i: (0, i, 0))
    vec = pl.BlockSpec((1, d), lambda i: (0, 0))
    return pl.pallas_call(
        body, name=name, grid=(s // t,), in_specs=[row, row, vec],
        out_specs=[pl.BlockSpec((8, 128), lambda i: (0, 0)), row, vec],
        out_shape=[jax.ShapeDtypeStruct((8, 128), F32), jax.ShapeDtypeStruct(x3.shape, F32), jax.ShapeDtypeStruct((1, d), F32)],
        compiler_params=pltpu.CompilerParams(dimension_semantics=("arbitrary",), vmem_limit_bytes=VMEM_LIMIT_BYTES),
    )(x3, t3, g)


_ADA_TN = 512


def ada_forward(name, c16, ada_w):
    depth, d, cols = ada_w.shape

    def body(c_ref, w_ref, o_ref):
        o_ref[0] = _mxu(_silu(c_ref[...]), w_ref[0], "nn")

    return pl.pallas_call(
        body, name=name, grid=(depth, cols // _ADA_TN),
        in_specs=[pl.BlockSpec((16, d), lambda l, j: (0, 0)), pl.BlockSpec((1, d, _ADA_TN), lambda l, j: (l, 0, j))],
        out_specs=pl.BlockSpec((1, 16, _ADA_TN), lambda l, j: (l, 0, j)),
        out_shape=jax.ShapeDtypeStruct((depth, 16, cols), F32),
        compiler_params=pltpu.CompilerParams(dimension_semantics=("arbitrary", "arbitrary"), vmem_limit_bytes=VMEM_LIMIT_BYTES),
    )(c16, ada_w)


def ada_backward(name, c16, dmod16, w, m, v):
    depth, d, cols = w.shape

    def body(c_ref, dm_ref, w_ref, m_ref, v_ref, g_ref, dl_ref, nm_ref, nv_ref):
        g = _mxu(_silu(c_ref[...]), dm_ref[0], "tn")
        _, (delta, nm, nv) = _adam_fn(None, None, [], [w_ref[0], g, m_ref[0], v_ref[0]], [])
        g_ref[0], dl_ref[0], nm_ref[0], nv_ref[0] = g, delta, nm, nv

    blk = pl.BlockSpec((1, d, _ADA_TN), lambda l, j: (l, 0, j))
    return pl.pallas_call(
        body, name=name, grid=(depth, cols // _ADA_TN),
        in_specs=[pl.BlockSpec((16, d), lambda l, j: (0, 0)), pl.BlockSpec((1, 16, _ADA_TN), lambda l, j: (l, 0, j)), blk, blk, blk],
        out_specs=[blk] * 4, out_shape=[jax.ShapeDtypeStruct(w.shape, F32)] * 4,
        compiler_params=pltpu.CompilerParams(dimension_semantics=("arbitrary", "arbitrary"), vmem_limit_bytes=VMEM_LIMIT_BYTES),
    )(c16, dmod16, w, m, v)


def _sum_fn(ci, b, carries, rows, vecs):
    acc = rows[0]
    for r in rows[1:]:
        acc = acc + r
    return [], [acc]


def sum_slots(name, a, nsum, out_dtype=F32):
    n, r, c = a.shape
    nb = n // nsum
    t = _tile(r, 256, 8)
    rows = [Row(a, fb=(lambda b, k=k: k * nb + b)) for k in range(nsum)]
    (out,), _ = scan_fwd(name, _sum_fn, nb=nb, nchunk=r // t, t=t, rows=rows, vecs=[], carries=[],
                         outs=[out_row((nb, r, c), out_dtype, fb=lambda b: b)], save=False)
    return out


def _sum_my_layer_fn(ci, b, carries, rows, vecs):
    layer0, layer1, theirs = rows
    return [], [jnp.where(lax.axis_index("c") == 0, layer0, layer1) + theirs]


def sum_cores(name, g, theirs, out_dtype):
    _, nb, r, c = g.shape
    g8 = g.reshape(2 * nb, r, c)
    t = _tile(r, 256, 8)
    rows = [Row(g8, fb=lambda b: b), Row(g8, fb=lambda b: nb + b), Row(theirs, fb=lambda b: b)]
    (out,), _ = scan_fwd(name, _sum_my_layer_fn, nb=nb, nchunk=r // t, t=t, rows=rows, vecs=[], carries=[],
                         outs=[out_row((nb, r, c), out_dtype, fb=lambda b: b)], save=False)
    return out


def _flip(mask, pos):
    return tuple((1 - p) if m else p for m, p in zip(mask, pos))


ALL_PEERS = [(a, b, c) for a in (0, 1) for b in (0, 1) for c in (0, 1)][1:]
CHIP_PEERS = [(1, 0, 0), (0, 1, 0), (1, 1, 0)]
SIBLING = [(0, 0, 1)]


def _divisor(size, target, unit):
    best = 1
    for n in range(1, target + 1):
        if size % n == 0 and (size // n) % unit == 0:
            best = n
    return best


def _pieces(src, dst, pieces):
    shape = src.shape
    unit = 16 if src.dtype == BF16 else 8
    if pieces <= 1:
        return [(src, dst)]
    if len(shape) == 2:
        n = _divisor(shape[0], pieces, unit)
        s = shape[0] // n
        return [(src.at[pl.ds(i * s, s)], dst.at[pl.ds(i * s, s)]) for i in range(n)]
    assert len(shape) == 3, shape
    n = _divisor(shape[1], max(pieces // shape[0], 1), unit)
    s = shape[1] // n
    return [(src.at[j, pl.ds(i * s, s)], dst.at[j, pl.ds(i * s, s)]) for j in range(shape[0]) for i in range(n)]


def comm_call(name, arrays, out_shapes, masks, src_fn, dst_fn, local_fn=None, pieces=1):
    na, npeer = len(arrays), len(masks)

    def body(*refs):
        ins, outs = refs[:na], refs[na:2 * na]
        send_sems, recv_sems, loc_sems = refs[2 * na:]
        me = (lax.axis_index("x"), lax.axis_index("y"), lax.axis_index("c"))
        local = []
        if local_fn is not None:
            for k in range(na):
                s, d = local_fn(k, ins[k], outs[k], me)
                for ps, pd in _pieces(s, d, pieces):
                    pltpu.make_async_copy(ps, pd, loc_sems.at[k]).start()
                local.append(pltpu.make_async_copy(s, d, loc_sems.at[k]))

        def remote(k, p, src, dst, to):
            return pltpu.make_async_remote_copy(
                src_ref=src, dst_ref=dst, send_sem=send_sems.at[k * npeer + p], recv_sem=recv_sems.at[k * npeer + p],
                device_id=to, device_id_type=MESH)

        for k in range(na):
            for p in range(npeer):
                peer = _flip(masks[p], me)
                for ps, pd in _pieces(src_fn(k, ins[k], me, peer), dst_fn(k, outs[k], me), pieces):
                    remote(k, p, ps, pd, peer).start()
        for k in range(na):
            for p in range(npeer):
                peer = _flip(masks[p], me)
                remote(k, p, src_fn(k, ins[k], me, peer), dst_fn(k, outs[k], peer), peer).wait_recv()
        for k in range(na):
            for p in range(npeer):
                peer = _flip(masks[p], me)
                remote(k, p, src_fn(k, ins[k], me, peer), dst_fn(k, outs[k], me), peer).wait_send()
        for cp in local:
            cp.wait()

    hbm = pl.BlockSpec(memory_space=pl.ANY)
    out = pl.pallas_call(
        body, name=name, in_specs=[hbm] * na, out_specs=[hbm] * na,
        out_shape=[jax.ShapeDtypeStruct(s, a.dtype) for s, a in zip(out_shapes, arrays)],
        scratch_shapes=[pltpu.SemaphoreType.DMA((na * npeer,)), pltpu.SemaphoreType.DMA((na * npeer,)),
                        pltpu.SemaphoreType.DMA((na,))],
    )(*arrays)
    return list(out)


def _dev(pos):
    return 4 * pos[0] + 2 * pos[1] + pos[2]


def _chip(pos):
    return 2 * pos[0] + pos[1]


def allgather8(name, a):
    (out,) = comm_call(name, [a], [(8,) + a.shape], ALL_PEERS,
                       src_fn=lambda k, r, me, peer: r, dst_fn=lambda k, o, sender: o.at[_dev(sender)],
                       local_fn=lambda k, r, o, me: (r, o.at[_dev(me)]))
    return out


def gather_layer_from_chips(name, arrays):
    return comm_call(name, arrays, [(4,) + a.shape[1:] for a in arrays], CHIP_PEERS,
                     src_fn=lambda k, r, me, peer: r.at[me[2]], dst_fn=lambda k, o, sender: o.at[_chip(sender)],
                     local_fn=lambda k, r, o, me: (r.at[me[2]], o.at[_chip(me)]), pieces=8)


def swap_layers(name, arrays, c):
    got = comm_call(name, arrays, [a.shape for a in arrays], SIBLING,
                    src_fn=lambda k, r, me, peer: r, dst_fn=lambda k, o, sender: o, pieces=32)
    return [[jnp.where(c == 0, a, g), jnp.where(c == 0, g, a)] for a, g in zip(arrays, got)]


def swap_other_layer(name, arrays):
    return comm_call(name, arrays, [a.shape[1:] for a in arrays], SIBLING,
                     src_fn=lambda k, r, me, peer: r.at[peer[2]], dst_fn=lambda k, o, sender: o, pieces=32)


def scatter_to_chips(name, arrays):
    return comm_call(name, arrays, [a.shape for a in arrays], CHIP_PEERS,
                     src_fn=lambda k, r, me, peer: r.at[_chip(peer)], dst_fn=lambda k, o, sender: o.at[_chip(sender)],
                     local_fn=lambda k, r, o, me: (r.at[_chip(me)], o.at[_chip(me)]), pieces=8)


def _rows_of(shape):
    return -(-int(np.prod(shape)) // 1024) * 8


def _pack(arrs):
    parts = []
    for a in arrs:
        flat = a.reshape(-1).astype(F32)
        parts.append(jnp.pad(flat, (0, _rows_of(a.shape) * 128 - flat.shape[0])).reshape(-1, 128))
    rows = sum(p.shape[0] for p in parts)
    parts.append(jnp.zeros(((-rows) % _ROW_T, 128), F32))
    return jnp.concatenate(parts, axis=0)


def _unpack(buf, shapes):
    out, o = [], 0
    for s in shapes:
        r, n = _rows_of(s), int(np.prod(s))
        out.append(buf[o:o + r].reshape(-1)[:n].reshape(s))
        o += r
    return out


_WEIGHTS = ["ada_w", "ada_b", "norm1_g", "w_in", "ssd_conv_w", "ssd_conv_b", "ssd_dt_bias", "ssd_a_log", "ssd_d", "ssd_norm_g",
            "pool_w", "pool_scale", "w_out", "norm2_g", "ffn_up", "ffn_conv_w", "ffn_conv_b", "ffn_down", "final_g"]
_BIG = ["w_in", "w_out", "ffn_up", "ffn_down"]
_SMALL = [n for n in _WEIGHTS if n not in _BIG and n != "ada_w"]
_COL_SHARDED_SMALL = {"ssd_conv_w": 256, "ffn_conv_w": 1408}


def _pad_lanes(v, n=128):
    return jnp.pad(v.astype(F32), (0, n - v.shape[0]))[None]


def _perm_cols(w):
    pad = jnp.zeros(w.shape[:-1] + (IN_WP - IN_W,), w.dtype)
    return jnp.concatenate([w[..., :1536], w[..., 1544:1800], w[..., 1536:1544], pad, w[..., 1800:]], axis=-1)


def _unperm_cols(g):
    return jnp.concatenate([g[..., :1536], g[..., 1792:1800], g[..., 1536:1792], g[..., IN_MAIN:]], axis=-1)


_CHIP2_PARTS = [(1284, 1536), (1792, 1800), (1536, 1792), (IN_MAIN, IN_MAIN + 126)]


def _w_in_chip_cols(gp):
    q = IN_W // 4
    return [gp[:, :q], gp[:, q:2 * q], jnp.concatenate([gp[:, a:b] for a, b in _CHIP2_PARTS], axis=1), gp[:, IN_WP - q:]]


def _w_in_from_chips(a):
    c2 = a[2]
    pad = jnp.zeros((a.shape[1], IN_WP - IN_W), a.dtype)
    return jnp.concatenate([a[0], a[1], c2[:, :252], c2[:, 260:516], c2[:, 252:260], pad, c2[:, 516:], a[3]], axis=1)


def _ffn_block_perm(a):
    n = a.shape[-1] // 4
    return jnp.concatenate([a[..., j * n:(j + 1) * n] for j in FFN_BLOCK_ORDER], axis=-1)


def _layer_forward(i, x3, modv, wts, sp, cs3, sn3, rides=None):
    sh1, sc1, g1, sh2, sc2, g2 = modv
    rode = {}
    h1 = norm_mod_forward(f"l{i}_norm1", x3, wts["norm1_g"], sc1, sh1)
    proj3 = _mm_host(rides, rode, "proj", f"l{i}_proj", h1[0], wts["w_in"][:, :IN_MAIN], "nn")[None]
    qkv3 = mm(f"l{i}_qkv", h1[0], wts["w_in"][:, IN_MAIN:], "nn")[None]
    y_ssd, sv_ssd = ssd_forward(f"l{i}_ssd", proj3, sp)
    y_pool, sv_pool = pool_forward(f"l{i}_pool", proj3, wts["wbd"], wts["pool_scale"])
    y_att, res_att = attention_forward(f"l{i}", qkv3, cs3, sn3)
    mix = jnp.concatenate([y_ssd, y_pool, y_att], axis=-1)
    out, x1 = _mm_host(rides, rode, "wout", f"l{i}_wout", mix[0], wts["w_out"], "nn", res=x3[0], gate=g1)
    x1 = x1[None]
    h2 = norm_mod_forward(f"l{i}_norm2", x1, wts["norm2_g"], sc2, sh2)
    up3 = _mm_host(rides, rode, "up", f"l{i}_up", h2[0], wts["ffn_up"], "nn")[None]
    act, sv_ffn = ffn_mid_forward(f"l{i}_ffn", up3, wts["ffn_conv_w"], wts["ffn_conv_b"])
    dn, x2 = _mm_host(rides, rode, "down", f"l{i}_down", act[0], wts["ffn_down"], "nn", res=x1[0], gate=g2)
    keep = dict(x=x3, h1=h1, proj3=proj3, qkv3=qkv3, sv_ssd=sv_ssd, sv_pool=sv_pool, res_att=res_att, mix=mix, out=out[None],
                x1=x1, h2=h2, up3=up3, act=act, sv_ffn=sv_ffn, dn=dn[None])
    return x2[None], keep, rode


def _layer_backward(i, dx2, keep, modv, wts, sp, cs3, sn3, rides=None):
    rode = {}
    sh1, sc1, g1, sh2, sc2, g2 = modv
    k = keep
    d_dn, d_g2 = gate_backward(f"l{i}_gate2_b", k["dn"], g2, dx2)
    d_act = _mm_host(rides, rode, "down_bx", f"l{i}_down_bx", d_dn[0], wts["ffn_down"], "nt")
    g_down = mm(f"l{i}_down_bw", k["act"][0], d_dn[0], "tn")
    (d_up,), dv_ffn = ffn_mid_backward(f"l{i}_ffn_b", k["up3"], wts["ffn_conv_w"], wts["ffn_conv_b"], k["sv_ffn"], d_act[None])
    d_h2 = _mm_host(rides, rode, "up_bx", f"l{i}_up_bx", d_up[0], wts["ffn_up"], "nt")
    g_up = _mm_host(rides, rode, "up_bw", f"l{i}_up_bw", k["h2"][0], d_up[0], "tn", tn=_FFN_CW,
                    into=((4, D_MODEL, _FFN_CW), lambda r, c: ((c % 2) * 2 + c // 2, r, 0)))
    dx1, (d_n2, d_sc2, d_sh2) = norm_mod_backward(f"l{i}_norm2_b", k["x1"], wts["norm2_g"], sc2, sh2, d_h2[None], dx2)
    d_out, d_g1 = gate_backward(f"l{i}_gate1_b", k["out"], g1, dx1)
    d_mix = _mm_host(rides, rode, "wout_bx", f"l{i}_wout_bx", d_out[0], wts["w_out"], "nt")[None]
    g_wout = mm(f"l{i}_wout_bw", k["mix"][0], d_out[0], "tn")
    (dz, dxs, dbm, dcm, ddt), dv_ssd = ssd_backward(f"l{i}_ssd_b", k["proj3"], sp, k["sv_ssd"], d_mix)
    (du_pool,), (d_wbd, d_pscale) = pool_backward(f"l{i}_pool_b", k["proj3"], wts["wbd"], wts["pool_scale"], k["sv_pool"], d_mix)
    d_qkv = attention_backward(f"l{i}", k["qkv3"], cs3, sn3, k["res_att"], d_mix)
    d_proj = jnp.concatenate([dz[0], dxs[0], dbm[0], dcm[0], du_pool[0], (ddt[0] + ddt[1]).astype(BF16), d_qkv[0]], axis=-1)
    d_h1 = mm(f"l{i}_proj_bx", d_proj, wts["w_in"], "nt")
    g_win = mm(f"l{i}_proj_bw", k["h1"][0], d_proj, "tn")
    dx, (d_n1, d_sc1, d_sh1) = norm_mod_backward(f"l{i}_norm1_b", k["x"], wts["norm1_g"], sc1, sh1, d_h1[None], dx1)
    dcwx, dcbx, dcwb, dcbb, dcwc, dcbc, ddtb, dalog, ddsk, dng = dv_ssd
    small = dict(
        norm1_g=d_n1[0], norm2_g=d_n2[0],
        ssd_conv_w=jnp.concatenate([dcwx[:, :512], dcwb[:, 512:768], dcwc[:, 768:]], axis=1),
        ssd_conv_b=jnp.concatenate([dcbx[0, :512], dcbb[0, 512:768], dcbc[0, 768:]]),
        ssd_dt_bias=ddtb[0, :8], ssd_a_log=dalog[0, :8], ssd_d=ddsk[0, :8], ssd_norm_g=dng[0],
        pool_w=jnp.stack([d_wbd[64 * g:64 * g + 64, 64 * g:64 * g + 64] for g in range(4)]), pool_scale=d_pscale[0],
        ffn_conv_w=_ffn_block_perm(dv_ffn[0]), ffn_conv_b=_ffn_block_perm(dv_ffn[1][0]),
    )
    dmod = jnp.concatenate([d_sh1[0], d_sc1[0], d_g1[0], d_sh2[0], d_sc2[0], d_g2[0]])
    by_chip = [jnp.stack(_w_in_chip_cols(g_win)), g_wout.reshape(4, D_MODEL // 4, D_MODEL), g_up,
               g_down.reshape(4, FFN_DIM // 4, D_MODEL)]
    return dx, by_chip, small, dmod, rode


def kernel(x, c, positions, ada_w, ada_b, norm1_g, w_in, ssd_conv_w, ssd_conv_b, ssd_dt_bias, ssd_a_log, ssd_d, ssd_norm_g, pool_w, pool_scale, w_out, norm2_g, ffn_up, ffn_conv_w, ffn_conv_b, ffn_down, final_g, loss_target, m_ada_w, m_ada_b, m_norm1_g, m_w_in, m_ssd_conv_w, m_ssd_conv_b, m_ssd_dt_bias, m_ssd_a_log, m_ssd_d, m_ssd_norm_g, m_pool_w, m_pool_scale, m_w_out, m_norm2_g, m_ffn_up, m_ffn_conv_w, m_ffn_conv_b, m_ffn_down, m_final_g, v_ada_w, v_ada_b, v_norm1_g, v_w_in, v_ssd_conv_w, v_ssd_conv_b, v_ssd_dt_bias, v_ssd_a_log, v_ssd_d, v_ssd_norm_g, v_pool_w, v_pool_scale, v_w_out, v_norm2_g, v_ffn_up, v_ffn_conv_w, v_ffn_conv_b, v_ffn_down, v_final_g):
    args = dict(locals())
    w = {n: args[n] for n in _WEIGHTS}
    m = {n: args["m_" + n] for n in _WEIGHTS}
    v = {n: args["v_" + n] for n in _WEIGHTS}
    d = D_MODEL
    me = (lax.axis_index("x"), lax.axis_index("y"), lax.axis_index("c"))
    chip, dev = _chip(me), _dev(me)

    shapes0 = [c.shape, ssd_conv_w.shape, ffn_conv_w.shape]
    g0 = allgather8("gather_c_conv", _pack([c, ssd_conv_w, ffn_conv_w]))
    c16 = jnp.pad(g0[:, :d // 128, :].reshape(8, d), ((0, 8), (0, 0)))
    by_chip = [_unpack(g0[2 * j], shapes0) for j in range(4)]
    conv_w_full = jnp.concatenate([p[1] for p in by_chip], axis=-1)
    fconv_w_full = jnp.concatenate([p[2] for p in by_chip], axis=-1)

    modp = ada_forward("ada_fwd", c16, ada_w)[:, :8]
    g1 = allgather8("gather_mod", _pack([modp]))
    modfull = jnp.concatenate([_unpack(g1[2 * j], [modp.shape])[0] for j in range(4)], axis=-1)
    mod = lax.dynamic_index_in_dim(modfull, dev, axis=1, keepdims=False) + ada_b
    modv = [[mod[i, q * d:(q + 1) * d][None] for q in range(6)] for i in range(DEPTH)]

    shards = [w[n].astype(BF16) for n in _BIG]

    def assemble(layer, got):
        parts = [[jnp.where(chip == j, sh[layer], a[j]) for j in range(4)] for sh, a in zip(shards, got)]
        return dict(w_in=_w_in_from_chips(parts[0]), w_out=jnp.concatenate(parts[1], axis=0),
                    ffn_up=jnp.concatenate([parts[2][j] for j in FFN_BLOCK_ORDER], axis=1), ffn_down=jnp.concatenate(parts[3], axis=0))

    cs3, sn3 = rope_tables(positions[0])
    eye4 = jnp.eye(4, dtype=F32)
    wts, sps = [], []
    for i in range(DEPTH):
        wts.append(dict(
            norm1_g=norm1_g[i][None], norm2_g=norm2_g[i][None], pool_scale=pool_scale[i][None],
            wbd=(eye4[:, None, :, None] * pool_w[i][:, :, None, :]).reshape(POOL_W, POOL_W),
            ffn_conv_w=_ffn_block_perm(fconv_w_full[i]), ffn_conv_b=_ffn_block_perm(ffn_conv_b[i])[None]))
        sps.append(dict(cw=conv_w_full[i], cb=ssd_conv_b[i][None], dtb=_pad_lanes(ssd_dt_bias[i]), alog=_pad_lanes(ssd_a_log[i]),
                        dsk=_pad_lanes(ssd_d[i]), ng=ssd_norm_g[i][None]))

    wts[0].update(assemble(0, ride_alone("gather_w0", gather_ride(0, shards))))
    hosts = dict(wout=0, proj=1, up=2, down=3)
    x1_, keep0, rode = _layer_forward(0, x, modv[0], wts[0], sps[0], cs3, sn3,
                                      rides={h: gather_ride(1, [shards[k]]) for h, k in hosts.items()})
    got1 = [None] * 4
    for h, k in hosts.items():
        got1[k] = rode[h][0]
    wts[1].update(assemble(1, got1))
    xc, keep1, _ = _layer_forward(1, x1_, modv[1], wts[1], sps[1], cs3, sn3)
    keeps = [keep0, keep1]
    lossblk, dx, d_final = final_loss("final_loss", xc, loss_target, final_g[None])
    loss = lax.psum(lossblk[0, 0], ("x", "y", "c"))

    small_g, dmods, part_sum, from_chips = [None] * DEPTH, [None] * DEPTH, [None] * DEPTH, [None] * DEPTH

    def owner_sum(layer, by_chip):
        theirs = ride_alone(f"to_owner{layer}", to_owner_ride(layer, by_chip))
        return [add_arrays(f"sum_cores{layer}_{n}", [g, t], BF16) for n, g, t in zip(_BIG, by_chip, theirs)]

    dx, by_chip1, small_g[1], dmods[1], _ = _layer_backward(1, dx, keeps[1], modv[1], wts[1], sps[1], cs3, sn3)
    part_sum[1] = owner_sum(1, by_chip1)
    hosts_b = dict(up_bw=0, wout_bx=1, up_bx=2, down_bx=3)
    dx, by_chip0, small_g[0], dmods[0], rode = _layer_backward(
        0, dx, keeps[0], modv[0], wts[0], sps[0], cs3, sn3, rides={h: scatter_ride(1, [part_sum[1][k]]) for h, k in hosts_b.items()})
    from_chips[1] = [None] * 4
    for h, k in hosts_b.items():
        from_chips[1][k] = rode[h][0]
    part_sum[0] = owner_sum(0, by_chip0)
    from_chips[0] = ride_alone("scatter_g0", scatter_ride(0, part_sum[0]))
    mine = [sum_chips_mine(f"sum_chips_{n}", part_sum[0][k], from_chips[0][k], part_sum[1][k], from_chips[1][k])
            for k, n in enumerate(_BIG)]
    reduced = swap_layers("swap_r", mine, me[2])
    grads = {n: jnp.stack(r) for n, r in zip(_BIG, reduced)}

    part = dict(ada_b=jnp.stack(dmods), final_g=d_final[0])
    for n in _SMALL:
        if n not in part:
            part[n] = jnp.stack([small_g[i][n] for i in range(DEPTH)])
    full_shapes = [part[n].shape for n in _SMALL]
    gs = allgather8("gather_small", _pack([part[n] for n in _SMALL]))
    tot = _unpack(sum_slots("sum_small", gs, 8)[0], full_shapes)
    small_tot = dict(zip(_SMALL, tot))
    dmod_all = gs[:, :DEPTH * 6 * d // 128, :].reshape(8, DEPTH, 6 * d)
    for n, ncol in _COL_SHARDED_SMALL.items():
        small_tot[n] = lax.dynamic_slice_in_dim(small_tot[n], chip * ncol, ncol, axis=2)
    grads.update(small_tot)

    ncol = ada_w.shape[2]
    dm = lax.dynamic_slice_in_dim(dmod_all, chip * ncol, ncol, axis=2).transpose(1, 0, 2)
    upd = {}
    g_ada, *upd["ada_w"] = ada_backward("ada_bwd", c16, jnp.pad(dm, ((0, 0), (0, 8), (0, 0))), ada_w, m["ada_w"], v["ada_w"])
    grads["ada_w"] = g_ada

    for n in _BIG:
        upd[n] = adamw(f"adam_{n}", w[n], grads[n], m[n], v[n])
    shapes_s = [w[n].shape for n in _SMALL]
    packed = [_pack([src[n] for n in _SMALL]) for src in (w, grads, m, v)]
    outs_s = [_unpack(o, shapes_s) for o in adamw("adam_small", *packed)]
    for q, n in enumerate(_SMALL):
        upd[n] = [outs_s[0][q], outs_s[1][q], outs_s[2][q]]

    return (loss, dx, *[grads[n] for n in _WEIGHTS], *[upd[n][0] for n in _WEIGHTS], *[upd[n][1] for n in _WEIGHTS],
            *[upd[n][2] for n in _WEIGHTS])


class Ride:
    def __init__(self, ins, out_shapes, nsem, start, finish):
        self.ins, self.out_shapes, self.nsem, self.start, self.finish = ins, out_shapes, nsem, start, finish

    def specs(self):
        hbm = pl.BlockSpec(memory_space=pl.ANY)
        return [hbm] * len(self.ins), [hbm] * len(self.out_shapes), [pltpu.SemaphoreType.DMA((self.nsem,))] * 2

    def begin(self, in_refs, out_refs, sems, cond=None):
        me = (lax.axis_index("x"), lax.axis_index("y"), lax.axis_index("c"))
        go = lambda: self.start(in_refs, out_refs, sems[0], sems[1], me)
        go() if cond is None else pl.when(cond)(go)

    def end(self, in_refs, out_refs, sems, cond=None):
        me = (lax.axis_index("x"), lax.axis_index("y"), lax.axis_index("c"))
        go = lambda: self.finish(in_refs, out_refs, sems[0], sems[1], me)
        go() if cond is None else pl.when(cond)(go)


def ride_alone(name, ride):
    ni, no = len(ride.ins), len(ride.out_shapes)

    def body(*refs):
        ride.begin(refs[:ni], refs[ni:ni + no], refs[ni + no:])
        ride.end(refs[:ni], refs[ni:ni + no], refs[ni + no:])

    in_specs, out_specs, scratch = ride.specs()
    return list(pl.pallas_call(body, name=name, in_specs=in_specs, out_specs=out_specs, out_shape=ride.out_shapes,
                               scratch_shapes=scratch)(*ride.ins))


def mm(name, a, b, mode, out_dtype=F32, res=None, gate=None, tm=1408, tn=1536, tk=1408, into=None, ride=None):
    if mode == "nn":
        (m, k), n = a.shape, b.shape[1]
    elif mode == "nt":
        (m, k), n = a.shape, b.shape[0]
    else:
        (k, m), n = a.shape, b.shape[1]
    tm, tn, tk = _tile(m, tm), _tile(n, tn), _tile(k, tk)
    ni, nj, nk = m // tm, n // tn, k // tk
    a_spec = pl.BlockSpec((tk, tm), lambda i, j, q: (q, i)) if mode == "tn" else pl.BlockSpec((tm, tk), lambda i, j, q: (i, q))
    b_spec = pl.BlockSpec((tn, tk), lambda i, j, q: (j, q)) if mode == "nt" else pl.BlockSpec((tk, tn), lambda i, j, q: (q, j))
    o_spec = pl.BlockSpec((tm, tn), lambda i, j, q: (i, j))
    fused = res is not None
    lead = 0 if into is None else len(into[0]) - 2
    first = (0,) * lead + (slice(None), slice(None))
    ins, in_specs = [a, b], [a_spec, b_spec]
    out_shape, out_specs = [jax.ShapeDtypeStruct((m, n), out_dtype)], [o_spec]
    if fused:
        ins += [res, gate]
        in_specs += [o_spec, pl.BlockSpec((1, tn), lambda i, j, q: (0, j))]
        out_shape.append(jax.ShapeDtypeStruct((m, n), F32))
        out_specs.append(o_spec)
    if into is not None:
        shape, omap = into
        out_shape = [jax.ShapeDtypeStruct(shape, out_dtype)]
        out_specs = [pl.BlockSpec((1,) * lead + (tm, tn), lambda i, j, q: omap(i, j))]
    n_in, n_out = len(ins), len(out_shape)
    scratch = [pltpu.VMEM((tm, tn), F32)]
    if ride is not None:
        r_in, r_out, r_scr = ride.specs()
        ins, in_specs = ins + list(ride.ins), in_specs + r_in
        out_shape, out_specs = out_shape + list(ride.out_shapes), out_specs + r_out
        scratch = scratch + r_scr

    def body(*refs):
        a_ref, b_ref = refs[:2]
        o_ref = refs[len(ins)]
        acc = refs[len(ins) + len(out_shape)]
        i, j, q = pl.program_id(0), pl.program_id(1), pl.program_id(2)
        at = lambda x, y, z: jnp.logical_and(jnp.logical_and(i == x, j == y), q == z)
        r_refs = (refs[n_in:len(ins)], refs[len(ins) + n_out:len(ins) + len(out_shape)], refs[len(ins) + len(out_shape) + 1:])
        if ride is not None:
            ride.begin(*r_refs, at(0, 0, 0))

        @pl.when(q == 0)
        def _():
            acc[...] = jnp.zeros(acc.shape, F32)

        acc[...] += _mxu(a_ref[...], b_ref[...], mode)

        @pl.when(q == nk - 1)
        def _():
            o_ref[first] = acc[...].astype(o_ref.dtype)
            if fused:
                refs[len(ins) + 1][...] = refs[2][...] + refs[3][...] * acc[...]

        if ride is not None:
            ride.end(*r_refs, at(ni - 1, nj - 1, nk - 1))

    sem = ("arbitrary",) * 3 if ride is not None else ("parallel", "parallel", "arbitrary")
    out = pl.pallas_call(
        body, name=name, grid=(ni, nj, nk), in_specs=in_specs, out_specs=out_specs, out_shape=out_shape, scratch_shapes=scratch,
        compiler_params=pltpu.CompilerParams(dimension_semantics=sem, vmem_limit_bytes=VMEM_LIMIT_BYTES),
    )(*ins)
    main = tuple(out[:n_out]) if fused else out[0]
    return main if ride is None else (main, list(out[n_out:]))


def add_arrays(name, arrs, out_dtype=F32):
    nb, r, c = arrs[0].shape
    t = _tile(r, 256, 8)
    (out,), _ = scan_fwd(name, _sum_fn, nb=nb, nchunk=r // t, t=t, rows=[Row(a, fb=lambda b: b) for a in arrs], vecs=[], carries=[],
                         outs=[out_row((nb, r, c), out_dtype, fb=lambda b: b)], save=False)
    return out


def _sum_chips_mine_fn(ci, b, carries, rows, vecs):
    mine_layer = lax.axis_index("c")
    chip = 2 * lax.axis_index("x") + lax.axis_index("y")
    tot = None
    for j in range(4):
        own = jnp.where(mine_layer == 0, rows[j], rows[8 + j])
        sent = jnp.where(mine_layer == 0, rows[4 + j], rows[12 + j])
        term = jnp.where(chip == j, own, sent)
        tot = term if tot is None else tot + term
    return [], [tot]


def sum_chips_mine(name, p0, q0, p1, q1):
    _, r, c = p0.shape
    t = _tile(r, 256, 8)
    rows = [Row(a, fb=(lambda b, j=j: j)) for a in (p0, q0, p1, q1) for j in range(4)]
    (out,), _ = scan_fwd(name, _sum_chips_mine_fn, nb=1, nchunk=r // t, t=t, rows=rows, vecs=[], carries=[],
                         outs=[out_row((1, r, c))], save=False)
    return out[0]


def _remote(src, dst, send_sems, recv_sems, k, to):
    return pltpu.make_async_remote_copy(src_ref=src, dst_ref=dst, send_sem=send_sems.at[k], recv_sem=recv_sems.at[k],
                                        device_id=to, device_id_type=MESH)


def gather_ride(layer, shards):
    na = len(shards)

    def start(ins, outs, ss, rs, me):
        @pl.when(me[2] == layer)
        def _():
            for k in range(na):
                for p, mask in enumerate(CHIP_PEERS):
                    _remote(ins[k].at[layer], outs[k].at[_chip(me)], ss, rs, 6 * k + p, _flip(mask, me)).start()

    def finish(ins, outs, ss, rs, me):
        sibling = _flip(SIBLING[0], me)

        @pl.when(me[2] == layer)
        def _():
            for k in range(na):
                for p, mask in enumerate(CHIP_PEERS):
                    slot = outs[k].at[_chip(_flip(mask, me))]
                    _remote(ins[k].at[layer], slot, ss, rs, 6 * k + p, _flip(mask, me)).wait_recv()
                    _remote(slot, slot, ss, rs, 6 * k + 3 + p, sibling).start()
            for k in range(na):
                for p, mask in enumerate(CHIP_PEERS):
                    slot = outs[k].at[_chip(_flip(mask, me))]
                    _remote(ins[k].at[layer], slot, ss, rs, 6 * k + p, _flip(mask, me)).wait_send()
                    _remote(slot, slot, ss, rs, 6 * k + 3 + p, sibling).wait_send()

        @pl.when(me[2] != layer)
        def _():
            for k in range(na):
                for p, mask in enumerate(CHIP_PEERS):
                    slot = outs[k].at[_chip(_flip(mask, me))]
                    _remote(slot, slot, ss, rs, 6 * k + 3 + p, sibling).wait_recv()

    return Ride(list(shards), [jax.ShapeDtypeStruct((4,) + a.shape[1:], a.dtype) for a in shards], 6 * na, start, finish)


def scatter_ride(layer, parts):
    na = len(parts)

    def start(ins, outs, ss, rs, me):
        @pl.when(me[2] == layer)
        def _():
            for k in range(na):
                for p, mask in enumerate(CHIP_PEERS):
                    peer = _flip(mask, me)
                    _remote(ins[k].at[_chip(peer)], outs[k].at[_chip(me)], ss, rs, 3 * k + p, peer).start()

    def finish(ins, outs, ss, rs, me):
        @pl.when(me[2] == layer)
        def _():
            for k in range(na):
                for p, mask in enumerate(CHIP_PEERS):
                    peer = _flip(mask, me)
                    _remote(ins[k].at[_chip(peer)], outs[k].at[_chip(peer)], ss, rs, 3 * k + p, peer).wait_recv()
                    _remote(ins[k].at[_chip(peer)], outs[k].at[_chip(me)], ss, rs, 3 * k + p, peer).wait_send()

    return Ride(list(parts), [jax.ShapeDtypeStruct(a.shape, a.dtype) for a in parts], 3 * na, start, finish)


def to_owner_ride(layer, arrays):
    na = len(arrays)

    def start(ins, outs, ss, rs, me):
        @pl.when(me[2] != layer)
        def _():
            for k in range(na):
                _remote(ins[k], outs[k], ss, rs, k, _flip(SIBLING[0], me)).start()

    def finish(ins, outs, ss, rs, me):
        for k in range(na):
            cp = _remote(ins[k], outs[k], ss, rs, k, _flip(SIBLING[0], me))
            pl.when(me[2] != layer)(cp.wait_send)
            pl.when(me[2] == layer)(cp.wait_recv)

    return Ride(list(arrays), [jax.ShapeDtypeStruct(a.shape, a.dtype) for a in arrays], na, start, finish)


def _w_in_from_chips(a):
    c2 = a[2]
    pad = jnp.zeros((c2.shape[0], IN_WP - IN_W), c2.dtype)
    return jnp.concatenate([a[0], a[1], c2[:, :252], c2[:, 260:516], c2[:, 252:260], pad, c2[:, 516:], a[3]], axis=1)


def _mm_host(rides, rode, key, *args, **kw):
    if rides is None or key not in rides:
        return mm(*args, **kw)
    main, rode[key] = mm(*args, ride=rides[key], **kw)
    return main
```

```python
import functools
import math

import numpy as np
import jax
import jax.numpy as jnp
from jax import lax
from jax.experimental import pallas as pl
from jax.experimental.pallas import tpu as pltpu

F32 = jnp.float32
BF16 = jnp.bfloat16
HI = lax.Precision.HIGHEST
MESH = pl.DeviceIdType.MESH

D_MODEL = 1024
SEQ = 4096
DEPTH = 2
SSD_INNER = 512
SSD_HEADS = 8
SSD_STATE = 128
POOL_W = 256
POOL_WINDOWS = (2, 4, 8, 16)
ATT_W = 256
ATT_HEADS = 4
ATT_HEAD_DIM = 64
ATT_PATTERNS = ((128, 1), (512, 4), (2048, 16))
ATT_BLOCK = 128
ROT_DIM = 16
ROPE_THETA = 500000.0
IN_W = 2568
IN_WP = 2688
IN_MAIN = 1920
FFN_DIM = 2816
NORM_EPS = 1e-6
ADAM_LR, ADAM_B1, ADAM_B2, ADAM_EPS, ADAM_WD, ADAM_STEP = 0.001, 0.9, 0.999, 1e-08, 0.01, 10

VMEM_LIMIT_BYTES = 56 * 1024 * 1024
NEG = -1e30


def _mxu(a, b, mode):
    dims = {"nn": ((1,), (0,)), "nt": ((1,), (1,)), "tn": ((0,), (0,))}[mode]
    return lax.dot_general(a.astype(BF16), b.astype(BF16), (dims, ((), ())), preferred_element_type=F32)


@functools.partial(jax.custom_vjp, nondiff_argnums=(2,))
def _bdot(a, b, mode):
    return _mxu(a, b, mode)


def _bdot_fwd(a, b, mode):
    return _mxu(a, b, mode), (a, b)


def _bdot_bwd(mode, res, g):
    a, b = res
    if mode == "nn":
        return _mxu(g, b, "nt"), _mxu(a, g, "tn")
    if mode == "nt":
        return _mxu(g, b, "nn"), _mxu(g, a, "tn")
    return _mxu(b, g, "nt"), _mxu(a, g, "nn")


_bdot.defvjp(_bdot_fwd, _bdot_bwd)


def _fxu(a, b, mode):
    dims = {"nn": ((1,), (0,)), "nt": ((1,), (1,)), "tn": ((0,), (0,))}[mode]
    return lax.dot_general(a, b, (dims, ((), ())), precision=HI, preferred_element_type=F32)


@functools.partial(jax.custom_vjp, nondiff_argnums=(2,))
def _fdot(a, b, mode):
    return _fxu(a, b, mode)


def _fdot_fwd(a, b, mode):
    return _fxu(a, b, mode), (a, b)


def _fdot_bwd(mode, res, g):
    a, b = res
    if mode == "nn":
        return _fxu(g, b, "nt"), _fxu(a, g, "tn")
    if mode == "nt":
        return _fxu(g, b, "nn"), _fxu(g, a, "tn")
    return _fxu(b, g, "nt"), _fxu(a, g, "nn")


_fdot.defvjp(_fdot_fwd, _fdot_bwd)


def _iota(shape, dim):
    return lax.broadcasted_iota(jnp.int32, shape, dim)


def _make_shift(h):
    @functools.partial(jax.custom_vjp, nondiff_argnums=(2,))
    def shift(halo, cur, k):
        if k == 0:
            return cur
        full = jnp.concatenate([halo, cur], axis=0)
        return pltpu.roll(full, k, 0)[h:]

    def fwd(halo, cur, k):
        return shift(halo, cur, k), None

    def bwd(k, _, g):
        t, w = g.shape
        if k == 0:
            return jnp.zeros((h, w), F32), g
        d_cur = jnp.where(_iota((t, w), 0) < t - k, pltpu.roll(g, t - k, 0), 0.0)
        top = g[:h]
        d_halo = jnp.where(_iota((h, w), 0) >= h - k, pltpu.roll(top, h - k, 0) if k < h else top, 0.0)
        return d_halo, d_cur

    shift.defvjp(fwd, bwd)
    return shift


_shift8 = _make_shift(8)
_shift16 = _make_shift(16)


def _make_tail(h):
    @jax.custom_vjp
    def tail(x):
        return x[x.shape[0] - h:]

    def fwd(x):
        return tail(x), x.shape[0]

    def bwd(t, g):
        return (jnp.concatenate([jnp.zeros((t - h, g.shape[1]), F32), g], axis=0),)

    tail.defvjp(fwd, bwd)
    return tail


_tail8 = _make_tail(8)
_tail16 = _make_tail(16)


@jax.custom_vjp
def _cumsum_rows(x):
    t = x.shape[0]
    row, s = _iota(x.shape, 0), 1
    while s < t:
        x = x + jnp.where(row >= s, pltpu.roll(x, s, 0), 0.0)
        s *= 2
    return x


def _cumsum_rows_fwd(x):
    return _cumsum_rows(x), None


def _cumsum_rows_bwd(_, g):
    t = g.shape[0]
    row, s = _iota(g.shape, 0), 1
    while s < t:
        g = g + jnp.where(row < t - s, pltpu.roll(g, t - s, 0), 0.0)
        s *= 2
    return (g,)


_cumsum_rows.defvjp(_cumsum_rows_fwd, _cumsum_rows_bwd)


@jax.custom_vjp
def _rot_pairs(t):
    e = _iota(t.shape, 1) % ATT_HEAD_DIM
    n = t.shape[1]
    return jnp.where(e < 8, -pltpu.roll(t, n - 8, 1), jnp.where(e < 16, pltpu.roll(t, 8, 1), 0.0))


def _rot_pairs_fwd(t):
    return _rot_pairs(t), None


def _rot_pairs_bwd(_, g):
    e = _iota(g.shape, 1) % ATT_HEAD_DIM
    n = g.shape[1]
    return (pltpu.roll(jnp.where(e < 8, -g, 0.0), 8, 1) + pltpu.roll(jnp.where(jnp.logical_and(e >= 8, e < 16), g, 0.0), n - 8, 1),)


_rot_pairs.defvjp(_rot_pairs_fwd, _rot_pairs_bwd)


def _make_thirds():
    @jax.custom_vjp
    def thirds(x):
        w = x.shape[1] // 3
        return x[:, :w], x[:, w:2 * w], x[:, 2 * w:]

    def fwd(x):
        return thirds(x), None

    def bwd(_, g):
        return (jnp.concatenate(g, axis=1),)

    thirds.defvjp(fwd, bwd)
    return thirds


_thirds = _make_thirds()


def _rowk(w, k):
    return jnp.sum(jnp.where(_iota(w.shape, 0) == k, w, 0.0), axis=0, keepdims=True)


def _silu(x):
    return x * (0.5 * jnp.tanh(0.5 * x) + 0.5)


def _softplus(x):
    return jnp.maximum(x, 0.0) + jnp.log(1.0 + jnp.exp(-jnp.abs(x)))


def _tile(dim, target, unit=128):
    if dim <= target:
        return dim
    best = None
    for t in range(unit, target + 1, unit):
        if dim % t == 0:
            best = t
    assert best is not None, (dim, target)
    return best


class Ride:
    def __init__(self, ins, out_shapes, nsem, start, finish):
        self.ins, self.out_shapes, self.nsem, self.start, self.finish = ins, out_shapes, nsem, start, finish

    def specs(self):
        hbm = pl.BlockSpec(memory_space=pl.ANY)
        return [hbm] * len(self.ins), [hbm] * len(self.out_shapes), [pltpu.SemaphoreType.DMA((self.nsem,))] * 2

    def begin(self, in_refs, out_refs, sems, cond=None):
        me = (lax.axis_index("x"), lax.axis_index("y"), lax.axis_index("c"))
        go = lambda: self.start(in_refs, out_refs, sems[0], sems[1], me)
        go() if cond is None else pl.when(cond)(go)

    def end(self, in_refs, out_refs, sems, cond=None):
        me = (lax.axis_index("x"), lax.axis_index("y"), lax.axis_index("c"))
        go = lambda: self.finish(in_refs, out_refs, sems[0], sems[1], me)
        go() if cond is None else pl.when(cond)(go)


class _Riders:
    def reset(self):
        self.booked, self.done = {}, {}

    def book(self, host, ride):
        assert host not in self.booked, host
        self.booked[host] = ride

    def take(self, host):
        return self.booked.pop(host, None)

    def result(self, host):
        return self.done[host]


RIDERS = _Riders()
RIDERS.reset()


class Row:
    def __init__(self, arr, w=None, fb=None, fc=None, diff=True, slot=False, dcols=None, dfc=None, ddtype=F32, view=None):
        self.ddtype = ddtype
        self.view = view
        self.arr = arr
        self.w = arr.shape[2] if w is None else w
        self.fb = (lambda b: 0) if fb is None else fb
        self.fc = (lambda b: 0) if fc is None else fc
        self.diff = diff
        self.slot = slot
        self.dcols = dcols
        self.dfc = dfc


class Vec:
    def __init__(self, arr, w=None, fc=None, diff=True):
        self.arr = arr
        self.w = arr.shape[1] if w is None else w
        self.fc = fc
        self.diff = diff


def _row_spec(r, t, nchunk, reverse):
    shape = (1, t, r.w) if r.view is None else (1, t // r.view, r.view * r.w)
    if reverse:
        return pl.BlockSpec(shape, lambda b, i, r=r: (r.fb(b), nchunk - 1 - i, r.fc(b)))
    return pl.BlockSpec(shape, lambda b, i, r=r: (r.fb(b), i, r.fc(b)))


def _load_row(ref, r, t, scr):
    if r.view is None:
        return ref[0]
    d, w = r.view, r.w
    for q in range(d):
        for j in range(w // 128):
            scr[j, pl.ds(q, t // d, stride=d), :] = ref[0, :, q * w + 128 * j:q * w + 128 * (j + 1)].astype(F32)
    return jnp.concatenate([scr[j] for j in range(w // 128)], axis=1)


def _store_row(ref, r, t, scr, val):
    if r.view is None:
        ref[0] = val.astype(ref.dtype)
        return
    d, w = r.view, r.w
    for j in range(w // 128):
        scr[j] = val[:, 128 * j:128 * (j + 1)]
    for q in range(d):
        for j in range(w // 128):
            ref[0, :, q * w + 128 * j:q * w + 128 * (j + 1)] = scr[j, pl.ds(q, t // d, stride=d), :].astype(ref.dtype)


def _view_scratch(specs, t):
    ws = [r.w for r in specs if r.view is not None]
    return [pltpu.VMEM((max(ws) // 128, t, 128), F32)] if ws else []


def _vec_spec(v):
    if v.fc is None:
        return pl.BlockSpec(v.arr.shape, lambda b, i: (0, 0))
    return pl.BlockSpec((v.arr.shape[0], v.w), lambda b, i, v=v: (0, v.fc(b)))


def _cparams():
    return pltpu.CompilerParams(dimension_semantics=("arbitrary", "arbitrary"), vmem_limit_bytes=VMEM_LIMIT_BYTES)


def scan_fwd(name, fn, *, nb, nchunk, t, rows, vecs, carries, outs, save):
    nr, nv, nc, no = len(rows), len(vecs), len(carries), len(outs)
    ns = nc if save else 0
    ride = RIDERS.take(name)
    r_in, r_out, r_scr = ride.specs() if ride else ([], [], [])

    def body(*refs):
        p = 0
        row_refs = refs[p:p + nr]; p += nr
        vec_refs = refs[p:p + nv]; p += nv
        ride_in = refs[p:p + len(r_in)]; p += len(r_in)
        out_refs = refs[p:p + no]; p += no
        save_refs = refs[p:p + ns]; p += ns
        ride_out = refs[p:p + len(r_out)]; p += len(r_out)
        car = refs[p:p + nc]; p += nc
        scr = refs[p] if stage else None
        sems = refs[p + len(stage):]
        b, i = pl.program_id(0), pl.program_id(1)
        if ride:
            ride.begin(ride_in, ride_out, sems, jnp.logical_and(b == 0, i == 0))
        if nc:
            @pl.when(i == 0)
            def _():
                for c_ref in car:
                    c_ref[...] = jnp.zeros(c_ref.shape, F32)
        cin = [c_ref[...] for c_ref in car]
        if save:
            for s_ref, cv in zip(save_refs, cin):
                s_ref[0, 0] = cv
        new_c, o = fn(i, b, cin, [_load_row(ref, r, t, scr) for ref, r in zip(row_refs, rows)], [v[...] for v in vec_refs])
        for c_ref, cv in zip(car, new_c):
            c_ref[...] = cv
        for o_ref, spec, ov in zip(out_refs, outs, o):
            _store_row(o_ref, spec, t, scr, ov)
        if ride:
            ride.end(ride_in, ride_out, sems, jnp.logical_and(b == nb - 1, i == nchunk - 1))

    stage = _view_scratch(list(rows) + list(outs), t)
    out_shape = [o.arr for o in outs]
    out_specs = [_row_spec(o, t, nchunk, False) for o in outs]
    if save:
        for cs in carries:
            out_shape.append(jax.ShapeDtypeStruct((nb, nchunk) + tuple(cs), F32))
            out_specs.append(pl.BlockSpec((1, 1) + tuple(cs), lambda b, i: (b, i, 0, 0)))
    res = pl.pallas_call(
        body, name=name, grid=(nb, nchunk),
        in_specs=[_row_spec(r, t, nchunk, False) for r in rows] + [_vec_spec(v) for v in vecs] + r_in,
        out_specs=out_specs + r_out, out_shape=out_shape + (list(ride.out_shapes) if ride else []),
        scratch_shapes=[pltpu.VMEM(tuple(cs), F32) for cs in carries] + stage + r_scr,
        compiler_params=_cparams(),
    )(*[r.arr for r in rows], *[v.arr for v in vecs], *(ride.ins if ride else []))
    if ride:
        RIDERS.done[name] = list(res[no + ns:])
    return list(res[:no]), list(res[no:no + ns])


def scan_bwd(name, fn, *, nb, nchunk, t, rows, vecs, carries, saved, douts, adds=None):
    adds = adds or {}
    nr, nv, nc, no = len(rows), len(vecs), len(carries), len(douts)
    dri = [k for k, r in enumerate(rows) if r.diff]
    dvi = [k for k, v in enumerate(vecs) if v.diff]
    add_keys = sorted(adds)
    na = len(add_keys)
    ride = RIDERS.take(name)
    r_in, r_out, r_scr = ride.specs() if ride else ([], [], [])

    def body(*refs):
        p = 0
        row_refs = refs[p:p + nr]; p += nr
        vec_refs = refs[p:p + nv]; p += nv
        save_refs = refs[p:p + nc]; p += nc
        dout_refs = refs[p:p + no]; p += no
        add_refs = refs[p:p + na]; p += na
        ride_in = refs[p:p + len(r_in)]; p += len(r_in)
        drow_refs = refs[p:p + len(dri)]; p += len(dri)
        dvec_refs = refs[p:p + len(dvi)]; p += len(dvi)
        ride_out = refs[p:p + len(r_out)]; p += len(r_out)
        dcar = refs[p:p + nc]; p += nc
        scr = refs[p] if stage else None
        sems = refs[p + len(stage):]
        b, ir = pl.program_id(0), pl.program_id(1)
        ci = nchunk - 1 - ir
        if ride:
            ride.begin(ride_in, ride_out, sems, jnp.logical_and(b == 0, ir == 0))
        if nc:
            @pl.when(ir == 0)
            def _():
                for c_ref in dcar:
                    c_ref[...] = jnp.zeros(c_ref.shape, F32)
        rows_v = [_load_row(ref, r, t, scr) for ref, r in zip(row_refs, rows)]
        vecs_v = [v[...] for v in vec_refs]
        cin = [s[0, 0] for s in save_refs]
        dc = [c_ref[...] for c_ref in dcar]
        dout_v = [_load_row(ref, r, t, scr).astype(F32) for ref, r in zip(dout_refs, douts)]

        def f(cs, dr, dv):
            rr, vv = list(rows_v), list(vecs_v)
            for k, idx in enumerate(dri):
                rr[idx] = dr[k]
            for k, idx in enumerate(dvi):
                vv[idx] = dv[k]
            return fn(ci, b, cs, rr, vv)

        _, vjp = jax.vjp(f, cin, [rows_v[k].astype(F32) for k in dri], [vecs_v[k].astype(F32) for k in dvi])
        dcin, drows, dvecs = vjp((dc, dout_v))
        for c_ref, cv in zip(dcar, dcin):
            c_ref[...] = cv
        for k, (o_ref, ov) in enumerate(zip(drow_refs, drows)):
            if dri[k] in adds:
                ov = ov + add_refs[add_keys.index(dri[k])][0].astype(F32)
            _store_row(o_ref, rows[dri[k]], t, scr, ov)
        for k, (o_ref, ov) in enumerate(zip(dvec_refs, dvecs)):
            first = (ir == 0) if vecs[dvi[k]].fc is not None else jnp.logical_and(ir == 0, b == 0)

            @pl.when(first)
            def _(o_ref=o_ref, ov=ov):
                o_ref[...] = ov

            @pl.when(jnp.logical_not(first))
            def _(o_ref=o_ref, ov=ov):
                o_ref[...] += ov

        if ride:
            ride.end(ride_in, ride_out, sems, jnp.logical_and(b == nb - 1, ir == nchunk - 1))

    stage = _view_scratch(list(rows) + list(douts), t)
    in_specs = ([_row_spec(r, t, nchunk, True) for r in rows] + [_vec_spec(v) for v in vecs]
                + [pl.BlockSpec((1, 1) + tuple(cs), lambda b, i: (b, nchunk - 1 - i, 0, 0)) for cs in carries]
                + [_row_spec(d, t, nchunk, True) for d in douts]
                + [_row_spec(adds[k], t, nchunk, True) for k in add_keys] + r_in)
    out_shape, out_specs = [], []
    for k in dri:
        r = rows[k]
        if r.slot:
            out_shape.append(jax.ShapeDtypeStruct((nb, r.arr.shape[1], r.w), r.ddtype))
            out_specs.append(pl.BlockSpec((1, t, r.w), lambda b, i: (b, nchunk - 1 - i, 0)))
        elif r.dcols is not None:
            out_shape.append(jax.ShapeDtypeStruct((r.arr.shape[0], r.arr.shape[1], r.dcols), r.ddtype))
            out_specs.append(pl.BlockSpec((1, t, r.w), lambda b, i, r=r: (r.fb(b), nchunk - 1 - i, r.dfc(b))))
        else:
            out_shape.append(jax.ShapeDtypeStruct(r.arr.shape, r.ddtype))
            out_specs.append(_row_spec(r, t, nchunk, True))
    for k in dvi:
        out_shape.append(jax.ShapeDtypeStruct(vecs[k].arr.shape, F32))
        out_specs.append(_vec_spec(vecs[k]))
    nd = len(dri) + len(dvi)
    res = pl.pallas_call(
        body, name=name, grid=(nb, nchunk), in_specs=in_specs, out_specs=out_specs + r_out,
        out_shape=out_shape + (list(ride.out_shapes) if ride else []),
        scratch_shapes=[pltpu.VMEM(tuple(cs), F32) for cs in carries] + stage + r_scr,
        compiler_params=_cparams(),
    )(*[r.arr for r in rows], *[v.arr for v in vecs], *saved, *[d.arr for d in douts], *[adds[k].arr for k in add_keys],
      *(ride.ins if ride else []))
    if ride:
        RIDERS.done[name] = list(res[nd:])
    return list(res[:len(dri)]), list(res[len(dri):nd])


def out_row(shape, dtype=F32, w=None, fb=None, fc=None):
    return Row(jax.ShapeDtypeStruct(shape, dtype), w, fb, fc)


def _conv(shift, halo, cur, w, bias, taps):
    y = bias
    for k in range(taps):
        y = y + _rowk(w, k) * shift(halo, cur, taps - 1 - k)
    return y


def _ssd_fn(ci, b, carries, rows, vecs):
    cx, cb_, cc, ht = carries
    z, xr, br, cr, dtr = rows
    cwx, cbx, cwb, cbb, cwc, cbc, dtb, alog, dsk, ng = vecs
    t = z.shape[0]
    xs = _silu(_conv(_shift8, cx, xr, cwx, cbx, 4))
    bm = _silu(_conv(_shift8, cb_, br, cwb, cbb, 4))
    cm = _silu(_conv(_shift8, cc, cr, cwc, cbc, 4))
    dt = _softplus(dtr + dtb)
    acol = _cumsum_rows(dt * (-jnp.exp(alog)))
    arow = acol.T
    r, c = _iota((t, t), 0), _iota((t, t), 1)
    causal = r >= c
    cbm = _bdot(cm, bm, "nt")
    lane, sub = _iota(acol.shape, 1), _iota(arow.shape, 0)
    colh = _iota(xs.shape, 1) // 64
    a, dtx, dx, acs = jnp.zeros(xs.shape, F32), jnp.zeros(xs.shape, F32), jnp.zeros((1, xs.shape[1]), F32), []
    for j in range(4):
        h = 4 * b + j
        ac = jnp.sum(jnp.where(lane == h, acol, 0.0), axis=1, keepdims=True)
        acs.append(ac)
        a = jnp.where(colh == j, ac, a)
        dtx = jnp.where(colh == j, jnp.sum(jnp.where(lane == h, dt, 0.0), axis=1, keepdims=True), dtx)
        dx = jnp.where(_iota(dx.shape, 1) // 64 == j, jnp.sum(jnp.where(_iota(dsk.shape, 1) == h, dsk, 0.0), axis=1, keepdims=True), dx)
    atot = jnp.sum(jnp.where(_iota(a.shape, 0) == t - 1, a, 0.0), axis=0, keepdims=True)
    x = xs * dtx
    ydiag = jnp.zeros(x.shape, F32)
    for j in range(4):
        ar = jnp.sum(jnp.where(sub == 4 * b + j, arow, 0.0), axis=0, keepdims=True)
        lmat = jnp.exp(jnp.where(causal, acs[j] - ar, NEG))
        ydiag = ydiag + _bdot(cbm * lmat, jnp.where(colh == j, x, 0.0), "nn")
    yoff = _bdot(cm, ht, "nn") * jnp.exp(a)
    ht_new = ht * jnp.exp(atot) + _bdot(bm, x * jnp.exp(atot - a), "tn")
    y = ydiag + yoff + dx * xs
    yz = y * _silu(z)
    yn = yz * lax.rsqrt(jnp.mean(yz * yz, axis=-1, keepdims=True) + NORM_EPS) * ng
    return [_tail8(xr), _tail8(br), _tail8(cr), ht_new], [yn]


_SSD_T = 256
_SSD_CARRIES = [(8, 256), (8, 128), (8, 128), (128, 256)]


def _ssd_io(proj3, p):
    own = lambda b: b
    rows = [Row(proj3, 256, fc=own, dcols=512, dfc=own, ddtype=BF16),
            Row(proj3, 256, fc=lambda b: 2 + b, dcols=512, dfc=own, ddtype=BF16),
            Row(proj3, 128, fc=lambda b: 8 + b, dcols=256, dfc=own, ddtype=BF16),
            Row(proj3, 128, fc=lambda b: 10 + b, dcols=256, dfc=own, ddtype=BF16),
            Row(proj3, 128, fc=lambda b: 14, slot=True)]
    vecs = [Vec(p["cw"], 256, lambda b: b), Vec(p["cb"], 256, lambda b: b),
            Vec(p["cw"], 128, lambda b: 4 + b), Vec(p["cb"], 128, lambda b: 4 + b),
            Vec(p["cw"], 128, lambda b: 6 + b), Vec(p["cb"], 128, lambda b: 6 + b),
            Vec(p["dtb"]), Vec(p["alog"]), Vec(p["dsk"]), Vec(p["ng"], 256, lambda b: b)]
    return rows, vecs


def ssd_forward(name, proj3, p):
    rows, vecs = _ssd_io(proj3, p)
    s = proj3.shape[1]
    (y,), saved = scan_fwd(name, _ssd_fn, nb=2, nchunk=s // _SSD_T, t=_SSD_T, rows=rows, vecs=vecs,
                           carries=_SSD_CARRIES, outs=[out_row((1, s, SSD_INNER), BF16, 256, fc=lambda b: b)], save=True)
    return y, saved


def ssd_backward(name, proj3, p, saved, dmix3):
    rows, vecs = _ssd_io(proj3, p)
    s = proj3.shape[1]
    drows, dvecs = scan_bwd(name, _ssd_fn, nb=2, nchunk=s // _SSD_T, t=_SSD_T, rows=rows, vecs=vecs,
                            carries=_SSD_CARRIES, saved=saved, douts=[Row(dmix3, 256, fc=lambda b: b)])
    return drows, dvecs


def _pool_fn(ci, b, carries, rows, vecs):
    (cu,) = carries
    (u,) = rows
    wbd, scale = vecs
    t = u.shape[0]
    pos = ci * t + _iota(u.shape, 0)
    grp = _iota(u.shape, 1) // 64
    acc, pooled, k = u, jnp.zeros(u.shape, F32), 1
    for gi, w in enumerate(POOL_WINDOWS):
        while k < w:
            acc = acc + _shift16(cu, u, k)
            k += 1
        pooled = jnp.where(grp == gi, acc / jnp.minimum(pos + 1, w).astype(F32), pooled)
    y = _bdot(pooled - u, wbd, "nn") * scale
    return [_tail16(u)], [y]


_POOL_T = 256


def _pool_io(proj3, wbd, scale):
    return [Row(proj3, 256, fc=lambda b: 6, dcols=256, dfc=lambda b: 0, ddtype=BF16)], [Vec(wbd), Vec(scale)]


def pool_forward(name, proj3, wbd, scale):
    rows, vecs = _pool_io(proj3, wbd, scale)
    s = proj3.shape[1]
    (y,), saved = scan_fwd(name, _pool_fn, nb=1, nchunk=s // _POOL_T, t=_POOL_T, rows=rows, vecs=vecs,
                           carries=[(16, 256)], outs=[out_row((1, s, POOL_W), BF16)], save=True)
    return y, saved


def pool_backward(name, proj3, wbd, scale, saved, dmix3):
    rows, vecs = _pool_io(proj3, wbd, scale)
    s = proj3.shape[1]
    return scan_bwd(name, _pool_fn, nb=1, nchunk=s // _POOL_T, t=_POOL_T, rows=rows, vecs=vecs,
                    carries=[(16, 256)], saved=saved, douts=[Row(dmix3, 256, fc=lambda b: 2)])


def _attn_fn(ci, b, carries, rows, vecs):
    kp, vp = carries
    qr, kr, v = _thirds(rows[0])
    scale = ATT_HEAD_DIM ** -0.5
    q = qr
    n = q.shape[0]
    r, c = _iota((n, n), 0), _iota((n, n), 1)
    prev_ok, cur_ok = jnp.logical_and(c >= r, ci > 0), r >= c
    head = _iota(q.shape, 1) // ATT_HEAD_DIM
    o, lse = jnp.zeros(q.shape, F32), jnp.zeros(q.shape, F32)
    for h in range(ATT_HEADS):
        mine = head == h
        qh = jnp.where(mine, qr, 0.0)
        sp = jnp.where(prev_ok, _bdot(qh, kp, "nt") * scale, NEG)
        sc = jnp.where(cur_ok, _bdot(qh, kr, "nt") * scale, NEG)
        m = lax.stop_gradient(jnp.maximum(jnp.max(sp, axis=1, keepdims=True), jnp.max(sc, axis=1, keepdims=True)))
        pp, pc = jnp.exp(sp - m), jnp.exp(sc - m)
        l = jnp.sum(pp, axis=1, keepdims=True) + jnp.sum(pc, axis=1, keepdims=True)
        o = jnp.where(mine, (_bdot(pp, vp, "nn") + _bdot(pc, v, "nn")) / l, o)
        lse = jnp.where(mine, m + jnp.log(l), lse)
    return [kr, v], [o, lse]


_ATT_CARRIES = [(ATT_BLOCK, ATT_W), (ATT_BLOCK, ATT_W)]


def attn_forward(name, pv, d):
    l = pv.shape[1]
    own = lambda b: b
    outs = [out_row((1, l, d * ATT_W), F32, ATT_W, fc=own) for _ in range(2)]
    (o, lse), saved = scan_fwd(name, _attn_fn, nb=d, nchunk=l // ATT_BLOCK, t=ATT_BLOCK, rows=[Row(pv, 3 * ATT_W, fc=own)],
                               vecs=[], carries=_ATT_CARRIES, outs=outs, save=True)
    return o, lse, saved


def attn_backward(name, pv, d, saved, do, dlse):
    l = pv.shape[1]
    own = lambda b: b
    (dpv,), _ = scan_bwd(name, _attn_fn, nb=d, nchunk=l // ATT_BLOCK, t=ATT_BLOCK, rows=[Row(pv, 3 * ATT_W, fc=own)], vecs=[],
                         carries=_ATT_CARRIES, saved=saved, douts=[Row(do, ATT_W, fc=own), Row(dlse, ATT_W, fc=own)])
    return dpv


def _rope_fn(ci, b, carries, rows, vecs):
    x, cs, sn = rows
    return [], [x * cs + _rot_pairs(x) * sn]


def _rope3_fn(ci, b, carries, rows, vecs):
    _, (y,) = _rope_fn(ci, b, carries, rows, vecs)
    return [], [y, y, y]


def _by_residue(a_or_shape, w, d):
    if isinstance(a_or_shape, tuple):
        _, s, _ = a_or_shape
        return Row(jax.ShapeDtypeStruct((1, s // d, d * w), F32), w, view=None if d == 1 else d)
    return Row(a_or_shape, w, view=None if d == 1 else d)


def rope_forward(name, qkv3, cs3, sn3):
    s, w = qkv3.shape[1], qkv3.shape[2]
    ys, _ = scan_fwd(name, _rope3_fn, nb=1, nchunk=s // _ROW_T, t=_ROW_T, vecs=[], carries=[], save=False,
                     rows=[Row(qkv3), Row(cs3, diff=False), Row(sn3, diff=False)],
                     outs=[_by_residue(qkv3.shape, w, d) for _, d in ATT_PATTERNS])
    return ys


def rope_backward(name, qkv3, cs3, sn3, dys):
    s, w = qkv3.shape[1], qkv3.shape[2]
    (dx,), _ = scan_bwd(name, _rope3_fn, nb=1, nchunk=s // _ROW_T, t=_ROW_T, vecs=[], carries=[], saved=[],
                        rows=[Row(qkv3, ddtype=BF16), Row(cs3, diff=False), Row(sn3, diff=False)],
                        douts=[_by_residue(a, w, d) for a, (_, d) in zip(dys, ATT_PATTERNS)])
    return dx


def _merge_fn(ci, b, carries, rows, vecs):
    o1, o2, o3, l1, l2, l3 = rows
    mx = lax.stop_gradient(jnp.maximum(l1, jnp.maximum(l2, l3)))
    e1, e2, e3 = jnp.exp(l1 - mx), jnp.exp(l2 - mx), jnp.exp(l3 - mx)
    return [], [(e1 * o1 + e2 * o2 + e3 * o3) / (e1 + e2 + e3)]


_ROW_T = 256


def _merge_rows(os_, ls_):
    ds = [d for _, d in ATT_PATTERNS]
    return [_by_residue(a, ATT_W, d) for a, d in zip(os_, ds)] + [_by_residue(a, ATT_W, d) for a, d in zip(ls_, ds)]


def merge_forward(name, os_, ls_, s):
    (y,), _ = scan_fwd(name, _merge_fn, nb=1, nchunk=s // _ROW_T, t=_ROW_T, rows=_merge_rows(os_, ls_), vecs=[],
                       carries=[], outs=[out_row((1, s, ATT_W), BF16)], save=False)
    return y


def merge_backward(name, os_, ls_, dmix3):
    s = dmix3.shape[1]
    drows, _ = scan_bwd(name, _merge_fn, nb=1, nchunk=s // _ROW_T, t=_ROW_T, rows=_merge_rows(os_, ls_), vecs=[],
                        carries=[], saved=[], douts=[Row(dmix3, 256, fc=lambda b: 3)])
    return drows


def _norm_mod_fn(ci, b, carries, rows, vecs):
    (x,) = rows
    g, sc, sh = vecs
    xn = x * lax.rsqrt(jnp.mean(x * x, axis=-1, keepdims=True) + NORM_EPS)
    return [], [xn * g * (1.0 + sc) + sh]


def norm_mod_forward(name, x3, g, sc, sh):
    s = x3.shape[1]
    (h,), _ = scan_fwd(name, _norm_mod_fn, nb=1, nchunk=s // _ROW_T, t=_ROW_T, rows=[Row(x3)], vecs=[Vec(g), Vec(sc), Vec(sh)],
                       carries=[], outs=[out_row(x3.shape, BF16)], save=False)
    return h


def norm_mod_backward(name, x3, g, sc, sh, dh3, add3):
    s = x3.shape[1]
    (dx,), dv = scan_bwd(name, _norm_mod_fn, nb=1, nchunk=s // _ROW_T, t=_ROW_T, rows=[Row(x3)], vecs=[Vec(g), Vec(sc), Vec(sh)],
                         carries=[], saved=[], douts=[Row(dh3)], adds={0: Row(add3)})
    return dx, dv


def _gate_fn(ci, b, carries, rows, vecs):
    return [], [rows[0] * vecs[0]]


def gate_backward(name, o3, g, dx3):
    s = o3.shape[1]
    (do,), (dg,) = scan_bwd(name, _gate_fn, nb=1, nchunk=s // _ROW_T, t=_ROW_T, rows=[Row(o3, ddtype=BF16)], vecs=[Vec(g)],
                            carries=[], saved=[], douts=[Row(dx3)])
    return do, dg


def _make_halves():
    @jax.custom_vjp
    def halves(x):
        h = x.shape[1] // 2
        return x[:, :h], x[:, h:]

    def fwd(x):
        return halves(x), None

    def bwd(_, g):
        return (jnp.concatenate(g, axis=1),)

    halves.defvjp(fwd, bwd)
    return halves


_halves = _make_halves()


def _ffn_fn(ci, b, carries, rows, vecs):
    (cu,) = carries
    (u,) = rows
    w, bias = vecs
    hg, hu = _halves(_conv(_shift8, cu, u, w, bias, 3))
    return [_tail8(u)], [_silu(hg) * hu]


_FFN_T = 256
_FFN_CW = FFN_DIM // 2
_FFN_CARRIES = [(8, 2 * _FFN_CW)]
FFN_BLOCK_ORDER = [0, 2, 1, 3]


def _ffn_io(up3, cw, cb):
    own = lambda b: b
    return [Row(up3, 2 * _FFN_CW, fc=own, ddtype=BF16)], [Vec(cw, 2 * _FFN_CW, own), Vec(cb, 2 * _FFN_CW, own)]


def ffn_mid_forward(name, up3, cw, cb):
    rows, vecs = _ffn_io(up3, cw, cb)
    s = up3.shape[1]
    (act,), saved = scan_fwd(name, _ffn_fn, nb=2, nchunk=s // _FFN_T, t=_FFN_T, rows=rows, vecs=vecs, carries=_FFN_CARRIES,
                             outs=[out_row((1, s, FFN_DIM), BF16, _FFN_CW, fc=lambda b: b)], save=True)
    return act, saved


def ffn_mid_backward(name, up3, cw, cb, saved, dact3):
    rows, vecs = _ffn_io(up3, cw, cb)
    s = up3.shape[1]
    return scan_bwd(name, _ffn_fn, nb=2, nchunk=s // _FFN_T, t=_FFN_T, rows=rows, vecs=vecs, carries=_FFN_CARRIES,
                    saved=saved, douts=[Row(dact3, _FFN_CW, fc=lambda b: b)])


def _adam_fn(ci, b, carries, rows, vecs):
    w, g, m, v = rows
    m = ADAM_B1 * m + (1.0 - ADAM_B1) * g
    v = ADAM_B2 * v + (1.0 - ADAM_B2) * (g * g)
    m_hat = m / (1.0 - ADAM_B1 ** ADAM_STEP)
    v_hat = v / (1.0 - ADAM_B2 ** ADAM_STEP)
    delta = -ADAM_LR * (m_hat / (jnp.sqrt(v_hat) + ADAM_EPS) + ADAM_WD * w)
    return [], [delta, m, v]


def adamw(name, w, g, m, v):
    shape = w.shape
    c = shape[-1]
    r = int(np.prod(shape[:-1]))
    t = _tile(r, 256, 8)
    as3 = lambda a: a.reshape(1, r, c)
    outs, _ = scan_fwd(name, _adam_fn, nb=1, nchunk=r // t, t=t, rows=[Row(as3(a)) for a in (w, g, m, v)], vecs=[], carries=[],
                       outs=[out_row((1, r, c)) for _ in range(3)], save=False)
    return [o.reshape(shape) for o in outs]


def rope_tables(positions):
    inv_freq = ROPE_THETA ** (-jnp.arange(0, ROT_DIM, 2, dtype=F32) / ROT_DIM)
    ang = positions.astype(F32)[:, None] * inv_freq
    s = positions.shape[0]
    cs = jnp.concatenate([jnp.cos(ang), jnp.cos(ang), jnp.ones((s, ATT_HEAD_DIM - ROT_DIM), F32)], axis=1)
    sn = jnp.concatenate([jnp.sin(ang), jnp.sin(ang), jnp.zeros((s, ATT_HEAD_DIM - ROT_DIM), F32)], axis=1)
    cs3 = jnp.concatenate([jnp.tile(cs, (1, 2 * ATT_HEADS)), jnp.ones((s, ATT_W), F32)], axis=1)
    sn3 = jnp.concatenate([jnp.tile(sn, (1, 2 * ATT_HEADS)), jnp.zeros((s, ATT_W), F32)], axis=1)
    return cs3[None], sn3[None]


def attention_forward(lname, qkv3, cs3, sn3):
    s = qkv3.shape[1]
    rotated = rope_forward(f"{lname}_rope", qkv3, cs3, sn3)
    os_, ls_, keep = [], [], []
    for pi, (_, d) in enumerate(ATT_PATTERNS):
        o, lse, saved = attn_forward(f"{lname}_attn{pi}", rotated[pi], d)
        os_.append(o)
        ls_.append(lse)
        keep.append(saved)
    y = merge_forward(f"{lname}_merge", os_, ls_, s)
    return y, (rotated, os_, ls_, keep)


def attention_backward(lname, qkv3, cs3, sn3, res, dmix3):
    rotated, os_, ls_, keep = res
    dm = merge_backward(f"{lname}_merge_b", os_, ls_, dmix3)
    dys = [attn_backward(f"{lname}_attn{pi}_b", rotated[pi], d, keep[pi], dm[pi], dm[3 + pi]) for pi, (_, d) in enumerate(ATT_PATTERNS)]
    return rope_backward(f"{lname}_rope_b", qkv3, cs3, sn3, dys)


def mm(name, a, b, mode, out_dtype=F32, res=None, gate=None, tm=1408, tn=1536, tk=1408, into=None):
    if mode == "nn":
        (m, k), n = a.shape, b.shape[1]
    elif mode == "nt":
        (m, k), n = a.shape, b.shape[0]
    else:
        (k, m), n = a.shape, b.shape[1]
    tm, tn, tk = _tile(m, tm), _tile(n, tn), _tile(k, tk)
    nk = k // tk
    a_spec = pl.BlockSpec((tk, tm), lambda i, j, q: (q, i)) if mode == "tn" else pl.BlockSpec((tm, tk), lambda i, j, q: (i, q))
    b_spec = pl.BlockSpec((tn, tk), lambda i, j, q: (j, q)) if mode == "nt" else pl.BlockSpec((tk, tn), lambda i, j, q: (q, j))
    o_spec = pl.BlockSpec((tm, tn), lambda i, j, q: (i, j))
    fused = res is not None
    lead = 0 if into is None else into[0].ndim - 2
    first = (0,) * lead + (slice(None), slice(None))

    def body(*refs):
        if fused:
            a_ref, b_ref, r_ref, g_ref, o_ref, o2_ref, acc = refs
        elif into is not None:
            a_ref, b_ref, _, o_ref, acc = refs
        else:
            a_ref, b_ref, o_ref, acc = refs
        q = pl.program_id(2)

        @pl.when(q == 0)
        def _():
            acc[...] = jnp.zeros(acc.shape, F32)

        acc[...] += _mxu(a_ref[...], b_ref[...], mode)

        @pl.when(q == nk - 1)
        def _():
            o_ref[first] = acc[...].astype(o_ref.dtype)
            if fused:
                o2_ref[...] = r_ref[...] + g_ref[...] * acc[...]

    ins, in_specs = [a, b], [a_spec, b_spec]
    out_shape, out_specs = [jax.ShapeDtypeStruct((m, n), out_dtype)], [o_spec]
    if fused:
        ins += [res, gate]
        in_specs += [o_spec, pl.BlockSpec((1, tn), lambda i, j, q: (0, j))]
        out_shape.append(jax.ShapeDtypeStruct((m, n), F32))
        out_specs.append(o_spec)
    aliases = {}
    if into is not None:
        buf, omap = into
        ins.append(buf)
        in_specs.append(pl.BlockSpec(memory_space=pl.ANY))
        out_shape = [jax.ShapeDtypeStruct(buf.shape, buf.dtype)]
        out_specs = [pl.BlockSpec((1,) * lead + (tm, tn), lambda i, j, q: omap(i, j))]
        aliases = {2: 0}
    out = pl.pallas_call(
        body, name=name, grid=(m // tm, n // tn, nk), in_specs=in_specs, out_specs=out_specs, out_shape=out_shape,
        scratch_shapes=[pltpu.VMEM((tm, tn), F32)], input_output_aliases=aliases,
        compiler_params=pltpu.CompilerParams(dimension_semantics=("parallel", "parallel", "arbitrary"),
                                             vmem_limit_bytes=VMEM_LIMIT_BYTES),
    )(*ins)
    return tuple(out) if fused else out[0]


def final_loss(name, x3, t3, g):
    s, d = x3.shape[1], x3.shape[2]
    t = _ROW_T

    def body(x_ref, t_ref, g_ref, loss_ref, dx_ref, dg_ref):
        i = pl.program_id(0)
        tv = t_ref[0]

        def f(x, gg):
            y = x * lax.rsqrt(jnp.mean(x * x, axis=-1, keepdims=True) + NORM_EPS) * gg
            e = y - tv
            return 0.5 * jnp.sum(jnp.mean(e * e, axis=-1, keepdims=True), axis=0, keepdims=True)

        l, vjp = jax.vjp(f, x_ref[0], g_ref[...])
        dx, dg = vjp(jnp.ones((1, 1), F32))
        dx_ref[0] = dx

        @pl.when(i == 0)
        def _():
            loss_ref[...] = jnp.zeros(loss_ref.shape, F32)
            dg_ref[...] = jnp.zeros(dg_ref.shape, F32)

        loss_ref[...] += jnp.broadcast_to(l, loss_ref.shape)
        dg_ref[...] += dg

    row = pl.BlockSpec((1, t, d), lambda i: (0, i, 0))
    vec = pl.BlockSpec((1, d), lambda i: (0, 0))
    return pl.pallas_call(
        body, name=name, grid=(s // t,), in_specs=[row, row, vec],
        out_specs=[pl.BlockSpec((8, 128), lambda i: (0, 0)), row, vec],
        out_shape=[jax.ShapeDtypeStruct((8, 128), F32), jax.ShapeDtypeStruct(x3.shape, F32), jax.ShapeDtypeStruct((1, d), F32)],
        compiler_params=pltpu.CompilerParams(dimension_semantics=("arbitrary",), vmem_limit_bytes=VMEM_LIMIT_BYTES),
    )(x3, t3, g)


_ADA_TN = 512


def ada_forward(name, c16, ada_w):
    depth, d, cols = ada_w.shape

    def body(c_ref, w_ref, o_ref):
        o_ref[0] = _mxu(_silu(c_ref[...]), w_ref[0], "nn")

    return pl.pallas_call(
        body, name=name, grid=(depth, cols // _ADA_TN),
        in_specs=[pl.BlockSpec((16, d), lambda l, j: (0, 0)), pl.BlockSpec((1, d, _ADA_TN), lambda l, j: (l, 0, j))],
        out_specs=pl.BlockSpec((1, 16, _ADA_TN), lambda l, j: (l, 0, j)),
        out_shape=jax.ShapeDtypeStruct((depth, 16, cols), F32),
        compiler_params=pltpu.CompilerParams(dimension_semantics=("arbitrary", "arbitrary"), vmem_limit_bytes=VMEM_LIMIT_BYTES),
    )(c16, ada_w)


def ada_backward(name, c16, dmod16, w, m, v):
    depth, d, cols = w.shape

    def body(c_ref, dm_ref, w_ref, m_ref, v_ref, g_ref, dl_ref, nm_ref, nv_ref):
        g = _mxu(_silu(c_ref[...]), dm_ref[0], "tn")
        _, (delta, nm, nv) = _adam_fn(None, None, [], [w_ref[0], g, m_ref[0], v_ref[0]], [])
        g_ref[0], dl_ref[0], nm_ref[0], nv_ref[0] = g, delta, nm, nv

    blk = pl.BlockSpec((1, d, _ADA_TN), lambda l, j: (l, 0, j))
    return pl.pallas_call(
        body, name=name, grid=(depth, cols // _ADA_TN),
        in_specs=[pl.BlockSpec((16, d), lambda l, j: (0, 0)), pl.BlockSpec((1, 16, _ADA_TN), lambda l, j: (l, 0, j)), blk, blk, blk],
        out_specs=[blk] * 4, out_shape=[jax.ShapeDtypeStruct(w.shape, F32)] * 4,
        compiler_params=pltpu.CompilerParams(dimension_semantics=("arbitrary", "arbitrary"), vmem_limit_bytes=VMEM_LIMIT_BYTES),
    )(c16, dmod16, w, m, v)


def _sum_fn(ci, b, carries, rows, vecs):
    acc = rows[0]
    for r in rows[1:]:
        acc = acc + r
    return [], [acc]


def sum_slots(name, a, nsum, out_dtype=F32):
    n, r, c = a.shape
    nb = n // nsum
    t = _tile(r, 256, 8)
    rows = [Row(a, fb=(lambda b, k=k: k * nb + b)) for k in range(nsum)]
    (out,), _ = scan_fwd(name, _sum_fn, nb=nb, nchunk=r // t, t=t, rows=rows, vecs=[], carries=[],
                         outs=[out_row((nb, r, c), out_dtype, fb=lambda b: b)], save=False)
    return out


def _sum_my_layer_fn(ci, b, carries, rows, vecs):
    layer0, layer1, theirs = rows
    return [], [jnp.where(lax.axis_index("c") == 0, layer0, layer1) + theirs]


def sum_cores(name, g, theirs, out_dtype):
    _, nb, r, c = g.shape
    g8 = g.reshape(2 * nb, r, c)
    t = _tile(r, 256, 8)
    rows = [Row(g8, fb=lambda b: b), Row(g8, fb=lambda b: nb + b), Row(theirs, fb=lambda b: b)]
    (out,), _ = scan_fwd(name, _sum_my_layer_fn, nb=nb, nchunk=r // t, t=t, rows=rows, vecs=[], carries=[],
                         outs=[out_row((nb, r, c), out_dtype, fb=lambda b: b)], save=False)
    return out


def _flip(mask, pos):
    return tuple((1 - p) if m else p for m, p in zip(mask, pos))


ALL_PEERS = [(a, b, c) for a in (0, 1) for b in (0, 1) for c in (0, 1)][1:]
CHIP_PEERS = [(1, 0, 0), (0, 1, 0), (1, 1, 0)]
SIBLING = [(0, 0, 1)]


def _divisor(size, target, unit):
    best = 1
    for n in range(1, target + 1):
        if size % n == 0 and (size // n) % unit == 0:
            best = n
    return best


def _pieces(src, dst, pieces):
    shape = src.shape
    unit = 16 if src.dtype == BF16 else 8
    if pieces <= 1:
        return [(src, dst)]
    if len(shape) == 2:
        n = _divisor(shape[0], pieces, unit)
        s = shape[0] // n
        return [(src.at[pl.ds(i * s, s)], dst.at[pl.ds(i * s, s)]) for i in range(n)]
    assert len(shape) == 3, shape
    n = _divisor(shape[1], max(pieces // shape[0], 1), unit)
    s = shape[1] // n
    return [(src.at[j, pl.ds(i * s, s)], dst.at[j, pl.ds(i * s, s)]) for j in range(shape[0]) for i in range(n)]


def comm_call(name, arrays, out_shapes, masks, src_fn, dst_fn, local_fn=None, pieces=1):
    na, npeer = len(arrays), len(masks)

    def body(*refs):
        ins, outs = refs[:na], refs[na:2 * na]
        send_sems, recv_sems, loc_sems = refs[2 * na:]
        me = (lax.axis_index("x"), lax.axis_index("y"), lax.axis_index("c"))
        local = []
        if local_fn is not None:
            for k in range(na):
                s, d = local_fn(k, ins[k], outs[k], me)
                for ps, pd in _pieces(s, d, pieces):
                    pltpu.make_async_copy(ps, pd, loc_sems.at[k]).start()
                local.append(pltpu.make_async_copy(s, d, loc_sems.at[k]))

        def remote(k, p, src, dst, to):
            return pltpu.make_async_remote_copy(
                src_ref=src, dst_ref=dst, send_sem=send_sems.at[k * npeer + p], recv_sem=recv_sems.at[k * npeer + p],
                device_id=to, device_id_type=MESH)

        for k in range(na):
            for p in range(npeer):
                peer = _flip(masks[p], me)
                for ps, pd in _pieces(src_fn(k, ins[k], me, peer), dst_fn(k, outs[k], me), pieces):
                    remote(k, p, ps, pd, peer).start()
        for k in range(na):
            for p in range(npeer):
                peer = _flip(masks[p], me)
                remote(k, p, src_fn(k, ins[k], me, peer), dst_fn(k, outs[k], peer), peer).wait_recv()
        for k in range(na):
            for p in range(npeer):
                peer = _flip(masks[p], me)
                remote(k, p, src_fn(k, ins[k], me, peer), dst_fn(k, outs[k], me), peer).wait_send()
        for cp in local:
            cp.wait()

    hbm = pl.BlockSpec(memory_space=pl.ANY)
    out = pl.pallas_call(
        body, name=name, in_specs=[hbm] * na, out_specs=[hbm] * na,
        out_shape=[jax.ShapeDtypeStruct(s, a.dtype) for s, a in zip(out_shapes, arrays)],
        scratch_shapes=[pltpu.SemaphoreType.DMA((na * npeer,)), pltpu.SemaphoreType.DMA((na * npeer,)),
                        pltpu.SemaphoreType.DMA((na,))],
    )(*arrays)
    return list(out)


def _dev(pos):
    return 4 * pos[0] + 2 * pos[1] + pos[2]


def _chip(pos):
    return 2 * pos[0] + pos[1]


def allgather8(name, a):
    (out,) = comm_call(name, [a], [(8,) + a.shape], ALL_PEERS,
                       src_fn=lambda k, r, me, peer: r, dst_fn=lambda k, o, sender: o.at[_dev(sender)],
                       local_fn=lambda k, r, o, me: (r, o.at[_dev(me)]))
    return out


def gather_layer_from_chips(name, arrays):
    return comm_call(name, arrays, [(4,) + a.shape[1:] for a in arrays], CHIP_PEERS,
                     src_fn=lambda k, r, me, peer: r.at[me[2]], dst_fn=lambda k, o, sender: o.at[_chip(sender)],
                     local_fn=lambda k, r, o, me: (r.at[me[2]], o.at[_chip(me)]), pieces=8)


def swap_layers(name, arrays, c):
    got = comm_call(name, arrays, [a.shape for a in arrays], SIBLING,
                    src_fn=lambda k, r, me, peer: r, dst_fn=lambda k, o, sender: o, pieces=32)
    return [[jnp.where(c == 0, a, g), jnp.where(c == 0, g, a)] for a, g in zip(arrays, got)]


def swap_other_layer(name, arrays):
    return comm_call(name, arrays, [a.shape[1:] for a in arrays], SIBLING,
                     src_fn=lambda k, r, me, peer: r.at[peer[2]], dst_fn=lambda k, o, sender: o, pieces=32)


def scatter_to_chips(name, arrays):
    return comm_call(name, arrays, [a.shape for a in arrays], CHIP_PEERS,
                     src_fn=lambda k, r, me, peer: r.at[_chip(peer)], dst_fn=lambda k, o, sender: o.at[_chip(sender)],
                     local_fn=lambda k, r, o, me: (r.at[_chip(me)], o.at[_chip(me)]), pieces=8)


def _rows_of(shape):
    return -(-int(np.prod(shape)) // 1024) * 8


def _pack(arrs):
    parts = []
    for a in arrs:
        flat = a.reshape(-1).astype(F32)
        parts.append(jnp.pad(flat, (0, _rows_of(a.shape) * 128 - flat.shape[0])).reshape(-1, 128))
    rows = sum(p.shape[0] for p in parts)
    parts.append(jnp.zeros(((-rows) % _ROW_T, 128), F32))
    return jnp.concatenate(parts, axis=0)


def _unpack(buf, shapes):
    out, o = [], 0
    for s in shapes:
        r, n = _rows_of(s), int(np.prod(s))
        out.append(buf[o:o + r].reshape(-1)[:n].reshape(s))
        o += r
    return out


_WEIGHTS = ["ada_w", "ada_b", "norm1_g", "w_in", "ssd_conv_w", "ssd_conv_b", "ssd_dt_bias", "ssd_a_log", "ssd_d", "ssd_norm_g",
            "pool_w", "pool_scale", "w_out", "norm2_g", "ffn_up", "ffn_conv_w", "ffn_conv_b", "ffn_down", "final_g"]
_BIG = ["w_in", "w_out", "ffn_up", "ffn_down"]
_SMALL = [n for n in _WEIGHTS if n not in _BIG and n != "ada_w"]
_COL_SHARDED_SMALL = {"ssd_conv_w": 256, "ffn_conv_w": 1408}


def _pad_lanes(v, n=128):
    return jnp.pad(v.astype(F32), (0, n - v.shape[0]))[None]


def _perm_cols(w):
    pad = jnp.zeros(w.shape[:-1] + (IN_WP - IN_W,), w.dtype)
    return jnp.concatenate([w[..., :1536], w[..., 1544:1800], w[..., 1536:1544], pad, w[..., 1800:]], axis=-1)


def _unperm_cols(g):
    return jnp.concatenate([g[..., :1536], g[..., 1792:1800], g[..., 1536:1792], g[..., IN_MAIN:]], axis=-1)


_CHIP2_PARTS = [(1284, 1536), (1792, 1800), (1536, 1792), (IN_MAIN, IN_MAIN + 126)]


def _w_in_chip_cols(gp):
    q = IN_W // 4
    return [gp[:, :q], gp[:, q:2 * q], jnp.concatenate([gp[:, a:b] for a, b in _CHIP2_PARTS], axis=1), gp[:, IN_WP - q:]]


def _w_in_from_chips(a):
    c2 = a[2]
    pad = jnp.zeros((a.shape[1], IN_WP - IN_W), a.dtype)
    return jnp.concatenate([a[0], a[1], c2[:, :252], c2[:, 260:516], c2[:, 252:260], pad, c2[:, 516:], a[3]], axis=1)


def _ffn_block_perm(a):
    n = a.shape[-1] // 4
    return jnp.concatenate([a[..., j * n:(j + 1) * n] for j in FFN_BLOCK_ORDER], axis=-1)


def _layer_forward(i, x3, modv, wts, sp, cs3, sn3):
    sh1, sc1, g1, sh2, sc2, g2 = modv
    big = lambda n: wts[n]() if callable(wts[n]) else wts[n]
    h1 = norm_mod_forward(f"l{i}_norm1", x3, wts["norm1_g"], sc1, sh1)
    proj3 = mm(f"l{i}_proj", h1[0], big("w_in")[:, :IN_MAIN], "nn")[None]
    qkv3 = mm(f"l{i}_qkv", h1[0], big("w_in")[:, IN_MAIN:], "nn")[None]
    y_ssd, sv_ssd = ssd_forward(f"l{i}_ssd", proj3, sp)
    y_pool, sv_pool = pool_forward(f"l{i}_pool", proj3, wts["wbd"], wts["pool_scale"])
    y_att, res_att = attention_forward(f"l{i}", qkv3, cs3, sn3)
    mix = jnp.concatenate([y_ssd, y_pool, y_att], axis=-1)
    out, x1 = mm(f"l{i}_wout", mix[0], big("w_out"), "nn", res=x3[0], gate=g1)
    x1 = x1[None]
    h2 = norm_mod_forward(f"l{i}_norm2", x1, wts["norm2_g"], sc2, sh2)
    up3 = mm(f"l{i}_up", h2[0], big("ffn_up"), "nn")[None]
    act, sv_ffn = ffn_mid_forward(f"l{i}_ffn", up3, wts["ffn_conv_w"], wts["ffn_conv_b"])
    dn, x2 = mm(f"l{i}_down", act[0], big("ffn_down"), "nn", res=x1[0], gate=g2)
    keep = dict(x=x3, h1=h1, proj3=proj3, qkv3=qkv3, sv_ssd=sv_ssd, sv_pool=sv_pool, res_att=res_att, mix=mix, out=out[None],
                x1=x1, h2=h2, up3=up3, act=act, sv_ffn=sv_ffn, dn=dn[None])
    return x2[None], keep


def _layer_backward(i, dx2, keep, modv, wts, sp, cs3, sn3, after=None):
    sh1, sc1, g1, sh2, sc2, g2 = modv
    k = keep
    big = lambda n: wts[n]() if callable(wts[n]) else wts[n]
    tell = lambda step, *a: after[step](*a) if after and step in after else None
    d_dn, d_g2 = gate_backward(f"l{i}_gate2_b", k["dn"], g2, dx2)
    d_act = mm(f"l{i}_down_bx", d_dn[0], big("ffn_down"), "nt")
    g_down = mm(f"l{i}_down_bw", k["act"][0], d_dn[0], "tn").reshape(4, FFN_DIM // 4, D_MODEL)
    (d_up,), dv_ffn = ffn_mid_backward(f"l{i}_ffn_b", k["up3"], wts["ffn_conv_w"], wts["ffn_conv_b"], k["sv_ffn"], d_act[None])
    tell("ffn_b")
    d_h2 = mm(f"l{i}_up_bx", d_up[0], big("ffn_up"), "nt")
    g_up = mm(f"l{i}_up_bw", k["h2"][0], d_up[0], "tn", tn=_FFN_CW,
              into=((4, D_MODEL, _FFN_CW), lambda r, c: ((c % 2) * 2 + c // 2, r, 0)))
    dx1, (d_n2, d_sc2, d_sh2) = norm_mod_backward(f"l{i}_norm2_b", k["x1"], wts["norm2_g"], sc2, sh2, d_h2[None], dx2)
    d_out, d_g1 = gate_backward(f"l{i}_gate1_b", k["out"], g1, dx1)
    d_mix = mm(f"l{i}_wout_bx", d_out[0], big("w_out"), "nt")[None]
    g_wout = mm(f"l{i}_wout_bw", k["mix"][0], d_out[0], "tn").reshape(4, D_MODEL // 4, D_MODEL)
    tell("wout_bw", g_wout, g_up, g_down)
    (dz, dxs, dbm, dcm, ddt), dv_ssd = ssd_backward(f"l{i}_ssd_b", k["proj3"], sp, k["sv_ssd"], d_mix)
    tell("ssd_b")
    (du_pool,), (d_wbd, d_pscale) = pool_backward(f"l{i}_pool_b", k["proj3"], wts["wbd"], wts["pool_scale"], k["sv_pool"], d_mix)
    d_qkv = attention_backward(f"l{i}", k["qkv3"], cs3, sn3, k["res_att"], d_mix)
    d_proj = jnp.concatenate([dz[0], dxs[0], dbm[0], dcm[0], du_pool[0], (ddt[0] + ddt[1]).astype(BF16), d_qkv[0]], axis=-1)
    d_h1 = mm(f"l{i}_proj_bx", d_proj, big("w_in"), "nt")
    g_win = mm(f"l{i}_proj_bw", k["h1"][0], d_proj, "tn")
    dx, (d_n1, d_sc1, d_sh1) = norm_mod_backward(f"l{i}_norm1_b", k["x"], wts["norm1_g"], sc1, sh1, d_h1[None], dx1)
    dcwx, dcbx, dcwb, dcbb, dcwc, dcbc, ddtb, dalog, ddsk, dng = dv_ssd
    small = dict(
        norm1_g=d_n1[0], norm2_g=d_n2[0],
        ssd_conv_w=jnp.concatenate([dcwx[:, :512], dcwb[:, 512:768], dcwc[:, 768:]], axis=1),
        ssd_conv_b=jnp.concatenate([dcbx[0, :512], dcbb[0, 512:768], dcbc[0, 768:]]),
        ssd_dt_bias=ddtb[0, :8], ssd_a_log=dalog[0, :8], ssd_d=ddsk[0, :8], ssd_norm_g=dng[0],
        pool_w=jnp.stack([d_wbd[64 * g:64 * g + 64, 64 * g:64 * g + 64] for g in range(4)]), pool_scale=d_pscale[0],
        ffn_conv_w=_ffn_block_perm(dv_ffn[0]), ffn_conv_b=_ffn_block_perm(dv_ffn[1][0]),
    )
    dmod = jnp.concatenate([d_sh1[0], d_sc1[0], d_g1[0], d_sh2[0], d_sc2[0], d_g2[0]])
    return dx, [jnp.stack(_w_in_chip_cols(g_win)), g_wout, g_up, g_down], small, dmod


def kernel(x, c, positions, ada_w, ada_b, norm1_g, w_in, ssd_conv_w, ssd_conv_b, ssd_dt_bias, ssd_a_log, ssd_d, ssd_norm_g, pool_w, pool_scale, w_out, norm2_g, ffn_up, ffn_conv_w, ffn_conv_b, ffn_down, final_g, loss_target, m_ada_w, m_ada_b, m_norm1_g, m_w_in, m_ssd_conv_w, m_ssd_conv_b, m_ssd_dt_bias, m_ssd_a_log, m_ssd_d, m_ssd_norm_g, m_pool_w, m_pool_scale, m_w_out, m_norm2_g, m_ffn_up, m_ffn_conv_w, m_ffn_conv_b, m_ffn_down, m_final_g, v_ada_w, v_ada_b, v_norm1_g, v_w_in, v_ssd_conv_w, v_ssd_conv_b, v_ssd_dt_bias, v_ssd_a_log, v_ssd_d, v_ssd_norm_g, v_pool_w, v_pool_scale, v_w_out, v_norm2_g, v_ffn_up, v_ffn_conv_w, v_ffn_conv_b, v_ffn_down, v_final_g):
    args = dict(locals())
    w = {n: args[n] for n in _WEIGHTS}
    m = {n: args["m_" + n] for n in _WEIGHTS}
    v = {n: args["v_" + n] for n in _WEIGHTS}
    d = D_MODEL
    me = (lax.axis_index("x"), lax.axis_index("y"), lax.axis_index("c"))
    chip, dev = _chip(me), _dev(me)
    RIDERS.reset()

    shapes0 = [c.shape, ssd_conv_w.shape, ffn_conv_w.shape]
    g0 = allgather8("gather_c_conv", _pack([c, ssd_conv_w, ffn_conv_w]))
    c16 = jnp.pad(g0[:, :d // 128, :].reshape(8, d), ((0, 8), (0, 0)))
    by_chip = [_unpack(g0[2 * j], shapes0) for j in range(4)]
    conv_w_full = jnp.concatenate([p[1] for p in by_chip], axis=-1)
    fconv_w_full = jnp.concatenate([p[2] for p in by_chip], axis=-1)

    modp = ada_forward("ada_fwd", c16, ada_w)[:, :8]
    g1 = allgather8("gather_mod", _pack([modp]))
    modfull = jnp.concatenate([_unpack(g1[2 * j], [modp.shape])[0] for j in range(4)], axis=-1)
    mod = lax.dynamic_index_in_dim(modfull, dev, axis=1, keepdims=False) + ada_b
    modv = [[mod[i, q * d:(q + 1) * d][None] for q in range(6)] for i in range(DEPTH)]

    shards = [w[n].astype(BF16) for n in _BIG]

    def weight(k, layer, got):
        parts = [jnp.where(chip == j, shards[k][layer], got[j]) for j in range(4)]
        if k == 0:
            return _w_in_from_chips(parts)
        return jnp.concatenate([parts[j] for j in FFN_BLOCK_ORDER], axis=1) if k == 2 else jnp.concatenate(parts, axis=0)

    def later(k, layer, host, pos):
        made = []

        def get():
            if not made:
                made.append(weight(k, layer, RIDERS.result(host)[pos]))
            return made[0]
        return get

    cs3, sn3 = rope_tables(positions[0])
    eye4 = jnp.eye(4, dtype=F32)
    wts, sps = [], []
    for i in range(DEPTH):
        wts.append(dict(
            norm1_g=norm1_g[i][None], norm2_g=norm2_g[i][None], pool_scale=pool_scale[i][None],
            wbd=(eye4[:, None, :, None] * pool_w[i][:, :, None, :]).reshape(POOL_W, POOL_W),
            ffn_conv_w=_ffn_block_perm(fconv_w_full[i]), ffn_conv_b=_ffn_block_perm(ffn_conv_b[i])[None]))
        sps.append(dict(cw=conv_w_full[i], cb=ssd_conv_b[i][None], dtb=_pad_lanes(ssd_dt_bias[i]), alog=_pad_lanes(ssd_a_log[i]),
                        dsk=_pad_lanes(ssd_d[i]), ng=ssd_norm_g[i][None]))

    (w_in0,) = ride_alone("gather_w_in0", gather_ride(0, [shards[0]]))
    RIDERS.book("l0_ssd", gather_ride(0, [shards[1], shards[3]]))
    RIDERS.book("l0_attn0", gather_ride(0, [shards[2]]))
    wts[0].update(w_in=weight(0, 0, w_in0), w_out=later(1, 0, "l0_ssd", 0), ffn_down=later(3, 0, "l0_ssd", 1),
                  ffn_up=later(2, 0, "l0_attn0", 0))
    for host, k in (("l0_wout", 0), ("l0_proj", 1), ("l0_up", 2), ("l0_down", 3)):
        RIDERS.book(host, gather_ride(1, [shards[k]]))
        wts[1][_BIG[k]] = later(k, 1, host, 0)
    x1_, keep0 = _layer_forward(0, x, modv[0], wts[0], sps[0], cs3, sn3)
    xc, keep1 = _layer_forward(1, x1_, modv[1], wts[1], sps[1], cs3, sn3)
    keeps = [keep0, keep1]
    lossblk, dx, d_final = final_loss("final_loss", xc, loss_target, final_g[None])
    loss = lax.psum(lossblk[0, 0], ("x", "y", "c"))

    small_g, dmods = [None] * DEPTH, [None] * DEPTH
    part_sum, from_chips = [[None] * 4 for _ in range(DEPTH)], [[None] * 4 for _ in range(DEPTH)]

    def owner_sum(layer, ks, mine, theirs):
        for k, g, t in zip(ks, mine, theirs):
            part_sum[layer][k] = add_arrays(f"sum_cores{layer}_{_BIG[k]}", [g, t], BF16)

    dx, by_chip1, small_g[1], dmods[1] = _layer_backward(1, dx, keeps[1], modv[1], wts[1], sps[1], cs3, sn3)
    RIDERS.book("l0_ffn_b", to_owner_ride(1, by_chip1))

    def after_ffn_b():
        owner_sum(1, range(4), by_chip1, RIDERS.result("l0_ffn_b"))
        RIDERS.book("l0_up_bx", scatter_ride(1, [part_sum[1][2]]))
        RIDERS.book("l0_up_bw", scatter_ride(1, [part_sum[1][0], part_sum[1][1]]))
        RIDERS.book("l0_norm2_b", scatter_ride(1, [part_sum[1][3]]))

    early = []

    def after_wout_bw(g_wout, g_up, g_down):
        early.extend([g_wout, g_up, g_down])
        RIDERS.book("l0_ssd_b", to_owner_ride(0, early))

    def after_ssd_b():
        owner_sum(0, [1, 2, 3], early, RIDERS.result("l0_ssd_b"))
        for host, k in (("l0_attn0_b", 2), ("l0_attn1_b", 3), ("l0_attn2_b", 1)):
            RIDERS.book(host, scatter_ride(0, [part_sum[0][k]]))

    dx, by_chip0, small_g[0], dmods[0] = _layer_backward(0, dx, keeps[0], modv[0], wts[0], sps[0], cs3, sn3,
                                                         after=dict(ffn_b=after_ffn_b, wout_bw=after_wout_bw, ssd_b=after_ssd_b))
    from_chips[1][2], (from_chips[1][0], from_chips[1][1]) = RIDERS.result("l0_up_bx")[0], RIDERS.result("l0_up_bw")
    from_chips[1][3] = RIDERS.result("l0_norm2_b")[0]
    for host, k in (("l0_attn0_b", 2), ("l0_attn1_b", 3), ("l0_attn2_b", 1)):
        from_chips[0][k] = RIDERS.result(host)[0]
    owner_sum(0, [0], by_chip0[:1], ride_alone("to_owner0_w_in", to_owner_ride(0, by_chip0[:1])))
    (from_chips[0][0],) = ride_alone("scatter_g0_w_in", scatter_ride(0, [part_sum[0][0]]))
    mine = [sum_chips_mine(f"sum_chips_{n}", part_sum[0][k], from_chips[0][k], part_sum[1][k], from_chips[1][k])
            for k, n in enumerate(_BIG)]
    reduced = swap_layers("swap_r", mine, me[2])
    grads = {n: jnp.stack(r) for n, r in zip(_BIG, reduced)}

    part = dict(ada_b=jnp.stack(dmods), final_g=d_final[0])
    for n in _SMALL:
        if n not in part:
            part[n] = jnp.stack([small_g[i][n] for i in range(DEPTH)])
    full_shapes = [part[n].shape for n in _SMALL]
    gs = allgather8("gather_small", _pack([part[n] for n in _SMALL]))
    tot = _unpack(sum_slots("sum_small", gs, 8)[0], full_shapes)
    small_tot = dict(zip(_SMALL, tot))
    dmod_all = gs[:, :DEPTH * 6 * d // 128, :].reshape(8, DEPTH, 6 * d)
    for n, ncol in _COL_SHARDED_SMALL.items():
        small_tot[n] = lax.dynamic_slice_in_dim(small_tot[n], chip * ncol, ncol, axis=2)
    grads.update(small_tot)

    ncol = ada_w.shape[2]
    dm = lax.dynamic_slice_in_dim(dmod_all, chip * ncol, ncol, axis=2).transpose(1, 0, 2)
    upd = {}
    g_ada, *upd["ada_w"] = ada_backward("ada_bwd", c16, jnp.pad(dm, ((0, 0), (0, 8), (0, 0))), ada_w, m["ada_w"], v["ada_w"])
    grads["ada_w"] = g_ada

    for n in _BIG:
        upd[n] = adamw(f"adam_{n}", w[n], grads[n], m[n], v[n])
    shapes_s = [w[n].shape for n in _SMALL]
    packed = [_pack([src[n] for n in _SMALL]) for src in (w, grads, m, v)]
    outs_s = [_unpack(o, shapes_s) for o in adamw("adam_small", *packed)]
    for q, n in enumerate(_SMALL):
        upd[n] = [outs_s[0][q], outs_s[1][q], outs_s[2][q]]

    return (loss, dx, *[grads[n] for n in _WEIGHTS], *[upd[n][0] for n in _WEIGHTS], *[upd[n][1] for n in _WEIGHTS],
            *[upd[n][2] for n in _WEIGHTS])


class Ride:
    def __init__(self, ins, out_shapes, nsem, start, finish):
        self.ins, self.out_shapes, self.nsem, self.start, self.finish = ins, out_shapes, nsem, start, finish

    def specs(self):
        hbm = pl.BlockSpec(memory_space=pl.ANY)
        return [hbm] * len(self.ins), [hbm] * len(self.out_shapes), [pltpu.SemaphoreType.DMA((self.nsem,))] * 2

    def begin(self, in_refs, out_refs, sems, cond=None):
        me = (lax.axis_index("x"), lax.axis_index("y"), lax.axis_index("c"))
        go = lambda: self.start(in_refs, out_refs, sems[0], sems[1], me)
        go() if cond is None else pl.when(cond)(go)

    def end(self, in_refs, out_refs, sems, cond=None):
        me = (lax.axis_index("x"), lax.axis_index("y"), lax.axis_index("c"))
        go = lambda: self.finish(in_refs, out_refs, sems[0], sems[1], me)
        go() if cond is None else pl.when(cond)(go)


def ride_alone(name, ride):
    ni, no = len(ride.ins), len(ride.out_shapes)

    def body(*refs):
        ride.begin(refs[:ni], refs[ni:ni + no], refs[ni + no:])
        ride.end(refs[:ni], refs[ni:ni + no], refs[ni + no:])

    in_specs, out_specs, scratch = ride.specs()
    return list(pl.pallas_call(body, name=name, in_specs=in_specs, out_specs=out_specs, out_shape=ride.out_shapes,
                               scratch_shapes=scratch)(*ride.ins))


def mm(name, a, b, mode, out_dtype=F32, res=None, gate=None, tm=1408, tn=1536, tk=1408, into=None):
    ride = RIDERS.take(name)
    if mode == "nn":
        (m, k), n = a.shape, b.shape[1]
    elif mode == "nt":
        (m, k), n = a.shape, b.shape[0]
    else:
        (k, m), n = a.shape, b.shape[1]
    tm, tn, tk = _tile(m, tm), _tile(n, tn), _tile(k, tk)
    ni, nj, nk = m // tm, n // tn, k // tk
    a_spec = pl.BlockSpec((tk, tm), lambda i, j, q: (q, i)) if mode == "tn" else pl.BlockSpec((tm, tk), lambda i, j, q: (i, q))
    b_spec = pl.BlockSpec((tn, tk), lambda i, j, q: (j, q)) if mode == "nt" else pl.BlockSpec((tk, tn), lambda i, j, q: (q, j))
    o_spec = pl.BlockSpec((tm, tn), lambda i, j, q: (i, j))
    fused = res is not None
    lead = 0 if into is None else len(into[0]) - 2
    first = (0,) * lead + (slice(None), slice(None))
    ins, in_specs = [a, b], [a_spec, b_spec]
    out_shape, out_specs = [jax.ShapeDtypeStruct((m, n), out_dtype)], [o_spec]
    if fused:
        ins += [res, gate]
        in_specs += [o_spec, pl.BlockSpec((1, tn), lambda i, j, q: (0, j))]
        out_shape.append(jax.ShapeDtypeStruct((m, n), F32))
        out_specs.append(o_spec)
    if into is not None:
        shape, omap = into
        out_shape = [jax.ShapeDtypeStruct(shape, out_dtype)]
        out_specs = [pl.BlockSpec((1,) * lead + (tm, tn), lambda i, j, q: omap(i, j))]
    n_in, n_out = len(ins), len(out_shape)
    scratch = [pltpu.VMEM((tm, tn), F32)]
    if ride is not None:
        r_in, r_out, r_scr = ride.specs()
        ins, in_specs = ins + list(ride.ins), in_specs + r_in
        out_shape, out_specs = out_shape + list(ride.out_shapes), out_specs + r_out
        scratch = scratch + r_scr

    def body(*refs):
        a_ref, b_ref = refs[:2]
        o_ref = refs[len(ins)]
        acc = refs[len(ins) + len(out_shape)]
        i, j, q = pl.program_id(0), pl.program_id(1), pl.program_id(2)
        at = lambda x, y, z: jnp.logical_and(jnp.logical_and(i == x, j == y), q == z)
        r_refs = (refs[n_in:len(ins)], refs[len(ins) + n_out:len(ins) + len(out_shape)], refs[len(ins) + len(out_shape) + 1:])
        if ride is not None:
            ride.begin(*r_refs, at(0, 0, 0))

        @pl.when(q == 0)
        def _():
            acc[...] = jnp.zeros(acc.shape, F32)

        acc[...] += _mxu(a_ref[...], b_ref[...], mode)

        @pl.when(q == nk - 1)
        def _():
            o_ref[first] = acc[...].astype(o_ref.dtype)
            if fused:
                refs[len(ins) + 1][...] = refs[2][...] + refs[3][...] * acc[...]

        if ride is not None:
            ride.end(*r_refs, at(ni - 1, nj - 1, nk - 1))

    sem = ("arbitrary",) * 3 if ride is not None else ("parallel", "parallel", "arbitrary")
    out = pl.pallas_call(
        body, name=name, grid=(ni, nj, nk), in_specs=in_specs, out_specs=out_specs, out_shape=out_shape, scratch_shapes=scratch,
        compiler_params=pltpu.CompilerParams(dimension_semantics=sem, vmem_limit_bytes=VMEM_LIMIT_BYTES),
    )(*ins)
    if ride is not None:
        RIDERS.done[name] = list(out[n_out:])
    return tuple(out[:n_out]) if fused else out[0]


def add_arrays(name, arrs, out_dtype=F32):
    nb, r, c = arrs[0].shape
    t = _tile(r, 256, 8)
    (out,), _ = scan_fwd(name, _sum_fn, nb=nb, nchunk=r // t, t=t, rows=[Row(a, fb=lambda b: b) for a in arrs], vecs=[], carries=[],
                         outs=[out_row((nb, r, c), out_dtype, fb=lambda b: b)], save=False)
    return out


def _sum_chips_mine_fn(ci, b, carries, rows, vecs):
    mine_layer = lax.axis_index("c")
    chip = 2 * lax.axis_index("x") + lax.axis_index("y")
    tot = None
    for j in range(4):
        own = jnp.where(mine_layer == 0, rows[j], rows[8 + j])
        sent = jnp.where(mine_layer == 0, rows[4 + j], rows[12 + j])
        term = jnp.where(chip == j, own, sent)
        tot = term if tot is None else tot + term
    return [], [tot]


def sum_chips_mine(name, p0, q0, p1, q1):
    _, r, c = p0.shape
    t = _tile(r, 256, 8)
    rows = [Row(a, fb=(lambda b, j=j: j)) for a in (p0, q0, p1, q1) for j in range(4)]
    (out,), _ = scan_fwd(name, _sum_chips_mine_fn, nb=1, nchunk=r // t, t=t, rows=rows, vecs=[], carries=[],
                         outs=[out_row((1, r, c))], save=False)
    return out[0]


def _remote(src, dst, send_sems, recv_sems, k, to):
    return pltpu.make_async_remote_copy(src_ref=src, dst_ref=dst, send_sem=send_sems.at[k], recv_sem=recv_sems.at[k],
                                        device_id=to, device_id_type=MESH)


def gather_ride(layer, shards):
    na = len(shards)

    def start(ins, outs, ss, rs, me):
        @pl.when(me[2] == layer)
        def _():
            for k in range(na):
                for p, mask in enumerate(CHIP_PEERS):
                    _remote(ins[k].at[layer], outs[k].at[_chip(me)], ss, rs, 6 * k + p, _flip(mask, me)).start()

    def finish(ins, outs, ss, rs, me):
        sibling = _flip(SIBLING[0], me)

        @pl.when(me[2] == layer)
        def _():
            for k in range(na):
                for p, mask in enumerate(CHIP_PEERS):
                    slot = outs[k].at[_chip(_flip(mask, me))]
                    _remote(ins[k].at[layer], slot, ss, rs, 6 * k + p, _flip(mask, me)).wait_recv()
                    _remote(slot, slot, ss, rs, 6 * k + 3 + p, sibling).start()
            for k in range(na):
                for p, mask in enumerate(CHIP_PEERS):
                    slot = outs[k].at[_chip(_flip(mask, me))]
                    _remote(ins[k].at[layer], slot, ss, rs, 6 * k + p, _flip(mask, me)).wait_send()
                    _remote(slot, slot, ss, rs, 6 * k + 3 + p, sibling).wait_send()

        @pl.when(me[2] != layer)
        def _():
            for k in range(na):
                for p, mask in enumerate(CHIP_PEERS):
                    slot = outs[k].at[_chip(_flip(mask, me))]
                    _remote(slot, slot, ss, rs, 6 * k + 3 + p, sibling).wait_recv()

    return Ride(list(shards), [jax.ShapeDtypeStruct((4,) + a.shape[1:], a.dtype) for a in shards], 6 * na, start, finish)


def scatter_ride(layer, parts):
    na = len(parts)

    def start(ins, outs, ss, rs, me):
        @pl.when(me[2] == layer)
        def _():
            for k in range(na):
                for p, mask in enumerate(CHIP_PEERS):
                    peer = _flip(mask, me)
                    _remote(ins[k].at[_chip(peer)], outs[k].at[_chip(me)], ss, rs, 3 * k + p, peer).start()

    def finish(ins, outs, ss, rs, me):
        @pl.when(me[2] == layer)
        def _():
            for k in range(na):
                for p, mask in enumerate(CHIP_PEERS):
                    peer = _flip(mask, me)
                    _remote(ins[k].at[_chip(peer)], outs[k].at[_chip(peer)], ss, rs, 3 * k + p, peer).wait_recv()
                    _remote(ins[k].at[_chip(peer)], outs[k].at[_chip(me)], ss, rs, 3 * k + p, peer).wait_send()

    return Ride(list(parts), [jax.ShapeDtypeStruct(a.shape, a.dtype) for a in parts], 3 * na, start, finish)


def to_owner_ride(layer, arrays):
    na = len(arrays)

    def start(ins, outs, ss, rs, me):
        @pl.when(me[2] != layer)
        def _():
            for k in range(na):
                _remote(ins[k], outs[k], ss, rs, k, _flip(SIBLING[0], me)).start()

    def finish(ins, outs, ss, rs, me):
        for k in range(na):
            cp = _remote(ins[k], outs[k], ss, rs, k, _flip(SIBLING[0], me))
            pl.when(me[2] != layer)(cp.wait_send)
            pl.when(me[2] == layer)(cp.wait_recv)

    return Ride(list(arrays), [jax.ShapeDtypeStruct(a.shape, a.dtype) for a in arrays], na, start, finish)


def _w_in_from_chips(a):
    c2 = a[2]
    pad = jnp.zeros((c2.shape[0], IN_WP - IN_W), c2.dtype)
    return jnp.concatenate([a[0], a[1], c2[:, :252], c2[:, 260:516], c2[:, 252:260], pad, c2[:, 516:], a[3]], axis=1)


def _mm_host(rides, rode, key, *args, **kw):
    if rides is None or key not in rides:
        return mm(*args, **kw)
    main, rode[key] = mm(*args, ride=rides[key], **kw)
    return main
```

```python
import functools
import math

import numpy as np
import jax
import jax.numpy as jnp
from jax import lax
from jax.experimental import pallas as pl
from jax.experimental.pallas import tpu as pltpu

F32 = jnp.float32
BF16 = jnp.bfloat16
HI = lax.Precision.HIGHEST
MESH = pl.DeviceIdType.MESH

D_MODEL = 1024
SEQ = 4096
DEPTH = 2
SSD_INNER = 512
SSD_HEADS = 8
SSD_STATE = 128
POOL_W = 256
POOL_WINDOWS = (2, 4, 8, 16)
ATT_W = 256
ATT_HEADS = 4
ATT_HEAD_DIM = 64
ATT_PATTERNS = ((128, 1), (512, 4), (2048, 16))
ATT_BLOCK = 128
ROT_DIM = 16
ROPE_THETA = 500000.0
IN_W = 2568
IN_WP = 2688
IN_MAIN = 1920
FFN_DIM = 2816
NORM_EPS = 1e-6
ADAM_LR, ADAM_B1, ADAM_B2, ADAM_EPS, ADAM_WD, ADAM_STEP = 0.001, 0.9, 0.999, 1e-08, 0.01, 10

VMEM_LIMIT_BYTES = 56 * 1024 * 1024
NEG = -1e30


def _mxu(a, b, mode):
    dims = {"nn": ((1,), (0,)), "nt": ((1,), (1,)), "tn": ((0,), (0,))}[mode]
    return lax.dot_general(a.astype(BF16), b.astype(BF16), (dims, ((), ())), preferred_element_type=F32)


@functools.partial(jax.custom_vjp, nondiff_argnums=(2,))
def _bdot(a, b, mode):
    return _mxu(a, b, mode)


def _bdot_fwd(a, b, mode):
    return _mxu(a, b, mode), (a, b)


def _bdot_bwd(mode, res, g):
    a, b = res
    if mode == "nn":
        return _mxu(g, b, "nt"), _mxu(a, g, "tn")
    if mode == "nt":
        return _mxu(g, b, "nn"), _mxu(g, a, "tn")
    return _mxu(b, g, "nt"), _mxu(a, g, "nn")


_bdot.defvjp(_bdot_fwd, _bdot_bwd)


def _fxu(a, b, mode):
    dims = {"nn": ((1,), (0,)), "nt": ((1,), (1,)), "tn": ((0,), (0,))}[mode]
    return lax.dot_general(a, b, (dims, ((), ())), precision=HI, preferred_element_type=F32)


@functools.partial(jax.custom_vjp, nondiff_argnums=(2,))
def _fdot(a, b, mode):
    return _fxu(a, b, mode)


def _fdot_fwd(a, b, mode):
    return _fxu(a, b, mode), (a, b)


def _fdot_bwd(mode, res, g):
    a, b = res
    if mode == "nn":
        return _fxu(g, b, "nt"), _fxu(a, g, "tn")
    if mode == "nt":
        return _fxu(g, b, "nn"), _fxu(g, a, "tn")
    return _fxu(b, g, "nt"), _fxu(a, g, "nn")


_fdot.defvjp(_fdot_fwd, _fdot_bwd)


def _iota(shape, dim):
    return lax.broadcasted_iota(jnp.int32, shape, dim)


def _make_shift(h):
    @functools.partial(jax.custom_vjp, nondiff_argnums=(2,))
    def shift(halo, cur, k):
        if k == 0:
            return cur
        full = jnp.concatenate([halo, cur], axis=0)
        return pltpu.roll(full, k, 0)[h:]

    def fwd(halo, cur, k):
        return shift(halo, cur, k), None

    def bwd(k, _, g):
        t, w = g.shape
        if k == 0:
            return jnp.zeros((h, w), F32), g
        d_cur = jnp.where(_iota((t, w), 0) < t - k, pltpu.roll(g, t - k, 0), 0.0)
        top = g[:h]
        d_halo = jnp.where(_iota((h, w), 0) >= h - k, pltpu.roll(top, h - k, 0) if k < h else top, 0.0)
        return d_halo, d_cur

    shift.defvjp(fwd, bwd)
    return shift


_shift8 = _make_shift(8)
_shift16 = _make_shift(16)


def _make_tail(h):
    @jax.custom_vjp
    def tail(x):
        return x[x.shape[0] - h:]

    def fwd(x):
        return tail(x), x.shape[0]

    def bwd(t, g):
        return (jnp.concatenate([jnp.zeros((t - h, g.shape[1]), F32), g], axis=0),)

    tail.defvjp(fwd, bwd)
    return tail


_tail8 = _make_tail(8)
_tail16 = _make_tail(16)


@jax.custom_vjp
def _cumsum_rows(x):
    t = x.shape[0]
    row, s = _iota(x.shape, 0), 1
    while s < t:
        x = x + jnp.where(row >= s, pltpu.roll(x, s, 0), 0.0)
        s *= 2
    return x


def _cumsum_rows_fwd(x):
    return _cumsum_rows(x), None


def _cumsum_rows_bwd(_, g):
    t = g.shape[0]
    row, s = _iota(g.shape, 0), 1
    while s < t:
        g = g + jnp.where(row < t - s, pltpu.roll(g, t - s, 0), 0.0)
        s *= 2
    return (g,)


_cumsum_rows.defvjp(_cumsum_rows_fwd, _cumsum_rows_bwd)


@jax.custom_vjp
def _rot_pairs(t):
    e = _iota(t.shape, 1) % ATT_HEAD_DIM
    n = t.shape[1]
    return jnp.where(e < 8, -pltpu.roll(t, n - 8, 1), jnp.where(e < 16, pltpu.roll(t, 8, 1), 0.0))


def _rot_pairs_fwd(t):
    return _rot_pairs(t), None


def _rot_pairs_bwd(_, g):
    e = _iota(g.shape, 1) % ATT_HEAD_DIM
    n = g.shape[1]
    return (pltpu.roll(jnp.where(e < 8, -g, 0.0), 8, 1) + pltpu.roll(jnp.where(jnp.logical_and(e >= 8, e < 16), g, 0.0), n - 8, 1),)


_rot_pairs.defvjp(_rot_pairs_fwd, _rot_pairs_bwd)


def _make_thirds():
    @jax.custom_vjp
    def thirds(x):
        w = x.shape[1] // 3
        return x[:, :w], x[:, w:2 * w], x[:, 2 * w:]

    def fwd(x):
        return thirds(x), None

    def bwd(_, g):
        return (jnp.concatenate(g, axis=1),)

    thirds.defvjp(fwd, bwd)
    return thirds


_thirds = _make_thirds()


def _rowk(w, k):
    return jnp.sum(jnp.where(_iota(w.shape, 0) == k, w, 0.0), axis=0, keepdims=True)


def _silu(x):
    return x * (0.5 * jnp.tanh(0.5 * x) + 0.5)


def _softplus(x):
    return jnp.maximum(x, 0.0) + jnp.log(1.0 + jnp.exp(-jnp.abs(x)))


def _tile(dim, target, unit=128):
    if dim <= target:
        return dim
    best = None
    for t in range(unit, target + 1, unit):
        if dim % t == 0:
            best = t
    assert best is not None, (dim, target)
    return best


class Ride:
    def __init__(self, ins, out_shapes, nsem, start, finish):
        self.ins, self.out_shapes, self.nsem, self.start, self.finish = ins, out_shapes, nsem, start, finish

    def specs(self):
        hbm = pl.BlockSpec(memory_space=pl.ANY)
        return [hbm] * len(self.ins), [hbm] * len(self.out_shapes), [pltpu.SemaphoreType.DMA((self.nsem,))] * 2

    def begin(self, in_refs, out_refs, sems, cond=None):
        me = (lax.axis_index("x"), lax.axis_index("y"), lax.axis_index("c"))
        go = lambda: self.start(in_refs, out_refs, sems[0], sems[1], me)
        go() if cond is None else pl.when(cond)(go)

    def end(self, in_refs, out_refs, sems, cond=None):
        me = (lax.axis_index("x"), lax.axis_index("y"), lax.axis_index("c"))
        go = lambda: self.finish(in_refs, out_refs, sems[0], sems[1], me)
        go() if cond is None else pl.when(cond)(go)


class _Riders:
    def reset(self):
        self.booked, self.done = {}, {}

    def book(self, host, ride):
        assert host not in self.booked, host
        self.booked[host] = ride

    def take(self, host):
        return self.booked.pop(host, None)

    def result(self, host):
        return self.done[host]


RIDERS = _Riders()
RIDERS.reset()


class Row:
    def __init__(self, arr, w=None, fb=None, fc=None, diff=True, slot=False, dcols=None, dfc=None, ddtype=F32, view=None):
        self.ddtype = ddtype
        self.view = view
        self.arr = arr
        self.w = arr.shape[2] if w is None else w
        self.fb = (lambda b: 0) if fb is None else fb
        self.fc = (lambda b: 0) if fc is None else fc
        self.diff = diff
        self.slot = slot
        self.dcols = dcols
        self.dfc = dfc


class Vec:
    def __init__(self, arr, w=None, fc=None, diff=True):
        self.arr = arr
        self.w = arr.shape[1] if w is None else w
        self.fc = fc
        self.diff = diff


def _row_spec(r, t, nchunk, reverse):
    shape = (1, t, r.w) if r.view is None else (1, t // r.view, r.view * r.w)
    if reverse:
        return pl.BlockSpec(shape, lambda b, i, r=r: (r.fb(b), nchunk - 1 - i, r.fc(b)))
    return pl.BlockSpec(shape, lambda b, i, r=r: (r.fb(b), i, r.fc(b)))


def _load_row(ref, r, t, scr):
    if r.view is None:
        return ref[0]
    d, w = r.view, r.w
    for q in range(d):
        for j in range(w // 128):
            scr[j, pl.ds(q, t // d, stride=d), :] = ref[0, :, q * w + 128 * j:q * w + 128 * (j + 1)].astype(F32)
    return jnp.concatenate([scr[j] for j in range(w // 128)], axis=1)


def _store_row(ref, r, t, scr, val):
    if r.view is None:
        ref[0] = val.astype(ref.dtype)
        return
    d, w = r.view, r.w
    for j in range(w // 128):
        scr[j] = val[:, 128 * j:128 * (j + 1)]
    for q in range(d):
        for j in range(w // 128):
            ref[0, :, q * w + 128 * j:q * w + 128 * (j + 1)] = scr[j, pl.ds(q, t // d, stride=d), :].astype(ref.dtype)


def _view_scratch(specs, t):
    ws = [r.w for r in specs if r.view is not None]
    return [pltpu.VMEM((max(ws) // 128, t, 128), F32)] if ws else []


def _vec_spec(v):
    if v.fc is None:
        return pl.BlockSpec(v.arr.shape, lambda b, i: (0, 0))
    return pl.BlockSpec((v.arr.shape[0], v.w), lambda b, i, v=v: (0, v.fc(b)))


def _cparams():
    return pltpu.CompilerParams(dimension_semantics=("arbitrary", "arbitrary"), vmem_limit_bytes=VMEM_LIMIT_BYTES)


def scan_fwd(name, fn, *, nb, nchunk, t, rows, vecs, carries, outs, save):
    nr, nv, nc, no = len(rows), len(vecs), len(carries), len(outs)
    ns = nc if save else 0
    ride = RIDERS.take(name)
    r_in, r_out, r_scr = ride.specs() if ride else ([], [], [])

    def body(*refs):
        p = 0
        row_refs = refs[p:p + nr]; p += nr
        vec_refs = refs[p:p + nv]; p += nv
        ride_in = refs[p:p + len(r_in)]; p += len(r_in)
        out_refs = refs[p:p + no]; p += no
        save_refs = refs[p:p + ns]; p += ns
        ride_out = refs[p:p + len(r_out)]; p += len(r_out)
        car = refs[p:p + nc]; p += nc
        scr = refs[p] if stage else None
        sems = refs[p + len(stage):]
        b, i = pl.program_id(0), pl.program_id(1)
        if ride:
            ride.begin(ride_in, ride_out, sems, jnp.logical_and(b == 0, i == 0))
        if nc:
            @pl.when(i == 0)
            def _():
                for c_ref in car:
                    c_ref[...] = jnp.zeros(c_ref.shape, F32)
        cin = [c_ref[...] for c_ref in car]
        if save:
            for s_ref, cv in zip(save_refs, cin):
                s_ref[0, 0] = cv
        new_c, o = fn(i, b, cin, [_load_row(ref, r, t, scr) for ref, r in zip(row_refs, rows)], [v[...] for v in vec_refs])
        for c_ref, cv in zip(car, new_c):
            c_ref[...] = cv
        for o_ref, spec, ov in zip(out_refs, outs, o):
            _store_row(o_ref, spec, t, scr, ov)
        if ride:
            ride.end(ride_in, ride_out, sems, jnp.logical_and(b == nb - 1, i == nchunk - 1))

    stage = _view_scratch(list(rows) + list(outs), t)
    out_shape = [o.arr for o in outs]
    out_specs = [_row_spec(o, t, nchunk, False) for o in outs]
    if save:
        for cs in carries:
            out_shape.append(jax.ShapeDtypeStruct((nb, nchunk) + tuple(cs), F32))
            out_specs.append(pl.BlockSpec((1, 1) + tuple(cs), lambda b, i: (b, i, 0, 0)))
    res = pl.pallas_call(
        body, name=name, grid=(nb, nchunk),
        in_specs=[_row_spec(r, t, nchunk, False) for r in rows] + [_vec_spec(v) for v in vecs] + r_in,
        out_specs=out_specs + r_out, out_shape=out_shape + (list(ride.out_shapes) if ride else []),
        scratch_shapes=[pltpu.VMEM(tuple(cs), F32) for cs in carries] + stage + r_scr,
        compiler_params=_cparams(),
    )(*[r.arr for r in rows], *[v.arr for v in vecs], *(ride.ins if ride else []))
    if ride:
        RIDERS.done[name] = list(res[no + ns:])
    return list(res[:no]), list(res[no:no + ns])


def scan_bwd(name, fn, *, nb, nchunk, t, rows, vecs, carries, saved, douts, adds=None):
    adds = adds or {}
    nr, nv, nc, no = len(rows), len(vecs), len(carries), len(douts)
    dri = [k for k, r in enumerate(rows) if r.diff]
    dvi = [k for k, v in enumerate(vecs) if v.diff]
    add_keys = sorted(adds)
    na = len(add_keys)
    ride = RIDERS.take(name)
    r_in, r_out, r_scr = ride.specs() if ride else ([], [], [])

    def body(*refs):
        p = 0
        row_refs = refs[p:p + nr]; p += nr
        vec_refs = refs[p:p + nv]; p += nv
        save_refs = refs[p:p + nc]; p += nc
        dout_refs = refs[p:p + no]; p += no
        add_refs = refs[p:p + na]; p += na
        ride_in = refs[p:p + len(r_in)]; p += len(r_in)
        drow_refs = refs[p:p + len(dri)]; p += len(dri)
        dvec_refs = refs[p:p + len(dvi)]; p += len(dvi)
        ride_out = refs[p:p + len(r_out)]; p += len(r_out)
        dcar = refs[p:p + nc]; p += nc
        scr = refs[p] if stage else None
        sems = refs[p + len(stage):]
        b, ir = pl.program_id(0), pl.program_id(1)
        ci = nchunk - 1 - ir
        if ride:
            ride.begin(ride_in, ride_out, sems, jnp.logical_and(b == 0, ir == 0))
        if nc:
            @pl.when(ir == 0)
            def _():
                for c_ref in dcar:
                    c_ref[...] = jnp.zeros(c_ref.shape, F32)
        rows_v = [_load_row(ref, r, t, scr) for ref, r in zip(row_refs, rows)]
        vecs_v = [v[...] for v in vec_refs]
        cin = [s[0, 0] for s in save_refs]
        dc = [c_ref[...] for c_ref in dcar]
        dout_v = [_load_row(ref, r, t, scr).astype(F32) for ref, r in zip(dout_refs, douts)]

        def f(cs, dr, dv):
            rr, vv = list(rows_v), list(vecs_v)
            for k, idx in enumerate(dri):
                rr[idx] = dr[k]
            for k, idx in enumerate(dvi):
                vv[idx] = dv[k]
            return fn(ci, b, cs, rr, vv)

        _, vjp = jax.vjp(f, cin, [rows_v[k].astype(F32) for k in dri], [vecs_v[k].astype(F32) for k in dvi])
        dcin, drows, dvecs = vjp((dc, dout_v))
        for c_ref, cv in zip(dcar, dcin):
            c_ref[...] = cv
        for k, (o_ref, ov) in enumerate(zip(drow_refs, drows)):
            if dri[k] in adds:
                ov = ov + add_refs[add_keys.index(dri[k])][0].astype(F32)
            _store_row(o_ref, rows[dri[k]], t, scr, ov)
        for k, (o_ref, ov) in enumerate(zip(dvec_refs, dvecs)):
            first = (ir == 0) if vecs[dvi[k]].fc is not None else jnp.logical_and(ir == 0, b == 0)

            @pl.when(first)
            def _(o_ref=o_ref, ov=ov):
                o_ref[...] = ov

            @pl.when(jnp.logical_not(first))
            def _(o_ref=o_ref, ov=ov):
                o_ref[...] += ov

        if ride:
            ride.end(ride_in, ride_out, sems, jnp.logical_and(b == nb - 1, ir == nchunk - 1))

    stage = _view_scratch(list(rows) + list(douts), t)
    in_specs = ([_row_spec(r, t, nchunk, True) for r in rows] + [_vec_spec(v) for v in vecs]
                + [pl.BlockSpec((1, 1) + tuple(cs), lambda b, i: (b, nchunk - 1 - i, 0, 0)) for cs in carries]
                + [_row_spec(d, t, nchunk, True) for d in douts]
                + [_row_spec(adds[k], t, nchunk, True) for k in add_keys] + r_in)
    out_shape, out_specs = [], []
    for k in dri:
        r = rows[k]
        if r.slot:
            out_shape.append(jax.ShapeDtypeStruct((nb, r.arr.shape[1], r.w), r.ddtype))
            out_specs.append(pl.BlockSpec((1, t, r.w), lambda b, i: (b, nchunk - 1 - i, 0)))
        elif r.dcols is not None:
            out_shape.append(jax.ShapeDtypeStruct((r.arr.shape[0], r.arr.shape[1], r.dcols), r.ddtype))
            out_specs.append(pl.BlockSpec((1, t, r.w), lambda b, i, r=r: (r.fb(b), nchunk - 1 - i, r.dfc(b))))
        else:
            out_shape.append(jax.ShapeDtypeStruct(r.arr.shape, r.ddtype))
            out_specs.append(_row_spec(r, t, nchunk, True))
    for k in dvi:
        out_shape.append(jax.ShapeDtypeStruct(vecs[k].arr.shape, F32))
        out_specs.append(_vec_spec(vecs[k]))
    nd = len(dri) + len(dvi)
    res = pl.pallas_call(
        body, name=name, grid=(nb, nchunk), in_specs=in_specs, out_specs=out_specs + r_out,
        out_shape=out_shape + (list(ride.out_shapes) if ride else []),
        scratch_shapes=[pltpu.VMEM(tuple(cs), F32) for cs in carries] + stage + r_scr,
        compiler_params=_cparams(),
    )(*[r.arr for r in rows], *[v.arr for v in vecs], *saved, *[d.arr for d in douts], *[adds[k].arr for k in add_keys],
      *(ride.ins if ride else []))
    if ride:
        RIDERS.done[name] = list(res[nd:])
    return list(res[:len(dri)]), list(res[len(dri):nd])


def out_row(shape, dtype=F32, w=None, fb=None, fc=None):
    return Row(jax.ShapeDtypeStruct(shape, dtype), w, fb, fc)


def _conv(shift, halo, cur, w, bias, taps):
    y = bias
    for k in range(taps):
        y = y + _rowk(w, k) * shift(halo, cur, taps - 1 - k)
    return y


def _ssd_fn(ci, b, carries, rows, vecs):
    cx, cb_, cc, ht = carries
    z, xr, br, cr, dtr = rows
    cwx, cbx, cwb, cbb, cwc, cbc, dtb, alog, dsk, ng = vecs
    t = z.shape[0]
    xs = _silu(_conv(_shift8, cx, xr, cwx, cbx, 4))
    bm = _silu(_conv(_shift8, cb_, br, cwb, cbb, 4))
    cm = _silu(_conv(_shift8, cc, cr, cwc, cbc, 4))
    dt = _softplus(dtr + dtb)
    acol = _cumsum_rows(dt * (-jnp.exp(alog)))
    arow = acol.T
    r, c = _iota((t, t), 0), _iota((t, t), 1)
    causal = r >= c
    cbm = _bdot(cm, bm, "nt")
    lane, sub = _iota(acol.shape, 1), _iota(arow.shape, 0)
    colh = _iota(xs.shape, 1) // 64
    a, dtx, dx, acs = jnp.zeros(xs.shape, F32), jnp.zeros(xs.shape, F32), jnp.zeros((1, xs.shape[1]), F32), []
    for j in range(4):
        h = 4 * b + j
        ac = jnp.sum(jnp.where(lane == h, acol, 0.0), axis=1, keepdims=True)
        acs.append(ac)
        a = jnp.where(colh == j, ac, a)
        dtx = jnp.where(colh == j, jnp.sum(jnp.where(lane == h, dt, 0.0), axis=1, keepdims=True), dtx)
        dx = jnp.where(_iota(dx.shape, 1) // 64 == j, jnp.sum(jnp.where(_iota(dsk.shape, 1) == h, dsk, 0.0), axis=1, keepdims=True), dx)
    atot = jnp.sum(jnp.where(_iota(a.shape, 0) == t - 1, a, 0.0), axis=0, keepdims=True)
    x = xs * dtx
    ydiag = jnp.zeros(x.shape, F32)
    for j in range(4):
        ar = jnp.sum(jnp.where(sub == 4 * b + j, arow, 0.0), axis=0, keepdims=True)
        lmat = jnp.exp(jnp.where(causal, acs[j] - ar, NEG))
        ydiag = ydiag + _bdot(cbm * lmat, jnp.where(colh == j, x, 0.0), "nn")
    yoff = _bdot(cm, ht, "nn") * jnp.exp(a)
    ht_new = ht * jnp.exp(atot) + _bdot(bm, x * jnp.exp(atot - a), "tn")
    y = ydiag + yoff + dx * xs
    yz = y * _silu(z)
    yn = yz * lax.rsqrt(jnp.mean(yz * yz, axis=-1, keepdims=True) + NORM_EPS) * ng
    return [_tail8(xr), _tail8(br), _tail8(cr), ht_new], [yn]


_SSD_T = 256
_SSD_CARRIES = [(8, 256), (8, 128), (8, 128), (128, 256)]


def _ssd_io(proj3, p):
    own = lambda b: b
    rows = [Row(proj3, 256, fc=own, dcols=512, dfc=own, ddtype=BF16),
            Row(proj3, 256, fc=lambda b: 2 + b, dcols=512, dfc=own, ddtype=BF16),
            Row(proj3, 128, fc=lambda b: 8 + b, dcols=256, dfc=own, ddtype=BF16),
            Row(proj3, 128, fc=lambda b: 10 + b, dcols=256, dfc=own, ddtype=BF16),
            Row(proj3, 128, fc=lambda b: 14, slot=True)]
    vecs = [Vec(p["cw"], 256, lambda b: b), Vec(p["cb"], 256, lambda b: b),
            Vec(p["cw"], 128, lambda b: 4 + b), Vec(p["cb"], 128, lambda b: 4 + b),
            Vec(p["cw"], 128, lambda b: 6 + b), Vec(p["cb"], 128, lambda b: 6 + b),
            Vec(p["dtb"]), Vec(p["alog"]), Vec(p["dsk"]), Vec(p["ng"], 256, lambda b: b)]
    return rows, vecs


def ssd_forward(name, proj3, p):
    rows, vecs = _ssd_io(proj3, p)
    s = proj3.shape[1]
    (y,), saved = scan_fwd(name, _ssd_fn, nb=2, nchunk=s // _SSD_T, t=_SSD_T, rows=rows, vecs=vecs,
                           carries=_SSD_CARRIES, outs=[out_row((1, s, SSD_INNER), BF16, 256, fc=lambda b: b)], save=True)
    return y, saved


def ssd_backward(name, proj3, p, saved, dmix3):
    rows, vecs = _ssd_io(proj3, p)
    s = proj3.shape[1]
    drows, dvecs = scan_bwd(name, _ssd_fn, nb=2, nchunk=s // _SSD_T, t=_SSD_T, rows=rows, vecs=vecs,
                            carries=_SSD_CARRIES, saved=saved, douts=[Row(dmix3, 256, fc=lambda b: b)])
    return drows, dvecs


def _pool_fn(ci, b, carries, rows, vecs):
    (cu,) = carries
    (u,) = rows
    wbd, scale = vecs
    t = u.shape[0]
    pos = ci * t + _iota(u.shape, 0)
    grp = _iota(u.shape, 1) // 64
    acc, pooled, k = u, jnp.zeros(u.shape, F32), 1
    for gi, w in enumerate(POOL_WINDOWS):
        while k < w:
            acc = acc + _shift16(cu, u, k)
            k += 1
        pooled = jnp.where(grp == gi, acc / jnp.minimum(pos + 1, w).astype(F32), pooled)
    y = _bdot(pooled - u, wbd, "nn") * scale
    return [_tail16(u)], [y]


_POOL_T = 256


def _pool_io(proj3, wbd, scale):
    return [Row(proj3, 256, fc=lambda b: 6, dcols=256, dfc=lambda b: 0, ddtype=BF16)], [Vec(wbd), Vec(scale)]


def pool_forward(name, proj3, wbd, scale):
    rows, vecs = _pool_io(proj3, wbd, scale)
    s = proj3.shape[1]
    (y,), saved = scan_fwd(name, _pool_fn, nb=1, nchunk=s // _POOL_T, t=_POOL_T, rows=rows, vecs=vecs,
                           carries=[(16, 256)], outs=[out_row((1, s, POOL_W), BF16)], save=True)
    return y, saved


def pool_backward(name, proj3, wbd, scale, saved, dmix3):
    rows, vecs = _pool_io(proj3, wbd, scale)
    s = proj3.shape[1]
    return scan_bwd(name, _pool_fn, nb=1, nchunk=s // _POOL_T, t=_POOL_T, rows=rows, vecs=vecs,
                    carries=[(16, 256)], saved=saved, douts=[Row(dmix3, 256, fc=lambda b: 2)])


def _attn_fn(ci, b, carries, rows, vecs):
    kp, vp = carries
    qr, kr, v = _thirds(rows[0])
    scale = ATT_HEAD_DIM ** -0.5
    q = qr
    n = q.shape[0]
    r, c = _iota((n, n), 0), _iota((n, n), 1)
    prev_ok, cur_ok = jnp.logical_and(c >= r, ci > 0), r >= c
    head = _iota(q.shape, 1) // ATT_HEAD_DIM
    o, lse = jnp.zeros(q.shape, F32), jnp.zeros(q.shape, F32)
    for h in range(ATT_HEADS):
        mine = head == h
        qh = jnp.where(mine, qr, 0.0)
        sp = jnp.where(prev_ok, _bdot(qh, kp, "nt") * scale, NEG)
        sc = jnp.where(cur_ok, _bdot(qh, kr, "nt") * scale, NEG)
        m = lax.stop_gradient(jnp.maximum(jnp.max(sp, axis=1, keepdims=True), jnp.max(sc, axis=1, keepdims=True)))
        pp, pc = jnp.exp(sp - m), jnp.exp(sc - m)
        l = jnp.sum(pp, axis=1, keepdims=True) + jnp.sum(pc, axis=1, keepdims=True)
        o = jnp.where(mine, (_bdot(pp, vp, "nn") + _bdot(pc, v, "nn")) / l, o)
        lse = jnp.where(mine, m + jnp.log(l), lse)
    return [kr, v], [o, lse]


_ATT_CARRIES = [(ATT_BLOCK, ATT_W), (ATT_BLOCK, ATT_W)]


def attn_forward(name, pv, d):
    l = pv.shape[1]
    own = lambda b: b
    outs = [out_row((1, l, d * ATT_W), F32, ATT_W, fc=own) for _ in range(2)]
    (o, lse), saved = scan_fwd(name, _attn_fn, nb=d, nchunk=l // ATT_BLOCK, t=ATT_BLOCK, rows=[Row(pv, 3 * ATT_W, fc=own)],
                               vecs=[], carries=_ATT_CARRIES, outs=outs, save=True)
    return o, lse, saved


def attn_backward(name, pv, d, saved, do, dlse):
    l = pv.shape[1]
    own = lambda b: b
    (dpv,), _ = scan_bwd(name, _attn_fn, nb=d, nchunk=l // ATT_BLOCK, t=ATT_BLOCK, rows=[Row(pv, 3 * ATT_W, fc=own)], vecs=[],
                         carries=_ATT_CARRIES, saved=saved, douts=[Row(do, ATT_W, fc=own), Row(dlse, ATT_W, fc=own)])
    return dpv


def _rope_fn(ci, b, carries, rows, vecs):
    x, cs, sn = rows
    return [], [x * cs + _rot_pairs(x) * sn]


def _rope3_fn(ci, b, carries, rows, vecs):
    _, (y,) = _rope_fn(ci, b, carries, rows, vecs)
    return [], [y, y, y]


def _by_residue(a_or_shape, w, d):
    if isinstance(a_or_shape, tuple):
        _, s, _ = a_or_shape
        return Row(jax.ShapeDtypeStruct((1, s // d, d * w), F32), w, view=None if d == 1 else d)
    return Row(a_or_shape, w, view=None if d == 1 else d)


def rope_forward(name, qkv3, cs3, sn3):
    s, w = qkv3.shape[1], qkv3.shape[2]
    ys, _ = scan_fwd(name, _rope3_fn, nb=1, nchunk=s // _ROW_T, t=_ROW_T, vecs=[], carries=[], save=False,
                     rows=[Row(qkv3), Row(cs3, diff=False), Row(sn3, diff=False)],
                     outs=[_by_residue(qkv3.shape, w, d) for _, d in ATT_PATTERNS])
    return ys


def rope_backward(name, qkv3, cs3, sn3, dys):
    s, w = qkv3.shape[1], qkv3.shape[2]
    (dx,), _ = scan_bwd(name, _rope3_fn, nb=1, nchunk=s // _ROW_T, t=_ROW_T, vecs=[], carries=[], saved=[],
                        rows=[Row(qkv3, ddtype=BF16), Row(cs3, diff=False), Row(sn3, diff=False)],
                        douts=[_by_residue(a, w, d) for a, (_, d) in zip(dys, ATT_PATTERNS)])
    return dx


def _merge_fn(ci, b, carries, rows, vecs):
    o1, o2, o3, l1, l2, l3 = rows
    mx = lax.stop_gradient(jnp.maximum(l1, jnp.maximum(l2, l3)))
    e1, e2, e3 = jnp.exp(l1 - mx), jnp.exp(l2 - mx), jnp.exp(l3 - mx)
    return [], [(e1 * o1 + e2 * o2 + e3 * o3) / (e1 + e2 + e3)]


_ROW_T = 256


def _merge_rows(os_, ls_):
    ds = [d for _, d in ATT_PATTERNS]
    return [_by_residue(a, ATT_W, d) for a, d in zip(os_, ds)] + [_by_residue(a, ATT_W, d) for a, d in zip(ls_, ds)]


def merge_forward(name, os_, ls_, s):
    (y,), _ = scan_fwd(name, _merge_fn, nb=1, nchunk=s // _ROW_T, t=_ROW_T, rows=_merge_rows(os_, ls_), vecs=[],
                       carries=[], outs=[out_row((1, s, ATT_W), BF16)], save=False)
    return y


def merge_backward(name, os_, ls_, dmix3):
    s = dmix3.shape[1]
    drows, _ = scan_bwd(name, _merge_fn, nb=1, nchunk=s // _ROW_T, t=_ROW_T, rows=_merge_rows(os_, ls_), vecs=[],
                        carries=[], saved=[], douts=[Row(dmix3, 256, fc=lambda b: 3)])
    return drows


def _norm_mod_fn(ci, b, carries, rows, vecs):
    (x,) = rows
    g, sc, sh = vecs
    xn = x * lax.rsqrt(jnp.mean(x * x, axis=-1, keepdims=True) + NORM_EPS)
    return [], [xn * g * (1.0 + sc) + sh]


def norm_mod_forward(name, x3, g, sc, sh):
    s = x3.shape[1]
    (h,), _ = scan_fwd(name, _norm_mod_fn, nb=1, nchunk=s // _ROW_T, t=_ROW_T, rows=[Row(x3)], vecs=[Vec(g), Vec(sc), Vec(sh)],
                       carries=[], outs=[out_row(x3.shape, BF16)], save=False)
    return h


def norm_mod_backward(name, x3, g, sc, sh, dh3, add3):
    s = x3.shape[1]
    (dx,), dv = scan_bwd(name, _norm_mod_fn, nb=1, nchunk=s // _ROW_T, t=_ROW_T, rows=[Row(x3)], vecs=[Vec(g), Vec(sc), Vec(sh)],
                         carries=[], saved=[], douts=[Row(dh3)], adds={0: Row(add3)})
    return dx, dv


def _gate_fn(ci, b, carries, rows, vecs):
    return [], [rows[0] * vecs[0]]


def gate_backward(name, o3, g, dx3):
    s = o3.shape[1]
    (do,), (dg,) = scan_bwd(name, _gate_fn, nb=1, nchunk=s // _ROW_T, t=_ROW_T, rows=[Row(o3, ddtype=BF16)], vecs=[Vec(g)],
                            carries=[], saved=[], douts=[Row(dx3)])
    return do, dg


def _make_halves():
    @jax.custom_vjp
    def halves(x):
        h = x.shape[1] // 2
        return x[:, :h], x[:, h:]

    def fwd(x):
        return halves(x), None

    def bwd(_, g):
        return (jnp.concatenate(g, axis=1),)

    halves.defvjp(fwd, bwd)
    return halves


_halves = _make_halves()


def _ffn_fn(ci, b, carries, rows, vecs):
    (cu,) = carries
    (u,) = rows
    w, bias = vecs
    hg, hu = _halves(_conv(_shift8, cu, u, w, bias, 3))
    return [_tail8(u)], [_silu(hg) * hu]


_FFN_T = 256
_FFN_CW = FFN_DIM // 2
_FFN_CARRIES = [(8, 2 * _FFN_CW)]
FFN_BLOCK_ORDER = [0, 2, 1, 3]


def _ffn_io(up3, cw, cb):
    own = lambda b: b
    return [Row(up3, 2 * _FFN_CW, fc=own, ddtype=BF16)], [Vec(cw, 2 * _FFN_CW, own), Vec(cb, 2 * _FFN_CW, own)]


def ffn_mid_forward(name, up3, cw, cb):
    rows, vecs = _ffn_io(up3, cw, cb)
    s = up3.shape[1]
    (act,), saved = scan_fwd(name, _ffn_fn, nb=2, nchunk=s // _FFN_T, t=_FFN_T, rows=rows, vecs=vecs, carries=_FFN_CARRIES,
                             outs=[out_row((1, s, FFN_DIM), BF16, _FFN_CW, fc=lambda b: b)], save=True)
    return act, saved


def ffn_mid_backward(name, up3, cw, cb, saved, dact3):
    rows, vecs = _ffn_io(up3, cw, cb)
    s = up3.shape[1]
    return scan_bwd(name, _ffn_fn, nb=2, nchunk=s // _FFN_T, t=_FFN_T, rows=rows, vecs=vecs, carries=_FFN_CARRIES,
                    saved=saved, douts=[Row(dact3, _FFN_CW, fc=lambda b: b)])


def _adam_fn(ci, b, carries, rows, vecs):
    w, g, m, v = rows
    m = ADAM_B1 * m + (1.0 - ADAM_B1) * g
    v = ADAM_B2 * v + (1.0 - ADAM_B2) * (g * g)
    m_hat = m / (1.0 - ADAM_B1 ** ADAM_STEP)
    v_hat = v / (1.0 - ADAM_B2 ** ADAM_STEP)
    delta = -ADAM_LR * (m_hat / (jnp.sqrt(v_hat) + ADAM_EPS) + ADAM_WD * w)
    return [], [delta, m, v]


def adamw(name, w, g, m, v):
    shape = w.shape
    c = shape[-1]
    r = int(np.prod(shape[:-1]))
    t = _tile(r, 256, 8)
    as3 = lambda a: a.reshape(1, r, c)
    outs, _ = scan_fwd(name, _adam_fn, nb=1, nchunk=r // t, t=t, rows=[Row(as3(a)) for a in (w, g, m, v)], vecs=[], carries=[],
                       outs=[out_row((1, r, c)) for _ in range(3)], save=False)
    return [o.reshape(shape) for o in outs]


def rope_tables(positions):
    inv_freq = ROPE_THETA ** (-jnp.arange(0, ROT_DIM, 2, dtype=F32) / ROT_DIM)
    ang = positions.astype(F32)[:, None] * inv_freq
    s = positions.shape[0]
    cs = jnp.concatenate([jnp.cos(ang), jnp.cos(ang), jnp.ones((s, ATT_HEAD_DIM - ROT_DIM), F32)], axis=1)
    sn = jnp.concatenate([jnp.sin(ang), jnp.sin(ang), jnp.zeros((s, ATT_HEAD_DIM - ROT_DIM), F32)], axis=1)
    cs3 = jnp.concatenate([jnp.tile(cs, (1, 2 * ATT_HEADS)), jnp.ones((s, ATT_W), F32)], axis=1)
    sn3 = jnp.concatenate([jnp.tile(sn, (1, 2 * ATT_HEADS)), jnp.zeros((s, ATT_W), F32)], axis=1)
    return cs3[None], sn3[None]


def attention_forward(lname, qkv3, cs3, sn3):
    s = qkv3.shape[1]
    rotated = rope_forward(f"{lname}_rope", qkv3, cs3, sn3)
    os_, ls_, keep = [], [], []
    for pi, (_, d) in enumerate(ATT_PATTERNS):
        o, lse, saved = attn_forward(f"{lname}_attn{pi}", rotated[pi], d)
        os_.append(o)
        ls_.append(lse)
        keep.append(saved)
    y = merge_forward(f"{lname}_merge", os_, ls_, s)
    return y, (rotated, os_, ls_, keep)


def attention_backward(lname, qkv3, cs3, sn3, res, dmix3):
    rotated, os_, ls_, keep = res
    dm = merge_backward(f"{lname}_merge_b", os_, ls_, dmix3)
    dys = [attn_backward(f"{lname}_attn{pi}_b", rotated[pi], d, keep[pi], dm[pi], dm[3 + pi]) for pi, (_, d) in enumerate(ATT_PATTERNS)]
    return rope_backward(f"{lname}_rope_b", qkv3, cs3, sn3, dys)


def mm(name, a, b, mode, out_dtype=F32, res=None, gate=None, tm=1408, tn=1536, tk=1408, into=None):
    if mode == "nn":
        (m, k), n = a.shape, b.shape[1]
    elif mode == "nt":
        (m, k), n = a.shape, b.shape[0]
    else:
        (k, m), n = a.shape, b.shape[1]
    tm, tn, tk = _tile(m, tm), _tile(n, tn), _tile(k, tk)
    nk = k // tk
    a_spec = pl.BlockSpec((tk, tm), lambda i, j, q: (q, i)) if mode == "tn" else pl.BlockSpec((tm, tk), lambda i, j, q: (i, q))
    b_spec = pl.BlockSpec((tn, tk), lambda i, j, q: (j, q)) if mode == "nt" else pl.BlockSpec((tk, tn), lambda i, j, q: (q, j))
    o_spec = pl.BlockSpec((tm, tn), lambda i, j, q: (i, j))
    fused = res is not None
    lead = 0 if into is None else into[0].ndim - 2
    first = (0,) * lead + (slice(None), slice(None))

    def body(*refs):
        if fused:
            a_ref, b_ref, r_ref, g_ref, o_ref, o2_ref, acc = refs
        elif into is not None:
            a_ref, b_ref, _, o_ref, acc = refs
        else:
            a_ref, b_ref, o_ref, acc = refs
        q = pl.program_id(2)

        @pl.when(q == 0)
        def _():
            acc[...] = jnp.zeros(acc.shape, F32)

        acc[...] += _mxu(a_ref[...], b_ref[...], mode)

        @pl.when(q == nk - 1)
        def _():
            o_ref[first] = acc[...].astype(o_ref.dtype)
            if fused:
                o2_ref[...] = r_ref[...] + g_ref[...] * acc[...]

    ins, in_specs = [a, b], [a_spec, b_spec]
    out_shape, out_specs = [jax.ShapeDtypeStruct((m, n), out_dtype)], [o_spec]
    if fused:
        ins += [res, gate]
        in_specs += [o_spec, pl.BlockSpec((1, tn), lambda i, j, q: (0, j))]
        out_shape.append(jax.ShapeDtypeStruct((m, n), F32))
        out_specs.append(o_spec)
    aliases = {}
    if into is not None:
        buf, omap = into
        ins.append(buf)
        in_specs.append(pl.BlockSpec(memory_space=pl.ANY))
        out_shape = [jax.ShapeDtypeStruct(buf.shape, buf.dtype)]
        out_specs = [pl.BlockSpec((1,) * lead + (tm, tn), lambda i, j, q: omap(i, j))]
        aliases = {2: 0}
    out = pl.pallas_call(
        body, name=name, grid=(m // tm, n // tn, nk), in_specs=in_specs, out_specs=out_specs, out_shape=out_shape,
        scratch_shapes=[pltpu.VMEM((tm, tn), F32)], input_output_aliases=aliases,
        compiler_params=pltpu.CompilerParams(dimension_semantics=("parallel", "parallel", "arbitrary"),
                                             vmem_limit_bytes=VMEM_LIMIT_BYTES),
    )(*ins)
    return tuple(out) if fused else out[0]


def final_loss(name, x3, t3, g):
    s, d = x3.shape[1], x3.shape[2]
    t = _ROW_T

    def body(x_ref, t_ref, g_ref, loss_ref, dx_ref, dg_ref):
        i = pl.program_id(0)
        tv = t_ref[0]

        def f(x, gg):
            y = x * lax.rsqrt(jnp.mean(x * x, axis=-1, keepdims=True) + NORM_EPS) * gg
            e = y - tv
            return 0.5 * jnp.sum(jnp.mean(e * e, axis=-1, keepdims=True), axis=0, keepdims=True)

        l, vjp = jax.vjp(f, x_ref[0], g_ref[...])
        dx, dg = vjp(jnp.ones((1, 1), F32))
        dx_ref[0] = dx

        @pl.when(i == 0)
        def _():
            loss_ref[...] = jnp.zeros(loss_ref.shape, F32)
            dg_ref[...] = jnp.zeros(dg_ref.shape, F32)

        loss_ref[...] += jnp.broadcast_to(l, loss_ref.shape)
        dg_ref[...] += dg

    row = pl.BlockSpec((1, t, d), lambda i: (0, i, 0))
    vec = pl.BlockSpec((1, d), lambda i: (0, 0))
    return pl.pallas_call(
        body, name=name, grid=(s // t,), in_specs=[row, row, vec],
        out_specs=[pl.BlockSpec((8, 128), lambda i: (0, 0)), row, vec],
        out_shape=[jax.ShapeDtypeStruct((8, 128), F32), jax.ShapeDtypeStruct(x3.shape, F32), jax.ShapeDtypeStruct((1, d), F32)],
        compiler_params=pltpu.CompilerParams(dimension_semantics=("arbitrary",), vmem_limit_bytes=VMEM_LIMIT_BYTES),
    )(x3, t3, g)


_ADA_TN = 512


def ada_forward(name, c16, ada_w):
    depth, d, cols = ada_w.shape

    def body(c_ref, w_ref, o_ref):
        o_ref[0] = _mxu(_silu(c_ref[...]), w_ref[0], "nn")

    return pl.pallas_call(
        body, name=name, grid=(depth, cols // _ADA_TN),
        in_specs=[pl.BlockSpec((16, d), lambda l, j: (0, 0)), pl.BlockSpec((1, d, _ADA_TN), lambda l, j: (l, 0, j))],
        out_specs=pl.BlockSpec((1, 16, _ADA_TN), lambda l, j: (l, 0, j)),
        out_shape=jax.ShapeDtypeStruct((depth, 16, cols), F32),
        compiler_params=pltpu.CompilerParams(dimension_semantics=("arbitrary", "arbitrary"), vmem_limit_bytes=VMEM_LIMIT_BYTES),
    )(c16, ada_w)


def ada_backward(name, c16, dmod16, w, m, v):
    depth, d, cols = w.shape

    def body(c_ref, dm_ref, w_ref, m_ref, v_ref, g_ref, dl_ref, nm_ref, nv_ref):
        g = _mxu(_silu(c_ref[...]), dm_ref[0], "tn")
        _, (delta, nm, nv) = _adam_fn(None, None, [], [w_ref[0], g, m_ref[0], v_ref[0]], [])
        g_ref[0], dl_ref[0], nm_ref[0], nv_ref[0] = g, delta, nm, nv

    blk = pl.BlockSpec((1, d, _ADA_TN), lambda l, j: (l, 0, j))
    return pl.pallas_call(
        body, name=name, grid=(depth, cols // _ADA_TN),
        in_specs=[pl.BlockSpec((16, d), lambda l, j: (0, 0)), pl.BlockSpec((1, 16, _ADA_TN), lambda l, j: (l, 0, j)), blk, blk, blk],
        out_specs=[blk] * 4, out_shape=[jax.ShapeDtypeStruct(w.shape, F32)] * 4,
        compiler_params=pltpu.CompilerParams(dimension_semantics=("arbitrary", "arbitrary"), vmem_limit_bytes=VMEM_LIMIT_BYTES),
    )(c16, dmod16, w, m, v)


def _sum_fn(ci, b, carries, rows, vecs):
    acc = rows[0]
    for r in rows[1:]:
        acc = acc + r
    return [], [acc]


def sum_slots(name, a, nsum, out_dtype=F32):
    n, r, c = a.shape
    nb = n // nsum
    t = _tile(r, 256, 8)
    rows = [Row(a, fb=(lambda b, k=k: k * nb + b)) for k in range(nsum)]
    (out,), _ = scan_fwd(name, _sum_fn, nb=nb, nchunk=r // t, t=t, rows=rows, vecs=[], carries=[],
                         outs=[out_row((nb, r, c), out_dtype, fb=lambda b: b)], save=False)
    return out


def _sum_my_layer_fn(ci, b, carries, rows, vecs):
    layer0, layer1, theirs = rows
    return [], [jnp.where(lax.axis_index("c") == 0, layer0, layer1) + theirs]


def sum_cores(name, g, theirs, out_dtype):
    _, nb, r, c = g.shape
    g8 = g.reshape(2 * nb, r, c)
    t = _tile(r, 256, 8)
    rows = [Row(g8, fb=lambda b: b), Row(g8, fb=lambda b: nb + b), Row(theirs, fb=lambda b: b)]
    (out,), _ = scan_fwd(name, _sum_my_layer_fn, nb=nb, nchunk=r // t, t=t, rows=rows, vecs=[], carries=[],
                         outs=[out_row((nb, r, c), out_dtype, fb=lambda b: b)], save=False)
    return out


def _flip(mask, pos):
    return tuple((1 - p) if m else p for m, p in zip(mask, pos))


ALL_PEERS = [(a, b, c) for a in (0, 1) for b in (0, 1) for c in (0, 1)][1:]
CHIP_PEERS = [(1, 0, 0), (0, 1, 0), (1, 1, 0)]
SIBLING = [(0, 0, 1)]


def _divisor(size, target, unit):
    best = 1
    for n in range(1, target + 1):
        if size % n == 0 and (size // n) % unit == 0:
            best = n
    return best


def _pieces(src, dst, pieces):
    shape = src.shape
    unit = 16 if src.dtype == BF16 else 8
    if pieces <= 1:
        return [(src, dst)]
    if len(shape) == 2:
        n = _divisor(shape[0], pieces, unit)
        s = shape[0] // n
        return [(src.at[pl.ds(i * s, s)], dst.at[pl.ds(i * s, s)]) for i in range(n)]
    assert len(shape) == 3, shape
    n = _divisor(shape[1], max(pieces // shape[0], 1), unit)
    s = shape[1] // n
    return [(src.at[j, pl.ds(i * s, s)], dst.at[j, pl.ds(i * s, s)]) for j in range(shape[0]) for i in range(n)]


def comm_call(name, arrays, out_shapes, masks, src_fn, dst_fn, local_fn=None, pieces=1):
    na, npeer = len(arrays), len(masks)

    def body(*refs):
        ins, outs = refs[:na], refs[na:2 * na]
        send_sems, recv_sems, loc_sems = refs[2 * na:]
        me = (lax.axis_index("x"), lax.axis_index("y"), lax.axis_index("c"))
        local = []
        if local_fn is not None:
            for k in range(na):
                s, d = local_fn(k, ins[k], outs[k], me)
                for ps, pd in _pieces(s, d, pieces):
                    pltpu.make_async_copy(ps, pd, loc_sems.at[k]).start()
                local.append(pltpu.make_async_copy(s, d, loc_sems.at[k]))

        def remote(k, p, src, dst, to):
            return pltpu.make_async_remote_copy(
                src_ref=src, dst_ref=dst, send_sem=send_sems.at[k * npeer + p], recv_sem=recv_sems.at[k * npeer + p],
                device_id=to, device_id_type=MESH)

        for k in range(na):
            for p in range(npeer):
                peer = _flip(masks[p], me)
                for ps, pd in _pieces(src_fn(k, ins[k], me, peer), dst_fn(k, outs[k], me), pieces):
                    remote(k, p, ps, pd, peer).start()
        for k in range(na):
            for p in range(npeer):
                peer = _flip(masks[p], me)
                remote(k, p, src_fn(k, ins[k], me, peer), dst_fn(k, outs[k], peer), peer).wait_recv()
        for k in range(na):
            for p in range(npeer):
                peer = _flip(masks[p], me)
                remote(k, p, src_fn(k, ins[k], me, peer), dst_fn(k, outs[k], me), peer).wait_send()
        for cp in local:
            cp.wait()

    hbm = pl.BlockSpec(memory_space=pl.ANY)
    out = pl.pallas_call(
        body, name=name, in_specs=[hbm] * na, out_specs=[hbm] * na,
        out_shape=[jax.ShapeDtypeStruct(s, a.dtype) for s, a in zip(out_shapes, arrays)],
        scratch_shapes=[pltpu.SemaphoreType.DMA((na * npeer,)), pltpu.SemaphoreType.DMA((na * npeer,)),
                        pltpu.SemaphoreType.DMA((na,))],
    )(*arrays)
    return list(out)


def _dev(pos):
    return 4 * pos[0] + 2 * pos[1] + pos[2]


def _chip(pos):
    return 2 * pos[0] + pos[1]


def allgather8(name, a):
    (out,) = comm_call(name, [a], [(8,) + a.shape], ALL_PEERS,
                       src_fn=lambda k, r, me, peer: r, dst_fn=lambda k, o, sender: o.at[_dev(sender)],
                       local_fn=lambda k, r, o, me: (r, o.at[_dev(me)]))
    return out


def gather_layer_from_chips(name, arrays):
    return comm_call(name, arrays, [(4,) + a.shape[1:] for a in arrays], CHIP_PEERS,
                     src_fn=lambda k, r, me, peer: r.at[me[2]], dst_fn=lambda k, o, sender: o.at[_chip(sender)],
                     local_fn=lambda k, r, o, me: (r.at[me[2]], o.at[_chip(me)]), pieces=8)


def swap_layers(name, arrays, c):
    got = comm_call(name, arrays, [a.shape for a in arrays], SIBLING,
                    src_fn=lambda k, r, me, peer: r, dst_fn=lambda k, o, sender: o, pieces=32)
    return [[jnp.where(c == 0, a, g), jnp.where(c == 0, g, a)] for a, g in zip(arrays, got)]


def swap_other_layer(name, arrays):
    return comm_call(name, arrays, [a.shape[1:] for a in arrays], SIBLING,
                     src_fn=lambda k, r, me, peer: r.at[peer[2]], dst_fn=lambda k, o, sender: o, pieces=32)


def scatter_to_chips(name, arrays):
    return comm_call(name, arrays, [a.shape for a in arrays], CHIP_PEERS,
                     src_fn=lambda k, r, me, peer: r.at[_chip(peer)], dst_fn=lambda k, o, sender: o.at[_chip(sender)],
                     local_fn=lambda k, r, o, me: (r.at[_chip(me)], o.at[_chip(me)]), pieces=8)


def _rows_of(shape):
    return -(-int(np.prod(shape)) // 1024) * 8


def _pack(arrs):
    parts = []
    for a in arrs:
        flat = a.reshape(-1).astype(F32)
        parts.append(jnp.pad(flat, (0, _rows_of(a.shape) * 128 - flat.shape[0])).reshape(-1, 128))
    rows = sum(p.shape[0] for p in parts)
    parts.append(jnp.zeros(((-rows) % _ROW_T, 128), F32))
    return jnp.concatenate(parts, axis=0)


def _unpack(buf, shapes):
    out, o = [], 0
    for s in shapes:
        r, n = _rows_of(s), int(np.prod(s))
        out.append(buf[o:o + r].reshape(-1)[:n].reshape(s))
        o += r
    return out


_WEIGHTS = ["ada_w", "ada_b", "norm1_g", "w_in", "ssd_conv_w", "ssd_conv_b", "ssd_dt_bias", "ssd_a_log", "ssd_d", "ssd_norm_g",
            "pool_w", "pool_scale", "w_out", "norm2_g", "ffn_up", "ffn_conv_w", "ffn_conv_b", "ffn_down", "final_g"]
_BIG = ["w_in", "w_out", "ffn_up", "ffn_down"]
_SMALL = [n for n in _WEIGHTS if n not in _BIG and n != "ada_w"]
_COL_SHARDED_SMALL = {"ssd_conv_w": 256, "ffn_conv_w": 1408}


def _pad_lanes(v, n=128):
    return jnp.pad(v.astype(F32), (0, n - v.shape[0]))[None]


def _perm_cols(w):
    pad = jnp.zeros(w.shape[:-1] + (IN_WP - IN_W,), w.dtype)
    return jnp.concatenate([w[..., :1536], w[..., 1544:1800], w[..., 1536:1544], pad, w[..., 1800:]], axis=-1)


def _unperm_cols(g):
    return jnp.concatenate([g[..., :1536], g[..., 1792:1800], g[..., 1536:1792], g[..., IN_MAIN:]], axis=-1)


_CHIP2_PARTS = [(1284, 1536), (1792, 1800), (1536, 1792), (IN_MAIN, IN_MAIN + 126)]


def _w_in_chip_cols(gp):
    q = IN_W // 4
    return [gp[:, :q], gp[:, q:2 * q], jnp.concatenate([gp[:, a:b] for a, b in _CHIP2_PARTS], axis=1), gp[:, IN_WP - q:]]


def _w_in_from_chips(a):
    c2 = a[2]
    pad = jnp.zeros((a.shape[1], IN_WP - IN_W), a.dtype)
    return jnp.concatenate([a[0], a[1], c2[:, :252], c2[:, 260:516], c2[:, 252:260], pad, c2[:, 516:], a[3]], axis=1)


def _ffn_block_perm(a):
    n = a.shape[-1] // 4
    return jnp.concatenate([a[..., j * n:(j + 1) * n] for j in FFN_BLOCK_ORDER], axis=-1)


def _layer_forward(i, x3, modv, wts, sp, cs3, sn3):
    sh1, sc1, g1, sh2, sc2, g2 = modv
    big = lambda n: wts[n]() if callable(wts[n]) else wts[n]
    h1 = norm_mod_forward(f"l{i}_norm1", x3, wts["norm1_g"], sc1, sh1)
    proj3 = mm(f"l{i}_proj", h1[0], big("w_in")[:, :IN_MAIN], "nn")[None]
    qkv3 = mm(f"l{i}_qkv", h1[0], big("w_in")[:, IN_MAIN:], "nn")[None]
    y_ssd, sv_ssd = ssd_forward(f"l{i}_ssd", proj3, sp)
    y_pool, sv_pool = pool_forward(f"l{i}_pool", proj3, wts["wbd"], wts["pool_scale"])
    y_att, res_att = attention_forward(f"l{i}", qkv3, cs3, sn3)
    mix = jnp.concatenate([y_ssd, y_pool, y_att], axis=-1)
    out, x1 = mm(f"l{i}_wout", mix[0], big("w_out"), "nn", res=x3[0], gate=g1)
    x1 = x1[None]
    h2 = norm_mod_forward(f"l{i}_norm2", x1, wts["norm2_g"], sc2, sh2)
    up3 = mm(f"l{i}_up", h2[0], big("ffn_up"), "nn")[None]
    act, sv_ffn = ffn_mid_forward(f"l{i}_ffn", up3, wts["ffn_conv_w"], wts["ffn_conv_b"])
    dn, x2 = mm(f"l{i}_down", act[0], big("ffn_down"), "nn", res=x1[0], gate=g2)
    keep = dict(x=x3, h1=h1, proj3=proj3, qkv3=qkv3, sv_ssd=sv_ssd, sv_pool=sv_pool, res_att=res_att, mix=mix, out=out[None],
                x1=x1, h2=h2, up3=up3, act=act, sv_ffn=sv_ffn, dn=dn[None])
    return x2[None], keep


def _layer_backward(i, dx2, keep, modv, wts, sp, cs3, sn3, after=None):
    sh1, sc1, g1, sh2, sc2, g2 = modv
    k = keep
    big = lambda n: wts[n]() if callable(wts[n]) else wts[n]
    tell = lambda step, *a: after[step](*a) if after and step in after else None
    d_dn, d_g2 = gate_backward(f"l{i}_gate2_b", k["dn"], g2, dx2)
    d_act = mm(f"l{i}_down_bx", d_dn[0], big("ffn_down"), "nt")
    g_down = mm(f"l{i}_down_bw", k["act"][0], d_dn[0], "tn").reshape(4, FFN_DIM // 4, D_MODEL)
    (d_up,), dv_ffn = ffn_mid_backward(f"l{i}_ffn_b", k["up3"], wts["ffn_conv_w"], wts["ffn_conv_b"], k["sv_ffn"], d_act[None])
    tell("ffn_b")
    d_h2 = mm(f"l{i}_up_bx", d_up[0], big("ffn_up"), "nt")
    g_up = mm(f"l{i}_up_bw", k["h2"][0], d_up[0], "tn", tn=_FFN_CW,
              into=((4, D_MODEL, _FFN_CW), lambda r, c: ((c % 2) * 2 + c // 2, r, 0)))
    dx1, (d_n2, d_sc2, d_sh2) = norm_mod_backward(f"l{i}_norm2_b", k["x1"], wts["norm2_g"], sc2, sh2, d_h2[None], dx2)
    d_out, d_g1 = gate_backward(f"l{i}_gate1_b", k["out"], g1, dx1)
    d_mix = mm(f"l{i}_wout_bx", d_out[0], big("w_out"), "nt")[None]
    g_wout = mm(f"l{i}_wout_bw", k["mix"][0], d_out[0], "tn").reshape(4, D_MODEL // 4, D_MODEL)
    tell("wout_bw", g_wout, g_up, g_down)
    (dz, dxs, dbm, dcm, ddt), dv_ssd = ssd_backward(f"l{i}_ssd_b", k["proj3"], sp, k["sv_ssd"], d_mix)
    tell("ssd_b")
    (du_pool,), (d_wbd, d_pscale) = pool_backward(f"l{i}_pool_b", k["proj3"], wts["wbd"], wts["pool_scale"], k["sv_pool"], d_mix)
    d_qkv = attention_backward(f"l{i}", k["qkv3"], cs3, sn3, k["res_att"], d_mix)
    d_proj = jnp.concatenate([dz[0], dxs[0], dbm[0], dcm[0], du_pool[0], (ddt[0] + ddt[1]).astype(BF16), d_qkv[0]], axis=-1)
    g_win = jnp.stack(_w_in_chip_cols(mm(f"l{i}_proj_bw", k["h1"][0], d_proj, "tn")))
    tell("proj_bw", g_win)
    d_h1 = mm(f"l{i}_proj_bx", d_proj, big("w_in"), "nt")
    tell("proj_bx")
    dx, (d_n1, d_sc1, d_sh1) = norm_mod_backward(f"l{i}_norm1_b", k["x"], wts["norm1_g"], sc1, sh1, d_h1[None], dx1)
    dcwx, dcbx, dcwb, dcbb, dcwc, dcbc, ddtb, dalog, ddsk, dng = dv_ssd
    small = dict(
        norm1_g=d_n1[0], norm2_g=d_n2[0],
        ssd_conv_w=jnp.concatenate([dcwx[:, :512], dcwb[:, 512:768], dcwc[:, 768:]], axis=1),
        ssd_conv_b=jnp.concatenate([dcbx[0, :512], dcbb[0, 512:768], dcbc[0, 768:]]),
        ssd_dt_bias=ddtb[0, :8], ssd_a_log=dalog[0, :8], ssd_d=ddsk[0, :8], ssd_norm_g=dng[0],
        pool_w=jnp.stack([d_wbd[64 * g:64 * g + 64, 64 * g:64 * g + 64] for g in range(4)]), pool_scale=d_pscale[0],
        ffn_conv_w=_ffn_block_perm(dv_ffn[0]), ffn_conv_b=_ffn_block_perm(dv_ffn[1][0]),
    )
    dmod = jnp.concatenate([d_sh1[0], d_sc1[0], d_g1[0], d_sh2[0], d_sc2[0], d_g2[0]])
    return dx, [g_win, g_wout, g_up, g_down], small, dmod


def kernel(x, c, positions, ada_w, ada_b, norm1_g, w_in, ssd_conv_w, ssd_conv_b, ssd_dt_bias, ssd_a_log, ssd_d, ssd_norm_g, pool_w, pool_scale, w_out, norm2_g, ffn_up, ffn_conv_w, ffn_conv_b, ffn_down, final_g, loss_target, m_ada_w, m_ada_b, m_norm1_g, m_w_in, m_ssd_conv_w, m_ssd_conv_b, m_ssd_dt_bias, m_ssd_a_log, m_ssd_d, m_ssd_norm_g, m_pool_w, m_pool_scale, m_w_out, m_norm2_g, m_ffn_up, m_ffn_conv_w, m_ffn_conv_b, m_ffn_down, m_final_g, v_ada_w, v_ada_b, v_norm1_g, v_w_in, v_ssd_conv_w, v_ssd_conv_b, v_ssd_dt_bias, v_ssd_a_log, v_ssd_d, v_ssd_norm_g, v_pool_w, v_pool_scale, v_w_out, v_norm2_g, v_ffn_up, v_ffn_conv_w, v_ffn_conv_b, v_ffn_down, v_final_g):
    args = dict(locals())
    w = {n: args[n] for n in _WEIGHTS}
    m = {n: args["m_" + n] for n in _WEIGHTS}
    v = {n: args["v_" + n] for n in _WEIGHTS}
    d = D_MODEL
    me = (lax.axis_index("x"), lax.axis_index("y"), lax.axis_index("c"))
    chip, dev = _chip(me), _dev(me)
    RIDERS.reset()

    shapes0 = [c.shape, ssd_conv_w.shape, ffn_conv_w.shape]
    g0 = allgather8("gather_c_conv", _pack([c, ssd_conv_w, ffn_conv_w]))
    c16 = jnp.pad(g0[:, :d // 128, :].reshape(8, d), ((0, 8), (0, 0)))
    by_chip = [_unpack(g0[2 * j], shapes0) for j in range(4)]
    conv_w_full = jnp.concatenate([p[1] for p in by_chip], axis=-1)
    fconv_w_full = jnp.concatenate([p[2] for p in by_chip], axis=-1)

    modp = ada_forward("ada_fwd", c16, ada_w)[:, :8]
    g1 = allgather8("gather_mod", _pack([modp]))
    modfull = jnp.concatenate([_unpack(g1[2 * j], [modp.shape])[0] for j in range(4)], axis=-1)
    mod = lax.dynamic_index_in_dim(modfull, dev, axis=1, keepdims=False) + ada_b
    modv = [[mod[i, q * d:(q + 1) * d][None] for q in range(6)] for i in range(DEPTH)]

    shards = [w[n].astype(BF16) for n in _BIG]

    def weight(k, layer, got):
        parts = [jnp.where(chip == j, shards[k][layer], got[j]) for j in range(4)]
        if k == 0:
            return _w_in_from_chips(parts)
        return jnp.concatenate([parts[j] for j in FFN_BLOCK_ORDER], axis=1) if k == 2 else jnp.concatenate(parts, axis=0)

    def later(k, layer, *sources):
        made = []

        def get():
            if not made:
                got = [RIDERS.result(host)[pos] for host, pos in sources]
                made.append(weight(k, layer, got[0] if len(got) == 1 else jnp.concatenate(got, axis=1)))
            return made[0]
        return get

    cs3, sn3 = rope_tables(positions[0])
    eye4 = jnp.eye(4, dtype=F32)
    wts, sps = [], []
    for i in range(DEPTH):
        wts.append(dict(
            norm1_g=norm1_g[i][None], norm2_g=norm2_g[i][None], pool_scale=pool_scale[i][None],
            wbd=(eye4[:, None, :, None] * pool_w[i][:, :, None, :]).reshape(POOL_W, POOL_W),
            ffn_conv_w=_ffn_block_perm(fconv_w_full[i]), ffn_conv_b=_ffn_block_perm(ffn_conv_b[i])[None]))
        sps.append(dict(cw=conv_w_full[i], cb=ssd_conv_b[i][None], dtb=_pad_lanes(ssd_dt_bias[i]), alog=_pad_lanes(ssd_a_log[i]),
                        dsk=_pad_lanes(ssd_d[i]), ng=ssd_norm_g[i][None]))

    (w_in0,) = ride_alone("gather_w_in0", gather_ride(0, [shards[0]]))
    RIDERS.book("l0_ssd", gather_ride(0, [shards[1], shards[3]]))
    half = shards[2].shape[1] // 2
    RIDERS.book("l0_attn0", gather_ride(0, [shards[2][:, :half]]))
    RIDERS.book("l0_attn1", gather_ride(0, [shards[2][:, half:]]))
    wts[0].update(w_in=weight(0, 0, w_in0), w_out=later(1, 0, ("l0_ssd", 0)), ffn_down=later(3, 0, ("l0_ssd", 1)),
                  ffn_up=later(2, 0, ("l0_attn0", 0), ("l0_attn1", 0)))
    RIDERS.book("l0_attn2", gather_ride(1, [shards[0], shards[1]]))
    RIDERS.book("l0_ffn", gather_ride(1, [shards[2]]))
    RIDERS.book("l0_down", gather_ride(1, [shards[3]]))
    wts[1].update(w_in=later(0, 1, ("l0_attn2", 0)), w_out=later(1, 1, ("l0_attn2", 1)), ffn_up=later(2, 1, ("l0_ffn", 0)),
                  ffn_down=later(3, 1, ("l0_down", 0)))
    x1_, keep0 = _layer_forward(0, x, modv[0], wts[0], sps[0], cs3, sn3)
    xc, keep1 = _layer_forward(1, x1_, modv[1], wts[1], sps[1], cs3, sn3)
    keeps = [keep0, keep1]
    lossblk, dx, d_final = final_loss("final_loss", xc, loss_target, final_g[None])
    loss = lax.psum(lossblk[0, 0], ("x", "y", "c"))

    small_g, dmods = [None] * DEPTH, [None] * DEPTH
    part_sum, from_chips = [[None] * 4 for _ in range(DEPTH)], [[None] * 4 for _ in range(DEPTH)]

    def owner_sum(layer, ks, mine, theirs):
        for k, g, t in zip(ks, mine, theirs):
            part_sum[layer][k] = add_arrays(f"sum_cores{layer}_{_BIG[k]}", [g, t], BF16)

    dx, by_chip1, small_g[1], dmods[1] = _layer_backward(1, dx, keeps[1], modv[1], wts[1], sps[1], cs3, sn3)
    RIDERS.book("l0_ffn_b", to_owner_ride(1, by_chip1))

    def after_ffn_b():
        owner_sum(1, range(4), by_chip1, RIDERS.result("l0_ffn_b"))
        RIDERS.book("l0_up_bx", scatter_ride(1, [part_sum[1][2]]))
        RIDERS.book("l0_up_bw", scatter_ride(1, [part_sum[1][0], part_sum[1][1]]))
        RIDERS.book("l0_norm2_b", scatter_ride(1, [part_sum[1][3]]))

    early = []

    def after_wout_bw(g_wout, g_up, g_down):
        early.extend([g_wout, g_up, g_down])
        RIDERS.book("l0_ssd_b", to_owner_ride(0, early))

    def after_ssd_b():
        owner_sum(0, [1, 2, 3], early, RIDERS.result("l0_ssd_b"))
        for host, k in (("l0_attn0_b", 2), ("l0_attn1_b", 3), ("l0_attn2_b", 1)):
            RIDERS.book(host, scatter_ride(0, [part_sum[0][k]]))

    last = []

    def after_proj_bw(g_win):
        last.append(g_win)
        RIDERS.book("l0_proj_bx", to_owner_ride(0, last))

    def after_proj_bx():
        owner_sum(0, [0], last, RIDERS.result("l0_proj_bx"))
        RIDERS.book("l0_norm1_b", scatter_ride(0, [part_sum[0][0]]))

    hooks = dict(ffn_b=after_ffn_b, wout_bw=after_wout_bw, ssd_b=after_ssd_b, proj_bw=after_proj_bw, proj_bx=after_proj_bx)
    dx, _, small_g[0], dmods[0] = _layer_backward(0, dx, keeps[0], modv[0], wts[0], sps[0], cs3, sn3, after=hooks)
    from_chips[1][2], (from_chips[1][0], from_chips[1][1]) = RIDERS.result("l0_up_bx")[0], RIDERS.result("l0_up_bw")
    from_chips[1][3] = RIDERS.result("l0_norm2_b")[0]
    for host, k in (("l0_attn0_b", 2), ("l0_attn1_b", 3), ("l0_attn2_b", 1), ("l0_norm1_b", 0)):
        from_chips[0][k] = RIDERS.result(host)[0]
    mine = [sum_chips_mine(f"sum_chips_{n}", part_sum[0][k], from_chips[0][k], part_sum[1][k], from_chips[1][k])
            for k, n in enumerate(_BIG)]
    reduced = swap_layers("swap_r", mine, me[2])
    grads = {n: jnp.stack(r) for n, r in zip(_BIG, reduced)}

    part = dict(ada_b=jnp.stack(dmods), final_g=d_final[0])
    for n in _SMALL:
        if n not in part:
            part[n] = jnp.stack([small_g[i][n] for i in range(DEPTH)])
    full_shapes = [part[n].shape for n in _SMALL]
    gs = allgather8("gather_small", _pack([part[n] for n in _SMALL]))
    tot = _unpack(sum_slots("sum_small", gs, 8)[0], full_shapes)
    small_tot = dict(zip(_SMALL, tot))
    dmod_all = gs[:, :DEPTH * 6 * d // 128, :].reshape(8, DEPTH, 6 * d)
    for n, ncol in _COL_SHARDED_SMALL.items():
        small_tot[n] = lax.dynamic_slice_in_dim(small_tot[n], chip * ncol, ncol, axis=2)
    grads.update(small_tot)

    ncol = ada_w.shape[2]
    dm = lax.dynamic_slice_in_dim(dmod_all, chip * ncol, ncol, axis=2).transpose(1, 0, 2)
    upd = {}
    g_ada, *upd["ada_w"] = ada_backward("ada_bwd", c16, jnp.pad(dm, ((0, 0), (0, 8), (0, 0))), ada_w, m["ada_w"], v["ada_w"])
    grads["ada_w"] = g_ada

    for n in _BIG:
        upd[n] = adamw(f"adam_{n}", w[n], grads[n], m[n], v[n])
    shapes_s = [w[n].shape for n in _SMALL]
    packed = [_pack([src[n] for n in _SMALL]) for src in (w, grads, m, v)]
    outs_s = [_unpack(o, shapes_s) for o in adamw("adam_small", *packed)]
    for q, n in enumerate(_SMALL):
        upd[n] = [outs_s[0][q], outs_s[1][q], outs_s[2][q]]

    return (loss, dx, *[grads[n] for n in _WEIGHTS], *[upd[n][0] for n in _WEIGHTS], *[upd[n][1] for n in _WEIGHTS],
            *[upd[n][2] for n in _WEIGHTS])


class Ride:
    def __init__(self, ins, out_shapes, nsem, start, finish):
        self.ins, self.out_shapes, self.nsem, self.start, self.finish = ins, out_shapes, nsem, start, finish

    def specs(self):
        hbm = pl.BlockSpec(memory_space=pl.ANY)
        return [hbm] * len(self.ins), [hbm] * len(self.out_shapes), [pltpu.SemaphoreType.DMA((self.nsem,))] * 2

    def begin(self, in_refs, out_refs, sems, cond=None):
        me = (lax.axis_index("x"), lax.axis_index("y"), lax.axis_index("c"))
        go = lambda: self.start(in_refs, out_refs, sems[0], sems[1], me)
        go() if cond is None else pl.when(cond)(go)

    def end(self, in_refs, out_refs, sems, cond=None):
        me = (lax.axis_index("x"), lax.axis_index("y"), lax.axis_index("c"))
        go = lambda: self.finish(in_refs, out_refs, sems[0], sems[1], me)
        go() if cond is None else pl.when(cond)(go)


def ride_alone(name, ride):
    ni, no = len(ride.ins), len(ride.out_shapes)

    def body(*refs):
        ride.begin(refs[:ni], refs[ni:ni + no], refs[ni + no:])
        ride.end(refs[:ni], refs[ni:ni + no], refs[ni + no:])

    in_specs, out_specs, scratch = ride.specs()
    return list(pl.pallas_call(body, name=name, in_specs=in_specs, out_specs=out_specs, out_shape=ride.out_shapes,
                               scratch_shapes=scratch)(*ride.ins))


def mm(name, a, b, mode, out_dtype=F32, res=None, gate=None, tm=1408, tn=1536, tk=1408, into=None):
    ride = RIDERS.take(name)
    if mode == "nn":
        (m, k), n = a.shape, b.shape[1]
    elif mode == "nt":
        (m, k), n = a.shape, b.shape[0]
    else:
        (k, m), n = a.shape, b.shape[1]
    tm, tn, tk = _tile(m, tm), _tile(n, tn), _tile(k, tk)
    ni, nj, nk = m // tm, n // tn, k // tk
    a_spec = pl.BlockSpec((tk, tm), lambda i, j, q: (q, i)) if mode == "tn" else pl.BlockSpec((tm, tk), lambda i, j, q: (i, q))
    b_spec = pl.BlockSpec((tn, tk), lambda i, j, q: (j, q)) if mode == "nt" else pl.BlockSpec((tk, tn), lambda i, j, q: (q, j))
    o_spec = pl.BlockSpec((tm, tn), lambda i, j, q: (i, j))
    fused = res is not None
    lead = 0 if into is None else len(into[0]) - 2
    first = (0,) * lead + (slice(None), slice(None))
    ins, in_specs = [a, b], [a_spec, b_spec]
    out_shape, out_specs = [jax.ShapeDtypeStruct((m, n), out_dtype)], [o_spec]
    if fused:
        ins += [res, gate]
        in_specs += [o_spec, pl.BlockSpec((1, tn), lambda i, j, q: (0, j))]
        out_shape.append(jax.ShapeDtypeStruct((m, n), F32))
        out_specs.append(o_spec)
    if into is not None:
        shape, omap = into
        out_shape = [jax.ShapeDtypeStruct(shape, out_dtype)]
        out_specs = [pl.BlockSpec((1,) * lead + (tm, tn), lambda i, j, q: omap(i, j))]
    n_in, n_out = len(ins), len(out_shape)
    scratch = [pltpu.VMEM((tm, tn), F32)]
    if ride is not None:
        r_in, r_out, r_scr = ride.specs()
        ins, in_specs = ins + list(ride.ins), in_specs + r_in
        out_shape, out_specs = out_shape + list(ride.out_shapes), out_specs + r_out
        scratch = scratch + r_scr

    def body(*refs):
        a_ref, b_ref = refs[:2]
        o_ref = refs[len(ins)]
        acc = refs[len(ins) + len(out_shape)]
        i, j, q = pl.program_id(0), pl.program_id(1), pl.program_id(2)
        at = lambda x, y, z: jnp.logical_and(jnp.logical_and(i == x, j == y), q == z)
        r_refs = (refs[n_in:len(ins)], refs[len(ins) + n_out:len(ins) + len(out_shape)], refs[len(ins) + len(out_shape) + 1:])
        if ride is not None:
            ride.begin(*r_refs, at(0, 0, 0))

        @pl.when(q == 0)
        def _():
            acc[...] = jnp.zeros(acc.shape, F32)

        acc[...] += _mxu(a_ref[...], b_ref[...], mode)

        @pl.when(q == nk - 1)
        def _():
            o_ref[first] = acc[...].astype(o_ref.dtype)
            if fused:
                refs[len(ins) + 1][...] = refs[2][...] + refs[3][...] * acc[...]

        if ride is not None:
            ride.end(*r_refs, at(ni - 1, nj - 1, nk - 1))

    sem = ("arbitrary",) * 3 if ride is not None else ("parallel", "parallel", "arbitrary")
    out = pl.pallas_call(
        body, name=name, grid=(ni, nj, nk), in_specs=in_specs, out_specs=out_specs, out_shape=out_shape, scratch_shapes=scratch,
        compiler_params=pltpu.CompilerParams(dimension_semantics=sem, vmem_limit_bytes=VMEM_LIMIT_BYTES),
    )(*ins)
    if ride is not None:
        RIDERS.done[name] = list(out[n_out:])
    return tuple(out[:n_out]) if fused else out[0]


def add_arrays(name, arrs, out_dtype=F32):
    nb, r, c = arrs[0].shape
    t = _tile(r, 256, 8)
    (out,), _ = scan_fwd(name, _sum_fn, nb=nb, nchunk=r // t, t=t, rows=[Row(a, fb=lambda b: b) for a in arrs], vecs=[], carries=[],
                         outs=[out_row((nb, r, c), out_dtype, fb=lambda b: b)], save=False)
    return out


def _sum_chips_mine_fn(ci, b, carries, rows, vecs):
    mine_layer = lax.axis_index("c")
    chip = 2 * lax.axis_index("x") + lax.axis_index("y")
    tot = None
    for j in range(4):
        own = jnp.where(mine_layer == 0, rows[j], rows[8 + j])
        sent = jnp.where(mine_layer == 0, rows[4 + j], rows[12 + j])
        term = jnp.where(chip == j, own, sent)
        tot = term if tot is None else tot + term
    return [], [tot]


def sum_chips_mine(name, p0, q0, p1, q1):
    _, r, c = p0.shape
    t = _tile(r, 256, 8)
    rows = [Row(a, fb=(lambda b, j=j: j)) for a in (p0, q0, p1, q1) for j in range(4)]
    (out,), _ = scan_fwd(name, _sum_chips_mine_fn, nb=1, nchunk=r // t, t=t, rows=rows, vecs=[], carries=[],
                         outs=[out_row((1, r, c))], save=False)
    return out[0]


def _remote(src, dst, send_sems, recv_sems, k, to):
    return pltpu.make_async_remote_copy(src_ref=src, dst_ref=dst, send_sem=send_sems.at[k], recv_sem=recv_sems.at[k],
                                        device_id=to, device_id_type=MESH)


def gather_ride(layer, shards):
    na = len(shards)

    def start(ins, outs, ss, rs, me):
        @pl.when(me[2] == layer)
        def _():
            for k in range(na):
                for p, mask in enumerate(CHIP_PEERS):
                    _remote(ins[k].at[layer], outs[k].at[_chip(me)], ss, rs, 6 * k + p, _flip(mask, me)).start()

    def finish(ins, outs, ss, rs, me):
        sibling = _flip(SIBLING[0], me)

        @pl.when(me[2] == layer)
        def _():
            for k in range(na):
                for p, mask in enumerate(CHIP_PEERS):
                    slot = outs[k].at[_chip(_flip(mask, me))]
                    _remote(ins[k].at[layer], slot, ss, rs, 6 * k + p, _flip(mask, me)).wait_recv()
                    _remote(slot, slot, ss, rs, 6 * k + 3 + p, sibling).start()
            for k in range(na):
                for p, mask in enumerate(CHIP_PEERS):
                    slot = outs[k].at[_chip(_flip(mask, me))]
                    _remote(ins[k].at[layer], slot, ss, rs, 6 * k + p, _flip(mask, me)).wait_send()
                    _remote(slot, slot, ss, rs, 6 * k + 3 + p, sibling).wait_send()

        @pl.when(me[2] != layer)
        def _():
            for k in range(na):
                for p, mask in enumerate(CHIP_PEERS):
                    slot = outs[k].at[_chip(_flip(mask, me))]
                    _remote(slot, slot, ss, rs, 6 * k + 3 + p, sibling).wait_recv()

    return Ride(list(shards), [jax.ShapeDtypeStruct((4,) + a.shape[1:], a.dtype) for a in shards], 6 * na, start, finish)


def scatter_ride(layer, parts):
    na = len(parts)

    def start(ins, outs, ss, rs, me):
        @pl.when(me[2] == layer)
        def _():
            for k in range(na):
                for p, mask in enumerate(CHIP_PEERS):
                    peer = _flip(mask, me)
                    _remote(ins[k].at[_chip(peer)], outs[k].at[_chip(me)], ss, rs, 3 * k + p, peer).start()

    def finish(ins, outs, ss, rs, me):
        @pl.when(me[2] == layer)
        def _():
            for k in range(na):
                for p, mask in enumerate(CHIP_PEERS):
                    peer = _flip(mask, me)
                    _remote(ins[k].at[_chip(peer)], outs[k].at[_chip(peer)], ss, rs, 3 * k + p, peer).wait_recv()
                    _remote(ins[k].at[_chip(peer)], outs[k].at[_chip(me)], ss, rs, 3 * k + p, peer).wait_send()

    return Ride(list(parts), [jax.ShapeDtypeStruct(a.shape, a.dtype) for a in parts], 3 * na, start, finish)


def to_owner_ride(layer, arrays):
    na = len(arrays)

    def start(ins, outs, ss, rs, me):
        @pl.when(me[2] != layer)
        def _():
            for k in range(na):
                _remote(ins[k], outs[k], ss, rs, k, _flip(SIBLING[0], me)).start()

    def finish(ins, outs, ss, rs, me):
        for k in range(na):
            cp = _remote(ins[k], outs[k], ss, rs, k, _flip(SIBLING[0], me))
            pl.when(me[2] != layer)(cp.wait_send)
            pl.when(me[2] == layer)(cp.wait_recv)

    return Ride(list(arrays), [jax.ShapeDtypeStruct(a.shape, a.dtype) for a in arrays], na, start, finish)


def _w_in_from_chips(a):
    c2 = a[2]
    pad = jnp.zeros((c2.shape[0], IN_WP - IN_W), c2.dtype)
    return jnp.concatenate([a[0], a[1], c2[:, :252], c2[:, 260:516], c2[:, 252:260], pad, c2[:, 516:], a[3]], axis=1)


def _mm_host(rides, rode, key, *args, **kw):
    if rides is None or key not in rides:
        return mm(*args, **kw)
    main, rode[key] = mm(*args, ride=rides[key], **kw)
    return main
```

```python
import functools
import math

import numpy as np
import jax
import jax.numpy as jnp
from jax import lax
from jax.experimental import pallas as pl
from jax.experimental.pallas import tpu as pltpu

F32 = jnp.float32
BF16 = jnp.bfloat16
HI = lax.Precision.HIGHEST
MESH = pl.DeviceIdType.MESH

D_MODEL = 1024
SEQ = 4096
DEPTH = 2
SSD_INNER = 512
SSD_HEADS = 8
SSD_STATE = 128
POOL_W = 256
POOL_WINDOWS = (2, 4, 8, 16)
ATT_W = 256
ATT_HEADS = 4
ATT_HEAD_DIM = 64
ATT_PATTERNS = ((128, 1), (512, 4), (2048, 16))
ATT_BLOCK = 128
ROT_DIM = 16
ROPE_THETA = 500000.0
IN_W = 2568
IN_WP = 2688
IN_MAIN = 1920
FFN_DIM = 2816
NORM_EPS = 1e-6
ADAM_LR, ADAM_B1, ADAM_B2, ADAM_EPS, ADAM_WD, ADAM_STEP = 0.001, 0.9, 0.999, 1e-08, 0.01, 10

VMEM_LIMIT_BYTES = 56 * 1024 * 1024
NEG = -1e30


def _mxu(a, b, mode):
    dims = {"nn": ((1,), (0,)), "nt": ((1,), (1,)), "tn": ((0,), (0,))}[mode]
    return lax.dot_general(a.astype(BF16), b.astype(BF16), (dims, ((), ())), preferred_element_type=F32)


@functools.partial(jax.custom_vjp, nondiff_argnums=(2,))
def _bdot(a, b, mode):
    return _mxu(a, b, mode)


def _bdot_fwd(a, b, mode):
    return _mxu(a, b, mode), (a, b)


def _bdot_bwd(mode, res, g):
    a, b = res
    if mode == "nn":
        return _mxu(g, b, "nt"), _mxu(a, g, "tn")
    if mode == "nt":
        return _mxu(g, b, "nn"), _mxu(g, a, "tn")
    return _mxu(b, g, "nt"), _mxu(a, g, "nn")


_bdot.defvjp(_bdot_fwd, _bdot_bwd)


def _fxu(a, b, mode):
    dims = {"nn": ((1,), (0,)), "nt": ((1,), (1,)), "tn": ((0,), (0,))}[mode]
    return lax.dot_general(a, b, (dims, ((), ())), precision=HI, preferred_element_type=F32)


@functools.partial(jax.custom_vjp, nondiff_argnums=(2,))
def _fdot(a, b, mode):
    return _fxu(a, b, mode)


def _fdot_fwd(a, b, mode):
    return _fxu(a, b, mode), (a, b)


def _fdot_bwd(mode, res, g):
    a, b = res
    if mode == "nn":
        return _fxu(g, b, "nt"), _fxu(a, g, "tn")
    if mode == "nt":
        return _fxu(g, b, "nn"), _fxu(g, a, "tn")
    return _fxu(b, g, "nt"), _fxu(a, g, "nn")


_fdot.defvjp(_fdot_fwd, _fdot_bwd)


def _iota(shape, dim):
    return lax.broadcasted_iota(jnp.int32, shape, dim)


def _make_shift(h):
    @functools.partial(jax.custom_vjp, nondiff_argnums=(2,))
    def shift(halo, cur, k):
        if k == 0:
            return cur
        full = jnp.concatenate([halo, cur], axis=0)
        return pltpu.roll(full, k, 0)[h:]

    def fwd(halo, cur, k):
        return shift(halo, cur, k), None

    def bwd(k, _, g):
        t, w = g.shape
        if k == 0:
            return jnp.zeros((h, w), F32), g
        d_cur = jnp.where(_iota((t, w), 0) < t - k, pltpu.roll(g, t - k, 0), 0.0)
        top = g[:h]
        d_halo = jnp.where(_iota((h, w), 0) >= h - k, pltpu.roll(top, h - k, 0) if k < h else top, 0.0)
        return d_halo, d_cur

    shift.defvjp(fwd, bwd)
    return shift


_shift8 = _make_shift(8)
_shift16 = _make_shift(16)


def _make_tail(h):
    @jax.custom_vjp
    def tail(x):
        return x[x.shape[0] - h:]

    def fwd(x):
        return tail(x), x.shape[0]

    def bwd(t, g):
        return (jnp.concatenate([jnp.zeros((t - h, g.shape[1]), F32), g], axis=0),)

    tail.defvjp(fwd, bwd)
    return tail


_tail8 = _make_tail(8)
_tail16 = _make_tail(16)


@jax.custom_vjp
def _cumsum_rows(x):
    t = x.shape[0]
    row, s = _iota(x.shape, 0), 1
    while s < t:
        x = x + jnp.where(row >= s, pltpu.roll(x, s, 0), 0.0)
        s *= 2
    return x


def _cumsum_rows_fwd(x):
    return _cumsum_rows(x), None


def _cumsum_rows_bwd(_, g):
    t = g.shape[0]
    row, s = _iota(g.shape, 0), 1
    while s < t:
        g = g + jnp.where(row < t - s, pltpu.roll(g, t - s, 0), 0.0)
        s *= 2
    return (g,)


_cumsum_rows.defvjp(_cumsum_rows_fwd, _cumsum_rows_bwd)


@jax.custom_vjp
def _rot_pairs(t):
    e = _iota(t.shape, 1) % ATT_HEAD_DIM
    n = t.shape[1]
    return jnp.where(e < 8, -pltpu.roll(t, n - 8, 1), jnp.where(e < 16, pltpu.roll(t, 8, 1), 0.0))


def _rot_pairs_fwd(t):
    return _rot_pairs(t), None


def _rot_pairs_bwd(_, g):
    e = _iota(g.shape, 1) % ATT_HEAD_DIM
    n = g.shape[1]
    return (pltpu.roll(jnp.where(e < 8, -g, 0.0), 8, 1) + pltpu.roll(jnp.where(jnp.logical_and(e >= 8, e < 16), g, 0.0), n - 8, 1),)


_rot_pairs.defvjp(_rot_pairs_fwd, _rot_pairs_bwd)


def _make_thirds():
    @jax.custom_vjp
    def thirds(x):
        w = x.shape[1] // 3
        return x[:, :w], x[:, w:2 * w], x[:, 2 * w:]

    def fwd(x):
        return thirds(x), None

    def bwd(_, g):
        return (jnp.concatenate(g, axis=1),)

    thirds.defvjp(fwd, bwd)
    return thirds


_thirds = _make_thirds()


def _rowk(w, k):
    return jnp.sum(jnp.where(_iota(w.shape, 0) == k, w, 0.0), axis=0, keepdims=True)


def _silu(x):
    return x * (0.5 * jnp.tanh(0.5 * x) + 0.5)


def _softplus(x):
    return jnp.maximum(x, 0.0) + jnp.log(1.0 + jnp.exp(-jnp.abs(x)))


def _tile(dim, target, unit=128):
    if dim <= target:
        return dim
    best = None
    for t in range(unit, target + 1, unit):
        if dim % t == 0:
            best = t
    assert best is not None, (dim, target)
    return best


class Ride:
    def __init__(self, ins, out_shapes, nsem, start, finish):
        self.ins, self.out_shapes, self.nsem, self.start, self.finish = ins, out_shapes, nsem, start, finish

    def specs(self):
        hbm = pl.BlockSpec(memory_space=pl.ANY)
        return [hbm] * len(self.ins), [hbm] * len(self.out_shapes), [pltpu.SemaphoreType.DMA((self.nsem,))] * 2

    def begin(self, in_refs, out_refs, sems, cond=None):
        me = (lax.axis_index("x"), lax.axis_index("y"), lax.axis_index("c"))
        go = lambda: self.start(in_refs, out_refs, sems[0], sems[1], me)
        go() if cond is None else pl.when(cond)(go)

    def end(self, in_refs, out_refs, sems, cond=None):
        me = (lax.axis_index("x"), lax.axis_index("y"), lax.axis_index("c"))
        go = lambda: self.finish(in_refs, out_refs, sems[0], sems[1], me)
        go() if cond is None else pl.when(cond)(go)


class _Riders:
    def reset(self):
        self.booked, self.done = {}, {}

    def book(self, host, ride):
        assert host not in self.booked, host
        self.booked[host] = ride

    def take(self, host):
        return self.booked.pop(host, None)

    def result(self, host):
        return self.done[host]


RIDERS = _Riders()
RIDERS.reset()


class Row:
    def __init__(self, arr, w=None, fb=None, fc=None, diff=True, slot=False, dcols=None, dfc=None, ddtype=F32, view=None):
        self.ddtype = ddtype
        self.view = view
        self.arr = arr
        self.w = arr.shape[2] if w is None else w
        self.fb = (lambda b: 0) if fb is None else fb
        self.fc = (lambda b: 0) if fc is None else fc
        self.diff = diff
        self.slot = slot
        self.dcols = dcols
        self.dfc = dfc


class Vec:
    def __init__(self, arr, w=None, fc=None, diff=True):
        self.arr = arr
        self.w = arr.shape[1] if w is None else w
        self.fc = fc
        self.diff = diff


def _row_spec(r, t, nchunk, reverse):
    shape = (1, t, r.w) if r.view is None else (1, t // r.view, r.view * r.w)
    if reverse:
        return pl.BlockSpec(shape, lambda b, i, r=r: (r.fb(b), nchunk - 1 - i, r.fc(b)))
    return pl.BlockSpec(shape, lambda b, i, r=r: (r.fb(b), i, r.fc(b)))


def _load_row(ref, r, t, scr):
    if r.view is None:
        return ref[0]
    d, w = r.view, r.w
    for q in range(d):
        for j in range(w // 128):
            scr[j, pl.ds(q, t // d, stride=d), :] = ref[0, :, q * w + 128 * j:q * w + 128 * (j + 1)].astype(F32)
    return jnp.concatenate([scr[j] for j in range(w // 128)], axis=1)


def _store_row(ref, r, t, scr, val):
    if r.view is None:
        ref[0] = val.astype(ref.dtype)
        return
    d, w = r.view, r.w
    for j in range(w // 128):
        scr[j] = val[:, 128 * j:128 * (j + 1)]
    for q in range(d):
        for j in range(w // 128):
            ref[0, :, q * w + 128 * j:q * w + 128 * (j + 1)] = scr[j, pl.ds(q, t // d, stride=d), :].astype(ref.dtype)


def _view_scratch(specs, t):
    ws = [r.w for r in specs if r.view is not None]
    return [pltpu.VMEM((max(ws) // 128, t, 128), F32)] if ws else []


def _vec_spec(v):
    if v.fc is None:
        return pl.BlockSpec(v.arr.shape, lambda b, i: (0, 0))
    return pl.BlockSpec((v.arr.shape[0], v.w), lambda b, i, v=v: (0, v.fc(b)))


def _cparams():
    return pltpu.CompilerParams(dimension_semantics=("arbitrary", "arbitrary"), vmem_limit_bytes=VMEM_LIMIT_BYTES)


def scan_fwd(name, fn, *, nb, nchunk, t, rows, vecs, carries, outs, save):
    nr, nv, nc, no = len(rows), len(vecs), len(carries), len(outs)
    ns = nc if save else 0
    ride = RIDERS.take(name)
    r_in, r_out, r_scr = ride.specs() if ride else ([], [], [])

    def body(*refs):
        p = 0
        row_refs = refs[p:p + nr]; p += nr
        vec_refs = refs[p:p + nv]; p += nv
        ride_in = refs[p:p + len(r_in)]; p += len(r_in)
        out_refs = refs[p:p + no]; p += no
        save_refs = refs[p:p + ns]; p += ns
        ride_out = refs[p:p + len(r_out)]; p += len(r_out)
        car = refs[p:p + nc]; p += nc
        scr = refs[p] if stage else None
        sems = refs[p + len(stage):]
        b, i = pl.program_id(0), pl.program_id(1)
        if ride:
            ride.begin(ride_in, ride_out, sems, jnp.logical_and(b == 0, i == 0))
        if nc:
            @pl.when(i == 0)
            def _():
                for c_ref in car:
                    c_ref[...] = jnp.zeros(c_ref.shape, F32)
        cin = [c_ref[...] for c_ref in car]
        if save:
            for s_ref, cv in zip(save_refs, cin):
                s_ref[0, 0] = cv
        new_c, o = fn(i, b, cin, [_load_row(ref, r, t, scr) for ref, r in zip(row_refs, rows)], [v[...] for v in vec_refs])
        for c_ref, cv in zip(car, new_c):
            c_ref[...] = cv
        for o_ref, spec, ov in zip(out_refs, outs, o):
            _store_row(o_ref, spec, t, scr, ov)
        if ride:
            ride.end(ride_in, ride_out, sems, jnp.logical_and(b == nb - 1, i == nchunk - 1))

    stage = _view_scratch(list(rows) + list(outs), t)
    out_shape = [o.arr for o in outs]
    out_specs = [_row_spec(o, t, nchunk, False) for o in outs]
    if save:
        for cs in carries:
            out_shape.append(jax.ShapeDtypeStruct((nb, nchunk) + tuple(cs), F32))
            out_specs.append(pl.BlockSpec((1, 1) + tuple(cs), lambda b, i: (b, i, 0, 0)))
    res = pl.pallas_call(
        body, name=name, grid=(nb, nchunk),
        in_specs=[_row_spec(r, t, nchunk, False) for r in rows] + [_vec_spec(v) for v in vecs] + r_in,
        out_specs=out_specs + r_out, out_shape=out_shape + (list(ride.out_shapes) if ride else []),
        scratch_shapes=[pltpu.VMEM(tuple(cs), F32) for cs in carries] + stage + r_scr,
        compiler_params=_cparams(),
    )(*[r.arr for r in rows], *[v.arr for v in vecs], *(ride.ins if ride else []))
    if ride:
        RIDERS.done[name] = list(res[no + ns:])
    return list(res[:no]), list(res[no:no + ns])


def scan_bwd(name, fn, *, nb, nchunk, t, rows, vecs, carries, saved, douts, adds=None):
    adds = adds or {}
    nr, nv, nc, no = len(rows), len(vecs), len(carries), len(douts)
    dri = [k for k, r in enumerate(rows) if r.diff]
    dvi = [k for k, v in enumerate(vecs) if v.diff]
    add_keys = sorted(adds)
    na = len(add_keys)
    ride = RIDERS.take(name)
    r_in, r_out, r_scr = ride.specs() if ride else ([], [], [])

    def body(*refs):
        p = 0
        row_refs = refs[p:p + nr]; p += nr
        vec_refs = refs[p:p + nv]; p += nv
        save_refs = refs[p:p + nc]; p += nc
        dout_refs = refs[p:p + no]; p += no
        add_refs = refs[p:p + na]; p += na
        ride_in = refs[p:p + len(r_in)]; p += len(r_in)
        drow_refs = refs[p:p + len(dri)]; p += len(dri)
        dvec_refs = refs[p:p + len(dvi)]; p += len(dvi)
        ride_out = refs[p:p + len(r_out)]; p += len(r_out)
        dcar = refs[p:p + nc]; p += nc
        scr = refs[p] if stage else None
        sems = refs[p + len(stage):]
        b, ir = pl.program_id(0), pl.program_id(1)
        ci = nchunk - 1 - ir
        if ride:
            ride.begin(ride_in, ride_out, sems, jnp.logical_and(b == 0, ir == 0))
        if nc:
            @pl.when(ir == 0)
            def _():
                for c_ref in dcar:
                    c_ref[...] = jnp.zeros(c_ref.shape, F32)
        rows_v = [_load_row(ref, r, t, scr) for ref, r in zip(row_refs, rows)]
        vecs_v = [v[...] for v in vec_refs]
        cin = [s[0, 0] for s in save_refs]
        dc = [c_ref[...] for c_ref in dcar]
        dout_v = [_load_row(ref, r, t, scr).astype(F32) for ref, r in zip(dout_refs, douts)]

        def f(cs, dr, dv):
            rr, vv = list(rows_v), list(vecs_v)
            for k, idx in enumerate(dri):
                rr[idx] = dr[k]
            for k, idx in enumerate(dvi):
                vv[idx] = dv[k]
            return fn(ci, b, cs, rr, vv)

        _, vjp = jax.vjp(f, cin, [rows_v[k].astype(F32) for k in dri], [vecs_v[k].astype(F32) for k in dvi])
        dcin, drows, dvecs = vjp((dc, dout_v))
        for c_ref, cv in zip(dcar, dcin):
            c_ref[...] = cv
        for k, (o_ref, ov) in enumerate(zip(drow_refs, drows)):
            if dri[k] in adds:
                ov = ov + add_refs[add_keys.index(dri[k])][0].astype(F32)
            _store_row(o_ref, rows[dri[k]], t, scr, ov)
        for k, (o_ref, ov) in enumerate(zip(dvec_refs, dvecs)):
            first = (ir == 0) if vecs[dvi[k]].fc is not None else jnp.logical_and(ir == 0, b == 0)

            @pl.when(first)
            def _(o_ref=o_ref, ov=ov):
                o_ref[...] = ov

            @pl.when(jnp.logical_not(first))
            def _(o_ref=o_ref, ov=ov):
                o_ref[...] += ov

        if ride:
            ride.end(ride_in, ride_out, sems, jnp.logical_and(b == nb - 1, ir == nchunk - 1))

    stage = _view_scratch(list(rows) + list(douts), t)
    in_specs = ([_row_spec(r, t, nchunk, True) for r in rows] + [_vec_spec(v) for v in vecs]
                + [pl.BlockSpec((1, 1) + tuple(cs), lambda b, i: (b, nchunk - 1 - i, 0, 0)) for cs in carries]
                + [_row_spec(d, t, nchunk, True) for d in douts]
                + [_row_spec(adds[k], t, nchunk, True) for k in add_keys] + r_in)
    out_shape, out_specs = [], []
    for k in dri:
        r = rows[k]
        if r.slot:
            out_shape.append(jax.ShapeDtypeStruct((nb, r.arr.shape[1], r.w), r.ddtype))
            out_specs.append(pl.BlockSpec((1, t, r.w), lambda b, i: (b, nchunk - 1 - i, 0)))
        elif r.dcols is not None:
            out_shape.append(jax.ShapeDtypeStruct((r.arr.shape[0], r.arr.shape[1], r.dcols), r.ddtype))
            out_specs.append(pl.BlockSpec((1, t, r.w), lambda b, i, r=r: (r.fb(b), nchunk - 1 - i, r.dfc(b))))
        else:
            out_shape.append(jax.ShapeDtypeStruct(r.arr.shape, r.ddtype))
            out_specs.append(_row_spec(r, t, nchunk, True))
    for k in dvi:
        out_shape.append(jax.ShapeDtypeStruct(vecs[k].arr.shape, F32))
        out_specs.append(_vec_spec(vecs[k]))
    nd = len(dri) + len(dvi)
    res = pl.pallas_call(
        body, name=name, grid=(nb, nchunk), in_specs=in_specs, out_specs=out_specs + r_out,
        out_shape=out_shape + (list(ride.out_shapes) if ride else []),
        scratch_shapes=[pltpu.VMEM(tuple(cs), F32) for cs in carries] + stage + r_scr,
        compiler_params=_cparams(),
    )(*[r.arr for r in rows], *[v.arr for v in vecs], *saved, *[d.arr for d in douts], *[adds[k].arr for k in add_keys],
      *(ride.ins if ride else []))
    if ride:
        RIDERS.done[name] = list(res[nd:])
    return list(res[:len(dri)]), list(res[len(dri):nd])


def out_row(shape, dtype=F32, w=None, fb=None, fc=None):
    return Row(jax.ShapeDtypeStruct(shape, dtype), w, fb, fc)


def _conv(shift, halo, cur, w, bias, taps):
    y = bias
    for k in range(taps):
        y = y + _rowk(w, k) * shift(halo, cur, taps - 1 - k)
    return y


def _ssd_fn(ci, b, carries, rows, vecs):
    cx, cb_, cc, ht = carries
    z, xr, br, cr, dtr = rows
    cwx, cbx, cwb, cbb, cwc, cbc, dtb, alog, dsk, ng = vecs
    t = z.shape[0]
    xs = _silu(_conv(_shift8, cx, xr, cwx, cbx, 4))
    bm = _silu(_conv(_shift8, cb_, br, cwb, cbb, 4))
    cm = _silu(_conv(_shift8, cc, cr, cwc, cbc, 4))
    dt = _softplus(dtr + dtb)
    acol = _cumsum_rows(dt * (-jnp.exp(alog)))
    arow = acol.T
    r, c = _iota((t, t), 0), _iota((t, t), 1)
    causal = r >= c
    cbm = _bdot(cm, bm, "nt")
    lane, sub = _iota(acol.shape, 1), _iota(arow.shape, 0)
    colh = _iota(xs.shape, 1) // 64
    a, dtx, dx, acs = jnp.zeros(xs.shape, F32), jnp.zeros(xs.shape, F32), jnp.zeros((1, xs.shape[1]), F32), []
    for j in range(4):
        h = 4 * b + j
        ac = jnp.sum(jnp.where(lane == h, acol, 0.0), axis=1, keepdims=True)
        acs.append(ac)
        a = jnp.where(colh == j, ac, a)
        dtx = jnp.where(colh == j, jnp.sum(jnp.where(lane == h, dt, 0.0), axis=1, keepdims=True), dtx)
        dx = jnp.where(_iota(dx.shape, 1) // 64 == j, jnp.sum(jnp.where(_iota(dsk.shape, 1) == h, dsk, 0.0), axis=1, keepdims=True), dx)
    atot = jnp.sum(jnp.where(_iota(a.shape, 0) == t - 1, a, 0.0), axis=0, keepdims=True)
    x = xs * dtx
    ms, xh = [], []
    for j in range(4):
        ar = jnp.sum(jnp.where(sub == 4 * b + j, arow, 0.0), axis=0, keepdims=True)
        ms.append(cbm * jnp.exp(jnp.where(causal, acs[j] - ar, NEG)))
        xh.append(jnp.where(colh == j, x, 0.0))
    ydiag = _bdot(jnp.concatenate(ms, axis=1), jnp.concatenate(xh, axis=0), "nn")
    yoff = _bdot(cm, ht, "nn") * jnp.exp(a)
    ht_new = ht * jnp.exp(atot) + _bdot(bm, x * jnp.exp(atot - a), "tn")
    y = ydiag + yoff + dx * xs
    yz = y * _silu(z)
    yn = yz * lax.rsqrt(jnp.mean(yz * yz, axis=-1, keepdims=True) + NORM_EPS) * ng
    return [_tail8(xr), _tail8(br), _tail8(cr), ht_new], [yn]


_SSD_T = 256
_SSD_CARRIES = [(8, 256), (8, 128), (8, 128), (128, 256)]


def _ssd_io(proj3, p):
    own = lambda b: b
    rows = [Row(proj3, 256, fc=own, dcols=512, dfc=own, ddtype=BF16),
            Row(proj3, 256, fc=lambda b: 2 + b, dcols=512, dfc=own, ddtype=BF16),
            Row(proj3, 128, fc=lambda b: 8 + b, dcols=256, dfc=own, ddtype=BF16),
            Row(proj3, 128, fc=lambda b: 10 + b, dcols=256, dfc=own, ddtype=BF16),
            Row(proj3, 128, fc=lambda b: 14, slot=True)]
    vecs = [Vec(p["cw"], 256, lambda b: b), Vec(p["cb"], 256, lambda b: b),
            Vec(p["cw"], 128, lambda b: 4 + b), Vec(p["cb"], 128, lambda b: 4 + b),
            Vec(p["cw"], 128, lambda b: 6 + b), Vec(p["cb"], 128, lambda b: 6 + b),
            Vec(p["dtb"]), Vec(p["alog"]), Vec(p["dsk"]), Vec(p["ng"], 256, lambda b: b)]
    return rows, vecs


def ssd_forward(name, proj3, p):
    rows, vecs = _ssd_io(proj3, p)
    s = proj3.shape[1]
    (y,), saved = scan_fwd(name, _ssd_fn, nb=2, nchunk=s // _SSD_T, t=_SSD_T, rows=rows, vecs=vecs,
                           carries=_SSD_CARRIES, outs=[out_row((1, s, SSD_INNER), BF16, 256, fc=lambda b: b)], save=True)
    return y, saved


def ssd_backward(name, proj3, p, saved, dmix3):
    rows, vecs = _ssd_io(proj3, p)
    s = proj3.shape[1]
    drows, dvecs = scan_bwd(name, _ssd_fn, nb=2, nchunk=s // _SSD_T, t=_SSD_T, rows=rows, vecs=vecs,
                            carries=_SSD_CARRIES, saved=saved, douts=[Row(dmix3, 256, fc=lambda b: b)])
    return drows, dvecs


def _pool_fn(ci, b, carries, rows, vecs):
    (cu,) = carries
    (u,) = rows
    wbd, scale = vecs
    t = u.shape[0]
    pos = ci * t + _iota(u.shape, 0)
    grp = _iota(u.shape, 1) // 64
    acc, pooled, k = u, jnp.zeros(u.shape, F32), 1
    for gi, w in enumerate(POOL_WINDOWS):
        while k < w:
            acc = acc + _shift16(cu, u, k)
            k += 1
        pooled = jnp.where(grp == gi, acc / jnp.minimum(pos + 1, w).astype(F32), pooled)
    y = _bdot(pooled - u, wbd, "nn") * scale
    return [_tail16(u)], [y]


_POOL_T = 256


def _pool_io(proj3, wbd, scale):
    return [Row(proj3, 256, fc=lambda b: 6, dcols=256, dfc=lambda b: 0, ddtype=BF16)], [Vec(wbd), Vec(scale)]


def pool_forward(name, proj3, wbd, scale):
    rows, vecs = _pool_io(proj3, wbd, scale)
    s = proj3.shape[1]
    (y,), saved = scan_fwd(name, _pool_fn, nb=1, nchunk=s // _POOL_T, t=_POOL_T, rows=rows, vecs=vecs,
                           carries=[(16, 256)], outs=[out_row((1, s, POOL_W), BF16)], save=True)
    return y, saved


def pool_backward(name, proj3, wbd, scale, saved, dmix3):
    rows, vecs = _pool_io(proj3, wbd, scale)
    s = proj3.shape[1]
    return scan_bwd(name, _pool_fn, nb=1, nchunk=s // _POOL_T, t=_POOL_T, rows=rows, vecs=vecs,
                    carries=[(16, 256)], saved=saved, douts=[Row(dmix3, 256, fc=lambda b: 2)])


def _attn_fn(ci, b, carries, rows, vecs):
    kp, vp = carries
    qr, kr, v = _thirds(rows[0])
    scale = ATT_HEAD_DIM ** -0.5
    n = qr.shape[0]
    keys, vals = jnp.concatenate([kp, kr], axis=0), jnp.concatenate([vp, v], axis=0)
    r, c = _iota((n, 2 * n), 0), _iota((n, 2 * n), 1)
    ok = jnp.logical_or(jnp.logical_and(jnp.logical_and(c < n, c >= r), ci > 0), jnp.logical_and(c >= n, r >= c - n))
    head = _iota(qr.shape, 1) // ATT_HEAD_DIM
    o, lse = jnp.zeros(qr.shape, F32), jnp.zeros(qr.shape, F32)
    for h in range(ATT_HEADS):
        mine = head == h
        s = jnp.where(ok, _bdot(jnp.where(mine, qr, 0.0), keys, "nt") * scale, NEG)
        m = lax.stop_gradient(jnp.max(s, axis=1, keepdims=True))
        p = jnp.exp(s - m)
        l = jnp.sum(p, axis=1, keepdims=True)
        o = jnp.where(mine, _bdot(p, vals, "nn") / l, o)
        lse = jnp.where(mine, m + jnp.log(l), lse)
    return [kr, v], [o, lse]


_ATT_CARRIES = [(ATT_BLOCK, ATT_W), (ATT_BLOCK, ATT_W)]


def attn_forward(name, pv, d):
    l = pv.shape[1]
    own = lambda b: b
    outs = [out_row((1, l, d * ATT_W), F32, ATT_W, fc=own) for _ in range(2)]
    (o, lse), saved = scan_fwd(name, _attn_fn, nb=d, nchunk=l // ATT_BLOCK, t=ATT_BLOCK, rows=[Row(pv, 3 * ATT_W, fc=own)],
                               vecs=[], carries=_ATT_CARRIES, outs=outs, save=True)
    return o, lse, saved


def attn_backward(name, pv, d, saved, do, dlse):
    l = pv.shape[1]
    own = lambda b: b
    (dpv,), _ = scan_bwd(name, _attn_fn, nb=d, nchunk=l // ATT_BLOCK, t=ATT_BLOCK, rows=[Row(pv, 3 * ATT_W, fc=own)], vecs=[],
                         carries=_ATT_CARRIES, saved=saved, douts=[Row(do, ATT_W, fc=own), Row(dlse, ATT_W, fc=own)])
    return dpv


def _rope_fn(ci, b, carries, rows, vecs):
    x, cs, sn = rows
    return [], [x * cs + _rot_pairs(x) * sn]


def _rope3_fn(ci, b, carries, rows, vecs):
    _, (y,) = _rope_fn(ci, b, carries, rows, vecs)
    return [], [y, y, y]


def _by_residue(a_or_shape, w, d):
    if isinstance(a_or_shape, tuple):
        _, s, _ = a_or_shape
        return Row(jax.ShapeDtypeStruct((1, s // d, d * w), F32), w, view=None if d == 1 else d)
    return Row(a_or_shape, w, view=None if d == 1 else d)


def rope_forward(name, qkv3, cs3, sn3):
    s, w = qkv3.shape[1], qkv3.shape[2]
    ys, _ = scan_fwd(name, _rope3_fn, nb=1, nchunk=s // _ROW_T, t=_ROW_T, vecs=[], carries=[], save=False,
                     rows=[Row(qkv3), Row(cs3, diff=False), Row(sn3, diff=False)],
                     outs=[_by_residue(qkv3.shape, w, d) for _, d in ATT_PATTERNS])
    return ys


def rope_backward(name, qkv3, cs3, sn3, dys):
    s, w = qkv3.shape[1], qkv3.shape[2]
    (dx,), _ = scan_bwd(name, _rope3_fn, nb=1, nchunk=s // _ROW_T, t=_ROW_T, vecs=[], carries=[], saved=[],
                        rows=[Row(qkv3, ddtype=BF16), Row(cs3, diff=False), Row(sn3, diff=False)],
                        douts=[_by_residue(a, w, d) for a, (_, d) in zip(dys, ATT_PATTERNS)])
    return dx


def _merge_fn(ci, b, carries, rows, vecs):
    o1, o2, o3, l1, l2, l3 = rows
    mx = lax.stop_gradient(jnp.maximum(l1, jnp.maximum(l2, l3)))
    e1, e2, e3 = jnp.exp(l1 - mx), jnp.exp(l2 - mx), jnp.exp(l3 - mx)
    return [], [(e1 * o1 + e2 * o2 + e3 * o3) / (e1 + e2 + e3)]


_ROW_T = 256


def _merge_rows(os_, ls_):
    ds = [d for _, d in ATT_PATTERNS]
    return [_by_residue(a, ATT_W, d) for a, d in zip(os_, ds)] + [_by_residue(a, ATT_W, d) for a, d in zip(ls_, ds)]


def merge_forward(name, os_, ls_, s):
    (y,), _ = scan_fwd(name, _merge_fn, nb=1, nchunk=s // _ROW_T, t=_ROW_T, rows=_merge_rows(os_, ls_), vecs=[],
                       carries=[], outs=[out_row((1, s, ATT_W), BF16)], save=False)
    return y


def merge_backward(name, os_, ls_, dmix3):
    s = dmix3.shape[1]
    drows, _ = scan_bwd(name, _merge_fn, nb=1, nchunk=s // _ROW_T, t=_ROW_T, rows=_merge_rows(os_, ls_), vecs=[],
                        carries=[], saved=[], douts=[Row(dmix3, 256, fc=lambda b: 3)])
    return drows


def _norm_mod_fn(ci, b, carries, rows, vecs):
    (x,) = rows
    g, sc, sh = vecs
    xn = x * lax.rsqrt(jnp.mean(x * x, axis=-1, keepdims=True) + NORM_EPS)
    return [], [xn * g * (1.0 + sc) + sh]


def norm_mod_forward(name, x3, g, sc, sh):
    s = x3.shape[1]
    (h,), _ = scan_fwd(name, _norm_mod_fn, nb=1, nchunk=s // _ROW_T, t=_ROW_T, rows=[Row(x3)], vecs=[Vec(g), Vec(sc), Vec(sh)],
                       carries=[], outs=[out_row(x3.shape, BF16)], save=False)
    return h


def norm_mod_backward(name, x3, g, sc, sh, dh3, add3):
    s = x3.shape[1]
    (dx,), dv = scan_bwd(name, _norm_mod_fn, nb=1, nchunk=s // _ROW_T, t=_ROW_T, rows=[Row(x3)], vecs=[Vec(g), Vec(sc), Vec(sh)],
                         carries=[], saved=[], douts=[Row(dh3)], adds={0: Row(add3)})
    return dx, dv


def _gate_fn(ci, b, carries, rows, vecs):
    return [], [rows[0] * vecs[0]]


def gate_backward(name, o3, g, dx3):
    s = o3.shape[1]
    (do,), (dg,) = scan_bwd(name, _gate_fn, nb=1, nchunk=s // _ROW_T, t=_ROW_T, rows=[Row(o3, ddtype=BF16)], vecs=[Vec(g)],
                            carries=[], saved=[], douts=[Row(dx3)])
    return do, dg


def _make_halves():
    @jax.custom_vjp
    def halves(x):
        h = x.shape[1] // 2
        return x[:, :h], x[:, h:]

    def fwd(x):
        return halves(x), None

    def bwd(_, g):
        return (jnp.concatenate(g, axis=1),)

    halves.defvjp(fwd, bwd)
    return halves


_halves = _make_halves()


def _ffn_fn(ci, b, carries, rows, vecs):
    (cu,) = carries
    (u,) = rows
    w, bias = vecs
    hg, hu = _halves(_conv(_shift8, cu, u, w, bias, 3))
    return [_tail8(u)], [_silu(hg) * hu]


_FFN_T = 256
_FFN_CW = FFN_DIM // 2
_FFN_CARRIES = [(8, 2 * _FFN_CW)]
FFN_BLOCK_ORDER = [0, 2, 1, 3]


def _ffn_io(up3, cw, cb):
    own = lambda b: b
    return [Row(up3, 2 * _FFN_CW, fc=own, ddtype=BF16)], [Vec(cw, 2 * _FFN_CW, own), Vec(cb, 2 * _FFN_CW, own)]


def ffn_mid_forward(name, up3, cw, cb):
    rows, vecs = _ffn_io(up3, cw, cb)
    s = up3.shape[1]
    (act,), saved = scan_fwd(name, _ffn_fn, nb=2, nchunk=s // _FFN_T, t=_FFN_T, rows=rows, vecs=vecs, carries=_FFN_CARRIES,
                             outs=[out_row((1, s, FFN_DIM), BF16, _FFN_CW, fc=lambda b: b)], save=True)
    return act, saved


def ffn_mid_backward(name, up3, cw, cb, saved, dact3):
    rows, vecs = _ffn_io(up3, cw, cb)
    s = up3.shape[1]
    return scan_bwd(name, _ffn_fn, nb=2, nchunk=s // _FFN_T, t=_FFN_T, rows=rows, vecs=vecs, carries=_FFN_CARRIES,
                    saved=saved, douts=[Row(dact3, _FFN_CW, fc=lambda b: b)])


def _adam_fn(ci, b, carries, rows, vecs):
    w, g, m, v = rows
    m = ADAM_B1 * m + (1.0 - ADAM_B1) * g
    v = ADAM_B2 * v + (1.0 - ADAM_B2) * (g * g)
    m_hat = m / (1.0 - ADAM_B1 ** ADAM_STEP)
    v_hat = v / (1.0 - ADAM_B2 ** ADAM_STEP)
    delta = -ADAM_LR * (m_hat / (jnp.sqrt(v_hat) + ADAM_EPS) + ADAM_WD * w)
    return [], [delta, m, v]


def adamw(name, w, g, m, v):
    shape = w.shape
    c = shape[-1]
    r = int(np.prod(shape[:-1]))
    t = _tile(r, 256, 8)
    as3 = lambda a: a.reshape(1, r, c)
    outs, _ = scan_fwd(name, _adam_fn, nb=1, nchunk=r // t, t=t, rows=[Row(as3(a)) for a in (w, g, m, v)], vecs=[], carries=[],
                       outs=[out_row((1, r, c)) for _ in range(3)], save=False)
    return [o.reshape(shape) for o in outs]


def rope_tables(positions):
    inv_freq = ROPE_THETA ** (-jnp.arange(0, ROT_DIM, 2, dtype=F32) / ROT_DIM)
    ang = positions.astype(F32)[:, None] * inv_freq
    s = positions.shape[0]
    cs = jnp.concatenate([jnp.cos(ang), jnp.cos(ang), jnp.ones((s, ATT_HEAD_DIM - ROT_DIM), F32)], axis=1)
    sn = jnp.concatenate([jnp.sin(ang), jnp.sin(ang), jnp.zeros((s, ATT_HEAD_DIM - ROT_DIM), F32)], axis=1)
    cs3 = jnp.concatenate([jnp.tile(cs, (1, 2 * ATT_HEADS)), jnp.ones((s, ATT_W), F32)], axis=1)
    sn3 = jnp.concatenate([jnp.tile(sn, (1, 2 * ATT_HEADS)), jnp.zeros((s, ATT_W), F32)], axis=1)
    return cs3[None], sn3[None]


def attention_forward(lname, qkv3, cs3, sn3):
    s = qkv3.shape[1]
    rotated = rope_forward(f"{lname}_rope", qkv3, cs3, sn3)
    os_, ls_, keep = [], [], []
    for pi, (_, d) in enumerate(ATT_PATTERNS):
        o, lse, saved = attn_forward(f"{lname}_attn{pi}", rotated[pi], d)
        os_.append(o)
        ls_.append(lse)
        keep.append(saved)
    y = merge_forward(f"{lname}_merge", os_, ls_, s)
    return y, (rotated, os_, ls_, keep)


def attention_backward(lname, qkv3, cs3, sn3, res, dmix3):
    rotated, os_, ls_, keep = res
    dm = merge_backward(f"{lname}_merge_b", os_, ls_, dmix3)
    dys = [attn_backward(f"{lname}_attn{pi}_b", rotated[pi], d, keep[pi], dm[pi], dm[3 + pi]) for pi, (_, d) in enumerate(ATT_PATTERNS)]
    return rope_backward(f"{lname}_rope_b", qkv3, cs3, sn3, dys)


def mm(name, a, b, mode, out_dtype=F32, res=None, gate=None, tm=1408, tn=1536, tk=1408, into=None):
    if mode == "nn":
        (m, k), n = a.shape, b.shape[1]
    elif mode == "nt":
        (m, k), n = a.shape, b.shape[0]
    else:
        (k, m), n = a.shape, b.shape[1]
    tm, tn, tk = _tile(m, tm), _tile(n, tn), _tile(k, tk)
    nk = k // tk
    a_spec = pl.BlockSpec((tk, tm), lambda i, j, q: (q, i)) if mode == "tn" else pl.BlockSpec((tm, tk), lambda i, j, q: (i, q))
    b_spec = pl.BlockSpec((tn, tk), lambda i, j, q: (j, q)) if mode == "nt" else pl.BlockSpec((tk, tn), lambda i, j, q: (q, j))
    o_spec = pl.BlockSpec((tm, tn), lambda i, j, q: (i, j))
    fused = res is not None
    lead = 0 if into is None else into[0].ndim - 2
    first = (0,) * lead + (slice(None), slice(None))

    def body(*refs):
        if fused:
            a_ref, b_ref, r_ref, g_ref, o_ref, o2_ref, acc = refs
        elif into is not None:
            a_ref, b_ref, _, o_ref, acc = refs
        else:
            a_ref, b_ref, o_ref, acc = refs
        q = pl.program_id(2)

        @pl.when(q == 0)
        def _():
            acc[...] = jnp.zeros(acc.shape, F32)

        acc[...] += _mxu(a_ref[...], b_ref[...], mode)

        @pl.when(q == nk - 1)
        def _():
            o_ref[first] = acc[...].astype(o_ref.dtype)
            if fused:
                o2_ref[...] = r_ref[...] + g_ref[...] * acc[...]

    ins, in_specs = [a, b], [a_spec, b_spec]
    out_shape, out_specs = [jax.ShapeDtypeStruct((m, n), out_dtype)], [o_spec]
    if fused:
        ins += [res, gate]
        in_specs += [o_spec, pl.BlockSpec((1, tn), lambda i, j, q: (0, j))]
        out_shape.append(jax.ShapeDtypeStruct((m, n), F32))
        out_specs.append(o_spec)
    aliases = {}
    if into is not None:
        buf, omap = into
        ins.append(buf)
        in_specs.append(pl.BlockSpec(memory_space=pl.ANY))
        out_shape = [jax.ShapeDtypeStruct(buf.shape, buf.dtype)]
        out_specs = [pl.BlockSpec((1,) * lead + (tm, tn), lambda i, j, q: omap(i, j))]
        aliases = {2: 0}
    out = pl.pallas_call(
        body, name=name, grid=(m // tm, n // tn, nk), in_specs=in_specs, out_specs=out_specs, out_shape=out_shape,
        scratch_shapes=[pltpu.VMEM((tm, tn), F32)], input_output_aliases=aliases,
        compiler_params=pltpu.CompilerParams(dimension_semantics=("parallel", "parallel", "arbitrary"),
                                             vmem_limit_bytes=VMEM_LIMIT_BYTES),
    )(*ins)
    return tuple(out) if fused else out[0]


def final_loss(name, x3, t3, g):
    s, d = x3.shape[1], x3.shape[2]
    t = _ROW_T

    def body(x_ref, t_ref, g_ref, loss_ref, dx_ref, dg_ref):
        i = pl.program_id(0)
        tv = t_ref[0]

        def f(x, gg):
            y = x * lax.rsqrt(jnp.mean(x * x, axis=-1, keepdims=True) + NORM_EPS) * gg
            e = y - tv
            return 0.5 * jnp.sum(jnp.mean(e * e, axis=-1, keepdims=True), axis=0, keepdims=True)

        l, vjp = jax.vjp(f, x_ref[0], g_ref[...])
        dx, dg = vjp(jnp.ones((1, 1), F32))
        dx_ref[0] = dx

        @pl.when(i == 0)
        def _():
            loss_ref[...] = jnp.zeros(loss_ref.shape, F32)
            dg_ref[...] = jnp.zeros(dg_ref.shape, F32)

        loss_ref[...] += jnp.broadcast_to(l, loss_ref.shape)
        dg_ref[...] += dg

    row = pl.BlockSpec((1, t, d), lambda i: (0, i, 0))
    vec = pl.BlockSpec((1, d), lambda i: (0, 0))
    return pl.pallas_call(
        body, name=name, grid=(s // t,), in_specs=[row, row, vec],
        out_specs=[pl.BlockSpec((8, 128), lambda i: (0, 0)), row, vec],
        out_shape=[jax.ShapeDtypeStruct((8, 128), F32), jax.ShapeDtypeStruct(x3.shape, F32), jax.ShapeDtypeStruct((1, d), F32)],
        compiler_params=pltpu.CompilerParams(dimension_semantics=("arbitrary",), vmem_limit_bytes=VMEM_LIMIT_BYTES),
    )(x3, t3, g)


_ADA_TN = 512


def ada_forward(name, c16, ada_w):
    depth, d, cols = ada_w.shape

    def body(c_ref, w_ref, o_ref):
        o_ref[0] = _mxu(_silu(c_ref[...]), w_ref[0], "nn")

    return pl.pallas_call(
        body, name=name, grid=(depth, cols // _ADA_TN),
        in_specs=[pl.BlockSpec((16, d), lambda l, j: (0, 0)), pl.BlockSpec((1, d, _ADA_TN), lambda l, j: (l, 0, j))],
        out_specs=pl.BlockSpec((1, 16, _ADA_TN), lambda l, j: (l, 0, j)),
        out_shape=jax.ShapeDtypeStruct((depth, 16, cols), F32),
        compiler_params=pltpu.CompilerParams(dimension_semantics=("arbitrary", "arbitrary"), vmem_limit_bytes=VMEM_LIMIT_BYTES),
    )(c16, ada_w)


def ada_backward(name, c16, dmod16, w, m, v):
    depth, d, cols = w.shape

    def body(c_ref, dm_ref, w_ref, m_ref, v_ref, g_ref, dl_ref, nm_ref, nv_ref):
        g = _mxu(_silu(c_ref[...]), dm_ref[0], "tn")
        _, (delta, nm, nv) = _adam_fn(None, None, [], [w_ref[0], g, m_ref[0], v_ref[0]], [])
        g_ref[0], dl_ref[0], nm_ref[0], nv_ref[0] = g, delta, nm, nv

    blk = pl.BlockSpec((1, d, _ADA_TN), lambda l, j: (l, 0, j))
    return pl.pallas_call(
        body, name=name, grid=(depth, cols // _ADA_TN),
        in_specs=[pl.BlockSpec((16, d), lambda l, j: (0, 0)), pl.BlockSpec((1, 16, _ADA_TN), lambda l, j: (l, 0, j)), blk, blk, blk],
        out_specs=[blk] * 4, out_shape=[jax.ShapeDtypeStruct(w.shape, F32)] * 4,
        compiler_params=pltpu.CompilerParams(dimension_semantics=("arbitrary", "arbitrary"), vmem_limit_bytes=VMEM_LIMIT_BYTES),
    )(c16, dmod16, w, m, v)


def _sum_fn(ci, b, carries, rows, vecs):
    acc = rows[0]
    for r in rows[1:]:
        acc = acc + r
    return [], [acc]


def sum_slots(name, a, nsum, out_dtype=F32):
    n, r, c = a.shape
    nb = n // nsum
    t = _tile(r, 256, 8)
    rows = [Row(a, fb=(lambda b, k=k: k * nb + b)) for k in range(nsum)]
    (out,), _ = scan_fwd(name, _sum_fn, nb=nb, nchunk=r // t, t=t, rows=rows, vecs=[], carries=[],
                         outs=[out_row((nb, r, c), out_dtype, fb=lambda b: b)], save=False)
    return out


def _sum_my_layer_fn(ci, b, carries, rows, vecs):
    layer0, layer1, theirs = rows
    return [], [jnp.where(lax.axis_index("c") == 0, layer0, layer1) + theirs]


def sum_cores(name, g, theirs, out_dtype):
    _, nb, r, c = g.shape
    g8 = g.reshape(2 * nb, r, c)
    t = _tile(r, 256, 8)
    rows = [Row(g8, fb=lambda b: b), Row(g8, fb=lambda b: nb + b), Row(theirs, fb=lambda b: b)]
    (out,), _ = scan_fwd(name, _sum_my_layer_fn, nb=nb, nchunk=r // t, t=t, rows=rows, vecs=[], carries=[],
                         outs=[out_row((nb, r, c), out_dtype, fb=lambda b: b)], save=False)
    return out


def _flip(mask, pos):
    return tuple((1 - p) if m else p for m, p in zip(mask, pos))


ALL_PEERS = [(a, b, c) for a in (0, 1) for b in (0, 1) for c in (0, 1)][1:]
CHIP_PEERS = [(1, 0, 0), (0, 1, 0), (1, 1, 0)]
SIBLING = [(0, 0, 1)]


def _divisor(size, target, unit):
    best = 1
    for n in range(1, target + 1):
        if size % n == 0 and (size // n) % unit == 0:
            best = n
    return best


def _pieces(src, dst, pieces):
    shape = src.shape
    unit = 16 if src.dtype == BF16 else 8
    if pieces <= 1:
        return [(src, dst)]
    if len(shape) == 2:
        n = _divisor(shape[0], pieces, unit)
        s = shape[0] // n
        return [(src.at[pl.ds(i * s, s)], dst.at[pl.ds(i * s, s)]) for i in range(n)]
    assert len(shape) == 3, shape
    n = _divisor(shape[1], max(pieces // shape[0], 1), unit)
    s = shape[1] // n
    return [(src.at[j, pl.ds(i * s, s)], dst.at[j, pl.ds(i * s, s)]) for j in range(shape[0]) for i in range(n)]


def comm_call(name, arrays, out_shapes, masks, src_fn, dst_fn, local_fn=None, pieces=1):
    na, npeer = len(arrays), len(masks)

    def body(*refs):
        ins, outs = refs[:na], refs[na:2 * na]
        send_sems, recv_sems, loc_sems = refs[2 * na:]
        me = (lax.axis_index("x"), lax.axis_index("y"), lax.axis_index("c"))
        local = []
        if local_fn is not None:
            for k in range(na):
                s, d = local_fn(k, ins[k], outs[k], me)
                for ps, pd in _pieces(s, d, pieces):
                    pltpu.make_async_copy(ps, pd, loc_sems.at[k]).start()
                local.append(pltpu.make_async_copy(s, d, loc_sems.at[k]))

        def remote(k, p, src, dst, to):
            return pltpu.make_async_remote_copy(
                src_ref=src, dst_ref=dst, send_sem=send_sems.at[k * npeer + p], recv_sem=recv_sems.at[k * npeer + p],
                device_id=to, device_id_type=MESH)

        for k in range(na):
            for p in range(npeer):
                peer = _flip(masks[p], me)
                for ps, pd in _pieces(src_fn(k, ins[k], me, peer), dst_fn(k, outs[k], me), pieces):
                    remote(k, p, ps, pd, peer).start()
        for k in range(na):
            for p in range(npeer):
                peer = _flip(masks[p], me)
                remote(k, p, src_fn(k, ins[k], me, peer), dst_fn(k, outs[k], peer), peer).wait_recv()
        for k in range(na):
            for p in range(npeer):
                peer = _flip(masks[p], me)
                remote(k, p, src_fn(k, ins[k], me, peer), dst_fn(k, outs[k], me), peer).wait_send()
        for cp in local:
            cp.wait()

    hbm = pl.BlockSpec(memory_space=pl.ANY)
    out = pl.pallas_call(
        body, name=name, in_specs=[hbm] * na, out_specs=[hbm] * na,
        out_shape=[jax.ShapeDtypeStruct(s, a.dtype) for s, a in zip(out_shapes, arrays)],
        scratch_shapes=[pltpu.SemaphoreType.DMA((na * npeer,)), pltpu.SemaphoreType.DMA((na * npeer,)),
                        pltpu.SemaphoreType.DMA((na,))],
    )(*arrays)
    return list(out)


def _dev(pos):
    return 4 * pos[0] + 2 * pos[1] + pos[2]


def _chip(pos):
    return 2 * pos[0] + pos[1]


def allgather8(name, a):
    (out,) = comm_call(name, [a], [(8,) + a.shape], ALL_PEERS,
                       src_fn=lambda k, r, me, peer: r, dst_fn=lambda k, o, sender: o.at[_dev(sender)],
                       local_fn=lambda k, r, o, me: (r, o.at[_dev(me)]))
    return out


def gather_layer_from_chips(name, arrays):
    return comm_call(name, arrays, [(4,) + a.shape[1:] for a in arrays], CHIP_PEERS,
                     src_fn=lambda k, r, me, peer: r.at[me[2]], dst_fn=lambda k, o, sender: o.at[_chip(sender)],
                     local_fn=lambda k, r, o, me: (r.at[me[2]], o.at[_chip(me)]), pieces=8)


def swap_layers(name, arrays, c):
    got = comm_call(name, arrays, [a.shape for a in arrays], SIBLING,
                    src_fn=lambda k, r, me, peer: r, dst_fn=lambda k, o, sender: o, pieces=32)
    return [[jnp.where(c == 0, a, g), jnp.where(c == 0, g, a)] for a, g in zip(arrays, got)]


def swap_other_layer(name, arrays):
    return comm_call(name, arrays, [a.shape[1:] for a in arrays], SIBLING,
                     src_fn=lambda k, r, me, peer: r.at[peer[2]], dst_fn=lambda k, o, sender: o, pieces=32)


def scatter_to_chips(name, arrays):
    return comm_call(name, arrays, [a.shape for a in arrays], CHIP_PEERS,
                     src_fn=lambda k, r, me, peer: r.at[_chip(peer)], dst_fn=lambda k, o, sender: o.at[_chip(sender)],
                     local_fn=lambda k, r, o, me: (r.at[_chip(me)], o.at[_chip(me)]), pieces=8)


def _rows_of(shape):
    return -(-int(np.prod(shape)) // 1024) * 8


def _pack(arrs):
    parts = []
    for a in arrs:
        flat = a.reshape(-1).astype(F32)
        parts.append(jnp.pad(flat, (0, _rows_of(a.shape) * 128 - flat.shape[0])).reshape(-1, 128))
    rows = sum(p.shape[0] for p in parts)
    parts.append(jnp.zeros(((-rows) % _ROW_T, 128), F32))
    return jnp.concatenate(parts, axis=0)


def _unpack(buf, shapes):
    out, o = [], 0
    for s in shapes:
        r, n = _rows_of(s), int(np.prod(s))
        out.append(buf[o:o + r].reshape(-1)[:n].reshape(s))
        o += r
    return out


_WEIGHTS = ["ada_w", "ada_b", "norm1_g", "w_in", "ssd_conv_w", "ssd_conv_b", "ssd_dt_bias", "ssd_a_log", "ssd_d", "ssd_norm_g",
            "pool_w", "pool_scale", "w_out", "norm2_g", "ffn_up", "ffn_conv_w", "ffn_conv_b", "ffn_down", "final_g"]
_BIG = ["w_in", "w_out", "ffn_up", "ffn_down"]
_SMALL = [n for n in _WEIGHTS if n not in _BIG and n != "ada_w"]
_COL_SHARDED_SMALL = {"ssd_conv_w": 256, "ffn_conv_w": 1408}


def _pad_lanes(v, n=128):
    return jnp.pad(v.astype(F32), (0, n - v.shape[0]))[None]


def _perm_cols(w):
    pad = jnp.zeros(w.shape[:-1] + (IN_WP - IN_W,), w.dtype)
    return jnp.concatenate([w[..., :1536], w[..., 1544:1800], w[..., 1536:1544], pad, w[..., 1800:]], axis=-1)


def _unperm_cols(g):
    return jnp.concatenate([g[..., :1536], g[..., 1792:1800], g[..., 1536:1792], g[..., IN_MAIN:]], axis=-1)


_CHIP2_PARTS = [(1284, 1536), (1792, 1800), (1536, 1792), (IN_MAIN, IN_MAIN + 126)]


def _w_in_chip_cols(gp):
    q = IN_W // 4
    return [gp[:, :q], gp[:, q:2 * q], jnp.concatenate([gp[:, a:b] for a, b in _CHIP2_PARTS], axis=1), gp[:, IN_WP - q:]]


def _w_in_from_chips(a):
    c2 = a[2]
    pad = jnp.zeros((a.shape[1], IN_WP - IN_W), a.dtype)
    return jnp.concatenate([a[0], a[1], c2[:, :252], c2[:, 260:516], c2[:, 252:260], pad, c2[:, 516:], a[3]], axis=1)


def _ffn_block_perm(a):
    n = a.shape[-1] // 4
    return jnp.concatenate([a[..., j * n:(j + 1) * n] for j in FFN_BLOCK_ORDER], axis=-1)


def _layer_forward(i, x3, modv, wts, sp, cs3, sn3):
    sh1, sc1, g1, sh2, sc2, g2 = modv
    big = lambda n: wts[n]() if callable(wts[n]) else wts[n]
    h1 = norm_mod_forward(f"l{i}_norm1", x3, wts["norm1_g"], sc1, sh1)
    proj3 = mm(f"l{i}_proj", h1[0], big("w_in")[:, :IN_MAIN], "nn")[None]
    qkv3 = mm(f"l{i}_qkv", h1[0], big("w_in")[:, IN_MAIN:], "nn")[None]
    y_ssd, sv_ssd = ssd_forward(f"l{i}_ssd", proj3, sp)
    y_pool, sv_pool = pool_forward(f"l{i}_pool", proj3, wts["wbd"], wts["pool_scale"])
    y_att, res_att = attention_forward(f"l{i}", qkv3, cs3, sn3)
    mix = jnp.concatenate([y_ssd, y_pool, y_att], axis=-1)
    out, x1 = mm(f"l{i}_wout", mix[0], big("w_out"), "nn", res=x3[0], gate=g1)
    x1 = x1[None]
    h2 = norm_mod_forward(f"l{i}_norm2", x1, wts["norm2_g"], sc2, sh2)
    up3 = mm(f"l{i}_up", h2[0], big("ffn_up"), "nn")[None]
    act, sv_ffn = ffn_mid_forward(f"l{i}_ffn", up3, wts["ffn_conv_w"], wts["ffn_conv_b"])
    dn, x2 = mm(f"l{i}_down", act[0], big("ffn_down"), "nn", res=x1[0], gate=g2)
    keep = dict(x=x3, h1=h1, proj3=proj3, qkv3=qkv3, sv_ssd=sv_ssd, sv_pool=sv_pool, res_att=res_att, mix=mix, out=out[None],
                x1=x1, h2=h2, up3=up3, act=act, sv_ffn=sv_ffn, dn=dn[None])
    return x2[None], keep


def _layer_backward(i, dx2, keep, modv, wts, sp, cs3, sn3, after=None):
    sh1, sc1, g1, sh2, sc2, g2 = modv
    k = keep
    big = lambda n: wts[n]() if callable(wts[n]) else wts[n]
    tell = lambda step, *a: after[step](*a) if after and step in after else None
    d_dn, d_g2 = gate_backward(f"l{i}_gate2_b", k["dn"], g2, dx2)
    d_act = mm(f"l{i}_down_bx", d_dn[0], big("ffn_down"), "nt")
    g_down = mm(f"l{i}_down_bw", k["act"][0], d_dn[0], "tn").reshape(4, FFN_DIM // 4, D_MODEL)
    (d_up,), dv_ffn = ffn_mid_backward(f"l{i}_ffn_b", k["up3"], wts["ffn_conv_w"], wts["ffn_conv_b"], k["sv_ffn"], d_act[None])
    tell("ffn_b")
    d_h2 = mm(f"l{i}_up_bx", d_up[0], big("ffn_up"), "nt")
    g_up = mm(f"l{i}_up_bw", k["h2"][0], d_up[0], "tn", tn=_FFN_CW,
              into=((4, D_MODEL, _FFN_CW), lambda r, c: ((c % 2) * 2 + c // 2, r, 0)))
    dx1, (d_n2, d_sc2, d_sh2) = norm_mod_backward(f"l{i}_norm2_b", k["x1"], wts["norm2_g"], sc2, sh2, d_h2[None], dx2)
    d_out, d_g1 = gate_backward(f"l{i}_gate1_b", k["out"], g1, dx1)
    d_mix = mm(f"l{i}_wout_bx", d_out[0], big("w_out"), "nt")[None]
    g_wout = mm(f"l{i}_wout_bw", k["mix"][0], d_out[0], "tn").reshape(4, D_MODEL // 4, D_MODEL)
    tell("wout_bw", g_wout, g_up, g_down)
    (dz, dxs, dbm, dcm, ddt), dv_ssd = ssd_backward(f"l{i}_ssd_b", k["proj3"], sp, k["sv_ssd"], d_mix)
    tell("ssd_b")
    (du_pool,), (d_wbd, d_pscale) = pool_backward(f"l{i}_pool_b", k["proj3"], wts["wbd"], wts["pool_scale"], k["sv_pool"], d_mix)
    d_qkv = attention_backward(f"l{i}", k["qkv3"], cs3, sn3, k["res_att"], d_mix)
    d_proj = jnp.concatenate([dz[0], dxs[0], dbm[0], dcm[0], du_pool[0], (ddt[0] + ddt[1]).astype(BF16), d_qkv[0]], axis=-1)
    g_win = jnp.stack(_w_in_chip_cols(mm(f"l{i}_proj_bw", k["h1"][0], d_proj, "tn")))
    tell("proj_bw", g_win)
    d_h1 = mm(f"l{i}_proj_bx", d_proj, big("w_in"), "nt")
    tell("proj_bx")
    dx, (d_n1, d_sc1, d_sh1) = norm_mod_backward(f"l{i}_norm1_b", k["x"], wts["norm1_g"], sc1, sh1, d_h1[None], dx1)
    dcwx, dcbx, dcwb, dcbb, dcwc, dcbc, ddtb, dalog, ddsk, dng = dv_ssd
    small = dict(
        norm1_g=d_n1[0], norm2_g=d_n2[0],
        ssd_conv_w=jnp.concatenate([dcwx[:, :512], dcwb[:, 512:768], dcwc[:, 768:]], axis=1),
        ssd_conv_b=jnp.concatenate([dcbx[0, :512], dcbb[0, 512:768], dcbc[0, 768:]]),
        ssd_dt_bias=ddtb[0, :8], ssd_a_log=dalog[0, :8], ssd_d=ddsk[0, :8], ssd_norm_g=dng[0],
        pool_w=jnp.stack([d_wbd[64 * g:64 * g + 64, 64 * g:64 * g + 64] for g in range(4)]), pool_scale=d_pscale[0],
        ffn_conv_w=_ffn_block_perm(dv_ffn[0]), ffn_conv_b=_ffn_block_perm(dv_ffn[1][0]),
    )
    dmod = jnp.concatenate([d_sh1[0], d_sc1[0], d_g1[0], d_sh2[0], d_sc2[0], d_g2[0]])
    return dx, [g_win, g_wout, g_up, g_down], small, dmod


def kernel(x, c, positions, ada_w, ada_b, norm1_g, w_in, ssd_conv_w, ssd_conv_b, ssd_dt_bias, ssd_a_log, ssd_d, ssd_norm_g, pool_w, pool_scale, w_out, norm2_g, ffn_up, ffn_conv_w, ffn_conv_b, ffn_down, final_g, loss_target, m_ada_w, m_ada_b, m_norm1_g, m_w_in, m_ssd_conv_w, m_ssd_conv_b, m_ssd_dt_bias, m_ssd_a_log, m_ssd_d, m_ssd_norm_g, m_pool_w, m_pool_scale, m_w_out, m_norm2_g, m_ffn_up, m_ffn_conv_w, m_ffn_conv_b, m_ffn_down, m_final_g, v_ada_w, v_ada_b, v_norm1_g, v_w_in, v_ssd_conv_w, v_ssd_conv_b, v_ssd_dt_bias, v_ssd_a_log, v_ssd_d, v_ssd_norm_g, v_pool_w, v_pool_scale, v_w_out, v_norm2_g, v_ffn_up, v_ffn_conv_w, v_ffn_conv_b, v_ffn_down, v_final_g):
    args = dict(locals())
    w = {n: args[n] for n in _WEIGHTS}
    m = {n: args["m_" + n] for n in _WEIGHTS}
    v = {n: args["v_" + n] for n in _WEIGHTS}
    d = D_MODEL
    me = (lax.axis_index("x"), lax.axis_index("y"), lax.axis_index("c"))
    chip, dev = _chip(me), _dev(me)
    RIDERS.reset()

    shapes0 = [c.shape, ssd_conv_w.shape, ffn_conv_w.shape]
    g0 = allgather8("gather_c_conv", _pack([c, ssd_conv_w, ffn_conv_w]))
    c16 = jnp.pad(g0[:, :d // 128, :].reshape(8, d), ((0, 8), (0, 0)))
    by_chip = [_unpack(g0[2 * j], shapes0) for j in range(4)]
    conv_w_full = jnp.concatenate([p[1] for p in by_chip], axis=-1)
    fconv_w_full = jnp.concatenate([p[2] for p in by_chip], axis=-1)

    modp = ada_forward("ada_fwd", c16, ada_w)[:, :8]
    g1 = allgather8("gather_mod", _pack([modp]))
    modfull = jnp.concatenate([_unpack(g1[2 * j], [modp.shape])[0] for j in range(4)], axis=-1)
    mod = lax.dynamic_index_in_dim(modfull, dev, axis=1, keepdims=False) + ada_b
    modv = [[mod[i, q * d:(q + 1) * d][None] for q in range(6)] for i in range(DEPTH)]

    shards = [w[n].astype(BF16) for n in _BIG]

    def weight(k, layer, got):
        parts = [jnp.where(chip == j, shards[k][layer], got[j]) for j in range(4)]
        if k == 0:
            return _w_in_from_chips(parts)
        return jnp.concatenate([parts[j] for j in FFN_BLOCK_ORDER], axis=1) if k == 2 else jnp.concatenate(parts, axis=0)

    def later(k, layer, *sources):
        made = []

        def get():
            if not made:
                got = [RIDERS.result(host)[pos] for host, pos in sources]
                made.append(weight(k, layer, got[0] if len(got) == 1 else jnp.concatenate(got, axis=1)))
            return made[0]
        return get

    cs3, sn3 = rope_tables(positions[0])
    eye4 = jnp.eye(4, dtype=F32)
    wts, sps = [], []
    for i in range(DEPTH):
        wts.append(dict(
            norm1_g=norm1_g[i][None], norm2_g=norm2_g[i][None], pool_scale=pool_scale[i][None],
            wbd=(eye4[:, None, :, None] * pool_w[i][:, :, None, :]).reshape(POOL_W, POOL_W),
            ffn_conv_w=_ffn_block_perm(fconv_w_full[i]), ffn_conv_b=_ffn_block_perm(ffn_conv_b[i])[None]))
        sps.append(dict(cw=conv_w_full[i], cb=ssd_conv_b[i][None], dtb=_pad_lanes(ssd_dt_bias[i]), alog=_pad_lanes(ssd_a_log[i]),
                        dsk=_pad_lanes(ssd_d[i]), ng=ssd_norm_g[i][None]))

    (w_in0,) = ride_alone("gather_w_in0", gather_ride(0, [shards[0]]))
    RIDERS.book("l0_ssd", gather_ride(0, [shards[1], shards[3]]))
    half = shards[2].shape[1] // 2
    RIDERS.book("l0_attn0", gather_ride(0, [shards[2][:, :half]]))
    RIDERS.book("l0_attn1", gather_ride(0, [shards[2][:, half:]]))
    wts[0].update(w_in=weight(0, 0, w_in0), w_out=later(1, 0, ("l0_ssd", 0)), ffn_down=later(3, 0, ("l0_ssd", 1)),
                  ffn_up=later(2, 0, ("l0_attn0", 0), ("l0_attn1", 0)))
    RIDERS.book("l0_attn2", gather_ride(1, [shards[0], shards[1]]))
    RIDERS.book("l0_ffn", gather_ride(1, [shards[2]]))
    RIDERS.book("l0_down", gather_ride(1, [shards[3]]))
    wts[1].update(w_in=later(0, 1, ("l0_attn2", 0)), w_out=later(1, 1, ("l0_attn2", 1)), ffn_up=later(2, 1, ("l0_ffn", 0)),
                  ffn_down=later(3, 1, ("l0_down", 0)))
    x1_, keep0 = _layer_forward(0, x, modv[0], wts[0], sps[0], cs3, sn3)
    xc, keep1 = _layer_forward(1, x1_, modv[1], wts[1], sps[1], cs3, sn3)
    keeps = [keep0, keep1]
    lossblk, dx, d_final = final_loss("final_loss", xc, loss_target, final_g[None])
    loss = lax.psum(lossblk[0, 0], ("x", "y", "c"))

    small_g, dmods = [None] * DEPTH, [None] * DEPTH
    part_sum, from_chips = [[None] * 4 for _ in range(DEPTH)], [[None] * 4 for _ in range(DEPTH)]

    def owner_sum(layer, ks, mine, theirs):
        for k, g, t in zip(ks, mine, theirs):
            part_sum[layer][k] = add_arrays(f"sum_cores{layer}_{_BIG[k]}", [g, t], BF16)

    dx, by_chip1, small_g[1], dmods[1] = _layer_backward(1, dx, keeps[1], modv[1], wts[1], sps[1], cs3, sn3)
    RIDERS.book("l0_ffn_b", to_owner_ride(1, by_chip1))

    def after_ffn_b():
        owner_sum(1, range(4), by_chip1, RIDERS.result("l0_ffn_b"))
        RIDERS.book("l0_up_bx", scatter_ride(1, [part_sum[1][2]]))
        RIDERS.book("l0_up_bw", scatter_ride(1, [part_sum[1][0], part_sum[1][1]]))
        RIDERS.book("l0_norm2_b", scatter_ride(1, [part_sum[1][3]]))

    early = []

    def after_wout_bw(g_wout, g_up, g_down):
        early.extend([g_wout, g_up, g_down])
        RIDERS.book("l0_ssd_b", to_owner_ride(0, early))

    def after_ssd_b():
        owner_sum(0, [1, 2, 3], early, RIDERS.result("l0_ssd_b"))
        for host, k in (("l0_attn0_b", 2), ("l0_attn1_b", 3), ("l0_attn2_b", 1)):
            RIDERS.book(host, scatter_ride(0, [part_sum[0][k]]))

    last = []

    def after_proj_bw(g_win):
        last.append(g_win)
        RIDERS.book("l0_proj_bx", to_owner_ride(0, last))

    def after_proj_bx():
        owner_sum(0, [0], last, RIDERS.result("l0_proj_bx"))
        RIDERS.book("l0_norm1_b", scatter_ride(0, [part_sum[0][0]]))

    hooks = dict(ffn_b=after_ffn_b, wout_bw=after_wout_bw, ssd_b=after_ssd_b, proj_bw=after_proj_bw, proj_bx=after_proj_bx)
    dx, _, small_g[0], dmods[0] = _layer_backward(0, dx, keeps[0], modv[0], wts[0], sps[0], cs3, sn3, after=hooks)
    from_chips[1][2], (from_chips[1][0], from_chips[1][1]) = RIDERS.result("l0_up_bx")[0], RIDERS.result("l0_up_bw")
    from_chips[1][3] = RIDERS.result("l0_norm2_b")[0]
    for host, k in (("l0_attn0_b", 2), ("l0_attn1_b", 3), ("l0_attn2_b", 1), ("l0_norm1_b", 0)):
        from_chips[0][k] = RIDERS.result(host)[0]
    mine = [sum_chips_mine(f"sum_chips_{n}", part_sum[0][k], from_chips[0][k], part_sum[1][k], from_chips[1][k])
            for k, n in enumerate(_BIG)]
    reduced = swap_layers("swap_r", mine, me[2])
    grads = {n: jnp.stack(r) for n, r in zip(_BIG, reduced)}

    part = dict(ada_b=jnp.stack(dmods), final_g=d_final[0])
    for n in _SMALL:
        if n not in part:
            part[n] = jnp.stack([small_g[i][n] for i in range(DEPTH)])
    full_shapes = [part[n].shape for n in _SMALL]
    gs = allgather8("gather_small", _pack([part[n] for n in _SMALL]))
    tot = _unpack(sum_slots("sum_small", gs, 8)[0], full_shapes)
    small_tot = dict(zip(_SMALL, tot))
    dmod_all = gs[:, :DEPTH * 6 * d // 128, :].reshape(8, DEPTH, 6 * d)
    for n, ncol in _COL_SHARDED_SMALL.items():
        small_tot[n] = lax.dynamic_slice_in_dim(small_tot[n], chip * ncol, ncol, axis=2)
    grads.update(small_tot)

    ncol = ada_w.shape[2]
    dm = lax.dynamic_slice_in_dim(dmod_all, chip * ncol, ncol, axis=2).transpose(1, 0, 2)
    upd = {}
    g_ada, *upd["ada_w"] = ada_backward("ada_bwd", c16, jnp.pad(dm, ((0, 0), (0, 8), (0, 0))), ada_w, m["ada_w"], v["ada_w"])
    grads["ada_w"] = g_ada

    for n in _BIG:
        upd[n] = adamw(f"adam_{n}", w[n], grads[n], m[n], v[n])
    shapes_s = [w[n].shape for n in _SMALL]
    packed = [_pack([src[n] for n in _SMALL]) for src in (w, grads, m, v)]
    outs_s = [_unpack(o, shapes_s) for o in adamw("adam_small", *packed)]
    for q, n in enumerate(_SMALL):
        upd[n] = [outs_s[0][q], outs_s[1][q], outs_s[2][q]]

    return (loss, dx, *[grads[n] for n in _WEIGHTS], *[upd[n][0] for n in _WEIGHTS], *[upd[n][1] for n in _WEIGHTS],
            *[upd[n][2] for n in _WEIGHTS])


class Ride:
    def __init__(self, ins, out_shapes, nsem, start, finish):
        self.ins, self.out_shapes, self.nsem, self.start, self.finish = ins, out_shapes, nsem, start, finish

    def specs(self):
        hbm = pl.BlockSpec(memory_space=pl.ANY)
        return [hbm] * len(self.ins), [hbm] * len(self.out_shapes), [pltpu.SemaphoreType.DMA((self.nsem,))] * 2

    def begin(self, in_refs, out_refs, sems, cond=None):
        me = (lax.axis_index("x"), lax.axis_index("y"), lax.axis_index("c"))
        go = lambda: self.start(in_refs, out_refs, sems[0], sems[1], me)
        go() if cond is None else pl.when(cond)(go)

    def end(self, in_refs, out_refs, sems, cond=None):
        me = (lax.axis_index("x"), lax.axis_index("y"), lax.axis_index("c"))
        go = lambda: self.finish(in_refs, out_refs, sems[0], sems[1], me)
        go() if cond is None else pl.when(cond)(go)


def ride_alone(name, ride):
    ni, no = len(ride.ins), len(ride.out_shapes)

    def body(*refs):
        ride.begin(refs[:ni], refs[ni:ni + no], refs[ni + no:])
        ride.end(refs[:ni], refs[ni:ni + no], refs[ni + no:])

    in_specs, out_specs, scratch = ride.specs()
    return list(pl.pallas_call(body, name=name, in_specs=in_specs, out_specs=out_specs, out_shape=ride.out_shapes,
                               scratch_shapes=scratch)(*ride.ins))


def mm(name, a, b, mode, out_dtype=F32, res=None, gate=None, tm=1408, tn=1536, tk=1408, into=None):
    ride = RIDERS.take(name)
    if mode == "nn":
        (m, k), n = a.shape, b.shape[1]
    elif mode == "nt":
        (m, k), n = a.shape, b.shape[0]
    else:
        (k, m), n = a.shape, b.shape[1]
    tm, tn, tk = _tile(m, tm), _tile(n, tn), _tile(k, tk)
    ni, nj, nk = m // tm, n // tn, k // tk
    a_spec = pl.BlockSpec((tk, tm), lambda i, j, q: (q, i)) if mode == "tn" else pl.BlockSpec((tm, tk), lambda i, j, q: (i, q))
    b_spec = pl.BlockSpec((tn, tk), lambda i, j, q: (j, q)) if mode == "nt" else pl.BlockSpec((tk, tn), lambda i, j, q: (q, j))
    o_spec = pl.BlockSpec((tm, tn), lambda i, j, q: (i, j))
    fused = res is not None
    lead = 0 if into is None else len(into[0]) - 2
    first = (0,) * lead + (slice(None), slice(None))
    ins, in_specs = [a, b], [a_spec, b_spec]
    out_shape, out_specs = [jax.ShapeDtypeStruct((m, n), out_dtype)], [o_spec]
    if fused:
        ins += [res, gate]
        in_specs += [o_spec, pl.BlockSpec((1, tn), lambda i, j, q: (0, j))]
        out_shape.append(jax.ShapeDtypeStruct((m, n), F32))
        out_specs.append(o_spec)
    if into is not None:
        shape, omap = into
        out_shape = [jax.ShapeDtypeStruct(shape, out_dtype)]
        out_specs = [pl.BlockSpec((1,) * lead + (tm, tn), lambda i, j, q: omap(i, j))]
    n_in, n_out = len(ins), len(out_shape)
    scratch = [pltpu.VMEM((tm, tn), F32)]
    if ride is not None:
        r_in, r_out, r_scr = ride.specs()
        ins, in_specs = ins + list(ride.ins), in_specs + r_in
        out_shape, out_specs = out_shape + list(ride.out_shapes), out_specs + r_out
        scratch = scratch + r_scr

    def body(*refs):
        a_ref, b_ref = refs[:2]
        o_ref = refs[len(ins)]
        acc = refs[len(ins) + len(out_shape)]
        i, j, q = pl.program_id(0), pl.program_id(1), pl.program_id(2)
        at = lambda x, y, z: jnp.logical_and(jnp.logical_and(i == x, j == y), q == z)
        r_refs = (refs[n_in:len(ins)], refs[len(ins) + n_out:len(ins) + len(out_shape)], refs[len(ins) + len(out_shape) + 1:])
        if ride is not None:
            ride.begin(*r_refs, at(0, 0, 0))

        @pl.when(q == 0)
        def _():
            acc[...] = jnp.zeros(acc.shape, F32)

        acc[...] += _mxu(a_ref[...], b_ref[...], mode)

        @pl.when(q == nk - 1)
        def _():
            o_ref[first] = acc[...].astype(o_ref.dtype)
            if fused:
                refs[len(ins) + 1][...] = refs[2][...] + refs[3][...] * acc[...]

        if ride is not None:
            ride.end(*r_refs, at(ni - 1, nj - 1, nk - 1))

    sem = ("arbitrary",) * 3 if ride is not None else ("parallel", "parallel", "arbitrary")
    out = pl.pallas_call(
        body, name=name, grid=(ni, nj, nk), in_specs=in_specs, out_specs=out_specs, out_shape=out_shape, scratch_shapes=scratch,
        compiler_params=pltpu.CompilerParams(dimension_semantics=sem, vmem_limit_bytes=VMEM_LIMIT_BYTES),
    )(*ins)
    if ride is not None:
        RIDERS.done[name] = list(out[n_out:])
    return tuple(out[:n_out]) if fused else out[0]


def add_arrays(name, arrs, out_dtype=F32):
    nb, r, c = arrs[0].shape
    t = _tile(r, 256, 8)
    (out,), _ = scan_fwd(name, _sum_fn, nb=nb, nchunk=r // t, t=t, rows=[Row(a, fb=lambda b: b) for a in arrs], vecs=[], carries=[],
                         outs=[out_row((nb, r, c), out_dtype, fb=lambda b: b)], save=False)
    return out


def _sum_chips_mine_fn(ci, b, carries, rows, vecs):
    mine_layer = lax.axis_index("c")
    chip = 2 * lax.axis_index("x") + lax.axis_index("y")
    tot = None
    for j in range(4):
        own = jnp.where(mine_layer == 0, rows[j], rows[8 + j])
        sent = jnp.where(mine_layer == 0, rows[4 + j], rows[12 + j])
        term = jnp.where(chip == j, own, sent)
        tot = term if tot is None else tot + term
    return [], [tot]


def sum_chips_mine(name, p0, q0, p1, q1):
    _, r, c = p0.shape
    t = _tile(r, 256, 8)
    rows = [Row(a, fb=(lambda b, j=j: j)) for a in (p0, q0, p1, q1) for j in range(4)]
    (out,), _ = scan_fwd(name, _sum_chips_mine_fn, nb=1, nchunk=r // t, t=t, rows=rows, vecs=[], carries=[],
                         outs=[out_row((1, r, c))], save=False)
    return out[0]


def _remote(src, dst, send_sems, recv_sems, k, to):
    return pltpu.make_async_remote_copy(src_ref=src, dst_ref=dst, send_sem=send_sems.at[k], recv_sem=recv_sems.at[k],
                                        device_id=to, device_id_type=MESH)


def gather_ride(layer, shards):
    na = len(shards)

    def start(ins, outs, ss, rs, me):
        @pl.when(me[2] == layer)
        def _():
            for k in range(na):
                for p, mask in enumerate(CHIP_PEERS):
                    _remote(ins[k].at[layer], outs[k].at[_chip(me)], ss, rs, 6 * k + p, _flip(mask, me)).start()

    def finish(ins, outs, ss, rs, me):
        sibling = _flip(SIBLING[0], me)

        @pl.when(me[2] == layer)
        def _():
            for k in range(na):
                for p, mask in enumerate(CHIP_PEERS):
                    slot = outs[k].at[_chip(_flip(mask, me))]
                    _remote(ins[k].at[layer], slot, ss, rs, 6 * k + p, _flip(mask, me)).wait_recv()
                    _remote(slot, slot, ss, rs, 6 * k + 3 + p, sibling).start()
            for k in range(na):
                for p, mask in enumerate(CHIP_PEERS):
                    slot = outs[k].at[_chip(_flip(mask, me))]
                    _remote(ins[k].at[layer], slot, ss, rs, 6 * k + p, _flip(mask, me)).wait_send()
                    _remote(slot, slot, ss, rs, 6 * k + 3 + p, sibling).wait_send()

        @pl.when(me[2] != layer)
        def _():
            for k in range(na):
                for p, mask in enumerate(CHIP_PEERS):
                    slot = outs[k].at[_chip(_flip(mask, me))]
                    _remote(slot, slot, ss, rs, 6 * k + 3 + p, sibling).wait_recv()

    return Ride(list(shards), [jax.ShapeDtypeStruct((4,) + a.shape[1:], a.dtype) for a in shards], 6 * na, start, finish)


def scatter_ride(layer, parts):
    na = len(parts)

    def start(ins, outs, ss, rs, me):
        @pl.when(me[2] == layer)
        def _():
            for k in range(na):
                for p, mask in enumerate(CHIP_PEERS):
                    peer = _flip(mask, me)
                    _remote(ins[k].at[_chip(peer)], outs[k].at[_chip(me)], ss, rs, 3 * k + p, peer).start()

    def finish(ins, outs, ss, rs, me):
        @pl.when(me[2] == layer)
        def _():
            for k in range(na):
                for p, mask in enumerate(CHIP_PEERS):
                    peer = _flip(mask, me)
                    _remote(ins[k].at[_chip(peer)], outs[k].at[_chip(peer)], ss, rs, 3 * k + p, peer).wait_recv()
                    _remote(ins[k].at[_chip(peer)], outs[k].at[_chip(me)], ss, rs, 3 * k + p, peer).wait_send()

    return Ride(list(parts), [jax.ShapeDtypeStruct(a.shape, a.dtype) for a in parts], 3 * na, start, finish)


def to_owner_ride(layer, arrays):
    na = len(arrays)

    def start(ins, outs, ss, rs, me):
        @pl.when(me[2] != layer)
        def _():
            for k in range(na):
                _remote(ins[k], outs[k], ss, rs, k, _flip(SIBLING[0], me)).start()

    def finish(ins, outs, ss, rs, me):
        for k in range(na):
            cp = _remote(ins[k], outs[k], ss, rs, k, _flip(SIBLING[0], me))
            pl.when(me[2] != layer)(cp.wait_send)
            pl.when(me[2] == layer)(cp.wait_recv)

    return Ride(list(arrays), [jax.ShapeDtypeStruct(a.shape, a.dtype) for a in arrays], na, start, finish)


def _w_in_from_chips(a):
    c2 = a[2]
    pad = jnp.zeros((c2.shape[0], IN_WP - IN_W), c2.dtype)
    return jnp.concatenate([a[0], a[1], c2[:, :252], c2[:, 260:516], c2[:, 252:260], pad, c2[:, 516:], a[3]], axis=1)


def _mm_host(rides, rode, key, *args, **kw):
    if rides is None or key not in rides:
        return mm(*args, **kw)
    main, rode[key] = mm(*args, ride=rides[key], **kw)
    return main
```

```python
import functools
import math

import numpy as np
import jax
import jax.numpy as jnp
from jax import lax
from jax.experimental import pallas as pl
from jax.experimental.pallas import tpu as pltpu

F32 = jnp.float32
BF16 = jnp.bfloat16
HI = lax.Precision.HIGHEST
MESH = pl.DeviceIdType.MESH

D_MODEL = 1024
SEQ = 4096
DEPTH = 2
SSD_INNER = 512
SSD_HEADS = 8
SSD_STATE = 128
POOL_W = 256
POOL_WINDOWS = (2, 4, 8, 16)
ATT_W = 256
ATT_HEADS = 4
ATT_HEAD_DIM = 64
ATT_PATTERNS = ((128, 1), (512, 4), (2048, 16))
ATT_BLOCK = 128
ROT_DIM = 16
ROPE_THETA = 500000.0
IN_W = 2568
IN_WP = 2688
IN_MAIN = 1920
FFN_DIM = 2816
NORM_EPS = 1e-6
ADAM_LR, ADAM_B1, ADAM_B2, ADAM_EPS, ADAM_WD, ADAM_STEP = 0.001, 0.9, 0.999, 1e-08, 0.01, 10

VMEM_LIMIT_BYTES = 56 * 1024 * 1024
NEG = -1e30


def _mxu(a, b, mode):
    dims = {"nn": ((1,), (0,)), "nt": ((1,), (1,)), "tn": ((0,), (0,))}[mode]
    return lax.dot_general(a.astype(BF16), b.astype(BF16), (dims, ((), ())), preferred_element_type=F32)


@functools.partial(jax.custom_vjp, nondiff_argnums=(2,))
def _bdot(a, b, mode):
    return _mxu(a, b, mode)


def _bdot_fwd(a, b, mode):
    return _mxu(a, b, mode), (a, b)


def _bdot_bwd(mode, res, g):
    a, b = res
    if mode == "nn":
        return _mxu(g, b, "nt"), _mxu(a, g, "tn")
    if mode == "nt":
        return _mxu(g, b, "nn"), _mxu(g, a, "tn")
    return _mxu(b, g, "nt"), _mxu(a, g, "nn")


_bdot.defvjp(_bdot_fwd, _bdot_bwd)


def _fxu(a, b, mode):
    dims = {"nn": ((1,), (0,)), "nt": ((1,), (1,)), "tn": ((0,), (0,))}[mode]
    return lax.dot_general(a, b, (dims, ((), ())), precision=HI, preferred_element_type=F32)


@functools.partial(jax.custom_vjp, nondiff_argnums=(2,))
def _fdot(a, b, mode):
    return _fxu(a, b, mode)


def _fdot_fwd(a, b, mode):
    return _fxu(a, b, mode), (a, b)


def _fdot_bwd(mode, res, g):
    a, b = res
    if mode == "nn":
        return _fxu(g, b, "nt"), _fxu(a, g, "tn")
    if mode == "nt":
        return _fxu(g, b, "nn"), _fxu(g, a, "tn")
    return _fxu(b, g, "nt"), _fxu(a, g, "nn")


_fdot.defvjp(_fdot_fwd, _fdot_bwd)


def _iota(shape, dim):
    return lax.broadcasted_iota(jnp.int32, shape, dim)


def _make_shift(h):
    @functools.partial(jax.custom_vjp, nondiff_argnums=(2,))
    def shift(halo, cur, k):
        if k == 0:
            return cur
        full = jnp.concatenate([halo, cur], axis=0)
        return pltpu.roll(full, k, 0)[h:]

    def fwd(halo, cur, k):
        return shift(halo, cur, k), None

    def bwd(k, _, g):
        t, w = g.shape
        if k == 0:
            return jnp.zeros((h, w), F32), g
        d_cur = jnp.where(_iota((t, w), 0) < t - k, pltpu.roll(g, t - k, 0), 0.0)
        top = g[:h]
        d_halo = jnp.where(_iota((h, w), 0) >= h - k, pltpu.roll(top, h - k, 0) if k < h else top, 0.0)
        return d_halo, d_cur

    shift.defvjp(fwd, bwd)
    return shift


_shift8 = _make_shift(8)
_shift16 = _make_shift(16)


def _make_tail(h):
    @jax.custom_vjp
    def tail(x):
        return x[x.shape[0] - h:]

    def fwd(x):
        return tail(x), x.shape[0]

    def bwd(t, g):
        return (jnp.concatenate([jnp.zeros((t - h, g.shape[1]), F32), g], axis=0),)

    tail.defvjp(fwd, bwd)
    return tail


_tail8 = _make_tail(8)
_tail16 = _make_tail(16)


@jax.custom_vjp
def _cumsum_rows(x):
    t = x.shape[0]
    row, s = _iota(x.shape, 0), 1
    while s < t:
        x = x + jnp.where(row >= s, pltpu.roll(x, s, 0), 0.0)
        s *= 2
    return x


def _cumsum_rows_fwd(x):
    return _cumsum_rows(x), None


def _cumsum_rows_bwd(_, g):
    t = g.shape[0]
    row, s = _iota(g.shape, 0), 1
    while s < t:
        g = g + jnp.where(row < t - s, pltpu.roll(g, t - s, 0), 0.0)
        s *= 2
    return (g,)


_cumsum_rows.defvjp(_cumsum_rows_fwd, _cumsum_rows_bwd)


@jax.custom_vjp
def _rot_pairs(t):
    e = _iota(t.shape, 1) % ATT_HEAD_DIM
    n = t.shape[1]
    return jnp.where(e < 8, -pltpu.roll(t, n - 8, 1), jnp.where(e < 16, pltpu.roll(t, 8, 1), 0.0))


def _rot_pairs_fwd(t):
    return _rot_pairs(t), None


def _rot_pairs_bwd(_, g):
    e = _iota(g.shape, 1) % ATT_HEAD_DIM
    n = g.shape[1]
    return (pltpu.roll(jnp.where(e < 8, -g, 0.0), 8, 1) + pltpu.roll(jnp.where(jnp.logical_and(e >= 8, e < 16), g, 0.0), n - 8, 1),)


_rot_pairs.defvjp(_rot_pairs_fwd, _rot_pairs_bwd)


def _make_thirds():
    @jax.custom_vjp
    def thirds(x):
        w = x.shape[1] // 3
        return x[:, :w], x[:, w:2 * w], x[:, 2 * w:]

    def fwd(x):
        return thirds(x), None

    def bwd(_, g):
        return (jnp.concatenate(g, axis=1),)

    thirds.defvjp(fwd, bwd)
    return thirds


_thirds = _make_thirds()


def _rowk(w, k):
    return jnp.sum(jnp.where(_iota(w.shape, 0) == k, w, 0.0), axis=0, keepdims=True)


def _silu(x):
    return x * (0.5 * jnp.tanh(0.5 * x) + 0.5)


def _softplus(x):
    return jnp.maximum(x, 0.0) + jnp.log(1.0 + jnp.exp(-jnp.abs(x)))


def _tile(dim, target, unit=128):
    if dim <= target:
        return dim
    best = None
    for t in range(unit, target + 1, unit):
        if dim % t == 0:
            best = t
    assert best is not None, (dim, target)
    return best


class Ride:
    def __init__(self, ins, out_shapes, nsem, start, finish):
        self.ins, self.out_shapes, self.nsem, self.start, self.finish = ins, out_shapes, nsem, start, finish

    def specs(self):
        hbm = pl.BlockSpec(memory_space=pl.ANY)
        return [hbm] * len(self.ins), [hbm] * len(self.out_shapes), [pltpu.SemaphoreType.DMA((self.nsem,))] * 2

    def begin(self, in_refs, out_refs, sems, cond=None):
        me = (lax.axis_index("x"), lax.axis_index("y"), lax.axis_index("c"))
        go = lambda: self.start(in_refs, out_refs, sems[0], sems[1], me)
        go() if cond is None else pl.when(cond)(go)

    def end(self, in_refs, out_refs, sems, cond=None):
        me = (lax.axis_index("x"), lax.axis_index("y"), lax.axis_index("c"))
        go = lambda: self.finish(in_refs, out_refs, sems[0], sems[1], me)
        go() if cond is None else pl.when(cond)(go)


class _Riders:
    def reset(self):
        self.booked, self.done = {}, {}

    def book(self, host, ride):
        assert host not in self.booked, host
        self.booked[host] = ride

    def take(self, host):
        return self.booked.pop(host, None)

    def result(self, host):
        return self.done[host]


RIDERS = _Riders()
RIDERS.reset()


class Row:
    def __init__(self, arr, w=None, fb=None, fc=None, diff=True, slot=False, dcols=None, dfc=None, ddtype=F32, view=None):
        self.ddtype = ddtype
        self.view = view
        self.arr = arr
        self.w = arr.shape[2] if w is None else w
        self.fb = (lambda b: 0) if fb is None else fb
        self.fc = (lambda b: 0) if fc is None else fc
        self.diff = diff
        self.slot = slot
        self.dcols = dcols
        self.dfc = dfc


class Vec:
    def __init__(self, arr, w=None, fc=None, diff=True):
        self.arr = arr
        self.w = arr.shape[1] if w is None else w
        self.fc = fc
        self.diff = diff


def _row_spec(r, t, nchunk, reverse):
    shape = (1, t, r.w) if r.view is None else (1, t // r.view, r.view * r.w)
    if reverse:
        return pl.BlockSpec(shape, lambda b, i, r=r: (r.fb(b), nchunk - 1 - i, r.fc(b)))
    return pl.BlockSpec(shape, lambda b, i, r=r: (r.fb(b), i, r.fc(b)))


def _load_row(ref, r, t, scr):
    if r.view is None:
        return ref[0]
    d, w = r.view, r.w
    for q in range(d):
        for j in range(w // 128):
            scr[j, pl.ds(q, t // d, stride=d), :] = ref[0, :, q * w + 128 * j:q * w + 128 * (j + 1)].astype(F32)
    return jnp.concatenate([scr[j] for j in range(w // 128)], axis=1)


def _store_row(ref, r, t, scr, val):
    if r.view is None:
        ref[0] = val.astype(ref.dtype)
        return
    d, w = r.view, r.w
    for j in range(w // 128):
        scr[j] = val[:, 128 * j:128 * (j + 1)]
    for q in range(d):
        for j in range(w // 128):
            ref[0, :, q * w + 128 * j:q * w + 128 * (j + 1)] = scr[j, pl.ds(q, t // d, stride=d), :].astype(ref.dtype)


def _view_scratch(specs, t):
    ws = [r.w for r in specs if r.view is not None]
    return [pltpu.VMEM((max(ws) // 128, t, 128), F32)] if ws else []


def _vec_spec(v):
    if v.fc is None:
        return pl.BlockSpec(v.arr.shape, lambda b, i: (0, 0))
    return pl.BlockSpec((v.arr.shape[0], v.w), lambda b, i, v=v: (0, v.fc(b)))


def _cparams():
    return pltpu.CompilerParams(dimension_semantics=("arbitrary", "arbitrary"), vmem_limit_bytes=VMEM_LIMIT_BYTES)


def scan_fwd(name, fn, *, nb, nchunk, t, rows, vecs, carries, outs, save):
    nr, nv, nc, no = len(rows), len(vecs), len(carries), len(outs)
    ns = nc if save else 0
    ride = RIDERS.take(name)
    r_in, r_out, r_scr = ride.specs() if ride else ([], [], [])

    def body(*refs):
        p = 0
        row_refs = refs[p:p + nr]; p += nr
        vec_refs = refs[p:p + nv]; p += nv
        ride_in = refs[p:p + len(r_in)]; p += len(r_in)
        out_refs = refs[p:p + no]; p += no
        save_refs = refs[p:p + ns]; p += ns
        ride_out = refs[p:p + len(r_out)]; p += len(r_out)
        car = refs[p:p + nc]; p += nc
        scr = refs[p] if stage else None
        sems = refs[p + len(stage):]
        b, i = pl.program_id(0), pl.program_id(1)
        if ride:
            ride.begin(ride_in, ride_out, sems, jnp.logical_and(b == 0, i == 0))
        if nc:
            @pl.when(i == 0)
            def _():
                for c_ref in car:
                    c_ref[...] = jnp.zeros(c_ref.shape, F32)
        cin = [c_ref[...] for c_ref in car]
        if save:
            for s_ref, cv in zip(save_refs, cin):
                s_ref[0, 0] = cv
        new_c, o = fn(i, b, cin, [_load_row(ref, r, t, scr) for ref, r in zip(row_refs, rows)], [v[...] for v in vec_refs])
        for c_ref, cv in zip(car, new_c):
            c_ref[...] = cv
        for o_ref, spec, ov in zip(out_refs, outs, o):
            _store_row(o_ref, spec, t, scr, ov)
        if ride:
            ride.end(ride_in, ride_out, sems, jnp.logical_and(b == nb - 1, i == nchunk - 1))

    stage = _view_scratch(list(rows) + list(outs), t)
    out_shape = [o.arr for o in outs]
    out_specs = [_row_spec(o, t, nchunk, False) for o in outs]
    if save:
        for cs in carries:
            out_shape.append(jax.ShapeDtypeStruct((nb, nchunk) + tuple(cs), F32))
            out_specs.append(pl.BlockSpec((1, 1) + tuple(cs), lambda b, i: (b, i, 0, 0)))
    res = pl.pallas_call(
        body, name=name, grid=(nb, nchunk),
        in_specs=[_row_spec(r, t, nchunk, False) for r in rows] + [_vec_spec(v) for v in vecs] + r_in,
        out_specs=out_specs + r_out, out_shape=out_shape + (list(ride.out_shapes) if ride else []),
        scratch_shapes=[pltpu.VMEM(tuple(cs), F32) for cs in carries] + stage + r_scr,
        compiler_params=_cparams(),
    )(*[r.arr for r in rows], *[v.arr for v in vecs], *(ride.ins if ride else []))
    if ride:
        RIDERS.done[name] = list(res[no + ns:])
    return list(res[:no]), list(res[no:no + ns])


def scan_bwd(name, fn, *, nb, nchunk, t, rows, vecs, carries, saved, douts, adds=None):
    adds = adds or {}
    nr, nv, nc, no = len(rows), len(vecs), len(carries), len(douts)
    dri = [k for k, r in enumerate(rows) if r.diff]
    dvi = [k for k, v in enumerate(vecs) if v.diff]
    add_keys = sorted(adds)
    na = len(add_keys)
    ride = RIDERS.take(name)
    r_in, r_out, r_scr = ride.specs() if ride else ([], [], [])

    def body(*refs):
        p = 0
        row_refs = refs[p:p + nr]; p += nr
        vec_refs = refs[p:p + nv]; p += nv
        save_refs = refs[p:p + nc]; p += nc
        dout_refs = refs[p:p + no]; p += no
        add_refs = refs[p:p + na]; p += na
        ride_in = refs[p:p + len(r_in)]; p += len(r_in)
        drow_refs = refs[p:p + len(dri)]; p += len(dri)
        dvec_refs = refs[p:p + len(dvi)]; p += len(dvi)
        ride_out = refs[p:p + len(r_out)]; p += len(r_out)
        dcar = refs[p:p + nc]; p += nc
        scr = refs[p] if stage else None
        sems = refs[p + len(stage):]
        b, ir = pl.program_id(0), pl.program_id(1)
        ci = nchunk - 1 - ir
        if ride:
            ride.begin(ride_in, ride_out, sems, jnp.logical_and(b == 0, ir == 0))
        if nc:
            @pl.when(ir == 0)
            def _():
                for c_ref in dcar:
                    c_ref[...] = jnp.zeros(c_ref.shape, F32)
        rows_v = [_load_row(ref, r, t, scr) for ref, r in zip(row_refs, rows)]
        vecs_v = [v[...] for v in vec_refs]
        cin = [s[0, 0] for s in save_refs]
        dc = [c_ref[...] for c_ref in dcar]
        dout_v = [_load_row(ref, r, t, scr).astype(F32) for ref, r in zip(dout_refs, douts)]

        def f(cs, dr, dv):
            rr, vv = list(rows_v), list(vecs_v)
            for k, idx in enumerate(dri):
                rr[idx] = dr[k]
            for k, idx in enumerate(dvi):
                vv[idx] = dv[k]
            return fn(ci, b, cs, rr, vv)

        _, vjp = jax.vjp(f, cin, [rows_v[k].astype(F32) for k in dri], [vecs_v[k].astype(F32) for k in dvi])
        dcin, drows, dvecs = vjp((dc, dout_v))
        for c_ref, cv in zip(dcar, dcin):
            c_ref[...] = cv
        for k, (o_ref, ov) in enumerate(zip(drow_refs, drows)):
            if dri[k] in adds:
                ov = ov + add_refs[add_keys.index(dri[k])][0].astype(F32)
            _store_row(o_ref, rows[dri[k]], t, scr, ov)
        for k, (o_ref, ov) in enumerate(zip(dvec_refs, dvecs)):
            first = (ir == 0) if vecs[dvi[k]].fc is not None else jnp.logical_and(ir == 0, b == 0)

            @pl.when(first)
            def _(o_ref=o_ref, ov=ov):
                o_ref[...] = ov

            @pl.when(jnp.logical_not(first))
            def _(o_ref=o_ref, ov=ov):
                o_ref[...] += ov

        if ride:
            ride.end(ride_in, ride_out, sems, jnp.logical_and(b == nb - 1, ir == nchunk - 1))

    stage = _view_scratch(list(rows) + list(douts), t)
    in_specs = ([_row_spec(r, t, nchunk, True) for r in rows] + [_vec_spec(v) for v in vecs]
                + [pl.BlockSpec((1, 1) + tuple(cs), lambda b, i: (b, nchunk - 1 - i, 0, 0)) for cs in carries]
                + [_row_spec(d, t, nchunk, True) for d in douts]
                + [_row_spec(adds[k], t, nchunk, True) for k in add_keys] + r_in)
    out_shape, out_specs = [], []
    for k in dri:
        r = rows[k]
        if r.slot:
            out_shape.append(jax.ShapeDtypeStruct((nb, r.arr.shape[1], r.w), r.ddtype))
            out_specs.append(pl.BlockSpec((1, t, r.w), lambda b, i: (b, nchunk - 1 - i, 0)))
        elif r.dcols is not None:
            out_shape.append(jax.ShapeDtypeStruct((r.arr.shape[0], r.arr.shape[1], r.dcols), r.ddtype))
            out_specs.append(pl.BlockSpec((1, t, r.w), lambda b, i, r=r: (r.fb(b), nchunk - 1 - i, r.dfc(b))))
        else:
            out_shape.append(jax.ShapeDtypeStruct(r.arr.shape, r.ddtype))
            out_specs.append(_row_spec(r, t, nchunk, True))
    for k in dvi:
        out_shape.append(jax.ShapeDtypeStruct(vecs[k].arr.shape, F32))
        out_specs.append(_vec_spec(vecs[k]))
    nd = len(dri) + len(dvi)
    res = pl.pallas_call(
        body, name=name, grid=(nb, nchunk), in_specs=in_specs, out_specs=out_specs + r_out,
        out_shape=out_shape + (list(ride.out_shapes) if ride else []),
        scratch_shapes=[pltpu.VMEM(tuple(cs), F32) for cs in carries] + stage + r_scr,
        compiler_params=_cparams(),
    )(*[r.arr for r in rows], *[v.arr for v in vecs], *saved, *[d.arr for d in douts], *[adds[k].arr for k in add_keys],
      *(ride.ins if ride else []))
    if ride:
        RIDERS.done[name] = list(res[nd:])
    return list(res[:len(dri)]), list(res[len(dri):nd])


def out_row(shape, dtype=F32, w=None, fb=None, fc=None):
    return Row(jax.ShapeDtypeStruct(shape, dtype), w, fb, fc)


def _conv(shift, halo, cur, w, bias, taps):
    y = bias
    for k in range(taps):
        y = y + _rowk(w, k) * shift(halo, cur, taps - 1 - k)
    return y


def _ssd_fn(ci, b, carries, rows, vecs):
    cx, cb_, cc, ht = carries
    z, xr, br, cr, dtr = rows
    cwx, cbx, cwb, cbb, cwc, cbc, dtb, alog, dsk, ng = vecs
    t = z.shape[0]
    xs = _silu(_conv(_shift8, cx, xr, cwx, cbx, 4))
    bm = _silu(_conv(_shift8, cb_, br, cwb, cbb, 4))
    cm = _silu(_conv(_shift8, cc, cr, cwc, cbc, 4))
    dt = _softplus(dtr + dtb)
    acol = _cumsum_rows(dt * (-jnp.exp(alog)))
    arow = acol.T
    r, c = _iota((t, t), 0), _iota((t, t), 1)
    causal = r >= c
    cbm = _bdot(cm, bm, "nt")
    lane, sub = _iota(acol.shape, 1), _iota(arow.shape, 0)
    colh = _iota(xs.shape, 1) // 64
    a, dtx, dx, acs = jnp.zeros(xs.shape, F32), jnp.zeros(xs.shape, F32), jnp.zeros((1, xs.shape[1]), F32), []
    for j in range(4):
        h = 4 * b + j
        ac = jnp.sum(jnp.where(lane == h, acol, 0.0), axis=1, keepdims=True)
        acs.append(ac)
        a = jnp.where(colh == j, ac, a)
        dtx = jnp.where(colh == j, jnp.sum(jnp.where(lane == h, dt, 0.0), axis=1, keepdims=True), dtx)
        dx = jnp.where(_iota(dx.shape, 1) // 64 == j, jnp.sum(jnp.where(_iota(dsk.shape, 1) == h, dsk, 0.0), axis=1, keepdims=True), dx)
    atot = jnp.sum(jnp.where(_iota(a.shape, 0) == t - 1, a, 0.0), axis=0, keepdims=True)
    x = xs * dtx
    ydiag = jnp.zeros(x.shape, F32)
    for j in range(4):
        ar = jnp.sum(jnp.where(sub == 4 * b + j, arow, 0.0), axis=0, keepdims=True)
        lmat = jnp.exp(jnp.where(causal, acs[j] - ar, NEG))
        ydiag = ydiag + _bdot(cbm * lmat, jnp.where(colh == j, x, 0.0), "nn")
    yoff = _bdot(cm, ht, "nn") * jnp.exp(a)
    ht_new = ht * jnp.exp(atot) + _bdot(bm, x * jnp.exp(atot - a), "tn")
    y = ydiag + yoff + dx * xs
    yz = y * _silu(z)
    yn = yz * lax.rsqrt(jnp.mean(yz * yz, axis=-1, keepdims=True) + NORM_EPS) * ng
    return [_tail8(xr), _tail8(br), _tail8(cr), ht_new], [yn]


_SSD_T = 128
_SSD_CARRIES = [(8, 256), (8, 128), (8, 128), (128, 256)]


def _ssd_io(proj3, p):
    own = lambda b: b
    rows = [Row(proj3, 256, fc=own, dcols=512, dfc=own, ddtype=BF16),
            Row(proj3, 256, fc=lambda b: 2 + b, dcols=512, dfc=own, ddtype=BF16),
            Row(proj3, 128, fc=lambda b: 8 + b, dcols=256, dfc=own, ddtype=BF16),
            Row(proj3, 128, fc=lambda b: 10 + b, dcols=256, dfc=own, ddtype=BF16),
            Row(proj3, 128, fc=lambda b: 14, slot=True)]
    vecs = [Vec(p["cw"], 256, lambda b: b), Vec(p["cb"], 256, lambda b: b),
            Vec(p["cw"], 128, lambda b: 4 + b), Vec(p["cb"], 128, lambda b: 4 + b),
            Vec(p["cw"], 128, lambda b: 6 + b), Vec(p["cb"], 128, lambda b: 6 + b),
            Vec(p["dtb"]), Vec(p["alog"]), Vec(p["dsk"]), Vec(p["ng"], 256, lambda b: b)]
    return rows, vecs


def ssd_forward(name, proj3, p):
    rows, vecs = _ssd_io(proj3, p)
    s = proj3.shape[1]
    (y,), saved = scan_fwd(name, _ssd_fn, nb=2, nchunk=s // _SSD_T, t=_SSD_T, rows=rows, vecs=vecs,
                           carries=_SSD_CARRIES, outs=[out_row((1, s, SSD_INNER), BF16, 256, fc=lambda b: b)], save=True)
    return y, saved


def ssd_backward(name, proj3, p, saved, dmix3):
    rows, vecs = _ssd_io(proj3, p)
    s = proj3.shape[1]
    drows, dvecs = scan_bwd(name, _ssd_fn, nb=2, nchunk=s // _SSD_T, t=_SSD_T, rows=rows, vecs=vecs,
                            carries=_SSD_CARRIES, saved=saved, douts=[Row(dmix3, 256, fc=lambda b: b)])
    return drows, dvecs


def _pool_fn(ci, b, carries, rows, vecs):
    (cu,) = carries
    (u,) = rows
    wbd, scale = vecs
    t = u.shape[0]
    pos = ci * t + _iota(u.shape, 0)
    grp = _iota(u.shape, 1) // 64
    acc, pooled, k = u, jnp.zeros(u.shape, F32), 1
    for gi, w in enumerate(POOL_WINDOWS):
        while k < w:
            acc = acc + _shift16(cu, u, k)
            k += 1
        pooled = jnp.where(grp == gi, acc / jnp.minimum(pos + 1, w).astype(F32), pooled)
    y = _bdot(pooled - u, wbd, "nn") * scale
    return [_tail16(u)], [y]


_POOL_T = 256


def _pool_io(proj3, wbd, scale):
    return [Row(proj3, 256, fc=lambda b: 6, dcols=256, dfc=lambda b: 0, ddtype=BF16)], [Vec(wbd), Vec(scale)]


def pool_forward(name, proj3, wbd, scale):
    rows, vecs = _pool_io(proj3, wbd, scale)
    s = proj3.shape[1]
    (y,), saved = scan_fwd(name, _pool_fn, nb=1, nchunk=s // _POOL_T, t=_POOL_T, rows=rows, vecs=vecs,
                           carries=[(16, 256)], outs=[out_row((1, s, POOL_W), BF16)], save=True)
    return y, saved


def pool_backward(name, proj3, wbd, scale, saved, dmix3):
    rows, vecs = _pool_io(proj3, wbd, scale)
    s = proj3.shape[1]
    return scan_bwd(name, _pool_fn, nb=1, nchunk=s // _POOL_T, t=_POOL_T, rows=rows, vecs=vecs,
                    carries=[(16, 256)], saved=saved, douts=[Row(dmix3, 256, fc=lambda b: 2)])


def _attn_fn(ci, b, carries, rows, vecs):
    kp, vp = carries
    qr, kr, v = _thirds(rows[0])
    scale = ATT_HEAD_DIM ** -0.5
    q = qr
    n = q.shape[0]
    r, c = _iota((n, n), 0), _iota((n, n), 1)
    prev_ok, cur_ok = jnp.logical_and(c >= r, ci > 0), r >= c
    head = _iota(q.shape, 1) // ATT_HEAD_DIM
    o, lse = jnp.zeros(q.shape, F32), jnp.zeros(q.shape, F32)
    for h in range(ATT_HEADS):
        mine = head == h
        qh = jnp.where(mine, qr, 0.0)
        sp = jnp.where(prev_ok, _bdot(qh, kp, "nt") * scale, NEG)
        sc = jnp.where(cur_ok, _bdot(qh, kr, "nt") * scale, NEG)
        m = lax.stop_gradient(jnp.maximum(jnp.max(sp, axis=1, keepdims=True), jnp.max(sc, axis=1, keepdims=True)))
        pp, pc = jnp.exp(sp - m), jnp.exp(sc - m)
        l = jnp.sum(pp, axis=1, keepdims=True) + jnp.sum(pc, axis=1, keepdims=True)
        o = jnp.where(mine, (_bdot(pp, vp, "nn") + _bdot(pc, v, "nn")) / l, o)
        lse = jnp.where(mine, m + jnp.log(l), lse)
    return [kr, v], [o, lse]


_ATT_CARRIES = [(ATT_BLOCK, ATT_W), (ATT_BLOCK, ATT_W)]


def attn_forward(name, pv, d):
    l = pv.shape[1]
    own = lambda b: b
    outs = [out_row((1, l, d * ATT_W), F32, ATT_W, fc=own) for _ in range(2)]
    (o, lse), saved = scan_fwd(name, _attn_fn, nb=d, nchunk=l // ATT_BLOCK, t=ATT_BLOCK, rows=[Row(pv, 3 * ATT_W, fc=own)],
                               vecs=[], carries=_ATT_CARRIES, outs=outs, save=True)
    return o, lse, saved


def attn_backward(name, pv, d, saved, do, dlse):
    l = pv.shape[1]
    own = lambda b: b
    (dpv,), _ = scan_bwd(name, _attn_fn, nb=d, nchunk=l // ATT_BLOCK, t=ATT_BLOCK, rows=[Row(pv, 3 * ATT_W, fc=own)], vecs=[],
                         carries=_ATT_CARRIES, saved=saved, douts=[Row(do, ATT_W, fc=own), Row(dlse, ATT_W, fc=own)])
    return dpv


def _rope_fn(ci, b, carries, rows, vecs):
    x, cs, sn = rows
    return [], [x * cs + _rot_pairs(x) * sn]


def _rope3_fn(ci, b, carries, rows, vecs):
    _, (y,) = _rope_fn(ci, b, carries, rows, vecs)
    return [], [y, y, y]


def _by_residue(a_or_shape, w, d):
    if isinstance(a_or_shape, tuple):
        _, s, _ = a_or_shape
        return Row(jax.ShapeDtypeStruct((1, s // d, d * w), F32), w, view=None if d == 1 else d)
    return Row(a_or_shape, w, view=None if d == 1 else d)


def rope_forward(name, qkv3, cs3, sn3):
    s, w = qkv3.shape[1], qkv3.shape[2]
    ys, _ = scan_fwd(name, _rope3_fn, nb=1, nchunk=s // _ROW_T, t=_ROW_T, vecs=[], carries=[], save=False,
                     rows=[Row(qkv3), Row(cs3, diff=False), Row(sn3, diff=False)],
                     outs=[_by_residue(qkv3.shape, w, d) for _, d in ATT_PATTERNS])
    return ys


def rope_backward(name, qkv3, cs3, sn3, dys):
    s, w = qkv3.shape[1], qkv3.shape[2]
    (dx,), _ = scan_bwd(name, _rope3_fn, nb=1, nchunk=s // _ROW_T, t=_ROW_T, vecs=[], carries=[], saved=[],
                        rows=[Row(qkv3, ddtype=BF16), Row(cs3, diff=False), Row(sn3, diff=False)],
                        douts=[_by_residue(a, w, d) for a, (_, d) in zip(dys, ATT_PATTERNS)])
    return dx


def _merge_fn(ci, b, carries, rows, vecs):
    o1, o2, o3, l1, l2, l3 = rows
    mx = lax.stop_gradient(jnp.maximum(l1, jnp.maximum(l2, l3)))
    e1, e2, e3 = jnp.exp(l1 - mx), jnp.exp(l2 - mx), jnp.exp(l3 - mx)
    return [], [(e1 * o1 + e2 * o2 + e3 * o3) / (e1 + e2 + e3)]


_ROW_T = 256


def _merge_rows(os_, ls_):
    ds = [d for _, d in ATT_PATTERNS]
    return [_by_residue(a, ATT_W, d) for a, d in zip(os_, ds)] + [_by_residue(a, ATT_W, d) for a, d in zip(ls_, ds)]


def merge_forward(name, os_, ls_, s):
    (y,), _ = scan_fwd(name, _merge_fn, nb=1, nchunk=s // _ROW_T, t=_ROW_T, rows=_merge_rows(os_, ls_), vecs=[],
                       carries=[], outs=[out_row((1, s, ATT_W), BF16)], save=False)
    return y


def merge_backward(name, os_, ls_, dmix3):
    s = dmix3.shape[1]
    drows, _ = scan_bwd(name, _merge_fn, nb=1, nchunk=s // _ROW_T, t=_ROW_T, rows=_merge_rows(os_, ls_), vecs=[],
                        carries=[], saved=[], douts=[Row(dmix3, 256, fc=lambda b: 3)])
    return drows


def _norm_mod_fn(ci, b, carries, rows, vecs):
    (x,) = rows
    g, sc, sh = vecs
    xn = x * lax.rsqrt(jnp.mean(x * x, axis=-1, keepdims=True) + NORM_EPS)
    return [], [xn * g * (1.0 + sc) + sh]


def norm_mod_forward(name, x3, g, sc, sh):
    s = x3.shape[1]
    (h,), _ = scan_fwd(name, _norm_mod_fn, nb=1, nchunk=s // _ROW_T, t=_ROW_T, rows=[Row(x3)], vecs=[Vec(g), Vec(sc), Vec(sh)],
                       carries=[], outs=[out_row(x3.shape, BF16)], save=False)
    return h


def norm_mod_backward(name, x3, g, sc, sh, dh3, add3):
    s = x3.shape[1]
    (dx,), dv = scan_bwd(name, _norm_mod_fn, nb=1, nchunk=s // _ROW_T, t=_ROW_T, rows=[Row(x3)], vecs=[Vec(g), Vec(sc), Vec(sh)],
                         carries=[], saved=[], douts=[Row(dh3)], adds={0: Row(add3)})
    return dx, dv


def _gate_fn(ci, b, carries, rows, vecs):
    return [], [rows[0] * vecs[0]]


def gate_backward(name, o3, g, dx3):
    s = o3.shape[1]
    (do,), (dg,) = scan_bwd(name, _gate_fn, nb=1, nchunk=s // _ROW_T, t=_ROW_T, rows=[Row(o3, ddtype=BF16)], vecs=[Vec(g)],
                            carries=[], saved=[], douts=[Row(dx3)])
    return do, dg


def _make_halves():
    @jax.custom_vjp
    def halves(x):
        h = x.shape[1] // 2
        return x[:, :h], x[:, h:]

    def fwd(x):
        return halves(x), None

    def bwd(_, g):
        return (jnp.concatenate(g, axis=1),)

    halves.defvjp(fwd, bwd)
    return halves


_halves = _make_halves()


def _ffn_fn(ci, b, carries, rows, vecs):
    (cu,) = carries
    (u,) = rows
    w, bias = vecs
    hg, hu = _halves(_conv(_shift8, cu, u, w, bias, 3))
    return [_tail8(u)], [_silu(hg) * hu]


_FFN_T = 256
_FFN_CW = FFN_DIM // 2
_FFN_CARRIES = [(8, 2 * _FFN_CW)]
FFN_BLOCK_ORDER = [0, 2, 1, 3]


def _ffn_io(up3, cw, cb):
    own = lambda b: b
    return [Row(up3, 2 * _FFN_CW, fc=own, ddtype=BF16)], [Vec(cw, 2 * _FFN_CW, own), Vec(cb, 2 * _FFN_CW, own)]


def ffn_mid_forward(name, up3, cw, cb):
    rows, vecs = _ffn_io(up3, cw, cb)
    s = up3.shape[1]
    (act,), saved = scan_fwd(name, _ffn_fn, nb=2, nchunk=s // _FFN_T, t=_FFN_T, rows=rows, vecs=vecs, carries=_FFN_CARRIES,
                             outs=[out_row((1, s, FFN_DIM), BF16, _FFN_CW, fc=lambda b: b)], save=True)
    return act, saved


def ffn_mid_backward(name, up3, cw, cb, saved, dact3):
    rows, vecs = _ffn_io(up3, cw, cb)
    s = up3.shape[1]
    return scan_bwd(name, _ffn_fn, nb=2, nchunk=s // _FFN_T, t=_FFN_T, rows=rows, vecs=vecs, carries=_FFN_CARRIES,
                    saved=saved, douts=[Row(dact3, _FFN_CW, fc=lambda b: b)])


def _adam_fn(ci, b, carries, rows, vecs):
    w, g, m, v = rows
    m = ADAM_B1 * m + (1.0 - ADAM_B1) * g
    v = ADAM_B2 * v + (1.0 - ADAM_B2) * (g * g)
    m_hat = m / (1.0 - ADAM_B1 ** ADAM_STEP)
    v_hat = v / (1.0 - ADAM_B2 ** ADAM_STEP)
    delta = -ADAM_LR * (m_hat / (jnp.sqrt(v_hat) + ADAM_EPS) + ADAM_WD * w)
    return [], [delta, m, v]


def adamw(name, w, g, m, v):
    shape = w.shape
    c = shape[-1]
    r = int(np.prod(shape[:-1]))
    t = _tile(r, 256, 8)
    as3 = lambda a: a.reshape(1, r, c)
    outs, _ = scan_fwd(name, _adam_fn, nb=1, nchunk=r // t, t=t, rows=[Row(as3(a)) for a in (w, g, m, v)], vecs=[], carries=[],
                       outs=[out_row((1, r, c)) for _ in range(3)], save=False)
    return [o.reshape(shape) for o in outs]


def rope_tables(positions):
    inv_freq = ROPE_THETA ** (-jnp.arange(0, ROT_DIM, 2, dtype=F32) / ROT_DIM)
    ang = positions.astype(F32)[:, None] * inv_freq
    s = positions.shape[0]
    cs = jnp.concatenate([jnp.cos(ang), jnp.cos(ang), jnp.ones((s, ATT_HEAD_DIM - ROT_DIM), F32)], axis=1)
    sn = jnp.concatenate([jnp.sin(ang), jnp.sin(ang), jnp.zeros((s, ATT_HEAD_DIM - ROT_DIM), F32)], axis=1)
    cs3 = jnp.concatenate([jnp.tile(cs, (1, 2 * ATT_HEADS)), jnp.ones((s, ATT_W), F32)], axis=1)
    sn3 = jnp.concatenate([jnp.tile(sn, (1, 2 * ATT_HEADS)), jnp.zeros((s, ATT_W), F32)], axis=1)
    return cs3[None], sn3[None]


def attention_forward(lname, qkv3, cs3, sn3):
    s = qkv3.shape[1]
    rotated = rope_forward(f"{lname}_rope", qkv3, cs3, sn3)
    os_, ls_, keep = [], [], []
    for pi, (_, d) in enumerate(ATT_PATTERNS):
        o, lse, saved = attn_forward(f"{lname}_attn{pi}", rotated[pi], d)
        os_.append(o)
        ls_.append(lse)
        keep.append(saved)
    y = merge_forward(f"{lname}_merge", os_, ls_, s)
    return y, (rotated, os_, ls_, keep)


def attention_backward(lname, qkv3, cs3, sn3, res, dmix3):
    rotated, os_, ls_, keep = res
    dm = merge_backward(f"{lname}_merge_b", os_, ls_, dmix3)
    dys = [attn_backward(f"{lname}_attn{pi}_b", rotated[pi], d, keep[pi], dm[pi], dm[3 + pi]) for pi, (_, d) in enumerate(ATT_PATTERNS)]
    return rope_backward(f"{lname}_rope_b", qkv3, cs3, sn3, dys)


def mm(name, a, b, mode, out_dtype=F32, res=None, gate=None, tm=1408, tn=1536, tk=1408, into=None):
    if mode == "nn":
        (m, k), n = a.shape, b.shape[1]
    elif mode == "nt":
        (m, k), n = a.shape, b.shape[0]
    else:
        (k, m), n = a.shape, b.shape[1]
    tm, tn, tk = _tile(m, tm), _tile(n, tn), _tile(k, tk)
    nk = k // tk
    a_spec = pl.BlockSpec((tk, tm), lambda i, j, q: (q, i)) if mode == "tn" else pl.BlockSpec((tm, tk), lambda i, j, q: (i, q))
    b_spec = pl.BlockSpec((tn, tk), lambda i, j, q: (j, q)) if mode == "nt" else pl.BlockSpec((tk, tn), lambda i, j, q: (q, j))
    o_spec = pl.BlockSpec((tm, tn), lambda i, j, q: (i, j))
    fused = res is not None
    lead = 0 if into is None else into[0].ndim - 2
    first = (0,) * lead + (slice(None), slice(None))

    def body(*refs):
        if fused:
            a_ref, b_ref, r_ref, g_ref, o_ref, o2_ref, acc = refs
        elif into is not None:
            a_ref, b_ref, _, o_ref, acc = refs
        else:
            a_ref, b_ref, o_ref, acc = refs
        q = pl.program_id(2)

        @pl.when(q == 0)
        def _():
            acc[...] = jnp.zeros(acc.shape, F32)

        acc[...] += _mxu(a_ref[...], b_ref[...], mode)

        @pl.when(q == nk - 1)
        def _():
            o_ref[first] = acc[...].astype(o_ref.dtype)
            if fused:
                o2_ref[...] = r_ref[...] + g_ref[...] * acc[...]

    ins, in_specs = [a, b], [a_spec, b_spec]
    out_shape, out_specs = [jax.ShapeDtypeStruct((m, n), out_dtype)], [o_spec]
    if fused:
        ins += [res, gate]
        in_specs += [o_spec, pl.BlockSpec((1, tn), lambda i, j, q: (0, j))]
        out_shape.append(jax.ShapeDtypeStruct((m, n), F32))
        out_specs.append(o_spec)
    aliases = {}
    if into is not None:
        buf, omap = into
        ins.append(buf)
        in_specs.append(pl.BlockSpec(memory_space=pl.ANY))
        out_shape = [jax.ShapeDtypeStruct(buf.shape, buf.dtype)]
        out_specs = [pl.BlockSpec((1,) * lead + (tm, tn), lambda i, j, q: omap(i, j))]
        aliases = {2: 0}
    out = pl.pallas_call(
        body, name=name, grid=(m // tm, n // tn, nk), in_specs=in_specs, out_specs=out_specs, out_shape=out_shape,
        scratch_shapes=[pltpu.VMEM((tm, tn), F32)], input_output_aliases=aliases,
        compiler_params=pltpu.CompilerParams(dimension_semantics=("parallel", "parallel", "arbitrary"),
                                             vmem_limit_bytes=VMEM_LIMIT_BYTES),
    )(*ins)
    return tuple(out) if fused else out[0]


def final_loss(name, x3, t3, g):
    s, d = x3.shape[1], x3.shape[2]
    t = _ROW_T

    def body(x_ref, t_ref, g_ref, loss_ref, dx_ref, dg_ref):
        i = pl.program_id(0)
        tv = t_ref[0]

        def f(x, gg):
            y = x * lax.rsqrt(jnp.mean(x * x, axis=-1, keepdims=True) + NORM_EPS) * gg
            e = y - tv
            return 0.5 * jnp.sum(jnp.mean(e * e, axis=-1, keepdims=True), axis=0, keepdims=True)

        l, vjp = jax.vjp(f, x_ref[0], g_ref[...])
        dx, dg = vjp(jnp.ones((1, 1), F32))
        dx_ref[0] = dx

        @pl.when(i == 0)
        def _():
            loss_ref[...] = jnp.zeros(loss_ref.shape, F32)
            dg_ref[...] = jnp.zeros(dg_ref.shape, F32)

        loss_ref[...] += jnp.broadcast_to(l, loss_ref.shape)
        dg_ref[...] += dg

    row = pl.BlockSpec((1, t, d), lambda i: (0, i, 0))
    vec = pl.BlockSpec((1, d), lambda i: (0, 0))
    return pl.pallas_call(
        body, name=name, grid=(s // t,), in_specs=[row, row, vec],
        out_specs=[pl.BlockSpec((8, 128), lambda i: (0, 0)), row, vec],
        out_shape=[jax.ShapeDtypeStruct((8, 128), F32), jax.ShapeDtypeStruct(x3.shape, F32), jax.ShapeDtypeStruct((1, d), F32)],
        compiler_params=pltpu.CompilerParams(dimension_semantics=("arbitrary",), vmem_limit_bytes=VMEM_LIMIT_BYTES),
    )(x3, t3, g)


_ADA_TN = 512


def ada_forward(name, c16, ada_w):
    depth, d, cols = ada_w.shape

    def body(c_ref, w_ref, o_ref):
        o_ref[0] = _mxu(_silu(c_ref[...]), w_ref[0], "nn")

    return pl.pallas_call(
        body, name=name, grid=(depth, cols // _ADA_TN),
        in_specs=[pl.BlockSpec((16, d), lambda l, j: (0, 0)), pl.BlockSpec((1, d, _ADA_TN), lambda l, j: (l, 0, j))],
        out_specs=pl.BlockSpec((1, 16, _ADA_TN), lambda l, j: (l, 0, j)),
        out_shape=jax.ShapeDtypeStruct((depth, 16, cols), F32),
        compiler_params=pltpu.CompilerParams(dimension_semantics=("arbitrary", "arbitrary"), vmem_limit_bytes=VMEM_LIMIT_BYTES),
    )(c16, ada_w)


def ada_backward(name, c16, dmod16, w, m, v):
    depth, d, cols = w.shape

    def body(c_ref, dm_ref, w_ref, m_ref, v_ref, g_ref, dl_ref, nm_ref, nv_ref):
        g = _mxu(_silu(c_ref[...]), dm_ref[0], "tn")
        _, (delta, nm, nv) = _adam_fn(None, None, [], [w_ref[0], g, m_ref[0], v_ref[0]], [])
        g_ref[0], dl_ref[0], nm_ref[0], nv_ref[0] = g, delta, nm, nv

    blk = pl.BlockSpec((1, d, _ADA_TN), lambda l, j: (l, 0, j))
    return pl.pallas_call(
        body, name=name, grid=(depth, cols // _ADA_TN),
        in_specs=[pl.BlockSpec((16, d), lambda l, j: (0, 0)), pl.BlockSpec((1, 16, _ADA_TN), lambda l, j: (l, 0, j)), blk, blk, blk],
        out_specs=[blk] * 4, out_shape=[jax.ShapeDtypeStruct(w.shape, F32)] * 4,
        compiler_params=pltpu.CompilerParams(dimension_semantics=("arbitrary", "arbitrary"), vmem_limit_bytes=VMEM_LIMIT_BYTES),
    )(c16, dmod16, w, m, v)


def _sum_fn(ci, b, carries, rows, vecs):
    acc = rows[0]
    for r in rows[1:]:
        acc = acc + r
    return [], [acc]


def sum_slots(name, a, nsum, out_dtype=F32):
    n, r, c = a.shape
    nb = n // nsum
    t = _tile(r, 256, 8)
    rows = [Row(a, fb=(lambda b, k=k: k * nb + b)) for k in range(nsum)]
    (out,), _ = scan_fwd(name, _sum_fn, nb=nb, nchunk=r // t, t=t, rows=rows, vecs=[], carries=[],
                         outs=[out_row((nb, r, c), out_dtype, fb=lambda b: b)], save=False)
    return out


def _sum_my_layer_fn(ci, b, carries, rows, vecs):
    layer0, layer1, theirs = rows
    return [], [jnp.where(lax.axis_index("c") == 0, layer0, layer1) + theirs]


def sum_cores(name, g, theirs, out_dtype):
    _, nb, r, c = g.shape
    g8 = g.reshape(2 * nb, r, c)
    t = _tile(r, 256, 8)
    rows = [Row(g8, fb=lambda b: b), Row(g8, fb=lambda b: nb + b), Row(theirs, fb=lambda b: b)]
    (out,), _ = scan_fwd(name, _sum_my_layer_fn, nb=nb, nchunk=r // t, t=t, rows=rows, vecs=[], carries=[],
                         outs=[out_row((nb, r, c), out_dtype, fb=lambda b: b)], save=False)
    return out


def _flip(mask, pos):
    return tuple((1 - p) if m else p for m, p in zip(mask, pos))


ALL_PEERS = [(a, b, c) for a in (0, 1) for b in (0, 1) for c in (0, 1)][1:]
CHIP_PEERS = [(1, 0, 0), (0, 1, 0), (1, 1, 0)]
SIBLING = [(0, 0, 1)]


def _divisor(size, target, unit):
    best = 1
    for n in range(1, target + 1):
        if size % n == 0 and (size // n) % unit == 0:
            best = n
    return best


def _pieces(src, dst, pieces):
    shape = src.shape
    unit = 16 if src.dtype == BF16 else 8
    if pieces <= 1:
        return [(src, dst)]
    if len(shape) == 2:
        n = _divisor(shape[0], pieces, unit)
        s = shape[0] // n
        return [(src.at[pl.ds(i * s, s)], dst.at[pl.ds(i * s, s)]) for i in range(n)]
    assert len(shape) == 3, shape
    n = _divisor(shape[1], max(pieces // shape[0], 1), unit)
    s = shape[1] // n
    return [(src.at[j, pl.ds(i * s, s)], dst.at[j, pl.ds(i * s, s)]) for j in range(shape[0]) for i in range(n)]


def comm_call(name, arrays, out_shapes, masks, src_fn, dst_fn, local_fn=None, pieces=1):
    na, npeer = len(arrays), len(masks)

    def body(*refs):
        ins, outs = refs[:na], refs[na:2 * na]
        send_sems, recv_sems, loc_sems = refs[2 * na:]
        me = (lax.axis_index("x"), lax.axis_index("y"), lax.axis_index("c"))
        local = []
        if local_fn is not None:
            for k in range(na):
                s, d = local_fn(k, ins[k], outs[k], me)
                for ps, pd in _pieces(s, d, pieces):
                    pltpu.make_async_copy(ps, pd, loc_sems.at[k]).start()
                local.append(pltpu.make_async_copy(s, d, loc_sems.at[k]))

        def remote(k, p, src, dst, to):
            return pltpu.make_async_remote_copy(
                src_ref=src, dst_ref=dst, send_sem=send_sems.at[k * npeer + p], recv_sem=recv_sems.at[k * npeer + p],
                device_id=to, device_id_type=MESH)

        for k in range(na):
            for p in range(npeer):
                peer = _flip(masks[p], me)
                for ps, pd in _pieces(src_fn(k, ins[k], me, peer), dst_fn(k, outs[k], me), pieces):
                    remote(k, p, ps, pd, peer).start()
        for k in range(na):
            for p in range(npeer):
                peer = _flip(masks[p], me)
                remote(k, p, src_fn(k, ins[k], me, peer), dst_fn(k, outs[k], peer), peer).wait_recv()
        for k in range(na):
            for p in range(npeer):
                peer = _flip(masks[p], me)
                remote(k, p, src_fn(k, ins[k], me, peer), dst_fn(k, outs[k], me), peer).wait_send()
        for cp in local:
            cp.wait()

    hbm = pl.BlockSpec(memory_space=pl.ANY)
    out = pl.pallas_call(
        body, name=name, in_specs=[hbm] * na, out_specs=[hbm] * na,
        out_shape=[jax.ShapeDtypeStruct(s, a.dtype) for s, a in zip(out_shapes, arrays)],
        scratch_shapes=[pltpu.SemaphoreType.DMA((na * npeer,)), pltpu.SemaphoreType.DMA((na * npeer,)),
                        pltpu.SemaphoreType.DMA((na,))],
    )(*arrays)
    return list(out)


def _dev(pos):
    return 4 * pos[0] + 2 * pos[1] + pos[2]


def _chip(pos):
    return 2 * pos[0] + pos[1]


def allgather8(name, a):
    (out,) = comm_call(name, [a], [(8,) + a.shape], ALL_PEERS,
                       src_fn=lambda k, r, me, peer: r, dst_fn=lambda k, o, sender: o.at[_dev(sender)],
                       local_fn=lambda k, r, o, me: (r, o.at[_dev(me)]))
    return out


def gather_layer_from_chips(name, arrays):
    return comm_call(name, arrays, [(4,) + a.shape[1:] for a in arrays], CHIP_PEERS,
                     src_fn=lambda k, r, me, peer: r.at[me[2]], dst_fn=lambda k, o, sender: o.at[_chip(sender)],
                     local_fn=lambda k, r, o, me: (r.at[me[2]], o.at[_chip(me)]), pieces=8)


def swap_layers(name, arrays, c):
    got = comm_call(name, arrays, [a.shape for a in arrays], SIBLING,
                    src_fn=lambda k, r, me, peer: r, dst_fn=lambda k, o, sender: o, pieces=32)
    return [[jnp.where(c == 0, a, g), jnp.where(c == 0, g, a)] for a, g in zip(arrays, got)]


def swap_other_layer(name, arrays):
    return comm_call(name, arrays, [a.shape[1:] for a in arrays], SIBLING,
                     src_fn=lambda k, r, me, peer: r.at[peer[2]], dst_fn=lambda k, o, sender: o, pieces=32)


def scatter_to_chips(name, arrays):
    return comm_call(name, arrays, [a.shape for a in arrays], CHIP_PEERS,
                     src_fn=lambda k, r, me, peer: r.at[_chip(peer)], dst_fn=lambda k, o, sender: o.at[_chip(sender)],
                     local_fn=lambda k, r, o, me: (r.at[_chip(me)], o.at[_chip(me)]), pieces=8)


def _rows_of(shape):
    return -(-int(np.prod(shape)) // 1024) * 8


def _pack(arrs):
    parts = []
    for a in arrs:
        flat = a.reshape(-1).astype(F32)
        parts.append(jnp.pad(flat, (0, _rows_of(a.shape) * 128 - flat.shape[0])).reshape(-1, 128))
    rows = sum(p.shape[0] for p in parts)
    parts.append(jnp.zeros(((-rows) % _ROW_T, 128), F32))
    return jnp.concatenate(parts, axis=0)


def _unpack(buf, shapes):
    out, o = [], 0
    for s in shapes:
        r, n = _rows_of(s), int(np.prod(s))
        out.append(buf[o:o + r].reshape(-1)[:n].reshape(s))
        o += r
    return out


_WEIGHTS = ["ada_w", "ada_b", "norm1_g", "w_in", "ssd_conv_w", "ssd_conv_b", "ssd_dt_bias", "ssd_a_log", "ssd_d", "ssd_norm_g",
            "pool_w", "pool_scale", "w_out", "norm2_g", "ffn_up", "ffn_conv_w", "ffn_conv_b", "ffn_down", "final_g"]
_BIG = ["w_in", "w_out", "ffn_up", "ffn_down"]
_SMALL = [n for n in _WEIGHTS if n not in _BIG and n != "ada_w"]
_COL_SHARDED_SMALL = {"ssd_conv_w": 256, "ffn_conv_w": 1408}


def _pad_lanes(v, n=128):
    return jnp.pad(v.astype(F32), (0, n - v.shape[0]))[None]


def _perm_cols(w):
    pad = jnp.zeros(w.shape[:-1] + (IN_WP - IN_W,), w.dtype)
    return jnp.concatenate([w[..., :1536], w[..., 1544:1800], w[..., 1536:1544], pad, w[..., 1800:]], axis=-1)


def _unperm_cols(g):
    return jnp.concatenate([g[..., :1536], g[..., 1792:1800], g[..., 1536:1792], g[..., IN_MAIN:]], axis=-1)


_CHIP2_PARTS = [(1284, 1536), (1792, 1800), (1536, 1792), (IN_MAIN, IN_MAIN + 126)]


def _w_in_chip_cols(gp):
    q = IN_W // 4
    return [gp[:, :q], gp[:, q:2 * q], jnp.concatenate([gp[:, a:b] for a, b in _CHIP2_PARTS], axis=1), gp[:, IN_WP - q:]]


def _w_in_from_chips(a):
    c2 = a[2]
    pad = jnp.zeros((a.shape[1], IN_WP - IN_W), a.dtype)
    return jnp.concatenate([a[0], a[1], c2[:, :252], c2[:, 260:516], c2[:, 252:260], pad, c2[:, 516:], a[3]], axis=1)


def _ffn_block_perm(a):
    n = a.shape[-1] // 4
    return jnp.concatenate([a[..., j * n:(j + 1) * n] for j in FFN_BLOCK_ORDER], axis=-1)


def _layer_forward(i, x3, modv, wts, sp, cs3, sn3):
    sh1, sc1, g1, sh2, sc2, g2 = modv
    big = lambda n: wts[n]() if callable(wts[n]) else wts[n]
    h1 = norm_mod_forward(f"l{i}_norm1", x3, wts["norm1_g"], sc1, sh1)
    proj3 = mm(f"l{i}_proj", h1[0], big("w_in")[:, :IN_MAIN], "nn")[None]
    qkv3 = mm(f"l{i}_qkv", h1[0], big("w_in")[:, IN_MAIN:], "nn")[None]
    y_ssd, sv_ssd = ssd_forward(f"l{i}_ssd", proj3, sp)
    y_pool, sv_pool = pool_forward(f"l{i}_pool", proj3, wts["wbd"], wts["pool_scale"])
    y_att, res_att = attention_forward(f"l{i}", qkv3, cs3, sn3)
    mix = jnp.concatenate([y_ssd, y_pool, y_att], axis=-1)
    out, x1 = mm(f"l{i}_wout", mix[0], big("w_out"), "nn", res=x3[0], gate=g1)
    x1 = x1[None]
    h2 = norm_mod_forward(f"l{i}_norm2", x1, wts["norm2_g"], sc2, sh2)
    up3 = mm(f"l{i}_up", h2[0], big("ffn_up"), "nn")[None]
    act, sv_ffn = ffn_mid_forward(f"l{i}_ffn", up3, wts["ffn_conv_w"], wts["ffn_conv_b"])
    dn, x2 = mm(f"l{i}_down", act[0], big("ffn_down"), "nn", res=x1[0], gate=g2)
    keep = dict(x=x3, h1=h1, proj3=proj3, qkv3=qkv3, sv_ssd=sv_ssd, sv_pool=sv_pool, res_att=res_att, mix=mix, out=out[None],
                x1=x1, h2=h2, up3=up3, act=act, sv_ffn=sv_ffn, dn=dn[None])
    return x2[None], keep


def _layer_backward(i, dx2, keep, modv, wts, sp, cs3, sn3, after=None):
    sh1, sc1, g1, sh2, sc2, g2 = modv
    k = keep
    big = lambda n: wts[n]() if callable(wts[n]) else wts[n]
    tell = lambda step, *a: after[step](*a) if after and step in after else None
    d_dn, d_g2 = gate_backward(f"l{i}_gate2_b", k["dn"], g2, dx2)
    d_act = mm(f"l{i}_down_bx", d_dn[0], big("ffn_down"), "nt")
    g_down = mm(f"l{i}_down_bw", k["act"][0], d_dn[0], "tn").reshape(4, FFN_DIM // 4, D_MODEL)
    (d_up,), dv_ffn = ffn_mid_backward(f"l{i}_ffn_b", k["up3"], wts["ffn_conv_w"], wts["ffn_conv_b"], k["sv_ffn"], d_act[None])
    tell("ffn_b")
    d_h2 = mm(f"l{i}_up_bx", d_up[0], big("ffn_up"), "nt")
    g_up = mm(f"l{i}_up_bw", k["h2"][0], d_up[0], "tn", tn=_FFN_CW,
              into=((4, D_MODEL, _FFN_CW), lambda r, c: ((c % 2) * 2 + c // 2, r, 0)))
    dx1, (d_n2, d_sc2, d_sh2) = norm_mod_backward(f"l{i}_norm2_b", k["x1"], wts["norm2_g"], sc2, sh2, d_h2[None], dx2)
    d_out, d_g1 = gate_backward(f"l{i}_gate1_b", k["out"], g1, dx1)
    d_mix = mm(f"l{i}_wout_bx", d_out[0], big("w_out"), "nt")[None]
    g_wout = mm(f"l{i}_wout_bw", k["mix"][0], d_out[0], "tn").reshape(4, D_MODEL // 4, D_MODEL)
    tell("wout_bw", g_wout, g_up, g_down)
    (dz, dxs, dbm, dcm, ddt), dv_ssd = ssd_backward(f"l{i}_ssd_b", k["proj3"], sp, k["sv_ssd"], d_mix)
    tell("ssd_b")
    (du_pool,), (d_wbd, d_pscale) = pool_backward(f"l{i}_pool_b", k["proj3"], wts["wbd"], wts["pool_scale"], k["sv_pool"], d_mix)
    d_qkv = attention_backward(f"l{i}", k["qkv3"], cs3, sn3, k["res_att"], d_mix)
    d_proj = jnp.concatenate([dz[0], dxs[0], dbm[0], dcm[0], du_pool[0], (ddt[0] + ddt[1]).astype(BF16), d_qkv[0]], axis=-1)
    g_win = jnp.stack(_w_in_chip_cols(mm(f"l{i}_proj_bw", k["h1"][0], d_proj, "tn")))
    tell("proj_bw", g_win)
    d_h1 = mm(f"l{i}_proj_bx", d_proj, big("w_in"), "nt")
    tell("proj_bx")
    dx, (d_n1, d_sc1, d_sh1) = norm_mod_backward(f"l{i}_norm1_b", k["x"], wts["norm1_g"], sc1, sh1, d_h1[None], dx1)
    dcwx, dcbx, dcwb, dcbb, dcwc, dcbc, ddtb, dalog, ddsk, dng = dv_ssd
    small = dict(
        norm1_g=d_n1[0], norm2_g=d_n2[0],
        ssd_conv_w=jnp.concatenate([dcwx[:, :512], dcwb[:, 512:768], dcwc[:, 768:]], axis=1),
        ssd_conv_b=jnp.concatenate([dcbx[0, :512], dcbb[0, 512:768], dcbc[0, 768:]]),
        ssd_dt_bias=ddtb[0, :8], ssd_a_log=dalog[0, :8], ssd_d=ddsk[0, :8], ssd_norm_g=dng[0],
        pool_w=jnp.stack([d_wbd[64 * g:64 * g + 64, 64 * g:64 * g + 64] for g in range(4)]), pool_scale=d_pscale[0],
        ffn_conv_w=_ffn_block_perm(dv_ffn[0]), ffn_conv_b=_ffn_block_perm(dv_ffn[1][0]),
    )
    dmod = jnp.concatenate([d_sh1[0], d_sc1[0], d_g1[0], d_sh2[0], d_sc2[0], d_g2[0]])
    return dx, [g_win, g_wout, g_up, g_down], small, dmod


def kernel(x, c, positions, ada_w, ada_b, norm1_g, w_in, ssd_conv_w, ssd_conv_b, ssd_dt_bias, ssd_a_log, ssd_d, ssd_norm_g, pool_w, pool_scale, w_out, norm2_g, ffn_up, ffn_conv_w, ffn_conv_b, ffn_down, final_g, loss_target, m_ada_w, m_ada_b, m_norm1_g, m_w_in, m_ssd_conv_w, m_ssd_conv_b, m_ssd_dt_bias, m_ssd_a_log, m_ssd_d, m_ssd_norm_g, m_pool_w, m_pool_scale, m_w_out, m_norm2_g, m_ffn_up, m_ffn_conv_w, m_ffn_conv_b, m_ffn_down, m_final_g, v_ada_w, v_ada_b, v_norm1_g, v_w_in, v_ssd_conv_w, v_ssd_conv_b, v_ssd_dt_bias, v_ssd_a_log, v_ssd_d, v_ssd_norm_g, v_pool_w, v_pool_scale, v_w_out, v_norm2_g, v_ffn_up, v_ffn_conv_w, v_ffn_conv_b, v_ffn_down, v_final_g):
    args = dict(locals())
    w = {n: args[n] for n in _WEIGHTS}
    m = {n: args["m_" + n] for n in _WEIGHTS}
    v = {n: args["v_" + n] for n in _WEIGHTS}
    d = D_MODEL
    me = (lax.axis_index("x"), lax.axis_index("y"), lax.axis_index("c"))
    chip, dev = _chip(me), _dev(me)
    RIDERS.reset()

    shapes0 = [c.shape, ssd_conv_w.shape, ffn_conv_w.shape]
    g0 = allgather8("gather_c_conv", _pack([c, ssd_conv_w, ffn_conv_w]))
    c16 = jnp.pad(g0[:, :d // 128, :].reshape(8, d), ((0, 8), (0, 0)))
    by_chip = [_unpack(g0[2 * j], shapes0) for j in range(4)]
    conv_w_full = jnp.concatenate([p[1] for p in by_chip], axis=-1)
    fconv_w_full = jnp.concatenate([p[2] for p in by_chip], axis=-1)

    modp = ada_forward("ada_fwd", c16, ada_w)[:, :8]
    g1 = allgather8("gather_mod", _pack([modp]))
    modfull = jnp.concatenate([_unpack(g1[2 * j], [modp.shape])[0] for j in range(4)], axis=-1)
    mod = lax.dynamic_index_in_dim(modfull, dev, axis=1, keepdims=False) + ada_b
    modv = [[mod[i, q * d:(q + 1) * d][None] for q in range(6)] for i in range(DEPTH)]

    shards = [w[n].astype(BF16) for n in _BIG]

    def weight(k, layer, got):
        parts = [jnp.where(chip == j, shards[k][layer], got[j]) for j in range(4)]
        if k == 0:
            return _w_in_from_chips(parts)
        return jnp.concatenate([parts[j] for j in FFN_BLOCK_ORDER], axis=1) if k == 2 else jnp.concatenate(parts, axis=0)

    def later(k, layer, *sources):
        made = []

        def get():
            if not made:
                got = [RIDERS.result(host)[pos] for host, pos in sources]
                made.append(weight(k, layer, got[0] if len(got) == 1 else jnp.concatenate(got, axis=1)))
            return made[0]
        return get

    cs3, sn3 = rope_tables(positions[0])
    eye4 = jnp.eye(4, dtype=F32)
    wts, sps = [], []
    for i in range(DEPTH):
        wts.append(dict(
            norm1_g=norm1_g[i][None], norm2_g=norm2_g[i][None], pool_scale=pool_scale[i][None],
            wbd=(eye4[:, None, :, None] * pool_w[i][:, :, None, :]).reshape(POOL_W, POOL_W),
            ffn_conv_w=_ffn_block_perm(fconv_w_full[i]), ffn_conv_b=_ffn_block_perm(ffn_conv_b[i])[None]))
        sps.append(dict(cw=conv_w_full[i], cb=ssd_conv_b[i][None], dtb=_pad_lanes(ssd_dt_bias[i]), alog=_pad_lanes(ssd_a_log[i]),
                        dsk=_pad_lanes(ssd_d[i]), ng=ssd_norm_g[i][None]))

    (w_in0,) = ride_alone("gather_w_in0", gather_ride(0, [shards[0]]))
    RIDERS.book("l0_ssd", gather_ride(0, [shards[1], shards[3]]))
    half = shards[2].shape[1] // 2
    RIDERS.book("l0_attn0", gather_ride(0, [shards[2][:, :half]]))
    RIDERS.book("l0_attn1", gather_ride(0, [shards[2][:, half:]]))
    wts[0].update(w_in=weight(0, 0, w_in0), w_out=later(1, 0, ("l0_ssd", 0)), ffn_down=later(3, 0, ("l0_ssd", 1)),
                  ffn_up=later(2, 0, ("l0_attn0", 0), ("l0_attn1", 0)))
    RIDERS.book("l0_attn2", gather_ride(1, [shards[0], shards[1]]))
    RIDERS.book("l0_ffn", gather_ride(1, [shards[2]]))
    RIDERS.book("l0_down", gather_ride(1, [shards[3]]))
    wts[1].update(w_in=later(0, 1, ("l0_attn2", 0)), w_out=later(1, 1, ("l0_attn2", 1)), ffn_up=later(2, 1, ("l0_ffn", 0)),
                  ffn_down=later(3, 1, ("l0_down", 0)))
    x1_, keep0 = _layer_forward(0, x, modv[0], wts[0], sps[0], cs3, sn3)
    xc, keep1 = _layer_forward(1, x1_, modv[1], wts[1], sps[1], cs3, sn3)
    keeps = [keep0, keep1]
    lossblk, dx, d_final = final_loss("final_loss", xc, loss_target, final_g[None])
    loss = lax.psum(lossblk[0, 0], ("x", "y", "c"))

    small_g, dmods = [None] * DEPTH, [None] * DEPTH
    part_sum, from_chips = [[None] * 4 for _ in range(DEPTH)], [[None] * 4 for _ in range(DEPTH)]

    def owner_sum(layer, ks, mine, theirs):
        for k, g, t in zip(ks, mine, theirs):
            part_sum[layer][k] = add_arrays(f"sum_cores{layer}_{_BIG[k]}", [g, t], BF16)

    dx, by_chip1, small_g[1], dmods[1] = _layer_backward(1, dx, keeps[1], modv[1], wts[1], sps[1], cs3, sn3)
    RIDERS.book("l0_ffn_b", to_owner_ride(1, by_chip1))

    def after_ffn_b():
        owner_sum(1, range(4), by_chip1, RIDERS.result("l0_ffn_b"))
        RIDERS.book("l0_up_bx", scatter_ride(1, [part_sum[1][2]]))
        RIDERS.book("l0_up_bw", scatter_ride(1, [part_sum[1][0], part_sum[1][1]]))
        RIDERS.book("l0_norm2_b", scatter_ride(1, [part_sum[1][3]]))

    early = []

    def after_wout_bw(g_wout, g_up, g_down):
        early.extend([g_wout, g_up, g_down])
        RIDERS.book("l0_ssd_b", to_owner_ride(0, early))

    def after_ssd_b():
        owner_sum(0, [1, 2, 3], early, RIDERS.result("l0_ssd_b"))
        for host, k in (("l0_attn0_b", 2), ("l0_attn1_b", 3), ("l0_attn2_b", 1)):
            RIDERS.book(host, scatter_ride(0, [part_sum[0][k]]))

    last = []

    def after_proj_bw(g_win):
        last.append(g_win)
        RIDERS.book("l0_proj_bx", to_owner_ride(0, last))

    def after_proj_bx():
        owner_sum(0, [0], last, RIDERS.result("l0_proj_bx"))
        RIDERS.book("l0_norm1_b", scatter_ride(0, [part_sum[0][0]]))

    hooks = dict(ffn_b=after_ffn_b, wout_bw=after_wout_bw, ssd_b=after_ssd_b, proj_bw=after_proj_bw, proj_bx=after_proj_bx)
    dx, _, small_g[0], dmods[0] = _layer_backward(0, dx, keeps[0], modv[0], wts[0], sps[0], cs3, sn3, after=hooks)
    from_chips[1][2], (from_chips[1][0], from_chips[1][1]) = RIDERS.result("l0_up_bx")[0], RIDERS.result("l0_up_bw")
    from_chips[1][3] = RIDERS.result("l0_norm2_b")[0]
    for host, k in (("l0_attn0_b", 2), ("l0_attn1_b", 3), ("l0_attn2_b", 1), ("l0_norm1_b", 0)):
        from_chips[0][k] = RIDERS.result(host)[0]
    mine = [sum_chips_mine(f"sum_chips_{n}", part_sum[0][k], from_chips[0][k], part_sum[1][k], from_chips[1][k])
            for k, n in enumerate(_BIG)]
    reduced = swap_layers("swap_r", mine, me[2])
    grads = {n: jnp.stack(r) for n, r in zip(_BIG, reduced)}

    part = dict(ada_b=jnp.stack(dmods), final_g=d_final[0])
    for n in _SMALL:
        if n not in part:
            part[n] = jnp.stack([small_g[i][n] for i in range(DEPTH)])
    full_shapes = [part[n].shape for n in _SMALL]
    gs = allgather8("gather_small", _pack([part[n] for n in _SMALL]))
    tot = _unpack(sum_slots("sum_small", gs, 8)[0], full_shapes)
    small_tot = dict(zip(_SMALL, tot))
    dmod_all = gs[:, :DEPTH * 6 * d // 128, :].reshape(8, DEPTH, 6 * d)
    for n, ncol in _COL_SHARDED_SMALL.items():
        small_tot[n] = lax.dynamic_slice_in_dim(small_tot[n], chip * ncol, ncol, axis=2)
    grads.update(small_tot)

    ncol = ada_w.shape[2]
    dm = lax.dynamic_slice_in_dim(dmod_all, chip * ncol, ncol, axis=2).transpose(1, 0, 2)
    upd = {}
    g_ada, *upd["ada_w"] = ada_backward("ada_bwd", c16, jnp.pad(dm, ((0, 0), (0, 8), (0, 0))), ada_w, m["ada_w"], v["ada_w"])
    grads["ada_w"] = g_ada

    for n in _BIG:
        upd[n] = adamw(f"adam_{n}", w[n], grads[n], m[n], v[n])
    shapes_s = [w[n].shape for n in _SMALL]
    packed = [_pack([src[n] for n in _SMALL]) for src in (w, grads, m, v)]
    outs_s = [_unpack(o, shapes_s) for o in adamw("adam_small", *packed)]
    for q, n in enumerate(_SMALL):
        upd[n] = [outs_s[0][q], outs_s[1][q], outs_s[2][q]]

    return (loss, dx, *[grads[n] for n in _WEIGHTS], *[upd[n][0] for n in _WEIGHTS], *[upd[n][1] for n in _WEIGHTS],
            *[upd[n][2] for n in _WEIGHTS])


class Ride:
    def __init__(self, ins, out_shapes, nsem, start, finish):
        self.ins, self.out_shapes, self.nsem, self.start, self.finish = ins, out_shapes, nsem, start, finish

    def specs(self):
        hbm = pl.BlockSpec(memory_space=pl.ANY)
        return [hbm] * len(self.ins), [hbm] * len(self.out_shapes), [pltpu.SemaphoreType.DMA((self.nsem,))] * 2

    def begin(self, in_refs, out_refs, sems, cond=None):
        me = (lax.axis_index("x"), lax.axis_index("y"), lax.axis_index("c"))
        go = lambda: self.start(in_refs, out_refs, sems[0], sems[1], me)
        go() if cond is None else pl.when(cond)(go)

    def end(self, in_refs, out_refs, sems, cond=None):
        me = (lax.axis_index("x"), lax.axis_index("y"), lax.axis_index("c"))
        go = lambda: self.finish(in_refs, out_refs, sems[0], sems[1], me)
        go() if cond is None else pl.when(cond)(go)


def ride_alone(name, ride):
    ni, no = len(ride.ins), len(ride.out_shapes)

    def body(*refs):
        ride.begin(refs[:ni], refs[ni:ni + no], refs[ni + no:])
        ride.end(refs[:ni], refs[ni:ni + no], refs[ni + no:])

    in_specs, out_specs, scratch = ride.specs()
    return list(pl.pallas_call(body, name=name, in_specs=in_specs, out_specs=out_specs, out_shape=ride.out_shapes,
                               scratch_shapes=scratch)(*ride.ins))


def mm(name, a, b, mode, out_dtype=F32, res=None, gate=None, tm=1408, tn=1536, tk=1408, into=None):
    ride = RIDERS.take(name)
    if mode == "nn":
        (m, k), n = a.shape, b.shape[1]
    elif mode == "nt":
        (m, k), n = a.shape, b.shape[0]
    else:
        (k, m), n = a.shape, b.shape[1]
    tm, tn, tk = _tile(m, tm), _tile(n, tn), _tile(k, tk)
    ni, nj, nk = m // tm, n // tn, k // tk
    a_spec = pl.BlockSpec((tk, tm), lambda i, j, q: (q, i)) if mode == "tn" else pl.BlockSpec((tm, tk), lambda i, j, q: (i, q))
    b_spec = pl.BlockSpec((tn, tk), lambda i, j, q: (j, q)) if mode == "nt" else pl.BlockSpec((tk, tn), lambda i, j, q: (q, j))
    o_spec = pl.BlockSpec((tm, tn), lambda i, j, q: (i, j))
    fused = res is not None
    lead = 0 if into is None else len(into[0]) - 2
    first = (0,) * lead + (slice(None), slice(None))
    ins, in_specs = [a, b], [a_spec, b_spec]
    out_shape, out_specs = [jax.ShapeDtypeStruct((m, n), out_dtype)], [o_spec]
    if fused:
        ins += [res, gate]
        in_specs += [o_spec, pl.BlockSpec((1, tn), lambda i, j, q: (0, j))]
        out_shape.append(jax.ShapeDtypeStruct((m, n), F32))
        out_specs.append(o_spec)
    if into is not None:
        shape, omap = into
        out_shape = [jax.ShapeDtypeStruct(shape, out_dtype)]
        out_specs = [pl.BlockSpec((1,) * lead + (tm, tn), lambda i, j, q: omap(i, j))]
    n_in, n_out = len(ins), len(out_shape)
    scratch = [pltpu.VMEM((tm, tn), F32)]
    if ride is not None:
        r_in, r_out, r_scr = ride.specs()
        ins, in_specs = ins + list(ride.ins), in_specs + r_in
        out_shape, out_specs = out_shape + list(ride.out_shapes), out_specs + r_out
        scratch = scratch + r_scr

    def body(*refs):
        a_ref, b_ref = refs[:2]
        o_ref = refs[len(ins)]
        acc = refs[len(ins) + len(out_shape)]
        i, j, q = pl.program_id(0), pl.program_id(1), pl.program_id(2)
        at = lambda x, y, z: jnp.logical_and(jnp.logical_and(i == x, j == y), q == z)
        r_refs = (refs[n_in:len(ins)], refs[len(ins) + n_out:len(ins) + len(out_shape)], refs[len(ins) + len(out_shape) + 1:])
        if ride is not None:
            ride.begin(*r_refs, at(0, 0, 0))

        @pl.when(q == 0)
        def _():
            acc[...] = jnp.zeros(acc.shape, F32)

        acc[...] += _mxu(a_ref[...], b_ref[...], mode)

        @pl.when(q == nk - 1)
        def _():
            o_ref[first] = acc[...].astype(o_ref.dtype)
            if fused:
                refs[len(ins) + 1][...] = refs[2][...] + refs[3][...] * acc[...]

        if ride is not None:
            ride.end(*r_refs, at(ni - 1, nj - 1, nk - 1))

    sem = ("arbitrary",) * 3 if ride is not None else ("parallel", "parallel", "arbitrary")
    out = pl.pallas_call(
        body, name=name, grid=(ni, nj, nk), in_specs=in_specs, out_specs=out_specs, out_shape=out_shape, scratch_shapes=scratch,
        compiler_params=pltpu.CompilerParams(dimension_semantics=sem, vmem_limit_bytes=VMEM_LIMIT_BYTES),
    )(*ins)
    if ride is not None:
        RIDERS.done[name] = list(out[n_out:])
    return tuple(out[:n_out]) if fused else out[0]


def add_arrays(name, arrs, out_dtype=F32):
    nb, r, c = arrs[0].shape
    t = _tile(r, 256, 8)
    (out,), _ = scan_fwd(name, _sum_fn, nb=nb, nchunk=r // t, t=t, rows=[Row(a, fb=lambda b: b) for a in arrs], vecs=[], carries=[],
                         outs=[out_row((nb, r, c), out_dtype, fb=lambda b: b)], save=False)
    return out


def _sum_chips_mine_fn(ci, b, carries, rows, vecs):
    mine_layer = lax.axis_index("c")
    chip = 2 * lax.axis_index("x") + lax.axis_index("y")
    tot = None
    for j in range(4):
        own = jnp.where(mine_layer == 0, rows[j], rows[8 + j])
        sent = jnp.where(mine_layer == 0, rows[4 + j], rows[12 + j])
        term = jnp.where(chip == j, own, sent)
        tot = term if tot is None else tot + term
    return [], [tot]


def sum_chips_mine(name, p0, q0, p1, q1):
    _, r, c = p0.shape
    t = _tile(r, 256, 8)
    rows = [Row(a, fb=(lambda b, j=j: j)) for a in (p0, q0, p1, q1) for j in range(4)]
    (out,), _ = scan_fwd(name, _sum_chips_mine_fn, nb=1, nchunk=r // t, t=t, rows=rows, vecs=[], carries=[],
                         outs=[out_row((1, r, c))], save=False)
    return out[0]


def _remote(src, dst, send_sems, recv_sems, k, to):
    return pltpu.make_async_remote_copy(src_ref=src, dst_ref=dst, send_sem=send_sems.at[k], recv_sem=recv_sems.at[k],
                                        device_id=to, device_id_type=MESH)


def gather_ride(layer, shards):
    na = len(shards)

    def start(ins, outs, ss, rs, me):
        @pl.when(me[2] == layer)
        def _():
            for k in range(na):
                for p, mask in enumerate(CHIP_PEERS):
                    _remote(ins[k].at[layer], outs[k].at[_chip(me)], ss, rs, 6 * k + p, _flip(mask, me)).start()

    def finish(ins, outs, ss, rs, me):
        sibling = _flip(SIBLING[0], me)

        @pl.when(me[2] == layer)
        def _():
            for k in range(na):
                for p, mask in enumerate(CHIP_PEERS):
                    slot = outs[k].at[_chip(_flip(mask, me))]
                    _remote(ins[k].at[layer], slot, ss, rs, 6 * k + p, _flip(mask, me)).wait_recv()
                    _remote(slot, slot, ss, rs, 6 * k + 3 + p, sibling).start()
            for k in range(na):
                for p, mask in enumerate(CHIP_PEERS):
                    slot = outs[k].at[_chip(_flip(mask, me))]
                    _remote(ins[k].at[layer], slot, ss, rs, 6 * k + p, _flip(mask, me)).wait_send()
                    _remote(slot, slot, ss, rs, 6 * k + 3 + p, sibling).wait_send()

        @pl.when(me[2] != layer)
        def _():
            for k in range(na):
                for p, mask in enumerate(CHIP_PEERS):
                    slot = outs[k].at[_chip(_flip(mask, me))]
                    _remote(slot, slot, ss, rs, 6 * k + 3 + p, sibling).wait_recv()

    return Ride(list(shards), [jax.ShapeDtypeStruct((4,) + a.shape[1:], a.dtype) for a in shards], 6 * na, start, finish)


def scatter_ride(layer, parts):
    na = len(parts)

    def start(ins, outs, ss, rs, me):
        @pl.when(me[2] == layer)
        def _():
            for k in range(na):
                for p, mask in enumerate(CHIP_PEERS):
                    peer = _flip(mask, me)
                    _remote(ins[k].at[_chip(peer)], outs[k].at[_chip(me)], ss, rs, 3 * k + p, peer).start()

    def finish(ins, outs, ss, rs, me):
        @pl.when(me[2] == layer)
        def _():
            for k in range(na):
                for p, mask in enumerate(CHIP_PEERS):
                    peer = _flip(mask, me)
                    _remote(ins[k].at[_chip(peer)], outs[k].at[_chip(peer)], ss, rs, 3 * k + p, peer).wait_recv()
                    _remote(ins[k].at[_chip(peer)], outs[k].at[_chip(me)], ss, rs, 3 * k + p, peer).wait_send()

    return Ride(list(parts), [jax.ShapeDtypeStruct(a.shape, a.dtype) for a in parts], 3 * na, start, finish)


def to_owner_ride(layer, arrays):
    na = len(arrays)

    def start(ins, outs, ss, rs, me):
        @pl.when(me[2] != layer)
        def _():
            for k in range(na):
                _remote(ins[k], outs[k], ss, rs, k, _flip(SIBLING[0], me)).start()

    def finish(ins, outs, ss, rs, me):
        for k in range(na):
            cp = _remote(ins[k], outs[k], ss, rs, k, _flip(SIBLING[0], me))
            pl.when(me[2] != layer)(cp.wait_send)
            pl.when(me[2] == layer)(cp.wait_recv)

    return Ride(list(arrays), [jax.ShapeDtypeStruct(a.shape, a.dtype) for a in arrays], na, start, finish)


def _w_in_from_chips(a):
    c2 = a[2]
    pad = jnp.zeros((c2.shape[0], IN_WP - IN_W), c2.dtype)
    return jnp.concatenate([a[0], a[1], c2[:, :252], c2[:, 260:516], c2[:, 252:260], pad, c2[:, 516:], a[3]], axis=1)


def _mm_host(rides, rode, key, *args, **kw):
    if rides is None or key not in rides:
        return mm(*args, **kw)
    main, rode[key] = mm(*args, ride=rides[key], **kw)
    return main
```

```python
import functools
import math

import numpy as np
import jax
import jax.numpy as jnp
from jax import lax
from jax.experimental import pallas as pl
from jax.experimental.pallas import tpu as pltpu

F32 = jnp.float32
BF16 = jnp.bfloat16
HI = lax.Precision.HIGHEST
MESH = pl.DeviceIdType.MESH

D_MODEL = 1024
SEQ = 4096
DEPTH = 2
SSD_INNER = 512
SSD_HEADS = 8
SSD_STATE = 128
POOL_W = 256
POOL_WINDOWS = (2, 4, 8, 16)
ATT_W = 256
ATT_HEADS = 4
ATT_HEAD_DIM = 64
ATT_PATTERNS = ((128, 1), (512, 4), (2048, 16))
ATT_BLOCK = 128
ROT_DIM = 16
ROPE_THETA = 500000.0
IN_W = 2568
IN_WP = 2688
IN_MAIN = 1920
FFN_DIM = 2816
NORM_EPS = 1e-6
ADAM_LR, ADAM_B1, ADAM_B2, ADAM_EPS, ADAM_WD, ADAM_STEP = 0.001, 0.9, 0.999, 1e-08, 0.01, 10

VMEM_LIMIT_BYTES = 56 * 1024 * 1024
NEG = -1e30


def _mxu(a, b, mode):
    dims = {"nn": ((1,), (0,)), "nt": ((1,), (1,)), "tn": ((0,), (0,))}[mode]
    return lax.dot_general(a.astype(BF16), b.astype(BF16), (dims, ((), ())), preferred_element_type=F32)


@functools.partial(jax.custom_vjp, nondiff_argnums=(2,))
def _bdot(a, b, mode):
    return _mxu(a, b, mode)


def _bdot_fwd(a, b, mode):
    return _mxu(a, b, mode), (a, b)


def _bdot_bwd(mode, res, g):
    a, b = res
    if mode == "nn":
        return _mxu(g, b, "nt"), _mxu(a, g, "tn")
    if mode == "nt":
        return _mxu(g, b, "nn"), _mxu(g, a, "tn")
    return _mxu(b, g, "nt"), _mxu(a, g, "nn")


_bdot.defvjp(_bdot_fwd, _bdot_bwd)


def _fxu(a, b, mode):
    dims = {"nn": ((1,), (0,)), "nt": ((1,), (1,)), "tn": ((0,), (0,))}[mode]
    return lax.dot_general(a, b, (dims, ((), ())), precision=HI, preferred_element_type=F32)


@functools.partial(jax.custom_vjp, nondiff_argnums=(2,))
def _fdot(a, b, mode):
    return _fxu(a, b, mode)


def _fdot_fwd(a, b, mode):
    return _fxu(a, b, mode), (a, b)


def _fdot_bwd(mode, res, g):
    a, b = res
    if mode == "nn":
        return _fxu(g, b, "nt"), _fxu(a, g, "tn")
    if mode == "nt":
        return _fxu(g, b, "nn"), _fxu(g, a, "tn")
    return _fxu(b, g, "nt"), _fxu(a, g, "nn")


_fdot.defvjp(_fdot_fwd, _fdot_bwd)


def _iota(shape, dim):
    return lax.broadcasted_iota(jnp.int32, shape, dim)


def _make_shift(h):
    @functools.partial(jax.custom_vjp, nondiff_argnums=(2,))
    def shift(halo, cur, k):
        if k == 0:
            return cur
        full = jnp.concatenate([halo, cur], axis=0)
        return pltpu.roll(full, k, 0)[h:]

    def fwd(halo, cur, k):
        return shift(halo, cur, k), None

    def bwd(k, _, g):
        t, w = g.shape
        if k == 0:
            return jnp.zeros((h, w), F32), g
        d_cur = jnp.where(_iota((t, w), 0) < t - k, pltpu.roll(g, t - k, 0), 0.0)
        top = g[:h]
        d_halo = jnp.where(_iota((h, w), 0) >= h - k, pltpu.roll(top, h - k, 0) if k < h else top, 0.0)
        return d_halo, d_cur

    shift.defvjp(fwd, bwd)
    return shift


_shift8 = _make_shift(8)
_shift16 = _make_shift(16)


def _make_tail(h):
    @jax.custom_vjp
    def tail(x):
        return x[x.shape[0] - h:]

    def fwd(x):
        return tail(x), x.shape[0]

    def bwd(t, g):
        return (jnp.concatenate([jnp.zeros((t - h, g.shape[1]), F32), g], axis=0),)

    tail.defvjp(fwd, bwd)
    return tail


_tail8 = _make_tail(8)
_tail16 = _make_tail(16)


@jax.custom_vjp
def _cumsum_rows(x):
    t = x.shape[0]
    row, s = _iota(x.shape, 0), 1
    while s < t:
        x = x + jnp.where(row >= s, pltpu.roll(x, s, 0), 0.0)
        s *= 2
    return x


def _cumsum_rows_fwd(x):
    return _cumsum_rows(x), None


def _cumsum_rows_bwd(_, g):
    t = g.shape[0]
    row, s = _iota(g.shape, 0), 1
    while s < t:
        g = g + jnp.where(row < t - s, pltpu.roll(g, t - s, 0), 0.0)
        s *= 2
    return (g,)


_cumsum_rows.defvjp(_cumsum_rows_fwd, _cumsum_rows_bwd)


@jax.custom_vjp
def _rot_pairs(t):
    e = _iota(t.shape, 1) % ATT_HEAD_DIM
    n = t.shape[1]
    return jnp.where(e < 8, -pltpu.roll(t, n - 8, 1), jnp.where(e < 16, pltpu.roll(t, 8, 1), 0.0))


def _rot_pairs_fwd(t):
    return _rot_pairs(t), None


def _rot_pairs_bwd(_, g):
    e = _iota(g.shape, 1) % ATT_HEAD_DIM
    n = g.shape[1]
    return (pltpu.roll(jnp.where(e < 8, -g, 0.0), 8, 1) + pltpu.roll(jnp.where(jnp.logical_and(e >= 8, e < 16), g, 0.0), n - 8, 1),)


_rot_pairs.defvjp(_rot_pairs_fwd, _rot_pairs_bwd)


def _make_thirds():
    @jax.custom_vjp
    def thirds(x):
        w = x.shape[1] // 3
        return x[:, :w], x[:, w:2 * w], x[:, 2 * w:]

    def fwd(x):
        return thirds(x), None

    def bwd(_, g):
        return (jnp.concatenate(g, axis=1),)

    thirds.defvjp(fwd, bwd)
    return thirds


_thirds = _make_thirds()


def _rowk(w, k):
    return jnp.sum(jnp.where(_iota(w.shape, 0) == k, w, 0.0), axis=0, keepdims=True)


def _silu(x):
    return x * (0.5 * jnp.tanh(0.5 * x) + 0.5)


def _softplus(x):
    return jnp.maximum(x, 0.0) + jnp.log(1.0 + jnp.exp(-jnp.abs(x)))


def _tile(dim, target, unit=128):
    if dim <= target:
        return dim
    best = None
    for t in range(unit, target + 1, unit):
        if dim % t == 0:
            best = t
    assert best is not None, (dim, target)
    return best


class Ride:
    def __init__(self, ins, out_shapes, nsem, start, finish):
        self.ins, self.out_shapes, self.nsem, self.start, self.finish = ins, out_shapes, nsem, start, finish

    def specs(self):
        hbm = pl.BlockSpec(memory_space=pl.ANY)
        return [hbm] * len(self.ins), [hbm] * len(self.out_shapes), [pltpu.SemaphoreType.DMA((self.nsem,))] * 2

    def begin(self, in_refs, out_refs, sems, cond=None):
        me = (lax.axis_index("x"), lax.axis_index("y"), lax.axis_index("c"))
        go = lambda: self.start(in_refs, out_refs, sems[0], sems[1], me)
        go() if cond is None else pl.when(cond)(go)

    def end(self, in_refs, out_refs, sems, cond=None):
        me = (lax.axis_index("x"), lax.axis_index("y"), lax.axis_index("c"))
        go = lambda: self.finish(in_refs, out_refs, sems[0], sems[1], me)
        go() if cond is None else pl.when(cond)(go)


class _Riders:
    def reset(self):
        self.booked, self.done = {}, {}

    def book(self, host, ride):
        assert host not in self.booked, host
        self.booked[host] = ride

    def take(self, host):
        return self.booked.pop(host, None)

    def result(self, host):
        return self.done[host]


RIDERS = _Riders()
RIDERS.reset()


class Row:
    def __init__(self, arr, w=None, fb=None, fc=None, diff=True, slot=False, dcols=None, dfc=None, ddtype=F32, view=None):
        self.ddtype = ddtype
        self.view = view
        self.arr = arr
        self.w = arr.shape[2] if w is None else w
        self.fb = (lambda b: 0) if fb is None else fb
        self.fc = (lambda b: 0) if fc is None else fc
        self.diff = diff
        self.slot = slot
        self.dcols = dcols
        self.dfc = dfc


class Vec:
    def __init__(self, arr, w=None, fc=None, diff=True):
        self.arr = arr
        self.w = arr.shape[1] if w is None else w
        self.fc = fc
        self.diff = diff


def _row_spec(r, t, nchunk, reverse):
    shape = (1, t, r.w) if r.view is None else (1, t // r.view, r.view * r.w)
    if reverse:
        return pl.BlockSpec(shape, lambda b, i, r=r: (r.fb(b), nchunk - 1 - i, r.fc(b)))
    return pl.BlockSpec(shape, lambda b, i, r=r: (r.fb(b), i, r.fc(b)))


def _load_row(ref, r, t, scr):
    if r.view is None:
        return ref[0]
    d, w = r.view, r.w
    for q in range(d):
        for j in range(w // 128):
            scr[j, pl.ds(q, t // d, stride=d), :] = ref[0, :, q * w + 128 * j:q * w + 128 * (j + 1)].astype(F32)
    return jnp.concatenate([scr[j] for j in range(w // 128)], axis=1)


def _store_row(ref, r, t, scr, val):
    if r.view is None:
        ref[0] = val.astype(ref.dtype)
        return
    d, w = r.view, r.w
    for j in range(w // 128):
        scr[j] = val[:, 128 * j:128 * (j + 1)]
    for q in range(d):
        for j in range(w // 128):
            ref[0, :, q * w + 128 * j:q * w + 128 * (j + 1)] = scr[j, pl.ds(q, t // d, stride=d), :].astype(ref.dtype)


def _view_scratch(specs, t):
    ws = [r.w for r in specs if r.view is not None]
    return [pltpu.VMEM((max(ws) // 128, t, 128), F32)] if ws else []


def _vec_spec(v):
    if v.fc is None:
        return pl.BlockSpec(v.arr.shape, lambda b, i: (0, 0))
    return pl.BlockSpec((v.arr.shape[0], v.w), lambda b, i, v=v: (0, v.fc(b)))


def _cparams():
    return pltpu.CompilerParams(dimension_semantics=("arbitrary", "arbitrary"), vmem_limit_bytes=VMEM_LIMIT_BYTES)


def scan_fwd(name, fn, *, nb, nchunk, t, rows, vecs, carries, outs, save):
    nr, nv, nc, no = len(rows), len(vecs), len(carries), len(outs)
    ns = nc if save else 0
    ride = RIDERS.take(name)
    r_in, r_out, r_scr = ride.specs() if ride else ([], [], [])

    def body(*refs):
        p = 0
        row_refs = refs[p:p + nr]; p += nr
        vec_refs = refs[p:p + nv]; p += nv
        ride_in = refs[p:p + len(r_in)]; p += len(r_in)
        out_refs = refs[p:p + no]; p += no
        save_refs = refs[p:p + ns]; p += ns
        ride_out = refs[p:p + len(r_out)]; p += len(r_out)
        car = refs[p:p + nc]; p += nc
        scr = refs[p] if stage else None
        sems = refs[p + len(stage):]
        b, i = pl.program_id(0), pl.program_id(1)
        if ride:
            ride.begin(ride_in, ride_out, sems, jnp.logical_and(b == 0, i == 0))
        if nc:
            @pl.when(i == 0)
            def _():
                for c_ref in car:
                    c_ref[...] = jnp.zeros(c_ref.shape, F32)
        cin = [c_ref[...] for c_ref in car]
        if save:
            for s_ref, cv in zip(save_refs, cin):
                s_ref[0, 0] = cv
        new_c, o = fn(i, b, cin, [_load_row(ref, r, t, scr) for ref, r in zip(row_refs, rows)], [v[...] for v in vec_refs])
        for c_ref, cv in zip(car, new_c):
            c_ref[...] = cv
        for o_ref, spec, ov in zip(out_refs, outs, o):
            _store_row(o_ref, spec, t, scr, ov)
        if ride:
            ride.end(ride_in, ride_out, sems, jnp.logical_and(b == nb - 1, i == nchunk - 1))

    stage = _view_scratch(list(rows) + list(outs), t)
    out_shape = [o.arr for o in outs]
    out_specs = [_row_spec(o, t, nchunk, False) for o in outs]
    if save:
        for cs in carries:
            out_shape.append(jax.ShapeDtypeStruct((nb, nchunk) + tuple(cs), F32))
            out_specs.append(pl.BlockSpec((1, 1) + tuple(cs), lambda b, i: (b, i, 0, 0)))
    res = pl.pallas_call(
        body, name=name, grid=(nb, nchunk),
        in_specs=[_row_spec(r, t, nchunk, False) for r in rows] + [_vec_spec(v) for v in vecs] + r_in,
        out_specs=out_specs + r_out, out_shape=out_shape + (list(ride.out_shapes) if ride else []),
        scratch_shapes=[pltpu.VMEM(tuple(cs), F32) for cs in carries] + stage + r_scr,
        compiler_params=_cparams(),
    )(*[r.arr for r in rows], *[v.arr for v in vecs], *(ride.ins if ride else []))
    if ride:
        RIDERS.done[name] = list(res[no + ns:])
    return list(res[:no]), list(res[no:no + ns])


def scan_bwd(name, fn, *, nb, nchunk, t, rows, vecs, carries, saved, douts, adds=None):
    adds = adds or {}
    nr, nv, nc, no = len(rows), len(vecs), len(carries), len(douts)
    dri = [k for k, r in enumerate(rows) if r.diff]
    dvi = [k for k, v in enumerate(vecs) if v.diff]
    add_keys = sorted(adds)
    na = len(add_keys)
    ride = RIDERS.take(name)
    r_in, r_out, r_scr = ride.specs() if ride else ([], [], [])

    def body(*refs):
        p = 0
        row_refs = refs[p:p + nr]; p += nr
        vec_refs = refs[p:p + nv]; p += nv
        save_refs = refs[p:p + nc]; p += nc
        dout_refs = refs[p:p + no]; p += no
        add_refs = refs[p:p + na]; p += na
        ride_in = refs[p:p + len(r_in)]; p += len(r_in)
        drow_refs = refs[p:p + len(dri)]; p += len(dri)
        dvec_refs = refs[p:p + len(dvi)]; p += len(dvi)
        ride_out = refs[p:p + len(r_out)]; p += len(r_out)
        dcar = refs[p:p + nc]; p += nc
        scr = refs[p] if stage else None
        sems = refs[p + len(stage):]
        b, ir = pl.program_id(0), pl.program_id(1)
        ci = nchunk - 1 - ir
        if ride:
            ride.begin(ride_in, ride_out, sems, jnp.logical_and(b == 0, ir == 0))
        if nc:
            @pl.when(ir == 0)
            def _():
                for c_ref in dcar:
                    c_ref[...] = jnp.zeros(c_ref.shape, F32)
        rows_v = [_load_row(ref, r, t, scr) for ref, r in zip(row_refs, rows)]
        vecs_v = [v[...] for v in vec_refs]
        cin = [s[0, 0] for s in save_refs]
        dc = [c_ref[...] for c_ref in dcar]
        dout_v = [_load_row(ref, r, t, scr).astype(F32) for ref, r in zip(dout_refs, douts)]

        def f(cs, dr, dv):
            rr, vv = list(rows_v), list(vecs_v)
            for k, idx in enumerate(dri):
                rr[idx] = dr[k]
            for k, idx in enumerate(dvi):
                vv[idx] = dv[k]
            return fn(ci, b, cs, rr, vv)

        _, vjp = jax.vjp(f, cin, [rows_v[k].astype(F32) for k in dri], [vecs_v[k].astype(F32) for k in dvi])
        dcin, drows, dvecs = vjp((dc, dout_v))
        for c_ref, cv in zip(dcar, dcin):
            c_ref[...] = cv
        for k, (o_ref, ov) in enumerate(zip(drow_refs, drows)):
            if dri[k] in adds:
                ov = ov + add_refs[add_keys.index(dri[k])][0].astype(F32)
            _store_row(o_ref, rows[dri[k]], t, scr, ov)
        for k, (o_ref, ov) in enumerate(zip(dvec_refs, dvecs)):
            first = (ir == 0) if vecs[dvi[k]].fc is not None else jnp.logical_and(ir == 0, b == 0)

            @pl.when(first)
            def _(o_ref=o_ref, ov=ov):
                o_ref[...] = ov

            @pl.when(jnp.logical_not(first))
            def _(o_ref=o_ref, ov=ov):
                o_ref[...] += ov

        if ride:
            ride.end(ride_in, ride_out, sems, jnp.logical_and(b == nb - 1, ir == nchunk - 1))

    stage = _view_scratch(list(rows) + list(douts), t)
    in_specs = ([_row_spec(r, t, nchunk, True) for r in rows] + [_vec_spec(v) for v in vecs]
                + [pl.BlockSpec((1, 1) + tuple(cs), lambda b, i: (b, nchunk - 1 - i, 0, 0)) for cs in carries]
                + [_row_spec(d, t, nchunk, True) for d in douts]
                + [_row_spec(adds[k], t, nchunk, True) for k in add_keys] + r_in)
    out_shape, out_specs = [], []
    for k in dri:
        r = rows[k]
        if r.slot:
            out_shape.append(jax.ShapeDtypeStruct((nb, r.arr.shape[1], r.w), r.ddtype))
            out_specs.append(pl.BlockSpec((1, t, r.w), lambda b, i: (b, nchunk - 1 - i, 0)))
        elif r.dcols is not None:
            out_shape.append(jax.ShapeDtypeStruct((r.arr.shape[0], r.arr.shape[1], r.dcols), r.ddtype))
            out_specs.append(pl.BlockSpec((1, t, r.w), lambda b, i, r=r: (r.fb(b), nchunk - 1 - i, r.dfc(b))))
        else:
            out_shape.append(jax.ShapeDtypeStruct(r.arr.shape, r.ddtype))
            out_specs.append(_row_spec(r, t, nchunk, True))
    for k in dvi:
        out_shape.append(jax.ShapeDtypeStruct(vecs[k].arr.shape, F32))
        out_specs.append(_vec_spec(vecs[k]))
    nd = len(dri) + len(dvi)
    res = pl.pallas_call(
        body, name=name, grid=(nb, nchunk), in_specs=in_specs, out_specs=out_specs + r_out,
        out_shape=out_shape + (list(ride.out_shapes) if ride else []),
        scratch_shapes=[pltpu.VMEM(tuple(cs), F32) for cs in carries] + stage + r_scr,
        compiler_params=_cparams(),
    )(*[r.arr for r in rows], *[v.arr for v in vecs], *saved, *[d.arr for d in douts], *[adds[k].arr for k in add_keys],
      *(ride.ins if ride else []))
    if ride:
        RIDERS.done[name] = list(res[nd:])
    return list(res[:len(dri)]), list(res[len(dri):nd])


def out_row(shape, dtype=F32, w=None, fb=None, fc=None):
    return Row(jax.ShapeDtypeStruct(shape, dtype), w, fb, fc)


def _conv(shift, halo, cur, w, bias, taps):
    y = bias
    for k in range(taps):
        y = y + _rowk(w, k) * shift(halo, cur, taps - 1 - k)
    return y


def _ssd_fn(ci, b, carries, rows, vecs):
    cx, cb_, cc, ht = carries
    z, xr, br, cr, dtr = rows
    cwx, cbx, cwb, cbb, cwc, cbc, dtb, alog, dsk, ng = vecs
    t = z.shape[0]
    xs = _silu(_conv(_shift8, cx, xr, cwx, cbx, 4))
    bm = _silu(_conv(_shift8, cb_, br, cwb, cbb, 4))
    cm = _silu(_conv(_shift8, cc, cr, cwc, cbc, 4))
    dt = _softplus(dtr + dtb)
    acol = _cumsum_rows(dt * (-jnp.exp(alog)))
    arow = acol.T
    r, c = _iota((t, t), 0), _iota((t, t), 1)
    causal = r >= c
    cbm = _bdot(cm, bm, "nt")
    lane, sub = _iota(acol.shape, 1), _iota(arow.shape, 0)
    colh = _iota(xs.shape, 1) // 64
    a, dtx, dx, acs = jnp.zeros(xs.shape, F32), jnp.zeros(xs.shape, F32), jnp.zeros((1, xs.shape[1]), F32), []
    for j in range(4):
        h = 4 * b + j
        ac = jnp.sum(jnp.where(lane == h, acol, 0.0), axis=1, keepdims=True)
        acs.append(ac)
        a = jnp.where(colh == j, ac, a)
        dtx = jnp.where(colh == j, jnp.sum(jnp.where(lane == h, dt, 0.0), axis=1, keepdims=True), dtx)
        dx = jnp.where(_iota(dx.shape, 1) // 64 == j, jnp.sum(jnp.where(_iota(dsk.shape, 1) == h, dsk, 0.0), axis=1, keepdims=True), dx)
    atot = jnp.sum(jnp.where(_iota(a.shape, 0) == t - 1, a, 0.0), axis=0, keepdims=True)
    x = xs * dtx
    ydiag = jnp.zeros(x.shape, F32)
    for j in range(4):
        ar = jnp.sum(jnp.where(sub == 4 * b + j, arow, 0.0), axis=0, keepdims=True)
        lmat = jnp.exp(jnp.where(causal, acs[j] - ar, NEG))
        ydiag = ydiag + _bdot(cbm * lmat, jnp.where(colh == j, x, 0.0), "nn")
    yoff = _bdot(cm, ht, "nn") * jnp.exp(a)
    ht_new = ht * jnp.exp(atot) + _bdot(bm, x * jnp.exp(atot - a), "tn")
    y = ydiag + yoff + dx * xs
    yz = y * _silu(z)
    yn = yz * lax.rsqrt(jnp.mean(yz * yz, axis=-1, keepdims=True) + NORM_EPS) * ng
    return [_tail8(xr), _tail8(br), _tail8(cr), ht_new], [yn]


_SSD_T = 256
_SSD_CARRIES = [(8, 256), (8, 128), (8, 128), (128, 256)]


def _ssd_io(proj3, p):
    own = lambda b: b
    rows = [Row(proj3, 256, fc=own, dcols=512, dfc=own, ddtype=BF16),
            Row(proj3, 256, fc=lambda b: 2 + b, dcols=512, dfc=own, ddtype=BF16),
            Row(proj3, 128, fc=lambda b: 8 + b, dcols=256, dfc=own, ddtype=BF16),
            Row(proj3, 128, fc=lambda b: 10 + b, dcols=256, dfc=own, ddtype=BF16),
            Row(proj3, 128, fc=lambda b: 14, slot=True)]
    vecs = [Vec(p["cw"], 256, lambda b: b), Vec(p["cb"], 256, lambda b: b),
            Vec(p["cw"], 128, lambda b: 4 + b), Vec(p["cb"], 128, lambda b: 4 + b),
            Vec(p["cw"], 128, lambda b: 6 + b), Vec(p["cb"], 128, lambda b: 6 + b),
            Vec(p["dtb"]), Vec(p["alog"]), Vec(p["dsk"]), Vec(p["ng"], 256, lambda b: b)]
    return rows, vecs


def ssd_forward(name, proj3, p):
    rows, vecs = _ssd_io(proj3, p)
    s = proj3.shape[1]
    (y,), saved = scan_fwd(name, _ssd_fn, nb=2, nchunk=s // _SSD_T, t=_SSD_T, rows=rows, vecs=vecs,
                           carries=_SSD_CARRIES, outs=[out_row((1, s, SSD_INNER), BF16, 256, fc=lambda b: b)], save=True)
    return y, saved


def ssd_backward(name, proj3, p, saved, dmix3):
    rows, vecs = _ssd_io(proj3, p)
    s = proj3.shape[1]
    drows, dvecs = scan_bwd(name, _ssd_fn, nb=2, nchunk=s // _SSD_T, t=_SSD_T, rows=rows, vecs=vecs,
                            carries=_SSD_CARRIES, saved=saved, douts=[Row(dmix3, 256, fc=lambda b: b)])
    return drows, dvecs


def _pool_fn(ci, b, carries, rows, vecs):
    (cu,) = carries
    (u,) = rows
    wbd, scale = vecs
    t = u.shape[0]
    pos = ci * t + _iota(u.shape, 0)
    grp = _iota(u.shape, 1) // 64
    acc, pooled, k = u, jnp.zeros(u.shape, F32), 1
    for gi, w in enumerate(POOL_WINDOWS):
        while k < w:
            acc = acc + _shift16(cu, u, k)
            k += 1
        pooled = jnp.where(grp == gi, acc / jnp.minimum(pos + 1, w).astype(F32), pooled)
    y = _bdot(pooled - u, wbd, "nn") * scale
    return [_tail16(u)], [y]


_POOL_T = 256


def _pool_io(proj3, wbd, scale):
    return [Row(proj3, 256, fc=lambda b: 6, dcols=256, dfc=lambda b: 0, ddtype=BF16)], [Vec(wbd), Vec(scale)]


def pool_forward(name, proj3, wbd, scale):
    rows, vecs = _pool_io(proj3, wbd, scale)
    s = proj3.shape[1]
    (y,), saved = scan_fwd(name, _pool_fn, nb=1, nchunk=s // _POOL_T, t=_POOL_T, rows=rows, vecs=vecs,
                           carries=[(16, 256)], outs=[out_row((1, s, POOL_W), BF16)], save=True)
    return y, saved


def pool_backward(name, proj3, wbd, scale, saved, dmix3):
    rows, vecs = _pool_io(proj3, wbd, scale)
    s = proj3.shape[1]
    return scan_bwd(name, _pool_fn, nb=1, nchunk=s // _POOL_T, t=_POOL_T, rows=rows, vecs=vecs,
                    carries=[(16, 256)], saved=saved, douts=[Row(dmix3, 256, fc=lambda b: 2)])


def _attn_fn(ci, b, carries, rows, vecs):
    kp, vp = carries
    qr, kr, v = _thirds(rows[0])
    scale = ATT_HEAD_DIM ** -0.5
    q = qr
    n = q.shape[0]
    r, c = _iota((n, n), 0), _iota((n, n), 1)
    prev_ok, cur_ok = jnp.logical_and(c >= r, ci > 0), r >= c
    head = _iota(q.shape, 1) // ATT_HEAD_DIM
    o, lse = jnp.zeros(q.shape, F32), jnp.zeros(q.shape, F32)
    for h in range(ATT_HEADS):
        mine = head == h
        qh = jnp.where(mine, qr, 0.0)
        sp = jnp.where(prev_ok, _bdot(qh, kp, "nt") * scale, NEG)
        sc = jnp.where(cur_ok, _bdot(qh, kr, "nt") * scale, NEG)
        m = lax.stop_gradient(jnp.maximum(jnp.max(sp, axis=1, keepdims=True), jnp.max(sc, axis=1, keepdims=True)))
        pp, pc = jnp.exp(sp - m), jnp.exp(sc - m)
        l = jnp.sum(pp, axis=1, keepdims=True) + jnp.sum(pc, axis=1, keepdims=True)
        o = jnp.where(mine, (_bdot(pp, vp, "nn") + _bdot(pc, v, "nn")) / l, o)
        lse = jnp.where(mine, m + jnp.log(l), lse)
    return [kr, v], [o, lse]


_ATT_CARRIES = [(ATT_BLOCK, ATT_W), (ATT_BLOCK, ATT_W)]


def attn_forward(name, pv, d):
    l = pv.shape[1]
    own = lambda b: b
    outs = [out_row((1, l, d * ATT_W), F32, ATT_W, fc=own) for _ in range(2)]
    (o, lse), saved = scan_fwd(name, _attn_fn, nb=d, nchunk=l // ATT_BLOCK, t=ATT_BLOCK, rows=[Row(pv, 3 * ATT_W, fc=own)],
                               vecs=[], carries=_ATT_CARRIES, outs=outs, save=True)
    return o, lse, saved


def attn_backward(name, pv, d, saved, do, dlse):
    l = pv.shape[1]
    own = lambda b: b
    (dpv,), _ = scan_bwd(name, _attn_fn, nb=d, nchunk=l // ATT_BLOCK, t=ATT_BLOCK, rows=[Row(pv, 3 * ATT_W, fc=own)], vecs=[],
                         carries=_ATT_CARRIES, saved=saved, douts=[Row(do, ATT_W, fc=own), Row(dlse, ATT_W, fc=own)])
    return dpv


def _rope_fn(ci, b, carries, rows, vecs):
    x, cs, sn = rows
    return [], [x * cs + _rot_pairs(x) * sn]


def _rope3_fn(ci, b, carries, rows, vecs):
    _, (y,) = _rope_fn(ci, b, carries, rows, vecs)
    return [], [y, y, y]


def _by_residue(a_or_shape, w, d):
    if isinstance(a_or_shape, tuple):
        _, s, _ = a_or_shape
        return Row(jax.ShapeDtypeStruct((1, s // d, d * w), F32), w, view=None if d == 1 else d)
    return Row(a_or_shape, w, view=None if d == 1 else d)


def rope_forward(name, qkv3, cs3, sn3):
    s, w = qkv3.shape[1], qkv3.shape[2]
    ys, _ = scan_fwd(name, _rope3_fn, nb=1, nchunk=s // _ROW_T, t=_ROW_T, vecs=[], carries=[], save=False,
                     rows=[Row(qkv3), Row(cs3, diff=False), Row(sn3, diff=False)],
                     outs=[_by_residue(qkv3.shape, w, d) for _, d in ATT_PATTERNS])
    return ys


def rope_backward(name, qkv3, cs3, sn3, dys):
    s, w = qkv3.shape[1], qkv3.shape[2]
    (dx,), _ = scan_bwd(name, _rope3_fn, nb=1, nchunk=s // _ROW_T, t=_ROW_T, vecs=[], carries=[], saved=[],
                        rows=[Row(qkv3, ddtype=BF16), Row(cs3, diff=False), Row(sn3, diff=False)],
                        douts=[_by_residue(a, w, d) for a, (_, d) in zip(dys, ATT_PATTERNS)])
    return dx


def _merge_fn(ci, b, carries, rows, vecs):
    o1, o2, o3, l1, l2, l3 = rows
    mx = lax.stop_gradient(jnp.maximum(l1, jnp.maximum(l2, l3)))
    e1, e2, e3 = jnp.exp(l1 - mx), jnp.exp(l2 - mx), jnp.exp(l3 - mx)
    return [], [(e1 * o1 + e2 * o2 + e3 * o3) / (e1 + e2 + e3)]


_ROW_T = 256


def _merge_rows(os_, ls_):
    ds = [d for _, d in ATT_PATTERNS]
    return [_by_residue(a, ATT_W, d) for a, d in zip(os_, ds)] + [_by_residue(a, ATT_W, d) for a, d in zip(ls_, ds)]


def merge_forward(name, os_, ls_, s):
    (y,), _ = scan_fwd(name, _merge_fn, nb=1, nchunk=s // _ROW_T, t=_ROW_T, rows=_merge_rows(os_, ls_), vecs=[],
                       carries=[], outs=[out_row((1, s, ATT_W), BF16)], save=False)
    return y


def merge_backward(name, os_, ls_, dmix3):
    s = dmix3.shape[1]
    drows, _ = scan_bwd(name, _merge_fn, nb=1, nchunk=s // _ROW_T, t=_ROW_T, rows=_merge_rows(os_, ls_), vecs=[],
                        carries=[], saved=[], douts=[Row(dmix3, 256, fc=lambda b: 3)])
    return drows


def _norm_mod_fn(ci, b, carries, rows, vecs):
    (x,) = rows
    g, sc, sh = vecs
    xn = x * lax.rsqrt(jnp.mean(x * x, axis=-1, keepdims=True) + NORM_EPS)
    return [], [xn * g * (1.0 + sc) + sh]


def norm_mod_forward(name, x3, g, sc, sh):
    s = x3.shape[1]
    (h,), _ = scan_fwd(name, _norm_mod_fn, nb=1, nchunk=s // _ROW_T, t=_ROW_T, rows=[Row(x3)], vecs=[Vec(g), Vec(sc), Vec(sh)],
                       carries=[], outs=[out_row(x3.shape, BF16)], save=False)
    return h


def norm_mod_backward(name, x3, g, sc, sh, dh3, add3):
    s = x3.shape[1]
    (dx,), dv = scan_bwd(name, _norm_mod_fn, nb=1, nchunk=s // _ROW_T, t=_ROW_T, rows=[Row(x3)], vecs=[Vec(g), Vec(sc), Vec(sh)],
                         carries=[], saved=[], douts=[Row(dh3)], adds={0: Row(add3)})
    return dx, dv


def _gate_fn(ci, b, carries, rows, vecs):
    return [], [rows[0] * vecs[0]]


def gate_backward(name, o3, g, dx3):
    s = o3.shape[1]
    (do,), (dg,) = scan_bwd(name, _gate_fn, nb=1, nchunk=s // _ROW_T, t=_ROW_T, rows=[Row(o3, ddtype=BF16)], vecs=[Vec(g)],
                            carries=[], saved=[], douts=[Row(dx3)])
    return do, dg


def _make_halves():
    @jax.custom_vjp
    def halves(x):
        h = x.shape[1] // 2
        return x[:, :h], x[:, h:]

    def fwd(x):
        return halves(x), None

    def bwd(_, g):
        return (jnp.concatenate(g, axis=1),)

    halves.defvjp(fwd, bwd)
    return halves


_halves = _make_halves()


def _ffn_fn(ci, b, carries, rows, vecs):
    (cu,) = carries
    (u,) = rows
    w, bias = vecs
    hg, hu = _halves(_conv(_shift8, cu, u, w, bias, 3))
    return [_tail8(u)], [_silu(hg) * hu]


_FFN_T = 256
_FFN_CW = FFN_DIM // 2
_FFN_CARRIES = [(8, 2 * _FFN_CW)]
FFN_BLOCK_ORDER = [0, 2, 1, 3]


def _ffn_io(up3, cw, cb):
    own = lambda b: b
    return [Row(up3, 2 * _FFN_CW, fc=own, ddtype=BF16)], [Vec(cw, 2 * _FFN_CW, own), Vec(cb, 2 * _FFN_CW, own)]


def ffn_mid_forward(name, up3, cw, cb):
    rows, vecs = _ffn_io(up3, cw, cb)
    s = up3.shape[1]
    (act,), saved = scan_fwd(name, _ffn_fn, nb=2, nchunk=s // _FFN_T, t=_FFN_T, rows=rows, vecs=vecs, carries=_FFN_CARRIES,
                             outs=[out_row((1, s, FFN_DIM), BF16, _FFN_CW, fc=lambda b: b)], save=True)
    return act, saved


def ffn_mid_backward(name, up3, cw, cb, saved, dact3):
    rows, vecs = _ffn_io(up3, cw, cb)
    s = up3.shape[1]
    return scan_bwd(name, _ffn_fn, nb=2, nchunk=s // _FFN_T, t=_FFN_T, rows=rows, vecs=vecs, carries=_FFN_CARRIES,
                    saved=saved, douts=[Row(dact3, _FFN_CW, fc=lambda b: b)])


def _adam_fn(ci, b, carries, rows, vecs):
    w, g, m, v = rows
    m = ADAM_B1 * m + (1.0 - ADAM_B1) * g
    v = ADAM_B2 * v + (1.0 - ADAM_B2) * (g * g)
    m_hat = m / (1.0 - ADAM_B1 ** ADAM_STEP)
    v_hat = v / (1.0 - ADAM_B2 ** ADAM_STEP)
    delta = -ADAM_LR * (m_hat / (jnp.sqrt(v_hat) + ADAM_EPS) + ADAM_WD * w)
    return [], [delta, m, v]


def adamw(name, w, g, m, v):
    shape = w.shape
    c = shape[-1]
    r = int(np.prod(shape[:-1]))
    t = _tile(r, 256, 8)
    as3 = lambda a: a.reshape(1, r, c)
    outs, _ = scan_fwd(name, _adam_fn, nb=1, nchunk=r // t, t=t, rows=[Row(as3(a)) for a in (w, g, m, v)], vecs=[], carries=[],
                       outs=[out_row((1, r, c)) for _ in range(3)], save=False)
    return [o.reshape(shape) for o in outs]


def rope_tables(positions):
    inv_freq = ROPE_THETA ** (-jnp.arange(0, ROT_DIM, 2, dtype=F32) / ROT_DIM)
    ang = positions.astype(F32)[:, None] * inv_freq
    s = positions.shape[0]
    cs = jnp.concatenate([jnp.cos(ang), jnp.cos(ang), jnp.ones((s, ATT_HEAD_DIM - ROT_DIM), F32)], axis=1)
    sn = jnp.concatenate([jnp.sin(ang), jnp.sin(ang), jnp.zeros((s, ATT_HEAD_DIM - ROT_DIM), F32)], axis=1)
    cs3 = jnp.concatenate([jnp.tile(cs, (1, 2 * ATT_HEADS)), jnp.ones((s, ATT_W), F32)], axis=1)
    sn3 = jnp.concatenate([jnp.tile(sn, (1, 2 * ATT_HEADS)), jnp.zeros((s, ATT_W), F32)], axis=1)
    return cs3[None], sn3[None]


def attention_forward(lname, qkv3, cs3, sn3):
    s = qkv3.shape[1]
    rotated = rope_forward(f"{lname}_rope", qkv3, cs3, sn3)
    os_, ls_, keep = [], [], []
    for pi, (_, d) in enumerate(ATT_PATTERNS):
        o, lse, saved = attn_forward(f"{lname}_attn{pi}", rotated[pi], d)
        os_.append(o)
        ls_.append(lse)
        keep.append(saved)
    y = merge_forward(f"{lname}_merge", os_, ls_, s)
    return y, (rotated, os_, ls_, keep)


def attention_backward(lname, qkv3, cs3, sn3, res, dmix3):
    rotated, os_, ls_, keep = res
    dm = merge_backward(f"{lname}_merge_b", os_, ls_, dmix3)
    dys = [attn_backward(f"{lname}_attn{pi}_b", rotated[pi], d, keep[pi], dm[pi], dm[3 + pi]) for pi, (_, d) in enumerate(ATT_PATTERNS)]
    return rope_backward(f"{lname}_rope_b", qkv3, cs3, sn3, dys)


def mm(name, a, b, mode, out_dtype=F32, res=None, gate=None, tm=1408, tn=1536, tk=1408, into=None):
    if mode == "nn":
        (m, k), n = a.shape, b.shape[1]
    elif mode == "nt":
        (m, k), n = a.shape, b.shape[0]
    else:
        (k, m), n = a.shape, b.shape[1]
    tm, tn, tk = _tile(m, tm), _tile(n, tn), _tile(k, tk)
    nk = k // tk
    a_spec = pl.BlockSpec((tk, tm), lambda i, j, q: (q, i)) if mode == "tn" else pl.BlockSpec((tm, tk), lambda i, j, q: (i, q))
    b_spec = pl.BlockSpec((tn, tk), lambda i, j, q: (j, q)) if mode == "nt" else pl.BlockSpec((tk, tn), lambda i, j, q: (q, j))
    o_spec = pl.BlockSpec((tm, tn), lambda i, j, q: (i, j))
    fused = res is not None
    lead = 0 if into is None else into[0].ndim - 2
    first = (0,) * lead + (slice(None), slice(None))

    def body(*refs):
        if fused:
            a_ref, b_ref, r_ref, g_ref, o_ref, o2_ref, acc = refs
        elif into is not None:
            a_ref, b_ref, _, o_ref, acc = refs
        else:
            a_ref, b_ref, o_ref, acc = refs
        q = pl.program_id(2)

        @pl.when(q == 0)
        def _():
            acc[...] = jnp.zeros(acc.shape, F32)

        acc[...] += _mxu(a_ref[...], b_ref[...], mode)

        @pl.when(q == nk - 1)
        def _():
            o_ref[first] = acc[...].astype(o_ref.dtype)
            if fused:
                o2_ref[...] = r_ref[...] + g_ref[...] * acc[...]

    ins, in_specs = [a, b], [a_spec, b_spec]
    out_shape, out_specs = [jax.ShapeDtypeStruct((m, n), out_dtype)], [o_spec]
    if fused:
        ins += [res, gate]
        in_specs += [o_spec, pl.BlockSpec((1, tn), lambda i, j, q: (0, j))]
        out_shape.append(jax.ShapeDtypeStruct((m, n), F32))
        out_specs.append(o_spec)
    aliases = {}
    if into is not None:
        buf, omap = into
        ins.append(buf)
        in_specs.append(pl.BlockSpec(memory_space=pl.ANY))
        out_shape = [jax.ShapeDtypeStruct(buf.shape, buf.dtype)]
        out_specs = [pl.BlockSpec((1,) * lead + (tm, tn), lambda i, j, q: omap(i, j))]
        aliases = {2: 0}
    out = pl.pallas_call(
        body, name=name, grid=(m // tm, n // tn, nk), in_specs=in_specs, out_specs=out_specs, out_shape=out_shape,
        scratch_shapes=[pltpu.VMEM((tm, tn), F32)], input_output_aliases=aliases,
        compiler_params=pltpu.CompilerParams(dimension_semantics=("parallel", "parallel", "arbitrary"),
                                             vmem_limit_bytes=VMEM_LIMIT_BYTES),
    )(*ins)
    return tuple(out) if fused else out[0]


def final_loss(name, x3, t3, g):
    s, d = x3.shape[1], x3.shape[2]
    t = _ROW_T

    def body(x_ref, t_ref, g_ref, loss_ref, dx_ref, dg_ref):
        i = pl.program_id(0)
        tv = t_ref[0]

        def f(x, gg):
            y = x * lax.rsqrt(jnp.mean(x * x, axis=-1, keepdims=True) + NORM_EPS) * gg
            e = y - tv
            return 0.5 * jnp.sum(jnp.mean(e * e, axis=-1, keepdims=True), axis=0, keepdims=True)

        l, vjp = jax.vjp(f, x_ref[0], g_ref[...])
        dx, dg = vjp(jnp.ones((1, 1), F32))
        dx_ref[0] = dx

        @pl.when(i == 0)
        def _():
            loss_ref[...] = jnp.zeros(loss_ref.shape, F32)
            dg_ref[...] = jnp.zeros(dg_ref.shape, F32)

        loss_ref[...] += jnp.broadcast_to(l, loss_ref.shape)
        dg_ref[...] += dg

    row = pl.BlockSpec((1, t, d), lambda i: (0, i, 0))
    vec = pl.BlockSpec((1, d), lambda i: (0, 0))
    return pl.pallas_call(
        body, name=name, grid=(s // t,), in_specs=[row, row, vec],
        out_specs=[pl.BlockSpec((8, 128), lambda i: (0, 0)), row, vec],
        out_shape=[jax.ShapeDtypeStruct((8, 128), F32), jax.ShapeDtypeStruct(x3.shape, F32), jax.ShapeDtypeStruct((1, d), F32)],
        compiler_params=pltpu.CompilerParams(dimension_semantics=("arbitrary",), vmem_limit_bytes=VMEM_LIMIT_BYTES),
    )(x3, t3, g)


_ADA_TN = 512


def ada_forward(name, c16, ada_w):
    depth, d, cols = ada_w.shape

    def body(c_ref, w_ref, o_ref):
        o_ref[0] = _mxu(_silu(c_ref[...]), w_ref[0], "nn")

    return pl.pallas_call(
        body, name=name, grid=(depth, cols // _ADA_TN),
        in_specs=[pl.BlockSpec((16, d), lambda l, j: (0, 0)), pl.BlockSpec((1, d, _ADA_TN), lambda l, j: (l, 0, j))],
        out_specs=pl.BlockSpec((1, 16, _ADA_TN), lambda l, j: (l, 0, j)),
        out_shape=jax.ShapeDtypeStruct((depth, 16, cols), F32),
        compiler_params=pltpu.CompilerParams(dimension_semantics=("arbitrary", "arbitrary"), vmem_limit_bytes=VMEM_LIMIT_BYTES),
    )(c16, ada_w)


def ada_backward(name, c16, dmod16, w, m, v):
    depth, d, cols = w.shape

    def body(c_ref, dm_ref, w_ref, m_ref, v_ref, g_ref, dl_ref, nm_ref, nv_ref):
        g = _mxu(_silu(c_ref[...]), dm_ref[0], "tn")
        _, (delta, nm, nv) = _adam_fn(None, None, [], [w_ref[0], g, m_ref[0], v_ref[0]], [])
        g_ref[0], dl_ref[0], nm_ref[0], nv_ref[0] = g, delta, nm, nv

    blk = pl.BlockSpec((1, d, _ADA_TN), lambda l, j: (l, 0, j))
    return pl.pallas_call(
        body, name=name, grid=(depth, cols // _ADA_TN),
        in_specs=[pl.BlockSpec((16, d), lambda l, j: (0, 0)), pl.BlockSpec((1, 16, _ADA_TN), lambda l, j: (l, 0, j)), blk, blk, blk],
        out_specs=[blk] * 4, out_shape=[jax.ShapeDtypeStruct(w.shape, F32)] * 4,
        compiler_params=pltpu.CompilerParams(dimension_semantics=("arbitrary", "arbitrary"), vmem_limit_bytes=VMEM_LIMIT_BYTES),
    )(c16, dmod16, w, m, v)


def _sum_fn(ci, b, carries, rows, vecs):
    acc = rows[0]
    for r in rows[1:]:
        acc = acc + r
    return [], [acc]


def sum_slots(name, a, nsum, out_dtype=F32):
    n, r, c = a.shape
    nb = n // nsum
    t = _tile(r, 256, 8)
    rows = [Row(a, fb=(lambda b, k=k: k * nb + b)) for k in range(nsum)]
    (out,), _ = scan_fwd(name, _sum_fn, nb=nb, nchunk=r // t, t=t, rows=rows, vecs=[], carries=[],
                         outs=[out_row((nb, r, c), out_dtype, fb=lambda b: b)], save=False)
    return out


def _sum_my_layer_fn(ci, b, carries, rows, vecs):
    layer0, layer1, theirs = rows
    return [], [jnp.where(lax.axis_index("c") == 0, layer0, layer1) + theirs]


def sum_cores(name, g, theirs, out_dtype):
    _, nb, r, c = g.shape
    g8 = g.reshape(2 * nb, r, c)
    t = _tile(r, 256, 8)
    rows = [Row(g8, fb=lambda b: b), Row(g8, fb=lambda b: nb + b), Row(theirs, fb=lambda b: b)]
    (out,), _ = scan_fwd(name, _sum_my_layer_fn, nb=nb, nchunk=r // t, t=t, rows=rows, vecs=[], carries=[],
                         outs=[out_row((nb, r, c), out_dtype, fb=lambda b: b)], save=False)
    return out


def _flip(mask, pos):
    return tuple((1 - p) if m else p for m, p in zip(mask, pos))


ALL_PEERS = [(a, b, c) for a in (0, 1) for b in (0, 1) for c in (0, 1)][1:]
CHIP_PEERS = [(1, 0, 0), (0, 1, 0), (1, 1, 0)]
SIBLING = [(0, 0, 1)]


def _divisor(size, target, unit):
    best = 1
    for n in range(1, target + 1):
        if size % n == 0 and (size // n) % unit == 0:
            best = n
    return best


def _pieces(src, dst, pieces):
    shape = src.shape
    unit = 16 if src.dtype == BF16 else 8
    if pieces <= 1:
        return [(src, dst)]
    if len(shape) == 2:
        n = _divisor(shape[0], pieces, unit)
        s = shape[0] // n
        return [(src.at[pl.ds(i * s, s)], dst.at[pl.ds(i * s, s)]) for i in range(n)]
    assert len(shape) == 3, shape
    n = _divisor(shape[1], max(pieces // shape[0], 1), unit)
    s = shape[1] // n
    return [(src.at[j, pl.ds(i * s, s)], dst.at[j, pl.ds(i * s, s)]) for j in range(shape[0]) for i in range(n)]


def comm_call(name, arrays, out_shapes, masks, src_fn, dst_fn, local_fn=None, pieces=1):
    na, npeer = len(arrays), len(masks)

    def body(*refs):
        ins, outs = refs[:na], refs[na:2 * na]
        send_sems, recv_sems, loc_sems = refs[2 * na:]
        me = (lax.axis_index("x"), lax.axis_index("y"), lax.axis_index("c"))
        local = []
        if local_fn is not None:
            for k in range(na):
                s, d = local_fn(k, ins[k], outs[k], me)
                for ps, pd in _pieces(s, d, pieces):
                    pltpu.make_async_copy(ps, pd, loc_sems.at[k]).start()
                local.append(pltpu.make_async_copy(s, d, loc_sems.at[k]))

        def remote(k, p, src, dst, to):
            return pltpu.make_async_remote_copy(
                src_ref=src, dst_ref=dst, send_sem=send_sems.at[k * npeer + p], recv_sem=recv_sems.at[k * npeer + p],
                device_id=to, device_id_type=MESH)

        for k in range(na):
            for p in range(npeer):
                peer = _flip(masks[p], me)
                for ps, pd in _pieces(src_fn(k, ins[k], me, peer), dst_fn(k, outs[k], me), pieces):
                    remote(k, p, ps, pd, peer).start()
        for k in range(na):
            for p in range(npeer):
                peer = _flip(masks[p], me)
                remote(k, p, src_fn(k, ins[k], me, peer), dst_fn(k, outs[k], peer), peer).wait_recv()
        for k in range(na):
            for p in range(npeer):
                peer = _flip(masks[p], me)
                remote(k, p, src_fn(k, ins[k], me, peer), dst_fn(k, outs[k], me), peer).wait_send()
        for cp in local:
            cp.wait()

    hbm = pl.BlockSpec(memory_space=pl.ANY)
    out = pl.pallas_call(
        body, name=name, in_specs=[hbm] * na, out_specs=[hbm] * na,
        out_shape=[jax.ShapeDtypeStruct(s, a.dtype) for s, a in zip(out_shapes, arrays)],
        scratch_shapes=[pltpu.SemaphoreType.DMA((na * npeer,)), pltpu.SemaphoreType.DMA((na * npeer,)),
                        pltpu.SemaphoreType.DMA((na,))],
    )(*arrays)
    return list(out)


def _dev(pos):
    return 4 * pos[0] + 2 * pos[1] + pos[2]


def _chip(pos):
    return 2 * pos[0] + pos[1]


def allgather8(name, a):
    (out,) = comm_call(name, [a], [(8,) + a.shape], ALL_PEERS,
                       src_fn=lambda k, r, me, peer: r, dst_fn=lambda k, o, sender: o.at[_dev(sender)],
                       local_fn=lambda k, r, o, me: (r, o.at[_dev(me)]))
    return out


def gather_layer_from_chips(name, arrays):
    return comm_call(name, arrays, [(4,) + a.shape[1:] for a in arrays], CHIP_PEERS,
                     src_fn=lambda k, r, me, peer: r.at[me[2]], dst_fn=lambda k, o, sender: o.at[_chip(sender)],
                     local_fn=lambda k, r, o, me: (r.at[me[2]], o.at[_chip(me)]), pieces=8)


def swap_layers(name, arrays, c):
    got = comm_call(name, arrays, [a.shape for a in arrays], SIBLING,
                    src_fn=lambda k, r, me, peer: r, dst_fn=lambda k, o, sender: o, pieces=32)
    return [[jnp.where(c == 0, a, g), jnp.where(c == 0, g, a)] for a, g in zip(arrays, got)]


def swap_other_layer(name, arrays):
    return comm_call(name, arrays, [a.shape[1:] for a in arrays], SIBLING,
                     src_fn=lambda k, r, me, peer: r.at[peer[2]], dst_fn=lambda k, o, sender: o, pieces=32)


def scatter_to_chips(name, arrays):
    return comm_call(name, arrays, [a.shape for a in arrays], CHIP_PEERS,
                     src_fn=lambda k, r, me, peer: r.at[_chip(peer)], dst_fn=lambda k, o, sender: o.at[_chip(sender)],
                     local_fn=lambda k, r, o, me: (r.at[_chip(me)], o.at[_chip(me)]), pieces=8)


def _rows_of(shape):
    return -(-int(np.prod(shape)) // 1024) * 8


def _pack(arrs):
    parts = []
    for a in arrs:
        flat = a.reshape(-1).astype(F32)
        parts.append(jnp.pad(flat, (0, _rows_of(a.shape) * 128 - flat.shape[0])).reshape(-1, 128))
    rows = sum(p.shape[0] for p in parts)
    parts.append(jnp.zeros(((-rows) % _ROW_T, 128), F32))
    return jnp.concatenate(parts, axis=0)


def _unpack(buf, shapes):
    out, o = [], 0
    for s in shapes:
        r, n = _rows_of(s), int(np.prod(s))
        out.append(buf[o:o + r].reshape(-1)[:n].reshape(s))
        o += r
    return out


_WEIGHTS = ["ada_w", "ada_b", "norm1_g", "w_in", "ssd_conv_w", "ssd_conv_b", "ssd_dt_bias", "ssd_a_log", "ssd_d", "ssd_norm_g",
            "pool_w", "pool_scale", "w_out", "norm2_g", "ffn_up", "ffn_conv_w", "ffn_conv_b", "ffn_down", "final_g"]
_BIG = ["w_in", "w_out", "ffn_up", "ffn_down"]
_SMALL = [n for n in _WEIGHTS if n not in _BIG and n != "ada_w"]
_COL_SHARDED_SMALL = {"ssd_conv_w": 256, "ffn_conv_w": 1408}


def _pad_lanes(v, n=128):
    return jnp.pad(v.astype(F32), (0, n - v.shape[0]))[None]


def _perm_cols(w):
    pad = jnp.zeros(w.shape[:-1] + (IN_WP - IN_W,), w.dtype)
    return jnp.concatenate([w[..., :1536], w[..., 1544:1800], w[..., 1536:1544], pad, w[..., 1800:]], axis=-1)


def _unperm_cols(g):
    return jnp.concatenate([g[..., :1536], g[..., 1792:1800], g[..., 1536:1792], g[..., IN_MAIN:]], axis=-1)


_CHIP2_PARTS = [(1284, 1536), (1792, 1800), (1536, 1792), (IN_MAIN, IN_MAIN + 126)]


def _w_in_chip_cols(gp):
    q = IN_W // 4
    return [gp[:, :q], gp[:, q:2 * q], jnp.concatenate([gp[:, a:b] for a, b in _CHIP2_PARTS], axis=1), gp[:, IN_WP - q:]]


def _w_in_from_chips(a):
    c2 = a[2]
    pad = jnp.zeros((a.shape[1], IN_WP - IN_W), a.dtype)
    return jnp.concatenate([a[0], a[1], c2[:, :252], c2[:, 260:516], c2[:, 252:260], pad, c2[:, 516:], a[3]], axis=1)


def _ffn_block_perm(a):
    n = a.shape[-1] // 4
    return jnp.concatenate([a[..., j * n:(j + 1) * n] for j in FFN_BLOCK_ORDER], axis=-1)


def _layer_forward(i, x3, modv, wts, sp, cs3, sn3):
    sh1, sc1, g1, sh2, sc2, g2 = modv
    big = lambda n: wts[n]() if callable(wts[n]) else wts[n]
    h1 = norm_mod_forward(f"l{i}_norm1", x3, wts["norm1_g"], sc1, sh1)
    proj3 = mm(f"l{i}_proj", h1[0], big("w_in")[:, :IN_MAIN], "nn")[None]
    qkv3 = mm(f"l{i}_qkv", h1[0], big("w_in")[:, IN_MAIN:], "nn")[None]
    y_ssd, sv_ssd = ssd_forward(f"l{i}_ssd", proj3, sp)
    y_pool, sv_pool = pool_forward(f"l{i}_pool", proj3, wts["wbd"], wts["pool_scale"])
    y_att, res_att = attention_forward(f"l{i}", qkv3, cs3, sn3)
    mix = jnp.concatenate([y_ssd, y_pool, y_att], axis=-1)
    out, x1 = mm(f"l{i}_wout", mix[0], big("w_out"), "nn", res=x3[0], gate=g1)
    x1 = x1[None]
    h2 = norm_mod_forward(f"l{i}_norm2", x1, wts["norm2_g"], sc2, sh2)
    up3 = mm(f"l{i}_up", h2[0], big("ffn_up"), "nn")[None]
    act, sv_ffn = ffn_mid_forward(f"l{i}_ffn", up3, wts["ffn_conv_w"], wts["ffn_conv_b"])
    dn, x2 = mm(f"l{i}_down", act[0], big("ffn_down"), "nn", res=x1[0], gate=g2)
    keep = dict(x=x3, h1=h1, proj3=proj3, qkv3=qkv3, sv_ssd=sv_ssd, sv_pool=sv_pool, res_att=res_att, mix=mix, out=out[None],
                x1=x1, h2=h2, up3=up3, act=act, sv_ffn=sv_ffn, dn=dn[None])
    return x2[None], keep


def _layer_backward(i, dx2, keep, modv, wts, sp, cs3, sn3, after=None):
    sh1, sc1, g1, sh2, sc2, g2 = modv
    k = keep
    big = lambda n: wts[n]() if callable(wts[n]) else wts[n]
    tell = lambda step, *a: after[step](*a) if after and step in after else None
    d_dn, d_g2 = gate_backward(f"l{i}_gate2_b", k["dn"], g2, dx2)
    d_act = mm(f"l{i}_down_bx", d_dn[0], big("ffn_down"), "nt")
    g_down = mm(f"l{i}_down_bw", k["act"][0], d_dn[0], "tn").reshape(4, FFN_DIM // 4, D_MODEL)
    (d_up,), dv_ffn = ffn_mid_backward(f"l{i}_ffn_b", k["up3"], wts["ffn_conv_w"], wts["ffn_conv_b"], k["sv_ffn"], d_act[None])
    tell("ffn_b")
    d_h2 = mm(f"l{i}_up_bx", d_up[0], big("ffn_up"), "nt")
    g_up = mm(f"l{i}_up_bw", k["h2"][0], d_up[0], "tn", tn=_FFN_CW,
              into=((4, D_MODEL, _FFN_CW), lambda r, c: ((c % 2) * 2 + c // 2, r, 0)))
    dx1, (d_n2, d_sc2, d_sh2) = norm_mod_backward(f"l{i}_norm2_b", k["x1"], wts["norm2_g"], sc2, sh2, d_h2[None], dx2)
    d_out, d_g1 = gate_backward(f"l{i}_gate1_b", k["out"], g1, dx1)
    d_mix = mm(f"l{i}_wout_bx", d_out[0], big("w_out"), "nt")[None]
    g_wout = mm(f"l{i}_wout_bw", k["mix"][0], d_out[0], "tn").reshape(4, D_MODEL // 4, D_MODEL)
    tell("wout_bw", g_wout, g_up, g_down)
    (dz, dxs, dbm, dcm, ddt), dv_ssd = ssd_backward(f"l{i}_ssd_b", k["proj3"], sp, k["sv_ssd"], d_mix)
    tell("ssd_b")
    (du_pool,), (d_wbd, d_pscale) = pool_backward(f"l{i}_pool_b", k["proj3"], wts["wbd"], wts["pool_scale"], k["sv_pool"], d_mix)
    d_qkv = attention_backward(f"l{i}", k["qkv3"], cs3, sn3, k["res_att"], d_mix)
    d_proj = jnp.concatenate([dz[0], dxs[0], dbm[0], dcm[0], du_pool[0], (ddt[0] + ddt[1]).astype(BF16), d_qkv[0]], axis=-1)
    g_win = jnp.stack(_w_in_chip_cols(mm(f"l{i}_proj_bw", k["h1"][0], d_proj, "tn")))
    tell("proj_bw", g_win)
    d_h1 = mm(f"l{i}_proj_bx", d_proj, big("w_in"), "nt")
    tell("proj_bx")
    dx, (d_n1, d_sc1, d_sh1) = norm_mod_backward(f"l{i}_norm1_b", k["x"], wts["norm1_g"], sc1, sh1, d_h1[None], dx1)
    dcwx, dcbx, dcwb, dcbb, dcwc, dcbc, ddtb, dalog, ddsk, dng = dv_ssd
    small = dict(
        norm1_g=d_n1[0], norm2_g=d_n2[0],
        ssd_conv_w=jnp.concatenate([dcwx[:, :512], dcwb[:, 512:768], dcwc[:, 768:]], axis=1),
        ssd_conv_b=jnp.concatenate([dcbx[0, :512], dcbb[0, 512:768], dcbc[0, 768:]]),
        ssd_dt_bias=ddtb[0, :8], ssd_a_log=dalog[0, :8], ssd_d=ddsk[0, :8], ssd_norm_g=dng[0],
        pool_w=jnp.stack([d_wbd[64 * g:64 * g + 64, 64 * g:64 * g + 64] for g in range(4)]), pool_scale=d_pscale[0],
        ffn_conv_w=_ffn_block_perm(dv_ffn[0]), ffn_conv_b=_ffn_block_perm(dv_ffn[1][0]),
    )
    dmod = jnp.concatenate([d_sh1[0], d_sc1[0], d_g1[0], d_sh2[0], d_sc2[0], d_g2[0]])
    return dx, [g_win, g_wout, g_up, g_down], small, dmod


def kernel(x, c, positions, ada_w, ada_b, norm1_g, w_in, ssd_conv_w, ssd_conv_b, ssd_dt_bias, ssd_a_log, ssd_d, ssd_norm_g, pool_w, pool_scale, w_out, norm2_g, ffn_up, ffn_conv_w, ffn_conv_b, ffn_down, final_g, loss_target, m_ada_w, m_ada_b, m_norm1_g, m_w_in, m_ssd_conv_w, m_ssd_conv_b, m_ssd_dt_bias, m_ssd_a_log, m_ssd_d, m_ssd_norm_g, m_pool_w, m_pool_scale, m_w_out, m_norm2_g, m_ffn_up, m_ffn_conv_w, m_ffn_conv_b, m_ffn_down, m_final_g, v_ada_w, v_ada_b, v_norm1_g, v_w_in, v_ssd_conv_w, v_ssd_conv_b, v_ssd_dt_bias, v_ssd_a_log, v_ssd_d, v_ssd_norm_g, v_pool_w, v_pool_scale, v_w_out, v_norm2_g, v_ffn_up, v_ffn_conv_w, v_ffn_conv_b, v_ffn_down, v_final_g):
    args = dict(locals())
    w = {n: args[n] for n in _WEIGHTS}
    m = {n: args["m_" + n] for n in _WEIGHTS}
    v = {n: args["v_" + n] for n in _WEIGHTS}
    d = D_MODEL
    me = (lax.axis_index("x"), lax.axis_index("y"), lax.axis_index("c"))
    chip, dev = _chip(me), _dev(me)
    RIDERS.reset()

    shapes0 = [c.shape, ssd_conv_w.shape, ffn_conv_w.shape]
    g0 = allgather8("gather_c_conv", _pack([c, ssd_conv_w, ffn_conv_w]))
    c16 = jnp.pad(g0[:, :d // 128, :].reshape(8, d), ((0, 8), (0, 0)))
    by_chip = [_unpack(g0[2 * j], shapes0) for j in range(4)]
    conv_w_full = jnp.concatenate([p[1] for p in by_chip], axis=-1)
    fconv_w_full = jnp.concatenate([p[2] for p in by_chip], axis=-1)

    modp = ada_forward("ada_fwd", c16, ada_w)[:, :8]
    g1 = allgather8("gather_mod", _pack([modp]))
    modfull = jnp.concatenate([_unpack(g1[2 * j], [modp.shape])[0] for j in range(4)], axis=-1)
    mod = lax.dynamic_index_in_dim(modfull, dev, axis=1, keepdims=False) + ada_b
    modv = [[mod[i, q * d:(q + 1) * d][None] for q in range(6)] for i in range(DEPTH)]

    shards = [w[n].astype(BF16) for n in _BIG]

    def weight(k, layer, got):
        parts = [jnp.where(chip == j, shards[k][layer], got[j]) for j in range(4)]
        if k == 0:
            return _w_in_from_chips(parts)
        return jnp.concatenate([parts[j] for j in FFN_BLOCK_ORDER], axis=1) if k == 2 else jnp.concatenate(parts, axis=0)

    def later(k, layer, *sources):
        made = []

        def get():
            if not made:
                got = [RIDERS.result(host)[pos] for host, pos in sources]
                made.append(weight(k, layer, got[0] if len(got) == 1 else jnp.concatenate(got, axis=1)))
            return made[0]
        return get

    cs3, sn3 = rope_tables(positions[0])
    eye4 = jnp.eye(4, dtype=F32)
    wts, sps = [], []
    for i in range(DEPTH):
        wts.append(dict(
            norm1_g=norm1_g[i][None], norm2_g=norm2_g[i][None], pool_scale=pool_scale[i][None],
            wbd=(eye4[:, None, :, None] * pool_w[i][:, :, None, :]).reshape(POOL_W, POOL_W),
            ffn_conv_w=_ffn_block_perm(fconv_w_full[i]), ffn_conv_b=_ffn_block_perm(ffn_conv_b[i])[None]))
        sps.append(dict(cw=conv_w_full[i], cb=ssd_conv_b[i][None], dtb=_pad_lanes(ssd_dt_bias[i]), alog=_pad_lanes(ssd_a_log[i]),
                        dsk=_pad_lanes(ssd_d[i]), ng=ssd_norm_g[i][None]))

    (w_in0,) = ride_alone("gather_w_in0", gather_ride(0, [shards[0]]))
    RIDERS.book("l0_ssd", gather_ride(0, [shards[1], shards[3]]))
    half = shards[2].shape[1] // 2
    RIDERS.book("l0_attn0", gather_ride(0, [shards[2][:, :half]]))
    RIDERS.book("l0_attn1", gather_ride(0, [shards[2][:, half:]]))
    wts[0].update(w_in=weight(0, 0, w_in0), w_out=later(1, 0, ("l0_ssd", 0)), ffn_down=later(3, 0, ("l0_ssd", 1)),
                  ffn_up=later(2, 0, ("l0_attn0", 0), ("l0_attn1", 0)))
    RIDERS.book("l0_attn2", gather_ride(1, [shards[0], shards[1]]))
    RIDERS.book("l0_ffn", gather_ride(1, [shards[2]]))
    RIDERS.book("l0_down", gather_ride(1, [shards[3]]))
    wts[1].update(w_in=later(0, 1, ("l0_attn2", 0)), w_out=later(1, 1, ("l0_attn2", 1)), ffn_up=later(2, 1, ("l0_ffn", 0)),
                  ffn_down=later(3, 1, ("l0_down", 0)))
    x1_, keep0 = _layer_forward(0, x, modv[0], wts[0], sps[0], cs3, sn3)
    xc, keep1 = _layer_forward(1, x1_, modv[1], wts[1], sps[1], cs3, sn3)
    keeps = [keep0, keep1]
    lossblk, dx, d_final = final_loss("final_loss", xc, loss_target, final_g[None])
    loss = lax.psum(lossblk[0, 0], ("x", "y", "c"))

    small_g, dmods = [None] * DEPTH, [None] * DEPTH
    part_sum, from_chips = [[None] * 4 for _ in range(DEPTH)], [[None] * 4 for _ in range(DEPTH)]

    def owner_sum(layer, ks, mine, theirs):
        for k, g, t in zip(ks, mine, theirs):
            part_sum[layer][k] = add_arrays(f"sum_cores{layer}_{_BIG[k]}", [g, t], BF16)

    dx, by_chip1, small_g[1], dmods[1] = _layer_backward(1, dx, keeps[1], modv[1], wts[1], sps[1], cs3, sn3)
    RIDERS.book("l0_ffn_b", to_owner_ride(1, by_chip1))

    def after_ffn_b():
        owner_sum(1, range(4), by_chip1, RIDERS.result("l0_ffn_b"))
        RIDERS.book("l0_up_bx", scatter_ride(1, [part_sum[1][2]]))
        RIDERS.book("l0_up_bw", scatter_ride(1, [part_sum[1][0], part_sum[1][1]]))
        RIDERS.book("l0_norm2_b", scatter_ride(1, [part_sum[1][3]]))

    early = []

    def after_wout_bw(g_wout, g_up, g_down):
        early.extend([g_wout, g_up, g_down])
        RIDERS.book("l0_ssd_b", to_owner_ride(0, early))

    def after_ssd_b():
        owner_sum(0, [1, 2, 3], early, RIDERS.result("l0_ssd_b"))
        for host, k in (("l0_attn0_b", 2), ("l0_attn1_b", 3), ("l0_attn2_b", 1)):
            RIDERS.book(host, scatter_ride(0, [part_sum[0][k]]))

    last = []

    def after_proj_bw(g_win):
        last.append(g_win)
        RIDERS.book("l0_proj_bx", to_owner_ride(0, last))

    def after_proj_bx():
        owner_sum(0, [0], last, RIDERS.result("l0_proj_bx"))
        RIDERS.book("l0_norm1_b", scatter_ride(0, [part_sum[0][0]]))

    hooks = dict(ffn_b=after_ffn_b, wout_bw=after_wout_bw, ssd_b=after_ssd_b, proj_bw=after_proj_bw, proj_bx=after_proj_bx)
    dx, _, small_g[0], dmods[0] = _layer_backward(0, dx, keeps[0], modv[0], wts[0], sps[0], cs3, sn3, after=hooks)
    from_chips[1][2], (from_chips[1][0], from_chips[1][1]) = RIDERS.result("l0_up_bx")[0], RIDERS.result("l0_up_bw")
    from_chips[1][3] = RIDERS.result("l0_norm2_b")[0]
    for host, k in (("l0_attn0_b", 2), ("l0_attn1_b", 3), ("l0_attn2_b", 1), ("l0_norm1_b", 0)):
        from_chips[0][k] = RIDERS.result(host)[0]
    mine = [sum_chips_mine(f"sum_chips_{n}", part_sum[0][k], from_chips[0][k], part_sum[1][k], from_chips[1][k])
            for k, n in enumerate(_BIG)]
    reduced = swap_layers("swap_r", mine, me[2])
    grads = {n: jnp.stack(r) for n, r in zip(_BIG, reduced)}

    part = dict(ada_b=jnp.stack(dmods), final_g=d_final[0])
    for n in _SMALL:
        if n not in part:
            part[n] = jnp.stack([small_g[i][n] for i in range(DEPTH)])
    full_shapes = [part[n].shape for n in _SMALL]
    gs = allgather8("gather_small", _pack([part[n] for n in _SMALL]))
    tot = _unpack(sum_slots("sum_small", gs, 8)[0], full_shapes)
    small_tot = dict(zip(_SMALL, tot))
    dmod_all = gs[:, :DEPTH * 6 * d // 128, :].reshape(8, DEPTH, 6 * d)
    for n, ncol in _COL_SHARDED_SMALL.items():
        small_tot[n] = lax.dynamic_slice_in_dim(small_tot[n], chip * ncol, ncol, axis=2)
    grads.update(small_tot)

    ncol = ada_w.shape[2]
    dm = lax.dynamic_slice_in_dim(dmod_all, chip * ncol, ncol, axis=2).transpose(1, 0, 2)
    upd = {}
    g_ada, *upd["ada_w"] = ada_backward("ada_bwd", c16, jnp.pad(dm, ((0, 0), (0, 8), (0, 0))), ada_w, m["ada_w"], v["ada_w"])
    grads["ada_w"] = g_ada

    for n in _BIG:
        upd[n] = adamw(f"adam_{n}", w[n], grads[n], m[n], v[n])
    shapes_s = [w[n].shape for n in _SMALL]
    packed = [_pack([src[n] for n in _SMALL]) for src in (w, grads, m, v)]
    outs_s = [_unpack(o, shapes_s) for o in adamw("adam_small", *packed)]
    for q, n in enumerate(_SMALL):
        upd[n] = [outs_s[0][q], outs_s[1][q], outs_s[2][q]]

    return (loss, dx, *[grads[n] for n in _WEIGHTS], *[upd[n][0] for n in _WEIGHTS], *[upd[n][1] for n in _WEIGHTS],
            *[upd[n][2] for n in _WEIGHTS])


class Ride:
    def __init__(self, ins, out_shapes, nsem, start, finish):
        self.ins, self.out_shapes, self.nsem, self.start, self.finish = ins, out_shapes, nsem, start, finish

    def specs(self):
        hbm = pl.BlockSpec(memory_space=pl.ANY)
        return [hbm] * len(self.ins), [hbm] * len(self.out_shapes), [pltpu.SemaphoreType.DMA((self.nsem,))] * 2

    def begin(self, in_refs, out_refs, sems, cond=None):
        me = (lax.axis_index("x"), lax.axis_index("y"), lax.axis_index("c"))
        go = lambda: self.start(in_refs, out_refs, sems[0], sems[1], me)
        go() if cond is None else pl.when(cond)(go)

    def end(self, in_refs, out_refs, sems, cond=None):
        me = (lax.axis_index("x"), lax.axis_index("y"), lax.axis_index("c"))
        go = lambda: self.finish(in_refs, out_refs, sems[0], sems[1], me)
        go() if cond is None else pl.when(cond)(go)


def ride_alone(name, ride):
    ni, no = len(ride.ins), len(ride.out_shapes)

    def body(*refs):
        ride.begin(refs[:ni], refs[ni:ni + no], refs[ni + no:])
        ride.end(refs[:ni], refs[ni:ni + no], refs[ni + no:])

    in_specs, out_specs, scratch = ride.specs()
    return list(pl.pallas_call(body, name=name, in_specs=in_specs, out_specs=out_specs, out_shape=ride.out_shapes,
                               scratch_shapes=scratch)(*ride.ins))


def mm(name, a, b, mode, out_dtype=F32, res=None, gate=None, tm=1408, tn=1536, tk=1408, into=None):
    ride = RIDERS.take(name)
    if mode == "nn":
        (m, k), n = a.shape, b.shape[1]
    elif mode == "nt":
        (m, k), n = a.shape, b.shape[0]
    else:
        (k, m), n = a.shape, b.shape[1]
    tm, tn, tk = _tile(m, tm), _tile(n, tn), _tile(k, tk)
    ni, nj, nk = m // tm, n // tn, k // tk
    a_spec = pl.BlockSpec((tk, tm), lambda i, j, q: (q, i)) if mode == "tn" else pl.BlockSpec((tm, tk), lambda i, j, q: (i, q))
    b_spec = pl.BlockSpec((tn, tk), lambda i, j, q: (j, q)) if mode == "nt" else pl.BlockSpec((tk, tn), lambda i, j, q: (q, j))
    o_spec = pl.BlockSpec((tm, tn), lambda i, j, q: (i, j))
    fused = res is not None
    lead = 0 if into is None else len(into[0]) - 2
    first = (0,) * lead + (slice(None), slice(None))
    ins, in_specs = [a, b], [a_spec, b_spec]
    out_shape, out_specs = [jax.ShapeDtypeStruct((m, n), out_dtype)], [o_spec]
    if fused:
        ins += [res, gate]
        in_specs += [o_spec, pl.BlockSpec((1, tn), lambda i, j, q: (0, j))]
        out_shape.append(jax.ShapeDtypeStruct((m, n), F32))
        out_specs.append(o_spec)
    if into is not None:
        shape, omap = into
        out_shape = [jax.ShapeDtypeStruct(shape, out_dtype)]
        out_specs = [pl.BlockSpec((1,) * lead + (tm, tn), lambda i, j, q: omap(i, j))]
    n_in, n_out = len(ins), len(out_shape)
    scratch = [pltpu.VMEM((tm, tn), F32)]
    if ride is not None:
        r_in, r_out, r_scr = ride.specs()
        ins, in_specs = ins + list(ride.ins), in_specs + r_in
        out_shape, out_specs = out_shape + list(ride.out_shapes), out_specs + r_out
        scratch = scratch + r_scr

    def body(*refs):
        a_ref, b_ref = refs[:2]
        o_ref = refs[len(ins)]
        acc = refs[len(ins) + len(out_shape)]
        i, j, q = pl.program_id(0), pl.program_id(1), pl.program_id(2)
        at = lambda x, y, z: jnp.logical_and(jnp.logical_and(i == x, j == y), q == z)
        r_refs = (refs[n_in:len(ins)], refs[len(ins) + n_out:len(ins) + len(out_shape)], refs[len(ins) + len(out_shape) + 1:])
        if ride is not None:
            ride.begin(*r_refs, at(0, 0, 0))

        def finish(total):
            o_ref[first] = total.astype(o_ref.dtype)
            if fused:
                refs[len(ins) + 1][...] = refs[2][...] + refs[3][...] * total

        part = _mxu(a_ref[...], b_ref[...], mode)
        if nk == 1:
            finish(part)
        else:
            pl.when(q == 0)(lambda: acc.__setitem__(Ellipsis, part))
            if nk > 2:
                @pl.when(jnp.logical_and(q > 0, q < nk - 1))
                def _():
                    acc[...] += part
            pl.when(q == nk - 1)(lambda: finish(acc[...] + part))

        if ride is not None:
            ride.end(*r_refs, at(ni - 1, nj - 1, nk - 1))

    sem = ("arbitrary",) * 3 if ride is not None else ("parallel", "parallel", "arbitrary")
    out = pl.pallas_call(
        body, name=name, grid=(ni, nj, nk), in_specs=in_specs, out_specs=out_specs, out_shape=out_shape, scratch_shapes=scratch,
        compiler_params=pltpu.CompilerParams(dimension_semantics=sem, vmem_limit_bytes=VMEM_LIMIT_BYTES),
    )(*ins)
    if ride is not None:
        RIDERS.done[name] = list(out[n_out:])
    return tuple(out[:n_out]) if fused else out[0]


def add_arrays(name, arrs, out_dtype=F32):
    nb, r, c = arrs[0].shape
    t = _tile(r, 256, 8)
    (out,), _ = scan_fwd(name, _sum_fn, nb=nb, nchunk=r // t, t=t, rows=[Row(a, fb=lambda b: b) for a in arrs], vecs=[], carries=[],
                         outs=[out_row((nb, r, c), out_dtype, fb=lambda b: b)], save=False)
    return out


def _sum_chips_mine_fn(ci, b, carries, rows, vecs):
    mine_layer = lax.axis_index("c")
    chip = 2 * lax.axis_index("x") + lax.axis_index("y")
    tot = None
    for j in range(4):
        own = jnp.where(mine_layer == 0, rows[j], rows[8 + j])
        sent = jnp.where(mine_layer == 0, rows[4 + j], rows[12 + j])
        term = jnp.where(chip == j, own, sent)
        tot = term if tot is None else tot + term
    return [], [tot]


def sum_chips_mine(name, p0, q0, p1, q1):
    _, r, c = p0.shape
    t = _tile(r, 256, 8)
    rows = [Row(a, fb=(lambda b, j=j: j)) for a in (p0, q0, p1, q1) for j in range(4)]
    (out,), _ = scan_fwd(name, _sum_chips_mine_fn, nb=1, nchunk=r // t, t=t, rows=rows, vecs=[], carries=[],
                         outs=[out_row((1, r, c))], save=False)
    return out[0]


def _remote(src, dst, send_sems, recv_sems, k, to):
    return pltpu.make_async_remote_copy(src_ref=src, dst_ref=dst, send_sem=send_sems.at[k], recv_sem=recv_sems.at[k],
                                        device_id=to, device_id_type=MESH)


def gather_ride(layer, shards):
    na = len(shards)

    def start(ins, outs, ss, rs, me):
        @pl.when(me[2] == layer)
        def _():
            for k in range(na):
                for p, mask in enumerate(CHIP_PEERS):
                    _remote(ins[k].at[layer], outs[k].at[_chip(me)], ss, rs, 6 * k + p, _flip(mask, me)).start()

    def finish(ins, outs, ss, rs, me):
        sibling = _flip(SIBLING[0], me)

        @pl.when(me[2] == layer)
        def _():
            for k in range(na):
                for p, mask in enumerate(CHIP_PEERS):
                    slot = outs[k].at[_chip(_flip(mask, me))]
                    _remote(ins[k].at[layer], slot, ss, rs, 6 * k + p, _flip(mask, me)).wait_recv()
                    _remote(slot, slot, ss, rs, 6 * k + 3 + p, sibling).start()
            for k in range(na):
                for p, mask in enumerate(CHIP_PEERS):
                    slot = outs[k].at[_chip(_flip(mask, me))]
                    _remote(ins[k].at[layer], slot, ss, rs, 6 * k + p, _flip(mask, me)).wait_send()
                    _remote(slot, slot, ss, rs, 6 * k + 3 + p, sibling).wait_send()

        @pl.when(me[2] != layer)
        def _():
            for k in range(na):
                for p, mask in enumerate(CHIP_PEERS):
                    slot = outs[k].at[_chip(_flip(mask, me))]
                    _remote(slot, slot, ss, rs, 6 * k + 3 + p, sibling).wait_recv()

    return Ride(list(shards), [jax.ShapeDtypeStruct((4,) + a.shape[1:], a.dtype) for a in shards], 6 * na, start, finish)


def scatter_ride(layer, parts):
    na = len(parts)

    def start(ins, outs, ss, rs, me):
        @pl.when(me[2] == layer)
        def _():
            for k in range(na):
                for p, mask in enumerate(CHIP_PEERS):
                    peer = _flip(mask, me)
                    _remote(ins[k].at[_chip(peer)], outs[k].at[_chip(me)], ss, rs, 3 * k + p, peer).start()

    def finish(ins, outs, ss, rs, me):
        @pl.when(me[2] == layer)
        def _():
            for k in range(na):
                for p, mask in enumerate(CHIP_PEERS):
                    peer = _flip(mask, me)
                    _remote(ins[k].at[_chip(peer)], outs[k].at[_chip(peer)], ss, rs, 3 * k + p, peer).wait_recv()
                    _remote(ins[k].at[_chip(peer)], outs[k].at[_chip(me)], ss, rs, 3 * k + p, peer).wait_send()

    return Ride(list(parts), [jax.ShapeDtypeStruct(a.shape, a.dtype) for a in parts], 3 * na, start, finish)


def to_owner_ride(layer, arrays):
    na = len(arrays)

    def start(ins, outs, ss, rs, me):
        @pl.when(me[2] != layer)
        def _():
            for k in range(na):
                _remote(ins[k], outs[k], ss, rs, k, _flip(SIBLING[0], me)).start()

    def finish(ins, outs, ss, rs, me):
        for k in range(na):
            cp = _remote(ins[k], outs[k], ss, rs, k, _flip(SIBLING[0], me))
            pl.when(me[2] != layer)(cp.wait_send)
            pl.when(me[2] == layer)(cp.wait_recv)

    return Ride(list(arrays), [jax.ShapeDtypeStruct(a.shape, a.dtype) for a in arrays], na, start, finish)


def _w_in_from_chips(a):
    c2 = a[2]
    pad = jnp.zeros((c2.shape[0], IN_WP - IN_W), c2.dtype)
    return jnp.concatenate([a[0], a[1], c2[:, :252], c2[:, 260:516], c2[:, 252:260], pad, c2[:, 516:], a[3]], axis=1)


def _mm_host(rides, rode, key, *args, **kw):
    if rides is None or key not in rides:
        return mm(*args, **kw)
    main, rode[key] = mm(*args, ride=rides[key], **kw)
    return main
```

```python
import functools
import math

import numpy as np
import jax
import jax.numpy as jnp
from jax import lax
from jax.experimental import pallas as pl
from jax.experimental.pallas import tpu as pltpu

F32 = jnp.float32
BF16 = jnp.bfloat16
HI = lax.Precision.HIGHEST
MESH = pl.DeviceIdType.MESH

D_MODEL = 1024
SEQ = 4096
DEPTH = 2
SSD_INNER = 512
SSD_HEADS = 8
SSD_STATE = 128
POOL_W = 256
POOL_WINDOWS = (2, 4, 8, 16)
ATT_W = 256
ATT_HEADS = 4
ATT_HEAD_DIM = 64
ATT_PATTERNS = ((128, 1), (512, 4), (2048, 16))
ATT_BLOCK = 128
ROT_DIM = 16
ROPE_THETA = 500000.0
IN_W = 2568
IN_WP = 2688
IN_MAIN = 1920
FFN_DIM = 2816
NORM_EPS = 1e-6
ADAM_LR, ADAM_B1, ADAM_B2, ADAM_EPS, ADAM_WD, ADAM_STEP = 0.001, 0.9, 0.999, 1e-08, 0.01, 10

VMEM_LIMIT_BYTES = 56 * 1024 * 1024
NEG = -1e30


def _mxu(a, b, mode):
    dims = {"nn": ((1,), (0,)), "nt": ((1,), (1,)), "tn": ((0,), (0,))}[mode]
    return lax.dot_general(a.astype(BF16), b.astype(BF16), (dims, ((), ())), preferred_element_type=F32)


@functools.partial(jax.custom_vjp, nondiff_argnums=(2,))
def _bdot(a, b, mode):
    return _mxu(a, b, mode)


def _bdot_fwd(a, b, mode):
    return _mxu(a, b, mode), (a, b)


def _bdot_bwd(mode, res, g):
    a, b = res
    if mode == "nn":
        return _mxu(g, b, "nt"), _mxu(a, g, "tn")
    if mode == "nt":
        return _mxu(g, b, "nn"), _mxu(g, a, "tn")
    return _mxu(b, g, "nt"), _mxu(a, g, "nn")


_bdot.defvjp(_bdot_fwd, _bdot_bwd)


def _fxu(a, b, mode):
    dims = {"nn": ((1,), (0,)), "nt": ((1,), (1,)), "tn": ((0,), (0,))}[mode]
    return lax.dot_general(a, b, (dims, ((), ())), precision=HI, preferred_element_type=F32)


@functools.partial(jax.custom_vjp, nondiff_argnums=(2,))
def _fdot(a, b, mode):
    return _fxu(a, b, mode)


def _fdot_fwd(a, b, mode):
    return _fxu(a, b, mode), (a, b)


def _fdot_bwd(mode, res, g):
    a, b = res
    if mode == "nn":
        return _fxu(g, b, "nt"), _fxu(a, g, "tn")
    if mode == "nt":
        return _fxu(g, b, "nn"), _fxu(g, a, "tn")
    return _fxu(b, g, "nt"), _fxu(a, g, "nn")


_fdot.defvjp(_fdot_fwd, _fdot_bwd)


def _iota(shape, dim):
    return lax.broadcasted_iota(jnp.int32, shape, dim)


def _make_shift(h):
    @functools.partial(jax.custom_vjp, nondiff_argnums=(2,))
    def shift(halo, cur, k):
        if k == 0:
            return cur
        full = jnp.concatenate([halo, cur], axis=0)
        return pltpu.roll(full, k, 0)[h:]

    def fwd(halo, cur, k):
        return shift(halo, cur, k), None

    def bwd(k, _, g):
        t, w = g.shape
        if k == 0:
            return jnp.zeros((h, w), F32), g
        d_cur = jnp.where(_iota((t, w), 0) < t - k, pltpu.roll(g, t - k, 0), 0.0)
        top = g[:h]
        d_halo = jnp.where(_iota((h, w), 0) >= h - k, pltpu.roll(top, h - k, 0) if k < h else top, 0.0)
        return d_halo, d_cur

    shift.defvjp(fwd, bwd)
    return shift


_shift8 = _make_shift(8)
_shift16 = _make_shift(16)


def _make_tail(h):
    @jax.custom_vjp
    def tail(x):
        return x[x.shape[0] - h:]

    def fwd(x):
        return tail(x), x.shape[0]

    def bwd(t, g):
        return (jnp.concatenate([jnp.zeros((t - h, g.shape[1]), F32), g], axis=0),)

    tail.defvjp(fwd, bwd)
    return tail


_tail8 = _make_tail(8)
_tail16 = _make_tail(16)


@jax.custom_vjp
def _cumsum_rows(x):
    t = x.shape[0]
    row, s = _iota(x.shape, 0), 1
    while s < t:
        x = x + jnp.where(row >= s, pltpu.roll(x, s, 0), 0.0)
        s *= 2
    return x


def _cumsum_rows_fwd(x):
    return _cumsum_rows(x), None


def _cumsum_rows_bwd(_, g):
    t = g.shape[0]
    row, s = _iota(g.shape, 0), 1
    while s < t:
        g = g + jnp.where(row < t - s, pltpu.roll(g, t - s, 0), 0.0)
        s *= 2
    return (g,)


_cumsum_rows.defvjp(_cumsum_rows_fwd, _cumsum_rows_bwd)


@jax.custom_vjp
def _rot_pairs(t):
    e = _iota(t.shape, 1) % ATT_HEAD_DIM
    n = t.shape[1]
    return jnp.where(e < 8, -pltpu.roll(t, n - 8, 1), jnp.where(e < 16, pltpu.roll(t, 8, 1), 0.0))


def _rot_pairs_fwd(t):
    return _rot_pairs(t), None


def _rot_pairs_bwd(_, g):
    e = _iota(g.shape, 1) % ATT_HEAD_DIM
    n = g.shape[1]
    return (pltpu.roll(jnp.where(e < 8, -g, 0.0), 8, 1) + pltpu.roll(jnp.where(jnp.logical_and(e >= 8, e < 16), g, 0.0), n - 8, 1),)


_rot_pairs.defvjp(_rot_pairs_fwd, _rot_pairs_bwd)


def _make_thirds():
    @jax.custom_vjp
    def thirds(x):
        w = x.shape[1] // 3
        return x[:, :w], x[:, w:2 * w], x[:, 2 * w:]

    def fwd(x):
        return thirds(x), None

    def bwd(_, g):
        return (jnp.concatenate(g, axis=1),)

    thirds.defvjp(fwd, bwd)
    return thirds


_thirds = _make_thirds()


def _rowk(w, k):
    return jnp.sum(jnp.where(_iota(w.shape, 0) == k, w, 0.0), axis=0, keepdims=True)


def _silu(x):
    return x * (0.5 * jnp.tanh(0.5 * x) + 0.5)


def _softplus(x):
    return jnp.maximum(x, 0.0) + jnp.log(1.0 + jnp.exp(-jnp.abs(x)))


def _tile(dim, target, unit=128):
    if dim <= target:
        return dim
    best = None
    for t in range(unit, target + 1, unit):
        if dim % t == 0:
            best = t
    assert best is not None, (dim, target)
    return best


class Ride:
    def __init__(self, ins, out_shapes, nsem, start, finish):
        self.ins, self.out_shapes, self.nsem, self.start, self.finish = ins, out_shapes, nsem, start, finish

    def specs(self):
        hbm = pl.BlockSpec(memory_space=pl.ANY)
        return [hbm] * len(self.ins), [hbm] * len(self.out_shapes), [pltpu.SemaphoreType.DMA((self.nsem,))] * 2

    def begin(self, in_refs, out_refs, sems, cond=None):
        me = (lax.axis_index("x"), lax.axis_index("y"), lax.axis_index("c"))
        go = lambda: self.start(in_refs, out_refs, sems[0], sems[1], me)
        go() if cond is None else pl.when(cond)(go)

    def end(self, in_refs, out_refs, sems, cond=None):
        me = (lax.axis_index("x"), lax.axis_index("y"), lax.axis_index("c"))
        go = lambda: self.finish(in_refs, out_refs, sems[0], sems[1], me)
        go() if cond is None else pl.when(cond)(go)


class _Riders:
    def reset(self):
        self.booked, self.done = {}, {}

    def book(self, host, ride):
        assert host not in self.booked, host
        self.booked[host] = ride

    def take(self, host):
        return self.booked.pop(host, None)

    def result(self, host):
        return self.done[host]


RIDERS = _Riders()
RIDERS.reset()


class Row:
    def __init__(self, arr, w=None, fb=None, fc=None, diff=True, slot=False, dcols=None, dfc=None, ddtype=F32, view=None):
        self.ddtype = ddtype
        self.view = view
        self.arr = arr
        self.w = arr.shape[2] if w is None else w
        self.fb = (lambda b: 0) if fb is None else fb
        self.fc = (lambda b: 0) if fc is None else fc
        self.diff = diff
        self.slot = slot
        self.dcols = dcols
        self.dfc = dfc


class Vec:
    def __init__(self, arr, w=None, fc=None, diff=True):
        self.arr = arr
        self.w = arr.shape[1] if w is None else w
        self.fc = fc
        self.diff = diff


def _row_spec(r, t, nchunk, reverse):
    shape = (1, t, r.w) if r.view is None else (1, t // r.view, r.view * r.w)
    if reverse:
        return pl.BlockSpec(shape, lambda b, i, r=r: (r.fb(b), nchunk - 1 - i, r.fc(b)))
    return pl.BlockSpec(shape, lambda b, i, r=r: (r.fb(b), i, r.fc(b)))


def _load_row(ref, r, t, scr):
    if r.view is None:
        return ref[0]
    d, w = r.view, r.w
    for q in range(d):
        for j in range(w // 128):
            scr[j, pl.ds(q, t // d, stride=d), :] = ref[0, :, q * w + 128 * j:q * w + 128 * (j + 1)].astype(F32)
    return jnp.concatenate([scr[j] for j in range(w // 128)], axis=1)


def _store_row(ref, r, t, scr, val):
    if r.view is None:
        ref[0] = val.astype(ref.dtype)
        return
    d, w = r.view, r.w
    for j in range(w // 128):
        scr[j] = val[:, 128 * j:128 * (j + 1)]
    for q in range(d):
        for j in range(w // 128):
            ref[0, :, q * w + 128 * j:q * w + 128 * (j + 1)] = scr[j, pl.ds(q, t // d, stride=d), :].astype(ref.dtype)


def _view_scratch(specs, t):
    ws = [r.w for r in specs if r.view is not None]
    return [pltpu.VMEM((max(ws) // 128, t, 128), F32)] if ws else []


def _vec_spec(v):
    if v.fc is None:
        return pl.BlockSpec(v.arr.shape, lambda b, i: (0, 0))
    return pl.BlockSpec((v.arr.shape[0], v.w), lambda b, i, v=v: (0, v.fc(b)))


def _cparams():
    return pltpu.CompilerParams(dimension_semantics=("arbitrary", "arbitrary"), vmem_limit_bytes=VMEM_LIMIT_BYTES)


def scan_fwd(name, fn, *, nb, nchunk, t, rows, vecs, carries, outs, save):
    nr, nv, nc, no = len(rows), len(vecs), len(carries), len(outs)
    ns = nc if save else 0
    ride = RIDERS.take(name)
    r_in, r_out, r_scr = ride.specs() if ride else ([], [], [])

    def body(*refs):
        p = 0
        row_refs = refs[p:p + nr]; p += nr
        vec_refs = refs[p:p + nv]; p += nv
        ride_in = refs[p:p + len(r_in)]; p += len(r_in)
        out_refs = refs[p:p + no]; p += no
        save_refs = refs[p:p + ns]; p += ns
        ride_out = refs[p:p + len(r_out)]; p += len(r_out)
        car = refs[p:p + nc]; p += nc
        scr = refs[p] if stage else None
        sems = refs[p + len(stage):]
        b, i = pl.program_id(0), pl.program_id(1)
        if ride:
            ride.begin(ride_in, ride_out, sems, jnp.logical_and(b == 0, i == 0))
        if nc:
            @pl.when(i == 0)
            def _():
                for c_ref in car:
                    c_ref[...] = jnp.zeros(c_ref.shape, F32)
        cin = [c_ref[...] for c_ref in car]
        if save:
            for s_ref, cv in zip(save_refs, cin):
                s_ref[0, 0] = cv
        new_c, o = fn(i, b, cin, [_load_row(ref, r, t, scr) for ref, r in zip(row_refs, rows)], [v[...] for v in vec_refs])
        for c_ref, cv in zip(car, new_c):
            c_ref[...] = cv
        for o_ref, spec, ov in zip(out_refs, outs, o):
            _store_row(o_ref, spec, t, scr, ov)
        if ride:
            ride.end(ride_in, ride_out, sems, jnp.logical_and(b == nb - 1, i == nchunk - 1))

    stage = _view_scratch(list(rows) + list(outs), t)
    out_shape = [o.arr for o in outs]
    out_specs = [_row_spec(o, t, nchunk, False) for o in outs]
    if save:
        for cs in carries:
            out_shape.append(jax.ShapeDtypeStruct((nb, nchunk) + tuple(cs), F32))
            out_specs.append(pl.BlockSpec((1, 1) + tuple(cs), lambda b, i: (b, i, 0, 0)))
    res = pl.pallas_call(
        body, name=name, grid=(nb, nchunk),
        in_specs=[_row_spec(r, t, nchunk, False) for r in rows] + [_vec_spec(v) for v in vecs] + r_in,
        out_specs=out_specs + r_out, out_shape=out_shape + (list(ride.out_shapes) if ride else []),
        scratch_shapes=[pltpu.VMEM(tuple(cs), F32) for cs in carries] + stage + r_scr,
        compiler_params=_cparams(),
    )(*[r.arr for r in rows], *[v.arr for v in vecs], *(ride.ins if ride else []))
    if ride:
        RIDERS.done[name] = list(res[no + ns:])
    return list(res[:no]), list(res[no:no + ns])


def scan_bwd(name, fn, *, nb, nchunk, t, rows, vecs, carries, saved, douts, adds=None):
    adds = adds or {}
    nr, nv, nc, no = len(rows), len(vecs), len(carries), len(douts)
    dri = [k for k, r in enumerate(rows) if r.diff]
    dvi = [k for k, v in enumerate(vecs) if v.diff]
    add_keys = sorted(adds)
    na = len(add_keys)
    ride = RIDERS.take(name)
    r_in, r_out, r_scr = ride.specs() if ride else ([], [], [])

    def body(*refs):
        p = 0
        row_refs = refs[p:p + nr]; p += nr
        vec_refs = refs[p:p + nv]; p += nv
        save_refs = refs[p:p + nc]; p += nc
        dout_refs = refs[p:p + no]; p += no
        add_refs = refs[p:p + na]; p += na
        ride_in = refs[p:p + len(r_in)]; p += len(r_in)
        drow_refs = refs[p:p + len(dri)]; p += len(dri)
        dvec_refs = refs[p:p + len(dvi)]; p += len(dvi)
        ride_out = refs[p:p + len(r_out)]; p += len(r_out)
        dcar = refs[p:p + nc]; p += nc
        scr = refs[p] if stage else None
        sems = refs[p + len(stage):]
        b, ir = pl.program_id(0), pl.program_id(1)
        ci = nchunk - 1 - ir
        if ride:
            ride.begin(ride_in, ride_out, sems, jnp.logical_and(b == 0, ir == 0))
        if nc:
            @pl.when(ir == 0)
            def _():
                for c_ref in dcar:
                    c_ref[...] = jnp.zeros(c_ref.shape, F32)
        rows_v = [_load_row(ref, r, t, scr) for ref, r in zip(row_refs, rows)]
        vecs_v = [v[...] for v in vec_refs]
        cin = [s[0, 0] for s in save_refs]
        dc = [c_ref[...] for c_ref in dcar]
        dout_v = [_load_row(ref, r, t, scr).astype(F32) for ref, r in zip(dout_refs, douts)]

        def f(cs, dr, dv):
            rr, vv = list(rows_v), list(vecs_v)
            for k, idx in enumerate(dri):
                rr[idx] = dr[k]
            for k, idx in enumerate(dvi):
                vv[idx] = dv[k]
            return fn(ci, b, cs, rr, vv)

        _, vjp = jax.vjp(f, cin, [rows_v[k].astype(F32) for k in dri], [vecs_v[k].astype(F32) for k in dvi])
        dcin, drows, dvecs = vjp((dc, dout_v))
        for c_ref, cv in zip(dcar, dcin):
            c_ref[...] = cv
        for k, (o_ref, ov) in enumerate(zip(drow_refs, drows)):
            if dri[k] in adds:
                ov = ov + add_refs[add_keys.index(dri[k])][0].astype(F32)
            _store_row(o_ref, rows[dri[k]], t, scr, ov)
        for k, (o_ref, ov) in enumerate(zip(dvec_refs, dvecs)):
            first = (ir == 0) if vecs[dvi[k]].fc is not None else jnp.logical_and(ir == 0, b == 0)

            @pl.when(first)
            def _(o_ref=o_ref, ov=ov):
                o_ref[...] = ov

            @pl.when(jnp.logical_not(first))
            def _(o_ref=o_ref, ov=ov):
                o_ref[...] += ov

        if ride:
            ride.end(ride_in, ride_out, sems, jnp.logical_and(b == nb - 1, ir == nchunk - 1))

    stage = _view_scratch(list(rows) + list(douts), t)
    in_specs = ([_row_spec(r, t, nchunk, True) for r in rows] + [_vec_spec(v) for v in vecs]
                + [pl.BlockSpec((1, 1) + tuple(cs), lambda b, i: (b, nchunk - 1 - i, 0, 0)) for cs in carries]
                + [_row_spec(d, t, nchunk, True) for d in douts]
                + [_row_spec(adds[k], t, nchunk, True) for k in add_keys] + r_in)
    out_shape, out_specs = [], []
    for k in dri:
        r = rows[k]
        if r.slot:
            out_shape.append(jax.ShapeDtypeStruct((nb, r.arr.shape[1], r.w), r.ddtype))
            out_specs.append(pl.BlockSpec((1, t, r.w), lambda b, i: (b, nchunk - 1 - i, 0)))
        elif r.dcols is not None:
            out_shape.append(jax.ShapeDtypeStruct((r.arr.shape[0], r.arr.shape[1], r.dcols), r.ddtype))
            out_specs.append(pl.BlockSpec((1, t, r.w), lambda b, i, r=r: (r.fb(b), nchunk - 1 - i, r.dfc(b))))
        else:
            out_shape.append(jax.ShapeDtypeStruct(r.arr.shape, r.ddtype))
            out_specs.append(_row_spec(r, t, nchunk, True))
    for k in dvi:
        out_shape.append(jax.ShapeDtypeStruct(vecs[k].arr.shape, F32))
        out_specs.append(_vec_spec(vecs[k]))
    nd = len(dri) + len(dvi)
    res = pl.pallas_call(
        body, name=name, grid=(nb, nchunk), in_specs=in_specs, out_specs=out_specs + r_out,
        out_shape=out_shape + (list(ride.out_shapes) if ride else []),
        scratch_shapes=[pltpu.VMEM(tuple(cs), F32) for cs in carries] + stage + r_scr,
        compiler_params=_cparams(),
    )(*[r.arr for r in rows], *[v.arr for v in vecs], *saved, *[d.arr for d in douts], *[adds[k].arr for k in add_keys],
      *(ride.ins if ride else []))
    if ride:
        RIDERS.done[name] = list(res[nd:])
    return list(res[:len(dri)]), list(res[len(dri):nd])


def out_row(shape, dtype=F32, w=None, fb=None, fc=None):
    return Row(jax.ShapeDtypeStruct(shape, dtype), w, fb, fc)


def _conv(shift, halo, cur, w, bias, taps):
    y = bias
    for k in range(taps):
        y = y + _rowk(w, k) * shift(halo, cur, taps - 1 - k)
    return y


def _ssd_fn(ci, b, carries, rows, vecs):
    cx, cb_, cc, ht = carries
    z, xr, br, cr, dtr = rows
    cwx, cbx, cwb, cbb, cwc, cbc, dtb, alog, dsk, ng = vecs
    t = z.shape[0]
    xs = _silu(_conv(_shift8, cx, xr, cwx, cbx, 4))
    bm = _silu(_conv(_shift8, cb_, br, cwb, cbb, 4))
    cm = _silu(_conv(_shift8, cc, cr, cwc, cbc, 4))
    dt = _softplus(dtr + dtb)
    acol = _cumsum_rows(dt * (-jnp.exp(alog)))
    arow = acol.T
    r, c = _iota((t, t), 0), _iota((t, t), 1)
    causal = r >= c
    cbm = _bdot(cm, bm, "nt")
    lane, sub = _iota(acol.shape, 1), _iota(arow.shape, 0)
    colh = _iota(xs.shape, 1) // 64
    a, dtx, dx, acs = jnp.zeros(xs.shape, F32), jnp.zeros(xs.shape, F32), jnp.zeros((1, xs.shape[1]), F32), []
    for j in range(4):
        h = 4 * b + j
        ac = jnp.sum(jnp.where(lane == h, acol, 0.0), axis=1, keepdims=True)
        acs.append(ac)
        a = jnp.where(colh == j, ac, a)
        dtx = jnp.where(colh == j, jnp.sum(jnp.where(lane == h, dt, 0.0), axis=1, keepdims=True), dtx)
        dx = jnp.where(_iota(dx.shape, 1) // 64 == j, jnp.sum(jnp.where(_iota(dsk.shape, 1) == h, dsk, 0.0), axis=1, keepdims=True), dx)
    atot = jnp.sum(jnp.where(_iota(a.shape, 0) == t - 1, a, 0.0), axis=0, keepdims=True)
    x = xs * dtx
    ydiag = jnp.zeros(x.shape, F32)
    for j in range(4):
        ar = jnp.sum(jnp.where(sub == 4 * b + j, arow, 0.0), axis=0, keepdims=True)
        lmat = jnp.exp(jnp.where(causal, acs[j] - ar, NEG))
        ydiag = ydiag + _bdot(cbm * lmat, jnp.where(colh == j, x, 0.0), "nn")
    yoff = _bdot(cm, ht, "nn") * jnp.exp(a)
    ht_new = ht * jnp.exp(atot) + _bdot(bm, x * jnp.exp(atot - a), "tn")
    y = ydiag + yoff + dx * xs
    yz = y * _silu(z)
    yn = yz * lax.rsqrt(jnp.mean(yz * yz, axis=-1, keepdims=True) + NORM_EPS) * ng
    return [_tail8(xr), _tail8(br), _tail8(cr), ht_new], [yn]


_SSD_T = 256
_SSD_CARRIES = [(8, 256), (8, 128), (8, 128), (128, 256)]


def _ssd_io(proj3, p):
    own = lambda b: b
    rows = [Row(proj3, 256, fc=own, dcols=512, dfc=own, ddtype=BF16),
            Row(proj3, 256, fc=lambda b: 2 + b, dcols=512, dfc=own, ddtype=BF16),
            Row(proj3, 128, fc=lambda b: 8 + b, dcols=256, dfc=own, ddtype=BF16),
            Row(proj3, 128, fc=lambda b: 10 + b, dcols=256, dfc=own, ddtype=BF16),
            Row(proj3, 128, fc=lambda b: 14, slot=True)]
    vecs = [Vec(p["cw"], 256, lambda b: b), Vec(p["cb"], 256, lambda b: b),
            Vec(p["cw"], 128, lambda b: 4 + b), Vec(p["cb"], 128, lambda b: 4 + b),
            Vec(p["cw"], 128, lambda b: 6 + b), Vec(p["cb"], 128, lambda b: 6 + b),
            Vec(p["dtb"]), Vec(p["alog"]), Vec(p["dsk"]), Vec(p["ng"], 256, lambda b: b)]
    return rows, vecs


def ssd_forward(name, proj3, p):
    rows, vecs = _ssd_io(proj3, p)
    s = proj3.shape[1]
    (y,), saved = scan_fwd(name, _ssd_fn, nb=2, nchunk=s // _SSD_T, t=_SSD_T, rows=rows, vecs=vecs,
                           carries=_SSD_CARRIES, outs=[out_row((1, s, SSD_INNER), BF16, 256, fc=lambda b: b)], save=True)
    return y, saved


def ssd_backward(name, proj3, p, saved, dmix3):
    rows, vecs = _ssd_io(proj3, p)
    s = proj3.shape[1]
    drows, dvecs = scan_bwd(name, _ssd_fn, nb=2, nchunk=s // _SSD_T, t=_SSD_T, rows=rows, vecs=vecs,
                            carries=_SSD_CARRIES, saved=saved, douts=[Row(dmix3, 256, fc=lambda b: b)])
    return drows, dvecs


def _pool_fn(ci, b, carries, rows, vecs):
    (cu,) = carries
    (u,) = rows
    wbd, scale = vecs
    t = u.shape[0]
    pos = ci * t + _iota(u.shape, 0)
    grp = _iota(u.shape, 1) // 64
    acc, pooled, k = u, jnp.zeros(u.shape, F32), 1
    for gi, w in enumerate(POOL_WINDOWS):
        while k < w:
            acc = acc + _shift16(cu, u, k)
            k += 1
        pooled = jnp.where(grp == gi, acc / jnp.minimum(pos + 1, w).astype(F32), pooled)
    y = _bdot(pooled - u, wbd, "nn") * scale
    return [_tail16(u)], [y]


_POOL_T = 256


def _pool_io(proj3, wbd, scale):
    return [Row(proj3, 256, fc=lambda b: 6, dcols=256, dfc=lambda b: 0, ddtype=BF16)], [Vec(wbd), Vec(scale)]


def pool_forward(name, proj3, wbd, scale):
    rows, vecs = _pool_io(proj3, wbd, scale)
    s = proj3.shape[1]
    (y,), saved = scan_fwd(name, _pool_fn, nb=1, nchunk=s // _POOL_T, t=_POOL_T, rows=rows, vecs=vecs,
                           carries=[(16, 256)], outs=[out_row((1, s, POOL_W), BF16)], save=True)
    return y, saved


def pool_backward(name, proj3, wbd, scale, saved, dmix3):
    rows, vecs = _pool_io(proj3, wbd, scale)
    s = proj3.shape[1]
    return scan_bwd(name, _pool_fn, nb=1, nchunk=s // _POOL_T, t=_POOL_T, rows=rows, vecs=vecs,
                    carries=[(16, 256)], saved=saved, douts=[Row(dmix3, 256, fc=lambda b: 2)])


def _attn_fn(ci, b, carries, rows, vecs):
    kp, vp = carries
    qr, kr, v = _thirds(rows[0])
    scale = ATT_HEAD_DIM ** -0.5
    q = qr
    n = q.shape[0]
    r, c = _iota((n, n), 0), _iota((n, n), 1)
    prev_ok, cur_ok = jnp.logical_and(c >= r, ci > 0), r >= c
    head = _iota(q.shape, 1) // ATT_HEAD_DIM
    o, lse = jnp.zeros(q.shape, F32), jnp.zeros(q.shape, F32)
    for h in range(ATT_HEADS):
        mine = head == h
        qh = jnp.where(mine, qr, 0.0)
        sp = jnp.where(prev_ok, _bdot(qh, kp, "nt") * scale, NEG)
        sc = jnp.where(cur_ok, _bdot(qh, kr, "nt") * scale, NEG)
        m = lax.stop_gradient(jnp.maximum(jnp.max(sp, axis=1, keepdims=True), jnp.max(sc, axis=1, keepdims=True)))
        pp, pc = jnp.exp(sp - m), jnp.exp(sc - m)
        l = jnp.sum(pp, axis=1, keepdims=True) + jnp.sum(pc, axis=1, keepdims=True)
        o = jnp.where(mine, (_bdot(pp, vp, "nn") + _bdot(pc, v, "nn")) / l, o)
        lse = jnp.where(mine, m + jnp.log(l), lse)
    return [kr, v], [o, lse]


_ATT_CARRIES = [(ATT_BLOCK, ATT_W), (ATT_BLOCK, ATT_W)]


def attn_forward(name, pv, d):
    l = pv.shape[1]
    own = lambda b: b
    outs = [out_row((1, l, d * ATT_W), F32, ATT_W, fc=own) for _ in range(2)]
    (o, lse), saved = scan_fwd(name, _attn_fn, nb=d, nchunk=l // ATT_BLOCK, t=ATT_BLOCK, rows=[Row(pv, 3 * ATT_W, fc=own)],
                               vecs=[], carries=_ATT_CARRIES, outs=outs, save=True)
    return o, lse, saved


def attn_backward(name, pv, d, saved, do, dlse):
    l = pv.shape[1]
    own = lambda b: b
    (dpv,), _ = scan_bwd(name, _attn_fn, nb=d, nchunk=l // ATT_BLOCK, t=ATT_BLOCK, rows=[Row(pv, 3 * ATT_W, fc=own)], vecs=[],
                         carries=_ATT_CARRIES, saved=saved, douts=[Row(do, ATT_W, fc=own), Row(dlse, ATT_W, fc=own)])
    return dpv


def _rope_fn(ci, b, carries, rows, vecs):
    x, cs, sn = rows
    return [], [x * cs + _rot_pairs(x) * sn]


def _rope3_fn(ci, b, carries, rows, vecs):
    _, (y,) = _rope_fn(ci, b, carries, rows, vecs)
    return [], [y, y, y]


def _by_residue(a_or_shape, w, d):
    if isinstance(a_or_shape, tuple):
        _, s, _ = a_or_shape
        return Row(jax.ShapeDtypeStruct((1, s // d, d * w), F32), w, view=None if d == 1 else d)
    return Row(a_or_shape, w, view=None if d == 1 else d)


def rope_forward(name, qkv3, cs3, sn3):
    s, w = qkv3.shape[1], qkv3.shape[2]
    ys, _ = scan_fwd(name, _rope3_fn, nb=1, nchunk=s // _ROW_T, t=_ROW_T, vecs=[], carries=[], save=False,
                     rows=[Row(qkv3), Row(cs3, diff=False), Row(sn3, diff=False)],
                     outs=[_by_residue(qkv3.shape, w, d) for _, d in ATT_PATTERNS])
    return ys


def rope_backward(name, qkv3, cs3, sn3, dys):
    s, w = qkv3.shape[1], qkv3.shape[2]
    (dx,), _ = scan_bwd(name, _rope3_fn, nb=1, nchunk=s // _ROW_T, t=_ROW_T, vecs=[], carries=[], saved=[],
                        rows=[Row(qkv3, ddtype=BF16), Row(cs3, diff=False), Row(sn3, diff=False)],
                        douts=[_by_residue(a, w, d) for a, (_, d) in zip(dys, ATT_PATTERNS)])
    return dx


def _merge_fn(ci, b, carries, rows, vecs):
    o1, o2, o3, l1, l2, l3 = rows
    mx = lax.stop_gradient(jnp.maximum(l1, jnp.maximum(l2, l3)))
    e1, e2, e3 = jnp.exp(l1 - mx), jnp.exp(l2 - mx), jnp.exp(l3 - mx)
    return [], [(e1 * o1 + e2 * o2 + e3 * o3) / (e1 + e2 + e3)]


_ROW_T = 256


def _merge_rows(os_, ls_):
    ds = [d for _, d in ATT_PATTERNS]
    return [_by_residue(a, ATT_W, d) for a, d in zip(os_, ds)] + [_by_residue(a, ATT_W, d) for a, d in zip(ls_, ds)]


def merge_forward(name, os_, ls_, s):
    (y,), _ = scan_fwd(name, _merge_fn, nb=1, nchunk=s // _ROW_T, t=_ROW_T, rows=_merge_rows(os_, ls_), vecs=[],
                       carries=[], outs=[out_row((1, s, ATT_W), BF16)], save=False)
    return y


def merge_backward(name, os_, ls_, dmix3):
    s = dmix3.shape[1]
    drows, _ = scan_bwd(name, _merge_fn, nb=1, nchunk=s // _ROW_T, t=_ROW_T, rows=_merge_rows(os_, ls_), vecs=[],
                        carries=[], saved=[], douts=[Row(dmix3, 256, fc=lambda b: 3)])
    return drows


def _norm_mod_fn(ci, b, carries, rows, vecs):
    (x,) = rows
    g, sc, sh = vecs
    xn = x * lax.rsqrt(jnp.mean(x * x, axis=-1, keepdims=True) + NORM_EPS)
    return [], [xn * g * (1.0 + sc) + sh]


def norm_mod_forward(name, x3, g, sc, sh):
    s = x3.shape[1]
    (h,), _ = scan_fwd(name, _norm_mod_fn, nb=1, nchunk=s // _ROW_T, t=_ROW_T, rows=[Row(x3)], vecs=[Vec(g), Vec(sc), Vec(sh)],
                       carries=[], outs=[out_row(x3.shape, BF16)], save=False)
    return h


def norm_mod_backward(name, x3, g, sc, sh, dh3, add3):
    s = x3.shape[1]
    (dx,), dv = scan_bwd(name, _norm_mod_fn, nb=1, nchunk=s // _ROW_T, t=_ROW_T, rows=[Row(x3)], vecs=[Vec(g), Vec(sc), Vec(sh)],
                         carries=[], saved=[], douts=[Row(dh3)], adds={0: Row(add3)})
    return dx, dv


def _gate_fn(ci, b, carries, rows, vecs):
    return [], [rows[0] * vecs[0]]


def gate_backward(name, o3, g, dx3):
    s = o3.shape[1]
    (do,), (dg,) = scan_bwd(name, _gate_fn, nb=1, nchunk=s // _ROW_T, t=_ROW_T, rows=[Row(o3, ddtype=BF16)], vecs=[Vec(g)],
                            carries=[], saved=[], douts=[Row(dx3)])
    return do, dg


def _make_halves():
    @jax.custom_vjp
    def halves(x):
        h = x.shape[1] // 2
        return x[:, :h], x[:, h:]

    def fwd(x):
        return halves(x), None

    def bwd(_, g):
        return (jnp.concatenate(g, axis=1),)

    halves.defvjp(fwd, bwd)
    return halves


_halves = _make_halves()


def _ffn_fn(ci, b, carries, rows, vecs):
    (cu,) = carries
    (u,) = rows
    w, bias = vecs
    hg, hu = _halves(_conv(_shift8, cu, u, w, bias, 3))
    return [_tail8(u)], [_silu(hg) * hu]


_FFN_T = 256
_FFN_CW = FFN_DIM // 2
_FFN_CARRIES = [(8, 2 * _FFN_CW)]
FFN_BLOCK_ORDER = [0, 2, 1, 3]


def _ffn_io(up3, cw, cb):
    own = lambda b: b
    return [Row(up3, 2 * _FFN_CW, fc=own, ddtype=BF16)], [Vec(cw, 2 * _FFN_CW, own), Vec(cb, 2 * _FFN_CW, own)]


def ffn_mid_forward(name, up3, cw, cb):
    rows, vecs = _ffn_io(up3, cw, cb)
    s = up3.shape[1]
    (act,), saved = scan_fwd(name, _ffn_fn, nb=2, nchunk=s // _FFN_T, t=_FFN_T, rows=rows, vecs=vecs, carries=_FFN_CARRIES,
                             outs=[out_row((1, s, FFN_DIM), BF16, _FFN_CW, fc=lambda b: b)], save=True)
    return act, saved


def ffn_down_forward(name, up3, cw, cb, w_down, res, gate):
    s, t, cw2 = up3.shape[1], _FFN_T, 2 * _FFN_CW
    d = w_down.shape[1]
    nchunk = s // t
    ride = RIDERS.take(name)
    r_in, r_out, r_scr = ride.specs() if ride else ([], [], [])

    def body(*refs):
        up_ref, cw_ref, cb_ref, wd_ref, res_ref, g_ref = refs[:6]
        ride_in = refs[6:6 + len(r_in)]
        act_ref, save_ref, dn_ref, x2_ref = refs[6 + len(r_in):10 + len(r_in)]
        ride_out = refs[10 + len(r_in):10 + len(r_in) + len(r_out)]
        car, acc = refs[10 + len(r_in) + len(r_out):12 + len(r_in) + len(r_out)]
        sems = refs[12 + len(r_in) + len(r_out):]
        i, b = pl.program_id(0), pl.program_id(1)
        if ride:
            ride.begin(ride_in, ride_out, sems, jnp.logical_and(i == 0, b == 0))

        @pl.when(i == 0)
        def _():
            car[b] = jnp.zeros(car.shape[1:], F32)

        cin = car[b]
        save_ref[0, 0] = cin
        (new_c,), (act,) = _ffn_fn(i, b, [cin], [up_ref[0]], [cw_ref[...], cb_ref[...]])
        car[b] = new_c
        act_ref[0] = act.astype(act_ref.dtype)
        part = _mxu(act, wd_ref[...], "nn")

        @pl.when(b == 0)
        def _():
            acc[...] = part

        @pl.when(b == 1)
        def _():
            tot = acc[...] + part
            dn_ref[...] = tot
            x2_ref[...] = res_ref[...] + g_ref[...] * tot

        if ride:
            ride.end(ride_in, ride_out, sems, jnp.logical_and(i == nchunk - 1, b == 1))

    tile = pl.BlockSpec((t, d), lambda i, b: (i, 0))
    out = pl.pallas_call(
        body, name=name, grid=(nchunk, 2),
        in_specs=[pl.BlockSpec((1, t, cw2), lambda i, b: (0, i, b)), pl.BlockSpec((cw.shape[0], cw2), lambda i, b: (0, b)),
                  pl.BlockSpec((1, cw2), lambda i, b: (0, b)), pl.BlockSpec((_FFN_CW, d), lambda i, b: (b, 0)), tile,
                  pl.BlockSpec((1, d), lambda i, b: (0, 0))] + r_in,
        out_specs=[pl.BlockSpec((1, t, _FFN_CW), lambda i, b: (0, i, b)), pl.BlockSpec((1, 1, 8, cw2), lambda i, b: (b, i, 0, 0)),
                   tile, tile] + r_out,
        out_shape=[jax.ShapeDtypeStruct((1, s, FFN_DIM), BF16), jax.ShapeDtypeStruct((2, nchunk, 8, cw2), F32),
                   jax.ShapeDtypeStruct((s, d), F32), jax.ShapeDtypeStruct((s, d), F32)] + (list(ride.out_shapes) if ride else []),
        scratch_shapes=[pltpu.VMEM((2, 8, cw2), F32), pltpu.VMEM((t, d), F32)] + r_scr,
        compiler_params=_cparams(),
    )(up3, cw, cb, w_down, res, gate, *(ride.ins if ride else []))
    if ride:
        RIDERS.done[name] = list(out[4:])
    return out[0], [out[1]], out[2], out[3]


def ffn_mid_backward(name, up3, cw, cb, saved, dact3):
    rows, vecs = _ffn_io(up3, cw, cb)
    s = up3.shape[1]
    return scan_bwd(name, _ffn_fn, nb=2, nchunk=s // _FFN_T, t=_FFN_T, rows=rows, vecs=vecs, carries=_FFN_CARRIES,
                    saved=saved, douts=[Row(dact3, _FFN_CW, fc=lambda b: b)])


def _adam_fn(ci, b, carries, rows, vecs):
    w, g, m, v = rows
    m = ADAM_B1 * m + (1.0 - ADAM_B1) * g
    v = ADAM_B2 * v + (1.0 - ADAM_B2) * (g * g)
    m_hat = m / (1.0 - ADAM_B1 ** ADAM_STEP)
    v_hat = v / (1.0 - ADAM_B2 ** ADAM_STEP)
    delta = -ADAM_LR * (m_hat / (jnp.sqrt(v_hat) + ADAM_EPS) + ADAM_WD * w)
    return [], [delta, m, v]


def adamw(name, w, g, m, v):
    shape = w.shape
    c = shape[-1]
    r = int(np.prod(shape[:-1]))
    t = _tile(r, 256, 8)
    as3 = lambda a: a.reshape(1, r, c)
    outs, _ = scan_fwd(name, _adam_fn, nb=1, nchunk=r // t, t=t, rows=[Row(as3(a)) for a in (w, g, m, v)], vecs=[], carries=[],
                       outs=[out_row((1, r, c)) for _ in range(3)], save=False)
    return [o.reshape(shape) for o in outs]


def rope_tables(positions):
    inv_freq = ROPE_THETA ** (-jnp.arange(0, ROT_DIM, 2, dtype=F32) / ROT_DIM)
    ang = positions.astype(F32)[:, None] * inv_freq
    s = positions.shape[0]
    cs = jnp.concatenate([jnp.cos(ang), jnp.cos(ang), jnp.ones((s, ATT_HEAD_DIM - ROT_DIM), F32)], axis=1)
    sn = jnp.concatenate([jnp.sin(ang), jnp.sin(ang), jnp.zeros((s, ATT_HEAD_DIM - ROT_DIM), F32)], axis=1)
    cs3 = jnp.concatenate([jnp.tile(cs, (1, 2 * ATT_HEADS)), jnp.ones((s, ATT_W), F32)], axis=1)
    sn3 = jnp.concatenate([jnp.tile(sn, (1, 2 * ATT_HEADS)), jnp.zeros((s, ATT_W), F32)], axis=1)
    return cs3[None], sn3[None]


def attention_forward(lname, qkv3, cs3, sn3):
    s = qkv3.shape[1]
    rotated = rope_forward(f"{lname}_rope", qkv3, cs3, sn3)
    os_, ls_, keep = [], [], []
    for pi, (_, d) in enumerate(ATT_PATTERNS):
        o, lse, saved = attn_forward(f"{lname}_attn{pi}", rotated[pi], d)
        os_.append(o)
        ls_.append(lse)
        keep.append(saved)
    y = merge_forward(f"{lname}_merge", os_, ls_, s)
    return y, (rotated, os_, ls_, keep)


def attention_backward(lname, qkv3, cs3, sn3, res, dmix3):
    rotated, os_, ls_, keep = res
    dm = merge_backward(f"{lname}_merge_b", os_, ls_, dmix3)
    dys = [attn_backward(f"{lname}_attn{pi}_b", rotated[pi], d, keep[pi], dm[pi], dm[3 + pi]) for pi, (_, d) in enumerate(ATT_PATTERNS)]
    return rope_backward(f"{lname}_rope_b", qkv3, cs3, sn3, dys)


def mm(name, a, b, mode, out_dtype=F32, res=None, gate=None, tm=1408, tn=1536, tk=1408, into=None):
    if mode == "nn":
        (m, k), n = a.shape, b.shape[1]
    elif mode == "nt":
        (m, k), n = a.shape, b.shape[0]
    else:
        (k, m), n = a.shape, b.shape[1]
    tm, tn, tk = _tile(m, tm), _tile(n, tn), _tile(k, tk)
    nk = k // tk
    a_spec = pl.BlockSpec((tk, tm), lambda i, j, q: (q, i)) if mode == "tn" else pl.BlockSpec((tm, tk), lambda i, j, q: (i, q))
    b_spec = pl.BlockSpec((tn, tk), lambda i, j, q: (j, q)) if mode == "nt" else pl.BlockSpec((tk, tn), lambda i, j, q: (q, j))
    o_spec = pl.BlockSpec((tm, tn), lambda i, j, q: (i, j))
    fused = res is not None
    lead = 0 if into is None else into[0].ndim - 2
    first = (0,) * lead + (slice(None), slice(None))

    def body(*refs):
        if fused:
            a_ref, b_ref, r_ref, g_ref, o_ref, o2_ref, acc = refs
        elif into is not None:
            a_ref, b_ref, _, o_ref, acc = refs
        else:
            a_ref, b_ref, o_ref, acc = refs
        q = pl.program_id(2)

        @pl.when(q == 0)
        def _():
            acc[...] = jnp.zeros(acc.shape, F32)

        acc[...] += _mxu(a_ref[...], b_ref[...], mode)

        @pl.when(q == nk - 1)
        def _():
            o_ref[first] = acc[...].astype(o_ref.dtype)
            if fused:
                o2_ref[...] = r_ref[...] + g_ref[...] * acc[...]

    ins, in_specs = [a, b], [a_spec, b_spec]
    out_shape, out_specs = [jax.ShapeDtypeStruct((m, n), out_dtype)], [o_spec]
    if fused:
        ins += [res, gate]
        in_specs += [o_spec, pl.BlockSpec((1, tn), lambda i, j, q: (0, j))]
        out_shape.append(jax.ShapeDtypeStruct((m, n), F32))
        out_specs.append(o_spec)
    aliases = {}
    if into is not None:
        buf, omap = into
        ins.append(buf)
        in_specs.append(pl.BlockSpec(memory_space=pl.ANY))
        out_shape = [jax.ShapeDtypeStruct(buf.shape, buf.dtype)]
        out_specs = [pl.BlockSpec((1,) * lead + (tm, tn), lambda i, j, q: omap(i, j))]
        aliases = {2: 0}
    out = pl.pallas_call(
        body, name=name, grid=(m // tm, n // tn, nk), in_specs=in_specs, out_specs=out_specs, out_shape=out_shape,
        scratch_shapes=[pltpu.VMEM((tm, tn), F32)], input_output_aliases=aliases,
        compiler_params=pltpu.CompilerParams(dimension_semantics=("parallel", "parallel", "arbitrary"),
                                             vmem_limit_bytes=VMEM_LIMIT_BYTES),
    )(*ins)
    return tuple(out) if fused else out[0]


def final_loss(name, x3, t3, g):
    s, d = x3.shape[1], x3.shape[2]
    t = _ROW_T

    def body(x_ref, t_ref, g_ref, loss_ref, dx_ref, dg_ref):
        i = pl.program_id(0)
        tv = t_ref[0]

        def f(x, gg):
            y = x * lax.rsqrt(jnp.mean(x * x, axis=-1, keepdims=True) + NORM_EPS) * gg
            e = y - tv
            return 0.5 * jnp.sum(jnp.mean(e * e, axis=-1, keepdims=True), axis=0, keepdims=True)

        l, vjp = jax.vjp(f, x_ref[0], g_ref[...])
        dx, dg = vjp(jnp.ones((1, 1), F32))
        dx_ref[0] = dx

        @pl.when(i == 0)
        def _():
            loss_ref[...] = jnp.zeros(loss_ref.shape, F32)
            dg_ref[...] = jnp.zeros(dg_ref.shape, F32)

        loss_ref[...] += jnp.broadcast_to(l, loss_ref.shape)
        dg_ref[...] += dg

    row = pl.BlockSpec((1, t, d), lambda i: (0, i, 0))
    vec = pl.BlockSpec((1, d), lambda i: (0, 0))
    return pl.pallas_call(
        body, name=name, grid=(s // t,), in_specs=[row, row, vec],
        out_specs=[pl.BlockSpec((8, 128), lambda i: (0, 0)), row, vec],
        out_shape=[jax.ShapeDtypeStruct((8, 128), F32), jax.ShapeDtypeStruct(x3.shape, F32), jax.ShapeDtypeStruct((1, d), F32)],
        compiler_params=pltpu.CompilerParams(dimension_semantics=("arbitrary",), vmem_limit_bytes=VMEM_LIMIT_BYTES),
    )(x3, t3, g)


_ADA_TN = 512


def ada_forward(name, c16, ada_w):
    depth, d, cols = ada_w.shape

    def body(c_ref, w_ref, o_ref):
        o_ref[0] = _mxu(_silu(c_ref[...]), w_ref[0], "nn")

    return pl.pallas_call(
        body, name=name, grid=(depth, cols // _ADA_TN),
        in_specs=[pl.BlockSpec((16, d), lambda l, j: (0, 0)), pl.BlockSpec((1, d, _ADA_TN), lambda l, j: (l, 0, j))],
        out_specs=pl.BlockSpec((1, 16, _ADA_TN), lambda l, j: (l, 0, j)),
        out_shape=jax.ShapeDtypeStruct((depth, 16, cols), F32),
        compiler_params=pltpu.CompilerParams(dimension_semantics=("arbitrary", "arbitrary"), vmem_limit_bytes=VMEM_LIMIT_BYTES),
    )(c16, ada_w)


def ada_backward(name, c16, dmod16, w, m, v):
    depth, d, cols = w.shape

    def body(c_ref, dm_ref, w_ref, m_ref, v_ref, g_ref, dl_ref, nm_ref, nv_ref):
        g = _mxu(_silu(c_ref[...]), dm_ref[0], "tn")
        _, (delta, nm, nv) = _adam_fn(None, None, [], [w_ref[0], g, m_ref[0], v_ref[0]], [])
        g_ref[0], dl_ref[0], nm_ref[0], nv_ref[0] = g, delta, nm, nv

    blk = pl.BlockSpec((1, d, _ADA_TN), lambda l, j: (l, 0, j))
    return pl.pallas_call(
        body, name=name, grid=(depth, cols // _ADA_TN),
        in_specs=[pl.BlockSpec((16, d), lambda l, j: (0, 0)), pl.BlockSpec((1, 16, _ADA_TN), lambda l, j: (l, 0, j)), blk, blk, blk],
        out_specs=[blk] * 4, out_shape=[jax.ShapeDtypeStruct(w.shape, F32)] * 4,
        compiler_params=pltpu.CompilerParams(dimension_semantics=("arbitrary", "arbitrary"), vmem_limit_bytes=VMEM_LIMIT_BYTES),
    )(c16, dmod16, w, m, v)


def _sum_fn(ci, b, carries, rows, vecs):
    acc = rows[0]
    for r in rows[1:]:
        acc = acc + r
    return [], [acc]


def sum_slots(name, a, nsum, out_dtype=F32):
    n, r, c = a.shape
    nb = n // nsum
    t = _tile(r, 256, 8)
    rows = [Row(a, fb=(lambda b, k=k: k * nb + b)) for k in range(nsum)]
    (out,), _ = scan_fwd(name, _sum_fn, nb=nb, nchunk=r // t, t=t, rows=rows, vecs=[], carries=[],
                         outs=[out_row((nb, r, c), out_dtype, fb=lambda b: b)], save=False)
    return out


def _sum_my_layer_fn(ci, b, carries, rows, vecs):
    layer0, layer1, theirs = rows
    return [], [jnp.where(lax.axis_index("c") == 0, layer0, layer1) + theirs]


def sum_cores(name, g, theirs, out_dtype):
    _, nb, r, c = g.shape
    g8 = g.reshape(2 * nb, r, c)
    t = _tile(r, 256, 8)
    rows = [Row(g8, fb=lambda b: b), Row(g8, fb=lambda b: nb + b), Row(theirs, fb=lambda b: b)]
    (out,), _ = scan_fwd(name, _sum_my_layer_fn, nb=nb, nchunk=r // t, t=t, rows=rows, vecs=[], carries=[],
                         outs=[out_row((nb, r, c), out_dtype, fb=lambda b: b)], save=False)
    return out


def _flip(mask, pos):
    return tuple((1 - p) if m else p for m, p in zip(mask, pos))


ALL_PEERS = [(a, b, c) for a in (0, 1) for b in (0, 1) for c in (0, 1)][1:]
CHIP_PEERS = [(1, 0, 0), (0, 1, 0), (1, 1, 0)]
SIBLING = [(0, 0, 1)]


def _divisor(size, target, unit):
    best = 1
    for n in range(1, target + 1):
        if size % n == 0 and (size // n) % unit == 0:
            best = n
    return best


def _pieces(src, dst, pieces):
    shape = src.shape
    unit = 16 if src.dtype == BF16 else 8
    if pieces <= 1:
        return [(src, dst)]
    if len(shape) == 2:
        n = _divisor(shape[0], pieces, unit)
        s = shape[0] // n
        return [(src.at[pl.ds(i * s, s)], dst.at[pl.ds(i * s, s)]) for i in range(n)]
    assert len(shape) == 3, shape
    n = _divisor(shape[1], max(pieces // shape[0], 1), unit)
    s = shape[1] // n
    return [(src.at[j, pl.ds(i * s, s)], dst.at[j, pl.ds(i * s, s)]) for j in range(shape[0]) for i in range(n)]


def comm_call(name, arrays, out_shapes, masks, src_fn, dst_fn, local_fn=None, pieces=1):
    na, npeer = len(arrays), len(masks)

    def body(*refs):
        ins, outs = refs[:na], refs[na:2 * na]
        send_sems, recv_sems, loc_sems = refs[2 * na:]
        me = (lax.axis_index("x"), lax.axis_index("y"), lax.axis_index("c"))
        local = []
        if local_fn is not None:
            for k in range(na):
                s, d = local_fn(k, ins[k], outs[k], me)
                for ps, pd in _pieces(s, d, pieces):
                    pltpu.make_async_copy(ps, pd, loc_sems.at[k]).start()
                local.append(pltpu.make_async_copy(s, d, loc_sems.at[k]))

        def remote(k, p, src, dst, to):
            return pltpu.make_async_remote_copy(
                src_ref=src, dst_ref=dst, send_sem=send_sems.at[k * npeer + p], recv_sem=recv_sems.at[k * npeer + p],
                device_id=to, device_id_type=MESH)

        for k in range(na):
            for p in range(npeer):
                peer = _flip(masks[p], me)
                for ps, pd in _pieces(src_fn(k, ins[k], me, peer), dst_fn(k, outs[k], me), pieces):
                    remote(k, p, ps, pd, peer).start()
        for k in range(na):
            for p in range(npeer):
                peer = _flip(masks[p], me)
                remote(k, p, src_fn(k, ins[k], me, peer), dst_fn(k, outs[k], peer), peer).wait_recv()
        for k in range(na):
            for p in range(npeer):
                peer = _flip(masks[p], me)
                remote(k, p, src_fn(k, ins[k], me, peer), dst_fn(k, outs[k], me), peer).wait_send()
        for cp in local:
            cp.wait()

    hbm = pl.BlockSpec(memory_space=pl.ANY)
    out = pl.pallas_call(
        body, name=name, in_specs=[hbm] * na, out_specs=[hbm] * na,
        out_shape=[jax.ShapeDtypeStruct(s, a.dtype) for s, a in zip(out_shapes, arrays)],
        scratch_shapes=[pltpu.SemaphoreType.DMA((na * npeer,)), pltpu.SemaphoreType.DMA((na * npeer,)),
                        pltpu.SemaphoreType.DMA((na,))],
    )(*arrays)
    return list(out)


def _dev(pos):
    return 4 * pos[0] + 2 * pos[1] + pos[2]


def _chip(pos):
    return 2 * pos[0] + pos[1]


def allgather8(name, a):
    (out,) = comm_call(name, [a], [(8,) + a.shape], ALL_PEERS,
                       src_fn=lambda k, r, me, peer: r, dst_fn=lambda k, o, sender: o.at[_dev(sender)],
                       local_fn=lambda k, r, o, me: (r, o.at[_dev(me)]))
    return out


def gather_layer_from_chips(name, arrays):
    return comm_call(name, arrays, [(4,) + a.shape[1:] for a in arrays], CHIP_PEERS,
                     src_fn=lambda k, r, me, peer: r.at[me[2]], dst_fn=lambda k, o, sender: o.at[_chip(sender)],
                     local_fn=lambda k, r, o, me: (r.at[me[2]], o.at[_chip(me)]), pieces=8)


def swap_layers(name, arrays, c):
    got = comm_call(name, arrays, [a.shape for a in arrays], SIBLING,
                    src_fn=lambda k, r, me, peer: r, dst_fn=lambda k, o, sender: o, pieces=32)
    return [[jnp.where(c == 0, a, g), jnp.where(c == 0, g, a)] for a, g in zip(arrays, got)]


def swap_other_layer(name, arrays):
    return comm_call(name, arrays, [a.shape[1:] for a in arrays], SIBLING,
                     src_fn=lambda k, r, me, peer: r.at[peer[2]], dst_fn=lambda k, o, sender: o, pieces=32)


def scatter_to_chips(name, arrays):
    return comm_call(name, arrays, [a.shape for a in arrays], CHIP_PEERS,
                     src_fn=lambda k, r, me, peer: r.at[_chip(peer)], dst_fn=lambda k, o, sender: o.at[_chip(sender)],
                     local_fn=lambda k, r, o, me: (r.at[_chip(me)], o.at[_chip(me)]), pieces=8)


def _rows_of(shape):
    return -(-int(np.prod(shape)) // 1024) * 8


def _pack(arrs):
    parts = []
    for a in arrs:
        flat = a.reshape(-1).astype(F32)
        parts.append(jnp.pad(flat, (0, _rows_of(a.shape) * 128 - flat.shape[0])).reshape(-1, 128))
    rows = sum(p.shape[0] for p in parts)
    parts.append(jnp.zeros(((-rows) % _ROW_T, 128), F32))
    return jnp.concatenate(parts, axis=0)


def _unpack(buf, shapes):
    out, o = [], 0
    for s in shapes:
        r, n = _rows_of(s), int(np.prod(s))
        out.append(buf[o:o + r].reshape(-1)[:n].reshape(s))
        o += r
    return out


_WEIGHTS = ["ada_w", "ada_b", "norm1_g", "w_in", "ssd_conv_w", "ssd_conv_b", "ssd_dt_bias", "ssd_a_log", "ssd_d", "ssd_norm_g",
            "pool_w", "pool_scale", "w_out", "norm2_g", "ffn_up", "ffn_conv_w", "ffn_conv_b", "ffn_down", "final_g"]
_BIG = ["w_in", "w_out", "ffn_up", "ffn_down"]
_SMALL = [n for n in _WEIGHTS if n not in _BIG and n != "ada_w"]
_COL_SHARDED_SMALL = {"ssd_conv_w": 256, "ffn_conv_w": 1408}


def _pad_lanes(v, n=128):
    return jnp.pad(v.astype(F32), (0, n - v.shape[0]))[None]


def _perm_cols(w):
    pad = jnp.zeros(w.shape[:-1] + (IN_WP - IN_W,), w.dtype)
    return jnp.concatenate([w[..., :1536], w[..., 1544:1800], w[..., 1536:1544], pad, w[..., 1800:]], axis=-1)


def _unperm_cols(g):
    return jnp.concatenate([g[..., :1536], g[..., 1792:1800], g[..., 1536:1792], g[..., IN_MAIN:]], axis=-1)


_CHIP2_PARTS = [(1284, 1536), (1792, 1800), (1536, 1792), (IN_MAIN, IN_MAIN + 126)]


def _w_in_chip_cols(gp):
    q = IN_W // 4
    return [gp[:, :q], gp[:, q:2 * q], jnp.concatenate([gp[:, a:b] for a, b in _CHIP2_PARTS], axis=1), gp[:, IN_WP - q:]]


def _w_in_from_chips(a):
    c2 = a[2]
    pad = jnp.zeros((a.shape[1], IN_WP - IN_W), a.dtype)
    return jnp.concatenate([a[0], a[1], c2[:, :252], c2[:, 260:516], c2[:, 252:260], pad, c2[:, 516:], a[3]], axis=1)


def _ffn_block_perm(a):
    n = a.shape[-1] // 4
    return jnp.concatenate([a[..., j * n:(j + 1) * n] for j in FFN_BLOCK_ORDER], axis=-1)


def _layer_forward(i, x3, modv, wts, sp, cs3, sn3):
    sh1, sc1, g1, sh2, sc2, g2 = modv
    big = lambda n: wts[n]() if callable(wts[n]) else wts[n]
    h1 = norm_mod_forward(f"l{i}_norm1", x3, wts["norm1_g"], sc1, sh1)
    proj3 = mm(f"l{i}_proj", h1[0], big("w_in")[:, :IN_MAIN], "nn")[None]
    qkv3 = mm(f"l{i}_qkv", h1[0], big("w_in")[:, IN_MAIN:], "nn")[None]
    y_ssd, sv_ssd = ssd_forward(f"l{i}_ssd", proj3, sp)
    y_pool, sv_pool = pool_forward(f"l{i}_pool", proj3, wts["wbd"], wts["pool_scale"])
    y_att, res_att = attention_forward(f"l{i}", qkv3, cs3, sn3)
    mix = jnp.concatenate([y_ssd, y_pool, y_att], axis=-1)
    out, x1 = mm(f"l{i}_wout", mix[0], big("w_out"), "nn", res=x3[0], gate=g1)
    x1 = x1[None]
    h2 = norm_mod_forward(f"l{i}_norm2", x1, wts["norm2_g"], sc2, sh2)
    up3 = mm(f"l{i}_up", h2[0], big("ffn_up"), "nn")[None]
    act, sv_ffn, dn, x2 = ffn_down_forward(f"l{i}_down", up3, wts["ffn_conv_w"], wts["ffn_conv_b"], big("ffn_down"), x1[0], g2)
    keep = dict(x=x3, h1=h1, proj3=proj3, qkv3=qkv3, sv_ssd=sv_ssd, sv_pool=sv_pool, res_att=res_att, mix=mix, out=out[None],
                x1=x1, h2=h2, up3=up3, act=act, sv_ffn=sv_ffn, dn=dn[None])
    return x2[None], keep


def _layer_backward(i, dx2, keep, modv, wts, sp, cs3, sn3, after=None):
    sh1, sc1, g1, sh2, sc2, g2 = modv
    k = keep
    big = lambda n: wts[n]() if callable(wts[n]) else wts[n]
    tell = lambda step, *a: after[step](*a) if after and step in after else None
    d_dn, d_g2 = gate_backward(f"l{i}_gate2_b", k["dn"], g2, dx2)
    d_act = mm(f"l{i}_down_bx", d_dn[0], big("ffn_down"), "nt")
    g_down = mm(f"l{i}_down_bw", k["act"][0], d_dn[0], "tn").reshape(4, FFN_DIM // 4, D_MODEL)
    (d_up,), dv_ffn = ffn_mid_backward(f"l{i}_ffn_b", k["up3"], wts["ffn_conv_w"], wts["ffn_conv_b"], k["sv_ffn"], d_act[None])
    tell("ffn_b")
    d_h2 = mm(f"l{i}_up_bx", d_up[0], big("ffn_up"), "nt")
    g_up = mm(f"l{i}_up_bw", k["h2"][0], d_up[0], "tn", tn=_FFN_CW,
              into=((4, D_MODEL, _FFN_CW), lambda r, c: ((c % 2) * 2 + c // 2, r, 0)))
    dx1, (d_n2, d_sc2, d_sh2) = norm_mod_backward(f"l{i}_norm2_b", k["x1"], wts["norm2_g"], sc2, sh2, d_h2[None], dx2)
    d_out, d_g1 = gate_backward(f"l{i}_gate1_b", k["out"], g1, dx1)
    d_mix = mm(f"l{i}_wout_bx", d_out[0], big("w_out"), "nt")[None]
    g_wout = mm(f"l{i}_wout_bw", k["mix"][0], d_out[0], "tn").reshape(4, D_MODEL // 4, D_MODEL)
    tell("wout_bw", g_wout, g_up, g_down)
    (dz, dxs, dbm, dcm, ddt), dv_ssd = ssd_backward(f"l{i}_ssd_b", k["proj3"], sp, k["sv_ssd"], d_mix)
    tell("ssd_b")
    (du_pool,), (d_wbd, d_pscale) = pool_backward(f"l{i}_pool_b", k["proj3"], wts["wbd"], wts["pool_scale"], k["sv_pool"], d_mix)
    d_qkv = attention_backward(f"l{i}", k["qkv3"], cs3, sn3, k["res_att"], d_mix)
    d_proj = jnp.concatenate([dz[0], dxs[0], dbm[0], dcm[0], du_pool[0], (ddt[0] + ddt[1]).astype(BF16), d_qkv[0]], axis=-1)
    g_win = jnp.stack(_w_in_chip_cols(mm(f"l{i}_proj_bw", k["h1"][0], d_proj, "tn")))
    tell("proj_bw", g_win)
    d_h1 = mm(f"l{i}_proj_bx", d_proj, big("w_in"), "nt")
    tell("proj_bx")
    dx, (d_n1, d_sc1, d_sh1) = norm_mod_backward(f"l{i}_norm1_b", k["x"], wts["norm1_g"], sc1, sh1, d_h1[None], dx1)
    dcwx, dcbx, dcwb, dcbb, dcwc, dcbc, ddtb, dalog, ddsk, dng = dv_ssd
    small = dict(
        norm1_g=d_n1[0], norm2_g=d_n2[0],
        ssd_conv_w=jnp.concatenate([dcwx[:, :512], dcwb[:, 512:768], dcwc[:, 768:]], axis=1),
        ssd_conv_b=jnp.concatenate([dcbx[0, :512], dcbb[0, 512:768], dcbc[0, 768:]]),
        ssd_dt_bias=ddtb[0, :8], ssd_a_log=dalog[0, :8], ssd_d=ddsk[0, :8], ssd_norm_g=dng[0],
        pool_w=jnp.stack([d_wbd[64 * g:64 * g + 64, 64 * g:64 * g + 64] for g in range(4)]), pool_scale=d_pscale[0],
        ffn_conv_w=_ffn_block_perm(dv_ffn[0]), ffn_conv_b=_ffn_block_perm(dv_ffn[1][0]),
    )
    dmod = jnp.concatenate([d_sh1[0], d_sc1[0], d_g1[0], d_sh2[0], d_sc2[0], d_g2[0]])
    return dx, [g_win, g_wout, g_up, g_down], small, dmod


def kernel(x, c, positions, ada_w, ada_b, norm1_g, w_in, ssd_conv_w, ssd_conv_b, ssd_dt_bias, ssd_a_log, ssd_d, ssd_norm_g, pool_w, pool_scale, w_out, norm2_g, ffn_up, ffn_conv_w, ffn_conv_b, ffn_down, final_g, loss_target, m_ada_w, m_ada_b, m_norm1_g, m_w_in, m_ssd_conv_w, m_ssd_conv_b, m_ssd_dt_bias, m_ssd_a_log, m_ssd_d, m_ssd_norm_g, m_pool_w, m_pool_scale, m_w_out, m_norm2_g, m_ffn_up, m_ffn_conv_w, m_ffn_conv_b, m_ffn_down, m_final_g, v_ada_w, v_ada_b, v_norm1_g, v_w_in, v_ssd_conv_w, v_ssd_conv_b, v_ssd_dt_bias, v_ssd_a_log, v_ssd_d, v_ssd_norm_g, v_pool_w, v_pool_scale, v_w_out, v_norm2_g, v_ffn_up, v_ffn_conv_w, v_ffn_conv_b, v_ffn_down, v_final_g):
    args = dict(locals())
    w = {n: args[n] for n in _WEIGHTS}
    m = {n: args["m_" + n] for n in _WEIGHTS}
    v = {n: args["v_" + n] for n in _WEIGHTS}
    d = D_MODEL
    me = (lax.axis_index("x"), lax.axis_index("y"), lax.axis_index("c"))
    chip, dev = _chip(me), _dev(me)
    RIDERS.reset()

    shapes0 = [c.shape, ssd_conv_w.shape, ffn_conv_w.shape]
    g0 = allgather8("gather_c_conv", _pack([c, ssd_conv_w, ffn_conv_w]))
    c16 = jnp.pad(g0[:, :d // 128, :].reshape(8, d), ((0, 8), (0, 0)))
    by_chip = [_unpack(g0[2 * j], shapes0) for j in range(4)]
    conv_w_full = jnp.concatenate([p[1] for p in by_chip], axis=-1)
    fconv_w_full = jnp.concatenate([p[2] for p in by_chip], axis=-1)

    modp = ada_forward("ada_fwd", c16, ada_w)[:, :8]
    g1 = allgather8("gather_mod", _pack([modp]))
    modfull = jnp.concatenate([_unpack(g1[2 * j], [modp.shape])[0] for j in range(4)], axis=-1)
    mod = lax.dynamic_index_in_dim(modfull, dev, axis=1, keepdims=False) + ada_b
    modv = [[mod[i, q * d:(q + 1) * d][None] for q in range(6)] for i in range(DEPTH)]

    shards = [w[n].astype(BF16) for n in _BIG]

    def weight(k, layer, got):
        parts = [jnp.where(chip == j, shards[k][layer], got[j]) for j in range(4)]
        if k == 0:
            return _w_in_from_chips(parts)
        return jnp.concatenate([parts[j] for j in FFN_BLOCK_ORDER], axis=1) if k == 2 else jnp.concatenate(parts, axis=0)

    def later(k, layer, *sources):
        made = []

        def get():
            if not made:
                got = [RIDERS.result(host)[pos] for host, pos in sources]
                made.append(weight(k, layer, got[0] if len(got) == 1 else jnp.concatenate(got, axis=1)))
            return made[0]
        return get

    cs3, sn3 = rope_tables(positions[0])
    eye4 = jnp.eye(4, dtype=F32)
    wts, sps = [], []
    for i in range(DEPTH):
        wts.append(dict(
            norm1_g=norm1_g[i][None], norm2_g=norm2_g[i][None], pool_scale=pool_scale[i][None],
            wbd=(eye4[:, None, :, None] * pool_w[i][:, :, None, :]).reshape(POOL_W, POOL_W),
            ffn_conv_w=_ffn_block_perm(fconv_w_full[i]), ffn_conv_b=_ffn_block_perm(ffn_conv_b[i])[None]))
        sps.append(dict(cw=conv_w_full[i], cb=ssd_conv_b[i][None], dtb=_pad_lanes(ssd_dt_bias[i]), alog=_pad_lanes(ssd_a_log[i]),
                        dsk=_pad_lanes(ssd_d[i]), ng=ssd_norm_g[i][None]))

    (w_in0,) = ride_alone("gather_w_in0", gather_ride(0, [shards[0]]))
    RIDERS.book("l0_ssd", gather_ride(0, [shards[1], shards[3]]))
    half = shards[2].shape[1] // 2
    RIDERS.book("l0_attn0", gather_ride(0, [shards[2][:, :half]]))
    RIDERS.book("l0_attn1", gather_ride(0, [shards[2][:, half:]]))
    wts[0].update(w_in=weight(0, 0, w_in0), w_out=later(1, 0, ("l0_ssd", 0)), ffn_down=later(3, 0, ("l0_ssd", 1)),
                  ffn_up=later(2, 0, ("l0_attn0", 0), ("l0_attn1", 0)))
    RIDERS.book("l0_attn2", gather_ride(1, [shards[0], shards[1]]))
    RIDERS.book("l0_up", gather_ride(1, [shards[2]]))
    RIDERS.book("l0_down", gather_ride(1, [shards[3]]))
    wts[1].update(w_in=later(0, 1, ("l0_attn2", 0)), w_out=later(1, 1, ("l0_attn2", 1)), ffn_up=later(2, 1, ("l0_up", 0)),
                  ffn_down=later(3, 1, ("l0_down", 0)))
    x1_, keep0 = _layer_forward(0, x, modv[0], wts[0], sps[0], cs3, sn3)
    xc, keep1 = _layer_forward(1, x1_, modv[1], wts[1], sps[1], cs3, sn3)
    keeps = [keep0, keep1]
    lossblk, dx, d_final = final_loss("final_loss", xc, loss_target, final_g[None])
    loss = lax.psum(lossblk[0, 0], ("x", "y", "c"))

    small_g, dmods = [None] * DEPTH, [None] * DEPTH
    part_sum, from_chips = [[None] * 4 for _ in range(DEPTH)], [[None] * 4 for _ in range(DEPTH)]

    def owner_sum(layer, ks, mine, theirs):
        for k, g, t in zip(ks, mine, theirs):
            part_sum[layer][k] = add_arrays(f"sum_cores{layer}_{_BIG[k]}", [g, t], BF16)

    dx, by_chip1, small_g[1], dmods[1] = _layer_backward(1, dx, keeps[1], modv[1], wts[1], sps[1], cs3, sn3)
    RIDERS.book("l0_ffn_b", to_owner_ride(1, by_chip1))

    def after_ffn_b():
        owner_sum(1, range(4), by_chip1, RIDERS.result("l0_ffn_b"))
        RIDERS.book("l0_up_bx", scatter_ride(1, [part_sum[1][2]]))
        RIDERS.book("l0_up_bw", scatter_ride(1, [part_sum[1][0], part_sum[1][1]]))
        RIDERS.book("l0_norm2_b", scatter_ride(1, [part_sum[1][3]]))

    early = []

    def after_wout_bw(g_wout, g_up, g_down):
        early.extend([g_wout, g_up, g_down])
        RIDERS.book("l0_ssd_b", to_owner_ride(0, early))

    def after_ssd_b():
        owner_sum(0, [1, 2, 3], early, RIDERS.result("l0_ssd_b"))
        for host, k in (("l0_attn0_b", 2), ("l0_attn1_b", 3), ("l0_attn2_b", 1)):
            RIDERS.book(host, scatter_ride(0, [part_sum[0][k]]))

    last = []

    def after_proj_bw(g_win):
        last.append(g_win)
        RIDERS.book("l0_proj_bx", to_owner_ride(0, last))

    def after_proj_bx():
        owner_sum(0, [0], last, RIDERS.result("l0_proj_bx"))
        RIDERS.book("l0_norm1_b", scatter_ride(0, [part_sum[0][0]]))

    hooks = dict(ffn_b=after_ffn_b, wout_bw=after_wout_bw, ssd_b=after_ssd_b, proj_bw=after_proj_bw, proj_bx=after_proj_bx)
    dx, _, small_g[0], dmods[0] = _layer_backward(0, dx, keeps[0], modv[0], wts[0], sps[0], cs3, sn3, after=hooks)
    from_chips[1][2], (from_chips[1][0], from_chips[1][1]) = RIDERS.result("l0_up_bx")[0], RIDERS.result("l0_up_bw")
    from_chips[1][3] = RIDERS.result("l0_norm2_b")[0]
    for host, k in (("l0_attn0_b", 2), ("l0_attn1_b", 3), ("l0_attn2_b", 1), ("l0_norm1_b", 0)):
        from_chips[0][k] = RIDERS.result(host)[0]
    mine = [sum_chips_mine(f"sum_chips_{n}", part_sum[0][k], from_chips[0][k], part_sum[1][k], from_chips[1][k])
            for k, n in enumerate(_BIG)]
    reduced = swap_layers("swap_r", mine, me[2])
    grads = {n: jnp.stack(r) for n, r in zip(_BIG, reduced)}

    part = dict(ada_b=jnp.stack(dmods), final_g=d_final[0])
    for n in _SMALL:
        if n not in part:
            part[n] = jnp.stack([small_g[i][n] for i in range(DEPTH)])
    full_shapes = [part[n].shape for n in _SMALL]
    gs = allgather8("gather_small", _pack([part[n] for n in _SMALL]))
    tot = _unpack(sum_slots("sum_small", gs, 8)[0], full_shapes)
    small_tot = dict(zip(_SMALL, tot))
    dmod_all = gs[:, :DEPTH * 6 * d // 128, :].reshape(8, DEPTH, 6 * d)
    for n, ncol in _COL_SHARDED_SMALL.items():
        small_tot[n] = lax.dynamic_slice_in_dim(small_tot[n], chip * ncol, ncol, axis=2)
    grads.update(small_tot)

    ncol = ada_w.shape[2]
    dm = lax.dynamic_slice_in_dim(dmod_all, chip * ncol, ncol, axis=2).transpose(1, 0, 2)
    upd = {}
    g_ada, *upd["ada_w"] = ada_backward("ada_bwd", c16, jnp.pad(dm, ((0, 0), (0, 8), (0, 0))), ada_w, m["ada_w"], v["ada_w"])
    grads["ada_w"] = g_ada

    for n in _BIG:
        upd[n] = adamw(f"adam_{n}", w[n], grads[n], m[n], v[n])
    shapes_s = [w[n].shape for n in _SMALL]
    packed = [_pack([src[n] for n in _SMALL]) for src in (w, grads, m, v)]
    outs_s = [_unpack(o, shapes_s) for o in adamw("adam_small", *packed)]
    for q, n in enumerate(_SMALL):
        upd[n] = [outs_s[0][q], outs_s[1][q], outs_s[2][q]]

    return (loss, dx, *[grads[n] for n in _WEIGHTS], *[upd[n][0] for n in _WEIGHTS], *[upd[n][1] for n in _WEIGHTS],
            *[upd[n][2] for n in _WEIGHTS])


class Ride:
    def __init__(self, ins, out_shapes, nsem, start, finish):
        self.ins, self.out_shapes, self.nsem, self.start, self.finish = ins, out_shapes, nsem, start, finish

    def specs(self):
        hbm = pl.BlockSpec(memory_space=pl.ANY)
        return [hbm] * len(self.ins), [hbm] * len(self.out_shapes), [pltpu.SemaphoreType.DMA((self.nsem,))] * 2

    def begin(self, in_refs, out_refs, sems, cond=None):
        me = (lax.axis_index("x"), lax.axis_index("y"), lax.axis_index("c"))
        go = lambda: self.start(in_refs, out_refs, sems[0], sems[1], me)
        go() if cond is None else pl.when(cond)(go)

    def end(self, in_refs, out_refs, sems, cond=None):
        me = (lax.axis_index("x"), lax.axis_index("y"), lax.axis_index("c"))
        go = lambda: self.finish(in_refs, out_refs, sems[0], sems[1], me)
        go() if cond is None else pl.when(cond)(go)


def ride_alone(name, ride):
    ni, no = len(ride.ins), len(ride.out_shapes)

    def body(*refs):
        ride.begin(refs[:ni], refs[ni:ni + no], refs[ni + no:])
        ride.end(refs[:ni], refs[ni:ni + no], refs[ni + no:])

    in_specs, out_specs, scratch = ride.specs()
    return list(pl.pallas_call(body, name=name, in_specs=in_specs, out_specs=out_specs, out_shape=ride.out_shapes,
                               scratch_shapes=scratch)(*ride.ins))


def mm(name, a, b, mode, out_dtype=F32, res=None, gate=None, tm=1408, tn=1536, tk=1408, into=None):
    ride = RIDERS.take(name)
    if mode == "nn":
        (m, k), n = a.shape, b.shape[1]
    elif mode == "nt":
        (m, k), n = a.shape, b.shape[0]
    else:
        (k, m), n = a.shape, b.shape[1]
    tm, tn, tk = _tile(m, tm), _tile(n, tn), _tile(k, tk)
    ni, nj, nk = m // tm, n // tn, k // tk
    a_spec = pl.BlockSpec((tk, tm), lambda i, j, q: (q, i)) if mode == "tn" else pl.BlockSpec((tm, tk), lambda i, j, q: (i, q))
    b_spec = pl.BlockSpec((tn, tk), lambda i, j, q: (j, q)) if mode == "nt" else pl.BlockSpec((tk, tn), lambda i, j, q: (q, j))
    o_spec = pl.BlockSpec((tm, tn), lambda i, j, q: (i, j))
    fused = res is not None
    lead = 0 if into is None else len(into[0]) - 2
    first = (0,) * lead + (slice(None), slice(None))
    ins, in_specs = [a, b], [a_spec, b_spec]
    out_shape, out_specs = [jax.ShapeDtypeStruct((m, n), out_dtype)], [o_spec]
    if fused:
        ins += [res, gate]
        in_specs += [o_spec, pl.BlockSpec((1, tn), lambda i, j, q: (0, j))]
        out_shape.append(jax.ShapeDtypeStruct((m, n), F32))
        out_specs.append(o_spec)
    if into is not None:
        shape, omap = into
        out_shape = [jax.ShapeDtypeStruct(shape, out_dtype)]
        out_specs = [pl.BlockSpec((1,) * lead + (tm, tn), lambda i, j, q: omap(i, j))]
    n_in, n_out = len(ins), len(out_shape)
    scratch = [pltpu.VMEM((tm, tn), F32)]
    if ride is not None:
        r_in, r_out, r_scr = ride.specs()
        ins, in_specs = ins + list(ride.ins), in_specs + r_in
        out_shape, out_specs = out_shape + list(ride.out_shapes), out_specs + r_out
        scratch = scratch + r_scr

    def body(*refs):
        a_ref, b_ref = refs[:2]
        o_ref = refs[len(ins)]
        acc = refs[len(ins) + len(out_shape)]
        i, j, q = pl.program_id(0), pl.program_id(1), pl.program_id(2)
        at = lambda x, y, z: jnp.logical_and(jnp.logical_and(i == x, j == y), q == z)
        r_refs = (refs[n_in:len(ins)], refs[len(ins) + n_out:len(ins) + len(out_shape)], refs[len(ins) + len(out_shape) + 1:])
        if ride is not None:
            ride.begin(*r_refs, at(0, 0, 0))

        @pl.when(q == 0)
        def _():
            acc[...] = jnp.zeros(acc.shape, F32)

        acc[...] += _mxu(a_ref[...], b_ref[...], mode)

        @pl.when(q == nk - 1)
        def _():
            o_ref[first] = acc[...].astype(o_ref.dtype)
            if fused:
                refs[len(ins) + 1][...] = refs[2][...] + refs[3][...] * acc[...]

        if ride is not None:
            ride.end(*r_refs, at(ni - 1, nj - 1, nk - 1))

    sem = ("arbitrary",) * 3 if ride is not None else ("parallel", "parallel", "arbitrary")
    out = pl.pallas_call(
        body, name=name, grid=(ni, nj, nk), in_specs=in_specs, out_specs=out_specs, out_shape=out_shape, scratch_shapes=scratch,
        compiler_params=pltpu.CompilerParams(dimension_semantics=sem, vmem_limit_bytes=VMEM_LIMIT_BYTES),
    )(*ins)
    if ride is not None:
        RIDERS.done[name] = list(out[n_out:])
    return tuple(out[:n_out]) if fused else out[0]


def add_arrays(name, arrs, out_dtype=F32):
    nb, r, c = arrs[0].shape
    t = _tile(r, 256, 8)
    (out,), _ = scan_fwd(name, _sum_fn, nb=nb, nchunk=r // t, t=t, rows=[Row(a, fb=lambda b: b) for a in arrs], vecs=[], carries=[],
                         outs=[out_row((nb, r, c), out_dtype, fb=lambda b: b)], save=False)
    return out


def _sum_chips_mine_fn(ci, b, carries, rows, vecs):
    mine_layer = lax.axis_index("c")
    chip = 2 * lax.axis_index("x") + lax.axis_index("y")
    tot = None
    for j in range(4):
        own = jnp.where(mine_layer == 0, rows[j], rows[8 + j])
        sent = jnp.where(mine_layer == 0, rows[4 + j], rows[12 + j])
        term = jnp.where(chip == j, own, sent)
        tot = term if tot is None else tot + term
    return [], [tot]


def sum_chips_mine(name, p0, q0, p1, q1):
    _, r, c = p0.shape
    t = _tile(r, 256, 8)
    rows = [Row(a, fb=(lambda b, j=j: j)) for a in (p0, q0, p1, q1) for j in range(4)]
    (out,), _ = scan_fwd(name, _sum_chips_mine_fn, nb=1, nchunk=r // t, t=t, rows=rows, vecs=[], carries=[],
                         outs=[out_row((1, r, c))], save=False)
    return out[0]


def _remote(src, dst, send_sems, recv_sems, k, to):
    return pltpu.make_async_remote_copy(src_ref=src, dst_ref=dst, send_sem=send_sems.at[k], recv_sem=recv_sems.at[k],
                                        device_id=to, device_id_type=MESH)


def gather_ride(layer, shards):
    na = len(shards)

    def start(ins, outs, ss, rs, me):
        @pl.when(me[2] == layer)
        def _():
            for k in range(na):
                for p, mask in enumerate(CHIP_PEERS):
                    _remote(ins[k].at[layer], outs[k].at[_chip(me)], ss, rs, 6 * k + p, _flip(mask, me)).start()

    def finish(ins, outs, ss, rs, me):
        sibling = _flip(SIBLING[0], me)

        @pl.when(me[2] == layer)
        def _():
            for k in range(na):
                for p, mask in enumerate(CHIP_PEERS):
                    slot = outs[k].at[_chip(_flip(mask, me))]
                    _remote(ins[k].at[layer], slot, ss, rs, 6 * k + p, _flip(mask, me)).wait_recv()
                    _remote(slot, slot, ss, rs, 6 * k + 3 + p, sibling).start()
            for k in range(na):
                for p, mask in enumerate(CHIP_PEERS):
                    slot = outs[k].at[_chip(_flip(mask, me))]
                    _remote(ins[k].at[layer], slot, ss, rs, 6 * k + p, _flip(mask, me)).wait_send()
                    _remote(slot, slot, ss, rs, 6 * k + 3 + p, sibling).wait_send()

        @pl.when(me[2] != layer)
        def _():
            for k in range(na):
                for p, mask in enumerate(CHIP_PEERS):
                    slot = outs[k].at[_chip(_flip(mask, me))]
                    _remote(slot, slot, ss, rs, 6 * k + 3 + p, sibling).wait_recv()

    return Ride(list(shards), [jax.ShapeDtypeStruct((4,) + a.shape[1:], a.dtype) for a in shards], 6 * na, start, finish)


def scatter_ride(layer, parts):
    na = len(parts)

    def start(ins, outs, ss, rs, me):
        @pl.when(me[2] == layer)
        def _():
            for k in range(na):
                for p, mask in enumerate(CHIP_PEERS):
                    peer = _flip(mask, me)
                    _remote(ins[k].at[_chip(peer)], outs[k].at[_chip(me)], ss, rs, 3 * k + p, peer).start()

    def finish(ins, outs, ss, rs, me):
        @pl.when(me[2] == layer)
        def _():
            for k in range(na):
                for p, mask in enumerate(CHIP_PEERS):
                    peer = _flip(mask, me)
                    _remote(ins[k].at[_chip(peer)], outs[k].at[_chip(peer)], ss, rs, 3 * k + p, peer).wait_recv()
                    _remote(ins[k].at[_chip(peer)], outs[k].at[_chip(me)], ss, rs, 3 * k + p, peer).wait_send()

    return Ride(list(parts), [jax.ShapeDtypeStruct(a.shape, a.dtype) for a in parts], 3 * na, start, finish)


def to_owner_ride(layer, arrays):
    na = len(arrays)

    def start(ins, outs, ss, rs, me):
        @pl.when(me[2] != layer)
        def _():
            for k in range(na):
                _remote(ins[k], outs[k], ss, rs, k, _flip(SIBLING[0], me)).start()

    def finish(ins, outs, ss, rs, me):
        for k in range(na):
            cp = _remote(ins[k], outs[k], ss, rs, k, _flip(SIBLING[0], me))
            pl.when(me[2] != layer)(cp.wait_send)
            pl.when(me[2] == layer)(cp.wait_recv)

    return Ride(list(arrays), [jax.ShapeDtypeStruct(a.shape, a.dtype) for a in arrays], na, start, finish)


def _w_in_from_chips(a):
    c2 = a[2]
    pad = jnp.zeros((c2.shape[0], IN_WP - IN_W), c2.dtype)
    return jnp.concatenate([a[0], a[1], c2[:, :252], c2[:, 260:516], c2[:, 252:260], pad, c2[:, 516:], a[3]], axis=1)


def _mm_host(rides, rode, key, *args, **kw):
    if rides is None or key not in rides:
        return mm(*args, **kw)
    main, rode[key] = mm(*args, ride=rides[key], **kw)
    return main
```

```python
import functools

import numpy as np
import jax
import jax.numpy as jnp
from jax import lax
from jax.experimental import pallas as pl
from jax.experimental.pallas import tpu as pltpu

F32 = jnp.float32
BF16 = jnp.bfloat16
MESH = pl.DeviceIdType.MESH

D_MODEL = 1024
SEQ = 4096
DEPTH = 2
SSD_INNER = 512
SSD_HEADS = 8
SSD_STATE = 128
POOL_W = 256
POOL_WINDOWS = (2, 4, 8, 16)
ATT_W = 256
ATT_HEADS = 4
ATT_HEAD_DIM = 64
ATT_PATTERNS = ((128, 1), (512, 4), (2048, 16))
ATT_BLOCK = 128
ROT_DIM = 16
ROPE_THETA = 500000.0
IN_W = 2568
IN_WP = 2688
IN_MAIN = 1920
FFN_DIM = 2816
NORM_EPS = 1e-6
ADAM_LR, ADAM_B1, ADAM_B2, ADAM_EPS, ADAM_WD, ADAM_STEP = 0.001, 0.9, 0.999, 1e-08, 0.01, 10

VMEM_LIMIT_BYTES = 56 * 1024 * 1024
NEG = -1e30


def _mxu(a, b, mode):
    dims = {"nn": ((1,), (0,)), "nt": ((1,), (1,)), "tn": ((0,), (0,))}[mode]
    return lax.dot_general(a.astype(BF16), b.astype(BF16), (dims, ((), ())), preferred_element_type=F32)


@functools.partial(jax.custom_vjp, nondiff_argnums=(2,))
def _bdot(a, b, mode):
    return _mxu(a, b, mode)


def _bdot_fwd(a, b, mode):
    return _mxu(a, b, mode), (a, b)


def _bdot_bwd(mode, res, g):
    a, b = res
    if mode == "nn":
        return _mxu(g, b, "nt"), _mxu(a, g, "tn")
    if mode == "nt":
        return _mxu(g, b, "nn"), _mxu(g, a, "tn")
    return _mxu(b, g, "nt"), _mxu(a, g, "nn")


_bdot.defvjp(_bdot_fwd, _bdot_bwd)


def _iota(shape, dim):
    return lax.broadcasted_iota(jnp.int32, shape, dim)


def _make_shift(h):
    @functools.partial(jax.custom_vjp, nondiff_argnums=(2,))
    def shift(halo, cur, k):
        if k == 0:
            return cur
        full = jnp.concatenate([halo, cur], axis=0)
        return pltpu.roll(full, k, 0)[h:]

    def fwd(halo, cur, k):
        return shift(halo, cur, k), None

    def bwd(k, _, g):
        t, w = g.shape
        if k == 0:
            return jnp.zeros((h, w), F32), g
        d_cur = jnp.where(_iota((t, w), 0) < t - k, pltpu.roll(g, t - k, 0), 0.0)
        top = g[:h]
        d_halo = jnp.where(_iota((h, w), 0) >= h - k, pltpu.roll(top, h - k, 0) if k < h else top, 0.0)
        return d_halo, d_cur

    shift.defvjp(fwd, bwd)
    return shift


_shift8 = _make_shift(8)
_shift16 = _make_shift(16)


def _make_tail(h):
    @jax.custom_vjp
    def tail(x):
        return x[x.shape[0] - h:]

    def fwd(x):
        return tail(x), x.shape[0]

    def bwd(t, g):
        return (jnp.concatenate([jnp.zeros((t - h, g.shape[1]), F32), g], axis=0),)

    tail.defvjp(fwd, bwd)
    return tail


_tail8 = _make_tail(8)
_tail16 = _make_tail(16)


@jax.custom_vjp
def _cumsum_rows(x):
    t = x.shape[0]
    row, s = _iota(x.shape, 0), 1
    while s < t:
        x = x + jnp.where(row >= s, pltpu.roll(x, s, 0), 0.0)
        s *= 2
    return x


def _cumsum_rows_fwd(x):
    return _cumsum_rows(x), None


def _cumsum_rows_bwd(_, g):
    t = g.shape[0]
    row, s = _iota(g.shape, 0), 1
    while s < t:
        g = g + jnp.where(row < t - s, pltpu.roll(g, t - s, 0), 0.0)
        s *= 2
    return (g,)


_cumsum_rows.defvjp(_cumsum_rows_fwd, _cumsum_rows_bwd)


@jax.custom_vjp
def _rot_pairs(t):
    e = _iota(t.shape, 1) % ATT_HEAD_DIM
    n = t.shape[1]
    return jnp.where(e < 8, -pltpu.roll(t, n - 8, 1), jnp.where(e < 16, pltpu.roll(t, 8, 1), 0.0))


def _rot_pairs_fwd(t):
    return _rot_pairs(t), None


def _rot_pairs_bwd(_, g):
    e = _iota(g.shape, 1) % ATT_HEAD_DIM
    n = g.shape[1]
    return (pltpu.roll(jnp.where(e < 8, -g, 0.0), 8, 1) + pltpu.roll(jnp.where(jnp.logical_and(e >= 8, e < 16), g, 0.0), n - 8, 1),)


_rot_pairs.defvjp(_rot_pairs_fwd, _rot_pairs_bwd)


def _make_thirds():
    @jax.custom_vjp
    def thirds(x):
        w = x.shape[1] // 3
        return x[:, :w], x[:, w:2 * w], x[:, 2 * w:]

    def fwd(x):
        return thirds(x), None

    def bwd(_, g):
        return (jnp.concatenate(g, axis=1),)

    thirds.defvjp(fwd, bwd)
    return thirds


_thirds = _make_thirds()


def _rowk(w, k):
    return jnp.sum(jnp.where(_iota(w.shape, 0) == k, w, 0.0), axis=0, keepdims=True)


def _silu(x):
    return x * (0.5 * jnp.tanh(0.5 * x) + 0.5)


def _softplus(x):
    return jnp.maximum(x, 0.0) + jnp.log(1.0 + jnp.exp(-jnp.abs(x)))


def _tile(dim, target, unit=128):
    if dim <= target:
        return dim
    best = None
    for t in range(unit, target + 1, unit):
        if dim % t == 0:
            best = t
    assert best is not None, (dim, target)
    return best


class Ride:
    def __init__(self, ins, out_shapes, nsem, start, finish):
        self.ins, self.out_shapes, self.nsem, self.start, self.finish = ins, out_shapes, nsem, start, finish

    def specs(self):
        hbm = pl.BlockSpec(memory_space=pl.ANY)
        return [hbm] * len(self.ins), [hbm] * len(self.out_shapes), [pltpu.SemaphoreType.DMA((self.nsem,))] * 2

    def begin(self, in_refs, out_refs, sems, cond=None):
        me = (lax.axis_index("x"), lax.axis_index("y"), lax.axis_index("c"))
        go = lambda: self.start(in_refs, out_refs, sems[0], sems[1], me)
        go() if cond is None else pl.when(cond)(go)

    def end(self, in_refs, out_refs, sems, cond=None):
        me = (lax.axis_index("x"), lax.axis_index("y"), lax.axis_index("c"))
        go = lambda: self.finish(in_refs, out_refs, sems[0], sems[1], me)
        go() if cond is None else pl.when(cond)(go)


class _Riders:
    def reset(self):
        self.booked, self.done = {}, {}

    def book(self, host, ride):
        assert host not in self.booked, host
        self.booked[host] = ride

    def take(self, host):
        return self.booked.pop(host, None)

    def result(self, host):
        return self.done[host]


RIDERS = _Riders()
RIDERS.reset()


class Row:
    def __init__(self, arr, w=None, fb=None, fc=None, diff=True, slot=False, dcols=None, dfc=None, ddtype=F32, view=None):
        self.ddtype = ddtype
        self.view = view
        self.arr = arr
        self.w = arr.shape[2] if w is None else w
        self.fb = (lambda b: 0) if fb is None else fb
        self.fc = (lambda b: 0) if fc is None else fc
        self.diff = diff
        self.slot = slot
        self.dcols = dcols
        self.dfc = dfc


class Vec:
    def __init__(self, arr, w=None, fc=None, diff=True):
        self.arr = arr
        self.w = arr.shape[1] if w is None else w
        self.fc = fc
        self.diff = diff


def _row_spec(r, t, nchunk, reverse):
    shape = (1, t, r.w) if r.view is None else (1, t // r.view, r.view * r.w)
    if reverse:
        return pl.BlockSpec(shape, lambda b, i, r=r: (r.fb(b), nchunk - 1 - i, r.fc(b)))
    return pl.BlockSpec(shape, lambda b, i, r=r: (r.fb(b), i, r.fc(b)))


def _load_row(ref, r, t, scr):
    if r.view is None:
        return ref[0]
    d, w = r.view, r.w
    for q in range(d):
        for j in range(w // 128):
            scr[j, pl.ds(q, t // d, stride=d), :] = ref[0, :, q * w + 128 * j:q * w + 128 * (j + 1)].astype(F32)
    return jnp.concatenate([scr[j] for j in range(w // 128)], axis=1)


def _store_row(ref, r, t, scr, val):
    if r.view is None:
        ref[0] = val.astype(ref.dtype)
        return
    d, w = r.view, r.w
    for j in range(w // 128):
        scr[j] = val[:, 128 * j:128 * (j + 1)]
    for q in range(d):
        for j in range(w // 128):
            ref[0, :, q * w + 128 * j:q * w + 128 * (j + 1)] = scr[j, pl.ds(q, t // d, stride=d), :].astype(ref.dtype)


def _view_scratch(specs, t):
    ws = [r.w for r in specs if r.view is not None]
    return [pltpu.VMEM((max(ws) // 128, t, 128), F32)] if ws else []


def _vec_spec(v):
    if v.fc is None:
        return pl.BlockSpec(v.arr.shape, lambda b, i: (0, 0))
    return pl.BlockSpec((v.arr.shape[0], v.w), lambda b, i, v=v: (0, v.fc(b)))


def _cparams():
    return pltpu.CompilerParams(dimension_semantics=("arbitrary", "arbitrary"), vmem_limit_bytes=VMEM_LIMIT_BYTES)


def scan_fwd(name, fn, *, nb, nchunk, t, rows, vecs, carries, outs, save):
    nr, nv, nc, no = len(rows), len(vecs), len(carries), len(outs)
    ns = nc if save else 0
    ride = RIDERS.take(name)
    r_in, r_out, r_scr = ride.specs() if ride else ([], [], [])

    def body(*refs):
        p = 0
        row_refs = refs[p:p + nr]; p += nr
        vec_refs = refs[p:p + nv]; p += nv
        ride_in = refs[p:p + len(r_in)]; p += len(r_in)
        out_refs = refs[p:p + no]; p += no
        save_refs = refs[p:p + ns]; p += ns
        ride_out = refs[p:p + len(r_out)]; p += len(r_out)
        car = refs[p:p + nc]; p += nc
        scr = refs[p] if stage else None
        sems = refs[p + len(stage):]
        b, i = pl.program_id(0), pl.program_id(1)
        if ride:
            ride.begin(ride_in, ride_out, sems, jnp.logical_and(b == 0, i == 0))
        if nc:
            @pl.when(i == 0)
            def _():
                for c_ref in car:
                    c_ref[...] = jnp.zeros(c_ref.shape, F32)
        cin = [c_ref[...] for c_ref in car]
        if save:
            for s_ref, cv in zip(save_refs, cin):
                s_ref[0, 0] = cv
        new_c, o = fn(i, b, cin, [_load_row(ref, r, t, scr) for ref, r in zip(row_refs, rows)], [v[...] for v in vec_refs])
        for c_ref, cv in zip(car, new_c):
            c_ref[...] = cv
        for o_ref, spec, ov in zip(out_refs, outs, o):
            _store_row(o_ref, spec, t, scr, ov)
        if ride:
            ride.end(ride_in, ride_out, sems, jnp.logical_and(b == nb - 1, i == nchunk - 1))

    stage = _view_scratch(list(rows) + list(outs), t)
    out_shape = [o.arr for o in outs]
    out_specs = [_row_spec(o, t, nchunk, False) for o in outs]
    if save:
        for cs in carries:
            out_shape.append(jax.ShapeDtypeStruct((nb, nchunk) + tuple(cs), F32))
            out_specs.append(pl.BlockSpec((1, 1) + tuple(cs), lambda b, i: (b, i, 0, 0)))
    res = pl.pallas_call(
        body, name=name, grid=(nb, nchunk),
        in_specs=[_row_spec(r, t, nchunk, False) for r in rows] + [_vec_spec(v) for v in vecs] + r_in,
        out_specs=out_specs + r_out, out_shape=out_shape + (list(ride.out_shapes) if ride else []),
        scratch_shapes=[pltpu.VMEM(tuple(cs), F32) for cs in carries] + stage + r_scr,
        compiler_params=_cparams(),
    )(*[r.arr for r in rows], *[v.arr for v in vecs], *(ride.ins if ride else []))
    if ride:
        RIDERS.done[name] = list(res[no + ns:])
    return list(res[:no]), list(res[no:no + ns])


def scan_bwd(name, fn, *, nb, nchunk, t, rows, vecs, carries, saved, douts, adds=None):
    adds = adds or {}
    nr, nv, nc, no = len(rows), len(vecs), len(carries), len(douts)
    dri = [k for k, r in enumerate(rows) if r.diff]
    dvi = [k for k, v in enumerate(vecs) if v.diff]
    add_keys = sorted(adds)
    na = len(add_keys)
    ride = RIDERS.take(name)
    r_in, r_out, r_scr = ride.specs() if ride else ([], [], [])

    def body(*refs):
        p = 0
        row_refs = refs[p:p + nr]; p += nr
        vec_refs = refs[p:p + nv]; p += nv
        save_refs = refs[p:p + nc]; p += nc
        dout_refs = refs[p:p + no]; p += no
        add_refs = refs[p:p + na]; p += na
        ride_in = refs[p:p + len(r_in)]; p += len(r_in)
        drow_refs = refs[p:p + len(dri)]; p += len(dri)
        dvec_refs = refs[p:p + len(dvi)]; p += len(dvi)
        ride_out = refs[p:p + len(r_out)]; p += len(r_out)
        dcar = refs[p:p + nc]; p += nc
        scr = refs[p] if stage else None
        sems = refs[p + len(stage):]
        b, ir = pl.program_id(0), pl.program_id(1)
        ci = nchunk - 1 - ir
        if ride:
            ride.begin(ride_in, ride_out, sems, jnp.logical_and(b == 0, ir == 0))
        if nc:
            @pl.when(ir == 0)
            def _():
                for c_ref in dcar:
                    c_ref[...] = jnp.zeros(c_ref.shape, F32)
        rows_v = [_load_row(ref, r, t, scr) for ref, r in zip(row_refs, rows)]
        vecs_v = [v[...] for v in vec_refs]
        cin = [s[0, 0] for s in save_refs]
        dc = [c_ref[...] for c_ref in dcar]
        dout_v = [_load_row(ref, r, t, scr).astype(F32) for ref, r in zip(dout_refs, douts)]

        def f(cs, dr, dv):
            rr, vv = list(rows_v), list(vecs_v)
            for k, idx in enumerate(dri):
                rr[idx] = dr[k]
            for k, idx in enumerate(dvi):
                vv[idx] = dv[k]
            return fn(ci, b, cs, rr, vv)

        _, vjp = jax.vjp(f, cin, [rows_v[k].astype(F32) for k in dri], [vecs_v[k].astype(F32) for k in dvi])
        dcin, drows, dvecs = vjp((dc, dout_v))
        for c_ref, cv in zip(dcar, dcin):
            c_ref[...] = cv
        for k, (o_ref, ov) in enumerate(zip(drow_refs, drows)):
            if dri[k] in adds:
                ov = ov + add_refs[add_keys.index(dri[k])][0].astype(F32)
            _store_row(o_ref, rows[dri[k]], t, scr, ov)
        for k, (o_ref, ov) in enumerate(zip(dvec_refs, dvecs)):
            first = (ir == 0) if vecs[dvi[k]].fc is not None else jnp.logical_and(ir == 0, b == 0)

            @pl.when(first)
            def _(o_ref=o_ref, ov=ov):
                o_ref[...] = ov

            @pl.when(jnp.logical_not(first))
            def _(o_ref=o_ref, ov=ov):
                o_ref[...] += ov

        if ride:
            ride.end(ride_in, ride_out, sems, jnp.logical_and(b == nb - 1, ir == nchunk - 1))

    stage = _view_scratch(list(rows) + list(douts), t)
    in_specs = ([_row_spec(r, t, nchunk, True) for r in rows] + [_vec_spec(v) for v in vecs]
                + [pl.BlockSpec((1, 1) + tuple(cs), lambda b, i: (b, nchunk - 1 - i, 0, 0)) for cs in carries]
                + [_row_spec(d, t, nchunk, True) for d in douts]
                + [_row_spec(adds[k], t, nchunk, True) for k in add_keys] + r_in)
    out_shape, out_specs = [], []
    for k in dri:
        r = rows[k]
        if r.slot:
            out_shape.append(jax.ShapeDtypeStruct((nb, r.arr.shape[1], r.w), r.ddtype))
            out_specs.append(pl.BlockSpec((1, t, r.w), lambda b, i: (b, nchunk - 1 - i, 0)))
        elif r.dcols is not None:
            out_shape.append(jax.ShapeDtypeStruct((r.arr.shape[0], r.arr.shape[1], r.dcols), r.ddtype))
            out_specs.append(pl.BlockSpec((1, t, r.w), lambda b, i, r=r: (r.fb(b), nchunk - 1 - i, r.dfc(b))))
        else:
            out_shape.append(jax.ShapeDtypeStruct(r.arr.shape, r.ddtype))
            out_specs.append(_row_spec(r, t, nchunk, True))
    for k in dvi:
        out_shape.append(jax.ShapeDtypeStruct(vecs[k].arr.shape, F32))
        out_specs.append(_vec_spec(vecs[k]))
    nd = len(dri) + len(dvi)
    res = pl.pallas_call(
        body, name=name, grid=(nb, nchunk), in_specs=in_specs, out_specs=out_specs + r_out,
        out_shape=out_shape + (list(ride.out_shapes) if ride else []),
        scratch_shapes=[pltpu.VMEM(tuple(cs), F32) for cs in carries] + stage + r_scr,
        compiler_params=_cparams(),
    )(*[r.arr for r in rows], *[v.arr for v in vecs], *saved, *[d.arr for d in douts], *[adds[k].arr for k in add_keys],
      *(ride.ins if ride else []))
    if ride:
        RIDERS.done[name] = list(res[nd:])
    return list(res[:len(dri)]), list(res[len(dri):nd])


def out_row(shape, dtype=F32, w=None, fb=None, fc=None):
    return Row(jax.ShapeDtypeStruct(shape, dtype), w, fb, fc)


def _conv(shift, halo, cur, w, bias, taps):
    y = bias
    for k in range(taps):
        y = y + _rowk(w, k) * shift(halo, cur, taps - 1 - k)
    return y


def _ssd_fn(ci, b, carries, rows, vecs):
    cx, cb_, cc, ht = carries
    z, xr, br, cr, dtr = rows
    cwx, cbx, cwb, cbb, cwc, cbc, dtb, alog, dsk, ng = vecs
    t = z.shape[0]
    xs = _silu(_conv(_shift8, cx, xr, cwx, cbx, 4))
    bm = _silu(_conv(_shift8, cb_, br, cwb, cbb, 4))
    cm = _silu(_conv(_shift8, cc, cr, cwc, cbc, 4))
    dt = _softplus(dtr + dtb)
    acol = _cumsum_rows(dt * (-jnp.exp(alog)))
    arow = acol.T
    r, c = _iota((t, t), 0), _iota((t, t), 1)
    causal = r >= c
    cbm = _bdot(cm, bm, "nt")
    lane, sub = _iota(acol.shape, 1), _iota(arow.shape, 0)
    colh = _iota(xs.shape, 1) // 64
    a, dtx, dx, acs = jnp.zeros(xs.shape, F32), jnp.zeros(xs.shape, F32), jnp.zeros((1, xs.shape[1]), F32), []
    for j in range(4):
        h = 4 * b + j
        ac = jnp.sum(jnp.where(lane == h, acol, 0.0), axis=1, keepdims=True)
        acs.append(ac)
        a = jnp.where(colh == j, ac, a)
        dtx = jnp.where(colh == j, jnp.sum(jnp.where(lane == h, dt, 0.0), axis=1, keepdims=True), dtx)
        dx = jnp.where(_iota(dx.shape, 1) // 64 == j, jnp.sum(jnp.where(_iota(dsk.shape, 1) == h, dsk, 0.0), axis=1, keepdims=True), dx)
    atot = jnp.sum(jnp.where(_iota(a.shape, 0) == t - 1, a, 0.0), axis=0, keepdims=True)
    x = xs * dtx
    ydiag = jnp.zeros(x.shape, F32)
    for j in range(4):
        ar = jnp.sum(jnp.where(sub == 4 * b + j, arow, 0.0), axis=0, keepdims=True)
        lmat = jnp.exp(jnp.where(causal, acs[j] - ar, NEG))
        ydiag = ydiag + _bdot(cbm * lmat, jnp.where(colh == j, x, 0.0), "nn")
    yoff = _bdot(cm, ht, "nn") * jnp.exp(a)
    ht_new = ht * jnp.exp(atot) + _bdot(bm, x * jnp.exp(atot - a), "tn")
    y = ydiag + yoff + dx * xs
    yz = y * _silu(z)
    yn = yz * lax.rsqrt(jnp.mean(yz * yz, axis=-1, keepdims=True) + NORM_EPS) * ng
    return [_tail8(xr), _tail8(br), _tail8(cr), ht_new], [yn]


_SSD_T = 256
_SSD_CARRIES = [(8, 256), (8, 128), (8, 128), (128, 256)]


def _ssd_io(proj3, p):
    own = lambda b: b
    rows = [Row(proj3, 256, fc=own, dcols=512, dfc=own, ddtype=BF16),
            Row(proj3, 256, fc=lambda b: 2 + b, dcols=512, dfc=own, ddtype=BF16),
            Row(proj3, 128, fc=lambda b: 8 + b, dcols=256, dfc=own, ddtype=BF16),
            Row(proj3, 128, fc=lambda b: 10 + b, dcols=256, dfc=own, ddtype=BF16),
            Row(proj3, 128, fc=lambda b: 14, slot=True)]
    vecs = [Vec(p["cw"], 256, lambda b: b), Vec(p["cb"], 256, lambda b: b),
            Vec(p["cw"], 128, lambda b: 4 + b), Vec(p["cb"], 128, lambda b: 4 + b),
            Vec(p["cw"], 128, lambda b: 6 + b), Vec(p["cb"], 128, lambda b: 6 + b),
            Vec(p["dtb"]), Vec(p["alog"]), Vec(p["dsk"]), Vec(p["ng"], 256, lambda b: b)]
    return rows, vecs


def ssd_forward(name, proj3, p):
    rows, vecs = _ssd_io(proj3, p)
    s = proj3.shape[1]
    (y,), saved = scan_fwd(name, _ssd_fn, nb=2, nchunk=s // _SSD_T, t=_SSD_T, rows=rows, vecs=vecs,
                           carries=_SSD_CARRIES, outs=[out_row((1, s, SSD_INNER), BF16, 256, fc=lambda b: b)], save=True)
    return y, saved


def ssd_backward(name, proj3, p, saved, dmix3):
    rows, vecs = _ssd_io(proj3, p)
    s = proj3.shape[1]
    drows, dvecs = scan_bwd(name, _ssd_fn, nb=2, nchunk=s // _SSD_T, t=_SSD_T, rows=rows, vecs=vecs,
                            carries=_SSD_CARRIES, saved=saved, douts=[Row(dmix3, 256, fc=lambda b: b)])
    return drows, dvecs


def _pool_fn(ci, b, carries, rows, vecs):
    (cu,) = carries
    (u,) = rows
    wbd, scale = vecs
    t = u.shape[0]
    pos = ci * t + _iota(u.shape, 0)
    grp = _iota(u.shape, 1) // 64
    acc, pooled, k = u, jnp.zeros(u.shape, F32), 1
    for gi, w in enumerate(POOL_WINDOWS):
        while k < w:
            acc = acc + _shift16(cu, u, k)
            k += 1
        pooled = jnp.where(grp == gi, acc / jnp.minimum(pos + 1, w).astype(F32), pooled)
    y = _bdot(pooled - u, wbd, "nn") * scale
    return [_tail16(u)], [y]


_POOL_T = 256


def _pool_io(proj3, wbd, scale):
    return [Row(proj3, 256, fc=lambda b: 6, dcols=256, dfc=lambda b: 0, ddtype=BF16)], [Vec(wbd), Vec(scale)]


def pool_forward(name, proj3, wbd, scale):
    rows, vecs = _pool_io(proj3, wbd, scale)
    s = proj3.shape[1]
    (y,), saved = scan_fwd(name, _pool_fn, nb=1, nchunk=s // _POOL_T, t=_POOL_T, rows=rows, vecs=vecs,
                           carries=[(16, 256)], outs=[out_row((1, s, POOL_W), BF16)], save=True)
    return y, saved


def pool_backward(name, proj3, wbd, scale, saved, dmix3):
    rows, vecs = _pool_io(proj3, wbd, scale)
    s = proj3.shape[1]
    return scan_bwd(name, _pool_fn, nb=1, nchunk=s // _POOL_T, t=_POOL_T, rows=rows, vecs=vecs,
                    carries=[(16, 256)], saved=saved, douts=[Row(dmix3, 256, fc=lambda b: 2)])


def _attn_fn(ci, b, carries, rows, vecs):
    kp, vp = carries
    qr, kr, v = _thirds(rows[0])
    scale = ATT_HEAD_DIM ** -0.5
    q = qr
    n = q.shape[0]
    r, c = _iota((n, n), 0), _iota((n, n), 1)
    prev_ok, cur_ok = jnp.logical_and(c >= r, ci > 0), r >= c
    head = _iota(q.shape, 1) // ATT_HEAD_DIM
    o, lse = jnp.zeros(q.shape, F32), jnp.zeros(q.shape, F32)
    for h in range(ATT_HEADS):
        mine = head == h
        qh = jnp.where(mine, qr, 0.0)
        sp = jnp.where(prev_ok, _bdot(qh, kp, "nt") * scale, NEG)
        sc = jnp.where(cur_ok, _bdot(qh, kr, "nt") * scale, NEG)
        m = lax.stop_gradient(jnp.maximum(jnp.max(sp, axis=1, keepdims=True), jnp.max(sc, axis=1, keepdims=True)))
        pp, pc = jnp.exp(sp - m), jnp.exp(sc - m)
        l = jnp.sum(pp, axis=1, keepdims=True) + jnp.sum(pc, axis=1, keepdims=True)
        o = jnp.where(mine, (_bdot(pp, vp, "nn") + _bdot(pc, v, "nn")) / l, o)
        lse = jnp.where(mine, m + jnp.log(l), lse)
    return [kr, v], [o, lse]


_ATT_CARRIES = [(ATT_BLOCK, ATT_W), (ATT_BLOCK, ATT_W)]


def attn_forward(name, pv, d):
    l = pv.shape[1]
    own = lambda b: b
    outs = [out_row((1, l, d * ATT_W), F32, ATT_W, fc=own) for _ in range(2)]
    (o, lse), saved = scan_fwd(name, _attn_fn, nb=d, nchunk=l // ATT_BLOCK, t=ATT_BLOCK, rows=[Row(pv, 3 * ATT_W, fc=own)],
                               vecs=[], carries=_ATT_CARRIES, outs=outs, save=True)
    return o, lse, saved


def attn_backward(name, pv, d, saved, do, dlse):
    l = pv.shape[1]
    own = lambda b: b
    (dpv,), _ = scan_bwd(name, _attn_fn, nb=d, nchunk=l // ATT_BLOCK, t=ATT_BLOCK, rows=[Row(pv, 3 * ATT_W, fc=own)], vecs=[],
                         carries=_ATT_CARRIES, saved=saved, douts=[Row(do, ATT_W, fc=own), Row(dlse, ATT_W, fc=own)])
    return dpv


def _rope_fn(ci, b, carries, rows, vecs):
    x, cs, sn = rows
    return [], [x * cs + _rot_pairs(x) * sn]


def _rope3_fn(ci, b, carries, rows, vecs):
    _, (y,) = _rope_fn(ci, b, carries, rows, vecs)
    return [], [y, y, y]


def _by_residue(a_or_shape, w, d):
    if isinstance(a_or_shape, tuple):
        _, s, _ = a_or_shape
        return Row(jax.ShapeDtypeStruct((1, s // d, d * w), F32), w, view=None if d == 1 else d)
    return Row(a_or_shape, w, view=None if d == 1 else d)


def rope_forward(name, qkv3, cs3, sn3):
    s, w = qkv3.shape[1], qkv3.shape[2]
    ys, _ = scan_fwd(name, _rope3_fn, nb=1, nchunk=s // _ROW_T, t=_ROW_T, vecs=[], carries=[], save=False,
                     rows=[Row(qkv3), Row(cs3, diff=False), Row(sn3, diff=False)],
                     outs=[_by_residue(qkv3.shape, w, d) for _, d in ATT_PATTERNS])
    return ys


def rope_backward(name, qkv3, cs3, sn3, dys):
    s, w = qkv3.shape[1], qkv3.shape[2]
    (dx,), _ = scan_bwd(name, _rope3_fn, nb=1, nchunk=s // _ROW_T, t=_ROW_T, vecs=[], carries=[], saved=[],
                        rows=[Row(qkv3, ddtype=BF16), Row(cs3, diff=False), Row(sn3, diff=False)],
                        douts=[_by_residue(a, w, d) for a, (_, d) in zip(dys, ATT_PATTERNS)])
    return dx


def _merge_fn(ci, b, carries, rows, vecs):
    o1, o2, o3, l1, l2, l3 = rows
    mx = lax.stop_gradient(jnp.maximum(l1, jnp.maximum(l2, l3)))
    e1, e2, e3 = jnp.exp(l1 - mx), jnp.exp(l2 - mx), jnp.exp(l3 - mx)
    return [], [(e1 * o1 + e2 * o2 + e3 * o3) / (e1 + e2 + e3)]


_ROW_T = 512


def _merge_rows(os_, ls_):
    ds = [d for _, d in ATT_PATTERNS]
    return [_by_residue(a, ATT_W, d) for a, d in zip(os_, ds)] + [_by_residue(a, ATT_W, d) for a, d in zip(ls_, ds)]


def merge_forward(name, os_, ls_, s):
    (y,), _ = scan_fwd(name, _merge_fn, nb=1, nchunk=s // _ROW_T, t=_ROW_T, rows=_merge_rows(os_, ls_), vecs=[],
                       carries=[], outs=[out_row((1, s, ATT_W), BF16)], save=False)
    return y


def merge_backward(name, os_, ls_, dmix3):
    s = dmix3.shape[1]
    drows, _ = scan_bwd(name, _merge_fn, nb=1, nchunk=s // _ROW_T, t=_ROW_T, rows=_merge_rows(os_, ls_), vecs=[],
                        carries=[], saved=[], douts=[Row(dmix3, 256, fc=lambda b: 3)])
    return drows


def _norm_mod_fn(ci, b, carries, rows, vecs):
    (x,) = rows
    g, sc, sh = vecs
    xn = x * lax.rsqrt(jnp.mean(x * x, axis=-1, keepdims=True) + NORM_EPS)
    return [], [xn * g * (1.0 + sc) + sh]


def norm_mod_forward(name, x3, g, sc, sh):
    s = x3.shape[1]
    (h,), _ = scan_fwd(name, _norm_mod_fn, nb=1, nchunk=s // _ROW_T, t=_ROW_T, rows=[Row(x3)], vecs=[Vec(g), Vec(sc), Vec(sh)],
                       carries=[], outs=[out_row(x3.shape, BF16)], save=False)
    return h


def norm_mod_backward(name, x3, g, sc, sh, dh3, add3):
    s = x3.shape[1]
    (dx,), dv = scan_bwd(name, _norm_mod_fn, nb=1, nchunk=s // _ROW_T, t=_ROW_T, rows=[Row(x3)], vecs=[Vec(g), Vec(sc), Vec(sh)],
                         carries=[], saved=[], douts=[Row(dh3)], adds={0: Row(add3)})
    return dx, dv


def _gate_fn(ci, b, carries, rows, vecs):
    return [], [rows[0] * vecs[0]]


def gate_backward(name, o3, g, dx3):
    s = o3.shape[1]
    (do,), (dg,) = scan_bwd(name, _gate_fn, nb=1, nchunk=s // _ROW_T, t=_ROW_T, rows=[Row(o3, ddtype=BF16)], vecs=[Vec(g)],
                            carries=[], saved=[], douts=[Row(dx3)])
    return do, dg


def _make_halves():
    @jax.custom_vjp
    def halves(x):
        h = x.shape[1] // 2
        return x[:, :h], x[:, h:]

    def fwd(x):
        return halves(x), None

    def bwd(_, g):
        return (jnp.concatenate(g, axis=1),)

    halves.defvjp(fwd, bwd)
    return halves


_halves = _make_halves()


def _ffn_fn(ci, b, carries, rows, vecs):
    (cu,) = carries
    (u,) = rows
    w, bias = vecs
    hg, hu = _halves(_conv(_shift8, cu, u, w, bias, 3))
    return [_tail8(u)], [_silu(hg) * hu]


_FFN_T = 256
_FFN_CW = FFN_DIM // 2
_FFN_CARRIES = [(8, 2 * _FFN_CW)]
FFN_BLOCK_ORDER = [0, 2, 1, 3]


def _ffn_io(up3, cw, cb):
    own = lambda b: b
    return [Row(up3, 2 * _FFN_CW, fc=own, ddtype=BF16)], [Vec(cw, 2 * _FFN_CW, own), Vec(cb, 2 * _FFN_CW, own)]


def ffn_down_forward(name, up3, cw, cb, w_down, res, gate):
    s, t, cw2 = up3.shape[1], _FFN_T, 2 * _FFN_CW
    d = w_down.shape[1]
    nchunk = s // t
    ride = RIDERS.take(name)
    r_in, r_out, r_scr = ride.specs() if ride else ([], [], [])

    def body(*refs):
        up_ref, cw_ref, cb_ref, wd_ref, res_ref, g_ref = refs[:6]
        ride_in = refs[6:6 + len(r_in)]
        act_ref, save_ref, dn_ref, x2_ref = refs[6 + len(r_in):10 + len(r_in)]
        ride_out = refs[10 + len(r_in):10 + len(r_in) + len(r_out)]
        car, acc = refs[10 + len(r_in) + len(r_out):12 + len(r_in) + len(r_out)]
        sems = refs[12 + len(r_in) + len(r_out):]
        i, b = pl.program_id(0), pl.program_id(1)
        if ride:
            ride.begin(ride_in, ride_out, sems, jnp.logical_and(i == 0, b == 0))

        @pl.when(i == 0)
        def _():
            car[b] = jnp.zeros(car.shape[1:], F32)

        cin = car[b]
        save_ref[0, 0] = cin
        (new_c,), (act,) = _ffn_fn(i, b, [cin], [up_ref[0]], [cw_ref[...], cb_ref[...]])
        car[b] = new_c
        act_ref[0] = act.astype(act_ref.dtype)
        part = _mxu(act, wd_ref[...], "nn")

        @pl.when(b == 0)
        def _():
            acc[...] = part

        @pl.when(b == 1)
        def _():
            tot = acc[...] + part
            dn_ref[...] = tot
            x2_ref[...] = res_ref[...] + g_ref[...] * tot

        if ride:
            ride.end(ride_in, ride_out, sems, jnp.logical_and(i == nchunk - 1, b == 1))

    tile = pl.BlockSpec((t, d), lambda i, b: (i, 0))
    out = pl.pallas_call(
        body, name=name, grid=(nchunk, 2),
        in_specs=[pl.BlockSpec((1, t, cw2), lambda i, b: (0, i, b)), pl.BlockSpec((cw.shape[0], cw2), lambda i, b: (0, b)),
                  pl.BlockSpec((1, cw2), lambda i, b: (0, b)), pl.BlockSpec((_FFN_CW, d), lambda i, b: (b, 0)), tile,
                  pl.BlockSpec((1, d), lambda i, b: (0, 0))] + r_in,
        out_specs=[pl.BlockSpec((1, t, _FFN_CW), lambda i, b: (0, i, b)), pl.BlockSpec((1, 1, 8, cw2), lambda i, b: (b, i, 0, 0)),
                   tile, tile] + r_out,
        out_shape=[jax.ShapeDtypeStruct((1, s, FFN_DIM), BF16), jax.ShapeDtypeStruct((2, nchunk, 8, cw2), F32),
                   jax.ShapeDtypeStruct((s, d), F32), jax.ShapeDtypeStruct((s, d), F32)] + (list(ride.out_shapes) if ride else []),
        scratch_shapes=[pltpu.VMEM((2, 8, cw2), F32), pltpu.VMEM((t, d), F32)] + r_scr,
        compiler_params=_cparams(),
    )(up3, cw, cb, w_down, res, gate, *(ride.ins if ride else []))
    if ride:
        RIDERS.done[name] = list(out[4:])
    return out[0], [out[1]], out[2], out[3]


def ffn_mid_backward(name, up3, cw, cb, saved, dact3):
    rows, vecs = _ffn_io(up3, cw, cb)
    s = up3.shape[1]
    return scan_bwd(name, _ffn_fn, nb=2, nchunk=s // _FFN_T, t=_FFN_T, rows=rows, vecs=vecs, carries=_FFN_CARRIES,
                    saved=saved, douts=[Row(dact3, _FFN_CW, fc=lambda b: b)])


def _adam_fn(ci, b, carries, rows, vecs):
    w, g, m, v = rows
    m = ADAM_B1 * m + (1.0 - ADAM_B1) * g
    v = ADAM_B2 * v + (1.0 - ADAM_B2) * (g * g)
    m_hat = m / (1.0 - ADAM_B1 ** ADAM_STEP)
    v_hat = v / (1.0 - ADAM_B2 ** ADAM_STEP)
    delta = -ADAM_LR * (m_hat / (jnp.sqrt(v_hat) + ADAM_EPS) + ADAM_WD * w)
    return [], [delta, m, v]


def adamw(name, w, g, m, v):
    shape = w.shape
    c = shape[-1]
    r = int(np.prod(shape[:-1]))
    t = _tile(r, 256, 8)
    as3 = lambda a: a.reshape(1, r, c)
    outs, _ = scan_fwd(name, _adam_fn, nb=1, nchunk=r // t, t=t, rows=[Row(as3(a)) for a in (w, g, m, v)], vecs=[], carries=[],
                       outs=[out_row((1, r, c)) for _ in range(3)], save=False)
    return [o.reshape(shape) for o in outs]


def rope_tables(positions):
    inv_freq = ROPE_THETA ** (-jnp.arange(0, ROT_DIM, 2, dtype=F32) / ROT_DIM)
    ang = positions.astype(F32)[:, None] * inv_freq
    s = positions.shape[0]
    cs = jnp.concatenate([jnp.cos(ang), jnp.cos(ang), jnp.ones((s, ATT_HEAD_DIM - ROT_DIM), F32)], axis=1)
    sn = jnp.concatenate([jnp.sin(ang), jnp.sin(ang), jnp.zeros((s, ATT_HEAD_DIM - ROT_DIM), F32)], axis=1)
    cs3 = jnp.concatenate([jnp.tile(cs, (1, 2 * ATT_HEADS)), jnp.ones((s, ATT_W), F32)], axis=1)
    sn3 = jnp.concatenate([jnp.tile(sn, (1, 2 * ATT_HEADS)), jnp.zeros((s, ATT_W), F32)], axis=1)
    return cs3[None], sn3[None]


def attention_forward(lname, qkv3, cs3, sn3):
    s = qkv3.shape[1]
    rotated = rope_forward(f"{lname}_rope", qkv3, cs3, sn3)
    os_, ls_, keep = [], [], []
    for pi, (_, d) in enumerate(ATT_PATTERNS):
        o, lse, saved = attn_forward(f"{lname}_attn{pi}", rotated[pi], d)
        os_.append(o)
        ls_.append(lse)
        keep.append(saved)
    y = merge_forward(f"{lname}_merge", os_, ls_, s)
    return y, (rotated, os_, ls_, keep)


def attention_backward(lname, qkv3, cs3, sn3, res, dmix3):
    rotated, os_, ls_, keep = res
    dm = merge_backward(f"{lname}_merge_b", os_, ls_, dmix3)
    dys = [attn_backward(f"{lname}_attn{pi}_b", rotated[pi], d, keep[pi], dm[pi], dm[3 + pi]) for pi, (_, d) in enumerate(ATT_PATTERNS)]
    return rope_backward(f"{lname}_rope_b", qkv3, cs3, sn3, dys)


def final_loss(name, x3, t3, g):
    s, d = x3.shape[1], x3.shape[2]
    t = _ROW_T

    def body(x_ref, t_ref, g_ref, loss_ref, dx_ref, dg_ref):
        i = pl.program_id(0)
        tv = t_ref[0]

        def f(x, gg):
            y = x * lax.rsqrt(jnp.mean(x * x, axis=-1, keepdims=True) + NORM_EPS) * gg
            e = y - tv
            return 0.5 * jnp.sum(jnp.mean(e * e, axis=-1, keepdims=True), axis=0, keepdims=True)

        l, vjp = jax.vjp(f, x_ref[0], g_ref[...])
        dx, dg = vjp(jnp.ones((1, 1), F32))
        dx_ref[0] = dx

        @pl.when(i == 0)
        def _():
            loss_ref[...] = jnp.zeros(loss_ref.shape, F32)
            dg_ref[...] = jnp.zeros(dg_ref.shape, F32)

        loss_ref[...] += jnp.broadcast_to(l, loss_ref.shape)
        dg_ref[...] += dg

    row = pl.BlockSpec((1, t, d), lambda i: (0, i, 0))
    vec = pl.BlockSpec((1, d), lambda i: (0, 0))
    return pl.pallas_call(
        body, name=name, grid=(s // t,), in_specs=[row, row, vec],
        out_specs=[pl.BlockSpec((8, 128), lambda i: (0, 0)), row, vec],
        out_shape=[jax.ShapeDtypeStruct((8, 128), F32), jax.ShapeDtypeStruct(x3.shape, F32), jax.ShapeDtypeStruct((1, d), F32)],
        compiler_params=pltpu.CompilerParams(dimension_semantics=("arbitrary",), vmem_limit_bytes=VMEM_LIMIT_BYTES),
    )(x3, t3, g)


_ADA_TN = 512


def ada_forward(name, c16, ada_w):
    depth, d, cols = ada_w.shape

    def body(c_ref, w_ref, o_ref):
        o_ref[0] = _mxu(_silu(c_ref[...]), w_ref[0], "nn")

    return pl.pallas_call(
        body, name=name, grid=(depth, cols // _ADA_TN),
        in_specs=[pl.BlockSpec((16, d), lambda l, j: (0, 0)), pl.BlockSpec((1, d, _ADA_TN), lambda l, j: (l, 0, j))],
        out_specs=pl.BlockSpec((1, 16, _ADA_TN), lambda l, j: (l, 0, j)),
        out_shape=jax.ShapeDtypeStruct((depth, 16, cols), F32),
        compiler_params=pltpu.CompilerParams(dimension_semantics=("arbitrary", "arbitrary"), vmem_limit_bytes=VMEM_LIMIT_BYTES),
    )(c16, ada_w)


def ada_backward(name, c16, dmod16, w, m, v):
    depth, d, cols = w.shape

    def body(c_ref, dm_ref, w_ref, m_ref, v_ref, g_ref, dl_ref, nm_ref, nv_ref):
        g = _mxu(_silu(c_ref[...]), dm_ref[0], "tn")
        _, (delta, nm, nv) = _adam_fn(None, None, [], [w_ref[0], g, m_ref[0], v_ref[0]], [])
        g_ref[0], dl_ref[0], nm_ref[0], nv_ref[0] = g, delta, nm, nv

    blk = pl.BlockSpec((1, d, _ADA_TN), lambda l, j: (l, 0, j))
    return pl.pallas_call(
        body, name=name, grid=(depth, cols // _ADA_TN),
        in_specs=[pl.BlockSpec((16, d), lambda l, j: (0, 0)), pl.BlockSpec((1, 16, _ADA_TN), lambda l, j: (l, 0, j)), blk, blk, blk],
        out_specs=[blk] * 4, out_shape=[jax.ShapeDtypeStruct(w.shape, F32)] * 4,
        compiler_params=pltpu.CompilerParams(dimension_semantics=("arbitrary", "arbitrary"), vmem_limit_bytes=VMEM_LIMIT_BYTES),
    )(c16, dmod16, w, m, v)


def _sum_fn(ci, b, carries, rows, vecs):
    acc = rows[0]
    for r in rows[1:]:
        acc = acc + r
    return [], [acc]


def sum_slots(name, a, nsum, out_dtype=F32):
    n, r, c = a.shape
    nb = n // nsum
    t = _tile(r, 256, 8)
    rows = [Row(a, fb=(lambda b, k=k: k * nb + b)) for k in range(nsum)]
    (out,), _ = scan_fwd(name, _sum_fn, nb=nb, nchunk=r // t, t=t, rows=rows, vecs=[], carries=[],
                         outs=[out_row((nb, r, c), out_dtype, fb=lambda b: b)], save=False)
    return out


def _flip(mask, pos):
    return tuple((1 - p) if m else p for m, p in zip(mask, pos))


ALL_PEERS = [(a, b, c) for a in (0, 1) for b in (0, 1) for c in (0, 1)][1:]
CHIP_PEERS = [(1, 0, 0), (0, 1, 0), (1, 1, 0)]
SIBLING = [(0, 0, 1)]


def _divisor(size, target, unit):
    best = 1
    for n in range(1, target + 1):
        if size % n == 0 and (size // n) % unit == 0:
            best = n
    return best


def _pieces(src, dst, pieces):
    shape = src.shape
    unit = 16 if src.dtype == BF16 else 8
    if pieces <= 1:
        return [(src, dst)]
    if len(shape) == 2:
        n = _divisor(shape[0], pieces, unit)
        s = shape[0] // n
        return [(src.at[pl.ds(i * s, s)], dst.at[pl.ds(i * s, s)]) for i in range(n)]
    assert len(shape) == 3, shape
    n = _divisor(shape[1], max(pieces // shape[0], 1), unit)
    s = shape[1] // n
    return [(src.at[j, pl.ds(i * s, s)], dst.at[j, pl.ds(i * s, s)]) for j in range(shape[0]) for i in range(n)]


def comm_call(name, arrays, out_shapes, masks, src_fn, dst_fn, local_fn=None, pieces=1):
    na, npeer = len(arrays), len(masks)

    def body(*refs):
        ins, outs = refs[:na], refs[na:2 * na]
        send_sems, recv_sems, loc_sems = refs[2 * na:]
        me = (lax.axis_index("x"), lax.axis_index("y"), lax.axis_index("c"))
        local = []
        if local_fn is not None:
            for k in range(na):
                s, d = local_fn(k, ins[k], outs[k], me)
                for ps, pd in _pieces(s, d, pieces):
                    pltpu.make_async_copy(ps, pd, loc_sems.at[k]).start()
                local.append(pltpu.make_async_copy(s, d, loc_sems.at[k]))

        def remote(k, p, src, dst, to):
            return pltpu.make_async_remote_copy(
                src_ref=src, dst_ref=dst, send_sem=send_sems.at[k * npeer + p], recv_sem=recv_sems.at[k * npeer + p],
                device_id=to, device_id_type=MESH)

        for k in range(na):
            for p in range(npeer):
                peer = _flip(masks[p], me)
                for ps, pd in _pieces(src_fn(k, ins[k], me, peer), dst_fn(k, outs[k], me), pieces):
                    remote(k, p, ps, pd, peer).start()
        for k in range(na):
            for p in range(npeer):
                peer = _flip(masks[p], me)
                remote(k, p, src_fn(k, ins[k], me, peer), dst_fn(k, outs[k], peer), peer).wait_recv()
        for k in range(na):
            for p in range(npeer):
                peer = _flip(masks[p], me)
                remote(k, p, src_fn(k, ins[k], me, peer), dst_fn(k, outs[k], me), peer).wait_send()
        for cp in local:
            cp.wait()

    hbm = pl.BlockSpec(memory_space=pl.ANY)
    out = pl.pallas_call(
        body, name=name, in_specs=[hbm] * na, out_specs=[hbm] * na,
        out_shape=[jax.ShapeDtypeStruct(s, a.dtype) for s, a in zip(out_shapes, arrays)],
        scratch_shapes=[pltpu.SemaphoreType.DMA((na * npeer,)), pltpu.SemaphoreType.DMA((na * npeer,)),
                        pltpu.SemaphoreType.DMA((na,))],
    )(*arrays)
    return list(out)


def _dev(pos):
    return 4 * pos[0] + 2 * pos[1] + pos[2]


def _chip(pos):
    return 2 * pos[0] + pos[1]


def allgather8(name, a):
    (out,) = comm_call(name, [a], [(8,) + a.shape], ALL_PEERS,
                       src_fn=lambda k, r, me, peer: r, dst_fn=lambda k, o, sender: o.at[_dev(sender)])
    me = _dev((lax.axis_index("x"), lax.axis_index("y"), lax.axis_index("c")))
    return jnp.where((jnp.arange(8) == me)[:, None, None], a[None], out)


def swap_layers(name, arrays, c):
    got = comm_call(name, arrays, [a.shape for a in arrays], SIBLING,
                    src_fn=lambda k, r, me, peer: r, dst_fn=lambda k, o, sender: o, pieces=32)
    return [[jnp.where(c == 0, a, g), jnp.where(c == 0, g, a)] for a, g in zip(arrays, got)]


def _rows_of(shape):
    return -(-int(np.prod(shape)) // 1024) * 8


def _pack(arrs):
    parts = []
    for a in arrs:
        flat = a.reshape(-1).astype(F32)
        parts.append(jnp.pad(flat, (0, _rows_of(a.shape) * 128 - flat.shape[0])).reshape(-1, 128))
    rows = sum(p.shape[0] for p in parts)
    parts.append(jnp.zeros(((-rows) % _ROW_T, 128), F32))
    return jnp.concatenate(parts, axis=0)


def _unpack(buf, shapes):
    out, o = [], 0
    for s in shapes:
        r, n = _rows_of(s), int(np.prod(s))
        out.append(buf[o:o + r].reshape(-1)[:n].reshape(s))
        o += r
    return out


_WEIGHTS = ["ada_w", "ada_b", "norm1_g", "w_in", "ssd_conv_w", "ssd_conv_b", "ssd_dt_bias", "ssd_a_log", "ssd_d", "ssd_norm_g",
            "pool_w", "pool_scale", "w_out", "norm2_g", "ffn_up", "ffn_conv_w", "ffn_conv_b", "ffn_down", "final_g"]
_BIG = ["w_in", "w_out", "ffn_up", "ffn_down"]
_SMALL = [n for n in _WEIGHTS if n not in _BIG and n != "ada_w"]
_COL_SHARDED_SMALL = {"ssd_conv_w": 256, "ffn_conv_w": 1408}


def _pad_lanes(v, n=128):
    return jnp.pad(v.astype(F32), (0, n - v.shape[0]))[None]


_CHIP2_PARTS = [(1284, 1536), (1792, 1800), (1536, 1792), (IN_MAIN, IN_MAIN + 126)]


def _w_in_chip_cols(gp):
    q = IN_W // 4
    return [gp[:, :q], gp[:, q:2 * q], jnp.concatenate([gp[:, a:b] for a, b in _CHIP2_PARTS], axis=1), gp[:, IN_WP - q:]]


def _ffn_block_perm(a):
    n = a.shape[-1] // 4
    return jnp.concatenate([a[..., j * n:(j + 1) * n] for j in FFN_BLOCK_ORDER], axis=-1)


def _layer_forward(i, x3, modv, wts, sp, cs3, sn3):
    sh1, sc1, g1, sh2, sc2, g2 = modv
    big = lambda n: wts[n]() if callable(wts[n]) else wts[n]
    h1 = norm_mod_forward(f"l{i}_norm1", x3, wts["norm1_g"], sc1, sh1)
    proj3 = mm(f"l{i}_proj", h1[0], big("w_in")[:, :IN_MAIN], "nn")[None]
    qkv3 = mm(f"l{i}_qkv", h1[0], big("w_in")[:, IN_MAIN:], "nn")[None]
    y_ssd, sv_ssd = ssd_forward(f"l{i}_ssd", proj3, sp)
    y_pool, sv_pool = pool_forward(f"l{i}_pool", proj3, wts["wbd"], wts["pool_scale"])
    y_att, res_att = attention_forward(f"l{i}", qkv3, cs3, sn3)
    mix = jnp.concatenate([y_ssd, y_pool, y_att], axis=-1)
    out, x1 = mm(f"l{i}_wout", mix[0], big("w_out"), "nn", res=x3[0], gate=g1)
    x1 = x1[None]
    h2 = norm_mod_forward(f"l{i}_norm2", x1, wts["norm2_g"], sc2, sh2)
    up3 = mm(f"l{i}_up", h2[0], big("ffn_up"), "nn")[None]
    act, sv_ffn, dn, x2 = ffn_down_forward(f"l{i}_down", up3, wts["ffn_conv_w"], wts["ffn_conv_b"], big("ffn_down"), x1[0], g2)
    keep = dict(x=x3, h1=h1, proj3=proj3, qkv3=qkv3, sv_ssd=sv_ssd, sv_pool=sv_pool, res_att=res_att, mix=mix, out=out[None],
                x1=x1, h2=h2, up3=up3, act=act, sv_ffn=sv_ffn, dn=dn[None])
    return x2[None], keep


def _layer_backward(i, dx2, keep, modv, wts, sp, cs3, sn3, after=None):
    sh1, sc1, g1, sh2, sc2, g2 = modv
    k = keep
    big = lambda n: wts[n]() if callable(wts[n]) else wts[n]
    tell = lambda step, *a: after[step](*a) if after and step in after else None
    d_dn, d_g2 = gate_backward(f"l{i}_gate2_b", k["dn"], g2, dx2)
    d_act = mm(f"l{i}_down_bx", d_dn[0], big("ffn_down"), "nt")
    g_down = mm(f"l{i}_down_bw", k["act"][0], d_dn[0], "tn").reshape(4, FFN_DIM // 4, D_MODEL)
    (d_up,), dv_ffn = ffn_mid_backward(f"l{i}_ffn_b", k["up3"], wts["ffn_conv_w"], wts["ffn_conv_b"], k["sv_ffn"], d_act[None])
    tell("ffn_b")
    d_h2 = mm(f"l{i}_up_bx", d_up[0], big("ffn_up"), "nt")
    g_up = mm(f"l{i}_up_bw", k["h2"][0], d_up[0], "tn", tn=_FFN_CW,
              into=((4, D_MODEL, _FFN_CW), lambda r, c: ((c % 2) * 2 + c // 2, r, 0)))
    dx1, (d_n2, d_sc2, d_sh2) = norm_mod_backward(f"l{i}_norm2_b", k["x1"], wts["norm2_g"], sc2, sh2, d_h2[None], dx2)
    d_out, d_g1 = gate_backward(f"l{i}_gate1_b", k["out"], g1, dx1)
    d_mix = mm(f"l{i}_wout_bx", d_out[0], big("w_out"), "nt")[None]
    g_wout = mm(f"l{i}_wout_bw", k["mix"][0], d_out[0], "tn").reshape(4, D_MODEL // 4, D_MODEL)
    tell("wout_bw", g_wout, g_up, g_down)
    (dz, dxs, dbm, dcm, ddt), dv_ssd = ssd_backward(f"l{i}_ssd_b", k["proj3"], sp, k["sv_ssd"], d_mix)
    tell("ssd_b")
    (du_pool,), (d_wbd, d_pscale) = pool_backward(f"l{i}_pool_b", k["proj3"], wts["wbd"], wts["pool_scale"], k["sv_pool"], d_mix)
    d_qkv = attention_backward(f"l{i}", k["qkv3"], cs3, sn3, k["res_att"], d_mix)
    d_proj = jnp.concatenate([dz[0], dxs[0], dbm[0], dcm[0], du_pool[0], (ddt[0] + ddt[1]).astype(BF16), d_qkv[0]], axis=-1)
    g_win = jnp.stack(_w_in_chip_cols(mm(f"l{i}_proj_bw", k["h1"][0], d_proj, "tn")))
    tell("proj_bw", g_win)
    d_h1 = mm(f"l{i}_proj_bx", d_proj, big("w_in"), "nt")
    tell("proj_bx")
    dx, (d_n1, d_sc1, d_sh1) = norm_mod_backward(f"l{i}_norm1_b", k["x"], wts["norm1_g"], sc1, sh1, d_h1[None], dx1)
    dcwx, dcbx, dcwb, dcbb, dcwc, dcbc, ddtb, dalog, ddsk, dng = dv_ssd
    small = dict(
        norm1_g=d_n1[0], norm2_g=d_n2[0],
        ssd_conv_w=jnp.concatenate([dcwx[:, :512], dcwb[:, 512:768], dcwc[:, 768:]], axis=1),
        ssd_conv_b=jnp.concatenate([dcbx[0, :512], dcbb[0, 512:768], dcbc[0, 768:]]),
        ssd_dt_bias=ddtb[0, :8], ssd_a_log=dalog[0, :8], ssd_d=ddsk[0, :8], ssd_norm_g=dng[0],
        pool_w=jnp.stack([d_wbd[64 * g:64 * g + 64, 64 * g:64 * g + 64] for g in range(4)]), pool_scale=d_pscale[0],
        ffn_conv_w=_ffn_block_perm(dv_ffn[0]), ffn_conv_b=_ffn_block_perm(dv_ffn[1][0]),
    )
    dmod = jnp.concatenate([d_sh1[0], d_sc1[0], d_g1[0], d_sh2[0], d_sc2[0], d_g2[0]])
    return dx, [g_win, g_wout, g_up, g_down], small, dmod


def kernel(x, c, positions, ada_w, ada_b, norm1_g, w_in, ssd_conv_w, ssd_conv_b, ssd_dt_bias, ssd_a_log, ssd_d, ssd_norm_g, pool_w, pool_scale, w_out, norm2_g, ffn_up, ffn_conv_w, ffn_conv_b, ffn_down, final_g, loss_target, m_ada_w, m_ada_b, m_norm1_g, m_w_in, m_ssd_conv_w, m_ssd_conv_b, m_ssd_dt_bias, m_ssd_a_log, m_ssd_d, m_ssd_norm_g, m_pool_w, m_pool_scale, m_w_out, m_norm2_g, m_ffn_up, m_ffn_conv_w, m_ffn_conv_b, m_ffn_down, m_final_g, v_ada_w, v_ada_b, v_norm1_g, v_w_in, v_ssd_conv_w, v_ssd_conv_b, v_ssd_dt_bias, v_ssd_a_log, v_ssd_d, v_ssd_norm_g, v_pool_w, v_pool_scale, v_w_out, v_norm2_g, v_ffn_up, v_ffn_conv_w, v_ffn_conv_b, v_ffn_down, v_final_g):
    args = dict(locals())
    w = {n: args[n] for n in _WEIGHTS}
    m = {n: args["m_" + n] for n in _WEIGHTS}
    v = {n: args["v_" + n] for n in _WEIGHTS}
    d = D_MODEL
    me = (lax.axis_index("x"), lax.axis_index("y"), lax.axis_index("c"))
    chip, dev = _chip(me), _dev(me)
    RIDERS.reset()

    shapes0 = [c.shape, ssd_conv_w.shape, ffn_conv_w.shape]
    g0 = allgather8("gather_c_conv", _pack([c, ssd_conv_w, ffn_conv_w]))
    c16 = jnp.pad(g0[:, :d // 128, :].reshape(8, d), ((0, 8), (0, 0)))
    by_chip = [_unpack(g0[2 * j], shapes0) for j in range(4)]
    conv_w_full = jnp.concatenate([p[1] for p in by_chip], axis=-1)
    fconv_w_full = jnp.concatenate([p[2] for p in by_chip], axis=-1)

    modp = ada_forward("ada_fwd", c16, ada_w)[:, :8]
    g1 = allgather8("gather_mod", _pack([modp]))
    modfull = jnp.concatenate([_unpack(g1[2 * j], [modp.shape])[0] for j in range(4)], axis=-1)
    mod = lax.dynamic_index_in_dim(modfull, dev, axis=1, keepdims=False) + ada_b
    modv = [[mod[i, q * d:(q + 1) * d][None] for q in range(6)] for i in range(DEPTH)]

    shards = [w[n].astype(BF16) for n in _BIG]

    def weight(k, layer, got):
        parts = [jnp.where(chip == j, shards[k][layer], got[j]) for j in range(4)]
        if k == 0:
            return _w_in_from_chips(parts)
        return jnp.concatenate([parts[j] for j in FFN_BLOCK_ORDER], axis=1) if k == 2 else jnp.concatenate(parts, axis=0)

    def later(k, layer, *sources):
        made = []

        def get():
            if not made:
                got = [RIDERS.result(host)[pos] for host, pos in sources]
                made.append(weight(k, layer, got[0] if len(got) == 1 else jnp.concatenate(got, axis=1)))
            return made[0]
        return get

    cs3, sn3 = rope_tables(positions[0])
    eye4 = jnp.eye(4, dtype=F32)
    wts, sps = [], []
    for i in range(DEPTH):
        wts.append(dict(
            norm1_g=norm1_g[i][None], norm2_g=norm2_g[i][None], pool_scale=pool_scale[i][None],
            wbd=(eye4[:, None, :, None] * pool_w[i][:, :, None, :]).reshape(POOL_W, POOL_W),
            ffn_conv_w=_ffn_block_perm(fconv_w_full[i]), ffn_conv_b=_ffn_block_perm(ffn_conv_b[i])[None]))
        sps.append(dict(cw=conv_w_full[i], cb=ssd_conv_b[i][None], dtb=_pad_lanes(ssd_dt_bias[i]), alog=_pad_lanes(ssd_a_log[i]),
                        dsk=_pad_lanes(ssd_d[i]), ng=ssd_norm_g[i][None]))

    (w_in0,) = ride_alone("gather_w_in0", gather_ride(0, [shards[0]]))
    RIDERS.book("l0_ssd", gather_ride(0, [shards[1], shards[3]]))
    half = shards[2].shape[1] // 2
    RIDERS.book("l0_attn0", gather_ride(0, [shards[2][:, :half]]))
    RIDERS.book("l0_attn1", gather_ride(0, [shards[2][:, half:]]))
    wts[0].update(w_in=weight(0, 0, w_in0), w_out=later(1, 0, ("l0_ssd", 0)), ffn_down=later(3, 0, ("l0_ssd", 1)),
                  ffn_up=later(2, 0, ("l0_attn0", 0), ("l0_attn1", 0)))
    RIDERS.book("l0_attn2", gather_ride(1, [shards[0], shards[1]]))
    RIDERS.book("l0_up", gather_ride(1, [shards[2]]))
    RIDERS.book("l0_down", gather_ride(1, [shards[3]]))
    wts[1].update(w_in=later(0, 1, ("l0_attn2", 0)), w_out=later(1, 1, ("l0_attn2", 1)), ffn_up=later(2, 1, ("l0_up", 0)),
                  ffn_down=later(3, 1, ("l0_down", 0)))
    x1_, keep0 = _layer_forward(0, x, modv[0], wts[0], sps[0], cs3, sn3)
    xc, keep1 = _layer_forward(1, x1_, modv[1], wts[1], sps[1], cs3, sn3)
    keeps = [keep0, keep1]
    lossblk, dx, d_final = final_loss("final_loss", xc, loss_target, final_g[None])
    loss = lax.psum(lossblk[0, 0], ("x", "y", "c"))

    small_g, dmods = [None] * DEPTH, [None] * DEPTH
    part_sum, from_chips = [[None] * 4 for _ in range(DEPTH)], [[None] * 4 for _ in range(DEPTH)]

    def owner_sum(layer, ks, mine, theirs):
        for k, g, t in zip(ks, mine, theirs):
            part_sum[layer][k] = add_arrays(f"sum_cores{layer}_{_BIG[k]}", [g, t], BF16)

    dx, by_chip1, small_g[1], dmods[1] = _layer_backward(1, dx, keeps[1], modv[1], wts[1], sps[1], cs3, sn3)
    RIDERS.book("l0_ffn_b", to_owner_ride(1, by_chip1))

    def after_ffn_b():
        owner_sum(1, range(4), by_chip1, RIDERS.result("l0_ffn_b"))
        RIDERS.book("l0_up_bx", scatter_ride(1, [part_sum[1][2]]))
        RIDERS.book("l0_up_bw", scatter_ride(1, [part_sum[1][0], part_sum[1][1]]))
        RIDERS.book("l0_norm2_b", scatter_ride(1, [part_sum[1][3]]))

    early = []

    def after_wout_bw(g_wout, g_up, g_down):
        early.extend([g_wout, g_up, g_down])
        RIDERS.book("l0_ssd_b", to_owner_ride(0, early))

    def after_ssd_b():
        owner_sum(0, [1, 2, 3], early, RIDERS.result("l0_ssd_b"))
        for host, k in (("l0_attn0_b", 2), ("l0_attn1_b", 3), ("l0_attn2_b", 1)):
            RIDERS.book(host, scatter_ride(0, [part_sum[0][k]]))

    last = []

    def after_proj_bw(g_win):
        last.append(g_win)
        RIDERS.book("l0_proj_bx", to_owner_ride(0, last))

    def after_proj_bx():
        owner_sum(0, [0], last, RIDERS.result("l0_proj_bx"))
        RIDERS.book("l0_norm1_b", scatter_ride(0, [part_sum[0][0]]))

    hooks = dict(ffn_b=after_ffn_b, wout_bw=after_wout_bw, ssd_b=after_ssd_b, proj_bw=after_proj_bw, proj_bx=after_proj_bx)
    dx, _, small_g[0], dmods[0] = _layer_backward(0, dx, keeps[0], modv[0], wts[0], sps[0], cs3, sn3, after=hooks)
    from_chips[1][2], (from_chips[1][0], from_chips[1][1]) = RIDERS.result("l0_up_bx")[0], RIDERS.result("l0_up_bw")
    from_chips[1][3] = RIDERS.result("l0_norm2_b")[0]
    for host, k in (("l0_attn0_b", 2), ("l0_attn1_b", 3), ("l0_attn2_b", 1), ("l0_norm1_b", 0)):
        from_chips[0][k] = RIDERS.result(host)[0]
    mine = [sum_chips_mine(f"sum_chips_{n}", part_sum[0][k], from_chips[0][k], part_sum[1][k], from_chips[1][k])
            for k, n in enumerate(_BIG)]
    reduced = swap_layers("swap_r", mine, me[2])
    grads = {n: jnp.stack(r) for n, r in zip(_BIG, reduced)}

    part = dict(ada_b=jnp.stack(dmods), final_g=d_final[0])
    for n in _SMALL:
        if n not in part:
            part[n] = jnp.stack([small_g[i][n] for i in range(DEPTH)])
    full_shapes = [part[n].shape for n in _SMALL]
    gs = allgather8("gather_small", _pack([part[n] for n in _SMALL]))
    tot = _unpack(sum_slots("sum_small", gs, 8)[0], full_shapes)
    small_tot = dict(zip(_SMALL, tot))
    dmod_all = gs[:, :DEPTH * 6 * d // 128, :].reshape(8, DEPTH, 6 * d)
    for n, ncol in _COL_SHARDED_SMALL.items():
        small_tot[n] = lax.dynamic_slice_in_dim(small_tot[n], chip * ncol, ncol, axis=2)
    grads.update(small_tot)

    ncol = ada_w.shape[2]
    dm = lax.dynamic_slice_in_dim(dmod_all, chip * ncol, ncol, axis=2).transpose(1, 0, 2)
    upd = {}
    g_ada, *upd["ada_w"] = ada_backward("ada_bwd", c16, jnp.pad(dm, ((0, 0), (0, 8), (0, 0))), ada_w, m["ada_w"], v["ada_w"])
    grads["ada_w"] = g_ada

    for n in _BIG:
        upd[n] = adamw(f"adam_{n}", w[n], grads[n], m[n], v[n])
    shapes_s = [w[n].shape for n in _SMALL]
    packed = [_pack([src[n] for n in _SMALL]) for src in (w, grads, m, v)]
    outs_s = [_unpack(o, shapes_s) for o in adamw("adam_small", *packed)]
    for q, n in enumerate(_SMALL):
        upd[n] = [outs_s[0][q], outs_s[1][q], outs_s[2][q]]

    return (loss, dx, *[grads[n] for n in _WEIGHTS], *[upd[n][0] for n in _WEIGHTS], *[upd[n][1] for n in _WEIGHTS],
            *[upd[n][2] for n in _WEIGHTS])


def ride_alone(name, ride):
    ni, no = len(ride.ins), len(ride.out_shapes)

    def body(*refs):
        ride.begin(refs[:ni], refs[ni:ni + no], refs[ni + no:])
        ride.end(refs[:ni], refs[ni:ni + no], refs[ni + no:])

    in_specs, out_specs, scratch = ride.specs()
    return list(pl.pallas_call(body, name=name, in_specs=in_specs, out_specs=out_specs, out_shape=ride.out_shapes,
                               scratch_shapes=scratch)(*ride.ins))


def mm(name, a, b, mode, out_dtype=F32, res=None, gate=None, tm=1408, tn=1536, tk=1408, into=None):
    ride = RIDERS.take(name)
    if mode == "nn":
        (m, k), n = a.shape, b.shape[1]
    elif mode == "nt":
        (m, k), n = a.shape, b.shape[0]
    else:
        (k, m), n = a.shape, b.shape[1]
    tm, tn, tk = _tile(m, tm), _tile(n, tn), _tile(k, tk)
    ni, nj, nk = m // tm, n // tn, k // tk
    a_spec = pl.BlockSpec((tk, tm), lambda i, j, q: (q, i)) if mode == "tn" else pl.BlockSpec((tm, tk), lambda i, j, q: (i, q))
    b_spec = pl.BlockSpec((tn, tk), lambda i, j, q: (j, q)) if mode == "nt" else pl.BlockSpec((tk, tn), lambda i, j, q: (q, j))
    o_spec = pl.BlockSpec((tm, tn), lambda i, j, q: (i, j))
    fused = res is not None
    lead = 0 if into is None else len(into[0]) - 2
    first = (0,) * lead + (slice(None), slice(None))
    ins, in_specs = [a, b], [a_spec, b_spec]
    out_shape, out_specs = [jax.ShapeDtypeStruct((m, n), out_dtype)], [o_spec]
    if fused:
        ins += [res, gate]
        in_specs += [o_spec, pl.BlockSpec((1, tn), lambda i, j, q: (0, j))]
        out_shape.append(jax.ShapeDtypeStruct((m, n), F32))
        out_specs.append(o_spec)
    if into is not None:
        shape, omap = into
        out_shape = [jax.ShapeDtypeStruct(shape, out_dtype)]
        out_specs = [pl.BlockSpec((1,) * lead + (tm, tn), lambda i, j, q: omap(i, j))]
    n_in, n_out = len(ins), len(out_shape)
    scratch = [pltpu.VMEM((tm, tn), F32)]
    if ride is not None:
        r_in, r_out, r_scr = ride.specs()
        ins, in_specs = ins + list(ride.ins), in_specs + r_in
        out_shape, out_specs = out_shape + list(ride.out_shapes), out_specs + r_out
        scratch = scratch + r_scr

    def body(*refs):
        a_ref, b_ref = refs[:2]
        o_ref = refs[len(ins)]
        acc = refs[len(ins) + len(out_shape)]
        i, j, q = pl.program_id(0), pl.program_id(1), pl.program_id(2)
        at = lambda x, y, z: jnp.logical_and(jnp.logical_and(i == x, j == y), q == z)
        r_refs = (refs[n_in:len(ins)], refs[len(ins) + n_out:len(ins) + len(out_shape)], refs[len(ins) + len(out_shape) + 1:])
        if ride is not None:
            ride.begin(*r_refs, at(0, 0, 0))

        @pl.when(q == 0)
        def _():
            acc[...] = jnp.zeros(acc.shape, F32)

        acc[...] += _mxu(a_ref[...], b_ref[...], mode)

        @pl.when(q == nk - 1)
        def _():
            o_ref[first] = acc[...].astype(o_ref.dtype)
            if fused:
                refs[len(ins) + 1][...] = refs[2][...] + refs[3][...] * acc[...]

        if ride is not None:
            ride.end(*r_refs, at(ni - 1, nj - 1, nk - 1))

    sem = ("arbitrary",) * 3 if ride is not None else ("parallel", "parallel", "arbitrary")
    out = pl.pallas_call(
        body, name=name, grid=(ni, nj, nk), in_specs=in_specs, out_specs=out_specs, out_shape=out_shape, scratch_shapes=scratch,
        compiler_params=pltpu.CompilerParams(dimension_semantics=sem, vmem_limit_bytes=VMEM_LIMIT_BYTES),
    )(*ins)
    if ride is not None:
        RIDERS.done[name] = list(out[n_out:])
    return tuple(out[:n_out]) if fused else out[0]


def add_arrays(name, arrs, out_dtype=F32):
    nb, r, c = arrs[0].shape
    t = _tile(r, 256, 8)
    (out,), _ = scan_fwd(name, _sum_fn, nb=nb, nchunk=r // t, t=t, rows=[Row(a, fb=lambda b: b) for a in arrs], vecs=[], carries=[],
                         outs=[out_row((nb, r, c), out_dtype, fb=lambda b: b)], save=False)
    return out


def _sum_chips_mine_fn(ci, b, carries, rows, vecs):
    mine_layer = lax.axis_index("c")
    chip = 2 * lax.axis_index("x") + lax.axis_index("y")
    tot = None
    for j in range(4):
        own = jnp.where(mine_layer == 0, rows[j], rows[8 + j])
        sent = jnp.where(mine_layer == 0, rows[4 + j], rows[12 + j])
        term = jnp.where(chip == j, own, sent)
        tot = term if tot is None else tot + term
    return [], [tot]


def sum_chips_mine(name, p0, q0, p1, q1):
    _, r, c = p0.shape
    t = _tile(r, 256, 8)
    rows = [Row(a, fb=(lambda b, j=j: j)) for a in (p0, q0, p1, q1) for j in range(4)]
    (out,), _ = scan_fwd(name, _sum_chips_mine_fn, nb=1, nchunk=r // t, t=t, rows=rows, vecs=[], carries=[],
                         outs=[out_row((1, r, c))], save=False)
    return out[0]


def _remote(src, dst, send_sems, recv_sems, k, to):
    return pltpu.make_async_remote_copy(src_ref=src, dst_ref=dst, send_sem=send_sems.at[k], recv_sem=recv_sems.at[k],
                                        device_id=to, device_id_type=MESH)


def gather_ride(layer, shards):
    na = len(shards)

    def start(ins, outs, ss, rs, me):
        @pl.when(me[2] == layer)
        def _():
            for k in range(na):
                for p, mask in enumerate(CHIP_PEERS):
                    _remote(ins[k].at[layer], outs[k].at[_chip(me)], ss, rs, 6 * k + p, _flip(mask, me)).start()

    def finish(ins, outs, ss, rs, me):
        sibling = _flip(SIBLING[0], me)

        @pl.when(me[2] == layer)
        def _():
            for k in range(na):
                for p, mask in enumerate(CHIP_PEERS):
                    slot = outs[k].at[_chip(_flip(mask, me))]
                    _remote(ins[k].at[layer], slot, ss, rs, 6 * k + p, _flip(mask, me)).wait_recv()
                    _remote(slot, slot, ss, rs, 6 * k + 3 + p, sibling).start()
            for k in range(na):
                for p, mask in enumerate(CHIP_PEERS):
                    slot = outs[k].at[_chip(_flip(mask, me))]
                    _remote(ins[k].at[layer], slot, ss, rs, 6 * k + p, _flip(mask, me)).wait_send()
                    _remote(slot, slot, ss, rs, 6 * k + 3 + p, sibling).wait_send()

        @pl.when(me[2] != layer)
        def _():
            for k in range(na):
                for p, mask in enumerate(CHIP_PEERS):
                    slot = outs[k].at[_chip(_flip(mask, me))]
                    _remote(slot, slot, ss, rs, 6 * k + 3 + p, sibling).wait_recv()

    return Ride(list(shards), [jax.ShapeDtypeStruct((4,) + a.shape[1:], a.dtype) for a in shards], 6 * na, start, finish)


def scatter_ride(layer, parts):
    na = len(parts)

    def start(ins, outs, ss, rs, me):
        @pl.when(me[2] == layer)
        def _():
            for k in range(na):
                for p, mask in enumerate(CHIP_PEERS):
                    peer = _flip(mask, me)
                    _remote(ins[k].at[_chip(peer)], outs[k].at[_chip(me)], ss, rs, 3 * k + p, peer).start()

    def finish(ins, outs, ss, rs, me):
        @pl.when(me[2] == layer)
        def _():
            for k in range(na):
                for p, mask in enumerate(CHIP_PEERS):
                    peer = _flip(mask, me)
                    _remote(ins[k].at[_chip(peer)], outs[k].at[_chip(peer)], ss, rs, 3 * k + p, peer).wait_recv()
                    _remote(ins[k].at[_chip(peer)], outs[k].at[_chip(me)], ss, rs, 3 * k + p, peer).wait_send()

    return Ride(list(parts), [jax.ShapeDtypeStruct(a.shape, a.dtype) for a in parts], 3 * na, start, finish)


def to_owner_ride(layer, arrays):
    na = len(arrays)

    def start(ins, outs, ss, rs, me):
        @pl.when(me[2] != layer)
        def _():
            for k in range(na):
                _remote(ins[k], outs[k], ss, rs, k, _flip(SIBLING[0], me)).start()

    def finish(ins, outs, ss, rs, me):
        for k in range(na):
            cp = _remote(ins[k], outs[k], ss, rs, k, _flip(SIBLING[0], me))
            pl.when(me[2] != layer)(cp.wait_send)
            pl.when(me[2] == layer)(cp.wait_recv)

    return Ride(list(arrays), [jax.ShapeDtypeStruct(a.shape, a.dtype) for a in arrays], na, start, finish)


def _w_in_from_chips(a):
    c2 = a[2]
    pad = jnp.zeros((c2.shape[0], IN_WP - IN_W), c2.dtype)
    return jnp.concatenate([a[0], a[1], c2[:, :252], c2[:, 260:516], c2[:, 252:260], pad, c2[:, 516:], a[3]], axis=1)
```

```python
import functools

import numpy as np
import jax
import jax.numpy as jnp
from jax import lax
from jax.experimental import pallas as pl
from jax.experimental.pallas import tpu as pltpu

F32 = jnp.float32
BF16 = jnp.bfloat16
MESH = pl.DeviceIdType.MESH

D_MODEL = 1024
SEQ = 4096
DEPTH = 2
SSD_INNER = 512
SSD_HEADS = 8
SSD_STATE = 128
POOL_W = 256
POOL_WINDOWS = (2, 4, 8, 16)
ATT_W = 256
ATT_HEADS = 4
ATT_HEAD_DIM = 64
ATT_PATTERNS = ((128, 1), (512, 4), (2048, 16))
ATT_BLOCK = 128
ROT_DIM = 16
ROPE_THETA = 500000.0
IN_W = 2568
IN_WP = 2688
IN_MAIN = 1920
FFN_DIM = 2816
NORM_EPS = 1e-6
ADAM_LR, ADAM_B1, ADAM_B2, ADAM_EPS, ADAM_WD, ADAM_STEP = 0.001, 0.9, 0.999, 1e-08, 0.01, 10

VMEM_LIMIT_BYTES = 56 * 1024 * 1024
NEG = -1e30


def _mxu(a, b, mode):
    dims = {"nn": ((1,), (0,)), "nt": ((1,), (1,)), "tn": ((0,), (0,))}[mode]
    return lax.dot_general(a.astype(BF16), b.astype(BF16), (dims, ((), ())), preferred_element_type=F32)


@functools.partial(jax.custom_vjp, nondiff_argnums=(2,))
def _bdot(a, b, mode):
    return _mxu(a, b, mode)


def _bdot_fwd(a, b, mode):
    return _mxu(a, b, mode), (a, b)


def _bdot_bwd(mode, res, g):
    a, b = res
    if mode == "nn":
        return _mxu(g, b, "nt"), _mxu(a, g, "tn")
    if mode == "nt":
        return _mxu(g, b, "nn"), _mxu(g, a, "tn")
    return _mxu(b, g, "nt"), _mxu(a, g, "nn")


_bdot.defvjp(_bdot_fwd, _bdot_bwd)


def _iota(shape, dim):
    return lax.broadcasted_iota(jnp.int32, shape, dim)


def _make_shift(h):
    @functools.partial(jax.custom_vjp, nondiff_argnums=(2,))
    def shift(halo, cur, k):
        if k == 0:
            return cur
        full = jnp.concatenate([halo, cur], axis=0)
        return pltpu.roll(full, k, 0)[h:]

    def fwd(halo, cur, k):
        return shift(halo, cur, k), None

    def bwd(k, _, g):
        t, w = g.shape
        if k == 0:
            return jnp.zeros((h, w), F32), g
        d_cur = jnp.where(_iota((t, w), 0) < t - k, pltpu.roll(g, t - k, 0), 0.0)
        top = g[:h]
        d_halo = jnp.where(_iota((h, w), 0) >= h - k, pltpu.roll(top, h - k, 0) if k < h else top, 0.0)
        return d_halo, d_cur

    shift.defvjp(fwd, bwd)
    return shift


_shift8 = _make_shift(8)
_shift16 = _make_shift(16)


def _make_tail(h):
    @jax.custom_vjp
    def tail(x):
        return x[x.shape[0] - h:]

    def fwd(x):
        return tail(x), x.shape[0]

    def bwd(t, g):
        return (jnp.concatenate([jnp.zeros((t - h, g.shape[1]), F32), g], axis=0),)

    tail.defvjp(fwd, bwd)
    return tail


_tail8 = _make_tail(8)
_tail16 = _make_tail(16)


@jax.custom_vjp
def _cumsum_rows(x):
    t = x.shape[0]
    row, s = _iota(x.shape, 0), 1
    while s < t:
        x = x + jnp.where(row >= s, pltpu.roll(x, s, 0), 0.0)
        s *= 2
    return x


def _cumsum_rows_fwd(x):
    return _cumsum_rows(x), None


def _cumsum_rows_bwd(_, g):
    t = g.shape[0]
    row, s = _iota(g.shape, 0), 1
    while s < t:
        g = g + jnp.where(row < t - s, pltpu.roll(g, t - s, 0), 0.0)
        s *= 2
    return (g,)


_cumsum_rows.defvjp(_cumsum_rows_fwd, _cumsum_rows_bwd)


@jax.custom_vjp
def _rot_pairs(t):
    e = _iota(t.shape, 1) % ATT_HEAD_DIM
    n = t.shape[1]
    return jnp.where(e < 8, -pltpu.roll(t, n - 8, 1), jnp.where(e < 16, pltpu.roll(t, 8, 1), 0.0))


def _rot_pairs_fwd(t):
    return _rot_pairs(t), None


def _rot_pairs_bwd(_, g):
    e = _iota(g.shape, 1) % ATT_HEAD_DIM
    n = g.shape[1]
    return (pltpu.roll(jnp.where(e < 8, -g, 0.0), 8, 1) + pltpu.roll(jnp.where(jnp.logical_and(e >= 8, e < 16), g, 0.0), n - 8, 1),)


_rot_pairs.defvjp(_rot_pairs_fwd, _rot_pairs_bwd)


def _make_thirds():
    @jax.custom_vjp
    def thirds(x):
        w = x.shape[1] // 3
        return x[:, :w], x[:, w:2 * w], x[:, 2 * w:]

    def fwd(x):
        return thirds(x), None

    def bwd(_, g):
        return (jnp.concatenate(g, axis=1),)

    thirds.defvjp(fwd, bwd)
    return thirds


_thirds = _make_thirds()


def _rowk(w, k):
    return jnp.sum(jnp.where(_iota(w.shape, 0) == k, w, 0.0), axis=0, keepdims=True)


def _silu(x):
    return x * (0.5 * jnp.tanh(0.5 * x) + 0.5)


def _softplus(x):
    return jnp.maximum(x, 0.0) + jnp.log(1.0 + jnp.exp(-jnp.abs(x)))


def _tile(dim, target, unit=128):
    if dim <= target:
        return dim
    best = None
    for t in range(unit, target + 1, unit):
        if dim % t == 0:
            best = t
    assert best is not None, (dim, target)
    return best


class Ride:
    def __init__(self, ins, out_shapes, nsem, start, finish):
        self.ins, self.out_shapes, self.nsem, self.start, self.finish = ins, out_shapes, nsem, start, finish

    def specs(self):
        hbm = pl.BlockSpec(memory_space=pl.ANY)
        return [hbm] * len(self.ins), [hbm] * len(self.out_shapes), [pltpu.SemaphoreType.DMA((self.nsem,))] * 2

    def begin(self, in_refs, out_refs, sems, cond=None):
        me = (lax.axis_index("x"), lax.axis_index("y"), lax.axis_index("c"))
        go = lambda: self.start(in_refs, out_refs, sems[0], sems[1], me)
        go() if cond is None else pl.when(cond)(go)

    def end(self, in_refs, out_refs, sems, cond=None):
        me = (lax.axis_index("x"), lax.axis_index("y"), lax.axis_index("c"))
        go = lambda: self.finish(in_refs, out_refs, sems[0], sems[1], me)
        go() if cond is None else pl.when(cond)(go)


class _Riders:
    def reset(self):
        self.booked, self.done = {}, {}

    def book(self, host, ride):
        assert host not in self.booked, host
        self.booked[host] = ride

    def take(self, host):
        return self.booked.pop(host, None)

    def result(self, host):
        return self.done[host]


RIDERS = _Riders()
RIDERS.reset()


class Row:
    def __init__(self, arr, w=None, fb=None, fc=None, diff=True, slot=False, dcols=None, dfc=None, ddtype=F32, view=None):
        self.ddtype = ddtype
        self.view = view
        self.arr = arr
        self.w = arr.shape[2] if w is None else w
        self.fb = (lambda b: 0) if fb is None else fb
        self.fc = (lambda b: 0) if fc is None else fc
        self.diff = diff
        self.slot = slot
        self.dcols = dcols
        self.dfc = dfc


class Vec:
    def __init__(self, arr, w=None, fc=None, diff=True):
        self.arr = arr
        self.w = arr.shape[1] if w is None else w
        self.fc = fc
        self.diff = diff


def _row_spec(r, t, nchunk, reverse):
    shape = (1, t, r.w) if r.view is None else (1, t // r.view, r.view * r.w)
    if reverse:
        return pl.BlockSpec(shape, lambda b, i, r=r: (r.fb(b), nchunk - 1 - i, r.fc(b)))
    return pl.BlockSpec(shape, lambda b, i, r=r: (r.fb(b), i, r.fc(b)))


def _load_row(ref, r, t, scr):
    if r.view is None:
        return ref[0]
    d, w = r.view, r.w
    for q in range(d):
        for j in range(w // 128):
            scr[j, pl.ds(q, t // d, stride=d), :] = ref[0, :, q * w + 128 * j:q * w + 128 * (j + 1)].astype(F32)
    return jnp.concatenate([scr[j] for j in range(w // 128)], axis=1)


def _store_row(ref, r, t, scr, val):
    if r.view is None:
        ref[0] = val.astype(ref.dtype)
        return
    d, w = r.view, r.w
    for j in range(w // 128):
        scr[j] = val[:, 128 * j:128 * (j + 1)]
    for q in range(d):
        for j in range(w // 128):
            ref[0, :, q * w + 128 * j:q * w + 128 * (j + 1)] = scr[j, pl.ds(q, t // d, stride=d), :].astype(ref.dtype)


def _view_scratch(specs, t):
    ws = [r.w for r in specs if r.view is not None]
    return [pltpu.VMEM((max(ws) // 128, t, 128), F32)] if ws else []


def _vec_spec(v):
    if v.fc is None:
        return pl.BlockSpec(v.arr.shape, lambda b, i: (0, 0))
    return pl.BlockSpec((v.arr.shape[0], v.w), lambda b, i, v=v: (0, v.fc(b)))


def _cparams():
    return pltpu.CompilerParams(dimension_semantics=("arbitrary", "arbitrary"), vmem_limit_bytes=VMEM_LIMIT_BYTES)


def scan_fwd(name, fn, *, nb, nchunk, t, rows, vecs, carries, outs, save):
    nr, nv, nc, no = len(rows), len(vecs), len(carries), len(outs)
    ns = nc if save else 0
    ride = RIDERS.take(name)
    r_in, r_out, r_scr = ride.specs() if ride else ([], [], [])

    def body(*refs):
        p = 0
        row_refs = refs[p:p + nr]; p += nr
        vec_refs = refs[p:p + nv]; p += nv
        ride_in = refs[p:p + len(r_in)]; p += len(r_in)
        out_refs = refs[p:p + no]; p += no
        save_refs = refs[p:p + ns]; p += ns
        ride_out = refs[p:p + len(r_out)]; p += len(r_out)
        car = refs[p:p + nc]; p += nc
        scr = refs[p] if stage else None
        sems = refs[p + len(stage):]
        b, i = pl.program_id(0), pl.program_id(1)
        if ride:
            ride.begin(ride_in, ride_out, sems, jnp.logical_and(b == 0, i == 0))
        if nc:
            @pl.when(i == 0)
            def _():
                for c_ref in car:
                    c_ref[...] = jnp.zeros(c_ref.shape, F32)
        cin = [c_ref[...] for c_ref in car]
        if save:
            for s_ref, cv in zip(save_refs, cin):
                s_ref[0, 0] = cv
        new_c, o = fn(i, b, cin, [_load_row(ref, r, t, scr) for ref, r in zip(row_refs, rows)], [v[...] for v in vec_refs])
        for c_ref, cv in zip(car, new_c):
            c_ref[...] = cv
        for o_ref, spec, ov in zip(out_refs, outs, o):
            _store_row(o_ref, spec, t, scr, ov)
        if ride:
            ride.end(ride_in, ride_out, sems, jnp.logical_and(b == nb - 1, i == nchunk - 1))

    stage = _view_scratch(list(rows) + list(outs), t)
    out_shape = [o.arr for o in outs]
    out_specs = [_row_spec(o, t, nchunk, False) for o in outs]
    if save:
        for cs in carries:
            out_shape.append(jax.ShapeDtypeStruct((nb, nchunk) + tuple(cs), F32))
            out_specs.append(pl.BlockSpec((1, 1) + tuple(cs), lambda b, i: (b, i, 0, 0)))
    res = pl.pallas_call(
        body, name=name, grid=(nb, nchunk),
        in_specs=[_row_spec(r, t, nchunk, False) for r in rows] + [_vec_spec(v) for v in vecs] + r_in,
        out_specs=out_specs + r_out, out_shape=out_shape + (list(ride.out_shapes) if ride else []),
        scratch_shapes=[pltpu.VMEM(tuple(cs), F32) for cs in carries] + stage + r_scr,
        compiler_params=_cparams(),
    )(*[r.arr for r in rows], *[v.arr for v in vecs], *(ride.ins if ride else []))
    if ride:
        RIDERS.done[name] = list(res[no + ns:])
    return list(res[:no]), list(res[no:no + ns])


def scan_bwd(name, fn, *, nb, nchunk, t, rows, vecs, carries, saved, douts, adds=None):
    adds = adds or {}
    nr, nv, nc, no = len(rows), len(vecs), len(carries), len(douts)
    dri = [k for k, r in enumerate(rows) if r.diff]
    dvi = [k for k, v in enumerate(vecs) if v.diff]
    add_keys = sorted(adds)
    na = len(add_keys)
    ride = RIDERS.take(name)
    r_in, r_out, r_scr = ride.specs() if ride else ([], [], [])

    def body(*refs):
        p = 0
        row_refs = refs[p:p + nr]; p += nr
        vec_refs = refs[p:p + nv]; p += nv
        save_refs = refs[p:p + nc]; p += nc
        dout_refs = refs[p:p + no]; p += no
        add_refs = refs[p:p + na]; p += na
        ride_in = refs[p:p + len(r_in)]; p += len(r_in)
        drow_refs = refs[p:p + len(dri)]; p += len(dri)
        dvec_refs = refs[p:p + len(dvi)]; p += len(dvi)
        ride_out = refs[p:p + len(r_out)]; p += len(r_out)
        dcar = refs[p:p + nc]; p += nc
        scr = refs[p] if stage else None
        sems = refs[p + len(stage):]
        b, ir = pl.program_id(0), pl.program_id(1)
        ci = nchunk - 1 - ir
        if ride:
            ride.begin(ride_in, ride_out, sems, jnp.logical_and(b == 0, ir == 0))
        if nc:
            @pl.when(ir == 0)
            def _():
                for c_ref in dcar:
                    c_ref[...] = jnp.zeros(c_ref.shape, F32)
        rows_v = [_load_row(ref, r, t, scr) for ref, r in zip(row_refs, rows)]
        vecs_v = [v[...] for v in vec_refs]
        cin = [s[0, 0] for s in save_refs]
        dc = [c_ref[...] for c_ref in dcar]
        dout_v = [_load_row(ref, r, t, scr).astype(F32) for ref, r in zip(dout_refs, douts)]

        def f(cs, dr, dv):
            rr, vv = list(rows_v), list(vecs_v)
            for k, idx in enumerate(dri):
                rr[idx] = dr[k]
            for k, idx in enumerate(dvi):
                vv[idx] = dv[k]
            return fn(ci, b, cs, rr, vv)

        _, vjp = jax.vjp(f, cin, [rows_v[k].astype(F32) for k in dri], [vecs_v[k].astype(F32) for k in dvi])
        dcin, drows, dvecs = vjp((dc, dout_v))
        for c_ref, cv in zip(dcar, dcin):
            c_ref[...] = cv
        for k, (o_ref, ov) in enumerate(zip(drow_refs, drows)):
            if dri[k] in adds:
                ov = ov + add_refs[add_keys.index(dri[k])][0].astype(F32)
            _store_row(o_ref, rows[dri[k]], t, scr, ov)
        for k, (o_ref, ov) in enumerate(zip(dvec_refs, dvecs)):
            first = (ir == 0) if vecs[dvi[k]].fc is not None else jnp.logical_and(ir == 0, b == 0)

            @pl.when(first)
            def _(o_ref=o_ref, ov=ov):
                o_ref[...] = ov

            @pl.when(jnp.logical_not(first))
            def _(o_ref=o_ref, ov=ov):
                o_ref[...] += ov

        if ride:
            ride.end(ride_in, ride_out, sems, jnp.logical_and(b == nb - 1, ir == nchunk - 1))

    stage = _view_scratch(list(rows) + list(douts), t)
    in_specs = ([_row_spec(r, t, nchunk, True) for r in rows] + [_vec_spec(v) for v in vecs]
                + [pl.BlockSpec((1, 1) + tuple(cs), lambda b, i: (b, nchunk - 1 - i, 0, 0)) for cs in carries]
                + [_row_spec(d, t, nchunk, True) for d in douts]
                + [_row_spec(adds[k], t, nchunk, True) for k in add_keys] + r_in)
    out_shape, out_specs = [], []
    for k in dri:
        r = rows[k]
        if r.slot:
            out_shape.append(jax.ShapeDtypeStruct((nb, r.arr.shape[1], r.w), r.ddtype))
            out_specs.append(pl.BlockSpec((1, t, r.w), lambda b, i: (b, nchunk - 1 - i, 0)))
        elif r.dcols is not None:
            out_shape.append(jax.ShapeDtypeStruct((r.arr.shape[0], r.arr.shape[1], r.dcols), r.ddtype))
            out_specs.append(pl.BlockSpec((1, t, r.w), lambda b, i, r=r: (r.fb(b), nchunk - 1 - i, r.dfc(b))))
        else:
            out_shape.append(jax.ShapeDtypeStruct(r.arr.shape, r.ddtype))
            out_specs.append(_row_spec(r, t, nchunk, True))
    for k in dvi:
        out_shape.append(jax.ShapeDtypeStruct(vecs[k].arr.shape, F32))
        out_specs.append(_vec_spec(vecs[k]))
    nd = len(dri) + len(dvi)
    res = pl.pallas_call(
        body, name=name, grid=(nb, nchunk), in_specs=in_specs, out_specs=out_specs + r_out,
        out_shape=out_shape + (list(ride.out_shapes) if ride else []),
        scratch_shapes=[pltpu.VMEM(tuple(cs), F32) for cs in carries] + stage + r_scr,
        compiler_params=_cparams(),
    )(*[r.arr for r in rows], *[v.arr for v in vecs], *saved, *[d.arr for d in douts], *[adds[k].arr for k in add_keys],
      *(ride.ins if ride else []))
    if ride:
        RIDERS.done[name] = list(res[nd:])
    return list(res[:len(dri)]), list(res[len(dri):nd])


def out_row(shape, dtype=F32, w=None, fb=None, fc=None):
    return Row(jax.ShapeDtypeStruct(shape, dtype), w, fb, fc)


def _conv(shift, halo, cur, w, bias, taps):
    y = bias
    for k in range(taps):
        y = y + _rowk(w, k) * shift(halo, cur, taps - 1 - k)
    return y


def _ssd_fn(ci, b, carries, rows, vecs):
    cx, cb_, cc, ht = carries
    z, xr, br, cr, dtr = rows
    cwx, cbx, cwb, cbb, cwc, cbc, dtb, alog, dsk, ng = vecs
    t = z.shape[0]
    xs = _silu(_conv(_shift8, cx, xr, cwx, cbx, 4))
    bm = _silu(_conv(_shift8, cb_, br, cwb, cbb, 4))
    cm = _silu(_conv(_shift8, cc, cr, cwc, cbc, 4))
    dt = _softplus(dtr + dtb)
    acol = _cumsum_rows(dt * (-jnp.exp(alog)))
    arow = acol.T
    r, c = _iota((t, t), 0), _iota((t, t), 1)
    causal = r >= c
    cbm = _bdot(cm, bm, "nt")
    lane, sub = _iota(acol.shape, 1), _iota(arow.shape, 0)
    colh = _iota(xs.shape, 1) // 64
    a, dtx, dx, acs = jnp.zeros(xs.shape, F32), jnp.zeros(xs.shape, F32), jnp.zeros((1, xs.shape[1]), F32), []
    for j in range(4):
        h = 4 * b + j
        ac = jnp.sum(jnp.where(lane == h, acol, 0.0), axis=1, keepdims=True)
        acs.append(ac)
        a = jnp.where(colh == j, ac, a)
        dtx = jnp.where(colh == j, jnp.sum(jnp.where(lane == h, dt, 0.0), axis=1, keepdims=True), dtx)
        dx = jnp.where(_iota(dx.shape, 1) // 64 == j, jnp.sum(jnp.where(_iota(dsk.shape, 1) == h, dsk, 0.0), axis=1, keepdims=True), dx)
    atot = jnp.sum(jnp.where(_iota(a.shape, 0) == t - 1, a, 0.0), axis=0, keepdims=True)
    x = xs * dtx
    ydiag = jnp.zeros(x.shape, F32)
    for j in range(4):
        ar = jnp.sum(jnp.where(sub == 4 * b + j, arow, 0.0), axis=0, keepdims=True)
        lmat = jnp.exp(jnp.where(causal, acs[j] - ar, NEG))
        ydiag = ydiag + _bdot(cbm * lmat, jnp.where(colh == j, x, 0.0), "nn")
    yoff = _bdot(cm, ht, "nn") * jnp.exp(a)
    ht_new = ht * jnp.exp(atot) + _bdot(bm, x * jnp.exp(atot - a), "tn")
    y = ydiag + yoff + dx * xs
    yz = y * _silu(z)
    yn = yz * lax.rsqrt(jnp.mean(yz * yz, axis=-1, keepdims=True) + NORM_EPS) * ng
    return [_tail8(xr), _tail8(br), _tail8(cr), ht_new], [yn]


_SSD_T = 256
_SSD_CARRIES = [(8, 256), (8, 128), (8, 128), (128, 256)]


def _ssd_io(proj3, p):
    own = lambda b: b
    rows = [Row(proj3, 256, fc=own, dcols=512, dfc=own, ddtype=BF16),
            Row(proj3, 256, fc=lambda b: 2 + b, dcols=512, dfc=own, ddtype=BF16),
            Row(proj3, 128, fc=lambda b: 8 + b, dcols=256, dfc=own, ddtype=BF16),
            Row(proj3, 128, fc=lambda b: 10 + b, dcols=256, dfc=own, ddtype=BF16),
            Row(proj3, 128, fc=lambda b: 14, slot=True)]
    vecs = [Vec(p["cw"], 256, lambda b: b), Vec(p["cb"], 256, lambda b: b),
            Vec(p["cw"], 128, lambda b: 4 + b), Vec(p["cb"], 128, lambda b: 4 + b),
            Vec(p["cw"], 128, lambda b: 6 + b), Vec(p["cb"], 128, lambda b: 6 + b),
            Vec(p["dtb"]), Vec(p["alog"]), Vec(p["dsk"]), Vec(p["ng"], 256, lambda b: b)]
    return rows, vecs


def ssd_forward(name, proj3, p):
    rows, vecs = _ssd_io(proj3, p)
    s = proj3.shape[1]
    (y,), saved = scan_fwd(name, _ssd_fn, nb=2, nchunk=s // _SSD_T, t=_SSD_T, rows=rows, vecs=vecs,
                           carries=_SSD_CARRIES, outs=[out_row((1, s, SSD_INNER), BF16, 256, fc=lambda b: b)], save=True)
    return y, saved


def ssd_backward(name, proj3, p, saved, dmix3):
    rows, vecs = _ssd_io(proj3, p)
    s = proj3.shape[1]
    drows, dvecs = scan_bwd(name, _ssd_fn, nb=2, nchunk=s // _SSD_T, t=_SSD_T, rows=rows, vecs=vecs,
                            carries=_SSD_CARRIES, saved=saved, douts=[Row(dmix3, 256, fc=lambda b: b)])
    return drows, dvecs


def _pool_fn(ci, b, carries, rows, vecs):
    (cu,) = carries
    (u,) = rows
    wbd, scale = vecs
    t = u.shape[0]
    pos = ci * t + _iota(u.shape, 0)
    grp = _iota(u.shape, 1) // 64
    acc, pooled, k = u, jnp.zeros(u.shape, F32), 1
    for gi, w in enumerate(POOL_WINDOWS):
        while k < w:
            acc = acc + _shift16(cu, u, k)
            k += 1
        pooled = jnp.where(grp == gi, acc / jnp.minimum(pos + 1, w).astype(F32), pooled)
    y = _bdot(pooled - u, wbd, "nn") * scale
    return [_tail16(u)], [y]


_POOL_T = 256


def _pool_io(proj3, wbd, scale):
    return [Row(proj3, 256, fc=lambda b: 6, dcols=256, dfc=lambda b: 0, ddtype=BF16)], [Vec(wbd), Vec(scale)]


def pool_forward(name, proj3, wbd, scale):
    rows, vecs = _pool_io(proj3, wbd, scale)
    s = proj3.shape[1]
    (y,), saved = scan_fwd(name, _pool_fn, nb=1, nchunk=s // _POOL_T, t=_POOL_T, rows=rows, vecs=vecs,
                           carries=[(16, 256)], outs=[out_row((1, s, POOL_W), BF16)], save=True)
    return y, saved


def pool_backward(name, proj3, wbd, scale, saved, dmix3):
    rows, vecs = _pool_io(proj3, wbd, scale)
    s = proj3.shape[1]
    return scan_bwd(name, _pool_fn, nb=1, nchunk=s // _POOL_T, t=_POOL_T, rows=rows, vecs=vecs,
                    carries=[(16, 256)], saved=saved, douts=[Row(dmix3, 256, fc=lambda b: 2)])


def _attn_fn(ci, b, carries, rows, vecs):
    kp, vp = carries
    qr, kr, v = _thirds(rows[0])
    scale = ATT_HEAD_DIM ** -0.5
    q = qr
    n = q.shape[0]
    r, c = _iota((n, n), 0), _iota((n, n), 1)
    prev_ok, cur_ok = jnp.logical_and(c >= r, ci > 0), r >= c
    head = _iota(q.shape, 1) // ATT_HEAD_DIM
    o, lse = jnp.zeros(q.shape, F32), jnp.zeros(q.shape, F32)
    for h in range(ATT_HEADS):
        mine = head == h
        qh = jnp.where(mine, qr, 0.0)
        sp = jnp.where(prev_ok, _bdot(qh, kp, "nt") * scale, NEG)
        sc = jnp.where(cur_ok, _bdot(qh, kr, "nt") * scale, NEG)
        m = lax.stop_gradient(jnp.maximum(jnp.max(sp, axis=1, keepdims=True), jnp.max(sc, axis=1, keepdims=True)))
        pp, pc = jnp.exp(sp - m), jnp.exp(sc - m)
        l = jnp.sum(pp, axis=1, keepdims=True) + jnp.sum(pc, axis=1, keepdims=True)
        o = jnp.where(mine, (_bdot(pp, vp, "nn") + _bdot(pc, v, "nn")) / l, o)
        lse = jnp.where(mine, m + jnp.log(l), lse)
    return [kr, v], [o, lse]


_ATT_CARRIES = [(ATT_BLOCK, ATT_W), (ATT_BLOCK, ATT_W)]


def attn_forward(name, pv, d):
    l = pv.shape[1]
    own = lambda b: b
    outs = [out_row((1, l, d * ATT_W), F32, ATT_W, fc=own) for _ in range(2)]
    (o, lse), saved = scan_fwd(name, _attn_fn, nb=d, nchunk=l // ATT_BLOCK, t=ATT_BLOCK, rows=[Row(pv, 3 * ATT_W, fc=own)],
                               vecs=[], carries=_ATT_CARRIES, outs=outs, save=True)
    return o, lse, saved


def attn_backward(name, pv, d, saved, do, dlse):
    l = pv.shape[1]
    own = lambda b: b
    (dpv,), _ = scan_bwd(name, _attn_fn, nb=d, nchunk=l // ATT_BLOCK, t=ATT_BLOCK, rows=[Row(pv, 3 * ATT_W, fc=own)], vecs=[],
                         carries=_ATT_CARRIES, saved=saved, douts=[Row(do, ATT_W, fc=own), Row(dlse, ATT_W, fc=own)])
    return dpv


def _rope_fn(ci, b, carries, rows, vecs):
    x, cs, sn = rows
    return [], [x * cs + _rot_pairs(x) * sn]


def _rope3_fn(ci, b, carries, rows, vecs):
    _, (y,) = _rope_fn(ci, b, carries, rows, vecs)
    return [], [y, y, y]


def _by_residue(a_or_shape, w, d):
    if isinstance(a_or_shape, tuple):
        _, s, _ = a_or_shape
        return Row(jax.ShapeDtypeStruct((1, s // d, d * w), F32), w, view=None if d == 1 else d)
    return Row(a_or_shape, w, view=None if d == 1 else d)


def rope_forward(name, qkv3, cs3, sn3):
    s, w = qkv3.shape[1], qkv3.shape[2]
    ys, _ = scan_fwd(name, _rope3_fn, nb=1, nchunk=s // _ROW_T, t=_ROW_T, vecs=[], carries=[], save=False,
                     rows=[Row(qkv3), Row(cs3, diff=False), Row(sn3, diff=False)],
                     outs=[_by_residue(qkv3.shape, w, d) for _, d in ATT_PATTERNS])
    return ys


def rope_backward(name, qkv3, cs3, sn3, dys):
    s, w = qkv3.shape[1], qkv3.shape[2]
    (dx,), _ = scan_bwd(name, _rope3_fn, nb=1, nchunk=s // _ROW_T, t=_ROW_T, vecs=[], carries=[], saved=[],
                        rows=[Row(qkv3, ddtype=BF16), Row(cs3, diff=False), Row(sn3, diff=False)],
                        douts=[_by_residue(a, w, d) for a, (_, d) in zip(dys, ATT_PATTERNS)])
    return dx


def _merge_fn(ci, b, carries, rows, vecs):
    o1, o2, o3, l1, l2, l3 = rows
    mx = lax.stop_gradient(jnp.maximum(l1, jnp.maximum(l2, l3)))
    e1, e2, e3 = jnp.exp(l1 - mx), jnp.exp(l2 - mx), jnp.exp(l3 - mx)
    return [], [(e1 * o1 + e2 * o2 + e3 * o3) / (e1 + e2 + e3)]


_ROW_T = 512


def _merge_rows(os_, ls_):
    ds = [d for _, d in ATT_PATTERNS]
    return [_by_residue(a, ATT_W, d) for a, d in zip(os_, ds)] + [_by_residue(a, ATT_W, d) for a, d in zip(ls_, ds)]


def merge_forward(name, os_, ls_, s):
    (y,), _ = scan_fwd(name, _merge_fn, nb=1, nchunk=s // _ROW_T, t=_ROW_T, rows=_merge_rows(os_, ls_), vecs=[],
                       carries=[], outs=[out_row((1, s, ATT_W), BF16)], save=False)
    return y


def merge_backward(name, os_, ls_, dmix3):
    s = dmix3.shape[1]
    drows, _ = scan_bwd(name, _merge_fn, nb=1, nchunk=s // _ROW_T, t=_ROW_T, rows=_merge_rows(os_, ls_), vecs=[],
                        carries=[], saved=[], douts=[Row(dmix3, 256, fc=lambda b: 3)])
    return drows


def _norm_mod_fn(ci, b, carries, rows, vecs):
    (x,) = rows
    g, sc, sh = vecs
    xn = x * lax.rsqrt(jnp.mean(x * x, axis=-1, keepdims=True) + NORM_EPS)
    return [], [xn * g * (1.0 + sc) + sh]


def norm_mod_forward(name, x3, g, sc, sh):
    s = x3.shape[1]
    (h,), _ = scan_fwd(name, _norm_mod_fn, nb=1, nchunk=s // _ROW_T, t=_ROW_T, rows=[Row(x3)], vecs=[Vec(g), Vec(sc), Vec(sh)],
                       carries=[], outs=[out_row(x3.shape, BF16)], save=False)
    return h


def norm_mod_backward(name, x3, g, sc, sh, dh3, add3):
    s = x3.shape[1]
    (dx,), dv = scan_bwd(name, _norm_mod_fn, nb=1, nchunk=s // _ROW_T, t=_ROW_T, rows=[Row(x3)], vecs=[Vec(g), Vec(sc), Vec(sh)],
                         carries=[], saved=[], douts=[Row(dh3)], adds={0: Row(add3)})
    return dx, dv


def _gate_fn(ci, b, carries, rows, vecs):
    return [], [rows[0] * vecs[0]]


def gate_backward(name, o3, g, dx3):
    s = o3.shape[1]
    (do,), (dg,) = scan_bwd(name, _gate_fn, nb=1, nchunk=s // _ROW_T, t=_ROW_T, rows=[Row(o3, ddtype=BF16)], vecs=[Vec(g)],
                            carries=[], saved=[], douts=[Row(dx3)])
    return do, dg


def _make_halves():
    @jax.custom_vjp
    def halves(x):
        h = x.shape[1] // 2
        return x[:, :h], x[:, h:]

    def fwd(x):
        return halves(x), None

    def bwd(_, g):
        return (jnp.concatenate(g, axis=1),)

    halves.defvjp(fwd, bwd)
    return halves


_halves = _make_halves()


def _ffn_fn(ci, b, carries, rows, vecs):
    (cu,) = carries
    (u,) = rows
    w, bias = vecs
    hg, hu = _halves(_conv(_shift8, cu, u, w, bias, 3))
    return [_tail8(u)], [_silu(hg) * hu]


_FFN_T = 256
_FFN_CW = FFN_DIM // 2
_FFN_CARRIES = [(8, 2 * _FFN_CW)]
FFN_BLOCK_ORDER = [0, 2, 1, 3]


def _ffn_io(up3, cw, cb):
    own = lambda b: b
    return [Row(up3, 2 * _FFN_CW, fc=own, ddtype=BF16)], [Vec(cw, 2 * _FFN_CW, own), Vec(cb, 2 * _FFN_CW, own)]


def ffn_down_forward(name, up3, cw, cb, w_down, res, gate):
    s, t, cw2 = up3.shape[1], _FFN_T, 2 * _FFN_CW
    d = w_down.shape[1]
    nchunk = s // t
    ride = RIDERS.take(name)
    r_in, r_out, r_scr = ride.specs() if ride else ([], [], [])

    def body(*refs):
        up_ref, cw_ref, cb_ref, wd_ref, res_ref, g_ref = refs[:6]
        ride_in = refs[6:6 + len(r_in)]
        act_ref, save_ref, dn_ref, x2_ref = refs[6 + len(r_in):10 + len(r_in)]
        ride_out = refs[10 + len(r_in):10 + len(r_in) + len(r_out)]
        car, acc = refs[10 + len(r_in) + len(r_out):12 + len(r_in) + len(r_out)]
        sems = refs[12 + len(r_in) + len(r_out):]
        i, b = pl.program_id(0), pl.program_id(1)
        if ride:
            ride.begin(ride_in, ride_out, sems, jnp.logical_and(i == 0, b == 0))

        @pl.when(i == 0)
        def _():
            car[b] = jnp.zeros(car.shape[1:], F32)

        cin = car[b]
        save_ref[0, 0] = cin
        (new_c,), (act,) = _ffn_fn(i, b, [cin], [up_ref[0]], [cw_ref[...], cb_ref[...]])
        car[b] = new_c
        act_ref[0] = act.astype(act_ref.dtype)
        part = _mxu(act, wd_ref[...], "nn")

        @pl.when(b == 0)
        def _():
            acc[...] = part

        @pl.when(b == 1)
        def _():
            tot = acc[...] + part
            dn_ref[...] = tot
            x2_ref[...] = res_ref[...] + g_ref[...] * tot

        if ride:
            ride.end(ride_in, ride_out, sems, jnp.logical_and(i == nchunk - 1, b == 1))

    tile = pl.BlockSpec((t, d), lambda i, b: (i, 0))
    out = pl.pallas_call(
        body, name=name, grid=(nchunk, 2),
        in_specs=[pl.BlockSpec((1, t, cw2), lambda i, b: (0, i, b)), pl.BlockSpec((cw.shape[0], cw2), lambda i, b: (0, b)),
                  pl.BlockSpec((1, cw2), lambda i, b: (0, b)), pl.BlockSpec((_FFN_CW, d), lambda i, b: (b, 0)), tile,
                  pl.BlockSpec((1, d), lambda i, b: (0, 0))] + r_in,
        out_specs=[pl.BlockSpec((1, t, _FFN_CW), lambda i, b: (0, i, b)), pl.BlockSpec((1, 1, 8, cw2), lambda i, b: (b, i, 0, 0)),
                   tile, tile] + r_out,
        out_shape=[jax.ShapeDtypeStruct((1, s, FFN_DIM), BF16), jax.ShapeDtypeStruct((2, nchunk, 8, cw2), F32),
                   jax.ShapeDtypeStruct((s, d), F32), jax.ShapeDtypeStruct((s, d), F32)] + (list(ride.out_shapes) if ride else []),
        scratch_shapes=[pltpu.VMEM((2, 8, cw2), F32), pltpu.VMEM((t, d), F32)] + r_scr,
        compiler_params=_cparams(),
    )(up3, cw, cb, w_down, res, gate, *(ride.ins if ride else []))
    if ride:
        RIDERS.done[name] = list(out[4:])
    return out[0], [out[1]], out[2], out[3]


def ffn_mid_backward(name, up3, cw, cb, saved, dact3):
    rows, vecs = _ffn_io(up3, cw, cb)
    s = up3.shape[1]
    return scan_bwd(name, _ffn_fn, nb=2, nchunk=s // _FFN_T, t=_FFN_T, rows=rows, vecs=vecs, carries=_FFN_CARRIES,
                    saved=saved, douts=[Row(dact3, _FFN_CW, fc=lambda b: b)])


def _adam_fn(ci, b, carries, rows, vecs):
    w, g, m, v = rows
    m = ADAM_B1 * m + (1.0 - ADAM_B1) * g
    v = ADAM_B2 * v + (1.0 - ADAM_B2) * (g * g)
    m_hat = m / (1.0 - ADAM_B1 ** ADAM_STEP)
    v_hat = v / (1.0 - ADAM_B2 ** ADAM_STEP)
    delta = -ADAM_LR * (m_hat / (jnp.sqrt(v_hat) + ADAM_EPS) + ADAM_WD * w)
    return [], [delta, m, v]


def adamw(name, w, g, m, v):
    shape = w.shape
    c = shape[-1]
    r = int(np.prod(shape[:-1]))
    t = _tile(r, 256, 8)
    as3 = lambda a: a.reshape(1, r, c)
    outs, _ = scan_fwd(name, _adam_fn, nb=1, nchunk=r // t, t=t, rows=[Row(as3(a)) for a in (w, g, m, v)], vecs=[], carries=[],
                       outs=[out_row((1, r, c)) for _ in range(3)], save=False)
    return [o.reshape(shape) for o in outs]


def rope_tables(positions):
    inv_freq = ROPE_THETA ** (-jnp.arange(0, ROT_DIM, 2, dtype=F32) / ROT_DIM)
    ang = positions.astype(F32)[:, None] * inv_freq
    s = positions.shape[0]
    cs = jnp.concatenate([jnp.cos(ang), jnp.cos(ang), jnp.ones((s, ATT_HEAD_DIM - ROT_DIM), F32)], axis=1)
    sn = jnp.concatenate([jnp.sin(ang), jnp.sin(ang), jnp.zeros((s, ATT_HEAD_DIM - ROT_DIM), F32)], axis=1)
    cs3 = jnp.concatenate([jnp.tile(cs, (1, 2 * ATT_HEADS)), jnp.ones((s, ATT_W), F32)], axis=1)
    sn3 = jnp.concatenate([jnp.tile(sn, (1, 2 * ATT_HEADS)), jnp.zeros((s, ATT_W), F32)], axis=1)
    return cs3[None], sn3[None]


def attention_forward(lname, qkv3, cs3, sn3):
    s = qkv3.shape[1]
    rotated = rope_forward(f"{lname}_rope", qkv3, cs3, sn3)
    os_, ls_, keep = [], [], []
    for pi, (_, d) in enumerate(ATT_PATTERNS):
        o, lse, saved = attn_forward(f"{lname}_attn{pi}", rotated[pi], d)
        os_.append(o)
        ls_.append(lse)
        keep.append(saved)
    y = merge_forward(f"{lname}_merge", os_, ls_, s)
    return y, (rotated, os_, ls_, keep)


def attention_backward(lname, qkv3, cs3, sn3, res, dmix3):
    rotated, os_, ls_, keep = res
    dm = merge_backward(f"{lname}_merge_b", os_, ls_, dmix3)
    dys = [attn_backward(f"{lname}_attn{pi}_b", rotated[pi], d, keep[pi], dm[pi], dm[3 + pi]) for pi, (_, d) in enumerate(ATT_PATTERNS)]
    return rope_backward(f"{lname}_rope_b", qkv3, cs3, sn3, dys)


def final_loss(name, x3, t3, g):
    s, d = x3.shape[1], x3.shape[2]
    t = _ROW_T

    def body(x_ref, t_ref, g_ref, loss_ref, dx_ref, dg_ref):
        i = pl.program_id(0)
        tv = t_ref[0]

        def f(x, gg):
            y = x * lax.rsqrt(jnp.mean(x * x, axis=-1, keepdims=True) + NORM_EPS) * gg
            e = y - tv
            return 0.5 * jnp.sum(jnp.mean(e * e, axis=-1, keepdims=True), axis=0, keepdims=True)

        l, vjp = jax.vjp(f, x_ref[0], g_ref[...])
        dx, dg = vjp(jnp.ones((1, 1), F32))
        dx_ref[0] = dx

        @pl.when(i == 0)
        def _():
            loss_ref[...] = jnp.zeros(loss_ref.shape, F32)
            dg_ref[...] = jnp.zeros(dg_ref.shape, F32)

        loss_ref[...] += jnp.broadcast_to(l, loss_ref.shape)
        dg_ref[...] += dg

    row = pl.BlockSpec((1, t, d), lambda i: (0, i, 0))
    vec = pl.BlockSpec((1, d), lambda i: (0, 0))
    return pl.pallas_call(
        body, name=name, grid=(s // t,), in_specs=[row, row, vec],
        out_specs=[pl.BlockSpec((8, 128), lambda i: (0, 0)), row, vec],
        out_shape=[jax.ShapeDtypeStruct((8, 128), F32), jax.ShapeDtypeStruct(x3.shape, F32), jax.ShapeDtypeStruct((1, d), F32)],
        compiler_params=pltpu.CompilerParams(dimension_semantics=("arbitrary",), vmem_limit_bytes=VMEM_LIMIT_BYTES),
    )(x3, t3, g)


_ADA_TN = 512


def ada_forward(name, c16, ada_w):
    depth, d, cols = ada_w.shape

    def body(c_ref, w_ref, o_ref):
        o_ref[0] = _mxu(_silu(c_ref[...]), w_ref[0], "nn")

    return pl.pallas_call(
        body, name=name, grid=(depth, cols // _ADA_TN),
        in_specs=[pl.BlockSpec((16, d), lambda l, j: (0, 0)), pl.BlockSpec((1, d, _ADA_TN), lambda l, j: (l, 0, j))],
        out_specs=pl.BlockSpec((1, 16, _ADA_TN), lambda l, j: (l, 0, j)),
        out_shape=jax.ShapeDtypeStruct((depth, 16, cols), F32),
        compiler_params=pltpu.CompilerParams(dimension_semantics=("arbitrary", "arbitrary"), vmem_limit_bytes=VMEM_LIMIT_BYTES),
    )(c16, ada_w)


def ada_backward(name, c16, dmod16, w, m, v):
    depth, d, cols = w.shape

    def body(c_ref, dm_ref, w_ref, m_ref, v_ref, g_ref, dl_ref, nm_ref, nv_ref):
        g = _mxu(_silu(c_ref[...]), dm_ref[0], "tn")
        _, (delta, nm, nv) = _adam_fn(None, None, [], [w_ref[0], g, m_ref[0], v_ref[0]], [])
        g_ref[0], dl_ref[0], nm_ref[0], nv_ref[0] = g, delta, nm, nv

    blk = pl.BlockSpec((1, d, _ADA_TN), lambda l, j: (l, 0, j))
    return pl.pallas_call(
        body, name=name, grid=(depth, cols // _ADA_TN),
        in_specs=[pl.BlockSpec((16, d), lambda l, j: (0, 0)), pl.BlockSpec((1, 16, _ADA_TN), lambda l, j: (l, 0, j)), blk, blk, blk],
        out_specs=[blk] * 4, out_shape=[jax.ShapeDtypeStruct(w.shape, F32)] * 4,
        compiler_params=pltpu.CompilerParams(dimension_semantics=("arbitrary", "arbitrary"), vmem_limit_bytes=VMEM_LIMIT_BYTES),
    )(c16, dmod16, w, m, v)


def _sum_fn(ci, b, carries, rows, vecs):
    acc = rows[0]
    for r in rows[1:]:
        acc = acc + r
    return [], [acc]


def sum_slots(name, a, nsum, out_dtype=F32):
    n, r, c = a.shape
    nb = n // nsum
    t = _tile(r, 256, 8)
    rows = [Row(a, fb=(lambda b, k=k: k * nb + b)) for k in range(nsum)]
    (out,), _ = scan_fwd(name, _sum_fn, nb=nb, nchunk=r // t, t=t, rows=rows, vecs=[], carries=[],
                         outs=[out_row((nb, r, c), out_dtype, fb=lambda b: b)], save=False)
    return out


def _flip(mask, pos):
    return tuple((1 - p) if m else p for m, p in zip(mask, pos))


ALL_PEERS = [(a, b, c) for a in (0, 1) for b in (0, 1) for c in (0, 1)][1:]
CHIP_PEERS = [(1, 0, 0), (0, 1, 0), (1, 1, 0)]
SIBLING = [(0, 0, 1)]


def _divisor(size, target, unit):
    best = 1
    for n in range(1, target + 1):
        if size % n == 0 and (size // n) % unit == 0:
            best = n
    return best


def _pieces(src, dst, pieces):
    shape = src.shape
    unit = 16 if src.dtype == BF16 else 8
    if pieces <= 1:
        return [(src, dst)]
    if len(shape) == 2:
        n = _divisor(shape[0], pieces, unit)
        s = shape[0] // n
        return [(src.at[pl.ds(i * s, s)], dst.at[pl.ds(i * s, s)]) for i in range(n)]
    assert len(shape) == 3, shape
    n = _divisor(shape[1], max(pieces // shape[0], 1), unit)
    s = shape[1] // n
    return [(src.at[j, pl.ds(i * s, s)], dst.at[j, pl.ds(i * s, s)]) for j in range(shape[0]) for i in range(n)]


def comm_call(name, arrays, out_shapes, masks, src_fn, dst_fn, local_fn=None, pieces=1):
    na, npeer = len(arrays), len(masks)

    def body(*refs):
        ins, outs = refs[:na], refs[na:2 * na]
        send_sems, recv_sems, loc_sems = refs[2 * na:]
        me = (lax.axis_index("x"), lax.axis_index("y"), lax.axis_index("c"))
        local = []
        if local_fn is not None:
            for k in range(na):
                s, d = local_fn(k, ins[k], outs[k], me)
                for ps, pd in _pieces(s, d, pieces):
                    pltpu.make_async_copy(ps, pd, loc_sems.at[k]).start()
                local.append(pltpu.make_async_copy(s, d, loc_sems.at[k]))

        def remote(k, p, src, dst, to):
            return pltpu.make_async_remote_copy(
                src_ref=src, dst_ref=dst, send_sem=send_sems.at[k * npeer + p], recv_sem=recv_sems.at[k * npeer + p],
                device_id=to, device_id_type=MESH)

        for k in range(na):
            for p in range(npeer):
                peer = _flip(masks[p], me)
                for ps, pd in _pieces(src_fn(k, ins[k], me, peer), dst_fn(k, outs[k], me), pieces):
                    remote(k, p, ps, pd, peer).start()
        for k in range(na):
            for p in range(npeer):
                peer = _flip(masks[p], me)
                remote(k, p, src_fn(k, ins[k], me, peer), dst_fn(k, outs[k], peer), peer).wait_recv()
        for k in range(na):
            for p in range(npeer):
                peer = _flip(masks[p], me)
                remote(k, p, src_fn(k, ins[k], me, peer), dst_fn(k, outs[k], me), peer).wait_send()
        for cp in local:
            cp.wait()

    hbm = pl.BlockSpec(memory_space=pl.ANY)
    out = pl.pallas_call(
        body, name=name, in_specs=[hbm] * na, out_specs=[hbm] * na,
        out_shape=[jax.ShapeDtypeStruct(s, a.dtype) for s, a in zip(out_shapes, arrays)],
        scratch_shapes=[pltpu.SemaphoreType.DMA((na * npeer,)), pltpu.SemaphoreType.DMA((na * npeer,)),
                        pltpu.SemaphoreType.DMA((na,))],
    )(*arrays)
    return list(out)


def _dev(pos):
    return 4 * pos[0] + 2 * pos[1] + pos[2]


def _chip(pos):
    return 2 * pos[0] + pos[1]


def allgather8(name, a):
    (out,) = comm_call(name, [a], [(8,) + a.shape], ALL_PEERS,
                       src_fn=lambda k, r, me, peer: r, dst_fn=lambda k, o, sender: o.at[_dev(sender)])
    me = _dev((lax.axis_index("x"), lax.axis_index("y"), lax.axis_index("c")))
    return jnp.where((jnp.arange(8) == me)[:, None, None], a[None], out)


def swap_layers(name, arrays, c):
    got = comm_call(name, arrays, [a.shape for a in arrays], SIBLING,
                    src_fn=lambda k, r, me, peer: r, dst_fn=lambda k, o, sender: o, pieces=32)
    return [[jnp.where(c == 0, a, g), jnp.where(c == 0, g, a)] for a, g in zip(arrays, got)]


def _rows_of(shape):
    return -(-int(np.prod(shape)) // 1024) * 8


def _pack(arrs):
    parts = []
    for a in arrs:
        flat = a.reshape(-1).astype(F32)
        parts.append(jnp.pad(flat, (0, _rows_of(a.shape) * 128 - flat.shape[0])).reshape(-1, 128))
    rows = sum(p.shape[0] for p in parts)
    parts.append(jnp.zeros(((-rows) % _ROW_T, 128), F32))
    return jnp.concatenate(parts, axis=0)


def _unpack(buf, shapes):
    out, o = [], 0
    for s in shapes:
        r, n = _rows_of(s), int(np.prod(s))
        out.append(buf[o:o + r].reshape(-1)[:n].reshape(s))
        o += r
    return out


_WEIGHTS = ["ada_w", "ada_b", "norm1_g", "w_in", "ssd_conv_w", "ssd_conv_b", "ssd_dt_bias", "ssd_a_log", "ssd_d", "ssd_norm_g",
            "pool_w", "pool_scale", "w_out", "norm2_g", "ffn_up", "ffn_conv_w", "ffn_conv_b", "ffn_down", "final_g"]
_BIG = ["w_in", "w_out", "ffn_up", "ffn_down"]
_SMALL = [n for n in _WEIGHTS if n not in _BIG and n != "ada_w"]
_COL_SHARDED_SMALL = {"ssd_conv_w": 256, "ffn_conv_w": 1408}


def _pad_lanes(v, n=128):
    return jnp.pad(v.astype(F32), (0, n - v.shape[0]))[None]


_CHIP2_PARTS = [(1284, 1536), (1792, 1800), (1536, 1792), (IN_MAIN, IN_MAIN + 126)]


def _w_in_chip_cols(gp):
    q = IN_W // 4
    return [gp[:, :q], gp[:, q:2 * q], jnp.concatenate([gp[:, a:b] for a, b in _CHIP2_PARTS], axis=1), gp[:, IN_WP - q:]]


def _ffn_block_perm(a):
    n = a.shape[-1] // 4
    return jnp.concatenate([a[..., j * n:(j + 1) * n] for j in FFN_BLOCK_ORDER], axis=-1)


def _layer_forward(i, x3, modv, wts, sp, cs3, sn3):
    sh1, sc1, g1, sh2, sc2, g2 = modv
    big = lambda n: wts[n]() if callable(wts[n]) else wts[n]
    h1 = norm_mod_forward(f"l{i}_norm1", x3, wts["norm1_g"], sc1, sh1)
    proj3 = mm(f"l{i}_proj", h1[0], big("w_in")[:, :IN_MAIN], "nn")[None]
    qkv3 = mm(f"l{i}_qkv", h1[0], big("w_in")[:, IN_MAIN:], "nn")[None]
    y_ssd, sv_ssd = ssd_forward(f"l{i}_ssd", proj3, sp)
    y_pool, sv_pool = pool_forward(f"l{i}_pool", proj3, wts["wbd"], wts["pool_scale"])
    y_att, res_att = attention_forward(f"l{i}", qkv3, cs3, sn3)
    mix = jnp.concatenate([y_ssd, y_pool, y_att], axis=-1)
    out, x1 = mm(f"l{i}_wout", mix[0], big("w_out"), "nn", res=x3[0], gate=g1)
    x1 = x1[None]
    h2 = norm_mod_forward(f"l{i}_norm2", x1, wts["norm2_g"], sc2, sh2)
    up3 = mm(f"l{i}_up", h2[0], big("ffn_up"), "nn")[None]
    act, sv_ffn, dn, x2 = ffn_down_forward(f"l{i}_down", up3, wts["ffn_conv_w"], wts["ffn_conv_b"], big("ffn_down"), x1[0], g2)
    keep = dict(x=x3, h1=h1, proj3=proj3, qkv3=qkv3, sv_ssd=sv_ssd, sv_pool=sv_pool, res_att=res_att, mix=mix, out=out[None],
                x1=x1, h2=h2, up3=up3, act=act, sv_ffn=sv_ffn, dn=dn[None])
    return x2[None], keep


def _layer_backward(i, dx2, keep, modv, wts, sp, cs3, sn3, after=None):
    sh1, sc1, g1, sh2, sc2, g2 = modv
    k = keep
    big = lambda n: wts[n]() if callable(wts[n]) else wts[n]
    tell = lambda step, *a: after[step](*a) if after and step in after else None
    d_dn, d_g2 = gate_backward(f"l{i}_gate2_b", k["dn"], g2, dx2)
    d_act = mm(f"l{i}_down_bx", d_dn[0], big("ffn_down"), "nt")
    g_down = mm(f"l{i}_down_bw", k["act"][0], d_dn[0], "tn").reshape(4, FFN_DIM // 4, D_MODEL)
    (d_up,), dv_ffn = ffn_mid_backward(f"l{i}_ffn_b", k["up3"], wts["ffn_conv_w"], wts["ffn_conv_b"], k["sv_ffn"], d_act[None])
    tell("ffn_b")
    d_h2 = mm(f"l{i}_up_bx", d_up[0], big("ffn_up"), "nt")
    g_up = mm(f"l{i}_up_bw", k["h2"][0], d_up[0], "tn", tn=_FFN_CW,
              into=((4, D_MODEL, _FFN_CW), lambda r, c: ((c % 2) * 2 + c // 2, r, 0)))
    dx1, (d_n2, d_sc2, d_sh2) = norm_mod_backward(f"l{i}_norm2_b", k["x1"], wts["norm2_g"], sc2, sh2, d_h2[None], dx2)
    d_out, d_g1 = gate_backward(f"l{i}_gate1_b", k["out"], g1, dx1)
    d_mix = mm(f"l{i}_wout_bx", d_out[0], big("w_out"), "nt")[None]
    g_wout = mm(f"l{i}_wout_bw", k["mix"][0], d_out[0], "tn").reshape(4, D_MODEL // 4, D_MODEL)
    tell("wout_bw", g_wout, g_up, g_down)
    (dz, dxs, dbm, dcm, ddt), dv_ssd = ssd_backward(f"l{i}_ssd_b", k["proj3"], sp, k["sv_ssd"], d_mix)
    tell("ssd_b")
    (du_pool,), (d_wbd, d_pscale) = pool_backward(f"l{i}_pool_b", k["proj3"], wts["wbd"], wts["pool_scale"], k["sv_pool"], d_mix)
    d_qkv = attention_backward(f"l{i}", k["qkv3"], cs3, sn3, k["res_att"], d_mix)
    d_proj = jnp.concatenate([dz[0], dxs[0], dbm[0], dcm[0], du_pool[0], (ddt[0] + ddt[1]).astype(BF16), d_qkv[0]], axis=-1)
    g_win = jnp.stack(_w_in_chip_cols(mm(f"l{i}_proj_bw", k["h1"][0], d_proj, "tn")))
    tell("proj_bw", g_win)
    d_h1 = mm(f"l{i}_proj_bx", d_proj, big("w_in"), "nt")
    tell("proj_bx")
    dx, (d_n1, d_sc1, d_sh1) = norm_mod_backward(f"l{i}_norm1_b", k["x"], wts["norm1_g"], sc1, sh1, d_h1[None], dx1)
    dcwx, dcbx, dcwb, dcbb, dcwc, dcbc, ddtb, dalog, ddsk, dng = dv_ssd
    small = dict(
        norm1_g=d_n1[0], norm2_g=d_n2[0],
        ssd_conv_w=jnp.concatenate([dcwx[:, :512], dcwb[:, 512:768], dcwc[:, 768:]], axis=1),
        ssd_conv_b=jnp.concatenate([dcbx[0, :512], dcbb[0, 512:768], dcbc[0, 768:]]),
        ssd_dt_bias=ddtb[0, :8], ssd_a_log=dalog[0, :8], ssd_d=ddsk[0, :8], ssd_norm_g=dng[0],
        pool_w=jnp.stack([d_wbd[64 * g:64 * g + 64, 64 * g:64 * g + 64] for g in range(4)]), pool_scale=d_pscale[0],
        ffn_conv_w=_ffn_block_perm(dv_ffn[0]), ffn_conv_b=_ffn_block_perm(dv_ffn[1][0]),
    )
    dmod = jnp.concatenate([d_sh1[0], d_sc1[0], d_g1[0], d_sh2[0], d_sc2[0], d_g2[0]])
    return dx, [g_win, g_wout, g_up, g_down], small, dmod


def kernel(x, c, positions, ada_w, ada_b, norm1_g, w_in, ssd_conv_w, ssd_conv_b, ssd_dt_bias, ssd_a_log, ssd_d, ssd_norm_g, pool_w, pool_scale, w_out, norm2_g, ffn_up, ffn_conv_w, ffn_conv_b, ffn_down, final_g, loss_target, m_ada_w, m_ada_b, m_norm1_g, m_w_in, m_ssd_conv_w, m_ssd_conv_b, m_ssd_dt_bias, m_ssd_a_log, m_ssd_d, m_ssd_norm_g, m_pool_w, m_pool_scale, m_w_out, m_norm2_g, m_ffn_up, m_ffn_conv_w, m_ffn_conv_b, m_ffn_down, m_final_g, v_ada_w, v_ada_b, v_norm1_g, v_w_in, v_ssd_conv_w, v_ssd_conv_b, v_ssd_dt_bias, v_ssd_a_log, v_ssd_d, v_ssd_norm_g, v_pool_w, v_pool_scale, v_w_out, v_norm2_g, v_ffn_up, v_ffn_conv_w, v_ffn_conv_b, v_ffn_down, v_final_g):
    args = dict(locals())
    w = {n: args[n] for n in _WEIGHTS}
    m = {n: args["m_" + n] for n in _WEIGHTS}
    v = {n: args["v_" + n] for n in _WEIGHTS}
    d = D_MODEL
    me = (lax.axis_index("x"), lax.axis_index("y"), lax.axis_index("c"))
    chip, dev = _chip(me), _dev(me)
    RIDERS.reset()

    shapes0 = [c.shape, ssd_conv_w.shape, ffn_conv_w.shape]
    g0 = allgather8("gather_c_conv", _pack([c, ssd_conv_w, ffn_conv_w]))
    c16 = jnp.pad(g0[:, :d // 128, :].reshape(8, d), ((0, 8), (0, 0)))
    by_chip = [_unpack(g0[2 * j], shapes0) for j in range(4)]
    conv_w_full = jnp.concatenate([p[1] for p in by_chip], axis=-1)
    fconv_w_full = jnp.concatenate([p[2] for p in by_chip], axis=-1)

    modp = ada_forward("ada_fwd", c16, ada_w)[:, :8]
    g1 = allgather8("gather_mod", _pack([modp]))
    modfull = jnp.concatenate([_unpack(g1[2 * j], [modp.shape])[0] for j in range(4)], axis=-1)
    mod = lax.dynamic_index_in_dim(modfull, dev, axis=1, keepdims=False) + ada_b
    modv = [[mod[i, q * d:(q + 1) * d][None] for q in range(6)] for i in range(DEPTH)]

    shards = [w[n].astype(BF16) for n in _BIG]

    def weight(k, layer, got):
        full = lax.dynamic_update_slice(got, shards[k][layer][None], (chip, 0, 0))
        if k == 0:
            return _w_in_from_chips(full)
        if k == 2:
            return jnp.concatenate([full[j] for j in FFN_BLOCK_ORDER], axis=1)
        return full.reshape(-1, full.shape[2])

    def later(k, layer, *sources):
        made = []

        def get():
            if not made:
                got = [RIDERS.result(host)[pos] for host, pos in sources]
                made.append(weight(k, layer, got[0] if len(got) == 1 else jnp.concatenate(got, axis=1)))
            return made[0]
        return get

    cs3, sn3 = rope_tables(positions[0])
    eye4 = jnp.eye(4, dtype=F32)
    wts, sps = [], []
    for i in range(DEPTH):
        wts.append(dict(
            norm1_g=norm1_g[i][None], norm2_g=norm2_g[i][None], pool_scale=pool_scale[i][None],
            wbd=(eye4[:, None, :, None] * pool_w[i][:, :, None, :]).reshape(POOL_W, POOL_W),
            ffn_conv_w=_ffn_block_perm(fconv_w_full[i]), ffn_conv_b=_ffn_block_perm(ffn_conv_b[i])[None]))
        sps.append(dict(cw=conv_w_full[i], cb=ssd_conv_b[i][None], dtb=_pad_lanes(ssd_dt_bias[i]), alog=_pad_lanes(ssd_a_log[i]),
                        dsk=_pad_lanes(ssd_d[i]), ng=ssd_norm_g[i][None]))

    (w_in0,) = ride_alone("gather_w_in0", gather_ride(0, [shards[0]]))
    RIDERS.book("l0_ssd", gather_ride(0, [shards[1], shards[3]]))
    half = shards[2].shape[1] // 2
    RIDERS.book("l0_attn0", gather_ride(0, [shards[2][:, :half]]))
    RIDERS.book("l0_attn1", gather_ride(0, [shards[2][:, half:]]))
    wts[0].update(w_in=weight(0, 0, w_in0), w_out=later(1, 0, ("l0_ssd", 0)), ffn_down=later(3, 0, ("l0_ssd", 1)),
                  ffn_up=later(2, 0, ("l0_attn0", 0), ("l0_attn1", 0)))
    RIDERS.book("l0_attn2", gather_ride(1, [shards[0], shards[1]]))
    RIDERS.book("l0_up", gather_ride(1, [shards[3]]))
    RIDERS.book("l0_down", gather_ride(1, [shards[2]]))
    wts[1].update(w_in=later(0, 1, ("l0_attn2", 0)), w_out=later(1, 1, ("l0_attn2", 1)), ffn_up=later(2, 1, ("l0_down", 0)),
                  ffn_down=later(3, 1, ("l0_up", 0)))
    x1_, keep0 = _layer_forward(0, x, modv[0], wts[0], sps[0], cs3, sn3)
    xc, keep1 = _layer_forward(1, x1_, modv[1], wts[1], sps[1], cs3, sn3)
    keeps = [keep0, keep1]
    lossblk, dx, d_final = final_loss("final_loss", xc, loss_target, final_g[None])
    loss = lax.psum(lossblk[0, 0], ("x", "y", "c"))

    small_g, dmods = [None] * DEPTH, [None] * DEPTH
    part_sum, from_chips = [[None] * 4 for _ in range(DEPTH)], [[None] * 4 for _ in range(DEPTH)]

    def owner_sum(layer, ks, mine, theirs):
        for k, g, t in zip(ks, mine, theirs):
            part_sum[layer][k] = add_arrays(f"sum_cores{layer}_{_BIG[k]}", [g, t], BF16)

    dx, by_chip1, small_g[1], dmods[1] = _layer_backward(1, dx, keeps[1], modv[1], wts[1], sps[1], cs3, sn3)
    RIDERS.book("l0_ffn_b", to_owner_ride(1, by_chip1))

    def after_ffn_b():
        owner_sum(1, range(4), by_chip1, RIDERS.result("l0_ffn_b"))
        RIDERS.book("l0_up_bx", scatter_ride(1, [part_sum[1][2]]))
        RIDERS.book("l0_up_bw", scatter_ride(1, [part_sum[1][0], part_sum[1][1]]))
        RIDERS.book("l0_norm2_b", scatter_ride(1, [part_sum[1][3]]))

    early = []

    def after_wout_bw(g_wout, g_up, g_down):
        early.extend([g_wout, g_up, g_down])
        RIDERS.book("l0_ssd_b", to_owner_ride(0, early))

    def after_ssd_b():
        owner_sum(0, [1, 2, 3], early, RIDERS.result("l0_ssd_b"))
        for host, k in (("l0_attn0_b", 2), ("l0_attn1_b", 3), ("l0_attn2_b", 1)):
            RIDERS.book(host, scatter_ride(0, [part_sum[0][k]]))

    last = []

    def after_proj_bw(g_win):
        last.append(g_win)
        RIDERS.book("l0_proj_bx", to_owner_ride(0, last))

    def after_proj_bx():
        owner_sum(0, [0], last, RIDERS.result("l0_proj_bx"))
        RIDERS.book("l0_norm1_b", scatter_ride(0, [part_sum[0][0]]))

    hooks = dict(ffn_b=after_ffn_b, wout_bw=after_wout_bw, ssd_b=after_ssd_b, proj_bw=after_proj_bw, proj_bx=after_proj_bx)
    dx, _, small_g[0], dmods[0] = _layer_backward(0, dx, keeps[0], modv[0], wts[0], sps[0], cs3, sn3, after=hooks)
    from_chips[1][2], (from_chips[1][0], from_chips[1][1]) = RIDERS.result("l0_up_bx")[0], RIDERS.result("l0_up_bw")
    from_chips[1][3] = RIDERS.result("l0_norm2_b")[0]
    for host, k in (("l0_attn0_b", 2), ("l0_attn1_b", 3), ("l0_attn2_b", 1), ("l0_norm1_b", 0)):
        from_chips[0][k] = RIDERS.result(host)[0]
    mine = [sum_chips_mine(f"sum_chips_{n}", part_sum[0][k], from_chips[0][k], part_sum[1][k], from_chips[1][k])
            for k, n in enumerate(_BIG)]
    reduced = swap_layers("swap_r", mine, me[2])
    grads = {n: jnp.stack(r) for n, r in zip(_BIG, reduced)}

    part = dict(ada_b=jnp.stack(dmods), final_g=d_final[0])
    for n in _SMALL:
        if n not in part:
            part[n] = jnp.stack([small_g[i][n] for i in range(DEPTH)])
    full_shapes = [part[n].shape for n in _SMALL]
    gs = allgather8("gather_small", _pack([part[n] for n in _SMALL]))
    tot = _unpack(sum_slots("sum_small", gs, 8)[0], full_shapes)
    small_tot = dict(zip(_SMALL, tot))
    dmod_all = gs[:, :DEPTH * 6 * d // 128, :].reshape(8, DEPTH, 6 * d)
    for n, ncol in _COL_SHARDED_SMALL.items():
        small_tot[n] = lax.dynamic_slice_in_dim(small_tot[n], chip * ncol, ncol, axis=2)
    grads.update(small_tot)

    ncol = ada_w.shape[2]
    dm = lax.dynamic_slice_in_dim(dmod_all, chip * ncol, ncol, axis=2).transpose(1, 0, 2)
    upd = {}
    g_ada, *upd["ada_w"] = ada_backward("ada_bwd", c16, jnp.pad(dm, ((0, 0), (0, 8), (0, 0))), ada_w, m["ada_w"], v["ada_w"])
    grads["ada_w"] = g_ada

    for n in _BIG:
        upd[n] = adamw(f"adam_{n}", w[n], grads[n], m[n], v[n])
    shapes_s = [w[n].shape for n in _SMALL]
    packed = [_pack([src[n] for n in _SMALL]) for src in (w, grads, m, v)]
    outs_s = [_unpack(o, shapes_s) for o in adamw("adam_small", *packed)]
    for q, n in enumerate(_SMALL):
        upd[n] = [outs_s[0][q], outs_s[1][q], outs_s[2][q]]

    return (loss, dx, *[grads[n] for n in _WEIGHTS], *[upd[n][0] for n in _WEIGHTS], *[upd[n][1] for n in _WEIGHTS],
            *[upd[n][2] for n in _WEIGHTS])


def ride_alone(name, ride):
    ni, no = len(ride.ins), len(ride.out_shapes)

    def body(*refs):
        ride.begin(refs[:ni], refs[ni:ni + no], refs[ni + no:])
        ride.end(refs[:ni], refs[ni:ni + no], refs[ni + no:])

    in_specs, out_specs, scratch = ride.specs()
    return list(pl.pallas_call(body, name=name, in_specs=in_specs, out_specs=out_specs, out_shape=ride.out_shapes,
                               scratch_shapes=scratch)(*ride.ins))


def mm(name, a, b, mode, out_dtype=F32, res=None, gate=None, tm=1408, tn=1536, tk=1408, into=None):
    ride = RIDERS.take(name)
    if mode == "nn":
        (m, k), n = a.shape, b.shape[1]
    elif mode == "nt":
        (m, k), n = a.shape, b.shape[0]
    else:
        (k, m), n = a.shape, b.shape[1]
    tm, tn, tk = _tile(m, tm), _tile(n, tn), _tile(k, tk)
    ni, nj, nk = m // tm, n // tn, k // tk
    a_spec = pl.BlockSpec((tk, tm), lambda i, j, q: (q, i)) if mode == "tn" else pl.BlockSpec((tm, tk), lambda i, j, q: (i, q))
    b_spec = pl.BlockSpec((tn, tk), lambda i, j, q: (j, q)) if mode == "nt" else pl.BlockSpec((tk, tn), lambda i, j, q: (q, j))
    o_spec = pl.BlockSpec((tm, tn), lambda i, j, q: (i, j))
    fused = res is not None
    lead = 0 if into is None else len(into[0]) - 2
    first = (0,) * lead + (slice(None), slice(None))
    ins, in_specs = [a, b], [a_spec, b_spec]
    out_shape, out_specs = [jax.ShapeDtypeStruct((m, n), out_dtype)], [o_spec]
    if fused:
        ins += [res, gate]
        in_specs += [o_spec, pl.BlockSpec((1, tn), lambda i, j, q: (0, j))]
        out_shape.append(jax.ShapeDtypeStruct((m, n), F32))
        out_specs.append(o_spec)
    if into is not None:
        shape, omap = into
        out_shape = [jax.ShapeDtypeStruct(shape, out_dtype)]
        out_specs = [pl.BlockSpec((1,) * lead + (tm, tn), lambda i, j, q: omap(i, j))]
    n_in, n_out = len(ins), len(out_shape)
    scratch = [pltpu.VMEM((tm, tn), F32)]
    if ride is not None:
        r_in, r_out, r_scr = ride.specs()
        ins, in_specs = ins + list(ride.ins), in_specs + r_in
        out_shape, out_specs = out_shape + list(ride.out_shapes), out_specs + r_out
        scratch = scratch + r_scr

    def body(*refs):
        a_ref, b_ref = refs[:2]
        o_ref = refs[len(ins)]
        acc = refs[len(ins) + len(out_shape)]
        i, j, q = pl.program_id(0), pl.program_id(1), pl.program_id(2)
        at = lambda x, y, z: jnp.logical_and(jnp.logical_and(i == x, j == y), q == z)
        r_refs = (refs[n_in:len(ins)], refs[len(ins) + n_out:len(ins) + len(out_shape)], refs[len(ins) + len(out_shape) + 1:])
        if ride is not None:
            ride.begin(*r_refs, at(0, 0, 0))

        @pl.when(q == 0)
        def _():
            acc[...] = jnp.zeros(acc.shape, F32)

        acc[...] += _mxu(a_ref[...], b_ref[...], mode)

        @pl.when(q == nk - 1)
        def _():
            o_ref[first] = acc[...].astype(o_ref.dtype)
            if fused:
                refs[len(ins) + 1][...] = refs[2][...] + refs[3][...] * acc[...]

        if ride is not None:
            ride.end(*r_refs, at(ni - 1, nj - 1, nk - 1))

    sem = ("arbitrary",) * 3 if ride is not None else ("parallel", "parallel", "arbitrary")
    out = pl.pallas_call(
        body, name=name, grid=(ni, nj, nk), in_specs=in_specs, out_specs=out_specs, out_shape=out_shape, scratch_shapes=scratch,
        compiler_params=pltpu.CompilerParams(dimension_semantics=sem, vmem_limit_bytes=VMEM_LIMIT_BYTES),
    )(*ins)
    if ride is not None:
        RIDERS.done[name] = list(out[n_out:])
    return tuple(out[:n_out]) if fused else out[0]


def add_arrays(name, arrs, out_dtype=F32):
    nb, r, c = arrs[0].shape
    t = _tile(r, 256, 8)
    (out,), _ = scan_fwd(name, _sum_fn, nb=nb, nchunk=r // t, t=t, rows=[Row(a, fb=lambda b: b) for a in arrs], vecs=[], carries=[],
                         outs=[out_row((nb, r, c), out_dtype, fb=lambda b: b)], save=False)
    return out


def _sum_chips_mine_fn(ci, b, carries, rows, vecs):
    mine_layer = lax.axis_index("c")
    chip = 2 * lax.axis_index("x") + lax.axis_index("y")
    tot = None
    for j in range(4):
        own = jnp.where(mine_layer == 0, rows[j], rows[8 + j])
        sent = jnp.where(mine_layer == 0, rows[4 + j], rows[12 + j])
        term = jnp.where(chip == j, own, sent)
        tot = term if tot is None else tot + term
    return [], [tot]


def sum_chips_mine(name, p0, q0, p1, q1):
    _, r, c = p0.shape
    t = _tile(r, 256, 8)
    rows = [Row(a, fb=(lambda b, j=j: j)) for a in (p0, q0, p1, q1) for j in range(4)]
    (out,), _ = scan_fwd(name, _sum_chips_mine_fn, nb=1, nchunk=r // t, t=t, rows=rows, vecs=[], carries=[],
                         outs=[out_row((1, r, c))], save=False)
    return out[0]


def _remote(src, dst, send_sems, recv_sems, k, to):
    return pltpu.make_async_remote_copy(src_ref=src, dst_ref=dst, send_sem=send_sems.at[k], recv_sem=recv_sems.at[k],
                                        device_id=to, device_id_type=MESH)


def gather_ride(layer, shards):
    na = len(shards)

    def start(ins, outs, ss, rs, me):
        @pl.when(me[2] == layer)
        def _():
            for k in range(na):
                for p, mask in enumerate(CHIP_PEERS):
                    _remote(ins[k].at[layer], outs[k].at[_chip(me)], ss, rs, 6 * k + p, _flip(mask, me)).start()

    def finish(ins, outs, ss, rs, me):
        sibling = _flip(SIBLING[0], me)

        @pl.when(me[2] == layer)
        def _():
            for k in range(na):
                for p, mask in enumerate(CHIP_PEERS):
                    slot = outs[k].at[_chip(_flip(mask, me))]
                    _remote(ins[k].at[layer], slot, ss, rs, 6 * k + p, _flip(mask, me)).wait_recv()
                    _remote(slot, slot, ss, rs, 6 * k + 3 + p, sibling).start()
            for k in range(na):
                for p, mask in enumerate(CHIP_PEERS):
                    slot = outs[k].at[_chip(_flip(mask, me))]
                    _remote(ins[k].at[layer], slot, ss, rs, 6 * k + p, _flip(mask, me)).wait_send()
                    _remote(slot, slot, ss, rs, 6 * k + 3 + p, sibling).wait_send()

        @pl.when(me[2] != layer)
        def _():
            for k in range(na):
                for p, mask in enumerate(CHIP_PEERS):
                    slot = outs[k].at[_chip(_flip(mask, me))]
                    _remote(slot, slot, ss, rs, 6 * k + 3 + p, sibling).wait_recv()

    return Ride(list(shards), [jax.ShapeDtypeStruct((4,) + a.shape[1:], a.dtype) for a in shards], 6 * na, start, finish)


def scatter_ride(layer, parts):
    na = len(parts)

    def start(ins, outs, ss, rs, me):
        @pl.when(me[2] == layer)
        def _():
            for k in range(na):
                for p, mask in enumerate(CHIP_PEERS):
                    peer = _flip(mask, me)
                    _remote(ins[k].at[_chip(peer)], outs[k].at[_chip(me)], ss, rs, 3 * k + p, peer).start()

    def finish(ins, outs, ss, rs, me):
        @pl.when(me[2] == layer)
        def _():
            for k in range(na):
                for p, mask in enumerate(CHIP_PEERS):
                    peer = _flip(mask, me)
                    _remote(ins[k].at[_chip(peer)], outs[k].at[_chip(peer)], ss, rs, 3 * k + p, peer).wait_recv()
                    _remote(ins[k].at[_chip(peer)], outs[k].at[_chip(me)], ss, rs, 3 * k + p, peer).wait_send()

    return Ride(list(parts), [jax.ShapeDtypeStruct(a.shape, a.dtype) for a in parts], 3 * na, start, finish)


def to_owner_ride(layer, arrays):
    na = len(arrays)

    def start(ins, outs, ss, rs, me):
        @pl.when(me[2] != layer)
        def _():
            for k in range(na):
                _remote(ins[k], outs[k], ss, rs, k, _flip(SIBLING[0], me)).start()

    def finish(ins, outs, ss, rs, me):
        for k in range(na):
            cp = _remote(ins[k], outs[k], ss, rs, k, _flip(SIBLING[0], me))
            pl.when(me[2] != layer)(cp.wait_send)
            pl.when(me[2] == layer)(cp.wait_recv)

    return Ride(list(arrays), [jax.ShapeDtypeStruct(a.shape, a.dtype) for a in arrays], na, start, finish)


def _w_in_from_chips(a):
    c2 = a[2]
    pad = jnp.zeros((c2.shape[0], IN_WP - IN_W), c2.dtype)
    return jnp.concatenate([a[0], a[1], c2[:, :252], c2[:, 260:516], c2[:, 252:260], pad, c2[:, 516:], a[3]], axis=1)
```

```python
import functools

import numpy as np
import jax
import jax.numpy as jnp
from jax import lax
from jax.experimental import pallas as pl
from jax.experimental.pallas import tpu as pltpu

F32 = jnp.float32
BF16 = jnp.bfloat16
MESH = pl.DeviceIdType.MESH

D_MODEL = 1024
DEPTH = 2
SSD_INNER = 512
POOL_W = 256
POOL_WINDOWS = (2, 4, 8, 16)
ATT_W = 256
ATT_HEADS = 4
ATT_HEAD_DIM = 64
ATT_PATTERNS = ((128, 1), (512, 4), (2048, 16))
ATT_BLOCK = 128
ROT_DIM = 16
ROPE_THETA = 500000.0
IN_W = 2568
IN_WP = 2688
IN_MAIN = 1920
FFN_DIM = 2816
NORM_EPS = 1e-6
ADAM_LR, ADAM_B1, ADAM_B2, ADAM_EPS, ADAM_WD, ADAM_STEP = 0.001, 0.9, 0.999, 1e-08, 0.01, 10

VMEM_LIMIT_BYTES = 56 * 1024 * 1024
NEG = -1e30


def _mxu(a, b, mode):
    dims = {"nn": ((1,), (0,)), "nt": ((1,), (1,)), "tn": ((0,), (0,))}[mode]
    return lax.dot_general(a.astype(BF16), b.astype(BF16), (dims, ((), ())), preferred_element_type=F32)


@functools.partial(jax.custom_vjp, nondiff_argnums=(2,))
def _bdot(a, b, mode):
    return _mxu(a, b, mode)


def _bdot_fwd(a, b, mode):
    return _mxu(a, b, mode), (a, b)


def _bdot_bwd(mode, res, g):
    a, b = res
    if mode == "nn":
        return _mxu(g, b, "nt"), _mxu(a, g, "tn")
    if mode == "nt":
        return _mxu(g, b, "nn"), _mxu(g, a, "tn")
    return _mxu(b, g, "nt"), _mxu(a, g, "nn")


_bdot.defvjp(_bdot_fwd, _bdot_bwd)


def _iota(shape, dim):
    return lax.broadcasted_iota(jnp.int32, shape, dim)


def _make_shift(h):
    @functools.partial(jax.custom_vjp, nondiff_argnums=(2,))
    def shift(halo, cur, k):
        if k == 0:
            return cur
        full = jnp.concatenate([halo, cur], axis=0)
        return pltpu.roll(full, k, 0)[h:]

    def fwd(halo, cur, k):
        return shift(halo, cur, k), None

    def bwd(k, _, g):
        t, w = g.shape
        if k == 0:
            return jnp.zeros((h, w), F32), g
        d_cur = jnp.where(_iota((t, w), 0) < t - k, pltpu.roll(g, t - k, 0), 0.0)
        top = g[:h]
        d_halo = jnp.where(_iota((h, w), 0) >= h - k, pltpu.roll(top, h - k, 0) if k < h else top, 0.0)
        return d_halo, d_cur

    shift.defvjp(fwd, bwd)
    return shift


_shift8 = _make_shift(8)
_shift16 = _make_shift(16)


def _make_tail(h):
    @jax.custom_vjp
    def tail(x):
        return x[x.shape[0] - h:]

    def fwd(x):
        return tail(x), x.shape[0]

    def bwd(t, g):
        return (jnp.concatenate([jnp.zeros((t - h, g.shape[1]), F32), g], axis=0),)

    tail.defvjp(fwd, bwd)
    return tail


_tail8 = _make_tail(8)
_tail16 = _make_tail(16)


@jax.custom_vjp
def _cumsum_rows(x):
    t = x.shape[0]
    row, s = _iota(x.shape, 0), 1
    while s < t:
        x = x + jnp.where(row >= s, pltpu.roll(x, s, 0), 0.0)
        s *= 2
    return x


def _cumsum_rows_fwd(x):
    return _cumsum_rows(x), None


def _cumsum_rows_bwd(_, g):
    t = g.shape[0]
    row, s = _iota(g.shape, 0), 1
    while s < t:
        g = g + jnp.where(row < t - s, pltpu.roll(g, t - s, 0), 0.0)
        s *= 2
    return (g,)


_cumsum_rows.defvjp(_cumsum_rows_fwd, _cumsum_rows_bwd)


@jax.custom_vjp
def _rot_pairs(t):
    e = _iota(t.shape, 1) % ATT_HEAD_DIM
    n = t.shape[1]
    return jnp.where(e < 8, -pltpu.roll(t, n - 8, 1), jnp.where(e < 16, pltpu.roll(t, 8, 1), 0.0))


def _rot_pairs_fwd(t):
    return _rot_pairs(t), None


def _rot_pairs_bwd(_, g):
    e = _iota(g.shape, 1) % ATT_HEAD_DIM
    n = g.shape[1]
    return (pltpu.roll(jnp.where(e < 8, -g, 0.0), 8, 1) + pltpu.roll(jnp.where(jnp.logical_and(e >= 8, e < 16), g, 0.0), n - 8, 1),)


_rot_pairs.defvjp(_rot_pairs_fwd, _rot_pairs_bwd)


def _make_thirds():
    @jax.custom_vjp
    def thirds(x):
        w = x.shape[1] // 3
        return x[:, :w], x[:, w:2 * w], x[:, 2 * w:]

    def fwd(x):
        return thirds(x), None

    def bwd(_, g):
        return (jnp.concatenate(g, axis=1),)

    thirds.defvjp(fwd, bwd)
    return thirds


_thirds = _make_thirds()


def _rowk(w, k):
    return jnp.sum(jnp.where(_iota(w.shape, 0) == k, w, 0.0), axis=0, keepdims=True)


def _silu(x):
    return x * (0.5 * jnp.tanh(0.5 * x) + 0.5)


def _softplus(x):
    return jnp.maximum(x, 0.0) + jnp.log(1.0 + jnp.exp(-jnp.abs(x)))


def _tile(dim, target, unit=128):
    if dim <= target:
        return dim
    best = None
    for t in range(unit, target + 1, unit):
        if dim % t == 0:
            best = t
    assert best is not None, (dim, target)
    return best


class Ride:
    def __init__(self, ins, out_shapes, nsem, start, finish):
        self.ins, self.out_shapes, self.nsem, self.start, self.finish = ins, out_shapes, nsem, start, finish

    def specs(self):
        hbm = pl.BlockSpec(memory_space=pl.ANY)
        return [hbm] * len(self.ins), [hbm] * len(self.out_shapes), [pltpu.SemaphoreType.DMA((self.nsem,))] * 2

    def begin(self, in_refs, out_refs, sems, cond=None):
        me = (lax.axis_index("x"), lax.axis_index("y"), lax.axis_index("c"))
        go = lambda: self.start(in_refs, out_refs, sems[0], sems[1], me)
        go() if cond is None else pl.when(cond)(go)

    def end(self, in_refs, out_refs, sems, cond=None):
        me = (lax.axis_index("x"), lax.axis_index("y"), lax.axis_index("c"))
        go = lambda: self.finish(in_refs, out_refs, sems[0], sems[1], me)
        go() if cond is None else pl.when(cond)(go)


class _Riders:
    def reset(self):
        self.booked, self.done = {}, {}

    def book(self, host, ride):
        assert host not in self.booked, host
        self.booked[host] = ride

    def take(self, host):
        return self.booked.pop(host, None)

    def result(self, host):
        return self.done[host]


RIDERS = _Riders()
RIDERS.reset()


class Row:
    def __init__(self, arr, w=None, fb=None, fc=None, diff=True, slot=False, dcols=None, dfc=None, ddtype=F32, view=None):
        self.ddtype = ddtype
        self.view = view
        self.arr = arr
        self.w = arr.shape[2] if w is None else w
        self.fb = (lambda b: 0) if fb is None else fb
        self.fc = (lambda b: 0) if fc is None else fc
        self.diff = diff
        self.slot = slot
        self.dcols = dcols
        self.dfc = dfc


class Vec:
    def __init__(self, arr, w=None, fc=None, diff=True):
        self.arr = arr
        self.w = arr.shape[1] if w is None else w
        self.fc = fc
        self.diff = diff


def _row_spec(r, t, nchunk, reverse):
    shape = (1, t, r.w) if r.view is None else (1, t // r.view, r.view * r.w)
    if reverse:
        return pl.BlockSpec(shape, lambda b, i, r=r: (r.fb(b), nchunk - 1 - i, r.fc(b)))
    return pl.BlockSpec(shape, lambda b, i, r=r: (r.fb(b), i, r.fc(b)))


def _load_row(ref, r, t, scr):
    if r.view is None:
        return ref[0]
    d, w = r.view, r.w
    for q in range(d):
        for j in range(w // 128):
            scr[j, pl.ds(q, t // d, stride=d), :] = ref[0, :, q * w + 128 * j:q * w + 128 * (j + 1)].astype(F32)
    return jnp.concatenate([scr[j] for j in range(w // 128)], axis=1)


def _store_row(ref, r, t, scr, val):
    if r.view is None:
        ref[0] = val.astype(ref.dtype)
        return
    d, w = r.view, r.w
    for j in range(w // 128):
        scr[j] = val[:, 128 * j:128 * (j + 1)]
    for q in range(d):
        for j in range(w // 128):
            ref[0, :, q * w + 128 * j:q * w + 128 * (j + 1)] = scr[j, pl.ds(q, t // d, stride=d), :].astype(ref.dtype)


def _view_scratch(specs, t):
    ws = [r.w for r in specs if r.view is not None]
    return [pltpu.VMEM((max(ws) // 128, t, 128), F32)] if ws else []


def _vec_spec(v):
    if v.fc is None:
        return pl.BlockSpec(v.arr.shape, lambda b, i: (0, 0))
    return pl.BlockSpec((v.arr.shape[0], v.w), lambda b, i, v=v: (0, v.fc(b)))


def _cparams():
    return pltpu.CompilerParams(dimension_semantics=("arbitrary", "arbitrary"), vmem_limit_bytes=VMEM_LIMIT_BYTES)


def scan_fwd(name, fn, *, nb, nchunk, t, rows, vecs, carries, outs, save):
    nr, nv, nc, no = len(rows), len(vecs), len(carries), len(outs)
    ns = nc if save else 0
    ride = RIDERS.take(name)
    r_in, r_out, r_scr = ride.specs() if ride else ([], [], [])

    def body(*refs):
        p = 0
        row_refs = refs[p:p + nr]; p += nr
        vec_refs = refs[p:p + nv]; p += nv
        ride_in = refs[p:p + len(r_in)]; p += len(r_in)
        out_refs = refs[p:p + no]; p += no
        save_refs = refs[p:p + ns]; p += ns
        ride_out = refs[p:p + len(r_out)]; p += len(r_out)
        car = refs[p:p + nc]; p += nc
        scr = refs[p] if stage else None
        sems = refs[p + len(stage):]
        b, i = pl.program_id(0), pl.program_id(1)
        if ride:
            ride.begin(ride_in, ride_out, sems, jnp.logical_and(b == 0, i == 0))
        if nc:
            @pl.when(i == 0)
            def _():
                for c_ref in car:
                    c_ref[...] = jnp.zeros(c_ref.shape, F32)
        cin = [c_ref[...] for c_ref in car]
        if save:
            for s_ref, cv in zip(save_refs, cin):
                s_ref[0, 0] = cv
        new_c, o = fn(i, b, cin, [_load_row(ref, r, t, scr) for ref, r in zip(row_refs, rows)], [v[...] for v in vec_refs])
        for c_ref, cv in zip(car, new_c):
            c_ref[...] = cv
        for o_ref, spec, ov in zip(out_refs, outs, o):
            _store_row(o_ref, spec, t, scr, ov)
        if ride:
            ride.end(ride_in, ride_out, sems, jnp.logical_and(b == nb - 1, i == nchunk - 1))

    stage = _view_scratch(list(rows) + list(outs), t)
    out_shape = [o.arr for o in outs]
    out_specs = [_row_spec(o, t, nchunk, False) for o in outs]
    if save:
        for cs in carries:
            out_shape.append(jax.ShapeDtypeStruct((nb, nchunk) + tuple(cs), F32))
            out_specs.append(pl.BlockSpec((1, 1) + tuple(cs), lambda b, i: (b, i, 0, 0)))
    res = pl.pallas_call(
        body, name=name, grid=(nb, nchunk),
        in_specs=[_row_spec(r, t, nchunk, False) for r in rows] + [_vec_spec(v) for v in vecs] + r_in,
        out_specs=out_specs + r_out, out_shape=out_shape + (list(ride.out_shapes) if ride else []),
        scratch_shapes=[pltpu.VMEM(tuple(cs), F32) for cs in carries] + stage + r_scr,
        compiler_params=_cparams(),
    )(*[r.arr for r in rows], *[v.arr for v in vecs], *(ride.ins if ride else []))
    if ride:
        RIDERS.done[name] = list(res[no + ns:])
    return list(res[:no]), list(res[no:no + ns])


def scan_bwd(name, fn, *, nb, nchunk, t, rows, vecs, carries, saved, douts, adds=None):
    adds = adds or {}
    nr, nv, nc, no = len(rows), len(vecs), len(carries), len(douts)
    dri = [k for k, r in enumerate(rows) if r.diff]
    dvi = [k for k, v in enumerate(vecs) if v.diff]
    add_keys = sorted(adds)
    na = len(add_keys)
    ride = RIDERS.take(name)
    r_in, r_out, r_scr = ride.specs() if ride else ([], [], [])

    def body(*refs):
        p = 0
        row_refs = refs[p:p + nr]; p += nr
        vec_refs = refs[p:p + nv]; p += nv
        save_refs = refs[p:p + nc]; p += nc
        dout_refs = refs[p:p + no]; p += no
        add_refs = refs[p:p + na]; p += na
        ride_in = refs[p:p + len(r_in)]; p += len(r_in)
        drow_refs = refs[p:p + len(dri)]; p += len(dri)
        dvec_refs = refs[p:p + len(dvi)]; p += len(dvi)
        ride_out = refs[p:p + len(r_out)]; p += len(r_out)
        dcar = refs[p:p + nc]; p += nc
        scr = refs[p] if stage else None
        sems = refs[p + len(stage):]
        b, ir = pl.program_id(0), pl.program_id(1)
        ci = nchunk - 1 - ir
        if ride:
            ride.begin(ride_in, ride_out, sems, jnp.logical_and(b == 0, ir == 0))
        if nc:
            @pl.when(ir == 0)
            def _():
                for c_ref in dcar:
                    c_ref[...] = jnp.zeros(c_ref.shape, F32)
        rows_v = [_load_row(ref, r, t, scr) for ref, r in zip(row_refs, rows)]
        vecs_v = [v[...] for v in vec_refs]
        cin = [s[0, 0] for s in save_refs]
        dc = [c_ref[...] for c_ref in dcar]
        dout_v = [_load_row(ref, r, t, scr).astype(F32) for ref, r in zip(dout_refs, douts)]

        def f(cs, dr, dv):
            rr, vv = list(rows_v), list(vecs_v)
            for k, idx in enumerate(dri):
                rr[idx] = dr[k]
            for k, idx in enumerate(dvi):
                vv[idx] = dv[k]
            return fn(ci, b, cs, rr, vv)

        _, vjp = jax.vjp(f, cin, [rows_v[k].astype(F32) for k in dri], [vecs_v[k].astype(F32) for k in dvi])
        dcin, drows, dvecs = vjp((dc, dout_v))
        for c_ref, cv in zip(dcar, dcin):
            c_ref[...] = cv
        for k, (o_ref, ov) in enumerate(zip(drow_refs, drows)):
            if dri[k] in adds:
                ov = ov + add_refs[add_keys.index(dri[k])][0].astype(F32)
            _store_row(o_ref, rows[dri[k]], t, scr, ov)
        for k, (o_ref, ov) in enumerate(zip(dvec_refs, dvecs)):
            first = (ir == 0) if vecs[dvi[k]].fc is not None else jnp.logical_and(ir == 0, b == 0)

            @pl.when(first)
            def _(o_ref=o_ref, ov=ov):
                o_ref[...] = ov

            @pl.when(jnp.logical_not(first))
            def _(o_ref=o_ref, ov=ov):
                o_ref[...] += ov

        if ride:
            ride.end(ride_in, ride_out, sems, jnp.logical_and(b == nb - 1, ir == nchunk - 1))

    stage = _view_scratch(list(rows) + list(douts), t)
    in_specs = ([_row_spec(r, t, nchunk, True) for r in rows] + [_vec_spec(v) for v in vecs]
                + [pl.BlockSpec((1, 1) + tuple(cs), lambda b, i: (b, nchunk - 1 - i, 0, 0)) for cs in carries]
                + [_row_spec(d, t, nchunk, True) for d in douts]
                + [_row_spec(adds[k], t, nchunk, True) for k in add_keys] + r_in)
    out_shape, out_specs = [], []
    for k in dri:
        r = rows[k]
        if r.slot:
            out_shape.append(jax.ShapeDtypeStruct((nb, r.arr.shape[1], r.w), r.ddtype))
            out_specs.append(pl.BlockSpec((1, t, r.w), lambda b, i: (b, nchunk - 1 - i, 0)))
        elif r.dcols is not None:
            out_shape.append(jax.ShapeDtypeStruct((r.arr.shape[0], r.arr.shape[1], r.dcols), r.ddtype))
            out_specs.append(pl.BlockSpec((1, t, r.w), lambda b, i, r=r: (r.fb(b), nchunk - 1 - i, r.dfc(b))))
        else:
            out_shape.append(jax.ShapeDtypeStruct(r.arr.shape, r.ddtype))
            out_specs.append(_row_spec(r, t, nchunk, True))
    for k in dvi:
        out_shape.append(jax.ShapeDtypeStruct(vecs[k].arr.shape, F32))
        out_specs.append(_vec_spec(vecs[k]))
    nd = len(dri) + len(dvi)
    res = pl.pallas_call(
        body, name=name, grid=(nb, nchunk), in_specs=in_specs, out_specs=out_specs + r_out,
        out_shape=out_shape + (list(ride.out_shapes) if ride else []),
        scratch_shapes=[pltpu.VMEM(tuple(cs), F32) for cs in carries] + stage + r_scr,
        compiler_params=_cparams(),
    )(*[r.arr for r in rows], *[v.arr for v in vecs], *saved, *[d.arr for d in douts], *[adds[k].arr for k in add_keys],
      *(ride.ins if ride else []))
    if ride:
        RIDERS.done[name] = list(res[nd:])
    return list(res[:len(dri)]), list(res[len(dri):nd])


def out_row(shape, dtype=F32, w=None, fb=None, fc=None):
    return Row(jax.ShapeDtypeStruct(shape, dtype), w, fb, fc)


def _conv(shift, halo, cur, w, bias, taps):
    y = bias
    for k in range(taps):
        y = y + _rowk(w, k) * shift(halo, cur, taps - 1 - k)
    return y


def _ssd_fn(ci, b, carries, rows, vecs):
    cx, cb_, cc, ht = carries
    z, xr, br, cr, dtr = rows
    cwx, cbx, cwb, cbb, cwc, cbc, dtb, alog, dsk, ng = vecs
    t = z.shape[0]
    xs = _silu(_conv(_shift8, cx, xr, cwx, cbx, 4))
    bm = _silu(_conv(_shift8, cb_, br, cwb, cbb, 4))
    cm = _silu(_conv(_shift8, cc, cr, cwc, cbc, 4))
    dt = _softplus(dtr + dtb)
    acol = _cumsum_rows(dt * (-jnp.exp(alog)))
    arow = acol.T
    r, c = _iota((t, t), 0), _iota((t, t), 1)
    causal = r >= c
    cbm = _bdot(cm, bm, "nt")
    lane, sub = _iota(acol.shape, 1), _iota(arow.shape, 0)
    colh = _iota(xs.shape, 1) // 64
    a, dtx, dx, acs = jnp.zeros(xs.shape, F32), jnp.zeros(xs.shape, F32), jnp.zeros((1, xs.shape[1]), F32), []
    for j in range(4):
        h = 4 * b + j
        ac = jnp.sum(jnp.where(lane == h, acol, 0.0), axis=1, keepdims=True)
        acs.append(ac)
        a = jnp.where(colh == j, ac, a)
        dtx = jnp.where(colh == j, jnp.sum(jnp.where(lane == h, dt, 0.0), axis=1, keepdims=True), dtx)
        dx = jnp.where(_iota(dx.shape, 1) // 64 == j, jnp.sum(jnp.where(_iota(dsk.shape, 1) == h, dsk, 0.0), axis=1, keepdims=True), dx)
    atot = jnp.sum(jnp.where(_iota(a.shape, 0) == t - 1, a, 0.0), axis=0, keepdims=True)
    x = xs * dtx
    ydiag = jnp.zeros(x.shape, F32)
    for j in range(4):
        ar = jnp.sum(jnp.where(sub == 4 * b + j, arow, 0.0), axis=0, keepdims=True)
        lmat = jnp.exp(jnp.where(causal, acs[j] - ar, NEG))
        ydiag = ydiag + _bdot(cbm * lmat, jnp.where(colh == j, x, 0.0), "nn")
    yoff = _bdot(cm, ht, "nn") * jnp.exp(a)
    ht_new = ht * jnp.exp(atot) + _bdot(bm, x * jnp.exp(atot - a), "tn")
    y = ydiag + yoff + dx * xs
    yz = y * _silu(z)
    yn = yz * lax.rsqrt(jnp.mean(yz * yz, axis=-1, keepdims=True) + NORM_EPS) * ng
    return [_tail8(xr), _tail8(br), _tail8(cr), ht_new], [yn]


_SSD_T = 256
_SSD_CARRIES = [(8, 256), (8, 128), (8, 128), (128, 256)]


def _ssd_io(proj3, p):
    own = lambda b: b
    rows = [Row(proj3, 256, fc=own, dcols=512, dfc=own, ddtype=BF16),
            Row(proj3, 256, fc=lambda b: 2 + b, dcols=512, dfc=own, ddtype=BF16),
            Row(proj3, 128, fc=lambda b: 8 + b, dcols=256, dfc=own, ddtype=BF16),
            Row(proj3, 128, fc=lambda b: 10 + b, dcols=256, dfc=own, ddtype=BF16),
            Row(proj3, 128, fc=lambda b: 14, slot=True)]
    vecs = [Vec(p["cw"], 256, lambda b: b), Vec(p["cb"], 256, lambda b: b),
            Vec(p["cw"], 128, lambda b: 4 + b), Vec(p["cb"], 128, lambda b: 4 + b),
            Vec(p["cw"], 128, lambda b: 6 + b), Vec(p["cb"], 128, lambda b: 6 + b),
            Vec(p["dtb"]), Vec(p["alog"]), Vec(p["dsk"]), Vec(p["ng"], 256, lambda b: b)]
    return rows, vecs


def ssd_forward(name, proj3, p):
    rows, vecs = _ssd_io(proj3, p)
    s = proj3.shape[1]
    (y,), saved = scan_fwd(name, _ssd_fn, nb=2, nchunk=s // _SSD_T, t=_SSD_T, rows=rows, vecs=vecs,
                           carries=_SSD_CARRIES, outs=[out_row((1, s, SSD_INNER), BF16, 256, fc=lambda b: b)], save=True)
    return y, saved


def ssd_backward(name, proj3, p, saved, dmix3):
    rows, vecs = _ssd_io(proj3, p)
    s = proj3.shape[1]
    drows, dvecs = scan_bwd(name, _ssd_fn, nb=2, nchunk=s // _SSD_T, t=_SSD_T, rows=rows, vecs=vecs,
                            carries=_SSD_CARRIES, saved=saved, douts=[Row(dmix3, 256, fc=lambda b: b)])
    return drows, dvecs


def _pool_fn(ci, b, carries, rows, vecs):
    (cu,) = carries
    (u,) = rows
    wbd, scale = vecs
    t = u.shape[0]
    pos = ci * t + _iota(u.shape, 0)
    grp = _iota(u.shape, 1) // 64
    acc, pooled, k = u, jnp.zeros(u.shape, F32), 1
    for gi, w in enumerate(POOL_WINDOWS):
        while k < w:
            acc = acc + _shift16(cu, u, k)
            k += 1
        pooled = jnp.where(grp == gi, acc / jnp.minimum(pos + 1, w).astype(F32), pooled)
    y = _bdot(pooled - u, wbd, "nn") * scale
    return [_tail16(u)], [y]


_POOL_T = 256


def _pool_io(proj3, wbd, scale):
    return [Row(proj3, 256, fc=lambda b: 6, dcols=256, dfc=lambda b: 0, ddtype=BF16)], [Vec(wbd), Vec(scale)]


def pool_forward(name, proj3, wbd, scale):
    rows, vecs = _pool_io(proj3, wbd, scale)
    s = proj3.shape[1]
    (y,), saved = scan_fwd(name, _pool_fn, nb=1, nchunk=s // _POOL_T, t=_POOL_T, rows=rows, vecs=vecs,
                           carries=[(16, 256)], outs=[out_row((1, s, POOL_W), BF16)], save=True)
    return y, saved


def pool_backward(name, proj3, wbd, scale, saved, dmix3):
    rows, vecs = _pool_io(proj3, wbd, scale)
    s = proj3.shape[1]
    return scan_bwd(name, _pool_fn, nb=1, nchunk=s // _POOL_T, t=_POOL_T, rows=rows, vecs=vecs,
                    carries=[(16, 256)], saved=saved, douts=[Row(dmix3, 256, fc=lambda b: 2)])


def _attn_fn(ci, b, carries, rows, vecs):
    kp, vp = carries
    qr, kr, v = _thirds(rows[0])
    scale = ATT_HEAD_DIM ** -0.5
    q = qr
    n = q.shape[0]
    r, c = _iota((n, n), 0), _iota((n, n), 1)
    prev_ok, cur_ok = jnp.logical_and(c >= r, ci > 0), r >= c
    head = _iota(q.shape, 1) // ATT_HEAD_DIM
    o, lse = jnp.zeros(q.shape, F32), jnp.zeros(q.shape, F32)
    for h in range(ATT_HEADS):
        mine = head == h
        qh = jnp.where(mine, qr, 0.0)
        sp = jnp.where(prev_ok, _bdot(qh, kp, "nt") * scale, NEG)
        sc = jnp.where(cur_ok, _bdot(qh, kr, "nt") * scale, NEG)
        m = lax.stop_gradient(jnp.maximum(jnp.max(sp, axis=1, keepdims=True), jnp.max(sc, axis=1, keepdims=True)))
        pp, pc = jnp.exp(sp - m), jnp.exp(sc - m)
        l = jnp.sum(pp, axis=1, keepdims=True) + jnp.sum(pc, axis=1, keepdims=True)
        o = jnp.where(mine, (_bdot(pp, vp, "nn") + _bdot(pc, v, "nn")) / l, o)
        lse = jnp.where(mine, m + jnp.log(l), lse)
    return [kr, v], [o, lse]


_ATT_CARRIES = [(ATT_BLOCK, ATT_W), (ATT_BLOCK, ATT_W)]


def attn_forward(name, pv, d):
    l = pv.shape[1]
    own = lambda b: b
    outs = [out_row((1, l, d * ATT_W), F32, ATT_W, fc=own) for _ in range(2)]
    (o, lse), saved = scan_fwd(name, _attn_fn, nb=d, nchunk=l // ATT_BLOCK, t=ATT_BLOCK, rows=[Row(pv, 3 * ATT_W, fc=own)],
                               vecs=[], carries=_ATT_CARRIES, outs=outs, save=True)
    return o, lse, saved


def attn_backward(name, pv, d, saved, do, dlse):
    l = pv.shape[1]
    own = lambda b: b
    (dpv,), _ = scan_bwd(name, _attn_fn, nb=d, nchunk=l // ATT_BLOCK, t=ATT_BLOCK, rows=[Row(pv, 3 * ATT_W, fc=own)], vecs=[],
                         carries=_ATT_CARRIES, saved=saved, douts=[Row(do, ATT_W, fc=own), Row(dlse, ATT_W, fc=own)])
    return dpv


def _rope_fn(ci, b, carries, rows, vecs):
    x, cs, sn = rows
    return [], [x * cs + _rot_pairs(x) * sn]


def _rope3_fn(ci, b, carries, rows, vecs):
    _, (y,) = _rope_fn(ci, b, carries, rows, vecs)
    return [], [y, y, y]


def _by_residue(a_or_shape, w, d):
    if isinstance(a_or_shape, tuple):
        _, s, _ = a_or_shape
        return Row(jax.ShapeDtypeStruct((1, s // d, d * w), F32), w, view=None if d == 1 else d)
    return Row(a_or_shape, w, view=None if d == 1 else d)


def rope_forward(name, qkv3, cs3, sn3):
    s, w = qkv3.shape[1], qkv3.shape[2]
    ys, _ = scan_fwd(name, _rope3_fn, nb=1, nchunk=s // _ROW_T, t=_ROW_T, vecs=[], carries=[], save=False,
                     rows=[Row(qkv3), Row(cs3, diff=False), Row(sn3, diff=False)],
                     outs=[_by_residue(qkv3.shape, w, d) for _, d in ATT_PATTERNS])
    return ys


def rope_backward(name, qkv3, cs3, sn3, dys):
    s, w = qkv3.shape[1], qkv3.shape[2]
    (dx,), _ = scan_bwd(name, _rope3_fn, nb=1, nchunk=s // _ROW_T, t=_ROW_T, vecs=[], carries=[], saved=[],
                        rows=[Row(qkv3, ddtype=BF16), Row(cs3, diff=False), Row(sn3, diff=False)],
                        douts=[_by_residue(a, w, d) for a, (_, d) in zip(dys, ATT_PATTERNS)])
    return dx


def _merge_fn(ci, b, carries, rows, vecs):
    o1, o2, o3, l1, l2, l3 = rows
    mx = lax.stop_gradient(jnp.maximum(l1, jnp.maximum(l2, l3)))
    e1, e2, e3 = jnp.exp(l1 - mx), jnp.exp(l2 - mx), jnp.exp(l3 - mx)
    return [], [(e1 * o1 + e2 * o2 + e3 * o3) / (e1 + e2 + e3)]


_ROW_T = 512


def _merge_rows(os_, ls_):
    ds = [d for _, d in ATT_PATTERNS]
    return [_by_residue(a, ATT_W, d) for a, d in zip(os_, ds)] + [_by_residue(a, ATT_W, d) for a, d in zip(ls_, ds)]


def merge_forward(name, os_, ls_, s):
    (y,), _ = scan_fwd(name, _merge_fn, nb=1, nchunk=s // _ROW_T, t=_ROW_T, rows=_merge_rows(os_, ls_), vecs=[],
                       carries=[], outs=[out_row((1, s, ATT_W), BF16)], save=False)
    return y


def merge_backward(name, os_, ls_, dmix3):
    s = dmix3.shape[1]
    drows, _ = scan_bwd(name, _merge_fn, nb=1, nchunk=s // _ROW_T, t=_ROW_T, rows=_merge_rows(os_, ls_), vecs=[],
                        carries=[], saved=[], douts=[Row(dmix3, 256, fc=lambda b: 3)])
    return drows


def _norm_mod_fn(ci, b, carries, rows, vecs):
    (x,) = rows
    g, sc, sh = vecs
    xn = x * lax.rsqrt(jnp.mean(x * x, axis=-1, keepdims=True) + NORM_EPS)
    return [], [xn * g * (1.0 + sc) + sh]


def norm_mod_forward(name, x3, g, sc, sh):
    s = x3.shape[1]
    (h,), _ = scan_fwd(name, _norm_mod_fn, nb=1, nchunk=s // _ROW_T, t=_ROW_T, rows=[Row(x3)], vecs=[Vec(g), Vec(sc), Vec(sh)],
                       carries=[], outs=[out_row(x3.shape, BF16)], save=False)
    return h


def norm_mod_backward(name, x3, g, sc, sh, dh3, add3):
    s = x3.shape[1]
    (dx,), dv = scan_bwd(name, _norm_mod_fn, nb=1, nchunk=s // _ROW_T, t=_ROW_T, rows=[Row(x3)], vecs=[Vec(g), Vec(sc), Vec(sh)],
                         carries=[], saved=[], douts=[Row(dh3)], adds={0: Row(add3)})
    return dx, dv


def _gate_fn(ci, b, carries, rows, vecs):
    return [], [rows[0] * vecs[0]]


def gate_backward(name, o3, g, dx3):
    s = o3.shape[1]
    (do,), (dg,) = scan_bwd(name, _gate_fn, nb=1, nchunk=s // _ROW_T, t=_ROW_T, rows=[Row(o3, ddtype=BF16)], vecs=[Vec(g)],
                            carries=[], saved=[], douts=[Row(dx3)])
    return do, dg


def _make_halves():
    @jax.custom_vjp
    def halves(x):
        h = x.shape[1] // 2
        return x[:, :h], x[:, h:]

    def fwd(x):
        return halves(x), None

    def bwd(_, g):
        return (jnp.concatenate(g, axis=1),)

    halves.defvjp(fwd, bwd)
    return halves


_halves = _make_halves()


def _ffn_fn(ci, b, carries, rows, vecs):
    (cu,) = carries
    (u,) = rows
    w, bias = vecs
    hg, hu = _halves(_conv(_shift8, cu, u, w, bias, 3))
    return [_tail8(u)], [_silu(hg) * hu]


_FFN_T = 256
_FFN_CW = FFN_DIM // 2
_FFN_CARRIES = [(8, 2 * _FFN_CW)]
FFN_BLOCK_ORDER = [0, 2, 1, 3]


def _ffn_io(up3, cw, cb):
    own = lambda b: b
    return [Row(up3, 2 * _FFN_CW, fc=own, ddtype=BF16)], [Vec(cw, 2 * _FFN_CW, own), Vec(cb, 2 * _FFN_CW, own)]


def ffn_down_forward(name, up3, cw, cb, w_down, res, gate):
    s, t, cw2 = up3.shape[1], _FFN_T, 2 * _FFN_CW
    d = w_down.shape[1]
    nchunk = s // t
    ride = RIDERS.take(name)
    r_in, r_out, r_scr = ride.specs() if ride else ([], [], [])

    def body(*refs):
        up_ref, cw_ref, cb_ref, wd_ref, res_ref, g_ref = refs[:6]
        ride_in = refs[6:6 + len(r_in)]
        act_ref, save_ref, dn_ref, x2_ref = refs[6 + len(r_in):10 + len(r_in)]
        ride_out = refs[10 + len(r_in):10 + len(r_in) + len(r_out)]
        car, acc = refs[10 + len(r_in) + len(r_out):12 + len(r_in) + len(r_out)]
        sems = refs[12 + len(r_in) + len(r_out):]
        i, b = pl.program_id(0), pl.program_id(1)
        if ride:
            ride.begin(ride_in, ride_out, sems, jnp.logical_and(i == 0, b == 0))

        @pl.when(i == 0)
        def _():
            car[b] = jnp.zeros(car.shape[1:], F32)

        cin = car[b]
        save_ref[0, 0] = cin
        (new_c,), (act,) = _ffn_fn(i, b, [cin], [up_ref[0]], [cw_ref[...], cb_ref[...]])
        car[b] = new_c
        act_ref[0] = act.astype(act_ref.dtype)
        part = _mxu(act, wd_ref[...], "nn")

        @pl.when(b == 0)
        def _():
            acc[...] = part

        @pl.when(b == 1)
        def _():
            tot = acc[...] + part
            dn_ref[...] = tot
            x2_ref[...] = res_ref[...] + g_ref[...] * tot

        if ride:
            ride.end(ride_in, ride_out, sems, jnp.logical_and(i == nchunk - 1, b == 1))

    tile = pl.BlockSpec((t, d), lambda i, b: (i, 0))
    out = pl.pallas_call(
        body, name=name, grid=(nchunk, 2),
        in_specs=[pl.BlockSpec((1, t, cw2), lambda i, b: (0, i, b)), pl.BlockSpec((cw.shape[0], cw2), lambda i, b: (0, b)),
                  pl.BlockSpec((1, cw2), lambda i, b: (0, b)), pl.BlockSpec((_FFN_CW, d), lambda i, b: (b, 0)), tile,
                  pl.BlockSpec((1, d), lambda i, b: (0, 0))] + r_in,
        out_specs=[pl.BlockSpec((1, t, _FFN_CW), lambda i, b: (0, i, b)), pl.BlockSpec((1, 1, 8, cw2), lambda i, b: (b, i, 0, 0)),
                   tile, tile] + r_out,
        out_shape=[jax.ShapeDtypeStruct((1, s, FFN_DIM), BF16), jax.ShapeDtypeStruct((2, nchunk, 8, cw2), F32),
                   jax.ShapeDtypeStruct((s, d), F32), jax.ShapeDtypeStruct((s, d), F32)] + (list(ride.out_shapes) if ride else []),
        scratch_shapes=[pltpu.VMEM((2, 8, cw2), F32), pltpu.VMEM((t, d), F32)] + r_scr,
        compiler_params=_cparams(),
    )(up3, cw, cb, w_down, res, gate, *(ride.ins if ride else []))
    if ride:
        RIDERS.done[name] = list(out[4:])
    return out[0], [out[1]], out[2], out[3]


def ffn_mid_backward(name, up3, cw, cb, saved, dact3):
    rows, vecs = _ffn_io(up3, cw, cb)
    s = up3.shape[1]
    return scan_bwd(name, _ffn_fn, nb=2, nchunk=s // _FFN_T, t=_FFN_T, rows=rows, vecs=vecs, carries=_FFN_CARRIES,
                    saved=saved, douts=[Row(dact3, _FFN_CW, fc=lambda b: b)])


def _adam_fn(ci, b, carries, rows, vecs):
    w, g, m, v = rows
    m = ADAM_B1 * m + (1.0 - ADAM_B1) * g
    v = ADAM_B2 * v + (1.0 - ADAM_B2) * (g * g)
    m_hat = m / (1.0 - ADAM_B1 ** ADAM_STEP)
    v_hat = v / (1.0 - ADAM_B2 ** ADAM_STEP)
    delta = -ADAM_LR * (m_hat / (jnp.sqrt(v_hat) + ADAM_EPS) + ADAM_WD * w)
    return [], [delta, m, v]


def adamw(name, w, g, m, v):
    shape = w.shape
    c = shape[-1]
    r = int(np.prod(shape[:-1]))
    t = _tile(r, 256, 8)
    as3 = lambda a: a.reshape(1, r, c)
    outs, _ = scan_fwd(name, _adam_fn, nb=1, nchunk=r // t, t=t, rows=[Row(as3(a)) for a in (w, g, m, v)], vecs=[], carries=[],
                       outs=[out_row((1, r, c)) for _ in range(3)], save=False)
    return [o.reshape(shape) for o in outs]


def rope_tables(positions):
    inv_freq = ROPE_THETA ** (-jnp.arange(0, ROT_DIM, 2, dtype=F32) / ROT_DIM)
    ang = positions.astype(F32)[:, None] * inv_freq
    s = positions.shape[0]
    cs = jnp.concatenate([jnp.cos(ang), jnp.cos(ang), jnp.ones((s, ATT_HEAD_DIM - ROT_DIM), F32)], axis=1)
    sn = jnp.concatenate([jnp.sin(ang), jnp.sin(ang), jnp.zeros((s, ATT_HEAD_DIM - ROT_DIM), F32)], axis=1)
    cs3 = jnp.concatenate([jnp.tile(cs, (1, 2 * ATT_HEADS)), jnp.ones((s, ATT_W), F32)], axis=1)
    sn3 = jnp.concatenate([jnp.tile(sn, (1, 2 * ATT_HEADS)), jnp.zeros((s, ATT_W), F32)], axis=1)
    return cs3[None], sn3[None]


def attention_forward(lname, qkv3, cs3, sn3):
    s = qkv3.shape[1]
    rotated = rope_forward(f"{lname}_rope", qkv3, cs3, sn3)
    os_, ls_, keep = [], [], []
    for pi, (_, d) in enumerate(ATT_PATTERNS):
        o, lse, saved = attn_forward(f"{lname}_attn{pi}", rotated[pi], d)
        os_.append(o)
        ls_.append(lse)
        keep.append(saved)
    y = merge_forward(f"{lname}_merge", os_, ls_, s)
    return y, (rotated, os_, ls_, keep)


def attention_backward(lname, qkv3, cs3, sn3, res, dmix3):
    rotated, os_, ls_, keep = res
    dm = merge_backward(f"{lname}_merge_b", os_, ls_, dmix3)
    dys = [attn_backward(f"{lname}_attn{pi}_b", rotated[pi], d, keep[pi], dm[pi], dm[3 + pi]) for pi, (_, d) in enumerate(ATT_PATTERNS)]
    return rope_backward(f"{lname}_rope_b", qkv3, cs3, sn3, dys)


def final_loss(name, x3, t3, g):
    s, d = x3.shape[1], x3.shape[2]
    t = _ROW_T

    def body(x_ref, t_ref, g_ref, loss_ref, dx_ref, dg_ref):
        i = pl.program_id(0)
        tv = t_ref[0]

        def f(x, gg):
            y = x * lax.rsqrt(jnp.mean(x * x, axis=-1, keepdims=True) + NORM_EPS) * gg
            e = y - tv
            return 0.5 * jnp.sum(jnp.mean(e * e, axis=-1, keepdims=True), axis=0, keepdims=True)

        l, vjp = jax.vjp(f, x_ref[0], g_ref[...])
        dx, dg = vjp(jnp.ones((1, 1), F32))
        dx_ref[0] = dx

        @pl.when(i == 0)
        def _():
            loss_ref[...] = jnp.zeros(loss_ref.shape, F32)
            dg_ref[...] = jnp.zeros(dg_ref.shape, F32)

        loss_ref[...] += jnp.broadcast_to(l, loss_ref.shape)
        dg_ref[...] += dg

    row = pl.BlockSpec((1, t, d), lambda i: (0, i, 0))
    vec = pl.BlockSpec((1, d), lambda i: (0, 0))
    return pl.pallas_call(
        body, name=name, grid=(s // t,), in_specs=[row, row, vec],
        out_specs=[pl.BlockSpec((8, 128), lambda i: (0, 0)), row, vec],
        out_shape=[jax.ShapeDtypeStruct((8, 128), F32), jax.ShapeDtypeStruct(x3.shape, F32), jax.ShapeDtypeStruct((1, d), F32)],
        compiler_params=pltpu.CompilerParams(dimension_semantics=("arbitrary",), vmem_limit_bytes=VMEM_LIMIT_BYTES),
    )(x3, t3, g)


_ADA_TN = 512


def ada_forward(name, c16, ada_w):
    depth, d, cols = ada_w.shape

    def body(c_ref, w_ref, o_ref):
        o_ref[0] = _mxu(_silu(c_ref[...]), w_ref[0], "nn")

    return pl.pallas_call(
        body, name=name, grid=(depth, cols // _ADA_TN),
        in_specs=[pl.BlockSpec((16, d), lambda l, j: (0, 0)), pl.BlockSpec((1, d, _ADA_TN), lambda l, j: (l, 0, j))],
        out_specs=pl.BlockSpec((1, 16, _ADA_TN), lambda l, j: (l, 0, j)),
        out_shape=jax.ShapeDtypeStruct((depth, 16, cols), F32),
        compiler_params=pltpu.CompilerParams(dimension_semantics=("arbitrary", "arbitrary"), vmem_limit_bytes=VMEM_LIMIT_BYTES),
    )(c16, ada_w)


def ada_backward(name, c16, dmod16, w, m, v):
    depth, d, cols = w.shape

    def body(c_ref, dm_ref, w_ref, m_ref, v_ref, g_ref, dl_ref, nm_ref, nv_ref):
        g = _mxu(_silu(c_ref[...]), dm_ref[0], "tn")
        _, (delta, nm, nv) = _adam_fn(None, None, [], [w_ref[0], g, m_ref[0], v_ref[0]], [])
        g_ref[0], dl_ref[0], nm_ref[0], nv_ref[0] = g, delta, nm, nv

    blk = pl.BlockSpec((1, d, _ADA_TN), lambda l, j: (l, 0, j))
    return pl.pallas_call(
        body, name=name, grid=(depth, cols // _ADA_TN),
        in_specs=[pl.BlockSpec((16, d), lambda l, j: (0, 0)), pl.BlockSpec((1, 16, _ADA_TN), lambda l, j: (l, 0, j)), blk, blk, blk],
        out_specs=[blk] * 4, out_shape=[jax.ShapeDtypeStruct(w.shape, F32)] * 4,
        compiler_params=pltpu.CompilerParams(dimension_semantics=("arbitrary", "arbitrary"), vmem_limit_bytes=VMEM_LIMIT_BYTES),
    )(c16, dmod16, w, m, v)


def _sum_fn(ci, b, carries, rows, vecs):
    acc = rows[0]
    for r in rows[1:]:
        acc = acc + r
    return [], [acc]


def sum_slots(name, a, nsum, out_dtype=F32):
    n, r, c = a.shape
    nb = n // nsum
    t = _tile(r, 256, 8)
    rows = [Row(a, fb=(lambda b, k=k: k * nb + b)) for k in range(nsum)]
    (out,), _ = scan_fwd(name, _sum_fn, nb=nb, nchunk=r // t, t=t, rows=rows, vecs=[], carries=[],
                         outs=[out_row((nb, r, c), out_dtype, fb=lambda b: b)], save=False)
    return out


def _flip(mask, pos):
    return tuple((1 - p) if m else p for m, p in zip(mask, pos))


ALL_PEERS = [(a, b, c) for a in (0, 1) for b in (0, 1) for c in (0, 1)][1:]
CHIP_PEERS = [(1, 0, 0), (0, 1, 0), (1, 1, 0)]
SIBLING = [(0, 0, 1)]


def _dev(pos):
    return 4 * pos[0] + 2 * pos[1] + pos[2]


def _chip(pos):
    return 2 * pos[0] + pos[1]


def allgather8(name, a):
    (out,) = ride_alone(name, allgather8_ride(a))
    return _with_own(out, a)


def _rows_of(shape):
    return -(-int(np.prod(shape)) // 1024) * 8


def _pack(arrs):
    parts = []
    for a in arrs:
        flat = a.reshape(-1).astype(F32)
        parts.append(jnp.pad(flat, (0, _rows_of(a.shape) * 128 - flat.shape[0])).reshape(-1, 128))
    rows = sum(p.shape[0] for p in parts)
    parts.append(jnp.zeros(((-rows) % _ROW_T, 128), F32))
    return jnp.concatenate(parts, axis=0)


def _unpack(buf, shapes):
    out, o = [], 0
    for s in shapes:
        r, n = _rows_of(s), int(np.prod(s))
        out.append(buf[o:o + r].reshape(-1)[:n].reshape(s))
        o += r
    return out


_WEIGHTS = ["ada_w", "ada_b", "norm1_g", "w_in", "ssd_conv_w", "ssd_conv_b", "ssd_dt_bias", "ssd_a_log", "ssd_d", "ssd_norm_g",
            "pool_w", "pool_scale", "w_out", "norm2_g", "ffn_up", "ffn_conv_w", "ffn_conv_b", "ffn_down", "final_g"]
_BIG = ["w_in", "w_out", "ffn_up", "ffn_down"]
_SMALL = [n for n in _WEIGHTS if n not in _BIG and n != "ada_w"]
_COL_SHARDED_SMALL = {"ssd_conv_w": 256, "ffn_conv_w": 1408}


def _pad_lanes(v, n=128):
    return jnp.pad(v.astype(F32), (0, n - v.shape[0]))[None]


_CHIP2_PARTS = [(1284, 1536), (1792, 1800), (1536, 1792), (IN_MAIN, IN_MAIN + 126)]


def _w_in_chip_cols(gp):
    q = IN_W // 4
    return [gp[:, :q], gp[:, q:2 * q], jnp.concatenate([gp[:, a:b] for a, b in _CHIP2_PARTS], axis=1), gp[:, IN_WP - q:]]


def _ffn_block_perm(a):
    n = a.shape[-1] // 4
    return jnp.concatenate([a[..., j * n:(j + 1) * n] for j in FFN_BLOCK_ORDER], axis=-1)


def _layer_forward(i, x3, modv, wts, sp, cs3, sn3):
    sh1, sc1, g1, sh2, sc2, g2 = modv
    big = lambda n: wts[n]() if callable(wts[n]) else wts[n]
    h1 = norm_mod_forward(f"l{i}_norm1", x3, wts["norm1_g"], sc1, sh1)
    proj3 = mm(f"l{i}_proj", h1[0], big("w_in")[:, :IN_MAIN], "nn")[None]
    qkv3 = mm(f"l{i}_qkv", h1[0], big("w_in")[:, IN_MAIN:], "nn")[None]
    y_ssd, sv_ssd = ssd_forward(f"l{i}_ssd", proj3, sp)
    y_pool, sv_pool = pool_forward(f"l{i}_pool", proj3, wts["wbd"], wts["pool_scale"])
    y_att, res_att = attention_forward(f"l{i}", qkv3, cs3, sn3)
    mix = jnp.concatenate([y_ssd, y_pool, y_att], axis=-1)
    out, x1 = mm(f"l{i}_wout", mix[0], big("w_out"), "nn", res=x3[0], gate=g1)
    x1 = x1[None]
    h2 = norm_mod_forward(f"l{i}_norm2", x1, wts["norm2_g"], sc2, sh2)
    up3 = mm(f"l{i}_up", h2[0], big("ffn_up"), "nn")[None]
    act, sv_ffn, dn, x2 = ffn_down_forward(f"l{i}_down", up3, wts["ffn_conv_w"], wts["ffn_conv_b"], big("ffn_down"), x1[0], g2)
    keep = dict(x=x3, h1=h1, proj3=proj3, qkv3=qkv3, sv_ssd=sv_ssd, sv_pool=sv_pool, res_att=res_att, mix=mix, out=out[None],
                x1=x1, h2=h2, up3=up3, act=act, sv_ffn=sv_ffn, dn=dn[None])
    return x2[None], keep


def _layer_backward(i, dx2, keep, modv, wts, sp, cs3, sn3, after=None):
    sh1, sc1, g1, sh2, sc2, g2 = modv
    k = keep
    big = lambda n: wts[n]() if callable(wts[n]) else wts[n]
    tell = lambda step, *a: after[step](*a) if after and step in after else None
    d_dn, d_g2 = gate_backward(f"l{i}_gate2_b", k["dn"], g2, dx2)
    d_act = mm(f"l{i}_down_bx", d_dn[0], big("ffn_down"), "nt")
    g_down = mm(f"l{i}_down_bw", k["act"][0], d_dn[0], "tn").reshape(4, FFN_DIM // 4, D_MODEL)
    (d_up,), dv_ffn = ffn_mid_backward(f"l{i}_ffn_b", k["up3"], wts["ffn_conv_w"], wts["ffn_conv_b"], k["sv_ffn"], d_act[None])
    tell("ffn_b")
    d_h2 = mm(f"l{i}_up_bx", d_up[0], big("ffn_up"), "nt")
    g_up = mm(f"l{i}_up_bw", k["h2"][0], d_up[0], "tn", tn=_FFN_CW,
              into=((4, D_MODEL, _FFN_CW), lambda r, c: ((c % 2) * 2 + c // 2, r, 0)))
    dx1, (d_n2, d_sc2, d_sh2) = norm_mod_backward(f"l{i}_norm2_b", k["x1"], wts["norm2_g"], sc2, sh2, d_h2[None], dx2)
    d_out, d_g1 = gate_backward(f"l{i}_gate1_b", k["out"], g1, dx1)
    d_mix = mm(f"l{i}_wout_bx", d_out[0], big("w_out"), "nt")[None]
    g_wout = mm(f"l{i}_wout_bw", k["mix"][0], d_out[0], "tn").reshape(4, D_MODEL // 4, D_MODEL)
    tell("wout_bw", g_wout, g_up, g_down)
    (dz, dxs, dbm, dcm, ddt), dv_ssd = ssd_backward(f"l{i}_ssd_b", k["proj3"], sp, k["sv_ssd"], d_mix)
    tell("ssd_b")
    (du_pool,), (d_wbd, d_pscale) = pool_backward(f"l{i}_pool_b", k["proj3"], wts["wbd"], wts["pool_scale"], k["sv_pool"], d_mix)
    d_qkv = attention_backward(f"l{i}", k["qkv3"], cs3, sn3, k["res_att"], d_mix)
    d_proj = jnp.concatenate([dz[0], dxs[0], dbm[0], dcm[0], du_pool[0], (ddt[0] + ddt[1]).astype(BF16), d_qkv[0]], axis=-1)
    g_win = jnp.stack(_w_in_chip_cols(mm(f"l{i}_proj_bw", k["h1"][0], d_proj, "tn")))
    tell("proj_bw", g_win)
    d_h1 = mm(f"l{i}_proj_bx", d_proj, big("w_in"), "nt")
    tell("proj_bx")
    dx, (d_n1, d_sc1, d_sh1) = norm_mod_backward(f"l{i}_norm1_b", k["x"], wts["norm1_g"], sc1, sh1, d_h1[None], dx1)
    dcwx, dcbx, dcwb, dcbb, dcwc, dcbc, ddtb, dalog, ddsk, dng = dv_ssd
    small = dict(
        norm1_g=d_n1[0], norm2_g=d_n2[0],
        ssd_conv_w=jnp.concatenate([dcwx[:, :512], dcwb[:, 512:768], dcwc[:, 768:]], axis=1),
        ssd_conv_b=jnp.concatenate([dcbx[0, :512], dcbb[0, 512:768], dcbc[0, 768:]]),
        ssd_dt_bias=ddtb[0, :8], ssd_a_log=dalog[0, :8], ssd_d=ddsk[0, :8], ssd_norm_g=dng[0],
        pool_w=jnp.stack([d_wbd[64 * g:64 * g + 64, 64 * g:64 * g + 64] for g in range(4)]), pool_scale=d_pscale[0],
        ffn_conv_w=_ffn_block_perm(dv_ffn[0]), ffn_conv_b=_ffn_block_perm(dv_ffn[1][0]),
    )
    dmod = jnp.concatenate([d_sh1[0], d_sc1[0], d_g1[0], d_sh2[0], d_sc2[0], d_g2[0]])
    return dx, [g_win, g_wout, g_up, g_down], small, dmod


def kernel(x, c, positions, ada_w, ada_b, norm1_g, w_in, ssd_conv_w, ssd_conv_b, ssd_dt_bias, ssd_a_log, ssd_d, ssd_norm_g, pool_w, pool_scale, w_out, norm2_g, ffn_up, ffn_conv_w, ffn_conv_b, ffn_down, final_g, loss_target, m_ada_w, m_ada_b, m_norm1_g, m_w_in, m_ssd_conv_w, m_ssd_conv_b, m_ssd_dt_bias, m_ssd_a_log, m_ssd_d, m_ssd_norm_g, m_pool_w, m_pool_scale, m_w_out, m_norm2_g, m_ffn_up, m_ffn_conv_w, m_ffn_conv_b, m_ffn_down, m_final_g, v_ada_w, v_ada_b, v_norm1_g, v_w_in, v_ssd_conv_w, v_ssd_conv_b, v_ssd_dt_bias, v_ssd_a_log, v_ssd_d, v_ssd_norm_g, v_pool_w, v_pool_scale, v_w_out, v_norm2_g, v_ffn_up, v_ffn_conv_w, v_ffn_conv_b, v_ffn_down, v_final_g):
    args = dict(locals())
    w = {n: args[n] for n in _WEIGHTS}
    m = {n: args["m_" + n] for n in _WEIGHTS}
    v = {n: args["v_" + n] for n in _WEIGHTS}
    d = D_MODEL
    me = (lax.axis_index("x"), lax.axis_index("y"), lax.axis_index("c"))
    chip, dev = _chip(me), _dev(me)
    RIDERS.reset()

    shapes0 = [c.shape, ssd_conv_w.shape, ffn_conv_w.shape]
    pack0 = _pack([c, ssd_conv_w, ffn_conv_w])
    shards = [w[n].astype(BF16) for n in _BIG]
    g0, w_in0 = ride_alone("gather_start", merge_rides([allgather8_ride(pack0), gather_ride(0, [shards[0]])]))
    g0 = _with_own(g0, pack0)
    c16 = jnp.pad(g0[:, :d // 128, :].reshape(8, d), ((0, 8), (0, 0)))
    by_chip = [_unpack(g0[2 * j], shapes0) for j in range(4)]
    conv_w_full = jnp.concatenate([p[1] for p in by_chip], axis=-1)
    fconv_w_full = jnp.concatenate([p[2] for p in by_chip], axis=-1)

    modp = ada_forward("ada_fwd", c16, ada_w)[:, :8]
    g1 = allgather8("gather_mod", _pack([modp]))
    modfull = jnp.concatenate([_unpack(g1[2 * j], [modp.shape])[0] for j in range(4)], axis=-1)
    mod = lax.dynamic_index_in_dim(modfull, dev, axis=1, keepdims=False) + ada_b
    modv = [[mod[i, q * d:(q + 1) * d][None] for q in range(6)] for i in range(DEPTH)]


    def weight(k, layer, got):
        full = lax.dynamic_update_slice(got, shards[k][layer][None], (chip, 0, 0))
        if k == 0:
            return _w_in_from_chips(full)
        if k == 2:
            return jnp.concatenate([full[j] for j in FFN_BLOCK_ORDER], axis=1)
        return full.reshape(-1, full.shape[2])

    def later(k, layer, *sources):
        made = []

        def get():
            if not made:
                got = [RIDERS.result(host)[pos] for host, pos in sources]
                made.append(weight(k, layer, got[0] if len(got) == 1 else jnp.concatenate(got, axis=1)))
            return made[0]
        return get

    cs3, sn3 = rope_tables(positions[0])
    eye4 = jnp.eye(4, dtype=F32)
    wts, sps = [], []
    for i in range(DEPTH):
        wts.append(dict(
            norm1_g=norm1_g[i][None], norm2_g=norm2_g[i][None], pool_scale=pool_scale[i][None],
            wbd=(eye4[:, None, :, None] * pool_w[i][:, :, None, :]).reshape(POOL_W, POOL_W),
            ffn_conv_w=_ffn_block_perm(fconv_w_full[i]), ffn_conv_b=_ffn_block_perm(ffn_conv_b[i])[None]))
        sps.append(dict(cw=conv_w_full[i], cb=ssd_conv_b[i][None], dtb=_pad_lanes(ssd_dt_bias[i]), alog=_pad_lanes(ssd_a_log[i]),
                        dsk=_pad_lanes(ssd_d[i]), ng=ssd_norm_g[i][None]))

    RIDERS.book("l0_ssd", gather_ride(0, [shards[1], shards[3]]))
    half = shards[2].shape[1] // 2
    RIDERS.book("l0_attn0", gather_ride(0, [shards[2][:, :half]]))
    RIDERS.book("l0_attn1", gather_ride(0, [shards[2][:, half:]]))
    wts[0].update(w_in=weight(0, 0, w_in0), w_out=later(1, 0, ("l0_ssd", 0)), ffn_down=later(3, 0, ("l0_ssd", 1)),
                  ffn_up=later(2, 0, ("l0_attn0", 0), ("l0_attn1", 0)))
    RIDERS.book("l0_attn2", gather_ride(1, [shards[0], shards[1]]))
    RIDERS.book("l0_up", gather_ride(1, [shards[3]]))
    RIDERS.book("l0_down", gather_ride(1, [shards[2]]))
    wts[1].update(w_in=later(0, 1, ("l0_attn2", 0)), w_out=later(1, 1, ("l0_attn2", 1)), ffn_up=later(2, 1, ("l0_down", 0)),
                  ffn_down=later(3, 1, ("l0_up", 0)))
    x1_, keep0 = _layer_forward(0, x, modv[0], wts[0], sps[0], cs3, sn3)
    xc, keep1 = _layer_forward(1, x1_, modv[1], wts[1], sps[1], cs3, sn3)
    keeps = [keep0, keep1]
    lossblk, dx, d_final = final_loss("final_loss", xc, loss_target, final_g[None])
    loss = lax.psum(lossblk[0, 0], ("x", "y", "c"))

    small_g, dmods = [None] * DEPTH, [None] * DEPTH
    part_sum, from_chips = [[None] * 4 for _ in range(DEPTH)], [[None] * 4 for _ in range(DEPTH)]

    def owner_sum(layer, ks, mine, theirs):
        for k, g, t in zip(ks, mine, theirs):
            part_sum[layer][k] = add_arrays(f"sum_cores{layer}_{_BIG[k]}", [g, t], BF16)

    dx, by_chip1, small_g[1], dmods[1] = _layer_backward(1, dx, keeps[1], modv[1], wts[1], sps[1], cs3, sn3)
    RIDERS.book("l0_ffn_b", to_owner_ride(1, by_chip1))

    def after_ffn_b():
        owner_sum(1, range(4), by_chip1, RIDERS.result("l0_ffn_b"))
        RIDERS.book("l0_up_bx", scatter_ride(1, [part_sum[1][2]]))
        RIDERS.book("l0_up_bw", scatter_ride(1, [part_sum[1][0], part_sum[1][1]]))
        RIDERS.book("l0_norm2_b", scatter_ride(1, [part_sum[1][3]]))

    early = []

    def after_wout_bw(g_wout, g_up, g_down):
        early.extend([g_wout, g_up, g_down])
        RIDERS.book("l0_ssd_b", to_owner_ride(0, early))

    def after_ssd_b():
        owner_sum(0, [1, 2, 3], early, RIDERS.result("l0_ssd_b"))
        for host, k in (("l0_attn0_b", 2), ("l0_attn1_b", 3), ("l0_attn2_b", 1)):
            RIDERS.book(host, scatter_ride(0, [part_sum[0][k]]))

    last = []

    def after_proj_bw(g_win):
        last.append(g_win)
        RIDERS.book("l0_proj_bx", to_owner_ride(0, last))

    def after_proj_bx():
        owner_sum(0, [0], last, RIDERS.result("l0_proj_bx"))
        RIDERS.book("l0_norm1_b", scatter_ride(0, [part_sum[0][0]]))

    hooks = dict(ffn_b=after_ffn_b, wout_bw=after_wout_bw, ssd_b=after_ssd_b, proj_bw=after_proj_bw, proj_bx=after_proj_bx)
    dx, _, small_g[0], dmods[0] = _layer_backward(0, dx, keeps[0], modv[0], wts[0], sps[0], cs3, sn3, after=hooks)
    from_chips[1][2], (from_chips[1][0], from_chips[1][1]) = RIDERS.result("l0_up_bx")[0], RIDERS.result("l0_up_bw")
    from_chips[1][3] = RIDERS.result("l0_norm2_b")[0]
    for host, k in (("l0_attn0_b", 2), ("l0_attn1_b", 3), ("l0_attn2_b", 1), ("l0_norm1_b", 0)):
        from_chips[0][k] = RIDERS.result(host)[0]
    mine = [sum_chips_mine(f"sum_chips_{n}", part_sum[0][k], from_chips[0][k], part_sum[1][k], from_chips[1][k])
            for k, n in enumerate(_BIG)]

    part = dict(ada_b=jnp.stack(dmods), final_g=d_final[0])
    for n in _SMALL:
        if n not in part:
            part[n] = jnp.stack([small_g[i][n] for i in range(DEPTH)])
    full_shapes = [part[n].shape for n in _SMALL]
    pack_small = _pack([part[n] for n in _SMALL])
    *theirs, gs = ride_alone("swap_r_gather_small", merge_rides([swap_ride(mine), allgather8_ride(pack_small)]))
    grads = {n: jnp.stack([jnp.where(me[2] == 0, a, g), jnp.where(me[2] == 0, g, a)]) for n, a, g in zip(_BIG, mine, theirs)}
    gs = _with_own(gs, pack_small)
    tot = _unpack(sum_slots("sum_small", gs, 8)[0], full_shapes)
    small_tot = dict(zip(_SMALL, tot))
    dmod_all = gs[:, :DEPTH * 6 * d // 128, :].reshape(8, DEPTH, 6 * d)
    for n, ncol in _COL_SHARDED_SMALL.items():
        small_tot[n] = lax.dynamic_slice_in_dim(small_tot[n], chip * ncol, ncol, axis=2)
    grads.update(small_tot)

    ncol = ada_w.shape[2]
    dm = lax.dynamic_slice_in_dim(dmod_all, chip * ncol, ncol, axis=2).transpose(1, 0, 2)
    upd = {}
    g_ada, *upd["ada_w"] = ada_backward("ada_bwd", c16, jnp.pad(dm, ((0, 0), (0, 8), (0, 0))), ada_w, m["ada_w"], v["ada_w"])
    grads["ada_w"] = g_ada

    for n in _BIG:
        upd[n] = adamw(f"adam_{n}", w[n], grads[n], m[n], v[n])
    shapes_s = [w[n].shape for n in _SMALL]
    packed = [_pack([src[n] for n in _SMALL]) for src in (w, grads, m, v)]
    outs_s = [_unpack(o, shapes_s) for o in adamw("adam_small", *packed)]
    for q, n in enumerate(_SMALL):
        upd[n] = [outs_s[0][q], outs_s[1][q], outs_s[2][q]]

    return (loss, dx, *[grads[n] for n in _WEIGHTS], *[upd[n][0] for n in _WEIGHTS], *[upd[n][1] for n in _WEIGHTS],
            *[upd[n][2] for n in _WEIGHTS])


def ride_alone(name, ride):
    ni, no = len(ride.ins), len(ride.out_shapes)

    def body(*refs):
        ride.begin(refs[:ni], refs[ni:ni + no], refs[ni + no:])
        ride.end(refs[:ni], refs[ni:ni + no], refs[ni + no:])

    in_specs, out_specs, scratch = ride.specs()
    return list(pl.pallas_call(body, name=name, in_specs=in_specs, out_specs=out_specs, out_shape=ride.out_shapes,
                               scratch_shapes=scratch)(*ride.ins))


def mm(name, a, b, mode, out_dtype=F32, res=None, gate=None, tm=1408, tn=1536, tk=1408, into=None):
    ride = RIDERS.take(name)
    if mode == "nn":
        (m, k), n = a.shape, b.shape[1]
    elif mode == "nt":
        (m, k), n = a.shape, b.shape[0]
    else:
        (k, m), n = a.shape, b.shape[1]
    tm, tn, tk = _tile(m, tm), _tile(n, tn), _tile(k, tk)
    ni, nj, nk = m // tm, n // tn, k // tk
    a_spec = pl.BlockSpec((tk, tm), lambda i, j, q: (q, i)) if mode == "tn" else pl.BlockSpec((tm, tk), lambda i, j, q: (i, q))
    b_spec = pl.BlockSpec((tn, tk), lambda i, j, q: (j, q)) if mode == "nt" else pl.BlockSpec((tk, tn), lambda i, j, q: (q, j))
    o_spec = pl.BlockSpec((tm, tn), lambda i, j, q: (i, j))
    fused = res is not None
    lead = 0 if into is None else len(into[0]) - 2
    first = (0,) * lead + (slice(None), slice(None))
    ins, in_specs = [a, b], [a_spec, b_spec]
    out_shape, out_specs = [jax.ShapeDtypeStruct((m, n), out_dtype)], [o_spec]
    if fused:
        ins += [res, gate]
        in_specs += [o_spec, pl.BlockSpec((1, tn), lambda i, j, q: (0, j))]
        out_shape.append(jax.ShapeDtypeStruct((m, n), F32))
        out_specs.append(o_spec)
    if into is not None:
        shape, omap = into
        out_shape = [jax.ShapeDtypeStruct(shape, out_dtype)]
        out_specs = [pl.BlockSpec((1,) * lead + (tm, tn), lambda i, j, q: omap(i, j))]
    n_in, n_out = len(ins), len(out_shape)
    scratch = [pltpu.VMEM((tm, tn), F32)]
    if ride is not None:
        r_in, r_out, r_scr = ride.specs()
        ins, in_specs = ins + list(ride.ins), in_specs + r_in
        out_shape, out_specs = out_shape + list(ride.out_shapes), out_specs + r_out
        scratch = scratch + r_scr

    def body(*refs):
        a_ref, b_ref = refs[:2]
        o_ref = refs[len(ins)]
        acc = refs[len(ins) + len(out_shape)]
        i, j, q = pl.program_id(0), pl.program_id(1), pl.program_id(2)
        at = lambda x, y, z: jnp.logical_and(jnp.logical_and(i == x, j == y), q == z)
        r_refs = (refs[n_in:len(ins)], refs[len(ins) + n_out:len(ins) + len(out_shape)], refs[len(ins) + len(out_shape) + 1:])
        if ride is not None:
            ride.begin(*r_refs, at(0, 0, 0))

        @pl.when(q == 0)
        def _():
            acc[...] = jnp.zeros(acc.shape, F32)

        acc[...] += _mxu(a_ref[...], b_ref[...], mode)

        @pl.when(q == nk - 1)
        def _():
            o_ref[first] = acc[...].astype(o_ref.dtype)
            if fused:
                refs[len(ins) + 1][...] = refs[2][...] + refs[3][...] * acc[...]

        if ride is not None:
            ride.end(*r_refs, at(ni - 1, nj - 1, nk - 1))

    sem = ("arbitrary",) * 3 if ride is not None else ("parallel", "parallel", "arbitrary")
    out = pl.pallas_call(
        body, name=name, grid=(ni, nj, nk), in_specs=in_specs, out_specs=out_specs, out_shape=out_shape, scratch_shapes=scratch,
        compiler_params=pltpu.CompilerParams(dimension_semantics=sem, vmem_limit_bytes=VMEM_LIMIT_BYTES),
    )(*ins)
    if ride is not None:
        RIDERS.done[name] = list(out[n_out:])
    return tuple(out[:n_out]) if fused else out[0]


def add_arrays(name, arrs, out_dtype=F32):
    nb, r, c = arrs[0].shape
    t = _tile(r, 256, 8)
    (out,), _ = scan_fwd(name, _sum_fn, nb=nb, nchunk=r // t, t=t, rows=[Row(a, fb=lambda b: b) for a in arrs], vecs=[], carries=[],
                         outs=[out_row((nb, r, c), out_dtype, fb=lambda b: b)], save=False)
    return out


def _sum_chips_mine_fn(ci, b, carries, rows, vecs):
    mine_layer = lax.axis_index("c")
    chip = 2 * lax.axis_index("x") + lax.axis_index("y")
    tot = None
    for j in range(4):
        own = jnp.where(mine_layer == 0, rows[j], rows[8 + j])
        sent = jnp.where(mine_layer == 0, rows[4 + j], rows[12 + j])
        term = jnp.where(chip == j, own, sent)
        tot = term if tot is None else tot + term
    return [], [tot]


def sum_chips_mine(name, p0, q0, p1, q1):
    _, r, c = p0.shape
    t = _tile(r, 256, 8)
    rows = [Row(a, fb=(lambda b, j=j: j)) for a in (p0, q0, p1, q1) for j in range(4)]
    (out,), _ = scan_fwd(name, _sum_chips_mine_fn, nb=1, nchunk=r // t, t=t, rows=rows, vecs=[], carries=[],
                         outs=[out_row((1, r, c))], save=False)
    return out[0]


def _remote(src, dst, send_sems, recv_sems, k, to):
    return pltpu.make_async_remote_copy(src_ref=src, dst_ref=dst, send_sem=send_sems.at[k], recv_sem=recv_sems.at[k],
                                        device_id=to, device_id_type=MESH)


def gather_ride(layer, shards):
    na = len(shards)

    def start(ins, outs, ss, rs, me):
        @pl.when(me[2] == layer)
        def _():
            for k in range(na):
                for p, mask in enumerate(CHIP_PEERS):
                    _remote(ins[k].at[layer], outs[k].at[_chip(me)], ss, rs, 6 * k + p, _flip(mask, me)).start()

    def finish(ins, outs, ss, rs, me):
        sibling = _flip(SIBLING[0], me)

        @pl.when(me[2] == layer)
        def _():
            for k in range(na):
                for p, mask in enumerate(CHIP_PEERS):
                    slot = outs[k].at[_chip(_flip(mask, me))]
                    _remote(ins[k].at[layer], slot, ss, rs, 6 * k + p, _flip(mask, me)).wait_recv()
                    _remote(slot, slot, ss, rs, 6 * k + 3 + p, sibling).start()
            for k in range(na):
                for p, mask in enumerate(CHIP_PEERS):
                    slot = outs[k].at[_chip(_flip(mask, me))]
                    _remote(ins[k].at[layer], slot, ss, rs, 6 * k + p, _flip(mask, me)).wait_send()
                    _remote(slot, slot, ss, rs, 6 * k + 3 + p, sibling).wait_send()

        @pl.when(me[2] != layer)
        def _():
            for k in range(na):
                for p, mask in enumerate(CHIP_PEERS):
                    slot = outs[k].at[_chip(_flip(mask, me))]
                    _remote(slot, slot, ss, rs, 6 * k + 3 + p, sibling).wait_recv()

    return Ride(list(shards), [jax.ShapeDtypeStruct((4,) + a.shape[1:], a.dtype) for a in shards], 6 * na, start, finish)


def scatter_ride(layer, parts):
    na = len(parts)

    def start(ins, outs, ss, rs, me):
        @pl.when(me[2] == layer)
        def _():
            for k in range(na):
                for p, mask in enumerate(CHIP_PEERS):
                    peer = _flip(mask, me)
                    _remote(ins[k].at[_chip(peer)], outs[k].at[_chip(me)], ss, rs, 3 * k + p, peer).start()

    def finish(ins, outs, ss, rs, me):
        @pl.when(me[2] == layer)
        def _():
            for k in range(na):
                for p, mask in enumerate(CHIP_PEERS):
                    peer = _flip(mask, me)
                    _remote(ins[k].at[_chip(peer)], outs[k].at[_chip(peer)], ss, rs, 3 * k + p, peer).wait_recv()
                    _remote(ins[k].at[_chip(peer)], outs[k].at[_chip(me)], ss, rs, 3 * k + p, peer).wait_send()

    return Ride(list(parts), [jax.ShapeDtypeStruct(a.shape, a.dtype) for a in parts], 3 * na, start, finish)


def to_owner_ride(layer, arrays):
    na = len(arrays)

    def start(ins, outs, ss, rs, me):
        @pl.when(me[2] != layer)
        def _():
            for k in range(na):
                _remote(ins[k], outs[k], ss, rs, k, _flip(SIBLING[0], me)).start()

    def finish(ins, outs, ss, rs, me):
        for k in range(na):
            cp = _remote(ins[k], outs[k], ss, rs, k, _flip(SIBLING[0], me))
            pl.when(me[2] != layer)(cp.wait_send)
            pl.when(me[2] == layer)(cp.wait_recv)

    return Ride(list(arrays), [jax.ShapeDtypeStruct(a.shape, a.dtype) for a in arrays], na, start, finish)


def allgather8_ride(a):
    def start(ins, outs, ss, rs, me):
        for p, mask in enumerate(ALL_PEERS):
            _remote(ins[0], outs[0].at[_dev(me)], ss, rs, p, _flip(mask, me)).start()

    def finish(ins, outs, ss, rs, me):
        for p, mask in enumerate(ALL_PEERS):
            peer = _flip(mask, me)
            _remote(ins[0], outs[0].at[_dev(peer)], ss, rs, p, peer).wait_recv()
            _remote(ins[0], outs[0].at[_dev(me)], ss, rs, p, peer).wait_send()

    return Ride([a], [jax.ShapeDtypeStruct((8,) + a.shape, a.dtype)], len(ALL_PEERS), start, finish)


def _with_own(gathered, own):
    me = _dev((lax.axis_index("x"), lax.axis_index("y"), lax.axis_index("c")))
    return jnp.where((jnp.arange(8) == me)[:, None, None], own[None], gathered)


def swap_ride(arrays):
    na = len(arrays)

    def start(ins, outs, ss, rs, me):
        for k in range(na):
            _remote(ins[k], outs[k], ss, rs, k, _flip(SIBLING[0], me)).start()

    def finish(ins, outs, ss, rs, me):
        for k in range(na):
            cp = _remote(ins[k], outs[k], ss, rs, k, _flip(SIBLING[0], me))
            cp.wait_recv()
            cp.wait_send()

    return Ride(list(arrays), [jax.ShapeDtypeStruct(a.shape, a.dtype) for a in arrays], na, start, finish)


class _Shifted:
    def __init__(self, ref, offset):
        self.ref, self.offset = ref, offset

    @property
    def at(self):
        return self

    def __getitem__(self, k):
        return self.ref.at[self.offset + k]


def merge_rides(rides):
    def spans(counts):
        out, o = [], 0
        for n in counts:
            out.append((o, o + n))
            o += n
        return out

    si, so = spans([len(r.ins) for r in rides]), spans([len(r.out_shapes) for r in rides])
    ss_ = spans([r.nsem for r in rides])

    def each(method):
        def run(ins, outs, ss, rs, me):
            for r, (i0, i1), (o0, o1), (s0, _) in zip(rides, si, so, ss_):
                getattr(r, method)(ins[i0:i1], outs[o0:o1], _Shifted(ss, s0), _Shifted(rs, s0), me)
        return run

    return Ride([a for r in rides for a in r.ins], [s for r in rides for s in r.out_shapes], sum(r.nsem for r in rides),
                each("start"), each("finish"))


def _w_in_from_chips(a):
    c2 = a[2]
    pad = jnp.zeros((c2.shape[0], IN_WP - IN_W), c2.dtype)
    return jnp.concatenate([a[0], a[1], c2[:, :252], c2[:, 260:516], c2[:, 252:260], pad, c2[:, 516:], a[3]], axis=1)
```

```python
import functools

import numpy as np
import jax
import jax.numpy as jnp
from jax import lax
from jax.experimental import pallas as pl
from jax.experimental.pallas import tpu as pltpu

F32 = jnp.float32
BF16 = jnp.bfloat16
MESH = pl.DeviceIdType.MESH

D_MODEL = 1024
DEPTH = 2
SSD_INNER = 512
POOL_W = 256
POOL_WINDOWS = (2, 4, 8, 16)
ATT_W = 256
ATT_HEADS = 4
ATT_HEAD_DIM = 64
ATT_PATTERNS = ((128, 1), (512, 4), (2048, 16))
ATT_BLOCK = 128
ROT_DIM = 16
ROPE_THETA = 500000.0
IN_W = 2568
IN_WP = 2688
IN_MAIN = 1920
FFN_DIM = 2816
NORM_EPS = 1e-6
ADAM_LR, ADAM_B1, ADAM_B2, ADAM_EPS, ADAM_WD, ADAM_STEP = 0.001, 0.9, 0.999, 1e-08, 0.01, 10

VMEM_LIMIT_BYTES = 56 * 1024 * 1024
NEG = -1e30


def _mxu(a, b, mode):
    dims = {"nn": ((1,), (0,)), "nt": ((1,), (1,)), "tn": ((0,), (0,))}[mode]
    return lax.dot_general(a.astype(BF16), b.astype(BF16), (dims, ((), ())), preferred_element_type=F32)


@functools.partial(jax.custom_vjp, nondiff_argnums=(2,))
def _bdot(a, b, mode):
    return _mxu(a, b, mode)


def _bdot_fwd(a, b, mode):
    return _mxu(a, b, mode), (a, b)


def _bdot_bwd(mode, res, g):
    a, b = res
    if mode == "nn":
        return _mxu(g, b, "nt"), _mxu(a, g, "tn")
    if mode == "nt":
        return _mxu(g, b, "nn"), _mxu(g, a, "tn")
    return _mxu(b, g, "nt"), _mxu(a, g, "nn")


_bdot.defvjp(_bdot_fwd, _bdot_bwd)


def _iota(shape, dim):
    return lax.broadcasted_iota(jnp.int32, shape, dim)


def _make_shift(h):
    @functools.partial(jax.custom_vjp, nondiff_argnums=(2,))
    def shift(halo, cur, k):
        if k == 0:
            return cur
        full = jnp.concatenate([halo, cur], axis=0)
        return pltpu.roll(full, k, 0)[h:]

    def fwd(halo, cur, k):
        return shift(halo, cur, k), None

    def bwd(k, _, g):
        t, w = g.shape
        if k == 0:
            return jnp.zeros((h, w), F32), g
        d_cur = jnp.where(_iota((t, w), 0) < t - k, pltpu.roll(g, t - k, 0), 0.0)
        top = g[:h]
        d_halo = jnp.where(_iota((h, w), 0) >= h - k, pltpu.roll(top, h - k, 0) if k < h else top, 0.0)
        return d_halo, d_cur

    shift.defvjp(fwd, bwd)
    return shift


_shift8 = _make_shift(8)
_shift16 = _make_shift(16)


def _make_tail(h):
    @jax.custom_vjp
    def tail(x):
        return x[x.shape[0] - h:]

    def fwd(x):
        return tail(x), x.shape[0]

    def bwd(t, g):
        return (jnp.concatenate([jnp.zeros((t - h, g.shape[1]), F32), g], axis=0),)

    tail.defvjp(fwd, bwd)
    return tail


_tail8 = _make_tail(8)
_tail16 = _make_tail(16)


@jax.custom_vjp
def _cumsum_rows(x):
    t = x.shape[0]
    row, s = _iota(x.shape, 0), 1
    while s < t:
        x = x + jnp.where(row >= s, pltpu.roll(x, s, 0), 0.0)
        s *= 2
    return x


def _cumsum_rows_fwd(x):
    return _cumsum_rows(x), None


def _cumsum_rows_bwd(_, g):
    t = g.shape[0]
    row, s = _iota(g.shape, 0), 1
    while s < t:
        g = g + jnp.where(row < t - s, pltpu.roll(g, t - s, 0), 0.0)
        s *= 2
    return (g,)


_cumsum_rows.defvjp(_cumsum_rows_fwd, _cumsum_rows_bwd)


@jax.custom_vjp
def _rot_pairs(t):
    e = _iota(t.shape, 1) % ATT_HEAD_DIM
    n = t.shape[1]
    return jnp.where(e < 8, -pltpu.roll(t, n - 8, 1), jnp.where(e < 16, pltpu.roll(t, 8, 1), 0.0))


def _rot_pairs_fwd(t):
    return _rot_pairs(t), None


def _rot_pairs_bwd(_, g):
    e = _iota(g.shape, 1) % ATT_HEAD_DIM
    n = g.shape[1]
    return (pltpu.roll(jnp.where(e < 8, -g, 0.0), 8, 1) + pltpu.roll(jnp.where(jnp.logical_and(e >= 8, e < 16), g, 0.0), n - 8, 1),)


_rot_pairs.defvjp(_rot_pairs_fwd, _rot_pairs_bwd)


def _make_thirds():
    @jax.custom_vjp
    def thirds(x):
        w = x.shape[1] // 3
        return x[:, :w], x[:, w:2 * w], x[:, 2 * w:]

    def fwd(x):
        return thirds(x), None

    def bwd(_, g):
        return (jnp.concatenate(g, axis=1),)

    thirds.defvjp(fwd, bwd)
    return thirds


_thirds = _make_thirds()


def _rowk(w, k):
    return jnp.sum(jnp.where(_iota(w.shape, 0) == k, w, 0.0), axis=0, keepdims=True)


def _silu(x):
    return x * (0.5 * jnp.tanh(0.5 * x) + 0.5)


def _softplus(x):
    return jnp.maximum(x, 0.0) + jnp.log(1.0 + jnp.exp(-jnp.abs(x)))


def _tile(dim, target, unit=128):
    if dim <= target:
        return dim
    best = None
    for t in range(unit, target + 1, unit):
        if dim % t == 0:
            best = t
    assert best is not None, (dim, target)
    return best


class Ride:
    def __init__(self, ins, out_shapes, nsem, start, finish):
        self.ins, self.out_shapes, self.nsem, self.start, self.finish = ins, out_shapes, nsem, start, finish

    def specs(self):
        hbm = pl.BlockSpec(memory_space=pl.ANY)
        return [hbm] * len(self.ins), [hbm] * len(self.out_shapes), [pltpu.SemaphoreType.DMA((self.nsem,))] * 2

    def begin(self, in_refs, out_refs, sems, cond=None):
        me = (lax.axis_index("x"), lax.axis_index("y"), lax.axis_index("c"))
        go = lambda: self.start(in_refs, out_refs, sems[0], sems[1], me)
        go() if cond is None else pl.when(cond)(go)

    def end(self, in_refs, out_refs, sems, cond=None):
        me = (lax.axis_index("x"), lax.axis_index("y"), lax.axis_index("c"))
        go = lambda: self.finish(in_refs, out_refs, sems[0], sems[1], me)
        go() if cond is None else pl.when(cond)(go)


class _Riders:
    def reset(self):
        self.booked, self.done = {}, {}

    def book(self, host, ride):
        assert host not in self.booked, host
        self.booked[host] = ride

    def take(self, host):
        return self.booked.pop(host, None)

    def result(self, host):
        return self.done[host]


RIDERS = _Riders()
RIDERS.reset()


class Row:
    def __init__(self, arr, w=None, fb=None, fc=None, diff=True, slot=False, dcols=None, dfc=None, ddtype=F32, view=None):
        self.ddtype = ddtype
        self.view = view
        self.arr = arr
        self.w = arr.shape[2] if w is None else w
        self.fb = (lambda b: 0) if fb is None else fb
        self.fc = (lambda b: 0) if fc is None else fc
        self.diff = diff
        self.slot = slot
        self.dcols = dcols
        self.dfc = dfc


class Vec:
    def __init__(self, arr, w=None, fc=None, diff=True):
        self.arr = arr
        self.w = arr.shape[1] if w is None else w
        self.fc = fc
        self.diff = diff


def _row_spec(r, t, nchunk, reverse):
    shape = (1, t, r.w) if r.view is None else (1, t // r.view, r.view * r.w)
    if reverse:
        return pl.BlockSpec(shape, lambda b, i, r=r: (r.fb(b), nchunk - 1 - i, r.fc(b)))
    return pl.BlockSpec(shape, lambda b, i, r=r: (r.fb(b), i, r.fc(b)))


def _load_row(ref, r, t, scr):
    if r.view is None:
        return ref[0]
    d, w = r.view, r.w
    for q in range(d):
        for j in range(w // 128):
            scr[j, pl.ds(q, t // d, stride=d), :] = ref[0, :, q * w + 128 * j:q * w + 128 * (j + 1)].astype(F32)
    return jnp.concatenate([scr[j] for j in range(w // 128)], axis=1)


def _store_row(ref, r, t, scr, val):
    if r.view is None:
        ref[0] = val.astype(ref.dtype)
        return
    d, w = r.view, r.w
    for j in range(w // 128):
        scr[j] = val[:, 128 * j:128 * (j + 1)]
    for q in range(d):
        for j in range(w // 128):
            ref[0, :, q * w + 128 * j:q * w + 128 * (j + 1)] = scr[j, pl.ds(q, t // d, stride=d), :].astype(ref.dtype)


def _view_scratch(specs, t):
    ws = [r.w for r in specs if r.view is not None]
    return [pltpu.VMEM((max(ws) // 128, t, 128), F32)] if ws else []


def _vec_spec(v):
    if v.fc is None:
        return pl.BlockSpec(v.arr.shape, lambda b, i: (0, 0))
    return pl.BlockSpec((v.arr.shape[0], v.w), lambda b, i, v=v: (0, v.fc(b)))


def _cparams():
    return pltpu.CompilerParams(dimension_semantics=("arbitrary", "arbitrary"), vmem_limit_bytes=VMEM_LIMIT_BYTES)


def scan_fwd(name, fn, *, nb, nchunk, t, rows, vecs, carries, outs, save):
    nr, nv, nc, no = len(rows), len(vecs), len(carries), len(outs)
    ns = nc if save else 0
    ride = RIDERS.take(name)
    r_in, r_out, r_scr = ride.specs() if ride else ([], [], [])

    def body(*refs):
        p = 0
        row_refs = refs[p:p + nr]; p += nr
        vec_refs = refs[p:p + nv]; p += nv
        ride_in = refs[p:p + len(r_in)]; p += len(r_in)
        out_refs = refs[p:p + no]; p += no
        save_refs = refs[p:p + ns]; p += ns
        ride_out = refs[p:p + len(r_out)]; p += len(r_out)
        car = refs[p:p + nc]; p += nc
        scr = refs[p] if stage else None
        sems = refs[p + len(stage):]
        b, i = pl.program_id(0), pl.program_id(1)
        if ride:
            ride.begin(ride_in, ride_out, sems, jnp.logical_and(b == 0, i == 0))
        if nc:
            @pl.when(i == 0)
            def _():
                for c_ref in car:
                    c_ref[...] = jnp.zeros(c_ref.shape, F32)
        cin = [c_ref[...] for c_ref in car]
        if save:
            for s_ref, cv in zip(save_refs, cin):
                s_ref[0, 0] = cv
        new_c, o = fn(i, b, cin, [_load_row(ref, r, t, scr) for ref, r in zip(row_refs, rows)], [v[...] for v in vec_refs])
        for c_ref, cv in zip(car, new_c):
            c_ref[...] = cv
        for o_ref, spec, ov in zip(out_refs, outs, o):
            _store_row(o_ref, spec, t, scr, ov)
        if ride:
            ride.end(ride_in, ride_out, sems, jnp.logical_and(b == nb - 1, i == nchunk - 1))

    stage = _view_scratch(list(rows) + list(outs), t)
    out_shape = [o.arr for o in outs]
    out_specs = [_row_spec(o, t, nchunk, False) for o in outs]
    if save:
        for cs in carries:
            out_shape.append(jax.ShapeDtypeStruct((nb, nchunk) + tuple(cs), F32))
            out_specs.append(pl.BlockSpec((1, 1) + tuple(cs), lambda b, i: (b, i, 0, 0)))
    res = pl.pallas_call(
        body, name=name, grid=(nb, nchunk),
        in_specs=[_row_spec(r, t, nchunk, False) for r in rows] + [_vec_spec(v) for v in vecs] + r_in,
        out_specs=out_specs + r_out, out_shape=out_shape + (list(ride.out_shapes) if ride else []),
        scratch_shapes=[pltpu.VMEM(tuple(cs), F32) for cs in carries] + stage + r_scr,
        compiler_params=_cparams(),
    )(*[r.arr for r in rows], *[v.arr for v in vecs], *(ride.ins if ride else []))
    if ride:
        RIDERS.done[name] = list(res[no + ns:])
    return list(res[:no]), list(res[no:no + ns])


def scan_bwd(name, fn, *, nb, nchunk, t, rows, vecs, carries, saved, douts, adds=None):
    adds = adds or {}
    nr, nv, nc, no = len(rows), len(vecs), len(carries), len(douts)
    dri = [k for k, r in enumerate(rows) if r.diff]
    dvi = [k for k, v in enumerate(vecs) if v.diff]
    add_keys = sorted(adds)
    na = len(add_keys)
    ride = RIDERS.take(name)
    r_in, r_out, r_scr = ride.specs() if ride else ([], [], [])

    def body(*refs):
        p = 0
        row_refs = refs[p:p + nr]; p += nr
        vec_refs = refs[p:p + nv]; p += nv
        save_refs = refs[p:p + nc]; p += nc
        dout_refs = refs[p:p + no]; p += no
        add_refs = refs[p:p + na]; p += na
        ride_in = refs[p:p + len(r_in)]; p += len(r_in)
        drow_refs = refs[p:p + len(dri)]; p += len(dri)
        dvec_refs = refs[p:p + len(dvi)]; p += len(dvi)
        ride_out = refs[p:p + len(r_out)]; p += len(r_out)
        dcar = refs[p:p + nc]; p += nc
        scr = refs[p] if stage else None
        sems = refs[p + len(stage):]
        b, ir = pl.program_id(0), pl.program_id(1)
        ci = nchunk - 1 - ir
        if ride:
            ride.begin(ride_in, ride_out, sems, jnp.logical_and(b == 0, ir == 0))
        if nc:
            @pl.when(ir == 0)
            def _():
                for c_ref in dcar:
                    c_ref[...] = jnp.zeros(c_ref.shape, F32)
        rows_v = [_load_row(ref, r, t, scr) for ref, r in zip(row_refs, rows)]
        vecs_v = [v[...] for v in vec_refs]
        cin = [s[0, 0] for s in save_refs]
        dc = [c_ref[...] for c_ref in dcar]
        dout_v = [_load_row(ref, r, t, scr).astype(F32) for ref, r in zip(dout_refs, douts)]

        def f(cs, dr, dv):
            rr, vv = list(rows_v), list(vecs_v)
            for k, idx in enumerate(dri):
                rr[idx] = dr[k]
            for k, idx in enumerate(dvi):
                vv[idx] = dv[k]
            return fn(ci, b, cs, rr, vv)

        _, vjp = jax.vjp(f, cin, [rows_v[k].astype(F32) for k in dri], [vecs_v[k].astype(F32) for k in dvi])
        dcin, drows, dvecs = vjp((dc, dout_v))
        for c_ref, cv in zip(dcar, dcin):
            c_ref[...] = cv
        for k, (o_ref, ov) in enumerate(zip(drow_refs, drows)):
            if dri[k] in adds:
                ov = ov + add_refs[add_keys.index(dri[k])][0].astype(F32)
            _store_row(o_ref, rows[dri[k]], t, scr, ov)
        for k, (o_ref, ov) in enumerate(zip(dvec_refs, dvecs)):
            first = (ir == 0) if vecs[dvi[k]].fc is not None else jnp.logical_and(ir == 0, b == 0)

            @pl.when(first)
            def _(o_ref=o_ref, ov=ov):
                o_ref[...] = ov

            @pl.when(jnp.logical_not(first))
            def _(o_ref=o_ref, ov=ov):
                o_ref[...] += ov

        if ride:
            ride.end(ride_in, ride_out, sems, jnp.logical_and(b == nb - 1, ir == nchunk - 1))

    stage = _view_scratch(list(rows) + list(douts), t)
    in_specs = ([_row_spec(r, t, nchunk, True) for r in rows] + [_vec_spec(v) for v in vecs]
                + [pl.BlockSpec((1, 1) + tuple(cs), lambda b, i: (b, nchunk - 1 - i, 0, 0)) for cs in carries]
                + [_row_spec(d, t, nchunk, True) for d in douts]
                + [_row_spec(adds[k], t, nchunk, True) for k in add_keys] + r_in)
    out_shape, out_specs = [], []
    for k in dri:
        r = rows[k]
        if r.slot:
            out_shape.append(jax.ShapeDtypeStruct((nb, r.arr.shape[1], r.w), r.ddtype))
            out_specs.append(pl.BlockSpec((1, t, r.w), lambda b, i: (b, nchunk - 1 - i, 0)))
        elif r.dcols is not None:
            out_shape.append(jax.ShapeDtypeStruct((r.arr.shape[0], r.arr.shape[1], r.dcols), r.ddtype))
            out_specs.append(pl.BlockSpec((1, t, r.w), lambda b, i, r=r: (r.fb(b), nchunk - 1 - i, r.dfc(b))))
        else:
            out_shape.append(jax.ShapeDtypeStruct(r.arr.shape, r.ddtype))
            out_specs.append(_row_spec(r, t, nchunk, True))
    for k in dvi:
        out_shape.append(jax.ShapeDtypeStruct(vecs[k].arr.shape, F32))
        out_specs.append(_vec_spec(vecs[k]))
    nd = len(dri) + len(dvi)
    res = pl.pallas_call(
        body, name=name, grid=(nb, nchunk), in_specs=in_specs, out_specs=out_specs + r_out,
        out_shape=out_shape + (list(ride.out_shapes) if ride else []),
        scratch_shapes=[pltpu.VMEM(tuple(cs), F32) for cs in carries] + stage + r_scr,
        compiler_params=_cparams(),
    )(*[r.arr for r in rows], *[v.arr for v in vecs], *saved, *[d.arr for d in douts], *[adds[k].arr for k in add_keys],
      *(ride.ins if ride else []))
    if ride:
        RIDERS.done[name] = list(res[nd:])
    return list(res[:len(dri)]), list(res[len(dri):nd])


def out_row(shape, dtype=F32, w=None, fb=None, fc=None):
    return Row(jax.ShapeDtypeStruct(shape, dtype), w, fb, fc)


def _conv(shift, halo, cur, w, bias, taps):
    y = bias
    for k in range(taps):
        y = y + _rowk(w, k) * shift(halo, cur, taps - 1 - k)
    return y


def _ssd_fn(ci, b, carries, rows, vecs):
    cx, cb_, cc, ht = carries
    z, xr, br, cr, dtr = rows
    cwx, cbx, cwb, cbb, cwc, cbc, dtb, alog, dsk, ng = vecs
    t = z.shape[0]
    xs = _silu(_conv(_shift8, cx, xr, cwx, cbx, 4))
    bm = _silu(_conv(_shift8, cb_, br, cwb, cbb, 4))
    cm = _silu(_conv(_shift8, cc, cr, cwc, cbc, 4))
    dt = _softplus(dtr + dtb)
    acol = _cumsum_rows(dt * (-jnp.exp(alog)))
    arow = acol.T
    r, c = _iota((t, t), 0), _iota((t, t), 1)
    causal = r >= c
    cbm = _bdot(cm, bm, "nt")
    lane, sub = _iota(acol.shape, 1), _iota(arow.shape, 0)
    colh = _iota(xs.shape, 1) // 64
    a, dtx, dx, acs = jnp.zeros(xs.shape, F32), jnp.zeros(xs.shape, F32), jnp.zeros((1, xs.shape[1]), F32), []
    for j in range(4):
        h = 4 * b + j
        ac = jnp.sum(jnp.where(lane == h, acol, 0.0), axis=1, keepdims=True)
        acs.append(ac)
        a = jnp.where(colh == j, ac, a)
        dtx = jnp.where(colh == j, jnp.sum(jnp.where(lane == h, dt, 0.0), axis=1, keepdims=True), dtx)
        dx = jnp.where(_iota(dx.shape, 1) // 64 == j, jnp.sum(jnp.where(_iota(dsk.shape, 1) == h, dsk, 0.0), axis=1, keepdims=True), dx)
    atot = jnp.sum(jnp.where(_iota(a.shape, 0) == t - 1, a, 0.0), axis=0, keepdims=True)
    x = xs * dtx
    ydiag = jnp.zeros(x.shape, F32)
    for j in range(4):
        ar = jnp.sum(jnp.where(sub == 4 * b + j, arow, 0.0), axis=0, keepdims=True)
        lmat = jnp.exp(jnp.where(causal, acs[j] - ar, NEG))
        ydiag = ydiag + _bdot(cbm * lmat, jnp.where(colh == j, x, 0.0), "nn")
    yoff = _bdot(cm, ht, "nn") * jnp.exp(a)
    ht_new = ht * jnp.exp(atot) + _bdot(bm, x * jnp.exp(atot - a), "tn")
    y = ydiag + yoff + dx * xs
    yz = y * _silu(z)
    yn = yz * lax.rsqrt(jnp.mean(yz * yz, axis=-1, keepdims=True) + NORM_EPS) * ng
    return [_tail8(xr), _tail8(br), _tail8(cr), ht_new], [yn]


_SSD_T = 256
_SSD_CARRIES = [(8, 256), (8, 128), (8, 128), (128, 256)]


def _ssd_io(proj3, p):
    own = lambda b: b
    rows = [Row(proj3, 256, fc=own, dcols=512, dfc=own, ddtype=BF16),
            Row(proj3, 256, fc=lambda b: 2 + b, dcols=512, dfc=own, ddtype=BF16),
            Row(proj3, 128, fc=lambda b: 8 + b, dcols=256, dfc=own, ddtype=BF16),
            Row(proj3, 128, fc=lambda b: 10 + b, dcols=256, dfc=own, ddtype=BF16),
            Row(proj3, 128, fc=lambda b: 14, slot=True)]
    vecs = [Vec(p["cw"], 256, lambda b: b), Vec(p["cb"], 256, lambda b: b),
            Vec(p["cw"], 128, lambda b: 4 + b), Vec(p["cb"], 128, lambda b: 4 + b),
            Vec(p["cw"], 128, lambda b: 6 + b), Vec(p["cb"], 128, lambda b: 6 + b),
            Vec(p["dtb"]), Vec(p["alog"]), Vec(p["dsk"]), Vec(p["ng"], 256, lambda b: b)]
    return rows, vecs


def ssd_forward(name, proj3, p):
    rows, vecs = _ssd_io(proj3, p)
    s = proj3.shape[1]
    (y,), saved = scan_fwd(name, _ssd_fn, nb=2, nchunk=s // _SSD_T, t=_SSD_T, rows=rows, vecs=vecs,
                           carries=_SSD_CARRIES, outs=[out_row((1, s, SSD_INNER), BF16, 256, fc=lambda b: b)], save=True)
    return y, saved


def ssd_backward(name, proj3, p, saved, dmix3):
    rows, vecs = _ssd_io(proj3, p)
    s = proj3.shape[1]
    drows, dvecs = scan_bwd(name, _ssd_fn, nb=2, nchunk=s // _SSD_T, t=_SSD_T, rows=rows, vecs=vecs,
                            carries=_SSD_CARRIES, saved=saved, douts=[Row(dmix3, 256, fc=lambda b: b)])
    return drows, dvecs


def _pool_fn(ci, b, carries, rows, vecs):
    (cu,) = carries
    (u,) = rows
    wbd, scale = vecs
    t = u.shape[0]
    pos = ci * t + _iota(u.shape, 0)
    grp = _iota(u.shape, 1) // 64
    acc, pooled, k = u, jnp.zeros(u.shape, F32), 1
    for gi, w in enumerate(POOL_WINDOWS):
        while k < w:
            acc = acc + _shift16(cu, u, k)
            k += 1
        pooled = jnp.where(grp == gi, acc / jnp.minimum(pos + 1, w).astype(F32), pooled)
    y = _bdot(pooled - u, wbd, "nn") * scale
    return [_tail16(u)], [y]


_POOL_T = 256


def _pool_io(proj3, wbd, scale):
    return [Row(proj3, 256, fc=lambda b: 6, dcols=256, dfc=lambda b: 0, ddtype=BF16)], [Vec(wbd), Vec(scale)]


def pool_forward(name, proj3, wbd, scale):
    rows, vecs = _pool_io(proj3, wbd, scale)
    s = proj3.shape[1]
    (y,), saved = scan_fwd(name, _pool_fn, nb=1, nchunk=s // _POOL_T, t=_POOL_T, rows=rows, vecs=vecs,
                           carries=[(16, 256)], outs=[out_row((1, s, POOL_W), BF16)], save=True)
    return y, saved


def pool_backward(name, proj3, wbd, scale, saved, dmix3):
    rows, vecs = _pool_io(proj3, wbd, scale)
    s = proj3.shape[1]
    return scan_bwd(name, _pool_fn, nb=1, nchunk=s // _POOL_T, t=_POOL_T, rows=rows, vecs=vecs,
                    carries=[(16, 256)], saved=saved, douts=[Row(dmix3, 256, fc=lambda b: 2)])


def _attn_fn(ci, b, carries, rows, vecs):
    kp, vp = carries
    qr, kr, v = _thirds(rows[0])
    scale = ATT_HEAD_DIM ** -0.5
    q = qr
    n = q.shape[0]
    r, c = _iota((n, n), 0), _iota((n, n), 1)
    prev_ok, cur_ok = jnp.logical_and(c >= r, ci > 0), r >= c
    head = _iota(q.shape, 1) // ATT_HEAD_DIM
    o, lse = jnp.zeros(q.shape, F32), jnp.zeros(q.shape, F32)
    for h in range(ATT_HEADS):
        mine = head == h
        qh = jnp.where(mine, qr, 0.0)
        sp = jnp.where(prev_ok, _bdot(qh, kp, "nt") * scale, NEG)
        sc = jnp.where(cur_ok, _bdot(qh, kr, "nt") * scale, NEG)
        m = lax.stop_gradient(jnp.maximum(jnp.max(sp, axis=1, keepdims=True), jnp.max(sc, axis=1, keepdims=True)))
        pp, pc = jnp.exp(sp - m), jnp.exp(sc - m)
        l = jnp.sum(pp, axis=1, keepdims=True) + jnp.sum(pc, axis=1, keepdims=True)
        o = jnp.where(mine, (_bdot(pp, vp, "nn") + _bdot(pc, v, "nn")) / l, o)
        lse = jnp.where(mine, m + jnp.log(l), lse)
    return [kr, v], [o, lse]


_ATT_CARRIES = [(ATT_BLOCK, ATT_W), (ATT_BLOCK, ATT_W)]


def attn_forward(name, pv, d):
    l = pv.shape[1]
    own = lambda b: b
    outs = [out_row((1, l, d * ATT_W), F32, ATT_W, fc=own) for _ in range(2)]
    (o, lse), saved = scan_fwd(name, _attn_fn, nb=d, nchunk=l // ATT_BLOCK, t=ATT_BLOCK, rows=[Row(pv, 3 * ATT_W, fc=own)],
                               vecs=[], carries=_ATT_CARRIES, outs=outs, save=True)
    return o, lse, saved


def attn_backward(name, pv, d, saved, do, dlse):
    l = pv.shape[1]
    own = lambda b: b
    (dpv,), _ = scan_bwd(name, _attn_fn, nb=d, nchunk=l // ATT_BLOCK, t=ATT_BLOCK, rows=[Row(pv, 3 * ATT_W, fc=own)], vecs=[],
                         carries=_ATT_CARRIES, saved=saved, douts=[Row(do, ATT_W, fc=own), Row(dlse, ATT_W, fc=own)])
    return dpv


def _rope_fn(ci, b, carries, rows, vecs):
    x, cs, sn = rows
    return [], [x * cs + _rot_pairs(x) * sn]


def _rope3_fn(ci, b, carries, rows, vecs):
    _, (y,) = _rope_fn(ci, b, carries, rows, vecs)
    return [], [y, y, y]


def _by_residue(a_or_shape, w, d):
    if isinstance(a_or_shape, tuple):
        _, s, _ = a_or_shape
        return Row(jax.ShapeDtypeStruct((1, s // d, d * w), F32), w, view=None if d == 1 else d)
    return Row(a_or_shape, w, view=None if d == 1 else d)


def rope_forward(name, qkv3, cs3, sn3):
    s, w = qkv3.shape[1], qkv3.shape[2]
    ys, _ = scan_fwd(name, _rope3_fn, nb=1, nchunk=s // _ROW_T, t=_ROW_T, vecs=[], carries=[], save=False,
                     rows=[Row(qkv3), Row(cs3, diff=False), Row(sn3, diff=False)],
                     outs=[_by_residue(qkv3.shape, w, d) for _, d in ATT_PATTERNS])
    return ys


def rope_backward(name, qkv3, cs3, sn3, dys):
    s, w = qkv3.shape[1], qkv3.shape[2]
    (dx,), _ = scan_bwd(name, _rope3_fn, nb=1, nchunk=s // _ROW_T, t=_ROW_T, vecs=[], carries=[], saved=[],
                        rows=[Row(qkv3, ddtype=BF16), Row(cs3, diff=False), Row(sn3, diff=False)],
                        douts=[_by_residue(a, w, d) for a, (_, d) in zip(dys, ATT_PATTERNS)])
    return dx


def _merge_fn(ci, b, carries, rows, vecs):
    o1, o2, o3, l1, l2, l3 = rows
    mx = lax.stop_gradient(jnp.maximum(l1, jnp.maximum(l2, l3)))
    e1, e2, e3 = jnp.exp(l1 - mx), jnp.exp(l2 - mx), jnp.exp(l3 - mx)
    return [], [(e1 * o1 + e2 * o2 + e3 * o3) / (e1 + e2 + e3)]


_ROW_T = 512


def _merge_rows(os_, ls_):
    ds = [d for _, d in ATT_PATTERNS]
    return [_by_residue(a, ATT_W, d) for a, d in zip(os_, ds)] + [_by_residue(a, ATT_W, d) for a, d in zip(ls_, ds)]


def merge_forward(name, os_, ls_, s):
    (y,), _ = scan_fwd(name, _merge_fn, nb=1, nchunk=s // _ROW_T, t=_ROW_T, rows=_merge_rows(os_, ls_), vecs=[],
                       carries=[], outs=[out_row((1, s, ATT_W), BF16)], save=False)
    return y


def merge_backward(name, os_, ls_, dmix3):
    s = dmix3.shape[1]
    drows, _ = scan_bwd(name, _merge_fn, nb=1, nchunk=s // _ROW_T, t=_ROW_T, rows=_merge_rows(os_, ls_), vecs=[],
                        carries=[], saved=[], douts=[Row(dmix3, 256, fc=lambda b: 3)])
    return drows


def _norm_mod_fn(ci, b, carries, rows, vecs):
    (x,) = rows
    g, sc, sh = vecs
    xn = x * lax.rsqrt(jnp.mean(x * x, axis=-1, keepdims=True) + NORM_EPS)
    return [], [xn * g * (1.0 + sc) + sh]


def norm_mod_forward(name, x3, g, sc, sh):
    s = x3.shape[1]
    (h,), _ = scan_fwd(name, _norm_mod_fn, nb=1, nchunk=s // _ROW_T, t=_ROW_T, rows=[Row(x3)], vecs=[Vec(g), Vec(sc), Vec(sh)],
                       carries=[], outs=[out_row(x3.shape, BF16)], save=False)
    return h


def norm_mod_backward(name, x3, g, sc, sh, dh3, add3):
    s = x3.shape[1]
    (dx,), dv = scan_bwd(name, _norm_mod_fn, nb=1, nchunk=s // _ROW_T, t=_ROW_T, rows=[Row(x3)], vecs=[Vec(g), Vec(sc), Vec(sh)],
                         carries=[], saved=[], douts=[Row(dh3)], adds={0: Row(add3)})
    return dx, dv


def _gate_fn(ci, b, carries, rows, vecs):
    return [], [rows[0] * vecs[0]]


def gate_backward(name, o3, g, dx3):
    s = o3.shape[1]
    (do,), (dg,) = scan_bwd(name, _gate_fn, nb=1, nchunk=s // _ROW_T, t=_ROW_T, rows=[Row(o3, ddtype=BF16)], vecs=[Vec(g)],
                            carries=[], saved=[], douts=[Row(dx3)])
    return do, dg


def _make_halves():
    @jax.custom_vjp
    def halves(x):
        h = x.shape[1] // 2
        return x[:, :h], x[:, h:]

    def fwd(x):
        return halves(x), None

    def bwd(_, g):
        return (jnp.concatenate(g, axis=1),)

    halves.defvjp(fwd, bwd)
    return halves


_halves = _make_halves()


def _ffn_fn(ci, b, carries, rows, vecs):
    (cu,) = carries
    (u,) = rows
    w, bias = vecs
    hg, hu = _halves(_conv(_shift8, cu, u, w, bias, 3))
    return [_tail8(u)], [_silu(hg) * hu]


_FFN_T = 256
_FFN_CW = FFN_DIM // 2
_FFN_CARRIES = [(8, 2 * _FFN_CW)]
FFN_BLOCK_ORDER = [0, 2, 1, 3]


def _ffn_io(up3, cw, cb):
    own = lambda b: b
    return [Row(up3, 2 * _FFN_CW, fc=own, ddtype=BF16)], [Vec(cw, 2 * _FFN_CW, own), Vec(cb, 2 * _FFN_CW, own)]


def ffn_down_forward(name, up3, cw, cb, w_down, res, gate):
    s, t, cw2 = up3.shape[1], _FFN_T, 2 * _FFN_CW
    d = w_down.shape[1]
    nchunk = s // t
    ride = RIDERS.take(name)
    r_in, r_out, r_scr = ride.specs() if ride else ([], [], [])

    def body(*refs):
        up_ref, cw_ref, cb_ref, wd_ref, res_ref, g_ref = refs[:6]
        ride_in = refs[6:6 + len(r_in)]
        act_ref, save_ref, dn_ref, x2_ref = refs[6 + len(r_in):10 + len(r_in)]
        ride_out = refs[10 + len(r_in):10 + len(r_in) + len(r_out)]
        car, acc = refs[10 + len(r_in) + len(r_out):12 + len(r_in) + len(r_out)]
        sems = refs[12 + len(r_in) + len(r_out):]
        i, b = pl.program_id(0), pl.program_id(1)
        if ride:
            ride.begin(ride_in, ride_out, sems, jnp.logical_and(i == 0, b == 0))

        @pl.when(i == 0)
        def _():
            car[b] = jnp.zeros(car.shape[1:], F32)

        cin = car[b]
        save_ref[0, 0] = cin
        (new_c,), (act,) = _ffn_fn(i, b, [cin], [up_ref[0]], [cw_ref[...], cb_ref[...]])
        car[b] = new_c
        act_ref[0] = act.astype(act_ref.dtype)
        part = _mxu(act, wd_ref[...], "nn")

        @pl.when(b == 0)
        def _():
            acc[...] = part

        @pl.when(b == 1)
        def _():
            tot = acc[...] + part
            dn_ref[...] = tot
            x2_ref[...] = res_ref[...] + g_ref[...] * tot

        if ride:
            ride.end(ride_in, ride_out, sems, jnp.logical_and(i == nchunk - 1, b == 1))

    tile = pl.BlockSpec((t, d), lambda i, b: (i, 0))
    out = pl.pallas_call(
        body, name=name, grid=(nchunk, 2),
        in_specs=[pl.BlockSpec((1, t, cw2), lambda i, b: (0, i, b)), pl.BlockSpec((cw.shape[0], cw2), lambda i, b: (0, b)),
                  pl.BlockSpec((1, cw2), lambda i, b: (0, b)), pl.BlockSpec((_FFN_CW, d), lambda i, b: (b, 0)), tile,
                  pl.BlockSpec((1, d), lambda i, b: (0, 0))] + r_in,
        out_specs=[pl.BlockSpec((1, t, _FFN_CW), lambda i, b: (0, i, b)), pl.BlockSpec((1, 1, 8, cw2), lambda i, b: (b, i, 0, 0)),
                   tile, tile] + r_out,
        out_shape=[jax.ShapeDtypeStruct((1, s, FFN_DIM), BF16), jax.ShapeDtypeStruct((2, nchunk, 8, cw2), F32),
                   jax.ShapeDtypeStruct((s, d), F32), jax.ShapeDtypeStruct((s, d), F32)] + (list(ride.out_shapes) if ride else []),
        scratch_shapes=[pltpu.VMEM((2, 8, cw2), F32), pltpu.VMEM((t, d), F32)] + r_scr,
        compiler_params=_cparams(),
    )(up3, cw, cb, w_down, res, gate, *(ride.ins if ride else []))
    if ride:
        RIDERS.done[name] = list(out[4:])
    return out[0], [out[1]], out[2], out[3]


def ffn_mid_backward(name, up3, cw, cb, saved, dact3):
    rows, vecs = _ffn_io(up3, cw, cb)
    s = up3.shape[1]
    return scan_bwd(name, _ffn_fn, nb=2, nchunk=s // _FFN_T, t=_FFN_T, rows=rows, vecs=vecs, carries=_FFN_CARRIES,
                    saved=saved, douts=[Row(dact3, _FFN_CW, fc=lambda b: b)])


def _adam_fn(ci, b, carries, rows, vecs):
    w, g, m, v = rows
    m = ADAM_B1 * m + (1.0 - ADAM_B1) * g
    v = ADAM_B2 * v + (1.0 - ADAM_B2) * (g * g)
    m_hat = m / (1.0 - ADAM_B1 ** ADAM_STEP)
    v_hat = v / (1.0 - ADAM_B2 ** ADAM_STEP)
    delta = -ADAM_LR * (m_hat / (jnp.sqrt(v_hat) + ADAM_EPS) + ADAM_WD * w)
    return [], [delta, m, v]


def adamw(name, w, g, m, v):
    shape = w.shape
    c = shape[-1]
    r = int(np.prod(shape[:-1]))
    t = _tile(r, 256, 8)
    as3 = lambda a: a.reshape(1, r, c)
    outs, _ = scan_fwd(name, _adam_fn, nb=1, nchunk=r // t, t=t, rows=[Row(as3(a)) for a in (w, g, m, v)], vecs=[], carries=[],
                       outs=[out_row((1, r, c)) for _ in range(3)], save=False)
    return [o.reshape(shape) for o in outs]


def rope_tables(positions):
    inv_freq = ROPE_THETA ** (-jnp.arange(0, ROT_DIM, 2, dtype=F32) / ROT_DIM)
    ang = positions.astype(F32)[:, None] * inv_freq
    s = positions.shape[0]
    cs = jnp.concatenate([jnp.cos(ang), jnp.cos(ang), jnp.ones((s, ATT_HEAD_DIM - ROT_DIM), F32)], axis=1)
    sn = jnp.concatenate([jnp.sin(ang), jnp.sin(ang), jnp.zeros((s, ATT_HEAD_DIM - ROT_DIM), F32)], axis=1)
    cs3 = jnp.concatenate([jnp.tile(cs, (1, 2 * ATT_HEADS)), jnp.ones((s, ATT_W), F32)], axis=1)
    sn3 = jnp.concatenate([jnp.tile(sn, (1, 2 * ATT_HEADS)), jnp.zeros((s, ATT_W), F32)], axis=1)
    return cs3[None], sn3[None]


def attention_forward(lname, qkv3, cs3, sn3):
    s = qkv3.shape[1]
    rotated = rope_forward(f"{lname}_rope", qkv3, cs3, sn3)
    os_, ls_, keep = [], [], []
    for pi, (_, d) in enumerate(ATT_PATTERNS):
        o, lse, saved = attn_forward(f"{lname}_attn{pi}", rotated[pi], d)
        os_.append(o)
        ls_.append(lse)
        keep.append(saved)
    y = merge_forward(f"{lname}_merge", os_, ls_, s)
    return y, (rotated, os_, ls_, keep)


def attention_backward(lname, qkv3, cs3, sn3, res, dmix3):
    rotated, os_, ls_, keep = res
    dm = merge_backward(f"{lname}_merge_b", os_, ls_, dmix3)
    dys = [attn_backward(f"{lname}_attn{pi}_b", rotated[pi], d, keep[pi], dm[pi], dm[3 + pi]) for pi, (_, d) in enumerate(ATT_PATTERNS)]
    return rope_backward(f"{lname}_rope_b", qkv3, cs3, sn3, dys)


def final_loss(name, x3, t3, g):
    s, d = x3.shape[1], x3.shape[2]
    t = _ROW_T

    def body(x_ref, t_ref, g_ref, loss_ref, dx_ref, dg_ref):
        i = pl.program_id(0)
        tv = t_ref[0]

        def f(x, gg):
            y = x * lax.rsqrt(jnp.mean(x * x, axis=-1, keepdims=True) + NORM_EPS) * gg
            e = y - tv
            return 0.5 * jnp.sum(jnp.mean(e * e, axis=-1, keepdims=True), axis=0, keepdims=True)

        l, vjp = jax.vjp(f, x_ref[0], g_ref[...])
        dx, dg = vjp(jnp.ones((1, 1), F32))
        dx_ref[0] = dx

        @pl.when(i == 0)
        def _():
            loss_ref[...] = jnp.zeros(loss_ref.shape, F32)
            dg_ref[...] = jnp.zeros(dg_ref.shape, F32)

        loss_ref[...] += jnp.broadcast_to(l, loss_ref.shape)
        dg_ref[...] += dg

    row = pl.BlockSpec((1, t, d), lambda i: (0, i, 0))
    vec = pl.BlockSpec((1, d), lambda i: (0, 0))
    return pl.pallas_call(
        body, name=name, grid=(s // t,), in_specs=[row, row, vec],
        out_specs=[pl.BlockSpec((8, 128), lambda i: (0, 0)), row, vec],
        out_shape=[jax.ShapeDtypeStruct((8, 128), F32), jax.ShapeDtypeStruct(x3.shape, F32), jax.ShapeDtypeStruct((1, d), F32)],
        compiler_params=pltpu.CompilerParams(dimension_semantics=("arbitrary",), vmem_limit_bytes=VMEM_LIMIT_BYTES),
    )(x3, t3, g)


_ADA_TN = 512


def ada_forward(name, c16, ada_w):
    depth, d, cols = ada_w.shape

    def body(c_ref, w_ref, o_ref):
        o_ref[0] = _mxu(_silu(c_ref[...]), w_ref[0], "nn")

    return pl.pallas_call(
        body, name=name, grid=(depth, cols // _ADA_TN),
        in_specs=[pl.BlockSpec((16, d), lambda l, j: (0, 0)), pl.BlockSpec((1, d, _ADA_TN), lambda l, j: (l, 0, j))],
        out_specs=pl.BlockSpec((1, 16, _ADA_TN), lambda l, j: (l, 0, j)),
        out_shape=jax.ShapeDtypeStruct((depth, 16, cols), F32),
        compiler_params=pltpu.CompilerParams(dimension_semantics=("arbitrary", "arbitrary"), vmem_limit_bytes=VMEM_LIMIT_BYTES),
    )(c16, ada_w)


def ada_backward(name, c16, dmod16, w, m, v):
    depth, d, cols = w.shape

    def body(c_ref, dm_ref, w_ref, m_ref, v_ref, g_ref, dl_ref, nm_ref, nv_ref):
        g = _mxu(_silu(c_ref[...]), dm_ref[0], "tn")
        _, (delta, nm, nv) = _adam_fn(None, None, [], [w_ref[0], g, m_ref[0], v_ref[0]], [])
        g_ref[0], dl_ref[0], nm_ref[0], nv_ref[0] = g, delta, nm, nv

    blk = pl.BlockSpec((1, d, _ADA_TN), lambda l, j: (l, 0, j))
    return pl.pallas_call(
        body, name=name, grid=(depth, cols // _ADA_TN),
        in_specs=[pl.BlockSpec((16, d), lambda l, j: (0, 0)), pl.BlockSpec((1, 16, _ADA_TN), lambda l, j: (l, 0, j)), blk, blk, blk],
        out_specs=[blk] * 4, out_shape=[jax.ShapeDtypeStruct(w.shape, F32)] * 4,
        compiler_params=pltpu.CompilerParams(dimension_semantics=("arbitrary", "arbitrary"), vmem_limit_bytes=VMEM_LIMIT_BYTES),
    )(c16, dmod16, w, m, v)


def _sum_fn(ci, b, carries, rows, vecs):
    acc = rows[0].astype(F32)
    for r in rows[1:]:
        acc = acc + r.astype(F32)
    return [], [acc]


def sum_slots(name, a, nsum, out_dtype=F32):
    n, r, c = a.shape
    nb = n // nsum
    t = _tile(r, 256, 8)
    rows = [Row(a, fb=(lambda b, k=k: k * nb + b)) for k in range(nsum)]
    (out,), _ = scan_fwd(name, _sum_fn, nb=nb, nchunk=r // t, t=t, rows=rows, vecs=[], carries=[],
                         outs=[out_row((nb, r, c), out_dtype, fb=lambda b: b)], save=False)
    return out


def _flip(mask, pos):
    return tuple((1 - p) if m else p for m, p in zip(mask, pos))


ALL_PEERS = [(a, b, c) for a in (0, 1) for b in (0, 1) for c in (0, 1)][1:]
CHIP_PEERS = [(1, 0, 0), (0, 1, 0), (1, 1, 0)]
SIBLING = [(0, 0, 1)]


def _dev(pos):
    return 4 * pos[0] + 2 * pos[1] + pos[2]


def _chip(pos):
    return 2 * pos[0] + pos[1]


def allgather8(name, a):
    (out,) = ride_alone(name, allgather8_ride(a))
    return _with_own(out, a)


def _rows_of(shape):
    return -(-int(np.prod(shape)) // 1024) * 8


def _pack(arrs):
    parts = []
    for a in arrs:
        flat = a.reshape(-1).astype(F32)
        parts.append(jnp.pad(flat, (0, _rows_of(a.shape) * 128 - flat.shape[0])).reshape(-1, 128))
    rows = sum(p.shape[0] for p in parts)
    parts.append(jnp.zeros(((-rows) % _ROW_T, 128), F32))
    return jnp.concatenate(parts, axis=0)


def _unpack(buf, shapes):
    out, o = [], 0
    for s in shapes:
        r, n = _rows_of(s), int(np.prod(s))
        out.append(buf[o:o + r].reshape(-1)[:n].reshape(s))
        o += r
    return out


_WEIGHTS = ["ada_w", "ada_b", "norm1_g", "w_in", "ssd_conv_w", "ssd_conv_b", "ssd_dt_bias", "ssd_a_log", "ssd_d", "ssd_norm_g",
            "pool_w", "pool_scale", "w_out", "norm2_g", "ffn_up", "ffn_conv_w", "ffn_conv_b", "ffn_down", "final_g"]
_BIG = ["w_in", "w_out", "ffn_up", "ffn_down"]
_SMALL = [n for n in _WEIGHTS if n not in _BIG and n != "ada_w"]
_COL_SHARDED_SMALL = {"ssd_conv_w": 256, "ffn_conv_w": 1408}


def _pad_lanes(v, n=128):
    return jnp.pad(v.astype(F32), (0, n - v.shape[0]))[None]


_CHIP2_PARTS = [(1284, 1536), (1792, 1800), (1536, 1792), (IN_MAIN, IN_MAIN + 126)]


def _w_in_chip_cols(gp):
    q = IN_W // 4
    return [gp[:, :q], gp[:, q:2 * q], jnp.concatenate([gp[:, a:b] for a, b in _CHIP2_PARTS], axis=1), gp[:, IN_WP - q:]]


def _ffn_block_perm(a):
    n = a.shape[-1] // 4
    return jnp.concatenate([a[..., j * n:(j + 1) * n] for j in FFN_BLOCK_ORDER], axis=-1)


def _layer_forward(i, x3, modv, wts, sp, cs3, sn3):
    sh1, sc1, g1, sh2, sc2, g2 = modv
    big = lambda n: wts[n]() if callable(wts[n]) else wts[n]
    h1 = norm_mod_forward(f"l{i}_norm1", x3, wts["norm1_g"], sc1, sh1)
    proj3 = mm(f"l{i}_proj", h1[0], big("w_in")[:, :IN_MAIN], "nn")[None]
    qkv3 = mm(f"l{i}_qkv", h1[0], big("w_in")[:, IN_MAIN:], "nn")[None]
    y_ssd, sv_ssd = ssd_forward(f"l{i}_ssd", proj3, sp)
    y_pool, sv_pool = pool_forward(f"l{i}_pool", proj3, wts["wbd"], wts["pool_scale"])
    y_att, res_att = attention_forward(f"l{i}", qkv3, cs3, sn3)
    mix = jnp.concatenate([y_ssd, y_pool, y_att], axis=-1)
    out, x1 = mm(f"l{i}_wout", mix[0], big("w_out"), "nn", res=x3[0], gate=g1)
    x1 = x1[None]
    h2 = norm_mod_forward(f"l{i}_norm2", x1, wts["norm2_g"], sc2, sh2)
    up3 = mm(f"l{i}_up", h2[0], big("ffn_up"), "nn")[None]
    act, sv_ffn, dn, x2 = ffn_down_forward(f"l{i}_down", up3, wts["ffn_conv_w"], wts["ffn_conv_b"], big("ffn_down"), x1[0], g2)
    keep = dict(x=x3, h1=h1, proj3=proj3, qkv3=qkv3, sv_ssd=sv_ssd, sv_pool=sv_pool, res_att=res_att, mix=mix, out=out[None],
                x1=x1, h2=h2, up3=up3, act=act, sv_ffn=sv_ffn, dn=dn[None])
    return x2[None], keep


def _layer_backward(i, dx2, keep, modv, wts, sp, cs3, sn3, after=None):
    sh1, sc1, g1, sh2, sc2, g2 = modv
    k = keep
    big = lambda n: wts[n]() if callable(wts[n]) else wts[n]
    tell = lambda step, *a: after[step](*a) if after and step in after else None
    d_dn, d_g2 = gate_backward(f"l{i}_gate2_b", k["dn"], g2, dx2)
    d_act = mm(f"l{i}_down_bx", d_dn[0], big("ffn_down"), "nt")
    g_down = mm(f"l{i}_down_bw", k["act"][0], d_dn[0], "tn", BF16).reshape(4, FFN_DIM // 4, D_MODEL)
    (d_up,), dv_ffn = ffn_mid_backward(f"l{i}_ffn_b", k["up3"], wts["ffn_conv_w"], wts["ffn_conv_b"], k["sv_ffn"], d_act[None])
    tell("ffn_b")
    d_h2 = mm(f"l{i}_up_bx", d_up[0], big("ffn_up"), "nt")
    g_up = mm(f"l{i}_up_bw", k["h2"][0], d_up[0], "tn", BF16, tn=_FFN_CW,
              into=((4, D_MODEL, _FFN_CW), lambda r, c: ((c % 2) * 2 + c // 2, r, 0)))
    dx1, (d_n2, d_sc2, d_sh2) = norm_mod_backward(f"l{i}_norm2_b", k["x1"], wts["norm2_g"], sc2, sh2, d_h2[None], dx2)
    d_out, d_g1 = gate_backward(f"l{i}_gate1_b", k["out"], g1, dx1)
    d_mix = mm(f"l{i}_wout_bx", d_out[0], big("w_out"), "nt")[None]
    g_wout = mm(f"l{i}_wout_bw", k["mix"][0], d_out[0], "tn", BF16).reshape(4, D_MODEL // 4, D_MODEL)
    tell("wout_bw", g_wout, g_up, g_down)
    (dz, dxs, dbm, dcm, ddt), dv_ssd = ssd_backward(f"l{i}_ssd_b", k["proj3"], sp, k["sv_ssd"], d_mix)
    tell("ssd_b")
    (du_pool,), (d_wbd, d_pscale) = pool_backward(f"l{i}_pool_b", k["proj3"], wts["wbd"], wts["pool_scale"], k["sv_pool"], d_mix)
    d_qkv = attention_backward(f"l{i}", k["qkv3"], cs3, sn3, k["res_att"], d_mix)
    d_proj = jnp.concatenate([dz[0], dxs[0], dbm[0], dcm[0], du_pool[0], (ddt[0] + ddt[1]).astype(BF16), d_qkv[0]], axis=-1)
    g_win = jnp.stack(_w_in_chip_cols(mm(f"l{i}_proj_bw", k["h1"][0], d_proj, "tn", BF16)))
    tell("proj_bw", g_win)
    d_h1 = mm(f"l{i}_proj_bx", d_proj, big("w_in"), "nt")
    tell("proj_bx")
    dx, (d_n1, d_sc1, d_sh1) = norm_mod_backward(f"l{i}_norm1_b", k["x"], wts["norm1_g"], sc1, sh1, d_h1[None], dx1)
    dcwx, dcbx, dcwb, dcbb, dcwc, dcbc, ddtb, dalog, ddsk, dng = dv_ssd
    small = dict(
        norm1_g=d_n1[0], norm2_g=d_n2[0],
        ssd_conv_w=jnp.concatenate([dcwx[:, :512], dcwb[:, 512:768], dcwc[:, 768:]], axis=1),
        ssd_conv_b=jnp.concatenate([dcbx[0, :512], dcbb[0, 512:768], dcbc[0, 768:]]),
        ssd_dt_bias=ddtb[0, :8], ssd_a_log=dalog[0, :8], ssd_d=ddsk[0, :8], ssd_norm_g=dng[0],
        pool_w=jnp.stack([d_wbd[64 * g:64 * g + 64, 64 * g:64 * g + 64] for g in range(4)]), pool_scale=d_pscale[0],
        ffn_conv_w=_ffn_block_perm(dv_ffn[0]), ffn_conv_b=_ffn_block_perm(dv_ffn[1][0]),
    )
    dmod = jnp.concatenate([d_sh1[0], d_sc1[0], d_g1[0], d_sh2[0], d_sc2[0], d_g2[0]])
    return dx, [g_win, g_wout, g_up, g_down], small, dmod


def kernel(x, c, positions, ada_w, ada_b, norm1_g, w_in, ssd_conv_w, ssd_conv_b, ssd_dt_bias, ssd_a_log, ssd_d, ssd_norm_g, pool_w, pool_scale, w_out, norm2_g, ffn_up, ffn_conv_w, ffn_conv_b, ffn_down, final_g, loss_target, m_ada_w, m_ada_b, m_norm1_g, m_w_in, m_ssd_conv_w, m_ssd_conv_b, m_ssd_dt_bias, m_ssd_a_log, m_ssd_d, m_ssd_norm_g, m_pool_w, m_pool_scale, m_w_out, m_norm2_g, m_ffn_up, m_ffn_conv_w, m_ffn_conv_b, m_ffn_down, m_final_g, v_ada_w, v_ada_b, v_norm1_g, v_w_in, v_ssd_conv_w, v_ssd_conv_b, v_ssd_dt_bias, v_ssd_a_log, v_ssd_d, v_ssd_norm_g, v_pool_w, v_pool_scale, v_w_out, v_norm2_g, v_ffn_up, v_ffn_conv_w, v_ffn_conv_b, v_ffn_down, v_final_g):
    args = dict(locals())
    w = {n: args[n] for n in _WEIGHTS}
    m = {n: args["m_" + n] for n in _WEIGHTS}
    v = {n: args["v_" + n] for n in _WEIGHTS}
    d = D_MODEL
    me = (lax.axis_index("x"), lax.axis_index("y"), lax.axis_index("c"))
    chip, dev = _chip(me), _dev(me)
    RIDERS.reset()

    shapes0 = [c.shape, ssd_conv_w.shape, ffn_conv_w.shape]
    pack0 = _pack([c, ssd_conv_w, ffn_conv_w])
    shards = [w[n].astype(BF16) for n in _BIG]
    g0 = allgather8("gather_c_conv", pack0)
    c16 = jnp.pad(g0[:, :d // 128, :].reshape(8, d), ((0, 8), (0, 0)))
    by_chip = [_unpack(g0[2 * j], shapes0) for j in range(4)]
    conv_w_full = jnp.concatenate([p[1] for p in by_chip], axis=-1)
    fconv_w_full = jnp.concatenate([p[2] for p in by_chip], axis=-1)

    modp = ada_forward("ada_fwd", c16, ada_w)[:, :8]
    pack1 = _pack([modp])
    g1, w_in0 = ride_alone("gather_mod_w_in0", merge_rides([allgather8_ride(pack1), gather_ride(0, [shards[0]])]))
    g1 = _with_own(g1, pack1)
    modfull = jnp.concatenate([_unpack(g1[2 * j], [modp.shape])[0] for j in range(4)], axis=-1)
    mod = lax.dynamic_index_in_dim(modfull, dev, axis=1, keepdims=False) + ada_b
    modv = [[mod[i, q * d:(q + 1) * d][None] for q in range(6)] for i in range(DEPTH)]


    def weight(k, layer, got):
        full = lax.dynamic_update_slice(got, shards[k][layer][None], (chip, 0, 0))
        if k == 0:
            return _w_in_from_chips(full)
        if k == 2:
            return jnp.concatenate([full[j] for j in FFN_BLOCK_ORDER], axis=1)
        return full.reshape(-1, full.shape[2])

    def later(k, layer, *sources):
        made = []

        def get():
            if not made:
                got = [RIDERS.result(host)[pos] for host, pos in sources]
                made.append(weight(k, layer, got[0] if len(got) == 1 else jnp.concatenate(got, axis=1)))
            return made[0]
        return get

    cs3, sn3 = rope_tables(positions[0])
    eye4 = jnp.eye(4, dtype=F32)
    wts, sps = [], []
    for i in range(DEPTH):
        wts.append(dict(
            norm1_g=norm1_g[i][None], norm2_g=norm2_g[i][None], pool_scale=pool_scale[i][None],
            wbd=(eye4[:, None, :, None] * pool_w[i][:, :, None, :]).reshape(POOL_W, POOL_W),
            ffn_conv_w=_ffn_block_perm(fconv_w_full[i]), ffn_conv_b=_ffn_block_perm(ffn_conv_b[i])[None]))
        sps.append(dict(cw=conv_w_full[i], cb=ssd_conv_b[i][None], dtb=_pad_lanes(ssd_dt_bias[i]), alog=_pad_lanes(ssd_a_log[i]),
                        dsk=_pad_lanes(ssd_d[i]), ng=ssd_norm_g[i][None]))

    RIDERS.book("l0_ssd", gather_ride(0, [shards[1], shards[3]]))
    half = shards[2].shape[1] // 2
    RIDERS.book("l0_attn0", gather_ride(0, [shards[2][:, :half]]))
    RIDERS.book("l0_attn1", gather_ride(0, [shards[2][:, half:]]))
    wts[0].update(w_in=weight(0, 0, w_in0), w_out=later(1, 0, ("l0_ssd", 0)), ffn_down=later(3, 0, ("l0_ssd", 1)),
                  ffn_up=later(2, 0, ("l0_attn0", 0), ("l0_attn1", 0)))
    RIDERS.book("l0_attn2", gather_ride(1, [shards[0], shards[1]]))
    RIDERS.book("l0_up", gather_ride(1, [shards[3]]))
    RIDERS.book("l0_down", gather_ride(1, [shards[2]]))
    wts[1].update(w_in=later(0, 1, ("l0_attn2", 0)), w_out=later(1, 1, ("l0_attn2", 1)), ffn_up=later(2, 1, ("l0_down", 0)),
                  ffn_down=later(3, 1, ("l0_up", 0)))
    x1_, keep0 = _layer_forward(0, x, modv[0], wts[0], sps[0], cs3, sn3)
    xc, keep1 = _layer_forward(1, x1_, modv[1], wts[1], sps[1], cs3, sn3)
    keeps = [keep0, keep1]
    lossblk, dx, d_final = final_loss("final_loss", xc, loss_target, final_g[None])
    loss = lax.psum(lossblk[0, 0], ("x", "y", "c"))

    small_g, dmods = [None] * DEPTH, [None] * DEPTH
    part_sum, from_chips = [[None] * 4 for _ in range(DEPTH)], [[None] * 4 for _ in range(DEPTH)]

    def owner_sum(layer, ks, mine, theirs):
        for k, g, t in zip(ks, mine, theirs):
            part_sum[layer][k] = add_arrays(f"sum_cores{layer}_{_BIG[k]}", [g, t], BF16)

    dx, by_chip1, small_g[1], dmods[1] = _layer_backward(1, dx, keeps[1], modv[1], wts[1], sps[1], cs3, sn3)
    RIDERS.book("l0_ffn_b", to_owner_ride(1, by_chip1))

    def after_ffn_b():
        owner_sum(1, range(4), by_chip1, RIDERS.result("l0_ffn_b"))
        RIDERS.book("l0_up_bx", scatter_ride(1, [part_sum[1][2]]))
        RIDERS.book("l0_up_bw", scatter_ride(1, [part_sum[1][0], part_sum[1][1]]))
        RIDERS.book("l0_norm2_b", scatter_ride(1, [part_sum[1][3]]))

    early = []

    def after_wout_bw(g_wout, g_up, g_down):
        early.extend([g_wout, g_up, g_down])
        RIDERS.book("l0_ssd_b", to_owner_ride(0, early))

    def after_ssd_b():
        owner_sum(0, [1, 2, 3], early, RIDERS.result("l0_ssd_b"))
        for host, k in (("l0_attn0_b", 2), ("l0_attn1_b", 3), ("l0_attn2_b", 1)):
            RIDERS.book(host, scatter_ride(0, [part_sum[0][k]]))

    last = []

    def after_proj_bw(g_win):
        last.append(g_win)
        RIDERS.book("l0_proj_bx", to_owner_ride(0, last))

    def after_proj_bx():
        owner_sum(0, [0], last, RIDERS.result("l0_proj_bx"))
        RIDERS.book("l0_norm1_b", scatter_ride(0, [part_sum[0][0]]))

    hooks = dict(ffn_b=after_ffn_b, wout_bw=after_wout_bw, ssd_b=after_ssd_b, proj_bw=after_proj_bw, proj_bx=after_proj_bx)
    dx, _, small_g[0], dmods[0] = _layer_backward(0, dx, keeps[0], modv[0], wts[0], sps[0], cs3, sn3, after=hooks)
    from_chips[1][2], (from_chips[1][0], from_chips[1][1]) = RIDERS.result("l0_up_bx")[0], RIDERS.result("l0_up_bw")
    from_chips[1][3] = RIDERS.result("l0_norm2_b")[0]
    for host, k in (("l0_attn0_b", 2), ("l0_attn1_b", 3), ("l0_attn2_b", 1), ("l0_norm1_b", 0)):
        from_chips[0][k] = RIDERS.result(host)[0]
    mine = [sum_chips_mine(f"sum_chips_{n}", part_sum[0][k], from_chips[0][k], part_sum[1][k], from_chips[1][k])
            for k, n in enumerate(_BIG)]

    part = dict(ada_b=jnp.stack(dmods), final_g=d_final[0])
    for n in _SMALL:
        if n not in part:
            part[n] = jnp.stack([small_g[i][n] for i in range(DEPTH)])
    full_shapes = [part[n].shape for n in _SMALL]
    pack_small = _pack([part[n] for n in _SMALL])
    *theirs, gs = ride_alone("swap_r_gather_small", merge_rides([swap_ride(mine), allgather8_ride(pack_small)]))
    grads = {n: jnp.stack([jnp.where(me[2] == 0, a, g), jnp.where(me[2] == 0, g, a)]) for n, a, g in zip(_BIG, mine, theirs)}
    gs = _with_own(gs, pack_small)
    tot = _unpack(sum_slots("sum_small", gs, 8)[0], full_shapes)
    small_tot = dict(zip(_SMALL, tot))
    dmod_all = gs[:, :DEPTH * 6 * d // 128, :].reshape(8, DEPTH, 6 * d)
    for n, ncol in _COL_SHARDED_SMALL.items():
        small_tot[n] = lax.dynamic_slice_in_dim(small_tot[n], chip * ncol, ncol, axis=2)
    grads.update(small_tot)

    ncol = ada_w.shape[2]
    dm = lax.dynamic_slice_in_dim(dmod_all, chip * ncol, ncol, axis=2).transpose(1, 0, 2)
    upd = {}
    g_ada, *upd["ada_w"] = ada_backward("ada_bwd", c16, jnp.pad(dm, ((0, 0), (0, 8), (0, 0))), ada_w, m["ada_w"], v["ada_w"])
    grads["ada_w"] = g_ada

    for n in _BIG:
        upd[n] = adamw(f"adam_{n}", w[n], grads[n], m[n], v[n])
    shapes_s = [w[n].shape for n in _SMALL]
    packed = [_pack([src[n] for n in _SMALL]) for src in (w, grads, m, v)]
    outs_s = [_unpack(o, shapes_s) for o in adamw("adam_small", *packed)]
    for q, n in enumerate(_SMALL):
        upd[n] = [outs_s[0][q], outs_s[1][q], outs_s[2][q]]

    return (loss, dx, *[grads[n] for n in _WEIGHTS], *[upd[n][0] for n in _WEIGHTS], *[upd[n][1] for n in _WEIGHTS],
            *[upd[n][2] for n in _WEIGHTS])


def ride_alone(name, ride):
    ni, no = len(ride.ins), len(ride.out_shapes)

    def body(*refs):
        ride.begin(refs[:ni], refs[ni:ni + no], refs[ni + no:])
        ride.end(refs[:ni], refs[ni:ni + no], refs[ni + no:])

    in_specs, out_specs, scratch = ride.specs()
    return list(pl.pallas_call(body, name=name, in_specs=in_specs, out_specs=out_specs, out_shape=ride.out_shapes,
                               scratch_shapes=scratch)(*ride.ins))


def mm(name, a, b, mode, out_dtype=F32, res=None, gate=None, tm=1408, tn=1536, tk=1408, into=None):
    ride = RIDERS.take(name)
    if mode == "nn":
        (m, k), n = a.shape, b.shape[1]
    elif mode == "nt":
        (m, k), n = a.shape, b.shape[0]
    else:
        (k, m), n = a.shape, b.shape[1]
    tm, tn, tk = _tile(m, tm), _tile(n, tn), _tile(k, tk)
    ni, nj, nk = m // tm, n // tn, k // tk
    a_spec = pl.BlockSpec((tk, tm), lambda i, j, q: (q, i)) if mode == "tn" else pl.BlockSpec((tm, tk), lambda i, j, q: (i, q))
    b_spec = pl.BlockSpec((tn, tk), lambda i, j, q: (j, q)) if mode == "nt" else pl.BlockSpec((tk, tn), lambda i, j, q: (q, j))
    o_spec = pl.BlockSpec((tm, tn), lambda i, j, q: (i, j))
    fused = res is not None
    lead = 0 if into is None else len(into[0]) - 2
    first = (0,) * lead + (slice(None), slice(None))
    ins, in_specs = [a, b], [a_spec, b_spec]
    out_shape, out_specs = [jax.ShapeDtypeStruct((m, n), out_dtype)], [o_spec]
    if fused:
        ins += [res, gate]
        in_specs += [o_spec, pl.BlockSpec((1, tn), lambda i, j, q: (0, j))]
        out_shape.append(jax.ShapeDtypeStruct((m, n), F32))
        out_specs.append(o_spec)
    if into is not None:
        shape, omap = into
        out_shape = [jax.ShapeDtypeStruct(shape, out_dtype)]
        out_specs = [pl.BlockSpec((1,) * lead + (tm, tn), lambda i, j, q: omap(i, j))]
    n_in, n_out = len(ins), len(out_shape)
    scratch = [pltpu.VMEM((tm, tn), F32)]
    if ride is not None:
        r_in, r_out, r_scr = ride.specs()
        ins, in_specs = ins + list(ride.ins), in_specs + r_in
        out_shape, out_specs = out_shape + list(ride.out_shapes), out_specs + r_out
        scratch = scratch + r_scr

    def body(*refs):
        a_ref, b_ref = refs[:2]
        o_ref = refs[len(ins)]
        acc = refs[len(ins) + len(out_shape)]
        i, j, q = pl.program_id(0), pl.program_id(1), pl.program_id(2)
        at = lambda x, y, z: jnp.logical_and(jnp.logical_and(i == x, j == y), q == z)
        r_refs = (refs[n_in:len(ins)], refs[len(ins) + n_out:len(ins) + len(out_shape)], refs[len(ins) + len(out_shape) + 1:])
        if ride is not None:
            ride.begin(*r_refs, at(0, 0, 0))

        @pl.when(q == 0)
        def _():
            acc[...] = jnp.zeros(acc.shape, F32)

        acc[...] += _mxu(a_ref[...], b_ref[...], mode)

        @pl.when(q == nk - 1)
        def _():
            o_ref[first] = acc[...].astype(o_ref.dtype)
            if fused:
                refs[len(ins) + 1][...] = refs[2][...] + refs[3][...] * acc[...]

        if ride is not None:
            ride.end(*r_refs, at(ni - 1, nj - 1, nk - 1))

    sem = ("arbitrary",) * 3 if ride is not None else ("parallel", "parallel", "arbitrary")
    out = pl.pallas_call(
        body, name=name, grid=(ni, nj, nk), in_specs=in_specs, out_specs=out_specs, out_shape=out_shape, scratch_shapes=scratch,
        compiler_params=pltpu.CompilerParams(dimension_semantics=sem, vmem_limit_bytes=VMEM_LIMIT_BYTES),
    )(*ins)
    if ride is not None:
        RIDERS.done[name] = list(out[n_out:])
    return tuple(out[:n_out]) if fused else out[0]


def add_arrays(name, arrs, out_dtype=F32):
    nb, r, c = arrs[0].shape
    t = _tile(r, 256, 8)
    (out,), _ = scan_fwd(name, _sum_fn, nb=nb, nchunk=r // t, t=t, rows=[Row(a, fb=lambda b: b) for a in arrs], vecs=[], carries=[],
                         outs=[out_row((nb, r, c), out_dtype, fb=lambda b: b)], save=False)
    return out


def _sum_chips_mine_fn(ci, b, carries, rows, vecs):
    mine_layer = lax.axis_index("c")
    chip = 2 * lax.axis_index("x") + lax.axis_index("y")
    tot = None
    for j in range(4):
        own = jnp.where(mine_layer == 0, rows[j], rows[8 + j])
        sent = jnp.where(mine_layer == 0, rows[4 + j], rows[12 + j])
        term = jnp.where(chip == j, own, sent).astype(F32)
        tot = term if tot is None else tot + term
    return [], [tot]


def sum_chips_mine(name, p0, q0, p1, q1):
    _, r, c = p0.shape
    t = _tile(r, 256, 8)
    rows = [Row(a, fb=(lambda b, j=j: j)) for a in (p0, q0, p1, q1) for j in range(4)]
    (out,), _ = scan_fwd(name, _sum_chips_mine_fn, nb=1, nchunk=r // t, t=t, rows=rows, vecs=[], carries=[],
                         outs=[out_row((1, r, c))], save=False)
    return out[0]


def _remote(src, dst, send_sems, recv_sems, k, to):
    return pltpu.make_async_remote_copy(src_ref=src, dst_ref=dst, send_sem=send_sems.at[k], recv_sem=recv_sems.at[k],
                                        device_id=to, device_id_type=MESH)


def gather_ride(layer, shards):
    na = len(shards)

    def start(ins, outs, ss, rs, me):
        @pl.when(me[2] == layer)
        def _():
            for k in range(na):
                for p, mask in enumerate(CHIP_PEERS):
                    _remote(ins[k].at[layer], outs[k].at[_chip(me)], ss, rs, 6 * k + p, _flip(mask, me)).start()

    def finish(ins, outs, ss, rs, me):
        sibling = _flip(SIBLING[0], me)

        @pl.when(me[2] == layer)
        def _():
            for k in range(na):
                for p, mask in enumerate(CHIP_PEERS):
                    slot = outs[k].at[_chip(_flip(mask, me))]
                    _remote(ins[k].at[layer], slot, ss, rs, 6 * k + p, _flip(mask, me)).wait_recv()
                    _remote(slot, slot, ss, rs, 6 * k + 3 + p, sibling).start()
            for k in range(na):
                for p, mask in enumerate(CHIP_PEERS):
                    slot = outs[k].at[_chip(_flip(mask, me))]
                    _remote(ins[k].at[layer], slot, ss, rs, 6 * k + p, _flip(mask, me)).wait_send()
                    _remote(slot, slot, ss, rs, 6 * k + 3 + p, sibling).wait_send()

        @pl.when(me[2] != layer)
        def _():
            for k in range(na):
                for p, mask in enumerate(CHIP_PEERS):
                    slot = outs[k].at[_chip(_flip(mask, me))]
                    _remote(slot, slot, ss, rs, 6 * k + 3 + p, sibling).wait_recv()

    return Ride(list(shards), [jax.ShapeDtypeStruct((4,) + a.shape[1:], a.dtype) for a in shards], 6 * na, start, finish)


def scatter_ride(layer, parts):
    na = len(parts)

    def start(ins, outs, ss, rs, me):
        @pl.when(me[2] == layer)
        def _():
            for k in range(na):
                for p, mask in enumerate(CHIP_PEERS):
                    peer = _flip(mask, me)
                    _remote(ins[k].at[_chip(peer)], outs[k].at[_chip(me)], ss, rs, 3 * k + p, peer).start()

    def finish(ins, outs, ss, rs, me):
        @pl.when(me[2] == layer)
        def _():
            for k in range(na):
                for p, mask in enumerate(CHIP_PEERS):
                    peer = _flip(mask, me)
                    _remote(ins[k].at[_chip(peer)], outs[k].at[_chip(peer)], ss, rs, 3 * k + p, peer).wait_recv()
                    _remote(ins[k].at[_chip(peer)], outs[k].at[_chip(me)], ss, rs, 3 * k + p, peer).wait_send()

    return Ride(list(parts), [jax.ShapeDtypeStruct(a.shape, a.dtype) for a in parts], 3 * na, start, finish)


def to_owner_ride(layer, arrays):
    na = len(arrays)

    def start(ins, outs, ss, rs, me):
        @pl.when(me[2] != layer)
        def _():
            for k in range(na):
                _remote(ins[k], outs[k], ss, rs, k, _flip(SIBLING[0], me)).start()

    def finish(ins, outs, ss, rs, me):
        for k in range(na):
            cp = _remote(ins[k], outs[k], ss, rs, k, _flip(SIBLING[0], me))
            pl.when(me[2] != layer)(cp.wait_send)
            pl.when(me[2] == layer)(cp.wait_recv)

    return Ride(list(arrays), [jax.ShapeDtypeStruct(a.shape, a.dtype) for a in arrays], na, start, finish)


def allgather8_ride(a):
    def start(ins, outs, ss, rs, me):
        for p, mask in enumerate(ALL_PEERS):
            _remote(ins[0], outs[0].at[_dev(me)], ss, rs, p, _flip(mask, me)).start()

    def finish(ins, outs, ss, rs, me):
        for p, mask in enumerate(ALL_PEERS):
            peer = _flip(mask, me)
            _remote(ins[0], outs[0].at[_dev(peer)], ss, rs, p, peer).wait_recv()
            _remote(ins[0], outs[0].at[_dev(me)], ss, rs, p, peer).wait_send()

    return Ride([a], [jax.ShapeDtypeStruct((8,) + a.shape, a.dtype)], len(ALL_PEERS), start, finish)


def _with_own(gathered, own):
    me = _dev((lax.axis_index("x"), lax.axis_index("y"), lax.axis_index("c")))
    return jnp.where((jnp.arange(8) == me)[:, None, None], own[None], gathered)


def swap_ride(arrays):
    na = len(arrays)

    def start(ins, outs, ss, rs, me):
        for k in range(na):
            _remote(ins[k], outs[k], ss, rs, k, _flip(SIBLING[0], me)).start()

    def finish(ins, outs, ss, rs, me):
        for k in range(na):
            cp = _remote(ins[k], outs[k], ss, rs, k, _flip(SIBLING[0], me))
            cp.wait_recv()
            cp.wait_send()

    return Ride(list(arrays), [jax.ShapeDtypeStruct(a.shape, a.dtype) for a in arrays], na, start, finish)


class _Shifted:
    def __init__(self, ref, offset):
        self.ref, self.offset = ref, offset

    @property
    def at(self):
        return self

    def __getitem__(self, k):
        return self.ref.at[self.offset + k]


def merge_rides(rides):
    def spans(counts):
        out, o = [], 0
        for n in counts:
            out.append((o, o + n))
            o += n
        return out

    si, so = spans([len(r.ins) for r in rides]), spans([len(r.out_shapes) for r in rides])
    ss_ = spans([r.nsem for r in rides])

    def each(method):
        def run(ins, outs, ss, rs, me):
            for r, (i0, i1), (o0, o1), (s0, _) in zip(rides, si, so, ss_):
                getattr(r, method)(ins[i0:i1], outs[o0:o1], _Shifted(ss, s0), _Shifted(rs, s0), me)
        return run

    return Ride([a for r in rides for a in r.ins], [s for r in rides for s in r.out_shapes], sum(r.nsem for r in rides),
                each("start"), each("finish"))


def _w_in_from_chips(a):
    c2 = a[2]
    pad = jnp.zeros((c2.shape[0], IN_WP - IN_W), c2.dtype)
    return jnp.concatenate([a[0], a[1], c2[:, :252], c2[:, 260:516], c2[:, 252:260], pad, c2[:, 516:], a[3]], axis=1)
```

```python
import functools

import numpy as np
import jax
import jax.numpy as jnp
from jax import lax
from jax.experimental import pallas as pl
from jax.experimental.pallas import tpu as pltpu

F32 = jnp.float32
BF16 = jnp.bfloat16
MESH = pl.DeviceIdType.MESH

D_MODEL = 1024
DEPTH = 2
SSD_INNER = 512
POOL_W = 256
POOL_WINDOWS = (2, 4, 8, 16)
ATT_W = 256
ATT_HEADS = 4
ATT_HEAD_DIM = 64
ATT_PATTERNS = ((128, 1), (512, 4), (2048, 16))
ATT_BLOCK = 128
ROT_DIM = 16
ROPE_THETA = 500000.0
IN_W = 2568
IN_WP = 2688
IN_MAIN = 1920
FFN_DIM = 2816
NORM_EPS = 1e-6
ADAM_LR, ADAM_B1, ADAM_B2, ADAM_EPS, ADAM_WD, ADAM_STEP = 0.001, 0.9, 0.999, 1e-08, 0.01, 10

VMEM_LIMIT_BYTES = 56 * 1024 * 1024
NEG = -1e30


def _mxu(a, b, mode):
    dims = {"nn": ((1,), (0,)), "nt": ((1,), (1,)), "tn": ((0,), (0,))}[mode]
    return lax.dot_general(a.astype(BF16), b.astype(BF16), (dims, ((), ())), preferred_element_type=F32)


@functools.partial(jax.custom_vjp, nondiff_argnums=(2,))
def _bdot(a, b, mode):
    return _mxu(a, b, mode)


def _bdot_fwd(a, b, mode):
    return _mxu(a, b, mode), (a, b)


def _bdot_bwd(mode, res, g):
    a, b = res
    if mode == "nn":
        return _mxu(g, b, "nt"), _mxu(a, g, "tn")
    if mode == "nt":
        return _mxu(g, b, "nn"), _mxu(g, a, "tn")
    return _mxu(b, g, "nt"), _mxu(a, g, "nn")


_bdot.defvjp(_bdot_fwd, _bdot_bwd)


def _iota(shape, dim):
    return lax.broadcasted_iota(jnp.int32, shape, dim)


def _make_shift(h):
    @functools.partial(jax.custom_vjp, nondiff_argnums=(2,))
    def shift(halo, cur, k):
        if k == 0:
            return cur
        full = jnp.concatenate([halo, cur], axis=0)
        return pltpu.roll(full, k, 0)[h:]

    def fwd(halo, cur, k):
        return shift(halo, cur, k), None

    def bwd(k, _, g):
        t, w = g.shape
        if k == 0:
            return jnp.zeros((h, w), F32), g
        d_cur = jnp.where(_iota((t, w), 0) < t - k, pltpu.roll(g, t - k, 0), 0.0)
        top = g[:h]
        d_halo = jnp.where(_iota((h, w), 0) >= h - k, pltpu.roll(top, h - k, 0) if k < h else top, 0.0)
        return d_halo, d_cur

    shift.defvjp(fwd, bwd)
    return shift


_shift8 = _make_shift(8)
_shift16 = _make_shift(16)


def _make_tail(h):
    @jax.custom_vjp
    def tail(x):
        return x[x.shape[0] - h:]

    def fwd(x):
        return tail(x), x.shape[0]

    def bwd(t, g):
        return (jnp.concatenate([jnp.zeros((t - h, g.shape[1]), F32), g], axis=0),)

    tail.defvjp(fwd, bwd)
    return tail


_tail8 = _make_tail(8)
_tail16 = _make_tail(16)


@jax.custom_vjp
def _cumsum_rows(x):
    t = x.shape[0]
    row, s = _iota(x.shape, 0), 1
    while s < t:
        x = x + jnp.where(row >= s, pltpu.roll(x, s, 0), 0.0)
        s *= 2
    return x


def _cumsum_rows_fwd(x):
    return _cumsum_rows(x), None


def _cumsum_rows_bwd(_, g):
    t = g.shape[0]
    row, s = _iota(g.shape, 0), 1
    while s < t:
        g = g + jnp.where(row < t - s, pltpu.roll(g, t - s, 0), 0.0)
        s *= 2
    return (g,)


_cumsum_rows.defvjp(_cumsum_rows_fwd, _cumsum_rows_bwd)


@jax.custom_vjp
def _rot_pairs(t):
    e = _iota(t.shape, 1) % ATT_HEAD_DIM
    n = t.shape[1]
    return jnp.where(e < 8, -pltpu.roll(t, n - 8, 1), jnp.where(e < 16, pltpu.roll(t, 8, 1), 0.0))


def _rot_pairs_fwd(t):
    return _rot_pairs(t), None


def _rot_pairs_bwd(_, g):
    e = _iota(g.shape, 1) % ATT_HEAD_DIM
    n = g.shape[1]
    return (pltpu.roll(jnp.where(e < 8, -g, 0.0), 8, 1) + pltpu.roll(jnp.where(jnp.logical_and(e >= 8, e < 16), g, 0.0), n - 8, 1),)


_rot_pairs.defvjp(_rot_pairs_fwd, _rot_pairs_bwd)


def _make_thirds():
    @jax.custom_vjp
    def thirds(x):
        w = x.shape[1] // 3
        return x[:, :w], x[:, w:2 * w], x[:, 2 * w:]

    def fwd(x):
        return thirds(x), None

    def bwd(_, g):
        return (jnp.concatenate(g, axis=1),)

    thirds.defvjp(fwd, bwd)
    return thirds


_thirds = _make_thirds()


def _rowk(w, k):
    return jnp.sum(jnp.where(_iota(w.shape, 0) == k, w, 0.0), axis=0, keepdims=True)


def _silu(x):
    return x * (0.5 * jnp.tanh(0.5 * x) + 0.5)


def _softplus(x):
    return jnp.maximum(x, 0.0) + jnp.log(1.0 + jnp.exp(-jnp.abs(x)))


def _tile(dim, target, unit=128):
    if dim <= target:
        return dim
    best = None
    for t in range(unit, target + 1, unit):
        if dim % t == 0:
            best = t
    assert best is not None, (dim, target)
    return best


class Ride:
    def __init__(self, ins, out_shapes, nsem, start, finish):
        self.ins, self.out_shapes, self.nsem, self.start, self.finish = ins, out_shapes, nsem, start, finish

    def specs(self):
        hbm = pl.BlockSpec(memory_space=pl.ANY)
        return [hbm] * len(self.ins), [hbm] * len(self.out_shapes), [pltpu.SemaphoreType.DMA((self.nsem,))] * 2

    def begin(self, in_refs, out_refs, sems, cond=None):
        me = (lax.axis_index("x"), lax.axis_index("y"), lax.axis_index("c"))
        go = lambda: self.start(in_refs, out_refs, sems[0], sems[1], me)
        go() if cond is None else pl.when(cond)(go)

    def end(self, in_refs, out_refs, sems, cond=None):
        me = (lax.axis_index("x"), lax.axis_index("y"), lax.axis_index("c"))
        go = lambda: self.finish(in_refs, out_refs, sems[0], sems[1], me)
        go() if cond is None else pl.when(cond)(go)


class _Riders:
    def reset(self):
        self.booked, self.done = {}, {}

    def book(self, host, ride):
        assert host not in self.booked, host
        self.booked[host] = ride

    def take(self, host):
        return self.booked.pop(host, None)

    def result(self, host):
        return self.done[host]


RIDERS = _Riders()
RIDERS.reset()


class Row:
    def __init__(self, arr, w=None, fb=None, fc=None, diff=True, slot=False, dcols=None, dfc=None, ddtype=F32, view=None):
        self.ddtype = ddtype
        self.view = view
        self.arr = arr
        self.w = arr.shape[2] if w is None else w
        self.fb = (lambda b: 0) if fb is None else fb
        self.fc = (lambda b: 0) if fc is None else fc
        self.diff = diff
        self.slot = slot
        self.dcols = dcols
        self.dfc = dfc


class Vec:
    def __init__(self, arr, w=None, fc=None, diff=True):
        self.arr = arr
        self.w = arr.shape[1] if w is None else w
        self.fc = fc
        self.diff = diff


def _row_spec(r, t, nchunk, reverse):
    shape = (1, t, r.w) if r.view is None else (1, t // r.view, r.view * r.w)
    if reverse:
        return pl.BlockSpec(shape, lambda b, i, r=r: (r.fb(b), nchunk - 1 - i, r.fc(b)))
    return pl.BlockSpec(shape, lambda b, i, r=r: (r.fb(b), i, r.fc(b)))


def _load_row(ref, r, t, scr):
    if r.view is None:
        return ref[0]
    d, w = r.view, r.w
    for q in range(d):
        for j in range(w // 128):
            scr[j, pl.ds(q, t // d, stride=d), :] = ref[0, :, q * w + 128 * j:q * w + 128 * (j + 1)].astype(F32)
    return jnp.concatenate([scr[j] for j in range(w // 128)], axis=1)


def _store_row(ref, r, t, scr, val):
    if r.view is None:
        ref[0] = val.astype(ref.dtype)
        return
    d, w = r.view, r.w
    for j in range(w // 128):
        scr[j] = val[:, 128 * j:128 * (j + 1)]
    for q in range(d):
        for j in range(w // 128):
            ref[0, :, q * w + 128 * j:q * w + 128 * (j + 1)] = scr[j, pl.ds(q, t // d, stride=d), :].astype(ref.dtype)


def _view_scratch(specs, t):
    ws = [r.w for r in specs if r.view is not None]
    return [pltpu.VMEM((max(ws) // 128, t, 128), F32)] if ws else []


def _vec_spec(v):
    if v.fc is None:
        return pl.BlockSpec(v.arr.shape, lambda b, i: (0, 0))
    return pl.BlockSpec((v.arr.shape[0], v.w), lambda b, i, v=v: (0, v.fc(b)))


def _cparams():
    return pltpu.CompilerParams(dimension_semantics=("arbitrary", "arbitrary"), vmem_limit_bytes=VMEM_LIMIT_BYTES)


def scan_fwd(name, fn, *, nb, nchunk, t, rows, vecs, carries, outs, save):
    nr, nv, nc, no = len(rows), len(vecs), len(carries), len(outs)
    ns = nc if save else 0
    ride = RIDERS.take(name)
    r_in, r_out, r_scr = ride.specs() if ride else ([], [], [])

    def body(*refs):
        p = 0
        row_refs = refs[p:p + nr]; p += nr
        vec_refs = refs[p:p + nv]; p += nv
        ride_in = refs[p:p + len(r_in)]; p += len(r_in)
        out_refs = refs[p:p + no]; p += no
        save_refs = refs[p:p + ns]; p += ns
        ride_out = refs[p:p + len(r_out)]; p += len(r_out)
        car = refs[p:p + nc]; p += nc
        scr = refs[p] if stage else None
        sems = refs[p + len(stage):]
        b, i = pl.program_id(0), pl.program_id(1)
        if ride:
            ride.begin(ride_in, ride_out, sems, jnp.logical_and(b == 0, i == 0))
        if nc:
            @pl.when(i == 0)
            def _():
                for c_ref in car:
                    c_ref[...] = jnp.zeros(c_ref.shape, F32)
        cin = [c_ref[...] for c_ref in car]
        if save:
            for s_ref, cv in zip(save_refs, cin):
                s_ref[0, 0] = cv
        new_c, o = fn(i, b, cin, [_load_row(ref, r, t, scr) for ref, r in zip(row_refs, rows)], [v[...] for v in vec_refs])
        for c_ref, cv in zip(car, new_c):
            c_ref[...] = cv
        for o_ref, spec, ov in zip(out_refs, outs, o):
            _store_row(o_ref, spec, t, scr, ov)
        if ride:
            ride.end(ride_in, ride_out, sems, jnp.logical_and(b == nb - 1, i == nchunk - 1))

    stage = _view_scratch(list(rows) + list(outs), t)
    out_shape = [o.arr for o in outs]
    out_specs = [_row_spec(o, t, nchunk, False) for o in outs]
    if save:
        for cs in carries:
            out_shape.append(jax.ShapeDtypeStruct((nb, nchunk) + tuple(cs), F32))
            out_specs.append(pl.BlockSpec((1, 1) + tuple(cs), lambda b, i: (b, i, 0, 0)))
    res = pl.pallas_call(
        body, name=name, grid=(nb, nchunk),
        in_specs=[_row_spec(r, t, nchunk, False) for r in rows] + [_vec_spec(v) for v in vecs] + r_in,
        out_specs=out_specs + r_out, out_shape=out_shape + (list(ride.out_shapes) if ride else []),
        scratch_shapes=[pltpu.VMEM(tuple(cs), F32) for cs in carries] + stage + r_scr,
        compiler_params=_cparams(),
    )(*[r.arr for r in rows], *[v.arr for v in vecs], *(ride.ins if ride else []))
    if ride:
        RIDERS.done[name] = list(res[no + ns:])
    return list(res[:no]), list(res[no:no + ns])


def scan_bwd(name, fn, *, nb, nchunk, t, rows, vecs, carries, saved, douts, adds=None):
    adds = adds or {}
    nr, nv, nc, no = len(rows), len(vecs), len(carries), len(douts)
    dri = [k for k, r in enumerate(rows) if r.diff]
    dvi = [k for k, v in enumerate(vecs) if v.diff]
    add_keys = sorted(adds)
    na = len(add_keys)
    ride = RIDERS.take(name)
    r_in, r_out, r_scr = ride.specs() if ride else ([], [], [])

    def body(*refs):
        p = 0
        row_refs = refs[p:p + nr]; p += nr
        vec_refs = refs[p:p + nv]; p += nv
        save_refs = refs[p:p + nc]; p += nc
        dout_refs = refs[p:p + no]; p += no
        add_refs = refs[p:p + na]; p += na
        ride_in = refs[p:p + len(r_in)]; p += len(r_in)
        drow_refs = refs[p:p + len(dri)]; p += len(dri)
        dvec_refs = refs[p:p + len(dvi)]; p += len(dvi)
        ride_out = refs[p:p + len(r_out)]; p += len(r_out)
        dcar = refs[p:p + nc]; p += nc
        scr = refs[p] if stage else None
        sems = refs[p + len(stage):]
        b, ir = pl.program_id(0), pl.program_id(1)
        ci = nchunk - 1 - ir
        if ride:
            ride.begin(ride_in, ride_out, sems, jnp.logical_and(b == 0, ir == 0))
        if nc:
            @pl.when(ir == 0)
            def _():
                for c_ref in dcar:
                    c_ref[...] = jnp.zeros(c_ref.shape, F32)
        rows_v = [_load_row(ref, r, t, scr) for ref, r in zip(row_refs, rows)]
        vecs_v = [v[...] for v in vec_refs]
        cin = [s[0, 0] for s in save_refs]
        dc = [c_ref[...] for c_ref in dcar]
        dout_v = [_load_row(ref, r, t, scr).astype(F32) for ref, r in zip(dout_refs, douts)]

        def f(cs, dr, dv):
            rr, vv = list(rows_v), list(vecs_v)
            for k, idx in enumerate(dri):
                rr[idx] = dr[k]
            for k, idx in enumerate(dvi):
                vv[idx] = dv[k]
            return fn(ci, b, cs, rr, vv)

        _, vjp = jax.vjp(f, cin, [rows_v[k].astype(F32) for k in dri], [vecs_v[k].astype(F32) for k in dvi])
        dcin, drows, dvecs = vjp((dc, dout_v))
        for c_ref, cv in zip(dcar, dcin):
            c_ref[...] = cv
        for k, (o_ref, ov) in enumerate(zip(drow_refs, drows)):
            if dri[k] in adds:
                ov = ov + add_refs[add_keys.index(dri[k])][0].astype(F32)
            _store_row(o_ref, rows[dri[k]], t, scr, ov)
        for k, (o_ref, ov) in enumerate(zip(dvec_refs, dvecs)):
            first = (ir == 0) if vecs[dvi[k]].fc is not None else jnp.logical_and(ir == 0, b == 0)

            @pl.when(first)
            def _(o_ref=o_ref, ov=ov):
                o_ref[...] = ov

            @pl.when(jnp.logical_not(first))
            def _(o_ref=o_ref, ov=ov):
                o_ref[...] += ov

        if ride:
            ride.end(ride_in, ride_out, sems, jnp.logical_and(b == nb - 1, ir == nchunk - 1))

    stage = _view_scratch(list(rows) + list(douts), t)
    in_specs = ([_row_spec(r, t, nchunk, True) for r in rows] + [_vec_spec(v) for v in vecs]
                + [pl.BlockSpec((1, 1) + tuple(cs), lambda b, i: (b, nchunk - 1 - i, 0, 0)) for cs in carries]
                + [_row_spec(d, t, nchunk, True) for d in douts]
                + [_row_spec(adds[k], t, nchunk, True) for k in add_keys] + r_in)
    out_shape, out_specs = [], []
    for k in dri:
        r = rows[k]
        if r.slot:
            out_shape.append(jax.ShapeDtypeStruct((nb, r.arr.shape[1], r.w), r.ddtype))
            out_specs.append(pl.BlockSpec((1, t, r.w), lambda b, i: (b, nchunk - 1 - i, 0)))
        elif r.dcols is not None:
            out_shape.append(jax.ShapeDtypeStruct((r.arr.shape[0], r.arr.shape[1], r.dcols), r.ddtype))
            out_specs.append(pl.BlockSpec((1, t, r.w), lambda b, i, r=r: (r.fb(b), nchunk - 1 - i, r.dfc(b))))
        else:
            out_shape.append(jax.ShapeDtypeStruct(r.arr.shape, r.ddtype))
            out_specs.append(_row_spec(r, t, nchunk, True))
    for k in dvi:
        out_shape.append(jax.ShapeDtypeStruct(vecs[k].arr.shape, F32))
        out_specs.append(_vec_spec(vecs[k]))
    nd = len(dri) + len(dvi)
    res = pl.pallas_call(
        body, name=name, grid=(nb, nchunk), in_specs=in_specs, out_specs=out_specs + r_out,
        out_shape=out_shape + (list(ride.out_shapes) if ride else []),
        scratch_shapes=[pltpu.VMEM(tuple(cs), F32) for cs in carries] + stage + r_scr,
        compiler_params=_cparams(),
    )(*[r.arr for r in rows], *[v.arr for v in vecs], *saved, *[d.arr for d in douts], *[adds[k].arr for k in add_keys],
      *(ride.ins if ride else []))
    if ride:
        RIDERS.done[name] = list(res[nd:])
    return list(res[:len(dri)]), list(res[len(dri):nd])


def out_row(shape, dtype=F32, w=None, fb=None, fc=None):
    return Row(jax.ShapeDtypeStruct(shape, dtype), w, fb, fc)


def _conv(shift, halo, cur, w, bias, taps):
    y = bias
    for k in range(taps):
        y = y + _rowk(w, k) * shift(halo, cur, taps - 1 - k)
    return y


def _ssd_fn(ci, b, carries, rows, vecs):
    cx, cb_, cc, ht = carries
    z, xr, br, cr, dtr = rows
    cwx, cbx, cwb, cbb, cwc, cbc, dtb, alog, dsk, ng = vecs
    t = z.shape[0]
    xs = _silu(_conv(_shift8, cx, xr, cwx, cbx, 4))
    bm = _silu(_conv(_shift8, cb_, br, cwb, cbb, 4))
    cm = _silu(_conv(_shift8, cc, cr, cwc, cbc, 4))
    dt = _softplus(dtr + dtb)
    acol = _cumsum_rows(dt * (-jnp.exp(alog)))
    arow = acol.T
    r, c = _iota((t, t), 0), _iota((t, t), 1)
    causal = r >= c
    cbm = _bdot(cm, bm, "nt")
    lane, sub = _iota(acol.shape, 1), _iota(arow.shape, 0)
    colh = _iota(xs.shape, 1) // 64
    a, dtx, dx, acs = jnp.zeros(xs.shape, F32), jnp.zeros(xs.shape, F32), jnp.zeros((1, xs.shape[1]), F32), []
    for j in range(4):
        h = 4 * b + j
        ac = jnp.sum(jnp.where(lane == h, acol, 0.0), axis=1, keepdims=True)
        acs.append(ac)
        a = jnp.where(colh == j, ac, a)
        dtx = jnp.where(colh == j, jnp.sum(jnp.where(lane == h, dt, 0.0), axis=1, keepdims=True), dtx)
        dx = jnp.where(_iota(dx.shape, 1) // 64 == j, jnp.sum(jnp.where(_iota(dsk.shape, 1) == h, dsk, 0.0), axis=1, keepdims=True), dx)
    atot = jnp.sum(jnp.where(_iota(a.shape, 0) == t - 1, a, 0.0), axis=0, keepdims=True)
    x = xs * dtx
    ydiag = jnp.zeros(x.shape, F32)
    for j in range(4):
        ar = jnp.sum(jnp.where(sub == 4 * b + j, arow, 0.0), axis=0, keepdims=True)
        lmat = jnp.exp(jnp.where(causal, acs[j] - ar, NEG))
        ydiag = ydiag + _bdot(cbm * lmat, jnp.where(colh == j, x, 0.0), "nn")
    yoff = _bdot(cm, ht, "nn") * jnp.exp(a)
    ht_new = ht * jnp.exp(atot) + _bdot(bm, x * jnp.exp(atot - a), "tn")
    y = ydiag + yoff + dx * xs
    yz = y * _silu(z)
    yn = yz * lax.rsqrt(jnp.mean(yz * yz, axis=-1, keepdims=True) + NORM_EPS) * ng
    return [_tail8(xr), _tail8(br), _tail8(cr), ht_new], [yn]


_SSD_T = 256
_SSD_CARRIES = [(8, 256), (8, 128), (8, 128), (128, 256)]


def _ssd_io(proj3, p):
    own = lambda b: b
    rows = [Row(proj3, 256, fc=own, dcols=512, dfc=own, ddtype=BF16),
            Row(proj3, 256, fc=lambda b: 2 + b, dcols=512, dfc=own, ddtype=BF16),
            Row(proj3, 128, fc=lambda b: 8 + b, dcols=256, dfc=own, ddtype=BF16),
            Row(proj3, 128, fc=lambda b: 10 + b, dcols=256, dfc=own, ddtype=BF16),
            Row(proj3, 128, fc=lambda b: 14, slot=True)]
    vecs = [Vec(p["cw"], 256, lambda b: b), Vec(p["cb"], 256, lambda b: b),
            Vec(p["cw"], 128, lambda b: 4 + b), Vec(p["cb"], 128, lambda b: 4 + b),
            Vec(p["cw"], 128, lambda b: 6 + b), Vec(p["cb"], 128, lambda b: 6 + b),
            Vec(p["dtb"]), Vec(p["alog"]), Vec(p["dsk"]), Vec(p["ng"], 256, lambda b: b)]
    return rows, vecs


def ssd_forward(name, proj3, p):
    rows, vecs = _ssd_io(proj3, p)
    s = proj3.shape[1]
    (y,), saved = scan_fwd(name, _ssd_fn, nb=2, nchunk=s // _SSD_T, t=_SSD_T, rows=rows, vecs=vecs,
                           carries=_SSD_CARRIES, outs=[out_row((1, s, SSD_INNER), BF16, 256, fc=lambda b: b)], save=True)
    return y, saved


def ssd_backward(name, proj3, p, saved, dmix3):
    rows, vecs = _ssd_io(proj3, p)
    s = proj3.shape[1]
    drows, dvecs = scan_bwd(name, _ssd_fn, nb=2, nchunk=s // _SSD_T, t=_SSD_T, rows=rows, vecs=vecs,
                            carries=_SSD_CARRIES, saved=saved, douts=[Row(dmix3, 256, fc=lambda b: b)])
    return drows, dvecs


def _pool_fn(ci, b, carries, rows, vecs):
    (cu,) = carries
    (u,) = rows
    wbd, scale = vecs
    t = u.shape[0]
    pos = ci * t + _iota(u.shape, 0)
    grp = _iota(u.shape, 1) // 64
    acc, pooled, k = u, jnp.zeros(u.shape, F32), 1
    for gi, w in enumerate(POOL_WINDOWS):
        while k < w:
            acc = acc + _shift16(cu, u, k)
            k += 1
        pooled = jnp.where(grp == gi, acc / jnp.minimum(pos + 1, w).astype(F32), pooled)
    y = _bdot(pooled - u, wbd, "nn") * scale
    return [_tail16(u)], [y]


_POOL_T = 256


def _pool_io(proj3, wbd, scale):
    return [Row(proj3, 256, fc=lambda b: 6, dcols=256, dfc=lambda b: 0, ddtype=BF16)], [Vec(wbd), Vec(scale)]


def pool_forward(name, proj3, wbd, scale):
    rows, vecs = _pool_io(proj3, wbd, scale)
    s = proj3.shape[1]
    (y,), saved = scan_fwd(name, _pool_fn, nb=1, nchunk=s // _POOL_T, t=_POOL_T, rows=rows, vecs=vecs,
                           carries=[(16, 256)], outs=[out_row((1, s, POOL_W), BF16)], save=True)
    return y, saved


def pool_backward(name, proj3, wbd, scale, saved, dmix3):
    rows, vecs = _pool_io(proj3, wbd, scale)
    s = proj3.shape[1]
    return scan_bwd(name, _pool_fn, nb=1, nchunk=s // _POOL_T, t=_POOL_T, rows=rows, vecs=vecs,
                    carries=[(16, 256)], saved=saved, douts=[Row(dmix3, 256, fc=lambda b: 2)])


def _attn_fn(ci, b, carries, rows, vecs):
    kp, vp = carries
    qr, kr, v = _thirds(rows[0])
    scale = ATT_HEAD_DIM ** -0.5
    q = qr
    n = q.shape[0]
    r, c = _iota((n, n), 0), _iota((n, n), 1)
    prev_ok, cur_ok = jnp.logical_and(c >= r, ci > 0), r >= c
    head = _iota(q.shape, 1) // ATT_HEAD_DIM
    o, lse = jnp.zeros(q.shape, F32), jnp.zeros(q.shape, F32)
    for h in range(ATT_HEADS):
        mine = head == h
        qh = jnp.where(mine, qr, 0.0)
        sp = jnp.where(prev_ok, _bdot(qh, kp, "nt") * scale, NEG)
        sc = jnp.where(cur_ok, _bdot(qh, kr, "nt") * scale, NEG)
        m = lax.stop_gradient(jnp.maximum(jnp.max(sp, axis=1, keepdims=True), jnp.max(sc, axis=1, keepdims=True)))
        pp, pc = jnp.exp(sp - m), jnp.exp(sc - m)
        l = jnp.sum(pp, axis=1, keepdims=True) + jnp.sum(pc, axis=1, keepdims=True)
        o = jnp.where(mine, (_bdot(pp, vp, "nn") + _bdot(pc, v, "nn")) / l, o)
        lse = jnp.where(mine, m + jnp.log(l), lse)
    return [kr, v], [o, lse]


_ATT_CARRIES = [(ATT_BLOCK, ATT_W), (ATT_BLOCK, ATT_W)]


def attn_forward(name, pv, d):
    l = pv.shape[1]
    own = lambda b: b
    outs = [out_row((1, l, d * ATT_W), F32, ATT_W, fc=own) for _ in range(2)]
    (o, lse), saved = scan_fwd(name, _attn_fn, nb=d, nchunk=l // ATT_BLOCK, t=ATT_BLOCK, rows=[Row(pv, 3 * ATT_W, fc=own)],
                               vecs=[], carries=_ATT_CARRIES, outs=outs, save=True)
    return o, lse, saved


def attn_backward(name, pv, d, saved, do, dlse):
    l = pv.shape[1]
    own = lambda b: b
    (dpv,), _ = scan_bwd(name, _attn_fn, nb=d, nchunk=l // ATT_BLOCK, t=ATT_BLOCK, rows=[Row(pv, 3 * ATT_W, fc=own)], vecs=[],
                         carries=_ATT_CARRIES, saved=saved, douts=[Row(do, ATT_W, fc=own), Row(dlse, ATT_W, fc=own)])
    return dpv


def _rope_fn(ci, b, carries, rows, vecs):
    x, cs, sn = rows
    return [], [x * cs + _rot_pairs(x) * sn]


def _rope3_fn(ci, b, carries, rows, vecs):
    _, (y,) = _rope_fn(ci, b, carries, rows, vecs)
    return [], [y, y, y]


def _by_residue(a_or_shape, w, d):
    if isinstance(a_or_shape, tuple):
        _, s, _ = a_or_shape
        return Row(jax.ShapeDtypeStruct((1, s // d, d * w), F32), w, view=None if d == 1 else d)
    return Row(a_or_shape, w, view=None if d == 1 else d)


def rope_forward(name, qkv3, cs3, sn3):
    s, w = qkv3.shape[1], qkv3.shape[2]
    ys, _ = scan_fwd(name, _rope3_fn, nb=1, nchunk=s // _ROW_T, t=_ROW_T, vecs=[], carries=[], save=False,
                     rows=[Row(qkv3), Row(cs3, diff=False), Row(sn3, diff=False)],
                     outs=[_by_residue(qkv3.shape, w, d) for _, d in ATT_PATTERNS])
    return ys


def rope_backward(name, qkv3, cs3, sn3, dys):
    s, w = qkv3.shape[1], qkv3.shape[2]
    (dx,), _ = scan_bwd(name, _rope3_fn, nb=1, nchunk=s // _ROW_T, t=_ROW_T, vecs=[], carries=[], saved=[],
                        rows=[Row(qkv3, ddtype=BF16), Row(cs3, diff=False), Row(sn3, diff=False)],
                        douts=[_by_residue(a, w, d) for a, (_, d) in zip(dys, ATT_PATTERNS)])
    return dx


def _merge_fn(ci, b, carries, rows, vecs):
    o1, o2, o3, l1, l2, l3 = rows
    mx = lax.stop_gradient(jnp.maximum(l1, jnp.maximum(l2, l3)))
    e1, e2, e3 = jnp.exp(l1 - mx), jnp.exp(l2 - mx), jnp.exp(l3 - mx)
    return [], [(e1 * o1 + e2 * o2 + e3 * o3) / (e1 + e2 + e3)]


_ROW_T = 512


def _merge_rows(os_, ls_):
    ds = [d for _, d in ATT_PATTERNS]
    return [_by_residue(a, ATT_W, d) for a, d in zip(os_, ds)] + [_by_residue(a, ATT_W, d) for a, d in zip(ls_, ds)]


def merge_forward(name, os_, ls_, s):
    (y,), _ = scan_fwd(name, _merge_fn, nb=1, nchunk=s // _ROW_T, t=_ROW_T, rows=_merge_rows(os_, ls_), vecs=[],
                       carries=[], outs=[out_row((1, s, ATT_W), BF16)], save=False)
    return y


def merge_backward(name, os_, ls_, dmix3):
    s = dmix3.shape[1]
    drows, _ = scan_bwd(name, _merge_fn, nb=1, nchunk=s // _ROW_T, t=_ROW_T, rows=_merge_rows(os_, ls_), vecs=[],
                        carries=[], saved=[], douts=[Row(dmix3, 256, fc=lambda b: 3)])
    return drows


def _norm_mod_fn(ci, b, carries, rows, vecs):
    (x,) = rows
    g, sc, sh = vecs
    xn = x * lax.rsqrt(jnp.mean(x * x, axis=-1, keepdims=True) + NORM_EPS)
    return [], [xn * g * (1.0 + sc) + sh]


def norm_mod_forward(name, x3, g, sc, sh):
    s = x3.shape[1]
    (h,), _ = scan_fwd(name, _norm_mod_fn, nb=1, nchunk=s // _ROW_T, t=_ROW_T, rows=[Row(x3)], vecs=[Vec(g), Vec(sc), Vec(sh)],
                       carries=[], outs=[out_row(x3.shape, BF16)], save=False)
    return h


def norm_mod_backward(name, x3, g, sc, sh, dh3, add3):
    s = x3.shape[1]
    (dx,), dv = scan_bwd(name, _norm_mod_fn, nb=1, nchunk=s // _ROW_T, t=_ROW_T, rows=[Row(x3)], vecs=[Vec(g), Vec(sc), Vec(sh)],
                         carries=[], saved=[], douts=[Row(dh3)], adds={0: Row(add3)})
    return dx, dv


def _gate_fn(ci, b, carries, rows, vecs):
    return [], [rows[0] * vecs[0]]


def gate_backward(name, o3, g, dx3):
    s = o3.shape[1]
    (do,), (dg,) = scan_bwd(name, _gate_fn, nb=1, nchunk=s // _ROW_T, t=_ROW_T, rows=[Row(o3, ddtype=BF16)], vecs=[Vec(g)],
                            carries=[], saved=[], douts=[Row(dx3)])
    return do, dg


def _make_halves():
    @jax.custom_vjp
    def halves(x):
        h = x.shape[1] // 2
        return x[:, :h], x[:, h:]

    def fwd(x):
        return halves(x), None

    def bwd(_, g):
        return (jnp.concatenate(g, axis=1),)

    halves.defvjp(fwd, bwd)
    return halves


_halves = _make_halves()


def _ffn_fn(ci, b, carries, rows, vecs):
    (cu,) = carries
    (u,) = rows
    w, bias = vecs
    hg, hu = _halves(_conv(_shift8, cu, u, w, bias, 3))
    return [_tail8(u)], [_silu(hg) * hu]


_FFN_T = 256
_FFN_CW = FFN_DIM // 2
_FFN_CARRIES = [(8, 2 * _FFN_CW)]
FFN_BLOCK_ORDER = [0, 2, 1, 3]


def _ffn_io(up3, cw, cb):
    own = lambda b: b
    return [Row(up3, 2 * _FFN_CW, fc=own, ddtype=BF16)], [Vec(cw, 2 * _FFN_CW, own), Vec(cb, 2 * _FFN_CW, own)]


def ffn_down_forward(name, up3, cw, cb, w_down, res, gate):
    s, t, cw2 = up3.shape[1], _FFN_T, 2 * _FFN_CW
    d = w_down.shape[1]
    nchunk = s // t
    ride = RIDERS.take(name)
    r_in, r_out, r_scr = ride.specs() if ride else ([], [], [])

    def body(*refs):
        up_ref, cw_ref, cb_ref, wd_ref, res_ref, g_ref = refs[:6]
        ride_in = refs[6:6 + len(r_in)]
        act_ref, save_ref, dn_ref, x2_ref = refs[6 + len(r_in):10 + len(r_in)]
        ride_out = refs[10 + len(r_in):10 + len(r_in) + len(r_out)]
        car, acc = refs[10 + len(r_in) + len(r_out):12 + len(r_in) + len(r_out)]
        sems = refs[12 + len(r_in) + len(r_out):]
        i, b = pl.program_id(0), pl.program_id(1)
        if ride:
            ride.begin(ride_in, ride_out, sems, jnp.logical_and(i == 0, b == 0))

        @pl.when(i == 0)
        def _():
            car[b] = jnp.zeros(car.shape[1:], F32)

        cin = car[b]
        save_ref[0, 0] = cin
        (new_c,), (act,) = _ffn_fn(i, b, [cin], [up_ref[0]], [cw_ref[...], cb_ref[...]])
        car[b] = new_c
        act_ref[0] = act.astype(act_ref.dtype)
        part = _mxu(act, wd_ref[...], "nn")

        @pl.when(b == 0)
        def _():
            acc[...] = part

        @pl.when(b == 1)
        def _():
            tot = acc[...] + part
            dn_ref[...] = tot
            x2_ref[...] = res_ref[...] + g_ref[...] * tot

        if ride:
            ride.end(ride_in, ride_out, sems, jnp.logical_and(i == nchunk - 1, b == 1))

    tile = pl.BlockSpec((t, d), lambda i, b: (i, 0))
    out = pl.pallas_call(
        body, name=name, grid=(nchunk, 2),
        in_specs=[pl.BlockSpec((1, t, cw2), lambda i, b: (0, i, b)), pl.BlockSpec((cw.shape[0], cw2), lambda i, b: (0, b)),
                  pl.BlockSpec((1, cw2), lambda i, b: (0, b)), pl.BlockSpec((_FFN_CW, d), lambda i, b: (b, 0)), tile,
                  pl.BlockSpec((1, d), lambda i, b: (0, 0))] + r_in,
        out_specs=[pl.BlockSpec((1, t, _FFN_CW), lambda i, b: (0, i, b)), pl.BlockSpec((1, 1, 8, cw2), lambda i, b: (b, i, 0, 0)),
                   tile, tile] + r_out,
        out_shape=[jax.ShapeDtypeStruct((1, s, FFN_DIM), BF16), jax.ShapeDtypeStruct((2, nchunk, 8, cw2), F32),
                   jax.ShapeDtypeStruct((s, d), F32), jax.ShapeDtypeStruct((s, d), F32)] + (list(ride.out_shapes) if ride else []),
        scratch_shapes=[pltpu.VMEM((2, 8, cw2), F32), pltpu.VMEM((t, d), F32)] + r_scr,
        compiler_params=_cparams(),
    )(up3, cw, cb, w_down, res, gate, *(ride.ins if ride else []))
    if ride:
        RIDERS.done[name] = list(out[4:])
    return out[0], [out[1]], out[2], out[3]


def ffn_mid_backward(name, up3, cw, cb, saved, dact3):
    rows, vecs = _ffn_io(up3, cw, cb)
    s = up3.shape[1]
    return scan_bwd(name, _ffn_fn, nb=2, nchunk=s // _FFN_T, t=_FFN_T, rows=rows, vecs=vecs, carries=_FFN_CARRIES,
                    saved=saved, douts=[Row(dact3, _FFN_CW, fc=lambda b: b)])


def _adam_fn(ci, b, carries, rows, vecs):
    w, g, m, v = rows
    m = ADAM_B1 * m + (1.0 - ADAM_B1) * g
    v = ADAM_B2 * v + (1.0 - ADAM_B2) * (g * g)
    m_hat = m / (1.0 - ADAM_B1 ** ADAM_STEP)
    v_hat = v / (1.0 - ADAM_B2 ** ADAM_STEP)
    delta = -ADAM_LR * (m_hat / (jnp.sqrt(v_hat) + ADAM_EPS) + ADAM_WD * w)
    return [], [delta, m, v]


def _adam_layers_fn(ci, b, carries, rows, vecs):
    w, mine, theirs, m, v = rows
    g = jnp.where(b == lax.axis_index("c"), mine, theirs)
    _, upd = _adam_fn(ci, b, carries, [w, g, m, v], vecs)
    return [], [g] + upd


def adamw_layers(name, w, mine, theirs, m, v):
    _, r, c = w.shape
    t = _tile(r, 256, 8)
    layer = lambda b: b
    rows = [Row(w, fb=layer), Row(mine[None]), Row(theirs[None]), Row(m, fb=layer), Row(v, fb=layer)]
    outs, _ = scan_fwd(name, _adam_layers_fn, nb=2, nchunk=r // t, t=t, rows=rows, vecs=[], carries=[],
                       outs=[out_row(w.shape, fb=layer) for _ in range(4)], save=False)
    return outs


def adamw(name, w, g, m, v):
    shape = w.shape
    c = shape[-1]
    r = int(np.prod(shape[:-1]))
    t = _tile(r, 256, 8)
    as3 = lambda a: a.reshape(1, r, c)
    outs, _ = scan_fwd(name, _adam_fn, nb=1, nchunk=r // t, t=t, rows=[Row(as3(a)) for a in (w, g, m, v)], vecs=[], carries=[],
                       outs=[out_row((1, r, c)) for _ in range(3)], save=False)
    return [o.reshape(shape) for o in outs]


def rope_tables(positions):
    inv_freq = ROPE_THETA ** (-jnp.arange(0, ROT_DIM, 2, dtype=F32) / ROT_DIM)
    ang = positions.astype(F32)[:, None] * inv_freq
    s = positions.shape[0]
    cs = jnp.concatenate([jnp.cos(ang), jnp.cos(ang), jnp.ones((s, ATT_HEAD_DIM - ROT_DIM), F32)], axis=1)
    sn = jnp.concatenate([jnp.sin(ang), jnp.sin(ang), jnp.zeros((s, ATT_HEAD_DIM - ROT_DIM), F32)], axis=1)
    cs3 = jnp.concatenate([jnp.tile(cs, (1, 2 * ATT_HEADS)), jnp.ones((s, ATT_W), F32)], axis=1)
    sn3 = jnp.concatenate([jnp.tile(sn, (1, 2 * ATT_HEADS)), jnp.zeros((s, ATT_W), F32)], axis=1)
    return cs3[None], sn3[None]


def attention_forward(lname, qkv3, cs3, sn3):
    s = qkv3.shape[1]
    rotated = rope_forward(f"{lname}_rope", qkv3, cs3, sn3)
    os_, ls_, keep = [], [], []
    for pi, (_, d) in enumerate(ATT_PATTERNS):
        o, lse, saved = attn_forward(f"{lname}_attn{pi}", rotated[pi], d)
        os_.append(o)
        ls_.append(lse)
        keep.append(saved)
    y = merge_forward(f"{lname}_merge", os_, ls_, s)
    return y, (rotated, os_, ls_, keep)


def attention_backward(lname, qkv3, cs3, sn3, res, dmix3):
    rotated, os_, ls_, keep = res
    dm = merge_backward(f"{lname}_merge_b", os_, ls_, dmix3)
    dys = [attn_backward(f"{lname}_attn{pi}_b", rotated[pi], d, keep[pi], dm[pi], dm[3 + pi]) for pi, (_, d) in enumerate(ATT_PATTERNS)]
    return rope_backward(f"{lname}_rope_b", qkv3, cs3, sn3, dys)


def final_loss(name, x3, t3, g):
    s, d = x3.shape[1], x3.shape[2]
    t = _ROW_T

    def body(x_ref, t_ref, g_ref, loss_ref, dx_ref, dg_ref):
        i = pl.program_id(0)
        tv = t_ref[0]

        def f(x, gg):
            y = x * lax.rsqrt(jnp.mean(x * x, axis=-1, keepdims=True) + NORM_EPS) * gg
            e = y - tv
            return 0.5 * jnp.sum(jnp.mean(e * e, axis=-1, keepdims=True), axis=0, keepdims=True)

        l, vjp = jax.vjp(f, x_ref[0], g_ref[...])
        dx, dg = vjp(jnp.ones((1, 1), F32))
        dx_ref[0] = dx

        @pl.when(i == 0)
        def _():
            loss_ref[...] = jnp.zeros(loss_ref.shape, F32)
            dg_ref[...] = jnp.zeros(dg_ref.shape, F32)

        loss_ref[...] += jnp.broadcast_to(l, loss_ref.shape)
        dg_ref[...] += dg

    row = pl.BlockSpec((1, t, d), lambda i: (0, i, 0))
    vec = pl.BlockSpec((1, d), lambda i: (0, 0))
    return pl.pallas_call(
        body, name=name, grid=(s // t,), in_specs=[row, row, vec],
        out_specs=[pl.BlockSpec((8, 128), lambda i: (0, 0)), row, vec],
        out_shape=[jax.ShapeDtypeStruct((8, 128), F32), jax.ShapeDtypeStruct(x3.shape, F32), jax.ShapeDtypeStruct((1, d), F32)],
        compiler_params=pltpu.CompilerParams(dimension_semantics=("arbitrary",), vmem_limit_bytes=VMEM_LIMIT_BYTES),
    )(x3, t3, g)


_ADA_TN = 512


def ada_forward(name, c16, ada_w):
    depth, d, cols = ada_w.shape

    def body(c_ref, w_ref, o_ref):
        o_ref[0] = _mxu(_silu(c_ref[...]), w_ref[0], "nn")

    return pl.pallas_call(
        body, name=name, grid=(depth, cols // _ADA_TN),
        in_specs=[pl.BlockSpec((16, d), lambda l, j: (0, 0)), pl.BlockSpec((1, d, _ADA_TN), lambda l, j: (l, 0, j))],
        out_specs=pl.BlockSpec((1, 16, _ADA_TN), lambda l, j: (l, 0, j)),
        out_shape=jax.ShapeDtypeStruct((depth, 16, cols), F32),
        compiler_params=pltpu.CompilerParams(dimension_semantics=("arbitrary", "arbitrary"), vmem_limit_bytes=VMEM_LIMIT_BYTES),
    )(c16, ada_w)


def ada_backward(name, c16, dmod16, w, m, v):
    depth, d, cols = w.shape

    def body(c_ref, dm_ref, w_ref, m_ref, v_ref, g_ref, dl_ref, nm_ref, nv_ref):
        g = _mxu(_silu(c_ref[...]), dm_ref[0], "tn")
        _, (delta, nm, nv) = _adam_fn(None, None, [], [w_ref[0], g, m_ref[0], v_ref[0]], [])
        g_ref[0], dl_ref[0], nm_ref[0], nv_ref[0] = g, delta, nm, nv

    blk = pl.BlockSpec((1, d, _ADA_TN), lambda l, j: (l, 0, j))
    return pl.pallas_call(
        body, name=name, grid=(depth, cols // _ADA_TN),
        in_specs=[pl.BlockSpec((16, d), lambda l, j: (0, 0)), pl.BlockSpec((1, 16, _ADA_TN), lambda l, j: (l, 0, j)), blk, blk, blk],
        out_specs=[blk] * 4, out_shape=[jax.ShapeDtypeStruct(w.shape, F32)] * 4,
        compiler_params=pltpu.CompilerParams(dimension_semantics=("arbitrary", "arbitrary"), vmem_limit_bytes=VMEM_LIMIT_BYTES),
    )(c16, dmod16, w, m, v)


def _sum_fn(ci, b, carries, rows, vecs):
    acc = rows[0].astype(F32)
    for r in rows[1:]:
        acc = acc + r.astype(F32)
    return [], [acc]


def sum_slots(name, a, nsum, out_dtype=F32):
    n, r, c = a.shape
    nb = n // nsum
    t = _tile(r, 256, 8)
    rows = [Row(a, fb=(lambda b, k=k: k * nb + b)) for k in range(nsum)]
    (out,), _ = scan_fwd(name, _sum_fn, nb=nb, nchunk=r // t, t=t, rows=rows, vecs=[], carries=[],
                         outs=[out_row((nb, r, c), out_dtype, fb=lambda b: b)], save=False)
    return out


def _flip(mask, pos):
    return tuple((1 - p) if m else p for m, p in zip(mask, pos))


ALL_PEERS = [(a, b, c) for a in (0, 1) for b in (0, 1) for c in (0, 1)][1:]
CHIP_PEERS = [(1, 0, 0), (0, 1, 0), (1, 1, 0)]
SIBLING = [(0, 0, 1)]


def _dev(pos):
    return 4 * pos[0] + 2 * pos[1] + pos[2]


def _chip(pos):
    return 2 * pos[0] + pos[1]


def allgather8(name, a):
    (out,) = ride_alone(name, allgather8_ride(a))
    return _with_own(out, a)


def _rows_of(shape):
    return -(-int(np.prod(shape)) // 1024) * 8


def _pack(arrs):
    parts = []
    for a in arrs:
        flat = a.reshape(-1).astype(F32)
        parts.append(jnp.pad(flat, (0, _rows_of(a.shape) * 128 - flat.shape[0])).reshape(-1, 128))
    rows = sum(p.shape[0] for p in parts)
    parts.append(jnp.zeros(((-rows) % _ROW_T, 128), F32))
    return jnp.concatenate(parts, axis=0)


def _unpack(buf, shapes):
    out, o = [], 0
    for s in shapes:
        r, n = _rows_of(s), int(np.prod(s))
        out.append(buf[o:o + r].reshape(-1)[:n].reshape(s))
        o += r
    return out


_WEIGHTS = ["ada_w", "ada_b", "norm1_g", "w_in", "ssd_conv_w", "ssd_conv_b", "ssd_dt_bias", "ssd_a_log", "ssd_d", "ssd_norm_g",
            "pool_w", "pool_scale", "w_out", "norm2_g", "ffn_up", "ffn_conv_w", "ffn_conv_b", "ffn_down", "final_g"]
_BIG = ["w_in", "w_out", "ffn_up", "ffn_down"]
_SMALL = [n for n in _WEIGHTS if n not in _BIG and n != "ada_w"]
_COL_SHARDED_SMALL = {"ssd_conv_w": 256, "ffn_conv_w": 1408}


def _pad_lanes(v, n=128):
    return jnp.pad(v.astype(F32), (0, n - v.shape[0]))[None]


_CHIP2_PARTS = [(1284, 1536), (1792, 1800), (1536, 1792), (IN_MAIN, IN_MAIN + 126)]


def _w_in_chip_cols(gp):
    q = IN_W // 4
    return [gp[:, :q], gp[:, q:2 * q], jnp.concatenate([gp[:, a:b] for a, b in _CHIP2_PARTS], axis=1), gp[:, IN_WP - q:]]


def _ffn_block_perm(a):
    n = a.shape[-1] // 4
    return jnp.concatenate([a[..., j * n:(j + 1) * n] for j in FFN_BLOCK_ORDER], axis=-1)


def _layer_forward(i, x3, modv, wts, sp, cs3, sn3):
    sh1, sc1, g1, sh2, sc2, g2 = modv
    big = lambda n: wts[n]() if callable(wts[n]) else wts[n]
    h1 = norm_mod_forward(f"l{i}_norm1", x3, wts["norm1_g"], sc1, sh1)
    proj3 = mm(f"l{i}_proj", h1[0], big("w_in")[:, :IN_MAIN], "nn")[None]
    qkv3 = mm(f"l{i}_qkv", h1[0], big("w_in")[:, IN_MAIN:], "nn")[None]
    y_ssd, sv_ssd = ssd_forward(f"l{i}_ssd", proj3, sp)
    y_pool, sv_pool = pool_forward(f"l{i}_pool", proj3, wts["wbd"], wts["pool_scale"])
    y_att, res_att = attention_forward(f"l{i}", qkv3, cs3, sn3)
    mix = jnp.concatenate([y_ssd, y_pool, y_att], axis=-1)
    out, x1 = mm(f"l{i}_wout", mix[0], big("w_out"), "nn", res=x3[0], gate=g1)
    x1 = x1[None]
    h2 = norm_mod_forward(f"l{i}_norm2", x1, wts["norm2_g"], sc2, sh2)
    up3 = mm(f"l{i}_up", h2[0], big("ffn_up"), "nn")[None]
    act, sv_ffn, dn, x2 = ffn_down_forward(f"l{i}_down", up3, wts["ffn_conv_w"], wts["ffn_conv_b"], big("ffn_down"), x1[0], g2)
    keep = dict(x=x3, h1=h1, proj3=proj3, qkv3=qkv3, sv_ssd=sv_ssd, sv_pool=sv_pool, res_att=res_att, mix=mix, out=out[None],
                x1=x1, h2=h2, up3=up3, act=act, sv_ffn=sv_ffn, dn=dn[None])
    return x2[None], keep


def _layer_backward(i, dx2, keep, modv, wts, sp, cs3, sn3, after=None):
    sh1, sc1, g1, sh2, sc2, g2 = modv
    k = keep
    big = lambda n: wts[n]() if callable(wts[n]) else wts[n]
    tell = lambda step, *a: after[step](*a) if after and step in after else None
    d_dn, d_g2 = gate_backward(f"l{i}_gate2_b", k["dn"], g2, dx2)
    d_act = mm(f"l{i}_down_bx", d_dn[0], big("ffn_down"), "nt")
    g_down = mm(f"l{i}_down_bw", k["act"][0], d_dn[0], "tn", BF16).reshape(4, FFN_DIM // 4, D_MODEL)
    (d_up,), dv_ffn = ffn_mid_backward(f"l{i}_ffn_b", k["up3"], wts["ffn_conv_w"], wts["ffn_conv_b"], k["sv_ffn"], d_act[None])
    tell("ffn_b")
    d_h2 = mm(f"l{i}_up_bx", d_up[0], big("ffn_up"), "nt")
    g_up = mm(f"l{i}_up_bw", k["h2"][0], d_up[0], "tn", BF16, tn=_FFN_CW,
              into=((4, D_MODEL, _FFN_CW), lambda r, c: ((c % 2) * 2 + c // 2, r, 0)))
    dx1, (d_n2, d_sc2, d_sh2) = norm_mod_backward(f"l{i}_norm2_b", k["x1"], wts["norm2_g"], sc2, sh2, d_h2[None], dx2)
    d_out, d_g1 = gate_backward(f"l{i}_gate1_b", k["out"], g1, dx1)
    d_mix = mm(f"l{i}_wout_bx", d_out[0], big("w_out"), "nt")[None]
    g_wout = mm(f"l{i}_wout_bw", k["mix"][0], d_out[0], "tn", BF16).reshape(4, D_MODEL // 4, D_MODEL)
    tell("wout_bw", g_wout, g_up, g_down)
    (dz, dxs, dbm, dcm, ddt), dv_ssd = ssd_backward(f"l{i}_ssd_b", k["proj3"], sp, k["sv_ssd"], d_mix)
    tell("ssd_b")
    (du_pool,), (d_wbd, d_pscale) = pool_backward(f"l{i}_pool_b", k["proj3"], wts["wbd"], wts["pool_scale"], k["sv_pool"], d_mix)
    d_qkv = attention_backward(f"l{i}", k["qkv3"], cs3, sn3, k["res_att"], d_mix)
    d_proj = jnp.concatenate([dz[0], dxs[0], dbm[0], dcm[0], du_pool[0], (ddt[0] + ddt[1]).astype(BF16), d_qkv[0]], axis=-1)
    g_win = jnp.stack(_w_in_chip_cols(mm(f"l{i}_proj_bw", k["h1"][0], d_proj, "tn", BF16)))
    tell("proj_bw", g_win)
    d_h1 = mm(f"l{i}_proj_bx", d_proj, big("w_in"), "nt")
    tell("proj_bx")
    dx, (d_n1, d_sc1, d_sh1) = norm_mod_backward(f"l{i}_norm1_b", k["x"], wts["norm1_g"], sc1, sh1, d_h1[None], dx1)
    dcwx, dcbx, dcwb, dcbb, dcwc, dcbc, ddtb, dalog, ddsk, dng = dv_ssd
    small = dict(
        norm1_g=d_n1[0], norm2_g=d_n2[0],
        ssd_conv_w=jnp.concatenate([dcwx[:, :512], dcwb[:, 512:768], dcwc[:, 768:]], axis=1),
        ssd_conv_b=jnp.concatenate([dcbx[0, :512], dcbb[0, 512:768], dcbc[0, 768:]]),
        ssd_dt_bias=ddtb[0, :8], ssd_a_log=dalog[0, :8], ssd_d=ddsk[0, :8], ssd_norm_g=dng[0],
        pool_w=jnp.stack([d_wbd[64 * g:64 * g + 64, 64 * g:64 * g + 64] for g in range(4)]), pool_scale=d_pscale[0],
        ffn_conv_w=_ffn_block_perm(dv_ffn[0]), ffn_conv_b=_ffn_block_perm(dv_ffn[1][0]),
    )
    dmod = jnp.concatenate([d_sh1[0], d_sc1[0], d_g1[0], d_sh2[0], d_sc2[0], d_g2[0]])
    return dx, [g_win, g_wout, g_up, g_down], small, dmod


def kernel(x, c, positions, ada_w, ada_b, norm1_g, w_in, ssd_conv_w, ssd_conv_b, ssd_dt_bias, ssd_a_log, ssd_d, ssd_norm_g, pool_w, pool_scale, w_out, norm2_g, ffn_up, ffn_conv_w, ffn_conv_b, ffn_down, final_g, loss_target, m_ada_w, m_ada_b, m_norm1_g, m_w_in, m_ssd_conv_w, m_ssd_conv_b, m_ssd_dt_bias, m_ssd_a_log, m_ssd_d, m_ssd_norm_g, m_pool_w, m_pool_scale, m_w_out, m_norm2_g, m_ffn_up, m_ffn_conv_w, m_ffn_conv_b, m_ffn_down, m_final_g, v_ada_w, v_ada_b, v_norm1_g, v_w_in, v_ssd_conv_w, v_ssd_conv_b, v_ssd_dt_bias, v_ssd_a_log, v_ssd_d, v_ssd_norm_g, v_pool_w, v_pool_scale, v_w_out, v_norm2_g, v_ffn_up, v_ffn_conv_w, v_ffn_conv_b, v_ffn_down, v_final_g):
    args = dict(locals())
    w = {n: args[n] for n in _WEIGHTS}
    m = {n: args["m_" + n] for n in _WEIGHTS}
    v = {n: args["v_" + n] for n in _WEIGHTS}
    d = D_MODEL
    me = (lax.axis_index("x"), lax.axis_index("y"), lax.axis_index("c"))
    chip, dev = _chip(me), _dev(me)
    RIDERS.reset()

    shapes0 = [c.shape, ssd_conv_w.shape, ffn_conv_w.shape]
    pack0 = _pack([c, ssd_conv_w, ffn_conv_w])
    shards = [w[n].astype(BF16) for n in _BIG]
    g0 = allgather8("gather_c_conv", pack0)
    c16 = jnp.pad(g0[:, :d // 128, :].reshape(8, d), ((0, 8), (0, 0)))
    by_chip = [_unpack(g0[2 * j], shapes0) for j in range(4)]
    conv_w_full = jnp.concatenate([p[1] for p in by_chip], axis=-1)
    fconv_w_full = jnp.concatenate([p[2] for p in by_chip], axis=-1)

    modp = ada_forward("ada_fwd", c16, ada_w)[:, :8]
    pack1 = _pack([modp])
    g1, w_in0 = ride_alone("gather_mod_w_in0", merge_rides([allgather8_ride(pack1), gather_ride(0, [shards[0]])]))
    g1 = _with_own(g1, pack1)
    modfull = jnp.concatenate([_unpack(g1[2 * j], [modp.shape])[0] for j in range(4)], axis=-1)
    mod = lax.dynamic_index_in_dim(modfull, dev, axis=1, keepdims=False) + ada_b
    modv = [[mod[i, q * d:(q + 1) * d][None] for q in range(6)] for i in range(DEPTH)]


    def weight(k, layer, got):
        full = lax.dynamic_update_slice(got, shards[k][layer][None], (chip, 0, 0))
        if k == 0:
            return _w_in_from_chips(full)
        if k == 2:
            return jnp.concatenate([full[j] for j in FFN_BLOCK_ORDER], axis=1)
        return full.reshape(-1, full.shape[2])

    def later(k, layer, *sources):
        made = []

        def get():
            if not made:
                got = [RIDERS.result(host)[pos] for host, pos in sources]
                made.append(weight(k, layer, got[0] if len(got) == 1 else jnp.concatenate(got, axis=1)))
            return made[0]
        return get

    cs3, sn3 = rope_tables(positions[0])
    eye4 = jnp.eye(4, dtype=F32)
    wts, sps = [], []
    for i in range(DEPTH):
        wts.append(dict(
            norm1_g=norm1_g[i][None], norm2_g=norm2_g[i][None], pool_scale=pool_scale[i][None],
            wbd=(eye4[:, None, :, None] * pool_w[i][:, :, None, :]).reshape(POOL_W, POOL_W),
            ffn_conv_w=_ffn_block_perm(fconv_w_full[i]), ffn_conv_b=_ffn_block_perm(ffn_conv_b[i])[None]))
        sps.append(dict(cw=conv_w_full[i], cb=ssd_conv_b[i][None], dtb=_pad_lanes(ssd_dt_bias[i]), alog=_pad_lanes(ssd_a_log[i]),
                        dsk=_pad_lanes(ssd_d[i]), ng=ssd_norm_g[i][None]))

    RIDERS.book("l0_ssd", gather_ride(0, [shards[1], shards[3]]))
    half = shards[2].shape[1] // 2
    RIDERS.book("l0_attn0", gather_ride(0, [shards[2][:, :half]]))
    RIDERS.book("l0_attn1", gather_ride(0, [shards[2][:, half:]]))
    wts[0].update(w_in=weight(0, 0, w_in0), w_out=later(1, 0, ("l0_ssd", 0)), ffn_down=later(3, 0, ("l0_ssd", 1)),
                  ffn_up=later(2, 0, ("l0_attn0", 0), ("l0_attn1", 0)))
    RIDERS.book("l0_attn2", gather_ride(1, [shards[0], shards[1]]))
    RIDERS.book("l0_up", gather_ride(1, [shards[3]]))
    RIDERS.book("l0_down", gather_ride(1, [shards[2]]))
    wts[1].update(w_in=later(0, 1, ("l0_attn2", 0)), w_out=later(1, 1, ("l0_attn2", 1)), ffn_up=later(2, 1, ("l0_down", 0)),
                  ffn_down=later(3, 1, ("l0_up", 0)))
    x1_, keep0 = _layer_forward(0, x, modv[0], wts[0], sps[0], cs3, sn3)
    xc, keep1 = _layer_forward(1, x1_, modv[1], wts[1], sps[1], cs3, sn3)
    keeps = [keep0, keep1]
    lossblk, dx, d_final = final_loss("final_loss", xc, loss_target, final_g[None])

    small_g, dmods = [None] * DEPTH, [None] * DEPTH
    part_sum, from_chips = [[None] * 4 for _ in range(DEPTH)], [[None] * 4 for _ in range(DEPTH)]

    def owner_sum(layer, ks, mine, theirs):
        for k, g, t in zip(ks, mine, theirs):
            part_sum[layer][k] = add_arrays(f"sum_cores{layer}_{_BIG[k]}", [g, t], BF16)

    dx, by_chip1, small_g[1], dmods[1] = _layer_backward(1, dx, keeps[1], modv[1], wts[1], sps[1], cs3, sn3)
    RIDERS.book("l0_ffn_b", to_owner_ride(1, by_chip1))

    def after_ffn_b():
        owner_sum(1, range(4), by_chip1, RIDERS.result("l0_ffn_b"))
        RIDERS.book("l0_up_bx", scatter_ride(1, [part_sum[1][2]]))
        RIDERS.book("l0_up_bw", scatter_ride(1, [part_sum[1][0], part_sum[1][1]]))
        RIDERS.book("l0_norm2_b", scatter_ride(1, [part_sum[1][3]]))

    early = []

    def after_wout_bw(g_wout, g_up, g_down):
        early.extend([g_wout, g_up, g_down])
        RIDERS.book("l0_ssd_b", to_owner_ride(0, early))

    def after_ssd_b():
        owner_sum(0, [1, 2, 3], early, RIDERS.result("l0_ssd_b"))
        for host, k in (("l0_attn0_b", 2), ("l0_attn1_b", 3), ("l0_attn2_b", 1)):
            RIDERS.book(host, scatter_ride(0, [part_sum[0][k]]))

    last = []

    def after_proj_bw(g_win):
        last.append(g_win)
        RIDERS.book("l0_proj_bx", to_owner_ride(0, last))

    def after_proj_bx():
        owner_sum(0, [0], last, RIDERS.result("l0_proj_bx"))
        RIDERS.book("l0_norm1_b", scatter_ride(0, [part_sum[0][0]]))

    hooks = dict(ffn_b=after_ffn_b, wout_bw=after_wout_bw, ssd_b=after_ssd_b, proj_bw=after_proj_bw, proj_bx=after_proj_bx)
    dx, _, small_g[0], dmods[0] = _layer_backward(0, dx, keeps[0], modv[0], wts[0], sps[0], cs3, sn3, after=hooks)
    from_chips[1][2], (from_chips[1][0], from_chips[1][1]) = RIDERS.result("l0_up_bx")[0], RIDERS.result("l0_up_bw")
    from_chips[1][3] = RIDERS.result("l0_norm2_b")[0]
    for host, k in (("l0_attn0_b", 2), ("l0_attn1_b", 3), ("l0_attn2_b", 1), ("l0_norm1_b", 0)):
        from_chips[0][k] = RIDERS.result(host)[0]
    mine = [sum_chips_mine(f"sum_chips_{n}", part_sum[0][k], from_chips[0][k], part_sum[1][k], from_chips[1][k])
            for k, n in enumerate(_BIG)]

    part = dict(ada_b=jnp.stack(dmods), final_g=d_final[0])
    for n in _SMALL:
        if n not in part:
            part[n] = jnp.stack([small_g[i][n] for i in range(DEPTH)])
    full_shapes = [part[n].shape for n in _SMALL] + [(1,)]
    pack_small = _pack([part[n] for n in _SMALL] + [lossblk[0, :1]])
    *theirs, gs = ride_alone("swap_r_gather_small", merge_rides([swap_ride(mine), allgather8_ride(pack_small)]))
    gs = _with_own(gs, pack_small)
    tot = _unpack(sum_slots("sum_small", gs, 8)[0], full_shapes)
    loss = tot[-1].reshape(())
    grads = {}
    small_tot = dict(zip(_SMALL, tot))
    dmod_all = gs[:, :DEPTH * 6 * d // 128, :].reshape(8, DEPTH, 6 * d)
    for n, ncol in _COL_SHARDED_SMALL.items():
        small_tot[n] = lax.dynamic_slice_in_dim(small_tot[n], chip * ncol, ncol, axis=2)
    grads.update(small_tot)

    ncol = ada_w.shape[2]
    dm = lax.dynamic_slice_in_dim(dmod_all, chip * ncol, ncol, axis=2).transpose(1, 0, 2)
    upd = {}
    g_ada, *upd["ada_w"] = ada_backward("ada_bwd", c16, jnp.pad(dm, ((0, 0), (0, 8), (0, 0))), ada_w, m["ada_w"], v["ada_w"])
    grads["ada_w"] = g_ada

    for n, a, g in zip(_BIG, mine, theirs):
        grads[n], *upd[n] = adamw_layers(f"adam_{n}", w[n], a, g, m[n], v[n])
    shapes_s = [w[n].shape for n in _SMALL]
    packed = [_pack([src[n] for n in _SMALL]) for src in (w, grads, m, v)]
    outs_s = [_unpack(o, shapes_s) for o in adamw("adam_small", *packed)]
    for q, n in enumerate(_SMALL):
        upd[n] = [outs_s[0][q], outs_s[1][q], outs_s[2][q]]

    return (loss, dx, *[grads[n] for n in _WEIGHTS], *[upd[n][0] for n in _WEIGHTS], *[upd[n][1] for n in _WEIGHTS],
            *[upd[n][2] for n in _WEIGHTS])


def ride_alone(name, ride):
    ni, no = len(ride.ins), len(ride.out_shapes)

    def body(*refs):
        ride.begin(refs[:ni], refs[ni:ni + no], refs[ni + no:])
        ride.end(refs[:ni], refs[ni:ni + no], refs[ni + no:])

    in_specs, out_specs, scratch = ride.specs()
    return list(pl.pallas_call(body, name=name, in_specs=in_specs, out_specs=out_specs, out_shape=ride.out_shapes,
                               scratch_shapes=scratch)(*ride.ins))


def mm(name, a, b, mode, out_dtype=F32, res=None, gate=None, tm=1408, tn=1536, tk=1408, into=None):
    ride = RIDERS.take(name)
    if mode == "nn":
        (m, k), n = a.shape, b.shape[1]
    elif mode == "nt":
        (m, k), n = a.shape, b.shape[0]
    else:
        (k, m), n = a.shape, b.shape[1]
    tm, tn, tk = _tile(m, tm), _tile(n, tn), _tile(k, tk)
    ni, nj, nk = m // tm, n // tn, k // tk
    a_spec = pl.BlockSpec((tk, tm), lambda i, j, q: (q, i)) if mode == "tn" else pl.BlockSpec((tm, tk), lambda i, j, q: (i, q))
    b_spec = pl.BlockSpec((tn, tk), lambda i, j, q: (j, q)) if mode == "nt" else pl.BlockSpec((tk, tn), lambda i, j, q: (q, j))
    o_spec = pl.BlockSpec((tm, tn), lambda i, j, q: (i, j))
    fused = res is not None
    lead = 0 if into is None else len(into[0]) - 2
    first = (0,) * lead + (slice(None), slice(None))
    ins, in_specs = [a, b], [a_spec, b_spec]
    out_shape, out_specs = [jax.ShapeDtypeStruct((m, n), out_dtype)], [o_spec]
    if fused:
        ins += [res, gate]
        in_specs += [o_spec, pl.BlockSpec((1, tn), lambda i, j, q: (0, j))]
        out_shape.append(jax.ShapeDtypeStruct((m, n), F32))
        out_specs.append(o_spec)
    if into is not None:
        shape, omap = into
        out_shape = [jax.ShapeDtypeStruct(shape, out_dtype)]
        out_specs = [pl.BlockSpec((1,) * lead + (tm, tn), lambda i, j, q: omap(i, j))]
    n_in, n_out = len(ins), len(out_shape)
    scratch = [pltpu.VMEM((tm, tn), F32)]
    if ride is not None:
        r_in, r_out, r_scr = ride.specs()
        ins, in_specs = ins + list(ride.ins), in_specs + r_in
        out_shape, out_specs = out_shape + list(ride.out_shapes), out_specs + r_out
        scratch = scratch + r_scr

    def body(*refs):
        a_ref, b_ref = refs[:2]
        o_ref = refs[len(ins)]
        acc = refs[len(ins) + len(out_shape)]
        i, j, q = pl.program_id(0), pl.program_id(1), pl.program_id(2)
        at = lambda x, y, z: jnp.logical_and(jnp.logical_and(i == x, j == y), q == z)
        r_refs = (refs[n_in:len(ins)], refs[len(ins) + n_out:len(ins) + len(out_shape)], refs[len(ins) + len(out_shape) + 1:])
        if ride is not None:
            ride.begin(*r_refs, at(0, 0, 0))

        @pl.when(q == 0)
        def _():
            acc[...] = jnp.zeros(acc.shape, F32)

        acc[...] += _mxu(a_ref[...], b_ref[...], mode)

        @pl.when(q == nk - 1)
        def _():
            o_ref[first] = acc[...].astype(o_ref.dtype)
            if fused:
                refs[len(ins) + 1][...] = refs[2][...] + refs[3][...] * acc[...]

        if ride is not None:
            ride.end(*r_refs, at(ni - 1, nj - 1, nk - 1))

    sem = ("arbitrary",) * 3 if ride is not None else ("parallel", "parallel", "arbitrary")
    out = pl.pallas_call(
        body, name=name, grid=(ni, nj, nk), in_specs=in_specs, out_specs=out_specs, out_shape=out_shape, scratch_shapes=scratch,
        compiler_params=pltpu.CompilerParams(dimension_semantics=sem, vmem_limit_bytes=VMEM_LIMIT_BYTES),
    )(*ins)
    if ride is not None:
        RIDERS.done[name] = list(out[n_out:])
    return tuple(out[:n_out]) if fused else out[0]


def add_arrays(name, arrs, out_dtype=F32):
    nb, r, c = arrs[0].shape
    t = _tile(r, 256, 8)
    (out,), _ = scan_fwd(name, _sum_fn, nb=nb, nchunk=r // t, t=t, rows=[Row(a, fb=lambda b: b) for a in arrs], vecs=[], carries=[],
                         outs=[out_row((nb, r, c), out_dtype, fb=lambda b: b)], save=False)
    return out


def _sum_chips_mine_fn(ci, b, carries, rows, vecs):
    mine_layer = lax.axis_index("c")
    chip = 2 * lax.axis_index("x") + lax.axis_index("y")
    tot = None
    for j in range(4):
        own = jnp.where(mine_layer == 0, rows[j], rows[8 + j])
        sent = jnp.where(mine_layer == 0, rows[4 + j], rows[12 + j])
        term = jnp.where(chip == j, own, sent).astype(F32)
        tot = term if tot is None else tot + term
    return [], [tot]


def sum_chips_mine(name, p0, q0, p1, q1):
    _, r, c = p0.shape
    t = _tile(r, 256, 8)
    rows = [Row(a, fb=(lambda b, j=j: j)) for a in (p0, q0, p1, q1) for j in range(4)]
    (out,), _ = scan_fwd(name, _sum_chips_mine_fn, nb=1, nchunk=r // t, t=t, rows=rows, vecs=[], carries=[],
                         outs=[out_row((1, r, c))], save=False)
    return out[0]


def _remote(src, dst, send_sems, recv_sems, k, to):
    return pltpu.make_async_remote_copy(src_ref=src, dst_ref=dst, send_sem=send_sems.at[k], recv_sem=recv_sems.at[k],
                                        device_id=to, device_id_type=MESH)


def gather_ride(layer, shards):
    na = len(shards)

    def start(ins, outs, ss, rs, me):
        @pl.when(me[2] == layer)
        def _():
            for k in range(na):
                for p, mask in enumerate(CHIP_PEERS):
                    _remote(ins[k].at[layer], outs[k].at[_chip(me)], ss, rs, 6 * k + p, _flip(mask, me)).start()

    def finish(ins, outs, ss, rs, me):
        sibling = _flip(SIBLING[0], me)

        @pl.when(me[2] == layer)
        def _():
            for k in range(na):
                for p, mask in enumerate(CHIP_PEERS):
                    slot = outs[k].at[_chip(_flip(mask, me))]
                    _remote(ins[k].at[layer], slot, ss, rs, 6 * k + p, _flip(mask, me)).wait_recv()
                    _remote(slot, slot, ss, rs, 6 * k + 3 + p, sibling).start()
            for k in range(na):
                for p, mask in enumerate(CHIP_PEERS):
                    slot = outs[k].at[_chip(_flip(mask, me))]
                    _remote(ins[k].at[layer], slot, ss, rs, 6 * k + p, _flip(mask, me)).wait_send()
                    _remote(slot, slot, ss, rs, 6 * k + 3 + p, sibling).wait_send()

        @pl.when(me[2] != layer)
        def _():
            for k in range(na):
                for p, mask in enumerate(CHIP_PEERS):
                    slot = outs[k].at[_chip(_flip(mask, me))]
                    _remote(slot, slot, ss, rs, 6 * k + 3 + p, sibling).wait_recv()

    return Ride(list(shards), [jax.ShapeDtypeStruct((4,) + a.shape[1:], a.dtype) for a in shards], 6 * na, start, finish)


def scatter_ride(layer, parts):
    na = len(parts)

    def start(ins, outs, ss, rs, me):
        @pl.when(me[2] == layer)
        def _():
            for k in range(na):
                for p, mask in enumerate(CHIP_PEERS):
                    peer = _flip(mask, me)
                    _remote(ins[k].at[_chip(peer)], outs[k].at[_chip(me)], ss, rs, 3 * k + p, peer).start()

    def finish(ins, outs, ss, rs, me):
        @pl.when(me[2] == layer)
        def _():
            for k in range(na):
                for p, mask in enumerate(CHIP_PEERS):
                    peer = _flip(mask, me)
                    _remote(ins[k].at[_chip(peer)], outs[k].at[_chip(peer)], ss, rs, 3 * k + p, peer).wait_recv()
                    _remote(ins[k].at[_chip(peer)], outs[k].at[_chip(me)], ss, rs, 3 * k + p, peer).wait_send()

    return Ride(list(parts), [jax.ShapeDtypeStruct(a.shape, a.dtype) for a in parts], 3 * na, start, finish)


def to_owner_ride(layer, arrays):
    na = len(arrays)

    def start(ins, outs, ss, rs, me):
        @pl.when(me[2] != layer)
        def _():
            for k in range(na):
                _remote(ins[k], outs[k], ss, rs, k, _flip(SIBLING[0], me)).start()

    def finish(ins, outs, ss, rs, me):
        for k in range(na):
            cp = _remote(ins[k], outs[k], ss, rs, k, _flip(SIBLING[0], me))
            pl.when(me[2] != layer)(cp.wait_send)
            pl.when(me[2] == layer)(cp.wait_recv)

    return Ride(list(arrays), [jax.ShapeDtypeStruct(a.shape, a.dtype) for a in arrays], na, start, finish)


def allgather8_ride(a):
    def start(ins, outs, ss, rs, me):
        for p, mask in enumerate(ALL_PEERS):
            _remote(ins[0], outs[0].at[_dev(me)], ss, rs, p, _flip(mask, me)).start()

    def finish(ins, outs, ss, rs, me):
        for p, mask in enumerate(ALL_PEERS):
            peer = _flip(mask, me)
            _remote(ins[0], outs[0].at[_dev(peer)], ss, rs, p, peer).wait_recv()
            _remote(ins[0], outs[0].at[_dev(me)], ss, rs, p, peer).wait_send()

    return Ride([a], [jax.ShapeDtypeStruct((8,) + a.shape, a.dtype)], len(ALL_PEERS), start, finish)


def _with_own(gathered, own):
    me = _dev((lax.axis_index("x"), lax.axis_index("y"), lax.axis_index("c")))
    return jnp.where((jnp.arange(8) == me)[:, None, None], own[None], gathered)


def swap_ride(arrays):
    na = len(arrays)

    def start(ins, outs, ss, rs, me):
        for k in range(na):
            _remote(ins[k], outs[k], ss, rs, k, _flip(SIBLING[0], me)).start()

    def finish(ins, outs, ss, rs, me):
        for k in range(na):
            cp = _remote(ins[k], outs[k], ss, rs, k, _flip(SIBLING[0], me))
            cp.wait_recv()
            cp.wait_send()

    return Ride(list(arrays), [jax.ShapeDtypeStruct(a.shape, a.dtype) for a in arrays], na, start, finish)


class _Shifted:
    def __init__(self, ref, offset):
        self.ref, self.offset = ref, offset

    @property
    def at(self):
        return self

    def __getitem__(self, k):
        return self.ref.at[self.offset + k]


def merge_rides(rides):
    def spans(counts):
        out, o = [], 0
        for n in counts:
            out.append((o, o + n))
            o += n
        return out

    si, so = spans([len(r.ins) for r in rides]), spans([len(r.out_shapes) for r in rides])
    ss_ = spans([r.nsem for r in rides])

    def each(method):
        def run(ins, outs, ss, rs, me):
            for r, (i0, i1), (o0, o1), (s0, _) in zip(rides, si, so, ss_):
                getattr(r, method)(ins[i0:i1], outs[o0:o1], _Shifted(ss, s0), _Shifted(rs, s0), me)
        return run

    return Ride([a for r in rides for a in r.ins], [s for r in rides for s in r.out_shapes], sum(r.nsem for r in rides),
                each("start"), each("finish"))


def _w_in_from_chips(a):
    c2 = a[2]
    pad = jnp.zeros((c2.shape[0], IN_WP - IN_W), c2.dtype)
    return jnp.concatenate([a[0], a[1], c2[:, :252], c2[:, 260:516], c2[:, 252:260], pad, c2[:, 516:], a[3]], axis=1)
```

```python
import functools

import numpy as np
import jax
import jax.numpy as jnp
from jax import lax
from jax.experimental import pallas as pl
from jax.experimental.pallas import tpu as pltpu

F32 = jnp.float32
BF16 = jnp.bfloat16
MESH = pl.DeviceIdType.MESH

D_MODEL = 1024
DEPTH = 2
SSD_INNER = 512
POOL_W = 256
POOL_WINDOWS = (2, 4, 8, 16)
ATT_W = 256
ATT_HEADS = 4
ATT_HEAD_DIM = 64
ATT_PATTERNS = ((128, 1), (512, 4), (2048, 16))
ATT_BLOCK = 128
ROT_DIM = 16
ROPE_THETA = 500000.0
IN_W = 2568
IN_WP = 2688
IN_MAIN = 1920
FFN_DIM = 2816
NORM_EPS = 1e-6
ADAM_LR, ADAM_B1, ADAM_B2, ADAM_EPS, ADAM_WD, ADAM_STEP = 0.001, 0.9, 0.999, 1e-08, 0.01, 10

VMEM_LIMIT_BYTES = 56 * 1024 * 1024
NEG = -1e30


def _mxu(a, b, mode):
    dims = {"nn": ((1,), (0,)), "nt": ((1,), (1,)), "tn": ((0,), (0,))}[mode]
    return lax.dot_general(a.astype(BF16), b.astype(BF16), (dims, ((), ())), preferred_element_type=F32)


@functools.partial(jax.custom_vjp, nondiff_argnums=(2,))
def _bdot(a, b, mode):
    return _mxu(a, b, mode)


def _bdot_fwd(a, b, mode):
    return _mxu(a, b, mode), (a, b)


def _bdot_bwd(mode, res, g):
    a, b = res
    if mode == "nn":
        return _mxu(g, b, "nt"), _mxu(a, g, "tn")
    if mode == "nt":
        return _mxu(g, b, "nn"), _mxu(g, a, "tn")
    return _mxu(b, g, "nt"), _mxu(a, g, "nn")


_bdot.defvjp(_bdot_fwd, _bdot_bwd)


def _iota(shape, dim):
    return lax.broadcasted_iota(jnp.int32, shape, dim)


def _make_shift(h):
    @functools.partial(jax.custom_vjp, nondiff_argnums=(2,))
    def shift(halo, cur, k):
        if k == 0:
            return cur
        full = jnp.concatenate([halo, cur], axis=0)
        return pltpu.roll(full, k, 0)[h:]

    def fwd(halo, cur, k):
        return shift(halo, cur, k), None

    def bwd(k, _, g):
        t, w = g.shape
        if k == 0:
            return jnp.zeros((h, w), F32), g
        d_cur = jnp.where(_iota((t, w), 0) < t - k, pltpu.roll(g, t - k, 0), 0.0)
        top = g[:h]
        d_halo = jnp.where(_iota((h, w), 0) >= h - k, pltpu.roll(top, h - k, 0) if k < h else top, 0.0)
        return d_halo, d_cur

    shift.defvjp(fwd, bwd)
    return shift


_shift8 = _make_shift(8)
_shift16 = _make_shift(16)


def _make_tail(h):
    @jax.custom_vjp
    def tail(x):
        return x[x.shape[0] - h:]

    def fwd(x):
        return tail(x), x.shape[0]

    def bwd(t, g):
        return (jnp.concatenate([jnp.zeros((t - h, g.shape[1]), F32), g], axis=0),)

    tail.defvjp(fwd, bwd)
    return tail


_tail8 = _make_tail(8)
_tail16 = _make_tail(16)


@jax.custom_vjp
def _cumsum_rows(x):
    t = x.shape[0]
    row, s = _iota(x.shape, 0), 1
    while s < t:
        x = x + jnp.where(row >= s, pltpu.roll(x, s, 0), 0.0)
        s *= 2
    return x


def _cumsum_rows_fwd(x):
    return _cumsum_rows(x), None


def _cumsum_rows_bwd(_, g):
    t = g.shape[0]
    row, s = _iota(g.shape, 0), 1
    while s < t:
        g = g + jnp.where(row < t - s, pltpu.roll(g, t - s, 0), 0.0)
        s *= 2
    return (g,)


_cumsum_rows.defvjp(_cumsum_rows_fwd, _cumsum_rows_bwd)


@jax.custom_vjp
def _rot_pairs(t):
    e = _iota(t.shape, 1) % ATT_HEAD_DIM
    n = t.shape[1]
    return jnp.where(e < 8, -pltpu.roll(t, n - 8, 1), jnp.where(e < 16, pltpu.roll(t, 8, 1), 0.0))


def _rot_pairs_fwd(t):
    return _rot_pairs(t), None


def _rot_pairs_bwd(_, g):
    e = _iota(g.shape, 1) % ATT_HEAD_DIM
    n = g.shape[1]
    return (pltpu.roll(jnp.where(e < 8, -g, 0.0), 8, 1) + pltpu.roll(jnp.where(jnp.logical_and(e >= 8, e < 16), g, 0.0), n - 8, 1),)


_rot_pairs.defvjp(_rot_pairs_fwd, _rot_pairs_bwd)


def _make_thirds():
    @jax.custom_vjp
    def thirds(x):
        w = x.shape[1] // 3
        return x[:, :w], x[:, w:2 * w], x[:, 2 * w:]

    def fwd(x):
        return thirds(x), None

    def bwd(_, g):
        return (jnp.concatenate(g, axis=1),)

    thirds.defvjp(fwd, bwd)
    return thirds


_thirds = _make_thirds()


def _rowk(w, k):
    return jnp.sum(jnp.where(_iota(w.shape, 0) == k, w, 0.0), axis=0, keepdims=True)


def _silu(x):
    return x * (0.5 * jnp.tanh(0.5 * x) + 0.5)


def _softplus(x):
    return jnp.maximum(x, 0.0) + jnp.log(1.0 + jnp.exp(-jnp.abs(x)))


def _tile(dim, target, unit=128):
    if dim <= target:
        return dim
    best = None
    for t in range(unit, target + 1, unit):
        if dim % t == 0:
            best = t
    assert best is not None, (dim, target)
    return best


class Ride:
    def __init__(self, ins, out_shapes, nsem, start, finish):
        self.ins, self.out_shapes, self.nsem, self.start, self.finish = ins, out_shapes, nsem, start, finish

    def specs(self):
        hbm = pl.BlockSpec(memory_space=pl.ANY)
        return [hbm] * len(self.ins), [hbm] * len(self.out_shapes), [pltpu.SemaphoreType.DMA((self.nsem,))] * 2

    def begin(self, in_refs, out_refs, sems, cond=None):
        me = (lax.axis_index("x"), lax.axis_index("y"), lax.axis_index("c"))
        go = lambda: self.start(in_refs, out_refs, sems[0], sems[1], me)
        go() if cond is None else pl.when(cond)(go)

    def end(self, in_refs, out_refs, sems, cond=None):
        me = (lax.axis_index("x"), lax.axis_index("y"), lax.axis_index("c"))
        go = lambda: self.finish(in_refs, out_refs, sems[0], sems[1], me)
        go() if cond is None else pl.when(cond)(go)


class _Riders:
    def reset(self):
        self.booked, self.done = {}, {}

    def book(self, host, ride):
        assert host not in self.booked, host
        self.booked[host] = ride

    def take(self, host):
        return self.booked.pop(host, None)

    def result(self, host):
        return self.done[host]


RIDERS = _Riders()
RIDERS.reset()


class Row:
    def __init__(self, arr, w=None, fb=None, fc=None, diff=True, slot=False, dcols=None, dfc=None, ddtype=F32, view=None):
        self.ddtype = ddtype
        self.view = view
        self.arr = arr
        self.w = arr.shape[2] if w is None else w
        self.fb = (lambda b: 0) if fb is None else fb
        self.fc = (lambda b: 0) if fc is None else fc
        self.diff = diff
        self.slot = slot
        self.dcols = dcols
        self.dfc = dfc


class Vec:
    def __init__(self, arr, w=None, fc=None, diff=True):
        self.arr = arr
        self.w = arr.shape[1] if w is None else w
        self.fc = fc
        self.diff = diff


def _row_spec(r, t, nchunk, reverse):
    shape = (1, t, r.w) if r.view is None else (1, t // r.view, r.view * r.w)
    if reverse:
        return pl.BlockSpec(shape, lambda b, i, r=r: (r.fb(b), nchunk - 1 - i, r.fc(b)))
    return pl.BlockSpec(shape, lambda b, i, r=r: (r.fb(b), i, r.fc(b)))


def _load_row(ref, r, t, scr):
    if r.view is None:
        return ref[0]
    d, w = r.view, r.w
    for q in range(d):
        for j in range(w // 128):
            scr[j, pl.ds(q, t // d, stride=d), :] = ref[0, :, q * w + 128 * j:q * w + 128 * (j + 1)].astype(F32)
    return jnp.concatenate([scr[j] for j in range(w // 128)], axis=1)


def _store_row(ref, r, t, scr, val):
    if r.view is None:
        ref[0] = val.astype(ref.dtype)
        return
    d, w = r.view, r.w
    for j in range(w // 128):
        scr[j] = val[:, 128 * j:128 * (j + 1)]
    for q in range(d):
        for j in range(w // 128):
            ref[0, :, q * w + 128 * j:q * w + 128 * (j + 1)] = scr[j, pl.ds(q, t // d, stride=d), :].astype(ref.dtype)


def _view_scratch(specs, t):
    ws = [r.w for r in specs if r.view is not None]
    return [pltpu.VMEM((max(ws) // 128, t, 128), F32)] if ws else []


def _vec_spec(v):
    if v.fc is None:
        return pl.BlockSpec(v.arr.shape, lambda b, i: (0, 0))
    return pl.BlockSpec((v.arr.shape[0], v.w), lambda b, i, v=v: (0, v.fc(b)))


def _cparams():
    return pltpu.CompilerParams(dimension_semantics=("arbitrary", "arbitrary"), vmem_limit_bytes=VMEM_LIMIT_BYTES)


def scan_fwd(name, fn, *, nb, nchunk, t, rows, vecs, carries, outs, save):
    nr, nv, nc, no = len(rows), len(vecs), len(carries), len(outs)
    ns = nc if save else 0
    ride = RIDERS.take(name)
    r_in, r_out, r_scr = ride.specs() if ride else ([], [], [])

    def body(*refs):
        p = 0
        row_refs = refs[p:p + nr]; p += nr
        vec_refs = refs[p:p + nv]; p += nv
        ride_in = refs[p:p + len(r_in)]; p += len(r_in)
        out_refs = refs[p:p + no]; p += no
        save_refs = refs[p:p + ns]; p += ns
        ride_out = refs[p:p + len(r_out)]; p += len(r_out)
        car = refs[p:p + nc]; p += nc
        scr = refs[p] if stage else None
        sems = refs[p + len(stage):]
        b, i = pl.program_id(0), pl.program_id(1)
        if ride:
            ride.begin(ride_in, ride_out, sems, jnp.logical_and(b == 0, i == 0))
        if nc:
            @pl.when(i == 0)
            def _():
                for c_ref in car:
                    c_ref[...] = jnp.zeros(c_ref.shape, F32)
        cin = [c_ref[...] for c_ref in car]
        if save:
            for s_ref, cv in zip(save_refs, cin):
                s_ref[0, 0] = cv
        new_c, o = fn(i, b, cin, [_load_row(ref, r, t, scr) for ref, r in zip(row_refs, rows)], [v[...] for v in vec_refs])
        for c_ref, cv in zip(car, new_c):
            c_ref[...] = cv
        for o_ref, spec, ov in zip(out_refs, outs, o):
            _store_row(o_ref, spec, t, scr, ov)
        if ride:
            ride.end(ride_in, ride_out, sems, jnp.logical_and(b == nb - 1, i == nchunk - 1))

    stage = _view_scratch(list(rows) + list(outs), t)
    out_shape = [o.arr for o in outs]
    out_specs = [_row_spec(o, t, nchunk, False) for o in outs]
    if save:
        for cs in carries:
            out_shape.append(jax.ShapeDtypeStruct((nb, nchunk) + tuple(cs), F32))
            out_specs.append(pl.BlockSpec((1, 1) + tuple(cs), lambda b, i: (b, i, 0, 0)))
    res = pl.pallas_call(
        body, name=name, grid=(nb, nchunk),
        in_specs=[_row_spec(r, t, nchunk, False) for r in rows] + [_vec_spec(v) for v in vecs] + r_in,
        out_specs=out_specs + r_out, out_shape=out_shape + (list(ride.out_shapes) if ride else []),
        scratch_shapes=[pltpu.VMEM(tuple(cs), F32) for cs in carries] + stage + r_scr,
        compiler_params=_cparams(),
    )(*[r.arr for r in rows], *[v.arr for v in vecs], *(ride.ins if ride else []))
    if ride:
        RIDERS.done[name] = list(res[no + ns:])
    return list(res[:no]), list(res[no:no + ns])


def scan_bwd(name, fn, *, nb, nchunk, t, rows, vecs, carries, saved, douts, adds=None):
    adds = adds or {}
    nr, nv, nc, no = len(rows), len(vecs), len(carries), len(douts)
    dri = [k for k, r in enumerate(rows) if r.diff]
    dvi = [k for k, v in enumerate(vecs) if v.diff]
    add_keys = sorted(adds)
    na = len(add_keys)
    ride = RIDERS.take(name)
    r_in, r_out, r_scr = ride.specs() if ride else ([], [], [])

    def body(*refs):
        p = 0
        row_refs = refs[p:p + nr]; p += nr
        vec_refs = refs[p:p + nv]; p += nv
        save_refs = refs[p:p + nc]; p += nc
        dout_refs = refs[p:p + no]; p += no
        add_refs = refs[p:p + na]; p += na
        ride_in = refs[p:p + len(r_in)]; p += len(r_in)
        drow_refs = refs[p:p + len(dri)]; p += len(dri)
        dvec_refs = refs[p:p + len(dvi)]; p += len(dvi)
        ride_out = refs[p:p + len(r_out)]; p += len(r_out)
        dcar = refs[p:p + nc]; p += nc
        scr = refs[p] if stage else None
        sems = refs[p + len(stage):]
        b, ir = pl.program_id(0), pl.program_id(1)
        ci = nchunk - 1 - ir
        if ride:
            ride.begin(ride_in, ride_out, sems, jnp.logical_and(b == 0, ir == 0))
        if nc:
            @pl.when(ir == 0)
            def _():
                for c_ref in dcar:
                    c_ref[...] = jnp.zeros(c_ref.shape, F32)
        rows_v = [_load_row(ref, r, t, scr) for ref, r in zip(row_refs, rows)]
        vecs_v = [v[...] for v in vec_refs]
        cin = [s[0, 0] for s in save_refs]
        dc = [c_ref[...] for c_ref in dcar]
        dout_v = [_load_row(ref, r, t, scr).astype(F32) for ref, r in zip(dout_refs, douts)]

        def f(cs, dr, dv):
            rr, vv = list(rows_v), list(vecs_v)
            for k, idx in enumerate(dri):
                rr[idx] = dr[k]
            for k, idx in enumerate(dvi):
                vv[idx] = dv[k]
            return fn(ci, b, cs, rr, vv)

        _, vjp = jax.vjp(f, cin, [rows_v[k].astype(F32) for k in dri], [vecs_v[k].astype(F32) for k in dvi])
        dcin, drows, dvecs = vjp((dc, dout_v))
        for c_ref, cv in zip(dcar, dcin):
            c_ref[...] = cv
        for k, (o_ref, ov) in enumerate(zip(drow_refs, drows)):
            if dri[k] in adds:
                ov = ov + add_refs[add_keys.index(dri[k])][0].astype(F32)
            _store_row(o_ref, rows[dri[k]], t, scr, ov)
        for k, (o_ref, ov) in enumerate(zip(dvec_refs, dvecs)):
            first = (ir == 0) if vecs[dvi[k]].fc is not None else jnp.logical_and(ir == 0, b == 0)

            @pl.when(first)
            def _(o_ref=o_ref, ov=ov):
                o_ref[...] = ov

            @pl.when(jnp.logical_not(first))
            def _(o_ref=o_ref, ov=ov):
                o_ref[...] += ov

        if ride:
            ride.end(ride_in, ride_out, sems, jnp.logical_and(b == nb - 1, ir == nchunk - 1))

    stage = _view_scratch(list(rows) + list(douts), t)
    in_specs = ([_row_spec(r, t, nchunk, True) for r in rows] + [_vec_spec(v) for v in vecs]
                + [pl.BlockSpec((1, 1) + tuple(cs), lambda b, i: (b, nchunk - 1 - i, 0, 0)) for cs in carries]
                + [_row_spec(d, t, nchunk, True) for d in douts]
                + [_row_spec(adds[k], t, nchunk, True) for k in add_keys] + r_in)
    out_shape, out_specs = [], []
    for k in dri:
        r = rows[k]
        if r.slot:
            out_shape.append(jax.ShapeDtypeStruct((nb, r.arr.shape[1], r.w), r.ddtype))
            out_specs.append(pl.BlockSpec((1, t, r.w), lambda b, i: (b, nchunk - 1 - i, 0)))
        elif r.dcols is not None:
            out_shape.append(jax.ShapeDtypeStruct((r.arr.shape[0], r.arr.shape[1], r.dcols), r.ddtype))
            out_specs.append(pl.BlockSpec((1, t, r.w), lambda b, i, r=r: (r.fb(b), nchunk - 1 - i, r.dfc(b))))
        else:
            out_shape.append(jax.ShapeDtypeStruct(r.arr.shape, r.ddtype))
            out_specs.append(_row_spec(r, t, nchunk, True))
    for k in dvi:
        out_shape.append(jax.ShapeDtypeStruct(vecs[k].arr.shape, F32))
        out_specs.append(_vec_spec(vecs[k]))
    nd = len(dri) + len(dvi)
    res = pl.pallas_call(
        body, name=name, grid=(nb, nchunk), in_specs=in_specs, out_specs=out_specs + r_out,
        out_shape=out_shape + (list(ride.out_shapes) if ride else []),
        scratch_shapes=[pltpu.VMEM(tuple(cs), F32) for cs in carries] + stage + r_scr,
        compiler_params=_cparams(),
    )(*[r.arr for r in rows], *[v.arr for v in vecs], *saved, *[d.arr for d in douts], *[adds[k].arr for k in add_keys],
      *(ride.ins if ride else []))
    if ride:
        RIDERS.done[name] = list(res[nd:])
    return list(res[:len(dri)]), list(res[len(dri):nd])


def out_row(shape, dtype=F32, w=None, fb=None, fc=None):
    return Row(jax.ShapeDtypeStruct(shape, dtype), w, fb, fc)


def _conv(shift, halo, cur, w, bias, taps):
    y = bias
    for k in range(taps):
        y = y + _rowk(w, k) * shift(halo, cur, taps - 1 - k)
    return y


def _ssd_fn(ci, b, carries, rows, vecs):
    cx, cb_, cc, ht = carries
    z, xr, br, cr, dtr = rows
    cwx, cbx, cwb, cbb, cwc, cbc, dtb, alog, dsk, ng = vecs
    t = z.shape[0]
    xs = _silu(_conv(_shift8, cx, xr, cwx, cbx, 4))
    bm = _silu(_conv(_shift8, cb_, br, cwb, cbb, 4))
    cm = _silu(_conv(_shift8, cc, cr, cwc, cbc, 4))
    dt = _softplus(dtr + dtb)
    acol = _cumsum_rows(dt * (-jnp.exp(alog)))
    arow = acol.T
    r, c = _iota((t, t), 0), _iota((t, t), 1)
    causal = r >= c
    cbm = _bdot(cm, bm, "nt")
    lane, sub = _iota(acol.shape, 1), _iota(arow.shape, 0)
    colh = _iota(xs.shape, 1) // 64
    a, dtx, dx, acs = jnp.zeros(xs.shape, F32), jnp.zeros(xs.shape, F32), jnp.zeros((1, xs.shape[1]), F32), []
    for j in range(4):
        h = 4 * b + j
        ac = jnp.sum(jnp.where(lane == h, acol, 0.0), axis=1, keepdims=True)
        acs.append(ac)
        a = jnp.where(colh == j, ac, a)
        dtx = jnp.where(colh == j, jnp.sum(jnp.where(lane == h, dt, 0.0), axis=1, keepdims=True), dtx)
        dx = jnp.where(_iota(dx.shape, 1) // 64 == j, jnp.sum(jnp.where(_iota(dsk.shape, 1) == h, dsk, 0.0), axis=1, keepdims=True), dx)
    atot = jnp.sum(jnp.where(_iota(a.shape, 0) == t - 1, a, 0.0), axis=0, keepdims=True)
    x = xs * dtx
    ydiag = jnp.zeros(x.shape, F32)
    for j in range(4):
        ar = jnp.sum(jnp.where(sub == 4 * b + j, arow, 0.0), axis=0, keepdims=True)
        lmat = jnp.exp(jnp.where(causal, acs[j] - ar, NEG))
        ydiag = ydiag + _bdot(cbm * lmat, jnp.where(colh == j, x, 0.0), "nn")
    yoff = _bdot(cm, ht, "nn") * jnp.exp(a)
    ht_new = ht * jnp.exp(atot) + _bdot(bm, x * jnp.exp(atot - a), "tn")
    y = ydiag + yoff + dx * xs
    yz = y * _silu(z)
    yn = yz * lax.rsqrt(jnp.mean(yz * yz, axis=-1, keepdims=True) + NORM_EPS) * ng
    return [_tail8(xr), _tail8(br), _tail8(cr), ht_new], [yn]


_SSD_T = 256
_SSD_CARRIES = [(8, 256), (8, 128), (8, 128), (128, 256)]


def _ssd_io(proj3, p):
    own = lambda b: b
    rows = [Row(proj3, 256, fc=own, dcols=512, dfc=own, ddtype=BF16),
            Row(proj3, 256, fc=lambda b: 2 + b, dcols=512, dfc=own, ddtype=BF16),
            Row(proj3, 128, fc=lambda b: 8 + b, dcols=256, dfc=own, ddtype=BF16),
            Row(proj3, 128, fc=lambda b: 10 + b, dcols=256, dfc=own, ddtype=BF16),
            Row(proj3, 128, fc=lambda b: 14, slot=True)]
    vecs = [Vec(p["cw"], 256, lambda b: b), Vec(p["cb"], 256, lambda b: b),
            Vec(p["cw"], 128, lambda b: 4 + b), Vec(p["cb"], 128, lambda b: 4 + b),
            Vec(p["cw"], 128, lambda b: 6 + b), Vec(p["cb"], 128, lambda b: 6 + b),
            Vec(p["dtb"]), Vec(p["alog"]), Vec(p["dsk"]), Vec(p["ng"], 256, lambda b: b)]
    return rows, vecs


def ssd_forward(name, proj3, p):
    rows, vecs = _ssd_io(proj3, p)
    s = proj3.shape[1]
    (y,), saved = scan_fwd(name, _ssd_fn, nb=2, nchunk=s // _SSD_T, t=_SSD_T, rows=rows, vecs=vecs,
                           carries=_SSD_CARRIES, outs=[out_row((1, s, SSD_INNER), BF16, 256, fc=lambda b: b)], save=True)
    return y, saved


def ssd_backward(name, proj3, p, saved, dmix3):
    rows, vecs = _ssd_io(proj3, p)
    s = proj3.shape[1]
    drows, dvecs = scan_bwd(name, _ssd_fn, nb=2, nchunk=s // _SSD_T, t=_SSD_T, rows=rows, vecs=vecs,
                            carries=_SSD_CARRIES, saved=saved, douts=[Row(dmix3, 256, fc=lambda b: b)])
    return drows, dvecs


def _pool_fn(ci, b, carries, rows, vecs):
    (cu,) = carries
    (u,) = rows
    wbd, scale = vecs
    t = u.shape[0]
    pos = ci * t + _iota(u.shape, 0)
    grp = _iota(u.shape, 1) // 64
    acc, pooled, k = u, jnp.zeros(u.shape, F32), 1
    for gi, w in enumerate(POOL_WINDOWS):
        while k < w:
            acc = acc + _shift16(cu, u, k)
            k += 1
        pooled = jnp.where(grp == gi, acc / jnp.minimum(pos + 1, w).astype(F32), pooled)
    y = _bdot(pooled - u, wbd, "nn") * scale
    return [_tail16(u)], [y]


_POOL_T = 256


def _pool_io(proj3, wbd, scale):
    return [Row(proj3, 256, fc=lambda b: 6, dcols=256, dfc=lambda b: 0, ddtype=BF16)], [Vec(wbd), Vec(scale)]


def pool_forward(name, proj3, wbd, scale):
    rows, vecs = _pool_io(proj3, wbd, scale)
    s = proj3.shape[1]
    (y,), saved = scan_fwd(name, _pool_fn, nb=1, nchunk=s // _POOL_T, t=_POOL_T, rows=rows, vecs=vecs,
                           carries=[(16, 256)], outs=[out_row((1, s, POOL_W), BF16)], save=True)
    return y, saved


def pool_backward(name, proj3, wbd, scale, saved, dmix3):
    rows, vecs = _pool_io(proj3, wbd, scale)
    s = proj3.shape[1]
    return scan_bwd(name, _pool_fn, nb=1, nchunk=s // _POOL_T, t=_POOL_T, rows=rows, vecs=vecs,
                    carries=[(16, 256)], saved=saved, douts=[Row(dmix3, 256, fc=lambda b: 2)])


def _attn_fn(ci, b, carries, rows, vecs):
    kp, vp = carries
    qr, kr, v = _thirds(rows[0])
    scale = ATT_HEAD_DIM ** -0.5
    q = qr
    n = q.shape[0]
    r, c = _iota((n, n), 0), _iota((n, n), 1)
    prev_ok, cur_ok = jnp.logical_and(c >= r, ci > 0), r >= c
    head = _iota(q.shape, 1) // ATT_HEAD_DIM
    o, lse = jnp.zeros(q.shape, F32), jnp.zeros(q.shape, F32)
    for h in range(ATT_HEADS):
        mine = head == h
        qh = jnp.where(mine, qr, 0.0)
        sp = jnp.where(prev_ok, _bdot(qh, kp, "nt") * scale, NEG)
        sc = jnp.where(cur_ok, _bdot(qh, kr, "nt") * scale, NEG)
        m = lax.stop_gradient(jnp.maximum(jnp.max(sp, axis=1, keepdims=True), jnp.max(sc, axis=1, keepdims=True)))
        pp, pc = jnp.exp(sp - m), jnp.exp(sc - m)
        l = jnp.sum(pp, axis=1, keepdims=True) + jnp.sum(pc, axis=1, keepdims=True)
        o = jnp.where(mine, (_bdot(pp, vp, "nn") + _bdot(pc, v, "nn")) / l, o)
        lse = jnp.where(mine, m + jnp.log(l), lse)
    return [kr, v], [o, lse]


_ATT_CARRIES = [(ATT_BLOCK, ATT_W), (ATT_BLOCK, ATT_W)]


def attn_forward(name, pv, d):
    l = pv.shape[1]
    own = lambda b: b
    outs = [out_row((1, l, d * ATT_W), F32, ATT_W, fc=own) for _ in range(2)]
    (o, lse), saved = scan_fwd(name, _attn_fn, nb=d, nchunk=l // ATT_BLOCK, t=ATT_BLOCK, rows=[Row(pv, 3 * ATT_W, fc=own)],
                               vecs=[], carries=_ATT_CARRIES, outs=outs, save=True)
    return o, lse, saved


def attn_backward(name, pv, d, saved, do, dlse):
    l = pv.shape[1]
    own = lambda b: b
    (dpv,), _ = scan_bwd(name, _attn_fn, nb=d, nchunk=l // ATT_BLOCK, t=ATT_BLOCK, rows=[Row(pv, 3 * ATT_W, fc=own)], vecs=[],
                         carries=_ATT_CARRIES, saved=saved, douts=[Row(do, ATT_W, fc=own), Row(dlse, ATT_W, fc=own)])
    return dpv


def _rope_fn(ci, b, carries, rows, vecs):
    x, cs, sn = rows
    return [], [x * cs + _rot_pairs(x) * sn]


def _rope3_fn(ci, b, carries, rows, vecs):
    _, (y,) = _rope_fn(ci, b, carries, rows, vecs)
    return [], [y, y, y]


def _by_residue(a_or_shape, w, d):
    if isinstance(a_or_shape, tuple):
        _, s, _ = a_or_shape
        return Row(jax.ShapeDtypeStruct((1, s // d, d * w), F32), w, view=None if d == 1 else d)
    return Row(a_or_shape, w, view=None if d == 1 else d)


def rope_forward(name, qkv3, cs3, sn3):
    s, w = qkv3.shape[1], qkv3.shape[2]
    ys, _ = scan_fwd(name, _rope3_fn, nb=1, nchunk=s // _ROW_T, t=_ROW_T, vecs=[], carries=[], save=False,
                     rows=[Row(qkv3), Row(cs3, diff=False), Row(sn3, diff=False)],
                     outs=[_by_residue(qkv3.shape, w, d) for _, d in ATT_PATTERNS])
    return ys


def rope_backward(name, qkv3, cs3, sn3, dys):
    s, w = qkv3.shape[1], qkv3.shape[2]
    (dx,), _ = scan_bwd(name, _rope3_fn, nb=1, nchunk=s // _ROW_T, t=_ROW_T, vecs=[], carries=[], saved=[],
                        rows=[Row(qkv3, ddtype=BF16), Row(cs3, diff=False), Row(sn3, diff=False)],
                        douts=[_by_residue(a, w, d) for a, (_, d) in zip(dys, ATT_PATTERNS)])
    return dx


def _merge_fn(ci, b, carries, rows, vecs):
    o1, o2, o3, l1, l2, l3 = rows
    mx = lax.stop_gradient(jnp.maximum(l1, jnp.maximum(l2, l3)))
    e1, e2, e3 = jnp.exp(l1 - mx), jnp.exp(l2 - mx), jnp.exp(l3 - mx)
    return [], [(e1 * o1 + e2 * o2 + e3 * o3) / (e1 + e2 + e3)]


_ROW_T = 512


def _merge_rows(os_, ls_):
    ds = [d for _, d in ATT_PATTERNS]
    return [_by_residue(a, ATT_W, d) for a, d in zip(os_, ds)] + [_by_residue(a, ATT_W, d) for a, d in zip(ls_, ds)]


def merge_forward(name, os_, ls_, s):
    (y,), _ = scan_fwd(name, _merge_fn, nb=1, nchunk=s // _ROW_T, t=_ROW_T, rows=_merge_rows(os_, ls_), vecs=[],
                       carries=[], outs=[out_row((1, s, ATT_W), BF16)], save=False)
    return y


def merge_backward(name, os_, ls_, dmix3):
    s = dmix3.shape[1]
    drows, _ = scan_bwd(name, _merge_fn, nb=1, nchunk=s // _ROW_T, t=_ROW_T, rows=_merge_rows(os_, ls_), vecs=[],
                        carries=[], saved=[], douts=[Row(dmix3, 256, fc=lambda b: 3)])
    return drows


def _norm_mod_fn(ci, b, carries, rows, vecs):
    (x,) = rows
    g, sc, sh = vecs
    xn = x * lax.rsqrt(jnp.mean(x * x, axis=-1, keepdims=True) + NORM_EPS)
    return [], [xn * g * (1.0 + sc) + sh]


def norm_mod_forward(name, x3, g, sc, sh):
    s = x3.shape[1]
    (h,), _ = scan_fwd(name, _norm_mod_fn, nb=1, nchunk=s // _ROW_T, t=_ROW_T, rows=[Row(x3)], vecs=[Vec(g), Vec(sc), Vec(sh)],
                       carries=[], outs=[out_row(x3.shape, BF16)], save=False)
    return h


def norm_mod_backward(name, x3, g, sc, sh, dh3, add3):
    s = x3.shape[1]
    (dx,), dv = scan_bwd(name, _norm_mod_fn, nb=1, nchunk=s // _ROW_T, t=_ROW_T, rows=[Row(x3)], vecs=[Vec(g), Vec(sc), Vec(sh)],
                         carries=[], saved=[], douts=[Row(dh3)], adds={0: Row(add3)})
    return dx, dv


def _gate_fn(ci, b, carries, rows, vecs):
    return [], [rows[0] * vecs[0]]


def gate_backward(name, o3, g, dx3):
    s = o3.shape[1]
    (do,), (dg,) = scan_bwd(name, _gate_fn, nb=1, nchunk=s // _ROW_T, t=_ROW_T, rows=[Row(o3, ddtype=BF16)], vecs=[Vec(g)],
                            carries=[], saved=[], douts=[Row(dx3)])
    return do, dg


def _make_halves():
    @jax.custom_vjp
    def halves(x):
        h = x.shape[1] // 2
        return x[:, :h], x[:, h:]

    def fwd(x):
        return halves(x), None

    def bwd(_, g):
        return (jnp.concatenate(g, axis=1),)

    halves.defvjp(fwd, bwd)
    return halves


_halves = _make_halves()


def _ffn_fn(ci, b, carries, rows, vecs):
    (cu,) = carries
    (u,) = rows
    w, bias = vecs
    hg, hu = _halves(_conv(_shift8, cu, u, w, bias, 3))
    return [_tail8(u)], [_silu(hg) * hu]


_FFN_T = 256
_FFN_CW = FFN_DIM // 2
_FFN_CARRIES = [(8, 2 * _FFN_CW)]
FFN_BLOCK_ORDER = [0, 2, 1, 3]


def _ffn_io(up3, cw, cb):
    own = lambda b: b
    return [Row(up3, 2 * _FFN_CW, fc=own, ddtype=BF16)], [Vec(cw, 2 * _FFN_CW, own), Vec(cb, 2 * _FFN_CW, own)]


def ffn_down_forward(name, up3, cw, cb, w_down, res, gate):
    s, t, cw2 = up3.shape[1], _FFN_T, 2 * _FFN_CW
    d = w_down.shape[1]
    nchunk = s // t
    ride = RIDERS.take(name)
    r_in, r_out, r_scr = ride.specs() if ride else ([], [], [])

    def body(*refs):
        up_ref, cw_ref, cb_ref, wd_ref, res_ref, g_ref = refs[:6]
        ride_in = refs[6:6 + len(r_in)]
        act_ref, save_ref, dn_ref, x2_ref = refs[6 + len(r_in):10 + len(r_in)]
        ride_out = refs[10 + len(r_in):10 + len(r_in) + len(r_out)]
        car, acc = refs[10 + len(r_in) + len(r_out):12 + len(r_in) + len(r_out)]
        sems = refs[12 + len(r_in) + len(r_out):]
        i, b = pl.program_id(0), pl.program_id(1)
        if ride:
            ride.begin(ride_in, ride_out, sems, jnp.logical_and(i == 0, b == 0))

        @pl.when(i == 0)
        def _():
            car[b] = jnp.zeros(car.shape[1:], F32)

        cin = car[b]
        save_ref[0, 0] = cin
        (new_c,), (act,) = _ffn_fn(i, b, [cin], [up_ref[0]], [cw_ref[...], cb_ref[...]])
        car[b] = new_c
        act_ref[0] = act.astype(act_ref.dtype)
        part = _mxu(act, wd_ref[...], "nn")

        @pl.when(b == 0)
        def _():
            acc[...] = part

        @pl.when(b == 1)
        def _():
            tot = acc[...] + part
            dn_ref[...] = tot
            x2_ref[...] = res_ref[...] + g_ref[...] * tot

        if ride:
            ride.end(ride_in, ride_out, sems, jnp.logical_and(i == nchunk - 1, b == 1))

    tile = pl.BlockSpec((t, d), lambda i, b: (i, 0))
    out = pl.pallas_call(
        body, name=name, grid=(nchunk, 2),
        in_specs=[pl.BlockSpec((1, t, cw2), lambda i, b: (0, i, b)), pl.BlockSpec((cw.shape[0], cw2), lambda i, b: (0, b)),
                  pl.BlockSpec((1, cw2), lambda i, b: (0, b)), pl.BlockSpec((_FFN_CW, d), lambda i, b: (b, 0)), tile,
                  pl.BlockSpec((1, d), lambda i, b: (0, 0))] + r_in,
        out_specs=[pl.BlockSpec((1, t, _FFN_CW), lambda i, b: (0, i, b)), pl.BlockSpec((1, 1, 8, cw2), lambda i, b: (b, i, 0, 0)),
                   tile, tile] + r_out,
        out_shape=[jax.ShapeDtypeStruct((1, s, FFN_DIM), BF16), jax.ShapeDtypeStruct((2, nchunk, 8, cw2), F32),
                   jax.ShapeDtypeStruct((s, d), F32), jax.ShapeDtypeStruct((s, d), F32)] + (list(ride.out_shapes) if ride else []),
        scratch_shapes=[pltpu.VMEM((2, 8, cw2), F32), pltpu.VMEM((t, d), F32)] + r_scr,
        compiler_params=_cparams(),
    )(up3, cw, cb, w_down, res, gate, *(ride.ins if ride else []))
    if ride:
        RIDERS.done[name] = list(out[4:])
    return out[0], [out[1]], out[2], out[3]


def ffn_mid_backward(name, up3, cw, cb, saved, dact3):
    rows, vecs = _ffn_io(up3, cw, cb)
    s = up3.shape[1]
    return scan_bwd(name, _ffn_fn, nb=2, nchunk=s // _FFN_T, t=_FFN_T, rows=rows, vecs=vecs, carries=_FFN_CARRIES,
                    saved=saved, douts=[Row(dact3, _FFN_CW, fc=lambda b: b)])


def _adam_fn(ci, b, carries, rows, vecs):
    w, g, m, v = rows
    m = ADAM_B1 * m + (1.0 - ADAM_B1) * g
    v = ADAM_B2 * v + (1.0 - ADAM_B2) * (g * g)
    m_hat = m / (1.0 - ADAM_B1 ** ADAM_STEP)
    v_hat = v / (1.0 - ADAM_B2 ** ADAM_STEP)
    delta = -ADAM_LR * (m_hat / (jnp.sqrt(v_hat) + ADAM_EPS) + ADAM_WD * w)
    return [], [delta, m, v]


def _adam_layers_fn(ci, b, carries, rows, vecs):
    w, mine, theirs, m, v = rows
    g = jnp.where(b == lax.axis_index("c"), mine, theirs)
    _, upd = _adam_fn(ci, b, carries, [w, g, m, v], vecs)
    return [], [g] + upd


def adamw_layers(name, w, mine, theirs, m, v):
    _, r, c = w.shape
    t = _tile(r, 256, 8)
    layer = lambda b: b
    rows = [Row(w, fb=layer), Row(mine[None]), Row(theirs[None]), Row(m, fb=layer), Row(v, fb=layer)]
    outs, _ = scan_fwd(name, _adam_layers_fn, nb=2, nchunk=r // t, t=t, rows=rows, vecs=[], carries=[],
                       outs=[out_row(w.shape, fb=layer) for _ in range(4)], save=False)
    return outs


def adamw(name, w, g, m, v):
    shape = w.shape
    c = shape[-1]
    r = int(np.prod(shape[:-1]))
    t = _tile(r, 256, 8)
    as3 = lambda a: a.reshape(1, r, c)
    outs, _ = scan_fwd(name, _adam_fn, nb=1, nchunk=r // t, t=t, rows=[Row(as3(a)) for a in (w, g, m, v)], vecs=[], carries=[],
                       outs=[out_row((1, r, c)) for _ in range(3)], save=False)
    return [o.reshape(shape) for o in outs]


def rope_tables(positions):
    inv_freq = ROPE_THETA ** (-jnp.arange(0, ROT_DIM, 2, dtype=F32) / ROT_DIM)
    ang = positions.astype(F32)[:, None] * inv_freq
    s = positions.shape[0]
    cs = jnp.concatenate([jnp.cos(ang), jnp.cos(ang), jnp.ones((s, ATT_HEAD_DIM - ROT_DIM), F32)], axis=1)
    sn = jnp.concatenate([jnp.sin(ang), jnp.sin(ang), jnp.zeros((s, ATT_HEAD_DIM - ROT_DIM), F32)], axis=1)
    cs3 = jnp.concatenate([jnp.tile(cs, (1, 2 * ATT_HEADS)), jnp.ones((s, ATT_W), F32)], axis=1)
    sn3 = jnp.concatenate([jnp.tile(sn, (1, 2 * ATT_HEADS)), jnp.zeros((s, ATT_W), F32)], axis=1)
    return cs3[None], sn3[None]


def attention_forward(lname, qkv3, cs3, sn3):
    s = qkv3.shape[1]
    rotated = rope_forward(f"{lname}_rope", qkv3, cs3, sn3)
    os_, ls_, keep = [], [], []
    for pi, (_, d) in enumerate(ATT_PATTERNS):
        o, lse, saved = attn_forward(f"{lname}_attn{pi}", rotated[pi], d)
        os_.append(o)
        ls_.append(lse)
        keep.append(saved)
    y = merge_forward(f"{lname}_merge", os_, ls_, s)
    return y, (rotated, os_, ls_, keep)


def attention_backward(lname, qkv3, cs3, sn3, res, dmix3):
    rotated, os_, ls_, keep = res
    dm = merge_backward(f"{lname}_merge_b", os_, ls_, dmix3)
    dys = [attn_backward(f"{lname}_attn{pi}_b", rotated[pi], d, keep[pi], dm[pi], dm[3 + pi]) for pi, (_, d) in enumerate(ATT_PATTERNS)]
    return rope_backward(f"{lname}_rope_b", qkv3, cs3, sn3, dys)


def final_loss(name, x3, t3, g):
    s, d = x3.shape[1], x3.shape[2]
    t = _ROW_T

    def body(x_ref, t_ref, g_ref, loss_ref, dx_ref, dg_ref):
        i = pl.program_id(0)
        tv = t_ref[0]

        def f(x, gg):
            y = x * lax.rsqrt(jnp.mean(x * x, axis=-1, keepdims=True) + NORM_EPS) * gg
            e = y - tv
            return 0.5 * jnp.sum(jnp.mean(e * e, axis=-1, keepdims=True), axis=0, keepdims=True)

        l, vjp = jax.vjp(f, x_ref[0], g_ref[...])
        dx, dg = vjp(jnp.ones((1, 1), F32))
        dx_ref[0] = dx

        @pl.when(i == 0)
        def _():
            loss_ref[...] = jnp.zeros(loss_ref.shape, F32)
            dg_ref[...] = jnp.zeros(dg_ref.shape, F32)

        loss_ref[...] += jnp.broadcast_to(l, loss_ref.shape)
        dg_ref[...] += dg

    row = pl.BlockSpec((1, t, d), lambda i: (0, i, 0))
    vec = pl.BlockSpec((1, d), lambda i: (0, 0))
    return pl.pallas_call(
        body, name=name, grid=(s // t,), in_specs=[row, row, vec],
        out_specs=[pl.BlockSpec((8, 128), lambda i: (0, 0)), row, vec],
        out_shape=[jax.ShapeDtypeStruct((8, 128), F32), jax.ShapeDtypeStruct(x3.shape, F32), jax.ShapeDtypeStruct((1, d), F32)],
        compiler_params=pltpu.CompilerParams(dimension_semantics=("arbitrary",), vmem_limit_bytes=VMEM_LIMIT_BYTES),
    )(x3, t3, g)


_ADA_TN = 512


def ada_forward(name, c16, ada_w):
    depth, d, cols = ada_w.shape

    def body(c_ref, w_ref, o_ref):
        o_ref[0] = _mxu(_silu(c_ref[...]), w_ref[0], "nn")

    return pl.pallas_call(
        body, name=name, grid=(depth, cols // _ADA_TN),
        in_specs=[pl.BlockSpec((16, d), lambda l, j: (0, 0)), pl.BlockSpec((1, d, _ADA_TN), lambda l, j: (l, 0, j))],
        out_specs=pl.BlockSpec((1, 16, _ADA_TN), lambda l, j: (l, 0, j)),
        out_shape=jax.ShapeDtypeStruct((depth, 16, cols), F32),
        compiler_params=pltpu.CompilerParams(dimension_semantics=("arbitrary", "arbitrary"), vmem_limit_bytes=VMEM_LIMIT_BYTES),
    )(c16, ada_w)


def ada_backward(name, c16, dmod16, w, m, v):
    depth, d, cols = w.shape

    def body(c_ref, dm_ref, w_ref, m_ref, v_ref, g_ref, dl_ref, nm_ref, nv_ref):
        g = _mxu(_silu(c_ref[...]), dm_ref[0], "tn")
        _, (delta, nm, nv) = _adam_fn(None, None, [], [w_ref[0], g, m_ref[0], v_ref[0]], [])
        g_ref[0], dl_ref[0], nm_ref[0], nv_ref[0] = g, delta, nm, nv

    blk = pl.BlockSpec((1, d, _ADA_TN), lambda l, j: (l, 0, j))
    return pl.pallas_call(
        body, name=name, grid=(depth, cols // _ADA_TN),
        in_specs=[pl.BlockSpec((16, d), lambda l, j: (0, 0)), pl.BlockSpec((1, 16, _ADA_TN), lambda l, j: (l, 0, j)), blk, blk, blk],
        out_specs=[blk] * 4, out_shape=[jax.ShapeDtypeStruct(w.shape, F32)] * 4,
        compiler_params=pltpu.CompilerParams(dimension_semantics=("arbitrary", "arbitrary"), vmem_limit_bytes=VMEM_LIMIT_BYTES),
    )(c16, dmod16, w, m, v)


def _sum_fn(ci, b, carries, rows, vecs):
    acc = rows[0].astype(F32)
    for r in rows[1:]:
        acc = acc + r.astype(F32)
    return [], [acc]


def sum_slots(name, a, nsum, out_dtype=F32):
    n, r, c = a.shape
    nb = n // nsum
    t = _tile(r, 256, 8)
    rows = [Row(a, fb=(lambda b, k=k: k * nb + b)) for k in range(nsum)]
    (out,), _ = scan_fwd(name, _sum_fn, nb=nb, nchunk=r // t, t=t, rows=rows, vecs=[], carries=[],
                         outs=[out_row((nb, r, c), out_dtype, fb=lambda b: b)], save=False)
    return out


def _flip(mask, pos):
    return tuple((1 - p) if m else p for m, p in zip(mask, pos))


ALL_PEERS = [(a, b, c) for a in (0, 1) for b in (0, 1) for c in (0, 1)][1:]
CHIP_PEERS = [(1, 0, 0), (0, 1, 0), (1, 1, 0)]
SIBLING = [(0, 0, 1)]


def _dev(pos):
    return 4 * pos[0] + 2 * pos[1] + pos[2]


def _chip(pos):
    return 2 * pos[0] + pos[1]


def allgather8(name, a):
    (out,) = ride_alone(name, allgather8_ride(a))
    return _with_own(out, a)


def _rows_of(shape):
    return -(-int(np.prod(shape)) // 1024) * 8


def _pack(arrs):
    parts = []
    for a in arrs:
        flat = a.reshape(-1).astype(F32)
        parts.append(jnp.pad(flat, (0, _rows_of(a.shape) * 128 - flat.shape[0])).reshape(-1, 128))
    rows = sum(p.shape[0] for p in parts)
    parts.append(jnp.zeros(((-rows) % _ROW_T, 128), F32))
    return jnp.concatenate(parts, axis=0)


def _unpack(buf, shapes):
    out, o = [], 0
    for s in shapes:
        r, n = _rows_of(s), int(np.prod(s))
        out.append(buf[o:o + r].reshape(-1)[:n].reshape(s))
        o += r
    return out


_WEIGHTS = ["ada_w", "ada_b", "norm1_g", "w_in", "ssd_conv_w", "ssd_conv_b", "ssd_dt_bias", "ssd_a_log", "ssd_d", "ssd_norm_g",
            "pool_w", "pool_scale", "w_out", "norm2_g", "ffn_up", "ffn_conv_w", "ffn_conv_b", "ffn_down", "final_g"]
_BIG = ["w_in", "w_out", "ffn_up", "ffn_down"]
_SMALL = [n for n in _WEIGHTS if n not in _BIG and n != "ada_w"]
_COL_SHARDED_SMALL = {"ssd_conv_w": 256, "ffn_conv_w": 1408}


def _pad_lanes(v, n=128):
    return jnp.pad(v.astype(F32), (0, n - v.shape[0]))[None]


_CHIP2_PARTS = [(1284, 1536), (1792, 1800), (1536, 1792), (IN_MAIN, IN_MAIN + 126)]


def _w_in_chip_cols(gp):
    q = IN_W // 4
    return [gp[:, :q], gp[:, q:2 * q], jnp.concatenate([gp[:, a:b] for a, b in _CHIP2_PARTS], axis=1), gp[:, IN_WP - q:]]


def _ffn_block_perm(a):
    n = a.shape[-1] // 4
    return jnp.concatenate([a[..., j * n:(j + 1) * n] for j in FFN_BLOCK_ORDER], axis=-1)


def _layer_forward(i, x3, modv, wts, sp, cs3, sn3):
    sh1, sc1, g1, sh2, sc2, g2 = modv
    big = lambda n: wts[n]() if callable(wts[n]) else wts[n]
    h1 = norm_mod_forward(f"l{i}_norm1", x3, wts["norm1_g"], sc1, sh1)
    proj3 = mm(f"l{i}_proj", h1[0], big("w_in")[:, :IN_MAIN], "nn")[None]
    qkv3 = mm(f"l{i}_qkv", h1[0], big("w_in")[:, IN_MAIN:], "nn")[None]
    y_ssd, sv_ssd = ssd_forward(f"l{i}_ssd", proj3, sp)
    y_pool, sv_pool = pool_forward(f"l{i}_pool", proj3, wts["wbd"], wts["pool_scale"])
    y_att, res_att = attention_forward(f"l{i}", qkv3, cs3, sn3)
    mix = jnp.concatenate([y_ssd, y_pool, y_att], axis=-1)
    out, x1 = mm(f"l{i}_wout", mix[0], big("w_out"), "nn", res=x3[0], gate=g1)
    x1 = x1[None]
    h2 = norm_mod_forward(f"l{i}_norm2", x1, wts["norm2_g"], sc2, sh2)
    up3 = mm(f"l{i}_up", h2[0], big("ffn_up"), "nn")[None]
    act, sv_ffn, dn, x2 = ffn_down_forward(f"l{i}_down", up3, wts["ffn_conv_w"], wts["ffn_conv_b"], big("ffn_down"), x1[0], g2)
    keep = dict(x=x3, h1=h1, proj3=proj3, qkv3=qkv3, sv_ssd=sv_ssd, sv_pool=sv_pool, res_att=res_att, mix=mix, out=out[None],
                x1=x1, h2=h2, up3=up3, act=act, sv_ffn=sv_ffn, dn=dn[None])
    return x2[None], keep


def _layer_backward(i, dx2, keep, modv, wts, sp, cs3, sn3, after=None):
    sh1, sc1, g1, sh2, sc2, g2 = modv
    k = keep
    big = lambda n: wts[n]() if callable(wts[n]) else wts[n]
    tell = lambda step, *a: after[step](*a) if after and step in after else None
    d_dn, d_g2 = gate_backward(f"l{i}_gate2_b", k["dn"], g2, dx2)
    d_act = mm(f"l{i}_down_bx", d_dn[0], big("ffn_down"), "nt")
    g_down = mm(f"l{i}_down_bw", k["act"][0], d_dn[0], "tn", BF16).reshape(4, FFN_DIM // 4, D_MODEL)
    (d_up,), dv_ffn = ffn_mid_backward(f"l{i}_ffn_b", k["up3"], wts["ffn_conv_w"], wts["ffn_conv_b"], k["sv_ffn"], d_act[None])
    tell("ffn_b")
    d_h2 = mm(f"l{i}_up_bx", d_up[0], big("ffn_up"), "nt")
    g_up = mm(f"l{i}_up_bw", k["h2"][0], d_up[0], "tn", BF16, tn=_FFN_CW,
              into=((4, D_MODEL, _FFN_CW), lambda r, c: ((c % 2) * 2 + c // 2, r, 0)))
    dx1, (d_n2, d_sc2, d_sh2) = norm_mod_backward(f"l{i}_norm2_b", k["x1"], wts["norm2_g"], sc2, sh2, d_h2[None], dx2)
    d_out, d_g1 = gate_backward(f"l{i}_gate1_b", k["out"], g1, dx1)
    d_mix = mm(f"l{i}_wout_bx", d_out[0], big("w_out"), "nt")[None]
    g_wout = mm(f"l{i}_wout_bw", k["mix"][0], d_out[0], "tn", BF16).reshape(4, D_MODEL // 4, D_MODEL)
    tell("wout_bw", g_wout, g_up, g_down)
    (dz, dxs, dbm, dcm, ddt), dv_ssd = ssd_backward(f"l{i}_ssd_b", k["proj3"], sp, k["sv_ssd"], d_mix)
    tell("ssd_b")
    (du_pool,), (d_wbd, d_pscale) = pool_backward(f"l{i}_pool_b", k["proj3"], wts["wbd"], wts["pool_scale"], k["sv_pool"], d_mix)
    d_qkv = attention_backward(f"l{i}", k["qkv3"], cs3, sn3, k["res_att"], d_mix)
    d_proj = jnp.concatenate([dz[0], dxs[0], dbm[0], dcm[0], du_pool[0], (ddt[0] + ddt[1]).astype(BF16), d_qkv[0]], axis=-1)
    g_win = jnp.stack(_w_in_chip_cols(mm(f"l{i}_proj_bw", k["h1"][0], d_proj, "tn", BF16)))
    tell("proj_bw", g_win)
    d_h1 = mm(f"l{i}_proj_bx", d_proj, big("w_in"), "nt")
    tell("proj_bx")
    dx, (d_n1, d_sc1, d_sh1) = norm_mod_backward(f"l{i}_norm1_b", k["x"], wts["norm1_g"], sc1, sh1, d_h1[None], dx1)
    dcwx, dcbx, dcwb, dcbb, dcwc, dcbc, ddtb, dalog, ddsk, dng = dv_ssd
    small = dict(
        norm1_g=d_n1[0], norm2_g=d_n2[0],
        ssd_conv_w=jnp.concatenate([dcwx[:, :512], dcwb[:, 512:768], dcwc[:, 768:]], axis=1),
        ssd_conv_b=jnp.concatenate([dcbx[0, :512], dcbb[0, 512:768], dcbc[0, 768:]]),
        ssd_dt_bias=ddtb[0, :8], ssd_a_log=dalog[0, :8], ssd_d=ddsk[0, :8], ssd_norm_g=dng[0],
        pool_w=jnp.stack([d_wbd[64 * g:64 * g + 64, 64 * g:64 * g + 64] for g in range(4)]), pool_scale=d_pscale[0],
        ffn_conv_w=_ffn_block_perm(dv_ffn[0]), ffn_conv_b=_ffn_block_perm(dv_ffn[1][0]),
    )
    dmod = jnp.concatenate([d_sh1[0], d_sc1[0], d_g1[0], d_sh2[0], d_sc2[0], d_g2[0]])
    return dx, [g_win, g_wout, g_up, g_down], small, dmod


def kernel(x, c, positions, ada_w, ada_b, norm1_g, w_in, ssd_conv_w, ssd_conv_b, ssd_dt_bias, ssd_a_log, ssd_d, ssd_norm_g, pool_w, pool_scale, w_out, norm2_g, ffn_up, ffn_conv_w, ffn_conv_b, ffn_down, final_g, loss_target, m_ada_w, m_ada_b, m_norm1_g, m_w_in, m_ssd_conv_w, m_ssd_conv_b, m_ssd_dt_bias, m_ssd_a_log, m_ssd_d, m_ssd_norm_g, m_pool_w, m_pool_scale, m_w_out, m_norm2_g, m_ffn_up, m_ffn_conv_w, m_ffn_conv_b, m_ffn_down, m_final_g, v_ada_w, v_ada_b, v_norm1_g, v_w_in, v_ssd_conv_w, v_ssd_conv_b, v_ssd_dt_bias, v_ssd_a_log, v_ssd_d, v_ssd_norm_g, v_pool_w, v_pool_scale, v_w_out, v_norm2_g, v_ffn_up, v_ffn_conv_w, v_ffn_conv_b, v_ffn_down, v_final_g):
    args = dict(locals())
    w = {n: args[n] for n in _WEIGHTS}
    m = {n: args["m_" + n] for n in _WEIGHTS}
    v = {n: args["v_" + n] for n in _WEIGHTS}
    d = D_MODEL
    me = (lax.axis_index("x"), lax.axis_index("y"), lax.axis_index("c"))
    chip, dev = _chip(me), _dev(me)
    RIDERS.reset()

    shapes0 = [c.shape, ssd_conv_w.shape, ffn_conv_w.shape]
    pack0 = _pack([c, ssd_conv_w, ffn_conv_w])
    shards = [w[n].astype(BF16) for n in _BIG]
    g0 = allgather8("gather_c_conv", pack0)
    c16 = jnp.pad(g0[:, :d // 128, :].reshape(8, d), ((0, 8), (0, 0)))
    by_chip = [_unpack(g0[2 * j], shapes0) for j in range(4)]
    conv_w_full = jnp.concatenate([p[1] for p in by_chip], axis=-1)
    fconv_w_full = jnp.concatenate([p[2] for p in by_chip], axis=-1)

    modp = ada_forward("ada_fwd", c16, ada_w)[:, :8]
    pack1 = _pack([modp])
    g1, w_in0 = ride_alone("gather_mod_w_in0", merge_rides([allgather8_ride(pack1), gather_ride(0, [shards[0]])]))
    g1 = _with_own(g1, pack1)
    modfull = jnp.concatenate([_unpack(g1[2 * j], [modp.shape])[0] for j in range(4)], axis=-1)
    mod = lax.dynamic_index_in_dim(modfull, dev, axis=1, keepdims=False) + ada_b
    modv = [[mod[i, q * d:(q + 1) * d][None] for q in range(6)] for i in range(DEPTH)]


    def weight(k, layer, got):
        full = lax.dynamic_update_slice(got, shards[k][layer][None], (chip, 0, 0))
        if k == 0:
            return _w_in_from_chips(full)
        if k == 2:
            return jnp.concatenate([full[j] for j in FFN_BLOCK_ORDER], axis=1)
        return full.reshape(-1, full.shape[2])

    def later(k, layer, *sources):
        made = []

        def get():
            if not made:
                got = [RIDERS.result(host)[pos] for host, pos in sources]
                made.append(weight(k, layer, got[0] if len(got) == 1 else jnp.concatenate(got, axis=1)))
            return made[0]
        return get

    cs3, sn3 = rope_tables(positions[0])
    eye4 = jnp.eye(4, dtype=F32)
    wts, sps = [], []
    for i in range(DEPTH):
        wts.append(dict(
            norm1_g=norm1_g[i][None], norm2_g=norm2_g[i][None], pool_scale=pool_scale[i][None],
            wbd=(eye4[:, None, :, None] * pool_w[i][:, :, None, :]).reshape(POOL_W, POOL_W),
            ffn_conv_w=_ffn_block_perm(fconv_w_full[i]), ffn_conv_b=_ffn_block_perm(ffn_conv_b[i])[None]))
        sps.append(dict(cw=conv_w_full[i], cb=ssd_conv_b[i][None], dtb=_pad_lanes(ssd_dt_bias[i]), alog=_pad_lanes(ssd_a_log[i]),
                        dsk=_pad_lanes(ssd_d[i]), ng=ssd_norm_g[i][None]))

    RIDERS.book("l0_ssd", gather_ride(0, [shards[1], shards[3]]))
    half = shards[2].shape[1] // 2
    RIDERS.book("l0_attn0", gather_ride(0, [shards[2][:, :half]]))
    RIDERS.book("l0_attn1", gather_ride(0, [shards[2][:, half:]]))
    wts[0].update(w_in=weight(0, 0, w_in0), w_out=later(1, 0, ("l0_ssd", 0)), ffn_down=later(3, 0, ("l0_ssd", 1)),
                  ffn_up=later(2, 0, ("l0_attn0", 0), ("l0_attn1", 0)))
    RIDERS.book("l0_attn2", gather_ride(1, [shards[0], shards[1]]))
    RIDERS.book("l0_up", gather_ride(1, [shards[3]]))
    RIDERS.book("l0_down", gather_ride(1, [shards[2]]))
    wts[1].update(w_in=later(0, 1, ("l0_attn2", 0)), w_out=later(1, 1, ("l0_attn2", 1)), ffn_up=later(2, 1, ("l0_down", 0)),
                  ffn_down=later(3, 1, ("l0_up", 0)))
    x1_, keep0 = _layer_forward(0, x, modv[0], wts[0], sps[0], cs3, sn3)
    xc, keep1 = _layer_forward(1, x1_, modv[1], wts[1], sps[1], cs3, sn3)
    keeps = [keep0, keep1]
    lossblk, dx, d_final = final_loss("final_loss", xc, loss_target, final_g[None])

    small_g, dmods = [None] * DEPTH, [None] * DEPTH
    part_sum, from_chips = [[None] * 4 for _ in range(DEPTH)], [[None] * 4 for _ in range(DEPTH)]

    def owner_sum(layer, ks, mine, theirs):
        for k, g, t in zip(ks, mine, theirs):
            part_sum[layer][k] = add_arrays(f"sum_cores{layer}_{_BIG[k]}", [g, t], BF16)

    dx, by_chip1, small_g[1], dmods[1] = _layer_backward(1, dx, keeps[1], modv[1], wts[1], sps[1], cs3, sn3)
    RIDERS.book("l0_ffn_b", to_owner_ride(1, by_chip1))

    def after_ffn_b():
        owner_sum(1, range(4), by_chip1, RIDERS.result("l0_ffn_b"))
        RIDERS.book("l0_up_bx", scatter_ride(1, [part_sum[1][2]]))
        RIDERS.book("l0_up_bw", scatter_ride(1, [part_sum[1][0], part_sum[1][1]]))
        RIDERS.book("l0_norm2_b", scatter_ride(1, [part_sum[1][3]]))

    early = []

    def after_wout_bw(g_wout, g_up, g_down):
        early.extend([g_wout, g_up, g_down])
        RIDERS.book("l0_ssd_b", to_owner_ride(0, early))

    def after_ssd_b():
        owner_sum(0, [1, 2, 3], early, RIDERS.result("l0_ssd_b"))
        for host, k in (("l0_attn0_b", 2), ("l0_attn1_b", 3), ("l0_attn2_b", 1)):
            RIDERS.book(host, scatter_ride(0, [part_sum[0][k]]))

    last = []

    def after_proj_bw(g_win):
        last.append(g_win)
        RIDERS.book("l0_proj_bx", to_owner_ride(0, last))

    def after_proj_bx():
        owner_sum(0, [0], last, RIDERS.result("l0_proj_bx"))
        RIDERS.book("l0_norm1_b", scatter_ride(0, [part_sum[0][0]]))

    hooks = dict(ffn_b=after_ffn_b, wout_bw=after_wout_bw, ssd_b=after_ssd_b, proj_bw=after_proj_bw, proj_bx=after_proj_bx)
    dx, _, small_g[0], dmods[0] = _layer_backward(0, dx, keeps[0], modv[0], wts[0], sps[0], cs3, sn3, after=hooks)
    from_chips[1][2], (from_chips[1][0], from_chips[1][1]) = RIDERS.result("l0_up_bx")[0], RIDERS.result("l0_up_bw")
    from_chips[1][3] = RIDERS.result("l0_norm2_b")[0]
    for host, k in (("l0_attn0_b", 2), ("l0_attn1_b", 3), ("l0_attn2_b", 1), ("l0_norm1_b", 0)):
        from_chips[0][k] = RIDERS.result(host)[0]
    mine = [sum_chips_mine(f"sum_chips_{n}", part_sum[0][k], from_chips[0][k], part_sum[1][k], from_chips[1][k])
            for k, n in enumerate(_BIG)]

    part = dict(ada_b=jnp.stack(dmods), final_g=d_final[0])
    for n in _SMALL:
        if n not in part:
            part[n] = jnp.stack([small_g[i][n] for i in range(DEPTH)])
    full_shapes = [part[n].shape for n in _SMALL] + [(1,)]
    pack_small = _pack([part[n] for n in _SMALL] + [lossblk[0, :1]])
    *theirs, gs = ride_alone("swap_r_gather_small", merge_rides([swap_ride(mine), allgather8_ride(pack_small)]))
    gs = _with_own(gs, pack_small)
    tot = _unpack(sum_slots("sum_small", gs, 8)[0], full_shapes)
    loss = tot[-1].reshape(())
    grads = {}
    small_tot = dict(zip(_SMALL, tot))
    dmod_all = gs[:, :DEPTH * 6 * d // 128, :].reshape(8, DEPTH, 6 * d)
    for n, ncol in _COL_SHARDED_SMALL.items():
        small_tot[n] = lax.dynamic_slice_in_dim(small_tot[n], chip * ncol, ncol, axis=2)
    grads.update(small_tot)

    ncol = ada_w.shape[2]
    dm = lax.dynamic_slice_in_dim(dmod_all, chip * ncol, ncol, axis=2).transpose(1, 0, 2)
    upd = {}
    g_ada, *upd["ada_w"] = ada_backward("ada_bwd", c16, jnp.pad(dm, ((0, 0), (0, 8), (0, 0))), ada_w, m["ada_w"], v["ada_w"])
    grads["ada_w"] = g_ada

    for n, a, g in zip(_BIG, mine, theirs):
        grads[n], *upd[n] = adamw_layers(f"adam_{n}", w[n], a, g, m[n], v[n])
    shapes_s = [w[n].shape for n in _SMALL]
    packed = [_pack([src[n] for n in _SMALL]) for src in (w, grads, m, v)]
    outs_s = [_unpack(o, shapes_s) for o in adamw("adam_small", *packed)]
    for q, n in enumerate(_SMALL):
        upd[n] = [outs_s[0][q], outs_s[1][q], outs_s[2][q]]

    return (loss, dx, *[grads[n] for n in _WEIGHTS], *[upd[n][0] for n in _WEIGHTS], *[upd[n][1] for n in _WEIGHTS],
            *[upd[n][2] for n in _WEIGHTS])


def ride_alone(name, ride):
    ni, no = len(ride.ins), len(ride.out_shapes)

    def body(*refs):
        ride.begin(refs[:ni], refs[ni:ni + no], refs[ni + no:])
        ride.end(refs[:ni], refs[ni:ni + no], refs[ni + no:])

    in_specs, out_specs, scratch = ride.specs()
    return list(pl.pallas_call(body, name=name, in_specs=in_specs, out_specs=out_specs, out_shape=ride.out_shapes,
                               scratch_shapes=scratch)(*ride.ins))


def mm(name, a, b, mode, out_dtype=F32, res=None, gate=None, tm=1408, tn=1536, tk=1408, into=None):
    ride = RIDERS.take(name)
    if mode == "nn":
        (m, k), n = a.shape, b.shape[1]
    elif mode == "nt":
        (m, k), n = a.shape, b.shape[0]
    else:
        (k, m), n = a.shape, b.shape[1]
    tm, tn, tk = _tile(m, tm), _tile(n, tn), _tile(k, tk)
    ni, nj, nk = m // tm, n // tn, k // tk
    a_spec = pl.BlockSpec((tk, tm), lambda i, j, q: (q, i)) if mode == "tn" else pl.BlockSpec((tm, tk), lambda i, j, q: (i, q))
    b_spec = pl.BlockSpec((tn, tk), lambda i, j, q: (j, q)) if mode == "nt" else pl.BlockSpec((tk, tn), lambda i, j, q: (q, j))
    o_spec = pl.BlockSpec((tm, tn), lambda i, j, q: (i, j))
    fused = res is not None
    lead = 0 if into is None else len(into[0]) - 2
    first = (0,) * lead + (slice(None), slice(None))
    ins, in_specs = [a, b], [a_spec, b_spec]
    out_shape, out_specs = [jax.ShapeDtypeStruct((m, n), out_dtype)], [o_spec]
    if fused:
        ins += [res, gate]
        in_specs += [o_spec, pl.BlockSpec((1, tn), lambda i, j, q: (0, j))]
        out_shape.append(jax.ShapeDtypeStruct((m, n), F32))
        out_specs.append(o_spec)
    if into is not None:
        shape, omap = into
        out_shape = [jax.ShapeDtypeStruct(shape, out_dtype)]
        out_specs = [pl.BlockSpec((1,) * lead + (tm, tn), lambda i, j, q: omap(i, j))]
    n_in, n_out = len(ins), len(out_shape)
    scratch = [pltpu.VMEM((tm, tn), F32)]
    if ride is not None:
        r_in, r_out, r_scr = ride.specs()
        ins, in_specs = ins + list(ride.ins), in_specs + r_in
        out_shape, out_specs = out_shape + list(ride.out_shapes), out_specs + r_out
        scratch = scratch + r_scr

    def body(*refs):
        a_ref, b_ref = refs[:2]
        o_ref = refs[len(ins)]
        acc = refs[len(ins) + len(out_shape)]
        i, j, q = pl.program_id(0), pl.program_id(1), pl.program_id(2)
        at = lambda x, y, z: jnp.logical_and(jnp.logical_and(i == x, j == y), q == z)
        r_refs = (refs[n_in:len(ins)], refs[len(ins) + n_out:len(ins) + len(out_shape)], refs[len(ins) + len(out_shape) + 1:])
        if ride is not None:
            ride.begin(*r_refs, at(0, 0, 0))

        @pl.when(q == 0)
        def _():
            acc[...] = jnp.zeros(acc.shape, F32)

        acc[...] += _mxu(a_ref[...], b_ref[...], mode)

        @pl.when(q == nk - 1)
        def _():
            o_ref[first] = acc[...].astype(o_ref.dtype)
            if fused:
                refs[len(ins) + 1][...] = refs[2][...] + refs[3][...] * acc[...]

        if ride is not None:
            ride.end(*r_refs, at(ni - 1, nj - 1, nk - 1))

    sem = ("arbitrary",) * 3 if ride is not None else ("parallel", "parallel", "arbitrary")
    out = pl.pallas_call(
        body, name=name, grid=(ni, nj, nk), in_specs=in_specs, out_specs=out_specs, out_shape=out_shape, scratch_shapes=scratch,
        compiler_params=pltpu.CompilerParams(dimension_semantics=sem, vmem_limit_bytes=VMEM_LIMIT_BYTES),
    )(*ins)
    if ride is not None:
        RIDERS.done[name] = list(out[n_out:])
    return tuple(out[:n_out]) if fused else out[0]


def add_arrays(name, arrs, out_dtype=F32):
    nb, r, c = arrs[0].shape
    t = _tile(r, 256, 8)
    (out,), _ = scan_fwd(name, _sum_fn, nb=nb, nchunk=r // t, t=t, rows=[Row(a, fb=lambda b: b) for a in arrs], vecs=[], carries=[],
                         outs=[out_row((nb, r, c), out_dtype, fb=lambda b: b)], save=False)
    return out


def _sum_chips_mine_fn(ci, b, carries, rows, vecs):
    mine_layer = lax.axis_index("c")
    chip = 2 * lax.axis_index("x") + lax.axis_index("y")
    tot = None
    for j in range(4):
        own = jnp.where(mine_layer == 0, rows[j], rows[8 + j])
        sent = jnp.where(mine_layer == 0, rows[4 + j], rows[12 + j])
        term = jnp.where(chip == j, own, sent).astype(F32)
        tot = term if tot is None else tot + term
    return [], [tot]


def sum_chips_mine(name, p0, q0, p1, q1):
    _, r, c = p0.shape
    t = _tile(r, 256, 8)
    rows = [Row(a, fb=(lambda b, j=j: j)) for a in (p0, q0, p1, q1) for j in range(4)]
    (out,), _ = scan_fwd(name, _sum_chips_mine_fn, nb=1, nchunk=r // t, t=t, rows=rows, vecs=[], carries=[],
                         outs=[out_row((1, r, c))], save=False)
    return out[0]


def _remote(src, dst, send_sems, recv_sems, k, to):
    return pltpu.make_async_remote_copy(src_ref=src, dst_ref=dst, send_sem=send_sems.at[k], recv_sem=recv_sems.at[k],
                                        device_id=to, device_id_type=MESH)


def gather_ride(layer, shards):
    na = len(shards)

    def rows(ref, c):
        h = ref.shape[0] // 2
        return ref.at[pl.ds(c * h, h)]

    def start(ins, outs, ss, rs, me):
        for k in range(na):
            for p, mask in enumerate(CHIP_PEERS):
                _remote(rows(ins[k].at[layer], me[2]), rows(outs[k].at[_chip(me)], me[2]), ss, rs, 6 * k + p, _flip(mask, me)).start()

    def finish(ins, outs, ss, rs, me):
        sibling = _flip(SIBLING[0], me)
        for k in range(na):
            for p, mask in enumerate(CHIP_PEERS):
                got = rows(outs[k].at[_chip(_flip(mask, me))], me[2])
                _remote(rows(ins[k].at[layer], me[2]), got, ss, rs, 6 * k + p, _flip(mask, me)).wait_recv()
                _remote(got, got, ss, rs, 6 * k + 3 + p, sibling).start()
        for k in range(na):
            for p, mask in enumerate(CHIP_PEERS):
                got = rows(outs[k].at[_chip(_flip(mask, me))], me[2])
                other = rows(outs[k].at[_chip(_flip(mask, me))], 1 - me[2])
                _remote(other, other, ss, rs, 6 * k + 3 + p, sibling).wait_recv()
                _remote(rows(ins[k].at[layer], me[2]), got, ss, rs, 6 * k + p, _flip(mask, me)).wait_send()
                _remote(got, got, ss, rs, 6 * k + 3 + p, sibling).wait_send()

    return Ride(list(shards), [jax.ShapeDtypeStruct((4,) + a.shape[1:], a.dtype) for a in shards], 6 * na, start, finish)


def scatter_ride(layer, parts):
    na = len(parts)

    def start(ins, outs, ss, rs, me):
        @pl.when(me[2] == layer)
        def _():
            for k in range(na):
                for p, mask in enumerate(CHIP_PEERS):
                    peer = _flip(mask, me)
                    _remote(ins[k].at[_chip(peer)], outs[k].at[_chip(me)], ss, rs, 3 * k + p, peer).start()

    def finish(ins, outs, ss, rs, me):
        @pl.when(me[2] == layer)
        def _():
            for k in range(na):
                for p, mask in enumerate(CHIP_PEERS):
                    peer = _flip(mask, me)
                    _remote(ins[k].at[_chip(peer)], outs[k].at[_chip(peer)], ss, rs, 3 * k + p, peer).wait_recv()
                    _remote(ins[k].at[_chip(peer)], outs[k].at[_chip(me)], ss, rs, 3 * k + p, peer).wait_send()

    return Ride(list(parts), [jax.ShapeDtypeStruct(a.shape, a.dtype) for a in parts], 3 * na, start, finish)


def to_owner_ride(layer, arrays):
    na = len(arrays)

    def start(ins, outs, ss, rs, me):
        @pl.when(me[2] != layer)
        def _():
            for k in range(na):
                _remote(ins[k], outs[k], ss, rs, k, _flip(SIBLING[0], me)).start()

    def finish(ins, outs, ss, rs, me):
        for k in range(na):
            cp = _remote(ins[k], outs[k], ss, rs, k, _flip(SIBLING[0], me))
            pl.when(me[2] != layer)(cp.wait_send)
            pl.when(me[2] == layer)(cp.wait_recv)

    return Ride(list(arrays), [jax.ShapeDtypeStruct(a.shape, a.dtype) for a in arrays], na, start, finish)


def allgather8_ride(a):
    def start(ins, outs, ss, rs, me):
        for p, mask in enumerate(ALL_PEERS):
            _remote(ins[0], outs[0].at[_dev(me)], ss, rs, p, _flip(mask, me)).start()

    def finish(ins, outs, ss, rs, me):
        for p, mask in enumerate(ALL_PEERS):
            peer = _flip(mask, me)
            _remote(ins[0], outs[0].at[_dev(peer)], ss, rs, p, peer).wait_recv()
            _remote(ins[0], outs[0].at[_dev(me)], ss, rs, p, peer).wait_send()

    return Ride([a], [jax.ShapeDtypeStruct((8,) + a.shape, a.dtype)], len(ALL_PEERS), start, finish)


def _with_own(gathered, own):
    me = _dev((lax.axis_index("x"), lax.axis_index("y"), lax.axis_index("c")))
    return jnp.where((jnp.arange(8) == me)[:, None, None], own[None], gathered)


def swap_ride(arrays):
    na = len(arrays)

    def start(ins, outs, ss, rs, me):
        for k in range(na):
            _remote(ins[k], outs[k], ss, rs, k, _flip(SIBLING[0], me)).start()

    def finish(ins, outs, ss, rs, me):
        for k in range(na):
            cp = _remote(ins[k], outs[k], ss, rs, k, _flip(SIBLING[0], me))
            cp.wait_recv()
            cp.wait_send()

    return Ride(list(arrays), [jax.ShapeDtypeStruct(a.shape, a.dtype) for a in arrays], na, start, finish)


class _Shifted:
    def __init__(self, ref, offset):
        self.ref, self.offset = ref, offset

    @property
    def at(self):
        return self

    def __getitem__(self, k):
        return self.ref.at[self.offset + k]


def merge_rides(rides):
    def spans(counts):
        out, o = [], 0
        for n in counts:
            out.append((o, o + n))
            o += n
        return out

    si, so = spans([len(r.ins) for r in rides]), spans([len(r.out_shapes) for r in rides])
    ss_ = spans([r.nsem for r in rides])

    def each(method):
        def run(ins, outs, ss, rs, me):
            for r, (i0, i1), (o0, o1), (s0, _) in zip(rides, si, so, ss_):
                getattr(r, method)(ins[i0:i1], outs[o0:o1], _Shifted(ss, s0), _Shifted(rs, s0), me)
        return run

    return Ride([a for r in rides for a in r.ins], [s for r in rides for s in r.out_shapes], sum(r.nsem for r in rides),
                each("start"), each("finish"))


def _w_in_from_chips(a):
    c2 = a[2]
    pad = jnp.zeros((c2.shape[0], IN_WP - IN_W), c2.dtype)
    return jnp.concatenate([a[0], a[1], c2[:, :252], c2[:, 260:516], c2[:, 252:260], pad, c2[:, 516:], a[3]], axis=1)
```

```python
import functools

import numpy as np
import jax
import jax.numpy as jnp
from jax import lax
from jax.experimental import pallas as pl
from jax.experimental.pallas import tpu as pltpu

F32 = jnp.float32
BF16 = jnp.bfloat16
MESH = pl.DeviceIdType.MESH

D_MODEL = 1024
DEPTH = 2
SSD_INNER = 512
POOL_W = 256
POOL_WINDOWS = (2, 4, 8, 16)
ATT_W = 256
ATT_HEADS = 4
ATT_HEAD_DIM = 64
ATT_PATTERNS = ((128, 1), (512, 4), (2048, 16))
ATT_BLOCK = 128
ROT_DIM = 16
ROPE_THETA = 500000.0
IN_W = 2568
IN_WP = 2688
IN_MAIN = 1920
FFN_DIM = 2816
NORM_EPS = 1e-6
ADAM_LR, ADAM_B1, ADAM_B2, ADAM_EPS, ADAM_WD, ADAM_STEP = 0.001, 0.9, 0.999, 1e-08, 0.01, 10

VMEM_LIMIT_BYTES = 56 * 1024 * 1024
NEG = -1e30


def _mxu(a, b, mode):
    dims = {"nn": ((1,), (0,)), "nt": ((1,), (1,)), "tn": ((0,), (0,))}[mode]
    return lax.dot_general(a.astype(BF16), b.astype(BF16), (dims, ((), ())), preferred_element_type=F32)


@functools.partial(jax.custom_vjp, nondiff_argnums=(2,))
def _bdot(a, b, mode):
    return _mxu(a, b, mode)


def _bdot_fwd(a, b, mode):
    return _mxu(a, b, mode), (a, b)


def _bdot_bwd(mode, res, g):
    a, b = res
    if mode == "nn":
        return _mxu(g, b, "nt"), _mxu(a, g, "tn")
    if mode == "nt":
        return _mxu(g, b, "nn"), _mxu(g, a, "tn")
    return _mxu(b, g, "nt"), _mxu(a, g, "nn")


_bdot.defvjp(_bdot_fwd, _bdot_bwd)


def _iota(shape, dim):
    return lax.broadcasted_iota(jnp.int32, shape, dim)


def _make_shift(h):
    @functools.partial(jax.custom_vjp, nondiff_argnums=(2,))
    def shift(halo, cur, k):
        if k == 0:
            return cur
        full = jnp.concatenate([halo, cur], axis=0)
        return pltpu.roll(full, k, 0)[h:]

    def fwd(halo, cur, k):
        return shift(halo, cur, k), None

    def bwd(k, _, g):
        t, w = g.shape
        if k == 0:
            return jnp.zeros((h, w), F32), g
        d_cur = jnp.where(_iota((t, w), 0) < t - k, pltpu.roll(g, t - k, 0), 0.0)
        top = g[:h]
        d_halo = jnp.where(_iota((h, w), 0) >= h - k, pltpu.roll(top, h - k, 0) if k < h else top, 0.0)
        return d_halo, d_cur

    shift.defvjp(fwd, bwd)
    return shift


_shift8 = _make_shift(8)
_shift16 = _make_shift(16)


def _make_tail(h):
    @jax.custom_vjp
    def tail(x):
        return x[x.shape[0] - h:]

    def fwd(x):
        return tail(x), x.shape[0]

    def bwd(t, g):
        return (jnp.concatenate([jnp.zeros((t - h, g.shape[1]), F32), g], axis=0),)

    tail.defvjp(fwd, bwd)
    return tail


_tail8 = _make_tail(8)
_tail16 = _make_tail(16)


@jax.custom_vjp
def _cumsum_rows(x):
    t = x.shape[0]
    row, s = _iota(x.shape, 0), 1
    while s < t:
        x = x + jnp.where(row >= s, pltpu.roll(x, s, 0), 0.0)
        s *= 2
    return x


def _cumsum_rows_fwd(x):
    return _cumsum_rows(x), None


def _cumsum_rows_bwd(_, g):
    t = g.shape[0]
    row, s = _iota(g.shape, 0), 1
    while s < t:
        g = g + jnp.where(row < t - s, pltpu.roll(g, t - s, 0), 0.0)
        s *= 2
    return (g,)


_cumsum_rows.defvjp(_cumsum_rows_fwd, _cumsum_rows_bwd)


@jax.custom_vjp
def _rot_pairs(t):
    e = _iota(t.shape, 1) % ATT_HEAD_DIM
    n = t.shape[1]
    return jnp.where(e < 8, -pltpu.roll(t, n - 8, 1), jnp.where(e < 16, pltpu.roll(t, 8, 1), 0.0))


def _rot_pairs_fwd(t):
    return _rot_pairs(t), None


def _rot_pairs_bwd(_, g):
    e = _iota(g.shape, 1) % ATT_HEAD_DIM
    n = g.shape[1]
    return (pltpu.roll(jnp.where(e < 8, -g, 0.0), 8, 1) + pltpu.roll(jnp.where(jnp.logical_and(e >= 8, e < 16), g, 0.0), n - 8, 1),)


_rot_pairs.defvjp(_rot_pairs_fwd, _rot_pairs_bwd)


def _make_thirds():
    @jax.custom_vjp
    def thirds(x):
        w = x.shape[1] // 3
        return x[:, :w], x[:, w:2 * w], x[:, 2 * w:]

    def fwd(x):
        return thirds(x), None

    def bwd(_, g):
        return (jnp.concatenate(g, axis=1),)

    thirds.defvjp(fwd, bwd)
    return thirds


_thirds = _make_thirds()


def _rowk(w, k):
    return jnp.sum(jnp.where(_iota(w.shape, 0) == k, w, 0.0), axis=0, keepdims=True)


def _silu(x):
    return x * (0.5 * jnp.tanh(0.5 * x) + 0.5)


def _softplus(x):
    return jnp.maximum(x, 0.0) + jnp.log(1.0 + jnp.exp(-jnp.abs(x)))


def _tile(dim, target, unit=128):
    if dim <= target:
        return dim
    best = None
    for t in range(unit, target + 1, unit):
        if dim % t == 0:
            best = t
    assert best is not None, (dim, target)
    return best


class Ride:
    def __init__(self, ins, out_shapes, nsem, start, finish):
        self.ins, self.out_shapes, self.nsem, self.start, self.finish = ins, out_shapes, nsem, start, finish

    def specs(self):
        hbm = pl.BlockSpec(memory_space=pl.ANY)
        return [hbm] * len(self.ins), [hbm] * len(self.out_shapes), [pltpu.SemaphoreType.DMA((self.nsem,))] * 2

    def begin(self, in_refs, out_refs, sems, cond=None):
        me = (lax.axis_index("x"), lax.axis_index("y"), lax.axis_index("c"))
        go = lambda: self.start(in_refs, out_refs, sems[0], sems[1], me)
        go() if cond is None else pl.when(cond)(go)

    def end(self, in_refs, out_refs, sems, cond=None):
        me = (lax.axis_index("x"), lax.axis_index("y"), lax.axis_index("c"))
        go = lambda: self.finish(in_refs, out_refs, sems[0], sems[1], me)
        go() if cond is None else pl.when(cond)(go)


class _Riders:
    def reset(self):
        self.booked, self.done = {}, {}

    def book(self, host, ride):
        assert host not in self.booked, host
        self.booked[host] = ride

    def take(self, host):
        return self.booked.pop(host, None)

    def result(self, host):
        return self.done[host]


RIDERS = _Riders()
RIDERS.reset()


class Row:
    def __init__(self, arr, w=None, fb=None, fc=None, diff=True, slot=False, dcols=None, dfc=None, ddtype=F32, view=None):
        self.ddtype = ddtype
        self.view = view
        self.arr = arr
        self.w = arr.shape[2] if w is None else w
        self.fb = (lambda b: 0) if fb is None else fb
        self.fc = (lambda b: 0) if fc is None else fc
        self.diff = diff
        self.slot = slot
        self.dcols = dcols
        self.dfc = dfc


class Vec:
    def __init__(self, arr, w=None, fc=None, diff=True):
        self.arr = arr
        self.w = arr.shape[1] if w is None else w
        self.fc = fc
        self.diff = diff


def _row_spec(r, t, nchunk, reverse):
    shape = (1, t, r.w) if r.view is None else (1, t // r.view, r.view * r.w)
    if reverse:
        return pl.BlockSpec(shape, lambda b, i, r=r: (r.fb(b), nchunk - 1 - i, r.fc(b)))
    return pl.BlockSpec(shape, lambda b, i, r=r: (r.fb(b), i, r.fc(b)))


def _load_row(ref, r, t, scr):
    if r.view is None:
        return ref[0]
    d, w = r.view, r.w
    for q in range(d):
        for j in range(w // 128):
            scr[j, pl.ds(q, t // d, stride=d), :] = ref[0, :, q * w + 128 * j:q * w + 128 * (j + 1)].astype(F32)
    return jnp.concatenate([scr[j] for j in range(w // 128)], axis=1)


def _store_row(ref, r, t, scr, val):
    if r.view is None:
        ref[0] = val.astype(ref.dtype)
        return
    d, w = r.view, r.w
    for j in range(w // 128):
        scr[j] = val[:, 128 * j:128 * (j + 1)]
    for q in range(d):
        for j in range(w // 128):
            ref[0, :, q * w + 128 * j:q * w + 128 * (j + 1)] = scr[j, pl.ds(q, t // d, stride=d), :].astype(ref.dtype)


def _view_scratch(specs, t):
    ws = [r.w for r in specs if r.view is not None]
    return [pltpu.VMEM((max(ws) // 128, t, 128), F32)] if ws else []


def _vec_spec(v):
    if v.fc is None:
        return pl.BlockSpec(v.arr.shape, lambda b, i: (0, 0))
    return pl.BlockSpec((v.arr.shape[0], v.w), lambda b, i, v=v: (0, v.fc(b)))


def _cparams():
    return pltpu.CompilerParams(dimension_semantics=("arbitrary", "arbitrary"), vmem_limit_bytes=VMEM_LIMIT_BYTES)


def scan_fwd(name, fn, *, nb, nchunk, t, rows, vecs, carries, outs, save):
    nr, nv, nc, no = len(rows), len(vecs), len(carries), len(outs)
    ns = nc if save else 0
    ride = RIDERS.take(name)
    r_in, r_out, r_scr = ride.specs() if ride else ([], [], [])

    def body(*refs):
        p = 0
        row_refs = refs[p:p + nr]; p += nr
        vec_refs = refs[p:p + nv]; p += nv
        ride_in = refs[p:p + len(r_in)]; p += len(r_in)
        out_refs = refs[p:p + no]; p += no
        save_refs = refs[p:p + ns]; p += ns
        ride_out = refs[p:p + len(r_out)]; p += len(r_out)
        car = refs[p:p + nc]; p += nc
        scr = refs[p] if stage else None
        sems = refs[p + len(stage):]
        b, i = pl.program_id(0), pl.program_id(1)
        if ride:
            ride.begin(ride_in, ride_out, sems, jnp.logical_and(b == 0, i == 0))
        if nc:
            @pl.when(i == 0)
            def _():
                for c_ref in car:
                    c_ref[...] = jnp.zeros(c_ref.shape, F32)
        cin = [c_ref[...] for c_ref in car]
        if save:
            for s_ref, cv in zip(save_refs, cin):
                s_ref[0, 0] = cv
        new_c, o = fn(i, b, cin, [_load_row(ref, r, t, scr) for ref, r in zip(row_refs, rows)], [v[...] for v in vec_refs])
        for c_ref, cv in zip(car, new_c):
            c_ref[...] = cv
        for o_ref, spec, ov in zip(out_refs, outs, o):
            _store_row(o_ref, spec, t, scr, ov)
        if ride:
            ride.end(ride_in, ride_out, sems, jnp.logical_and(b == nb - 1, i == nchunk - 1))

    stage = _view_scratch(list(rows) + list(outs), t)
    out_shape = [o.arr for o in outs]
    out_specs = [_row_spec(o, t, nchunk, False) for o in outs]
    if save:
        for cs in carries:
            out_shape.append(jax.ShapeDtypeStruct((nb, nchunk) + tuple(cs), F32))
            out_specs.append(pl.BlockSpec((1, 1) + tuple(cs), lambda b, i: (b, i, 0, 0)))
    res = pl.pallas_call(
        body, name=name, grid=(nb, nchunk),
        in_specs=[_row_spec(r, t, nchunk, False) for r in rows] + [_vec_spec(v) for v in vecs] + r_in,
        out_specs=out_specs + r_out, out_shape=out_shape + (list(ride.out_shapes) if ride else []),
        scratch_shapes=[pltpu.VMEM(tuple(cs), F32) for cs in carries] + stage + r_scr,
        compiler_params=_cparams(),
    )(*[r.arr for r in rows], *[v.arr for v in vecs], *(ride.ins if ride else []))
    if ride:
        RIDERS.done[name] = list(res[no + ns:])
    return list(res[:no]), list(res[no:no + ns])


def scan_bwd(name, fn, *, nb, nchunk, t, rows, vecs, carries, saved, douts, adds=None):
    adds = adds or {}
    nr, nv, nc, no = len(rows), len(vecs), len(carries), len(douts)
    dri = [k for k, r in enumerate(rows) if r.diff]
    dvi = [k for k, v in enumerate(vecs) if v.diff]
    add_keys = sorted(adds)
    na = len(add_keys)
    ride = RIDERS.take(name)
    r_in, r_out, r_scr = ride.specs() if ride else ([], [], [])

    def body(*refs):
        p = 0
        row_refs = refs[p:p + nr]; p += nr
        vec_refs = refs[p:p + nv]; p += nv
        save_refs = refs[p:p + nc]; p += nc
        dout_refs = refs[p:p + no]; p += no
        add_refs = refs[p:p + na]; p += na
        ride_in = refs[p:p + len(r_in)]; p += len(r_in)
        drow_refs = refs[p:p + len(dri)]; p += len(dri)
        dvec_refs = refs[p:p + len(dvi)]; p += len(dvi)
        ride_out = refs[p:p + len(r_out)]; p += len(r_out)
        dcar = refs[p:p + nc]; p += nc
        scr = refs[p] if stage else None
        sems = refs[p + len(stage):]
        b, ir = pl.program_id(0), pl.program_id(1)
        ci = nchunk - 1 - ir
        if ride:
            ride.begin(ride_in, ride_out, sems, jnp.logical_and(b == 0, ir == 0))
        if nc:
            @pl.when(ir == 0)
            def _():
                for c_ref in dcar:
                    c_ref[...] = jnp.zeros(c_ref.shape, F32)
        rows_v = [_load_row(ref, r, t, scr) for ref, r in zip(row_refs, rows)]
        vecs_v = [v[...] for v in vec_refs]
        cin = [s[0, 0] for s in save_refs]
        dc = [c_ref[...] for c_ref in dcar]
        dout_v = [_load_row(ref, r, t, scr).astype(F32) for ref, r in zip(dout_refs, douts)]

        def f(cs, dr, dv):
            rr, vv = list(rows_v), list(vecs_v)
            for k, idx in enumerate(dri):
                rr[idx] = dr[k]
            for k, idx in enumerate(dvi):
                vv[idx] = dv[k]
            return fn(ci, b, cs, rr, vv)

        _, vjp = jax.vjp(f, cin, [rows_v[k].astype(F32) for k in dri], [vecs_v[k].astype(F32) for k in dvi])
        dcin, drows, dvecs = vjp((dc, dout_v))
        for c_ref, cv in zip(dcar, dcin):
            c_ref[...] = cv
        for k, (o_ref, ov) in enumerate(zip(drow_refs, drows)):
            if dri[k] in adds:
                ov = ov + add_refs[add_keys.index(dri[k])][0].astype(F32)
            _store_row(o_ref, rows[dri[k]], t, scr, ov)
        for k, (o_ref, ov) in enumerate(zip(dvec_refs, dvecs)):
            first = (ir == 0) if vecs[dvi[k]].fc is not None else jnp.logical_and(ir == 0, b == 0)

            @pl.when(first)
            def _(o_ref=o_ref, ov=ov):
                o_ref[...] = ov

            @pl.when(jnp.logical_not(first))
            def _(o_ref=o_ref, ov=ov):
                o_ref[...] += ov

        if ride:
            ride.end(ride_in, ride_out, sems, jnp.logical_and(b == nb - 1, ir == nchunk - 1))

    stage = _view_scratch(list(rows) + list(douts), t)
    in_specs = ([_row_spec(r, t, nchunk, True) for r in rows] + [_vec_spec(v) for v in vecs]
                + [pl.BlockSpec((1, 1) + tuple(cs), lambda b, i: (b, nchunk - 1 - i, 0, 0)) for cs in carries]
                + [_row_spec(d, t, nchunk, True) for d in douts]
                + [_row_spec(adds[k], t, nchunk, True) for k in add_keys] + r_in)
    out_shape, out_specs = [], []
    for k in dri:
        r = rows[k]
        if r.slot:
            out_shape.append(jax.ShapeDtypeStruct((nb, r.arr.shape[1], r.w), r.ddtype))
            out_specs.append(pl.BlockSpec((1, t, r.w), lambda b, i: (b, nchunk - 1 - i, 0)))
        elif r.dcols is not None:
            out_shape.append(jax.ShapeDtypeStruct((r.arr.shape[0], r.arr.shape[1], r.dcols), r.ddtype))
            out_specs.append(pl.BlockSpec((1, t, r.w), lambda b, i, r=r: (r.fb(b), nchunk - 1 - i, r.dfc(b))))
        else:
            out_shape.append(jax.ShapeDtypeStruct(r.arr.shape, r.ddtype))
            out_specs.append(_row_spec(r, t, nchunk, True))
    for k in dvi:
        out_shape.append(jax.ShapeDtypeStruct(vecs[k].arr.shape, F32))
        out_specs.append(_vec_spec(vecs[k]))
    nd = len(dri) + len(dvi)
    res = pl.pallas_call(
        body, name=name, grid=(nb, nchunk), in_specs=in_specs, out_specs=out_specs + r_out,
        out_shape=out_shape + (list(ride.out_shapes) if ride else []),
        scratch_shapes=[pltpu.VMEM(tuple(cs), F32) for cs in carries] + stage + r_scr,
        compiler_params=_cparams(),
    )(*[r.arr for r in rows], *[v.arr for v in vecs], *saved, *[d.arr for d in douts], *[adds[k].arr for k in add_keys],
      *(ride.ins if ride else []))
    if ride:
        RIDERS.done[name] = list(res[nd:])
    return list(res[:len(dri)]), list(res[len(dri):nd])


def out_row(shape, dtype=F32, w=None, fb=None, fc=None):
    return Row(jax.ShapeDtypeStruct(shape, dtype), w, fb, fc)


def _conv(shift, halo, cur, w, bias, taps):
    y = bias
    for k in range(taps):
        y = y + _rowk(w, k) * shift(halo, cur, taps - 1 - k)
    return y


def _ssd_fn(ci, b, carries, rows, vecs):
    cx, cb_, cc, ht = carries
    z, xr, br, cr, dtr = rows
    cwx, cbx, cwb, cbb, cwc, cbc, dtb, alog, dsk, ng = vecs
    t = z.shape[0]
    xs = _silu(_conv(_shift8, cx, xr, cwx, cbx, 4))
    bm = _silu(_conv(_shift8, cb_, br, cwb, cbb, 4))
    cm = _silu(_conv(_shift8, cc, cr, cwc, cbc, 4))
    dt = _softplus(dtr + dtb)
    acol = _cumsum_rows(dt * (-jnp.exp(alog)))
    arow = acol.T
    r, c = _iota((t, t), 0), _iota((t, t), 1)
    causal = r >= c
    cbm = _bdot(cm, bm, "nt")
    lane, sub = _iota(acol.shape, 1), _iota(arow.shape, 0)
    colh = _iota(xs.shape, 1) // 64
    a, dtx, dx, acs = jnp.zeros(xs.shape, F32), jnp.zeros(xs.shape, F32), jnp.zeros((1, xs.shape[1]), F32), []
    for j in range(4):
        h = 4 * b + j
        ac = jnp.sum(jnp.where(lane == h, acol, 0.0), axis=1, keepdims=True)
        acs.append(ac)
        a = jnp.where(colh == j, ac, a)
        dtx = jnp.where(colh == j, jnp.sum(jnp.where(lane == h, dt, 0.0), axis=1, keepdims=True), dtx)
        dx = jnp.where(_iota(dx.shape, 1) // 64 == j, jnp.sum(jnp.where(_iota(dsk.shape, 1) == h, dsk, 0.0), axis=1, keepdims=True), dx)
    atot = jnp.sum(jnp.where(_iota(a.shape, 0) == t - 1, a, 0.0), axis=0, keepdims=True)
    x = xs * dtx
    ydiag = jnp.zeros(x.shape, F32)
    for j in range(4):
        ar = jnp.sum(jnp.where(sub == 4 * b + j, arow, 0.0), axis=0, keepdims=True)
        lmat = jnp.exp(jnp.where(causal, acs[j] - ar, NEG))
        ydiag = ydiag + _bdot(cbm * lmat, jnp.where(colh == j, x, 0.0), "nn")
    yoff = _bdot(cm, ht, "nn") * jnp.exp(a)
    ht_new = ht * jnp.exp(atot) + _bdot(bm, x * jnp.exp(atot - a), "tn")
    y = ydiag + yoff + dx * xs
    yz = y * _silu(z)
    yn = yz * lax.rsqrt(jnp.mean(yz * yz, axis=-1, keepdims=True) + NORM_EPS) * ng
    return [_tail8(xr), _tail8(br), _tail8(cr), ht_new], [yn]


_SSD_T = 256
_SSD_CARRIES = [(8, 256), (8, 128), (8, 128), (128, 256)]


def _ssd_io(proj3, p):
    own = lambda b: b
    rows = [Row(proj3, 256, fc=own, dcols=512, dfc=own, ddtype=BF16),
            Row(proj3, 256, fc=lambda b: 2 + b, dcols=512, dfc=own, ddtype=BF16),
            Row(proj3, 128, fc=lambda b: 8 + b, dcols=256, dfc=own, ddtype=BF16),
            Row(proj3, 128, fc=lambda b: 10 + b, dcols=256, dfc=own, ddtype=BF16),
            Row(proj3, 128, fc=lambda b: 14, slot=True)]
    vecs = [Vec(p["cw"], 256, lambda b: b), Vec(p["cb"], 256, lambda b: b),
            Vec(p["cw"], 128, lambda b: 4 + b), Vec(p["cb"], 128, lambda b: 4 + b),
            Vec(p["cw"], 128, lambda b: 6 + b), Vec(p["cb"], 128, lambda b: 6 + b),
            Vec(p["dtb"]), Vec(p["alog"]), Vec(p["dsk"]), Vec(p["ng"], 256, lambda b: b)]
    return rows, vecs


def ssd_forward(name, proj3, p):
    rows, vecs = _ssd_io(proj3, p)
    s = proj3.shape[1]
    (y,), saved = scan_fwd(name, _ssd_fn, nb=2, nchunk=s // _SSD_T, t=_SSD_T, rows=rows, vecs=vecs,
                           carries=_SSD_CARRIES, outs=[out_row((1, s, SSD_INNER), BF16, 256, fc=lambda b: b)], save=True)
    return y, saved


def ssd_backward(name, proj3, p, saved, dmix3):
    rows, vecs = _ssd_io(proj3, p)
    s = proj3.shape[1]
    drows, dvecs = scan_bwd(name, _ssd_fn, nb=2, nchunk=s // _SSD_T, t=_SSD_T, rows=rows, vecs=vecs,
                            carries=_SSD_CARRIES, saved=saved, douts=[Row(dmix3, 256, fc=lambda b: b)])
    return drows, dvecs


def _pool_fn(ci, b, carries, rows, vecs):
    (cu,) = carries
    (u,) = rows
    wbd, scale = vecs
    t = u.shape[0]
    pos = ci * t + _iota(u.shape, 0)
    grp = _iota(u.shape, 1) // 64
    acc, pooled, k = u, jnp.zeros(u.shape, F32), 1
    for gi, w in enumerate(POOL_WINDOWS):
        while k < w:
            acc = acc + _shift16(cu, u, k)
            k += 1
        pooled = jnp.where(grp == gi, acc / jnp.minimum(pos + 1, w).astype(F32), pooled)
    y = _bdot(pooled - u, wbd, "nn") * scale
    return [_tail16(u)], [y]


_POOL_T = 256


def _pool_io(proj3, wbd, scale):
    return [Row(proj3, 256, fc=lambda b: 6, dcols=256, dfc=lambda b: 0, ddtype=BF16)], [Vec(wbd), Vec(scale)]


def pool_forward(name, proj3, wbd, scale):
    rows, vecs = _pool_io(proj3, wbd, scale)
    s = proj3.shape[1]
    (y,), saved = scan_fwd(name, _pool_fn, nb=1, nchunk=s // _POOL_T, t=_POOL_T, rows=rows, vecs=vecs,
                           carries=[(16, 256)], outs=[out_row((1, s, POOL_W), BF16)], save=True)
    return y, saved


def pool_backward(name, proj3, wbd, scale, saved, dmix3):
    rows, vecs = _pool_io(proj3, wbd, scale)
    s = proj3.shape[1]
    return scan_bwd(name, _pool_fn, nb=1, nchunk=s // _POOL_T, t=_POOL_T, rows=rows, vecs=vecs,
                    carries=[(16, 256)], saved=saved, douts=[Row(dmix3, 256, fc=lambda b: 2)])


def _attn_fn(ci, b, carries, rows, vecs):
    kp, vp = carries
    qr, kr, v = _thirds(rows[0])
    scale = ATT_HEAD_DIM ** -0.5
    q = qr
    n = q.shape[0]
    r, c = _iota((n, n), 0), _iota((n, n), 1)
    prev_ok, cur_ok = jnp.logical_and(c >= r, ci > 0), r >= c
    head = _iota(q.shape, 1) // ATT_HEAD_DIM
    o, lse = jnp.zeros(q.shape, F32), jnp.zeros(q.shape, F32)
    for h in range(ATT_HEADS):
        mine = head == h
        qh = jnp.where(mine, qr, 0.0)
        sp = jnp.where(prev_ok, _bdot(qh, kp, "nt") * scale, NEG)
        sc = jnp.where(cur_ok, _bdot(qh, kr, "nt") * scale, NEG)
        m = lax.stop_gradient(jnp.maximum(jnp.max(sp, axis=1, keepdims=True), jnp.max(sc, axis=1, keepdims=True)))
        pp, pc = jnp.exp(sp - m), jnp.exp(sc - m)
        l = jnp.sum(pp, axis=1, keepdims=True) + jnp.sum(pc, axis=1, keepdims=True)
        o = jnp.where(mine, (_bdot(pp, vp, "nn") + _bdot(pc, v, "nn")) / l, o)
        lse = jnp.where(mine, m + jnp.log(l), lse)
    return [kr, v], [o, lse]


_ATT_CARRIES = [(ATT_BLOCK, ATT_W), (ATT_BLOCK, ATT_W)]


def attn_forward(name, pv, d):
    l = pv.shape[1]
    own = lambda b: b
    outs = [out_row((1, l, d * ATT_W), F32, ATT_W, fc=own) for _ in range(2)]
    (o, lse), saved = scan_fwd(name, _attn_fn, nb=d, nchunk=l // ATT_BLOCK, t=ATT_BLOCK, rows=[Row(pv, 3 * ATT_W, fc=own)],
                               vecs=[], carries=_ATT_CARRIES, outs=outs, save=True)
    return o, lse, saved


def attn_backward(name, pv, d, saved, do, dlse):
    l = pv.shape[1]
    own = lambda b: b
    (dpv,), _ = scan_bwd(name, _attn_fn, nb=d, nchunk=l // ATT_BLOCK, t=ATT_BLOCK, rows=[Row(pv, 3 * ATT_W, fc=own)], vecs=[],
                         carries=_ATT_CARRIES, saved=saved, douts=[Row(do, ATT_W, fc=own), Row(dlse, ATT_W, fc=own)])
    return dpv


def _rope_fn(ci, b, carries, rows, vecs):
    x, cs, sn = rows
    return [], [x * cs + _rot_pairs(x) * sn]


def _rope3_fn(ci, b, carries, rows, vecs):
    _, (y,) = _rope_fn(ci, b, carries, rows, vecs)
    return [], [y, y, y]


def _by_residue(a_or_shape, w, d):
    if isinstance(a_or_shape, tuple):
        _, s, _ = a_or_shape
        return Row(jax.ShapeDtypeStruct((1, s // d, d * w), F32), w, view=None if d == 1 else d)
    return Row(a_or_shape, w, view=None if d == 1 else d)


def rope_forward(name, qkv3, cs3, sn3):
    s, w = qkv3.shape[1], qkv3.shape[2]
    ys, _ = scan_fwd(name, _rope3_fn, nb=1, nchunk=s // _ROW_T, t=_ROW_T, vecs=[], carries=[], save=False,
                     rows=[Row(qkv3), Row(cs3, diff=False), Row(sn3, diff=False)],
                     outs=[_by_residue(qkv3.shape, w, d) for _, d in ATT_PATTERNS])
    return ys


def rope_backward(name, qkv3, cs3, sn3, dys):
    s, w = qkv3.shape[1], qkv3.shape[2]
    (dx,), _ = scan_bwd(name, _rope3_fn, nb=1, nchunk=s // _ROW_T, t=_ROW_T, vecs=[], carries=[], saved=[],
                        rows=[Row(qkv3, ddtype=BF16), Row(cs3, diff=False), Row(sn3, diff=False)],
                        douts=[_by_residue(a, w, d) for a, (_, d) in zip(dys, ATT_PATTERNS)])
    return dx


def _merge_fn(ci, b, carries, rows, vecs):
    o1, o2, o3, l1, l2, l3 = rows
    mx = lax.stop_gradient(jnp.maximum(l1, jnp.maximum(l2, l3)))
    e1, e2, e3 = jnp.exp(l1 - mx), jnp.exp(l2 - mx), jnp.exp(l3 - mx)
    return [], [(e1 * o1 + e2 * o2 + e3 * o3) / (e1 + e2 + e3)]


_ROW_T = 512


def _merge_rows(os_, ls_):
    ds = [d for _, d in ATT_PATTERNS]
    return [_by_residue(a, ATT_W, d) for a, d in zip(os_, ds)] + [_by_residue(a, ATT_W, d) for a, d in zip(ls_, ds)]


def merge_forward(name, os_, ls_, s):
    (y,), _ = scan_fwd(name, _merge_fn, nb=1, nchunk=s // _ROW_T, t=_ROW_T, rows=_merge_rows(os_, ls_), vecs=[],
                       carries=[], outs=[out_row((1, s, ATT_W), BF16)], save=False)
    return y


def merge_backward(name, os_, ls_, dmix3):
    s = dmix3.shape[1]
    drows, _ = scan_bwd(name, _merge_fn, nb=1, nchunk=s // _ROW_T, t=_ROW_T, rows=_merge_rows(os_, ls_), vecs=[],
                        carries=[], saved=[], douts=[Row(dmix3, 256, fc=lambda b: 3)])
    return drows


def _norm_mod_fn(ci, b, carries, rows, vecs):
    (x,) = rows
    g, sc, sh = vecs
    xn = x * lax.rsqrt(jnp.mean(x * x, axis=-1, keepdims=True) + NORM_EPS)
    return [], [xn * g * (1.0 + sc) + sh]


def norm_mod_forward(name, x3, g, sc, sh):
    s = x3.shape[1]
    (h,), _ = scan_fwd(name, _norm_mod_fn, nb=1, nchunk=s // _ROW_T, t=_ROW_T, rows=[Row(x3)], vecs=[Vec(g), Vec(sc), Vec(sh)],
                       carries=[], outs=[out_row(x3.shape, BF16)], save=False)
    return h


def norm_mod_backward(name, x3, g, sc, sh, dh3, add3):
    s = x3.shape[1]
    (dx,), dv = scan_bwd(name, _norm_mod_fn, nb=1, nchunk=s // _ROW_T, t=_ROW_T, rows=[Row(x3)], vecs=[Vec(g), Vec(sc), Vec(sh)],
                         carries=[], saved=[], douts=[Row(dh3)], adds={0: Row(add3)})
    return dx, dv


def _gate_fn(ci, b, carries, rows, vecs):
    return [], [rows[0] * vecs[0]]


def gate_backward(name, o3, g, dx3):
    s = o3.shape[1]
    (do,), (dg,) = scan_bwd(name, _gate_fn, nb=1, nchunk=s // _ROW_T, t=_ROW_T, rows=[Row(o3, ddtype=BF16)], vecs=[Vec(g)],
                            carries=[], saved=[], douts=[Row(dx3)])
    return do, dg


def _make_halves():
    @jax.custom_vjp
    def halves(x):
        h = x.shape[1] // 2
        return x[:, :h], x[:, h:]

    def fwd(x):
        return halves(x), None

    def bwd(_, g):
        return (jnp.concatenate(g, axis=1),)

    halves.defvjp(fwd, bwd)
    return halves


_halves = _make_halves()


def _ffn_fn(ci, b, carries, rows, vecs):
    (cu,) = carries
    (u,) = rows
    w, bias = vecs
    hg, hu = _halves(_conv(_shift8, cu, u, w, bias, 3))
    return [_tail8(u)], [_silu(hg) * hu]


_FFN_T = 256
_FFN_CW = FFN_DIM // 2
_FFN_CARRIES = [(8, 2 * _FFN_CW)]
FFN_BLOCK_ORDER = [0, 2, 1, 3]


def _ffn_io(up3, cw, cb):
    own = lambda b: b
    return [Row(up3, 2 * _FFN_CW, fc=own, ddtype=BF16)], [Vec(cw, 2 * _FFN_CW, own), Vec(cb, 2 * _FFN_CW, own)]


def ffn_down_forward(name, up3, cw, cb, w_down, res, gate):
    s, t, cw2 = up3.shape[1], _FFN_T, 2 * _FFN_CW
    d = w_down.shape[1]
    nchunk = s // t
    ride = RIDERS.take(name)
    r_in, r_out, r_scr = ride.specs() if ride else ([], [], [])

    def body(*refs):
        up_ref, cw_ref, cb_ref, wd_ref, res_ref, g_ref = refs[:6]
        ride_in = refs[6:6 + len(r_in)]
        act_ref, save_ref, dn_ref, x2_ref = refs[6 + len(r_in):10 + len(r_in)]
        ride_out = refs[10 + len(r_in):10 + len(r_in) + len(r_out)]
        car, acc = refs[10 + len(r_in) + len(r_out):12 + len(r_in) + len(r_out)]
        sems = refs[12 + len(r_in) + len(r_out):]
        i, b = pl.program_id(0), pl.program_id(1)
        if ride:
            ride.begin(ride_in, ride_out, sems, jnp.logical_and(i == 0, b == 0))

        @pl.when(i == 0)
        def _():
            car[b] = jnp.zeros(car.shape[1:], F32)

        cin = car[b]
        save_ref[0, 0] = cin
        (new_c,), (act,) = _ffn_fn(i, b, [cin], [up_ref[0]], [cw_ref[...], cb_ref[...]])
        car[b] = new_c
        act_ref[0] = act.astype(act_ref.dtype)
        part = _mxu(act, wd_ref[...], "nn")

        @pl.when(b == 0)
        def _():
            acc[...] = part

        @pl.when(b == 1)
        def _():
            tot = acc[...] + part
            dn_ref[...] = tot
            x2_ref[...] = res_ref[...] + g_ref[...] * tot

        if ride:
            ride.end(ride_in, ride_out, sems, jnp.logical_and(i == nchunk - 1, b == 1))

    tile = pl.BlockSpec((t, d), lambda i, b: (i, 0))
    out = pl.pallas_call(
        body, name=name, grid=(nchunk, 2),
        in_specs=[pl.BlockSpec((1, t, cw2), lambda i, b: (0, i, b)), pl.BlockSpec((cw.shape[0], cw2), lambda i, b: (0, b)),
                  pl.BlockSpec((1, cw2), lambda i, b: (0, b)), pl.BlockSpec((_FFN_CW, d), lambda i, b: (b, 0)), tile,
                  pl.BlockSpec((1, d), lambda i, b: (0, 0))] + r_in,
        out_specs=[pl.BlockSpec((1, t, _FFN_CW), lambda i, b: (0, i, b)), pl.BlockSpec((1, 1, 8, cw2), lambda i, b: (b, i, 0, 0)),
                   tile, tile] + r_out,
        out_shape=[jax.ShapeDtypeStruct((1, s, FFN_DIM), BF16), jax.ShapeDtypeStruct((2, nchunk, 8, cw2), F32),
                   jax.ShapeDtypeStruct((s, d), F32), jax.ShapeDtypeStruct((s, d), F32)] + (list(ride.out_shapes) if ride else []),
        scratch_shapes=[pltpu.VMEM((2, 8, cw2), F32), pltpu.VMEM((t, d), F32)] + r_scr,
        compiler_params=_cparams(),
    )(up3, cw, cb, w_down, res, gate, *(ride.ins if ride else []))
    if ride:
        RIDERS.done[name] = list(out[4:])
    return out[0], [out[1]], out[2], out[3]


def ffn_mid_backward(name, up3, cw, cb, saved, dact3):
    rows, vecs = _ffn_io(up3, cw, cb)
    s = up3.shape[1]
    return scan_bwd(name, _ffn_fn, nb=2, nchunk=s // _FFN_T, t=_FFN_T, rows=rows, vecs=vecs, carries=_FFN_CARRIES,
                    saved=saved, douts=[Row(dact3, _FFN_CW, fc=lambda b: b)])


def _adam_fn(ci, b, carries, rows, vecs):
    w, g, m, v = rows
    m = ADAM_B1 * m + (1.0 - ADAM_B1) * g
    v = ADAM_B2 * v + (1.0 - ADAM_B2) * (g * g)
    m_hat = m / (1.0 - ADAM_B1 ** ADAM_STEP)
    v_hat = v / (1.0 - ADAM_B2 ** ADAM_STEP)
    delta = -ADAM_LR * (m_hat / (jnp.sqrt(v_hat) + ADAM_EPS) + ADAM_WD * w)
    return [], [delta, m, v]


def _adam_layers_fn(ci, b, carries, rows, vecs):
    w, mine, theirs, m, v = rows
    g = jnp.where(b == lax.axis_index("c"), mine, theirs)
    _, upd = _adam_fn(ci, b, carries, [w, g, m, v], vecs)
    return [], [g] + upd


def adamw_layers(name, w, mine, theirs, m, v):
    _, r, c = w.shape
    t = _tile(r, 256, 8)
    layer = lambda b: b
    rows = [Row(w, fb=layer), Row(mine[None]), Row(theirs[None]), Row(m, fb=layer), Row(v, fb=layer)]
    outs, _ = scan_fwd(name, _adam_layers_fn, nb=2, nchunk=r // t, t=t, rows=rows, vecs=[], carries=[],
                       outs=[out_row(w.shape, fb=layer) for _ in range(4)], save=False)
    return outs


def adamw(name, w, g, m, v):
    shape = w.shape
    c = shape[-1]
    r = int(np.prod(shape[:-1]))
    t = _tile(r, 256, 8)
    as3 = lambda a: a.reshape(1, r, c)
    outs, _ = scan_fwd(name, _adam_fn, nb=1, nchunk=r // t, t=t, rows=[Row(as3(a)) for a in (w, g, m, v)], vecs=[], carries=[],
                       outs=[out_row((1, r, c)) for _ in range(3)], save=False)
    return [o.reshape(shape) for o in outs]


def rope_tables(positions):
    inv_freq = ROPE_THETA ** (-jnp.arange(0, ROT_DIM, 2, dtype=F32) / ROT_DIM)
    ang = positions.astype(F32)[:, None] * inv_freq
    s = positions.shape[0]
    cs = jnp.concatenate([jnp.cos(ang), jnp.cos(ang), jnp.ones((s, ATT_HEAD_DIM - ROT_DIM), F32)], axis=1)
    sn = jnp.concatenate([jnp.sin(ang), jnp.sin(ang), jnp.zeros((s, ATT_HEAD_DIM - ROT_DIM), F32)], axis=1)
    cs3 = jnp.concatenate([jnp.tile(cs, (1, 2 * ATT_HEADS)), jnp.ones((s, ATT_W), F32)], axis=1)
    sn3 = jnp.concatenate([jnp.tile(sn, (1, 2 * ATT_HEADS)), jnp.zeros((s, ATT_W), F32)], axis=1)
    return cs3[None], sn3[None]


def attention_forward(lname, qkv3, cs3, sn3):
    s = qkv3.shape[1]
    rotated = rope_forward(f"{lname}_rope", qkv3, cs3, sn3)
    os_, ls_, keep = [], [], []
    for pi, (_, d) in enumerate(ATT_PATTERNS):
        o, lse, saved = attn_forward(f"{lname}_attn{pi}", rotated[pi], d)
        os_.append(o)
        ls_.append(lse)
        keep.append(saved)
    y = merge_forward(f"{lname}_merge", os_, ls_, s)
    return y, (rotated, os_, ls_, keep)


def attention_backward(lname, qkv3, cs3, sn3, res, dmix3):
    rotated, os_, ls_, keep = res
    dm = merge_backward(f"{lname}_merge_b", os_, ls_, dmix3)
    dys = [attn_backward(f"{lname}_attn{pi}_b", rotated[pi], d, keep[pi], dm[pi], dm[3 + pi]) for pi, (_, d) in enumerate(ATT_PATTERNS)]
    return rope_backward(f"{lname}_rope_b", qkv3, cs3, sn3, dys)


def final_loss(name, x3, t3, g):
    s, d = x3.shape[1], x3.shape[2]
    t = _ROW_T

    def body(x_ref, t_ref, g_ref, loss_ref, dx_ref, dg_ref):
        i = pl.program_id(0)
        tv = t_ref[0]

        def f(x, gg):
            y = x * lax.rsqrt(jnp.mean(x * x, axis=-1, keepdims=True) + NORM_EPS) * gg
            e = y - tv
            return 0.5 * jnp.sum(jnp.mean(e * e, axis=-1, keepdims=True), axis=0, keepdims=True)

        l, vjp = jax.vjp(f, x_ref[0], g_ref[...])
        dx, dg = vjp(jnp.ones((1, 1), F32))
        dx_ref[0] = dx

        @pl.when(i == 0)
        def _():
            loss_ref[...] = jnp.zeros(loss_ref.shape, F32)
            dg_ref[...] = jnp.zeros(dg_ref.shape, F32)

        loss_ref[...] += jnp.broadcast_to(l, loss_ref.shape)
        dg_ref[...] += dg

    row = pl.BlockSpec((1, t, d), lambda i: (0, i, 0))
    vec = pl.BlockSpec((1, d), lambda i: (0, 0))
    return pl.pallas_call(
        body, name=name, grid=(s // t,), in_specs=[row, row, vec],
        out_specs=[pl.BlockSpec((8, 128), lambda i: (0, 0)), row, vec],
        out_shape=[jax.ShapeDtypeStruct((8, 128), F32), jax.ShapeDtypeStruct(x3.shape, F32), jax.ShapeDtypeStruct((1, d), F32)],
        compiler_params=pltpu.CompilerParams(dimension_semantics=("arbitrary",), vmem_limit_bytes=VMEM_LIMIT_BYTES),
    )(x3, t3, g)


_ADA_TN = 512


def ada_forward(name, c16, ada_w):
    depth, d, cols = ada_w.shape

    def body(c_ref, w_ref, o_ref):
        o_ref[0] = _mxu(_silu(c_ref[...]), w_ref[0], "nn")

    return pl.pallas_call(
        body, name=name, grid=(depth, cols // _ADA_TN),
        in_specs=[pl.BlockSpec((16, d), lambda l, j: (0, 0)), pl.BlockSpec((1, d, _ADA_TN), lambda l, j: (l, 0, j))],
        out_specs=pl.BlockSpec((1, 16, _ADA_TN), lambda l, j: (l, 0, j)),
        out_shape=jax.ShapeDtypeStruct((depth, 16, cols), F32),
        compiler_params=pltpu.CompilerParams(dimension_semantics=("arbitrary", "arbitrary"), vmem_limit_bytes=VMEM_LIMIT_BYTES),
    )(c16, ada_w)


def ada_backward(name, c16, dmod16, w, m, v):
    depth, d, cols = w.shape

    def body(c_ref, dm_ref, w_ref, m_ref, v_ref, g_ref, dl_ref, nm_ref, nv_ref):
        g = _mxu(_silu(c_ref[...]), dm_ref[0], "tn")
        _, (delta, nm, nv) = _adam_fn(None, None, [], [w_ref[0], g, m_ref[0], v_ref[0]], [])
        g_ref[0], dl_ref[0], nm_ref[0], nv_ref[0] = g, delta, nm, nv

    blk = pl.BlockSpec((1, d, _ADA_TN), lambda l, j: (l, 0, j))
    return pl.pallas_call(
        body, name=name, grid=(depth, cols // _ADA_TN),
        in_specs=[pl.BlockSpec((16, d), lambda l, j: (0, 0)), pl.BlockSpec((1, 16, _ADA_TN), lambda l, j: (l, 0, j)), blk, blk, blk],
        out_specs=[blk] * 4, out_shape=[jax.ShapeDtypeStruct(w.shape, F32)] * 4,
        compiler_params=pltpu.CompilerParams(dimension_semantics=("arbitrary", "arbitrary"), vmem_limit_bytes=VMEM_LIMIT_BYTES),
    )(c16, dmod16, w, m, v)


def _sum_fn(ci, b, carries, rows, vecs):
    acc = rows[0].astype(F32)
    for r in rows[1:]:
        acc = acc + r.astype(F32)
    return [], [acc]


def sum_slots(name, a, nsum, out_dtype=F32):
    n, r, c = a.shape
    nb = n // nsum
    t = _tile(r, 256, 8)
    rows = [Row(a, fb=(lambda b, k=k: k * nb + b)) for k in range(nsum)]
    (out,), _ = scan_fwd(name, _sum_fn, nb=nb, nchunk=r // t, t=t, rows=rows, vecs=[], carries=[],
                         outs=[out_row((nb, r, c), out_dtype, fb=lambda b: b)], save=False)
    return out


def _flip(mask, pos):
    return tuple((1 - p) if m else p for m, p in zip(mask, pos))


ALL_PEERS = [(a, b, c) for a in (0, 1) for b in (0, 1) for c in (0, 1)][1:]
CHIP_PEERS = [(1, 0, 0), (0, 1, 0), (1, 1, 0)]
SIBLING = [(0, 0, 1)]


def _dev(pos):
    return 4 * pos[0] + 2 * pos[1] + pos[2]


def _chip(pos):
    return 2 * pos[0] + pos[1]


def allgather8(name, a):
    (out,) = ride_alone(name, allgather8_ride(a))
    return _with_own(out, a)


def _rows_of(shape):
    return -(-int(np.prod(shape)) // 1024) * 8


def _pack(arrs):
    parts = []
    for a in arrs:
        flat = a.reshape(-1).astype(F32)
        parts.append(jnp.pad(flat, (0, _rows_of(a.shape) * 128 - flat.shape[0])).reshape(-1, 128))
    rows = sum(p.shape[0] for p in parts)
    parts.append(jnp.zeros(((-rows) % _ROW_T, 128), F32))
    return jnp.concatenate(parts, axis=0)


def _unpack(buf, shapes):
    out, o = [], 0
    for s in shapes:
        r, n = _rows_of(s), int(np.prod(s))
        out.append(buf[o:o + r].reshape(-1)[:n].reshape(s))
        o += r
    return out


_WEIGHTS = ["ada_w", "ada_b", "norm1_g", "w_in", "ssd_conv_w", "ssd_conv_b", "ssd_dt_bias", "ssd_a_log", "ssd_d", "ssd_norm_g",
            "pool_w", "pool_scale", "w_out", "norm2_g", "ffn_up", "ffn_conv_w", "ffn_conv_b", "ffn_down", "final_g"]
_BIG = ["w_in", "w_out", "ffn_up", "ffn_down"]
_SMALL = [n for n in _WEIGHTS if n not in _BIG and n != "ada_w"]
_COL_SHARDED_SMALL = {"ssd_conv_w": 256, "ffn_conv_w": 1408}


def _pad_lanes(v, n=128):
    return jnp.pad(v.astype(F32), (0, n - v.shape[0]))[None]


_CHIP2_PARTS = [(1284, 1536), (1792, 1800), (1536, 1792), (IN_MAIN, IN_MAIN + 126)]


def _w_in_chip_cols(gp):
    q = IN_W // 4
    return [gp[:, :q], gp[:, q:2 * q], jnp.concatenate([gp[:, a:b] for a, b in _CHIP2_PARTS], axis=1), gp[:, IN_WP - q:]]


def _ffn_block_perm(a):
    n = a.shape[-1] // 4
    return jnp.concatenate([a[..., j * n:(j + 1) * n] for j in FFN_BLOCK_ORDER], axis=-1)


def _layer_forward(i, x3, modv, wts, sp, cs3, sn3):
    sh1, sc1, g1, sh2, sc2, g2 = modv
    big = lambda n: wts[n]() if callable(wts[n]) else wts[n]
    h1 = norm_mod_forward(f"l{i}_norm1", x3, wts["norm1_g"], sc1, sh1)
    proj3 = mm(f"l{i}_proj", h1[0], big("w_in")[:, :IN_MAIN], "nn")[None]
    qkv3 = mm(f"l{i}_qkv", h1[0], big("w_in")[:, IN_MAIN:], "nn")[None]
    y_ssd, sv_ssd = ssd_forward(f"l{i}_ssd", proj3, sp)
    y_pool, sv_pool = pool_forward(f"l{i}_pool", proj3, wts["wbd"], wts["pool_scale"])
    y_att, res_att = attention_forward(f"l{i}", qkv3, cs3, sn3)
    mix = jnp.concatenate([y_ssd, y_pool, y_att], axis=-1)
    out, x1 = mm(f"l{i}_wout", mix[0], big("w_out"), "nn", res=x3[0], gate=g1)
    x1 = x1[None]
    h2 = norm_mod_forward(f"l{i}_norm2", x1, wts["norm2_g"], sc2, sh2)
    up3 = mm(f"l{i}_up", h2[0], big("ffn_up"), "nn")[None]
    act, sv_ffn, dn, x2 = ffn_down_forward(f"l{i}_down", up3, wts["ffn_conv_w"], wts["ffn_conv_b"], big("ffn_down"), x1[0], g2)
    keep = dict(x=x3, h1=h1, proj3=proj3, qkv3=qkv3, sv_ssd=sv_ssd, sv_pool=sv_pool, res_att=res_att, mix=mix, out=out[None],
                x1=x1, h2=h2, up3=up3, act=act, sv_ffn=sv_ffn, dn=dn[None])
    return x2[None], keep


def _layer_backward(i, dx2, keep, modv, wts, sp, cs3, sn3, after=None):
    sh1, sc1, g1, sh2, sc2, g2 = modv
    k = keep
    big = lambda n: wts[n]() if callable(wts[n]) else wts[n]
    tell = lambda step, *a: after[step](*a) if after and step in after else None
    d_dn, d_g2 = gate_backward(f"l{i}_gate2_b", k["dn"], g2, dx2)
    d_act = mm(f"l{i}_down_bx", d_dn[0], big("ffn_down"), "nt")
    g_down = mm(f"l{i}_down_bw", k["act"][0], d_dn[0], "tn", BF16).reshape(4, FFN_DIM // 4, D_MODEL)
    (d_up,), dv_ffn = ffn_mid_backward(f"l{i}_ffn_b", k["up3"], wts["ffn_conv_w"], wts["ffn_conv_b"], k["sv_ffn"], d_act[None])
    tell("ffn_b")
    d_h2 = mm(f"l{i}_up_bx", d_up[0], big("ffn_up"), "nt")
    g_up = mm(f"l{i}_up_bw", k["h2"][0], d_up[0], "tn", BF16, tn=_FFN_CW,
              into=((4, D_MODEL, _FFN_CW), lambda r, c: ((c % 2) * 2 + c // 2, r, 0)))
    dx1, (d_n2, d_sc2, d_sh2) = norm_mod_backward(f"l{i}_norm2_b", k["x1"], wts["norm2_g"], sc2, sh2, d_h2[None], dx2)
    d_out, d_g1 = gate_backward(f"l{i}_gate1_b", k["out"], g1, dx1)
    d_mix = mm(f"l{i}_wout_bx", d_out[0], big("w_out"), "nt")[None]
    g_wout = mm(f"l{i}_wout_bw", k["mix"][0], d_out[0], "tn", BF16).reshape(4, D_MODEL // 4, D_MODEL)
    tell("wout_bw", g_wout, g_up, g_down)
    (dz, dxs, dbm, dcm, ddt), dv_ssd = ssd_backward(f"l{i}_ssd_b", k["proj3"], sp, k["sv_ssd"], d_mix)
    tell("ssd_b")
    (du_pool,), (d_wbd, d_pscale) = pool_backward(f"l{i}_pool_b", k["proj3"], wts["wbd"], wts["pool_scale"], k["sv_pool"], d_mix)
    d_qkv = attention_backward(f"l{i}", k["qkv3"], cs3, sn3, k["res_att"], d_mix)
    d_proj = jnp.concatenate([dz[0], dxs[0], dbm[0], dcm[0], du_pool[0], (ddt[0] + ddt[1]).astype(BF16), d_qkv[0]], axis=-1)
    g_win = jnp.stack(_w_in_chip_cols(mm(f"l{i}_proj_bw", k["h1"][0], d_proj, "tn", BF16)))
    tell("proj_bw", g_win)
    d_h1 = mm(f"l{i}_proj_bx", d_proj, big("w_in"), "nt")
    tell("proj_bx")
    dx, (d_n1, d_sc1, d_sh1) = norm_mod_backward(f"l{i}_norm1_b", k["x"], wts["norm1_g"], sc1, sh1, d_h1[None], dx1)
    dcwx, dcbx, dcwb, dcbb, dcwc, dcbc, ddtb, dalog, ddsk, dng = dv_ssd
    small = dict(
        norm1_g=d_n1[0], norm2_g=d_n2[0],
        ssd_conv_w=jnp.concatenate([dcwx[:, :512], dcwb[:, 512:768], dcwc[:, 768:]], axis=1),
        ssd_conv_b=jnp.concatenate([dcbx[0, :512], dcbb[0, 512:768], dcbc[0, 768:]]),
        ssd_dt_bias=ddtb[0, :8], ssd_a_log=dalog[0, :8], ssd_d=ddsk[0, :8], ssd_norm_g=dng[0],
        pool_w=jnp.stack([d_wbd[64 * g:64 * g + 64, 64 * g:64 * g + 64] for g in range(4)]), pool_scale=d_pscale[0],
        ffn_conv_w=_ffn_block_perm(dv_ffn[0]), ffn_conv_b=_ffn_block_perm(dv_ffn[1][0]),
    )
    dmod = jnp.concatenate([d_sh1[0], d_sc1[0], d_g1[0], d_sh2[0], d_sc2[0], d_g2[0]])
    return dx, [g_win, g_wout, g_up, g_down], small, dmod


def kernel(x, c, positions, ada_w, ada_b, norm1_g, w_in, ssd_conv_w, ssd_conv_b, ssd_dt_bias, ssd_a_log, ssd_d, ssd_norm_g, pool_w, pool_scale, w_out, norm2_g, ffn_up, ffn_conv_w, ffn_conv_b, ffn_down, final_g, loss_target, m_ada_w, m_ada_b, m_norm1_g, m_w_in, m_ssd_conv_w, m_ssd_conv_b, m_ssd_dt_bias, m_ssd_a_log, m_ssd_d, m_ssd_norm_g, m_pool_w, m_pool_scale, m_w_out, m_norm2_g, m_ffn_up, m_ffn_conv_w, m_ffn_conv_b, m_ffn_down, m_final_g, v_ada_w, v_ada_b, v_norm1_g, v_w_in, v_ssd_conv_w, v_ssd_conv_b, v_ssd_dt_bias, v_ssd_a_log, v_ssd_d, v_ssd_norm_g, v_pool_w, v_pool_scale, v_w_out, v_norm2_g, v_ffn_up, v_ffn_conv_w, v_ffn_conv_b, v_ffn_down, v_final_g):
    args = dict(locals())
    w = {n: args[n] for n in _WEIGHTS}
    m = {n: args["m_" + n] for n in _WEIGHTS}
    v = {n: args["v_" + n] for n in _WEIGHTS}
    d = D_MODEL
    me = (lax.axis_index("x"), lax.axis_index("y"), lax.axis_index("c"))
    chip, dev = _chip(me), _dev(me)
    RIDERS.reset()

    shapes0 = [c.shape, ssd_conv_w.shape, ffn_conv_w.shape]
    pack0 = _pack([c, ssd_conv_w, ffn_conv_w])
    shards = [w[n].astype(BF16) for n in _BIG]
    g0 = allgather8("gather_c_conv", pack0)
    c16 = jnp.pad(g0[:, :d // 128, :].reshape(8, d), ((0, 8), (0, 0)))
    by_chip = [_unpack(g0[2 * j], shapes0) for j in range(4)]
    conv_w_full = jnp.concatenate([p[1] for p in by_chip], axis=-1)
    fconv_w_full = jnp.concatenate([p[2] for p in by_chip], axis=-1)

    modp = ada_forward("ada_fwd", c16, ada_w)[:, :8]
    pack1 = _pack([modp])
    g1, w_in0 = ride_alone("gather_mod_w_in0", merge_rides([allgather8_ride(pack1), gather_ride(0, [shards[0]])]))
    g1 = _with_own(g1, pack1)
    modfull = jnp.concatenate([_unpack(g1[2 * j], [modp.shape])[0] for j in range(4)], axis=-1)
    mod = lax.dynamic_index_in_dim(modfull, dev, axis=1, keepdims=False) + ada_b
    modv = [[mod[i, q * d:(q + 1) * d][None] for q in range(6)] for i in range(DEPTH)]


    def weight(k, layer, got):
        full = lax.dynamic_update_slice(got, shards[k][layer][None], (chip, 0, 0))
        if k == 0:
            return _w_in_from_chips(full)
        if k == 2:
            return jnp.concatenate([full[j] for j in FFN_BLOCK_ORDER], axis=1)
        return full.reshape(-1, full.shape[2])

    def later(k, layer, *sources):
        made = []

        def get():
            if not made:
                got = [RIDERS.result(host)[pos] for host, pos in sources]
                made.append(weight(k, layer, got[0] if len(got) == 1 else jnp.concatenate(got, axis=1)))
            return made[0]
        return get

    cs3, sn3 = rope_tables(positions[0])
    eye4 = jnp.eye(4, dtype=F32)
    wts, sps = [], []
    for i in range(DEPTH):
        wts.append(dict(
            norm1_g=norm1_g[i][None], norm2_g=norm2_g[i][None], pool_scale=pool_scale[i][None],
            wbd=(eye4[:, None, :, None] * pool_w[i][:, :, None, :]).reshape(POOL_W, POOL_W),
            ffn_conv_w=_ffn_block_perm(fconv_w_full[i]), ffn_conv_b=_ffn_block_perm(ffn_conv_b[i])[None]))
        sps.append(dict(cw=conv_w_full[i], cb=ssd_conv_b[i][None], dtb=_pad_lanes(ssd_dt_bias[i]), alog=_pad_lanes(ssd_a_log[i]),
                        dsk=_pad_lanes(ssd_d[i]), ng=ssd_norm_g[i][None]))

    RIDERS.book("l0_ssd", gather_ride(0, [shards[1], shards[3]]))
    half = shards[2].shape[1] // 2
    RIDERS.book("l0_attn0", gather_ride(0, [shards[2][:, :half]]))
    RIDERS.book("l0_attn1", gather_ride(0, [shards[2][:, half:]]))
    wts[0].update(w_in=weight(0, 0, w_in0), w_out=later(1, 0, ("l0_ssd", 0)), ffn_down=later(3, 0, ("l0_ssd", 1)),
                  ffn_up=later(2, 0, ("l0_attn0", 0), ("l0_attn1", 0)))
    RIDERS.book("l0_attn2", gather_ride(1, [shards[0], shards[1]]))
    RIDERS.book("l0_up", gather_ride(1, [shards[3]]))
    RIDERS.book("l0_down", gather_ride(1, [shards[2]]))
    wts[1].update(w_in=later(0, 1, ("l0_attn2", 0)), w_out=later(1, 1, ("l0_attn2", 1)), ffn_up=later(2, 1, ("l0_down", 0)),
                  ffn_down=later(3, 1, ("l0_up", 0)))
    x1_, keep0 = _layer_forward(0, x, modv[0], wts[0], sps[0], cs3, sn3)
    xc, keep1 = _layer_forward(1, x1_, modv[1], wts[1], sps[1], cs3, sn3)
    keeps = [keep0, keep1]
    lossblk, dx, d_final = final_loss("final_loss", xc, loss_target, final_g[None])

    small_g, dmods = [None] * DEPTH, [None] * DEPTH
    part_sum, from_chips = [[None] * 4 for _ in range(DEPTH)], [[None] * 4 for _ in range(DEPTH)]

    def owner_sum(layer, ks, mine, theirs):
        for k, g, t in zip(ks, mine, theirs):
            part_sum[layer][k] = add_arrays(f"sum_cores{layer}_{_BIG[k]}", [g, t], BF16)

    dx, by_chip1, small_g[1], dmods[1] = _layer_backward(1, dx, keeps[1], modv[1], wts[1], sps[1], cs3, sn3)
    RIDERS.book("l0_ffn_b", to_owner_ride(1, by_chip1))

    def after_ffn_b():
        owner_sum(1, range(4), by_chip1, RIDERS.result("l0_ffn_b"))
        RIDERS.book("l0_up_bx", scatter_ride(1, [part_sum[1][2]]))
        RIDERS.book("l0_up_bw", scatter_ride(1, [part_sum[1][0], part_sum[1][1]]))
        RIDERS.book("l0_norm2_b", scatter_ride(1, [part_sum[1][3]]))

    early = []

    def after_wout_bw(g_wout, g_up, g_down):
        early.extend([g_wout, g_up, g_down])
        RIDERS.book("l0_ssd_b", to_owner_ride(0, early))

    def after_ssd_b():
        owner_sum(0, [1, 2, 3], early, RIDERS.result("l0_ssd_b"))
        for host, k in (("l0_attn0_b", 2), ("l0_attn1_b", 3), ("l0_attn2_b", 1)):
            RIDERS.book(host, scatter_ride(0, [part_sum[0][k]]))

    last = []

    def after_proj_bw(g_win):
        last.append(g_win)
        RIDERS.book("l0_proj_bx", to_owner_ride(0, last))

    def after_proj_bx():
        owner_sum(0, [0], last, RIDERS.result("l0_proj_bx"))
        RIDERS.book("l0_norm1_b", scatter_ride(0, [part_sum[0][0]]))

    hooks = dict(ffn_b=after_ffn_b, wout_bw=after_wout_bw, ssd_b=after_ssd_b, proj_bw=after_proj_bw, proj_bx=after_proj_bx)
    dx, _, small_g[0], dmods[0] = _layer_backward(0, dx, keeps[0], modv[0], wts[0], sps[0], cs3, sn3, after=hooks)
    from_chips[1][2], (from_chips[1][0], from_chips[1][1]) = RIDERS.result("l0_up_bx")[0], RIDERS.result("l0_up_bw")[:2]
    from_chips[1][3] = RIDERS.result("l0_norm2_b")[0]
    for host, k in (("l0_attn0_b", 2), ("l0_attn1_b", 3), ("l0_attn2_b", 1), ("l0_norm1_b", 0)):
        from_chips[0][k] = RIDERS.result(host)[0]
    mine = [sum_chips_mine(f"sum_chips_{n}", part_sum[0][k], from_chips[0][k], part_sum[1][k], from_chips[1][k])
            for k, n in enumerate(_BIG)]

    part = dict(ada_b=jnp.stack(dmods), final_g=d_final[0])
    for n in _SMALL:
        if n not in part:
            part[n] = jnp.stack([small_g[i][n] for i in range(DEPTH)])
    full_shapes = [part[n].shape for n in _SMALL] + [(1,)]
    pack_small = _pack([part[n] for n in _SMALL] + [lossblk[0, :1]])
    *theirs, gs = ride_alone("swap_r_gather_small", merge_rides([swap_ride(mine), allgather8_ride(pack_small)]))
    gs = _with_own(gs, pack_small)
    tot = _unpack(sum_slots("sum_small", gs, 8)[0], full_shapes)
    loss = tot[-1].reshape(())
    grads = {}
    small_tot = dict(zip(_SMALL, tot))
    dmod_all = gs[:, :DEPTH * 6 * d // 128, :].reshape(8, DEPTH, 6 * d)
    for n, ncol in _COL_SHARDED_SMALL.items():
        small_tot[n] = lax.dynamic_slice_in_dim(small_tot[n], chip * ncol, ncol, axis=2)
    grads.update(small_tot)

    ncol = ada_w.shape[2]
    dm = lax.dynamic_slice_in_dim(dmod_all, chip * ncol, ncol, axis=2).transpose(1, 0, 2)
    upd = {}
    g_ada, *upd["ada_w"] = ada_backward("ada_bwd", c16, jnp.pad(dm, ((0, 0), (0, 8), (0, 0))), ada_w, m["ada_w"], v["ada_w"])
    grads["ada_w"] = g_ada

    for n, a, g in zip(_BIG, mine, theirs):
        grads[n], *upd[n] = adamw_layers(f"adam_{n}", w[n], a, g, m[n], v[n])
    shapes_s = [w[n].shape for n in _SMALL]
    packed = [_pack([src[n] for n in _SMALL]) for src in (w, grads, m, v)]
    outs_s = [_unpack(o, shapes_s) for o in adamw("adam_small", *packed)]
    for q, n in enumerate(_SMALL):
        upd[n] = [outs_s[0][q], outs_s[1][q], outs_s[2][q]]

    return (loss, dx, *[grads[n] for n in _WEIGHTS], *[upd[n][0] for n in _WEIGHTS], *[upd[n][1] for n in _WEIGHTS],
            *[upd[n][2] for n in _WEIGHTS])


def ride_alone(name, ride):
    ni, no = len(ride.ins), len(ride.out_shapes)

    def body(*refs):
        ride.begin(refs[:ni], refs[ni:ni + no], refs[ni + no:])
        ride.end(refs[:ni], refs[ni:ni + no], refs[ni + no:])

    in_specs, out_specs, scratch = ride.specs()
    return list(pl.pallas_call(body, name=name, in_specs=in_specs, out_specs=out_specs, out_shape=ride.out_shapes,
                               scratch_shapes=scratch)(*ride.ins))


def mm(name, a, b, mode, out_dtype=F32, res=None, gate=None, tm=1408, tn=1536, tk=1408, into=None):
    ride = RIDERS.take(name)
    if mode == "nn":
        (m, k), n = a.shape, b.shape[1]
    elif mode == "nt":
        (m, k), n = a.shape, b.shape[0]
    else:
        (k, m), n = a.shape, b.shape[1]
    tm, tn, tk = _tile(m, tm), _tile(n, tn), _tile(k, tk)
    ni, nj, nk = m // tm, n // tn, k // tk
    a_spec = pl.BlockSpec((tk, tm), lambda i, j, q: (q, i)) if mode == "tn" else pl.BlockSpec((tm, tk), lambda i, j, q: (i, q))
    b_spec = pl.BlockSpec((tn, tk), lambda i, j, q: (j, q)) if mode == "nt" else pl.BlockSpec((tk, tn), lambda i, j, q: (q, j))
    o_spec = pl.BlockSpec((tm, tn), lambda i, j, q: (i, j))
    fused = res is not None
    lead = 0 if into is None else len(into[0]) - 2
    first = (0,) * lead + (slice(None), slice(None))
    ins, in_specs = [a, b], [a_spec, b_spec]
    out_shape, out_specs = [jax.ShapeDtypeStruct((m, n), out_dtype)], [o_spec]
    if fused:
        ins += [res, gate]
        in_specs += [o_spec, pl.BlockSpec((1, tn), lambda i, j, q: (0, j))]
        out_shape.append(jax.ShapeDtypeStruct((m, n), F32))
        out_specs.append(o_spec)
    if into is not None:
        shape, omap = into
        out_shape = [jax.ShapeDtypeStruct(shape, out_dtype)]
        out_specs = [pl.BlockSpec((1,) * lead + (tm, tn), lambda i, j, q: omap(i, j))]
    n_in, n_out = len(ins), len(out_shape)
    scratch = [pltpu.VMEM((tm, tn), F32)]
    if ride is not None:
        r_in, r_out, r_scr = ride.specs()
        ins, in_specs = ins + list(ride.ins), in_specs + r_in
        out_shape, out_specs = out_shape + list(ride.out_shapes), out_specs + r_out
        scratch = scratch + r_scr

    def body(*refs):
        a_ref, b_ref = refs[:2]
        o_ref = refs[len(ins)]
        acc = refs[len(ins) + len(out_shape)]
        i, j, q = pl.program_id(0), pl.program_id(1), pl.program_id(2)
        at = lambda x, y, z: jnp.logical_and(jnp.logical_and(i == x, j == y), q == z)
        r_refs = (refs[n_in:len(ins)], refs[len(ins) + n_out:len(ins) + len(out_shape)], refs[len(ins) + len(out_shape) + 1:])
        if ride is not None:
            ride.begin(*r_refs, at(0, 0, 0))

        @pl.when(q == 0)
        def _():
            acc[...] = jnp.zeros(acc.shape, F32)

        acc[...] += _mxu(a_ref[...], b_ref[...], mode)

        @pl.when(q == nk - 1)
        def _():
            o_ref[first] = acc[...].astype(o_ref.dtype)
            if fused:
                refs[len(ins) + 1][...] = refs[2][...] + refs[3][...] * acc[...]

        if ride is not None:
            ride.end(*r_refs, at(ni - 1, nj - 1, nk - 1))

    sem = ("arbitrary",) * 3 if ride is not None else ("parallel", "parallel", "arbitrary")
    out = pl.pallas_call(
        body, name=name, grid=(ni, nj, nk), in_specs=in_specs, out_specs=out_specs, out_shape=out_shape, scratch_shapes=scratch,
        compiler_params=pltpu.CompilerParams(dimension_semantics=sem, vmem_limit_bytes=VMEM_LIMIT_BYTES),
    )(*ins)
    if ride is not None:
        RIDERS.done[name] = list(out[n_out:])
    return tuple(out[:n_out]) if fused else out[0]


def add_arrays(name, arrs, out_dtype=F32):
    nb, r, c = arrs[0].shape
    t = _tile(r, 256, 8)
    (out,), _ = scan_fwd(name, _sum_fn, nb=nb, nchunk=r // t, t=t, rows=[Row(a, fb=lambda b: b) for a in arrs], vecs=[], carries=[],
                         outs=[out_row((nb, r, c), out_dtype, fb=lambda b: b)], save=False)
    return out


def _sum_chips_mine_fn(ci, b, carries, rows, vecs):
    mine_layer = lax.axis_index("c")
    chip = 2 * lax.axis_index("x") + lax.axis_index("y")
    tot = None
    for j in range(4):
        own = jnp.where(mine_layer == 0, rows[j], rows[8 + j])
        sent = jnp.where(mine_layer == 0, rows[4 + j], rows[12 + j])
        term = jnp.where(chip == j, own, sent).astype(F32)
        tot = term if tot is None else tot + term
    return [], [tot]


def sum_chips_mine(name, p0, q0, p1, q1):
    _, r, c = p0.shape
    t = _tile(r, 256, 8)
    rows = [Row(a, fb=(lambda b, j=j: j)) for a in (p0, q0, p1, q1) for j in range(4)]
    (out,), _ = scan_fwd(name, _sum_chips_mine_fn, nb=1, nchunk=r // t, t=t, rows=rows, vecs=[], carries=[],
                         outs=[out_row((1, r, c))], save=False)
    return out[0]


def _remote(src, dst, send_sems, recv_sems, k, to):
    return pltpu.make_async_remote_copy(src_ref=src, dst_ref=dst, send_sem=send_sems.at[k], recv_sem=recv_sems.at[k],
                                        device_id=to, device_id_type=MESH)


def gather_ride(layer, shards):
    na = len(shards)

    def rows(ref, c):
        h = ref.shape[0] // 2
        return ref.at[pl.ds(c * h, h)]

    def start(ins, outs, ss, rs, me):
        for k in range(na):
            for p, mask in enumerate(CHIP_PEERS):
                _remote(rows(ins[k].at[layer], me[2]), rows(outs[k].at[_chip(me)], me[2]), ss, rs, 6 * k + p, _flip(mask, me)).start()

    def finish(ins, outs, ss, rs, me):
        sibling = _flip(SIBLING[0], me)
        for k in range(na):
            for p, mask in enumerate(CHIP_PEERS):
                got = rows(outs[k].at[_chip(_flip(mask, me))], me[2])
                _remote(rows(ins[k].at[layer], me[2]), got, ss, rs, 6 * k + p, _flip(mask, me)).wait_recv()
                _remote(got, got, ss, rs, 6 * k + 3 + p, sibling).start()
        for k in range(na):
            for p, mask in enumerate(CHIP_PEERS):
                got = rows(outs[k].at[_chip(_flip(mask, me))], me[2])
                other = rows(outs[k].at[_chip(_flip(mask, me))], 1 - me[2])
                _remote(other, other, ss, rs, 6 * k + 3 + p, sibling).wait_recv()
                _remote(rows(ins[k].at[layer], me[2]), got, ss, rs, 6 * k + p, _flip(mask, me)).wait_send()
                _remote(got, got, ss, rs, 6 * k + 3 + p, sibling).wait_send()

    return Ride(list(shards), [jax.ShapeDtypeStruct((4,) + a.shape[1:], a.dtype) for a in shards], 6 * na, start, finish)


def scatter_ride(layer, parts):
    na = len(parts)

    def half(ref, i):
        h = ref.shape[0] // 2
        return ref.at[pl.ds(i * h, h)]

    def each():
        return [(k, p, mask) for k in range(na) for p, mask in enumerate(CHIP_PEERS)]

    def start(ins, outs, ss, rs, me):
        @pl.when(me[2] == layer)
        def _():
            for k, p, mask in each():
                peer = _flip(mask, me)
                blk = ins[k].at[_chip(peer)]
                _remote(half(blk, 1), outs[na + k].at[p], ss, rs, 9 * k + 6 + p, _flip(SIBLING[0], me)).start()
                _remote(half(blk, 0), half(outs[k].at[_chip(me)], 0), ss, rs, 9 * k + p, peer).start()

    def finish(ins, outs, ss, rs, me):
        sibling = _flip(SIBLING[0], me)

        @pl.when(me[2] == layer)
        def _():
            for k, p, mask in each():
                peer = _flip(mask, me)
                blk, land = ins[k].at[_chip(peer)], outs[k].at[_chip(peer)]
                _remote(half(blk, 0), half(land, 0), ss, rs, 9 * k + p, peer).wait_recv()
                _remote(half(blk, 1), half(land, 1), ss, rs, 9 * k + 3 + p, peer).wait_recv()
                _remote(half(blk, 1), outs[na + k].at[p], ss, rs, 9 * k + 6 + p, sibling).wait_send()
                _remote(half(blk, 0), half(outs[k].at[_chip(me)], 0), ss, rs, 9 * k + p, peer).wait_send()

        @pl.when(me[2] != layer)
        def _():
            for k, p, mask in each():
                stage = outs[na + k].at[p]
                _remote(stage, stage, ss, rs, 9 * k + 6 + p, sibling).wait_recv()
                _remote(stage, half(outs[k].at[_chip(me)], 1), ss, rs, 9 * k + 3 + p, _flip(mask[:2] + (1,), me)).start()
            for k, p, mask in each():
                stage = outs[na + k].at[p]
                _remote(stage, half(outs[k].at[_chip(me)], 1), ss, rs, 9 * k + 3 + p, _flip(mask[:2] + (1,), me)).wait_send()

    stages = [jax.ShapeDtypeStruct((3, a.shape[1] // 2, a.shape[2]), a.dtype) for a in parts]
    return Ride(list(parts), [jax.ShapeDtypeStruct(a.shape, a.dtype) for a in parts] + stages, 9 * na, start, finish)


def to_owner_ride(layer, arrays):
    na = len(arrays)

    def start(ins, outs, ss, rs, me):
        @pl.when(me[2] != layer)
        def _():
            for k in range(na):
                _remote(ins[k], outs[k], ss, rs, k, _flip(SIBLING[0], me)).start()

    def finish(ins, outs, ss, rs, me):
        for k in range(na):
            cp = _remote(ins[k], outs[k], ss, rs, k, _flip(SIBLING[0], me))
            pl.when(me[2] != layer)(cp.wait_send)
            pl.when(me[2] == layer)(cp.wait_recv)

    return Ride(list(arrays), [jax.ShapeDtypeStruct(a.shape, a.dtype) for a in arrays], na, start, finish)


def allgather8_ride(a):
    def start(ins, outs, ss, rs, me):
        for p, mask in enumerate(ALL_PEERS):
            _remote(ins[0], outs[0].at[_dev(me)], ss, rs, p, _flip(mask, me)).start()

    def finish(ins, outs, ss, rs, me):
        for p, mask in enumerate(ALL_PEERS):
            peer = _flip(mask, me)
            _remote(ins[0], outs[0].at[_dev(peer)], ss, rs, p, peer).wait_recv()
            _remote(ins[0], outs[0].at[_dev(me)], ss, rs, p, peer).wait_send()

    return Ride([a], [jax.ShapeDtypeStruct((8,) + a.shape, a.dtype)], len(ALL_PEERS), start, finish)


def _with_own(gathered, own):
    me = _dev((lax.axis_index("x"), lax.axis_index("y"), lax.axis_index("c")))
    return jnp.where((jnp.arange(8) == me)[:, None, None], own[None], gathered)


def swap_ride(arrays):
    na = len(arrays)

    def start(ins, outs, ss, rs, me):
        for k in range(na):
            _remote(ins[k], outs[k], ss, rs, k, _flip(SIBLING[0], me)).start()

    def finish(ins, outs, ss, rs, me):
        for k in range(na):
            cp = _remote(ins[k], outs[k], ss, rs, k, _flip(SIBLING[0], me))
            cp.wait_recv()
            cp.wait_send()

    return Ride(list(arrays), [jax.ShapeDtypeStruct(a.shape, a.dtype) for a in arrays], na, start, finish)


class _Shifted:
    def __init__(self, ref, offset):
        self.ref, self.offset = ref, offset

    @property
    def at(self):
        return self

    def __getitem__(self, k):
        return self.ref.at[self.offset + k]


def merge_rides(rides):
    def spans(counts):
        out, o = [], 0
        for n in counts:
            out.append((o, o + n))
            o += n
        return out

    si, so = spans([len(r.ins) for r in rides]), spans([len(r.out_shapes) for r in rides])
    ss_ = spans([r.nsem for r in rides])

    def each(method):
        def run(ins, outs, ss, rs, me):
            for r, (i0, i1), (o0, o1), (s0, _) in zip(rides, si, so, ss_):
                getattr(r, method)(ins[i0:i1], outs[o0:o1], _Shifted(ss, s0), _Shifted(rs, s0), me)
        return run

    return Ride([a for r in rides for a in r.ins], [s for r in rides for s in r.out_shapes], sum(r.nsem for r in rides),
                each("start"), each("finish"))


def _w_in_from_chips(a):
    c2 = a[2]
    pad = jnp.zeros((c2.shape[0], IN_WP - IN_W), c2.dtype)
    return jnp.concatenate([a[0], a[1], c2[:, :252], c2[:, 260:516], c2[:, 252:260], pad, c2[:, 516:], a[3]], axis=1)
```

```python
import functools

import numpy as np
import jax
import jax.numpy as jnp
from jax import lax
from jax.experimental import pallas as pl
from jax.experimental.pallas import tpu as pltpu

F32 = jnp.float32
BF16 = jnp.bfloat16
MESH = pl.DeviceIdType.MESH

D_MODEL = 1024
DEPTH = 2
SSD_INNER = 512
POOL_W = 256
POOL_WINDOWS = (2, 4, 8, 16)
ATT_W = 256
ATT_HEADS = 4
ATT_HEAD_DIM = 64
ATT_PATTERNS = ((128, 1), (512, 4), (2048, 16))
ATT_BLOCK = 128
ROT_DIM = 16
ROPE_THETA = 500000.0
IN_W = 2568
IN_WP = 2688
IN_MAIN = 1920
FFN_DIM = 2816
NORM_EPS = 1e-6
ADAM_LR, ADAM_B1, ADAM_B2, ADAM_EPS, ADAM_WD, ADAM_STEP = 0.001, 0.9, 0.999, 1e-08, 0.01, 10

VMEM_LIMIT_BYTES = 56 * 1024 * 1024
NEG = -1e30


def _mxu(a, b, mode):
    dims = {"nn": ((1,), (0,)), "nt": ((1,), (1,)), "tn": ((0,), (0,))}[mode]
    return lax.dot_general(a.astype(BF16), b.astype(BF16), (dims, ((), ())), preferred_element_type=F32)


@functools.partial(jax.custom_vjp, nondiff_argnums=(2,))
def _bdot(a, b, mode):
    return _mxu(a, b, mode)


def _bdot_fwd(a, b, mode):
    return _mxu(a, b, mode), (a, b)


def _bdot_bwd(mode, res, g):
    a, b = res
    if mode == "nn":
        return _mxu(g, b, "nt"), _mxu(a, g, "tn")
    if mode == "nt":
        return _mxu(g, b, "nn"), _mxu(g, a, "tn")
    return _mxu(b, g, "nt"), _mxu(a, g, "nn")


_bdot.defvjp(_bdot_fwd, _bdot_bwd)


def _iota(shape, dim):
    return lax.broadcasted_iota(jnp.int32, shape, dim)


def _make_shift(h):
    @functools.partial(jax.custom_vjp, nondiff_argnums=(2,))
    def shift(halo, cur, k):
        if k == 0:
            return cur
        full = jnp.concatenate([halo, cur], axis=0)
        return pltpu.roll(full, k, 0)[h:]

    def fwd(halo, cur, k):
        return shift(halo, cur, k), None

    def bwd(k, _, g):
        t, w = g.shape
        if k == 0:
            return jnp.zeros((h, w), F32), g
        d_cur = jnp.where(_iota((t, w), 0) < t - k, pltpu.roll(g, t - k, 0), 0.0)
        top = g[:h]
        d_halo = jnp.where(_iota((h, w), 0) >= h - k, pltpu.roll(top, h - k, 0) if k < h else top, 0.0)
        return d_halo, d_cur

    shift.defvjp(fwd, bwd)
    return shift


_shift8 = _make_shift(8)
_shift16 = _make_shift(16)


def _make_tail(h):
    @jax.custom_vjp
    def tail(x):
        return x[x.shape[0] - h:]

    def fwd(x):
        return tail(x), x.shape[0]

    def bwd(t, g):
        return (jnp.concatenate([jnp.zeros((t - h, g.shape[1]), F32), g], axis=0),)

    tail.defvjp(fwd, bwd)
    return tail


_tail8 = _make_tail(8)
_tail16 = _make_tail(16)


@jax.custom_vjp
def _cumsum_rows(x):
    t = x.shape[0]
    row, s = _iota(x.shape, 0), 1
    while s < t:
        x = x + jnp.where(row >= s, pltpu.roll(x, s, 0), 0.0)
        s *= 2
    return x


def _cumsum_rows_fwd(x):
    return _cumsum_rows(x), None


def _cumsum_rows_bwd(_, g):
    t = g.shape[0]
    row, s = _iota(g.shape, 0), 1
    while s < t:
        g = g + jnp.where(row < t - s, pltpu.roll(g, t - s, 0), 0.0)
        s *= 2
    return (g,)


_cumsum_rows.defvjp(_cumsum_rows_fwd, _cumsum_rows_bwd)


@jax.custom_vjp
def _rot_pairs(t):
    e = _iota(t.shape, 1) % ATT_HEAD_DIM
    n = t.shape[1]
    return jnp.where(e < 8, -pltpu.roll(t, n - 8, 1), jnp.where(e < 16, pltpu.roll(t, 8, 1), 0.0))


def _rot_pairs_fwd(t):
    return _rot_pairs(t), None


def _rot_pairs_bwd(_, g):
    e = _iota(g.shape, 1) % ATT_HEAD_DIM
    n = g.shape[1]
    return (pltpu.roll(jnp.where(e < 8, -g, 0.0), 8, 1) + pltpu.roll(jnp.where(jnp.logical_and(e >= 8, e < 16), g, 0.0), n - 8, 1),)


_rot_pairs.defvjp(_rot_pairs_fwd, _rot_pairs_bwd)


def _make_thirds():
    @jax.custom_vjp
    def thirds(x):
        w = x.shape[1] // 3
        return x[:, :w], x[:, w:2 * w], x[:, 2 * w:]

    def fwd(x):
        return thirds(x), None

    def bwd(_, g):
        return (jnp.concatenate(g, axis=1),)

    thirds.defvjp(fwd, bwd)
    return thirds


_thirds = _make_thirds()


def _rowk(w, k):
    return jnp.sum(jnp.where(_iota(w.shape, 0) == k, w, 0.0), axis=0, keepdims=True)


def _silu(x):
    return x * (0.5 * jnp.tanh(0.5 * x) + 0.5)


def _softplus(x):
    return jnp.maximum(x, 0.0) + jnp.log(1.0 + jnp.exp(-jnp.abs(x)))


def _tile(dim, target, unit=128):
    if dim <= target:
        return dim
    best = None
    for t in range(unit, target + 1, unit):
        if dim % t == 0:
            best = t
    assert best is not None, (dim, target)
    return best


class Ride:
    def __init__(self, ins, out_shapes, nsem, start, finish):
        self.ins, self.out_shapes, self.nsem, self.start, self.finish = ins, out_shapes, nsem, start, finish

    def specs(self):
        hbm = pl.BlockSpec(memory_space=pl.ANY)
        return [hbm] * len(self.ins), [hbm] * len(self.out_shapes), [pltpu.SemaphoreType.DMA((self.nsem,))] * 2

    def begin(self, in_refs, out_refs, sems, cond=None):
        me = (lax.axis_index("x"), lax.axis_index("y"), lax.axis_index("c"))
        go = lambda: self.start(in_refs, out_refs, sems[0], sems[1], me)
        go() if cond is None else pl.when(cond)(go)

    def end(self, in_refs, out_refs, sems, cond=None):
        me = (lax.axis_index("x"), lax.axis_index("y"), lax.axis_index("c"))
        go = lambda: self.finish(in_refs, out_refs, sems[0], sems[1], me)
        go() if cond is None else pl.when(cond)(go)


class _Riders:
    def reset(self):
        self.booked, self.done = {}, {}

    def book(self, host, ride):
        assert host not in self.booked, host
        self.booked[host] = ride

    def take(self, host):
        return self.booked.pop(host, None)

    def result(self, host):
        return self.done[host]


RIDERS = _Riders()
RIDERS.reset()


class Row:
    def __init__(self, arr, w=None, fb=None, fc=None, diff=True, slot=False, dcols=None, dfc=None, ddtype=F32, view=None):
        self.ddtype = ddtype
        self.view = view
        self.arr = arr
        self.w = arr.shape[2] if w is None else w
        self.fb = (lambda b: 0) if fb is None else fb
        self.fc = (lambda b: 0) if fc is None else fc
        self.diff = diff
        self.slot = slot
        self.dcols = dcols
        self.dfc = dfc


class Vec:
    def __init__(self, arr, w=None, fc=None, diff=True):
        self.arr = arr
        self.w = arr.shape[1] if w is None else w
        self.fc = fc
        self.diff = diff


def _row_spec(r, t, nchunk, reverse):
    shape = (1, t, r.w) if r.view is None else (1, t // r.view, r.view * r.w)
    if reverse:
        return pl.BlockSpec(shape, lambda b, i, r=r: (r.fb(b), nchunk - 1 - i, r.fc(b)))
    return pl.BlockSpec(shape, lambda b, i, r=r: (r.fb(b), i, r.fc(b)))


def _load_row(ref, r, t, scr):
    if r.view is None:
        return ref[0]
    d, w = r.view, r.w
    for q in range(d):
        for j in range(w // 128):
            scr[j, pl.ds(q, t // d, stride=d), :] = ref[0, :, q * w + 128 * j:q * w + 128 * (j + 1)].astype(F32)
    return jnp.concatenate([scr[j] for j in range(w // 128)], axis=1)


def _store_row(ref, r, t, scr, val):
    if r.view is None:
        ref[0] = val.astype(ref.dtype)
        return
    d, w = r.view, r.w
    for j in range(w // 128):
        scr[j] = val[:, 128 * j:128 * (j + 1)]
    for q in range(d):
        for j in range(w // 128):
            ref[0, :, q * w + 128 * j:q * w + 128 * (j + 1)] = scr[j, pl.ds(q, t // d, stride=d), :].astype(ref.dtype)


def _view_scratch(specs, t):
    ws = [r.w for r in specs if r.view is not None]
    return [pltpu.VMEM((max(ws) // 128, t, 128), F32)] if ws else []


def _vec_spec(v):
    if v.fc is None:
        return pl.BlockSpec(v.arr.shape, lambda b, i: (0, 0))
    return pl.BlockSpec((v.arr.shape[0], v.w), lambda b, i, v=v: (0, v.fc(b)))


def _cparams():
    return pltpu.CompilerParams(dimension_semantics=("arbitrary", "arbitrary"), vmem_limit_bytes=VMEM_LIMIT_BYTES)


def scan_fwd(name, fn, *, nb, nchunk, t, rows, vecs, carries, outs, save):
    nr, nv, nc, no = len(rows), len(vecs), len(carries), len(outs)
    ns = nc if save else 0
    ride = RIDERS.take(name)
    r_in, r_out, r_scr = ride.specs() if ride else ([], [], [])

    def body(*refs):
        p = 0
        row_refs = refs[p:p + nr]; p += nr
        vec_refs = refs[p:p + nv]; p += nv
        ride_in = refs[p:p + len(r_in)]; p += len(r_in)
        out_refs = refs[p:p + no]; p += no
        save_refs = refs[p:p + ns]; p += ns
        ride_out = refs[p:p + len(r_out)]; p += len(r_out)
        car = refs[p:p + nc]; p += nc
        scr = refs[p] if stage else None
        sems = refs[p + len(stage):]
        b, i = pl.program_id(0), pl.program_id(1)
        if ride:
            ride.begin(ride_in, ride_out, sems, jnp.logical_and(b == 0, i == 0))
        if nc:
            @pl.when(i == 0)
            def _():
                for c_ref in car:
                    c_ref[...] = jnp.zeros(c_ref.shape, F32)
        cin = [c_ref[...] for c_ref in car]
        if save:
            for s_ref, cv in zip(save_refs, cin):
                s_ref[0, 0] = cv
        new_c, o = fn(i, b, cin, [_load_row(ref, r, t, scr) for ref, r in zip(row_refs, rows)], [v[...] for v in vec_refs])
        for c_ref, cv in zip(car, new_c):
            c_ref[...] = cv
        for o_ref, spec, ov in zip(out_refs, outs, o):
            _store_row(o_ref, spec, t, scr, ov)
        if ride:
            ride.end(ride_in, ride_out, sems, jnp.logical_and(b == nb - 1, i == nchunk - 1))

    stage = _view_scratch(list(rows) + list(outs), t)
    out_shape = [o.arr for o in outs]
    out_specs = [_row_spec(o, t, nchunk, False) for o in outs]
    if save:
        for cs in carries:
            out_shape.append(jax.ShapeDtypeStruct((nb, nchunk) + tuple(cs), F32))
            out_specs.append(pl.BlockSpec((1, 1) + tuple(cs), lambda b, i: (b, i, 0, 0)))
    res = pl.pallas_call(
        body, name=name, grid=(nb, nchunk),
        in_specs=[_row_spec(r, t, nchunk, False) for r in rows] + [_vec_spec(v) for v in vecs] + r_in,
        out_specs=out_specs + r_out, out_shape=out_shape + (list(ride.out_shapes) if ride else []),
        scratch_shapes=[pltpu.VMEM(tuple(cs), F32) for cs in carries] + stage + r_scr,
        compiler_params=_cparams(),
    )(*[r.arr for r in rows], *[v.arr for v in vecs], *(ride.ins if ride else []))
    if ride:
        RIDERS.done[name] = list(res[no + ns:])
    return list(res[:no]), list(res[no:no + ns])


def scan_bwd(name, fn, *, nb, nchunk, t, rows, vecs, carries, saved, douts, adds=None):
    adds = adds or {}
    nr, nv, nc, no = len(rows), len(vecs), len(carries), len(douts)
    dri = [k for k, r in enumerate(rows) if r.diff]
    dvi = [k for k, v in enumerate(vecs) if v.diff]
    add_keys = sorted(adds)
    na = len(add_keys)
    ride = RIDERS.take(name)
    r_in, r_out, r_scr = ride.specs() if ride else ([], [], [])

    def body(*refs):
        p = 0
        row_refs = refs[p:p + nr]; p += nr
        vec_refs = refs[p:p + nv]; p += nv
        save_refs = refs[p:p + nc]; p += nc
        dout_refs = refs[p:p + no]; p += no
        add_refs = refs[p:p + na]; p += na
        ride_in = refs[p:p + len(r_in)]; p += len(r_in)
        drow_refs = refs[p:p + len(dri)]; p += len(dri)
        dvec_refs = refs[p:p + len(dvi)]; p += len(dvi)
        ride_out = refs[p:p + len(r_out)]; p += len(r_out)
        dcar = refs[p:p + nc]; p += nc
        scr = refs[p] if stage else None
        sems = refs[p + len(stage):]
        b, ir = pl.program_id(0), pl.program_id(1)
        ci = nchunk - 1 - ir
        if ride:
            ride.begin(ride_in, ride_out, sems, jnp.logical_and(b == 0, ir == 0))
        if nc:
            @pl.when(ir == 0)
            def _():
                for c_ref in dcar:
                    c_ref[...] = jnp.zeros(c_ref.shape, F32)
        rows_v = [_load_row(ref, r, t, scr) for ref, r in zip(row_refs, rows)]
        vecs_v = [v[...] for v in vec_refs]
        cin = [s[0, 0] for s in save_refs]
        dc = [c_ref[...] for c_ref in dcar]
        dout_v = [_load_row(ref, r, t, scr).astype(F32) for ref, r in zip(dout_refs, douts)]

        def f(cs, dr, dv):
            rr, vv = list(rows_v), list(vecs_v)
            for k, idx in enumerate(dri):
                rr[idx] = dr[k]
            for k, idx in enumerate(dvi):
                vv[idx] = dv[k]
            return fn(ci, b, cs, rr, vv)

        _, vjp = jax.vjp(f, cin, [rows_v[k].astype(F32) for k in dri], [vecs_v[k].astype(F32) for k in dvi])
        dcin, drows, dvecs = vjp((dc, dout_v))
        for c_ref, cv in zip(dcar, dcin):
            c_ref[...] = cv
        for k, (o_ref, ov) in enumerate(zip(drow_refs, drows)):
            if dri[k] in adds:
                ov = ov + add_refs[add_keys.index(dri[k])][0].astype(F32)
            _store_row(o_ref, rows[dri[k]], t, scr, ov)
        for k, (o_ref, ov) in enumerate(zip(dvec_refs, dvecs)):
            first = (ir == 0) if vecs[dvi[k]].fc is not None else jnp.logical_and(ir == 0, b == 0)

            @pl.when(first)
            def _(o_ref=o_ref, ov=ov):
                o_ref[...] = ov

            @pl.when(jnp.logical_not(first))
            def _(o_ref=o_ref, ov=ov):
                o_ref[...] += ov

        if ride:
            ride.end(ride_in, ride_out, sems, jnp.logical_and(b == nb - 1, ir == nchunk - 1))

    stage = _view_scratch(list(rows) + list(douts), t)
    in_specs = ([_row_spec(r, t, nchunk, True) for r in rows] + [_vec_spec(v) for v in vecs]
                + [pl.BlockSpec((1, 1) + tuple(cs), lambda b, i: (b, nchunk - 1 - i, 0, 0)) for cs in carries]
                + [_row_spec(d, t, nchunk, True) for d in douts]
                + [_row_spec(adds[k], t, nchunk, True) for k in add_keys] + r_in)
    out_shape, out_specs = [], []
    for k in dri:
        r = rows[k]
        if r.slot:
            out_shape.append(jax.ShapeDtypeStruct((nb, r.arr.shape[1], r.w), r.ddtype))
            out_specs.append(pl.BlockSpec((1, t, r.w), lambda b, i: (b, nchunk - 1 - i, 0)))
        elif r.dcols is not None:
            out_shape.append(jax.ShapeDtypeStruct((r.arr.shape[0], r.arr.shape[1], r.dcols), r.ddtype))
            out_specs.append(pl.BlockSpec((1, t, r.w), lambda b, i, r=r: (r.fb(b), nchunk - 1 - i, r.dfc(b))))
        else:
            out_shape.append(jax.ShapeDtypeStruct(r.arr.shape, r.ddtype))
            out_specs.append(_row_spec(r, t, nchunk, True))
    for k in dvi:
        out_shape.append(jax.ShapeDtypeStruct(vecs[k].arr.shape, F32))
        out_specs.append(_vec_spec(vecs[k]))
    nd = len(dri) + len(dvi)
    res = pl.pallas_call(
        body, name=name, grid=(nb, nchunk), in_specs=in_specs, out_specs=out_specs + r_out,
        out_shape=out_shape + (list(ride.out_shapes) if ride else []),
        scratch_shapes=[pltpu.VMEM(tuple(cs), F32) for cs in carries] + stage + r_scr,
        compiler_params=_cparams(),
    )(*[r.arr for r in rows], *[v.arr for v in vecs], *saved, *[d.arr for d in douts], *[adds[k].arr for k in add_keys],
      *(ride.ins if ride else []))
    if ride:
        RIDERS.done[name] = list(res[nd:])
    return list(res[:len(dri)]), list(res[len(dri):nd])


def out_row(shape, dtype=F32, w=None, fb=None, fc=None):
    return Row(jax.ShapeDtypeStruct(shape, dtype), w, fb, fc)


def _conv(shift, halo, cur, w, bias, taps):
    y = bias
    for k in range(taps):
        y = y + _rowk(w, k) * shift(halo, cur, taps - 1 - k)
    return y


def _ssd_fn(ci, b, carries, rows, vecs):
    cx, cb_, cc, ht = carries
    z, xr, br, cr, dtr = rows
    cwx, cbx, cwb, cbb, cwc, cbc, dtb, alog, dsk, ng = vecs
    t = z.shape[0]
    xs = _silu(_conv(_shift8, cx, xr, cwx, cbx, 4))
    bm = _silu(_conv(_shift8, cb_, br, cwb, cbb, 4))
    cm = _silu(_conv(_shift8, cc, cr, cwc, cbc, 4))
    dt = _softplus(dtr + dtb)
    acol = _cumsum_rows(dt * (-jnp.exp(alog)))
    arow = acol.T
    r, c = _iota((t, t), 0), _iota((t, t), 1)
    causal = r >= c
    cbm = _bdot(cm, bm, "nt")
    lane, sub = _iota(acol.shape, 1), _iota(arow.shape, 0)
    colh = _iota(xs.shape, 1) // 64
    a, dtx, dx, acs = jnp.zeros(xs.shape, F32), jnp.zeros(xs.shape, F32), jnp.zeros((1, xs.shape[1]), F32), []
    for j in range(4):
        h = 4 * b + j
        ac = jnp.sum(jnp.where(lane == h, acol, 0.0), axis=1, keepdims=True)
        acs.append(ac)
        a = jnp.where(colh == j, ac, a)
        dtx = jnp.where(colh == j, jnp.sum(jnp.where(lane == h, dt, 0.0), axis=1, keepdims=True), dtx)
        dx = jnp.where(_iota(dx.shape, 1) // 64 == j, jnp.sum(jnp.where(_iota(dsk.shape, 1) == h, dsk, 0.0), axis=1, keepdims=True), dx)
    atot = jnp.sum(jnp.where(_iota(a.shape, 0) == t - 1, a, 0.0), axis=0, keepdims=True)
    x = xs * dtx
    ydiag = jnp.zeros(x.shape, F32)
    for j in range(4):
        ar = jnp.sum(jnp.where(sub == 4 * b + j, arow, 0.0), axis=0, keepdims=True)
        lmat = jnp.exp(jnp.where(causal, acs[j] - ar, NEG))
        ydiag = ydiag + _bdot(cbm * lmat, jnp.where(colh == j, x, 0.0), "nn")
    yoff = _bdot(cm, ht, "nn") * jnp.exp(a)
    ht_new = ht * jnp.exp(atot) + _bdot(bm, x * jnp.exp(atot - a), "tn")
    y = ydiag + yoff + dx * xs
    yz = y * _silu(z)
    yn = yz * lax.rsqrt(jnp.mean(yz * yz, axis=-1, keepdims=True) + NORM_EPS) * ng
    return [_tail8(xr), _tail8(br), _tail8(cr), ht_new], [yn]


_SSD_T = 256
_SSD_CARRIES = [(8, 256), (8, 128), (8, 128), (128, 256)]


def _ssd_io(proj3, p):
    own = lambda b: b
    rows = [Row(proj3, 256, fc=own, dcols=512, dfc=own, ddtype=BF16),
            Row(proj3, 256, fc=lambda b: 2 + b, dcols=512, dfc=own, ddtype=BF16),
            Row(proj3, 128, fc=lambda b: 8 + b, dcols=256, dfc=own, ddtype=BF16),
            Row(proj3, 128, fc=lambda b: 10 + b, dcols=256, dfc=own, ddtype=BF16),
            Row(proj3, 128, fc=lambda b: 14, slot=True)]
    vecs = [Vec(p["cw"], 256, lambda b: b), Vec(p["cb"], 256, lambda b: b),
            Vec(p["cw"], 128, lambda b: 4 + b), Vec(p["cb"], 128, lambda b: 4 + b),
            Vec(p["cw"], 128, lambda b: 6 + b), Vec(p["cb"], 128, lambda b: 6 + b),
            Vec(p["dtb"]), Vec(p["alog"]), Vec(p["dsk"]), Vec(p["ng"], 256, lambda b: b)]
    return rows, vecs


def ssd_forward(name, proj3, p):
    rows, vecs = _ssd_io(proj3, p)
    s = proj3.shape[1]
    (y,), saved = scan_fwd(name, _ssd_fn, nb=2, nchunk=s // _SSD_T, t=_SSD_T, rows=rows, vecs=vecs,
                           carries=_SSD_CARRIES, outs=[out_row((1, s, SSD_INNER), BF16, 256, fc=lambda b: b)], save=True)
    return y, saved


def ssd_backward(name, proj3, p, saved, dmix3):
    rows, vecs = _ssd_io(proj3, p)
    s = proj3.shape[1]
    drows, dvecs = scan_bwd(name, _ssd_fn, nb=2, nchunk=s // _SSD_T, t=_SSD_T, rows=rows, vecs=vecs,
                            carries=_SSD_CARRIES, saved=saved, douts=[Row(dmix3, 256, fc=lambda b: b)])
    return drows, dvecs


def _pool_fn(ci, b, carries, rows, vecs):
    (cu,) = carries
    (u,) = rows
    wbd, scale = vecs
    t = u.shape[0]
    pos = ci * t + _iota(u.shape, 0)
    grp = _iota(u.shape, 1) // 64
    acc, pooled, k = u, jnp.zeros(u.shape, F32), 1
    for gi, w in enumerate(POOL_WINDOWS):
        while k < w:
            acc = acc + _shift16(cu, u, k)
            k += 1
        pooled = jnp.where(grp == gi, acc / jnp.minimum(pos + 1, w).astype(F32), pooled)
    y = _bdot(pooled - u, wbd, "nn") * scale
    return [_tail16(u)], [y]


_POOL_T = 256


def _pool_io(proj3, wbd, scale):
    return [Row(proj3, 256, fc=lambda b: 6, dcols=256, dfc=lambda b: 0, ddtype=BF16)], [Vec(wbd), Vec(scale)]


def pool_forward(name, proj3, wbd, scale):
    rows, vecs = _pool_io(proj3, wbd, scale)
    s = proj3.shape[1]
    (y,), saved = scan_fwd(name, _pool_fn, nb=1, nchunk=s // _POOL_T, t=_POOL_T, rows=rows, vecs=vecs,
                           carries=[(16, 256)], outs=[out_row((1, s, POOL_W), BF16)], save=True)
    return y, saved


def pool_backward(name, proj3, wbd, scale, saved, dmix3):
    rows, vecs = _pool_io(proj3, wbd, scale)
    s = proj3.shape[1]
    return scan_bwd(name, _pool_fn, nb=1, nchunk=s // _POOL_T, t=_POOL_T, rows=rows, vecs=vecs,
                    carries=[(16, 256)], saved=saved, douts=[Row(dmix3, 256, fc=lambda b: 2)])


def _attn_fn(ci, b, carries, rows, vecs):
    kp, vp = carries
    qr, kr, v = _thirds(rows[0])
    scale = ATT_HEAD_DIM ** -0.5
    q = qr
    n = q.shape[0]
    r, c = _iota((n, n), 0), _iota((n, n), 1)
    prev_ok, cur_ok = jnp.logical_and(c >= r, ci > 0), r >= c
    head = _iota(q.shape, 1) // ATT_HEAD_DIM
    o, lse = jnp.zeros(q.shape, F32), jnp.zeros(q.shape, F32)
    for h in range(ATT_HEADS):
        mine = head == h
        qh = jnp.where(mine, qr, 0.0)
        sp = jnp.where(prev_ok, _bdot(qh, kp, "nt") * scale, NEG)
        sc = jnp.where(cur_ok, _bdot(qh, kr, "nt") * scale, NEG)
        m = lax.stop_gradient(jnp.maximum(jnp.max(sp, axis=1, keepdims=True), jnp.max(sc, axis=1, keepdims=True)))
        pp, pc = jnp.exp(sp - m), jnp.exp(sc - m)
        l = jnp.sum(pp, axis=1, keepdims=True) + jnp.sum(pc, axis=1, keepdims=True)
        o = jnp.where(mine, (_bdot(pp, vp, "nn") + _bdot(pc, v, "nn")) / l, o)
        lse = jnp.where(mine, m + jnp.log(l), lse)
    return [kr, v], [o, lse]


_ATT_CARRIES = [(ATT_BLOCK, ATT_W), (ATT_BLOCK, ATT_W)]


def attn_forward(name, pv, d):
    l = pv.shape[1]
    own = lambda b: b
    outs = [out_row((1, l, d * ATT_W), F32, ATT_W, fc=own) for _ in range(2)]
    (o, lse), saved = scan_fwd(name, _attn_fn, nb=d, nchunk=l // ATT_BLOCK, t=ATT_BLOCK, rows=[Row(pv, 3 * ATT_W, fc=own)],
                               vecs=[], carries=_ATT_CARRIES, outs=outs, save=True)
    return o, lse, saved


def attn_backward(name, pv, d, saved, do, dlse):
    l = pv.shape[1]
    own = lambda b: b
    (dpv,), _ = scan_bwd(name, _attn_fn, nb=d, nchunk=l // ATT_BLOCK, t=ATT_BLOCK, rows=[Row(pv, 3 * ATT_W, fc=own)], vecs=[],
                         carries=_ATT_CARRIES, saved=saved, douts=[Row(do, ATT_W, fc=own), Row(dlse, ATT_W, fc=own)])
    return dpv


def _rope_fn(ci, b, carries, rows, vecs):
    x, cs, sn = rows
    return [], [x * cs + _rot_pairs(x) * sn]


def _rope3_fn(ci, b, carries, rows, vecs):
    _, (y,) = _rope_fn(ci, b, carries, rows, vecs)
    return [], [y, y, y]


def _by_residue(a_or_shape, w, d):
    if isinstance(a_or_shape, tuple):
        _, s, _ = a_or_shape
        return Row(jax.ShapeDtypeStruct((1, s // d, d * w), F32), w, view=None if d == 1 else d)
    return Row(a_or_shape, w, view=None if d == 1 else d)


def rope_forward(name, qkv3, cs3, sn3):
    s, w = qkv3.shape[1], qkv3.shape[2]
    ys, _ = scan_fwd(name, _rope3_fn, nb=1, nchunk=s // _ROW_T, t=_ROW_T, vecs=[], carries=[], save=False,
                     rows=[Row(qkv3), Row(cs3, diff=False), Row(sn3, diff=False)],
                     outs=[_by_residue(qkv3.shape, w, d) for _, d in ATT_PATTERNS])
    return ys


def rope_backward(name, qkv3, cs3, sn3, dys):
    s, w = qkv3.shape[1], qkv3.shape[2]
    (dx,), _ = scan_bwd(name, _rope3_fn, nb=1, nchunk=s // _ROW_T, t=_ROW_T, vecs=[], carries=[], saved=[],
                        rows=[Row(qkv3, ddtype=BF16), Row(cs3, diff=False), Row(sn3, diff=False)],
                        douts=[_by_residue(a, w, d) for a, (_, d) in zip(dys, ATT_PATTERNS)])
    return dx


def _merge_fn(ci, b, carries, rows, vecs):
    o1, o2, o3, l1, l2, l3 = rows
    mx = lax.stop_gradient(jnp.maximum(l1, jnp.maximum(l2, l3)))
    e1, e2, e3 = jnp.exp(l1 - mx), jnp.exp(l2 - mx), jnp.exp(l3 - mx)
    return [], [(e1 * o1 + e2 * o2 + e3 * o3) / (e1 + e2 + e3)]


_ROW_T = 512


def _merge_rows(os_, ls_):
    ds = [d for _, d in ATT_PATTERNS]
    return [_by_residue(a, ATT_W, d) for a, d in zip(os_, ds)] + [_by_residue(a, ATT_W, d) for a, d in zip(ls_, ds)]


def merge_forward(name, os_, ls_, s):
    (y,), _ = scan_fwd(name, _merge_fn, nb=1, nchunk=s // _ROW_T, t=_ROW_T, rows=_merge_rows(os_, ls_), vecs=[],
                       carries=[], outs=[out_row((1, s, ATT_W), BF16)], save=False)
    return y


def merge_backward(name, os_, ls_, dmix3):
    s = dmix3.shape[1]
    drows, _ = scan_bwd(name, _merge_fn, nb=1, nchunk=s // _ROW_T, t=_ROW_T, rows=_merge_rows(os_, ls_), vecs=[],
                        carries=[], saved=[], douts=[Row(dmix3, 256, fc=lambda b: 3)])
    return drows


def _norm_mod_fn(ci, b, carries, rows, vecs):
    (x,) = rows
    g, sc, sh = vecs
    xn = x * lax.rsqrt(jnp.mean(x * x, axis=-1, keepdims=True) + NORM_EPS)
    return [], [xn * g * (1.0 + sc) + sh]


def norm_mod_forward(name, x3, g, sc, sh):
    s = x3.shape[1]
    (h,), _ = scan_fwd(name, _norm_mod_fn, nb=1, nchunk=s // _ROW_T, t=_ROW_T, rows=[Row(x3)], vecs=[Vec(g), Vec(sc), Vec(sh)],
                       carries=[], outs=[out_row(x3.shape, BF16)], save=False)
    return h


def norm_mod_backward(name, x3, g, sc, sh, dh3, add3):
    s = x3.shape[1]
    (dx,), dv = scan_bwd(name, _norm_mod_fn, nb=1, nchunk=s // _ROW_T, t=_ROW_T, rows=[Row(x3)], vecs=[Vec(g), Vec(sc), Vec(sh)],
                         carries=[], saved=[], douts=[Row(dh3)], adds={0: Row(add3)})
    return dx, dv


def _gate_fn(ci, b, carries, rows, vecs):
    return [], [rows[0] * vecs[0]]


def gate_backward(name, o3, g, dx3):
    s = o3.shape[1]
    (do,), (dg,) = scan_bwd(name, _gate_fn, nb=1, nchunk=s // _ROW_T, t=_ROW_T, rows=[Row(o3, ddtype=BF16)], vecs=[Vec(g)],
                            carries=[], saved=[], douts=[Row(dx3)])
    return do, dg


def _make_halves():
    @jax.custom_vjp
    def halves(x):
        h = x.shape[1] // 2
        return x[:, :h], x[:, h:]

    def fwd(x):
        return halves(x), None

    def bwd(_, g):
        return (jnp.concatenate(g, axis=1),)

    halves.defvjp(fwd, bwd)
    return halves


_halves = _make_halves()


def _ffn_fn(ci, b, carries, rows, vecs):
    (cu,) = carries
    (u,) = rows
    w, bias = vecs
    hg, hu = _halves(_conv(_shift8, cu, u, w, bias, 3))
    return [_tail8(u)], [_silu(hg) * hu]


_FFN_T = 256
_FFN_CW = FFN_DIM // 2
_FFN_CARRIES = [(8, 2 * _FFN_CW)]
FFN_BLOCK_ORDER = [0, 2, 1, 3]


def _ffn_io(up3, cw, cb):
    own = lambda b: b
    return [Row(up3, 2 * _FFN_CW, fc=own, ddtype=BF16)], [Vec(cw, 2 * _FFN_CW, own), Vec(cb, 2 * _FFN_CW, own)]


def ffn_down_forward(name, up3, cw, cb, w_down, res, gate):
    s, t, cw2 = up3.shape[1], _FFN_T, 2 * _FFN_CW
    d = w_down.shape[1]
    nchunk = s // t
    ride = RIDERS.take(name)
    r_in, r_out, r_scr = ride.specs() if ride else ([], [], [])

    def body(*refs):
        up_ref, cw_ref, cb_ref, wd_ref, res_ref, g_ref = refs[:6]
        ride_in = refs[6:6 + len(r_in)]
        act_ref, save_ref, dn_ref, x2_ref = refs[6 + len(r_in):10 + len(r_in)]
        ride_out = refs[10 + len(r_in):10 + len(r_in) + len(r_out)]
        car, acc = refs[10 + len(r_in) + len(r_out):12 + len(r_in) + len(r_out)]
        sems = refs[12 + len(r_in) + len(r_out):]
        i, b = pl.program_id(0), pl.program_id(1)
        if ride:
            ride.begin(ride_in, ride_out, sems, jnp.logical_and(i == 0, b == 0))

        @pl.when(i == 0)
        def _():
            car[b] = jnp.zeros(car.shape[1:], F32)

        cin = car[b]
        save_ref[0, 0] = cin
        (new_c,), (act,) = _ffn_fn(i, b, [cin], [up_ref[0]], [cw_ref[...], cb_ref[...]])
        car[b] = new_c
        act_ref[0] = act.astype(act_ref.dtype)
        part = _mxu(act, wd_ref[...], "nn")

        @pl.when(b == 0)
        def _():
            acc[...] = part

        @pl.when(b == 1)
        def _():
            tot = acc[...] + part
            dn_ref[...] = tot
            x2_ref[...] = res_ref[...] + g_ref[...] * tot

        if ride:
            ride.end(ride_in, ride_out, sems, jnp.logical_and(i == nchunk - 1, b == 1))

    tile = pl.BlockSpec((t, d), lambda i, b: (i, 0))
    out = pl.pallas_call(
        body, name=name, grid=(nchunk, 2),
        in_specs=[pl.BlockSpec((1, t, cw2), lambda i, b: (0, i, b)), pl.BlockSpec((cw.shape[0], cw2), lambda i, b: (0, b)),
                  pl.BlockSpec((1, cw2), lambda i, b: (0, b)), pl.BlockSpec((_FFN_CW, d), lambda i, b: (b, 0)), tile,
                  pl.BlockSpec((1, d), lambda i, b: (0, 0))] + r_in,
        out_specs=[pl.BlockSpec((1, t, _FFN_CW), lambda i, b: (0, i, b)), pl.BlockSpec((1, 1, 8, cw2), lambda i, b: (b, i, 0, 0)),
                   tile, tile] + r_out,
        out_shape=[jax.ShapeDtypeStruct((1, s, FFN_DIM), BF16), jax.ShapeDtypeStruct((2, nchunk, 8, cw2), F32),
                   jax.ShapeDtypeStruct((s, d), F32), jax.ShapeDtypeStruct((s, d), F32)] + (list(ride.out_shapes) if ride else []),
        scratch_shapes=[pltpu.VMEM((2, 8, cw2), F32), pltpu.VMEM((t, d), F32)] + r_scr,
        compiler_params=_cparams(),
    )(up3, cw, cb, w_down, res, gate, *(ride.ins if ride else []))
    if ride:
        RIDERS.done[name] = list(out[4:])
    return out[0], [out[1]], out[2], out[3]


def ffn_mid_backward(name, up3, cw, cb, saved, dact3):
    rows, vecs = _ffn_io(up3, cw, cb)
    s = up3.shape[1]
    return scan_bwd(name, _ffn_fn, nb=2, nchunk=s // _FFN_T, t=_FFN_T, rows=rows, vecs=vecs, carries=_FFN_CARRIES,
                    saved=saved, douts=[Row(dact3, _FFN_CW, fc=lambda b: b)])


def _adam_fn(ci, b, carries, rows, vecs):
    w, g, m, v = rows
    m = ADAM_B1 * m + (1.0 - ADAM_B1) * g
    v = ADAM_B2 * v + (1.0 - ADAM_B2) * (g * g)
    m_hat = m / (1.0 - ADAM_B1 ** ADAM_STEP)
    v_hat = v / (1.0 - ADAM_B2 ** ADAM_STEP)
    delta = -ADAM_LR * (m_hat / (jnp.sqrt(v_hat) + ADAM_EPS) + ADAM_WD * w)
    return [], [delta, m, v]


def _adam_layers_fn(ci, b, carries, rows, vecs):
    w, mine, theirs, m, v = rows
    g = jnp.where(b == lax.axis_index("c"), mine, theirs)
    _, upd = _adam_fn(ci, b, carries, [w, g, m, v], vecs)
    return [], [g] + upd


def adamw_layers(name, w, mine, theirs, m, v):
    _, r, c = w.shape
    t = _tile(r, 256, 8)
    layer = lambda b: b
    rows = [Row(w, fb=layer), Row(mine[None]), Row(theirs[None]), Row(m, fb=layer), Row(v, fb=layer)]
    outs, _ = scan_fwd(name, _adam_layers_fn, nb=2, nchunk=r // t, t=t, rows=rows, vecs=[], carries=[],
                       outs=[out_row(w.shape, fb=layer) for _ in range(4)], save=False)
    return outs


def adamw(name, w, g, m, v):
    shape = w.shape
    c = shape[-1]
    r = int(np.prod(shape[:-1]))
    t = _tile(r, 256, 8)
    as3 = lambda a: a.reshape(1, r, c)
    outs, _ = scan_fwd(name, _adam_fn, nb=1, nchunk=r // t, t=t, rows=[Row(as3(a)) for a in (w, g, m, v)], vecs=[], carries=[],
                       outs=[out_row((1, r, c)) for _ in range(3)], save=False)
    return [o.reshape(shape) for o in outs]


def rope_tables(positions):
    inv_freq = ROPE_THETA ** (-jnp.arange(0, ROT_DIM, 2, dtype=F32) / ROT_DIM)
    ang = positions.astype(F32)[:, None] * inv_freq
    s = positions.shape[0]
    cs = jnp.concatenate([jnp.cos(ang), jnp.cos(ang), jnp.ones((s, ATT_HEAD_DIM - ROT_DIM), F32)], axis=1)
    sn = jnp.concatenate([jnp.sin(ang), jnp.sin(ang), jnp.zeros((s, ATT_HEAD_DIM - ROT_DIM), F32)], axis=1)
    cs3 = jnp.concatenate([jnp.tile(cs, (1, 2 * ATT_HEADS)), jnp.ones((s, ATT_W), F32)], axis=1)
    sn3 = jnp.concatenate([jnp.tile(sn, (1, 2 * ATT_HEADS)), jnp.zeros((s, ATT_W), F32)], axis=1)
    return cs3[None], sn3[None]


def attention_forward(lname, qkv3, cs3, sn3):
    s = qkv3.shape[1]
    rotated = rope_forward(f"{lname}_rope", qkv3, cs3, sn3)
    os_, ls_, keep = [], [], []
    for pi, (_, d) in enumerate(ATT_PATTERNS):
        o, lse, saved = attn_forward(f"{lname}_attn{pi}", rotated[pi], d)
        os_.append(o)
        ls_.append(lse)
        keep.append(saved)
    y = merge_forward(f"{lname}_merge", os_, ls_, s)
    return y, (rotated, os_, ls_, keep)


def attention_backward(lname, qkv3, cs3, sn3, res, dmix3):
    rotated, os_, ls_, keep = res
    dm = merge_backward(f"{lname}_merge_b", os_, ls_, dmix3)
    dys = [attn_backward(f"{lname}_attn{pi}_b", rotated[pi], d, keep[pi], dm[pi], dm[3 + pi]) for pi, (_, d) in enumerate(ATT_PATTERNS)]
    return rope_backward(f"{lname}_rope_b", qkv3, cs3, sn3, dys)


def final_loss(name, x3, t3, g):
    s, d = x3.shape[1], x3.shape[2]
    t = _ROW_T

    def body(x_ref, t_ref, g_ref, loss_ref, dx_ref, dg_ref):
        i = pl.program_id(0)
        tv = t_ref[0]

        def f(x, gg):
            y = x * lax.rsqrt(jnp.mean(x * x, axis=-1, keepdims=True) + NORM_EPS) * gg
            e = y - tv
            return 0.5 * jnp.sum(jnp.mean(e * e, axis=-1, keepdims=True), axis=0, keepdims=True)

        l, vjp = jax.vjp(f, x_ref[0], g_ref[...])
        dx, dg = vjp(jnp.ones((1, 1), F32))
        dx_ref[0] = dx

        @pl.when(i == 0)
        def _():
            loss_ref[...] = jnp.zeros(loss_ref.shape, F32)
            dg_ref[...] = jnp.zeros(dg_ref.shape, F32)

        loss_ref[...] += jnp.broadcast_to(l, loss_ref.shape)
        dg_ref[...] += dg

    row = pl.BlockSpec((1, t, d), lambda i: (0, i, 0))
    vec = pl.BlockSpec((1, d), lambda i: (0, 0))
    return pl.pallas_call(
        body, name=name, grid=(s // t,), in_specs=[row, row, vec],
        out_specs=[pl.BlockSpec((8, 128), lambda i: (0, 0)), row, vec],
        out_shape=[jax.ShapeDtypeStruct((8, 128), F32), jax.ShapeDtypeStruct(x3.shape, F32), jax.ShapeDtypeStruct((1, d), F32)],
        compiler_params=pltpu.CompilerParams(dimension_semantics=("arbitrary",), vmem_limit_bytes=VMEM_LIMIT_BYTES),
    )(x3, t3, g)


_ADA_TN = 512


def ada_forward(name, c16, ada_w):
    depth, d, cols = ada_w.shape

    def body(c_ref, w_ref, o_ref):
        o_ref[0] = _mxu(_silu(c_ref[...]), w_ref[0], "nn")

    return pl.pallas_call(
        body, name=name, grid=(depth, cols // _ADA_TN),
        in_specs=[pl.BlockSpec((16, d), lambda l, j: (0, 0)), pl.BlockSpec((1, d, _ADA_TN), lambda l, j: (l, 0, j))],
        out_specs=pl.BlockSpec((1, 16, _ADA_TN), lambda l, j: (l, 0, j)),
        out_shape=jax.ShapeDtypeStruct((depth, 16, cols), F32),
        compiler_params=pltpu.CompilerParams(dimension_semantics=("arbitrary", "arbitrary"), vmem_limit_bytes=VMEM_LIMIT_BYTES),
    )(c16, ada_w)


def ada_backward(name, c16, dmod16, w, m, v):
    depth, d, cols = w.shape

    def body(c_ref, dm_ref, w_ref, m_ref, v_ref, g_ref, dl_ref, nm_ref, nv_ref):
        g = _mxu(_silu(c_ref[...]), dm_ref[0], "tn")
        _, (delta, nm, nv) = _adam_fn(None, None, [], [w_ref[0], g, m_ref[0], v_ref[0]], [])
        g_ref[0], dl_ref[0], nm_ref[0], nv_ref[0] = g, delta, nm, nv

    blk = pl.BlockSpec((1, d, _ADA_TN), lambda l, j: (l, 0, j))
    return pl.pallas_call(
        body, name=name, grid=(depth, cols // _ADA_TN),
        in_specs=[pl.BlockSpec((16, d), lambda l, j: (0, 0)), pl.BlockSpec((1, 16, _ADA_TN), lambda l, j: (l, 0, j)), blk, blk, blk],
        out_specs=[blk] * 4, out_shape=[jax.ShapeDtypeStruct(w.shape, F32)] * 4,
        compiler_params=pltpu.CompilerParams(dimension_semantics=("arbitrary", "arbitrary"), vmem_limit_bytes=VMEM_LIMIT_BYTES),
    )(c16, dmod16, w, m, v)


def _sum_fn(ci, b, carries, rows, vecs):
    acc = rows[0].astype(F32)
    for r in rows[1:]:
        acc = acc + r.astype(F32)
    return [], [acc]


def sum_slots(name, a, nsum, out_dtype=F32):
    n, r, c = a.shape
    nb = n // nsum
    t = _tile(r, 256, 8)
    rows = [Row(a, fb=(lambda b, k=k: k * nb + b)) for k in range(nsum)]
    (out,), _ = scan_fwd(name, _sum_fn, nb=nb, nchunk=r // t, t=t, rows=rows, vecs=[], carries=[],
                         outs=[out_row((nb, r, c), out_dtype, fb=lambda b: b)], save=False)
    return out


def _flip(mask, pos):
    return tuple((1 - p) if m else p for m, p in zip(mask, pos))


ALL_PEERS = [(a, b, c) for a in (0, 1) for b in (0, 1) for c in (0, 1)][1:]
CHIP_PEERS = [(1, 0, 0), (0, 1, 0), (1, 1, 0)]
SIBLING = [(0, 0, 1)]


def _dev(pos):
    return 4 * pos[0] + 2 * pos[1] + pos[2]


def _chip(pos):
    return 2 * pos[0] + pos[1]


def allgather8(name, a):
    (out,) = ride_alone(name, allgather8_ride(a))
    return _with_own(out, a)


def _rows_of(shape):
    return -(-int(np.prod(shape)) // 1024) * 8


def _pack(arrs):
    parts = []
    for a in arrs:
        flat = a.reshape(-1).astype(F32)
        parts.append(jnp.pad(flat, (0, _rows_of(a.shape) * 128 - flat.shape[0])).reshape(-1, 128))
    rows = sum(p.shape[0] for p in parts)
    parts.append(jnp.zeros(((-rows) % _ROW_T, 128), F32))
    return jnp.concatenate(parts, axis=0)


def _unpack(buf, shapes):
    out, o = [], 0
    for s in shapes:
        r, n = _rows_of(s), int(np.prod(s))
        out.append(buf[o:o + r].reshape(-1)[:n].reshape(s))
        o += r
    return out


_WEIGHTS = ["ada_w", "ada_b", "norm1_g", "w_in", "ssd_conv_w", "ssd_conv_b", "ssd_dt_bias", "ssd_a_log", "ssd_d", "ssd_norm_g",
            "pool_w", "pool_scale", "w_out", "norm2_g", "ffn_up", "ffn_conv_w", "ffn_conv_b", "ffn_down", "final_g"]
_BIG = ["w_in", "w_out", "ffn_up", "ffn_down"]
_SMALL = [n for n in _WEIGHTS if n not in _BIG and n != "ada_w"]
_COL_SHARDED_SMALL = {"ssd_conv_w": 256, "ffn_conv_w": 1408}


def _pad_lanes(v, n=128):
    return jnp.pad(v.astype(F32), (0, n - v.shape[0]))[None]


_CHIP2_PARTS = [(1284, 1536), (1792, 1800), (1536, 1792), (IN_MAIN, IN_MAIN + 126)]


def _w_in_chip_cols(gp):
    q = IN_W // 4
    return [gp[:, :q], gp[:, q:2 * q], jnp.concatenate([gp[:, a:b] for a, b in _CHIP2_PARTS], axis=1), gp[:, IN_WP - q:]]


def _ffn_block_perm(a):
    n = a.shape[-1] // 4
    return jnp.concatenate([a[..., j * n:(j + 1) * n] for j in FFN_BLOCK_ORDER], axis=-1)


def _layer_forward(i, x3, modv, wts, sp, cs3, sn3):
    sh1, sc1, g1, sh2, sc2, g2 = modv
    big = lambda n: wts[n]() if callable(wts[n]) else wts[n]
    h1 = norm_mod_forward(f"l{i}_norm1", x3, wts["norm1_g"], sc1, sh1)
    proj3 = mm(f"l{i}_proj", h1[0], big("w_in")[:, :IN_MAIN], "nn")[None]
    qkv3 = mm(f"l{i}_qkv", h1[0], big("w_in")[:, IN_MAIN:], "nn")[None]
    y_ssd, sv_ssd = ssd_forward(f"l{i}_ssd", proj3, sp)
    y_pool, sv_pool = pool_forward(f"l{i}_pool", proj3, wts["wbd"], wts["pool_scale"])
    y_att, res_att = attention_forward(f"l{i}", qkv3, cs3, sn3)
    mix = jnp.concatenate([y_ssd, y_pool, y_att], axis=-1)
    out, x1 = mm(f"l{i}_wout", mix[0], big("w_out"), "nn", res=x3[0], gate=g1)
    x1 = x1[None]
    h2 = norm_mod_forward(f"l{i}_norm2", x1, wts["norm2_g"], sc2, sh2)
    up3 = mm(f"l{i}_up", h2[0], big("ffn_up"), "nn")[None]
    act, sv_ffn, dn, x2 = ffn_down_forward(f"l{i}_down", up3, wts["ffn_conv_w"], wts["ffn_conv_b"], big("ffn_down"), x1[0], g2)
    keep = dict(x=x3, h1=h1, proj3=proj3, qkv3=qkv3, sv_ssd=sv_ssd, sv_pool=sv_pool, res_att=res_att, mix=mix, out=out[None],
                x1=x1, h2=h2, up3=up3, act=act, sv_ffn=sv_ffn, dn=dn[None])
    return x2[None], keep


def _layer_backward(i, dx2, keep, modv, wts, sp, cs3, sn3, after=None):
    sh1, sc1, g1, sh2, sc2, g2 = modv
    k = keep
    big = lambda n: wts[n]() if callable(wts[n]) else wts[n]
    tell = lambda step, *a: after[step](*a) if after and step in after else None
    d_dn, d_g2 = gate_backward(f"l{i}_gate2_b", k["dn"], g2, dx2)
    d_act = mm(f"l{i}_down_bx", d_dn[0], big("ffn_down"), "nt")
    g_down = mm(f"l{i}_down_bw", k["act"][0], d_dn[0], "tn", BF16).reshape(4, FFN_DIM // 4, D_MODEL)
    (d_up,), dv_ffn = ffn_mid_backward(f"l{i}_ffn_b", k["up3"], wts["ffn_conv_w"], wts["ffn_conv_b"], k["sv_ffn"], d_act[None])
    tell("ffn_b")
    d_h2 = mm(f"l{i}_up_bx", d_up[0], big("ffn_up"), "nt")
    g_up = mm(f"l{i}_up_bw", k["h2"][0], d_up[0], "tn", BF16, tn=_FFN_CW,
              into=((4, D_MODEL, _FFN_CW), lambda r, c: ((c % 2) * 2 + c // 2, r, 0)))
    dx1, (d_n2, d_sc2, d_sh2) = norm_mod_backward(f"l{i}_norm2_b", k["x1"], wts["norm2_g"], sc2, sh2, d_h2[None], dx2)
    d_out, d_g1 = gate_backward(f"l{i}_gate1_b", k["out"], g1, dx1)
    d_mix = mm(f"l{i}_wout_bx", d_out[0], big("w_out"), "nt")[None]
    g_wout = mm(f"l{i}_wout_bw", k["mix"][0], d_out[0], "tn", BF16).reshape(4, D_MODEL // 4, D_MODEL)
    tell("wout_bw", g_wout, g_up, g_down)
    (dz, dxs, dbm, dcm, ddt), dv_ssd = ssd_backward(f"l{i}_ssd_b", k["proj3"], sp, k["sv_ssd"], d_mix)
    tell("ssd_b")
    (du_pool,), (d_wbd, d_pscale) = pool_backward(f"l{i}_pool_b", k["proj3"], wts["wbd"], wts["pool_scale"], k["sv_pool"], d_mix)
    d_qkv = attention_backward(f"l{i}", k["qkv3"], cs3, sn3, k["res_att"], d_mix)
    d_proj = jnp.concatenate([dz[0], dxs[0], dbm[0], dcm[0], du_pool[0], (ddt[0] + ddt[1]).astype(BF16), d_qkv[0]], axis=-1)
    g_win = jnp.stack(_w_in_chip_cols(mm(f"l{i}_proj_bw", k["h1"][0], d_proj, "tn", BF16)))
    tell("proj_bw", g_win)
    d_h1 = mm(f"l{i}_proj_bx", d_proj, big("w_in"), "nt")
    tell("proj_bx")
    dx, (d_n1, d_sc1, d_sh1) = norm_mod_backward(f"l{i}_norm1_b", k["x"], wts["norm1_g"], sc1, sh1, d_h1[None], dx1)
    dcwx, dcbx, dcwb, dcbb, dcwc, dcbc, ddtb, dalog, ddsk, dng = dv_ssd
    small = dict(
        norm1_g=d_n1[0], norm2_g=d_n2[0],
        ssd_conv_w=jnp.concatenate([dcwx[:, :512], dcwb[:, 512:768], dcwc[:, 768:]], axis=1),
        ssd_conv_b=jnp.concatenate([dcbx[0, :512], dcbb[0, 512:768], dcbc[0, 768:]]),
        ssd_dt_bias=ddtb[0, :8], ssd_a_log=dalog[0, :8], ssd_d=ddsk[0, :8], ssd_norm_g=dng[0],
        pool_w=jnp.stack([d_wbd[64 * g:64 * g + 64, 64 * g:64 * g + 64] for g in range(4)]), pool_scale=d_pscale[0],
        ffn_conv_w=_ffn_block_perm(dv_ffn[0]), ffn_conv_b=_ffn_block_perm(dv_ffn[1][0]),
    )
    dmod = jnp.concatenate([d_sh1[0], d_sc1[0], d_g1[0], d_sh2[0], d_sc2[0], d_g2[0]])
    return dx, [g_win, g_wout, g_up, g_down], small, dmod


def kernel(x, c, positions, ada_w, ada_b, norm1_g, w_in, ssd_conv_w, ssd_conv_b, ssd_dt_bias, ssd_a_log, ssd_d, ssd_norm_g, pool_w, pool_scale, w_out, norm2_g, ffn_up, ffn_conv_w, ffn_conv_b, ffn_down, final_g, loss_target, m_ada_w, m_ada_b, m_norm1_g, m_w_in, m_ssd_conv_w, m_ssd_conv_b, m_ssd_dt_bias, m_ssd_a_log, m_ssd_d, m_ssd_norm_g, m_pool_w, m_pool_scale, m_w_out, m_norm2_g, m_ffn_up, m_ffn_conv_w, m_ffn_conv_b, m_ffn_down, m_final_g, v_ada_w, v_ada_b, v_norm1_g, v_w_in, v_ssd_conv_w, v_ssd_conv_b, v_ssd_dt_bias, v_ssd_a_log, v_ssd_d, v_ssd_norm_g, v_pool_w, v_pool_scale, v_w_out, v_norm2_g, v_ffn_up, v_ffn_conv_w, v_ffn_conv_b, v_ffn_down, v_final_g):
    args = dict(locals())
    w = {n: args[n] for n in _WEIGHTS}
    m = {n: args["m_" + n] for n in _WEIGHTS}
    v = {n: args["v_" + n] for n in _WEIGHTS}
    d = D_MODEL
    me = (lax.axis_index("x"), lax.axis_index("y"), lax.axis_index("c"))
    chip, dev = _chip(me), _dev(me)
    RIDERS.reset()

    shapes0 = [c.shape, ssd_conv_w.shape, ffn_conv_w.shape]
    pack0 = _pack([c, ssd_conv_w, ffn_conv_w])
    shards = [w[n].astype(BF16) for n in _BIG]
    g0 = allgather8("gather_c_conv", pack0)
    c16 = jnp.pad(g0[:, :d // 128, :].reshape(8, d), ((0, 8), (0, 0)))
    by_chip = [_unpack(g0[2 * j], shapes0) for j in range(4)]
    conv_w_full = jnp.concatenate([p[1] for p in by_chip], axis=-1)
    fconv_w_full = jnp.concatenate([p[2] for p in by_chip], axis=-1)

    modp = ada_forward("ada_fwd", c16, ada_w)[:, :8]
    pack1 = _pack([modp])
    g1 = allgather8("gather_mod", pack1)
    modfull = jnp.concatenate([_unpack(g1[2 * j], [modp.shape])[0] for j in range(4)], axis=-1)
    mod = lax.dynamic_index_in_dim(modfull, dev, axis=1, keepdims=False) + ada_b
    modv = [[mod[i, q * d:(q + 1) * d][None] for q in range(6)] for i in range(DEPTH)]


    def weight(k, layer, got):
        full = lax.dynamic_update_slice(got, shards[k][layer][None], (chip, 0, 0))
        if k == 0:
            return _w_in_from_chips(full)
        if k == 2:
            return jnp.concatenate([full[j] for j in FFN_BLOCK_ORDER], axis=1)
        return full.reshape(-1, full.shape[2])

    def later(k, layer, *sources):
        made = []

        def get():
            if not made:
                got = [RIDERS.result(host)[pos] for host, pos in sources]
                made.append(weight(k, layer, got[0] if len(got) == 1 else jnp.concatenate(got, axis=1)))
            return made[0]
        return get

    cs3, sn3 = rope_tables(positions[0])
    eye4 = jnp.eye(4, dtype=F32)
    wts, sps = [], []
    for i in range(DEPTH):
        wts.append(dict(
            norm1_g=norm1_g[i][None], norm2_g=norm2_g[i][None], pool_scale=pool_scale[i][None],
            wbd=(eye4[:, None, :, None] * pool_w[i][:, :, None, :]).reshape(POOL_W, POOL_W),
            ffn_conv_w=_ffn_block_perm(fconv_w_full[i]), ffn_conv_b=_ffn_block_perm(ffn_conv_b[i])[None]))
        sps.append(dict(cw=conv_w_full[i], cb=ssd_conv_b[i][None], dtb=_pad_lanes(ssd_dt_bias[i]), alog=_pad_lanes(ssd_a_log[i]),
                        dsk=_pad_lanes(ssd_d[i]), ng=ssd_norm_g[i][None]))

    RIDERS.book("l0_norm1", gather_ride(0, [shards[0]]))
    RIDERS.book("l0_ssd", gather_ride(0, [shards[1], shards[3]]))
    half = shards[2].shape[1] // 2
    RIDERS.book("l0_attn0", gather_ride(0, [shards[2][:, :half]]))
    RIDERS.book("l0_attn1", gather_ride(0, [shards[2][:, half:]]))
    wts[0].update(w_in=later(0, 0, ("l0_norm1", 0)), w_out=later(1, 0, ("l0_ssd", 0)), ffn_down=later(3, 0, ("l0_ssd", 1)),
                  ffn_up=later(2, 0, ("l0_attn0", 0), ("l0_attn1", 0)))
    RIDERS.book("l0_attn2", gather_ride(1, [shards[0], shards[1]]))
    RIDERS.book("l0_up", gather_ride(1, [shards[3]]))
    RIDERS.book("l0_down", gather_ride(1, [shards[2]]))
    wts[1].update(w_in=later(0, 1, ("l0_attn2", 0)), w_out=later(1, 1, ("l0_attn2", 1)), ffn_up=later(2, 1, ("l0_down", 0)),
                  ffn_down=later(3, 1, ("l0_up", 0)))
    x1_, keep0 = _layer_forward(0, x, modv[0], wts[0], sps[0], cs3, sn3)
    xc, keep1 = _layer_forward(1, x1_, modv[1], wts[1], sps[1], cs3, sn3)
    keeps = [keep0, keep1]
    lossblk, dx, d_final = final_loss("final_loss", xc, loss_target, final_g[None])

    small_g, dmods = [None] * DEPTH, [None] * DEPTH
    part_sum, from_chips = [[None] * 4 for _ in range(DEPTH)], [[None] * 4 for _ in range(DEPTH)]

    def owner_sum(layer, ks, mine, theirs):
        for k, g, t in zip(ks, mine, theirs):
            part_sum[layer][k] = add_arrays(f"sum_cores{layer}_{_BIG[k]}", [g, t], BF16)

    dx, by_chip1, small_g[1], dmods[1] = _layer_backward(1, dx, keeps[1], modv[1], wts[1], sps[1], cs3, sn3)
    RIDERS.book("l0_ffn_b", to_owner_ride(1, by_chip1))

    def after_ffn_b():
        owner_sum(1, range(4), by_chip1, RIDERS.result("l0_ffn_b"))
        RIDERS.book("l0_up_bx", scatter_ride(1, [part_sum[1][2]]))
        RIDERS.book("l0_up_bw", scatter_ride(1, [part_sum[1][0], part_sum[1][1]]))
        RIDERS.book("l0_norm2_b", scatter_ride(1, [part_sum[1][3]]))

    early = []

    def after_wout_bw(g_wout, g_up, g_down):
        early.extend([g_wout, g_up, g_down])
        RIDERS.book("l0_ssd_b", to_owner_ride(0, early))

    def after_ssd_b():
        owner_sum(0, [1, 2, 3], early, RIDERS.result("l0_ssd_b"))
        for host, k in (("l0_attn0_b", 2), ("l0_attn1_b", 3), ("l0_attn2_b", 1)):
            RIDERS.book(host, scatter_ride(0, [part_sum[0][k]]))

    last = []

    def after_proj_bw(g_win):
        last.append(g_win)
        RIDERS.book("l0_proj_bx", to_owner_ride(0, last))

    def after_proj_bx():
        owner_sum(0, [0], last, RIDERS.result("l0_proj_bx"))
        RIDERS.book("l0_norm1_b", scatter_ride(0, [part_sum[0][0]]))

    hooks = dict(ffn_b=after_ffn_b, wout_bw=after_wout_bw, ssd_b=after_ssd_b, proj_bw=after_proj_bw, proj_bx=after_proj_bx)
    dx, _, small_g[0], dmods[0] = _layer_backward(0, dx, keeps[0], modv[0], wts[0], sps[0], cs3, sn3, after=hooks)
    from_chips[1][2], (from_chips[1][0], from_chips[1][1]) = RIDERS.result("l0_up_bx")[0], RIDERS.result("l0_up_bw")
    from_chips[1][3] = RIDERS.result("l0_norm2_b")[0]
    for host, k in (("l0_attn0_b", 2), ("l0_attn1_b", 3), ("l0_attn2_b", 1), ("l0_norm1_b", 0)):
        from_chips[0][k] = RIDERS.result(host)[0]
    mine = [sum_chips_mine(f"sum_chips_{n}", part_sum[0][k], from_chips[0][k], part_sum[1][k], from_chips[1][k])
            for k, n in enumerate(_BIG)]

    part = dict(ada_b=jnp.stack(dmods), final_g=d_final[0])
    for n in _SMALL:
        if n not in part:
            part[n] = jnp.stack([small_g[i][n] for i in range(DEPTH)])
    full_shapes = [part[n].shape for n in _SMALL] + [(1,)]
    pack_small = _pack([part[n] for n in _SMALL] + [lossblk[0, :1]])
    *theirs, gs = ride_alone("swap_r_gather_small", merge_rides([swap_ride(mine), allgather8_ride(pack_small)]))
    gs = _with_own(gs, pack_small)
    tot = _unpack(sum_slots("sum_small", gs, 8)[0], full_shapes)
    loss = tot[-1].reshape(())
    grads = {}
    small_tot = dict(zip(_SMALL, tot))
    dmod_all = gs[:, :DEPTH * 6 * d // 128, :].reshape(8, DEPTH, 6 * d)
    for n, ncol in _COL_SHARDED_SMALL.items():
        small_tot[n] = lax.dynamic_slice_in_dim(small_tot[n], chip * ncol, ncol, axis=2)
    grads.update(small_tot)

    ncol = ada_w.shape[2]
    dm = lax.dynamic_slice_in_dim(dmod_all, chip * ncol, ncol, axis=2).transpose(1, 0, 2)
    upd = {}
    g_ada, *upd["ada_w"] = ada_backward("ada_bwd", c16, jnp.pad(dm, ((0, 0), (0, 8), (0, 0))), ada_w, m["ada_w"], v["ada_w"])
    grads["ada_w"] = g_ada

    for n, a, g in zip(_BIG, mine, theirs):
        grads[n], *upd[n] = adamw_layers(f"adam_{n}", w[n], a, g, m[n], v[n])
    shapes_s = [w[n].shape for n in _SMALL]
    packed = [_pack([src[n] for n in _SMALL]) for src in (w, grads, m, v)]
    outs_s = [_unpack(o, shapes_s) for o in adamw("adam_small", *packed)]
    for q, n in enumerate(_SMALL):
        upd[n] = [outs_s[0][q], outs_s[1][q], outs_s[2][q]]

    return (loss, dx, *[grads[n] for n in _WEIGHTS], *[upd[n][0] for n in _WEIGHTS], *[upd[n][1] for n in _WEIGHTS],
            *[upd[n][2] for n in _WEIGHTS])


def ride_alone(name, ride):
    ni, no = len(ride.ins), len(ride.out_shapes)

    def body(*refs):
        ride.begin(refs[:ni], refs[ni:ni + no], refs[ni + no:])
        ride.end(refs[:ni], refs[ni:ni + no], refs[ni + no:])

    in_specs, out_specs, scratch = ride.specs()
    return list(pl.pallas_call(body, name=name, in_specs=in_specs, out_specs=out_specs, out_shape=ride.out_shapes,
                               scratch_shapes=scratch)(*ride.ins))


def mm(name, a, b, mode, out_dtype=F32, res=None, gate=None, tm=1408, tn=1536, tk=1408, into=None):
    ride = RIDERS.take(name)
    if mode == "nn":
        (m, k), n = a.shape, b.shape[1]
    elif mode == "nt":
        (m, k), n = a.shape, b.shape[0]
    else:
        (k, m), n = a.shape, b.shape[1]
    tm, tn, tk = _tile(m, tm), _tile(n, tn), _tile(k, tk)
    ni, nj, nk = m // tm, n // tn, k // tk
    a_spec = pl.BlockSpec((tk, tm), lambda i, j, q: (q, i)) if mode == "tn" else pl.BlockSpec((tm, tk), lambda i, j, q: (i, q))
    b_spec = pl.BlockSpec((tn, tk), lambda i, j, q: (j, q)) if mode == "nt" else pl.BlockSpec((tk, tn), lambda i, j, q: (q, j))
    o_spec = pl.BlockSpec((tm, tn), lambda i, j, q: (i, j))
    fused = res is not None
    lead = 0 if into is None else len(into[0]) - 2
    first = (0,) * lead + (slice(None), slice(None))
    ins, in_specs = [a, b], [a_spec, b_spec]
    out_shape, out_specs = [jax.ShapeDtypeStruct((m, n), out_dtype)], [o_spec]
    if fused:
        ins += [res, gate]
        in_specs += [o_spec, pl.BlockSpec((1, tn), lambda i, j, q: (0, j))]
        out_shape.append(jax.ShapeDtypeStruct((m, n), F32))
        out_specs.append(o_spec)
    if into is not None:
        shape, omap = into
        out_shape = [jax.ShapeDtypeStruct(shape, out_dtype)]
        out_specs = [pl.BlockSpec((1,) * lead + (tm, tn), lambda i, j, q: omap(i, j))]
    n_in, n_out = len(ins), len(out_shape)
    scratch = [pltpu.VMEM((tm, tn), F32)]
    if ride is not None:
        r_in, r_out, r_scr = ride.specs()
        ins, in_specs = ins + list(ride.ins), in_specs + r_in
        out_shape, out_specs = out_shape + list(ride.out_shapes), out_specs + r_out
        scratch = scratch + r_scr

    def body(*refs):
        a_ref, b_ref = refs[:2]
        o_ref = refs[len(ins)]
        acc = refs[len(ins) + len(out_shape)]
        i, j, q = pl.program_id(0), pl.program_id(1), pl.program_id(2)
        at = lambda x, y, z: jnp.logical_and(jnp.logical_and(i == x, j == y), q == z)
        r_refs = (refs[n_in:len(ins)], refs[len(ins) + n_out:len(ins) + len(out_shape)], refs[len(ins) + len(out_shape) + 1:])
        if ride is not None:
            ride.begin(*r_refs, at(0, 0, 0))

        @pl.when(q == 0)
        def _():
            acc[...] = jnp.zeros(acc.shape, F32)

        acc[...] += _mxu(a_ref[...], b_ref[...], mode)

        @pl.when(q == nk - 1)
        def _():
            o_ref[first] = acc[...].astype(o_ref.dtype)
            if fused:
                refs[len(ins) + 1][...] = refs[2][...] + refs[3][...] * acc[...]

        if ride is not None:
            ride.end(*r_refs, at(ni - 1, nj - 1, nk - 1))

    sem = ("arbitrary",) * 3 if ride is not None else ("parallel", "parallel", "arbitrary")
    out = pl.pallas_call(
        body, name=name, grid=(ni, nj, nk), in_specs=in_specs, out_specs=out_specs, out_shape=out_shape, scratch_shapes=scratch,
        compiler_params=pltpu.CompilerParams(dimension_semantics=sem, vmem_limit_bytes=VMEM_LIMIT_BYTES),
    )(*ins)
    if ride is not None:
        RIDERS.done[name] = list(out[n_out:])
    return tuple(out[:n_out]) if fused else out[0]


def add_arrays(name, arrs, out_dtype=F32):
    nb, r, c = arrs[0].shape
    t = _tile(r, 256, 8)
    (out,), _ = scan_fwd(name, _sum_fn, nb=nb, nchunk=r // t, t=t, rows=[Row(a, fb=lambda b: b) for a in arrs], vecs=[], carries=[],
                         outs=[out_row((nb, r, c), out_dtype, fb=lambda b: b)], save=False)
    return out


def _sum_chips_mine_fn(ci, b, carries, rows, vecs):
    mine_layer = lax.axis_index("c")
    chip = 2 * lax.axis_index("x") + lax.axis_index("y")
    tot = None
    for j in range(4):
        own = jnp.where(mine_layer == 0, rows[j], rows[8 + j])
        sent = jnp.where(mine_layer == 0, rows[4 + j], rows[12 + j])
        term = jnp.where(chip == j, own, sent).astype(F32)
        tot = term if tot is None else tot + term
    return [], [tot]


def sum_chips_mine(name, p0, q0, p1, q1):
    _, r, c = p0.shape
    t = _tile(r, 256, 8)
    rows = [Row(a, fb=(lambda b, j=j: j)) for a in (p0, q0, p1, q1) for j in range(4)]
    (out,), _ = scan_fwd(name, _sum_chips_mine_fn, nb=1, nchunk=r // t, t=t, rows=rows, vecs=[], carries=[],
                         outs=[out_row((1, r, c))], save=False)
    return out[0]


def _remote(src, dst, send_sems, recv_sems, k, to):
    return pltpu.make_async_remote_copy(src_ref=src, dst_ref=dst, send_sem=send_sems.at[k], recv_sem=recv_sems.at[k],
                                        device_id=to, device_id_type=MESH)


def gather_ride(layer, shards):
    na = len(shards)

    def rows(ref, c):
        h = ref.shape[0] // 2
        return ref.at[pl.ds(c * h, h)]

    def start(ins, outs, ss, rs, me):
        for k in range(na):
            for p, mask in enumerate(CHIP_PEERS):
                _remote(rows(ins[k].at[layer], me[2]), rows(outs[k].at[_chip(me)], me[2]), ss, rs, 6 * k + p, _flip(mask, me)).start()

    def finish(ins, outs, ss, rs, me):
        sibling = _flip(SIBLING[0], me)
        for k in range(na):
            for p, mask in enumerate(CHIP_PEERS):
                got = rows(outs[k].at[_chip(_flip(mask, me))], me[2])
                _remote(rows(ins[k].at[layer], me[2]), got, ss, rs, 6 * k + p, _flip(mask, me)).wait_recv()
                _remote(got, got, ss, rs, 6 * k + 3 + p, sibling).start()
        for k in range(na):
            for p, mask in enumerate(CHIP_PEERS):
                got = rows(outs[k].at[_chip(_flip(mask, me))], me[2])
                other = rows(outs[k].at[_chip(_flip(mask, me))], 1 - me[2])
                _remote(other, other, ss, rs, 6 * k + 3 + p, sibling).wait_recv()
                _remote(rows(ins[k].at[layer], me[2]), got, ss, rs, 6 * k + p, _flip(mask, me)).wait_send()
                _remote(got, got, ss, rs, 6 * k + 3 + p, sibling).wait_send()

    return Ride(list(shards), [jax.ShapeDtypeStruct((4,) + a.shape[1:], a.dtype) for a in shards], 6 * na, start, finish)


def scatter_ride(layer, parts):
    na = len(parts)

    def start(ins, outs, ss, rs, me):
        @pl.when(me[2] == layer)
        def _():
            for k in range(na):
                for p, mask in enumerate(CHIP_PEERS):
                    peer = _flip(mask, me)
                    _remote(ins[k].at[_chip(peer)], outs[k].at[_chip(me)], ss, rs, 3 * k + p, peer).start()

    def finish(ins, outs, ss, rs, me):
        @pl.when(me[2] == layer)
        def _():
            for k in range(na):
                for p, mask in enumerate(CHIP_PEERS):
                    peer = _flip(mask, me)
                    _remote(ins[k].at[_chip(peer)], outs[k].at[_chip(peer)], ss, rs, 3 * k + p, peer).wait_recv()
                    _remote(ins[k].at[_chip(peer)], outs[k].at[_chip(me)], ss, rs, 3 * k + p, peer).wait_send()

    return Ride(list(parts), [jax.ShapeDtypeStruct(a.shape, a.dtype) for a in parts], 3 * na, start, finish)


def to_owner_ride(layer, arrays):
    na = len(arrays)

    def start(ins, outs, ss, rs, me):
        @pl.when(me[2] != layer)
        def _():
            for k in range(na):
                _remote(ins[k], outs[k], ss, rs, k, _flip(SIBLING[0], me)).start()

    def finish(ins, outs, ss, rs, me):
        for k in range(na):
            cp = _remote(ins[k], outs[k], ss, rs, k, _flip(SIBLING[0], me))
            pl.when(me[2] != layer)(cp.wait_send)
            pl.when(me[2] == layer)(cp.wait_recv)

    return Ride(list(arrays), [jax.ShapeDtypeStruct(a.shape, a.dtype) for a in arrays], na, start, finish)


def allgather8_ride(a):
    def start(ins, outs, ss, rs, me):
        for p, mask in enumerate(ALL_PEERS):
            _remote(ins[0], outs[0].at[_dev(me)], ss, rs, p, _flip(mask, me)).start()

    def finish(ins, outs, ss, rs, me):
        for p, mask in enumerate(ALL_PEERS):
            peer = _flip(mask, me)
            _remote(ins[0], outs[0].at[_dev(peer)], ss, rs, p, peer).wait_recv()
            _remote(ins[0], outs[0].at[_dev(me)], ss, rs, p, peer).wait_send()

    return Ride([a], [jax.ShapeDtypeStruct((8,) + a.shape, a.dtype)], len(ALL_PEERS), start, finish)


def _with_own(gathered, own):
    me = _dev((lax.axis_index("x"), lax.axis_index("y"), lax.axis_index("c")))
    return jnp.where((jnp.arange(8) == me)[:, None, None], own[None], gathered)


def swap_ride(arrays):
    na = len(arrays)

    def start(ins, outs, ss, rs, me):
        for k in range(na):
            _remote(ins[k], outs[k], ss, rs, k, _flip(SIBLING[0], me)).start()

    def finish(ins, outs, ss, rs, me):
        for k in range(na):
            cp = _remote(ins[k], outs[k], ss, rs, k, _flip(SIBLING[0], me))
            cp.wait_recv()
            cp.wait_send()

    return Ride(list(arrays), [jax.ShapeDtypeStruct(a.shape, a.dtype) for a in arrays], na, start, finish)


class _Shifted:
    def __init__(self, ref, offset):
        self.ref, self.offset = ref, offset

    @property
    def at(self):
        return self

    def __getitem__(self, k):
        return self.ref.at[self.offset + k]


def merge_rides(rides):
    def spans(counts):
        out, o = [], 0
        for n in counts:
            out.append((o, o + n))
            o += n
        return out

    si, so = spans([len(r.ins) for r in rides]), spans([len(r.out_shapes) for r in rides])
    ss_ = spans([r.nsem for r in rides])

    def each(method):
        def run(ins, outs, ss, rs, me):
            for r, (i0, i1), (o0, o1), (s0, _) in zip(rides, si, so, ss_):
                getattr(r, method)(ins[i0:i1], outs[o0:o1], _Shifted(ss, s0), _Shifted(rs, s0), me)
        return run

    return Ride([a for r in rides for a in r.ins], [s for r in rides for s in r.out_shapes], sum(r.nsem for r in rides),
                each("start"), each("finish"))


def _w_in_from_chips(a):
    c2 = a[2]
    pad = jnp.zeros((c2.shape[0], IN_WP - IN_W), c2.dtype)
    return jnp.concatenate([a[0], a[1], c2[:, :252], c2[:, 260:516], c2[:, 252:260], pad, c2[:, 516:], a[3]], axis=1)
```

```python
import functools

import numpy as np
import jax
import jax.numpy as jnp
from jax import lax
from jax.experimental import pallas as pl
from jax.experimental.pallas import tpu as pltpu

F32 = jnp.float32
BF16 = jnp.bfloat16
MESH = pl.DeviceIdType.MESH

D_MODEL = 1024
DEPTH = 2
SSD_INNER = 512
POOL_W = 256
POOL_WINDOWS = (2, 4, 8, 16)
ATT_W = 256
ATT_HEADS = 4
ATT_HEAD_DIM = 64
ATT_PATTERNS = ((128, 1), (512, 4), (2048, 16))
ATT_BLOCK = 128
ROT_DIM = 16
ROPE_THETA = 500000.0
IN_W = 2568
IN_WP = 2688
IN_MAIN = 1920
FFN_DIM = 2816
NORM_EPS = 1e-6
ADAM_LR, ADAM_B1, ADAM_B2, ADAM_EPS, ADAM_WD, ADAM_STEP = 0.001, 0.9, 0.999, 1e-08, 0.01, 10

VMEM_LIMIT_BYTES = 56 * 1024 * 1024
NEG = -1e30


def _mxu(a, b, mode):
    dims = {"nn": ((1,), (0,)), "nt": ((1,), (1,)), "tn": ((0,), (0,))}[mode]
    return lax.dot_general(a.astype(BF16), b.astype(BF16), (dims, ((), ())), preferred_element_type=F32)


@functools.partial(jax.custom_vjp, nondiff_argnums=(2,))
def _bdot(a, b, mode):
    return _mxu(a, b, mode)


def _bdot_fwd(a, b, mode):
    return _mxu(a, b, mode), (a, b)


def _bdot_bwd(mode, res, g):
    a, b = res
    if mode == "nn":
        return _mxu(g, b, "nt"), _mxu(a, g, "tn")
    if mode == "nt":
        return _mxu(g, b, "nn"), _mxu(g, a, "tn")
    return _mxu(b, g, "nt"), _mxu(a, g, "nn")


_bdot.defvjp(_bdot_fwd, _bdot_bwd)


def _iota(shape, dim):
    return lax.broadcasted_iota(jnp.int32, shape, dim)


def _make_shift(h):
    @functools.partial(jax.custom_vjp, nondiff_argnums=(2,))
    def shift(halo, cur, k):
        if k == 0:
            return cur
        full = jnp.concatenate([halo, cur], axis=0)
        return pltpu.roll(full, k, 0)[h:]

    def fwd(halo, cur, k):
        return shift(halo, cur, k), None

    def bwd(k, _, g):
        t, w = g.shape
        if k == 0:
            return jnp.zeros((h, w), F32), g
        d_cur = jnp.where(_iota((t, w), 0) < t - k, pltpu.roll(g, t - k, 0), 0.0)
        top = g[:h]
        d_halo = jnp.where(_iota((h, w), 0) >= h - k, pltpu.roll(top, h - k, 0) if k < h else top, 0.0)
        return d_halo, d_cur

    shift.defvjp(fwd, bwd)
    return shift


_shift8 = _make_shift(8)
_shift16 = _make_shift(16)


def _make_tail(h):
    @jax.custom_vjp
    def tail(x):
        return x[x.shape[0] - h:]

    def fwd(x):
        return tail(x), x.shape[0]

    def bwd(t, g):
        return (jnp.concatenate([jnp.zeros((t - h, g.shape[1]), F32), g], axis=0),)

    tail.defvjp(fwd, bwd)
    return tail


_tail8 = _make_tail(8)
_tail16 = _make_tail(16)


@jax.custom_vjp
def _cumsum_rows(x):
    t = x.shape[0]
    row, s = _iota(x.shape, 0), 1
    while s < t:
        x = x + jnp.where(row >= s, pltpu.roll(x, s, 0), 0.0)
        s *= 2
    return x


def _cumsum_rows_fwd(x):
    return _cumsum_rows(x), None


def _cumsum_rows_bwd(_, g):
    t = g.shape[0]
    row, s = _iota(g.shape, 0), 1
    while s < t:
        g = g + jnp.where(row < t - s, pltpu.roll(g, t - s, 0), 0.0)
        s *= 2
    return (g,)


_cumsum_rows.defvjp(_cumsum_rows_fwd, _cumsum_rows_bwd)


@jax.custom_vjp
def _rot_pairs(t):
    e = _iota(t.shape, 1) % ATT_HEAD_DIM
    n = t.shape[1]
    return jnp.where(e < 8, -pltpu.roll(t, n - 8, 1), jnp.where(e < 16, pltpu.roll(t, 8, 1), 0.0))


def _rot_pairs_fwd(t):
    return _rot_pairs(t), None


def _rot_pairs_bwd(_, g):
    e = _iota(g.shape, 1) % ATT_HEAD_DIM
    n = g.shape[1]
    return (pltpu.roll(jnp.where(e < 8, -g, 0.0), 8, 1) + pltpu.roll(jnp.where(jnp.logical_and(e >= 8, e < 16), g, 0.0), n - 8, 1),)


_rot_pairs.defvjp(_rot_pairs_fwd, _rot_pairs_bwd)


def _make_thirds():
    @jax.custom_vjp
    def thirds(x):
        w = x.shape[1] // 3
        return x[:, :w], x[:, w:2 * w], x[:, 2 * w:]

    def fwd(x):
        return thirds(x), None

    def bwd(_, g):
        return (jnp.concatenate(g, axis=1),)

    thirds.defvjp(fwd, bwd)
    return thirds


_thirds = _make_thirds()


def _rowk(w, k):
    return jnp.sum(jnp.where(_iota(w.shape, 0) == k, w, 0.0), axis=0, keepdims=True)


def _silu(x):
    return x * (0.5 * jnp.tanh(0.5 * x) + 0.5)


def _softplus(x):
    return jnp.maximum(x, 0.0) + jnp.log(1.0 + jnp.exp(-jnp.abs(x)))


def _tile(dim, target, unit=128):
    if dim <= target:
        return dim
    best = None
    for t in range(unit, target + 1, unit):
        if dim % t == 0:
            best = t
    assert best is not None, (dim, target)
    return best


class Ride:
    def __init__(self, ins, out_shapes, nsem, start, finish, middle=None):
        self.ins, self.out_shapes, self.nsem, self.start, self.finish = ins, out_shapes, nsem, start, finish
        self.middle, self.relayed = middle, False

    def specs(self):
        hbm = pl.BlockSpec(memory_space=pl.ANY)
        return [hbm] * len(self.ins), [hbm] * len(self.out_shapes), [pltpu.SemaphoreType.DMA((self.nsem,))] * 2

    def begin(self, in_refs, out_refs, sems, cond=None):
        me = (lax.axis_index("x"), lax.axis_index("y"), lax.axis_index("c"))
        go = lambda: self.start(in_refs, out_refs, sems[0], sems[1], me)
        go() if cond is None else pl.when(cond)(go)

    def relay(self, in_refs, out_refs, sems, step, steps):
        if self.middle is None:
            return
        me = (lax.axis_index("x"), lax.axis_index("y"), lax.axis_index("c"))
        self.relayed = True
        pl.when(step == (3 * steps) // 4)(lambda: self.middle(in_refs, out_refs, sems[0], sems[1], me))

    def end(self, in_refs, out_refs, sems, cond=None):
        me = (lax.axis_index("x"), lax.axis_index("y"), lax.axis_index("c"))

        def go():
            if self.middle is not None and not self.relayed:
                self.middle(in_refs, out_refs, sems[0], sems[1], me)
            self.finish(in_refs, out_refs, sems[0], sems[1], me)
        go() if cond is None else pl.when(cond)(go)


class _Riders:
    def reset(self):
        self.booked, self.done = {}, {}

    def book(self, host, ride):
        assert host not in self.booked, host
        self.booked[host] = ride

    def take(self, host):
        return self.booked.pop(host, None)

    def result(self, host):
        return self.done[host]


RIDERS = _Riders()
RIDERS.reset()


class Row:
    def __init__(self, arr, w=None, fb=None, fc=None, diff=True, slot=False, dcols=None, dfc=None, ddtype=F32, view=None):
        self.ddtype = ddtype
        self.view = view
        self.arr = arr
        self.w = arr.shape[2] if w is None else w
        self.fb = (lambda b: 0) if fb is None else fb
        self.fc = (lambda b: 0) if fc is None else fc
        self.diff = diff
        self.slot = slot
        self.dcols = dcols
        self.dfc = dfc


class Vec:
    def __init__(self, arr, w=None, fc=None, diff=True):
        self.arr = arr
        self.w = arr.shape[1] if w is None else w
        self.fc = fc
        self.diff = diff


def _row_spec(r, t, nchunk, reverse):
    shape = (1, t, r.w) if r.view is None else (1, t // r.view, r.view * r.w)
    if reverse:
        return pl.BlockSpec(shape, lambda b, i, r=r: (r.fb(b), nchunk - 1 - i, r.fc(b)))
    return pl.BlockSpec(shape, lambda b, i, r=r: (r.fb(b), i, r.fc(b)))


def _load_row(ref, r, t, scr):
    if r.view is None:
        return ref[0]
    d, w = r.view, r.w
    for q in range(d):
        for j in range(w // 128):
            scr[j, pl.ds(q, t // d, stride=d), :] = ref[0, :, q * w + 128 * j:q * w + 128 * (j + 1)].astype(F32)
    return jnp.concatenate([scr[j] for j in range(w // 128)], axis=1)


def _store_row(ref, r, t, scr, val):
    if r.view is None:
        ref[0] = val.astype(ref.dtype)
        return
    d, w = r.view, r.w
    for j in range(w // 128):
        scr[j] = val[:, 128 * j:128 * (j + 1)]
    for q in range(d):
        for j in range(w // 128):
            ref[0, :, q * w + 128 * j:q * w + 128 * (j + 1)] = scr[j, pl.ds(q, t // d, stride=d), :].astype(ref.dtype)


def _view_scratch(specs, t):
    ws = [r.w for r in specs if r.view is not None]
    return [pltpu.VMEM((max(ws) // 128, t, 128), F32)] if ws else []


def _vec_spec(v):
    if v.fc is None:
        return pl.BlockSpec(v.arr.shape, lambda b, i: (0, 0))
    return pl.BlockSpec((v.arr.shape[0], v.w), lambda b, i, v=v: (0, v.fc(b)))


def _cparams():
    return pltpu.CompilerParams(dimension_semantics=("arbitrary", "arbitrary"), vmem_limit_bytes=VMEM_LIMIT_BYTES)


def scan_fwd(name, fn, *, nb, nchunk, t, rows, vecs, carries, outs, save):
    nr, nv, nc, no = len(rows), len(vecs), len(carries), len(outs)
    ns = nc if save else 0
    ride = RIDERS.take(name)
    r_in, r_out, r_scr = ride.specs() if ride else ([], [], [])

    def body(*refs):
        p = 0
        row_refs = refs[p:p + nr]; p += nr
        vec_refs = refs[p:p + nv]; p += nv
        ride_in = refs[p:p + len(r_in)]; p += len(r_in)
        out_refs = refs[p:p + no]; p += no
        save_refs = refs[p:p + ns]; p += ns
        ride_out = refs[p:p + len(r_out)]; p += len(r_out)
        car = refs[p:p + nc]; p += nc
        scr = refs[p] if stage else None
        sems = refs[p + len(stage):]
        b, i = pl.program_id(0), pl.program_id(1)
        if ride:
            ride.begin(ride_in, ride_out, sems, jnp.logical_and(b == 0, i == 0))
            ride.relay(ride_in, ride_out, sems, b * nchunk + i, nb * nchunk)
        if nc:
            @pl.when(i == 0)
            def _():
                for c_ref in car:
                    c_ref[...] = jnp.zeros(c_ref.shape, F32)
        cin = [c_ref[...] for c_ref in car]
        if save:
            for s_ref, cv in zip(save_refs, cin):
                s_ref[0, 0] = cv
        new_c, o = fn(i, b, cin, [_load_row(ref, r, t, scr) for ref, r in zip(row_refs, rows)], [v[...] for v in vec_refs])
        for c_ref, cv in zip(car, new_c):
            c_ref[...] = cv
        for o_ref, spec, ov in zip(out_refs, outs, o):
            _store_row(o_ref, spec, t, scr, ov)
        if ride:
            ride.end(ride_in, ride_out, sems, jnp.logical_and(b == nb - 1, i == nchunk - 1))

    stage = _view_scratch(list(rows) + list(outs), t)
    out_shape = [o.arr for o in outs]
    out_specs = [_row_spec(o, t, nchunk, False) for o in outs]
    if save:
        for cs in carries:
            out_shape.append(jax.ShapeDtypeStruct((nb, nchunk) + tuple(cs), F32))
            out_specs.append(pl.BlockSpec((1, 1) + tuple(cs), lambda b, i: (b, i, 0, 0)))
    res = pl.pallas_call(
        body, name=name, grid=(nb, nchunk),
        in_specs=[_row_spec(r, t, nchunk, False) for r in rows] + [_vec_spec(v) for v in vecs] + r_in,
        out_specs=out_specs + r_out, out_shape=out_shape + (list(ride.out_shapes) if ride else []),
        scratch_shapes=[pltpu.VMEM(tuple(cs), F32) for cs in carries] + stage + r_scr,
        compiler_params=_cparams(),
    )(*[r.arr for r in rows], *[v.arr for v in vecs], *(ride.ins if ride else []))
    if ride:
        RIDERS.done[name] = list(res[no + ns:])
    return list(res[:no]), list(res[no:no + ns])


def scan_bwd(name, fn, *, nb, nchunk, t, rows, vecs, carries, saved, douts, adds=None):
    adds = adds or {}
    nr, nv, nc, no = len(rows), len(vecs), len(carries), len(douts)
    dri = [k for k, r in enumerate(rows) if r.diff]
    dvi = [k for k, v in enumerate(vecs) if v.diff]
    add_keys = sorted(adds)
    na = len(add_keys)
    ride = RIDERS.take(name)
    r_in, r_out, r_scr = ride.specs() if ride else ([], [], [])

    def body(*refs):
        p = 0
        row_refs = refs[p:p + nr]; p += nr
        vec_refs = refs[p:p + nv]; p += nv
        save_refs = refs[p:p + nc]; p += nc
        dout_refs = refs[p:p + no]; p += no
        add_refs = refs[p:p + na]; p += na
        ride_in = refs[p:p + len(r_in)]; p += len(r_in)
        drow_refs = refs[p:p + len(dri)]; p += len(dri)
        dvec_refs = refs[p:p + len(dvi)]; p += len(dvi)
        ride_out = refs[p:p + len(r_out)]; p += len(r_out)
        dcar = refs[p:p + nc]; p += nc
        scr = refs[p] if stage else None
        sems = refs[p + len(stage):]
        b, ir = pl.program_id(0), pl.program_id(1)
        ci = nchunk - 1 - ir
        if ride:
            ride.begin(ride_in, ride_out, sems, jnp.logical_and(b == 0, ir == 0))
        if nc:
            @pl.when(ir == 0)
            def _():
                for c_ref in dcar:
                    c_ref[...] = jnp.zeros(c_ref.shape, F32)
        rows_v = [_load_row(ref, r, t, scr) for ref, r in zip(row_refs, rows)]
        vecs_v = [v[...] for v in vec_refs]
        cin = [s[0, 0] for s in save_refs]
        dc = [c_ref[...] for c_ref in dcar]
        dout_v = [_load_row(ref, r, t, scr).astype(F32) for ref, r in zip(dout_refs, douts)]

        def f(cs, dr, dv):
            rr, vv = list(rows_v), list(vecs_v)
            for k, idx in enumerate(dri):
                rr[idx] = dr[k]
            for k, idx in enumerate(dvi):
                vv[idx] = dv[k]
            return fn(ci, b, cs, rr, vv)

        _, vjp = jax.vjp(f, cin, [rows_v[k].astype(F32) for k in dri], [vecs_v[k].astype(F32) for k in dvi])
        dcin, drows, dvecs = vjp((dc, dout_v))
        for c_ref, cv in zip(dcar, dcin):
            c_ref[...] = cv
        for k, (o_ref, ov) in enumerate(zip(drow_refs, drows)):
            if dri[k] in adds:
                ov = ov + add_refs[add_keys.index(dri[k])][0].astype(F32)
            _store_row(o_ref, rows[dri[k]], t, scr, ov)
        for k, (o_ref, ov) in enumerate(zip(dvec_refs, dvecs)):
            first = (ir == 0) if vecs[dvi[k]].fc is not None else jnp.logical_and(ir == 0, b == 0)

            @pl.when(first)
            def _(o_ref=o_ref, ov=ov):
                o_ref[...] = ov

            @pl.when(jnp.logical_not(first))
            def _(o_ref=o_ref, ov=ov):
                o_ref[...] += ov

        if ride:
            ride.end(ride_in, ride_out, sems, jnp.logical_and(b == nb - 1, ir == nchunk - 1))

    stage = _view_scratch(list(rows) + list(douts), t)
    in_specs = ([_row_spec(r, t, nchunk, True) for r in rows] + [_vec_spec(v) for v in vecs]
                + [pl.BlockSpec((1, 1) + tuple(cs), lambda b, i: (b, nchunk - 1 - i, 0, 0)) for cs in carries]
                + [_row_spec(d, t, nchunk, True) for d in douts]
                + [_row_spec(adds[k], t, nchunk, True) for k in add_keys] + r_in)
    out_shape, out_specs = [], []
    for k in dri:
        r = rows[k]
        if r.slot:
            out_shape.append(jax.ShapeDtypeStruct((nb, r.arr.shape[1], r.w), r.ddtype))
            out_specs.append(pl.BlockSpec((1, t, r.w), lambda b, i: (b, nchunk - 1 - i, 0)))
        elif r.dcols is not None:
            out_shape.append(jax.ShapeDtypeStruct((r.arr.shape[0], r.arr.shape[1], r.dcols), r.ddtype))
            out_specs.append(pl.BlockSpec((1, t, r.w), lambda b, i, r=r: (r.fb(b), nchunk - 1 - i, r.dfc(b))))
        else:
            out_shape.append(jax.ShapeDtypeStruct(r.arr.shape, r.ddtype))
            out_specs.append(_row_spec(r, t, nchunk, True))
    for k in dvi:
        out_shape.append(jax.ShapeDtypeStruct(vecs[k].arr.shape, F32))
        out_specs.append(_vec_spec(vecs[k]))
    nd = len(dri) + len(dvi)
    res = pl.pallas_call(
        body, name=name, grid=(nb, nchunk), in_specs=in_specs, out_specs=out_specs + r_out,
        out_shape=out_shape + (list(ride.out_shapes) if ride else []),
        scratch_shapes=[pltpu.VMEM(tuple(cs), F32) for cs in carries] + stage + r_scr,
        compiler_params=_cparams(),
    )(*[r.arr for r in rows], *[v.arr for v in vecs], *saved, *[d.arr for d in douts], *[adds[k].arr for k in add_keys],
      *(ride.ins if ride else []))
    if ride:
        RIDERS.done[name] = list(res[nd:])
    return list(res[:len(dri)]), list(res[len(dri):nd])


def out_row(shape, dtype=F32, w=None, fb=None, fc=None):
    return Row(jax.ShapeDtypeStruct(shape, dtype), w, fb, fc)


def _conv(shift, halo, cur, w, bias, taps):
    y = bias
    for k in range(taps):
        y = y + _rowk(w, k) * shift(halo, cur, taps - 1 - k)
    return y


def _ssd_fn(ci, b, carries, rows, vecs):
    cx, cb_, cc, ht = carries
    z, xr, br, cr, dtr = rows
    cwx, cbx, cwb, cbb, cwc, cbc, dtb, alog, dsk, ng = vecs
    t = z.shape[0]
    xs = _silu(_conv(_shift8, cx, xr, cwx, cbx, 4))
    bm = _silu(_conv(_shift8, cb_, br, cwb, cbb, 4))
    cm = _silu(_conv(_shift8, cc, cr, cwc, cbc, 4))
    dt = _softplus(dtr + dtb)
    acol = _cumsum_rows(dt * (-jnp.exp(alog)))
    arow = acol.T
    r, c = _iota((t, t), 0), _iota((t, t), 1)
    causal = r >= c
    cbm = _bdot(cm, bm, "nt")
    lane, sub = _iota(acol.shape, 1), _iota(arow.shape, 0)
    colh = _iota(xs.shape, 1) // 64
    a, dtx, dx, acs = jnp.zeros(xs.shape, F32), jnp.zeros(xs.shape, F32), jnp.zeros((1, xs.shape[1]), F32), []
    for j in range(4):
        h = 4 * b + j
        ac = jnp.sum(jnp.where(lane == h, acol, 0.0), axis=1, keepdims=True)
        acs.append(ac)
        a = jnp.where(colh == j, ac, a)
        dtx = jnp.where(colh == j, jnp.sum(jnp.where(lane == h, dt, 0.0), axis=1, keepdims=True), dtx)
        dx = jnp.where(_iota(dx.shape, 1) // 64 == j, jnp.sum(jnp.where(_iota(dsk.shape, 1) == h, dsk, 0.0), axis=1, keepdims=True), dx)
    atot = jnp.sum(jnp.where(_iota(a.shape, 0) == t - 1, a, 0.0), axis=0, keepdims=True)
    x = xs * dtx
    ydiag = jnp.zeros(x.shape, F32)
    for j in range(4):
        ar = jnp.sum(jnp.where(sub == 4 * b + j, arow, 0.0), axis=0, keepdims=True)
        lmat = jnp.exp(jnp.where(causal, acs[j] - ar, NEG))
        ydiag = ydiag + _bdot(cbm * lmat, jnp.where(colh == j, x, 0.0), "nn")
    yoff = _bdot(cm, ht, "nn") * jnp.exp(a)
    ht_new = ht * jnp.exp(atot) + _bdot(bm, x * jnp.exp(atot - a), "tn")
    y = ydiag + yoff + dx * xs
    yz = y * _silu(z)
    yn = yz * lax.rsqrt(jnp.mean(yz * yz, axis=-1, keepdims=True) + NORM_EPS) * ng
    return [_tail8(xr), _tail8(br), _tail8(cr), ht_new], [yn]


_SSD_T = 256
_SSD_CARRIES = [(8, 256), (8, 128), (8, 128), (128, 256)]


def _ssd_io(proj3, p):
    own = lambda b: b
    rows = [Row(proj3, 256, fc=own, dcols=512, dfc=own, ddtype=BF16),
            Row(proj3, 256, fc=lambda b: 2 + b, dcols=512, dfc=own, ddtype=BF16),
            Row(proj3, 128, fc=lambda b: 8 + b, dcols=256, dfc=own, ddtype=BF16),
            Row(proj3, 128, fc=lambda b: 10 + b, dcols=256, dfc=own, ddtype=BF16),
            Row(proj3, 128, fc=lambda b: 14, slot=True)]
    vecs = [Vec(p["cw"], 256, lambda b: b), Vec(p["cb"], 256, lambda b: b),
            Vec(p["cw"], 128, lambda b: 4 + b), Vec(p["cb"], 128, lambda b: 4 + b),
            Vec(p["cw"], 128, lambda b: 6 + b), Vec(p["cb"], 128, lambda b: 6 + b),
            Vec(p["dtb"]), Vec(p["alog"]), Vec(p["dsk"]), Vec(p["ng"], 256, lambda b: b)]
    return rows, vecs


def ssd_forward(name, proj3, p):
    rows, vecs = _ssd_io(proj3, p)
    s = proj3.shape[1]
    (y,), saved = scan_fwd(name, _ssd_fn, nb=2, nchunk=s // _SSD_T, t=_SSD_T, rows=rows, vecs=vecs,
                           carries=_SSD_CARRIES, outs=[out_row((1, s, SSD_INNER), BF16, 256, fc=lambda b: b)], save=True)
    return y, saved


def ssd_backward(name, proj3, p, saved, dmix3):
    rows, vecs = _ssd_io(proj3, p)
    s = proj3.shape[1]
    drows, dvecs = scan_bwd(name, _ssd_fn, nb=2, nchunk=s // _SSD_T, t=_SSD_T, rows=rows, vecs=vecs,
                            carries=_SSD_CARRIES, saved=saved, douts=[Row(dmix3, 256, fc=lambda b: b)])
    return drows, dvecs


def _pool_fn(ci, b, carries, rows, vecs):
    (cu,) = carries
    (u,) = rows
    wbd, scale = vecs
    t = u.shape[0]
    pos = ci * t + _iota(u.shape, 0)
    grp = _iota(u.shape, 1) // 64
    acc, pooled, k = u, jnp.zeros(u.shape, F32), 1
    for gi, w in enumerate(POOL_WINDOWS):
        while k < w:
            acc = acc + _shift16(cu, u, k)
            k += 1
        pooled = jnp.where(grp == gi, acc / jnp.minimum(pos + 1, w).astype(F32), pooled)
    y = _bdot(pooled - u, wbd, "nn") * scale
    return [_tail16(u)], [y]


_POOL_T = 256


def _pool_io(proj3, wbd, scale):
    return [Row(proj3, 256, fc=lambda b: 6, dcols=256, dfc=lambda b: 0, ddtype=BF16)], [Vec(wbd), Vec(scale)]


def pool_forward(name, proj3, wbd, scale):
    rows, vecs = _pool_io(proj3, wbd, scale)
    s = proj3.shape[1]
    (y,), saved = scan_fwd(name, _pool_fn, nb=1, nchunk=s // _POOL_T, t=_POOL_T, rows=rows, vecs=vecs,
                           carries=[(16, 256)], outs=[out_row((1, s, POOL_W), BF16)], save=True)
    return y, saved


def pool_backward(name, proj3, wbd, scale, saved, dmix3):
    rows, vecs = _pool_io(proj3, wbd, scale)
    s = proj3.shape[1]
    return scan_bwd(name, _pool_fn, nb=1, nchunk=s // _POOL_T, t=_POOL_T, rows=rows, vecs=vecs,
                    carries=[(16, 256)], saved=saved, douts=[Row(dmix3, 256, fc=lambda b: 2)])


def _attn_fn(ci, b, carries, rows, vecs):
    kp, vp = carries
    qr, kr, v = _thirds(rows[0])
    scale = ATT_HEAD_DIM ** -0.5
    q = qr
    n = q.shape[0]
    r, c = _iota((n, n), 0), _iota((n, n), 1)
    prev_ok, cur_ok = jnp.logical_and(c >= r, ci > 0), r >= c
    head = _iota(q.shape, 1) // ATT_HEAD_DIM
    o, lse = jnp.zeros(q.shape, F32), jnp.zeros(q.shape, F32)
    for h in range(ATT_HEADS):
        mine = head == h
        qh = jnp.where(mine, qr, 0.0)
        sp = jnp.where(prev_ok, _bdot(qh, kp, "nt") * scale, NEG)
        sc = jnp.where(cur_ok, _bdot(qh, kr, "nt") * scale, NEG)
        m = lax.stop_gradient(jnp.maximum(jnp.max(sp, axis=1, keepdims=True), jnp.max(sc, axis=1, keepdims=True)))
        pp, pc = jnp.exp(sp - m), jnp.exp(sc - m)
        l = jnp.sum(pp, axis=1, keepdims=True) + jnp.sum(pc, axis=1, keepdims=True)
        o = jnp.where(mine, (_bdot(pp, vp, "nn") + _bdot(pc, v, "nn")) / l, o)
        lse = jnp.where(mine, m + jnp.log(l), lse)
    return [kr, v], [o, lse]


_ATT_CARRIES = [(ATT_BLOCK, ATT_W), (ATT_BLOCK, ATT_W)]


def attn_forward(name, pv, d):
    l = pv.shape[1]
    own = lambda b: b
    outs = [out_row((1, l, d * ATT_W), F32, ATT_W, fc=own) for _ in range(2)]
    (o, lse), saved = scan_fwd(name, _attn_fn, nb=d, nchunk=l // ATT_BLOCK, t=ATT_BLOCK, rows=[Row(pv, 3 * ATT_W, fc=own)],
                               vecs=[], carries=_ATT_CARRIES, outs=outs, save=True)
    return o, lse, saved


def attn_backward(name, pv, d, saved, do, dlse):
    l = pv.shape[1]
    own = lambda b: b
    (dpv,), _ = scan_bwd(name, _attn_fn, nb=d, nchunk=l // ATT_BLOCK, t=ATT_BLOCK, rows=[Row(pv, 3 * ATT_W, fc=own)], vecs=[],
                         carries=_ATT_CARRIES, saved=saved, douts=[Row(do, ATT_W, fc=own), Row(dlse, ATT_W, fc=own)])
    return dpv


def _rope_fn(ci, b, carries, rows, vecs):
    x, cs, sn = rows
    return [], [x * cs + _rot_pairs(x) * sn]


def _rope3_fn(ci, b, carries, rows, vecs):
    _, (y,) = _rope_fn(ci, b, carries, rows, vecs)
    return [], [y, y, y]


def _by_residue(a_or_shape, w, d):
    if isinstance(a_or_shape, tuple):
        _, s, _ = a_or_shape
        return Row(jax.ShapeDtypeStruct((1, s // d, d * w), F32), w, view=None if d == 1 else d)
    return Row(a_or_shape, w, view=None if d == 1 else d)


def rope_forward(name, qkv3, cs3, sn3):
    s, w = qkv3.shape[1], qkv3.shape[2]
    ys, _ = scan_fwd(name, _rope3_fn, nb=1, nchunk=s // _ROW_T, t=_ROW_T, vecs=[], carries=[], save=False,
                     rows=[Row(qkv3), Row(cs3, diff=False), Row(sn3, diff=False)],
                     outs=[_by_residue(qkv3.shape, w, d) for _, d in ATT_PATTERNS])
    return ys


def rope_backward(name, qkv3, cs3, sn3, dys):
    s, w = qkv3.shape[1], qkv3.shape[2]
    (dx,), _ = scan_bwd(name, _rope3_fn, nb=1, nchunk=s // _ROW_T, t=_ROW_T, vecs=[], carries=[], saved=[],
                        rows=[Row(qkv3, ddtype=BF16), Row(cs3, diff=False), Row(sn3, diff=False)],
                        douts=[_by_residue(a, w, d) for a, (_, d) in zip(dys, ATT_PATTERNS)])
    return dx


def _merge_fn(ci, b, carries, rows, vecs):
    o1, o2, o3, l1, l2, l3 = rows
    mx = lax.stop_gradient(jnp.maximum(l1, jnp.maximum(l2, l3)))
    e1, e2, e3 = jnp.exp(l1 - mx), jnp.exp(l2 - mx), jnp.exp(l3 - mx)
    return [], [(e1 * o1 + e2 * o2 + e3 * o3) / (e1 + e2 + e3)]


_ROW_T = 512


def _merge_rows(os_, ls_):
    ds = [d for _, d in ATT_PATTERNS]
    return [_by_residue(a, ATT_W, d) for a, d in zip(os_, ds)] + [_by_residue(a, ATT_W, d) for a, d in zip(ls_, ds)]


def merge_forward(name, os_, ls_, s):
    (y,), _ = scan_fwd(name, _merge_fn, nb=1, nchunk=s // _ROW_T, t=_ROW_T, rows=_merge_rows(os_, ls_), vecs=[],
                       carries=[], outs=[out_row((1, s, ATT_W), BF16)], save=False)
    return y


def merge_backward(name, os_, ls_, dmix3):
    s = dmix3.shape[1]
    drows, _ = scan_bwd(name, _merge_fn, nb=1, nchunk=s // _ROW_T, t=_ROW_T, rows=_merge_rows(os_, ls_), vecs=[],
                        carries=[], saved=[], douts=[Row(dmix3, 256, fc=lambda b: 3)])
    return drows


def _norm_mod_fn(ci, b, carries, rows, vecs):
    (x,) = rows
    g, sc, sh = vecs
    xn = x * lax.rsqrt(jnp.mean(x * x, axis=-1, keepdims=True) + NORM_EPS)
    return [], [xn * g * (1.0 + sc) + sh]


def norm_mod_forward(name, x3, g, sc, sh):
    s = x3.shape[1]
    (h,), _ = scan_fwd(name, _norm_mod_fn, nb=1, nchunk=s // _ROW_T, t=_ROW_T, rows=[Row(x3)], vecs=[Vec(g), Vec(sc), Vec(sh)],
                       carries=[], outs=[out_row(x3.shape, BF16)], save=False)
    return h


def norm_mod_backward(name, x3, g, sc, sh, dh3, add3):
    s = x3.shape[1]
    (dx,), dv = scan_bwd(name, _norm_mod_fn, nb=1, nchunk=s // _ROW_T, t=_ROW_T, rows=[Row(x3)], vecs=[Vec(g), Vec(sc), Vec(sh)],
                         carries=[], saved=[], douts=[Row(dh3)], adds={0: Row(add3)})
    return dx, dv


def _gate_fn(ci, b, carries, rows, vecs):
    return [], [rows[0] * vecs[0]]


def gate_backward(name, o3, g, dx3):
    s = o3.shape[1]
    (do,), (dg,) = scan_bwd(name, _gate_fn, nb=1, nchunk=s // _ROW_T, t=_ROW_T, rows=[Row(o3, ddtype=BF16)], vecs=[Vec(g)],
                            carries=[], saved=[], douts=[Row(dx3)])
    return do, dg


def _make_halves():
    @jax.custom_vjp
    def halves(x):
        h = x.shape[1] // 2
        return x[:, :h], x[:, h:]

    def fwd(x):
        return halves(x), None

    def bwd(_, g):
        return (jnp.concatenate(g, axis=1),)

    halves.defvjp(fwd, bwd)
    return halves


_halves = _make_halves()


def _ffn_fn(ci, b, carries, rows, vecs):
    (cu,) = carries
    (u,) = rows
    w, bias = vecs
    hg, hu = _halves(_conv(_shift8, cu, u, w, bias, 3))
    return [_tail8(u)], [_silu(hg) * hu]


_FFN_T = 256
_FFN_CW = FFN_DIM // 2
_FFN_CARRIES = [(8, 2 * _FFN_CW)]
FFN_BLOCK_ORDER = [0, 2, 1, 3]


def _ffn_io(up3, cw, cb):
    own = lambda b: b
    return [Row(up3, 2 * _FFN_CW, fc=own, ddtype=BF16)], [Vec(cw, 2 * _FFN_CW, own), Vec(cb, 2 * _FFN_CW, own)]


def ffn_down_forward(name, up3, cw, cb, w_down, res, gate):
    s, t, cw2 = up3.shape[1], _FFN_T, 2 * _FFN_CW
    d = w_down.shape[1]
    nchunk = s // t
    ride = RIDERS.take(name)
    r_in, r_out, r_scr = ride.specs() if ride else ([], [], [])

    def body(*refs):
        up_ref, cw_ref, cb_ref, wd_ref, res_ref, g_ref = refs[:6]
        ride_in = refs[6:6 + len(r_in)]
        act_ref, save_ref, dn_ref, x2_ref = refs[6 + len(r_in):10 + len(r_in)]
        ride_out = refs[10 + len(r_in):10 + len(r_in) + len(r_out)]
        car, acc = refs[10 + len(r_in) + len(r_out):12 + len(r_in) + len(r_out)]
        sems = refs[12 + len(r_in) + len(r_out):]
        i, b = pl.program_id(0), pl.program_id(1)
        if ride:
            ride.begin(ride_in, ride_out, sems, jnp.logical_and(i == 0, b == 0))
            ride.relay(ride_in, ride_out, sems, 2 * i + b, 2 * nchunk)

        @pl.when(i == 0)
        def _():
            car[b] = jnp.zeros(car.shape[1:], F32)

        cin = car[b]
        save_ref[0, 0] = cin
        (new_c,), (act,) = _ffn_fn(i, b, [cin], [up_ref[0]], [cw_ref[...], cb_ref[...]])
        car[b] = new_c
        act_ref[0] = act.astype(act_ref.dtype)
        part = _mxu(act, wd_ref[...], "nn")

        @pl.when(b == 0)
        def _():
            acc[...] = part

        @pl.when(b == 1)
        def _():
            tot = acc[...] + part
            dn_ref[...] = tot
            x2_ref[...] = res_ref[...] + g_ref[...] * tot

        if ride:
            ride.end(ride_in, ride_out, sems, jnp.logical_and(i == nchunk - 1, b == 1))

    tile = pl.BlockSpec((t, d), lambda i, b: (i, 0))
    out = pl.pallas_call(
        body, name=name, grid=(nchunk, 2),
        in_specs=[pl.BlockSpec((1, t, cw2), lambda i, b: (0, i, b)), pl.BlockSpec((cw.shape[0], cw2), lambda i, b: (0, b)),
                  pl.BlockSpec((1, cw2), lambda i, b: (0, b)), pl.BlockSpec((_FFN_CW, d), lambda i, b: (b, 0)), tile,
                  pl.BlockSpec((1, d), lambda i, b: (0, 0))] + r_in,
        out_specs=[pl.BlockSpec((1, t, _FFN_CW), lambda i, b: (0, i, b)), pl.BlockSpec((1, 1, 8, cw2), lambda i, b: (b, i, 0, 0)),
                   tile, tile] + r_out,
        out_shape=[jax.ShapeDtypeStruct((1, s, FFN_DIM), BF16), jax.ShapeDtypeStruct((2, nchunk, 8, cw2), F32),
                   jax.ShapeDtypeStruct((s, d), F32), jax.ShapeDtypeStruct((s, d), F32)] + (list(ride.out_shapes) if ride else []),
        scratch_shapes=[pltpu.VMEM((2, 8, cw2), F32), pltpu.VMEM((t, d), F32)] + r_scr,
        compiler_params=_cparams(),
    )(up3, cw, cb, w_down, res, gate, *(ride.ins if ride else []))
    if ride:
        RIDERS.done[name] = list(out[4:])
    return out[0], [out[1]], out[2], out[3]


def ffn_mid_backward(name, up3, cw, cb, saved, dact3):
    rows, vecs = _ffn_io(up3, cw, cb)
    s = up3.shape[1]
    return scan_bwd(name, _ffn_fn, nb=2, nchunk=s // _FFN_T, t=_FFN_T, rows=rows, vecs=vecs, carries=_FFN_CARRIES,
                    saved=saved, douts=[Row(dact3, _FFN_CW, fc=lambda b: b)])


def _adam_fn(ci, b, carries, rows, vecs):
    w, g, m, v = rows
    m = ADAM_B1 * m + (1.0 - ADAM_B1) * g
    v = ADAM_B2 * v + (1.0 - ADAM_B2) * (g * g)
    m_hat = m / (1.0 - ADAM_B1 ** ADAM_STEP)
    v_hat = v / (1.0 - ADAM_B2 ** ADAM_STEP)
    delta = -ADAM_LR * (m_hat / (jnp.sqrt(v_hat) + ADAM_EPS) + ADAM_WD * w)
    return [], [delta, m, v]


def _adam_layers_fn(ci, b, carries, rows, vecs):
    w, mine, theirs, m, v = rows
    g = jnp.where(b == lax.axis_index("c"), mine, theirs)
    _, upd = _adam_fn(ci, b, carries, [w, g, m, v], vecs)
    return [], [g] + upd


def adamw_layers(name, w, mine, theirs, m, v):
    _, r, c = w.shape
    t = _tile(r, 256, 8)
    layer = lambda b: b
    rows = [Row(w, fb=layer), Row(mine[None]), Row(theirs[None]), Row(m, fb=layer), Row(v, fb=layer)]
    outs, _ = scan_fwd(name, _adam_layers_fn, nb=2, nchunk=r // t, t=t, rows=rows, vecs=[], carries=[],
                       outs=[out_row(w.shape, fb=layer) for _ in range(4)], save=False)
    return outs


def adamw(name, w, g, m, v):
    shape = w.shape
    c = shape[-1]
    r = int(np.prod(shape[:-1]))
    t = _tile(r, 256, 8)
    as3 = lambda a: a.reshape(1, r, c)
    outs, _ = scan_fwd(name, _adam_fn, nb=1, nchunk=r // t, t=t, rows=[Row(as3(a)) for a in (w, g, m, v)], vecs=[], carries=[],
                       outs=[out_row((1, r, c)) for _ in range(3)], save=False)
    return [o.reshape(shape) for o in outs]


def rope_tables(positions):
    inv_freq = ROPE_THETA ** (-jnp.arange(0, ROT_DIM, 2, dtype=F32) / ROT_DIM)
    ang = positions.astype(F32)[:, None] * inv_freq
    s = positions.shape[0]
    cs = jnp.concatenate([jnp.cos(ang), jnp.cos(ang), jnp.ones((s, ATT_HEAD_DIM - ROT_DIM), F32)], axis=1)
    sn = jnp.concatenate([jnp.sin(ang), jnp.sin(ang), jnp.zeros((s, ATT_HEAD_DIM - ROT_DIM), F32)], axis=1)
    cs3 = jnp.concatenate([jnp.tile(cs, (1, 2 * ATT_HEADS)), jnp.ones((s, ATT_W), F32)], axis=1)
    sn3 = jnp.concatenate([jnp.tile(sn, (1, 2 * ATT_HEADS)), jnp.zeros((s, ATT_W), F32)], axis=1)
    return cs3[None], sn3[None]


def attention_forward(lname, qkv3, cs3, sn3):
    s = qkv3.shape[1]
    rotated = rope_forward(f"{lname}_rope", qkv3, cs3, sn3)
    os_, ls_, keep = [], [], []
    for pi, (_, d) in enumerate(ATT_PATTERNS):
        o, lse, saved = attn_forward(f"{lname}_attn{pi}", rotated[pi], d)
        os_.append(o)
        ls_.append(lse)
        keep.append(saved)
    y = merge_forward(f"{lname}_merge", os_, ls_, s)
    return y, (rotated, os_, ls_, keep)


def attention_backward(lname, qkv3, cs3, sn3, res, dmix3):
    rotated, os_, ls_, keep = res
    dm = merge_backward(f"{lname}_merge_b", os_, ls_, dmix3)
    dys = [attn_backward(f"{lname}_attn{pi}_b", rotated[pi], d, keep[pi], dm[pi], dm[3 + pi]) for pi, (_, d) in enumerate(ATT_PATTERNS)]
    return rope_backward(f"{lname}_rope_b", qkv3, cs3, sn3, dys)


def final_loss(name, x3, t3, g):
    s, d = x3.shape[1], x3.shape[2]
    t = _ROW_T

    def body(x_ref, t_ref, g_ref, loss_ref, dx_ref, dg_ref):
        i = pl.program_id(0)
        tv = t_ref[0]

        def f(x, gg):
            y = x * lax.rsqrt(jnp.mean(x * x, axis=-1, keepdims=True) + NORM_EPS) * gg
            e = y - tv
            return 0.5 * jnp.sum(jnp.mean(e * e, axis=-1, keepdims=True), axis=0, keepdims=True)

        l, vjp = jax.vjp(f, x_ref[0], g_ref[...])
        dx, dg = vjp(jnp.ones((1, 1), F32))
        dx_ref[0] = dx

        @pl.when(i == 0)
        def _():
            loss_ref[...] = jnp.zeros(loss_ref.shape, F32)
            dg_ref[...] = jnp.zeros(dg_ref.shape, F32)

        loss_ref[...] += jnp.broadcast_to(l, loss_ref.shape)
        dg_ref[...] += dg

    row = pl.BlockSpec((1, t, d), lambda i: (0, i, 0))
    vec = pl.BlockSpec((1, d), lambda i: (0, 0))
    return pl.pallas_call(
        body, name=name, grid=(s // t,), in_specs=[row, row, vec],
        out_specs=[pl.BlockSpec((8, 128), lambda i: (0, 0)), row, vec],
        out_shape=[jax.ShapeDtypeStruct((8, 128), F32), jax.ShapeDtypeStruct(x3.shape, F32), jax.ShapeDtypeStruct((1, d), F32)],
        compiler_params=pltpu.CompilerParams(dimension_semantics=("arbitrary",), vmem_limit_bytes=VMEM_LIMIT_BYTES),
    )(x3, t3, g)


_ADA_TN = 512


def ada_forward(name, c16, ada_w):
    depth, d, cols = ada_w.shape

    def body(c_ref, w_ref, o_ref):
        o_ref[0] = _mxu(_silu(c_ref[...]), w_ref[0], "nn")

    return pl.pallas_call(
        body, name=name, grid=(depth, cols // _ADA_TN),
        in_specs=[pl.BlockSpec((16, d), lambda l, j: (0, 0)), pl.BlockSpec((1, d, _ADA_TN), lambda l, j: (l, 0, j))],
        out_specs=pl.BlockSpec((1, 16, _ADA_TN), lambda l, j: (l, 0, j)),
        out_shape=jax.ShapeDtypeStruct((depth, 16, cols), F32),
        compiler_params=pltpu.CompilerParams(dimension_semantics=("arbitrary", "arbitrary"), vmem_limit_bytes=VMEM_LIMIT_BYTES),
    )(c16, ada_w)


def ada_backward(name, c16, dmod16, w, m, v):
    depth, d, cols = w.shape

    def body(c_ref, dm_ref, w_ref, m_ref, v_ref, g_ref, dl_ref, nm_ref, nv_ref):
        g = _mxu(_silu(c_ref[...]), dm_ref[0], "tn")
        _, (delta, nm, nv) = _adam_fn(None, None, [], [w_ref[0], g, m_ref[0], v_ref[0]], [])
        g_ref[0], dl_ref[0], nm_ref[0], nv_ref[0] = g, delta, nm, nv

    blk = pl.BlockSpec((1, d, _ADA_TN), lambda l, j: (l, 0, j))
    return pl.pallas_call(
        body, name=name, grid=(depth, cols // _ADA_TN),
        in_specs=[pl.BlockSpec((16, d), lambda l, j: (0, 0)), pl.BlockSpec((1, 16, _ADA_TN), lambda l, j: (l, 0, j)), blk, blk, blk],
        out_specs=[blk] * 4, out_shape=[jax.ShapeDtypeStruct(w.shape, F32)] * 4,
        compiler_params=pltpu.CompilerParams(dimension_semantics=("arbitrary", "arbitrary"), vmem_limit_bytes=VMEM_LIMIT_BYTES),
    )(c16, dmod16, w, m, v)


def _sum_fn(ci, b, carries, rows, vecs):
    acc = rows[0].astype(F32)
    for r in rows[1:]:
        acc = acc + r.astype(F32)
    return [], [acc]


def sum_slots(name, a, nsum, out_dtype=F32):
    n, r, c = a.shape
    nb = n // nsum
    t = _tile(r, 256, 8)
    rows = [Row(a, fb=(lambda b, k=k: k * nb + b)) for k in range(nsum)]
    (out,), _ = scan_fwd(name, _sum_fn, nb=nb, nchunk=r // t, t=t, rows=rows, vecs=[], carries=[],
                         outs=[out_row((nb, r, c), out_dtype, fb=lambda b: b)], save=False)
    return out


def _flip(mask, pos):
    return tuple((1 - p) if m else p for m, p in zip(mask, pos))


ALL_PEERS = [(a, b, c) for a in (0, 1) for b in (0, 1) for c in (0, 1)][1:]
CHIP_PEERS = [(1, 0, 0), (0, 1, 0), (1, 1, 0)]
SIBLING = [(0, 0, 1)]


def _dev(pos):
    return 4 * pos[0] + 2 * pos[1] + pos[2]


def _chip(pos):
    return 2 * pos[0] + pos[1]


def allgather8(name, a):
    (out,) = ride_alone(name, allgather8_ride(a))
    return _with_own(out, a)


def _rows_of(shape):
    return -(-int(np.prod(shape)) // 1024) * 8


def _pack(arrs):
    parts = []
    for a in arrs:
        flat = a.reshape(-1).astype(F32)
        parts.append(jnp.pad(flat, (0, _rows_of(a.shape) * 128 - flat.shape[0])).reshape(-1, 128))
    rows = sum(p.shape[0] for p in parts)
    parts.append(jnp.zeros(((-rows) % _ROW_T, 128), F32))
    return jnp.concatenate(parts, axis=0)


def _unpack(buf, shapes):
    out, o = [], 0
    for s in shapes:
        r, n = _rows_of(s), int(np.prod(s))
        out.append(buf[o:o + r].reshape(-1)[:n].reshape(s))
        o += r
    return out


_WEIGHTS = ["ada_w", "ada_b", "norm1_g", "w_in", "ssd_conv_w", "ssd_conv_b", "ssd_dt_bias", "ssd_a_log", "ssd_d", "ssd_norm_g",
            "pool_w", "pool_scale", "w_out", "norm2_g", "ffn_up", "ffn_conv_w", "ffn_conv_b", "ffn_down", "final_g"]
_BIG = ["w_in", "w_out", "ffn_up", "ffn_down"]
_SMALL = [n for n in _WEIGHTS if n not in _BIG and n != "ada_w"]
_COL_SHARDED_SMALL = {"ssd_conv_w": 256, "ffn_conv_w": 1408}


def _pad_lanes(v, n=128):
    return jnp.pad(v.astype(F32), (0, n - v.shape[0]))[None]


_CHIP2_PARTS = [(1284, 1536), (1792, 1800), (1536, 1792), (IN_MAIN, IN_MAIN + 126)]


def _w_in_chip_cols(gp):
    q = IN_W // 4
    return [gp[:, :q], gp[:, q:2 * q], jnp.concatenate([gp[:, a:b] for a, b in _CHIP2_PARTS], axis=1), gp[:, IN_WP - q:]]


def _ffn_block_perm(a):
    n = a.shape[-1] // 4
    return jnp.concatenate([a[..., j * n:(j + 1) * n] for j in FFN_BLOCK_ORDER], axis=-1)


def _layer_forward(i, x3, modv, wts, sp, cs3, sn3):
    sh1, sc1, g1, sh2, sc2, g2 = modv
    big = lambda n: wts[n]() if callable(wts[n]) else wts[n]
    h1 = norm_mod_forward(f"l{i}_norm1", x3, wts["norm1_g"], sc1, sh1)
    proj3 = mm(f"l{i}_proj", h1[0], big("w_in")[:, :IN_MAIN], "nn")[None]
    qkv3 = mm(f"l{i}_qkv", h1[0], big("w_in")[:, IN_MAIN:], "nn")[None]
    y_ssd, sv_ssd = ssd_forward(f"l{i}_ssd", proj3, sp)
    y_pool, sv_pool = pool_forward(f"l{i}_pool", proj3, wts["wbd"], wts["pool_scale"])
    y_att, res_att = attention_forward(f"l{i}", qkv3, cs3, sn3)
    mix = jnp.concatenate([y_ssd, y_pool, y_att], axis=-1)
    out, x1 = mm(f"l{i}_wout", mix[0], big("w_out"), "nn", res=x3[0], gate=g1)
    x1 = x1[None]
    h2 = norm_mod_forward(f"l{i}_norm2", x1, wts["norm2_g"], sc2, sh2)
    up3 = mm(f"l{i}_up", h2[0], big("ffn_up"), "nn")[None]
    act, sv_ffn, dn, x2 = ffn_down_forward(f"l{i}_down", up3, wts["ffn_conv_w"], wts["ffn_conv_b"], big("ffn_down"), x1[0], g2)
    keep = dict(x=x3, h1=h1, proj3=proj3, qkv3=qkv3, sv_ssd=sv_ssd, sv_pool=sv_pool, res_att=res_att, mix=mix, out=out[None],
                x1=x1, h2=h2, up3=up3, act=act, sv_ffn=sv_ffn, dn=dn[None])
    return x2[None], keep


def _layer_backward(i, dx2, keep, modv, wts, sp, cs3, sn3, after=None):
    sh1, sc1, g1, sh2, sc2, g2 = modv
    k = keep
    big = lambda n: wts[n]() if callable(wts[n]) else wts[n]
    tell = lambda step, *a: after[step](*a) if after and step in after else None
    d_dn, d_g2 = gate_backward(f"l{i}_gate2_b", k["dn"], g2, dx2)
    d_act = mm(f"l{i}_down_bx", d_dn[0], big("ffn_down"), "nt")
    g_down = mm(f"l{i}_down_bw", k["act"][0], d_dn[0], "tn", BF16).reshape(4, FFN_DIM // 4, D_MODEL)
    (d_up,), dv_ffn = ffn_mid_backward(f"l{i}_ffn_b", k["up3"], wts["ffn_conv_w"], wts["ffn_conv_b"], k["sv_ffn"], d_act[None])
    tell("ffn_b")
    d_h2 = mm(f"l{i}_up_bx", d_up[0], big("ffn_up"), "nt")
    g_up = mm(f"l{i}_up_bw", k["h2"][0], d_up[0], "tn", BF16, tn=_FFN_CW,
              into=((4, D_MODEL, _FFN_CW), lambda r, c: ((c % 2) * 2 + c // 2, r, 0)))
    dx1, (d_n2, d_sc2, d_sh2) = norm_mod_backward(f"l{i}_norm2_b", k["x1"], wts["norm2_g"], sc2, sh2, d_h2[None], dx2)
    d_out, d_g1 = gate_backward(f"l{i}_gate1_b", k["out"], g1, dx1)
    d_mix = mm(f"l{i}_wout_bx", d_out[0], big("w_out"), "nt")[None]
    g_wout = mm(f"l{i}_wout_bw", k["mix"][0], d_out[0], "tn", BF16).reshape(4, D_MODEL // 4, D_MODEL)
    tell("wout_bw", g_wout, g_up, g_down)
    (dz, dxs, dbm, dcm, ddt), dv_ssd = ssd_backward(f"l{i}_ssd_b", k["proj3"], sp, k["sv_ssd"], d_mix)
    tell("ssd_b")
    (du_pool,), (d_wbd, d_pscale) = pool_backward(f"l{i}_pool_b", k["proj3"], wts["wbd"], wts["pool_scale"], k["sv_pool"], d_mix)
    d_qkv = attention_backward(f"l{i}", k["qkv3"], cs3, sn3, k["res_att"], d_mix)
    d_proj = jnp.concatenate([dz[0], dxs[0], dbm[0], dcm[0], du_pool[0], (ddt[0] + ddt[1]).astype(BF16), d_qkv[0]], axis=-1)
    g_win = jnp.stack(_w_in_chip_cols(mm(f"l{i}_proj_bw", k["h1"][0], d_proj, "tn", BF16)))
    tell("proj_bw", g_win)
    d_h1 = mm(f"l{i}_proj_bx", d_proj, big("w_in"), "nt")
    tell("proj_bx")
    dx, (d_n1, d_sc1, d_sh1) = norm_mod_backward(f"l{i}_norm1_b", k["x"], wts["norm1_g"], sc1, sh1, d_h1[None], dx1)
    dcwx, dcbx, dcwb, dcbb, dcwc, dcbc, ddtb, dalog, ddsk, dng = dv_ssd
    small = dict(
        norm1_g=d_n1[0], norm2_g=d_n2[0],
        ssd_conv_w=jnp.concatenate([dcwx[:, :512], dcwb[:, 512:768], dcwc[:, 768:]], axis=1),
        ssd_conv_b=jnp.concatenate([dcbx[0, :512], dcbb[0, 512:768], dcbc[0, 768:]]),
        ssd_dt_bias=ddtb[0, :8], ssd_a_log=dalog[0, :8], ssd_d=ddsk[0, :8], ssd_norm_g=dng[0],
        pool_w=jnp.stack([d_wbd[64 * g:64 * g + 64, 64 * g:64 * g + 64] for g in range(4)]), pool_scale=d_pscale[0],
        ffn_conv_w=_ffn_block_perm(dv_ffn[0]), ffn_conv_b=_ffn_block_perm(dv_ffn[1][0]),
    )
    dmod = jnp.concatenate([d_sh1[0], d_sc1[0], d_g1[0], d_sh2[0], d_sc2[0], d_g2[0]])
    return dx, [g_win, g_wout, g_up, g_down], small, dmod


def kernel(x, c, positions, ada_w, ada_b, norm1_g, w_in, ssd_conv_w, ssd_conv_b, ssd_dt_bias, ssd_a_log, ssd_d, ssd_norm_g, pool_w, pool_scale, w_out, norm2_g, ffn_up, ffn_conv_w, ffn_conv_b, ffn_down, final_g, loss_target, m_ada_w, m_ada_b, m_norm1_g, m_w_in, m_ssd_conv_w, m_ssd_conv_b, m_ssd_dt_bias, m_ssd_a_log, m_ssd_d, m_ssd_norm_g, m_pool_w, m_pool_scale, m_w_out, m_norm2_g, m_ffn_up, m_ffn_conv_w, m_ffn_conv_b, m_ffn_down, m_final_g, v_ada_w, v_ada_b, v_norm1_g, v_w_in, v_ssd_conv_w, v_ssd_conv_b, v_ssd_dt_bias, v_ssd_a_log, v_ssd_d, v_ssd_norm_g, v_pool_w, v_pool_scale, v_w_out, v_norm2_g, v_ffn_up, v_ffn_conv_w, v_ffn_conv_b, v_ffn_down, v_final_g):
    args = dict(locals())
    w = {n: args[n] for n in _WEIGHTS}
    m = {n: args["m_" + n] for n in _WEIGHTS}
    v = {n: args["v_" + n] for n in _WEIGHTS}
    d = D_MODEL
    me = (lax.axis_index("x"), lax.axis_index("y"), lax.axis_index("c"))
    chip, dev = _chip(me), _dev(me)
    RIDERS.reset()

    shapes0 = [c.shape, ssd_conv_w.shape, ffn_conv_w.shape]
    pack0 = _pack([c, ssd_conv_w, ffn_conv_w])
    shards = [w[n].astype(BF16) for n in _BIG]
    g0 = allgather8("gather_c_conv", pack0)
    c16 = jnp.pad(g0[:, :d // 128, :].reshape(8, d), ((0, 8), (0, 0)))
    by_chip = [_unpack(g0[2 * j], shapes0) for j in range(4)]
    conv_w_full = jnp.concatenate([p[1] for p in by_chip], axis=-1)
    fconv_w_full = jnp.concatenate([p[2] for p in by_chip], axis=-1)

    modp = ada_forward("ada_fwd", c16, ada_w)[:, :8]
    pack1 = _pack([modp])
    g1, w_in0 = ride_alone("gather_mod_w_in0", merge_rides([allgather8_ride(pack1), gather_ride(0, [shards[0]])]))
    g1 = _with_own(g1, pack1)
    modfull = jnp.concatenate([_unpack(g1[2 * j], [modp.shape])[0] for j in range(4)], axis=-1)
    mod = lax.dynamic_index_in_dim(modfull, dev, axis=1, keepdims=False) + ada_b
    modv = [[mod[i, q * d:(q + 1) * d][None] for q in range(6)] for i in range(DEPTH)]


    def weight(k, layer, got):
        full = lax.dynamic_update_slice(got, shards[k][layer][None], (chip, 0, 0))
        if k == 0:
            return _w_in_from_chips(full)
        if k == 2:
            return jnp.concatenate([full[j] for j in FFN_BLOCK_ORDER], axis=1)
        return full.reshape(-1, full.shape[2])

    def later(k, layer, *sources):
        made = []

        def get():
            if not made:
                got = [RIDERS.result(host)[pos] for host, pos in sources]
                made.append(weight(k, layer, got[0] if len(got) == 1 else jnp.concatenate(got, axis=1)))
            return made[0]
        return get

    cs3, sn3 = rope_tables(positions[0])
    eye4 = jnp.eye(4, dtype=F32)
    wts, sps = [], []
    for i in range(DEPTH):
        wts.append(dict(
            norm1_g=norm1_g[i][None], norm2_g=norm2_g[i][None], pool_scale=pool_scale[i][None],
            wbd=(eye4[:, None, :, None] * pool_w[i][:, :, None, :]).reshape(POOL_W, POOL_W),
            ffn_conv_w=_ffn_block_perm(fconv_w_full[i]), ffn_conv_b=_ffn_block_perm(ffn_conv_b[i])[None]))
        sps.append(dict(cw=conv_w_full[i], cb=ssd_conv_b[i][None], dtb=_pad_lanes(ssd_dt_bias[i]), alog=_pad_lanes(ssd_a_log[i]),
                        dsk=_pad_lanes(ssd_d[i]), ng=ssd_norm_g[i][None]))

    RIDERS.book("l0_ssd", gather_ride(0, [shards[1], shards[3]]))
    half = shards[2].shape[1] // 2
    RIDERS.book("l0_attn0", gather_ride(0, [shards[2][:, :half]]))
    RIDERS.book("l0_attn1", gather_ride(0, [shards[2][:, half:]]))
    wts[0].update(w_in=weight(0, 0, w_in0), w_out=later(1, 0, ("l0_ssd", 0)), ffn_down=later(3, 0, ("l0_ssd", 1)),
                  ffn_up=later(2, 0, ("l0_attn0", 0), ("l0_attn1", 0)))
    RIDERS.book("l0_attn2", gather_ride(1, [shards[0], shards[1]]))
    RIDERS.book("l0_up", gather_ride(1, [shards[3]]))
    RIDERS.book("l0_down", gather_ride(1, [shards[2]]))
    wts[1].update(w_in=later(0, 1, ("l0_attn2", 0)), w_out=later(1, 1, ("l0_attn2", 1)), ffn_up=later(2, 1, ("l0_down", 0)),
                  ffn_down=later(3, 1, ("l0_up", 0)))
    x1_, keep0 = _layer_forward(0, x, modv[0], wts[0], sps[0], cs3, sn3)
    xc, keep1 = _layer_forward(1, x1_, modv[1], wts[1], sps[1], cs3, sn3)
    keeps = [keep0, keep1]
    lossblk, dx, d_final = final_loss("final_loss", xc, loss_target, final_g[None])

    small_g, dmods = [None] * DEPTH, [None] * DEPTH
    part_sum, from_chips = [[None] * 4 for _ in range(DEPTH)], [[None] * 4 for _ in range(DEPTH)]

    def owner_sum(layer, ks, mine, theirs):
        for k, g, t in zip(ks, mine, theirs):
            part_sum[layer][k] = add_arrays(f"sum_cores{layer}_{_BIG[k]}", [g, t], BF16)

    dx, by_chip1, small_g[1], dmods[1] = _layer_backward(1, dx, keeps[1], modv[1], wts[1], sps[1], cs3, sn3)
    RIDERS.book("l0_ffn_b", to_owner_ride(1, by_chip1))

    def after_ffn_b():
        owner_sum(1, range(4), by_chip1, RIDERS.result("l0_ffn_b"))
        RIDERS.book("l0_up_bx", scatter_ride(1, [part_sum[1][2]]))
        RIDERS.book("l0_up_bw", scatter_ride(1, [part_sum[1][0], part_sum[1][1]]))
        RIDERS.book("l0_norm2_b", scatter_ride(1, [part_sum[1][3]]))

    early = []

    def after_wout_bw(g_wout, g_up, g_down):
        early.extend([g_wout, g_up, g_down])
        RIDERS.book("l0_ssd_b", to_owner_ride(0, early))

    def after_ssd_b():
        owner_sum(0, [1, 2, 3], early, RIDERS.result("l0_ssd_b"))
        for host, k in (("l0_attn0_b", 2), ("l0_attn1_b", 3), ("l0_attn2_b", 1)):
            RIDERS.book(host, scatter_ride(0, [part_sum[0][k]]))

    last = []

    def after_proj_bw(g_win):
        last.append(g_win)
        RIDERS.book("l0_proj_bx", to_owner_ride(0, last))

    def after_proj_bx():
        owner_sum(0, [0], last, RIDERS.result("l0_proj_bx"))
        RIDERS.book("l0_norm1_b", scatter_ride(0, [part_sum[0][0]]))

    hooks = dict(ffn_b=after_ffn_b, wout_bw=after_wout_bw, ssd_b=after_ssd_b, proj_bw=after_proj_bw, proj_bx=after_proj_bx)
    dx, _, small_g[0], dmods[0] = _layer_backward(0, dx, keeps[0], modv[0], wts[0], sps[0], cs3, sn3, after=hooks)
    from_chips[1][2], (from_chips[1][0], from_chips[1][1]) = RIDERS.result("l0_up_bx")[0], RIDERS.result("l0_up_bw")
    from_chips[1][3] = RIDERS.result("l0_norm2_b")[0]
    for host, k in (("l0_attn0_b", 2), ("l0_attn1_b", 3), ("l0_attn2_b", 1), ("l0_norm1_b", 0)):
        from_chips[0][k] = RIDERS.result(host)[0]
    mine = [sum_chips_mine(f"sum_chips_{n}", part_sum[0][k], from_chips[0][k], part_sum[1][k], from_chips[1][k])
            for k, n in enumerate(_BIG)]

    part = dict(ada_b=jnp.stack(dmods), final_g=d_final[0])
    for n in _SMALL:
        if n not in part:
            part[n] = jnp.stack([small_g[i][n] for i in range(DEPTH)])
    full_shapes = [part[n].shape for n in _SMALL] + [(1,)]
    pack_small = _pack([part[n] for n in _SMALL] + [lossblk[0, :1]])
    *theirs, gs = ride_alone("swap_r_gather_small", merge_rides([swap_ride(mine), allgather8_ride(pack_small)]))
    gs = _with_own(gs, pack_small)
    tot = _unpack(sum_slots("sum_small", gs, 8)[0], full_shapes)
    loss = tot[-1].reshape(())
    grads = {}
    small_tot = dict(zip(_SMALL, tot))
    dmod_all = gs[:, :DEPTH * 6 * d // 128, :].reshape(8, DEPTH, 6 * d)
    for n, ncol in _COL_SHARDED_SMALL.items():
        small_tot[n] = lax.dynamic_slice_in_dim(small_tot[n], chip * ncol, ncol, axis=2)
    grads.update(small_tot)

    ncol = ada_w.shape[2]
    dm = lax.dynamic_slice_in_dim(dmod_all, chip * ncol, ncol, axis=2).transpose(1, 0, 2)
    upd = {}
    g_ada, *upd["ada_w"] = ada_backward("ada_bwd", c16, jnp.pad(dm, ((0, 0), (0, 8), (0, 0))), ada_w, m["ada_w"], v["ada_w"])
    grads["ada_w"] = g_ada

    for n, a, g in zip(_BIG, mine, theirs):
        grads[n], *upd[n] = adamw_layers(f"adam_{n}", w[n], a, g, m[n], v[n])
    shapes_s = [w[n].shape for n in _SMALL]
    packed = [_pack([src[n] for n in _SMALL]) for src in (w, grads, m, v)]
    outs_s = [_unpack(o, shapes_s) for o in adamw("adam_small", *packed)]
    for q, n in enumerate(_SMALL):
        upd[n] = [outs_s[0][q], outs_s[1][q], outs_s[2][q]]

    return (loss, dx, *[grads[n] for n in _WEIGHTS], *[upd[n][0] for n in _WEIGHTS], *[upd[n][1] for n in _WEIGHTS],
            *[upd[n][2] for n in _WEIGHTS])


def ride_alone(name, ride):
    ni, no = len(ride.ins), len(ride.out_shapes)

    def body(*refs):
        ride.begin(refs[:ni], refs[ni:ni + no], refs[ni + no:])
        ride.end(refs[:ni], refs[ni:ni + no], refs[ni + no:])

    in_specs, out_specs, scratch = ride.specs()
    return list(pl.pallas_call(body, name=name, in_specs=in_specs, out_specs=out_specs, out_shape=ride.out_shapes,
                               scratch_shapes=scratch)(*ride.ins))


def mm(name, a, b, mode, out_dtype=F32, res=None, gate=None, tm=1408, tn=1536, tk=1408, into=None):
    ride = RIDERS.take(name)
    if mode == "nn":
        (m, k), n = a.shape, b.shape[1]
    elif mode == "nt":
        (m, k), n = a.shape, b.shape[0]
    else:
        (k, m), n = a.shape, b.shape[1]
    tm, tn, tk = _tile(m, tm), _tile(n, tn), _tile(k, tk)
    ni, nj, nk = m // tm, n // tn, k // tk
    a_spec = pl.BlockSpec((tk, tm), lambda i, j, q: (q, i)) if mode == "tn" else pl.BlockSpec((tm, tk), lambda i, j, q: (i, q))
    b_spec = pl.BlockSpec((tn, tk), lambda i, j, q: (j, q)) if mode == "nt" else pl.BlockSpec((tk, tn), lambda i, j, q: (q, j))
    o_spec = pl.BlockSpec((tm, tn), lambda i, j, q: (i, j))
    fused = res is not None
    lead = 0 if into is None else len(into[0]) - 2
    first = (0,) * lead + (slice(None), slice(None))
    ins, in_specs = [a, b], [a_spec, b_spec]
    out_shape, out_specs = [jax.ShapeDtypeStruct((m, n), out_dtype)], [o_spec]
    if fused:
        ins += [res, gate]
        in_specs += [o_spec, pl.BlockSpec((1, tn), lambda i, j, q: (0, j))]
        out_shape.append(jax.ShapeDtypeStruct((m, n), F32))
        out_specs.append(o_spec)
    if into is not None:
        shape, omap = into
        out_shape = [jax.ShapeDtypeStruct(shape, out_dtype)]
        out_specs = [pl.BlockSpec((1,) * lead + (tm, tn), lambda i, j, q: omap(i, j))]
    n_in, n_out = len(ins), len(out_shape)
    scratch = [pltpu.VMEM((tm, tn), F32)]
    if ride is not None:
        r_in, r_out, r_scr = ride.specs()
        ins, in_specs = ins + list(ride.ins), in_specs + r_in
        out_shape, out_specs = out_shape + list(ride.out_shapes), out_specs + r_out
        scratch = scratch + r_scr

    def body(*refs):
        a_ref, b_ref = refs[:2]
        o_ref = refs[len(ins)]
        acc = refs[len(ins) + len(out_shape)]
        i, j, q = pl.program_id(0), pl.program_id(1), pl.program_id(2)
        at = lambda x, y, z: jnp.logical_and(jnp.logical_and(i == x, j == y), q == z)
        r_refs = (refs[n_in:len(ins)], refs[len(ins) + n_out:len(ins) + len(out_shape)], refs[len(ins) + len(out_shape) + 1:])
        if ride is not None:
            ride.begin(*r_refs, at(0, 0, 0))
            ride.relay(*r_refs, (i * nj + j) * nk + q, ni * nj * nk)

        @pl.when(q == 0)
        def _():
            acc[...] = jnp.zeros(acc.shape, F32)

        acc[...] += _mxu(a_ref[...], b_ref[...], mode)

        @pl.when(q == nk - 1)
        def _():
            o_ref[first] = acc[...].astype(o_ref.dtype)
            if fused:
                refs[len(ins) + 1][...] = refs[2][...] + refs[3][...] * acc[...]

        if ride is not None:
            ride.end(*r_refs, at(ni - 1, nj - 1, nk - 1))

    sem = ("arbitrary",) * 3 if ride is not None else ("parallel", "parallel", "arbitrary")
    out = pl.pallas_call(
        body, name=name, grid=(ni, nj, nk), in_specs=in_specs, out_specs=out_specs, out_shape=out_shape, scratch_shapes=scratch,
        compiler_params=pltpu.CompilerParams(dimension_semantics=sem, vmem_limit_bytes=VMEM_LIMIT_BYTES),
    )(*ins)
    if ride is not None:
        RIDERS.done[name] = list(out[n_out:])
    return tuple(out[:n_out]) if fused else out[0]


def add_arrays(name, arrs, out_dtype=F32):
    nb, r, c = arrs[0].shape
    t = _tile(r, 256, 8)
    (out,), _ = scan_fwd(name, _sum_fn, nb=nb, nchunk=r // t, t=t, rows=[Row(a, fb=lambda b: b) for a in arrs], vecs=[], carries=[],
                         outs=[out_row((nb, r, c), out_dtype, fb=lambda b: b)], save=False)
    return out


def _sum_chips_mine_fn(ci, b, carries, rows, vecs):
    mine_layer = lax.axis_index("c")
    chip = 2 * lax.axis_index("x") + lax.axis_index("y")
    tot = None
    for j in range(4):
        own = jnp.where(mine_layer == 0, rows[j], rows[8 + j])
        sent = jnp.where(mine_layer == 0, rows[4 + j], rows[12 + j])
        term = jnp.where(chip == j, own, sent).astype(F32)
        tot = term if tot is None else tot + term
    return [], [tot]


def sum_chips_mine(name, p0, q0, p1, q1):
    _, r, c = p0.shape
    t = _tile(r, 256, 8)
    rows = [Row(a, fb=(lambda b, j=j: j)) for a in (p0, q0, p1, q1) for j in range(4)]
    (out,), _ = scan_fwd(name, _sum_chips_mine_fn, nb=1, nchunk=r // t, t=t, rows=rows, vecs=[], carries=[],
                         outs=[out_row((1, r, c))], save=False)
    return out[0]


def _remote(src, dst, send_sems, recv_sems, k, to):
    return pltpu.make_async_remote_copy(src_ref=src, dst_ref=dst, send_sem=send_sems.at[k], recv_sem=recv_sems.at[k],
                                        device_id=to, device_id_type=MESH)


def gather_ride(layer, shards):
    na = len(shards)

    def rows(ref, c):
        h = ref.shape[0] // 2
        return ref.at[pl.ds(c * h, h)]

    def start(ins, outs, ss, rs, me):
        for k in range(na):
            for p, mask in enumerate(CHIP_PEERS):
                _remote(rows(ins[k].at[layer], me[2]), rows(outs[k].at[_chip(me)], me[2]), ss, rs, 6 * k + p, _flip(mask, me)).start()

    def middle(ins, outs, ss, rs, me):
        sibling = _flip(SIBLING[0], me)
        for k in range(na):
            for p, mask in enumerate(CHIP_PEERS):
                got = rows(outs[k].at[_chip(_flip(mask, me))], me[2])
                _remote(rows(ins[k].at[layer], me[2]), got, ss, rs, 6 * k + p, _flip(mask, me)).wait_recv()
                _remote(got, got, ss, rs, 6 * k + 3 + p, sibling).start()

    def finish(ins, outs, ss, rs, me):
        sibling = _flip(SIBLING[0], me)
        for k in range(na):
            for p, mask in enumerate(CHIP_PEERS):
                got = rows(outs[k].at[_chip(_flip(mask, me))], me[2])
                other = rows(outs[k].at[_chip(_flip(mask, me))], 1 - me[2])
                _remote(other, other, ss, rs, 6 * k + 3 + p, sibling).wait_recv()
                _remote(rows(ins[k].at[layer], me[2]), got, ss, rs, 6 * k + p, _flip(mask, me)).wait_send()
                _remote(got, got, ss, rs, 6 * k + 3 + p, sibling).wait_send()

    return Ride(list(shards), [jax.ShapeDtypeStruct((4,) + a.shape[1:], a.dtype) for a in shards], 6 * na, start, finish, middle)


def scatter_ride(layer, parts):
    na = len(parts)

    def start(ins, outs, ss, rs, me):
        @pl.when(me[2] == layer)
        def _():
            for k in range(na):
                for p, mask in enumerate(CHIP_PEERS):
                    peer = _flip(mask, me)
                    _remote(ins[k].at[_chip(peer)], outs[k].at[_chip(me)], ss, rs, 3 * k + p, peer).start()

    def finish(ins, outs, ss, rs, me):
        @pl.when(me[2] == layer)
        def _():
            for k in range(na):
                for p, mask in enumerate(CHIP_PEERS):
                    peer = _flip(mask, me)
                    _remote(ins[k].at[_chip(peer)], outs[k].at[_chip(peer)], ss, rs, 3 * k + p, peer).wait_recv()
                    _remote(ins[k].at[_chip(peer)], outs[k].at[_chip(me)], ss, rs, 3 * k + p, peer).wait_send()

    return Ride(list(parts), [jax.ShapeDtypeStruct(a.shape, a.dtype) for a in parts], 3 * na, start, finish)


def to_owner_ride(layer, arrays):
    na = len(arrays)

    def start(ins, outs, ss, rs, me):
        @pl.when(me[2] != layer)
        def _():
            for k in range(na):
                _remote(ins[k], outs[k], ss, rs, k, _flip(SIBLING[0], me)).start()

    def finish(ins, outs, ss, rs, me):
        for k in range(na):
            cp = _remote(ins[k], outs[k], ss, rs, k, _flip(SIBLING[0], me))
            pl.when(me[2] != layer)(cp.wait_send)
            pl.when(me[2] == layer)(cp.wait_recv)

    return Ride(list(arrays), [jax.ShapeDtypeStruct(a.shape, a.dtype) for a in arrays], na, start, finish)


def allgather8_ride(a):
    def start(ins, outs, ss, rs, me):
        for p, mask in enumerate(ALL_PEERS):
            _remote(ins[0], outs[0].at[_dev(me)], ss, rs, p, _flip(mask, me)).start()

    def finish(ins, outs, ss, rs, me):
        for p, mask in enumerate(ALL_PEERS):
            peer = _flip(mask, me)
            _remote(ins[0], outs[0].at[_dev(peer)], ss, rs, p, peer).wait_recv()
            _remote(ins[0], outs[0].at[_dev(me)], ss, rs, p, peer).wait_send()

    return Ride([a], [jax.ShapeDtypeStruct((8,) + a.shape, a.dtype)], len(ALL_PEERS), start, finish)


def _with_own(gathered, own):
    me = _dev((lax.axis_index("x"), lax.axis_index("y"), lax.axis_index("c")))
    return jnp.where((jnp.arange(8) == me)[:, None, None], own[None], gathered)


def swap_ride(arrays):
    na = len(arrays)

    def start(ins, outs, ss, rs, me):
        for k in range(na):
            _remote(ins[k], outs[k], ss, rs, k, _flip(SIBLING[0], me)).start()

    def finish(ins, outs, ss, rs, me):
        for k in range(na):
            cp = _remote(ins[k], outs[k], ss, rs, k, _flip(SIBLING[0], me))
            cp.wait_recv()
            cp.wait_send()

    return Ride(list(arrays), [jax.ShapeDtypeStruct(a.shape, a.dtype) for a in arrays], na, start, finish)


class _Shifted:
    def __init__(self, ref, offset):
        self.ref, self.offset = ref, offset

    @property
    def at(self):
        return self

    def __getitem__(self, k):
        return self.ref.at[self.offset + k]


def merge_rides(rides):
    def spans(counts):
        out, o = [], 0
        for n in counts:
            out.append((o, o + n))
            o += n
        return out

    si, so = spans([len(r.ins) for r in rides]), spans([len(r.out_shapes) for r in rides])
    ss_ = spans([r.nsem for r in rides])

    def each(method):
        def run(ins, outs, ss, rs, me):
            for r, (i0, i1), (o0, o1), (s0, _) in zip(rides, si, so, ss_):
                if getattr(r, method) is not None:
                    getattr(r, method)(ins[i0:i1], outs[o0:o1], _Shifted(ss, s0), _Shifted(rs, s0), me)
        return run

    return Ride([a for r in rides for a in r.ins], [s for r in rides for s in r.out_shapes], sum(r.nsem for r in rides),
                each("start"), each("finish"), each("middle"))


def _w_in_from_chips(a):
    c2 = a[2]
    pad = jnp.zeros((c2.shape[0], IN_WP - IN_W), c2.dtype)
    return jnp.concatenate([a[0], a[1], c2[:, :252], c2[:, 260:516], c2[:, 252:260], pad, c2[:, 516:], a[3]], axis=1)
```
